```python
import math
import jax
import jax.numpy as jnp
from jax import lax
import numpy as np

D_MODEL = 1024
BATCH = 8
SEQ = 4096
DEPTH = 1

HEAD_DIM = 64
A_Q_HEADS = 8
A_KV_HEADS = 2
A_WINDOW = 128
B_GROUPS = ((128, 1), (512, 4), (2048, 16))
B_HEADS_PER_GROUP = 8
A_WIDTH = A_Q_HEADS * HEAD_DIM
B_WIDTH = B_HEADS_PER_GROUP * HEAD_DIM
N_BRANCHES = 2
N_BUCKETS = 32
MAX_DISTANCE = 1024
REL_HEADS = A_Q_HEADS + len(B_GROUPS) * B_HEADS_PER_GROUP
PROJ_SIZES = (A_WIDTH, A_KV_HEADS * HEAD_DIM, A_KV_HEADS * HEAD_DIM,
              len(B_GROUPS) * B_WIDTH, len(B_GROUPS) * B_WIDTH, len(B_GROUPS) * B_WIDTH,
              A_WIDTH, B_WIDTH, N_BRANCHES * D_MODEL)
IN_WIDTH = sum(PROJ_SIZES)
EPS = 1e-6
NEG_INF = -1e30

kernel_name = "hybrid_gated_window_dilated_attention_block"


def rms_norm(x, gain):
    xf = x.astype(jnp.float32)
    y = xf * lax.rsqrt(jnp.mean(xf * xf, axis=-1, keepdims=True) + EPS)
    return (y * gain.astype(jnp.float32)).astype(x.dtype)


def t5_bucket(rel):
    half = N_BUCKETS // 2
    max_exact = half // 2
    ret = (rel > 0).astype(jnp.int32) * half
    n = jnp.abs(rel)
    nf = jnp.maximum(n, max_exact).astype(jnp.float32)
    large = max_exact + (jnp.log(nf / max_exact) / math.log(MAX_DISTANCE / max_exact)
                         * (half - max_exact)).astype(jnp.int32)
    large = jnp.minimum(large, half - 1)
    return ret + jnp.where(n < max_exact, n, large)


def banded_attention(q, k, v, half_window, stride, bias_table, sink):
    bsz, L, H, dh = q.shape
    KV = k.shape[2]
    G = H // KV
    blk = half_window
    nb = -(-L // blk)
    Lp = nb * blk
    pad = Lp - L
    qb = jnp.pad(q, ((0, 0), (0, pad), (0, 0), (0, 0))).reshape(bsz, nb, blk, KV, G, dh)

    def windows(t):
        tp = jnp.pad(t, ((0, 0), (blk, blk + pad), (0, 0), (0, 0))).reshape(bsz, nb + 2, blk, KV, dh)
        return jnp.concatenate([tp[:, :-2], tp[:, 1:-1], tp[:, 2:]], axis=2)

    kw = windows(k)
    vw = windows(v)
    scores = jnp.einsum('bnqkgd,bnmkd->bkgnqm', qb, kw).astype(jnp.float32)

    qi = jnp.arange(blk)
    mi = jnp.arange(3 * blk)
    rel = mi[None, :] - blk - qi[:, None]
    bias = bias_table[t5_bucket(rel * stride)]
    bias = bias.transpose(2, 0, 1).reshape(KV, G, 1, blk, 3 * blk).astype(jnp.float32)
    blocks = jnp.arange(nb)[:, None, None] * blk
    qpos = blocks + qi[None, :, None]
    kpos = blocks - blk + mi[None, None, :]
    valid = (kpos >= 0) & (kpos < L) & (jnp.abs(kpos - qpos) <= half_window)

    logits = jnp.where(valid, scores + bias, NEG_INF)
    lse = jax.nn.logsumexp(logits, axis=-1)
    if sink is not None:
        lse = jnp.logaddexp(lse, sink.astype(jnp.float32).reshape(KV, G, 1, 1))
    probs = jnp.exp(logits - lse[..., None])
    out = jnp.einsum('bkgnqm,bnmkd->bnqkgd', probs.astype(v.dtype), vw)
    out = out.reshape(bsz, Lp, H, dh)[:, :L]
    lse = lse.transpose(0, 3, 4, 1, 2).reshape(bsz, Lp, H)[:, :L]
    return out, lse


def dilated_group(q, k, v, window, dilation, bias_table):
    bsz, S, H, dh = q.shape
    L = S // dilation

    def to_sub(t):
        return t.reshape(bsz, L, dilation, H, dh).transpose(0, 2, 1, 3, 4).reshape(bsz * dilation, L, H, dh)

    out, lse = banded_attention(to_sub(q), to_sub(k), to_sub(v), window // (2 * dilation),
                                dilation, bias_table, None)
    out = out.reshape(bsz, dilation, L, H, dh).transpose(0, 2, 1, 3, 4).reshape(bsz, S, H, dh)
    lse = lse.reshape(bsz, dilation, L, H).transpose(0, 2, 1, 3).reshape(bsz, S, H)
    return out, lse


def head_rms_norm(t, gain):
    return rms_norm(t, gain)


def _fwd_setup_inputs(seed: int = 0) -> dict:
    key = jax.random.key(seed)
    ks = jax.random.split(key, 16)
    f32 = jnp.float32
    x = jax.random.normal(ks[0], (BATCH, SEQ, D_MODEL), f32)
    norm_gain = 1.0 + 0.02 * jax.random.normal(ks[1], (DEPTH, D_MODEL), f32)
    w_in = jax.random.normal(ks[2], (DEPTH, D_MODEL, IN_WIDTH), f32) * D_MODEL ** -0.5
    q_norm_a = 1.0 + 0.02 * jax.random.normal(ks[3], (DEPTH, HEAD_DIM), f32)
    k_norm_a = 1.0 + 0.02 * jax.random.normal(ks[4], (DEPTH, HEAD_DIM), f32)
    q_norm_b = 1.0 + 0.02 * jax.random.normal(ks[5], (DEPTH, HEAD_DIM), f32)
    k_norm_b = 1.0 + 0.02 * jax.random.normal(ks[6], (DEPTH, HEAD_DIM), f32)
    sink_a = 0.5 * jax.random.normal(ks[7], (DEPTH, A_Q_HEADS), f32)
    rel_bias = 0.5 * jax.random.normal(ks[8], (N_BUCKETS, REL_HEADS), f32)
    w_branch_a = jax.random.normal(ks[9], (DEPTH, A_WIDTH, D_MODEL), f32) * A_WIDTH ** -0.5
    w_branch_b = jax.random.normal(ks[10], (DEPTH, B_WIDTH, D_MODEL), f32) * B_WIDTH ** -0.5
    b_merge = 0.1 * jax.random.normal(ks[11], (DEPTH, N_BRANCHES, D_MODEL), f32)
    w_out = jax.random.normal(ks[12], (DEPTH, D_MODEL, D_MODEL), f32) * D_MODEL ** -0.5
    return {"x": x, "norm_gain": norm_gain, "w_in": w_in, "q_norm_a": q_norm_a,
            "k_norm_a": k_norm_a, "q_norm_b": q_norm_b, "k_norm_b": k_norm_b,
            "sink_a": sink_a, "rel_bias": rel_bias, "w_branch_a": w_branch_a,
            "w_branch_b": w_branch_b, "b_merge": b_merge, "w_out": w_out}


def _fwd_reference(x, norm_gain, w_in, q_norm_a, k_norm_a, q_norm_b, k_norm_b, sink_a,
              rel_bias, w_branch_a, w_branch_b, b_merge, w_out):
    bsz, S, D = x.shape
    n_groups = len(B_GROUPS)
    scale = HEAD_DIM ** -0.5
    split_idx = [int(i) for i in np.cumsum(PROJ_SIZES)[:-1]]
    for layer in range(DEPTH):
        h = rms_norm(x, norm_gain[layer])
        proj = h @ w_in[layer]
        qa, ka, va, qb, kb, vb, ga, gb, mg = jnp.split(proj, split_idx, axis=-1)

        qa = head_rms_norm(qa.reshape(bsz, S, A_Q_HEADS, HEAD_DIM), q_norm_a[layer]) * scale
        ka = head_rms_norm(ka.reshape(bsz, S, A_KV_HEADS, HEAD_DIM), k_norm_a[layer])
        va = va.reshape(bsz, S, A_KV_HEADS, HEAD_DIM)
        ya, _ = banded_attention(qa, ka, va, A_WINDOW, 1, rel_bias[:, :A_Q_HEADS], sink_a[layer])
        ya = ya.reshape(bsz, S, A_WIDTH) * jax.nn.silu(ga)

        qb = head_rms_norm(qb.reshape(bsz, S, n_groups, B_HEADS_PER_GROUP, HEAD_DIM), q_norm_b[layer]) * scale
        kb = head_rms_norm(kb.reshape(bsz, S, n_groups, B_HEADS_PER_GROUP, HEAD_DIM), k_norm_b[layer])
        vb = vb.reshape(bsz, S, n_groups, B_HEADS_PER_GROUP, HEAD_DIM)
        outs = []
        lses = []
        for g, (window, dilation) in enumerate(B_GROUPS):
            c0 = A_Q_HEADS + g * B_HEADS_PER_GROUP
            o, l = dilated_group(qb[:, :, g], kb[:, :, g], vb[:, :, g], window, dilation,
                                 rel_bias[:, c0:c0 + B_HEADS_PER_GROUP])
            outs.append(o)
            lses.append(l)
        alpha = jax.nn.softmax(jnp.stack(lses, axis=0), axis=0)
        yb = jnp.sum(alpha[..., None].astype(x.dtype) * jnp.stack(outs, axis=0), axis=0)
        yb = yb.reshape(bsz, S, B_WIDTH) * jax.nn.silu(gb)

        br_a = ya @ w_branch_a[layer]
        br_b = yb @ w_branch_b[layer]
        gates = jax.nn.sigmoid(mg.reshape(bsz, S, N_BRANCHES, D).astype(jnp.float32)
                               + b_merge[layer].astype(jnp.float32)).astype(x.dtype)
        merged = gates[:, :, 0] * br_a + gates[:, :, 1] * br_b
        x = x + merged @ w_out[layer]
    return x


import jax as _jax
import jax.numpy as _jnp

TWIN_FORMAT = 'train_step'
FWD_PARAMS = ['x', 'norm_gain', 'w_in', 'q_norm_a', 'k_norm_a', 'q_norm_b', 'k_norm_b', 'sink_a', 'rel_bias', 'w_branch_a', 'w_branch_b', 'b_merge', 'w_out']
TWIN_WEIGHTS = ['norm_gain', 'w_in', 'q_norm_a', 'k_norm_a', 'q_norm_b', 'k_norm_b', 'sink_a', 'rel_bias', 'w_branch_a', 'w_branch_b', 'b_merge', 'w_out']
TWIN_DIFF_INPUT = 'x'
TWIN_INPUTS = ['x', 'norm_gain', 'w_in', 'q_norm_a', 'k_norm_a', 'q_norm_b', 'k_norm_b', 'sink_a', 'rel_bias', 'w_branch_a', 'w_branch_b', 'b_merge', 'w_out', 'loss_target', 'm_norm_gain', 'm_w_in', 'm_q_norm_a', 'm_k_norm_a', 'm_q_norm_b', 'm_k_norm_b', 'm_sink_a', 'm_rel_bias', 'm_w_branch_a', 'm_w_branch_b', 'm_b_merge', 'm_w_out', 'v_norm_gain', 'v_w_in', 'v_q_norm_a', 'v_k_norm_a', 'v_q_norm_b', 'v_k_norm_b', 'v_sink_a', 'v_rel_bias', 'v_w_branch_a', 'v_w_branch_b', 'v_b_merge', 'v_w_out']
TWIN_OUTPUTS = ['loss', 'grad_x', 'grad_norm_gain', 'grad_w_in', 'grad_q_norm_a', 'grad_k_norm_a', 'grad_q_norm_b', 'grad_k_norm_b', 'grad_sink_a', 'grad_rel_bias', 'grad_w_branch_a', 'grad_w_branch_b', 'grad_b_merge', 'grad_w_out', 'delta_norm_gain', 'delta_w_in', 'delta_q_norm_a', 'delta_k_norm_a', 'delta_q_norm_b', 'delta_k_norm_b', 'delta_sink_a', 'delta_rel_bias', 'delta_w_branch_a', 'delta_w_branch_b', 'delta_b_merge', 'delta_w_out', 'new_m_norm_gain', 'new_m_w_in', 'new_m_q_norm_a', 'new_m_k_norm_a', 'new_m_q_norm_b', 'new_m_k_norm_b', 'new_m_sink_a', 'new_m_rel_bias', 'new_m_w_branch_a', 'new_m_w_branch_b', 'new_m_b_merge', 'new_m_w_out', 'new_v_norm_gain', 'new_v_w_in', 'new_v_q_norm_a', 'new_v_k_norm_a', 'new_v_q_norm_b', 'new_v_k_norm_b', 'new_v_sink_a', 'new_v_rel_bias', 'new_v_w_branch_a', 'new_v_w_branch_b', 'new_v_b_merge', 'new_v_w_out']
TWIN_LEAF_KINDS = {'loss': 'loss', 'grad_x': 'grad_x', 'grad_norm_gain': 'grad_w', 'grad_w_in': 'grad_w', 'grad_q_norm_a': 'grad_w', 'grad_k_norm_a': 'grad_w', 'grad_q_norm_b': 'grad_w', 'grad_k_norm_b': 'grad_w', 'grad_sink_a': 'grad_w', 'grad_rel_bias': 'grad_w', 'grad_w_branch_a': 'grad_w', 'grad_w_branch_b': 'grad_w', 'grad_b_merge': 'grad_w', 'grad_w_out': 'grad_w', 'delta_norm_gain': 'delta_w', 'delta_w_in': 'delta_w', 'delta_q_norm_a': 'delta_w', 'delta_k_norm_a': 'delta_w', 'delta_q_norm_b': 'delta_w', 'delta_k_norm_b': 'delta_w', 'delta_sink_a': 'delta_w', 'delta_rel_bias': 'delta_w', 'delta_w_branch_a': 'delta_w', 'delta_w_branch_b': 'delta_w', 'delta_b_merge': 'delta_w', 'delta_w_out': 'delta_w', 'new_m_norm_gain': 'new_m', 'new_m_w_in': 'new_m', 'new_m_q_norm_a': 'new_m', 'new_m_k_norm_a': 'new_m', 'new_m_q_norm_b': 'new_m', 'new_m_k_norm_b': 'new_m', 'new_m_sink_a': 'new_m', 'new_m_rel_bias': 'new_m', 'new_m_w_branch_a': 'new_m', 'new_m_w_branch_b': 'new_m', 'new_m_b_merge': 'new_m', 'new_m_w_out': 'new_m', 'new_v_norm_gain': 'new_v', 'new_v_w_in': 'new_v', 'new_v_q_norm_a': 'new_v', 'new_v_k_norm_a': 'new_v', 'new_v_q_norm_b': 'new_v', 'new_v_k_norm_b': 'new_v', 'new_v_sink_a': 'new_v', 'new_v_rel_bias': 'new_v', 'new_v_w_branch_a': 'new_v', 'new_v_w_branch_b': 'new_v', 'new_v_b_merge': 'new_v', 'new_v_w_out': 'new_v'}


def _forward(args):
    return _fwd_reference(*[args[k] for k in FWD_PARAMS])


def _output_shape():
    out = _jax.eval_shape(lambda: _forward(_fwd_setup_inputs(0)))
    return out.shape, out.dtype

N_MICROBATCH = 1
ADAM_LR = 0.001
ADAM_B1 = 0.9
ADAM_B2 = 0.999
ADAM_EPS = 1e-08
ADAM_WD = 0.01
ADAM_STEP = 10
PER_EXAMPLE_BATCH_AXIS = {'x': 0, 'loss_target': 0}
SHARED_INPUTS = []
_WEIGHT_DTYPES = {'norm_gain': _jnp.float32, 'w_in': _jnp.float32, 'q_norm_a': _jnp.float32, 'k_norm_a': _jnp.float32, 'q_norm_b': _jnp.float32, 'k_norm_b': _jnp.float32, 'sink_a': _jnp.float32, 'rel_bias': _jnp.float32, 'w_branch_a': _jnp.float32, 'w_branch_b': _jnp.float32, 'b_merge': _jnp.float32, 'w_out': _jnp.float32}
MOMENT_SCALE = {'norm_gain': 1.027707e-01, 'w_in': 1.057557e-02, 'q_norm_a': 4.741677e-01, 'k_norm_a': 4.744837e-01, 'q_norm_b': 2.674065e-01, 'k_norm_b': 2.670910e-01, 'sink_a': 5.842438e-03, 'rel_bias': 4.164830e-02, 'w_branch_a': 1.062124e-02, 'w_branch_b': 1.049599e-02, 'b_merge': 7.822669e-03, 'w_out': 1.491332e-02}


def _to_microbatches(a, axis):
    t = _jnp.moveaxis(a, axis, 0)
    t = t.reshape((N_MICROBATCH, t.shape[0] // N_MICROBATCH) + t.shape[1:])
    return _jnp.moveaxis(t, 1, axis + 1)


def setup_inputs(seed: int = 0) -> dict:
    inp = _fwd_setup_inputs(seed)
    key = _jax.random.fold_in(_jax.random.key(seed), 7919)
    shape, _ = _output_shape()
    out = dict(inp)
    out["loss_target"] = _jax.random.normal(_jax.random.fold_in(key, 0), shape, _jnp.float32)
    for i, name in enumerate(TWIN_WEIGHTS):
        w = inp[name].astype(_jnp.float32)
        if MOMENT_SCALE is None:
            s = _jnp.sqrt(_jnp.mean(_jnp.square(w)) + 1e-30)
        else:
            s = MOMENT_SCALE[name]
        km, kv = _jax.random.split(_jax.random.fold_in(key, i + 1))
        out[name] = w
        out["m_" + name] = s * _jax.random.normal(km, w.shape, _jnp.float32)
        out["v_" + name] = (s * s) * _jax.random.uniform(kv, w.shape, _jnp.float32, 0.5, 1.5)
    if N_MICROBATCH > 1:
        for name, axis in PER_EXAMPLE_BATCH_AXIS.items():
            out[name] = _to_microbatches(out[name], axis)
    return {'x': out['x'], 'norm_gain': out['norm_gain'], 'w_in': out['w_in'], 'q_norm_a': out['q_norm_a'], 'k_norm_a': out['k_norm_a'], 'q_norm_b': out['q_norm_b'], 'k_norm_b': out['k_norm_b'], 'sink_a': out['sink_a'], 'rel_bias': out['rel_bias'], 'w_branch_a': out['w_branch_a'], 'w_branch_b': out['w_branch_b'], 'b_merge': out['b_merge'], 'w_out': out['w_out'], 'loss_target': out['loss_target'], 'm_norm_gain': out['m_norm_gain'], 'm_w_in': out['m_w_in'], 'm_q_norm_a': out['m_q_norm_a'], 'm_k_norm_a': out['m_k_norm_a'], 'm_q_norm_b': out['m_q_norm_b'], 'm_k_norm_b': out['m_k_norm_b'], 'm_sink_a': out['m_sink_a'], 'm_rel_bias': out['m_rel_bias'], 'm_w_branch_a': out['m_w_branch_a'], 'm_w_branch_b': out['m_w_branch_b'], 'm_b_merge': out['m_b_merge'], 'm_w_out': out['m_w_out'], 'v_norm_gain': out['v_norm_gain'], 'v_w_in': out['v_w_in'], 'v_q_norm_a': out['v_q_norm_a'], 'v_k_norm_a': out['v_k_norm_a'], 'v_q_norm_b': out['v_q_norm_b'], 'v_k_norm_b': out['v_k_norm_b'], 'v_sink_a': out['v_sink_a'], 'v_rel_bias': out['v_rel_bias'], 'v_w_branch_a': out['v_w_branch_a'], 'v_w_branch_b': out['v_w_branch_b'], 'v_b_merge': out['v_b_merge'], 'v_w_out': out['v_w_out']}


def _loss(weights, diff, rest, loss_target):
    with _jax.named_scope("forward"):
        args = {**rest, TWIN_DIFF_INPUT: diff, **{k: w.astype(_WEIGHT_DTYPES[k]) for k, w in weights.items()}}
        y = _forward(args)
    with _jax.named_scope("loss_head"):
        err = _jnp.square(y.astype(_jnp.float32) - loss_target)
        return 0.5 * _jnp.sum(_jnp.mean(err, axis=-1)) if err.ndim else 0.5 * err


def _adamw(w, g, m, v):
    m = ADAM_B1 * m + (1.0 - ADAM_B1) * g
    v = ADAM_B2 * v + (1.0 - ADAM_B2) * _jnp.square(g)
    m_hat = m / (1.0 - ADAM_B1 ** ADAM_STEP)
    v_hat = v / (1.0 - ADAM_B2 ** ADAM_STEP)
    delta = -ADAM_LR * (m_hat / (_jnp.sqrt(v_hat) + ADAM_EPS) + ADAM_WD * w)
    return delta, m, v


def reference(x, norm_gain, w_in, q_norm_a, k_norm_a, q_norm_b, k_norm_b, sink_a, rel_bias, w_branch_a, w_branch_b, b_merge, w_out, loss_target, m_norm_gain, m_w_in, m_q_norm_a, m_k_norm_a, m_q_norm_b, m_k_norm_b, m_sink_a, m_rel_bias, m_w_branch_a, m_w_branch_b, m_b_merge, m_w_out, v_norm_gain, v_w_in, v_q_norm_a, v_k_norm_a, v_q_norm_b, v_k_norm_b, v_sink_a, v_rel_bias, v_w_branch_a, v_w_branch_b, v_b_merge, v_w_out):
    given = dict(x=x, norm_gain=norm_gain, w_in=w_in, q_norm_a=q_norm_a, k_norm_a=k_norm_a, q_norm_b=q_norm_b, k_norm_b=k_norm_b, sink_a=sink_a, rel_bias=rel_bias, w_branch_a=w_branch_a, w_branch_b=w_branch_b, b_merge=b_merge, w_out=w_out, loss_target=loss_target, m_norm_gain=m_norm_gain, m_w_in=m_w_in, m_q_norm_a=m_q_norm_a, m_k_norm_a=m_k_norm_a, m_q_norm_b=m_q_norm_b, m_k_norm_b=m_k_norm_b, m_sink_a=m_sink_a, m_rel_bias=m_rel_bias, m_w_branch_a=m_w_branch_a, m_w_branch_b=m_w_branch_b, m_b_merge=m_b_merge, m_w_out=m_w_out, v_norm_gain=v_norm_gain, v_w_in=v_w_in, v_q_norm_a=v_q_norm_a, v_k_norm_a=v_k_norm_a, v_q_norm_b=v_q_norm_b, v_k_norm_b=v_k_norm_b, v_sink_a=v_sink_a, v_rel_bias=v_rel_bias, v_w_branch_a=v_w_branch_a, v_w_branch_b=v_w_branch_b, v_b_merge=v_b_merge, v_w_out=v_w_out)
    weights = {n: given[n] for n in TWIN_WEIGHTS}
    shared = {n: given[n] for n in SHARED_INPUTS}
    per_example = {n: given[n] for n in ['x']}
    grad_fn = _jax.value_and_grad(_loss, argnums=(0, 1))

    def one_microbatch(ex, loss_target):
        ex = dict(ex)
        diff = ex.pop(TWIN_DIFF_INPUT)
        return grad_fn(weights, diff, {**shared, **ex}, loss_target)

    if N_MICROBATCH == 1:
        loss, (grad_w, grad_x) = one_microbatch(per_example, given["loss_target"])
    else:
        def body(carry, xs):
            loss_sum, grad_sum = carry
            l_k, (gw_k, gx_k) = one_microbatch(xs[0], xs[1])
            with _jax.named_scope("update"):
                return (loss_sum + l_k, _jax.tree.map(_jnp.add, grad_sum, gw_k)), gx_k

        init = (_jnp.zeros((), _jnp.float32), _jax.tree.map(_jnp.zeros_like, weights))
        (loss, grad_w), grad_x = _jax.lax.scan(body, init, (per_example, given["loss_target"]))
    with _jax.named_scope("update"):
        delta_w, new_m, new_v = {}, {}, {}
        for n in TWIN_WEIGHTS:
            delta_w[n], new_m[n], new_v[n] = _adamw(weights[n], grad_w[n], given["m_" + n], given["v_" + n])
    return (loss, grad_x, *[grad_w[n] for n in TWIN_WEIGHTS], *[delta_w[n] for n in TWIN_WEIGHTS],
            *[new_m[n] for n in TWIN_WEIGHTS], *[new_v[n] for n in TWIN_WEIGHTS])
```

```python
import math

import numpy as np
import jax
import jax.numpy as jnp
from jax import lax
from jax.experimental import pallas as pl
from jax.experimental.pallas import tpu as pltpu

F32 = jnp.float32
BF16 = jnp.bfloat16

SEQ = 4096
D_MODEL = 1024
HEAD_DIM = 64
LANES = 128
EPS = 1e-6
NEG_INF = -1e30
SCALE = HEAD_DIM ** -0.5
N_BUCKETS = 32
MAX_DISTANCE = 1024
N_CHIPS = 4

A_HALF_WINDOW = 128
B_HALF_WINDOW = 64
B_DILATIONS = (1, 4, 16)
Q_BLOCK = 128

QKV_WIDTH = 5376
GATE_WIDTH = 3072
QA_BLK, KA_BLK, VA_BLK = 0, 4, 5
QB_BLK, KB_BLK, VB_BLK = 6, 18, 30
IN_WIDTH = QKV_WIDTH + GATE_WIDTH
W_IN_SHARD = IN_WIDTH // N_CHIPS

SMALL_ROWS = 544
GRAD_ROWS = 2688
GRAD_HALF = GRAD_ROWS // 2

ADAM_LR = 0.001
ADAM_B1 = 0.9
ADAM_B2 = 0.999
ADAM_EPS = 1e-08
ADAM_WD = 0.01
ADAM_STEP = 10

VMEM_LIMIT = 56 * 1024 * 1024

NT = (((1,), (1,)), ((), ()))
TN = (((0,), (0,)), ((), ()))
MESH = pl.DeviceIdType.MESH
ANY = pl.BlockSpec(memory_space=pl.ANY)


def _dot(a, b, dims=None):
    if dims is None:
        return jnp.dot(a, b, preferred_element_type=F32)
    return lax.dot_general(a, b, dims, preferred_element_type=F32)


def _params(*semantics):
    return pltpu.CompilerParams(dimension_semantics=semantics or None, vmem_limit_bytes=VMEM_LIMIT)


def _bucket_onehot(half_window, stride):
    w = Q_BLOCK + 2 * half_window
    rel = (np.arange(w)[None, :] - half_window - np.arange(Q_BLOCK)[:, None])
    band = np.abs(rel) <= half_window
    rel = rel * stride
    half, max_exact = N_BUCKETS // 2, N_BUCKETS // 4
    n = np.abs(rel)
    nf = np.maximum(n, max_exact).astype(np.float32)
    large = max_exact + (np.log(nf / np.float32(max_exact)) / np.float32(math.log(MAX_DISTANCE / max_exact))
                         * np.float32(half - max_exact)).astype(np.int32)
    large = np.minimum(large, half - 1)
    bucket = (rel > 0).astype(np.int32) * half + np.where(n < max_exact, n, large)
    onehot = (bucket[..., None] == np.arange(N_BUCKETS)) & band[..., None]
    return onehot.reshape(Q_BLOCK * w, N_BUCKETS).astype(np.float32), band


def _bias_table(rel_bias_cols, half_window, stride):
    onehot, band = _bucket_onehot(half_window, stride)
    h = rel_bias_cols.shape[1]
    w = Q_BLOCK + 2 * half_window
    t = jnp.einsum("pb,bh->hp", jnp.asarray(onehot), rel_bias_cols, precision=lax.Precision.HIGHEST)
    t = t.reshape(h, Q_BLOCK, w) + jnp.asarray(np.where(band, 0.0, NEG_INF).astype(np.float32))
    return t.reshape(h // 2, 2, Q_BLOCK, w)


def _bias_grad(ds_sum, half_window, stride):
    onehot, _ = _bucket_onehot(half_window, stride)
    h = ds_sum.shape[0] * 2
    return jnp.einsum("pb,hp->bh", jnp.asarray(onehot), ds_sum.reshape(h, -1), precision=lax.Precision.HIGHEST)


def _transpose_cast(w, out_dtype, name):
    r, c = w.shape

    def body(w_ref, o_ref):
        o_ref[...] = w_ref[...].T.astype(out_dtype)

    if r % LANES == 0:
        steps = pl.cdiv(c, LANES)
        in_spec, out_spec = pl.BlockSpec((r, LANES), lambda j: (0, j)), pl.BlockSpec((LANES, r), lambda j: (j, 0))
    else:
        steps = pl.cdiv(r, LANES)
        in_spec, out_spec = pl.BlockSpec((LANES, c), lambda j: (j, 0)), pl.BlockSpec((c, LANES), lambda j: (0, j))
    return pl.pallas_call(
        body, name=name, grid=(steps,), in_specs=[in_spec], out_specs=out_spec,
        out_shape=jax.ShapeDtypeStruct((c, r), out_dtype),
        compiler_params=_params("arbitrary"),
    )(w)


def _gather_weights(wt_shard, small_shard):
    bufs = ((W_IN_SHARD, IN_WIDTH), (SMALL_ROWS, N_CHIPS * SMALL_ROWS))

    def body(wt_in, sm_in, wt_out, sm_out, send_sems, recv_sems, local_sems):
        x, y, c = lax.axis_index("x"), lax.axis_index("y"), lax.axis_index("c")
        sibling = (x, y, 1 - c)
        chips = [(1 - x, y), (x, 1 - y), (1 - x, 1 - y)]
        my_chip = 2 * x + y
        refs = ((wt_in, wt_out), (sm_in, sm_out))

        def half_of(b, chip, half):
            rows = bufs[b][0]
            start = pl.multiple_of(chip * rows + half * (rows // 2), 16)
            return refs[b][1].at[pl.ds(start, rows // 2), :]

        def copy(k, src, dst, to):
            return pltpu.make_async_remote_copy(src_ref=src, dst_ref=dst, send_sem=send_sems.at[k],
                                                recv_sem=recv_sems.at[k], device_id=to, device_id_type=MESH)

        local, first, passed = [], [], []
        for b in range(2):
            rows = bufs[b][0]
            mine = refs[b][1].at[pl.ds(pl.multiple_of(my_chip * rows, 16), rows), :]
            local.append(pltpu.make_async_copy(refs[b][0], mine, local_sems.at[b]))
            src = refs[b][0].at[pl.ds(pl.multiple_of(c * (rows // 2), 16), rows // 2), :]
            for j, chip in enumerate(chips):
                first.append(copy(3 * b + j, src, half_of(b, my_chip, c), (*chip, c)))
        for cp in local + first:
            cp.start()
        for b in range(2):
            for j, (cx, cy) in enumerate(chips):
                landed = half_of(b, 2 * cx + cy, c)
                copy(3 * b + j, landed, landed, sibling).wait_recv()
                fwd = copy(6 + 3 * b + j, landed, landed, sibling)
                fwd.start()
                passed.append(fwd)
        for b in range(2):
            for j, (cx, cy) in enumerate(chips):
                other = half_of(b, 2 * cx + cy, 1 - c)
                copy(6 + 3 * b + j, other, other, sibling).wait_recv()
        for cp in first + passed:
            cp.wait_send()
        for cp in local:
            cp.wait()

    return pl.pallas_call(
        body, name="gather_weights",
        in_specs=[ANY, ANY], out_specs=[ANY, ANY],
        out_shape=[jax.ShapeDtypeStruct((bufs[0][1], D_MODEL), BF16),
                   jax.ShapeDtypeStruct((bufs[1][1], D_MODEL), BF16)],
        scratch_shapes=[pltpu.SemaphoreType.DMA((12,)), pltpu.SemaphoreType.DMA((12,)),
                        pltpu.SemaphoreType.DMA((2,))],
    )(wt_shard, small_shard)


def _in_proj(x, gain, w_t, first_block, n_blocks, out_dtype, name):
    tm, tn = 1024, 256

    def body(x_ref, g_ref, w_ref, o_ref, h_ref):
        @pl.when(pl.program_id(1) == 0)
        def _():
            xf = x_ref[...]
            r = lax.rsqrt(jnp.mean(xf * xf, axis=-1, keepdims=True) + EPS)
            h_ref[...] = ((xf * r) * g_ref[...]).astype(BF16)

        o_ref[...] = _dot(h_ref[...], w_ref[...], NT).astype(out_dtype)

    return pl.pallas_call(
        body, name=name, grid=(SEQ // tm, n_blocks),
        in_specs=[pl.BlockSpec((tm, D_MODEL), lambda i, j: (i, 0)),
                  pl.BlockSpec((1, D_MODEL), lambda i, j: (0, 0)),
                  pl.BlockSpec((tn, D_MODEL), lambda i, j: (j + first_block, 0))],
        out_specs=[pl.BlockSpec((tm, tn), lambda i, j: (i, j)),
                   pl.BlockSpec((tm, D_MODEL), lambda i, j: (i, 0))],
        out_shape=[jax.ShapeDtypeStruct((SEQ, tn * n_blocks), out_dtype),
                   jax.ShapeDtypeStruct((SEQ, D_MODEL), BF16)],
        compiler_params=_params("arbitrary", "arbitrary"),
    )(x, gain, w_t)


CHUNK = 512


def _low_half():
    return lax.broadcasted_iota(jnp.int32, (1, LANES), 1) < HEAD_DIM


def _half_sum(v, low):
    s0 = jnp.sum(jnp.where(low, v, 0.0), axis=-1, keepdims=True)
    s1 = jnp.sum(jnp.where(low, 0.0, v), axis=-1, keepdims=True)
    return jnp.where(low, s0, s1)


def _chunks(fn, init=0):
    return lax.fori_loop(0, SEQ // CHUNK, lambda i, carry: fn(pl.multiple_of(i * CHUNK, CHUNK), carry), init)


def _inv_rms(t, low):
    return lax.rsqrt(_half_sum(t * t, low) * (1.0 / HEAD_DIM) + EPS)


def _prep_q(q_ref, gain_ref, qn_ref):
    low = _low_half()

    def step(r0, carry):
        q = q_ref[pl.ds(r0, CHUNK), :].astype(F32)
        qn_ref[pl.ds(r0, CHUNK), :] = ((q * _inv_rms(q, low)) * gain_ref[...]) * SCALE
        return carry

    _chunks(step)


def _own_half(t, keep):
    return jnp.where(keep, t, pltpu.roll(t, HEAD_DIM, 1))


def _prep_kv(k_ref, v_ref, gain_ref, kp_ref, vp_ref, pad, keep=None):
    low = _low_half()
    zeros = jnp.zeros((pad, LANES), F32)
    for ref in (kp_ref, vp_ref):
        ref[pl.ds(0, pad), :] = zeros
        ref[pl.ds(pad + SEQ, pad), :] = zeros

    def step(r0, carry):
        k = k_ref[pl.ds(r0, CHUNK), :].astype(F32)
        v = v_ref[pl.ds(r0, CHUNK), :].astype(F32)
        kn = (k * _inv_rms(k, low)) * gain_ref[...]
        if keep is not None:
            kn, v = _own_half(kn, keep), _own_half(v, keep)
        kp_ref[pl.ds(pad + r0, CHUNK), :] = kn
        vp_ref[pl.ds(pad + r0, CHUNK), :] = v
        return carry

    _chunks(step)


def _tiles(d, half_window, fn):
    w = Q_BLOCK + 2 * half_window
    length = SEQ // d
    n_blocks = length // Q_BLOCK
    col = lax.broadcasted_iota(jnp.int32, (1, w), 1)

    def step(it, carry):
        c, n = it // n_blocks, it % n_blocks
        start = c + (d * Q_BLOCK) * n
        if d == 1:
            start = pl.multiple_of(start, Q_BLOCK)
            q_rows, k_rows = pl.ds(start, Q_BLOCK), pl.ds(start, w)
        else:
            q_rows, k_rows = pl.ds(start, Q_BLOCK, stride=d), pl.ds(start, w, stride=d)
        t = n * Q_BLOCK - half_window + col
        edge = jnp.where((t < 0) | (t >= length), NEG_INF, 0.0)
        fn(q_rows, k_rows, edge)
        return carry

    lax.fori_loop(0, d * n_blocks, step, 0)


def _fwd_tiles(qn_ref, kp_ref, vp_ref, bias_ref, emit, *, d, half_window, sinks=None):
    low = _low_half()

    def tile(q_rows, k_rows, edge):
        q = qn_ref[q_rows, :]
        k = kp_ref[k_rows, :].astype(BF16)
        v = vp_ref[k_rows, :].astype(BF16)
        outs, lses = [], []
        for j in range(2):
            mine = low if j == 0 else jnp.logical_not(low)
            s = _dot(jnp.where(mine, q, 0.0).astype(BF16), k, NT) + bias_ref[j] + edge
            m = jnp.max(s, axis=-1, keepdims=True)
            if sinks is not None:
                m = jnp.maximum(m, sinks[j])
            p = jnp.exp(s - m)
            l = jnp.sum(p, axis=-1, keepdims=True)
            if sinks is not None:
                l = l + jnp.exp(sinks[j] - m)
            outs.append(_dot(p.astype(BF16), v) * (1.0 / l))
            lses.append(m + jnp.log(l))
        emit(q_rows, jnp.where(low, outs[0], outs[1]), jnp.where(low, lses[0], lses[1]))

    _tiles(d, half_window, tile)


def _bwd_tiles(qn_ref, kp_ref, vp_ref, bias_ref, do_ref, lse_ref, delta_ref, dq_ref, dk_ref, dv_ref, ds_ref,
               *, d, half_window, sinks=None, dsink_ref=None):
    low = _low_half()

    def tile(q_rows, k_rows, edge):
        q = qn_ref[q_rows, :]
        k = kp_ref[k_rows, :].astype(BF16)
        v = vp_ref[k_rows, :].astype(BF16)
        do = do_ref[q_rows, :]
        lse = lse_ref[q_rows, :]
        delta = delta_ref[q_rows, :]
        dqs, dk, dv = [], None, None
        for j in range(2):
            mine = low if j == 0 else jnp.logical_not(low)
            qj = jnp.where(mine, q, 0.0).astype(BF16)
            doj = jnp.where(mine, do, 0.0).astype(BF16)
            lse_j = lse[:, j * HEAD_DIM:j * HEAD_DIM + 1]
            delta_j = delta[:, j * HEAD_DIM:j * HEAD_DIM + 1]
            p = jnp.exp(_dot(qj, k, NT) + bias_ref[j] + edge - lse_j)
            ds = p * (_dot(doj, v, NT) - delta_j)
            ds_ref[j] += ds
            if sinks is not None:
                dsink_ref[j] += -jnp.exp(sinks[j] - lse_j) * delta_j
            dsb, pb = ds.astype(BF16), p.astype(BF16)
            dqs.append(_dot(dsb, k))
            dkj, dvj = _dot(dsb, qj, TN), _dot(pb, doj, TN)
            dk, dv = (dkj, dvj) if j == 0 else (dk + dkj, dv + dvj)
        dq_ref[q_rows, :] = jnp.where(low, dqs[0], dqs[1])
        dk_ref[k_rows, :] += dk
        dv_ref[k_rows, :] += dv

    _tiles(d, half_window, tile)


def _prep_delta(do_ref, o_ref, delta_ref):
    low = _low_half()

    def step(r0, carry):
        delta_ref[pl.ds(r0, CHUNK), :] = _half_sum(do_ref[pl.ds(r0, CHUNK), :] * o_ref[pl.ds(r0, CHUNK), :], low)
        return carry

    _chunks(step)


def _norm_bwd(raw_ref, gain_ref, dn_ref, dn_offset, out_ref, scale):
    low = _low_half()

    def step(r0, dgain):
        t = raw_ref[pl.ds(r0, CHUNK), :].astype(F32)
        dn = dn_ref[pl.ds(dn_offset + r0, CHUNK), :]
        r = _inv_rms(t, low)
        th = t * r
        dth = dn * (gain_ref[...] * scale)
        out_ref[pl.ds(r0, CHUNK), :] = (r * (dth - th * (_half_sum(dth * th, low) * (1.0 / HEAD_DIM)))).astype(BF16)
        return dgain + jnp.sum(dn * th, axis=0, keepdims=True) * scale

    return _chunks(step, jnp.zeros((1, LANES), F32))


def _rows8(v):
    return jnp.broadcast_to(v, (8, v.shape[-1]))


A_W = Q_BLOCK + 2 * A_HALF_WINDOW
A_PAD = A_HALF_WINDOW


def _seq_block(col_fn):
    return pl.BlockSpec((SEQ, LANES), col_fn)


def _attn_a_fwd(qkv, gain_q, gain_k, bias, sink):
    def body(sink_ref, q_ref, k_ref, v_ref, gq_ref, gk_ref, bias_ref, o_ref, lse_ref, qn_ref, kp_ref, vp_ref):
        hp = pl.program_id(0)
        keep = (lax.broadcasted_iota(jnp.int32, (1, LANES), 1) // HEAD_DIM) == hp // 2
        _prep_q(q_ref, gq_ref, qn_ref)
        _prep_kv(k_ref, v_ref, gk_ref, kp_ref, vp_ref, A_PAD, keep)

        def emit(rows, out, lse):
            o_ref[rows, :] = out
            lse_ref[rows, :] = lse

        _fwd_tiles(qn_ref, kp_ref, vp_ref, bias_ref, emit, d=1, half_window=A_HALF_WINDOW,
                   sinks=(sink_ref[2 * hp], sink_ref[2 * hp + 1]))

    vec = pl.BlockSpec((1, LANES), lambda hp, s: (0, 0))
    return pl.pallas_call(
        body, name="attn_a_fwd",
        grid_spec=pltpu.PrefetchScalarGridSpec(
            num_scalar_prefetch=1, grid=(4,),
            in_specs=[_seq_block(lambda hp, s: (0, QA_BLK + hp)), _seq_block(lambda hp, s: (0, KA_BLK)),
                      _seq_block(lambda hp, s: (0, VA_BLK)), vec, vec,
                      pl.BlockSpec((None, 2, Q_BLOCK, A_W), lambda hp, s: (hp, 0, 0, 0))],
            out_specs=[_seq_block(lambda hp, s: (0, hp)), _seq_block(lambda hp, s: (0, hp))],
            scratch_shapes=[pltpu.VMEM((SEQ, LANES), F32), pltpu.VMEM((SEQ + 2 * A_PAD, LANES), F32),
                            pltpu.VMEM((SEQ + 2 * A_PAD, LANES), F32)]),
        out_shape=[jax.ShapeDtypeStruct((SEQ, 512), F32)] * 2,
        compiler_params=_params("arbitrary"),
    )(sink.reshape(8), qkv, qkv, qkv, gain_q, gain_k, bias)


def _attn_a_bwd(qkv, gain_q, gain_k, bias, sink, out, lse, d_out):
    def body(sink_ref, q_ref, k_ref, v_ref, gq_ref, gk_ref, bias_ref, o_ref, lse_ref, do_ref,
             dq_out, dk_out, dv_out, dgq_out, dgk_out, ds_out, dsink_out,
             qn_ref, kp_ref, vp_ref, delta_ref, dq_ref, dk_ref, dv_ref, dk_tot, dv_tot):
        hp = pl.program_id(0)
        kv_head = hp // 2
        keep = (lax.broadcasted_iota(jnp.int32, (1, LANES), 1) // HEAD_DIM) == kv_head
        _prep_q(q_ref, gq_ref, qn_ref)
        _prep_kv(k_ref, v_ref, gk_ref, kp_ref, vp_ref, A_PAD, keep)
        _prep_delta(do_ref, o_ref, delta_ref)
        dk_ref[...] = jnp.zeros_like(dk_ref)
        dv_ref[...] = jnp.zeros_like(dv_ref)
        ds_out[...] = jnp.zeros_like(ds_out)
        dsink_out[...] = jnp.zeros_like(dsink_out)

        @pl.when(hp == 0)
        def _():
            dk_tot[...] = jnp.zeros_like(dk_tot)
            dv_tot[...] = jnp.zeros_like(dv_tot)

        _bwd_tiles(qn_ref, kp_ref, vp_ref, bias_ref, do_ref, lse_ref, delta_ref, dq_ref, dk_ref, dv_ref, ds_out,
                   d=1, half_window=A_HALF_WINDOW, sinks=(sink_ref[2 * hp], sink_ref[2 * hp + 1]),
                   dsink_ref=dsink_out)
        dgq_out[...] = _rows8(_norm_bwd(q_ref, gq_ref, dq_ref, 0, dq_out, SCALE))

        def fold(r0, carry):
            rows = pl.ds(A_PAD + r0, CHUNK)
            for acc, tot in ((dk_ref, dk_tot), (dv_ref, dv_tot)):
                t = acc[rows, :]
                tot[pl.ds(r0, CHUNK), :] += jnp.where(keep, t + pltpu.roll(t, HEAD_DIM, 1), 0.0)
            return carry

        _chunks(fold)

        @pl.when(hp == 3)
        def _():
            dgk_out[...] = _rows8(_norm_bwd(k_ref, gk_ref, dk_tot, 0, dk_out, 1.0))
            dv_out[...] = dv_tot[...].astype(BF16)

    vec = pl.BlockSpec((1, LANES), lambda hp, s: (0, 0))
    seq_f32 = pltpu.VMEM((SEQ, LANES), F32)
    padded = pltpu.VMEM((SEQ + 2 * A_PAD, LANES), F32)
    return pl.pallas_call(
        body, name="attn_a_bwd",
        grid_spec=pltpu.PrefetchScalarGridSpec(
            num_scalar_prefetch=1, grid=(4,),
            in_specs=[_seq_block(lambda hp, s: (0, QA_BLK + hp)), _seq_block(lambda hp, s: (0, KA_BLK)),
                      _seq_block(lambda hp, s: (0, VA_BLK)), vec, vec,
                      pl.BlockSpec((None, 2, Q_BLOCK, A_W), lambda hp, s: (hp, 0, 0, 0)),
                      _seq_block(lambda hp, s: (0, hp)), _seq_block(lambda hp, s: (0, hp)),
                      _seq_block(lambda hp, s: (0, hp))],
            out_specs=[_seq_block(lambda hp, s: (0, hp)), _seq_block(lambda hp, s: (0, KA_BLK)),
                       _seq_block(lambda hp, s: (0, VA_BLK)),
                       pl.BlockSpec((None, 8, LANES), lambda hp, s: (hp, 0, 0)),
                       pl.BlockSpec((8, LANES), lambda hp, s: (0, 0)),
                       pl.BlockSpec((None, 2, Q_BLOCK, A_W), lambda hp, s: (hp, 0, 0, 0)),
                       pl.BlockSpec((None, 2, Q_BLOCK, 1), lambda hp, s: (hp, 0, 0, 0))],
            scratch_shapes=[seq_f32, padded, padded, seq_f32, seq_f32, padded, padded, seq_f32, seq_f32]),
        out_shape=[jax.ShapeDtypeStruct((SEQ, 768), BF16)] * 3
        + [jax.ShapeDtypeStruct((4, 8, LANES), F32), jax.ShapeDtypeStruct((8, LANES), F32),
           jax.ShapeDtypeStruct((4, 2, Q_BLOCK, A_W), F32), jax.ShapeDtypeStruct((4, 2, Q_BLOCK, 1), F32)],
        compiler_params=_params("arbitrary"),
    )(sink.reshape(8), qkv, qkv, qkv, gain_q, gain_k, bias, out, lse, d_out)


B_W = Q_BLOCK + 2 * B_HALF_WINDOW
B_PAD_MAX = B_HALF_WINDOW * B_DILATIONS[-1]


def _attn_b_fwd(qkv, gain_q, gain_k, bias):
    def body(q_ref, k_ref, v_ref, gq_ref, gk_ref, bias_ref, o_ref, lse_ref, qn_ref, kp_ref, vp_ref):
        g = pl.program_id(1)
        _prep_q(q_ref, gq_ref, qn_ref)

        def first(rows, out, lse):
            o_ref[rows, :] = out
            lse_ref[rows, :] = lse

        def combine(rows, out, lse):
            old = lse_ref[rows, :]
            new = jnp.maximum(old, lse) + jnp.log(1.0 + jnp.exp(-jnp.abs(old - lse)))
            o_ref[rows, :] = o_ref[rows, :] * jnp.exp(old - new) + out * jnp.exp(lse - new)
            lse_ref[rows, :] = new

        for gi, d in enumerate(B_DILATIONS):
            @pl.when(g == gi)
            def _():
                _prep_kv(k_ref, v_ref, gk_ref, kp_ref, vp_ref, B_HALF_WINDOW * d)
                _fwd_tiles(qn_ref, kp_ref, vp_ref, bias_ref, first if gi == 0 else combine,
                           d=d, half_window=B_HALF_WINDOW)

    vec = pl.BlockSpec((1, LANES), lambda hp, g: (0, 0))
    padded = pltpu.VMEM((SEQ + 2 * B_PAD_MAX, LANES), F32)
    return pl.pallas_call(
        body, name="attn_b_fwd", grid=(4, 3),
        in_specs=[_seq_block(lambda hp, g: (0, QB_BLK + 4 * g + hp)), _seq_block(lambda hp, g: (0, KB_BLK + 4 * g + hp)),
                  _seq_block(lambda hp, g: (0, VB_BLK + 4 * g + hp)), vec, vec,
                  pl.BlockSpec((None, 2, Q_BLOCK, B_W), lambda hp, g: (4 * g + hp, 0, 0, 0))],
        out_specs=[_seq_block(lambda hp, g: (0, hp)), _seq_block(lambda hp, g: (0, hp))],
        out_shape=[jax.ShapeDtypeStruct((SEQ, 512), F32)] * 2,
        scratch_shapes=[pltpu.VMEM((SEQ, LANES), F32), padded, padded],
        compiler_params=_params("arbitrary", "arbitrary"),
    )(qkv, qkv, qkv, gain_q, gain_k, bias)


def _attn_b_bwd(qkv, gain_q, gain_k, bias, out, lse, d_out):
    def body(q_ref, k_ref, v_ref, gq_ref, gk_ref, bias_ref, o_ref, lse_ref, do_ref,
             dq_out, dk_out, dv_out, dgq_out, dgk_out, ds_out,
             qn_ref, kp_ref, vp_ref, delta_ref, dq_ref, dk_ref, dv_ref):
        g = pl.program_id(1)
        _prep_q(q_ref, gq_ref, qn_ref)
        _prep_delta(do_ref, o_ref, delta_ref)
        dk_ref[...] = jnp.zeros_like(dk_ref)
        dv_ref[...] = jnp.zeros_like(dv_ref)
        ds_out[...] = jnp.zeros_like(ds_out)
        for gi, d in enumerate(B_DILATIONS):
            @pl.when(g == gi)
            def _():
                pad = B_HALF_WINDOW * d
                _prep_kv(k_ref, v_ref, gk_ref, kp_ref, vp_ref, pad)
                _bwd_tiles(qn_ref, kp_ref, vp_ref, bias_ref, do_ref, lse_ref, delta_ref, dq_ref, dk_ref, dv_ref,
                           ds_out, d=d, half_window=B_HALF_WINDOW)
                dgk_out[...] = _rows8(_norm_bwd(k_ref, gk_ref, dk_ref, pad, dk_out, 1.0))
                dv_out[...] = dv_ref[pl.ds(pad, SEQ), :].astype(BF16)
        dgq_out[...] = _rows8(_norm_bwd(q_ref, gq_ref, dq_ref, 0, dq_out, SCALE))

    vec = pl.BlockSpec((1, LANES), lambda hp, g: (0, 0))
    seq_f32 = pltpu.VMEM((SEQ, LANES), F32)
    padded = pltpu.VMEM((SEQ + 2 * B_PAD_MAX, LANES), F32)
    part = pl.BlockSpec((None, 8, LANES), lambda hp, g: (4 * g + hp, 0, 0))
    return pl.pallas_call(
        body, name="attn_b_bwd", grid=(4, 3),
        in_specs=[_seq_block(lambda hp, g: (0, QB_BLK + 4 * g + hp)), _seq_block(lambda hp, g: (0, KB_BLK + 4 * g + hp)),
                  _seq_block(lambda hp, g: (0, VB_BLK + 4 * g + hp)), vec, vec,
                  pl.BlockSpec((None, 2, Q_BLOCK, B_W), lambda hp, g: (4 * g + hp, 0, 0, 0)),
                  _seq_block(lambda hp, g: (0, hp)), _seq_block(lambda hp, g: (0, hp)), _seq_block(lambda hp, g: (0, hp))],
        out_specs=[_seq_block(lambda hp, g: (0, 4 * g + hp)), _seq_block(lambda hp, g: (0, 12 + 4 * g + hp)),
                   _seq_block(lambda hp, g: (0, 24 + 4 * g + hp)), part, part,
                   pl.BlockSpec((None, 2, Q_BLOCK, B_W), lambda hp, g: (4 * g + hp, 0, 0, 0))],
        out_shape=[jax.ShapeDtypeStruct((SEQ, 4608), BF16)] * 3
        + [jax.ShapeDtypeStruct((12, 8, LANES), F32)] * 2 + [jax.ShapeDtypeStruct((12, 2, Q_BLOCK, B_W), F32)],
        scratch_shapes=[seq_f32, padded, padded, seq_f32, seq_f32, padded, padded],
        compiler_params=_params("arbitrary", "arbitrary"),
    )(qkv, qkv, qkv, gain_q, gain_k, bias, out, lse, d_out)


def _sigmoid(t):
    return 1.0 / (1.0 + jnp.exp(-t))


def _middle(out_a, out_b, gates, x, target, w_a, w_b, w_out, b_merge):
    tm = 256
    n_steps = SEQ // tm

    def body(oa_ref, ob_ref, g_ref, x_ref, t_ref, wa_ref, wb_ref, wo_ref, bm_ref,
             dy_ref, dg_ref, doa_ref, dob_ref, dwa_ref, dwb_ref, dwo_ref, dbm_ref, sq_ref):
        @pl.when(pl.program_id(0) == 0)
        def _():
            for ref in (dwa_ref, dwb_ref, dwo_ref, dbm_ref, sq_ref):
                ref[...] = jnp.zeros_like(ref)

        gate_a, gate_b = g_ref[:, 0:512], g_ref[:, 512:1024]
        sig_a, sig_b = _sigmoid(gate_a), _sigmoid(gate_b)
        silu_a, silu_b = gate_a * sig_a, gate_b * sig_b
        oa, ob = oa_ref[...], ob_ref[...]
        ya, yb = (oa * silu_a).astype(BF16), (ob * silu_b).astype(BF16)
        br_a, br_b = _dot(ya, wa_ref[...]), _dot(yb, wb_ref[...])
        m0 = _sigmoid(g_ref[:, 1024:2048] + bm_ref[0:1, :])
        m1 = _sigmoid(g_ref[:, 2048:3072] + bm_ref[1:2, :])
        merged = (m0 * br_a + m1 * br_b).astype(BF16)
        err = (x_ref[...] + _dot(merged, wo_ref[...])) - t_ref[...]
        sq_ref[...] += jnp.sum(err * err, axis=0, keepdims=True)

        dy = err * (1.0 / D_MODEL)
        dy_ref[...] = dy
        dyb = dy.astype(BF16)
        dmerged = _dot(dyb, wo_ref[...], NT)
        dwo_ref[...] += _dot(merged, dyb, TN)
        dbr_a, dbr_b = (dmerged * m0).astype(BF16), (dmerged * m1).astype(BF16)
        dm0 = (dmerged * br_a) * (m0 * (1.0 - m0))
        dm1 = (dmerged * br_b) * (m1 * (1.0 - m1))
        dbm_ref[0:1, :] += jnp.sum(dm0, axis=0, keepdims=True)
        dbm_ref[1:2, :] += jnp.sum(dm1, axis=0, keepdims=True)
        for s in range(N_CHIPS):
            cols = slice(256 * s, 256 * (s + 1))
            dwa_ref[s] += _dot(ya, dbr_a[:, cols], TN)
            dwb_ref[s] += _dot(yb, dbr_b[:, cols], TN)
        dya, dyb_ = _dot(dbr_a, wa_ref[...], NT), _dot(dbr_b, wb_ref[...], NT)
        doa_ref[...] = dya * silu_a
        dob_ref[...] = dyb_ * silu_b
        dg_ref[:, 0:512] = ((dya * oa) * (sig_a * (1.0 + gate_a * (1.0 - sig_a)))).astype(BF16)
        dg_ref[:, 512:1024] = ((dyb_ * ob) * (sig_b * (1.0 + gate_b * (1.0 - sig_b)))).astype(BF16)
        dg_ref[:, 1024:2048] = dm0.astype(BF16)
        dg_ref[:, 2048:3072] = dm1.astype(BF16)

    def rows(width):
        return pl.BlockSpec((tm, width), lambda i: (i, 0))

    def whole(*shape):
        return pl.BlockSpec(shape, lambda i: (0,) * len(shape))

    return pl.pallas_call(
        body, name="middle", grid=(n_steps,),
        in_specs=[rows(512), rows(512), rows(GATE_WIDTH), rows(D_MODEL), rows(D_MODEL),
                  whole(512, D_MODEL), whole(512, D_MODEL), whole(D_MODEL, D_MODEL), whole(2, D_MODEL)],
        out_specs=[rows(D_MODEL), rows(GATE_WIDTH), rows(512), rows(512),
                   whole(N_CHIPS, 512, 256), whole(N_CHIPS, 512, 256), whole(D_MODEL, D_MODEL),
                   whole(2, D_MODEL), whole(1, D_MODEL)],
        out_shape=[jax.ShapeDtypeStruct((SEQ, D_MODEL), F32), jax.ShapeDtypeStruct((SEQ, GATE_WIDTH), BF16),
                   jax.ShapeDtypeStruct((SEQ, 512), F32), jax.ShapeDtypeStruct((SEQ, 512), F32),
                   jax.ShapeDtypeStruct((N_CHIPS, 512, 256), F32), jax.ShapeDtypeStruct((N_CHIPS, 512, 256), F32),
                   jax.ShapeDtypeStruct((D_MODEL, D_MODEL), F32), jax.ShapeDtypeStruct((2, D_MODEL), F32),
                   jax.ShapeDtypeStruct((1, D_MODEL), F32)],
        compiler_params=_params("arbitrary"),
    )(out_a, out_b, gates, x, target, w_a, w_b, w_out, b_merge)


def _three_way(j, edges, refs, fn):
    lo = 0
    for hi, ref in zip(edges, refs):
        @pl.when((j >= lo) & (j < hi))
        def _(ref=ref):
            fn(ref)
        lo = hi


def _d_w_in(dqkv_a, dqkv_b, dgates, h):
    tn = 256
    edges = (768 // tn, QKV_WIDTH // tn, IN_WIDTH // tn)

    def body(a_ref, b_ref, c_ref, h_ref, o_ref):
        def emit(ref):
            o_ref[...] = _dot(ref[...], h_ref[...], TN)

        _three_way(pl.program_id(0), edges, (a_ref, b_ref, c_ref), emit)

    def cols(lo, hi):
        return pl.BlockSpec((SEQ, tn), lambda j: (0, jnp.clip(j - lo, 0, hi - lo - 1)))

    return pl.pallas_call(
        body, name="d_w_in", grid=(edges[2],),
        in_specs=[cols(0, edges[0]), cols(edges[0], edges[1]), cols(edges[1], edges[2]),
                  pl.BlockSpec((SEQ, D_MODEL), lambda j: (0, 0))],
        out_specs=pl.BlockSpec((tn, D_MODEL), lambda j: (j, 0)),
        out_shape=jax.ShapeDtypeStruct((IN_WIDTH, D_MODEL), F32),
        compiler_params=_params("arbitrary"),
    )(dqkv_a, dqkv_b, dgates, h)


def _d_x(dqkv_a, dqkv_b, dgates, w_t, x, gain, dy):
    tm, tk = 512, 768
    edges = (768 // tk, QKV_WIDTH // tk, IN_WIDTH // tk)

    def body(a_ref, b_ref, c_ref, w_ref, x_ref, g_ref, dy_ref, dx_ref, dgain_ref, acc_ref):
        i, j = pl.program_id(0), pl.program_id(1)

        @pl.when(j == 0)
        def _():
            acc_ref[...] = jnp.zeros_like(acc_ref)

        @pl.when((i == 0) & (j == 0))
        def _():
            dgain_ref[...] = jnp.zeros_like(dgain_ref)

        def emit(ref):
            acc_ref[...] += _dot(ref[...], w_ref[...])

        _three_way(j, edges, (a_ref, b_ref, c_ref), emit)

        @pl.when(j == edges[2] - 1)
        def _():
            xf = x_ref[...]
            r = lax.rsqrt(jnp.mean(xf * xf, axis=-1, keepdims=True) + EPS)
            xh = xf * r
            dh = acc_ref[...]
            dxh = dh * g_ref[...]
            dx_ref[...] = r * (dxh - xh * jnp.mean(dxh * xh, axis=-1, keepdims=True)) + dy_ref[...]
            dgain_ref[...] += _rows8(jnp.sum(dh * xh, axis=0, keepdims=True))

    def cols(lo, hi):
        return pl.BlockSpec((tm, tk), lambda i, j: (i, jnp.clip(j - lo, 0, hi - lo - 1)))

    row = pl.BlockSpec((tm, D_MODEL), lambda i, j: (i, 0))
    return pl.pallas_call(
        body, name="d_x", grid=(SEQ // tm, edges[2]),
        in_specs=[cols(0, edges[0]), cols(edges[0], edges[1]), cols(edges[1], edges[2]),
                  pl.BlockSpec((tk, D_MODEL), lambda i, j: (j, 0)), row,
                  pl.BlockSpec((1, D_MODEL), lambda i, j: (0, 0)), row],
        out_specs=[row, pl.BlockSpec((8, D_MODEL), lambda i, j: (0, 0))],
        out_shape=[jax.ShapeDtypeStruct((SEQ, D_MODEL), F32), jax.ShapeDtypeStruct((8, D_MODEL), F32)],
        scratch_shapes=[pltpu.VMEM((tm, D_MODEL), F32)],
        compiler_params=_params("arbitrary", "arbitrary"),
    )(dqkv_a, dqkv_b, dgates, w_t, x, gain, dy)


def _my_place():
    x, y, c = lax.axis_index("x"), lax.axis_index("y"), lax.axis_index("c")
    return jnp.stack([2 * x + y, c]).astype(jnp.int32)


def _swap_halves(grads):
    def body(g_ref, o_ref, send_sem, recv_sem):
        x, y, c = lax.axis_index("x"), lax.axis_index("y"), lax.axis_index("c")
        theirs = g_ref.at[:, pl.ds(pl.multiple_of((1 - c) * GRAD_HALF, 8), GRAD_HALF), :]
        cp = pltpu.make_async_remote_copy(src_ref=theirs, dst_ref=o_ref, send_sem=send_sem, recv_sem=recv_sem,
                                          device_id=(x, y, 1 - c), device_id_type=MESH)
        cp.start()
        cp.wait()

    return pl.pallas_call(
        body, name="reduce_swap_halves", in_specs=[ANY], out_specs=ANY,
        out_shape=jax.ShapeDtypeStruct((N_CHIPS, GRAD_HALF, D_MODEL), F32),
        scratch_shapes=[pltpu.SemaphoreType.DMA, pltpu.SemaphoreType.DMA],
    )(grads)


def _add_halves(place, grads, theirs):
    tr = 336
    n = GRAD_HALF // tr

    def body(place_ref, g_ref, t_ref, o_ref):
        o_ref[...] = (g_ref[...] + t_ref[...]).astype(BF16)

    return pl.pallas_call(
        body, name="reduce_add_halves",
        grid_spec=pltpu.PrefetchScalarGridSpec(
            num_scalar_prefetch=1, grid=(N_CHIPS, n),
            in_specs=[pl.BlockSpec((None, tr, D_MODEL), lambda s, i, p: (s, p[1] * n + i, 0)),
                      pl.BlockSpec((None, tr, D_MODEL), lambda s, i, p: (s, i, 0))],
            out_specs=pl.BlockSpec((None, tr, D_MODEL), lambda s, i, p: (s, i, 0))),
        out_shape=jax.ShapeDtypeStruct((N_CHIPS, GRAD_HALF, D_MODEL), BF16),
        compiler_params=_params("arbitrary", "arbitrary"),
    )(place, grads, theirs)


def _scatter_chips(chip_sums):
    def body(q_ref, o_ref, send_sems, recv_sems):
        x, y, c = lax.axis_index("x"), lax.axis_index("y"), lax.axis_index("c")
        chips = [(1 - x, y), (x, 1 - y), (1 - x, 1 - y)]
        copies = [pltpu.make_async_remote_copy(src_ref=q_ref.at[2 * cx + cy], dst_ref=o_ref.at[j],
                                               send_sem=send_sems.at[j], recv_sem=recv_sems.at[j],
                                               device_id=(cx, cy, c), device_id_type=MESH)
                  for j, (cx, cy) in enumerate(chips)]
        for cp in copies:
            cp.start()
        for cp in copies:
            cp.wait()

    return pl.pallas_call(
        body, name="reduce_scatter_chips", in_specs=[ANY], out_specs=ANY,
        out_shape=jax.ShapeDtypeStruct((3, GRAD_HALF, D_MODEL), BF16),
        scratch_shapes=[pltpu.SemaphoreType.DMA((3,)), pltpu.SemaphoreType.DMA((3,))],
    )(chip_sums)


def _add_chips(place, chip_sums, others):
    tr = 336
    n = GRAD_HALF // tr

    def body(place_ref, q_ref, o_ref, r_ref):
        acc = q_ref[...].astype(F32)
        for j in range(3):
            acc = acc + o_ref[j].astype(F32)
        r_ref[...] = acc

    return pl.pallas_call(
        body, name="reduce_add_chips",
        grid_spec=pltpu.PrefetchScalarGridSpec(
            num_scalar_prefetch=1, grid=(n,),
            in_specs=[pl.BlockSpec((None, tr, D_MODEL), lambda i, p: (p[0], i, 0)),
                      pl.BlockSpec((3, tr, D_MODEL), lambda i, p: (0, i, 0))],
            out_specs=pl.BlockSpec((tr, D_MODEL), lambda i, p: (i, 0))),
        out_shape=jax.ShapeDtypeStruct((GRAD_HALF, D_MODEL), F32),
        compiler_params=_params("arbitrary"),
    )(place, chip_sums, others)


def _join_halves(half):
    def body(h_ref, o_ref, send_sem, recv_sem, local_sem):
        x, y, c = lax.axis_index("x"), lax.axis_index("y"), lax.axis_index("c")
        mine = o_ref.at[pl.ds(pl.multiple_of(c * GRAD_HALF, 8), GRAD_HALF), :]
        other = o_ref.at[pl.ds(pl.multiple_of((1 - c) * GRAD_HALF, 8), GRAD_HALF), :]
        keep = pltpu.make_async_copy(h_ref, mine, local_sem)
        keep.start()
        send = pltpu.make_async_remote_copy(src_ref=h_ref, dst_ref=mine, send_sem=send_sem, recv_sem=recv_sem,
                                            device_id=(x, y, 1 - c), device_id_type=MESH)
        send.start()
        pltpu.make_async_remote_copy(src_ref=h_ref, dst_ref=other, send_sem=send_sem, recv_sem=recv_sem,
                                     device_id=(x, y, 1 - c), device_id_type=MESH).wait_recv()
        send.wait_send()
        keep.wait()

    return pl.pallas_call(
        body, name="reduce_join_halves", in_specs=[ANY], out_specs=ANY,
        out_shape=jax.ShapeDtypeStruct((GRAD_ROWS, D_MODEL), F32),
        scratch_shapes=[pltpu.SemaphoreType.DMA, pltpu.SemaphoreType.DMA, pltpu.SemaphoreType.DMA],
    )(half)


def _gather_small(block):
    rows = block.shape[0]

    def body(b_ref, o_ref, send_sems, recv_sems, local_sem):
        x, y, c = lax.axis_index("x"), lax.axis_index("y"), lax.axis_index("c")
        me, sibling = (x, y, c), (x, y, 1 - c)
        chips = [(1 - x, y), (x, 1 - y), (1 - x, 1 - y)]

        def at(px, py, pc):
            return o_ref.at[pl.ds(pl.multiple_of((4 * px + 2 * py + pc) * rows, 8), rows), :]

        def copy(k, block_of, to, src=None):
            return pltpu.make_async_remote_copy(src_ref=at(*block_of) if src is None else src, dst_ref=at(*block_of),
                                                send_sem=send_sems.at[k], recv_sem=recv_sems.at[k],
                                                device_id=to, device_id_type=MESH)

        mine = pltpu.make_async_copy(b_ref, at(*me), local_sem)
        mine.start()
        first = [copy(0, me, sibling, src=b_ref)]
        first += [copy(1 + j, me, (*chip, c), src=b_ref) for j, chip in enumerate(chips)]
        for cp in first:
            cp.start()
        passed = [copy(4 + j, (*chip, c), sibling) for j, chip in enumerate(chips)]
        for j, chip in enumerate(chips):
            copy(1 + j, (*chip, c), me).wait_recv()
            passed[j].start()
        copy(0, sibling, me).wait_recv()
        for j, chip in enumerate(chips):
            copy(4 + j, (*chip, 1 - c), me).wait_recv()
        for cp in first + passed:
            cp.wait_send()
        mine.wait()

    return pl.pallas_call(
        body, name="gather_small_grads",
        in_specs=[pl.BlockSpec(memory_space=pltpu.VMEM)], out_specs=pl.BlockSpec(memory_space=pltpu.VMEM),
        out_shape=jax.ShapeDtypeStruct((8 * rows, D_MODEL), F32),
        scratch_shapes=[pltpu.SemaphoreType.DMA((7,)), pltpu.SemaphoreType.DMA((7,)), pltpu.SemaphoreType.DMA],
    )(block)


def _sum_devices(blocks):
    def body(b_ref, o_ref):
        acc = b_ref[0:8, :]
        for dev in range(1, 8):
            acc = acc + b_ref[8 * dev:8 * dev + 8, :]
        o_ref[...] = acc

    return pl.pallas_call(body, name="sum_small_grads", out_shape=jax.ShapeDtypeStruct((8, D_MODEL), F32))(blocks)


def _adamw_math(w, g, m, v):
    m = ADAM_B1 * m + (1.0 - ADAM_B1) * g
    v = ADAM_B2 * v + (1.0 - ADAM_B2) * (g * g)
    m_hat = m / (1.0 - ADAM_B1 ** ADAM_STEP)
    v_hat = v / (1.0 - ADAM_B2 ** ADAM_STEP)
    return -ADAM_LR * (m_hat / (jnp.sqrt(v_hat) + ADAM_EPS) + ADAM_WD * w), m, v


def _adamw(w, g, m, v, name):
    r, c = w.shape
    tr = 128 if r % 128 == 0 else r

    def body(w_ref, g_ref, m_ref, v_ref, d_ref, nm_ref, nv_ref):
        d_ref[...], nm_ref[...], nv_ref[...] = _adamw_math(w_ref[...], g_ref[...], m_ref[...], v_ref[...])

    spec = pl.BlockSpec((tr, c), lambda i: (i, 0))
    return pl.pallas_call(
        body, name=name, grid=(r // tr,), in_specs=[spec] * 4, out_specs=[spec] * 3,
        out_shape=[jax.ShapeDtypeStruct((r, c), F32)] * 3, compiler_params=_params("arbitrary"),
    )(w, g, m, v)


def _adamw_small(ws, gs, ms, vs):
    n = len(ws)

    def body(*refs):
        ins, outs = refs[:4 * n], refs[4 * n:]
        for k in range(n):
            d, m, v = _adamw_math(ins[k][...], ins[n + k][...], ins[2 * n + k][...], ins[3 * n + k][...])
            outs[k][...], outs[n + k][...], outs[2 * n + k][...] = d, m, v

    shapes = [jax.ShapeDtypeStruct(w.shape, F32) for w in ws]
    res = pl.pallas_call(body, name="adamw_small", out_shape=shapes * 3)(*ws, *gs, *ms, *vs)
    return res[:n], res[n:2 * n], res[2 * n:]


def _fold_heads(partials):
    t = jnp.sum(partials[:, 0, :], axis=0)
    return (t[:HEAD_DIM] + t[HEAD_DIM:]).reshape(1, HEAD_DIM)


def _local_step(x, target, norm_gain, w_t, w_a, w_b, w_o, b_m, q_norm_a, k_norm_a, q_norm_b, k_norm_b, sink_a,
                rel_bias):
    two = lambda gain: jnp.concatenate([gain, gain], axis=1)
    bias_a = _bias_table(rel_bias[:, :8], A_HALF_WINDOW, 1)
    bias_b = jnp.concatenate([_bias_table(rel_bias[:, 8 + 8 * g:16 + 8 * g], B_HALF_WINDOW, d)
                              for g, d in enumerate(B_DILATIONS)], axis=0)

    qkv, h = _in_proj(x, norm_gain, w_t, 0, QKV_WIDTH // 256, BF16, "in_proj_qkv")
    gates, _ = _in_proj(x, norm_gain, w_t, QKV_WIDTH // 256, GATE_WIDTH // 256, F32, "in_proj_gates")
    out_a, lse_a = _attn_a_fwd(qkv, two(q_norm_a), two(k_norm_a), bias_a, sink_a)
    out_b, lse_b = _attn_b_fwd(qkv, two(q_norm_b), two(k_norm_b), bias_b)

    dy, dgates, d_out_a, d_out_b, d_wa, d_wb, d_wo, d_bm, sq = _middle(
        out_a, out_b, gates, x, target, w_a, w_b, w_o, b_m)
    loss = (0.5 / D_MODEL) * jnp.sum(sq)

    dqa, dka, dva, dgq_a, dgk_a, ds_a, dsink = _attn_a_bwd(
        qkv, two(q_norm_a), two(k_norm_a), bias_a, sink_a, out_a, lse_a, d_out_a)
    dqkv_a = jnp.concatenate([dqa[:, :512], dka[:, 512:640], dva[:, 640:768]], axis=1)
    dqb, dkb, dvb, dgq_b, dgk_b, ds_b = _attn_b_bwd(
        qkv, two(q_norm_b), two(k_norm_b), bias_b, out_b, lse_b, d_out_b)
    dqkv_b = jnp.concatenate([dqb[:, :1536], dkb[:, 1536:3072], dvb[:, 3072:]], axis=1)

    d_wt = _d_w_in(dqkv_a, dqkv_b, dgates, h)
    grad_x, d_gain = _d_x(dqkv_a, dqkv_b, dgates, w_t, x, norm_gain, dy)

    d_rel = jnp.concatenate(
        [_bias_grad(ds_a, A_HALF_WINDOW, 1)]
        + [_bias_grad(ds_b[4 * g:4 * g + 4], B_HALF_WINDOW, d) for g, d in enumerate(B_DILATIONS)], axis=1)
    d_sink = jnp.sum(dsink, axis=(2, 3)).reshape(1, 8)
    dgk_a_row = dgk_a[0]
    small = jnp.zeros((8, D_MODEL), F32)
    small = small.at[0].set(d_gain[0])
    small = small.at[1].set(d_rel.reshape(-1))
    misc = jnp.concatenate([_fold_heads(dgq_a), (dgk_a_row[:HEAD_DIM] + dgk_a_row[HEAD_DIM:]).reshape(1, HEAD_DIM),
                            _fold_heads(dgq_b), _fold_heads(dgk_b), d_sink], axis=1)
    small = small.at[2, :264].set(misc[0])

    pad = jnp.zeros((N_CHIPS, GRAD_ROWS - 2626, D_MODEL), F32)
    d_bm_rows = jnp.pad(d_bm.reshape(2, N_CHIPS, 256).transpose(1, 0, 2), ((0, 0), (0, 0), (0, D_MODEL - 256)))
    big = jnp.concatenate([d_wt.reshape(N_CHIPS, W_IN_SHARD, D_MODEL), d_wo.reshape(N_CHIPS, 256, D_MODEL),
                           d_wa.reshape(N_CHIPS, 128, D_MODEL), d_wb.reshape(N_CHIPS, 128, D_MODEL),
                           d_bm_rows, pad], axis=1)
    return loss, grad_x, big, small


def _unpack_weights(w_t_all, small_all):
    sm = small_all.reshape(N_CHIPS, SMALL_ROWS, D_MODEL)
    w_o = sm[:, 0:256].reshape(D_MODEL, D_MODEL)
    w_a = sm[:, 256:384].reshape(N_CHIPS, 512, 256).transpose(1, 0, 2).reshape(512, D_MODEL)
    w_b = sm[:, 384:512].reshape(N_CHIPS, 512, 256).transpose(1, 0, 2).reshape(512, D_MODEL)
    b_m = lax.bitcast_convert_type(sm[:, 512].reshape(N_CHIPS, 2, 256, 2), F32)
    return w_t_all, w_a, w_b, w_o, b_m.transpose(1, 0, 2).reshape(2, D_MODEL)


def _pack_small_weights(w_branch_a, w_branch_b, b_merge, w_out):
    b_m = lax.bitcast_convert_type(b_merge, BF16).reshape(1, D_MODEL)
    return jnp.concatenate([w_out.astype(BF16), w_branch_a.astype(BF16).reshape(128, D_MODEL),
                            w_branch_b.astype(BF16).reshape(128, D_MODEL), b_m,
                            jnp.zeros((SMALL_ROWS - 513, D_MODEL), BF16)], axis=0)


def kernel(x, norm_gain, w_in, q_norm_a, k_norm_a, q_norm_b, k_norm_b, sink_a, rel_bias, w_branch_a, w_branch_b, b_merge, w_out, loss_target, m_norm_gain, m_w_in, m_q_norm_a, m_k_norm_a, m_q_norm_b, m_k_norm_b, m_sink_a, m_rel_bias, m_w_branch_a, m_w_branch_b, m_b_merge, m_w_out, v_norm_gain, v_w_in, v_q_norm_a, v_k_norm_a, v_q_norm_b, v_k_norm_b, v_sink_a, v_rel_bias, v_w_branch_a, v_w_branch_b, v_b_merge, v_w_out):
    w_in, w_branch_a, w_branch_b, b_merge, w_out = w_in[0], w_branch_a[0], w_branch_b[0], b_merge[0], w_out[0]

    wt_shard = _transpose_cast(w_in, BF16, "w_in_transpose")
    w_t, w_a, w_b, w_o, b_m = _unpack_weights(
        *_gather_weights(wt_shard, _pack_small_weights(w_branch_a, w_branch_b, b_merge, w_out)))

    loss_part, grad_x, big, small = _local_step(
        x[0], loss_target[0], norm_gain, w_t, w_a, w_b, w_o, b_m, q_norm_a, k_norm_a, q_norm_b, k_norm_b,
        sink_a, rel_bias)
    loss = lax.psum(loss_part, ("x", "y", "c"))

    place = _my_place()
    chip_sums = _add_halves(place, big, _swap_halves(big))
    shard = _join_halves(_add_chips(place, chip_sums, _scatter_chips(chip_sums)))
    small = _sum_devices(_gather_small(small))

    g_w_in = _transpose_cast(shard[:W_IN_SHARD], F32, "grad_w_in_transpose")
    g_w_out = shard[2112:2368]
    g_w_a = shard[2368:2496].reshape(512, 256)
    g_w_b = shard[2496:2624].reshape(512, 256)
    g_b_merge = shard[2624:2626, :256]
    g_norm_gain = small[0:1]
    g_rel_bias = small[1].reshape(N_BUCKETS, N_BUCKETS)
    g_q_a, g_k_a, g_q_b, g_k_b = (small[2:3, 64 * k:64 * k + 64] for k in range(4))
    g_sink = small[2:3, 256:264]

    big_names = (("w_in", w_in, g_w_in, m_w_in[0], v_w_in[0]),
                 ("w_branch_a", w_branch_a, g_w_a, m_w_branch_a[0], v_w_branch_a[0]),
                 ("w_branch_b", w_branch_b, g_w_b, m_w_branch_b[0], v_w_branch_b[0]),
                 ("w_out", w_out, g_w_out, m_w_out[0], v_w_out[0]))
    upd = {name: (g,) + tuple(_adamw(w, g, m, v, "adamw_" + name)) for name, w, g, m, v in big_names}
    small_names = ("norm_gain", "q_norm_a", "k_norm_a", "q_norm_b", "k_norm_b", "sink_a", "rel_bias", "b_merge")
    ws = [norm_gain, q_norm_a, k_norm_a, q_norm_b, k_norm_b, sink_a, rel_bias, b_merge]
    gs = [g_norm_gain, g_q_a, g_k_a, g_q_b, g_k_b, g_sink, g_rel_bias, g_b_merge]
    ms = [m_norm_gain, m_q_norm_a, m_k_norm_a, m_q_norm_b, m_k_norm_b, m_sink_a, m_rel_bias, m_b_merge[0]]
    vs = [v_norm_gain, v_q_norm_a, v_k_norm_a, v_q_norm_b, v_k_norm_b, v_sink_a, v_rel_bias, v_b_merge[0]]
    ds, nms, nvs = _adamw_small(ws, gs, ms, vs)
    for k, name in enumerate(small_names):
        upd[name] = (gs[k], ds[k], nms[k], nvs[k])

    order = ("norm_gain", "w_in", "q_norm_a", "k_norm_a", "q_norm_b", "k_norm_b", "sink_a", "rel_bias",
             "w_branch_a", "w_branch_b", "b_merge", "w_out")
    lead = {"w_in", "w_branch_a", "w_branch_b", "b_merge", "w_out"}
    outs = [loss, grad_x[None]]
    for part in range(4):
        outs += [upd[name][part][None] if name in lead else upd[name][part] for name in order]
    return tuple(outs)
```

```python
import math

import numpy as np
import jax
import jax.numpy as jnp
from jax import lax
from jax.experimental import pallas as pl
from jax.experimental.pallas import tpu as pltpu

F32 = jnp.float32
BF16 = jnp.bfloat16

SEQ = 4096
D_MODEL = 1024
HEAD_DIM = 64
LANES = 128
EPS = 1e-6
NEG_INF = -1e30
SCALE = HEAD_DIM ** -0.5
N_BUCKETS = 32
MAX_DISTANCE = 1024
N_CHIPS = 4

A_HALF_WINDOW = 128
B_HALF_WINDOW = 64
B_DILATIONS = (1, 4, 16)
Q_BLOCK = 128

QKV_WIDTH = 5376
GATE_WIDTH = 3072
QA_BLK, KA_BLK, VA_BLK = 0, 4, 5
QB_BLK, KB_BLK, VB_BLK = 6, 18, 30
IN_WIDTH = QKV_WIDTH + GATE_WIDTH
W_IN_SHARD = IN_WIDTH // N_CHIPS

SMALL_ROWS = 544
GRAD_ROWS = 2688
GRAD_HALF = GRAD_ROWS // 2

ADAM_LR = 0.001
ADAM_B1 = 0.9
ADAM_B2 = 0.999
ADAM_EPS = 1e-08
ADAM_WD = 0.01
ADAM_STEP = 10

VMEM_LIMIT = 56 * 1024 * 1024

NT = (((1,), (1,)), ((), ()))
TN = (((0,), (0,)), ((), ()))
MESH = pl.DeviceIdType.MESH
ANY = pl.BlockSpec(memory_space=pl.ANY)


def _dot(a, b, dims=None):
    if dims is None:
        return jnp.dot(a, b, preferred_element_type=F32)
    return lax.dot_general(a, b, dims, preferred_element_type=F32)


def _params(*semantics):
    return pltpu.CompilerParams(dimension_semantics=semantics or None, vmem_limit_bytes=VMEM_LIMIT)


def _bucket_onehot(half_window, stride):
    w = Q_BLOCK + 2 * half_window
    rel = (np.arange(w)[None, :] - half_window - np.arange(Q_BLOCK)[:, None])
    band = np.abs(rel) <= half_window
    rel = rel * stride
    half, max_exact = N_BUCKETS // 2, N_BUCKETS // 4
    n = np.abs(rel)
    nf = np.maximum(n, max_exact).astype(np.float32)
    large = max_exact + (np.log(nf / np.float32(max_exact)) / np.float32(math.log(MAX_DISTANCE / max_exact))
                         * np.float32(half - max_exact)).astype(np.int32)
    large = np.minimum(large, half - 1)
    bucket = (rel > 0).astype(np.int32) * half + np.where(n < max_exact, n, large)
    onehot = (bucket[..., None] == np.arange(N_BUCKETS)) & band[..., None]
    return onehot.reshape(Q_BLOCK * w, N_BUCKETS).astype(np.float32), band


def _bias_table(rel_bias_cols, half_window, stride):
    onehot, band = _bucket_onehot(half_window, stride)
    h = rel_bias_cols.shape[1]
    w = Q_BLOCK + 2 * half_window
    t = jnp.einsum("pb,bh->hp", jnp.asarray(onehot), rel_bias_cols, precision=lax.Precision.HIGHEST)
    t = t.reshape(h, Q_BLOCK, w) + jnp.asarray(np.where(band, 0.0, NEG_INF).astype(np.float32))
    return t.reshape(h // 2, 2, Q_BLOCK, w)


def _bias_grad(ds_sum, half_window, stride):
    onehot, _ = _bucket_onehot(half_window, stride)
    h = ds_sum.shape[0] * 2
    return jnp.einsum("pb,hp->bh", jnp.asarray(onehot), ds_sum.reshape(h, -1), precision=lax.Precision.HIGHEST)


def _transpose_cast(w, out_dtype, name):
    r, c = w.shape

    def body(w_ref, o_ref):
        o_ref[...] = w_ref[...].T.astype(out_dtype)

    if r % LANES == 0:
        steps = pl.cdiv(c, LANES)
        in_spec, out_spec = pl.BlockSpec((r, LANES), lambda j: (0, j)), pl.BlockSpec((LANES, r), lambda j: (j, 0))
    else:
        steps = pl.cdiv(r, LANES)
        in_spec, out_spec = pl.BlockSpec((LANES, c), lambda j: (j, 0)), pl.BlockSpec((c, LANES), lambda j: (0, j))
    return pl.pallas_call(
        body, name=name, grid=(steps,), in_specs=[in_spec], out_specs=out_spec,
        out_shape=jax.ShapeDtypeStruct((c, r), out_dtype),
        compiler_params=_params("arbitrary"),
    )(w)


def _gather_weights(wt_shard, small_shard):
    bufs = ((W_IN_SHARD, IN_WIDTH), (SMALL_ROWS, N_CHIPS * SMALL_ROWS))

    def body(wt_in, sm_in, wt_out, sm_out, send_sems, recv_sems, local_sems):
        x, y, c = lax.axis_index("x"), lax.axis_index("y"), lax.axis_index("c")
        sibling = (x, y, 1 - c)
        chips = [(1 - x, y), (x, 1 - y), (1 - x, 1 - y)]
        my_chip = 2 * x + y
        refs = ((wt_in, wt_out), (sm_in, sm_out))

        def half_of(b, chip, half):
            rows = bufs[b][0]
            start = pl.multiple_of(chip * rows + half * (rows // 2), 16)
            return refs[b][1].at[pl.ds(start, rows // 2), :]

        def copy(k, src, dst, to):
            return pltpu.make_async_remote_copy(src_ref=src, dst_ref=dst, send_sem=send_sems.at[k],
                                                recv_sem=recv_sems.at[k], device_id=to, device_id_type=MESH)

        local, first, passed = [], [], []
        for b in range(2):
            rows = bufs[b][0]
            mine = refs[b][1].at[pl.ds(pl.multiple_of(my_chip * rows, 16), rows), :]
            local.append(pltpu.make_async_copy(refs[b][0], mine, local_sems.at[b]))
            src = refs[b][0].at[pl.ds(pl.multiple_of(c * (rows // 2), 16), rows // 2), :]
            for j, chip in enumerate(chips):
                first.append(copy(3 * b + j, src, half_of(b, my_chip, c), (*chip, c)))
        for cp in local + first:
            cp.start()
        for b in range(2):
            for j, (cx, cy) in enumerate(chips):
                landed = half_of(b, 2 * cx + cy, c)
                copy(3 * b + j, landed, landed, sibling).wait_recv()
                fwd = copy(6 + 3 * b + j, landed, landed, sibling)
                fwd.start()
                passed.append(fwd)
        for b in range(2):
            for j, (cx, cy) in enumerate(chips):
                other = half_of(b, 2 * cx + cy, 1 - c)
                copy(6 + 3 * b + j, other, other, sibling).wait_recv()
        for cp in first + passed:
            cp.wait_send()
        for cp in local:
            cp.wait()

    return pl.pallas_call(
        body, name="gather_weights",
        in_specs=[ANY, ANY], out_specs=[ANY, ANY],
        out_shape=[jax.ShapeDtypeStruct((bufs[0][1], D_MODEL), BF16),
                   jax.ShapeDtypeStruct((bufs[1][1], D_MODEL), BF16)],
        scratch_shapes=[pltpu.SemaphoreType.DMA((12,)), pltpu.SemaphoreType.DMA((12,)),
                        pltpu.SemaphoreType.DMA((2,))],
    )(wt_shard, small_shard)


def _in_proj(x, gain, w_t, first_block, n_blocks, out_dtype, name):
    tm, tn = 1024, 256

    def body(x_ref, g_ref, w_ref, o_ref, h_ref):
        @pl.when(pl.program_id(1) == 0)
        def _():
            xf = x_ref[...]
            r = lax.rsqrt(jnp.mean(xf * xf, axis=-1, keepdims=True) + EPS)
            h_ref[...] = ((xf * r) * g_ref[...]).astype(BF16)

        o_ref[...] = _dot(h_ref[...], w_ref[...], NT).astype(out_dtype)

    return pl.pallas_call(
        body, name=name, grid=(SEQ // tm, n_blocks),
        in_specs=[pl.BlockSpec((tm, D_MODEL), lambda i, j: (i, 0)),
                  pl.BlockSpec((1, D_MODEL), lambda i, j: (0, 0)),
                  pl.BlockSpec((tn, D_MODEL), lambda i, j: (j + first_block, 0))],
        out_specs=[pl.BlockSpec((tm, tn), lambda i, j: (i, j)),
                   pl.BlockSpec((tm, D_MODEL), lambda i, j: (i, 0))],
        out_shape=[jax.ShapeDtypeStruct((SEQ, tn * n_blocks), out_dtype),
                   jax.ShapeDtypeStruct((SEQ, D_MODEL), BF16)],
        compiler_params=_params("arbitrary", "arbitrary"),
    )(x, gain, w_t)


CHUNK = 512
TILE_UNROLL = 4


def _low_half():
    return lax.broadcasted_iota(jnp.int32, (1, LANES), 1) < HEAD_DIM


def _half_sum(v, low):
    s0 = jnp.sum(jnp.where(low, v, 0.0), axis=-1, keepdims=True)
    s1 = jnp.sum(jnp.where(low, 0.0, v), axis=-1, keepdims=True)
    return jnp.where(low, s0, s1)


def _chunks(fn, init=0):
    return lax.fori_loop(0, SEQ // CHUNK, lambda i, carry: fn(pl.multiple_of(i * CHUNK, CHUNK), carry), init)


def _inv_rms(t, low):
    return lax.rsqrt(_half_sum(t * t, low) * (1.0 / HEAD_DIM) + EPS)


def _prep_q(q_ref, gain_ref, qn_ref):
    low = _low_half()

    def step(r0, carry):
        q = q_ref[pl.ds(r0, CHUNK), :].astype(F32)
        qn_ref[pl.ds(r0, CHUNK), :] = ((q * _inv_rms(q, low)) * gain_ref[...]) * SCALE
        return carry

    _chunks(step)


def _own_half(t, keep):
    return jnp.where(keep, t, pltpu.roll(t, HEAD_DIM, 1))


def _prep_kv(k_ref, v_ref, gain_ref, kp_ref, vp_ref, pad, keep=None):
    low = _low_half()
    zeros = jnp.zeros((pad, LANES), F32)
    for ref in (kp_ref, vp_ref):
        ref[pl.ds(0, pad), :] = zeros
        ref[pl.ds(pad + SEQ, pad), :] = zeros

    def step(r0, carry):
        k = k_ref[pl.ds(r0, CHUNK), :].astype(F32)
        v = v_ref[pl.ds(r0, CHUNK), :].astype(F32)
        kn = (k * _inv_rms(k, low)) * gain_ref[...]
        if keep is not None:
            kn, v = _own_half(kn, keep), _own_half(v, keep)
        kp_ref[pl.ds(pad + r0, CHUNK), :] = kn
        vp_ref[pl.ds(pad + r0, CHUNK), :] = v
        return carry

    _chunks(step)


def _tiles(d, half_window, fn):
    w = Q_BLOCK + 2 * half_window
    length = SEQ // d
    n_blocks = length // Q_BLOCK
    col = lax.broadcasted_iota(jnp.int32, (1, w), 1)

    def step(it, carry):
        c, n = it // n_blocks, it % n_blocks
        start = c + (d * Q_BLOCK) * n
        if d == 1:
            start = pl.multiple_of(start, Q_BLOCK)
            q_rows, k_rows = pl.ds(start, Q_BLOCK), pl.ds(start, w)
        else:
            q_rows, k_rows = pl.ds(start, Q_BLOCK, stride=d), pl.ds(start, w, stride=d)
        t = n * Q_BLOCK - half_window + col
        edge = jnp.where((t < 0) | (t >= length), NEG_INF, 0.0)
        fn(q_rows, k_rows, edge)
        return carry

    lax.fori_loop(0, d * n_blocks, step, 0, unroll=TILE_UNROLL)


def _fwd_tiles(qn_ref, kp_ref, vp_ref, bias_ref, emit, *, d, half_window, sinks=None):
    low = _low_half()

    def tile(q_rows, k_rows, edge):
        q = qn_ref[q_rows, :]
        k = kp_ref[k_rows, :].astype(BF16)
        v = vp_ref[k_rows, :].astype(BF16)
        outs, lses = [], []
        for j in range(2):
            mine = low if j == 0 else jnp.logical_not(low)
            s = _dot(jnp.where(mine, q, 0.0).astype(BF16), k, NT) + bias_ref[j] + edge
            m = jnp.max(s, axis=-1, keepdims=True)
            if sinks is not None:
                m = jnp.maximum(m, sinks[j])
            p = jnp.exp(s - m)
            l = jnp.sum(p, axis=-1, keepdims=True)
            if sinks is not None:
                l = l + jnp.exp(sinks[j] - m)
            outs.append(_dot(p.astype(BF16), v) * (1.0 / l))
            lses.append(m + jnp.log(l))
        emit(q_rows, jnp.where(low, outs[0], outs[1]), jnp.where(low, lses[0], lses[1]))

    _tiles(d, half_window, tile)


def _bwd_tiles(qn_ref, kp_ref, vp_ref, bias_ref, do_ref, lse_ref, delta_ref, dq_ref, dk_ref, dv_ref, ds_ref,
               *, d, half_window, sinks=None, dsink_ref=None):
    low = _low_half()

    def tile(q_rows, k_rows, edge):
        q = qn_ref[q_rows, :]
        k = kp_ref[k_rows, :].astype(BF16)
        v = vp_ref[k_rows, :].astype(BF16)
        do = do_ref[q_rows, :]
        lse = lse_ref[q_rows, :]
        delta = delta_ref[q_rows, :]
        dqs, dk, dv = [], None, None
        for j in range(2):
            mine = low if j == 0 else jnp.logical_not(low)
            qj = jnp.where(mine, q, 0.0).astype(BF16)
            doj = jnp.where(mine, do, 0.0).astype(BF16)
            lse_j = lse[:, j * HEAD_DIM:j * HEAD_DIM + 1]
            delta_j = delta[:, j * HEAD_DIM:j * HEAD_DIM + 1]
            p = jnp.exp(_dot(qj, k, NT) + bias_ref[j] + edge - lse_j)
            ds = p * (_dot(doj, v, NT) - delta_j)
            ds_ref[j] += ds
            if sinks is not None:
                dsink_ref[j] += -jnp.exp(sinks[j] - lse_j) * delta_j
            dsb, pb = ds.astype(BF16), p.astype(BF16)
            dqs.append(_dot(dsb, k))
            dkj, dvj = _dot(dsb, qj, TN), _dot(pb, doj, TN)
            dk, dv = (dkj, dvj) if j == 0 else (dk + dkj, dv + dvj)
        dq_ref[q_rows, :] = jnp.where(low, dqs[0], dqs[1])
        dk_ref[k_rows, :] += dk
        dv_ref[k_rows, :] += dv

    _tiles(d, half_window, tile)


def _prep_delta(do_ref, o_ref, delta_ref):
    low = _low_half()

    def step(r0, carry):
        delta_ref[pl.ds(r0, CHUNK), :] = _half_sum(do_ref[pl.ds(r0, CHUNK), :] * o_ref[pl.ds(r0, CHUNK), :], low)
        return carry

    _chunks(step)


def _norm_bwd(raw_ref, gain_ref, dn_ref, dn_offset, out_ref, scale):
    low = _low_half()

    def step(r0, dgain):
        t = raw_ref[pl.ds(r0, CHUNK), :].astype(F32)
        dn = dn_ref[pl.ds(dn_offset + r0, CHUNK), :]
        r = _inv_rms(t, low)
        th = t * r
        dth = dn * (gain_ref[...] * scale)
        out_ref[pl.ds(r0, CHUNK), :] = (r * (dth - th * (_half_sum(dth * th, low) * (1.0 / HEAD_DIM)))).astype(BF16)
        return dgain + jnp.sum(dn * th, axis=0, keepdims=True) * scale

    return _chunks(step, jnp.zeros((1, LANES), F32))


def _rows8(v):
    return jnp.broadcast_to(v, (8, v.shape[-1]))


A_W = Q_BLOCK + 2 * A_HALF_WINDOW
A_PAD = A_HALF_WINDOW


def _seq_block(col_fn):
    return pl.BlockSpec((SEQ, LANES), col_fn)


def _attn_a_fwd(qkv, gain_q, gain_k, bias, sink):
    def body(sink_ref, q_ref, k_ref, v_ref, gq_ref, gk_ref, bias_ref, o_ref, lse_ref, qn_ref, kp_ref, vp_ref):
        hp = pl.program_id(0)
        keep = (lax.broadcasted_iota(jnp.int32, (1, LANES), 1) // HEAD_DIM) == hp // 2
        _prep_q(q_ref, gq_ref, qn_ref)
        _prep_kv(k_ref, v_ref, gk_ref, kp_ref, vp_ref, A_PAD, keep)

        def emit(rows, out, lse):
            o_ref[rows, :] = out
            lse_ref[rows, :] = lse

        _fwd_tiles(qn_ref, kp_ref, vp_ref, bias_ref, emit, d=1, half_window=A_HALF_WINDOW,
                   sinks=(sink_ref[2 * hp], sink_ref[2 * hp + 1]))

    vec = pl.BlockSpec((1, LANES), lambda hp, s: (0, 0))
    return pl.pallas_call(
        body, name="attn_a_fwd",
        grid_spec=pltpu.PrefetchScalarGridSpec(
            num_scalar_prefetch=1, grid=(4,),
            in_specs=[_seq_block(lambda hp, s: (0, QA_BLK + hp)), _seq_block(lambda hp, s: (0, KA_BLK)),
                      _seq_block(lambda hp, s: (0, VA_BLK)), vec, vec,
                      pl.BlockSpec((None, 2, Q_BLOCK, A_W), lambda hp, s: (hp, 0, 0, 0))],
            out_specs=[_seq_block(lambda hp, s: (0, hp)), _seq_block(lambda hp, s: (0, hp))],
            scratch_shapes=[pltpu.VMEM((SEQ, LANES), F32), pltpu.VMEM((SEQ + 2 * A_PAD, LANES), F32),
                            pltpu.VMEM((SEQ + 2 * A_PAD, LANES), F32)]),
        out_shape=[jax.ShapeDtypeStruct((SEQ, 512), F32)] * 2,
        compiler_params=_params("arbitrary"),
    )(sink.reshape(8), qkv, qkv, qkv, gain_q, gain_k, bias)


def _attn_a_bwd(qkv, gain_q, gain_k, bias, sink, out, lse, d_out):
    def body(sink_ref, q_ref, k_ref, v_ref, gq_ref, gk_ref, bias_ref, o_ref, lse_ref, do_ref,
             dq_out, dk_out, dv_out, dgq_out, dgk_out, ds_out, dsink_out,
             qn_ref, kp_ref, vp_ref, delta_ref, dq_ref, dk_ref, dv_ref, dk_tot, dv_tot):
        hp = pl.program_id(0)
        kv_head = hp // 2
        keep = (lax.broadcasted_iota(jnp.int32, (1, LANES), 1) // HEAD_DIM) == kv_head
        _prep_q(q_ref, gq_ref, qn_ref)
        _prep_kv(k_ref, v_ref, gk_ref, kp_ref, vp_ref, A_PAD, keep)
        _prep_delta(do_ref, o_ref, delta_ref)
        dk_ref[...] = jnp.zeros_like(dk_ref)
        dv_ref[...] = jnp.zeros_like(dv_ref)
        ds_out[...] = jnp.zeros_like(ds_out)
        dsink_out[...] = jnp.zeros_like(dsink_out)

        @pl.when(hp == 0)
        def _():
            dk_tot[...] = jnp.zeros_like(dk_tot)
            dv_tot[...] = jnp.zeros_like(dv_tot)

        _bwd_tiles(qn_ref, kp_ref, vp_ref, bias_ref, do_ref, lse_ref, delta_ref, dq_ref, dk_ref, dv_ref, ds_out,
                   d=1, half_window=A_HALF_WINDOW, sinks=(sink_ref[2 * hp], sink_ref[2 * hp + 1]),
                   dsink_ref=dsink_out)
        dgq_out[...] = _rows8(_norm_bwd(q_ref, gq_ref, dq_ref, 0, dq_out, SCALE))

        def fold(r0, carry):
            rows = pl.ds(A_PAD + r0, CHUNK)
            for acc, tot in ((dk_ref, dk_tot), (dv_ref, dv_tot)):
                t = acc[rows, :]
                tot[pl.ds(r0, CHUNK), :] += jnp.where(keep, t + pltpu.roll(t, HEAD_DIM, 1), 0.0)
            return carry

        _chunks(fold)

        @pl.when(hp == 3)
        def _():
            dgk_out[...] = _rows8(_norm_bwd(k_ref, gk_ref, dk_tot, 0, dk_out, 1.0))
            dv_out[...] = dv_tot[...].astype(BF16)

    vec = pl.BlockSpec((1, LANES), lambda hp, s: (0, 0))
    seq_f32 = pltpu.VMEM((SEQ, LANES), F32)
    padded = pltpu.VMEM((SEQ + 2 * A_PAD, LANES), F32)
    return pl.pallas_call(
        body, name="attn_a_bwd",
        grid_spec=pltpu.PrefetchScalarGridSpec(
            num_scalar_prefetch=1, grid=(4,),
            in_specs=[_seq_block(lambda hp, s: (0, QA_BLK + hp)), _seq_block(lambda hp, s: (0, KA_BLK)),
                      _seq_block(lambda hp, s: (0, VA_BLK)), vec, vec,
                      pl.BlockSpec((None, 2, Q_BLOCK, A_W), lambda hp, s: (hp, 0, 0, 0)),
                      _seq_block(lambda hp, s: (0, hp)), _seq_block(lambda hp, s: (0, hp)),
                      _seq_block(lambda hp, s: (0, hp))],
            out_specs=[_seq_block(lambda hp, s: (0, hp)), _seq_block(lambda hp, s: (0, KA_BLK)),
                       _seq_block(lambda hp, s: (0, VA_BLK)),
                       pl.BlockSpec((None, 8, LANES), lambda hp, s: (hp, 0, 0)),
                       pl.BlockSpec((8, LANES), lambda hp, s: (0, 0)),
                       pl.BlockSpec((None, 2, Q_BLOCK, A_W), lambda hp, s: (hp, 0, 0, 0)),
                       pl.BlockSpec((None, 2, Q_BLOCK, 1), lambda hp, s: (hp, 0, 0, 0))],
            scratch_shapes=[seq_f32, padded, padded, seq_f32, seq_f32, padded, padded, seq_f32, seq_f32]),
        out_shape=[jax.ShapeDtypeStruct((SEQ, 768), BF16)] * 3
        + [jax.ShapeDtypeStruct((4, 8, LANES), F32), jax.ShapeDtypeStruct((8, LANES), F32),
           jax.ShapeDtypeStruct((4, 2, Q_BLOCK, A_W), F32), jax.ShapeDtypeStruct((4, 2, Q_BLOCK, 1), F32)],
        compiler_params=_params("arbitrary"),
    )(sink.reshape(8), qkv, qkv, qkv, gain_q, gain_k, bias, out, lse, d_out)


B_W = Q_BLOCK + 2 * B_HALF_WINDOW
B_PAD_MAX = B_HALF_WINDOW * B_DILATIONS[-1]


def _attn_b_fwd(qkv, gain_q, gain_k, bias):
    def body(q_ref, k_ref, v_ref, gq_ref, gk_ref, bias_ref, o_ref, lse_ref, qn_ref, kp_ref, vp_ref):
        g = pl.program_id(1)
        _prep_q(q_ref, gq_ref, qn_ref)

        def first(rows, out, lse):
            o_ref[rows, :] = out
            lse_ref[rows, :] = lse

        def combine(rows, out, lse):
            old = lse_ref[rows, :]
            new = jnp.maximum(old, lse) + jnp.log(1.0 + jnp.exp(-jnp.abs(old - lse)))
            o_ref[rows, :] = o_ref[rows, :] * jnp.exp(old - new) + out * jnp.exp(lse - new)
            lse_ref[rows, :] = new

        for gi, d in enumerate(B_DILATIONS):
            @pl.when(g == gi)
            def _():
                _prep_kv(k_ref, v_ref, gk_ref, kp_ref, vp_ref, B_HALF_WINDOW * d)
                _fwd_tiles(qn_ref, kp_ref, vp_ref, bias_ref, first if gi == 0 else combine,
                           d=d, half_window=B_HALF_WINDOW)

    vec = pl.BlockSpec((1, LANES), lambda hp, g: (0, 0))
    padded = pltpu.VMEM((SEQ + 2 * B_PAD_MAX, LANES), F32)
    return pl.pallas_call(
        body, name="attn_b_fwd", grid=(4, 3),
        in_specs=[_seq_block(lambda hp, g: (0, QB_BLK + 4 * g + hp)), _seq_block(lambda hp, g: (0, KB_BLK + 4 * g + hp)),
                  _seq_block(lambda hp, g: (0, VB_BLK + 4 * g + hp)), vec, vec,
                  pl.BlockSpec((None, 2, Q_BLOCK, B_W), lambda hp, g: (4 * g + hp, 0, 0, 0))],
        out_specs=[_seq_block(lambda hp, g: (0, hp)), _seq_block(lambda hp, g: (0, hp))],
        out_shape=[jax.ShapeDtypeStruct((SEQ, 512), F32)] * 2,
        scratch_shapes=[pltpu.VMEM((SEQ, LANES), F32), padded, padded],
        compiler_params=_params("arbitrary", "arbitrary"),
    )(qkv, qkv, qkv, gain_q, gain_k, bias)


def _attn_b_bwd(qkv, gain_q, gain_k, bias, out, lse, d_out):
    def body(q_ref, k_ref, v_ref, gq_ref, gk_ref, bias_ref, o_ref, lse_ref, do_ref,
             dq_out, dk_out, dv_out, dgq_out, dgk_out, ds_out,
             qn_ref, kp_ref, vp_ref, delta_ref, dq_ref, dk_ref, dv_ref):
        g = pl.program_id(1)
        _prep_q(q_ref, gq_ref, qn_ref)
        _prep_delta(do_ref, o_ref, delta_ref)
        dk_ref[...] = jnp.zeros_like(dk_ref)
        dv_ref[...] = jnp.zeros_like(dv_ref)
        ds_out[...] = jnp.zeros_like(ds_out)
        for gi, d in enumerate(B_DILATIONS):
            @pl.when(g == gi)
            def _():
                pad = B_HALF_WINDOW * d
                _prep_kv(k_ref, v_ref, gk_ref, kp_ref, vp_ref, pad)
                _bwd_tiles(qn_ref, kp_ref, vp_ref, bias_ref, do_ref, lse_ref, delta_ref, dq_ref, dk_ref, dv_ref,
                           ds_out, d=d, half_window=B_HALF_WINDOW)
                dgk_out[...] = _rows8(_norm_bwd(k_ref, gk_ref, dk_ref, pad, dk_out, 1.0))
                dv_out[...] = dv_ref[pl.ds(pad, SEQ), :].astype(BF16)
        dgq_out[...] = _rows8(_norm_bwd(q_ref, gq_ref, dq_ref, 0, dq_out, SCALE))

    vec = pl.BlockSpec((1, LANES), lambda hp, g: (0, 0))
    seq_f32 = pltpu.VMEM((SEQ, LANES), F32)
    padded = pltpu.VMEM((SEQ + 2 * B_PAD_MAX, LANES), F32)
    part = pl.BlockSpec((None, 8, LANES), lambda hp, g: (4 * g + hp, 0, 0))
    return pl.pallas_call(
        body, name="attn_b_bwd", grid=(4, 3),
        in_specs=[_seq_block(lambda hp, g: (0, QB_BLK + 4 * g + hp)), _seq_block(lambda hp, g: (0, KB_BLK + 4 * g + hp)),
                  _seq_block(lambda hp, g: (0, VB_BLK + 4 * g + hp)), vec, vec,
                  pl.BlockSpec((None, 2, Q_BLOCK, B_W), lambda hp, g: (4 * g + hp, 0, 0, 0)),
                  _seq_block(lambda hp, g: (0, hp)), _seq_block(lambda hp, g: (0, hp)), _seq_block(lambda hp, g: (0, hp))],
        out_specs=[_seq_block(lambda hp, g: (0, 4 * g + hp)), _seq_block(lambda hp, g: (0, 12 + 4 * g + hp)),
                   _seq_block(lambda hp, g: (0, 24 + 4 * g + hp)), part, part,
                   pl.BlockSpec((None, 2, Q_BLOCK, B_W), lambda hp, g: (4 * g + hp, 0, 0, 0))],
        out_shape=[jax.ShapeDtypeStruct((SEQ, 4608), BF16)] * 3
        + [jax.ShapeDtypeStruct((12, 8, LANES), F32)] * 2 + [jax.ShapeDtypeStruct((12, 2, Q_BLOCK, B_W), F32)],
        scratch_shapes=[seq_f32, padded, padded, seq_f32, seq_f32, padded, padded],
        compiler_params=_params("arbitrary", "arbitrary"),
    )(qkv, qkv, qkv, gain_q, gain_k, bias, out, lse, d_out)


def _sigmoid(t):
    return 1.0 / (1.0 + jnp.exp(-t))


def _middle(out_a, out_b, gates, x, target, w_a, w_b, w_out, b_merge):
    tm = 256
    n_steps = SEQ // tm

    def body(oa_ref, ob_ref, g_ref, x_ref, t_ref, wa_ref, wb_ref, wo_ref, bm_ref,
             dy_ref, dg_ref, doa_ref, dob_ref, dwa_ref, dwb_ref, dwo_ref, dbm_ref, sq_ref):
        @pl.when(pl.program_id(0) == 0)
        def _():
            for ref in (dwa_ref, dwb_ref, dwo_ref, dbm_ref, sq_ref):
                ref[...] = jnp.zeros_like(ref)

        gate_a, gate_b = g_ref[:, 0:512], g_ref[:, 512:1024]
        sig_a, sig_b = _sigmoid(gate_a), _sigmoid(gate_b)
        silu_a, silu_b = gate_a * sig_a, gate_b * sig_b
        oa, ob = oa_ref[...], ob_ref[...]
        ya, yb = (oa * silu_a).astype(BF16), (ob * silu_b).astype(BF16)
        br_a, br_b = _dot(ya, wa_ref[...]), _dot(yb, wb_ref[...])
        m0 = _sigmoid(g_ref[:, 1024:2048] + bm_ref[0:1, :])
        m1 = _sigmoid(g_ref[:, 2048:3072] + bm_ref[1:2, :])
        merged = (m0 * br_a + m1 * br_b).astype(BF16)
        err = (x_ref[...] + _dot(merged, wo_ref[...])) - t_ref[...]
        sq_ref[...] += jnp.sum(err * err, axis=0, keepdims=True)

        dy = err * (1.0 / D_MODEL)
        dy_ref[...] = dy
        dyb = dy.astype(BF16)
        dmerged = _dot(dyb, wo_ref[...], NT)
        dwo_ref[...] += _dot(merged, dyb, TN)
        dbr_a, dbr_b = (dmerged * m0).astype(BF16), (dmerged * m1).astype(BF16)
        dm0 = (dmerged * br_a) * (m0 * (1.0 - m0))
        dm1 = (dmerged * br_b) * (m1 * (1.0 - m1))
        dbm_ref[0:1, :] += jnp.sum(dm0, axis=0, keepdims=True)
        dbm_ref[1:2, :] += jnp.sum(dm1, axis=0, keepdims=True)
        for s in range(N_CHIPS):
            cols = slice(256 * s, 256 * (s + 1))
            dwa_ref[s] += _dot(ya, dbr_a[:, cols], TN)
            dwb_ref[s] += _dot(yb, dbr_b[:, cols], TN)
        dya, dyb_ = _dot(dbr_a, wa_ref[...], NT), _dot(dbr_b, wb_ref[...], NT)
        doa_ref[...] = dya * silu_a
        dob_ref[...] = dyb_ * silu_b
        dg_ref[:, 0:512] = ((dya * oa) * (sig_a * (1.0 + gate_a * (1.0 - sig_a)))).astype(BF16)
        dg_ref[:, 512:1024] = ((dyb_ * ob) * (sig_b * (1.0 + gate_b * (1.0 - sig_b)))).astype(BF16)
        dg_ref[:, 1024:2048] = dm0.astype(BF16)
        dg_ref[:, 2048:3072] = dm1.astype(BF16)

    def rows(width):
        return pl.BlockSpec((tm, width), lambda i: (i, 0))

    def whole(*shape):
        return pl.BlockSpec(shape, lambda i: (0,) * len(shape))

    return pl.pallas_call(
        body, name="middle", grid=(n_steps,),
        in_specs=[rows(512), rows(512), rows(GATE_WIDTH), rows(D_MODEL), rows(D_MODEL),
                  whole(512, D_MODEL), whole(512, D_MODEL), whole(D_MODEL, D_MODEL), whole(2, D_MODEL)],
        out_specs=[rows(D_MODEL), rows(GATE_WIDTH), rows(512), rows(512),
                   whole(N_CHIPS, 512, 256), whole(N_CHIPS, 512, 256), whole(D_MODEL, D_MODEL),
                   whole(2, D_MODEL), whole(1, D_MODEL)],
        out_shape=[jax.ShapeDtypeStruct((SEQ, D_MODEL), F32), jax.ShapeDtypeStruct((SEQ, GATE_WIDTH), BF16),
                   jax.ShapeDtypeStruct((SEQ, 512), F32), jax.ShapeDtypeStruct((SEQ, 512), F32),
                   jax.ShapeDtypeStruct((N_CHIPS, 512, 256), F32), jax.ShapeDtypeStruct((N_CHIPS, 512, 256), F32),
                   jax.ShapeDtypeStruct((D_MODEL, D_MODEL), F32), jax.ShapeDtypeStruct((2, D_MODEL), F32),
                   jax.ShapeDtypeStruct((1, D_MODEL), F32)],
        compiler_params=_params("arbitrary"),
    )(out_a, out_b, gates, x, target, w_a, w_b, w_out, b_merge)


def _three_way(j, edges, refs, fn):
    lo = 0
    for hi, ref in zip(edges, refs):
        @pl.when((j >= lo) & (j < hi))
        def _(ref=ref):
            fn(ref)
        lo = hi


def _d_w_in(dqkv_a, dqkv_b, dgates, h):
    tn = 256
    edges = (768 // tn, QKV_WIDTH // tn, IN_WIDTH // tn)

    def body(a_ref, b_ref, c_ref, h_ref, o_ref):
        def emit(ref):
            o_ref[...] = _dot(ref[...], h_ref[...], TN)

        _three_way(pl.program_id(0), edges, (a_ref, b_ref, c_ref), emit)

    def cols(lo, hi):
        return pl.BlockSpec((SEQ, tn), lambda j: (0, jnp.clip(j - lo, 0, hi - lo - 1)))

    return pl.pallas_call(
        body, name="d_w_in", grid=(edges[2],),
        in_specs=[cols(0, edges[0]), cols(edges[0], edges[1]), cols(edges[1], edges[2]),
                  pl.BlockSpec((SEQ, D_MODEL), lambda j: (0, 0))],
        out_specs=pl.BlockSpec((tn, D_MODEL), lambda j: (j, 0)),
        out_shape=jax.ShapeDtypeStruct((IN_WIDTH, D_MODEL), F32),
        compiler_params=_params("arbitrary"),
    )(dqkv_a, dqkv_b, dgates, h)


def _d_x(dqkv_a, dqkv_b, dgates, w_t, x, gain, dy):
    tm, tk = 512, 768
    edges = (768 // tk, QKV_WIDTH // tk, IN_WIDTH // tk)

    def body(a_ref, b_ref, c_ref, w_ref, x_ref, g_ref, dy_ref, dx_ref, dgain_ref, acc_ref):
        i, j = pl.program_id(0), pl.program_id(1)

        @pl.when(j == 0)
        def _():
            acc_ref[...] = jnp.zeros_like(acc_ref)

        @pl.when((i == 0) & (j == 0))
        def _():
            dgain_ref[...] = jnp.zeros_like(dgain_ref)

        def emit(ref):
            acc_ref[...] += _dot(ref[...], w_ref[...])

        _three_way(j, edges, (a_ref, b_ref, c_ref), emit)

        @pl.when(j == edges[2] - 1)
        def _():
            xf = x_ref[...]
            r = lax.rsqrt(jnp.mean(xf * xf, axis=-1, keepdims=True) + EPS)
            xh = xf * r
            dh = acc_ref[...]
            dxh = dh * g_ref[...]
            dx_ref[...] = r * (dxh - xh * jnp.mean(dxh * xh, axis=-1, keepdims=True)) + dy_ref[...]
            dgain_ref[...] += _rows8(jnp.sum(dh * xh, axis=0, keepdims=True))

    def cols(lo, hi):
        return pl.BlockSpec((tm, tk), lambda i, j: (i, jnp.clip(j - lo, 0, hi - lo - 1)))

    row = pl.BlockSpec((tm, D_MODEL), lambda i, j: (i, 0))
    return pl.pallas_call(
        body, name="d_x", grid=(SEQ // tm, edges[2]),
        in_specs=[cols(0, edges[0]), cols(edges[0], edges[1]), cols(edges[1], edges[2]),
                  pl.BlockSpec((tk, D_MODEL), lambda i, j: (j, 0)), row,
                  pl.BlockSpec((1, D_MODEL), lambda i, j: (0, 0)), row],
        out_specs=[row, pl.BlockSpec((8, D_MODEL), lambda i, j: (0, 0))],
        out_shape=[jax.ShapeDtypeStruct((SEQ, D_MODEL), F32), jax.ShapeDtypeStruct((8, D_MODEL), F32)],
        scratch_shapes=[pltpu.VMEM((tm, D_MODEL), F32)],
        compiler_params=_params("arbitrary", "arbitrary"),
    )(dqkv_a, dqkv_b, dgates, w_t, x, gain, dy)


def _my_place():
    x, y, c = lax.axis_index("x"), lax.axis_index("y"), lax.axis_index("c")
    return jnp.stack([2 * x + y, c]).astype(jnp.int32)


def _swap_halves(grads):
    def body(g_ref, o_ref, send_sem, recv_sem):
        x, y, c = lax.axis_index("x"), lax.axis_index("y"), lax.axis_index("c")
        theirs = g_ref.at[:, pl.ds(pl.multiple_of((1 - c) * GRAD_HALF, 8), GRAD_HALF), :]
        cp = pltpu.make_async_remote_copy(src_ref=theirs, dst_ref=o_ref, send_sem=send_sem, recv_sem=recv_sem,
                                          device_id=(x, y, 1 - c), device_id_type=MESH)
        cp.start()
        cp.wait()

    return pl.pallas_call(
        body, name="reduce_swap_halves", in_specs=[ANY], out_specs=ANY,
        out_shape=jax.ShapeDtypeStruct((N_CHIPS, GRAD_HALF, D_MODEL), F32),
        scratch_shapes=[pltpu.SemaphoreType.DMA, pltpu.SemaphoreType.DMA],
    )(grads)


def _add_halves(place, grads, theirs):
    tr = 336
    n = GRAD_HALF // tr

    def body(place_ref, g_ref, t_ref, o_ref):
        o_ref[...] = (g_ref[...] + t_ref[...]).astype(BF16)

    return pl.pallas_call(
        body, name="reduce_add_halves",
        grid_spec=pltpu.PrefetchScalarGridSpec(
            num_scalar_prefetch=1, grid=(N_CHIPS, n),
            in_specs=[pl.BlockSpec((None, tr, D_MODEL), lambda s, i, p: (s, p[1] * n + i, 0)),
                      pl.BlockSpec((None, tr, D_MODEL), lambda s, i, p: (s, i, 0))],
            out_specs=pl.BlockSpec((None, tr, D_MODEL), lambda s, i, p: (s, i, 0))),
        out_shape=jax.ShapeDtypeStruct((N_CHIPS, GRAD_HALF, D_MODEL), BF16),
        compiler_params=_params("arbitrary", "arbitrary"),
    )(place, grads, theirs)


def _scatter_chips(chip_sums):
    def body(q_ref, o_ref, send_sems, recv_sems):
        x, y, c = lax.axis_index("x"), lax.axis_index("y"), lax.axis_index("c")
        chips = [(1 - x, y), (x, 1 - y), (1 - x, 1 - y)]
        copies = [pltpu.make_async_remote_copy(src_ref=q_ref.at[2 * cx + cy], dst_ref=o_ref.at[j],
                                               send_sem=send_sems.at[j], recv_sem=recv_sems.at[j],
                                               device_id=(cx, cy, c), device_id_type=MESH)
                  for j, (cx, cy) in enumerate(chips)]
        for cp in copies:
            cp.start()
        for cp in copies:
            cp.wait()

    return pl.pallas_call(
        body, name="reduce_scatter_chips", in_specs=[ANY], out_specs=ANY,
        out_shape=jax.ShapeDtypeStruct((3, GRAD_HALF, D_MODEL), BF16),
        scratch_shapes=[pltpu.SemaphoreType.DMA((3,)), pltpu.SemaphoreType.DMA((3,))],
    )(chip_sums)


def _add_chips(place, chip_sums, others):
    tr = 336
    n = GRAD_HALF // tr

    def body(place_ref, q_ref, o_ref, r_ref):
        acc = q_ref[...].astype(F32)
        for j in range(3):
            acc = acc + o_ref[j].astype(F32)
        r_ref[...] = acc

    return pl.pallas_call(
        body, name="reduce_add_chips",
        grid_spec=pltpu.PrefetchScalarGridSpec(
            num_scalar_prefetch=1, grid=(n,),
            in_specs=[pl.BlockSpec((None, tr, D_MODEL), lambda i, p: (p[0], i, 0)),
                      pl.BlockSpec((3, tr, D_MODEL), lambda i, p: (0, i, 0))],
            out_specs=pl.BlockSpec((tr, D_MODEL), lambda i, p: (i, 0))),
        out_shape=jax.ShapeDtypeStruct((GRAD_HALF, D_MODEL), F32),
        compiler_params=_params("arbitrary"),
    )(place, chip_sums, others)


def _join_halves(half):
    def body(h_ref, o_ref, send_sem, recv_sem, local_sem):
        x, y, c = lax.axis_index("x"), lax.axis_index("y"), lax.axis_index("c")
        mine = o_ref.at[pl.ds(pl.multiple_of(c * GRAD_HALF, 8), GRAD_HALF), :]
        other = o_ref.at[pl.ds(pl.multiple_of((1 - c) * GRAD_HALF, 8), GRAD_HALF), :]
        keep = pltpu.make_async_copy(h_ref, mine, local_sem)
        keep.start()
        send = pltpu.make_async_remote_copy(src_ref=h_ref, dst_ref=mine, send_sem=send_sem, recv_sem=recv_sem,
                                            device_id=(x, y, 1 - c), device_id_type=MESH)
        send.start()
        pltpu.make_async_remote_copy(src_ref=h_ref, dst_ref=other, send_sem=send_sem, recv_sem=recv_sem,
                                     device_id=(x, y, 1 - c), device_id_type=MESH).wait_recv()
        send.wait_send()
        keep.wait()

    return pl.pallas_call(
        body, name="reduce_join_halves", in_specs=[ANY], out_specs=ANY,
        out_shape=jax.ShapeDtypeStruct((GRAD_ROWS, D_MODEL), F32),
        scratch_shapes=[pltpu.SemaphoreType.DMA, pltpu.SemaphoreType.DMA, pltpu.SemaphoreType.DMA],
    )(half)


def _gather_small(block):
    rows = block.shape[0]

    def body(b_ref, o_ref, send_sems, recv_sems, local_sem):
        x, y, c = lax.axis_index("x"), lax.axis_index("y"), lax.axis_index("c")
        me, sibling = (x, y, c), (x, y, 1 - c)
        chips = [(1 - x, y), (x, 1 - y), (1 - x, 1 - y)]

        def at(px, py, pc):
            return o_ref.at[pl.ds(pl.multiple_of((4 * px + 2 * py + pc) * rows, 8), rows), :]

        def copy(k, block_of, to, src=None):
            return pltpu.make_async_remote_copy(src_ref=at(*block_of) if src is None else src, dst_ref=at(*block_of),
                                                send_sem=send_sems.at[k], recv_sem=recv_sems.at[k],
                                                device_id=to, device_id_type=MESH)

        mine = pltpu.make_async_copy(b_ref, at(*me), local_sem)
        mine.start()
        first = [copy(0, me, sibling, src=b_ref)]
        first += [copy(1 + j, me, (*chip, c), src=b_ref) for j, chip in enumerate(chips)]
        for cp in first:
            cp.start()
        passed = [copy(4 + j, (*chip, c), sibling) for j, chip in enumerate(chips)]
        for j, chip in enumerate(chips):
            copy(1 + j, (*chip, c), me).wait_recv()
            passed[j].start()
        copy(0, sibling, me).wait_recv()
        for j, chip in enumerate(chips):
            copy(4 + j, (*chip, 1 - c), me).wait_recv()
        for cp in first + passed:
            cp.wait_send()
        mine.wait()

    return pl.pallas_call(
        body, name="gather_small_grads",
        in_specs=[pl.BlockSpec(memory_space=pltpu.VMEM)], out_specs=pl.BlockSpec(memory_space=pltpu.VMEM),
        out_shape=jax.ShapeDtypeStruct((8 * rows, D_MODEL), F32),
        scratch_shapes=[pltpu.SemaphoreType.DMA((7,)), pltpu.SemaphoreType.DMA((7,)), pltpu.SemaphoreType.DMA],
    )(block)


def _sum_devices(blocks):
    def body(b_ref, o_ref):
        acc = b_ref[0:8, :]
        for dev in range(1, 8):
            acc = acc + b_ref[8 * dev:8 * dev + 8, :]
        o_ref[...] = acc

    return pl.pallas_call(body, name="sum_small_grads", out_shape=jax.ShapeDtypeStruct((8, D_MODEL), F32))(blocks)


def _adamw_math(w, g, m, v):
    m = ADAM_B1 * m + (1.0 - ADAM_B1) * g
    v = ADAM_B2 * v + (1.0 - ADAM_B2) * (g * g)
    m_hat = m / (1.0 - ADAM_B1 ** ADAM_STEP)
    v_hat = v / (1.0 - ADAM_B2 ** ADAM_STEP)
    return -ADAM_LR * (m_hat / (jnp.sqrt(v_hat) + ADAM_EPS) + ADAM_WD * w), m, v


def _adamw(w, g, m, v, name):
    r, c = w.shape
    tr = 128 if r % 128 == 0 else r

    def body(w_ref, g_ref, m_ref, v_ref, d_ref, nm_ref, nv_ref):
        d_ref[...], nm_ref[...], nv_ref[...] = _adamw_math(w_ref[...], g_ref[...], m_ref[...], v_ref[...])

    spec = pl.BlockSpec((tr, c), lambda i: (i, 0))
    return pl.pallas_call(
        body, name=name, grid=(r // tr,), in_specs=[spec] * 4, out_specs=[spec] * 3,
        out_shape=[jax.ShapeDtypeStruct((r, c), F32)] * 3, compiler_params=_params("arbitrary"),
    )(w, g, m, v)


def _adamw_small(ws, gs, ms, vs):
    n = len(ws)

    def body(*refs):
        ins, outs = refs[:4 * n], refs[4 * n:]
        for k in range(n):
            d, m, v = _adamw_math(ins[k][...], ins[n + k][...], ins[2 * n + k][...], ins[3 * n + k][...])
            outs[k][...], outs[n + k][...], outs[2 * n + k][...] = d, m, v

    shapes = [jax.ShapeDtypeStruct(w.shape, F32) for w in ws]
    res = pl.pallas_call(body, name="adamw_small", out_shape=shapes * 3)(*ws, *gs, *ms, *vs)
    return res[:n], res[n:2 * n], res[2 * n:]


def _fold_heads(partials):
    t = jnp.sum(partials[:, 0, :], axis=0)
    return (t[:HEAD_DIM] + t[HEAD_DIM:]).reshape(1, HEAD_DIM)


def _local_step(x, target, norm_gain, w_t, w_a, w_b, w_o, b_m, q_norm_a, k_norm_a, q_norm_b, k_norm_b, sink_a,
                rel_bias):
    two = lambda gain: jnp.concatenate([gain, gain], axis=1)
    bias_a = _bias_table(rel_bias[:, :8], A_HALF_WINDOW, 1)
    bias_b = jnp.concatenate([_bias_table(rel_bias[:, 8 + 8 * g:16 + 8 * g], B_HALF_WINDOW, d)
                              for g, d in enumerate(B_DILATIONS)], axis=0)

    qkv, h = _in_proj(x, norm_gain, w_t, 0, QKV_WIDTH // 256, BF16, "in_proj_qkv")
    gates, _ = _in_proj(x, norm_gain, w_t, QKV_WIDTH // 256, GATE_WIDTH // 256, F32, "in_proj_gates")
    out_a, lse_a = _attn_a_fwd(qkv, two(q_norm_a), two(k_norm_a), bias_a, sink_a)
    out_b, lse_b = _attn_b_fwd(qkv, two(q_norm_b), two(k_norm_b), bias_b)

    dy, dgates, d_out_a, d_out_b, d_wa, d_wb, d_wo, d_bm, sq = _middle(
        out_a, out_b, gates, x, target, w_a, w_b, w_o, b_m)
    loss = (0.5 / D_MODEL) * jnp.sum(sq)

    dqa, dka, dva, dgq_a, dgk_a, ds_a, dsink = _attn_a_bwd(
        qkv, two(q_norm_a), two(k_norm_a), bias_a, sink_a, out_a, lse_a, d_out_a)
    dqkv_a = jnp.concatenate([dqa[:, :512], dka[:, 512:640], dva[:, 640:768]], axis=1)
    dqb, dkb, dvb, dgq_b, dgk_b, ds_b = _attn_b_bwd(
        qkv, two(q_norm_b), two(k_norm_b), bias_b, out_b, lse_b, d_out_b)
    dqkv_b = jnp.concatenate([dqb[:, :1536], dkb[:, 1536:3072], dvb[:, 3072:]], axis=1)

    d_wt = _d_w_in(dqkv_a, dqkv_b, dgates, h)
    grad_x, d_gain = _d_x(dqkv_a, dqkv_b, dgates, w_t, x, norm_gain, dy)

    d_rel = jnp.concatenate(
        [_bias_grad(ds_a, A_HALF_WINDOW, 1)]
        + [_bias_grad(ds_b[4 * g:4 * g + 4], B_HALF_WINDOW, d) for g, d in enumerate(B_DILATIONS)], axis=1)
    d_sink = jnp.sum(dsink, axis=(2, 3)).reshape(1, 8)
    dgk_a_row = dgk_a[0]
    small = jnp.zeros((8, D_MODEL), F32)
    small = small.at[0].set(d_gain[0])
    small = small.at[1].set(d_rel.reshape(-1))
    misc = jnp.concatenate([_fold_heads(dgq_a), (dgk_a_row[:HEAD_DIM] + dgk_a_row[HEAD_DIM:]).reshape(1, HEAD_DIM),
                            _fold_heads(dgq_b), _fold_heads(dgk_b), d_sink], axis=1)
    small = small.at[2, :264].set(misc[0])

    d_bm_rows = jnp.pad(d_bm.reshape(2, N_CHIPS, 256).transpose(1, 0, 2),
                        ((0, 0), (0, GRAD_ROWS - 2626), (0, D_MODEL - 256)))
    big = jnp.concatenate([d_wt.reshape(N_CHIPS, W_IN_SHARD, D_MODEL), d_wo.reshape(N_CHIPS, 256, D_MODEL),
                           d_wa.reshape(N_CHIPS, 128, D_MODEL), d_wb.reshape(N_CHIPS, 128, D_MODEL),
                           d_bm_rows], axis=1)
    return loss, grad_x, big, small


def _unpack_weights(w_t_all, small_all):
    sm = small_all.reshape(N_CHIPS, SMALL_ROWS, D_MODEL)
    w_o = sm[:, 0:256].reshape(D_MODEL, D_MODEL)
    w_a = sm[:, 256:384].reshape(N_CHIPS, 512, 256).transpose(1, 0, 2).reshape(512, D_MODEL)
    w_b = sm[:, 384:512].reshape(N_CHIPS, 512, 256).transpose(1, 0, 2).reshape(512, D_MODEL)
    b_m = lax.bitcast_convert_type(sm[:, 512].reshape(N_CHIPS, 2, 256, 2), F32)
    return w_t_all, w_a, w_b, w_o, b_m.transpose(1, 0, 2).reshape(2, D_MODEL)


def _pack_small_weights(w_branch_a, w_branch_b, b_merge, w_out):
    b_m = jnp.pad(lax.bitcast_convert_type(b_merge, BF16).reshape(1, D_MODEL), ((0, SMALL_ROWS - 513), (0, 0)))
    return jnp.concatenate([w_out.astype(BF16), w_branch_a.astype(BF16).reshape(128, D_MODEL),
                            w_branch_b.astype(BF16).reshape(128, D_MODEL), b_m], axis=0)


def kernel(x, norm_gain, w_in, q_norm_a, k_norm_a, q_norm_b, k_norm_b, sink_a, rel_bias, w_branch_a, w_branch_b, b_merge, w_out, loss_target, m_norm_gain, m_w_in, m_q_norm_a, m_k_norm_a, m_q_norm_b, m_k_norm_b, m_sink_a, m_rel_bias, m_w_branch_a, m_w_branch_b, m_b_merge, m_w_out, v_norm_gain, v_w_in, v_q_norm_a, v_k_norm_a, v_q_norm_b, v_k_norm_b, v_sink_a, v_rel_bias, v_w_branch_a, v_w_branch_b, v_b_merge, v_w_out):
    w_in, w_branch_a, w_branch_b, b_merge, w_out = w_in[0], w_branch_a[0], w_branch_b[0], b_merge[0], w_out[0]

    wt_shard = _transpose_cast(w_in, BF16, "w_in_transpose")
    w_t, w_a, w_b, w_o, b_m = _unpack_weights(
        *_gather_weights(wt_shard, _pack_small_weights(w_branch_a, w_branch_b, b_merge, w_out)))

    loss_part, grad_x, big, small = _local_step(
        x[0], loss_target[0], norm_gain, w_t, w_a, w_b, w_o, b_m, q_norm_a, k_norm_a, q_norm_b, k_norm_b,
        sink_a, rel_bias)
    loss = lax.psum(loss_part, ("x", "y", "c"))

    place = _my_place()
    chip_sums = _add_halves(place, big, _swap_halves(big))
    shard = _join_halves(_add_chips(place, chip_sums, _scatter_chips(chip_sums)))
    small = _sum_devices(_gather_small(small))

    g_w_in = _transpose_cast(shard[:W_IN_SHARD], F32, "grad_w_in_transpose")
    g_w_out = shard[2112:2368]
    g_w_a = shard[2368:2496].reshape(512, 256)
    g_w_b = shard[2496:2624].reshape(512, 256)
    g_b_merge = shard[2624:2626, :256]
    g_norm_gain = small[0:1]
    g_rel_bias = small[1].reshape(N_BUCKETS, N_BUCKETS)
    g_q_a, g_k_a, g_q_b, g_k_b = (small[2:3, 64 * k:64 * k + 64] for k in range(4))
    g_sink = small[2:3, 256:264]

    big_names = (("w_in", w_in, g_w_in, m_w_in[0], v_w_in[0]),
                 ("w_branch_a", w_branch_a, g_w_a, m_w_branch_a[0], v_w_branch_a[0]),
                 ("w_branch_b", w_branch_b, g_w_b, m_w_branch_b[0], v_w_branch_b[0]),
                 ("w_out", w_out, g_w_out, m_w_out[0], v_w_out[0]))
    upd = {name: (g,) + tuple(_adamw(w, g, m, v, "adamw_" + name)) for name, w, g, m, v in big_names}
    small_names = ("norm_gain", "q_norm_a", "k_norm_a", "q_norm_b", "k_norm_b", "sink_a", "rel_bias", "b_merge")
    ws = [norm_gain, q_norm_a, k_norm_a, q_norm_b, k_norm_b, sink_a, rel_bias, b_merge]
    gs = [g_norm_gain, g_q_a, g_k_a, g_q_b, g_k_b, g_sink, g_rel_bias, g_b_merge]
    ms = [m_norm_gain, m_q_norm_a, m_k_norm_a, m_q_norm_b, m_k_norm_b, m_sink_a, m_rel_bias, m_b_merge[0]]
    vs = [v_norm_gain, v_q_norm_a, v_k_norm_a, v_q_norm_b, v_k_norm_b, v_sink_a, v_rel_bias, v_b_merge[0]]
    ds, nms, nvs = _adamw_small(ws, gs, ms, vs)
    for k, name in enumerate(small_names):
        upd[name] = (gs[k], ds[k], nms[k], nvs[k])

    order = ("norm_gain", "w_in", "q_norm_a", "k_norm_a", "q_norm_b", "k_norm_b", "sink_a", "rel_bias",
             "w_branch_a", "w_branch_b", "b_merge", "w_out")
    lead = {"w_in", "w_branch_a", "w_branch_b", "b_merge", "w_out"}
    outs = [loss, grad_x[None]]
    for part in range(4):
        outs += [upd[name][part][None] if name in lead else upd[name][part] for name in order]
    return tuple(outs)
```

```python
import math

import numpy as np
import jax
import jax.numpy as jnp
from jax import lax
from jax.experimental import pallas as pl
from jax.experimental.pallas import tpu as pltpu

F32 = jnp.float32
BF16 = jnp.bfloat16

SEQ = 4096
D_MODEL = 1024
HEAD_DIM = 64
LANES = 128
EPS = 1e-6
NEG_INF = -1e30
SCALE = HEAD_DIM ** -0.5
N_BUCKETS = 32
MAX_DISTANCE = 1024
N_CHIPS = 4

A_HALF_WINDOW = 128
B_HALF_WINDOW = 64
B_DILATIONS = (1, 4, 16)
Q_BLOCK = 128

QKV_WIDTH = 5376
GATE_WIDTH = 3072
QA_BLK, KA_BLK, VA_BLK = 0, 4, 5
QB_BLK, KB_BLK, VB_BLK = 6, 18, 30
IN_WIDTH = QKV_WIDTH + GATE_WIDTH
W_IN_SHARD = IN_WIDTH // N_CHIPS

SMALL_ROWS = 544
GRAD_ROWS = 2688
GRAD_HALF = GRAD_ROWS // 2

ADAM_LR = 0.001
ADAM_B1 = 0.9
ADAM_B2 = 0.999
ADAM_EPS = 1e-08
ADAM_WD = 0.01
ADAM_STEP = 10

VMEM_LIMIT = 56 * 1024 * 1024

NT = (((1,), (1,)), ((), ()))
TN = (((0,), (0,)), ((), ()))
MESH = pl.DeviceIdType.MESH
ANY = pl.BlockSpec(memory_space=pl.ANY)


def _dot(a, b, dims=None):
    if dims is None:
        return jnp.dot(a, b, preferred_element_type=F32)
    return lax.dot_general(a, b, dims, preferred_element_type=F32)


def _params(*semantics):
    return pltpu.CompilerParams(dimension_semantics=semantics or None, vmem_limit_bytes=VMEM_LIMIT)


def _bucket_onehot(half_window, stride):
    w = Q_BLOCK + 2 * half_window
    rel = (np.arange(w)[None, :] - half_window - np.arange(Q_BLOCK)[:, None])
    band = np.abs(rel) <= half_window
    rel = rel * stride
    half, max_exact = N_BUCKETS // 2, N_BUCKETS // 4
    n = np.abs(rel)
    nf = np.maximum(n, max_exact).astype(np.float32)
    large = max_exact + (np.log(nf / np.float32(max_exact)) / np.float32(math.log(MAX_DISTANCE / max_exact))
                         * np.float32(half - max_exact)).astype(np.int32)
    large = np.minimum(large, half - 1)
    bucket = (rel > 0).astype(np.int32) * half + np.where(n < max_exact, n, large)
    onehot = (bucket[..., None] == np.arange(N_BUCKETS)) & band[..., None]
    return onehot.reshape(Q_BLOCK * w, N_BUCKETS).astype(np.float32), band


def _bias_table(rel_bias_cols, half_window, stride):
    onehot, band = _bucket_onehot(half_window, stride)
    h = rel_bias_cols.shape[1]
    w = Q_BLOCK + 2 * half_window
    t = jnp.einsum("pb,bh->hp", jnp.asarray(onehot), rel_bias_cols, precision=lax.Precision.HIGHEST)
    t = t.reshape(h, Q_BLOCK, w) + jnp.asarray(np.where(band, 0.0, NEG_INF).astype(np.float32))
    return t.reshape(h // 2, 2, Q_BLOCK, w)


def _bias_grad(ds_sum, half_window, stride):
    onehot, _ = _bucket_onehot(half_window, stride)
    h = ds_sum.shape[0] * 2
    return jnp.einsum("pb,hp->bh", jnp.asarray(onehot), ds_sum.reshape(h, -1), precision=lax.Precision.HIGHEST)


def _transpose_cast(w, out_dtype, name):
    r, c = w.shape

    def body(w_ref, o_ref):
        o_ref[...] = w_ref[...].T.astype(out_dtype)

    if r % LANES == 0:
        steps = pl.cdiv(c, LANES)
        in_spec, out_spec = pl.BlockSpec((r, LANES), lambda j: (0, j)), pl.BlockSpec((LANES, r), lambda j: (j, 0))
    else:
        steps = pl.cdiv(r, LANES)
        in_spec, out_spec = pl.BlockSpec((LANES, c), lambda j: (j, 0)), pl.BlockSpec((c, LANES), lambda j: (0, j))
    return pl.pallas_call(
        body, name=name, grid=(steps,), in_specs=[in_spec], out_specs=out_spec,
        out_shape=jax.ShapeDtypeStruct((c, r), out_dtype),
        compiler_params=_params("arbitrary"),
    )(w)


def _gather_weights(wt_shard, small_shard):
    bufs = ((W_IN_SHARD, IN_WIDTH), (SMALL_ROWS, N_CHIPS * SMALL_ROWS))

    stage_rows = 528

    def body(wt_in, sm_in, wt_out, sm_out, send_sems, recv_sems, in_sems, out_sems, stage):
        x, y, c = lax.axis_index("x"), lax.axis_index("y"), lax.axis_index("c")
        sibling = (x, y, 1 - c)
        chips = [(1 - x, y), (x, 1 - y), (1 - x, 1 - y)]
        my_chip = 2 * x + y
        refs = ((wt_in, wt_out), (sm_in, sm_out))

        def keep_own():
            pieces = [(b, r0) for b in range(2) for r0 in range(0, bufs[b][0], stage_rows)]
            outs = []
            for i, (b, r0) in enumerate(pieces):
                rows = min(stage_rows, bufs[b][0] - r0)
                slot = i % 2
                if i >= 2:
                    outs[i - 2].wait()
                buf = stage.at[slot, pl.ds(0, rows), :]
                load = pltpu.make_async_copy(refs[b][0].at[pl.ds(r0, rows), :], buf, in_sems.at[slot])
                load.start()
                load.wait()
                start = pl.multiple_of(my_chip * bufs[b][0] + r0, 16)
                outs.append(pltpu.make_async_copy(buf, refs[b][1].at[pl.ds(start, rows), :], out_sems.at[slot]))
                outs[i].start()
            for cp in outs[-2:]:
                cp.wait()

        def half_of(b, chip, half):
            rows = bufs[b][0]
            start = pl.multiple_of(chip * rows + half * (rows // 2), 16)
            return refs[b][1].at[pl.ds(start, rows // 2), :]

        def copy(k, src, dst, to):
            return pltpu.make_async_remote_copy(src_ref=src, dst_ref=dst, send_sem=send_sems.at[k],
                                                recv_sem=recv_sems.at[k], device_id=to, device_id_type=MESH)

        first, passed = [], []
        for b in range(2):
            rows = bufs[b][0]
            src = refs[b][0].at[pl.ds(pl.multiple_of(c * (rows // 2), 16), rows // 2), :]
            for j, chip in enumerate(chips):
                first.append(copy(3 * b + j, src, half_of(b, my_chip, c), (*chip, c)))
        for cp in first:
            cp.start()
        keep_own()
        for b in range(2):
            for j, (cx, cy) in enumerate(chips):
                landed = half_of(b, 2 * cx + cy, c)
                copy(3 * b + j, landed, landed, sibling).wait_recv()
                fwd = copy(6 + 3 * b + j, landed, landed, sibling)
                fwd.start()
                passed.append(fwd)
        for b in range(2):
            for j, (cx, cy) in enumerate(chips):
                other = half_of(b, 2 * cx + cy, 1 - c)
                copy(6 + 3 * b + j, other, other, sibling).wait_recv()
        for cp in first + passed:
            cp.wait_send()

    return pl.pallas_call(
        body, name="gather_weights",
        in_specs=[ANY, ANY], out_specs=[ANY, ANY],
        out_shape=[jax.ShapeDtypeStruct((bufs[0][1], D_MODEL), BF16),
                   jax.ShapeDtypeStruct((bufs[1][1], D_MODEL), BF16)],
        scratch_shapes=[pltpu.SemaphoreType.DMA((12,)), pltpu.SemaphoreType.DMA((12,)),
                        pltpu.SemaphoreType.DMA((2,)), pltpu.SemaphoreType.DMA((2,)),
                        pltpu.VMEM((2, stage_rows, D_MODEL), BF16)],
    )(wt_shard, small_shard)


W_BLOCK = 768


def _w_blocks(first, count):
    return [pl.BlockSpec((W_BLOCK, D_MODEL), lambda *_, k=k: (first + k, 0)) for k in range(count)]


def _in_proj(x, gain, w_t, first_block, n_blocks, out_dtype, name):
    tm = 256

    def body(x_ref, g_ref, *refs):
        w_refs, (o_ref, h_ref) = refs[:n_blocks], refs[n_blocks:]
        xf = x_ref[...]
        r = lax.rsqrt(jnp.mean(xf * xf, axis=-1, keepdims=True) + EPS)
        h = ((xf * r) * g_ref[...]).astype(BF16)
        h_ref[...] = h
        for k, w_ref in enumerate(w_refs):
            o_ref[:, k * W_BLOCK:(k + 1) * W_BLOCK] = _dot(h, w_ref[...], NT).astype(out_dtype)

    return pl.pallas_call(
        body, name=name, grid=(SEQ // tm,),
        in_specs=[pl.BlockSpec((tm, D_MODEL), lambda i: (i, 0)), pl.BlockSpec((1, D_MODEL), lambda i: (0, 0))]
        + _w_blocks(first_block, n_blocks),
        out_specs=[pl.BlockSpec((tm, W_BLOCK * n_blocks), lambda i: (i, 0)),
                   pl.BlockSpec((tm, D_MODEL), lambda i: (i, 0))],
        out_shape=[jax.ShapeDtypeStruct((SEQ, W_BLOCK * n_blocks), out_dtype),
                   jax.ShapeDtypeStruct((SEQ, D_MODEL), BF16)],
        compiler_params=_params("arbitrary"),
    )(x, gain, *([w_t] * n_blocks))


CHUNK = 512
TILE_UNROLL = 4


def _low_half():
    return lax.broadcasted_iota(jnp.int32, (1, LANES), 1) < HEAD_DIM


def _half_sum(v, low):
    s0 = jnp.sum(jnp.where(low, v, 0.0), axis=-1, keepdims=True)
    s1 = jnp.sum(jnp.where(low, 0.0, v), axis=-1, keepdims=True)
    return jnp.where(low, s0, s1)


def _chunks(fn, init=0):
    return lax.fori_loop(0, SEQ // CHUNK, lambda i, carry: fn(pl.multiple_of(i * CHUNK, CHUNK), carry), init)


def _inv_rms(t, low):
    return lax.rsqrt(_half_sum(t * t, low) * (1.0 / HEAD_DIM) + EPS)


def _prep_q(q_ref, gain_ref, qn_ref):
    low = _low_half()

    def step(r0, carry):
        q = q_ref[pl.ds(r0, CHUNK), :].astype(F32)
        qn_ref[pl.ds(r0, CHUNK), :] = ((q * _inv_rms(q, low)) * gain_ref[...]) * SCALE
        return carry

    _chunks(step)


def _own_half(t, keep):
    return jnp.where(keep, t, pltpu.roll(t, HEAD_DIM, 1))


def _prep_kv(k_ref, v_ref, gain_ref, kp_ref, vp_ref, pad, keep=None):
    low = _low_half()
    zeros = jnp.zeros((pad, LANES), F32)
    for ref in (kp_ref, vp_ref):
        ref[pl.ds(0, pad), :] = zeros
        ref[pl.ds(pad + SEQ, pad), :] = zeros

    def step(r0, carry):
        k = k_ref[pl.ds(r0, CHUNK), :].astype(F32)
        v = v_ref[pl.ds(r0, CHUNK), :].astype(F32)
        kn = (k * _inv_rms(k, low)) * gain_ref[...]
        if keep is not None:
            kn, v = _own_half(kn, keep), _own_half(v, keep)
        kp_ref[pl.ds(pad + r0, CHUNK), :] = kn
        vp_ref[pl.ds(pad + r0, CHUNK), :] = v
        return carry

    _chunks(step)


def _tiles(d, half_window, fn):
    w = Q_BLOCK + 2 * half_window
    length = SEQ // d
    n_blocks = length // Q_BLOCK
    col = lax.broadcasted_iota(jnp.int32, (1, w), 1)

    def step(it, carry):
        c, n = it // n_blocks, it % n_blocks
        start = c + (d * Q_BLOCK) * n
        if d == 1:
            start = pl.multiple_of(start, Q_BLOCK)
            q_rows, k_rows = pl.ds(start, Q_BLOCK), pl.ds(start, w)
        else:
            q_rows, k_rows = pl.ds(start, Q_BLOCK, stride=d), pl.ds(start, w, stride=d)
        t = n * Q_BLOCK - half_window + col
        edge = jnp.where((t < 0) | (t >= length), NEG_INF, 0.0)
        fn(q_rows, k_rows, edge)
        return carry

    lax.fori_loop(0, d * n_blocks, step, 0, unroll=TILE_UNROLL)


def _fwd_tiles(qn_ref, kp_ref, vp_ref, bias_ref, emit, *, d, half_window, sinks=None):
    low = _low_half()

    def tile(q_rows, k_rows, edge):
        q = qn_ref[q_rows, :]
        k = kp_ref[k_rows, :].astype(BF16)
        v = vp_ref[k_rows, :].astype(BF16)
        outs, lses = [], []
        for j in range(2):
            mine = low if j == 0 else jnp.logical_not(low)
            s = _dot(jnp.where(mine, q, 0.0).astype(BF16), k, NT) + bias_ref[j] + edge
            m = jnp.max(s, axis=-1, keepdims=True)
            if sinks is not None:
                m = jnp.maximum(m, sinks[j])
            p = jnp.exp(s - m)
            l = jnp.sum(p, axis=-1, keepdims=True)
            if sinks is not None:
                l = l + jnp.exp(sinks[j] - m)
            outs.append(_dot(p.astype(BF16), v) * (1.0 / l))
            lses.append(m + jnp.log(l))
        emit(q_rows, jnp.where(low, outs[0], outs[1]), jnp.where(low, lses[0], lses[1]))

    _tiles(d, half_window, tile)


def _bwd_tiles(qn_ref, kp_ref, vp_ref, bias_ref, do_ref, lse_ref, delta_ref, dq_ref, dk_ref, dv_ref, ds_ref,
               *, d, half_window, sinks=None, dsink_ref=None):
    low = _low_half()

    def tile(q_rows, k_rows, edge):
        q = qn_ref[q_rows, :]
        k = kp_ref[k_rows, :].astype(BF16)
        v = vp_ref[k_rows, :].astype(BF16)
        do = do_ref[q_rows, :]
        lse = lse_ref[q_rows, :]
        delta = delta_ref[q_rows, :]
        dqs, dk, dv = [], None, None
        for j in range(2):
            mine = low if j == 0 else jnp.logical_not(low)
            qj = jnp.where(mine, q, 0.0).astype(BF16)
            doj = jnp.where(mine, do, 0.0).astype(BF16)
            lse_j = lse[:, j * HEAD_DIM:j * HEAD_DIM + 1]
            delta_j = delta[:, j * HEAD_DIM:j * HEAD_DIM + 1]
            p = jnp.exp(_dot(qj, k, NT) + bias_ref[j] + edge - lse_j)
            ds = p * (_dot(doj, v, NT) - delta_j)
            ds_ref[j] += ds
            if sinks is not None:
                dsink_ref[j] += -jnp.exp(sinks[j] - lse_j) * delta_j
            dsb, pb = ds.astype(BF16), p.astype(BF16)
            dqs.append(_dot(dsb, k))
            dkj, dvj = _dot(dsb, qj, TN), _dot(pb, doj, TN)
            dk, dv = (dkj, dvj) if j == 0 else (dk + dkj, dv + dvj)
        dq_ref[q_rows, :] = jnp.where(low, dqs[0], dqs[1])
        dk_ref[k_rows, :] += dk
        dv_ref[k_rows, :] += dv

    _tiles(d, half_window, tile)


def _prep_delta(do_ref, o_ref, delta_ref):
    low = _low_half()

    def step(r0, carry):
        delta_ref[pl.ds(r0, CHUNK), :] = _half_sum(do_ref[pl.ds(r0, CHUNK), :] * o_ref[pl.ds(r0, CHUNK), :], low)
        return carry

    _chunks(step)


def _norm_bwd(raw_ref, gain_ref, dn_ref, dn_offset, out_ref, scale):
    low = _low_half()

    def step(r0, dgain):
        t = raw_ref[pl.ds(r0, CHUNK), :].astype(F32)
        dn = dn_ref[pl.ds(dn_offset + r0, CHUNK), :]
        r = _inv_rms(t, low)
        th = t * r
        dth = dn * (gain_ref[...] * scale)
        out_ref[pl.ds(r0, CHUNK), :] = (r * (dth - th * (_half_sum(dth * th, low) * (1.0 / HEAD_DIM)))).astype(BF16)
        return dgain + jnp.sum(dn * th, axis=0, keepdims=True) * scale

    return _chunks(step, jnp.zeros((1, LANES), F32))


def _rows8(v):
    return jnp.broadcast_to(v, (8, v.shape[-1]))


A_W = Q_BLOCK + 2 * A_HALF_WINDOW
A_PAD = A_HALF_WINDOW


def _seq_block(col_fn):
    return pl.BlockSpec((SEQ, LANES), col_fn)


def _attn_a_fwd(qkv, gain_q, gain_k, bias, sink):
    def body(sink_ref, q_ref, k_ref, v_ref, gq_ref, gk_ref, bias_ref, o_ref, lse_ref, qn_ref, kp_ref, vp_ref):
        hp = pl.program_id(0)
        keep = (lax.broadcasted_iota(jnp.int32, (1, LANES), 1) // HEAD_DIM) == hp // 2
        _prep_q(q_ref, gq_ref, qn_ref)
        _prep_kv(k_ref, v_ref, gk_ref, kp_ref, vp_ref, A_PAD, keep)

        def emit(rows, out, lse):
            o_ref[rows, :] = out
            lse_ref[rows, :] = lse

        _fwd_tiles(qn_ref, kp_ref, vp_ref, bias_ref, emit, d=1, half_window=A_HALF_WINDOW,
                   sinks=(sink_ref[2 * hp], sink_ref[2 * hp + 1]))

    vec = pl.BlockSpec((1, LANES), lambda hp, s: (0, 0))
    return pl.pallas_call(
        body, name="attn_a_fwd",
        grid_spec=pltpu.PrefetchScalarGridSpec(
            num_scalar_prefetch=1, grid=(4,),
            in_specs=[_seq_block(lambda hp, s: (0, QA_BLK + hp)), _seq_block(lambda hp, s: (0, KA_BLK)),
                      _seq_block(lambda hp, s: (0, VA_BLK)), vec, vec,
                      pl.BlockSpec((None, 2, Q_BLOCK, A_W), lambda hp, s: (hp, 0, 0, 0))],
            out_specs=[_seq_block(lambda hp, s: (0, hp)), _seq_block(lambda hp, s: (0, hp))],
            scratch_shapes=[pltpu.VMEM((SEQ, LANES), F32), pltpu.VMEM((SEQ + 2 * A_PAD, LANES), F32),
                            pltpu.VMEM((SEQ + 2 * A_PAD, LANES), F32)]),
        out_shape=[jax.ShapeDtypeStruct((SEQ, 512), F32)] * 2,
        compiler_params=_params("arbitrary"),
    )(sink.reshape(8), qkv, qkv, qkv, gain_q, gain_k, bias)


def _attn_a_bwd(qkv, gain_q, gain_k, bias, sink, out, lse, d_out):
    def body(sink_ref, q_ref, k_ref, v_ref, gq_ref, gk_ref, bias_ref, o_ref, lse_ref, do_ref,
             dq_out, dkv_out, dgq_out, dgk_out, ds_out, dsink_out,
             qn_ref, kp_ref, vp_ref, delta_ref, dq_ref, dk_ref, dv_ref, dk_tot, dv_tot):
        hp = pl.program_id(0)
        kv_head = hp // 2
        keep = (lax.broadcasted_iota(jnp.int32, (1, LANES), 1) // HEAD_DIM) == kv_head
        _prep_q(q_ref, gq_ref, qn_ref)
        _prep_kv(k_ref, v_ref, gk_ref, kp_ref, vp_ref, A_PAD, keep)
        _prep_delta(do_ref, o_ref, delta_ref)
        dk_ref[...] = jnp.zeros_like(dk_ref)
        dv_ref[...] = jnp.zeros_like(dv_ref)
        ds_out[...] = jnp.zeros_like(ds_out)
        dsink_out[...] = jnp.zeros_like(dsink_out)

        @pl.when(hp == 0)
        def _():
            dk_tot[...] = jnp.zeros_like(dk_tot)
            dv_tot[...] = jnp.zeros_like(dv_tot)

        _bwd_tiles(qn_ref, kp_ref, vp_ref, bias_ref, do_ref, lse_ref, delta_ref, dq_ref, dk_ref, dv_ref, ds_out,
                   d=1, half_window=A_HALF_WINDOW, sinks=(sink_ref[2 * hp], sink_ref[2 * hp + 1]),
                   dsink_ref=dsink_out)
        dgq_out[...] = _rows8(_norm_bwd(q_ref, gq_ref, dq_ref, 0, dq_out, SCALE))

        def fold(r0, carry):
            rows = pl.ds(A_PAD + r0, CHUNK)
            for acc, tot in ((dk_ref, dk_tot), (dv_ref, dv_tot)):
                t = acc[rows, :]
                tot[pl.ds(r0, CHUNK), :] += jnp.where(keep, t + pltpu.roll(t, HEAD_DIM, 1), 0.0)
            return carry

        _chunks(fold)

        @pl.when(hp == 3)
        def _():
            dgk_out[...] = _rows8(_norm_bwd(k_ref, gk_ref, dk_tot, 0, dkv_out.at[:, pl.ds(0, LANES)], 1.0))
            dkv_out[:, LANES:2 * LANES] = dv_tot[...].astype(BF16)

    vec = pl.BlockSpec((1, LANES), lambda hp, s: (0, 0))
    seq_f32 = pltpu.VMEM((SEQ, LANES), F32)
    padded = pltpu.VMEM((SEQ + 2 * A_PAD, LANES), F32)
    return pl.pallas_call(
        body, name="attn_a_bwd",
        grid_spec=pltpu.PrefetchScalarGridSpec(
            num_scalar_prefetch=1, grid=(4,),
            in_specs=[_seq_block(lambda hp, s: (0, QA_BLK + hp)), _seq_block(lambda hp, s: (0, KA_BLK)),
                      _seq_block(lambda hp, s: (0, VA_BLK)), vec, vec,
                      pl.BlockSpec((None, 2, Q_BLOCK, A_W), lambda hp, s: (hp, 0, 0, 0)),
                      _seq_block(lambda hp, s: (0, hp)), _seq_block(lambda hp, s: (0, hp)),
                      _seq_block(lambda hp, s: (0, hp))],
            out_specs=[_seq_block(lambda hp, s: (0, hp)), pl.BlockSpec((SEQ, 2 * LANES), lambda hp, s: (0, 0)),
                       pl.BlockSpec((None, 8, LANES), lambda hp, s: (hp, 0, 0)),
                       pl.BlockSpec((8, LANES), lambda hp, s: (0, 0)),
                       pl.BlockSpec((None, 2, Q_BLOCK, A_W), lambda hp, s: (hp, 0, 0, 0)),
                       pl.BlockSpec((None, 2, Q_BLOCK, 1), lambda hp, s: (hp, 0, 0, 0))],
            scratch_shapes=[seq_f32, padded, padded, seq_f32, seq_f32, padded, padded, seq_f32, seq_f32]),
        out_shape=[jax.ShapeDtypeStruct((SEQ, 512), BF16), jax.ShapeDtypeStruct((SEQ, 2 * LANES), BF16),
                   jax.ShapeDtypeStruct((4, 8, LANES), F32), jax.ShapeDtypeStruct((8, LANES), F32),
           jax.ShapeDtypeStruct((4, 2, Q_BLOCK, A_W), F32), jax.ShapeDtypeStruct((4, 2, Q_BLOCK, 1), F32)],
        compiler_params=_params("arbitrary"),
    )(sink.reshape(8), qkv, qkv, qkv, gain_q, gain_k, bias, out, lse, d_out)


B_W = Q_BLOCK + 2 * B_HALF_WINDOW
B_PAD_MAX = B_HALF_WINDOW * B_DILATIONS[-1]


def _attn_b_fwd(qkv, gain_q, gain_k, bias):
    def body(q_ref, k_ref, v_ref, gq_ref, gk_ref, bias_ref, o_ref, lse_ref, qn_ref, kp_ref, vp_ref):
        g = pl.program_id(1)
        _prep_q(q_ref, gq_ref, qn_ref)

        def first(rows, out, lse):
            o_ref[rows, :] = out
            lse_ref[rows, :] = lse

        def combine(rows, out, lse):
            old = lse_ref[rows, :]
            new = jnp.maximum(old, lse) + jnp.log(1.0 + jnp.exp(-jnp.abs(old - lse)))
            o_ref[rows, :] = o_ref[rows, :] * jnp.exp(old - new) + out * jnp.exp(lse - new)
            lse_ref[rows, :] = new

        for gi, d in enumerate(B_DILATIONS):
            @pl.when(g == gi)
            def _():
                _prep_kv(k_ref, v_ref, gk_ref, kp_ref, vp_ref, B_HALF_WINDOW * d)
                _fwd_tiles(qn_ref, kp_ref, vp_ref, bias_ref, first if gi == 0 else combine,
                           d=d, half_window=B_HALF_WINDOW)

    vec = pl.BlockSpec((1, LANES), lambda hp, g: (0, 0))
    padded = pltpu.VMEM((SEQ + 2 * B_PAD_MAX, LANES), F32)
    return pl.pallas_call(
        body, name="attn_b_fwd", grid=(4, 3),
        in_specs=[_seq_block(lambda hp, g: (0, QB_BLK + 4 * g + hp)), _seq_block(lambda hp, g: (0, KB_BLK + 4 * g + hp)),
                  _seq_block(lambda hp, g: (0, VB_BLK + 4 * g + hp)), vec, vec,
                  pl.BlockSpec((None, 2, Q_BLOCK, B_W), lambda hp, g: (4 * g + hp, 0, 0, 0))],
        out_specs=[_seq_block(lambda hp, g: (0, hp)), _seq_block(lambda hp, g: (0, hp))],
        out_shape=[jax.ShapeDtypeStruct((SEQ, 512), F32)] * 2,
        scratch_shapes=[pltpu.VMEM((SEQ, LANES), F32), padded, padded],
        compiler_params=_params("arbitrary", "arbitrary"),
    )(qkv, qkv, qkv, gain_q, gain_k, bias)


def _attn_b_bwd(qkv, gain_q, gain_k, bias, out, lse, d_out):
    def body(q_ref, k_ref, v_ref, gq_ref, gk_ref, bias_ref, o_ref, lse_ref, do_ref,
             dq_out, dk_out, dv_out, dgq_out, dgk_out, ds_out,
             qn_ref, kp_ref, vp_ref, delta_ref, dq_ref, dk_ref, dv_ref):
        g = pl.program_id(1)
        _prep_q(q_ref, gq_ref, qn_ref)
        _prep_delta(do_ref, o_ref, delta_ref)
        dk_ref[...] = jnp.zeros_like(dk_ref)
        dv_ref[...] = jnp.zeros_like(dv_ref)
        ds_out[...] = jnp.zeros_like(ds_out)
        for gi, d in enumerate(B_DILATIONS):
            @pl.when(g == gi)
            def _():
                pad = B_HALF_WINDOW * d
                _prep_kv(k_ref, v_ref, gk_ref, kp_ref, vp_ref, pad)
                _bwd_tiles(qn_ref, kp_ref, vp_ref, bias_ref, do_ref, lse_ref, delta_ref, dq_ref, dk_ref, dv_ref,
                           ds_out, d=d, half_window=B_HALF_WINDOW)
                dgk_out[...] = _rows8(_norm_bwd(k_ref, gk_ref, dk_ref, pad, dk_out, 1.0))
                dv_out[...] = dv_ref[pl.ds(pad, SEQ), :].astype(BF16)
        dgq_out[...] = _rows8(_norm_bwd(q_ref, gq_ref, dq_ref, 0, dq_out, SCALE))

    vec = pl.BlockSpec((1, LANES), lambda hp, g: (0, 0))
    seq_f32 = pltpu.VMEM((SEQ, LANES), F32)
    padded = pltpu.VMEM((SEQ + 2 * B_PAD_MAX, LANES), F32)
    part = pl.BlockSpec((None, 8, LANES), lambda hp, g: (4 * g + hp, 0, 0))
    return pl.pallas_call(
        body, name="attn_b_bwd", grid=(4, 3),
        in_specs=[_seq_block(lambda hp, g: (0, QB_BLK + 4 * g + hp)), _seq_block(lambda hp, g: (0, KB_BLK + 4 * g + hp)),
                  _seq_block(lambda hp, g: (0, VB_BLK + 4 * g + hp)), vec, vec,
                  pl.BlockSpec((None, 2, Q_BLOCK, B_W), lambda hp, g: (4 * g + hp, 0, 0, 0)),
                  _seq_block(lambda hp, g: (0, hp)), _seq_block(lambda hp, g: (0, hp)), _seq_block(lambda hp, g: (0, hp))],
        out_specs=[_seq_block(lambda hp, g: (0, 4 * g + hp))] * 3 + [
            part, part, pl.BlockSpec((None, 2, Q_BLOCK, B_W), lambda hp, g: (4 * g + hp, 0, 0, 0))],
        out_shape=[jax.ShapeDtypeStruct((SEQ, 1536), BF16)] * 3
        + [jax.ShapeDtypeStruct((12, 8, LANES), F32)] * 2 + [jax.ShapeDtypeStruct((12, 2, Q_BLOCK, B_W), F32)],
        scratch_shapes=[seq_f32, padded, padded, seq_f32, seq_f32, padded, padded],
        compiler_params=_params("arbitrary", "arbitrary"),
    )(qkv, qkv, qkv, gain_q, gain_k, bias, out, lse, d_out)


def _sigmoid(t):
    return 1.0 / (1.0 + jnp.exp(-t))


def _middle(out_a, out_b, gates, x, target, w_a, w_b, w_out, b_merge):
    tm = 256
    n_steps = SEQ // tm

    def body(oa_ref, ob_ref, g_ref, x_ref, t_ref, wa_ref, wb_ref, wo_ref, bm_ref,
             dy_ref, dg_ref, doa_ref, dob_ref, dwa_ref, dwb_ref, dwo_ref, dbm_ref, sq_ref):
        @pl.when(pl.program_id(0) == 0)
        def _():
            for ref in (dwa_ref, dwb_ref, dwo_ref, dbm_ref, sq_ref):
                ref[...] = jnp.zeros_like(ref)

        gate_a, gate_b = g_ref[:, 0:512], g_ref[:, 512:1024]
        sig_a, sig_b = _sigmoid(gate_a), _sigmoid(gate_b)
        silu_a, silu_b = gate_a * sig_a, gate_b * sig_b
        oa, ob = oa_ref[...], ob_ref[...]
        ya, yb = (oa * silu_a).astype(BF16), (ob * silu_b).astype(BF16)
        br_a, br_b = _dot(ya, wa_ref[...]), _dot(yb, wb_ref[...])
        m0 = _sigmoid(g_ref[:, 1024:2048] + bm_ref[0:1, :])
        m1 = _sigmoid(g_ref[:, 2048:3072] + bm_ref[1:2, :])
        merged = (m0 * br_a + m1 * br_b).astype(BF16)
        err = (x_ref[...] + _dot(merged, wo_ref[...])) - t_ref[...]
        sq_ref[...] += jnp.sum(err * err, axis=0, keepdims=True)

        dy = err * (1.0 / D_MODEL)
        dy_ref[...] = dy
        dyb = dy.astype(BF16)
        dmerged = _dot(dyb, wo_ref[...], NT)
        dwo_ref[...] += _dot(merged, dyb, TN)
        dbr_a, dbr_b = (dmerged * m0).astype(BF16), (dmerged * m1).astype(BF16)
        dm0 = (dmerged * br_a) * (m0 * (1.0 - m0))
        dm1 = (dmerged * br_b) * (m1 * (1.0 - m1))
        dbm_ref[0:1, :] += jnp.sum(dm0, axis=0, keepdims=True)
        dbm_ref[1:2, :] += jnp.sum(dm1, axis=0, keepdims=True)
        for s in range(N_CHIPS):
            cols = slice(256 * s, 256 * (s + 1))
            dwa_ref[s] += _dot(ya, dbr_a[:, cols], TN)
            dwb_ref[s] += _dot(yb, dbr_b[:, cols], TN)
        dya, dyb_ = _dot(dbr_a, wa_ref[...], NT), _dot(dbr_b, wb_ref[...], NT)
        doa_ref[...] = dya * silu_a
        dob_ref[...] = dyb_ * silu_b
        dg_ref[:, 0:512] = ((dya * oa) * (sig_a * (1.0 + gate_a * (1.0 - sig_a)))).astype(BF16)
        dg_ref[:, 512:1024] = ((dyb_ * ob) * (sig_b * (1.0 + gate_b * (1.0 - sig_b)))).astype(BF16)
        dg_ref[:, 1024:2048] = dm0.astype(BF16)
        dg_ref[:, 2048:3072] = dm1.astype(BF16)

    def rows(width):
        return pl.BlockSpec((tm, width), lambda i: (i, 0))

    def whole(*shape):
        return pl.BlockSpec(shape, lambda i: (0,) * len(shape))

    return pl.pallas_call(
        body, name="middle", grid=(n_steps,),
        in_specs=[rows(512), rows(512), rows(GATE_WIDTH), rows(D_MODEL), rows(D_MODEL),
                  whole(512, D_MODEL), whole(512, D_MODEL), whole(D_MODEL, D_MODEL), whole(2, D_MODEL)],
        out_specs=[rows(D_MODEL), rows(GATE_WIDTH), rows(512), rows(512),
                   whole(N_CHIPS, 512, 256), whole(N_CHIPS, 512, 256), whole(D_MODEL, D_MODEL),
                   whole(2, D_MODEL), whole(1, D_MODEL)],
        out_shape=[jax.ShapeDtypeStruct((SEQ, D_MODEL), F32), jax.ShapeDtypeStruct((SEQ, GATE_WIDTH), BF16),
                   jax.ShapeDtypeStruct((SEQ, 512), F32), jax.ShapeDtypeStruct((SEQ, 512), F32),
                   jax.ShapeDtypeStruct((N_CHIPS, 512, 256), F32), jax.ShapeDtypeStruct((N_CHIPS, 512, 256), F32),
                   jax.ShapeDtypeStruct((D_MODEL, D_MODEL), F32), jax.ShapeDtypeStruct((2, D_MODEL), F32),
                   jax.ShapeDtypeStruct((1, D_MODEL), F32)],
        compiler_params=_params("arbitrary"),
    )(out_a, out_b, gates, x, target, w_a, w_b, w_out, b_merge)


def _which(j, edges, fns):
    lo = 0
    for hi, fn in zip(edges, fns):
        pl.when((j >= lo) & (j < hi))(fn)
        lo = hi


def _d_w_in(d_proj, h):
    tn = 256
    edges = tuple(np.cumsum([p.shape[1] // tn for p in d_proj]))

    def body(*refs):
        pieces, h_ref, o_ref = refs[:-2], refs[-2], refs[-1]

        def emit(ref):
            def fn():
                o_ref[...] = _dot(ref[...], h_ref[...], TN)
            return fn

        _which(pl.program_id(0), edges, [emit(ref) for ref in pieces])

    def cols(lo, hi):
        return pl.BlockSpec((SEQ, tn), lambda j: (0, jnp.clip(j - lo, 0, hi - lo - 1)))

    return pl.pallas_call(
        body, name="d_w_in", grid=(int(edges[-1]),),
        in_specs=[cols(int(lo), int(hi)) for lo, hi in zip((0,) + edges[:-1], edges)]
        + [pl.BlockSpec((SEQ, D_MODEL), lambda j: (0, 0))],
        out_specs=pl.BlockSpec((tn, D_MODEL), lambda j: (j, 0)),
        out_shape=jax.ShapeDtypeStruct((IN_WIDTH, D_MODEL), F32),
        compiler_params=_params("arbitrary"),
    )(*d_proj, h)


def _d_x(d_proj, w_t, x, gain, dy):
    tm = 256
    split = d_proj[0].shape[1]
    assert split + d_proj[1].shape[1] == W_BLOCK and all(p.shape[1] % W_BLOCK == 0 for p in d_proj[2:])
    n_w = IN_WIDTH // W_BLOCK

    def body(*refs):
        pieces, w_refs = refs[:len(d_proj)], refs[len(d_proj):len(d_proj) + n_w]
        x_ref, g_ref, dy_ref, dx_ref, dgain_ref = refs[len(d_proj) + n_w:]

        @pl.when(pl.program_id(0) == 0)
        def _():
            dgain_ref[...] = jnp.zeros_like(dgain_ref)

        dh = _dot(pieces[0][...], w_refs[0][0:split, :]) + _dot(pieces[1][...], w_refs[0][split:W_BLOCK, :])
        blk = 1
        for piece in pieces[2:]:
            for k in range(piece.shape[1] // W_BLOCK):
                dh = dh + _dot(piece[:, k * W_BLOCK:(k + 1) * W_BLOCK], w_refs[blk][...])
                blk += 1
        xf = x_ref[...]
        r = lax.rsqrt(jnp.mean(xf * xf, axis=-1, keepdims=True) + EPS)
        xh = xf * r
        dxh = dh * g_ref[...]
        dx_ref[...] = r * (dxh - xh * jnp.mean(dxh * xh, axis=-1, keepdims=True)) + dy_ref[...]
        dgain_ref[...] += _rows8(jnp.sum(dh * xh, axis=0, keepdims=True))

    row = pl.BlockSpec((tm, D_MODEL), lambda i: (i, 0))
    return pl.pallas_call(
        body, name="d_x", grid=(SEQ // tm,),
        in_specs=[pl.BlockSpec((tm, p.shape[1]), lambda i: (i, 0)) for p in d_proj] + _w_blocks(0, n_w)
        + [row, pl.BlockSpec((1, D_MODEL), lambda i: (0, 0)), row],
        out_specs=[row, pl.BlockSpec((8, D_MODEL), lambda i: (0, 0))],
        out_shape=[jax.ShapeDtypeStruct((SEQ, D_MODEL), F32), jax.ShapeDtypeStruct((8, D_MODEL), F32)],
        compiler_params=_params("arbitrary"),
    )(*d_proj, *([w_t] * n_w), x, gain, dy)


def _my_place():
    x, y, c = lax.axis_index("x"), lax.axis_index("y"), lax.axis_index("c")
    return jnp.stack([2 * x + y, c]).astype(jnp.int32)


def _swap_halves(grads):
    def body(g_ref, o_ref, send_sem, recv_sem):
        x, y, c = lax.axis_index("x"), lax.axis_index("y"), lax.axis_index("c")
        theirs = g_ref.at[:, pl.ds(pl.multiple_of((1 - c) * GRAD_HALF, 8), GRAD_HALF), :]
        cp = pltpu.make_async_remote_copy(src_ref=theirs, dst_ref=o_ref, send_sem=send_sem, recv_sem=recv_sem,
                                          device_id=(x, y, 1 - c), device_id_type=MESH)
        cp.start()
        cp.wait()

    return pl.pallas_call(
        body, name="reduce_swap_halves", in_specs=[ANY], out_specs=ANY,
        out_shape=jax.ShapeDtypeStruct((N_CHIPS, GRAD_HALF, D_MODEL), F32),
        scratch_shapes=[pltpu.SemaphoreType.DMA, pltpu.SemaphoreType.DMA],
    )(grads)


def _add_halves(place, grads, theirs):
    tr = 336
    n = GRAD_HALF // tr

    def body(place_ref, g_ref, t_ref, o_ref):
        o_ref[...] = (g_ref[...] + t_ref[...]).astype(BF16)

    return pl.pallas_call(
        body, name="reduce_add_halves",
        grid_spec=pltpu.PrefetchScalarGridSpec(
            num_scalar_prefetch=1, grid=(N_CHIPS, n),
            in_specs=[pl.BlockSpec((None, tr, D_MODEL), lambda s, i, p: (s, p[1] * n + i, 0)),
                      pl.BlockSpec((None, tr, D_MODEL), lambda s, i, p: (s, i, 0))],
            out_specs=pl.BlockSpec((None, tr, D_MODEL), lambda s, i, p: (s, i, 0))),
        out_shape=jax.ShapeDtypeStruct((N_CHIPS, GRAD_HALF, D_MODEL), BF16),
        compiler_params=_params("arbitrary", "arbitrary"),
    )(place, grads, theirs)


def _scatter_chips(chip_sums):
    def body(q_ref, o_ref, send_sems, recv_sems):
        x, y, c = lax.axis_index("x"), lax.axis_index("y"), lax.axis_index("c")
        chips = [(1 - x, y), (x, 1 - y), (1 - x, 1 - y)]
        copies = [pltpu.make_async_remote_copy(src_ref=q_ref.at[2 * cx + cy], dst_ref=o_ref.at[j],
                                               send_sem=send_sems.at[j], recv_sem=recv_sems.at[j],
                                               device_id=(cx, cy, c), device_id_type=MESH)
                  for j, (cx, cy) in enumerate(chips)]
        for cp in copies:
            cp.start()
        for cp in copies:
            cp.wait()

    return pl.pallas_call(
        body, name="reduce_scatter_chips", in_specs=[ANY], out_specs=ANY,
        out_shape=jax.ShapeDtypeStruct((3, GRAD_HALF, D_MODEL), BF16),
        scratch_shapes=[pltpu.SemaphoreType.DMA((3,)), pltpu.SemaphoreType.DMA((3,))],
    )(chip_sums)


def _add_chips(place, chip_sums, others):
    tr = 336
    n = GRAD_HALF // tr

    def body(place_ref, q_ref, o_ref, r_ref):
        acc = q_ref[...].astype(F32)
        for j in range(3):
            acc = acc + o_ref[j].astype(F32)
        r_ref[...] = acc

    return pl.pallas_call(
        body, name="reduce_add_chips",
        grid_spec=pltpu.PrefetchScalarGridSpec(
            num_scalar_prefetch=1, grid=(n,),
            in_specs=[pl.BlockSpec((None, tr, D_MODEL), lambda i, p: (p[0], i, 0)),
                      pl.BlockSpec((3, tr, D_MODEL), lambda i, p: (0, i, 0))],
            out_specs=pl.BlockSpec((tr, D_MODEL), lambda i, p: (p[1] * n + i, 0))),
        out_shape=jax.ShapeDtypeStruct((GRAD_ROWS, D_MODEL), F32),
        compiler_params=_params("arbitrary"),
    )(place, chip_sums, others)


def _join_halves(shard):
    def body(s_ref, o_ref, send_sem, recv_sem):
        x, y, c = lax.axis_index("x"), lax.axis_index("y"), lax.axis_index("c")
        mine = o_ref.at[pl.ds(pl.multiple_of(c * GRAD_HALF, 8), GRAD_HALF), :]
        other = o_ref.at[pl.ds(pl.multiple_of((1 - c) * GRAD_HALF, 8), GRAD_HALF), :]
        send = pltpu.make_async_remote_copy(src_ref=mine, dst_ref=mine, send_sem=send_sem, recv_sem=recv_sem,
                                            device_id=(x, y, 1 - c), device_id_type=MESH)
        send.start()
        pltpu.make_async_remote_copy(src_ref=other, dst_ref=other, send_sem=send_sem, recv_sem=recv_sem,
                                     device_id=(x, y, 1 - c), device_id_type=MESH).wait_recv()
        send.wait_send()

    return pl.pallas_call(
        body, name="reduce_join_halves", in_specs=[ANY], out_specs=ANY,
        out_shape=jax.ShapeDtypeStruct((GRAD_ROWS, D_MODEL), F32), input_output_aliases={0: 0},
        scratch_shapes=[pltpu.SemaphoreType.DMA, pltpu.SemaphoreType.DMA],
    )(shard)


def _gather_small(block):
    rows = block.shape[0]

    def body(b_ref, o_ref, send_sems, recv_sems, local_sem):
        x, y, c = lax.axis_index("x"), lax.axis_index("y"), lax.axis_index("c")
        me, sibling = (x, y, c), (x, y, 1 - c)
        chips = [(1 - x, y), (x, 1 - y), (1 - x, 1 - y)]

        def at(px, py, pc):
            return o_ref.at[pl.ds(pl.multiple_of((4 * px + 2 * py + pc) * rows, 8), rows), :]

        def copy(k, block_of, to, src=None):
            return pltpu.make_async_remote_copy(src_ref=at(*block_of) if src is None else src, dst_ref=at(*block_of),
                                                send_sem=send_sems.at[k], recv_sem=recv_sems.at[k],
                                                device_id=to, device_id_type=MESH)

        mine = pltpu.make_async_copy(b_ref, at(*me), local_sem)
        mine.start()
        first = [copy(0, me, sibling, src=b_ref)]
        first += [copy(1 + j, me, (*chip, c), src=b_ref) for j, chip in enumerate(chips)]
        for cp in first:
            cp.start()
        passed = [copy(4 + j, (*chip, c), sibling) for j, chip in enumerate(chips)]
        for j, chip in enumerate(chips):
            copy(1 + j, (*chip, c), me).wait_recv()
            passed[j].start()
        copy(0, sibling, me).wait_recv()
        for j, chip in enumerate(chips):
            copy(4 + j, (*chip, 1 - c), me).wait_recv()
        for cp in first + passed:
            cp.wait_send()
        mine.wait()

    return pl.pallas_call(
        body, name="gather_small_grads",
        in_specs=[pl.BlockSpec(memory_space=pltpu.VMEM)], out_specs=pl.BlockSpec(memory_space=pltpu.VMEM),
        out_shape=jax.ShapeDtypeStruct((8 * rows, D_MODEL), F32),
        scratch_shapes=[pltpu.SemaphoreType.DMA((7,)), pltpu.SemaphoreType.DMA((7,)), pltpu.SemaphoreType.DMA],
    )(block)


def _sum_devices(blocks):
    def body(b_ref, o_ref):
        acc = b_ref[0:8, :]
        for dev in range(1, 8):
            acc = acc + b_ref[8 * dev:8 * dev + 8, :]
        o_ref[...] = acc

    return pl.pallas_call(body, name="sum_small_grads", out_shape=jax.ShapeDtypeStruct((8, D_MODEL), F32))(blocks)


def _adamw_math(w, g, m, v):
    m = ADAM_B1 * m + (1.0 - ADAM_B1) * g
    v = ADAM_B2 * v + (1.0 - ADAM_B2) * (g * g)
    m_hat = m / (1.0 - ADAM_B1 ** ADAM_STEP)
    v_hat = v / (1.0 - ADAM_B2 ** ADAM_STEP)
    return -ADAM_LR * (m_hat / (jnp.sqrt(v_hat) + ADAM_EPS) + ADAM_WD * w), m, v


def _adamw(w, g, m, v, name):
    r, c = w.shape
    tr = 128 if r % 128 == 0 else r

    def body(w_ref, g_ref, m_ref, v_ref, d_ref, nm_ref, nv_ref):
        d_ref[...], nm_ref[...], nv_ref[...] = _adamw_math(w_ref[...], g_ref[...], m_ref[...], v_ref[...])

    spec = pl.BlockSpec((tr, c), lambda i: (i, 0))
    return pl.pallas_call(
        body, name=name, grid=(r // tr,), in_specs=[spec] * 4, out_specs=[spec] * 3,
        out_shape=[jax.ShapeDtypeStruct((r, c), F32)] * 3, compiler_params=_params("arbitrary"),
    )(w, g, m, v)


def _adamw_small(ws, gs, ms, vs):
    n = len(ws)

    def body(*refs):
        ins, outs = refs[:4 * n], refs[4 * n:]
        for k in range(n):
            d, m, v = _adamw_math(ins[k][...], ins[n + k][...], ins[2 * n + k][...], ins[3 * n + k][...])
            outs[k][...], outs[n + k][...], outs[2 * n + k][...] = d, m, v

    shapes = [jax.ShapeDtypeStruct(w.shape, F32) for w in ws]
    res = pl.pallas_call(body, name="adamw_small", out_shape=shapes * 3)(*ws, *gs, *ms, *vs)
    return res[:n], res[n:2 * n], res[2 * n:]


def _fold_heads(partials):
    t = jnp.sum(partials[:, 0, :], axis=0)
    return (t[:HEAD_DIM] + t[HEAD_DIM:]).reshape(1, HEAD_DIM)


def _local_step(x, target, norm_gain, w_t, w_a, w_b, w_o, b_m, q_norm_a, k_norm_a, q_norm_b, k_norm_b, sink_a,
                rel_bias):
    two = lambda gain: jnp.concatenate([gain, gain], axis=1)
    bias_a = _bias_table(rel_bias[:, :8], A_HALF_WINDOW, 1)
    bias_b = jnp.concatenate([_bias_table(rel_bias[:, 8 + 8 * g:16 + 8 * g], B_HALF_WINDOW, d)
                              for g, d in enumerate(B_DILATIONS)], axis=0)

    qkv, h = _in_proj(x, norm_gain, w_t, 0, QKV_WIDTH // W_BLOCK, BF16, "in_proj_qkv")
    gates, _ = _in_proj(x, norm_gain, w_t, QKV_WIDTH // W_BLOCK, GATE_WIDTH // W_BLOCK, F32, "in_proj_gates")
    out_a, lse_a = _attn_a_fwd(qkv, two(q_norm_a), two(k_norm_a), bias_a, sink_a)
    out_b, lse_b = _attn_b_fwd(qkv, two(q_norm_b), two(k_norm_b), bias_b)

    dy, dgates, d_out_a, d_out_b, d_wa, d_wb, d_wo, d_bm, sq = _middle(
        out_a, out_b, gates, x, target, w_a, w_b, w_o, b_m)
    loss = (0.5 / D_MODEL) * jnp.sum(sq)

    dq_a, dkv_a, dgq_a, dgk_a, ds_a, dsink = _attn_a_bwd(
        qkv, two(q_norm_a), two(k_norm_a), bias_a, sink_a, out_a, lse_a, d_out_a)
    dq_b, dk_b, dv_b, dgq_b, dgk_b, ds_b = _attn_b_bwd(
        qkv, two(q_norm_b), two(k_norm_b), bias_b, out_b, lse_b, d_out_b)
    d_proj = (dq_a, dkv_a, dq_b, dk_b, dv_b, dgates)

    d_wt = _d_w_in(d_proj, h)
    grad_x, d_gain = _d_x(d_proj, w_t, x, norm_gain, dy)

    d_rel = jnp.concatenate(
        [_bias_grad(ds_a, A_HALF_WINDOW, 1)]
        + [_bias_grad(ds_b[4 * g:4 * g + 4], B_HALF_WINDOW, d) for g, d in enumerate(B_DILATIONS)], axis=1)
    d_sink = jnp.sum(dsink, axis=(2, 3)).reshape(1, 8)
    dgk_a_row = dgk_a[0]
    small = jnp.zeros((8, D_MODEL), F32)
    small = small.at[0].set(d_gain[0])
    small = small.at[1].set(d_rel.reshape(-1))
    misc = jnp.concatenate([_fold_heads(dgq_a), (dgk_a_row[:HEAD_DIM] + dgk_a_row[HEAD_DIM:]).reshape(1, HEAD_DIM),
                            _fold_heads(dgq_b), _fold_heads(dgk_b), d_sink], axis=1)
    small = small.at[2, :264].set(misc[0])

    d_bm_rows = jnp.pad(d_bm.reshape(2, N_CHIPS, 256).transpose(1, 0, 2),
                        ((0, 0), (0, GRAD_ROWS - 2626), (0, D_MODEL - 256)))
    big = jnp.concatenate([d_wt.reshape(N_CHIPS, W_IN_SHARD, D_MODEL), d_wo.reshape(N_CHIPS, 256, D_MODEL),
                           d_wa.reshape(N_CHIPS, 128, D_MODEL), d_wb.reshape(N_CHIPS, 128, D_MODEL),
                           d_bm_rows], axis=1)
    return loss, grad_x, big, small


def _unpack_weights(w_t_all, small_all):
    sm = small_all.reshape(N_CHIPS, SMALL_ROWS, D_MODEL)
    w_o = sm[:, 0:256].reshape(D_MODEL, D_MODEL)
    w_a = sm[:, 256:384].reshape(N_CHIPS, 512, 256).transpose(1, 0, 2).reshape(512, D_MODEL)
    w_b = sm[:, 384:512].reshape(N_CHIPS, 512, 256).transpose(1, 0, 2).reshape(512, D_MODEL)
    b_m = lax.bitcast_convert_type(sm[:, 512].reshape(N_CHIPS, 2, 256, 2), F32)
    return w_t_all, w_a, w_b, w_o, b_m.transpose(1, 0, 2).reshape(2, D_MODEL)


def _pack_small_weights(w_branch_a, w_branch_b, b_merge, w_out):
    b_m = jnp.pad(lax.bitcast_convert_type(b_merge, BF16).reshape(1, D_MODEL), ((0, SMALL_ROWS - 513), (0, 0)))
    return jnp.concatenate([w_out.astype(BF16), w_branch_a.astype(BF16).reshape(128, D_MODEL),
                            w_branch_b.astype(BF16).reshape(128, D_MODEL), b_m], axis=0)


def kernel(x, norm_gain, w_in, q_norm_a, k_norm_a, q_norm_b, k_norm_b, sink_a, rel_bias, w_branch_a, w_branch_b, b_merge, w_out, loss_target, m_norm_gain, m_w_in, m_q_norm_a, m_k_norm_a, m_q_norm_b, m_k_norm_b, m_sink_a, m_rel_bias, m_w_branch_a, m_w_branch_b, m_b_merge, m_w_out, v_norm_gain, v_w_in, v_q_norm_a, v_k_norm_a, v_q_norm_b, v_k_norm_b, v_sink_a, v_rel_bias, v_w_branch_a, v_w_branch_b, v_b_merge, v_w_out):
    w_in, w_branch_a, w_branch_b, b_merge, w_out = w_in[0], w_branch_a[0], w_branch_b[0], b_merge[0], w_out[0]

    wt_shard = _transpose_cast(w_in, BF16, "w_in_transpose")
    w_t, w_a, w_b, w_o, b_m = _unpack_weights(
        *_gather_weights(wt_shard, _pack_small_weights(w_branch_a, w_branch_b, b_merge, w_out)))

    loss_part, grad_x, big, small = _local_step(
        x[0], loss_target[0], norm_gain, w_t, w_a, w_b, w_o, b_m, q_norm_a, k_norm_a, q_norm_b, k_norm_b,
        sink_a, rel_bias)
    loss = lax.psum(loss_part, ("x", "y", "c"))

    place = _my_place()
    chip_sums = _add_halves(place, big, _swap_halves(big))
    shard = _join_halves(_add_chips(place, chip_sums, _scatter_chips(chip_sums)))
    small = _sum_devices(_gather_small(small))

    g_w_in = _transpose_cast(shard[:W_IN_SHARD], F32, "grad_w_in_transpose")
    g_w_out = shard[2112:2368]
    g_w_a = shard[2368:2496].reshape(512, 256)
    g_w_b = shard[2496:2624].reshape(512, 256)
    g_b_merge = shard[2624:2626, :256]
    g_norm_gain = small[0:1]
    g_rel_bias = small[1].reshape(N_BUCKETS, N_BUCKETS)
    g_q_a, g_k_a, g_q_b, g_k_b = (small[2:3, 64 * k:64 * k + 64] for k in range(4))
    g_sink = small[2:3, 256:264]

    big_names = (("w_in", w_in, g_w_in, m_w_in[0], v_w_in[0]),
                 ("w_branch_a", w_branch_a, g_w_a, m_w_branch_a[0], v_w_branch_a[0]),
                 ("w_branch_b", w_branch_b, g_w_b, m_w_branch_b[0], v_w_branch_b[0]),
                 ("w_out", w_out, g_w_out, m_w_out[0], v_w_out[0]))
    upd = {name: (g,) + tuple(_adamw(w, g, m, v, "adamw_" + name)) for name, w, g, m, v in big_names}
    small_names = ("norm_gain", "q_norm_a", "k_norm_a", "q_norm_b", "k_norm_b", "sink_a", "rel_bias", "b_merge")
    ws = [norm_gain, q_norm_a, k_norm_a, q_norm_b, k_norm_b, sink_a, rel_bias, b_merge]
    gs = [g_norm_gain, g_q_a, g_k_a, g_q_b, g_k_b, g_sink, g_rel_bias, g_b_merge]
    ms = [m_norm_gain, m_q_norm_a, m_k_norm_a, m_q_norm_b, m_k_norm_b, m_sink_a, m_rel_bias, m_b_merge[0]]
    vs = [v_norm_gain, v_q_norm_a, v_k_norm_a, v_q_norm_b, v_k_norm_b, v_sink_a, v_rel_bias, v_b_merge[0]]
    ds, nms, nvs = _adamw_small(ws, gs, ms, vs)
    for k, name in enumerate(small_names):
        upd[name] = (gs[k], ds[k], nms[k], nvs[k])

    order = ("norm_gain", "w_in", "q_norm_a", "k_norm_a", "q_norm_b", "k_norm_b", "sink_a", "rel_bias",
             "w_branch_a", "w_branch_b", "b_merge", "w_out")
    lead = {"w_in", "w_branch_a", "w_branch_b", "b_merge", "w_out"}
    outs = [loss, grad_x[None]]
    for part in range(4):
        outs += [upd[name][part][None] if name in lead else upd[name][part] for name in order]
    return tuple(outs)
```

```python
import math

import numpy as np
import jax
import jax.numpy as jnp
from jax import lax
from jax.experimental import pallas as pl
from jax.experimental.pallas import tpu as pltpu

F32 = jnp.float32
BF16 = jnp.bfloat16

SEQ = 4096
D_MODEL = 1024
HEAD_DIM = 64
LANES = 128
EPS = 1e-6
NEG_INF = -1e30
SCALE = HEAD_DIM ** -0.5
N_BUCKETS = 32
MAX_DISTANCE = 1024
N_CHIPS = 4

A_HALF_WINDOW = 128
B_HALF_WINDOW = 64
B_DILATIONS = (1, 4, 16)
Q_BLOCK = 128

QKV_WIDTH = 5376
GATE_WIDTH = 3072
QA_BLK, KA_BLK, VA_BLK = 0, 4, 5
QB_BLK, KB_BLK, VB_BLK = 6, 18, 30
IN_WIDTH = QKV_WIDTH + GATE_WIDTH
W_IN_SHARD = IN_WIDTH // N_CHIPS

SMALL_ROWS = 544
GRAD_ROWS = 2688
GRAD_HALF = GRAD_ROWS // 2

ADAM_LR = 0.001
ADAM_B1 = 0.9
ADAM_B2 = 0.999
ADAM_EPS = 1e-08
ADAM_WD = 0.01
ADAM_STEP = 10

VMEM_LIMIT = 56 * 1024 * 1024

NT = (((1,), (1,)), ((), ()))
TN = (((0,), (0,)), ((), ()))
MESH = pl.DeviceIdType.MESH
ANY = pl.BlockSpec(memory_space=pl.ANY)


def _dot(a, b, dims=None):
    if dims is None:
        return jnp.dot(a, b, preferred_element_type=F32)
    return lax.dot_general(a, b, dims, preferred_element_type=F32)


def _params(*semantics):
    return pltpu.CompilerParams(dimension_semantics=semantics or None, vmem_limit_bytes=VMEM_LIMIT)


def _bucket_onehot(half_window, stride):
    w = Q_BLOCK + 2 * half_window
    rel = (np.arange(w)[None, :] - half_window - np.arange(Q_BLOCK)[:, None])
    band = np.abs(rel) <= half_window
    rel = rel * stride
    half, max_exact = N_BUCKETS // 2, N_BUCKETS // 4
    n = np.abs(rel)
    nf = np.maximum(n, max_exact).astype(np.float32)
    large = max_exact + (np.log(nf / np.float32(max_exact)) / np.float32(math.log(MAX_DISTANCE / max_exact))
                         * np.float32(half - max_exact)).astype(np.int32)
    large = np.minimum(large, half - 1)
    bucket = (rel > 0).astype(np.int32) * half + np.where(n < max_exact, n, large)
    onehot = (bucket[..., None] == np.arange(N_BUCKETS)) & band[..., None]
    return onehot.reshape(Q_BLOCK * w, N_BUCKETS).astype(np.float32), band


def _bias_table(rel_bias_cols, half_window, stride):
    onehot, band = _bucket_onehot(half_window, stride)
    h = rel_bias_cols.shape[1]
    w = Q_BLOCK + 2 * half_window
    t = jnp.einsum("pb,bh->hp", jnp.asarray(onehot), rel_bias_cols, precision=lax.Precision.HIGHEST)
    t = t.reshape(h, Q_BLOCK, w) + jnp.asarray(np.where(band, 0.0, NEG_INF).astype(np.float32))
    return t.reshape(h // 2, 2, Q_BLOCK, w)


def _bias_grad(ds_sum, half_window, stride):
    onehot, _ = _bucket_onehot(half_window, stride)
    h = ds_sum.shape[0] * 2
    return jnp.einsum("pb,hp->bh", jnp.asarray(onehot), ds_sum.reshape(h, -1), precision=lax.Precision.HIGHEST)


def _transpose_cast(w, out_dtype, name):
    r, c = w.shape

    def body(w_ref, o_ref):
        o_ref[...] = w_ref[...].T.astype(out_dtype)

    if r % LANES == 0:
        steps = pl.cdiv(c, LANES)
        in_spec, out_spec = pl.BlockSpec((r, LANES), lambda j: (0, j)), pl.BlockSpec((LANES, r), lambda j: (j, 0))
    else:
        steps = pl.cdiv(r, LANES)
        in_spec, out_spec = pl.BlockSpec((LANES, c), lambda j: (j, 0)), pl.BlockSpec((c, LANES), lambda j: (0, j))
    return pl.pallas_call(
        body, name=name, grid=(steps,), in_specs=[in_spec], out_specs=out_spec,
        out_shape=jax.ShapeDtypeStruct((c, r), out_dtype),
        compiler_params=_params("arbitrary"),
    )(w)


def _gather_weights(wt_shard, small_shard):
    bufs = ((W_IN_SHARD, IN_WIDTH), (SMALL_ROWS, N_CHIPS * SMALL_ROWS))

    stage_rows = 528

    def body(wt_in, sm_in, wt_out, sm_out, send_sems, recv_sems, in_sems, out_sems, stage):
        x, y, c = lax.axis_index("x"), lax.axis_index("y"), lax.axis_index("c")
        sibling = (x, y, 1 - c)
        chips = [(1 - x, y), (x, 1 - y), (1 - x, 1 - y)]
        my_chip = 2 * x + y
        refs = ((wt_in, wt_out), (sm_in, sm_out))

        def keep_own():
            pieces = [(b, r0) for b in range(2) for r0 in range(0, bufs[b][0], stage_rows)]
            outs = []
            for i, (b, r0) in enumerate(pieces):
                rows = min(stage_rows, bufs[b][0] - r0)
                slot = i % 2
                if i >= 2:
                    outs[i - 2].wait()
                buf = stage.at[slot, pl.ds(0, rows), :]
                load = pltpu.make_async_copy(refs[b][0].at[pl.ds(r0, rows), :], buf, in_sems.at[slot])
                load.start()
                load.wait()
                start = pl.multiple_of(my_chip * bufs[b][0] + r0, 16)
                outs.append(pltpu.make_async_copy(buf, refs[b][1].at[pl.ds(start, rows), :], out_sems.at[slot]))
                outs[i].start()
            for cp in outs[-2:]:
                cp.wait()

        def half_of(b, chip, half):
            rows = bufs[b][0]
            start = pl.multiple_of(chip * rows + half * (rows // 2), 16)
            return refs[b][1].at[pl.ds(start, rows // 2), :]

        def copy(k, src, dst, to):
            return pltpu.make_async_remote_copy(src_ref=src, dst_ref=dst, send_sem=send_sems.at[k],
                                                recv_sem=recv_sems.at[k], device_id=to, device_id_type=MESH)

        first, passed = [], []
        for b in range(2):
            rows = bufs[b][0]
            src = refs[b][0].at[pl.ds(pl.multiple_of(c * (rows // 2), 16), rows // 2), :]
            for j, chip in enumerate(chips):
                first.append(copy(3 * b + j, src, half_of(b, my_chip, c), (*chip, c)))
        for cp in first:
            cp.start()
        keep_own()
        for b in range(2):
            for j, (cx, cy) in enumerate(chips):
                landed = half_of(b, 2 * cx + cy, c)
                copy(3 * b + j, landed, landed, sibling).wait_recv()
                fwd = copy(6 + 3 * b + j, landed, landed, sibling)
                fwd.start()
                passed.append(fwd)
        for b in range(2):
            for j, (cx, cy) in enumerate(chips):
                other = half_of(b, 2 * cx + cy, 1 - c)
                copy(6 + 3 * b + j, other, other, sibling).wait_recv()
        for cp in first + passed:
            cp.wait_send()

    return pl.pallas_call(
        body, name="gather_weights",
        in_specs=[ANY, ANY], out_specs=[ANY, ANY],
        out_shape=[jax.ShapeDtypeStruct((bufs[0][1], D_MODEL), BF16),
                   jax.ShapeDtypeStruct((bufs[1][1], D_MODEL), BF16)],
        scratch_shapes=[pltpu.SemaphoreType.DMA((12,)), pltpu.SemaphoreType.DMA((12,)),
                        pltpu.SemaphoreType.DMA((2,)), pltpu.SemaphoreType.DMA((2,)),
                        pltpu.VMEM((2, stage_rows, D_MODEL), BF16)],
    )(wt_shard, small_shard)


W_BLOCK = 768


def _w_blocks(first, count):
    return [pl.BlockSpec((W_BLOCK, D_MODEL), lambda *_, k=k: (first + k, 0)) for k in range(count)]


def _in_proj(x, gain, w_t, first_block, n_blocks, out_dtype, name):
    tm = 256

    def body(x_ref, g_ref, *refs):
        w_refs, (o_ref, h_ref) = refs[:n_blocks], refs[n_blocks:]
        xf = x_ref[...]
        r = lax.rsqrt(jnp.mean(xf * xf, axis=-1, keepdims=True) + EPS)
        h = ((xf * r) * g_ref[...]).astype(BF16)
        h_ref[...] = h
        for k, w_ref in enumerate(w_refs):
            o_ref[:, k * W_BLOCK:(k + 1) * W_BLOCK] = _dot(h, w_ref[...], NT).astype(out_dtype)

    return pl.pallas_call(
        body, name=name, grid=(SEQ // tm,),
        in_specs=[pl.BlockSpec((tm, D_MODEL), lambda i: (i, 0)), pl.BlockSpec((1, D_MODEL), lambda i: (0, 0))]
        + _w_blocks(first_block, n_blocks),
        out_specs=[pl.BlockSpec((tm, W_BLOCK * n_blocks), lambda i: (i, 0)),
                   pl.BlockSpec((tm, D_MODEL), lambda i: (i, 0))],
        out_shape=[jax.ShapeDtypeStruct((SEQ, W_BLOCK * n_blocks), out_dtype),
                   jax.ShapeDtypeStruct((SEQ, D_MODEL), BF16)],
        compiler_params=_params("arbitrary"),
    )(x, gain, *([w_t] * n_blocks))


CHUNK = 256
CHUNK_UNROLL = 4
TILE_UNROLL = 4


def _low_half():
    return lax.broadcasted_iota(jnp.int32, (1, LANES), 1) < HEAD_DIM


def _half_sum(v, low):
    del low
    row = lax.broadcasted_iota(jnp.int32, (2 * LANES, LANES), 0)
    col = lax.broadcasted_iota(jnp.int32, (2 * LANES, LANES), 1)
    ones = jnp.where((row % LANES) // HEAD_DIM == col // HEAD_DIM, 1.0, 0.0).astype(BF16)
    hi = v.astype(BF16)
    lo = (v - hi.astype(F32)).astype(BF16)
    return _dot(jnp.concatenate([hi, lo], axis=1), ones)


def _chunks(fn, init=0):
    def body(i, carry):
        for u in range(CHUNK_UNROLL):
            carry = fn(pl.multiple_of((i * CHUNK_UNROLL + u) * CHUNK, CHUNK), carry)
        return carry

    return lax.fori_loop(0, SEQ // (CHUNK * CHUNK_UNROLL), body, init)


def _inv_rms(t, low):
    return lax.rsqrt(_half_sum(t * t, low) * (1.0 / HEAD_DIM) + EPS)


def _prep_q(q_ref, gain_ref, qn_ref):
    low = _low_half()

    def step(r0, carry):
        q = q_ref[pl.ds(r0, CHUNK), :].astype(F32)
        qn_ref[pl.ds(r0, CHUNK), :] = ((q * _inv_rms(q, low)) * gain_ref[...]) * SCALE
        return carry

    _chunks(step)


def _own_half(t, keep):
    return jnp.where(keep, t, pltpu.roll(t, HEAD_DIM, 1))


def _prep_kv(k_ref, v_ref, gain_ref, kp_ref, vp_ref, pad, keep=None):
    low = _low_half()
    zeros = jnp.zeros((pad, LANES), F32)
    for ref in (kp_ref, vp_ref):
        ref[pl.ds(0, pad), :] = zeros
        ref[pl.ds(pad + SEQ, pad), :] = zeros

    def step(r0, carry):
        k = k_ref[pl.ds(r0, CHUNK), :].astype(F32)
        v = v_ref[pl.ds(r0, CHUNK), :].astype(F32)
        kn = (k * _inv_rms(k, low)) * gain_ref[...]
        if keep is not None:
            kn, v = _own_half(kn, keep), _own_half(v, keep)
        kp_ref[pl.ds(pad + r0, CHUNK), :] = kn
        vp_ref[pl.ds(pad + r0, CHUNK), :] = v
        return carry

    _chunks(step)


def _tiles(d, half_window, fn):
    w = Q_BLOCK + 2 * half_window
    length = SEQ // d
    n_blocks = length // Q_BLOCK
    col = lax.broadcasted_iota(jnp.int32, (1, w), 1)

    def step(it, carry):
        c, n = it // n_blocks, it % n_blocks
        start = c + (d * Q_BLOCK) * n
        if d == 1:
            start = pl.multiple_of(start, Q_BLOCK)
            q_rows, k_rows = pl.ds(start, Q_BLOCK), pl.ds(start, w)
        else:
            q_rows, k_rows = pl.ds(start, Q_BLOCK, stride=d), pl.ds(start, w, stride=d)
        t = n * Q_BLOCK - half_window + col
        edge = jnp.where((t < 0) | (t >= length), NEG_INF, 0.0)
        fn(q_rows, k_rows, edge)
        return carry

    lax.fori_loop(0, d * n_blocks, step, 0, unroll=TILE_UNROLL)


def _stack_heads(t, low):
    return jnp.concatenate([jnp.where(low, t, 0.0), jnp.where(low, 0.0, t)], axis=0).astype(BF16)


def _unstack_heads(t, low):
    return jnp.where(low, t[:Q_BLOCK], t[Q_BLOCK:])


def _per_head(pair):
    return jnp.concatenate([jnp.full((Q_BLOCK, 1), pair[0], F32), jnp.full((Q_BLOCK, 1), pair[1], F32)], axis=0)


def _fwd_tiles(qn_ref, kp_ref, vp_ref, bias_ref, emit, *, d, half_window, sinks=None):
    low = _low_half()
    w = Q_BLOCK + 2 * half_window
    sink = None if sinks is None else _per_head(sinks)

    def tile(q_rows, k_rows, edge):
        q2 = _stack_heads(qn_ref[q_rows, :], low)
        k = kp_ref[k_rows, :].astype(BF16)
        v1 = jnp.concatenate([vp_ref[k_rows, :], jnp.ones((w, LANES), F32)], axis=1).astype(BF16)
        s = _dot(q2, k, NT) + bias_ref[...].reshape(2 * Q_BLOCK, w) + edge
        m = jnp.max(s, axis=-1, keepdims=True)
        if sink is not None:
            m = jnp.maximum(m, sink)
        o = _dot(jnp.exp(s - m).astype(BF16), v1)
        l = o[:, LANES:]
        if sink is not None:
            l = l + jnp.exp(sink - m)
        emit(q_rows, _unstack_heads(o[:, :LANES] * (1.0 / l), low), _unstack_heads(m + jnp.log(l), low))

    _tiles(d, half_window, tile)


def _bwd_tiles(qn_ref, kp_ref, vp_ref, bias_ref, do_ref, lse_ref, delta_ref, dq_ref, dk_ref, dv_ref, ds_ref,
               *, d, half_window, sinks=None, dsink_ref=None):
    low = _low_half()
    w = Q_BLOCK + 2 * half_window
    sink = None if sinks is None else _per_head(sinks)

    def rows_of(t):
        return jnp.concatenate([t[:, 0:1], t[:, HEAD_DIM:HEAD_DIM + 1]], axis=0)

    def tile(q_rows, k_rows, edge):
        q2 = _stack_heads(qn_ref[q_rows, :], low)
        do2 = _stack_heads(do_ref[q_rows, :], low)
        k = kp_ref[k_rows, :].astype(BF16)
        v = vp_ref[k_rows, :].astype(BF16)
        lse = rows_of(lse_ref[q_rows, :])
        delta = rows_of(delta_ref[q_rows, :])
        p = jnp.exp(_dot(q2, k, NT) + bias_ref[...].reshape(2 * Q_BLOCK, w) + edge - lse)
        ds = p * (_dot(do2, v, NT) - delta)
        ds_ref[...] += ds.reshape(2, Q_BLOCK, w)
        if sink is not None:
            dsink_ref[...] += (-jnp.exp(sink - lse) * delta).reshape(2, Q_BLOCK, 1)
        dsb, pb = ds.astype(BF16), p.astype(BF16)
        dq_ref[q_rows, :] = _unstack_heads(_dot(dsb, k), low)
        dk_ref[k_rows, :] += _dot(dsb, q2, TN)
        dv_ref[k_rows, :] += _dot(pb, do2, TN)

    _tiles(d, half_window, tile)


def _prep_delta(do_ref, o_ref, delta_ref):
    low = _low_half()

    def step(r0, carry):
        delta_ref[pl.ds(r0, CHUNK), :] = _half_sum(do_ref[pl.ds(r0, CHUNK), :] * o_ref[pl.ds(r0, CHUNK), :], low)
        return carry

    _chunks(step)


def _norm_bwd(raw_ref, gain_ref, dn_ref, dn_offset, out_ref, scale):
    low = _low_half()

    def step(r0, dgain):
        t = raw_ref[pl.ds(r0, CHUNK), :].astype(F32)
        dn = dn_ref[pl.ds(dn_offset + r0, CHUNK), :]
        dth = dn * (gain_ref[...] * scale)
        sums = _half_sum(jnp.concatenate([t * t, dth * t], axis=0), low)
        r = lax.rsqrt(sums[:CHUNK] * (1.0 / HEAD_DIM) + EPS)
        th = t * r
        out_ref[pl.ds(r0, CHUNK), :] = (r * (dth - th * (r * sums[CHUNK:] * (1.0 / HEAD_DIM)))).astype(BF16)
        return dgain + jnp.sum(dn * th, axis=0, keepdims=True) * scale

    return _chunks(step, jnp.zeros((1, LANES), F32))


def _rows8(v):
    return jnp.broadcast_to(v, (8, v.shape[-1]))


A_W = Q_BLOCK + 2 * A_HALF_WINDOW
A_PAD = A_HALF_WINDOW


def _seq_block(col_fn):
    return pl.BlockSpec((SEQ, LANES), col_fn)


def _attn_a_fwd(qkv, gain_q, gain_k, bias, sink):
    def body(sink_ref, q_ref, k_ref, v_ref, gq_ref, gk_ref, bias_ref, o_ref, lse_ref, qn_ref, kp_ref, vp_ref):
        hp = pl.program_id(0)
        keep = (lax.broadcasted_iota(jnp.int32, (1, LANES), 1) // HEAD_DIM) == hp // 2
        _prep_q(q_ref, gq_ref, qn_ref)
        _prep_kv(k_ref, v_ref, gk_ref, kp_ref, vp_ref, A_PAD, keep)

        def emit(rows, out, lse):
            o_ref[rows, :] = out
            lse_ref[rows, :] = lse

        _fwd_tiles(qn_ref, kp_ref, vp_ref, bias_ref, emit, d=1, half_window=A_HALF_WINDOW,
                   sinks=(sink_ref[2 * hp], sink_ref[2 * hp + 1]))

    vec = pl.BlockSpec((1, LANES), lambda hp, s: (0, 0))
    return pl.pallas_call(
        body, name="attn_a_fwd",
        grid_spec=pltpu.PrefetchScalarGridSpec(
            num_scalar_prefetch=1, grid=(4,),
            in_specs=[_seq_block(lambda hp, s: (0, QA_BLK + hp)), _seq_block(lambda hp, s: (0, KA_BLK)),
                      _seq_block(lambda hp, s: (0, VA_BLK)), vec, vec,
                      pl.BlockSpec((None, 2, Q_BLOCK, A_W), lambda hp, s: (hp, 0, 0, 0))],
            out_specs=[_seq_block(lambda hp, s: (0, hp)), _seq_block(lambda hp, s: (0, hp))],
            scratch_shapes=[pltpu.VMEM((SEQ, LANES), F32), pltpu.VMEM((SEQ + 2 * A_PAD, LANES), F32),
                            pltpu.VMEM((SEQ + 2 * A_PAD, LANES), F32)]),
        out_shape=[jax.ShapeDtypeStruct((SEQ, 512), F32)] * 2,
        compiler_params=_params("arbitrary"),
    )(sink.reshape(8), qkv, qkv, qkv, gain_q, gain_k, bias)


def _attn_a_bwd(qkv, gain_q, gain_k, bias, sink, out, lse, d_out):
    def body(sink_ref, q_ref, k_ref, v_ref, gq_ref, gk_ref, bias_ref, o_ref, lse_ref, do_ref,
             dq_out, dkv_out, dgq_out, dgk_out, ds_out, dsink_out,
             qn_ref, kp_ref, vp_ref, delta_ref, dq_ref, dk_ref, dv_ref, dk_tot, dv_tot):
        hp = pl.program_id(0)
        kv_head = hp // 2
        keep = (lax.broadcasted_iota(jnp.int32, (1, LANES), 1) // HEAD_DIM) == kv_head
        _prep_q(q_ref, gq_ref, qn_ref)
        _prep_kv(k_ref, v_ref, gk_ref, kp_ref, vp_ref, A_PAD, keep)
        _prep_delta(do_ref, o_ref, delta_ref)
        dk_ref[...] = jnp.zeros_like(dk_ref)
        dv_ref[...] = jnp.zeros_like(dv_ref)
        ds_out[...] = jnp.zeros_like(ds_out)
        dsink_out[...] = jnp.zeros_like(dsink_out)

        @pl.when(hp == 0)
        def _():
            dk_tot[...] = jnp.zeros_like(dk_tot)
            dv_tot[...] = jnp.zeros_like(dv_tot)

        _bwd_tiles(qn_ref, kp_ref, vp_ref, bias_ref, do_ref, lse_ref, delta_ref, dq_ref, dk_ref, dv_ref, ds_out,
                   d=1, half_window=A_HALF_WINDOW, sinks=(sink_ref[2 * hp], sink_ref[2 * hp + 1]),
                   dsink_ref=dsink_out)
        dgq_out[...] = _rows8(_norm_bwd(q_ref, gq_ref, dq_ref, 0, dq_out, SCALE))

        def fold(r0, carry):
            rows = pl.ds(A_PAD + r0, CHUNK)
            for acc, tot in ((dk_ref, dk_tot), (dv_ref, dv_tot)):
                t = acc[rows, :]
                tot[pl.ds(r0, CHUNK), :] += jnp.where(keep, t + pltpu.roll(t, HEAD_DIM, 1), 0.0)
            return carry

        _chunks(fold)

        @pl.when(hp == 3)
        def _():
            dgk_out[...] = _rows8(_norm_bwd(k_ref, gk_ref, dk_tot, 0, dkv_out.at[:, pl.ds(0, LANES)], 1.0))
            dkv_out[:, LANES:2 * LANES] = dv_tot[...].astype(BF16)

    vec = pl.BlockSpec((1, LANES), lambda hp, s: (0, 0))
    seq_f32 = pltpu.VMEM((SEQ, LANES), F32)
    padded = pltpu.VMEM((SEQ + 2 * A_PAD, LANES), F32)
    return pl.pallas_call(
        body, name="attn_a_bwd",
        grid_spec=pltpu.PrefetchScalarGridSpec(
            num_scalar_prefetch=1, grid=(4,),
            in_specs=[_seq_block(lambda hp, s: (0, QA_BLK + hp)), _seq_block(lambda hp, s: (0, KA_BLK)),
                      _seq_block(lambda hp, s: (0, VA_BLK)), vec, vec,
                      pl.BlockSpec((None, 2, Q_BLOCK, A_W), lambda hp, s: (hp, 0, 0, 0)),
                      _seq_block(lambda hp, s: (0, hp)), _seq_block(lambda hp, s: (0, hp)),
                      _seq_block(lambda hp, s: (0, hp))],
            out_specs=[_seq_block(lambda hp, s: (0, hp)), pl.BlockSpec((SEQ, 2 * LANES), lambda hp, s: (0, 0)),
                       pl.BlockSpec((None, 8, LANES), lambda hp, s: (hp, 0, 0)),
                       pl.BlockSpec((8, LANES), lambda hp, s: (0, 0)),
                       pl.BlockSpec((None, 2, Q_BLOCK, A_W), lambda hp, s: (hp, 0, 0, 0)),
                       pl.BlockSpec((None, 2, Q_BLOCK, 1), lambda hp, s: (hp, 0, 0, 0))],
            scratch_shapes=[seq_f32, padded, padded, seq_f32, seq_f32, padded, padded, seq_f32, seq_f32]),
        out_shape=[jax.ShapeDtypeStruct((SEQ, 512), BF16), jax.ShapeDtypeStruct((SEQ, 2 * LANES), BF16),
                   jax.ShapeDtypeStruct((4, 8, LANES), F32), jax.ShapeDtypeStruct((8, LANES), F32),
           jax.ShapeDtypeStruct((4, 2, Q_BLOCK, A_W), F32), jax.ShapeDtypeStruct((4, 2, Q_BLOCK, 1), F32)],
        compiler_params=_params("arbitrary"),
    )(sink.reshape(8), qkv, qkv, qkv, gain_q, gain_k, bias, out, lse, d_out)


B_W = Q_BLOCK + 2 * B_HALF_WINDOW
B_PAD_MAX = B_HALF_WINDOW * B_DILATIONS[-1]


def _attn_b_fwd(qkv, gain_q, gain_k, bias):
    def body(q_ref, k_ref, v_ref, gq_ref, gk_ref, bias_ref, o_ref, lse_ref, qn_ref, kp_ref, vp_ref):
        g = pl.program_id(1)
        _prep_q(q_ref, gq_ref, qn_ref)

        def first(rows, out, lse):
            o_ref[rows, :] = out
            lse_ref[rows, :] = lse

        def combine(rows, out, lse):
            old = lse_ref[rows, :]
            new = jnp.maximum(old, lse) + jnp.log(1.0 + jnp.exp(-jnp.abs(old - lse)))
            o_ref[rows, :] = o_ref[rows, :] * jnp.exp(old - new) + out * jnp.exp(lse - new)
            lse_ref[rows, :] = new

        for gi, d in enumerate(B_DILATIONS):
            @pl.when(g == gi)
            def _():
                _prep_kv(k_ref, v_ref, gk_ref, kp_ref, vp_ref, B_HALF_WINDOW * d)
                _fwd_tiles(qn_ref, kp_ref, vp_ref, bias_ref, first if gi == 0 else combine,
                           d=d, half_window=B_HALF_WINDOW)

    vec = pl.BlockSpec((1, LANES), lambda hp, g: (0, 0))
    padded = pltpu.VMEM((SEQ + 2 * B_PAD_MAX, LANES), F32)
    return pl.pallas_call(
        body, name="attn_b_fwd", grid=(4, 3),
        in_specs=[_seq_block(lambda hp, g: (0, QB_BLK + 4 * g + hp)), _seq_block(lambda hp, g: (0, KB_BLK + 4 * g + hp)),
                  _seq_block(lambda hp, g: (0, VB_BLK + 4 * g + hp)), vec, vec,
                  pl.BlockSpec((None, 2, Q_BLOCK, B_W), lambda hp, g: (4 * g + hp, 0, 0, 0))],
        out_specs=[_seq_block(lambda hp, g: (0, hp)), _seq_block(lambda hp, g: (0, hp))],
        out_shape=[jax.ShapeDtypeStruct((SEQ, 512), F32)] * 2,
        scratch_shapes=[pltpu.VMEM((SEQ, LANES), F32), padded, padded],
        compiler_params=_params("arbitrary", "arbitrary"),
    )(qkv, qkv, qkv, gain_q, gain_k, bias)


def _attn_b_bwd(qkv, gain_q, gain_k, bias, out, lse, d_out):
    def body(q_ref, k_ref, v_ref, gq_ref, gk_ref, bias_ref, o_ref, lse_ref, do_ref,
             dq_out, dk_out, dv_out, dgq_out, dgk_out, ds_out,
             qn_ref, kp_ref, vp_ref, delta_ref, dq_ref, dk_ref, dv_ref):
        g = pl.program_id(1)
        _prep_q(q_ref, gq_ref, qn_ref)
        _prep_delta(do_ref, o_ref, delta_ref)
        dk_ref[...] = jnp.zeros_like(dk_ref)
        dv_ref[...] = jnp.zeros_like(dv_ref)
        ds_out[...] = jnp.zeros_like(ds_out)
        for gi, d in enumerate(B_DILATIONS):
            @pl.when(g == gi)
            def _():
                pad = B_HALF_WINDOW * d
                _prep_kv(k_ref, v_ref, gk_ref, kp_ref, vp_ref, pad)
                _bwd_tiles(qn_ref, kp_ref, vp_ref, bias_ref, do_ref, lse_ref, delta_ref, dq_ref, dk_ref, dv_ref,
                           ds_out, d=d, half_window=B_HALF_WINDOW)
                dgk_out[...] = _rows8(_norm_bwd(k_ref, gk_ref, dk_ref, pad, dk_out, 1.0))
                dv_out[...] = dv_ref[pl.ds(pad, SEQ), :].astype(BF16)
        dgq_out[...] = _rows8(_norm_bwd(q_ref, gq_ref, dq_ref, 0, dq_out, SCALE))

    vec = pl.BlockSpec((1, LANES), lambda hp, g: (0, 0))
    seq_f32 = pltpu.VMEM((SEQ, LANES), F32)
    padded = pltpu.VMEM((SEQ + 2 * B_PAD_MAX, LANES), F32)
    part = pl.BlockSpec((None, 8, LANES), lambda hp, g: (4 * g + hp, 0, 0))
    return pl.pallas_call(
        body, name="attn_b_bwd", grid=(4, 3),
        in_specs=[_seq_block(lambda hp, g: (0, QB_BLK + 4 * g + hp)), _seq_block(lambda hp, g: (0, KB_BLK + 4 * g + hp)),
                  _seq_block(lambda hp, g: (0, VB_BLK + 4 * g + hp)), vec, vec,
                  pl.BlockSpec((None, 2, Q_BLOCK, B_W), lambda hp, g: (4 * g + hp, 0, 0, 0)),
                  _seq_block(lambda hp, g: (0, hp)), _seq_block(lambda hp, g: (0, hp)), _seq_block(lambda hp, g: (0, hp))],
        out_specs=[_seq_block(lambda hp, g: (0, 4 * g + hp))] * 3 + [
            part, part, pl.BlockSpec((None, 2, Q_BLOCK, B_W), lambda hp, g: (4 * g + hp, 0, 0, 0))],
        out_shape=[jax.ShapeDtypeStruct((SEQ, 1536), BF16)] * 3
        + [jax.ShapeDtypeStruct((12, 8, LANES), F32)] * 2 + [jax.ShapeDtypeStruct((12, 2, Q_BLOCK, B_W), F32)],
        scratch_shapes=[seq_f32, padded, padded, seq_f32, seq_f32, padded, padded],
        compiler_params=_params("arbitrary", "arbitrary"),
    )(qkv, qkv, qkv, gain_q, gain_k, bias, out, lse, d_out)


def _sigmoid(t):
    return 1.0 / (1.0 + jnp.exp(-t))


def _middle(out_a, out_b, gates, x, target, w_a, w_b, w_out, b_merge):
    tm = 256
    n_steps = SEQ // tm

    def body(oa_ref, ob_ref, g_ref, x_ref, t_ref, wa_ref, wb_ref, wo_ref, bm_ref,
             dy_ref, dg_ref, doa_ref, dob_ref, dwa_ref, dwb_ref, dwo_ref, dbm_ref, sq_ref):
        @pl.when(pl.program_id(0) == 0)
        def _():
            for ref in (dwa_ref, dwb_ref, dwo_ref, dbm_ref, sq_ref):
                ref[...] = jnp.zeros_like(ref)

        gate_a, gate_b = g_ref[:, 0:512], g_ref[:, 512:1024]
        sig_a, sig_b = _sigmoid(gate_a), _sigmoid(gate_b)
        silu_a, silu_b = gate_a * sig_a, gate_b * sig_b
        oa, ob = oa_ref[...], ob_ref[...]
        ya, yb = (oa * silu_a).astype(BF16), (ob * silu_b).astype(BF16)
        br_a, br_b = _dot(ya, wa_ref[...]), _dot(yb, wb_ref[...])
        m0 = _sigmoid(g_ref[:, 1024:2048] + bm_ref[0:1, :])
        m1 = _sigmoid(g_ref[:, 2048:3072] + bm_ref[1:2, :])
        merged = (m0 * br_a + m1 * br_b).astype(BF16)
        err = (x_ref[...] + _dot(merged, wo_ref[...])) - t_ref[...]
        sq_ref[...] += jnp.sum(err * err, axis=0, keepdims=True)

        dy = err * (1.0 / D_MODEL)
        dy_ref[...] = dy
        dyb = dy.astype(BF16)
        dmerged = _dot(dyb, wo_ref[...], NT)
        dwo_ref[...] += _dot(merged, dyb, TN)
        dbr_a, dbr_b = (dmerged * m0).astype(BF16), (dmerged * m1).astype(BF16)
        dm0 = (dmerged * br_a) * (m0 * (1.0 - m0))
        dm1 = (dmerged * br_b) * (m1 * (1.0 - m1))
        dbm_ref[0:1, :] += jnp.sum(dm0, axis=0, keepdims=True)
        dbm_ref[1:2, :] += jnp.sum(dm1, axis=0, keepdims=True)
        for s in range(N_CHIPS):
            cols = slice(256 * s, 256 * (s + 1))
            dwa_ref[s] += _dot(ya, dbr_a[:, cols], TN)
            dwb_ref[s] += _dot(yb, dbr_b[:, cols], TN)
        dya, dyb_ = _dot(dbr_a, wa_ref[...], NT), _dot(dbr_b, wb_ref[...], NT)
        doa_ref[...] = dya * silu_a
        dob_ref[...] = dyb_ * silu_b
        dg_ref[:, 0:512] = ((dya * oa) * (sig_a * (1.0 + gate_a * (1.0 - sig_a)))).astype(BF16)
        dg_ref[:, 512:1024] = ((dyb_ * ob) * (sig_b * (1.0 + gate_b * (1.0 - sig_b)))).astype(BF16)
        dg_ref[:, 1024:2048] = dm0.astype(BF16)
        dg_ref[:, 2048:3072] = dm1.astype(BF16)

    def rows(width):
        return pl.BlockSpec((tm, width), lambda i: (i, 0))

    def whole(*shape):
        return pl.BlockSpec(shape, lambda i: (0,) * len(shape))

    return pl.pallas_call(
        body, name="middle", grid=(n_steps,),
        in_specs=[rows(512), rows(512), rows(GATE_WIDTH), rows(D_MODEL), rows(D_MODEL),
                  whole(512, D_MODEL), whole(512, D_MODEL), whole(D_MODEL, D_MODEL), whole(2, D_MODEL)],
        out_specs=[rows(D_MODEL), rows(GATE_WIDTH), rows(512), rows(512),
                   whole(N_CHIPS, 512, 256), whole(N_CHIPS, 512, 256), whole(D_MODEL, D_MODEL),
                   whole(2, D_MODEL), whole(1, D_MODEL)],
        out_shape=[jax.ShapeDtypeStruct((SEQ, D_MODEL), F32), jax.ShapeDtypeStruct((SEQ, GATE_WIDTH), BF16),
                   jax.ShapeDtypeStruct((SEQ, 512), F32), jax.ShapeDtypeStruct((SEQ, 512), F32),
                   jax.ShapeDtypeStruct((N_CHIPS, 512, 256), F32), jax.ShapeDtypeStruct((N_CHIPS, 512, 256), F32),
                   jax.ShapeDtypeStruct((D_MODEL, D_MODEL), F32), jax.ShapeDtypeStruct((2, D_MODEL), F32),
                   jax.ShapeDtypeStruct((1, D_MODEL), F32)],
        compiler_params=_params("arbitrary"),
    )(out_a, out_b, gates, x, target, w_a, w_b, w_out, b_merge)


def _which(j, edges, fns):
    lo = 0
    for hi, fn in zip(edges, fns):
        pl.when((j >= lo) & (j < hi))(fn)
        lo = hi


def _d_w_in(d_proj, h):
    tn = 256
    edges = tuple(np.cumsum([p.shape[1] // tn for p in d_proj]))

    def body(*refs):
        pieces, h_ref, o_ref = refs[:-2], refs[-2], refs[-1]

        def emit(ref):
            def fn():
                o_ref[...] = _dot(ref[...], h_ref[...], TN)
            return fn

        _which(pl.program_id(0), edges, [emit(ref) for ref in pieces])

    def cols(lo, hi):
        return pl.BlockSpec((SEQ, tn), lambda j: (0, jnp.clip(j - lo, 0, hi - lo - 1)))

    return pl.pallas_call(
        body, name="d_w_in", grid=(int(edges[-1]),),
        in_specs=[cols(int(lo), int(hi)) for lo, hi in zip((0,) + edges[:-1], edges)]
        + [pl.BlockSpec((SEQ, D_MODEL), lambda j: (0, 0))],
        out_specs=pl.BlockSpec((tn, D_MODEL), lambda j: (j, 0)),
        out_shape=jax.ShapeDtypeStruct((IN_WIDTH, D_MODEL), F32),
        compiler_params=_params("arbitrary"),
    )(*d_proj, h)


def _d_x(d_proj, w_t, x, gain, dy):
    tm = 256
    split = d_proj[0].shape[1]
    assert split + d_proj[1].shape[1] == W_BLOCK and all(p.shape[1] % W_BLOCK == 0 for p in d_proj[2:])
    n_w = IN_WIDTH // W_BLOCK

    def body(*refs):
        pieces, w_refs = refs[:len(d_proj)], refs[len(d_proj):len(d_proj) + n_w]
        x_ref, g_ref, dy_ref, dx_ref, dgain_ref = refs[len(d_proj) + n_w:]

        @pl.when(pl.program_id(0) == 0)
        def _():
            dgain_ref[...] = jnp.zeros_like(dgain_ref)

        dh = _dot(pieces[0][...], w_refs[0][0:split, :]) + _dot(pieces[1][...], w_refs[0][split:W_BLOCK, :])
        blk = 1
        for piece in pieces[2:]:
            for k in range(piece.shape[1] // W_BLOCK):
                dh = dh + _dot(piece[:, k * W_BLOCK:(k + 1) * W_BLOCK], w_refs[blk][...])
                blk += 1
        xf = x_ref[...]
        r = lax.rsqrt(jnp.mean(xf * xf, axis=-1, keepdims=True) + EPS)
        xh = xf * r
        dxh = dh * g_ref[...]
        dx_ref[...] = r * (dxh - xh * jnp.mean(dxh * xh, axis=-1, keepdims=True)) + dy_ref[...]
        dgain_ref[...] += _rows8(jnp.sum(dh * xh, axis=0, keepdims=True))

    row = pl.BlockSpec((tm, D_MODEL), lambda i: (i, 0))
    return pl.pallas_call(
        body, name="d_x", grid=(SEQ // tm,),
        in_specs=[pl.BlockSpec((tm, p.shape[1]), lambda i: (i, 0)) for p in d_proj] + _w_blocks(0, n_w)
        + [row, pl.BlockSpec((1, D_MODEL), lambda i: (0, 0)), row],
        out_specs=[row, pl.BlockSpec((8, D_MODEL), lambda i: (0, 0))],
        out_shape=[jax.ShapeDtypeStruct((SEQ, D_MODEL), F32), jax.ShapeDtypeStruct((8, D_MODEL), F32)],
        compiler_params=_params("arbitrary"),
    )(*d_proj, *([w_t] * n_w), x, gain, dy)


def _my_place():
    x, y, c = lax.axis_index("x"), lax.axis_index("y"), lax.axis_index("c")
    return jnp.stack([2 * x + y, c]).astype(jnp.int32)


def _swap_halves(grads):
    def body(g_ref, o_ref, send_sem, recv_sem):
        x, y, c = lax.axis_index("x"), lax.axis_index("y"), lax.axis_index("c")
        theirs = g_ref.at[:, pl.ds(pl.multiple_of((1 - c) * GRAD_HALF, 8), GRAD_HALF), :]
        cp = pltpu.make_async_remote_copy(src_ref=theirs, dst_ref=o_ref, send_sem=send_sem, recv_sem=recv_sem,
                                          device_id=(x, y, 1 - c), device_id_type=MESH)
        cp.start()
        cp.wait()

    return pl.pallas_call(
        body, name="reduce_swap_halves", in_specs=[ANY], out_specs=ANY,
        out_shape=jax.ShapeDtypeStruct((N_CHIPS, GRAD_HALF, D_MODEL), F32),
        scratch_shapes=[pltpu.SemaphoreType.DMA, pltpu.SemaphoreType.DMA],
    )(grads)


def _add_halves(place, grads, theirs):
    tr = 336
    n = GRAD_HALF // tr

    def body(place_ref, g_ref, t_ref, o_ref):
        o_ref[...] = (g_ref[...] + t_ref[...]).astype(BF16)

    return pl.pallas_call(
        body, name="reduce_add_halves",
        grid_spec=pltpu.PrefetchScalarGridSpec(
            num_scalar_prefetch=1, grid=(N_CHIPS, n),
            in_specs=[pl.BlockSpec((None, tr, D_MODEL), lambda s, i, p: (s, p[1] * n + i, 0)),
                      pl.BlockSpec((None, tr, D_MODEL), lambda s, i, p: (s, i, 0))],
            out_specs=pl.BlockSpec((None, tr, D_MODEL), lambda s, i, p: (s, i, 0))),
        out_shape=jax.ShapeDtypeStruct((N_CHIPS, GRAD_HALF, D_MODEL), BF16),
        compiler_params=_params("arbitrary", "arbitrary"),
    )(place, grads, theirs)


def _scatter_chips(chip_sums):
    def body(q_ref, o_ref, send_sems, recv_sems):
        x, y, c = lax.axis_index("x"), lax.axis_index("y"), lax.axis_index("c")
        chips = [(1 - x, y), (x, 1 - y), (1 - x, 1 - y)]
        copies = [pltpu.make_async_remote_copy(src_ref=q_ref.at[2 * cx + cy], dst_ref=o_ref.at[j],
                                               send_sem=send_sems.at[j], recv_sem=recv_sems.at[j],
                                               device_id=(cx, cy, c), device_id_type=MESH)
                  for j, (cx, cy) in enumerate(chips)]
        for cp in copies:
            cp.start()
        for cp in copies:
            cp.wait()

    return pl.pallas_call(
        body, name="reduce_scatter_chips", in_specs=[ANY], out_specs=ANY,
        out_shape=jax.ShapeDtypeStruct((3, GRAD_HALF, D_MODEL), BF16),
        scratch_shapes=[pltpu.SemaphoreType.DMA((3,)), pltpu.SemaphoreType.DMA((3,))],
    )(chip_sums)


def _add_chips(place, chip_sums, others):
    tr = 336
    n = GRAD_HALF // tr

    def body(place_ref, q_ref, o_ref, r_ref):
        acc = q_ref[...].astype(F32)
        for j in range(3):
            acc = acc + o_ref[j].astype(F32)
        r_ref[...] = acc

    return pl.pallas_call(
        body, name="reduce_add_chips",
        grid_spec=pltpu.PrefetchScalarGridSpec(
            num_scalar_prefetch=1, grid=(n,),
            in_specs=[pl.BlockSpec((None, tr, D_MODEL), lambda i, p: (p[0], i, 0)),
                      pl.BlockSpec((3, tr, D_MODEL), lambda i, p: (0, i, 0))],
            out_specs=pl.BlockSpec((tr, D_MODEL), lambda i, p: (p[1] * n + i, 0))),
        out_shape=jax.ShapeDtypeStruct((GRAD_ROWS, D_MODEL), F32),
        compiler_params=_params("arbitrary"),
    )(place, chip_sums, others)


def _join_halves(shard):
    def body(s_ref, o_ref, send_sem, recv_sem):
        x, y, c = lax.axis_index("x"), lax.axis_index("y"), lax.axis_index("c")
        mine = o_ref.at[pl.ds(pl.multiple_of(c * GRAD_HALF, 8), GRAD_HALF), :]
        other = o_ref.at[pl.ds(pl.multiple_of((1 - c) * GRAD_HALF, 8), GRAD_HALF), :]
        send = pltpu.make_async_remote_copy(src_ref=mine, dst_ref=mine, send_sem=send_sem, recv_sem=recv_sem,
                                            device_id=(x, y, 1 - c), device_id_type=MESH)
        send.start()
        pltpu.make_async_remote_copy(src_ref=other, dst_ref=other, send_sem=send_sem, recv_sem=recv_sem,
                                     device_id=(x, y, 1 - c), device_id_type=MESH).wait_recv()
        send.wait_send()

    return pl.pallas_call(
        body, name="reduce_join_halves", in_specs=[ANY], out_specs=ANY,
        out_shape=jax.ShapeDtypeStruct((GRAD_ROWS, D_MODEL), F32), input_output_aliases={0: 0},
        scratch_shapes=[pltpu.SemaphoreType.DMA, pltpu.SemaphoreType.DMA],
    )(shard)


def _gather_small(block):
    rows = block.shape[0]

    def body(b_ref, o_ref, send_sems, recv_sems, local_sem):
        x, y, c = lax.axis_index("x"), lax.axis_index("y"), lax.axis_index("c")
        me, sibling = (x, y, c), (x, y, 1 - c)
        chips = [(1 - x, y), (x, 1 - y), (1 - x, 1 - y)]

        def at(px, py, pc):
            return o_ref.at[pl.ds(pl.multiple_of((4 * px + 2 * py + pc) * rows, 8), rows), :]

        def copy(k, block_of, to, src=None):
            return pltpu.make_async_remote_copy(src_ref=at(*block_of) if src is None else src, dst_ref=at(*block_of),
                                                send_sem=send_sems.at[k], recv_sem=recv_sems.at[k],
                                                device_id=to, device_id_type=MESH)

        mine = pltpu.make_async_copy(b_ref, at(*me), local_sem)
        mine.start()
        first = [copy(0, me, sibling, src=b_ref)]
        first += [copy(1 + j, me, (*chip, c), src=b_ref) for j, chip in enumerate(chips)]
        for cp in first:
            cp.start()
        passed = [copy(4 + j, (*chip, c), sibling) for j, chip in enumerate(chips)]
        for j, chip in enumerate(chips):
            copy(1 + j, (*chip, c), me).wait_recv()
            passed[j].start()
        copy(0, sibling, me).wait_recv()
        for j, chip in enumerate(chips):
            copy(4 + j, (*chip, 1 - c), me).wait_recv()
        for cp in first + passed:
            cp.wait_send()
        mine.wait()

    return pl.pallas_call(
        body, name="gather_small_grads",
        in_specs=[pl.BlockSpec(memory_space=pltpu.VMEM)], out_specs=pl.BlockSpec(memory_space=pltpu.VMEM),
        out_shape=jax.ShapeDtypeStruct((8 * rows, D_MODEL), F32),
        scratch_shapes=[pltpu.SemaphoreType.DMA((7,)), pltpu.SemaphoreType.DMA((7,)), pltpu.SemaphoreType.DMA],
    )(block)


def _sum_devices(blocks):
    def body(b_ref, o_ref):
        acc = b_ref[0:8, :]
        for dev in range(1, 8):
            acc = acc + b_ref[8 * dev:8 * dev + 8, :]
        o_ref[...] = acc

    return pl.pallas_call(body, name="sum_small_grads", out_shape=jax.ShapeDtypeStruct((8, D_MODEL), F32))(blocks)


def _adamw_math(w, g, m, v):
    m = ADAM_B1 * m + (1.0 - ADAM_B1) * g
    v = ADAM_B2 * v + (1.0 - ADAM_B2) * (g * g)
    m_hat = m / (1.0 - ADAM_B1 ** ADAM_STEP)
    v_hat = v / (1.0 - ADAM_B2 ** ADAM_STEP)
    return -ADAM_LR * (m_hat / (jnp.sqrt(v_hat) + ADAM_EPS) + ADAM_WD * w), m, v


def _adamw(w, g, m, v, name):
    r, c = w.shape
    tr = 128 if r % 128 == 0 else r

    def body(w_ref, g_ref, m_ref, v_ref, d_ref, nm_ref, nv_ref):
        d_ref[...], nm_ref[...], nv_ref[...] = _adamw_math(w_ref[...], g_ref[...], m_ref[...], v_ref[...])

    spec = pl.BlockSpec((tr, c), lambda i: (i, 0))
    return pl.pallas_call(
        body, name=name, grid=(r // tr,), in_specs=[spec] * 4, out_specs=[spec] * 3,
        out_shape=[jax.ShapeDtypeStruct((r, c), F32)] * 3, compiler_params=_params("arbitrary"),
    )(w, g, m, v)


def _adamw_small(ws, gs, ms, vs):
    n = len(ws)

    def body(*refs):
        ins, outs = refs[:4 * n], refs[4 * n:]
        for k in range(n):
            d, m, v = _adamw_math(ins[k][...], ins[n + k][...], ins[2 * n + k][...], ins[3 * n + k][...])
            outs[k][...], outs[n + k][...], outs[2 * n + k][...] = d, m, v

    shapes = [jax.ShapeDtypeStruct(w.shape, F32) for w in ws]
    res = pl.pallas_call(body, name="adamw_small", out_shape=shapes * 3)(*ws, *gs, *ms, *vs)
    return res[:n], res[n:2 * n], res[2 * n:]


def _fold_heads(partials):
    t = jnp.sum(partials[:, 0, :], axis=0)
    return (t[:HEAD_DIM] + t[HEAD_DIM:]).reshape(1, HEAD_DIM)


def _local_step(x, target, norm_gain, w_t, w_a, w_b, w_o, b_m, q_norm_a, k_norm_a, q_norm_b, k_norm_b, sink_a,
                rel_bias):
    two = lambda gain: jnp.concatenate([gain, gain], axis=1)
    bias_a = _bias_table(rel_bias[:, :8], A_HALF_WINDOW, 1)
    bias_b = jnp.concatenate([_bias_table(rel_bias[:, 8 + 8 * g:16 + 8 * g], B_HALF_WINDOW, d)
                              for g, d in enumerate(B_DILATIONS)], axis=0)

    qkv, h = _in_proj(x, norm_gain, w_t, 0, QKV_WIDTH // W_BLOCK, BF16, "in_proj_qkv")
    gates, _ = _in_proj(x, norm_gain, w_t, QKV_WIDTH // W_BLOCK, GATE_WIDTH // W_BLOCK, F32, "in_proj_gates")
    out_a, lse_a = _attn_a_fwd(qkv, two(q_norm_a), two(k_norm_a), bias_a, sink_a)
    out_b, lse_b = _attn_b_fwd(qkv, two(q_norm_b), two(k_norm_b), bias_b)

    dy, dgates, d_out_a, d_out_b, d_wa, d_wb, d_wo, d_bm, sq = _middle(
        out_a, out_b, gates, x, target, w_a, w_b, w_o, b_m)
    loss = (0.5 / D_MODEL) * jnp.sum(sq)

    dq_a, dkv_a, dgq_a, dgk_a, ds_a, dsink = _attn_a_bwd(
        qkv, two(q_norm_a), two(k_norm_a), bias_a, sink_a, out_a, lse_a, d_out_a)
    dq_b, dk_b, dv_b, dgq_b, dgk_b, ds_b = _attn_b_bwd(
        qkv, two(q_norm_b), two(k_norm_b), bias_b, out_b, lse_b, d_out_b)
    d_proj = (dq_a, dkv_a, dq_b, dk_b, dv_b, dgates)

    d_wt = _d_w_in(d_proj, h)
    grad_x, d_gain = _d_x(d_proj, w_t, x, norm_gain, dy)

    d_rel = jnp.concatenate(
        [_bias_grad(ds_a, A_HALF_WINDOW, 1)]
        + [_bias_grad(ds_b[4 * g:4 * g + 4], B_HALF_WINDOW, d) for g, d in enumerate(B_DILATIONS)], axis=1)
    d_sink = jnp.sum(dsink, axis=(2, 3)).reshape(1, 8)
    dgk_a_row = dgk_a[0]
    small = jnp.zeros((8, D_MODEL), F32)
    small = small.at[0].set(d_gain[0])
    small = small.at[1].set(d_rel.reshape(-1))
    misc = jnp.concatenate([_fold_heads(dgq_a), (dgk_a_row[:HEAD_DIM] + dgk_a_row[HEAD_DIM:]).reshape(1, HEAD_DIM),
                            _fold_heads(dgq_b), _fold_heads(dgk_b), d_sink], axis=1)
    small = small.at[2, :264].set(misc[0])

    d_bm_rows = jnp.pad(d_bm.reshape(2, N_CHIPS, 256).transpose(1, 0, 2),
                        ((0, 0), (0, GRAD_ROWS - 2626), (0, D_MODEL - 256)))
    big = jnp.concatenate([d_wt.reshape(N_CHIPS, W_IN_SHARD, D_MODEL), d_wo.reshape(N_CHIPS, 256, D_MODEL),
                           d_wa.reshape(N_CHIPS, 128, D_MODEL), d_wb.reshape(N_CHIPS, 128, D_MODEL),
                           d_bm_rows], axis=1)
    return loss, grad_x, big, small


def _unpack_weights(w_t_all, small_all):
    sm = small_all.reshape(N_CHIPS, SMALL_ROWS, D_MODEL)
    w_o = sm[:, 0:256].reshape(D_MODEL, D_MODEL)
    w_a = sm[:, 256:384].reshape(N_CHIPS, 512, 256).transpose(1, 0, 2).reshape(512, D_MODEL)
    w_b = sm[:, 384:512].reshape(N_CHIPS, 512, 256).transpose(1, 0, 2).reshape(512, D_MODEL)
    b_m = lax.bitcast_convert_type(sm[:, 512].reshape(N_CHIPS, 2, 256, 2), F32)
    return w_t_all, w_a, w_b, w_o, b_m.transpose(1, 0, 2).reshape(2, D_MODEL)


def _pack_small_weights(w_branch_a, w_branch_b, b_merge, w_out):
    b_m = jnp.pad(lax.bitcast_convert_type(b_merge, BF16).reshape(1, D_MODEL), ((0, SMALL_ROWS - 513), (0, 0)))
    return jnp.concatenate([w_out.astype(BF16), w_branch_a.astype(BF16).reshape(128, D_MODEL),
                            w_branch_b.astype(BF16).reshape(128, D_MODEL), b_m], axis=0)


def kernel(x, norm_gain, w_in, q_norm_a, k_norm_a, q_norm_b, k_norm_b, sink_a, rel_bias, w_branch_a, w_branch_b, b_merge, w_out, loss_target, m_norm_gain, m_w_in, m_q_norm_a, m_k_norm_a, m_q_norm_b, m_k_norm_b, m_sink_a, m_rel_bias, m_w_branch_a, m_w_branch_b, m_b_merge, m_w_out, v_norm_gain, v_w_in, v_q_norm_a, v_k_norm_a, v_q_norm_b, v_k_norm_b, v_sink_a, v_rel_bias, v_w_branch_a, v_w_branch_b, v_b_merge, v_w_out):
    w_in, w_branch_a, w_branch_b, b_merge, w_out = w_in[0], w_branch_a[0], w_branch_b[0], b_merge[0], w_out[0]

    wt_shard = _transpose_cast(w_in, BF16, "w_in_transpose")
    w_t, w_a, w_b, w_o, b_m = _unpack_weights(
        *_gather_weights(wt_shard, _pack_small_weights(w_branch_a, w_branch_b, b_merge, w_out)))

    loss_part, grad_x, big, small = _local_step(
        x[0], loss_target[0], norm_gain, w_t, w_a, w_b, w_o, b_m, q_norm_a, k_norm_a, q_norm_b, k_norm_b,
        sink_a, rel_bias)
    loss = lax.psum(loss_part, ("x", "y", "c"))

    place = _my_place()
    chip_sums = _add_halves(place, big, _swap_halves(big))
    shard = _join_halves(_add_chips(place, chip_sums, _scatter_chips(chip_sums)))
    small = _sum_devices(_gather_small(small))

    g_w_in = _transpose_cast(shard[:W_IN_SHARD], F32, "grad_w_in_transpose")
    g_w_out = shard[2112:2368]
    g_w_a = shard[2368:2496].reshape(512, 256)
    g_w_b = shard[2496:2624].reshape(512, 256)
    g_b_merge = shard[2624:2626, :256]
    g_norm_gain = small[0:1]
    g_rel_bias = small[1].reshape(N_BUCKETS, N_BUCKETS)
    g_q_a, g_k_a, g_q_b, g_k_b = (small[2:3, 64 * k:64 * k + 64] for k in range(4))
    g_sink = small[2:3, 256:264]

    big_names = (("w_in", w_in, g_w_in, m_w_in[0], v_w_in[0]),
                 ("w_branch_a", w_branch_a, g_w_a, m_w_branch_a[0], v_w_branch_a[0]),
                 ("w_branch_b", w_branch_b, g_w_b, m_w_branch_b[0], v_w_branch_b[0]),
                 ("w_out", w_out, g_w_out, m_w_out[0], v_w_out[0]))
    upd = {name: (g,) + tuple(_adamw(w, g, m, v, "adamw_" + name)) for name, w, g, m, v in big_names}
    small_names = ("norm_gain", "q_norm_a", "k_norm_a", "q_norm_b", "k_norm_b", "sink_a", "rel_bias", "b_merge")
    ws = [norm_gain, q_norm_a, k_norm_a, q_norm_b, k_norm_b, sink_a, rel_bias, b_merge]
    gs = [g_norm_gain, g_q_a, g_k_a, g_q_b, g_k_b, g_sink, g_rel_bias, g_b_merge]
    ms = [m_norm_gain, m_q_norm_a, m_k_norm_a, m_q_norm_b, m_k_norm_b, m_sink_a, m_rel_bias, m_b_merge[0]]
    vs = [v_norm_gain, v_q_norm_a, v_k_norm_a, v_q_norm_b, v_k_norm_b, v_sink_a, v_rel_bias, v_b_merge[0]]
    ds, nms, nvs = _adamw_small(ws, gs, ms, vs)
    for k, name in enumerate(small_names):
        upd[name] = (gs[k], ds[k], nms[k], nvs[k])

    order = ("norm_gain", "w_in", "q_norm_a", "k_norm_a", "q_norm_b", "k_norm_b", "sink_a", "rel_bias",
             "w_branch_a", "w_branch_b", "b_merge", "w_out")
    lead = {"w_in", "w_branch_a", "w_branch_b", "b_merge", "w_out"}
    outs = [loss, grad_x[None]]
    for part in range(4):
        outs += [upd[name][part][None] if name in lead else upd[name][part] for name in order]
    return tuple(outs)
```

```python
import math

import numpy as np
import jax
import jax.numpy as jnp
from jax import lax
from jax.experimental import pallas as pl
from jax.experimental.pallas import tpu as pltpu

F32 = jnp.float32
BF16 = jnp.bfloat16

SEQ = 4096
D_MODEL = 1024
HEAD_DIM = 64
LANES = 128
EPS = 1e-6
NEG_INF = -1e30
SCALE = HEAD_DIM ** -0.5
N_BUCKETS = 32
MAX_DISTANCE = 1024
N_CHIPS = 4

A_HALF_WINDOW = 128
B_HALF_WINDOW = 64
B_DILATIONS = (1, 4, 16)
Q_BLOCK = 128

QKV_WIDTH = 5376
GATE_WIDTH = 3072
QA_BLK, KA_BLK, VA_BLK = 0, 4, 5
QB_BLK, KB_BLK, VB_BLK = 6, 18, 30
IN_WIDTH = QKV_WIDTH + GATE_WIDTH
W_IN_SHARD = IN_WIDTH // N_CHIPS

SMALL_ROWS = 544
REST_ROWS = 544

ADAM_LR = 0.001
ADAM_B1 = 0.9
ADAM_B2 = 0.999
ADAM_EPS = 1e-08
ADAM_WD = 0.01
ADAM_STEP = 10

VMEM_LIMIT = 56 * 1024 * 1024

NT = (((1,), (1,)), ((), ()))
TN = (((0,), (0,)), ((), ()))
MESH = pl.DeviceIdType.MESH
ANY = pl.BlockSpec(memory_space=pl.ANY)


def _dot(a, b, dims=None):
    if dims is None:
        return jnp.dot(a, b, preferred_element_type=F32)
    return lax.dot_general(a, b, dims, preferred_element_type=F32)


def _params(*semantics):
    return pltpu.CompilerParams(dimension_semantics=semantics or None, vmem_limit_bytes=VMEM_LIMIT)


def _bucket_onehot(half_window, stride):
    w = Q_BLOCK + 2 * half_window
    rel = (np.arange(w)[None, :] - half_window - np.arange(Q_BLOCK)[:, None])
    band = np.abs(rel) <= half_window
    rel = rel * stride
    half, max_exact = N_BUCKETS // 2, N_BUCKETS // 4
    n = np.abs(rel)
    nf = np.maximum(n, max_exact).astype(np.float32)
    large = max_exact + (np.log(nf / np.float32(max_exact)) / np.float32(math.log(MAX_DISTANCE / max_exact))
                         * np.float32(half - max_exact)).astype(np.int32)
    large = np.minimum(large, half - 1)
    bucket = (rel > 0).astype(np.int32) * half + np.where(n < max_exact, n, large)
    onehot = (bucket[..., None] == np.arange(N_BUCKETS)) & band[..., None]
    return onehot.reshape(Q_BLOCK * w, N_BUCKETS).astype(np.float32), band


def _bias_table(rel_bias_cols, half_window, stride):
    onehot, band = _bucket_onehot(half_window, stride)
    h = rel_bias_cols.shape[1]
    w = Q_BLOCK + 2 * half_window
    t = jnp.einsum("pb,bh->hp", jnp.asarray(onehot), rel_bias_cols, precision=lax.Precision.HIGHEST)
    t = t.reshape(h, Q_BLOCK, w) + jnp.asarray(np.where(band, 0.0, NEG_INF).astype(np.float32))
    return t.reshape(h // 2, 2, Q_BLOCK, w)


def _bias_grad(ds_sum, half_window, stride):
    onehot, _ = _bucket_onehot(half_window, stride)
    h = ds_sum.shape[0] * 2
    return jnp.einsum("pb,hp->bh", jnp.asarray(onehot), ds_sum.reshape(h, -1), precision=lax.Precision.HIGHEST)


def _transpose_cast(w, out_dtype, name):
    r, c = w.shape

    def body(w_ref, o_ref):
        o_ref[...] = w_ref[...].T.astype(out_dtype)

    if r % LANES == 0:
        steps = pl.cdiv(c, LANES)
        in_spec, out_spec = pl.BlockSpec((r, LANES), lambda j: (0, j)), pl.BlockSpec((LANES, r), lambda j: (j, 0))
    else:
        steps = pl.cdiv(r, LANES)
        in_spec, out_spec = pl.BlockSpec((LANES, c), lambda j: (j, 0)), pl.BlockSpec((c, LANES), lambda j: (0, j))
    return pl.pallas_call(
        body, name=name, grid=(steps,), in_specs=[in_spec], out_specs=out_spec,
        out_shape=jax.ShapeDtypeStruct((c, r), out_dtype),
        compiler_params=_params("arbitrary"),
    )(w)


def _gather_weights(wt_shard, small_shard):
    bufs = ((W_IN_SHARD, IN_WIDTH), (SMALL_ROWS, N_CHIPS * SMALL_ROWS))

    stage_rows = 528

    def body(wt_in, sm_in, wt_out, sm_out, send_sems, recv_sems, in_sems, out_sems, stage):
        x, y, c = lax.axis_index("x"), lax.axis_index("y"), lax.axis_index("c")
        sibling = (x, y, 1 - c)
        chips = [(1 - x, y), (x, 1 - y), (1 - x, 1 - y)]
        my_chip = 2 * x + y
        refs = ((wt_in, wt_out), (sm_in, sm_out))

        def keep_own():
            pieces = [(b, r0) for b in range(2) for r0 in range(0, bufs[b][0], stage_rows)]
            outs = []
            for i, (b, r0) in enumerate(pieces):
                rows = min(stage_rows, bufs[b][0] - r0)
                slot = i % 2
                if i >= 2:
                    outs[i - 2].wait()
                buf = stage.at[slot, pl.ds(0, rows), :]
                load = pltpu.make_async_copy(refs[b][0].at[pl.ds(r0, rows), :], buf, in_sems.at[slot])
                load.start()
                load.wait()
                start = pl.multiple_of(my_chip * bufs[b][0] + r0, 16)
                outs.append(pltpu.make_async_copy(buf, refs[b][1].at[pl.ds(start, rows), :], out_sems.at[slot]))
                outs[i].start()
            for cp in outs[-2:]:
                cp.wait()

        def half_of(b, chip, half):
            rows = bufs[b][0]
            start = pl.multiple_of(chip * rows + half * (rows // 2), 16)
            return refs[b][1].at[pl.ds(start, rows // 2), :]

        def copy(k, src, dst, to):
            return pltpu.make_async_remote_copy(src_ref=src, dst_ref=dst, send_sem=send_sems.at[k],
                                                recv_sem=recv_sems.at[k], device_id=to, device_id_type=MESH)

        first, passed = [], []
        for b in range(2):
            rows = bufs[b][0]
            src = refs[b][0].at[pl.ds(pl.multiple_of(c * (rows // 2), 16), rows // 2), :]
            for j, chip in enumerate(chips):
                first.append(copy(3 * b + j, src, half_of(b, my_chip, c), (*chip, c)))
        for cp in first:
            cp.start()
        keep_own()
        for b in range(2):
            for j, (cx, cy) in enumerate(chips):
                landed = half_of(b, 2 * cx + cy, c)
                copy(3 * b + j, landed, landed, sibling).wait_recv()
                fwd = copy(6 + 3 * b + j, landed, landed, sibling)
                fwd.start()
                passed.append(fwd)
        for b in range(2):
            for j, (cx, cy) in enumerate(chips):
                other = half_of(b, 2 * cx + cy, 1 - c)
                copy(6 + 3 * b + j, other, other, sibling).wait_recv()
        for cp in first + passed:
            cp.wait_send()

    return pl.pallas_call(
        body, name="gather_weights",
        in_specs=[ANY, ANY], out_specs=[ANY, ANY],
        out_shape=[jax.ShapeDtypeStruct((bufs[0][1], D_MODEL), BF16),
                   jax.ShapeDtypeStruct((bufs[1][1], D_MODEL), BF16)],
        scratch_shapes=[pltpu.SemaphoreType.DMA((12,)), pltpu.SemaphoreType.DMA((12,)),
                        pltpu.SemaphoreType.DMA((2,)), pltpu.SemaphoreType.DMA((2,)),
                        pltpu.VMEM((2, stage_rows, D_MODEL), BF16)],
    )(wt_shard, small_shard)


W_BLOCK = 768


def _w_blocks(first, count):
    return [pl.BlockSpec((W_BLOCK, D_MODEL), lambda *_, k=k: (first + k, 0)) for k in range(count)]


def _in_proj(x, gain, w_t, first_block, n_blocks, out_dtype, name):
    tm = 256

    def body(x_ref, g_ref, *refs):
        w_refs, (o_ref, h_ref) = refs[:n_blocks], refs[n_blocks:]
        xf = x_ref[...]
        r = lax.rsqrt(jnp.mean(xf * xf, axis=-1, keepdims=True) + EPS)
        h = ((xf * r) * g_ref[...]).astype(BF16)
        h_ref[...] = h
        for k, w_ref in enumerate(w_refs):
            o_ref[:, k * W_BLOCK:(k + 1) * W_BLOCK] = _dot(h, w_ref[...], NT).astype(out_dtype)

    return pl.pallas_call(
        body, name=name, grid=(SEQ // tm,),
        in_specs=[pl.BlockSpec((tm, D_MODEL), lambda i: (i, 0)), pl.BlockSpec((1, D_MODEL), lambda i: (0, 0))]
        + _w_blocks(first_block, n_blocks),
        out_specs=[pl.BlockSpec((tm, W_BLOCK * n_blocks), lambda i: (i, 0)),
                   pl.BlockSpec((tm, D_MODEL), lambda i: (i, 0))],
        out_shape=[jax.ShapeDtypeStruct((SEQ, W_BLOCK * n_blocks), out_dtype),
                   jax.ShapeDtypeStruct((SEQ, D_MODEL), BF16)],
        compiler_params=_params("arbitrary"),
    )(x, gain, *([w_t] * n_blocks))


CHUNK = 256
CHUNK_UNROLL = 4
TILE_UNROLL = 4


def _low_half():
    return lax.broadcasted_iota(jnp.int32, (1, LANES), 1) < HEAD_DIM


def _half_sum(v, low):
    del low
    row = lax.broadcasted_iota(jnp.int32, (2 * LANES, LANES), 0)
    col = lax.broadcasted_iota(jnp.int32, (2 * LANES, LANES), 1)
    ones = jnp.where((row % LANES) // HEAD_DIM == col // HEAD_DIM, 1.0, 0.0).astype(BF16)
    hi = v.astype(BF16)
    lo = (v - hi.astype(F32)).astype(BF16)
    return _dot(jnp.concatenate([hi, lo], axis=1), ones)


def _chunks(fn, init=0):
    def body(i, carry):
        for u in range(CHUNK_UNROLL):
            carry = fn(pl.multiple_of((i * CHUNK_UNROLL + u) * CHUNK, CHUNK), carry)
        return carry

    return lax.fori_loop(0, SEQ // (CHUNK * CHUNK_UNROLL), body, init)


def _inv_rms(t, low):
    return lax.rsqrt(_half_sum(t * t, low) * (1.0 / HEAD_DIM) + EPS)


def _prep_q(q_ref, gain_ref, qn_ref):
    low = _low_half()

    def step(r0, carry):
        q = q_ref[pl.ds(r0, CHUNK), :].astype(F32)
        qn_ref[pl.ds(r0, CHUNK), :] = ((q * _inv_rms(q, low)) * gain_ref[...]) * SCALE
        return carry

    _chunks(step)


def _own_half(t, keep):
    return jnp.where(keep, t, pltpu.roll(t, HEAD_DIM, 1))


def _prep_kv(k_ref, v_ref, gain_ref, kp_ref, vp_ref, pad, keep=None):
    low = _low_half()
    zeros = jnp.zeros((pad, LANES), F32)
    for ref in (kp_ref, vp_ref):
        ref[pl.ds(0, pad), :] = zeros
        ref[pl.ds(pad + SEQ, pad), :] = zeros

    def step(r0, carry):
        k = k_ref[pl.ds(r0, CHUNK), :].astype(F32)
        v = v_ref[pl.ds(r0, CHUNK), :].astype(F32)
        kn = (k * _inv_rms(k, low)) * gain_ref[...]
        if keep is not None:
            kn, v = _own_half(kn, keep), _own_half(v, keep)
        kp_ref[pl.ds(pad + r0, CHUNK), :] = kn
        vp_ref[pl.ds(pad + r0, CHUNK), :] = v
        return carry

    _chunks(step)


def _tiles(d, half_window, fn):
    w = Q_BLOCK + 2 * half_window
    length = SEQ // d
    n_blocks = length // Q_BLOCK
    col = lax.broadcasted_iota(jnp.int32, (1, w), 1)

    def step(it, carry):
        c, n = it // n_blocks, it % n_blocks
        start = c + (d * Q_BLOCK) * n
        if d == 1:
            start = pl.multiple_of(start, Q_BLOCK)
            q_rows, k_rows = pl.ds(start, Q_BLOCK), pl.ds(start, w)
        else:
            q_rows, k_rows = pl.ds(start, Q_BLOCK, stride=d), pl.ds(start, w, stride=d)
        t = n * Q_BLOCK - half_window + col
        edge = jnp.where((t < 0) | (t >= length), NEG_INF, 0.0)
        fn(q_rows, k_rows, edge)
        return carry

    lax.fori_loop(0, d * n_blocks, step, 0, unroll=TILE_UNROLL)


def _stack_heads(t, low):
    return jnp.concatenate([jnp.where(low, t, 0.0), jnp.where(low, 0.0, t)], axis=0).astype(BF16)


def _unstack_heads(t, low):
    return jnp.where(low, t[:Q_BLOCK], t[Q_BLOCK:])


def _per_head(pair):
    return jnp.concatenate([jnp.full((Q_BLOCK, 1), pair[0], F32), jnp.full((Q_BLOCK, 1), pair[1], F32)], axis=0)


def _fwd_tiles(qn_ref, kp_ref, vp_ref, bias_ref, emit, *, d, half_window, sinks=None):
    low = _low_half()
    w = Q_BLOCK + 2 * half_window
    sink = None if sinks is None else _per_head(sinks)

    def tile(q_rows, k_rows, edge):
        q2 = _stack_heads(qn_ref[q_rows, :], low)
        k = kp_ref[k_rows, :].astype(BF16)
        v1 = jnp.concatenate([vp_ref[k_rows, :], jnp.ones((w, LANES), F32)], axis=1).astype(BF16)
        s = _dot(q2, k, NT) + bias_ref[...].reshape(2 * Q_BLOCK, w) + edge
        m = jnp.max(s, axis=-1, keepdims=True)
        if sink is not None:
            m = jnp.maximum(m, sink)
        o = _dot(jnp.exp(s - m).astype(BF16), v1)
        l = o[:, LANES:]
        if sink is not None:
            l = l + jnp.exp(sink - m)
        emit(q_rows, _unstack_heads(o[:, :LANES] * (1.0 / l), low), _unstack_heads(m + jnp.log(l), low))

    _tiles(d, half_window, tile)


def _bwd_tiles(qn_ref, kp_ref, vp_ref, bias_ref, do_ref, lse_ref, delta_ref, dq_ref, dk_ref, dv_ref, ds_ref,
               *, d, half_window, sinks=None, dsink_ref=None):
    low = _low_half()
    w = Q_BLOCK + 2 * half_window
    sink = None if sinks is None else _per_head(sinks)

    def rows_of(t):
        return jnp.concatenate([t[:, 0:1], t[:, HEAD_DIM:HEAD_DIM + 1]], axis=0)

    def tile(q_rows, k_rows, edge):
        q2 = _stack_heads(qn_ref[q_rows, :], low)
        do2 = _stack_heads(do_ref[q_rows, :], low)
        k = kp_ref[k_rows, :].astype(BF16)
        v = vp_ref[k_rows, :].astype(BF16)
        lse = rows_of(lse_ref[q_rows, :])
        delta = rows_of(delta_ref[q_rows, :])
        p = jnp.exp(_dot(q2, k, NT) + bias_ref[...].reshape(2 * Q_BLOCK, w) + edge - lse)
        ds = p * (_dot(do2, v, NT) - delta)
        ds_ref[...] += ds.reshape(2, Q_BLOCK, w)
        if sink is not None:
            dsink_ref[...] += (-jnp.exp(sink - lse) * delta).reshape(2, Q_BLOCK, 1)
        dsb, pb = ds.astype(BF16), p.astype(BF16)
        dq_ref[q_rows, :] = _unstack_heads(_dot(dsb, k), low)
        dk_ref[k_rows, :] += _dot(dsb, q2, TN)
        dv_ref[k_rows, :] += _dot(pb, do2, TN)

    _tiles(d, half_window, tile)


def _prep_delta(do_ref, o_ref, delta_ref):
    low = _low_half()

    def step(r0, carry):
        delta_ref[pl.ds(r0, CHUNK), :] = _half_sum(do_ref[pl.ds(r0, CHUNK), :] * o_ref[pl.ds(r0, CHUNK), :], low)
        return carry

    _chunks(step)


def _norm_bwd(raw_ref, gain_ref, dn_ref, dn_offset, out_ref, scale):
    low = _low_half()

    def step(r0, dgain):
        t = raw_ref[pl.ds(r0, CHUNK), :].astype(F32)
        dn = dn_ref[pl.ds(dn_offset + r0, CHUNK), :]
        dth = dn * (gain_ref[...] * scale)
        sums = _half_sum(jnp.concatenate([t * t, dth * t], axis=0), low)
        r = lax.rsqrt(sums[:CHUNK] * (1.0 / HEAD_DIM) + EPS)
        th = t * r
        out_ref[pl.ds(r0, CHUNK), :] = (r * (dth - th * (r * sums[CHUNK:] * (1.0 / HEAD_DIM)))).astype(BF16)
        return dgain + jnp.sum(dn * th, axis=0, keepdims=True) * scale

    return _chunks(step, jnp.zeros((1, LANES), F32))


def _rows8(v):
    return jnp.broadcast_to(v, (8, v.shape[-1]))


A_W = Q_BLOCK + 2 * A_HALF_WINDOW
A_PAD = A_HALF_WINDOW


def _seq_block(col_fn):
    return pl.BlockSpec((SEQ, LANES), col_fn)


def _attn_a_fwd(qkv, gain_q, gain_k, bias, sink):
    def body(sink_ref, q_ref, k_ref, v_ref, gq_ref, gk_ref, bias_ref, o_ref, lse_ref, qn_ref, kp_ref, vp_ref):
        hp = pl.program_id(0)
        keep = (lax.broadcasted_iota(jnp.int32, (1, LANES), 1) // HEAD_DIM) == hp // 2
        _prep_q(q_ref, gq_ref, qn_ref)
        _prep_kv(k_ref, v_ref, gk_ref, kp_ref, vp_ref, A_PAD, keep)

        def emit(rows, out, lse):
            o_ref[rows, :] = out
            lse_ref[rows, :] = lse

        _fwd_tiles(qn_ref, kp_ref, vp_ref, bias_ref, emit, d=1, half_window=A_HALF_WINDOW,
                   sinks=(sink_ref[2 * hp], sink_ref[2 * hp + 1]))

    vec = pl.BlockSpec((1, LANES), lambda hp, s: (0, 0))
    return pl.pallas_call(
        body, name="attn_a_fwd",
        grid_spec=pltpu.PrefetchScalarGridSpec(
            num_scalar_prefetch=1, grid=(4,),
            in_specs=[_seq_block(lambda hp, s: (0, QA_BLK + hp)), _seq_block(lambda hp, s: (0, KA_BLK)),
                      _seq_block(lambda hp, s: (0, VA_BLK)), vec, vec,
                      pl.BlockSpec((None, 2, Q_BLOCK, A_W), lambda hp, s: (hp, 0, 0, 0))],
            out_specs=[_seq_block(lambda hp, s: (0, hp)), _seq_block(lambda hp, s: (0, hp))],
            scratch_shapes=[pltpu.VMEM((SEQ, LANES), F32), pltpu.VMEM((SEQ + 2 * A_PAD, LANES), F32),
                            pltpu.VMEM((SEQ + 2 * A_PAD, LANES), F32)]),
        out_shape=[jax.ShapeDtypeStruct((SEQ, 512), F32)] * 2,
        compiler_params=_params("arbitrary"),
    )(sink.reshape(8), qkv, qkv, qkv, gain_q, gain_k, bias)


def _attn_a_bwd(qkv, gain_q, gain_k, bias, sink, out, lse, d_out):
    def body(sink_ref, q_ref, k_ref, v_ref, gq_ref, gk_ref, bias_ref, o_ref, lse_ref, do_ref,
             dq_out, dkv_out, dgq_out, dgk_out, ds_out, dsink_out,
             qn_ref, kp_ref, vp_ref, delta_ref, dq_ref, dk_ref, dv_ref, dk_tot, dv_tot):
        hp = pl.program_id(0)
        kv_head = hp // 2
        keep = (lax.broadcasted_iota(jnp.int32, (1, LANES), 1) // HEAD_DIM) == kv_head
        _prep_q(q_ref, gq_ref, qn_ref)
        _prep_kv(k_ref, v_ref, gk_ref, kp_ref, vp_ref, A_PAD, keep)
        _prep_delta(do_ref, o_ref, delta_ref)
        dk_ref[...] = jnp.zeros_like(dk_ref)
        dv_ref[...] = jnp.zeros_like(dv_ref)
        ds_out[...] = jnp.zeros_like(ds_out)
        dsink_out[...] = jnp.zeros_like(dsink_out)

        @pl.when(hp == 0)
        def _():
            dk_tot[...] = jnp.zeros_like(dk_tot)
            dv_tot[...] = jnp.zeros_like(dv_tot)

        _bwd_tiles(qn_ref, kp_ref, vp_ref, bias_ref, do_ref, lse_ref, delta_ref, dq_ref, dk_ref, dv_ref, ds_out,
                   d=1, half_window=A_HALF_WINDOW, sinks=(sink_ref[2 * hp], sink_ref[2 * hp + 1]),
                   dsink_ref=dsink_out)
        dgq_out[...] = _rows8(_norm_bwd(q_ref, gq_ref, dq_ref, 0, dq_out, SCALE))

        def fold(r0, carry):
            rows = pl.ds(A_PAD + r0, CHUNK)
            for acc, tot in ((dk_ref, dk_tot), (dv_ref, dv_tot)):
                t = acc[rows, :]
                tot[pl.ds(r0, CHUNK), :] += jnp.where(keep, t + pltpu.roll(t, HEAD_DIM, 1), 0.0)
            return carry

        _chunks(fold)

        @pl.when(hp == 3)
        def _():
            dgk_out[...] = _rows8(_norm_bwd(k_ref, gk_ref, dk_tot, 0, dkv_out.at[:, pl.ds(0, LANES)], 1.0))
            dkv_out[:, LANES:2 * LANES] = dv_tot[...].astype(BF16)

    vec = pl.BlockSpec((1, LANES), lambda hp, s: (0, 0))
    seq_f32 = pltpu.VMEM((SEQ, LANES), F32)
    padded = pltpu.VMEM((SEQ + 2 * A_PAD, LANES), F32)
    return pl.pallas_call(
        body, name="attn_a_bwd",
        grid_spec=pltpu.PrefetchScalarGridSpec(
            num_scalar_prefetch=1, grid=(4,),
            in_specs=[_seq_block(lambda hp, s: (0, QA_BLK + hp)), _seq_block(lambda hp, s: (0, KA_BLK)),
                      _seq_block(lambda hp, s: (0, VA_BLK)), vec, vec,
                      pl.BlockSpec((None, 2, Q_BLOCK, A_W), lambda hp, s: (hp, 0, 0, 0)),
                      _seq_block(lambda hp, s: (0, hp)), _seq_block(lambda hp, s: (0, hp)),
                      _seq_block(lambda hp, s: (0, hp))],
            out_specs=[_seq_block(lambda hp, s: (0, hp)), pl.BlockSpec((SEQ, 2 * LANES), lambda hp, s: (0, 0)),
                       pl.BlockSpec((None, 8, LANES), lambda hp, s: (hp, 0, 0)),
                       pl.BlockSpec((8, LANES), lambda hp, s: (0, 0)),
                       pl.BlockSpec((None, 2, Q_BLOCK, A_W), lambda hp, s: (hp, 0, 0, 0)),
                       pl.BlockSpec((None, 2, Q_BLOCK, 1), lambda hp, s: (hp, 0, 0, 0))],
            scratch_shapes=[seq_f32, padded, padded, seq_f32, seq_f32, padded, padded, seq_f32, seq_f32]),
        out_shape=[jax.ShapeDtypeStruct((SEQ, 512), BF16), jax.ShapeDtypeStruct((SEQ, 2 * LANES), BF16),
                   jax.ShapeDtypeStruct((4, 8, LANES), F32), jax.ShapeDtypeStruct((8, LANES), F32),
           jax.ShapeDtypeStruct((4, 2, Q_BLOCK, A_W), F32), jax.ShapeDtypeStruct((4, 2, Q_BLOCK, 1), F32)],
        compiler_params=_params("arbitrary"),
    )(sink.reshape(8), qkv, qkv, qkv, gain_q, gain_k, bias, out, lse, d_out)


B_W = Q_BLOCK + 2 * B_HALF_WINDOW
B_PAD_MAX = B_HALF_WINDOW * B_DILATIONS[-1]


def _attn_b_fwd(qkv, gain_q, gain_k, bias):
    def body(q_ref, k_ref, v_ref, gq_ref, gk_ref, bias_ref, o_ref, lse_ref, qn_ref, kp_ref, vp_ref):
        g = pl.program_id(1)
        _prep_q(q_ref, gq_ref, qn_ref)

        def first(rows, out, lse):
            o_ref[rows, :] = out
            lse_ref[rows, :] = lse

        def combine(rows, out, lse):
            old = lse_ref[rows, :]
            new = jnp.maximum(old, lse) + jnp.log(1.0 + jnp.exp(-jnp.abs(old - lse)))
            o_ref[rows, :] = o_ref[rows, :] * jnp.exp(old - new) + out * jnp.exp(lse - new)
            lse_ref[rows, :] = new

        for gi, d in enumerate(B_DILATIONS):
            @pl.when(g == gi)
            def _():
                _prep_kv(k_ref, v_ref, gk_ref, kp_ref, vp_ref, B_HALF_WINDOW * d)
                _fwd_tiles(qn_ref, kp_ref, vp_ref, bias_ref, first if gi == 0 else combine,
                           d=d, half_window=B_HALF_WINDOW)

    vec = pl.BlockSpec((1, LANES), lambda hp, g: (0, 0))
    padded = pltpu.VMEM((SEQ + 2 * B_PAD_MAX, LANES), F32)
    return pl.pallas_call(
        body, name="attn_b_fwd", grid=(4, 3),
        in_specs=[_seq_block(lambda hp, g: (0, QB_BLK + 4 * g + hp)), _seq_block(lambda hp, g: (0, KB_BLK + 4 * g + hp)),
                  _seq_block(lambda hp, g: (0, VB_BLK + 4 * g + hp)), vec, vec,
                  pl.BlockSpec((None, 2, Q_BLOCK, B_W), lambda hp, g: (4 * g + hp, 0, 0, 0))],
        out_specs=[_seq_block(lambda hp, g: (0, hp)), _seq_block(lambda hp, g: (0, hp))],
        out_shape=[jax.ShapeDtypeStruct((SEQ, 512), F32)] * 2,
        scratch_shapes=[pltpu.VMEM((SEQ, LANES), F32), padded, padded],
        compiler_params=_params("arbitrary", "arbitrary"),
    )(qkv, qkv, qkv, gain_q, gain_k, bias)


def _attn_b_bwd(qkv, gain_q, gain_k, bias, out, lse, d_out):
    def body(q_ref, k_ref, v_ref, gq_ref, gk_ref, bias_ref, o_ref, lse_ref, do_ref,
             dq_out, dk_out, dv_out, dgq_out, dgk_out, ds_out,
             qn_ref, kp_ref, vp_ref, delta_ref, dq_ref, dk_ref, dv_ref):
        g = pl.program_id(1)
        _prep_q(q_ref, gq_ref, qn_ref)
        _prep_delta(do_ref, o_ref, delta_ref)
        dk_ref[...] = jnp.zeros_like(dk_ref)
        dv_ref[...] = jnp.zeros_like(dv_ref)
        ds_out[...] = jnp.zeros_like(ds_out)
        for gi, d in enumerate(B_DILATIONS):
            @pl.when(g == gi)
            def _():
                pad = B_HALF_WINDOW * d
                _prep_kv(k_ref, v_ref, gk_ref, kp_ref, vp_ref, pad)
                _bwd_tiles(qn_ref, kp_ref, vp_ref, bias_ref, do_ref, lse_ref, delta_ref, dq_ref, dk_ref, dv_ref,
                           ds_out, d=d, half_window=B_HALF_WINDOW)
                dgk_out[...] = _rows8(_norm_bwd(k_ref, gk_ref, dk_ref, pad, dk_out, 1.0))
                dv_out[...] = dv_ref[pl.ds(pad, SEQ), :].astype(BF16)
        dgq_out[...] = _rows8(_norm_bwd(q_ref, gq_ref, dq_ref, 0, dq_out, SCALE))

    vec = pl.BlockSpec((1, LANES), lambda hp, g: (0, 0))
    seq_f32 = pltpu.VMEM((SEQ, LANES), F32)
    padded = pltpu.VMEM((SEQ + 2 * B_PAD_MAX, LANES), F32)
    part = pl.BlockSpec((None, 8, LANES), lambda hp, g: (4 * g + hp, 0, 0))
    return pl.pallas_call(
        body, name="attn_b_bwd", grid=(4, 3),
        in_specs=[_seq_block(lambda hp, g: (0, QB_BLK + 4 * g + hp)), _seq_block(lambda hp, g: (0, KB_BLK + 4 * g + hp)),
                  _seq_block(lambda hp, g: (0, VB_BLK + 4 * g + hp)), vec, vec,
                  pl.BlockSpec((None, 2, Q_BLOCK, B_W), lambda hp, g: (4 * g + hp, 0, 0, 0)),
                  _seq_block(lambda hp, g: (0, hp)), _seq_block(lambda hp, g: (0, hp)), _seq_block(lambda hp, g: (0, hp))],
        out_specs=[_seq_block(lambda hp, g: (0, 4 * g + hp))] * 3 + [
            part, part, pl.BlockSpec((None, 2, Q_BLOCK, B_W), lambda hp, g: (4 * g + hp, 0, 0, 0))],
        out_shape=[jax.ShapeDtypeStruct((SEQ, 1536), BF16)] * 3
        + [jax.ShapeDtypeStruct((12, 8, LANES), F32)] * 2 + [jax.ShapeDtypeStruct((12, 2, Q_BLOCK, B_W), F32)],
        scratch_shapes=[seq_f32, padded, padded, seq_f32, seq_f32, padded, padded],
        compiler_params=_params("arbitrary", "arbitrary"),
    )(qkv, qkv, qkv, gain_q, gain_k, bias, out, lse, d_out)


def _sigmoid(t):
    return 1.0 / (1.0 + jnp.exp(-t))


def _middle(out_a, out_b, gates, x, target, w_a, w_b, w_out, b_merge):
    tm = 256
    n_steps = SEQ // tm

    def body(oa_ref, ob_ref, g_ref, x_ref, t_ref, wa_ref, wb_ref, wo_ref, bm_ref,
             dy_ref, dg_ref, doa_ref, dob_ref, dwa_ref, dwb_ref, dwo_ref, dbm_ref, sq_ref):
        @pl.when(pl.program_id(0) == 0)
        def _():
            for ref in (dwa_ref, dwb_ref, dwo_ref, dbm_ref, sq_ref):
                ref[...] = jnp.zeros_like(ref)

        gate_a, gate_b = g_ref[:, 0:512], g_ref[:, 512:1024]
        sig_a, sig_b = _sigmoid(gate_a), _sigmoid(gate_b)
        silu_a, silu_b = gate_a * sig_a, gate_b * sig_b
        oa, ob = oa_ref[...], ob_ref[...]
        ya, yb = (oa * silu_a).astype(BF16), (ob * silu_b).astype(BF16)
        br_a, br_b = _dot(ya, wa_ref[...]), _dot(yb, wb_ref[...])
        m0 = _sigmoid(g_ref[:, 1024:2048] + bm_ref[0:1, :])
        m1 = _sigmoid(g_ref[:, 2048:3072] + bm_ref[1:2, :])
        merged = (m0 * br_a + m1 * br_b).astype(BF16)
        err = (x_ref[...] + _dot(merged, wo_ref[...])) - t_ref[...]
        sq_ref[...] += jnp.sum(err * err, axis=0, keepdims=True)

        dy = err * (1.0 / D_MODEL)
        dy_ref[...] = dy
        dyb = dy.astype(BF16)
        dmerged = _dot(dyb, wo_ref[...], NT)
        dwo_ref[...] += _dot(merged, dyb, TN)
        dbr_a, dbr_b = (dmerged * m0).astype(BF16), (dmerged * m1).astype(BF16)
        dm0 = (dmerged * br_a) * (m0 * (1.0 - m0))
        dm1 = (dmerged * br_b) * (m1 * (1.0 - m1))
        dbm_ref[0:1, :] += jnp.sum(dm0, axis=0, keepdims=True)
        dbm_ref[1:2, :] += jnp.sum(dm1, axis=0, keepdims=True)
        for s in range(N_CHIPS):
            cols = slice(256 * s, 256 * (s + 1))
            dwa_ref[s] += _dot(ya, dbr_a[:, cols], TN)
            dwb_ref[s] += _dot(yb, dbr_b[:, cols], TN)
        dya, dyb_ = _dot(dbr_a, wa_ref[...], NT), _dot(dbr_b, wb_ref[...], NT)
        doa_ref[...] = dya * silu_a
        dob_ref[...] = dyb_ * silu_b
        dg_ref[:, 0:512] = ((dya * oa) * (sig_a * (1.0 + gate_a * (1.0 - sig_a)))).astype(BF16)
        dg_ref[:, 512:1024] = ((dyb_ * ob) * (sig_b * (1.0 + gate_b * (1.0 - sig_b)))).astype(BF16)
        dg_ref[:, 1024:2048] = dm0.astype(BF16)
        dg_ref[:, 2048:3072] = dm1.astype(BF16)

    def rows(width):
        return pl.BlockSpec((tm, width), lambda i: (i, 0))

    def whole(*shape):
        return pl.BlockSpec(shape, lambda i: (0,) * len(shape))

    return pl.pallas_call(
        body, name="middle", grid=(n_steps,),
        in_specs=[rows(512), rows(512), rows(GATE_WIDTH), rows(D_MODEL), rows(D_MODEL),
                  whole(512, D_MODEL), whole(512, D_MODEL), whole(D_MODEL, D_MODEL), whole(2, D_MODEL)],
        out_specs=[rows(D_MODEL), rows(GATE_WIDTH), rows(512), rows(512),
                   whole(N_CHIPS, 512, 256), whole(N_CHIPS, 512, 256), whole(D_MODEL, D_MODEL),
                   whole(2, D_MODEL), whole(1, D_MODEL)],
        out_shape=[jax.ShapeDtypeStruct((SEQ, D_MODEL), F32), jax.ShapeDtypeStruct((SEQ, GATE_WIDTH), BF16),
                   jax.ShapeDtypeStruct((SEQ, 512), F32), jax.ShapeDtypeStruct((SEQ, 512), F32),
                   jax.ShapeDtypeStruct((N_CHIPS, 512, 256), F32), jax.ShapeDtypeStruct((N_CHIPS, 512, 256), F32),
                   jax.ShapeDtypeStruct((D_MODEL, D_MODEL), F32), jax.ShapeDtypeStruct((2, D_MODEL), F32),
                   jax.ShapeDtypeStruct((1, D_MODEL), F32)],
        compiler_params=_params("arbitrary"),
    )(out_a, out_b, gates, x, target, w_a, w_b, w_out, b_merge)


def _which(j, edges, fns):
    lo = 0
    for hi, fn in zip(edges, fns):
        pl.when((j >= lo) & (j < hi))(fn)
        lo = hi


def _d_w_in(d_proj, h):
    tn = 256
    edges = tuple(np.cumsum([p.shape[1] // tn for p in d_proj]))

    def body(*refs):
        pieces, h_ref, o_ref = refs[:-2], refs[-2], refs[-1]

        def emit(ref):
            def fn():
                o_ref[...] = _dot(ref[...], h_ref[...], TN)
            return fn

        _which(pl.program_id(0), edges, [emit(ref) for ref in pieces])

    def cols(lo, hi):
        return pl.BlockSpec((SEQ, tn), lambda j: (0, jnp.clip(j - lo, 0, hi - lo - 1)))

    return pl.pallas_call(
        body, name="d_w_in", grid=(int(edges[-1]),),
        in_specs=[cols(int(lo), int(hi)) for lo, hi in zip((0,) + edges[:-1], edges)]
        + [pl.BlockSpec((SEQ, D_MODEL), lambda j: (0, 0))],
        out_specs=pl.BlockSpec((tn, D_MODEL), lambda j: (j, 0)),
        out_shape=jax.ShapeDtypeStruct((IN_WIDTH, D_MODEL), F32),
        compiler_params=_params("arbitrary"),
    )(*d_proj, h)


def _d_x(d_proj, w_t, x, gain, dy, chip_sums):
    tm = 256
    n_steps = SEQ // tm
    split = d_proj[0].shape[1]
    assert split + d_proj[1].shape[1] == W_BLOCK and all(p.shape[1] % W_BLOCK == 0 for p in d_proj[2:])
    n_w = IN_WIDTH // W_BLOCK
    n_p, n_s = len(d_proj), len(chip_sums)

    def body(*refs):
        pieces, w_refs = refs[:n_p], refs[n_p:n_p + n_w]
        x_ref, g_ref, dy_ref = refs[n_p + n_w:n_p + n_w + 3]
        q_refs = refs[n_p + n_w + 3:n_p + n_w + 3 + n_s]
        dx_ref, dgain_ref = refs[n_p + n_w + 3 + n_s:n_p + n_w + 5 + n_s]
        o_refs = refs[n_p + n_w + 5 + n_s:n_p + n_w + 5 + 2 * n_s]
        send_sems, recv_sems = refs[n_p + n_w + 5 + 2 * n_s:] if n_s else (None, None)

        @pl.when(pl.program_id(0) == 0)
        def _():
            dgain_ref[...] = jnp.zeros_like(dgain_ref)
            if n_s:
                for cp in _scatter_copies(q_refs, o_refs, send_sems, recv_sems):
                    cp.start()

        dh = _dot(pieces[0][...], w_refs[0][0:split, :]) + _dot(pieces[1][...], w_refs[0][split:W_BLOCK, :])
        blk = 1
        for piece in pieces[2:]:
            for k in range(piece.shape[1] // W_BLOCK):
                dh = dh + _dot(piece[:, k * W_BLOCK:(k + 1) * W_BLOCK], w_refs[blk][...])
                blk += 1
        xf = x_ref[...]
        r = lax.rsqrt(jnp.mean(xf * xf, axis=-1, keepdims=True) + EPS)
        xh = xf * r
        dxh = dh * g_ref[...]
        dx_ref[...] = r * (dxh - xh * jnp.mean(dxh * xh, axis=-1, keepdims=True)) + dy_ref[...]
        dgain_ref[...] += _rows8(jnp.sum(dh * xh, axis=0, keepdims=True))

        if n_s:
            @pl.when(pl.program_id(0) == n_steps - 1)
            def _():
                for cp in _scatter_copies(q_refs, o_refs, send_sems, recv_sems):
                    cp.wait()

    row = pl.BlockSpec((tm, D_MODEL), lambda i: (i, 0))
    res = pl.pallas_call(
        body, name="d_x", grid=(n_steps,),
        in_specs=[pl.BlockSpec((tm, p.shape[1]), lambda i: (i, 0)) for p in d_proj] + _w_blocks(0, n_w)
        + [row, pl.BlockSpec((1, D_MODEL), lambda i: (0, 0)), row] + [ANY] * n_s,
        out_specs=[row, pl.BlockSpec((8, D_MODEL), lambda i: (0, 0))] + [ANY] * n_s,
        out_shape=[jax.ShapeDtypeStruct((SEQ, D_MODEL), F32), jax.ShapeDtypeStruct((8, D_MODEL), F32)]
        + [jax.ShapeDtypeStruct((3,) + q.shape[1:], BF16) for q in chip_sums],
        scratch_shapes=[pltpu.SemaphoreType.DMA((3 * n_s,)), pltpu.SemaphoreType.DMA((3 * n_s,))] if n_s else [],
        compiler_params=_params("arbitrary"),
    )(*d_proj, *([w_t] * n_w), x, gain, dy, *chip_sums)
    return res[0], res[1], res[2:]


def _my_place():
    x, y, c = lax.axis_index("x"), lax.axis_index("y"), lax.axis_index("c")
    return jnp.stack([2 * x + y, c]).astype(jnp.int32)


def _half_rows(ref, half):
    rows = ref.shape[-2] // 2
    idx = (slice(None),) * (len(ref.shape) - 2) + (pl.ds(pl.multiple_of(half * rows, 16), rows), slice(None))
    return ref.at[idx]


def _swap_halves(grads):
    n = len(grads)

    def body(*refs):
        g_refs, o_refs, (send_sems, recv_sems) = refs[:n], refs[n:2 * n], refs[2 * n:]
        x, y, c = lax.axis_index("x"), lax.axis_index("y"), lax.axis_index("c")
        copies = [pltpu.make_async_remote_copy(src_ref=_half_rows(g, 1 - c), dst_ref=o, send_sem=send_sems.at[k],
                                               recv_sem=recv_sems.at[k], device_id=(x, y, 1 - c), device_id_type=MESH)
                  for k, (g, o) in enumerate(zip(g_refs, o_refs))]
        for cp in copies:
            cp.start()
        for cp in copies:
            cp.wait()

    return pl.pallas_call(
        body, name="reduce_swap_halves", in_specs=[ANY] * n, out_specs=[ANY] * n,
        out_shape=[jax.ShapeDtypeStruct((N_CHIPS, g.shape[1] // 2, D_MODEL), F32) for g in grads],
        scratch_shapes=[pltpu.SemaphoreType.DMA((n,)), pltpu.SemaphoreType.DMA((n,))],
    )(*grads)


def _row_tile(rows):
    return max(t for t in range(16, 385, 16) if rows % t == 0)


def _add_halves(place, grads, theirs, name):
    half = theirs.shape[1]
    tr = _row_tile(half)
    n = half // tr

    def body(place_ref, g_ref, t_ref, o_ref):
        o_ref[...] = (g_ref[...] + t_ref[...]).astype(BF16)

    return pl.pallas_call(
        body, name=name,
        grid_spec=pltpu.PrefetchScalarGridSpec(
            num_scalar_prefetch=1, grid=(N_CHIPS, n),
            in_specs=[pl.BlockSpec((None, tr, D_MODEL), lambda s, i, p: (s, p[1] * n + i, 0)),
                      pl.BlockSpec((None, tr, D_MODEL), lambda s, i, p: (s, i, 0))],
            out_specs=pl.BlockSpec((None, tr, D_MODEL), lambda s, i, p: (s, i, 0))),
        out_shape=jax.ShapeDtypeStruct((N_CHIPS, half, D_MODEL), BF16),
        compiler_params=_params("arbitrary", "arbitrary"),
    )(place, grads, theirs)


def _scatter_copies(q_refs, o_refs, send_sems, recv_sems):
    x, y, c = lax.axis_index("x"), lax.axis_index("y"), lax.axis_index("c")
    chips = [(1 - x, y), (x, 1 - y), (1 - x, 1 - y)]
    return [pltpu.make_async_remote_copy(src_ref=q.at[2 * cx + cy], dst_ref=o.at[j],
                                         send_sem=send_sems.at[3 * k + j], recv_sem=recv_sems.at[3 * k + j],
                                         device_id=(cx, cy, c), device_id_type=MESH)
            for k, (q, o) in enumerate(zip(q_refs, o_refs)) for j, (cx, cy) in enumerate(chips)]


def _add_chips(place, chip_sums, others, name):
    half = others.shape[1]
    tr = _row_tile(half)
    n = half // tr

    def body(place_ref, q_ref, o_ref, r_ref):
        acc = q_ref[...].astype(F32)
        for j in range(3):
            acc = acc + o_ref[j].astype(F32)
        r_ref[...] = acc

    return pl.pallas_call(
        body, name=name,
        grid_spec=pltpu.PrefetchScalarGridSpec(
            num_scalar_prefetch=1, grid=(n,),
            in_specs=[pl.BlockSpec((None, tr, D_MODEL), lambda i, p: (p[0], i, 0)),
                      pl.BlockSpec((3, tr, D_MODEL), lambda i, p: (0, i, 0))],
            out_specs=pl.BlockSpec((tr, D_MODEL), lambda i, p: (p[1] * n + i, 0))),
        out_shape=jax.ShapeDtypeStruct((2 * half, D_MODEL), F32),
        compiler_params=_params("arbitrary"),
    )(place, chip_sums, others)


def _join_halves(shards):
    n = len(shards)

    def body(*refs):
        o_refs, (send_sems, recv_sems) = refs[n:2 * n], refs[2 * n:]
        x, y, c = lax.axis_index("x"), lax.axis_index("y"), lax.axis_index("c")

        def copy(k, rows):
            return pltpu.make_async_remote_copy(src_ref=rows, dst_ref=rows, send_sem=send_sems.at[k],
                                                recv_sem=recv_sems.at[k], device_id=(x, y, 1 - c), device_id_type=MESH)

        sends = [copy(k, _half_rows(o, c)) for k, o in enumerate(o_refs)]
        for cp in sends:
            cp.start()
        for k, o in enumerate(o_refs):
            copy(k, _half_rows(o, 1 - c)).wait_recv()
        for cp in sends:
            cp.wait_send()

    return pl.pallas_call(
        body, name="reduce_join_halves", in_specs=[ANY] * n, out_specs=[ANY] * n,
        out_shape=[jax.ShapeDtypeStruct(s.shape, F32) for s in shards],
        input_output_aliases={k: k for k in range(n)},
        scratch_shapes=[pltpu.SemaphoreType.DMA((n,)), pltpu.SemaphoreType.DMA((n,))],
    )(*shards)


def _gather_small(block):
    rows = block.shape[0]

    def body(b_ref, o_ref, send_sems, recv_sems, local_sem):
        x, y, c = lax.axis_index("x"), lax.axis_index("y"), lax.axis_index("c")
        me, sibling = (x, y, c), (x, y, 1 - c)
        chips = [(1 - x, y), (x, 1 - y), (1 - x, 1 - y)]

        def at(px, py, pc):
            return o_ref.at[pl.ds(pl.multiple_of((4 * px + 2 * py + pc) * rows, 8), rows), :]

        def copy(k, block_of, to, src=None):
            return pltpu.make_async_remote_copy(src_ref=at(*block_of) if src is None else src, dst_ref=at(*block_of),
                                                send_sem=send_sems.at[k], recv_sem=recv_sems.at[k],
                                                device_id=to, device_id_type=MESH)

        mine = pltpu.make_async_copy(b_ref, at(*me), local_sem)
        mine.start()
        first = [copy(0, me, sibling, src=b_ref)]
        first += [copy(1 + j, me, (*chip, c), src=b_ref) for j, chip in enumerate(chips)]
        for cp in first:
            cp.start()
        passed = [copy(4 + j, (*chip, c), sibling) for j, chip in enumerate(chips)]
        for j, chip in enumerate(chips):
            copy(1 + j, (*chip, c), me).wait_recv()
            passed[j].start()
        copy(0, sibling, me).wait_recv()
        for j, chip in enumerate(chips):
            copy(4 + j, (*chip, 1 - c), me).wait_recv()
        for cp in first + passed:
            cp.wait_send()
        mine.wait()

    return pl.pallas_call(
        body, name="gather_small_grads",
        in_specs=[pl.BlockSpec(memory_space=pltpu.VMEM)], out_specs=pl.BlockSpec(memory_space=pltpu.VMEM),
        out_shape=jax.ShapeDtypeStruct((8 * rows, D_MODEL), F32),
        scratch_shapes=[pltpu.SemaphoreType.DMA((7,)), pltpu.SemaphoreType.DMA((7,)), pltpu.SemaphoreType.DMA],
    )(block)


def _sum_devices(blocks):
    def body(b_ref, o_ref):
        acc = b_ref[0:8, :]
        for dev in range(1, 8):
            acc = acc + b_ref[8 * dev:8 * dev + 8, :]
        o_ref[...] = acc

    return pl.pallas_call(body, name="sum_small_grads", out_shape=jax.ShapeDtypeStruct((8, D_MODEL), F32))(blocks)


def _adamw_math(w, g, m, v):
    m = ADAM_B1 * m + (1.0 - ADAM_B1) * g
    v = ADAM_B2 * v + (1.0 - ADAM_B2) * (g * g)
    m_hat = m / (1.0 - ADAM_B1 ** ADAM_STEP)
    v_hat = v / (1.0 - ADAM_B2 ** ADAM_STEP)
    return -ADAM_LR * (m_hat / (jnp.sqrt(v_hat) + ADAM_EPS) + ADAM_WD * w), m, v


def _adamw(w, g, m, v, name):
    r, c = w.shape
    tr = 128 if r % 128 == 0 else r

    def body(w_ref, g_ref, m_ref, v_ref, d_ref, nm_ref, nv_ref):
        d_ref[...], nm_ref[...], nv_ref[...] = _adamw_math(w_ref[...], g_ref[...], m_ref[...], v_ref[...])

    spec = pl.BlockSpec((tr, c), lambda i: (i, 0))
    return pl.pallas_call(
        body, name=name, grid=(r // tr,), in_specs=[spec] * 4, out_specs=[spec] * 3,
        out_shape=[jax.ShapeDtypeStruct((r, c), F32)] * 3, compiler_params=_params("arbitrary"),
    )(w, g, m, v)


def _adamw_small(ws, gs, ms, vs):
    n = len(ws)

    def body(*refs):
        ins, outs = refs[:4 * n], refs[4 * n:]
        for k in range(n):
            d, m, v = _adamw_math(ins[k][...], ins[n + k][...], ins[2 * n + k][...], ins[3 * n + k][...])
            outs[k][...], outs[n + k][...], outs[2 * n + k][...] = d, m, v

    shapes = [jax.ShapeDtypeStruct(w.shape, F32) for w in ws]
    res = pl.pallas_call(body, name="adamw_small", out_shape=shapes * 3)(*ws, *gs, *ms, *vs)
    return res[:n], res[n:2 * n], res[2 * n:]


def _fold_heads(partials):
    t = jnp.sum(partials[:, 0, :], axis=0)
    return (t[:HEAD_DIM] + t[HEAD_DIM:]).reshape(1, HEAD_DIM)


def _local_step(x, target, norm_gain, w_t, w_a, w_b, w_o, b_m, q_norm_a, k_norm_a, q_norm_b, k_norm_b, sink_a,
                rel_bias, start_reduce=None):
    two = lambda gain: jnp.concatenate([gain, gain], axis=1)
    bias_a = _bias_table(rel_bias[:, :8], A_HALF_WINDOW, 1)
    bias_b = jnp.concatenate([_bias_table(rel_bias[:, 8 + 8 * g:16 + 8 * g], B_HALF_WINDOW, d)
                              for g, d in enumerate(B_DILATIONS)], axis=0)

    qkv, h = _in_proj(x, norm_gain, w_t, 0, QKV_WIDTH // W_BLOCK, BF16, "in_proj_qkv")
    gates, _ = _in_proj(x, norm_gain, w_t, QKV_WIDTH // W_BLOCK, GATE_WIDTH // W_BLOCK, F32, "in_proj_gates")
    out_a, lse_a = _attn_a_fwd(qkv, two(q_norm_a), two(k_norm_a), bias_a, sink_a)
    out_b, lse_b = _attn_b_fwd(qkv, two(q_norm_b), two(k_norm_b), bias_b)

    dy, dgates, d_out_a, d_out_b, d_wa, d_wb, d_wo, d_bm, sq = _middle(
        out_a, out_b, gates, x, target, w_a, w_b, w_o, b_m)
    loss = (0.5 / D_MODEL) * jnp.sum(sq)

    dq_a, dkv_a, dgq_a, dgk_a, ds_a, dsink = _attn_a_bwd(
        qkv, two(q_norm_a), two(k_norm_a), bias_a, sink_a, out_a, lse_a, d_out_a)
    dq_b, dk_b, dv_b, dgq_b, dgk_b, ds_b = _attn_b_bwd(
        qkv, two(q_norm_b), two(k_norm_b), bias_b, out_b, lse_b, d_out_b)
    d_proj = (dq_a, dkv_a, dq_b, dk_b, dv_b, dgates)

    d_bm_rows = jnp.pad(d_bm.reshape(2, N_CHIPS, 256).transpose(1, 0, 2),
                        ((0, 0), (0, REST_ROWS - 514), (0, D_MODEL - 256)))
    rest = jnp.concatenate([d_wo.reshape(N_CHIPS, 256, D_MODEL), d_wa.reshape(N_CHIPS, 128, D_MODEL),
                            d_wb.reshape(N_CHIPS, 128, D_MODEL), d_bm_rows], axis=1)
    grads = [_d_w_in(d_proj, h).reshape(N_CHIPS, W_IN_SHARD, D_MODEL), rest]
    chip_sums = start_reduce(grads) if start_reduce is not None else []
    grad_x, d_gain, others = _d_x(d_proj, w_t, x, norm_gain, dy, chip_sums)

    d_rel = jnp.concatenate(
        [_bias_grad(ds_a, A_HALF_WINDOW, 1)]
        + [_bias_grad(ds_b[4 * g:4 * g + 4], B_HALF_WINDOW, d) for g, d in enumerate(B_DILATIONS)], axis=1)
    d_sink = jnp.sum(dsink, axis=(2, 3)).reshape(1, 8)
    dgk_a_row = dgk_a[0]
    small = jnp.zeros((8, D_MODEL), F32)
    small = small.at[0].set(d_gain[0])
    small = small.at[1].set(d_rel.reshape(-1))
    misc = jnp.concatenate([_fold_heads(dgq_a), (dgk_a_row[:HEAD_DIM] + dgk_a_row[HEAD_DIM:]).reshape(1, HEAD_DIM),
                            _fold_heads(dgq_b), _fold_heads(dgk_b), d_sink], axis=1)
    small = small.at[2, :264].set(misc[0])

    return loss, grad_x, grads, small, chip_sums, others


def _unpack_weights(w_t_all, small_all):
    sm = small_all.reshape(N_CHIPS, SMALL_ROWS, D_MODEL)
    w_o = sm[:, 0:256].reshape(D_MODEL, D_MODEL)
    w_a = sm[:, 256:384].reshape(N_CHIPS, 512, 256).transpose(1, 0, 2).reshape(512, D_MODEL)
    w_b = sm[:, 384:512].reshape(N_CHIPS, 512, 256).transpose(1, 0, 2).reshape(512, D_MODEL)
    b_m = lax.bitcast_convert_type(sm[:, 512].reshape(N_CHIPS, 2, 256, 2), F32)
    return w_t_all, w_a, w_b, w_o, b_m.transpose(1, 0, 2).reshape(2, D_MODEL)


def _pack_small_weights(w_branch_a, w_branch_b, b_merge, w_out):
    b_m = jnp.pad(lax.bitcast_convert_type(b_merge, BF16).reshape(1, D_MODEL), ((0, SMALL_ROWS - 513), (0, 0)))
    return jnp.concatenate([w_out.astype(BF16), w_branch_a.astype(BF16).reshape(128, D_MODEL),
                            w_branch_b.astype(BF16).reshape(128, D_MODEL), b_m], axis=0)


def kernel(x, norm_gain, w_in, q_norm_a, k_norm_a, q_norm_b, k_norm_b, sink_a, rel_bias, w_branch_a, w_branch_b, b_merge, w_out, loss_target, m_norm_gain, m_w_in, m_q_norm_a, m_k_norm_a, m_q_norm_b, m_k_norm_b, m_sink_a, m_rel_bias, m_w_branch_a, m_w_branch_b, m_b_merge, m_w_out, v_norm_gain, v_w_in, v_q_norm_a, v_k_norm_a, v_q_norm_b, v_k_norm_b, v_sink_a, v_rel_bias, v_w_branch_a, v_w_branch_b, v_b_merge, v_w_out):
    w_in, w_branch_a, w_branch_b, b_merge, w_out = w_in[0], w_branch_a[0], w_branch_b[0], b_merge[0], w_out[0]

    wt_shard = _transpose_cast(w_in, BF16, "w_in_transpose")
    w_t, w_a, w_b, w_o, b_m = _unpack_weights(
        *_gather_weights(wt_shard, _pack_small_weights(w_branch_a, w_branch_b, b_merge, w_out)))

    place = _my_place()
    names = ("w_in", "rest")

    def start_reduce(grads):
        return [_add_halves(place, g, t, "reduce_add_halves_" + n) for g, t, n in zip(grads, _swap_halves(grads), names)]

    loss_part, grad_x, _, small, chip_sums, others = _local_step(
        x[0], loss_target[0], norm_gain, w_t, w_a, w_b, w_o, b_m, q_norm_a, k_norm_a, q_norm_b, k_norm_b,
        sink_a, rel_bias, start_reduce)
    loss = lax.psum(loss_part, ("x", "y", "c"))

    g_wt, g_rest = _join_halves([_add_chips(place, q, o, "reduce_add_chips_" + n)
                                 for q, o, n in zip(chip_sums, others, names)])
    small = _sum_devices(_gather_small(small))

    g_w_in = _transpose_cast(g_wt, F32, "grad_w_in_transpose")
    g_w_out = g_rest[0:256]
    g_w_a = g_rest[256:384].reshape(512, 256)
    g_w_b = g_rest[384:512].reshape(512, 256)
    g_b_merge = g_rest[512:514, :256]
    g_norm_gain = small[0:1]
    g_rel_bias = small[1].reshape(N_BUCKETS, N_BUCKETS)
    g_q_a, g_k_a, g_q_b, g_k_b = (small[2:3, 64 * k:64 * k + 64] for k in range(4))
    g_sink = small[2:3, 256:264]

    big_names = (("w_in", w_in, g_w_in, m_w_in[0], v_w_in[0]),
                 ("w_branch_a", w_branch_a, g_w_a, m_w_branch_a[0], v_w_branch_a[0]),
                 ("w_branch_b", w_branch_b, g_w_b, m_w_branch_b[0], v_w_branch_b[0]),
                 ("w_out", w_out, g_w_out, m_w_out[0], v_w_out[0]))
    upd = {name: (g,) + tuple(_adamw(w, g, m, v, "adamw_" + name)) for name, w, g, m, v in big_names}
    small_names = ("norm_gain", "q_norm_a", "k_norm_a", "q_norm_b", "k_norm_b", "sink_a", "rel_bias", "b_merge")
    ws = [norm_gain, q_norm_a, k_norm_a, q_norm_b, k_norm_b, sink_a, rel_bias, b_merge]
    gs = [g_norm_gain, g_q_a, g_k_a, g_q_b, g_k_b, g_sink, g_rel_bias, g_b_merge]
    ms = [m_norm_gain, m_q_norm_a, m_k_norm_a, m_q_norm_b, m_k_norm_b, m_sink_a, m_rel_bias, m_b_merge[0]]
    vs = [v_norm_gain, v_q_norm_a, v_k_norm_a, v_q_norm_b, v_k_norm_b, v_sink_a, v_rel_bias, v_b_merge[0]]
    ds, nms, nvs = _adamw_small(ws, gs, ms, vs)
    for k, name in enumerate(small_names):
        upd[name] = (gs[k], ds[k], nms[k], nvs[k])

    order = ("norm_gain", "w_in", "q_norm_a", "k_norm_a", "q_norm_b", "k_norm_b", "sink_a", "rel_bias",
             "w_branch_a", "w_branch_b", "b_merge", "w_out")
    lead = {"w_in", "w_branch_a", "w_branch_b", "b_merge", "w_out"}
    outs = [loss, grad_x[None]]
    for part in range(4):
        outs += [upd[name][part][None] if name in lead else upd[name][part] for name in order]
    return tuple(outs)
```

```python
import math

import numpy as np
import jax
import jax.numpy as jnp
from jax import lax
from jax.experimental import pallas as pl
from jax.experimental.pallas import tpu as pltpu

F32 = jnp.float32
BF16 = jnp.bfloat16

SEQ = 4096
D_MODEL = 1024
HEAD_DIM = 64
LANES = 128
EPS = 1e-6
NEG_INF = -1e30
SCALE = HEAD_DIM ** -0.5
N_BUCKETS = 32
MAX_DISTANCE = 1024
N_CHIPS = 4

A_HALF_WINDOW = 128
B_HALF_WINDOW = 64
B_DILATIONS = (1, 4, 16)
Q_BLOCK = 128

QKV_WIDTH = 5376
GATE_WIDTH = 3072
QA_BLK, KA_BLK, VA_BLK = 0, 4, 5
QB_BLK, KB_BLK, VB_BLK = 6, 18, 30
IN_WIDTH = QKV_WIDTH + GATE_WIDTH
W_IN_SHARD = IN_WIDTH // N_CHIPS

SMALL_ROWS = 544
REST_ROWS = 544

ADAM_LR = 0.001
ADAM_B1 = 0.9
ADAM_B2 = 0.999
ADAM_EPS = 1e-08
ADAM_WD = 0.01
ADAM_STEP = 10

VMEM_LIMIT = 56 * 1024 * 1024

NT = (((1,), (1,)), ((), ()))
TN = (((0,), (0,)), ((), ()))
MESH = pl.DeviceIdType.MESH
ANY = pl.BlockSpec(memory_space=pl.ANY)


def _dot(a, b, dims=None):
    if dims is None:
        return jnp.dot(a, b, preferred_element_type=F32)
    return lax.dot_general(a, b, dims, preferred_element_type=F32)


def _params(*semantics):
    return pltpu.CompilerParams(dimension_semantics=semantics or None, vmem_limit_bytes=VMEM_LIMIT)


def _bucket_onehot(half_window, stride):
    w = Q_BLOCK + 2 * half_window
    rel = (np.arange(w)[None, :] - half_window - np.arange(Q_BLOCK)[:, None])
    band = np.abs(rel) <= half_window
    rel = rel * stride
    half, max_exact = N_BUCKETS // 2, N_BUCKETS // 4
    n = np.abs(rel)
    nf = np.maximum(n, max_exact).astype(np.float32)
    large = max_exact + (np.log(nf / np.float32(max_exact)) / np.float32(math.log(MAX_DISTANCE / max_exact))
                         * np.float32(half - max_exact)).astype(np.int32)
    large = np.minimum(large, half - 1)
    bucket = (rel > 0).astype(np.int32) * half + np.where(n < max_exact, n, large)
    onehot = (bucket[..., None] == np.arange(N_BUCKETS)) & band[..., None]
    return onehot.reshape(Q_BLOCK * w, N_BUCKETS).astype(np.float32), band


def _bias_table(rel_bias_cols, half_window, stride):
    onehot, band = _bucket_onehot(half_window, stride)
    h = rel_bias_cols.shape[1]
    w = Q_BLOCK + 2 * half_window
    t = jnp.einsum("pb,bh->hp", jnp.asarray(onehot), rel_bias_cols, precision=lax.Precision.HIGHEST)
    t = t.reshape(h, Q_BLOCK, w) + jnp.asarray(np.where(band, 0.0, NEG_INF).astype(np.float32))
    return t.reshape(h // 2, 2, Q_BLOCK, w)


def _bias_grad(ds_sum, half_window, stride):
    onehot, _ = _bucket_onehot(half_window, stride)
    h = ds_sum.shape[0] * 2
    return jnp.einsum("pb,hp->bh", jnp.asarray(onehot), ds_sum.reshape(h, -1), precision=lax.Precision.HIGHEST)


def _transpose_cast(w, out_dtype, name):
    lead = (None,) * (w.ndim - 2)
    zero = (0,) * (w.ndim - 2)
    r, c = w.shape[-2:]

    def body(w_ref, o_ref):
        o_ref[...] = w_ref[...].T.astype(out_dtype)

    if r % LANES == 0:
        steps = pl.cdiv(c, LANES)
        in_spec = pl.BlockSpec(lead + (r, LANES), lambda j: zero + (0, j))
        out_spec = pl.BlockSpec((LANES, r), lambda j: (j, 0))
    else:
        steps = pl.cdiv(r, LANES)
        in_spec = pl.BlockSpec(lead + (LANES, c), lambda j: zero + (j, 0))
        out_spec = pl.BlockSpec((c, LANES), lambda j: (0, j))
    return pl.pallas_call(
        body, name=name, grid=(steps,), in_specs=[in_spec], out_specs=out_spec,
        out_shape=jax.ShapeDtypeStruct((c, r), out_dtype),
        compiler_params=_params("arbitrary"),
    )(w)


def _gather_weights(wt_shard, small_shard):
    bufs = ((W_IN_SHARD, IN_WIDTH), (SMALL_ROWS, N_CHIPS * SMALL_ROWS))

    stage_rows = 528

    def body(wt_in, sm_in, wt_out, sm_out, send_sems, recv_sems, in_sems, out_sems, stage):
        x, y, c = lax.axis_index("x"), lax.axis_index("y"), lax.axis_index("c")
        sibling = (x, y, 1 - c)
        chips = [(1 - x, y), (x, 1 - y), (1 - x, 1 - y)]
        my_chip = 2 * x + y
        refs = ((wt_in, wt_out), (sm_in, sm_out))

        def keep_own():
            pieces = [(b, r0) for b in range(2) for r0 in range(0, bufs[b][0], stage_rows)]
            outs = []
            for i, (b, r0) in enumerate(pieces):
                rows = min(stage_rows, bufs[b][0] - r0)
                slot = i % 2
                if i >= 2:
                    outs[i - 2].wait()
                buf = stage.at[slot, pl.ds(0, rows), :]
                load = pltpu.make_async_copy(refs[b][0].at[pl.ds(r0, rows), :], buf, in_sems.at[slot])
                load.start()
                load.wait()
                start = pl.multiple_of(my_chip * bufs[b][0] + r0, 16)
                outs.append(pltpu.make_async_copy(buf, refs[b][1].at[pl.ds(start, rows), :], out_sems.at[slot]))
                outs[i].start()
            for cp in outs[-2:]:
                cp.wait()

        def half_of(b, chip, half):
            rows = bufs[b][0]
            start = pl.multiple_of(chip * rows + half * (rows // 2), 16)
            return refs[b][1].at[pl.ds(start, rows // 2), :]

        def copy(k, src, dst, to):
            return pltpu.make_async_remote_copy(src_ref=src, dst_ref=dst, send_sem=send_sems.at[k],
                                                recv_sem=recv_sems.at[k], device_id=to, device_id_type=MESH)

        first, passed = [], []
        for b in range(2):
            rows = bufs[b][0]
            src = refs[b][0].at[pl.ds(pl.multiple_of(c * (rows // 2), 16), rows // 2), :]
            for j, chip in enumerate(chips):
                first.append(copy(3 * b + j, src, half_of(b, my_chip, c), (*chip, c)))
        for cp in first:
            cp.start()
        keep_own()
        for b in range(2):
            for j, (cx, cy) in enumerate(chips):
                landed = half_of(b, 2 * cx + cy, c)
                copy(3 * b + j, landed, landed, sibling).wait_recv()
                fwd = copy(6 + 3 * b + j, landed, landed, sibling)
                fwd.start()
                passed.append(fwd)
        for b in range(2):
            for j, (cx, cy) in enumerate(chips):
                other = half_of(b, 2 * cx + cy, 1 - c)
                copy(6 + 3 * b + j, other, other, sibling).wait_recv()
        for cp in first + passed:
            cp.wait_send()

    return pl.pallas_call(
        body, name="gather_weights",
        in_specs=[ANY, ANY], out_specs=[ANY, ANY],
        out_shape=[jax.ShapeDtypeStruct((bufs[0][1], D_MODEL), BF16),
                   jax.ShapeDtypeStruct((bufs[1][1], D_MODEL), BF16)],
        scratch_shapes=[pltpu.SemaphoreType.DMA((12,)), pltpu.SemaphoreType.DMA((12,)),
                        pltpu.SemaphoreType.DMA((2,)), pltpu.SemaphoreType.DMA((2,)),
                        pltpu.VMEM((2, stage_rows, D_MODEL), BF16)],
    )(wt_shard, small_shard)


W_BLOCK = 768


def _w_blocks(first, count):
    return [pl.BlockSpec((W_BLOCK, D_MODEL), lambda *_, k=k: (first + k, 0)) for k in range(count)]


def _in_proj(x, gain, w_t, first_block, n_blocks, out_dtype, name):
    tm = 256

    def body(x_ref, g_ref, *refs):
        w_refs, (o_ref, h_ref) = refs[:n_blocks], refs[n_blocks:]
        xf = x_ref[...]
        r = lax.rsqrt(jnp.mean(xf * xf, axis=-1, keepdims=True) + EPS)
        h = ((xf * r) * g_ref[...]).astype(BF16)
        h_ref[...] = h
        for k, w_ref in enumerate(w_refs):
            o_ref[:, k * W_BLOCK:(k + 1) * W_BLOCK] = _dot(h, w_ref[...], NT).astype(out_dtype)

    return pl.pallas_call(
        body, name=name, grid=(SEQ // tm,),
        in_specs=[pl.BlockSpec((tm, D_MODEL), lambda i: (i, 0)), pl.BlockSpec((1, D_MODEL), lambda i: (0, 0))]
        + _w_blocks(first_block, n_blocks),
        out_specs=[pl.BlockSpec((tm, W_BLOCK * n_blocks), lambda i: (i, 0)),
                   pl.BlockSpec((tm, D_MODEL), lambda i: (i, 0))],
        out_shape=[jax.ShapeDtypeStruct((SEQ, W_BLOCK * n_blocks), out_dtype),
                   jax.ShapeDtypeStruct((SEQ, D_MODEL), BF16)],
        compiler_params=_params("arbitrary"),
    )(x, gain, *([w_t] * n_blocks))


CHUNK = 256
CHUNK_UNROLL = 4
TILE_UNROLL = 4


def _low_half():
    return lax.broadcasted_iota(jnp.int32, (1, LANES), 1) < HEAD_DIM


def _half_sum(v, low):
    del low
    row = lax.broadcasted_iota(jnp.int32, (2 * LANES, LANES), 0)
    col = lax.broadcasted_iota(jnp.int32, (2 * LANES, LANES), 1)
    ones = jnp.where((row % LANES) // HEAD_DIM == col // HEAD_DIM, 1.0, 0.0).astype(BF16)
    hi = v.astype(BF16)
    lo = (v - hi.astype(F32)).astype(BF16)
    return _dot(jnp.concatenate([hi, lo], axis=1), ones)


def _chunks(fn, init=0):
    def body(i, carry):
        for u in range(CHUNK_UNROLL):
            carry = fn(pl.multiple_of((i * CHUNK_UNROLL + u) * CHUNK, CHUNK), carry)
        return carry

    return lax.fori_loop(0, SEQ // (CHUNK * CHUNK_UNROLL), body, init)


def _inv_rms(t, low):
    return lax.rsqrt(_half_sum(t * t, low) * (1.0 / HEAD_DIM) + EPS)


def _prep_q(q_ref, gain_ref, qn_ref):
    low = _low_half()

    def step(r0, carry):
        q = q_ref[pl.ds(r0, CHUNK), :].astype(F32)
        qn_ref[pl.ds(r0, CHUNK), :] = ((q * _inv_rms(q, low)) * gain_ref[...]) * SCALE
        return carry

    _chunks(step)


def _own_half(t, keep):
    return jnp.where(keep, t, pltpu.roll(t, HEAD_DIM, 1))


def _prep_kv(k_ref, v_ref, gain_ref, kp_ref, vp_ref, pad, keep=None):
    low = _low_half()
    zeros = jnp.zeros((pad, LANES), F32)
    for ref in (kp_ref, vp_ref):
        ref[pl.ds(0, pad), :] = zeros
        ref[pl.ds(pad + SEQ, pad), :] = zeros

    def step(r0, carry):
        k = k_ref[pl.ds(r0, CHUNK), :].astype(F32)
        v = v_ref[pl.ds(r0, CHUNK), :].astype(F32)
        kn = (k * _inv_rms(k, low)) * gain_ref[...]
        if keep is not None:
            kn, v = _own_half(kn, keep), _own_half(v, keep)
        kp_ref[pl.ds(pad + r0, CHUNK), :] = kn
        vp_ref[pl.ds(pad + r0, CHUNK), :] = v
        return carry

    _chunks(step)


def _tiles(d, half_window, fn):
    w = Q_BLOCK + 2 * half_window
    length = SEQ // d
    n_blocks = length // Q_BLOCK
    col = lax.broadcasted_iota(jnp.int32, (1, w), 1)

    def step(it, carry):
        c, n = it // n_blocks, it % n_blocks
        start = c + (d * Q_BLOCK) * n
        if d == 1:
            start = pl.multiple_of(start, Q_BLOCK)
            q_rows, k_rows = pl.ds(start, Q_BLOCK), pl.ds(start, w)
        else:
            q_rows, k_rows = pl.ds(start, Q_BLOCK, stride=d), pl.ds(start, w, stride=d)
        t = n * Q_BLOCK - half_window + col
        edge = jnp.where((t < 0) | (t >= length), NEG_INF, 0.0)
        fn(q_rows, k_rows, edge)
        return carry

    lax.fori_loop(0, d * n_blocks, step, 0, unroll=TILE_UNROLL)


def _stack_heads(t, low):
    return jnp.concatenate([jnp.where(low, t, 0.0), jnp.where(low, 0.0, t)], axis=0).astype(BF16)


def _unstack_heads(t, low):
    return jnp.where(low, t[:Q_BLOCK], t[Q_BLOCK:])


def _per_head(pair):
    return jnp.concatenate([jnp.full((Q_BLOCK, 1), pair[0], F32), jnp.full((Q_BLOCK, 1), pair[1], F32)], axis=0)


def _fwd_tiles(qn_ref, kp_ref, vp_ref, bias_ref, emit, *, d, half_window, sinks=None):
    low = _low_half()
    w = Q_BLOCK + 2 * half_window
    sink = None if sinks is None else _per_head(sinks)

    def tile(q_rows, k_rows, edge):
        q2 = _stack_heads(qn_ref[q_rows, :], low)
        k = kp_ref[k_rows, :].astype(BF16)
        v1 = jnp.concatenate([vp_ref[k_rows, :], jnp.ones((w, LANES), F32)], axis=1).astype(BF16)
        s = _dot(q2, k, NT) + bias_ref[...].reshape(2 * Q_BLOCK, w) + edge
        m = jnp.max(s, axis=-1, keepdims=True)
        if sink is not None:
            m = jnp.maximum(m, sink)
        o = _dot(jnp.exp(s - m).astype(BF16), v1)
        l = o[:, LANES:]
        if sink is not None:
            l = l + jnp.exp(sink - m)
        emit(q_rows, _unstack_heads(o[:, :LANES] * (1.0 / l), low), _unstack_heads(m + jnp.log(l), low))

    _tiles(d, half_window, tile)


def _bwd_tiles(qn_ref, kp_ref, vp_ref, bias_ref, do_ref, lse_ref, delta_ref, dq_ref, dk_ref, dv_ref, ds_ref,
               *, d, half_window, sinks=None, dsink_ref=None):
    low = _low_half()
    w = Q_BLOCK + 2 * half_window
    sink = None if sinks is None else _per_head(sinks)

    def rows_of(t):
        return jnp.concatenate([t[:, 0:1], t[:, HEAD_DIM:HEAD_DIM + 1]], axis=0)

    def tile(q_rows, k_rows, edge):
        q2 = _stack_heads(qn_ref[q_rows, :], low)
        do2 = _stack_heads(do_ref[q_rows, :], low)
        k = kp_ref[k_rows, :].astype(BF16)
        v = vp_ref[k_rows, :].astype(BF16)
        lse = rows_of(lse_ref[q_rows, :])
        delta = rows_of(delta_ref[q_rows, :])
        p = jnp.exp(_dot(q2, k, NT) + bias_ref[...].reshape(2 * Q_BLOCK, w) + edge - lse)
        ds = p * (_dot(do2, v, NT) - delta)
        ds_ref[...] += ds.reshape(2, Q_BLOCK, w)
        if sink is not None:
            dsink_ref[...] += (-jnp.exp(sink - lse) * delta).reshape(2, Q_BLOCK, 1)
        dsb, pb = ds.astype(BF16), p.astype(BF16)
        dq_ref[q_rows, :] = _unstack_heads(_dot(dsb, k), low)
        dk_ref[k_rows, :] += _dot(dsb, q2, TN)
        dv_ref[k_rows, :] += _dot(pb, do2, TN)

    _tiles(d, half_window, tile)


def _prep_delta(do_ref, o_ref, delta_ref):
    low = _low_half()

    def step(r0, carry):
        delta_ref[pl.ds(r0, CHUNK), :] = _half_sum(do_ref[pl.ds(r0, CHUNK), :] * o_ref[pl.ds(r0, CHUNK), :], low)
        return carry

    _chunks(step)


def _norm_bwd(raw_ref, gain_ref, dn_ref, dn_offset, out_ref, scale):
    low = _low_half()

    def step(r0, dgain):
        t = raw_ref[pl.ds(r0, CHUNK), :].astype(F32)
        dn = dn_ref[pl.ds(dn_offset + r0, CHUNK), :]
        dth = dn * (gain_ref[...] * scale)
        sums = _half_sum(jnp.concatenate([t * t, dth * t], axis=0), low)
        r = lax.rsqrt(sums[:CHUNK] * (1.0 / HEAD_DIM) + EPS)
        th = t * r
        out_ref[pl.ds(r0, CHUNK), :] = (r * (dth - th * (r * sums[CHUNK:] * (1.0 / HEAD_DIM)))).astype(BF16)
        return dgain + jnp.sum(dn * th, axis=0, keepdims=True) * scale

    return _chunks(step, jnp.zeros((1, LANES), F32))


def _rows8(v):
    return jnp.broadcast_to(v, (8, v.shape[-1]))


A_W = Q_BLOCK + 2 * A_HALF_WINDOW
A_PAD = A_HALF_WINDOW


def _seq_block(col_fn):
    return pl.BlockSpec((SEQ, LANES), col_fn)


def _attn_a_fwd(qkv, gain_q, gain_k, bias, sink):
    def body(sink_ref, q_ref, k_ref, v_ref, gq_ref, gk_ref, bias_ref, o_ref, lse_ref, qn_ref, kp_ref, vp_ref):
        hp = pl.program_id(0)
        keep = (lax.broadcasted_iota(jnp.int32, (1, LANES), 1) // HEAD_DIM) == hp // 2
        _prep_q(q_ref, gq_ref, qn_ref)
        _prep_kv(k_ref, v_ref, gk_ref, kp_ref, vp_ref, A_PAD, keep)

        def emit(rows, out, lse):
            o_ref[rows, :] = out
            lse_ref[rows, :] = lse

        _fwd_tiles(qn_ref, kp_ref, vp_ref, bias_ref, emit, d=1, half_window=A_HALF_WINDOW,
                   sinks=(sink_ref[2 * hp], sink_ref[2 * hp + 1]))

    vec = pl.BlockSpec((1, LANES), lambda hp, s: (0, 0))
    return pl.pallas_call(
        body, name="attn_a_fwd",
        grid_spec=pltpu.PrefetchScalarGridSpec(
            num_scalar_prefetch=1, grid=(4,),
            in_specs=[_seq_block(lambda hp, s: (0, QA_BLK + hp)), _seq_block(lambda hp, s: (0, KA_BLK)),
                      _seq_block(lambda hp, s: (0, VA_BLK)), vec, vec,
                      pl.BlockSpec((None, 2, Q_BLOCK, A_W), lambda hp, s: (hp, 0, 0, 0))],
            out_specs=[_seq_block(lambda hp, s: (0, hp)), _seq_block(lambda hp, s: (0, hp))],
            scratch_shapes=[pltpu.VMEM((SEQ, LANES), F32), pltpu.VMEM((SEQ + 2 * A_PAD, LANES), F32),
                            pltpu.VMEM((SEQ + 2 * A_PAD, LANES), F32)]),
        out_shape=[jax.ShapeDtypeStruct((SEQ, 512), F32)] * 2,
        compiler_params=_params("arbitrary"),
    )(sink.reshape(8), qkv, qkv, qkv, gain_q, gain_k, bias)


def _attn_a_bwd(qkv, gain_q, gain_k, bias, sink, out, lse, d_out):
    def body(sink_ref, q_ref, k_ref, v_ref, gq_ref, gk_ref, bias_ref, o_ref, lse_ref, do_ref,
             dq_out, dkv_out, dgq_out, dgk_out, ds_out, dsink_out,
             qn_ref, kp_ref, vp_ref, delta_ref, dq_ref, dk_ref, dv_ref, dk_tot, dv_tot):
        hp = pl.program_id(0)
        kv_head = hp // 2
        keep = (lax.broadcasted_iota(jnp.int32, (1, LANES), 1) // HEAD_DIM) == kv_head
        _prep_q(q_ref, gq_ref, qn_ref)
        _prep_kv(k_ref, v_ref, gk_ref, kp_ref, vp_ref, A_PAD, keep)
        _prep_delta(do_ref, o_ref, delta_ref)
        dk_ref[...] = jnp.zeros_like(dk_ref)
        dv_ref[...] = jnp.zeros_like(dv_ref)
        ds_out[...] = jnp.zeros_like(ds_out)
        dsink_out[...] = jnp.zeros_like(dsink_out)

        @pl.when(hp == 0)
        def _():
            dk_tot[...] = jnp.zeros_like(dk_tot)
            dv_tot[...] = jnp.zeros_like(dv_tot)

        _bwd_tiles(qn_ref, kp_ref, vp_ref, bias_ref, do_ref, lse_ref, delta_ref, dq_ref, dk_ref, dv_ref, ds_out,
                   d=1, half_window=A_HALF_WINDOW, sinks=(sink_ref[2 * hp], sink_ref[2 * hp + 1]),
                   dsink_ref=dsink_out)
        dgq_out[...] = _rows8(_norm_bwd(q_ref, gq_ref, dq_ref, 0, dq_out, SCALE))

        def fold(r0, carry):
            rows = pl.ds(A_PAD + r0, CHUNK)
            for acc, tot in ((dk_ref, dk_tot), (dv_ref, dv_tot)):
                t = acc[rows, :]
                tot[pl.ds(r0, CHUNK), :] += jnp.where(keep, t + pltpu.roll(t, HEAD_DIM, 1), 0.0)
            return carry

        _chunks(fold)

        @pl.when(hp == 3)
        def _():
            dgk_out[...] = _rows8(_norm_bwd(k_ref, gk_ref, dk_tot, 0, dkv_out.at[0], 1.0))
            dkv_out[1] = dv_tot[...].astype(BF16)

    vec = pl.BlockSpec((1, LANES), lambda hp, s: (0, 0))
    seq_f32 = pltpu.VMEM((SEQ, LANES), F32)
    padded = pltpu.VMEM((SEQ + 2 * A_PAD, LANES), F32)
    return pl.pallas_call(
        body, name="attn_a_bwd",
        grid_spec=pltpu.PrefetchScalarGridSpec(
            num_scalar_prefetch=1, grid=(4,),
            in_specs=[_seq_block(lambda hp, s: (0, QA_BLK + hp)), _seq_block(lambda hp, s: (0, KA_BLK)),
                      _seq_block(lambda hp, s: (0, VA_BLK)), vec, vec,
                      pl.BlockSpec((None, 2, Q_BLOCK, A_W), lambda hp, s: (hp, 0, 0, 0)),
                      _seq_block(lambda hp, s: (0, hp)), _seq_block(lambda hp, s: (0, hp)),
                      _seq_block(lambda hp, s: (0, hp))],
            out_specs=[pl.BlockSpec((None, SEQ, LANES), lambda hp, s: (hp, 0, 0)),
                       pl.BlockSpec((2, SEQ, LANES), lambda hp, s: (0, 0, 0)),
                       pl.BlockSpec((None, 8, LANES), lambda hp, s: (hp, 0, 0)),
                       pl.BlockSpec((8, LANES), lambda hp, s: (0, 0)),
                       pl.BlockSpec((None, 2, Q_BLOCK, A_W), lambda hp, s: (hp, 0, 0, 0)),
                       pl.BlockSpec((None, 2, Q_BLOCK, 1), lambda hp, s: (hp, 0, 0, 0))],
            scratch_shapes=[seq_f32, padded, padded, seq_f32, seq_f32, padded, padded, seq_f32, seq_f32]),
        out_shape=[jax.ShapeDtypeStruct((4, SEQ, LANES), BF16), jax.ShapeDtypeStruct((2, SEQ, LANES), BF16),
                   jax.ShapeDtypeStruct((4, 8, LANES), F32), jax.ShapeDtypeStruct((8, LANES), F32),
           jax.ShapeDtypeStruct((4, 2, Q_BLOCK, A_W), F32), jax.ShapeDtypeStruct((4, 2, Q_BLOCK, 1), F32)],
        compiler_params=_params("arbitrary"),
    )(sink.reshape(8), qkv, qkv, qkv, gain_q, gain_k, bias, out, lse, d_out)


B_W = Q_BLOCK + 2 * B_HALF_WINDOW
B_PAD_MAX = B_HALF_WINDOW * B_DILATIONS[-1]


def _attn_b_fwd(qkv, gain_q, gain_k, bias):
    def body(q_ref, k_ref, v_ref, gq_ref, gk_ref, bias_ref, o_ref, lse_ref, qn_ref, kp_ref, vp_ref):
        g = pl.program_id(1)
        _prep_q(q_ref, gq_ref, qn_ref)

        def first(rows, out, lse):
            o_ref[rows, :] = out
            lse_ref[rows, :] = lse

        def combine(rows, out, lse):
            old = lse_ref[rows, :]
            new = jnp.maximum(old, lse) + jnp.log(1.0 + jnp.exp(-jnp.abs(old - lse)))
            o_ref[rows, :] = o_ref[rows, :] * jnp.exp(old - new) + out * jnp.exp(lse - new)
            lse_ref[rows, :] = new

        for gi, d in enumerate(B_DILATIONS):
            @pl.when(g == gi)
            def _():
                _prep_kv(k_ref, v_ref, gk_ref, kp_ref, vp_ref, B_HALF_WINDOW * d)
                _fwd_tiles(qn_ref, kp_ref, vp_ref, bias_ref, first if gi == 0 else combine,
                           d=d, half_window=B_HALF_WINDOW)

    vec = pl.BlockSpec((1, LANES), lambda hp, g: (0, 0))
    padded = pltpu.VMEM((SEQ + 2 * B_PAD_MAX, LANES), F32)
    return pl.pallas_call(
        body, name="attn_b_fwd", grid=(4, 3),
        in_specs=[_seq_block(lambda hp, g: (0, QB_BLK + 4 * g + hp)), _seq_block(lambda hp, g: (0, KB_BLK + 4 * g + hp)),
                  _seq_block(lambda hp, g: (0, VB_BLK + 4 * g + hp)), vec, vec,
                  pl.BlockSpec((None, 2, Q_BLOCK, B_W), lambda hp, g: (4 * g + hp, 0, 0, 0))],
        out_specs=[_seq_block(lambda hp, g: (0, hp)), _seq_block(lambda hp, g: (0, hp))],
        out_shape=[jax.ShapeDtypeStruct((SEQ, 512), F32)] * 2,
        scratch_shapes=[pltpu.VMEM((SEQ, LANES), F32), padded, padded],
        compiler_params=_params("arbitrary", "arbitrary"),
    )(qkv, qkv, qkv, gain_q, gain_k, bias)


def _attn_b_bwd(qkv, gain_q, gain_k, bias, out, lse, d_out):
    def body(q_ref, k_ref, v_ref, gq_ref, gk_ref, bias_ref, o_ref, lse_ref, do_ref,
             dq_out, dk_out, dv_out, dgq_out, dgk_out, ds_out,
             qn_ref, kp_ref, vp_ref, delta_ref, dq_ref, dk_ref, dv_ref):
        g = pl.program_id(1)
        _prep_q(q_ref, gq_ref, qn_ref)
        _prep_delta(do_ref, o_ref, delta_ref)
        dk_ref[...] = jnp.zeros_like(dk_ref)
        dv_ref[...] = jnp.zeros_like(dv_ref)
        ds_out[...] = jnp.zeros_like(ds_out)
        for gi, d in enumerate(B_DILATIONS):
            @pl.when(g == gi)
            def _():
                pad = B_HALF_WINDOW * d
                _prep_kv(k_ref, v_ref, gk_ref, kp_ref, vp_ref, pad)
                _bwd_tiles(qn_ref, kp_ref, vp_ref, bias_ref, do_ref, lse_ref, delta_ref, dq_ref, dk_ref, dv_ref,
                           ds_out, d=d, half_window=B_HALF_WINDOW)
                dgk_out[...] = _rows8(_norm_bwd(k_ref, gk_ref, dk_ref, pad, dk_out, 1.0))
                dv_out[...] = dv_ref[pl.ds(pad, SEQ), :].astype(BF16)
        dgq_out[...] = _rows8(_norm_bwd(q_ref, gq_ref, dq_ref, 0, dq_out, SCALE))

    vec = pl.BlockSpec((1, LANES), lambda hp, g: (0, 0))
    seq_f32 = pltpu.VMEM((SEQ, LANES), F32)
    padded = pltpu.VMEM((SEQ + 2 * B_PAD_MAX, LANES), F32)
    part = pl.BlockSpec((None, 8, LANES), lambda hp, g: (4 * g + hp, 0, 0))
    return pl.pallas_call(
        body, name="attn_b_bwd", grid=(4, 3),
        in_specs=[_seq_block(lambda hp, g: (0, QB_BLK + 4 * g + hp)), _seq_block(lambda hp, g: (0, KB_BLK + 4 * g + hp)),
                  _seq_block(lambda hp, g: (0, VB_BLK + 4 * g + hp)), vec, vec,
                  pl.BlockSpec((None, 2, Q_BLOCK, B_W), lambda hp, g: (4 * g + hp, 0, 0, 0)),
                  _seq_block(lambda hp, g: (0, hp)), _seq_block(lambda hp, g: (0, hp)), _seq_block(lambda hp, g: (0, hp))],
        out_specs=[pl.BlockSpec((None, SEQ, LANES), lambda hp, g: (4 * g + hp, 0, 0))] * 3 + [
            part, part, pl.BlockSpec((None, 2, Q_BLOCK, B_W), lambda hp, g: (4 * g + hp, 0, 0, 0))],
        out_shape=[jax.ShapeDtypeStruct((12, SEQ, LANES), BF16)] * 3
        + [jax.ShapeDtypeStruct((12, 8, LANES), F32)] * 2 + [jax.ShapeDtypeStruct((12, 2, Q_BLOCK, B_W), F32)],
        scratch_shapes=[seq_f32, padded, padded, seq_f32, seq_f32, padded, padded],
        compiler_params=_params("arbitrary", "arbitrary"),
    )(qkv, qkv, qkv, gain_q, gain_k, bias, out, lse, d_out)


def _sigmoid(t):
    return 1.0 / (1.0 + jnp.exp(-t))


def _middle(out_a, out_b, gates, x, target, w_a, w_b, w_out, b_merge):
    tm = 256
    n_steps = SEQ // tm

    def body(oa_ref, ob_ref, g_ref, x_ref, t_ref, wa_ref, wb_ref, wo_ref, bm_ref,
             dy_ref, dg_ref, doa_ref, dob_ref, dwa_ref, dwb_ref, dwo_ref, dbm_ref, sq_ref):
        @pl.when(pl.program_id(0) == 0)
        def _():
            for ref in (dwa_ref, dwb_ref, dwo_ref, dbm_ref, sq_ref):
                ref[...] = jnp.zeros_like(ref)

        gate_a, gate_b = g_ref[:, 0:512], g_ref[:, 512:1024]
        sig_a, sig_b = _sigmoid(gate_a), _sigmoid(gate_b)
        silu_a, silu_b = gate_a * sig_a, gate_b * sig_b
        oa, ob = oa_ref[...], ob_ref[...]
        ya, yb = (oa * silu_a).astype(BF16), (ob * silu_b).astype(BF16)
        br_a, br_b = _dot(ya, wa_ref[...]), _dot(yb, wb_ref[...])
        m0 = _sigmoid(g_ref[:, 1024:2048] + bm_ref[0:1, :])
        m1 = _sigmoid(g_ref[:, 2048:3072] + bm_ref[1:2, :])
        merged = (m0 * br_a + m1 * br_b).astype(BF16)
        err = (x_ref[...] + _dot(merged, wo_ref[...])) - t_ref[...]
        sq_ref[...] += jnp.sum(err * err, axis=0, keepdims=True)

        dy = err * (1.0 / D_MODEL)
        dy_ref[...] = dy
        dyb = dy.astype(BF16)
        dmerged = _dot(dyb, wo_ref[...], NT)
        dwo_ref[...] += _dot(merged, dyb, TN)
        dbr_a, dbr_b = (dmerged * m0).astype(BF16), (dmerged * m1).astype(BF16)
        dm0 = (dmerged * br_a) * (m0 * (1.0 - m0))
        dm1 = (dmerged * br_b) * (m1 * (1.0 - m1))
        dbm_ref[0:1, :] += jnp.sum(dm0, axis=0, keepdims=True)
        dbm_ref[1:2, :] += jnp.sum(dm1, axis=0, keepdims=True)
        for s in range(N_CHIPS):
            cols = slice(256 * s, 256 * (s + 1))
            dwa_ref[s] += _dot(ya, dbr_a[:, cols], TN)
            dwb_ref[s] += _dot(yb, dbr_b[:, cols], TN)
        dya, dyb_ = _dot(dbr_a, wa_ref[...], NT), _dot(dbr_b, wb_ref[...], NT)
        doa_ref[...] = dya * silu_a
        dob_ref[...] = dyb_ * silu_b
        d_gates = (((dya * oa) * (sig_a * (1.0 + gate_a * (1.0 - sig_a)))).astype(BF16),
                   ((dyb_ * ob) * (sig_b * (1.0 + gate_b * (1.0 - sig_b)))).astype(BF16),
                   dm0.astype(BF16), dm1.astype(BF16))
        blk = 0
        for part in d_gates:
            for c0 in range(0, part.shape[1], 256):
                dg_ref[blk] = part[:, c0:c0 + 256]
                blk += 1

    def rows(width):
        return pl.BlockSpec((tm, width), lambda i: (i, 0))

    def whole(*shape):
        return pl.BlockSpec(shape, lambda i: (0,) * len(shape))

    return pl.pallas_call(
        body, name="middle", grid=(n_steps,),
        in_specs=[rows(512), rows(512), rows(GATE_WIDTH), rows(D_MODEL), rows(D_MODEL),
                  whole(512, D_MODEL), whole(512, D_MODEL), whole(D_MODEL, D_MODEL), whole(2, D_MODEL)],
        out_specs=[rows(D_MODEL), pl.BlockSpec((GATE_WIDTH // 256, tm, 256), lambda i: (0, i, 0)), rows(512), rows(512),
                   whole(N_CHIPS, 512, 256), whole(N_CHIPS, 512, 256), whole(D_MODEL, D_MODEL),
                   whole(2, D_MODEL), whole(1, D_MODEL)],
        out_shape=[jax.ShapeDtypeStruct((SEQ, D_MODEL), F32), jax.ShapeDtypeStruct((GATE_WIDTH // 256, SEQ, 256), BF16),
                   jax.ShapeDtypeStruct((SEQ, 512), F32), jax.ShapeDtypeStruct((SEQ, 512), F32),
                   jax.ShapeDtypeStruct((N_CHIPS, 512, 256), F32), jax.ShapeDtypeStruct((N_CHIPS, 512, 256), F32),
                   jax.ShapeDtypeStruct((D_MODEL, D_MODEL), F32), jax.ShapeDtypeStruct((2, D_MODEL), F32),
                   jax.ShapeDtypeStruct((1, D_MODEL), F32)],
        compiler_params=_params("arbitrary"),
    )(out_a, out_b, gates, x, target, w_a, w_b, w_out, b_merge)


def _which(j, edges, fns):
    lo = 0
    for hi, fn in zip(edges, fns):
        pl.when((j >= lo) & (j < hi))(fn)
        lo = hi


def _d_w_in(d_proj, h):
    tn = 256
    per_step = [tn // p.shape[2] for p in d_proj]
    edges = tuple(np.cumsum([p.shape[0] // n for p, n in zip(d_proj, per_step)]))

    def body(*refs):
        pieces, h_ref, o_ref = refs[:-2], refs[-2], refs[-1]

        def emit(ref):
            def fn():
                wb = ref.shape[2]
                for k in range(ref.shape[0]):
                    o_ref[k * wb:(k + 1) * wb, :] = _dot(ref[k], h_ref[...], TN)
            return fn

        _which(pl.program_id(0), edges, [emit(ref) for ref in pieces])

    def cols(piece, n, lo, hi):
        return pl.BlockSpec((n, SEQ, piece.shape[2]), lambda j: (jnp.clip(j - lo, 0, hi - lo - 1), 0, 0))

    return pl.pallas_call(
        body, name="d_w_in", grid=(int(edges[-1]),),
        in_specs=[cols(p, n, int(lo), int(hi)) for p, n, lo, hi in zip(d_proj, per_step, (0,) + edges[:-1], edges)]
        + [pl.BlockSpec((SEQ, D_MODEL), lambda j: (0, 0))],
        out_specs=pl.BlockSpec((tn, D_MODEL), lambda j: (j, 0)),
        out_shape=jax.ShapeDtypeStruct((IN_WIDTH, D_MODEL), F32),
        compiler_params=_params("arbitrary"),
    )(*d_proj, h)


def _d_x(d_proj, w_t, x, gain, dy, chip_sums):
    tm = 256
    n_steps = SEQ // tm
    n_w = IN_WIDTH // W_BLOCK
    n_p, n_s = len(d_proj), len(chip_sums)

    def body(*refs):
        pieces, w_refs = refs[:n_p], refs[n_p:n_p + n_w]
        x_ref, g_ref, dy_ref = refs[n_p + n_w:n_p + n_w + 3]
        q_refs = refs[n_p + n_w + 3:n_p + n_w + 3 + n_s]
        dx_ref, dgain_ref = refs[n_p + n_w + 3 + n_s:n_p + n_w + 5 + n_s]
        o_refs = refs[n_p + n_w + 5 + n_s:n_p + n_w + 5 + 2 * n_s]
        send_sems, recv_sems = refs[n_p + n_w + 5 + 2 * n_s:] if n_s else (None, None)

        @pl.when(pl.program_id(0) == 0)
        def _():
            dgain_ref[...] = jnp.zeros_like(dgain_ref)
            if n_s:
                for cp in _scatter_copies(q_refs, o_refs, send_sems, recv_sems):
                    cp.start()

        blocks = [(piece, k) for piece in pieces for k in range(piece.shape[0])]
        dh, group, width, blk = None, [], 0, 0
        for piece, k in blocks:
            group.append(piece[k])
            width += piece.shape[2]
            if width == W_BLOCK:
                term = _dot(jnp.concatenate(group, axis=1), w_refs[blk][...])
                dh = term if dh is None else dh + term
                group, width, blk = [], 0, blk + 1
        assert not group and blk == n_w
        xf = x_ref[...]
        r = lax.rsqrt(jnp.mean(xf * xf, axis=-1, keepdims=True) + EPS)
        xh = xf * r
        dxh = dh * g_ref[...]
        dx_ref[...] = r * (dxh - xh * jnp.mean(dxh * xh, axis=-1, keepdims=True)) + dy_ref[...]
        dgain_ref[...] += _rows8(jnp.sum(dh * xh, axis=0, keepdims=True))

        if n_s:
            @pl.when(pl.program_id(0) == n_steps - 1)
            def _():
                for cp in _scatter_copies(q_refs, o_refs, send_sems, recv_sems):
                    cp.wait()

    row = pl.BlockSpec((tm, D_MODEL), lambda i: (i, 0))
    res = pl.pallas_call(
        body, name="d_x", grid=(n_steps,),
        in_specs=[pl.BlockSpec((p.shape[0], tm, p.shape[2]), lambda i: (0, i, 0)) for p in d_proj] + _w_blocks(0, n_w)
        + [row, pl.BlockSpec((1, D_MODEL), lambda i: (0, 0)), row] + [ANY] * n_s,
        out_specs=[row, pl.BlockSpec((8, D_MODEL), lambda i: (0, 0))] + [ANY] * n_s,
        out_shape=[jax.ShapeDtypeStruct((SEQ, D_MODEL), F32), jax.ShapeDtypeStruct((8, D_MODEL), F32)]
        + [jax.ShapeDtypeStruct((3,) + q.shape[1:], BF16) for q in chip_sums],
        scratch_shapes=[pltpu.SemaphoreType.DMA((3 * n_s,)), pltpu.SemaphoreType.DMA((3 * n_s,))] if n_s else [],
        compiler_params=_params("arbitrary"),
    )(*d_proj, *([w_t] * n_w), x, gain, dy, *chip_sums)
    return res[0], res[1], res[2:]


def _my_place():
    x, y, c = lax.axis_index("x"), lax.axis_index("y"), lax.axis_index("c")
    return jnp.stack([2 * x + y, c]).astype(jnp.int32)


def _half_rows(ref, half):
    rows = ref.shape[-2] // 2
    idx = (slice(None),) * (len(ref.shape) - 2) + (pl.ds(pl.multiple_of(half * rows, 16), rows), slice(None))
    return ref.at[idx]


def _swap_halves(grads):
    n = len(grads)

    def body(*refs):
        g_refs, o_refs, (send_sems, recv_sems) = refs[:n], refs[n:2 * n], refs[2 * n:]
        x, y, c = lax.axis_index("x"), lax.axis_index("y"), lax.axis_index("c")
        copies = [pltpu.make_async_remote_copy(src_ref=_half_rows(g, 1 - c), dst_ref=o, send_sem=send_sems.at[k],
                                               recv_sem=recv_sems.at[k], device_id=(x, y, 1 - c), device_id_type=MESH)
                  for k, (g, o) in enumerate(zip(g_refs, o_refs))]
        for cp in copies:
            cp.start()
        for cp in copies:
            cp.wait()

    return pl.pallas_call(
        body, name="reduce_swap_halves", in_specs=[ANY] * n, out_specs=[ANY] * n,
        out_shape=[jax.ShapeDtypeStruct((N_CHIPS, g.shape[1] // 2, D_MODEL), F32) for g in grads],
        scratch_shapes=[pltpu.SemaphoreType.DMA((n,)), pltpu.SemaphoreType.DMA((n,))],
    )(*grads)


def _row_tile(rows):
    return max(t for t in range(16, 385, 16) if rows % t == 0)


def _add_halves(place, grads, theirs, name):
    half = theirs.shape[1]
    tr = _row_tile(half)
    n = half // tr

    def body(place_ref, g_ref, t_ref, o_ref):
        o_ref[...] = (g_ref[...] + t_ref[...]).astype(BF16)

    return pl.pallas_call(
        body, name=name,
        grid_spec=pltpu.PrefetchScalarGridSpec(
            num_scalar_prefetch=1, grid=(N_CHIPS, n),
            in_specs=[pl.BlockSpec((None, tr, D_MODEL), lambda s, i, p: (s, p[1] * n + i, 0)),
                      pl.BlockSpec((None, tr, D_MODEL), lambda s, i, p: (s, i, 0))],
            out_specs=pl.BlockSpec((None, tr, D_MODEL), lambda s, i, p: (s, i, 0))),
        out_shape=jax.ShapeDtypeStruct((N_CHIPS, half, D_MODEL), BF16),
        compiler_params=_params("arbitrary", "arbitrary"),
    )(place, grads, theirs)


def _scatter_copies(q_refs, o_refs, send_sems, recv_sems):
    x, y, c = lax.axis_index("x"), lax.axis_index("y"), lax.axis_index("c")
    chips = [(1 - x, y), (x, 1 - y), (1 - x, 1 - y)]
    return [pltpu.make_async_remote_copy(src_ref=q.at[2 * cx + cy], dst_ref=o.at[j],
                                         send_sem=send_sems.at[3 * k + j], recv_sem=recv_sems.at[3 * k + j],
                                         device_id=(cx, cy, c), device_id_type=MESH)
            for k, (q, o) in enumerate(zip(q_refs, o_refs)) for j, (cx, cy) in enumerate(chips)]


def _add_chips(place, chip_sums, others, name):
    half = others.shape[1]
    tr = _row_tile(half)
    n = half // tr

    def body(place_ref, q_ref, o_ref, r_ref):
        acc = q_ref[...].astype(F32)
        for j in range(3):
            acc = acc + o_ref[j].astype(F32)
        r_ref[...] = acc

    return pl.pallas_call(
        body, name=name,
        grid_spec=pltpu.PrefetchScalarGridSpec(
            num_scalar_prefetch=1, grid=(n,),
            in_specs=[pl.BlockSpec((None, tr, D_MODEL), lambda i, p: (p[0], i, 0)),
                      pl.BlockSpec((3, tr, D_MODEL), lambda i, p: (0, i, 0))],
            out_specs=pl.BlockSpec((tr, D_MODEL), lambda i, p: (p[1] * n + i, 0))),
        out_shape=jax.ShapeDtypeStruct((2 * half, D_MODEL), F32),
        compiler_params=_params("arbitrary"),
    )(place, chip_sums, others)


def _join_halves(shards):
    n = len(shards)

    def body(*refs):
        o_refs, (send_sems, recv_sems) = refs[n:2 * n], refs[2 * n:]
        x, y, c = lax.axis_index("x"), lax.axis_index("y"), lax.axis_index("c")

        def copy(k, rows):
            return pltpu.make_async_remote_copy(src_ref=rows, dst_ref=rows, send_sem=send_sems.at[k],
                                                recv_sem=recv_sems.at[k], device_id=(x, y, 1 - c), device_id_type=MESH)

        sends = [copy(k, _half_rows(o, c)) for k, o in enumerate(o_refs)]
        for cp in sends:
            cp.start()
        for k, o in enumerate(o_refs):
            copy(k, _half_rows(o, 1 - c)).wait_recv()
        for cp in sends:
            cp.wait_send()

    return pl.pallas_call(
        body, name="reduce_join_halves", in_specs=[ANY] * n, out_specs=[ANY] * n,
        out_shape=[jax.ShapeDtypeStruct(s.shape, F32) for s in shards],
        input_output_aliases={k: k for k in range(n)},
        scratch_shapes=[pltpu.SemaphoreType.DMA((n,)), pltpu.SemaphoreType.DMA((n,))],
    )(*shards)


def _gather_small(block):
    rows = block.shape[0]

    def body(b_ref, o_ref, send_sems, recv_sems, local_sem):
        x, y, c = lax.axis_index("x"), lax.axis_index("y"), lax.axis_index("c")
        me, sibling = (x, y, c), (x, y, 1 - c)
        chips = [(1 - x, y), (x, 1 - y), (1 - x, 1 - y)]

        def at(px, py, pc):
            return o_ref.at[pl.ds(pl.multiple_of((4 * px + 2 * py + pc) * rows, 8), rows), :]

        def copy(k, block_of, to, src=None):
            return pltpu.make_async_remote_copy(src_ref=at(*block_of) if src is None else src, dst_ref=at(*block_of),
                                                send_sem=send_sems.at[k], recv_sem=recv_sems.at[k],
                                                device_id=to, device_id_type=MESH)

        mine = pltpu.make_async_copy(b_ref, at(*me), local_sem)
        mine.start()
        first = [copy(0, me, sibling, src=b_ref)]
        first += [copy(1 + j, me, (*chip, c), src=b_ref) for j, chip in enumerate(chips)]
        for cp in first:
            cp.start()
        passed = [copy(4 + j, (*chip, c), sibling) for j, chip in enumerate(chips)]
        for j, chip in enumerate(chips):
            copy(1 + j, (*chip, c), me).wait_recv()
            passed[j].start()
        copy(0, sibling, me).wait_recv()
        for j, chip in enumerate(chips):
            copy(4 + j, (*chip, 1 - c), me).wait_recv()
        for cp in first + passed:
            cp.wait_send()
        mine.wait()

    return pl.pallas_call(
        body, name="gather_small_grads",
        in_specs=[pl.BlockSpec(memory_space=pltpu.VMEM)], out_specs=pl.BlockSpec(memory_space=pltpu.VMEM),
        out_shape=jax.ShapeDtypeStruct((8 * rows, D_MODEL), F32),
        scratch_shapes=[pltpu.SemaphoreType.DMA((7,)), pltpu.SemaphoreType.DMA((7,)), pltpu.SemaphoreType.DMA],
    )(block)


def _sum_devices(blocks):
    def body(b_ref, o_ref):
        acc = b_ref[0:8, :]
        for dev in range(1, 8):
            acc = acc + b_ref[8 * dev:8 * dev + 8, :]
        o_ref[...] = acc

    return pl.pallas_call(body, name="sum_small_grads", out_shape=jax.ShapeDtypeStruct((8, D_MODEL), F32))(blocks)


def _adamw_math(w, g, m, v):
    m = ADAM_B1 * m + (1.0 - ADAM_B1) * g
    v = ADAM_B2 * v + (1.0 - ADAM_B2) * (g * g)
    m_hat = m / (1.0 - ADAM_B1 ** ADAM_STEP)
    v_hat = v / (1.0 - ADAM_B2 ** ADAM_STEP)
    return -ADAM_LR * (m_hat / (jnp.sqrt(v_hat) + ADAM_EPS) + ADAM_WD * w), m, v


def _adamw(w, g, m, v, name):
    _, r, c = w.shape
    tr = 128 if r % 128 == 0 else r

    def body(w_ref, g_ref, m_ref, v_ref, d_ref, nm_ref, nv_ref):
        d_ref[...], nm_ref[...], nv_ref[...] = _adamw_math(w_ref[...], g_ref[...], m_ref[...], v_ref[...])

    spec = pl.BlockSpec((None, tr, c), lambda i: (0, i, 0))
    return pl.pallas_call(
        body, name=name, grid=(r // tr,), in_specs=[spec, pl.BlockSpec((tr, c), lambda i: (i, 0)), spec, spec],
        out_specs=[spec] * 3, out_shape=[jax.ShapeDtypeStruct((1, r, c), F32)] * 3,
        compiler_params=_params("arbitrary"),
    )(w, g, m, v)


def _adamw_small(ws, gs, ms, vs):
    n = len(ws)

    def body(*refs):
        ins, outs = refs[:4 * n], refs[4 * n:]
        for k in range(n):
            d, m, v = _adamw_math(ins[k][...], ins[n + k][...], ins[2 * n + k][...], ins[3 * n + k][...])
            outs[k][...], outs[n + k][...], outs[2 * n + k][...] = d, m, v

    shapes = [jax.ShapeDtypeStruct(w.shape, F32) for w in ws]
    res = pl.pallas_call(body, name="adamw_small", out_shape=shapes * 3)(*ws, *gs, *ms, *vs)
    return res[:n], res[n:2 * n], res[2 * n:]


def _fold_heads(partials):
    t = jnp.sum(partials[:, 0, :], axis=0)
    return (t[:HEAD_DIM] + t[HEAD_DIM:]).reshape(1, HEAD_DIM)


def _local_step(x, target, norm_gain, w_t, w_a, w_b, w_o, b_m, q_norm_a, k_norm_a, q_norm_b, k_norm_b, sink_a,
                rel_bias, start_reduce=None):
    two = lambda gain: jnp.concatenate([gain, gain], axis=1)
    bias_a = _bias_table(rel_bias[:, :8], A_HALF_WINDOW, 1)
    bias_b = jnp.concatenate([_bias_table(rel_bias[:, 8 + 8 * g:16 + 8 * g], B_HALF_WINDOW, d)
                              for g, d in enumerate(B_DILATIONS)], axis=0)

    qkv, h = _in_proj(x, norm_gain, w_t, 0, QKV_WIDTH // W_BLOCK, BF16, "in_proj_qkv")
    gates, _ = _in_proj(x, norm_gain, w_t, QKV_WIDTH // W_BLOCK, GATE_WIDTH // W_BLOCK, F32, "in_proj_gates")
    out_a, lse_a = _attn_a_fwd(qkv, two(q_norm_a), two(k_norm_a), bias_a, sink_a)
    out_b, lse_b = _attn_b_fwd(qkv, two(q_norm_b), two(k_norm_b), bias_b)

    dy, dgates, d_out_a, d_out_b, d_wa, d_wb, d_wo, d_bm, sq = _middle(
        out_a, out_b, gates, x, target, w_a, w_b, w_o, b_m)
    loss = (0.5 / D_MODEL) * jnp.sum(sq)

    dq_a, dkv_a, dgq_a, dgk_a, ds_a, dsink = _attn_a_bwd(
        qkv, two(q_norm_a), two(k_norm_a), bias_a, sink_a, out_a, lse_a, d_out_a)
    dq_b, dk_b, dv_b, dgq_b, dgk_b, ds_b = _attn_b_bwd(
        qkv, two(q_norm_b), two(k_norm_b), bias_b, out_b, lse_b, d_out_b)
    d_proj = (dq_a, dkv_a, dq_b, dk_b, dv_b, dgates)

    d_bm_rows = jnp.pad(d_bm.reshape(2, N_CHIPS, 256).transpose(1, 0, 2),
                        ((0, 0), (0, REST_ROWS - 514), (0, D_MODEL - 256)))
    rest = jnp.concatenate([d_wo.reshape(N_CHIPS, 256, D_MODEL), d_wa.reshape(N_CHIPS, 128, D_MODEL),
                            d_wb.reshape(N_CHIPS, 128, D_MODEL), d_bm_rows], axis=1)
    grads = [_d_w_in(d_proj, h).reshape(N_CHIPS, W_IN_SHARD, D_MODEL), rest]
    chip_sums = start_reduce(grads) if start_reduce is not None else []
    grad_x, d_gain, others = _d_x(d_proj, w_t, x, norm_gain, dy, chip_sums)

    d_rel = jnp.concatenate(
        [_bias_grad(ds_a, A_HALF_WINDOW, 1)]
        + [_bias_grad(ds_b[4 * g:4 * g + 4], B_HALF_WINDOW, d) for g, d in enumerate(B_DILATIONS)], axis=1)
    d_sink = jnp.sum(dsink, axis=(2, 3)).reshape(1, 8)
    dgk_a_row = dgk_a[0]
    small = jnp.zeros((8, D_MODEL), F32)
    small = small.at[0].set(d_gain[0])
    small = small.at[1].set(d_rel.reshape(-1))
    misc = jnp.concatenate([_fold_heads(dgq_a), (dgk_a_row[:HEAD_DIM] + dgk_a_row[HEAD_DIM:]).reshape(1, HEAD_DIM),
                            _fold_heads(dgq_b), _fold_heads(dgk_b), d_sink], axis=1)
    small = small.at[2, :264].set(misc[0])

    return loss, grad_x, grads, small, chip_sums, others


def _unpack_weights(w_t_all, small_all):
    sm = small_all.reshape(N_CHIPS, SMALL_ROWS, D_MODEL)
    w_o = sm[:, 0:256].reshape(D_MODEL, D_MODEL)
    w_a = sm[:, 256:384].reshape(N_CHIPS, 512, 256).transpose(1, 0, 2).reshape(512, D_MODEL)
    w_b = sm[:, 384:512].reshape(N_CHIPS, 512, 256).transpose(1, 0, 2).reshape(512, D_MODEL)
    b_m = lax.bitcast_convert_type(sm[:, 512].reshape(N_CHIPS, 2, 256, 2), F32)
    return w_t_all, w_a, w_b, w_o, b_m.transpose(1, 0, 2).reshape(2, D_MODEL)


def _pack_small_weights(w_branch_a, w_branch_b, b_merge, w_out):
    b_m = jnp.pad(lax.bitcast_convert_type(b_merge, BF16).reshape(1, D_MODEL), ((0, SMALL_ROWS - 513), (0, 0)))
    return jnp.concatenate([w_out.astype(BF16), w_branch_a.astype(BF16).reshape(128, D_MODEL),
                            w_branch_b.astype(BF16).reshape(128, D_MODEL), b_m], axis=0)


def kernel(x, norm_gain, w_in, q_norm_a, k_norm_a, q_norm_b, k_norm_b, sink_a, rel_bias, w_branch_a, w_branch_b, b_merge, w_out, loss_target, m_norm_gain, m_w_in, m_q_norm_a, m_k_norm_a, m_q_norm_b, m_k_norm_b, m_sink_a, m_rel_bias, m_w_branch_a, m_w_branch_b, m_b_merge, m_w_out, v_norm_gain, v_w_in, v_q_norm_a, v_k_norm_a, v_q_norm_b, v_k_norm_b, v_sink_a, v_rel_bias, v_w_branch_a, v_w_branch_b, v_b_merge, v_w_out):
    wt_shard = _transpose_cast(w_in, BF16, "w_in_transpose")
    w_t, w_a, w_b, w_o, b_m = _unpack_weights(
        *_gather_weights(wt_shard, _pack_small_weights(w_branch_a[0], w_branch_b[0], b_merge[0], w_out[0])))

    place = _my_place()
    names = ("w_in", "rest")

    def start_reduce(grads):
        return [_add_halves(place, g, t, "reduce_add_halves_" + n) for g, t, n in zip(grads, _swap_halves(grads), names)]

    loss_part, grad_x, _, small, chip_sums, others = _local_step(
        x[0], loss_target[0], norm_gain, w_t, w_a, w_b, w_o, b_m, q_norm_a, k_norm_a, q_norm_b, k_norm_b,
        sink_a, rel_bias, start_reduce)

    g_wt, g_rest = _join_halves([_add_chips(place, q, o, "reduce_add_chips_" + n)
                                 for q, o, n in zip(chip_sums, others, names)])
    small = _sum_devices(_gather_small(small.at[3, 0].set(loss_part)))
    loss = small[3, 0]

    g_w_in = _transpose_cast(g_wt, F32, "grad_w_in_transpose")
    g_w_out = g_rest[0:256]
    g_w_a = g_rest[256:384].reshape(512, 256)
    g_w_b = g_rest[384:512].reshape(512, 256)
    g_b_merge = g_rest[512:514, :256]
    g_norm_gain = small[0:1]
    g_rel_bias = small[1].reshape(N_BUCKETS, N_BUCKETS)
    g_q_a, g_k_a, g_q_b, g_k_b = (small[2:3, 64 * k:64 * k + 64] for k in range(4))
    g_sink = small[2:3, 256:264]

    big_names = (("w_in", w_in, g_w_in, m_w_in, v_w_in),
                 ("w_branch_a", w_branch_a, g_w_a, m_w_branch_a, v_w_branch_a),
                 ("w_branch_b", w_branch_b, g_w_b, m_w_branch_b, v_w_branch_b),
                 ("w_out", w_out, g_w_out, m_w_out, v_w_out))
    upd = {name: (g[None],) + tuple(_adamw(w, g, m, v, "adamw_" + name)) for name, w, g, m, v in big_names}
    small_names = ("norm_gain", "q_norm_a", "k_norm_a", "q_norm_b", "k_norm_b", "sink_a", "rel_bias", "b_merge")
    ws = [norm_gain, q_norm_a, k_norm_a, q_norm_b, k_norm_b, sink_a, rel_bias, b_merge[0]]
    gs = [g_norm_gain, g_q_a, g_k_a, g_q_b, g_k_b, g_sink, g_rel_bias, g_b_merge]
    ms = [m_norm_gain, m_q_norm_a, m_k_norm_a, m_q_norm_b, m_k_norm_b, m_sink_a, m_rel_bias, m_b_merge[0]]
    vs = [v_norm_gain, v_q_norm_a, v_k_norm_a, v_q_norm_b, v_k_norm_b, v_sink_a, v_rel_bias, v_b_merge[0]]
    ds, nms, nvs = _adamw_small(ws, gs, ms, vs)
    for k, name in enumerate(small_names):
        upd[name] = (gs[k], ds[k], nms[k], nvs[k])

    order = ("norm_gain", "w_in", "q_norm_a", "k_norm_a", "q_norm_b", "k_norm_b", "sink_a", "rel_bias",
             "w_branch_a", "w_branch_b", "b_merge", "w_out")
    lead = {"b_merge"}
    outs = [loss, grad_x[None]]
    for part in range(4):
        outs += [upd[name][part][None] if name in lead else upd[name][part] for name in order]
    return tuple(outs)
```

```python
import math

import numpy as np
import jax
import jax.numpy as jnp
from jax import lax
from jax.experimental import pallas as pl
from jax.experimental.pallas import tpu as pltpu

F32 = jnp.float32
BF16 = jnp.bfloat16

SEQ = 4096
D_MODEL = 1024
HEAD_DIM = 64
LANES = 128
EPS = 1e-6
NEG_INF = -1e30
SCALE = HEAD_DIM ** -0.5
N_BUCKETS = 32
MAX_DISTANCE = 1024
N_CHIPS = 4

A_HALF_WINDOW = 128
B_HALF_WINDOW = 64
B_DILATIONS = (1, 4, 16)
Q_BLOCK = 128

QKV_WIDTH = 5376
GATE_WIDTH = 3072
QA_BLK, KA_BLK, VA_BLK = 0, 4, 5
QB_BLK, KB_BLK, VB_BLK = 6, 18, 30
IN_WIDTH = QKV_WIDTH + GATE_WIDTH
W_IN_SHARD = IN_WIDTH // N_CHIPS

SMALL_ROWS = 544
REST_ROWS = 544

ADAM_LR = 0.001
ADAM_B1 = 0.9
ADAM_B2 = 0.999
ADAM_EPS = 1e-08
ADAM_WD = 0.01
ADAM_STEP = 10

VMEM_LIMIT = 56 * 1024 * 1024

NT = (((1,), (1,)), ((), ()))
TN = (((0,), (0,)), ((), ()))
MESH = pl.DeviceIdType.MESH
ANY = pl.BlockSpec(memory_space=pl.ANY)


def _dot(a, b, dims=None):
    if dims is None:
        return jnp.dot(a, b, preferred_element_type=F32)
    return lax.dot_general(a, b, dims, preferred_element_type=F32)


def _params(*semantics):
    return pltpu.CompilerParams(dimension_semantics=semantics or None, vmem_limit_bytes=VMEM_LIMIT)


def _bucket_onehot(half_window, stride):
    w = Q_BLOCK + 2 * half_window
    rel = (np.arange(w)[None, :] - half_window - np.arange(Q_BLOCK)[:, None])
    band = np.abs(rel) <= half_window
    rel = rel * stride
    half, max_exact = N_BUCKETS // 2, N_BUCKETS // 4
    n = np.abs(rel)
    nf = np.maximum(n, max_exact).astype(np.float32)
    large = max_exact + (np.log(nf / np.float32(max_exact)) / np.float32(math.log(MAX_DISTANCE / max_exact))
                         * np.float32(half - max_exact)).astype(np.int32)
    large = np.minimum(large, half - 1)
    bucket = (rel > 0).astype(np.int32) * half + np.where(n < max_exact, n, large)
    onehot = (bucket[..., None] == np.arange(N_BUCKETS)) & band[..., None]
    return onehot.reshape(Q_BLOCK * w, N_BUCKETS).astype(np.float32), band


def _bias_table(rel_bias_cols, half_window, stride):
    onehot, band = _bucket_onehot(half_window, stride)
    h = rel_bias_cols.shape[1]
    w = Q_BLOCK + 2 * half_window
    t = jnp.einsum("pb,bh->hp", jnp.asarray(onehot), rel_bias_cols, precision=lax.Precision.HIGHEST)
    t = t.reshape(h, Q_BLOCK, w) + jnp.asarray(np.where(band, 0.0, NEG_INF).astype(np.float32))
    return t.reshape(h // 2, 2, Q_BLOCK, w)


def _bias_grad(ds_sum, half_window, stride):
    onehot, _ = _bucket_onehot(half_window, stride)
    h = ds_sum.shape[0] * 2
    return jnp.einsum("pb,hp->bh", jnp.asarray(onehot), ds_sum.reshape(h, -1), precision=lax.Precision.HIGHEST)


def _transpose_cast(w, out_dtype, name):
    lead = (None,) * (w.ndim - 2)
    zero = (0,) * (w.ndim - 2)
    r, c = w.shape[-2:]

    def body(w_ref, o_ref):
        o_ref[...] = w_ref[...].T.astype(out_dtype)

    if r % LANES == 0:
        steps = pl.cdiv(c, LANES)
        in_spec = pl.BlockSpec(lead + (r, LANES), lambda j: zero + (0, j))
        out_spec = pl.BlockSpec((LANES, r), lambda j: (j, 0))
    else:
        steps = pl.cdiv(r, LANES)
        in_spec = pl.BlockSpec(lead + (LANES, c), lambda j: zero + (j, 0))
        out_spec = pl.BlockSpec((c, LANES), lambda j: (0, j))
    return pl.pallas_call(
        body, name=name, grid=(steps,), in_specs=[in_spec], out_specs=out_spec,
        out_shape=jax.ShapeDtypeStruct((c, r), out_dtype),
        compiler_params=_params("arbitrary"),
    )(w)


def _gather_weights(wt_shard, small_shard):
    bufs = ((W_IN_SHARD, IN_WIDTH), (SMALL_ROWS, N_CHIPS * SMALL_ROWS))

    stage_rows = 528

    def body(wt_in, sm_in, wt_out, sm_out, send_sems, recv_sems, in_sems, out_sems, stage):
        x, y, c = lax.axis_index("x"), lax.axis_index("y"), lax.axis_index("c")
        sibling = (x, y, 1 - c)
        chips = [(1 - x, y), (x, 1 - y), (1 - x, 1 - y)]
        my_chip = 2 * x + y
        refs = ((wt_in, wt_out), (sm_in, sm_out))

        def keep_own():
            pieces = [(b, r0) for b in range(2) for r0 in range(0, bufs[b][0], stage_rows)]
            outs = []
            for i, (b, r0) in enumerate(pieces):
                rows = min(stage_rows, bufs[b][0] - r0)
                slot = i % 2
                if i >= 2:
                    outs[i - 2].wait()
                buf = stage.at[slot, pl.ds(0, rows), :]
                load = pltpu.make_async_copy(refs[b][0].at[pl.ds(r0, rows), :], buf, in_sems.at[slot])
                load.start()
                load.wait()
                start = pl.multiple_of(my_chip * bufs[b][0] + r0, 16)
                outs.append(pltpu.make_async_copy(buf, refs[b][1].at[pl.ds(start, rows), :], out_sems.at[slot]))
                outs[i].start()
            for cp in outs[-2:]:
                cp.wait()

        def half_of(b, chip, half):
            rows = bufs[b][0]
            start = pl.multiple_of(chip * rows + half * (rows // 2), 16)
            return refs[b][1].at[pl.ds(start, rows // 2), :]

        def copy(k, src, dst, to):
            return pltpu.make_async_remote_copy(src_ref=src, dst_ref=dst, send_sem=send_sems.at[k],
                                                recv_sem=recv_sems.at[k], device_id=to, device_id_type=MESH)

        first, passed = [], []
        for b in range(2):
            rows = bufs[b][0]
            src = refs[b][0].at[pl.ds(pl.multiple_of(c * (rows // 2), 16), rows // 2), :]
            for j, chip in enumerate(chips):
                first.append(copy(3 * b + j, src, half_of(b, my_chip, c), (*chip, c)))
        for cp in first:
            cp.start()
        keep_own()
        for b in range(2):
            for j, (cx, cy) in enumerate(chips):
                landed = half_of(b, 2 * cx + cy, c)
                copy(3 * b + j, landed, landed, sibling).wait_recv()
                fwd = copy(6 + 3 * b + j, landed, landed, sibling)
                fwd.start()
                passed.append(fwd)
        for b in range(2):
            for j, (cx, cy) in enumerate(chips):
                other = half_of(b, 2 * cx + cy, 1 - c)
                copy(6 + 3 * b + j, other, other, sibling).wait_recv()
        for cp in first + passed:
            cp.wait_send()

    return pl.pallas_call(
        body, name="gather_weights",
        in_specs=[ANY, ANY], out_specs=[ANY, ANY],
        out_shape=[jax.ShapeDtypeStruct((bufs[0][1], D_MODEL), BF16),
                   jax.ShapeDtypeStruct((bufs[1][1], D_MODEL), BF16)],
        scratch_shapes=[pltpu.SemaphoreType.DMA((12,)), pltpu.SemaphoreType.DMA((12,)),
                        pltpu.SemaphoreType.DMA((2,)), pltpu.SemaphoreType.DMA((2,)),
                        pltpu.VMEM((2, stage_rows, D_MODEL), BF16)],
    )(wt_shard, small_shard)


W_BLOCK = 768


def _w_blocks(first, count):
    return [pl.BlockSpec((W_BLOCK, D_MODEL), lambda *_, k=k: (first + k, 0)) for k in range(count)]


def _in_proj(x, gain, w_t, first_block, n_blocks, out_dtype, name):
    tm = 256

    def body(x_ref, g_ref, *refs):
        w_refs, (o_ref, h_ref) = refs[:n_blocks], refs[n_blocks:]
        xf = x_ref[...]
        r = lax.rsqrt(jnp.mean(xf * xf, axis=-1, keepdims=True) + EPS)
        h = ((xf * r) * g_ref[...]).astype(BF16)
        h_ref[...] = h
        for k, w_ref in enumerate(w_refs):
            o_ref[:, k * W_BLOCK:(k + 1) * W_BLOCK] = _dot(h, w_ref[...], NT).astype(out_dtype)

    return pl.pallas_call(
        body, name=name, grid=(SEQ // tm,),
        in_specs=[pl.BlockSpec((tm, D_MODEL), lambda i: (i, 0)), pl.BlockSpec((1, D_MODEL), lambda i: (0, 0))]
        + _w_blocks(first_block, n_blocks),
        out_specs=[pl.BlockSpec((tm, W_BLOCK * n_blocks), lambda i: (i, 0)),
                   pl.BlockSpec((tm, D_MODEL), lambda i: (i, 0))],
        out_shape=[jax.ShapeDtypeStruct((SEQ, W_BLOCK * n_blocks), out_dtype),
                   jax.ShapeDtypeStruct((SEQ, D_MODEL), BF16)],
        compiler_params=_params("arbitrary"),
    )(x, gain, *([w_t] * n_blocks))


CHUNK = 256
CHUNK_UNROLL = 4
TILE_UNROLL = 4


def _low_half():
    return lax.broadcasted_iota(jnp.int32, (1, LANES), 1) < HEAD_DIM


def _half_sum(v, low):
    del low
    row = lax.broadcasted_iota(jnp.int32, (2 * LANES, LANES), 0)
    col = lax.broadcasted_iota(jnp.int32, (2 * LANES, LANES), 1)
    ones = jnp.where((row % LANES) // HEAD_DIM == col // HEAD_DIM, 1.0, 0.0).astype(BF16)
    hi = v.astype(BF16)
    lo = (v - hi.astype(F32)).astype(BF16)
    return _dot(jnp.concatenate([hi, lo], axis=1), ones)


def _chunks(fn, init=0):
    def body(i, carry):
        for u in range(CHUNK_UNROLL):
            carry = fn(pl.multiple_of((i * CHUNK_UNROLL + u) * CHUNK, CHUNK), carry)
        return carry

    return lax.fori_loop(0, SEQ // (CHUNK * CHUNK_UNROLL), body, init)


def _inv_rms(t, low):
    return lax.rsqrt(_half_sum(t * t, low) * (1.0 / HEAD_DIM) + EPS)


def _prep_q(q_ref, gain_ref, qn_ref):
    low = _low_half()

    def step(r0, carry):
        q = q_ref[pl.ds(r0, CHUNK), :].astype(F32)
        qn_ref[pl.ds(r0, CHUNK), :] = ((q * _inv_rms(q, low)) * gain_ref[...]) * SCALE
        return carry

    _chunks(step)


def _own_half(t, keep):
    return jnp.where(keep, t, pltpu.roll(t, HEAD_DIM, 1))


def _prep_kv(k_ref, v_ref, gain_ref, kp_ref, vp_ref, pad, keep=None):
    low = _low_half()
    zeros = jnp.zeros((pad, LANES), F32)
    for ref in (kp_ref, vp_ref):
        ref[pl.ds(0, pad), :] = zeros
        ref[pl.ds(pad + SEQ, pad), :] = zeros

    def step(r0, carry):
        k = k_ref[pl.ds(r0, CHUNK), :].astype(F32)
        v = v_ref[pl.ds(r0, CHUNK), :].astype(F32)
        kn = (k * _inv_rms(k, low)) * gain_ref[...]
        if keep is not None:
            kn, v = _own_half(kn, keep), _own_half(v, keep)
        kp_ref[pl.ds(pad + r0, CHUNK), :] = kn
        vp_ref[pl.ds(pad + r0, CHUNK), :] = v
        return carry

    _chunks(step)


def _tiles(d, half_window, fn):
    w = Q_BLOCK + 2 * half_window
    length = SEQ // d
    n_blocks = length // Q_BLOCK
    col = lax.broadcasted_iota(jnp.int32, (1, w), 1)

    def step(it, carry):
        c, n = it // n_blocks, it % n_blocks
        start = c + (d * Q_BLOCK) * n
        if d == 1:
            start = pl.multiple_of(start, Q_BLOCK)
            q_rows, k_rows = pl.ds(start, Q_BLOCK), pl.ds(start, w)
        else:
            q_rows, k_rows = pl.ds(start, Q_BLOCK, stride=d), pl.ds(start, w, stride=d)
        t = n * Q_BLOCK - half_window + col
        edge = jnp.where((t < 0) | (t >= length), NEG_INF, 0.0)
        fn(q_rows, k_rows, edge)
        return carry

    lax.fori_loop(0, d * n_blocks, step, 0, unroll=TILE_UNROLL)


def _stack_heads(t, low):
    return jnp.concatenate([jnp.where(low, t, 0.0), jnp.where(low, 0.0, t)], axis=0).astype(BF16)


def _unstack_heads(t, low):
    return jnp.where(low, t[:Q_BLOCK], t[Q_BLOCK:])


def _per_head(pair):
    return jnp.concatenate([jnp.full((Q_BLOCK, 1), pair[0], F32), jnp.full((Q_BLOCK, 1), pair[1], F32)], axis=0)


def _fwd_tiles(qn_ref, kp_ref, vp_ref, bias_ref, emit, *, d, half_window, sinks=None):
    low = _low_half()
    w = Q_BLOCK + 2 * half_window
    sink = None if sinks is None else _per_head(sinks)

    def tile(q_rows, k_rows, edge):
        q2 = _stack_heads(qn_ref[q_rows, :], low)
        k = kp_ref[k_rows, :].astype(BF16)
        v1 = jnp.concatenate([vp_ref[k_rows, :], jnp.ones((w, LANES), F32)], axis=1).astype(BF16)
        s = _dot(q2, k, NT) + bias_ref[...].reshape(2 * Q_BLOCK, w) + edge
        m = jnp.max(s, axis=-1, keepdims=True)
        if sink is not None:
            m = jnp.maximum(m, sink)
        o = _dot(jnp.exp(s - m).astype(BF16), v1)
        l = o[:, LANES:]
        if sink is not None:
            l = l + jnp.exp(sink - m)
        emit(q_rows, _unstack_heads(o[:, :LANES] * (1.0 / l), low), _unstack_heads(m + jnp.log(l), low))

    _tiles(d, half_window, tile)


def _bwd_tiles(qn_ref, kp_ref, vp_ref, bias_ref, do_ref, lse_ref, delta_ref, dq_ref, dk_ref, dv_ref, ds_ref,
               *, d, half_window, sinks=None, dsink_ref=None):
    low = _low_half()
    w = Q_BLOCK + 2 * half_window
    sink = None if sinks is None else _per_head(sinks)

    def rows_of(t):
        return jnp.concatenate([t[:, 0:1], t[:, HEAD_DIM:HEAD_DIM + 1]], axis=0)

    def tile(q_rows, k_rows, edge):
        q2 = _stack_heads(qn_ref[q_rows, :], low)
        do2 = _stack_heads(do_ref[q_rows, :], low)
        k = kp_ref[k_rows, :].astype(BF16)
        v = vp_ref[k_rows, :].astype(BF16)
        lse = rows_of(lse_ref[q_rows, :])
        delta = rows_of(delta_ref[q_rows, :])
        p = jnp.exp(_dot(q2, k, NT) + bias_ref[...].reshape(2 * Q_BLOCK, w) + edge - lse)
        ds = p * (_dot(do2, v, NT) - delta)
        ds_ref[...] += ds.reshape(2, Q_BLOCK, w)
        if sink is not None:
            dsink_ref[...] += (-jnp.exp(sink - lse) * delta).reshape(2, Q_BLOCK, 1)
        dsb, pb = ds.astype(BF16), p.astype(BF16)
        dq_ref[q_rows, :] = _unstack_heads(_dot(dsb, k), low)
        dk_ref[k_rows, :] += _dot(dsb, q2, TN)
        dv_ref[k_rows, :] += _dot(pb, do2, TN)

    _tiles(d, half_window, tile)


def _prep_delta(do_ref, o_ref, delta_ref):
    low = _low_half()

    def step(r0, carry):
        delta_ref[pl.ds(r0, CHUNK), :] = _half_sum(do_ref[pl.ds(r0, CHUNK), :] * o_ref[pl.ds(r0, CHUNK), :], low)
        return carry

    _chunks(step)


def _norm_bwd(raw_ref, gain_ref, dn_ref, dn_offset, out_ref, scale):
    low = _low_half()

    def step(r0, dgain):
        t = raw_ref[pl.ds(r0, CHUNK), :].astype(F32)
        dn = dn_ref[pl.ds(dn_offset + r0, CHUNK), :]
        dth = dn * (gain_ref[...] * scale)
        sums = _half_sum(jnp.concatenate([t * t, dth * t], axis=0), low)
        r = lax.rsqrt(sums[:CHUNK] * (1.0 / HEAD_DIM) + EPS)
        th = t * r
        out_ref[pl.ds(r0, CHUNK), :] = (r * (dth - th * (r * sums[CHUNK:] * (1.0 / HEAD_DIM)))).astype(BF16)
        return dgain + jnp.sum(dn * th, axis=0, keepdims=True) * scale

    return _chunks(step, jnp.zeros((1, LANES), F32))


def _rows8(v):
    return jnp.broadcast_to(v, (8, v.shape[-1]))


A_W = Q_BLOCK + 2 * A_HALF_WINDOW
A_PAD = A_HALF_WINDOW


def _seq_block(col_fn):
    return pl.BlockSpec((SEQ, LANES), col_fn)


def _attn_a_fwd(qkv, gain_q, gain_k, bias, sink):
    def body(sink_ref, q_ref, k_ref, v_ref, gq_ref, gk_ref, bias_ref, o_ref, lse_ref, qn_ref, kp_ref, vp_ref):
        hp = pl.program_id(0)
        keep = (lax.broadcasted_iota(jnp.int32, (1, LANES), 1) // HEAD_DIM) == hp // 2
        _prep_q(q_ref, gq_ref, qn_ref)
        _prep_kv(k_ref, v_ref, gk_ref, kp_ref, vp_ref, A_PAD, keep)

        def emit(rows, out, lse):
            o_ref[rows, :] = out
            lse_ref[rows, :] = lse

        _fwd_tiles(qn_ref, kp_ref, vp_ref, bias_ref, emit, d=1, half_window=A_HALF_WINDOW,
                   sinks=(sink_ref[2 * hp], sink_ref[2 * hp + 1]))

    vec = pl.BlockSpec((1, LANES), lambda hp, s: (0, 0))
    return pl.pallas_call(
        body, name="attn_a_fwd",
        grid_spec=pltpu.PrefetchScalarGridSpec(
            num_scalar_prefetch=1, grid=(4,),
            in_specs=[_seq_block(lambda hp, s: (0, QA_BLK + hp)), _seq_block(lambda hp, s: (0, KA_BLK)),
                      _seq_block(lambda hp, s: (0, VA_BLK)), vec, vec,
                      pl.BlockSpec((None, 2, Q_BLOCK, A_W), lambda hp, s: (hp, 0, 0, 0))],
            out_specs=[_seq_block(lambda hp, s: (0, hp)), _seq_block(lambda hp, s: (0, hp))],
            scratch_shapes=[pltpu.VMEM((SEQ, LANES), F32), pltpu.VMEM((SEQ + 2 * A_PAD, LANES), F32),
                            pltpu.VMEM((SEQ + 2 * A_PAD, LANES), F32)]),
        out_shape=[jax.ShapeDtypeStruct((SEQ, 512), F32)] * 2,
        compiler_params=_params("arbitrary"),
    )(sink.reshape(8), qkv, qkv, qkv, gain_q, gain_k, bias)


def _attn_a_bwd(qkv, gain_q, gain_k, bias, sink, out, lse, d_out):
    def body(sink_ref, q_ref, k_ref, v_ref, gq_ref, gk_ref, bias_ref, o_ref, lse_ref, do_ref,
             dq_out, dkv_out, dgq_out, dgk_out, ds_out, dsink_out,
             qn_ref, kp_ref, vp_ref, delta_ref, dq_ref, dk_ref, dv_ref, dk_tot, dv_tot):
        hp = pl.program_id(0)
        kv_head = hp // 2
        keep = (lax.broadcasted_iota(jnp.int32, (1, LANES), 1) // HEAD_DIM) == kv_head
        _prep_q(q_ref, gq_ref, qn_ref)
        _prep_kv(k_ref, v_ref, gk_ref, kp_ref, vp_ref, A_PAD, keep)
        _prep_delta(do_ref, o_ref, delta_ref)
        dk_ref[...] = jnp.zeros_like(dk_ref)
        dv_ref[...] = jnp.zeros_like(dv_ref)
        ds_out[...] = jnp.zeros_like(ds_out)
        dsink_out[...] = jnp.zeros_like(dsink_out)

        @pl.when(hp == 0)
        def _():
            dk_tot[...] = jnp.zeros_like(dk_tot)
            dv_tot[...] = jnp.zeros_like(dv_tot)

        _bwd_tiles(qn_ref, kp_ref, vp_ref, bias_ref, do_ref, lse_ref, delta_ref, dq_ref, dk_ref, dv_ref, ds_out,
                   d=1, half_window=A_HALF_WINDOW, sinks=(sink_ref[2 * hp], sink_ref[2 * hp + 1]),
                   dsink_ref=dsink_out)
        dgq_out[...] = _rows8(_norm_bwd(q_ref, gq_ref, dq_ref, 0, dq_out, SCALE))

        def fold(r0, carry):
            rows = pl.ds(A_PAD + r0, CHUNK)
            for acc, tot in ((dk_ref, dk_tot), (dv_ref, dv_tot)):
                t = acc[rows, :]
                tot[pl.ds(r0, CHUNK), :] += jnp.where(keep, t + pltpu.roll(t, HEAD_DIM, 1), 0.0)
            return carry

        _chunks(fold)

        @pl.when(hp == 3)
        def _():
            dgk_out[...] = _rows8(_norm_bwd(k_ref, gk_ref, dk_tot, 0, dkv_out.at[0], 1.0))
            dkv_out[1] = dv_tot[...].astype(BF16)

    vec = pl.BlockSpec((1, LANES), lambda hp, s: (0, 0))
    seq_f32 = pltpu.VMEM((SEQ, LANES), F32)
    padded = pltpu.VMEM((SEQ + 2 * A_PAD, LANES), F32)
    return pl.pallas_call(
        body, name="attn_a_bwd",
        grid_spec=pltpu.PrefetchScalarGridSpec(
            num_scalar_prefetch=1, grid=(4,),
            in_specs=[_seq_block(lambda hp, s: (0, QA_BLK + hp)), _seq_block(lambda hp, s: (0, KA_BLK)),
                      _seq_block(lambda hp, s: (0, VA_BLK)), vec, vec,
                      pl.BlockSpec((None, 2, Q_BLOCK, A_W), lambda hp, s: (hp, 0, 0, 0)),
                      _seq_block(lambda hp, s: (0, hp)), _seq_block(lambda hp, s: (0, hp)),
                      _seq_block(lambda hp, s: (0, hp))],
            out_specs=[pl.BlockSpec((None, SEQ, LANES), lambda hp, s: (hp, 0, 0)),
                       pl.BlockSpec((2, SEQ, LANES), lambda hp, s: (0, 0, 0)),
                       pl.BlockSpec((None, 8, LANES), lambda hp, s: (hp, 0, 0)),
                       pl.BlockSpec((8, LANES), lambda hp, s: (0, 0)),
                       pl.BlockSpec((None, 2, Q_BLOCK, A_W), lambda hp, s: (hp, 0, 0, 0)),
                       pl.BlockSpec((None, 2, Q_BLOCK, 1), lambda hp, s: (hp, 0, 0, 0))],
            scratch_shapes=[seq_f32, padded, padded, seq_f32, seq_f32, padded, padded, seq_f32, seq_f32]),
        out_shape=[jax.ShapeDtypeStruct((4, SEQ, LANES), BF16), jax.ShapeDtypeStruct((2, SEQ, LANES), BF16),
                   jax.ShapeDtypeStruct((4, 8, LANES), F32), jax.ShapeDtypeStruct((8, LANES), F32),
           jax.ShapeDtypeStruct((4, 2, Q_BLOCK, A_W), F32), jax.ShapeDtypeStruct((4, 2, Q_BLOCK, 1), F32)],
        compiler_params=_params("arbitrary"),
    )(sink.reshape(8), qkv, qkv, qkv, gain_q, gain_k, bias, out, lse, d_out)


B_W = Q_BLOCK + 2 * B_HALF_WINDOW
B_PAD_MAX = B_HALF_WINDOW * B_DILATIONS[-1]


def _attn_b_fwd(qkv, gain_q, gain_k, bias):
    def body(q_ref, k_ref, v_ref, gq_ref, gk_ref, bias_ref, o_ref, lse_ref, qn_ref, kp_ref, vp_ref):
        g = pl.program_id(1)
        _prep_q(q_ref, gq_ref, qn_ref)

        def first(rows, out, lse):
            o_ref[rows, :] = out
            lse_ref[rows, :] = lse

        def combine(rows, out, lse):
            old = lse_ref[rows, :]
            new = jnp.maximum(old, lse) + jnp.log(1.0 + jnp.exp(-jnp.abs(old - lse)))
            o_ref[rows, :] = o_ref[rows, :] * jnp.exp(old - new) + out * jnp.exp(lse - new)
            lse_ref[rows, :] = new

        for gi, d in enumerate(B_DILATIONS):
            @pl.when(g == gi)
            def _():
                _prep_kv(k_ref, v_ref, gk_ref, kp_ref, vp_ref, B_HALF_WINDOW * d)
                _fwd_tiles(qn_ref, kp_ref, vp_ref, bias_ref, first if gi == 0 else combine,
                           d=d, half_window=B_HALF_WINDOW)

    vec = pl.BlockSpec((1, LANES), lambda hp, g: (0, 0))
    padded = pltpu.VMEM((SEQ + 2 * B_PAD_MAX, LANES), F32)
    return pl.pallas_call(
        body, name="attn_b_fwd", grid=(4, 3),
        in_specs=[_seq_block(lambda hp, g: (0, QB_BLK + 4 * g + hp)), _seq_block(lambda hp, g: (0, KB_BLK + 4 * g + hp)),
                  _seq_block(lambda hp, g: (0, VB_BLK + 4 * g + hp)), vec, vec,
                  pl.BlockSpec((None, 2, Q_BLOCK, B_W), lambda hp, g: (4 * g + hp, 0, 0, 0))],
        out_specs=[_seq_block(lambda hp, g: (0, hp)), _seq_block(lambda hp, g: (0, hp))],
        out_shape=[jax.ShapeDtypeStruct((SEQ, 512), F32)] * 2,
        scratch_shapes=[pltpu.VMEM((SEQ, LANES), F32), padded, padded],
        compiler_params=_params("arbitrary", "arbitrary"),
    )(qkv, qkv, qkv, gain_q, gain_k, bias)


def _attn_b_bwd(qkv, gain_q, gain_k, bias, out, lse, d_out):
    def body(q_ref, k_ref, v_ref, gq_ref, gk_ref, bias_ref, o_ref, lse_ref, do_ref,
             dqkv_out, dgq_out, dgk_out, ds_out,
             qn_ref, kp_ref, vp_ref, delta_ref, dq_ref, dk_ref, dv_ref):
        dq_out, dk_out, dv_out = dqkv_out.at[0], dqkv_out.at[1], dqkv_out.at[2]
        g = pl.program_id(1)
        _prep_q(q_ref, gq_ref, qn_ref)
        _prep_delta(do_ref, o_ref, delta_ref)
        dk_ref[...] = jnp.zeros_like(dk_ref)
        dv_ref[...] = jnp.zeros_like(dv_ref)
        ds_out[...] = jnp.zeros_like(ds_out)
        for gi, d in enumerate(B_DILATIONS):
            @pl.when(g == gi)
            def _():
                pad = B_HALF_WINDOW * d
                _prep_kv(k_ref, v_ref, gk_ref, kp_ref, vp_ref, pad)
                _bwd_tiles(qn_ref, kp_ref, vp_ref, bias_ref, do_ref, lse_ref, delta_ref, dq_ref, dk_ref, dv_ref,
                           ds_out, d=d, half_window=B_HALF_WINDOW)
                dgk_out[...] = _rows8(_norm_bwd(k_ref, gk_ref, dk_ref, pad, dk_out, 1.0))
                dv_out[...] = dv_ref[pl.ds(pad, SEQ), :].astype(BF16)
        dgq_out[...] = _rows8(_norm_bwd(q_ref, gq_ref, dq_ref, 0, dq_out, SCALE))

    vec = pl.BlockSpec((1, LANES), lambda hp, g: (0, 0))
    seq_f32 = pltpu.VMEM((SEQ, LANES), F32)
    padded = pltpu.VMEM((SEQ + 2 * B_PAD_MAX, LANES), F32)
    part = pl.BlockSpec((None, 8, LANES), lambda hp, g: (4 * g + hp, 0, 0))
    return pl.pallas_call(
        body, name="attn_b_bwd", grid=(4, 3),
        in_specs=[_seq_block(lambda hp, g: (0, QB_BLK + 4 * g + hp)), _seq_block(lambda hp, g: (0, KB_BLK + 4 * g + hp)),
                  _seq_block(lambda hp, g: (0, VB_BLK + 4 * g + hp)), vec, vec,
                  pl.BlockSpec((None, 2, Q_BLOCK, B_W), lambda hp, g: (4 * g + hp, 0, 0, 0)),
                  _seq_block(lambda hp, g: (0, hp)), _seq_block(lambda hp, g: (0, hp)), _seq_block(lambda hp, g: (0, hp))],
        out_specs=[pl.BlockSpec((3, None, SEQ, LANES), lambda hp, g: (0, 4 * g + hp, 0, 0)),
                   part, part, pl.BlockSpec((None, 2, Q_BLOCK, B_W), lambda hp, g: (4 * g + hp, 0, 0, 0))],
        out_shape=[jax.ShapeDtypeStruct((3, 12, SEQ, LANES), BF16)]
        + [jax.ShapeDtypeStruct((12, 8, LANES), F32)] * 2 + [jax.ShapeDtypeStruct((12, 2, Q_BLOCK, B_W), F32)],
        scratch_shapes=[seq_f32, padded, padded, seq_f32, seq_f32, padded, padded],
        compiler_params=_params("arbitrary", "arbitrary"),
    )(qkv, qkv, qkv, gain_q, gain_k, bias, out, lse, d_out)


def _sigmoid(t):
    return 1.0 / (1.0 + jnp.exp(-t))


def _middle(out_a, out_b, gates, x, target, w_a, w_b, w_out, b_merge):
    tm = 256
    n_steps = SEQ // tm

    def body(oa_ref, ob_ref, g_ref, x_ref, t_ref, wa_ref, wb_ref, wo_ref, bm_ref,
             dy_ref, dg_ref, doa_ref, dob_ref, dwa_ref, dwb_ref, dwo_ref, dbm_ref, sq_ref):
        @pl.when(pl.program_id(0) == 0)
        def _():
            for ref in (dwa_ref, dwb_ref, dwo_ref, dbm_ref, sq_ref):
                ref[...] = jnp.zeros_like(ref)

        gate_a, gate_b = g_ref[:, 0:512], g_ref[:, 512:1024]
        sig_a, sig_b = _sigmoid(gate_a), _sigmoid(gate_b)
        silu_a, silu_b = gate_a * sig_a, gate_b * sig_b
        oa, ob = oa_ref[...], ob_ref[...]
        ya, yb = (oa * silu_a).astype(BF16), (ob * silu_b).astype(BF16)
        br_a, br_b = _dot(ya, wa_ref[...]), _dot(yb, wb_ref[...])
        m0 = _sigmoid(g_ref[:, 1024:2048] + bm_ref[0:1, :])
        m1 = _sigmoid(g_ref[:, 2048:3072] + bm_ref[1:2, :])
        merged = (m0 * br_a + m1 * br_b).astype(BF16)
        err = (x_ref[...] + _dot(merged, wo_ref[...])) - t_ref[...]
        sq_ref[...] += jnp.sum(err * err, axis=0, keepdims=True)

        dy = err * (1.0 / D_MODEL)
        dy_ref[...] = dy
        dyb = dy.astype(BF16)
        dmerged = _dot(dyb, wo_ref[...], NT)
        dwo_ref[...] += _dot(merged, dyb, TN)
        dbr_a, dbr_b = (dmerged * m0).astype(BF16), (dmerged * m1).astype(BF16)
        dm0 = (dmerged * br_a) * (m0 * (1.0 - m0))
        dm1 = (dmerged * br_b) * (m1 * (1.0 - m1))
        dbm_ref[0:1, :] += jnp.sum(dm0, axis=0, keepdims=True)
        dbm_ref[1:2, :] += jnp.sum(dm1, axis=0, keepdims=True)
        for s in range(N_CHIPS):
            cols = slice(256 * s, 256 * (s + 1))
            dwa_ref[s] += _dot(ya, dbr_a[:, cols], TN)
            dwb_ref[s] += _dot(yb, dbr_b[:, cols], TN)
        dya, dyb_ = _dot(dbr_a, wa_ref[...], NT), _dot(dbr_b, wb_ref[...], NT)
        doa_ref[...] = dya * silu_a
        dob_ref[...] = dyb_ * silu_b
        d_gates = (((dya * oa) * (sig_a * (1.0 + gate_a * (1.0 - sig_a)))).astype(BF16),
                   ((dyb_ * ob) * (sig_b * (1.0 + gate_b * (1.0 - sig_b)))).astype(BF16),
                   dm0.astype(BF16), dm1.astype(BF16))
        blk = 0
        for part in d_gates:
            for c0 in range(0, part.shape[1], 256):
                dg_ref[blk] = part[:, c0:c0 + 256]
                blk += 1

    def rows(width):
        return pl.BlockSpec((tm, width), lambda i: (i, 0))

    def whole(*shape):
        return pl.BlockSpec(shape, lambda i: (0,) * len(shape))

    return pl.pallas_call(
        body, name="middle", grid=(n_steps,),
        in_specs=[rows(512), rows(512), rows(GATE_WIDTH), rows(D_MODEL), rows(D_MODEL),
                  whole(512, D_MODEL), whole(512, D_MODEL), whole(D_MODEL, D_MODEL), whole(2, D_MODEL)],
        out_specs=[rows(D_MODEL), pl.BlockSpec((GATE_WIDTH // 256, tm, 256), lambda i: (0, i, 0)), rows(512), rows(512),
                   whole(N_CHIPS, 512, 256), whole(N_CHIPS, 512, 256), whole(D_MODEL, D_MODEL),
                   whole(2, D_MODEL), whole(1, D_MODEL)],
        out_shape=[jax.ShapeDtypeStruct((SEQ, D_MODEL), F32), jax.ShapeDtypeStruct((GATE_WIDTH // 256, SEQ, 256), BF16),
                   jax.ShapeDtypeStruct((SEQ, 512), F32), jax.ShapeDtypeStruct((SEQ, 512), F32),
                   jax.ShapeDtypeStruct((N_CHIPS, 512, 256), F32), jax.ShapeDtypeStruct((N_CHIPS, 512, 256), F32),
                   jax.ShapeDtypeStruct((D_MODEL, D_MODEL), F32), jax.ShapeDtypeStruct((2, D_MODEL), F32),
                   jax.ShapeDtypeStruct((1, D_MODEL), F32)],
        compiler_params=_params("arbitrary"),
    )(out_a, out_b, gates, x, target, w_a, w_b, w_out, b_merge)


def _which(j, edges, fns):
    lo = 0
    for hi, fn in zip(edges, fns):
        pl.when((j >= lo) & (j < hi))(fn)
        lo = hi


def _d_w_in(d_proj, h):
    plan, step, width = [], 0, 0
    for p in d_proj:
        total = p.shape[0] * p.shape[2]
        if width + total <= W_BLOCK:
            plan.append((p.shape[0], step, 1))
            width += total
            if width == W_BLOCK:
                step, width = step + 1, 0
        else:
            assert width == 0 and total % W_BLOCK == 0
            plan.append((W_BLOCK // p.shape[2], step, total // W_BLOCK))
            step += total // W_BLOCK
    assert width == 0 and step == IN_WIDTH // W_BLOCK
    firsts = sorted({first for _, first, _ in plan})
    edges = firsts[1:] + [step]
    halves = 2

    def body(*refs):
        pieces, h_ref, o_ref = refs[:-2], refs[-2], refs[-1]
        k = pl.program_id(1)

        def emit(group):
            def fn():
                cols = jnp.concatenate([ref[b] for ref in group for b in range(ref.shape[0])], axis=1)
                term = _dot(cols, h_ref[...], TN)

                @pl.when(k == 0)
                def _():
                    o_ref[...] = term

                @pl.when(k > 0)
                def _():
                    o_ref[...] += term
            return fn

        groups = [[ref for ref, (_, first, _) in zip(pieces, plan) if first == f] for f in firsts]
        _which(pl.program_id(0), edges, [emit(group) for group in groups])

    def cols_spec(piece, n, first, steps):
        def index(j, k):
            return jnp.clip(j - first, 0, steps - 1), jnp.where((j >= first) & (j < first + steps), k, 0), 0
        return pl.BlockSpec((n, SEQ // halves, piece.shape[2]), index)

    return pl.pallas_call(
        body, name="d_w_in", grid=(step, halves),
        in_specs=[cols_spec(p, *pl_) for p, pl_ in zip(d_proj, plan)]
        + [pl.BlockSpec((SEQ // halves, D_MODEL), lambda j, k: (k, 0))],
        out_specs=pl.BlockSpec((W_BLOCK, D_MODEL), lambda j, k: (j, 0)),
        out_shape=jax.ShapeDtypeStruct((IN_WIDTH, D_MODEL), F32),
        compiler_params=_params("arbitrary", "arbitrary"),
    )(*d_proj, h)


def _d_x(d_proj, w_t, x, gain, dy, chip_sums):
    tm = 256
    n_steps = SEQ // tm
    n_w = IN_WIDTH // W_BLOCK
    n_p, n_s = len(d_proj), len(chip_sums)

    def body(*refs):
        pieces, w_refs = refs[:n_p], refs[n_p:n_p + n_w]
        x_ref, g_ref, dy_ref = refs[n_p + n_w:n_p + n_w + 3]
        q_refs = refs[n_p + n_w + 3:n_p + n_w + 3 + n_s]
        dx_ref, dgain_ref = refs[n_p + n_w + 3 + n_s:n_p + n_w + 5 + n_s]
        o_refs = refs[n_p + n_w + 5 + n_s:n_p + n_w + 5 + 2 * n_s]
        send_sems, recv_sems = refs[n_p + n_w + 5 + 2 * n_s:] if n_s else (None, None)

        @pl.when(pl.program_id(0) == 0)
        def _():
            dgain_ref[...] = jnp.zeros_like(dgain_ref)
            if n_s:
                for cp in _scatter_copies(q_refs, o_refs, send_sems, recv_sems):
                    cp.start()

        blocks = [(piece, k) for piece in pieces for k in range(piece.shape[0])]
        dh, group, width, blk = None, [], 0, 0
        for piece, k in blocks:
            group.append(piece[k])
            width += piece.shape[2]
            if width == W_BLOCK:
                term = _dot(jnp.concatenate(group, axis=1), w_refs[blk][...])
                dh = term if dh is None else dh + term
                group, width, blk = [], 0, blk + 1
        assert not group and blk == n_w
        xf = x_ref[...]
        r = lax.rsqrt(jnp.mean(xf * xf, axis=-1, keepdims=True) + EPS)
        xh = xf * r
        dxh = dh * g_ref[...]
        dx_ref[...] = r * (dxh - xh * jnp.mean(dxh * xh, axis=-1, keepdims=True)) + dy_ref[...]
        dgain_ref[...] += _rows8(jnp.sum(dh * xh, axis=0, keepdims=True))

        if n_s:
            @pl.when(pl.program_id(0) == n_steps - 1)
            def _():
                for cp in _scatter_copies(q_refs, o_refs, send_sems, recv_sems):
                    cp.wait()

    row = pl.BlockSpec((tm, D_MODEL), lambda i: (i, 0))
    res = pl.pallas_call(
        body, name="d_x", grid=(n_steps,),
        in_specs=[pl.BlockSpec((p.shape[0], tm, p.shape[2]), lambda i: (0, i, 0)) for p in d_proj] + _w_blocks(0, n_w)
        + [row, pl.BlockSpec((1, D_MODEL), lambda i: (0, 0)), row] + [ANY] * n_s,
        out_specs=[row, pl.BlockSpec((8, D_MODEL), lambda i: (0, 0))] + [ANY] * n_s,
        out_shape=[jax.ShapeDtypeStruct((SEQ, D_MODEL), F32), jax.ShapeDtypeStruct((8, D_MODEL), F32)]
        + [jax.ShapeDtypeStruct((3,) + q.shape[1:], BF16) for q in chip_sums],
        scratch_shapes=[pltpu.SemaphoreType.DMA((3 * n_s,)), pltpu.SemaphoreType.DMA((3 * n_s,))] if n_s else [],
        compiler_params=_params("arbitrary"),
    )(*d_proj, *([w_t] * n_w), x, gain, dy, *chip_sums)
    return res[0], res[1], res[2:]


def _my_place():
    x, y, c = lax.axis_index("x"), lax.axis_index("y"), lax.axis_index("c")
    return jnp.stack([2 * x + y, c]).astype(jnp.int32)


def _half_rows(ref, half):
    rows = ref.shape[-2] // 2
    idx = (slice(None),) * (len(ref.shape) - 2) + (pl.ds(pl.multiple_of(half * rows, 16), rows), slice(None))
    return ref.at[idx]


def _swap_halves(grads):
    n = len(grads)

    def body(*refs):
        g_refs, o_refs, (send_sems, recv_sems) = refs[:n], refs[n:2 * n], refs[2 * n:]
        x, y, c = lax.axis_index("x"), lax.axis_index("y"), lax.axis_index("c")
        copies = [pltpu.make_async_remote_copy(src_ref=_half_rows(g, 1 - c), dst_ref=o, send_sem=send_sems.at[k],
                                               recv_sem=recv_sems.at[k], device_id=(x, y, 1 - c), device_id_type=MESH)
                  for k, (g, o) in enumerate(zip(g_refs, o_refs))]
        for cp in copies:
            cp.start()
        for cp in copies:
            cp.wait()

    return pl.pallas_call(
        body, name="reduce_swap_halves", in_specs=[ANY] * n, out_specs=[ANY] * n,
        out_shape=[jax.ShapeDtypeStruct((N_CHIPS, g.shape[1] // 2, D_MODEL), F32) for g in grads],
        scratch_shapes=[pltpu.SemaphoreType.DMA((n,)), pltpu.SemaphoreType.DMA((n,))],
    )(*grads)


def _row_tile(rows):
    return max(t for t in range(16, 385, 16) if rows % t == 0)


def _add_halves(place, grads, theirs, name):
    half = theirs.shape[1]
    tr = _row_tile(half)
    n = half // tr

    def body(place_ref, g_ref, t_ref, o_ref):
        o_ref[...] = (g_ref[...] + t_ref[...]).astype(BF16)

    return pl.pallas_call(
        body, name=name,
        grid_spec=pltpu.PrefetchScalarGridSpec(
            num_scalar_prefetch=1, grid=(N_CHIPS, n),
            in_specs=[pl.BlockSpec((None, tr, D_MODEL), lambda s, i, p: (s, p[1] * n + i, 0)),
                      pl.BlockSpec((None, tr, D_MODEL), lambda s, i, p: (s, i, 0))],
            out_specs=pl.BlockSpec((None, tr, D_MODEL), lambda s, i, p: (s, i, 0))),
        out_shape=jax.ShapeDtypeStruct((N_CHIPS, half, D_MODEL), BF16),
        compiler_params=_params("arbitrary", "arbitrary"),
    )(place, grads, theirs)


def _scatter_copies(q_refs, o_refs, send_sems, recv_sems):
    x, y, c = lax.axis_index("x"), lax.axis_index("y"), lax.axis_index("c")
    chips = [(1 - x, y), (x, 1 - y), (1 - x, 1 - y)]
    return [pltpu.make_async_remote_copy(src_ref=q.at[2 * cx + cy], dst_ref=o.at[j],
                                         send_sem=send_sems.at[3 * k + j], recv_sem=recv_sems.at[3 * k + j],
                                         device_id=(cx, cy, c), device_id_type=MESH)
            for k, (q, o) in enumerate(zip(q_refs, o_refs)) for j, (cx, cy) in enumerate(chips)]


def _add_chips(place, chip_sums, others, name):
    half = others.shape[1]
    tr = _row_tile(half)
    n = half // tr

    def body(place_ref, q_ref, o_ref, r_ref):
        acc = q_ref[...].astype(F32)
        for j in range(3):
            acc = acc + o_ref[j].astype(F32)
        r_ref[...] = acc

    return pl.pallas_call(
        body, name=name,
        grid_spec=pltpu.PrefetchScalarGridSpec(
            num_scalar_prefetch=1, grid=(n,),
            in_specs=[pl.BlockSpec((None, tr, D_MODEL), lambda i, p: (p[0], i, 0)),
                      pl.BlockSpec((3, tr, D_MODEL), lambda i, p: (0, i, 0))],
            out_specs=pl.BlockSpec((tr, D_MODEL), lambda i, p: (p[1] * n + i, 0))),
        out_shape=jax.ShapeDtypeStruct((2 * half, D_MODEL), F32),
        compiler_params=_params("arbitrary"),
    )(place, chip_sums, others)


def _join_halves(shards):
    n = len(shards)

    def body(*refs):
        o_refs, (send_sems, recv_sems) = refs[n:2 * n], refs[2 * n:]
        x, y, c = lax.axis_index("x"), lax.axis_index("y"), lax.axis_index("c")

        def copy(k, rows):
            return pltpu.make_async_remote_copy(src_ref=rows, dst_ref=rows, send_sem=send_sems.at[k],
                                                recv_sem=recv_sems.at[k], device_id=(x, y, 1 - c), device_id_type=MESH)

        sends = [copy(k, _half_rows(o, c)) for k, o in enumerate(o_refs)]
        for cp in sends:
            cp.start()
        for k, o in enumerate(o_refs):
            copy(k, _half_rows(o, 1 - c)).wait_recv()
        for cp in sends:
            cp.wait_send()

    return pl.pallas_call(
        body, name="reduce_join_halves", in_specs=[ANY] * n, out_specs=[ANY] * n,
        out_shape=[jax.ShapeDtypeStruct(s.shape, F32) for s in shards],
        input_output_aliases={k: k for k in range(n)},
        scratch_shapes=[pltpu.SemaphoreType.DMA((n,)), pltpu.SemaphoreType.DMA((n,))],
    )(*shards)


def _gather_small(block):
    rows = block.shape[0]

    def body(b_ref, o_ref, send_sems, recv_sems, local_sem):
        x, y, c = lax.axis_index("x"), lax.axis_index("y"), lax.axis_index("c")
        me, sibling = (x, y, c), (x, y, 1 - c)
        chips = [(1 - x, y), (x, 1 - y), (1 - x, 1 - y)]

        def at(px, py, pc):
            return o_ref.at[pl.ds(pl.multiple_of((4 * px + 2 * py + pc) * rows, 8), rows), :]

        def copy(k, block_of, to, src=None):
            return pltpu.make_async_remote_copy(src_ref=at(*block_of) if src is None else src, dst_ref=at(*block_of),
                                                send_sem=send_sems.at[k], recv_sem=recv_sems.at[k],
                                                device_id=to, device_id_type=MESH)

        mine = pltpu.make_async_copy(b_ref, at(*me), local_sem)
        mine.start()
        first = [copy(0, me, sibling, src=b_ref)]
        first += [copy(1 + j, me, (*chip, c), src=b_ref) for j, chip in enumerate(chips)]
        for cp in first:
            cp.start()
        passed = [copy(4 + j, (*chip, c), sibling) for j, chip in enumerate(chips)]
        for j, chip in enumerate(chips):
            copy(1 + j, (*chip, c), me).wait_recv()
            passed[j].start()
        copy(0, sibling, me).wait_recv()
        for j, chip in enumerate(chips):
            copy(4 + j, (*chip, 1 - c), me).wait_recv()
        for cp in first + passed:
            cp.wait_send()
        mine.wait()

    return pl.pallas_call(
        body, name="gather_small_grads",
        in_specs=[pl.BlockSpec(memory_space=pltpu.VMEM)], out_specs=pl.BlockSpec(memory_space=pltpu.VMEM),
        out_shape=jax.ShapeDtypeStruct((8 * rows, D_MODEL), F32),
        scratch_shapes=[pltpu.SemaphoreType.DMA((7,)), pltpu.SemaphoreType.DMA((7,)), pltpu.SemaphoreType.DMA],
    )(block)


def _sum_devices(blocks):
    def body(b_ref, o_ref):
        acc = b_ref[0:8, :]
        for dev in range(1, 8):
            acc = acc + b_ref[8 * dev:8 * dev + 8, :]
        o_ref[...] = acc

    return pl.pallas_call(body, name="sum_small_grads", out_shape=jax.ShapeDtypeStruct((8, D_MODEL), F32))(blocks)


def _adamw_math(w, g, m, v):
    m = ADAM_B1 * m + (1.0 - ADAM_B1) * g
    v = ADAM_B2 * v + (1.0 - ADAM_B2) * (g * g)
    m_hat = m / (1.0 - ADAM_B1 ** ADAM_STEP)
    v_hat = v / (1.0 - ADAM_B2 ** ADAM_STEP)
    return -ADAM_LR * (m_hat / (jnp.sqrt(v_hat) + ADAM_EPS) + ADAM_WD * w), m, v


def _adamw(w, g, m, v, name):
    r, c = w.shape
    tr = 128 if r % 128 == 0 else r

    def body(w_ref, g_ref, m_ref, v_ref, d_ref, nm_ref, nv_ref):
        d_ref[...], nm_ref[...], nv_ref[...] = _adamw_math(w_ref[...], g_ref[...], m_ref[...], v_ref[...])

    spec = pl.BlockSpec((tr, c), lambda i: (i, 0))
    return pl.pallas_call(
        body, name=name, grid=(r // tr,), in_specs=[spec] * 4, out_specs=[spec] * 3,
        out_shape=[jax.ShapeDtypeStruct((r, c), F32)] * 3, compiler_params=_params("arbitrary"),
    )(w, g, m, v)


def _adamw_small(ws, gs, ms, vs):
    n = len(ws)

    def body(*refs):
        ins, outs = refs[:4 * n], refs[4 * n:]
        for k in range(n):
            d, m, v = _adamw_math(ins[k][...], ins[n + k][...], ins[2 * n + k][...], ins[3 * n + k][...])
            outs[k][...], outs[n + k][...], outs[2 * n + k][...] = d, m, v

    shapes = [jax.ShapeDtypeStruct(w.shape, F32) for w in ws]
    res = pl.pallas_call(body, name="adamw_small", out_shape=shapes * 3)(*ws, *gs, *ms, *vs)
    return res[:n], res[n:2 * n], res[2 * n:]


def _fold_heads(partials):
    t = jnp.sum(partials[:, 0, :], axis=0)
    return (t[:HEAD_DIM] + t[HEAD_DIM:]).reshape(1, HEAD_DIM)


def _local_step(x, target, norm_gain, w_t, w_a, w_b, w_o, b_m, q_norm_a, k_norm_a, q_norm_b, k_norm_b, sink_a,
                rel_bias, start_reduce=None):
    two = lambda gain: jnp.concatenate([gain, gain], axis=1)
    bias_a = _bias_table(rel_bias[:, :8], A_HALF_WINDOW, 1)
    bias_b = jnp.concatenate([_bias_table(rel_bias[:, 8 + 8 * g:16 + 8 * g], B_HALF_WINDOW, d)
                              for g, d in enumerate(B_DILATIONS)], axis=0)

    qkv, h = _in_proj(x, norm_gain, w_t, 0, QKV_WIDTH // W_BLOCK, BF16, "in_proj_qkv")
    gates, _ = _in_proj(x, norm_gain, w_t, QKV_WIDTH // W_BLOCK, GATE_WIDTH // W_BLOCK, F32, "in_proj_gates")
    out_a, lse_a = _attn_a_fwd(qkv, two(q_norm_a), two(k_norm_a), bias_a, sink_a)
    out_b, lse_b = _attn_b_fwd(qkv, two(q_norm_b), two(k_norm_b), bias_b)

    dy, dgates, d_out_a, d_out_b, d_wa, d_wb, d_wo, d_bm, sq = _middle(
        out_a, out_b, gates, x, target, w_a, w_b, w_o, b_m)
    loss = (0.5 / D_MODEL) * jnp.sum(sq)

    dq_a, dkv_a, dgq_a, dgk_a, ds_a, dsink = _attn_a_bwd(
        qkv, two(q_norm_a), two(k_norm_a), bias_a, sink_a, out_a, lse_a, d_out_a)
    dqkv_b, dgq_b, dgk_b, ds_b = _attn_b_bwd(
        qkv, two(q_norm_b), two(k_norm_b), bias_b, out_b, lse_b, d_out_b)
    d_proj = (dq_a, dkv_a, dqkv_b.reshape(36, SEQ, LANES), dgates)

    d_bm_rows = jnp.pad(d_bm.reshape(2, N_CHIPS, 256).transpose(1, 0, 2),
                        ((0, 0), (0, REST_ROWS - 514), (0, D_MODEL - 256)))
    rest = jnp.concatenate([d_wo.reshape(N_CHIPS, 256, D_MODEL), d_wa.reshape(N_CHIPS, 128, D_MODEL),
                            d_wb.reshape(N_CHIPS, 128, D_MODEL), d_bm_rows], axis=1)
    grads = [_d_w_in(d_proj, h).reshape(N_CHIPS, W_IN_SHARD, D_MODEL), rest]
    chip_sums = start_reduce(grads) if start_reduce is not None else []
    grad_x, d_gain, others = _d_x(d_proj, w_t, x, norm_gain, dy, chip_sums)

    d_rel = jnp.concatenate(
        [_bias_grad(ds_a, A_HALF_WINDOW, 1)]
        + [_bias_grad(ds_b[4 * g:4 * g + 4], B_HALF_WINDOW, d) for g, d in enumerate(B_DILATIONS)], axis=1)
    d_sink = jnp.sum(dsink, axis=(2, 3)).reshape(1, 8)
    dgk_a_row = dgk_a[0]
    small = jnp.zeros((8, D_MODEL), F32)
    small = small.at[0].set(d_gain[0])
    small = small.at[1].set(d_rel.reshape(-1))
    misc = jnp.concatenate([_fold_heads(dgq_a), (dgk_a_row[:HEAD_DIM] + dgk_a_row[HEAD_DIM:]).reshape(1, HEAD_DIM),
                            _fold_heads(dgq_b), _fold_heads(dgk_b), d_sink], axis=1)
    small = small.at[2, :264].set(misc[0])

    return loss, grad_x, grads, small, chip_sums, others


def _unpack_weights(w_t_all, small_all):
    sm = small_all.reshape(N_CHIPS, SMALL_ROWS, D_MODEL)
    w_o = sm[:, 0:256].reshape(D_MODEL, D_MODEL)
    w_a = sm[:, 256:384].reshape(N_CHIPS, 512, 256).transpose(1, 0, 2).reshape(512, D_MODEL)
    w_b = sm[:, 384:512].reshape(N_CHIPS, 512, 256).transpose(1, 0, 2).reshape(512, D_MODEL)
    b_m = lax.bitcast_convert_type(sm[:, 512].reshape(N_CHIPS, 2, 256, 2), F32)
    return w_t_all, w_a, w_b, w_o, b_m.transpose(1, 0, 2).reshape(2, D_MODEL)


def _pack_small_weights(w_branch_a, w_branch_b, b_merge, w_out):
    b_m = jnp.pad(lax.bitcast_convert_type(b_merge, BF16).reshape(1, D_MODEL), ((0, SMALL_ROWS - 513), (0, 0)))
    return jnp.concatenate([w_out.astype(BF16), w_branch_a.astype(BF16).reshape(128, D_MODEL),
                            w_branch_b.astype(BF16).reshape(128, D_MODEL), b_m], axis=0)


def kernel(x, norm_gain, w_in, q_norm_a, k_norm_a, q_norm_b, k_norm_b, sink_a, rel_bias, w_branch_a, w_branch_b, b_merge, w_out, loss_target, m_norm_gain, m_w_in, m_q_norm_a, m_k_norm_a, m_q_norm_b, m_k_norm_b, m_sink_a, m_rel_bias, m_w_branch_a, m_w_branch_b, m_b_merge, m_w_out, v_norm_gain, v_w_in, v_q_norm_a, v_k_norm_a, v_q_norm_b, v_k_norm_b, v_sink_a, v_rel_bias, v_w_branch_a, v_w_branch_b, v_b_merge, v_w_out):
    wt_shard = _transpose_cast(w_in, BF16, "w_in_transpose")
    w_t, w_a, w_b, w_o, b_m = _unpack_weights(
        *_gather_weights(wt_shard, _pack_small_weights(w_branch_a[0], w_branch_b[0], b_merge[0], w_out[0])))

    place = _my_place()
    names = ("w_in", "rest")

    def start_reduce(grads):
        return [_add_halves(place, g, t, "reduce_add_halves_" + n) for g, t, n in zip(grads, _swap_halves(grads), names)]

    loss_part, grad_x, _, small, chip_sums, others = _local_step(
        x[0], loss_target[0], norm_gain, w_t, w_a, w_b, w_o, b_m, q_norm_a, k_norm_a, q_norm_b, k_norm_b,
        sink_a, rel_bias, start_reduce)

    g_wt, g_rest = _join_halves([_add_chips(place, q, o, "reduce_add_chips_" + n)
                                 for q, o, n in zip(chip_sums, others, names)])
    small = _sum_devices(_gather_small(small.at[3, 0].set(loss_part)))
    loss = small[3, 0]

    g_w_in = _transpose_cast(g_wt, F32, "grad_w_in_transpose")
    g_w_out = g_rest[0:256]
    g_w_a = g_rest[256:384].reshape(512, 256)
    g_w_b = g_rest[384:512].reshape(512, 256)
    g_b_merge = g_rest[512:514, :256]
    g_norm_gain = small[0:1]
    g_rel_bias = small[1].reshape(N_BUCKETS, N_BUCKETS)
    g_q_a, g_k_a, g_q_b, g_k_b = (small[2:3, 64 * k:64 * k + 64] for k in range(4))
    g_sink = small[2:3, 256:264]

    big_names = (("w_in", w_in, g_w_in, m_w_in, v_w_in),
                 ("w_branch_a", w_branch_a, g_w_a, m_w_branch_a, v_w_branch_a),
                 ("w_branch_b", w_branch_b, g_w_b, m_w_branch_b, v_w_branch_b),
                 ("w_out", w_out, g_w_out, m_w_out, v_w_out))
    upd = {name: (g,) + tuple(_adamw(w[0], g, m[0], v[0], "adamw_" + name)) for name, w, g, m, v in big_names}
    small_names = ("norm_gain", "q_norm_a", "k_norm_a", "q_norm_b", "k_norm_b", "sink_a", "rel_bias", "b_merge")
    ws = [norm_gain, q_norm_a, k_norm_a, q_norm_b, k_norm_b, sink_a, rel_bias, b_merge[0]]
    gs = [g_norm_gain, g_q_a, g_k_a, g_q_b, g_k_b, g_sink, g_rel_bias, g_b_merge]
    ms = [m_norm_gain, m_q_norm_a, m_k_norm_a, m_q_norm_b, m_k_norm_b, m_sink_a, m_rel_bias, m_b_merge[0]]
    vs = [v_norm_gain, v_q_norm_a, v_k_norm_a, v_q_norm_b, v_k_norm_b, v_sink_a, v_rel_bias, v_b_merge[0]]
    ds, nms, nvs = _adamw_small(ws, gs, ms, vs)
    for k, name in enumerate(small_names):
        upd[name] = (gs[k], ds[k], nms[k], nvs[k])

    order = ("norm_gain", "w_in", "q_norm_a", "k_norm_a", "q_norm_b", "k_norm_b", "sink_a", "rel_bias",
             "w_branch_a", "w_branch_b", "b_merge", "w_out")
    lead = {"w_in", "w_branch_a", "w_branch_b", "b_merge", "w_out"}
    outs = [loss, grad_x[None]]
    for part in range(4):
        outs += [upd[name][part][None] if name in lead else upd[name][part] for name in order]
    return tuple(outs)
```

```python
import math

import numpy as np
import jax
import jax.numpy as jnp
from jax import lax
from jax.experimental import pallas as pl
from jax.experimental.pallas import tpu as pltpu

F32 = jnp.float32
BF16 = jnp.bfloat16

SEQ = 4096
D_MODEL = 1024
HEAD_DIM = 64
LANES = 128
EPS = 1e-6
NEG_INF = -1e30
SCALE = HEAD_DIM ** -0.5
N_BUCKETS = 32
MAX_DISTANCE = 1024
N_CHIPS = 4

A_HALF_WINDOW = 128
B_HALF_WINDOW = 64
B_DILATIONS = (1, 4, 16)
Q_BLOCK = 128

QKV_WIDTH = 5376
GATE_WIDTH = 3072
QA_BLK, KA_BLK, VA_BLK = 0, 4, 5
QB_BLK, KB_BLK, VB_BLK = 6, 18, 30
IN_WIDTH = QKV_WIDTH + GATE_WIDTH
W_IN_SHARD = IN_WIDTH // N_CHIPS

SMALL_ROWS = 544
REST_ROWS = 544

ADAM_LR = 0.001
ADAM_B1 = 0.9
ADAM_B2 = 0.999
ADAM_EPS = 1e-08
ADAM_WD = 0.01
ADAM_STEP = 10

VMEM_LIMIT = 56 * 1024 * 1024

NT = (((1,), (1,)), ((), ()))
TN = (((0,), (0,)), ((), ()))
MESH = pl.DeviceIdType.MESH
ANY = pl.BlockSpec(memory_space=pl.ANY)


def _dot(a, b, dims=None):
    if dims is None:
        return jnp.dot(a, b, preferred_element_type=F32)
    return lax.dot_general(a, b, dims, preferred_element_type=F32)


def _params(*semantics):
    return pltpu.CompilerParams(dimension_semantics=semantics or None, vmem_limit_bytes=VMEM_LIMIT)


def _bucket_onehot(half_window, stride):
    w = Q_BLOCK + 2 * half_window
    rel = (np.arange(w)[None, :] - half_window - np.arange(Q_BLOCK)[:, None])
    band = np.abs(rel) <= half_window
    rel = rel * stride
    half, max_exact = N_BUCKETS // 2, N_BUCKETS // 4
    n = np.abs(rel)
    nf = np.maximum(n, max_exact).astype(np.float32)
    large = max_exact + (np.log(nf / np.float32(max_exact)) / np.float32(math.log(MAX_DISTANCE / max_exact))
                         * np.float32(half - max_exact)).astype(np.int32)
    large = np.minimum(large, half - 1)
    bucket = (rel > 0).astype(np.int32) * half + np.where(n < max_exact, n, large)
    onehot = (bucket[..., None] == np.arange(N_BUCKETS)) & band[..., None]
    return onehot.reshape(Q_BLOCK * w, N_BUCKETS).astype(np.float32), band


def _bias_table(rel_bias_cols, half_window, stride):
    onehot, band = _bucket_onehot(half_window, stride)
    h = rel_bias_cols.shape[1]
    w = Q_BLOCK + 2 * half_window
    t = jnp.einsum("pb,bh->hp", jnp.asarray(onehot), rel_bias_cols, precision=lax.Precision.HIGHEST)
    t = t.reshape(h, Q_BLOCK, w) + jnp.asarray(np.where(band, 0.0, NEG_INF).astype(np.float32))
    return t.reshape(h // 2, 2, Q_BLOCK, w)


def _bias_grad(ds_sum, half_window, stride):
    onehot, _ = _bucket_onehot(half_window, stride)
    h = ds_sum.shape[0] * 2
    return jnp.einsum("pb,hp->bh", jnp.asarray(onehot), ds_sum.reshape(h, -1), precision=lax.Precision.HIGHEST)


def _transpose_cast(w, out_dtype, name):
    lead = (None,) * (w.ndim - 2)
    zero = (0,) * (w.ndim - 2)
    r, c = w.shape[-2:]

    def body(w_ref, o_ref):
        o_ref[...] = w_ref[...].T.astype(out_dtype)

    if r % LANES == 0:
        steps = pl.cdiv(c, LANES)
        in_spec = pl.BlockSpec(lead + (r, LANES), lambda j: zero + (0, j))
        out_spec = pl.BlockSpec((LANES, r), lambda j: (j, 0))
    else:
        steps = pl.cdiv(r, LANES)
        in_spec = pl.BlockSpec(lead + (LANES, c), lambda j: zero + (j, 0))
        out_spec = pl.BlockSpec((c, LANES), lambda j: (0, j))
    return pl.pallas_call(
        body, name=name, grid=(steps,), in_specs=[in_spec], out_specs=out_spec,
        out_shape=jax.ShapeDtypeStruct((c, r), out_dtype),
        compiler_params=_params("arbitrary"),
    )(w)


def _gather_weights(wt_shard, small_shard):
    bufs = ((W_IN_SHARD, IN_WIDTH), (SMALL_ROWS, N_CHIPS * SMALL_ROWS))

    stage_rows = 528

    def body(wt_in, sm_in, wt_out, sm_out, send_sems, recv_sems, in_sems, out_sems, stage):
        x, y, c = lax.axis_index("x"), lax.axis_index("y"), lax.axis_index("c")
        sibling = (x, y, 1 - c)
        my_chip = 2 * x + y
        refs = ((wt_in, wt_out), (sm_in, sm_out))

        def keep_own():
            pieces = [(b, r0) for b in range(2) for r0 in range(0, bufs[b][0], stage_rows)]
            outs = []
            for i, (b, r0) in enumerate(pieces):
                rows = min(stage_rows, bufs[b][0] - r0)
                slot = i % 2
                if i >= 2:
                    outs[i - 2].wait()
                buf = stage.at[slot, pl.ds(0, rows), :]
                load = pltpu.make_async_copy(refs[b][0].at[pl.ds(r0, rows), :], buf, in_sems.at[slot])
                load.start()
                load.wait()
                start = pl.multiple_of(my_chip * bufs[b][0] + r0, 16)
                outs.append(pltpu.make_async_copy(buf, refs[b][1].at[pl.ds(start, rows), :], out_sems.at[slot]))
                outs[i].start()
            for cp in outs[-2:]:
                cp.wait()

        def half_of(b, chip, half):
            rows = bufs[b][0]
            start = pl.multiple_of(chip * rows + half * (rows // 2), 16)
            return refs[b][1].at[pl.ds(start, rows // 2), :]

        def copy(k, src, dst, to):
            return pltpu.make_async_remote_copy(src_ref=src, dst_ref=dst, send_sem=send_sems.at[k],
                                                recv_sem=recv_sems.at[k], device_id=to, device_id_type=MESH)

        near = (x + (1 - c) - 2 * x * (1 - c), y + c - 2 * y * c)
        far = (x + c - 2 * x * c, y + (1 - c) - 2 * y * (1 - c))
        diag = (1 - x, 1 - y)
        chip_no = lambda chip: 2 * chip[0] + chip[1]
        sends, passed = [], []
        for b in range(2):
            rows = bufs[b][0]
            src = refs[b][0].at[pl.ds(pl.multiple_of(c * (rows // 2), 16), rows // 2), :]
            sends += [copy(3 * b, src, half_of(b, my_chip, c), (*near, c)),
                      copy(3 * b + 1, src, half_of(b, my_chip, c), (*far, c))]
        for cp in sends:
            cp.start()
        keep_own()

        def pass_on(b, j, chip):
            landed = half_of(b, chip_no(chip), c)
            fwd = copy(6 + 3 * b + j, landed, landed, sibling)
            fwd.start()
            passed.append(fwd)

        for b in range(2):
            landed = half_of(b, chip_no(near), c)
            copy(3 * b, landed, landed, sibling).wait_recv()
            relay = copy(3 * b + 2, landed, landed, (*far, c))
            relay.start()
            sends.append(relay)
            pass_on(b, 0, near)
        for b in range(2):
            for j, chip in ((1, far), (2, diag)):
                landed = half_of(b, chip_no(chip), c)
                copy(3 * b + j, landed, landed, sibling).wait_recv()
                pass_on(b, j, chip)
        for b in range(2):
            for j, chip in ((0, far), (1, near), (2, diag)):
                other = half_of(b, chip_no(chip), 1 - c)
                copy(6 + 3 * b + j, other, other, sibling).wait_recv()
        for cp in sends + passed:
            cp.wait_send()

    return pl.pallas_call(
        body, name="gather_weights",
        in_specs=[ANY, ANY], out_specs=[ANY, ANY],
        out_shape=[jax.ShapeDtypeStruct((bufs[0][1], D_MODEL), BF16),
                   jax.ShapeDtypeStruct((bufs[1][1], D_MODEL), BF16)],
        scratch_shapes=[pltpu.SemaphoreType.DMA((12,)), pltpu.SemaphoreType.DMA((12,)),
                        pltpu.SemaphoreType.DMA((2,)), pltpu.SemaphoreType.DMA((2,)),
                        pltpu.VMEM((2, stage_rows, D_MODEL), BF16)],
    )(wt_shard, small_shard)


W_BLOCK = 768


def _w_blocks(first, count):
    return [pl.BlockSpec((W_BLOCK, D_MODEL), lambda *_, k=k: (first + k, 0)) for k in range(count)]


def _in_proj(x, gain, w_t, first_block, n_blocks, out_dtype, name):
    tm = 256

    def body(x_ref, g_ref, *refs):
        w_refs, (o_ref, h_ref) = refs[:n_blocks], refs[n_blocks:]
        xf = x_ref[...]
        r = lax.rsqrt(jnp.mean(xf * xf, axis=-1, keepdims=True) + EPS)
        h = ((xf * r) * g_ref[...]).astype(BF16)
        h_ref[...] = h
        for k, w_ref in enumerate(w_refs):
            o_ref[:, k * W_BLOCK:(k + 1) * W_BLOCK] = _dot(h, w_ref[...], NT).astype(out_dtype)

    return pl.pallas_call(
        body, name=name, grid=(SEQ // tm,),
        in_specs=[pl.BlockSpec((tm, D_MODEL), lambda i: (i, 0)), pl.BlockSpec((1, D_MODEL), lambda i: (0, 0))]
        + _w_blocks(first_block, n_blocks),
        out_specs=[pl.BlockSpec((tm, W_BLOCK * n_blocks), lambda i: (i, 0)),
                   pl.BlockSpec((tm, D_MODEL), lambda i: (i, 0))],
        out_shape=[jax.ShapeDtypeStruct((SEQ, W_BLOCK * n_blocks), out_dtype),
                   jax.ShapeDtypeStruct((SEQ, D_MODEL), BF16)],
        compiler_params=_params("arbitrary"),
    )(x, gain, *([w_t] * n_blocks))


CHUNK = 256
CHUNK_UNROLL = 4
TILE_UNROLL = 4


def _low_half():
    return lax.broadcasted_iota(jnp.int32, (1, LANES), 1) < HEAD_DIM


def _half_sum(v, low):
    del low
    row = lax.broadcasted_iota(jnp.int32, (2 * LANES, LANES), 0)
    col = lax.broadcasted_iota(jnp.int32, (2 * LANES, LANES), 1)
    ones = jnp.where((row % LANES) // HEAD_DIM == col // HEAD_DIM, 1.0, 0.0).astype(BF16)
    hi = v.astype(BF16)
    lo = (v - hi.astype(F32)).astype(BF16)
    return _dot(jnp.concatenate([hi, lo], axis=1), ones)


def _chunks(fn, init=0):
    def body(i, carry):
        for u in range(CHUNK_UNROLL):
            carry = fn(pl.multiple_of((i * CHUNK_UNROLL + u) * CHUNK, CHUNK), carry)
        return carry

    return lax.fori_loop(0, SEQ // (CHUNK * CHUNK_UNROLL), body, init)


def _inv_rms(t, low):
    return lax.rsqrt(_half_sum(t * t, low) * (1.0 / HEAD_DIM) + EPS)


def _prep_q(q_ref, gain_ref, qn_ref):
    low = _low_half()

    def step(r0, carry):
        q = q_ref[pl.ds(r0, CHUNK), :].astype(F32)
        qn_ref[pl.ds(r0, CHUNK), :] = ((q * _inv_rms(q, low)) * gain_ref[...]) * SCALE
        return carry

    _chunks(step)


def _own_half(t, keep):
    return jnp.where(keep, t, pltpu.roll(t, HEAD_DIM, 1))


def _prep_kv(k_ref, v_ref, gain_ref, kp_ref, vp_ref, pad, keep=None):
    low = _low_half()
    zeros = jnp.zeros((pad, LANES), F32)
    for ref in (kp_ref, vp_ref):
        ref[pl.ds(0, pad), :] = zeros
        ref[pl.ds(pad + SEQ, pad), :] = zeros

    def step(r0, carry):
        k = k_ref[pl.ds(r0, CHUNK), :].astype(F32)
        v = v_ref[pl.ds(r0, CHUNK), :].astype(F32)
        kn = (k * _inv_rms(k, low)) * gain_ref[...]
        if keep is not None:
            kn, v = _own_half(kn, keep), _own_half(v, keep)
        kp_ref[pl.ds(pad + r0, CHUNK), :] = kn
        vp_ref[pl.ds(pad + r0, CHUNK), :] = v
        return carry

    _chunks(step)


def _tiles(d, half_window, fn):
    w = Q_BLOCK + 2 * half_window
    length = SEQ // d
    n_blocks = length // Q_BLOCK
    col = lax.broadcasted_iota(jnp.int32, (1, w), 1)

    def step(it, carry):
        c, n = it // n_blocks, it % n_blocks
        start = c + (d * Q_BLOCK) * n
        if d == 1:
            start = pl.multiple_of(start, Q_BLOCK)
            q_rows, k_rows = pl.ds(start, Q_BLOCK), pl.ds(start, w)
        else:
            q_rows, k_rows = pl.ds(start, Q_BLOCK, stride=d), pl.ds(start, w, stride=d)
        t = n * Q_BLOCK - half_window + col
        edge = jnp.where((t < 0) | (t >= length), NEG_INF, 0.0)
        fn(q_rows, k_rows, edge)
        return carry

    lax.fori_loop(0, d * n_blocks, step, 0, unroll=TILE_UNROLL)


def _stack_heads(t, low):
    return jnp.concatenate([jnp.where(low, t, 0.0), jnp.where(low, 0.0, t)], axis=0).astype(BF16)


def _unstack_heads(t, low):
    return jnp.where(low, t[:Q_BLOCK], t[Q_BLOCK:])


def _per_head(pair):
    return jnp.concatenate([jnp.full((Q_BLOCK, 1), pair[0], F32), jnp.full((Q_BLOCK, 1), pair[1], F32)], axis=0)


def _fwd_tiles(qn_ref, kp_ref, vp_ref, bias_ref, emit, *, d, half_window, sinks=None):
    low = _low_half()
    w = Q_BLOCK + 2 * half_window
    sink = None if sinks is None else _per_head(sinks)

    def tile(q_rows, k_rows, edge):
        q2 = _stack_heads(qn_ref[q_rows, :], low)
        k = kp_ref[k_rows, :].astype(BF16)
        v1 = jnp.concatenate([vp_ref[k_rows, :], jnp.ones((w, LANES), F32)], axis=1).astype(BF16)
        s = _dot(q2, k, NT) + bias_ref[...].reshape(2 * Q_BLOCK, w) + edge
        m = jnp.max(s, axis=-1, keepdims=True)
        if sink is not None:
            m = jnp.maximum(m, sink)
        o = _dot(jnp.exp(s - m).astype(BF16), v1)
        l = o[:, LANES:]
        if sink is not None:
            l = l + jnp.exp(sink - m)
        emit(q_rows, _unstack_heads(o[:, :LANES] * (1.0 / l), low), _unstack_heads(m + jnp.log(l), low))

    _tiles(d, half_window, tile)


def _bwd_tiles(qn_ref, kp_ref, vp_ref, bias_ref, do_ref, lse_ref, delta_ref, dq_ref, dk_ref, dv_ref, ds_ref,
               *, d, half_window, sinks=None, dsink_ref=None):
    low = _low_half()
    w = Q_BLOCK + 2 * half_window
    sink = None if sinks is None else _per_head(sinks)

    def rows_of(t):
        return jnp.concatenate([t[:, 0:1], t[:, HEAD_DIM:HEAD_DIM + 1]], axis=0)

    def tile(q_rows, k_rows, edge):
        q2 = _stack_heads(qn_ref[q_rows, :], low)
        do2 = _stack_heads(do_ref[q_rows, :], low)
        k = kp_ref[k_rows, :].astype(BF16)
        v = vp_ref[k_rows, :].astype(BF16)
        lse = rows_of(lse_ref[q_rows, :])
        delta = rows_of(delta_ref[q_rows, :])
        p = jnp.exp(_dot(q2, k, NT) + bias_ref[...].reshape(2 * Q_BLOCK, w) + edge - lse)
        ds = p * (_dot(do2, v, NT) - delta)
        ds_ref[...] += ds.reshape(2, Q_BLOCK, w)
        if sink is not None:
            dsink_ref[...] += (-jnp.exp(sink - lse) * delta).reshape(2, Q_BLOCK, 1)
        dsb, pb = ds.astype(BF16), p.astype(BF16)
        dq_ref[q_rows, :] = _unstack_heads(_dot(dsb, k), low)
        dk_ref[k_rows, :] += _dot(dsb, q2, TN)
        dv_ref[k_rows, :] += _dot(pb, do2, TN)

    _tiles(d, half_window, tile)


def _prep_delta(do_ref, o_ref, delta_ref):
    low = _low_half()

    def step(r0, carry):
        delta_ref[pl.ds(r0, CHUNK), :] = _half_sum(do_ref[pl.ds(r0, CHUNK), :] * o_ref[pl.ds(r0, CHUNK), :], low)
        return carry

    _chunks(step)


def _norm_bwd(raw_ref, gain_ref, dn_ref, dn_offset, out_ref, scale):
    low = _low_half()

    def step(r0, dgain):
        t = raw_ref[pl.ds(r0, CHUNK), :].astype(F32)
        dn = dn_ref[pl.ds(dn_offset + r0, CHUNK), :]
        dth = dn * (gain_ref[...] * scale)
        sums = _half_sum(jnp.concatenate([t * t, dth * t], axis=0), low)
        r = lax.rsqrt(sums[:CHUNK] * (1.0 / HEAD_DIM) + EPS)
        th = t * r
        out_ref[pl.ds(r0, CHUNK), :] = (r * (dth - th * (r * sums[CHUNK:] * (1.0 / HEAD_DIM)))).astype(BF16)
        return dgain + jnp.sum(dn * th, axis=0, keepdims=True) * scale

    return _chunks(step, jnp.zeros((1, LANES), F32))


def _rows8(v):
    return jnp.broadcast_to(v, (8, v.shape[-1]))


A_W = Q_BLOCK + 2 * A_HALF_WINDOW
A_PAD = A_HALF_WINDOW


def _seq_block(col_fn):
    return pl.BlockSpec((SEQ, LANES), col_fn)


def _attn_a_fwd(qkv, gain_q, gain_k, bias, sink):
    def body(sink_ref, q_ref, k_ref, v_ref, gq_ref, gk_ref, bias_ref, o_ref, lse_ref, qn_ref, kp_ref, vp_ref):
        hp = pl.program_id(0)
        keep = (lax.broadcasted_iota(jnp.int32, (1, LANES), 1) // HEAD_DIM) == hp // 2
        _prep_q(q_ref, gq_ref, qn_ref)
        _prep_kv(k_ref, v_ref, gk_ref, kp_ref, vp_ref, A_PAD, keep)

        def emit(rows, out, lse):
            o_ref[rows, :] = out
            lse_ref[rows, :] = lse

        _fwd_tiles(qn_ref, kp_ref, vp_ref, bias_ref, emit, d=1, half_window=A_HALF_WINDOW,
                   sinks=(sink_ref[2 * hp], sink_ref[2 * hp + 1]))

    vec = pl.BlockSpec((1, LANES), lambda hp, s: (0, 0))
    return pl.pallas_call(
        body, name="attn_a_fwd",
        grid_spec=pltpu.PrefetchScalarGridSpec(
            num_scalar_prefetch=1, grid=(4,),
            in_specs=[_seq_block(lambda hp, s: (0, QA_BLK + hp)), _seq_block(lambda hp, s: (0, KA_BLK)),
                      _seq_block(lambda hp, s: (0, VA_BLK)), vec, vec,
                      pl.BlockSpec((None, 2, Q_BLOCK, A_W), lambda hp, s: (hp, 0, 0, 0))],
            out_specs=[_seq_block(lambda hp, s: (0, hp)), _seq_block(lambda hp, s: (0, hp))],
            scratch_shapes=[pltpu.VMEM((SEQ, LANES), F32), pltpu.VMEM((SEQ + 2 * A_PAD, LANES), F32),
                            pltpu.VMEM((SEQ + 2 * A_PAD, LANES), F32)]),
        out_shape=[jax.ShapeDtypeStruct((SEQ, 512), F32)] * 2,
        compiler_params=_params("arbitrary"),
    )(sink.reshape(8), qkv, qkv, qkv, gain_q, gain_k, bias)


def _attn_a_bwd(qkv, gain_q, gain_k, bias, sink, out, lse, d_out):
    def body(sink_ref, q_ref, k_ref, v_ref, gq_ref, gk_ref, bias_ref, o_ref, lse_ref, do_ref,
             dq_out, dkv_out, dgq_out, dgk_out, ds_out, dsink_out,
             qn_ref, kp_ref, vp_ref, delta_ref, dq_ref, dk_ref, dv_ref, dk_tot, dv_tot):
        hp = pl.program_id(0)
        kv_head = hp // 2
        keep = (lax.broadcasted_iota(jnp.int32, (1, LANES), 1) // HEAD_DIM) == kv_head
        _prep_q(q_ref, gq_ref, qn_ref)
        _prep_kv(k_ref, v_ref, gk_ref, kp_ref, vp_ref, A_PAD, keep)
        _prep_delta(do_ref, o_ref, delta_ref)
        dk_ref[...] = jnp.zeros_like(dk_ref)
        dv_ref[...] = jnp.zeros_like(dv_ref)
        ds_out[...] = jnp.zeros_like(ds_out)
        dsink_out[...] = jnp.zeros_like(dsink_out)

        @pl.when(hp == 0)
        def _():
            dk_tot[...] = jnp.zeros_like(dk_tot)
            dv_tot[...] = jnp.zeros_like(dv_tot)

        _bwd_tiles(qn_ref, kp_ref, vp_ref, bias_ref, do_ref, lse_ref, delta_ref, dq_ref, dk_ref, dv_ref, ds_out,
                   d=1, half_window=A_HALF_WINDOW, sinks=(sink_ref[2 * hp], sink_ref[2 * hp + 1]),
                   dsink_ref=dsink_out)
        dgq_out[...] = _rows8(_norm_bwd(q_ref, gq_ref, dq_ref, 0, dq_out, SCALE))

        def fold(r0, carry):
            rows = pl.ds(A_PAD + r0, CHUNK)
            for acc, tot in ((dk_ref, dk_tot), (dv_ref, dv_tot)):
                t = acc[rows, :]
                tot[pl.ds(r0, CHUNK), :] += jnp.where(keep, t + pltpu.roll(t, HEAD_DIM, 1), 0.0)
            return carry

        _chunks(fold)

        @pl.when(hp == 3)
        def _():
            dgk_out[...] = _rows8(_norm_bwd(k_ref, gk_ref, dk_tot, 0, dkv_out.at[0], 1.0))
            dkv_out[1] = dv_tot[...].astype(BF16)

    vec = pl.BlockSpec((1, LANES), lambda hp, s: (0, 0))
    seq_f32 = pltpu.VMEM((SEQ, LANES), F32)
    padded = pltpu.VMEM((SEQ + 2 * A_PAD, LANES), F32)
    return pl.pallas_call(
        body, name="attn_a_bwd",
        grid_spec=pltpu.PrefetchScalarGridSpec(
            num_scalar_prefetch=1, grid=(4,),
            in_specs=[_seq_block(lambda hp, s: (0, QA_BLK + hp)), _seq_block(lambda hp, s: (0, KA_BLK)),
                      _seq_block(lambda hp, s: (0, VA_BLK)), vec, vec,
                      pl.BlockSpec((None, 2, Q_BLOCK, A_W), lambda hp, s: (hp, 0, 0, 0)),
                      _seq_block(lambda hp, s: (0, hp)), _seq_block(lambda hp, s: (0, hp)),
                      _seq_block(lambda hp, s: (0, hp))],
            out_specs=[pl.BlockSpec((None, SEQ, LANES), lambda hp, s: (hp, 0, 0)),
                       pl.BlockSpec((2, SEQ, LANES), lambda hp, s: (0, 0, 0)),
                       pl.BlockSpec((None, 8, LANES), lambda hp, s: (hp, 0, 0)),
                       pl.BlockSpec((8, LANES), lambda hp, s: (0, 0)),
                       pl.BlockSpec((None, 2, Q_BLOCK, A_W), lambda hp, s: (hp, 0, 0, 0)),
                       pl.BlockSpec((None, 2, Q_BLOCK, 1), lambda hp, s: (hp, 0, 0, 0))],
            scratch_shapes=[seq_f32, padded, padded, seq_f32, seq_f32, padded, padded, seq_f32, seq_f32]),
        out_shape=[jax.ShapeDtypeStruct((4, SEQ, LANES), BF16), jax.ShapeDtypeStruct((2, SEQ, LANES), BF16),
                   jax.ShapeDtypeStruct((4, 8, LANES), F32), jax.ShapeDtypeStruct((8, LANES), F32),
           jax.ShapeDtypeStruct((4, 2, Q_BLOCK, A_W), F32), jax.ShapeDtypeStruct((4, 2, Q_BLOCK, 1), F32)],
        compiler_params=_params("arbitrary"),
    )(sink.reshape(8), qkv, qkv, qkv, gain_q, gain_k, bias, out, lse, d_out)


B_W = Q_BLOCK + 2 * B_HALF_WINDOW
B_PAD_MAX = B_HALF_WINDOW * B_DILATIONS[-1]


def _attn_b_fwd(qkv, gain_q, gain_k, bias):
    def body(q_ref, k_ref, v_ref, gq_ref, gk_ref, bias_ref, o_ref, lse_ref, qn_ref, kp_ref, vp_ref):
        g = pl.program_id(1)
        _prep_q(q_ref, gq_ref, qn_ref)

        def first(rows, out, lse):
            o_ref[rows, :] = out
            lse_ref[rows, :] = lse

        def combine(rows, out, lse):
            old = lse_ref[rows, :]
            new = jnp.maximum(old, lse) + jnp.log(1.0 + jnp.exp(-jnp.abs(old - lse)))
            o_ref[rows, :] = o_ref[rows, :] * jnp.exp(old - new) + out * jnp.exp(lse - new)
            lse_ref[rows, :] = new

        for gi, d in enumerate(B_DILATIONS):
            @pl.when(g == gi)
            def _():
                _prep_kv(k_ref, v_ref, gk_ref, kp_ref, vp_ref, B_HALF_WINDOW * d)
                _fwd_tiles(qn_ref, kp_ref, vp_ref, bias_ref, first if gi == 0 else combine,
                           d=d, half_window=B_HALF_WINDOW)

    vec = pl.BlockSpec((1, LANES), lambda hp, g: (0, 0))
    padded = pltpu.VMEM((SEQ + 2 * B_PAD_MAX, LANES), F32)
    return pl.pallas_call(
        body, name="attn_b_fwd", grid=(4, 3),
        in_specs=[_seq_block(lambda hp, g: (0, QB_BLK + 4 * g + hp)), _seq_block(lambda hp, g: (0, KB_BLK + 4 * g + hp)),
                  _seq_block(lambda hp, g: (0, VB_BLK + 4 * g + hp)), vec, vec,
                  pl.BlockSpec((None, 2, Q_BLOCK, B_W), lambda hp, g: (4 * g + hp, 0, 0, 0))],
        out_specs=[_seq_block(lambda hp, g: (0, hp)), _seq_block(lambda hp, g: (0, hp))],
        out_shape=[jax.ShapeDtypeStruct((SEQ, 512), F32)] * 2,
        scratch_shapes=[pltpu.VMEM((SEQ, LANES), F32), padded, padded],
        compiler_params=_params("arbitrary", "arbitrary"),
    )(qkv, qkv, qkv, gain_q, gain_k, bias)


def _attn_b_bwd(qkv, gain_q, gain_k, bias, out, lse, d_out):
    def body(q_ref, k_ref, v_ref, gq_ref, gk_ref, bias_ref, o_ref, lse_ref, do_ref,
             dq_out, dk_out, dv_out, dgq_out, dgk_out, ds_out,
             qn_ref, kp_ref, vp_ref, delta_ref, dq_ref, dk_ref, dv_ref):
        g = pl.program_id(1)
        _prep_q(q_ref, gq_ref, qn_ref)
        _prep_delta(do_ref, o_ref, delta_ref)
        dk_ref[...] = jnp.zeros_like(dk_ref)
        dv_ref[...] = jnp.zeros_like(dv_ref)
        ds_out[...] = jnp.zeros_like(ds_out)
        for gi, d in enumerate(B_DILATIONS):
            @pl.when(g == gi)
            def _():
                pad = B_HALF_WINDOW * d
                _prep_kv(k_ref, v_ref, gk_ref, kp_ref, vp_ref, pad)
                _bwd_tiles(qn_ref, kp_ref, vp_ref, bias_ref, do_ref, lse_ref, delta_ref, dq_ref, dk_ref, dv_ref,
                           ds_out, d=d, half_window=B_HALF_WINDOW)
                dgk_out[...] = _rows8(_norm_bwd(k_ref, gk_ref, dk_ref, pad, dk_out, 1.0))
                dv_out[...] = dv_ref[pl.ds(pad, SEQ), :].astype(BF16)
        dgq_out[...] = _rows8(_norm_bwd(q_ref, gq_ref, dq_ref, 0, dq_out, SCALE))

    vec = pl.BlockSpec((1, LANES), lambda hp, g: (0, 0))
    seq_f32 = pltpu.VMEM((SEQ, LANES), F32)
    padded = pltpu.VMEM((SEQ + 2 * B_PAD_MAX, LANES), F32)
    part = pl.BlockSpec((None, 8, LANES), lambda hp, g: (4 * g + hp, 0, 0))
    return pl.pallas_call(
        body, name="attn_b_bwd", grid=(4, 3),
        in_specs=[_seq_block(lambda hp, g: (0, QB_BLK + 4 * g + hp)), _seq_block(lambda hp, g: (0, KB_BLK + 4 * g + hp)),
                  _seq_block(lambda hp, g: (0, VB_BLK + 4 * g + hp)), vec, vec,
                  pl.BlockSpec((None, 2, Q_BLOCK, B_W), lambda hp, g: (4 * g + hp, 0, 0, 0)),
                  _seq_block(lambda hp, g: (0, hp)), _seq_block(lambda hp, g: (0, hp)), _seq_block(lambda hp, g: (0, hp))],
        out_specs=[pl.BlockSpec((None, SEQ, LANES), lambda hp, g: (4 * g + hp, 0, 0))] * 3 + [
            part, part, pl.BlockSpec((None, 2, Q_BLOCK, B_W), lambda hp, g: (4 * g + hp, 0, 0, 0))],
        out_shape=[jax.ShapeDtypeStruct((12, SEQ, LANES), BF16)] * 3
        + [jax.ShapeDtypeStruct((12, 8, LANES), F32)] * 2 + [jax.ShapeDtypeStruct((12, 2, Q_BLOCK, B_W), F32)],
        scratch_shapes=[seq_f32, padded, padded, seq_f32, seq_f32, padded, padded],
        compiler_params=_params("arbitrary", "arbitrary"),
    )(qkv, qkv, qkv, gain_q, gain_k, bias, out, lse, d_out)


def _sigmoid(t):
    return 1.0 / (1.0 + jnp.exp(-t))


def _middle(out_a, out_b, gates, x, target, w_a, w_b, w_out, b_merge):
    tm = 256
    n_steps = SEQ // tm

    def body(oa_ref, ob_ref, g_ref, x_ref, t_ref, wa_ref, wb_ref, wo_ref, bm_ref,
             dy_ref, dg_ref, doa_ref, dob_ref, dwa_ref, dwb_ref, dwo_ref, dbm_ref, sq_ref):
        @pl.when(pl.program_id(0) == 0)
        def _():
            for ref in (dwa_ref, dwb_ref, dwo_ref, dbm_ref, sq_ref):
                ref[...] = jnp.zeros_like(ref)

        gate_a, gate_b = g_ref[:, 0:512], g_ref[:, 512:1024]
        sig_a, sig_b = _sigmoid(gate_a), _sigmoid(gate_b)
        silu_a, silu_b = gate_a * sig_a, gate_b * sig_b
        oa, ob = oa_ref[...], ob_ref[...]
        ya, yb = (oa * silu_a).astype(BF16), (ob * silu_b).astype(BF16)
        br_a, br_b = _dot(ya, wa_ref[...]), _dot(yb, wb_ref[...])
        m0 = _sigmoid(g_ref[:, 1024:2048] + bm_ref[0:1, :])
        m1 = _sigmoid(g_ref[:, 2048:3072] + bm_ref[1:2, :])
        merged = (m0 * br_a + m1 * br_b).astype(BF16)
        err = (x_ref[...] + _dot(merged, wo_ref[...])) - t_ref[...]
        sq_ref[...] += jnp.sum(err * err, axis=0, keepdims=True)

        dy = err * (1.0 / D_MODEL)
        dy_ref[...] = dy
        dyb = dy.astype(BF16)
        dmerged = _dot(dyb, wo_ref[...], NT)
        dwo_ref[...] += _dot(merged, dyb, TN)
        dbr_a, dbr_b = (dmerged * m0).astype(BF16), (dmerged * m1).astype(BF16)
        dm0 = (dmerged * br_a) * (m0 * (1.0 - m0))
        dm1 = (dmerged * br_b) * (m1 * (1.0 - m1))
        dbm_ref[0:1, :] += jnp.sum(dm0, axis=0, keepdims=True)
        dbm_ref[1:2, :] += jnp.sum(dm1, axis=0, keepdims=True)
        for s in range(N_CHIPS):
            cols = slice(256 * s, 256 * (s + 1))
            dwa_ref[s] += _dot(ya, dbr_a[:, cols], TN)
            dwb_ref[s] += _dot(yb, dbr_b[:, cols], TN)
        dya, dyb_ = _dot(dbr_a, wa_ref[...], NT), _dot(dbr_b, wb_ref[...], NT)
        doa_ref[...] = dya * silu_a
        dob_ref[...] = dyb_ * silu_b
        d_gates = (((dya * oa) * (sig_a * (1.0 + gate_a * (1.0 - sig_a)))).astype(BF16),
                   ((dyb_ * ob) * (sig_b * (1.0 + gate_b * (1.0 - sig_b)))).astype(BF16),
                   dm0.astype(BF16), dm1.astype(BF16))
        blk = 0
        for part in d_gates:
            for c0 in range(0, part.shape[1], 256):
                dg_ref[blk] = part[:, c0:c0 + 256]
                blk += 1

    def rows(width):
        return pl.BlockSpec((tm, width), lambda i: (i, 0))

    def whole(*shape):
        return pl.BlockSpec(shape, lambda i: (0,) * len(shape))

    return pl.pallas_call(
        body, name="middle", grid=(n_steps,),
        in_specs=[rows(512), rows(512), rows(GATE_WIDTH), rows(D_MODEL), rows(D_MODEL),
                  whole(512, D_MODEL), whole(512, D_MODEL), whole(D_MODEL, D_MODEL), whole(2, D_MODEL)],
        out_specs=[rows(D_MODEL), pl.BlockSpec((GATE_WIDTH // 256, tm, 256), lambda i: (0, i, 0)), rows(512), rows(512),
                   whole(N_CHIPS, 512, 256), whole(N_CHIPS, 512, 256), whole(D_MODEL, D_MODEL),
                   whole(2, D_MODEL), whole(1, D_MODEL)],
        out_shape=[jax.ShapeDtypeStruct((SEQ, D_MODEL), F32), jax.ShapeDtypeStruct((GATE_WIDTH // 256, SEQ, 256), BF16),
                   jax.ShapeDtypeStruct((SEQ, 512), F32), jax.ShapeDtypeStruct((SEQ, 512), F32),
                   jax.ShapeDtypeStruct((N_CHIPS, 512, 256), F32), jax.ShapeDtypeStruct((N_CHIPS, 512, 256), F32),
                   jax.ShapeDtypeStruct((D_MODEL, D_MODEL), F32), jax.ShapeDtypeStruct((2, D_MODEL), F32),
                   jax.ShapeDtypeStruct((1, D_MODEL), F32)],
        compiler_params=_params("arbitrary"),
    )(out_a, out_b, gates, x, target, w_a, w_b, w_out, b_merge)


def _which(j, edges, fns):
    lo = 0
    for hi, fn in zip(edges, fns):
        pl.when((j >= lo) & (j < hi))(fn)
        lo = hi


def _d_w_in(d_proj, h):
    plan, step, width = [], 0, 0
    for p in d_proj:
        total = p.shape[0] * p.shape[2]
        if width + total <= W_BLOCK:
            plan.append((p.shape[0], step, 1))
            width += total
            if width == W_BLOCK:
                step, width = step + 1, 0
        else:
            assert width == 0 and total % W_BLOCK == 0
            plan.append((W_BLOCK // p.shape[2], step, total // W_BLOCK))
            step += total // W_BLOCK
    assert width == 0 and step == IN_WIDTH // W_BLOCK
    firsts = sorted({first for _, first, _ in plan})
    edges = firsts[1:] + [step]
    halves = 2

    def body(*refs):
        pieces, h_ref, o_ref, b_ref = refs[:-3], refs[-3], refs[-2], refs[-1]
        k = pl.program_id(1)

        def emit(group):
            def fn():
                cols = jnp.concatenate([ref[b] for ref in group for b in range(ref.shape[0])], axis=1)
                term = _dot(cols, h_ref[...], TN)

                @pl.when(k == 0)
                def _():
                    o_ref[...] = term

                @pl.when(k == halves - 1)
                def _():
                    total = o_ref[...] + term
                    o_ref[...] = total
                    b_ref[...] = total.astype(BF16)
            return fn

        groups = [[ref for ref, (_, first, _) in zip(pieces, plan) if first == f] for f in firsts]
        _which(pl.program_id(0), edges, [emit(group) for group in groups])

    def cols_spec(piece, n, first, steps):
        def index(j, k):
            return jnp.clip(j - first, 0, steps - 1), jnp.where((j >= first) & (j < first + steps), k, 0), 0
        return pl.BlockSpec((n, SEQ // halves, piece.shape[2]), index)

    return pl.pallas_call(
        body, name="d_w_in", grid=(step, halves),
        in_specs=[cols_spec(p, *pl_) for p, pl_ in zip(d_proj, plan)]
        + [pl.BlockSpec((SEQ // halves, D_MODEL), lambda j, k: (k, 0))],
        out_specs=[pl.BlockSpec((W_BLOCK, D_MODEL), lambda j, k: (j, 0))] * 2,
        out_shape=[jax.ShapeDtypeStruct((IN_WIDTH, D_MODEL), F32), jax.ShapeDtypeStruct((IN_WIDTH, D_MODEL), BF16)],
        compiler_params=_params("arbitrary", "arbitrary"),
    )(*d_proj, h)


def _d_x(d_proj, w_t, x, gain, dy, chip_sums):
    tm = 256
    n_steps = SEQ // tm
    n_w = IN_WIDTH // W_BLOCK
    n_p, n_s = len(d_proj), len(chip_sums)

    def body(*refs):
        pieces, w_refs = refs[:n_p], refs[n_p:n_p + n_w]
        x_ref, g_ref, dy_ref = refs[n_p + n_w:n_p + n_w + 3]
        q_refs = refs[n_p + n_w + 3:n_p + n_w + 3 + n_s]
        dx_ref, dgain_ref = refs[n_p + n_w + 3 + n_s:n_p + n_w + 5 + n_s]
        o_refs = refs[n_p + n_w + 5 + n_s:n_p + n_w + 5 + 2 * n_s]
        send_sems, recv_sems = refs[n_p + n_w + 5 + 2 * n_s:] if n_s else (None, None)

        @pl.when(pl.program_id(0) == 0)
        def _():
            dgain_ref[...] = jnp.zeros_like(dgain_ref)
            if n_s:
                for cp in _scatter_copies(q_refs, o_refs, send_sems, recv_sems):
                    cp.start()

        blocks = [(piece, k) for piece in pieces for k in range(piece.shape[0])]
        dh, group, width, blk = None, [], 0, 0
        for piece, k in blocks:
            group.append(piece[k])
            width += piece.shape[2]
            if width == W_BLOCK:
                term = _dot(jnp.concatenate(group, axis=1), w_refs[blk][...])
                dh = term if dh is None else dh + term
                group, width, blk = [], 0, blk + 1
        assert not group and blk == n_w
        xf = x_ref[...]
        r = lax.rsqrt(jnp.mean(xf * xf, axis=-1, keepdims=True) + EPS)
        xh = xf * r
        dxh = dh * g_ref[...]
        dx_ref[...] = r * (dxh - xh * jnp.mean(dxh * xh, axis=-1, keepdims=True)) + dy_ref[...]
        dgain_ref[...] += _rows8(jnp.sum(dh * xh, axis=0, keepdims=True))

        if n_s:
            @pl.when(pl.program_id(0) == n_steps - 1)
            def _():
                for cp in _scatter_copies(q_refs, o_refs, send_sems, recv_sems):
                    cp.wait()

    row = pl.BlockSpec((tm, D_MODEL), lambda i: (i, 0))
    res = pl.pallas_call(
        body, name="d_x", grid=(n_steps,),
        in_specs=[pl.BlockSpec((p.shape[0], tm, p.shape[2]), lambda i: (0, i, 0)) for p in d_proj] + _w_blocks(0, n_w)
        + [row, pl.BlockSpec((1, D_MODEL), lambda i: (0, 0)), row] + [ANY] * n_s,
        out_specs=[row, pl.BlockSpec((8, D_MODEL), lambda i: (0, 0))] + [ANY] * n_s,
        out_shape=[jax.ShapeDtypeStruct((SEQ, D_MODEL), F32), jax.ShapeDtypeStruct((8, D_MODEL), F32)]
        + [jax.ShapeDtypeStruct((3,) + q.shape[1:], BF16) for q in chip_sums],
        scratch_shapes=[pltpu.SemaphoreType.DMA((3 * n_s,)), pltpu.SemaphoreType.DMA((3 * n_s,))] if n_s else [],
        compiler_params=_params("arbitrary"),
    )(*d_proj, *([w_t] * n_w), x, gain, dy, *chip_sums)
    return res[0], res[1], res[2:]


def _my_place():
    x, y, c = lax.axis_index("x"), lax.axis_index("y"), lax.axis_index("c")
    return jnp.stack([2 * x + y, c]).astype(jnp.int32)


def _half_rows(ref, half):
    rows = ref.shape[-2] // 2
    idx = (slice(None),) * (len(ref.shape) - 2) + (pl.ds(pl.multiple_of(half * rows, 16), rows), slice(None))
    return ref.at[idx]


def _swap_halves(grads):
    n = len(grads)

    def body(*refs):
        g_refs, o_refs, (send_sems, recv_sems) = refs[:n], refs[n:2 * n], refs[2 * n:]
        x, y, c = lax.axis_index("x"), lax.axis_index("y"), lax.axis_index("c")
        copies = [pltpu.make_async_remote_copy(src_ref=_half_rows(g, 1 - c), dst_ref=o, send_sem=send_sems.at[k],
                                               recv_sem=recv_sems.at[k], device_id=(x, y, 1 - c), device_id_type=MESH)
                  for k, (g, o) in enumerate(zip(g_refs, o_refs))]
        for cp in copies:
            cp.start()
        for cp in copies:
            cp.wait()

    return pl.pallas_call(
        body, name="reduce_swap_halves", in_specs=[ANY] * n, out_specs=[ANY] * n,
        out_shape=[jax.ShapeDtypeStruct((N_CHIPS, g.shape[1] // 2, D_MODEL), g.dtype) for g in grads],
        scratch_shapes=[pltpu.SemaphoreType.DMA((n,)), pltpu.SemaphoreType.DMA((n,))],
    )(*grads)


def _row_tile(rows):
    return max(t for t in range(16, 385, 16) if rows % t == 0)


def _add_halves(place, grads, theirs, name):
    half = theirs.shape[1]
    tr = _row_tile(half)
    n = half // tr

    def body(place_ref, g_ref, t_ref, o_ref):
        o_ref[...] = (g_ref[...] + t_ref[...].astype(F32)).astype(BF16)

    return pl.pallas_call(
        body, name=name,
        grid_spec=pltpu.PrefetchScalarGridSpec(
            num_scalar_prefetch=1, grid=(N_CHIPS, n),
            in_specs=[pl.BlockSpec((None, tr, D_MODEL), lambda s, i, p: (s, p[1] * n + i, 0)),
                      pl.BlockSpec((None, tr, D_MODEL), lambda s, i, p: (s, i, 0))],
            out_specs=pl.BlockSpec((None, tr, D_MODEL), lambda s, i, p: (s, i, 0))),
        out_shape=jax.ShapeDtypeStruct((N_CHIPS, half, D_MODEL), BF16),
        compiler_params=_params("arbitrary", "arbitrary"),
    )(place, grads, theirs)


def _scatter_copies(q_refs, o_refs, send_sems, recv_sems):
    x, y, c = lax.axis_index("x"), lax.axis_index("y"), lax.axis_index("c")
    chips = [(1 - x, y), (x, 1 - y), (1 - x, 1 - y)]
    return [pltpu.make_async_remote_copy(src_ref=q.at[2 * cx + cy], dst_ref=o.at[j],
                                         send_sem=send_sems.at[3 * k + j], recv_sem=recv_sems.at[3 * k + j],
                                         device_id=(cx, cy, c), device_id_type=MESH)
            for k, (q, o) in enumerate(zip(q_refs, o_refs)) for j, (cx, cy) in enumerate(chips)]


def _add_chips(place, chip_sums, others, name):
    half = others.shape[1]
    tr = _row_tile(half)
    n = half // tr

    def body(place_ref, q_ref, o_ref, r_ref):
        acc = q_ref[...].astype(F32)
        for j in range(3):
            acc = acc + o_ref[j].astype(F32)
        r_ref[...] = acc

    return pl.pallas_call(
        body, name=name,
        grid_spec=pltpu.PrefetchScalarGridSpec(
            num_scalar_prefetch=1, grid=(n,),
            in_specs=[pl.BlockSpec((None, tr, D_MODEL), lambda i, p: (p[0], i, 0)),
                      pl.BlockSpec((3, tr, D_MODEL), lambda i, p: (0, i, 0))],
            out_specs=pl.BlockSpec((tr, D_MODEL), lambda i, p: (p[1] * n + i, 0))),
        out_shape=jax.ShapeDtypeStruct((2 * half, D_MODEL), F32),
        compiler_params=_params("arbitrary"),
    )(place, chip_sums, others)


def _join_halves(shards):
    n = len(shards)

    def body(*refs):
        o_refs, (send_sems, recv_sems) = refs[n:2 * n], refs[2 * n:]
        x, y, c = lax.axis_index("x"), lax.axis_index("y"), lax.axis_index("c")

        def copy(k, rows):
            return pltpu.make_async_remote_copy(src_ref=rows, dst_ref=rows, send_sem=send_sems.at[k],
                                                recv_sem=recv_sems.at[k], device_id=(x, y, 1 - c), device_id_type=MESH)

        sends = [copy(k, _half_rows(o, c)) for k, o in enumerate(o_refs)]
        for cp in sends:
            cp.start()
        for k, o in enumerate(o_refs):
            copy(k, _half_rows(o, 1 - c)).wait_recv()
        for cp in sends:
            cp.wait_send()

    return pl.pallas_call(
        body, name="reduce_join_halves", in_specs=[ANY] * n, out_specs=[ANY] * n,
        out_shape=[jax.ShapeDtypeStruct(s.shape, F32) for s in shards],
        input_output_aliases={k: k for k in range(n)},
        scratch_shapes=[pltpu.SemaphoreType.DMA((n,)), pltpu.SemaphoreType.DMA((n,))],
    )(*shards)


def _gather_small(block):
    rows = block.shape[0]

    def body(b_ref, o_ref, send_sems, recv_sems, local_sem):
        x, y, c = lax.axis_index("x"), lax.axis_index("y"), lax.axis_index("c")
        me, sibling = (x, y, c), (x, y, 1 - c)
        chips = [(1 - x, y), (x, 1 - y), (1 - x, 1 - y)]

        def at(px, py, pc):
            return o_ref.at[pl.ds(pl.multiple_of((4 * px + 2 * py + pc) * rows, 8), rows), :]

        def copy(k, block_of, to, src=None):
            return pltpu.make_async_remote_copy(src_ref=at(*block_of) if src is None else src, dst_ref=at(*block_of),
                                                send_sem=send_sems.at[k], recv_sem=recv_sems.at[k],
                                                device_id=to, device_id_type=MESH)

        mine = pltpu.make_async_copy(b_ref, at(*me), local_sem)
        mine.start()
        first = [copy(0, me, sibling, src=b_ref)]
        first += [copy(1 + j, me, (*chip, c), src=b_ref) for j, chip in enumerate(chips)]
        for cp in first:
            cp.start()
        passed = [copy(4 + j, (*chip, c), sibling) for j, chip in enumerate(chips)]
        for j, chip in enumerate(chips):
            copy(1 + j, (*chip, c), me).wait_recv()
            passed[j].start()
        copy(0, sibling, me).wait_recv()
        for j, chip in enumerate(chips):
            copy(4 + j, (*chip, 1 - c), me).wait_recv()
        for cp in first + passed:
            cp.wait_send()
        mine.wait()

    return pl.pallas_call(
        body, name="gather_small_grads",
        in_specs=[pl.BlockSpec(memory_space=pltpu.VMEM)], out_specs=pl.BlockSpec(memory_space=pltpu.VMEM),
        out_shape=jax.ShapeDtypeStruct((8 * rows, D_MODEL), F32),
        scratch_shapes=[pltpu.SemaphoreType.DMA((7,)), pltpu.SemaphoreType.DMA((7,)), pltpu.SemaphoreType.DMA],
    )(block)


def _sum_devices(blocks):
    def body(b_ref, o_ref):
        acc = b_ref[0:8, :]
        for dev in range(1, 8):
            acc = acc + b_ref[8 * dev:8 * dev + 8, :]
        o_ref[...] = acc

    return pl.pallas_call(body, name="sum_small_grads", out_shape=jax.ShapeDtypeStruct((8, D_MODEL), F32))(blocks)


def _adamw_math(w, g, m, v):
    m = ADAM_B1 * m + (1.0 - ADAM_B1) * g
    v = ADAM_B2 * v + (1.0 - ADAM_B2) * (g * g)
    m_hat = m / (1.0 - ADAM_B1 ** ADAM_STEP)
    v_hat = v / (1.0 - ADAM_B2 ** ADAM_STEP)
    return -ADAM_LR * (m_hat / (jnp.sqrt(v_hat) + ADAM_EPS) + ADAM_WD * w), m, v


def _adamw(w, g, m, v, name):
    r, c = w.shape
    tr = 128 if r % 128 == 0 else r

    def body(w_ref, g_ref, m_ref, v_ref, d_ref, nm_ref, nv_ref):
        d_ref[...], nm_ref[...], nv_ref[...] = _adamw_math(w_ref[...], g_ref[...], m_ref[...], v_ref[...])

    spec = pl.BlockSpec((tr, c), lambda i: (i, 0))
    return pl.pallas_call(
        body, name=name, grid=(r // tr,), in_specs=[spec] * 4, out_specs=[spec] * 3,
        out_shape=[jax.ShapeDtypeStruct((r, c), F32)] * 3, compiler_params=_params("arbitrary"),
    )(w, g, m, v)


def _adamw_small(ws, gs, ms, vs):
    n = len(ws)

    def body(*refs):
        ins, outs = refs[:4 * n], refs[4 * n:]
        for k in range(n):
            d, m, v = _adamw_math(ins[k][...], ins[n + k][...], ins[2 * n + k][...], ins[3 * n + k][...])
            outs[k][...], outs[n + k][...], outs[2 * n + k][...] = d, m, v

    shapes = [jax.ShapeDtypeStruct(w.shape, F32) for w in ws]
    res = pl.pallas_call(body, name="adamw_small", out_shape=shapes * 3)(*ws, *gs, *ms, *vs)
    return res[:n], res[n:2 * n], res[2 * n:]


def _fold_heads(partials):
    t = jnp.sum(partials[:, 0, :], axis=0)
    return (t[:HEAD_DIM] + t[HEAD_DIM:]).reshape(1, HEAD_DIM)


def _local_step(x, target, norm_gain, w_t, w_a, w_b, w_o, b_m, q_norm_a, k_norm_a, q_norm_b, k_norm_b, sink_a,
                rel_bias, start_reduce=None):
    two = lambda gain: jnp.concatenate([gain, gain], axis=1)
    bias_a = _bias_table(rel_bias[:, :8], A_HALF_WINDOW, 1)
    bias_b = jnp.concatenate([_bias_table(rel_bias[:, 8 + 8 * g:16 + 8 * g], B_HALF_WINDOW, d)
                              for g, d in enumerate(B_DILATIONS)], axis=0)

    qkv, h = _in_proj(x, norm_gain, w_t, 0, QKV_WIDTH // W_BLOCK, BF16, "in_proj_qkv")
    gates, _ = _in_proj(x, norm_gain, w_t, QKV_WIDTH // W_BLOCK, GATE_WIDTH // W_BLOCK, F32, "in_proj_gates")
    out_a, lse_a = _attn_a_fwd(qkv, two(q_norm_a), two(k_norm_a), bias_a, sink_a)
    out_b, lse_b = _attn_b_fwd(qkv, two(q_norm_b), two(k_norm_b), bias_b)

    dy, dgates, d_out_a, d_out_b, d_wa, d_wb, d_wo, d_bm, sq = _middle(
        out_a, out_b, gates, x, target, w_a, w_b, w_o, b_m)
    loss = (0.5 / D_MODEL) * jnp.sum(sq)

    dq_a, dkv_a, dgq_a, dgk_a, ds_a, dsink = _attn_a_bwd(
        qkv, two(q_norm_a), two(k_norm_a), bias_a, sink_a, out_a, lse_a, d_out_a)
    dq_b, dk_b, dv_b, dgq_b, dgk_b, ds_b = _attn_b_bwd(
        qkv, two(q_norm_b), two(k_norm_b), bias_b, out_b, lse_b, d_out_b)
    d_proj = (dq_a, dkv_a, dq_b, dk_b, dv_b, dgates)

    d_bm_rows = jnp.pad(d_bm.reshape(2, N_CHIPS, 256).transpose(1, 0, 2),
                        ((0, 0), (0, REST_ROWS - 514), (0, D_MODEL - 256)))
    rest = jnp.concatenate([d_wo.reshape(N_CHIPS, 256, D_MODEL), d_wa.reshape(N_CHIPS, 128, D_MODEL),
                            d_wb.reshape(N_CHIPS, 128, D_MODEL), d_bm_rows], axis=1)
    d_wt, d_wt_narrow = _d_w_in(d_proj, h)
    grads = [d_wt.reshape(N_CHIPS, W_IN_SHARD, D_MODEL), rest]
    narrow = [d_wt_narrow.reshape(N_CHIPS, W_IN_SHARD, D_MODEL), rest.astype(BF16)]
    chip_sums = start_reduce(grads, narrow) if start_reduce is not None else []
    grad_x, d_gain, others = _d_x(d_proj, w_t, x, norm_gain, dy, chip_sums)

    d_rel = jnp.concatenate(
        [_bias_grad(ds_a, A_HALF_WINDOW, 1)]
        + [_bias_grad(ds_b[4 * g:4 * g + 4], B_HALF_WINDOW, d) for g, d in enumerate(B_DILATIONS)], axis=1)
    d_sink = jnp.sum(dsink, axis=(2, 3)).reshape(1, 8)
    dgk_a_row = dgk_a[0]
    small = jnp.zeros((8, D_MODEL), F32)
    small = small.at[0].set(d_gain[0])
    small = small.at[1].set(d_rel.reshape(-1))
    misc = jnp.concatenate([_fold_heads(dgq_a), (dgk_a_row[:HEAD_DIM] + dgk_a_row[HEAD_DIM:]).reshape(1, HEAD_DIM),
                            _fold_heads(dgq_b), _fold_heads(dgk_b), d_sink], axis=1)
    small = small.at[2, :264].set(misc[0])

    return loss, grad_x, grads, small, chip_sums, others


def _unpack_weights(w_t_all, small_all):
    sm = small_all.reshape(N_CHIPS, SMALL_ROWS, D_MODEL)
    w_o = sm[:, 0:256].reshape(D_MODEL, D_MODEL)
    w_a = sm[:, 256:384].reshape(N_CHIPS, 512, 256).transpose(1, 0, 2).reshape(512, D_MODEL)
    w_b = sm[:, 384:512].reshape(N_CHIPS, 512, 256).transpose(1, 0, 2).reshape(512, D_MODEL)
    b_m = lax.bitcast_convert_type(sm[:, 512].reshape(N_CHIPS, 2, 256, 2), F32)
    return w_t_all, w_a, w_b, w_o, b_m.transpose(1, 0, 2).reshape(2, D_MODEL)


def _pack_small_weights(w_branch_a, w_branch_b, b_merge, w_out):
    b_m = jnp.pad(lax.bitcast_convert_type(b_merge, BF16).reshape(1, D_MODEL), ((0, SMALL_ROWS - 513), (0, 0)))
    return jnp.concatenate([w_out.astype(BF16), w_branch_a.astype(BF16).reshape(128, D_MODEL),
                            w_branch_b.astype(BF16).reshape(128, D_MODEL), b_m], axis=0)


def kernel(x, norm_gain, w_in, q_norm_a, k_norm_a, q_norm_b, k_norm_b, sink_a, rel_bias, w_branch_a, w_branch_b, b_merge, w_out, loss_target, m_norm_gain, m_w_in, m_q_norm_a, m_k_norm_a, m_q_norm_b, m_k_norm_b, m_sink_a, m_rel_bias, m_w_branch_a, m_w_branch_b, m_b_merge, m_w_out, v_norm_gain, v_w_in, v_q_norm_a, v_k_norm_a, v_q_norm_b, v_k_norm_b, v_sink_a, v_rel_bias, v_w_branch_a, v_w_branch_b, v_b_merge, v_w_out):
    wt_shard = _transpose_cast(w_in, BF16, "w_in_transpose")
    w_t, w_a, w_b, w_o, b_m = _unpack_weights(
        *_gather_weights(wt_shard, _pack_small_weights(w_branch_a[0], w_branch_b[0], b_merge[0], w_out[0])))

    place = _my_place()
    names = ("w_in", "rest")

    def start_reduce(grads, narrow):
        return [_add_halves(place, g, t, "reduce_add_halves_" + n) for g, t, n in zip(grads, _swap_halves(narrow), names)]

    loss_part, grad_x, _, small, chip_sums, others = _local_step(
        x[0], loss_target[0], norm_gain, w_t, w_a, w_b, w_o, b_m, q_norm_a, k_norm_a, q_norm_b, k_norm_b,
        sink_a, rel_bias, start_reduce)

    g_wt, g_rest = _join_halves([_add_chips(place, q, o, "reduce_add_chips_" + n)
                                 for q, o, n in zip(chip_sums, others, names)])
    small = _sum_devices(_gather_small(small.at[3, 0].set(loss_part)))
    loss = small[3, 0]

    g_w_in = _transpose_cast(g_wt, F32, "grad_w_in_transpose")
    g_w_out = g_rest[0:256]
    g_w_a = g_rest[256:384].reshape(512, 256)
    g_w_b = g_rest[384:512].reshape(512, 256)
    g_b_merge = g_rest[512:514, :256]
    g_norm_gain = small[0:1]
    g_rel_bias = small[1].reshape(N_BUCKETS, N_BUCKETS)
    g_q_a, g_k_a, g_q_b, g_k_b = (small[2:3, 64 * k:64 * k + 64] for k in range(4))
    g_sink = small[2:3, 256:264]

    big_names = (("w_in", w_in, g_w_in, m_w_in, v_w_in),
                 ("w_branch_a", w_branch_a, g_w_a, m_w_branch_a, v_w_branch_a),
                 ("w_branch_b", w_branch_b, g_w_b, m_w_branch_b, v_w_branch_b),
                 ("w_out", w_out, g_w_out, m_w_out, v_w_out))
    upd = {name: (g,) + tuple(_adamw(w[0], g, m[0], v[0], "adamw_" + name)) for name, w, g, m, v in big_names}
    small_names = ("norm_gain", "q_norm_a", "k_norm_a", "q_norm_b", "k_norm_b", "sink_a", "rel_bias", "b_merge")
    ws = [norm_gain, q_norm_a, k_norm_a, q_norm_b, k_norm_b, sink_a, rel_bias, b_merge[0]]
    gs = [g_norm_gain, g_q_a, g_k_a, g_q_b, g_k_b, g_sink, g_rel_bias, g_b_merge]
    ms = [m_norm_gain, m_q_norm_a, m_k_norm_a, m_q_norm_b, m_k_norm_b, m_sink_a, m_rel_bias, m_b_merge[0]]
    vs = [v_norm_gain, v_q_norm_a, v_k_norm_a, v_q_norm_b, v_k_norm_b, v_sink_a, v_rel_bias, v_b_merge[0]]
    ds, nms, nvs = _adamw_small(ws, gs, ms, vs)
    for k, name in enumerate(small_names):
        upd[name] = (gs[k], ds[k], nms[k], nvs[k])

    order = ("norm_gain", "w_in", "q_norm_a", "k_norm_a", "q_norm_b", "k_norm_b", "sink_a", "rel_bias",
             "w_branch_a", "w_branch_b", "b_merge", "w_out")
    lead = {"w_in", "w_branch_a", "w_branch_b", "b_merge", "w_out"}
    outs = [loss, grad_x[None]]
    for part in range(4):
        outs += [upd[name][part][None] if name in lead else upd[name][part] for name in order]
    return tuple(outs)
```

```python
import math

import numpy as np
import jax
import jax.numpy as jnp
from jax import lax
from jax.experimental import pallas as pl
from jax.experimental.pallas import tpu as pltpu

F32 = jnp.float32
BF16 = jnp.bfloat16

SEQ = 4096
D_MODEL = 1024
HEAD_DIM = 64
LANES = 128
EPS = 1e-6
NEG_INF = -1e30
SCALE = HEAD_DIM ** -0.5
N_BUCKETS = 32
MAX_DISTANCE = 1024
N_CHIPS = 4

A_HALF_WINDOW = 128
B_HALF_WINDOW = 64
B_DILATIONS = (1, 4, 16)
Q_BLOCK = 128

QKV_WIDTH = 5376
GATE_WIDTH = 3072
QA_BLK, KA_BLK, VA_BLK = 0, 4, 5
QB_BLK, KB_BLK, VB_BLK = 6, 18, 30
IN_WIDTH = QKV_WIDTH + GATE_WIDTH
W_IN_SHARD = IN_WIDTH // N_CHIPS

SMALL_ROWS = 544
REST_ROWS = 544

ADAM_LR = 0.001
ADAM_B1 = 0.9
ADAM_B2 = 0.999
ADAM_EPS = 1e-08
ADAM_WD = 0.01
ADAM_STEP = 10

VMEM_LIMIT = 56 * 1024 * 1024

NT = (((1,), (1,)), ((), ()))
TN = (((0,), (0,)), ((), ()))
MESH = pl.DeviceIdType.MESH
ANY = pl.BlockSpec(memory_space=pl.ANY)


def _dot(a, b, dims=None):
    if dims is None:
        return jnp.dot(a, b, preferred_element_type=F32)
    return lax.dot_general(a, b, dims, preferred_element_type=F32)


def _params(*semantics):
    return pltpu.CompilerParams(dimension_semantics=semantics or None, vmem_limit_bytes=VMEM_LIMIT)


def _bucket_onehot(half_window, stride):
    w = Q_BLOCK + 2 * half_window
    rel = (np.arange(w)[None, :] - half_window - np.arange(Q_BLOCK)[:, None])
    band = np.abs(rel) <= half_window
    rel = rel * stride
    half, max_exact = N_BUCKETS // 2, N_BUCKETS // 4
    n = np.abs(rel)
    nf = np.maximum(n, max_exact).astype(np.float32)
    large = max_exact + (np.log(nf / np.float32(max_exact)) / np.float32(math.log(MAX_DISTANCE / max_exact))
                         * np.float32(half - max_exact)).astype(np.int32)
    large = np.minimum(large, half - 1)
    bucket = (rel > 0).astype(np.int32) * half + np.where(n < max_exact, n, large)
    onehot = (bucket[..., None] == np.arange(N_BUCKETS)) & band[..., None]
    return onehot.reshape(Q_BLOCK * w, N_BUCKETS).astype(np.float32), band


def _bias_table(rel_bias_cols, half_window, stride):
    onehot, band = _bucket_onehot(half_window, stride)
    h = rel_bias_cols.shape[1]
    w = Q_BLOCK + 2 * half_window
    t = jnp.einsum("pb,bh->hp", jnp.asarray(onehot), rel_bias_cols, precision=lax.Precision.HIGHEST)
    t = t.reshape(h, Q_BLOCK, w) + jnp.asarray(np.where(band, 0.0, NEG_INF).astype(np.float32))
    return t.reshape(h // 2, 2, Q_BLOCK, w)


def _bias_grad(ds_sum, half_window, stride):
    onehot, _ = _bucket_onehot(half_window, stride)
    h = ds_sum.shape[0] * 2
    return jnp.einsum("pb,hp->bh", jnp.asarray(onehot), ds_sum.reshape(h, -1), precision=lax.Precision.HIGHEST)


def _transpose_cast(w, out_dtype, name):
    lead = (None,) * (w.ndim - 2)
    zero = (0,) * (w.ndim - 2)
    r, c = w.shape[-2:]

    def body(w_ref, o_ref):
        o_ref[...] = w_ref[...].T.astype(out_dtype)

    if r % LANES == 0:
        steps = pl.cdiv(c, LANES)
        in_spec = pl.BlockSpec(lead + (r, LANES), lambda j: zero + (0, j))
        out_spec = pl.BlockSpec((LANES, r), lambda j: (j, 0))
    else:
        steps = pl.cdiv(r, LANES)
        in_spec = pl.BlockSpec(lead + (LANES, c), lambda j: zero + (j, 0))
        out_spec = pl.BlockSpec((c, LANES), lambda j: (0, j))
    return pl.pallas_call(
        body, name=name, grid=(steps,), in_specs=[in_spec], out_specs=out_spec,
        out_shape=jax.ShapeDtypeStruct((c, r), out_dtype),
        compiler_params=_params("arbitrary"),
    )(w)


def _gather_weights(wt_shard, small_shard):
    bufs = ((W_IN_SHARD, IN_WIDTH), (SMALL_ROWS, N_CHIPS * SMALL_ROWS))

    stage_rows = 528

    def body(wt_in, sm_in, wt_out, sm_out, send_sems, recv_sems, in_sems, out_sems, stage):
        x, y, c = lax.axis_index("x"), lax.axis_index("y"), lax.axis_index("c")
        sibling = (x, y, 1 - c)
        my_chip = 2 * x + y
        refs = ((wt_in, wt_out), (sm_in, sm_out))

        def keep_own():
            pieces = [(b, r0) for b in range(2) for r0 in range(0, bufs[b][0], stage_rows)]
            outs = []
            for i, (b, r0) in enumerate(pieces):
                rows = min(stage_rows, bufs[b][0] - r0)
                slot = i % 2
                if i >= 2:
                    outs[i - 2].wait()
                buf = stage.at[slot, pl.ds(0, rows), :]
                load = pltpu.make_async_copy(refs[b][0].at[pl.ds(r0, rows), :], buf, in_sems.at[slot])
                load.start()
                load.wait()
                start = pl.multiple_of(my_chip * bufs[b][0] + r0, 16)
                outs.append(pltpu.make_async_copy(buf, refs[b][1].at[pl.ds(start, rows), :], out_sems.at[slot]))
                outs[i].start()
            for cp in outs[-2:]:
                cp.wait()

        def half_of(b, chip, half):
            rows = bufs[b][0]
            start = pl.multiple_of(chip * rows + half * (rows // 2), 16)
            return refs[b][1].at[pl.ds(start, rows // 2), :]

        def copy(k, src, dst, to):
            return pltpu.make_async_remote_copy(src_ref=src, dst_ref=dst, send_sem=send_sems.at[k],
                                                recv_sem=recv_sems.at[k], device_id=to, device_id_type=MESH)

        near = (x + (1 - c) - 2 * x * (1 - c), y + c - 2 * y * c)
        far = (x + c - 2 * x * c, y + (1 - c) - 2 * y * (1 - c))
        diag = (1 - x, 1 - y)
        chip_no = lambda chip: 2 * chip[0] + chip[1]
        sends, passed = [], []
        for b in range(2):
            rows = bufs[b][0]
            src = refs[b][0].at[pl.ds(pl.multiple_of(c * (rows // 2), 16), rows // 2), :]
            sends += [copy(3 * b, src, half_of(b, my_chip, c), (*near, c)),
                      copy(3 * b + 1, src, half_of(b, my_chip, c), (*far, c))]
        for cp in sends:
            cp.start()
        keep_own()

        def pass_on(b, j, chip):
            landed = half_of(b, chip_no(chip), c)
            fwd = copy(6 + 3 * b + j, landed, landed, sibling)
            fwd.start()
            passed.append(fwd)

        for b in range(2):
            landed = half_of(b, chip_no(near), c)
            copy(3 * b, landed, landed, sibling).wait_recv()
            relay = copy(3 * b + 2, landed, landed, (*far, c))
            relay.start()
            sends.append(relay)
            pass_on(b, 0, near)
        for b in range(2):
            for j, chip in ((1, far), (2, diag)):
                landed = half_of(b, chip_no(chip), c)
                copy(3 * b + j, landed, landed, sibling).wait_recv()
                pass_on(b, j, chip)
        for b in range(2):
            for j, chip in ((0, far), (1, near), (2, diag)):
                other = half_of(b, chip_no(chip), 1 - c)
                copy(6 + 3 * b + j, other, other, sibling).wait_recv()
        for cp in sends + passed:
            cp.wait_send()

    return pl.pallas_call(
        body, name="gather_weights",
        in_specs=[ANY, ANY], out_specs=[ANY, ANY],
        out_shape=[jax.ShapeDtypeStruct((bufs[0][1], D_MODEL), BF16),
                   jax.ShapeDtypeStruct((bufs[1][1], D_MODEL), BF16)],
        scratch_shapes=[pltpu.SemaphoreType.DMA((12,)), pltpu.SemaphoreType.DMA((12,)),
                        pltpu.SemaphoreType.DMA((2,)), pltpu.SemaphoreType.DMA((2,)),
                        pltpu.VMEM((2, stage_rows, D_MODEL), BF16)],
    )(wt_shard, small_shard)


W_BLOCK = 768


def _w_blocks(first, count):
    return [pl.BlockSpec((W_BLOCK, D_MODEL), lambda *_, k=k: (first + k, 0)) for k in range(count)]


def _in_proj(x, gain, w_t, first_block, n_blocks, out_dtype, name, keep_h):
    tm = 512

    def body(x_ref, g_ref, *refs):
        w_refs, outs = refs[:n_blocks], refs[n_blocks:]
        xf = x_ref[...]
        r = lax.rsqrt(jnp.mean(xf * xf, axis=-1, keepdims=True) + EPS)
        h = ((xf * r) * g_ref[...]).astype(BF16)
        if keep_h:
            outs[1][...] = h
        for k, w_ref in enumerate(w_refs):
            outs[0][:, k * W_BLOCK:(k + 1) * W_BLOCK] = _dot(h, w_ref[...], NT).astype(out_dtype)

    return pl.pallas_call(
        body, name=name, grid=(SEQ // tm,),
        in_specs=[pl.BlockSpec((tm, D_MODEL), lambda i: (i, 0)), pl.BlockSpec((1, D_MODEL), lambda i: (0, 0))]
        + _w_blocks(first_block, n_blocks),
        out_specs=[pl.BlockSpec((tm, W_BLOCK * n_blocks), lambda i: (i, 0)),
                   pl.BlockSpec((tm, D_MODEL), lambda i: (i, 0))][:2 if keep_h else 1],
        out_shape=[jax.ShapeDtypeStruct((SEQ, W_BLOCK * n_blocks), out_dtype),
                   jax.ShapeDtypeStruct((SEQ, D_MODEL), BF16)][:2 if keep_h else 1],
        compiler_params=_params("arbitrary"),
    )(x, gain, *([w_t] * n_blocks))


CHUNK = 256
CHUNK_UNROLL = 4
TILE_UNROLL = 8


def _low_half():
    return lax.broadcasted_iota(jnp.int32, (1, LANES), 1) < HEAD_DIM


def _half_sum(v, low):
    del low
    row = lax.broadcasted_iota(jnp.int32, (2 * LANES, LANES), 0)
    col = lax.broadcasted_iota(jnp.int32, (2 * LANES, LANES), 1)
    ones = jnp.where((row % LANES) // HEAD_DIM == col // HEAD_DIM, 1.0, 0.0).astype(BF16)
    hi = v.astype(BF16)
    lo = (v - hi.astype(F32)).astype(BF16)
    return _dot(jnp.concatenate([hi, lo], axis=1), ones)


def _chunks(fn, init=0):
    def body(i, carry):
        for u in range(CHUNK_UNROLL):
            carry = fn(pl.multiple_of((i * CHUNK_UNROLL + u) * CHUNK, CHUNK), carry)
        return carry

    return lax.fori_loop(0, SEQ // (CHUNK * CHUNK_UNROLL), body, init)


def _inv_rms(t, low):
    return lax.rsqrt(_half_sum(t * t, low) * (1.0 / HEAD_DIM) + EPS)


def _prep_q(q_ref, gain_ref, qn_ref):
    low = _low_half()

    def step(r0, carry):
        q = q_ref[pl.ds(r0, CHUNK), :].astype(F32)
        qn_ref[pl.ds(r0, CHUNK), :] = ((q * _inv_rms(q, low)) * gain_ref[...]) * SCALE
        return carry

    _chunks(step)


def _own_half(t, keep):
    return jnp.where(keep, t, pltpu.roll(t, HEAD_DIM, 1))


def _prep_kv(k_ref, v_ref, gain_ref, kp_ref, vp_ref, pad, keep=None):
    low = _low_half()
    zeros = jnp.zeros((pad, LANES), F32)
    for ref in (kp_ref, vp_ref):
        ref[pl.ds(0, pad), :] = zeros
        ref[pl.ds(pad + SEQ, pad), :] = zeros

    def step(r0, carry):
        k = k_ref[pl.ds(r0, CHUNK), :].astype(F32)
        v = v_ref[pl.ds(r0, CHUNK), :].astype(F32)
        kn = (k * _inv_rms(k, low)) * gain_ref[...]
        if keep is not None:
            kn, v = _own_half(kn, keep), _own_half(v, keep)
        kp_ref[pl.ds(pad + r0, CHUNK), :] = kn
        vp_ref[pl.ds(pad + r0, CHUNK), :] = v
        return carry

    _chunks(step)


def _tiles(d, half_window, fn):
    w = Q_BLOCK + 2 * half_window
    length = SEQ // d
    n_blocks = length // Q_BLOCK
    col = lax.broadcasted_iota(jnp.int32, (1, w), 1)

    def step(it, carry):
        c, n = it // n_blocks, it % n_blocks
        start = c + (d * Q_BLOCK) * n
        if d == 1:
            start = pl.multiple_of(start, Q_BLOCK)
            q_rows, k_rows = pl.ds(start, Q_BLOCK), pl.ds(start, w)
        else:
            q_rows, k_rows = pl.ds(start, Q_BLOCK, stride=d), pl.ds(start, w, stride=d)
        t = n * Q_BLOCK - half_window + col
        edge = jnp.where((t < 0) | (t >= length), NEG_INF, 0.0)
        fn(q_rows, k_rows, edge)
        return carry

    lax.fori_loop(0, d * n_blocks, step, 0, unroll=TILE_UNROLL)


def _stack_heads(t, low):
    return jnp.concatenate([jnp.where(low, t, 0.0), jnp.where(low, 0.0, t)], axis=0).astype(BF16)


def _unstack_heads(t, low):
    return jnp.where(low, t[:Q_BLOCK], t[Q_BLOCK:])


def _per_head(pair):
    return jnp.concatenate([jnp.full((Q_BLOCK, 1), pair[0], F32), jnp.full((Q_BLOCK, 1), pair[1], F32)], axis=0)


def _fwd_tiles(qn_ref, kp_ref, vp_ref, bias_ref, emit, *, d, half_window, sinks=None):
    low = _low_half()
    w = Q_BLOCK + 2 * half_window
    sink = None if sinks is None else _per_head(sinks)

    def tile(q_rows, k_rows, edge):
        q2 = _stack_heads(qn_ref[q_rows, :], low)
        k = kp_ref[k_rows, :].astype(BF16)
        v1 = jnp.concatenate([vp_ref[k_rows, :], jnp.ones((w, LANES), F32)], axis=1).astype(BF16)
        s = _dot(q2, k, NT) + bias_ref[...].reshape(2 * Q_BLOCK, w) + edge
        m = jnp.max(s, axis=-1, keepdims=True)
        if sink is not None:
            m = jnp.maximum(m, sink)
        o = _dot(jnp.exp(s - m).astype(BF16), v1)
        l = o[:, LANES:]
        if sink is not None:
            l = l + jnp.exp(sink - m)
        emit(q_rows, _unstack_heads(o[:, :LANES] * (1.0 / l), low), _unstack_heads(m + jnp.log(l), low))

    _tiles(d, half_window, tile)


def _bwd_tiles(qn_ref, kp_ref, vp_ref, bias_ref, do_ref, lse_ref, delta_ref, dq_ref, dk_ref, dv_ref, ds_ref,
               *, d, half_window, sinks=None, dsink_ref=None):
    low = _low_half()
    w = Q_BLOCK + 2 * half_window
    sink = None if sinks is None else _per_head(sinks)

    def rows_of(t):
        return jnp.concatenate([t[:, 0:1], t[:, HEAD_DIM:HEAD_DIM + 1]], axis=0)

    def tile(q_rows, k_rows, edge):
        q2 = _stack_heads(qn_ref[q_rows, :], low)
        do2 = _stack_heads(do_ref[q_rows, :], low)
        k = kp_ref[k_rows, :].astype(BF16)
        v = vp_ref[k_rows, :].astype(BF16)
        lse = rows_of(lse_ref[q_rows, :])
        delta = rows_of(delta_ref[q_rows, :])
        p = jnp.exp(_dot(q2, k, NT) + bias_ref[...].reshape(2 * Q_BLOCK, w) + edge - lse)
        ds = p * (_dot(do2, v, NT) - delta)
        ds_ref[...] += ds.reshape(2, Q_BLOCK, w)
        if sink is not None:
            dsink_ref[...] += (-jnp.exp(sink - lse) * delta).reshape(2, Q_BLOCK, 1)
        dsb, pb = ds.astype(BF16), p.astype(BF16)
        dq_ref[q_rows, :] = _unstack_heads(_dot(dsb, k), low)
        dk_ref[k_rows, :] += _dot(dsb, q2, TN)
        dv_ref[k_rows, :] += _dot(pb, do2, TN)

    _tiles(d, half_window, tile)


def _prep_delta(do_ref, o_ref, delta_ref):
    low = _low_half()

    def step(r0, carry):
        delta_ref[pl.ds(r0, CHUNK), :] = _half_sum(do_ref[pl.ds(r0, CHUNK), :] * o_ref[pl.ds(r0, CHUNK), :], low)
        return carry

    _chunks(step)


def _norm_bwd(raw_ref, gain_ref, dn_ref, dn_offset, out_ref, scale):
    low = _low_half()

    def step(r0, dgain):
        t = raw_ref[pl.ds(r0, CHUNK), :].astype(F32)
        dn = dn_ref[pl.ds(dn_offset + r0, CHUNK), :]
        dth = dn * (gain_ref[...] * scale)
        sums = _half_sum(jnp.concatenate([t * t, dth * t], axis=0), low)
        r = lax.rsqrt(sums[:CHUNK] * (1.0 / HEAD_DIM) + EPS)
        th = t * r
        out_ref[pl.ds(r0, CHUNK), :] = (r * (dth - th * (r * sums[CHUNK:] * (1.0 / HEAD_DIM)))).astype(BF16)
        return dgain + jnp.sum(dn * th, axis=0, keepdims=True) * scale

    return _chunks(step, jnp.zeros((1, LANES), F32))


def _rows8(v):
    return jnp.broadcast_to(v, (8, v.shape[-1]))


A_W = Q_BLOCK + 2 * A_HALF_WINDOW
A_PAD = A_HALF_WINDOW


def _seq_block(col_fn):
    return pl.BlockSpec((SEQ, LANES), col_fn)


def _attn_a_fwd(qkv, gain_q, gain_k, bias, sink):
    def body(sink_ref, q_ref, k_ref, v_ref, gq_ref, gk_ref, bias_ref, o_ref, lse_ref, qn_ref, kp_ref, vp_ref):
        hp = pl.program_id(0)
        keep = (lax.broadcasted_iota(jnp.int32, (1, LANES), 1) // HEAD_DIM) == hp // 2
        _prep_q(q_ref, gq_ref, qn_ref)
        _prep_kv(k_ref, v_ref, gk_ref, kp_ref, vp_ref, A_PAD, keep)

        def emit(rows, out, lse):
            o_ref[rows, :] = out
            lse_ref[rows, :] = lse

        _fwd_tiles(qn_ref, kp_ref, vp_ref, bias_ref, emit, d=1, half_window=A_HALF_WINDOW,
                   sinks=(sink_ref[2 * hp], sink_ref[2 * hp + 1]))

    vec = pl.BlockSpec((1, LANES), lambda hp, s: (0, 0))
    return pl.pallas_call(
        body, name="attn_a_fwd",
        grid_spec=pltpu.PrefetchScalarGridSpec(
            num_scalar_prefetch=1, grid=(4,),
            in_specs=[_seq_block(lambda hp, s: (0, QA_BLK + hp)), _seq_block(lambda hp, s: (0, KA_BLK)),
                      _seq_block(lambda hp, s: (0, VA_BLK)), vec, vec,
                      pl.BlockSpec((None, 2, Q_BLOCK, A_W), lambda hp, s: (hp, 0, 0, 0))],
            out_specs=[_seq_block(lambda hp, s: (0, hp)), _seq_block(lambda hp, s: (0, hp))],
            scratch_shapes=[pltpu.VMEM((SEQ, LANES), F32), pltpu.VMEM((SEQ + 2 * A_PAD, LANES), F32),
                            pltpu.VMEM((SEQ + 2 * A_PAD, LANES), F32)]),
        out_shape=[jax.ShapeDtypeStruct((SEQ, 512), F32)] * 2,
        compiler_params=_params("arbitrary"),
    )(sink.reshape(8), qkv, qkv, qkv, gain_q, gain_k, bias)


def _attn_a_bwd(qkv, gain_q, gain_k, bias, sink, out, lse, d_out):
    def body(sink_ref, q_ref, k_ref, v_ref, gq_ref, gk_ref, bias_ref, o_ref, lse_ref, do_ref,
             dq_out, dkv_out, dgq_out, dgk_out, ds_out, dsink_out,
             qn_ref, kp_ref, vp_ref, delta_ref, dq_ref, dk_ref, dv_ref, dk_tot, dv_tot):
        hp = pl.program_id(0)
        kv_head = hp // 2
        keep = (lax.broadcasted_iota(jnp.int32, (1, LANES), 1) // HEAD_DIM) == kv_head
        _prep_q(q_ref, gq_ref, qn_ref)
        _prep_kv(k_ref, v_ref, gk_ref, kp_ref, vp_ref, A_PAD, keep)
        _prep_delta(do_ref, o_ref, delta_ref)
        dk_ref[...] = jnp.zeros_like(dk_ref)
        dv_ref[...] = jnp.zeros_like(dv_ref)
        ds_out[...] = jnp.zeros_like(ds_out)
        dsink_out[...] = jnp.zeros_like(dsink_out)

        @pl.when(hp == 0)
        def _():
            dk_tot[...] = jnp.zeros_like(dk_tot)
            dv_tot[...] = jnp.zeros_like(dv_tot)

        _bwd_tiles(qn_ref, kp_ref, vp_ref, bias_ref, do_ref, lse_ref, delta_ref, dq_ref, dk_ref, dv_ref, ds_out,
                   d=1, half_window=A_HALF_WINDOW, sinks=(sink_ref[2 * hp], sink_ref[2 * hp + 1]),
                   dsink_ref=dsink_out)
        dgq_out[...] = _rows8(_norm_bwd(q_ref, gq_ref, dq_ref, 0, dq_out, SCALE))

        def fold(r0, carry):
            rows = pl.ds(A_PAD + r0, CHUNK)
            for acc, tot in ((dk_ref, dk_tot), (dv_ref, dv_tot)):
                t = acc[rows, :]
                tot[pl.ds(r0, CHUNK), :] += jnp.where(keep, t + pltpu.roll(t, HEAD_DIM, 1), 0.0)
            return carry

        _chunks(fold)

        @pl.when(hp == 3)
        def _():
            dgk_out[...] = _rows8(_norm_bwd(k_ref, gk_ref, dk_tot, 0, dkv_out.at[0], 1.0))
            dkv_out[1] = dv_tot[...].astype(BF16)

    vec = pl.BlockSpec((1, LANES), lambda hp, s: (0, 0))
    seq_f32 = pltpu.VMEM((SEQ, LANES), F32)
    padded = pltpu.VMEM((SEQ + 2 * A_PAD, LANES), F32)
    return pl.pallas_call(
        body, name="attn_a_bwd",
        grid_spec=pltpu.PrefetchScalarGridSpec(
            num_scalar_prefetch=1, grid=(4,),
            in_specs=[_seq_block(lambda hp, s: (0, QA_BLK + hp)), _seq_block(lambda hp, s: (0, KA_BLK)),
                      _seq_block(lambda hp, s: (0, VA_BLK)), vec, vec,
                      pl.BlockSpec((None, 2, Q_BLOCK, A_W), lambda hp, s: (hp, 0, 0, 0)),
                      _seq_block(lambda hp, s: (0, hp)), _seq_block(lambda hp, s: (0, hp)),
                      _seq_block(lambda hp, s: (0, hp))],
            out_specs=[pl.BlockSpec((None, SEQ, LANES), lambda hp, s: (hp, 0, 0)),
                       pl.BlockSpec((2, SEQ, LANES), lambda hp, s: (0, 0, 0)),
                       pl.BlockSpec((None, 8, LANES), lambda hp, s: (hp, 0, 0)),
                       pl.BlockSpec((8, LANES), lambda hp, s: (0, 0)),
                       pl.BlockSpec((None, 2, Q_BLOCK, A_W), lambda hp, s: (hp, 0, 0, 0)),
                       pl.BlockSpec((None, 2, Q_BLOCK, 1), lambda hp, s: (hp, 0, 0, 0))],
            scratch_shapes=[seq_f32, padded, padded, seq_f32, seq_f32, padded, padded, seq_f32, seq_f32]),
        out_shape=[jax.ShapeDtypeStruct((4, SEQ, LANES), BF16), jax.ShapeDtypeStruct((2, SEQ, LANES), BF16),
                   jax.ShapeDtypeStruct((4, 8, LANES), F32), jax.ShapeDtypeStruct((8, LANES), F32),
           jax.ShapeDtypeStruct((4, 2, Q_BLOCK, A_W), F32), jax.ShapeDtypeStruct((4, 2, Q_BLOCK, 1), F32)],
        compiler_params=_params("arbitrary"),
    )(sink.reshape(8), qkv, qkv, qkv, gain_q, gain_k, bias, out, lse, d_out)


B_W = Q_BLOCK + 2 * B_HALF_WINDOW
B_PAD_MAX = B_HALF_WINDOW * B_DILATIONS[-1]


def _attn_b_fwd(qkv, gain_q, gain_k, bias):
    def body(q_ref, k_ref, v_ref, gq_ref, gk_ref, bias_ref, o_ref, lse_ref, qn_ref, kp_ref, vp_ref):
        g = pl.program_id(1)
        _prep_q(q_ref, gq_ref, qn_ref)

        def first(rows, out, lse):
            o_ref[rows, :] = out
            lse_ref[rows, :] = lse

        def combine(rows, out, lse):
            old = lse_ref[rows, :]
            new = jnp.maximum(old, lse) + jnp.log(1.0 + jnp.exp(-jnp.abs(old - lse)))
            o_ref[rows, :] = o_ref[rows, :] * jnp.exp(old - new) + out * jnp.exp(lse - new)
            lse_ref[rows, :] = new

        for gi, d in enumerate(B_DILATIONS):
            @pl.when(g == gi)
            def _():
                _prep_kv(k_ref, v_ref, gk_ref, kp_ref, vp_ref, B_HALF_WINDOW * d)
                _fwd_tiles(qn_ref, kp_ref, vp_ref, bias_ref, first if gi == 0 else combine,
                           d=d, half_window=B_HALF_WINDOW)

    vec = pl.BlockSpec((1, LANES), lambda hp, g: (0, 0))
    padded = pltpu.VMEM((SEQ + 2 * B_PAD_MAX, LANES), F32)
    return pl.pallas_call(
        body, name="attn_b_fwd", grid=(4, 3),
        in_specs=[_seq_block(lambda hp, g: (0, QB_BLK + 4 * g + hp)), _seq_block(lambda hp, g: (0, KB_BLK + 4 * g + hp)),
                  _seq_block(lambda hp, g: (0, VB_BLK + 4 * g + hp)), vec, vec,
                  pl.BlockSpec((None, 2, Q_BLOCK, B_W), lambda hp, g: (4 * g + hp, 0, 0, 0))],
        out_specs=[_seq_block(lambda hp, g: (0, hp)), _seq_block(lambda hp, g: (0, hp))],
        out_shape=[jax.ShapeDtypeStruct((SEQ, 512), F32)] * 2,
        scratch_shapes=[pltpu.VMEM((SEQ, LANES), F32), padded, padded],
        compiler_params=_params("arbitrary", "arbitrary"),
    )(qkv, qkv, qkv, gain_q, gain_k, bias)


def _attn_b_bwd(qkv, gain_q, gain_k, bias, out, lse, d_out):
    def body(q_ref, k_ref, v_ref, gq_ref, gk_ref, bias_ref, o_ref, lse_ref, do_ref,
             dq_out, dk_out, dv_out, dgq_out, dgk_out, ds_out,
             qn_ref, kp_ref, vp_ref, delta_ref, dq_ref, dk_ref, dv_ref):
        g = pl.program_id(1)
        _prep_q(q_ref, gq_ref, qn_ref)
        _prep_delta(do_ref, o_ref, delta_ref)
        dk_ref[...] = jnp.zeros_like(dk_ref)
        dv_ref[...] = jnp.zeros_like(dv_ref)
        ds_out[...] = jnp.zeros_like(ds_out)
        for gi, d in enumerate(B_DILATIONS):
            @pl.when(g == gi)
            def _():
                pad = B_HALF_WINDOW * d
                _prep_kv(k_ref, v_ref, gk_ref, kp_ref, vp_ref, pad)
                _bwd_tiles(qn_ref, kp_ref, vp_ref, bias_ref, do_ref, lse_ref, delta_ref, dq_ref, dk_ref, dv_ref,
                           ds_out, d=d, half_window=B_HALF_WINDOW)
                dgk_out[...] = _rows8(_norm_bwd(k_ref, gk_ref, dk_ref, pad, dk_out, 1.0))
                dv_out[...] = dv_ref[pl.ds(pad, SEQ), :].astype(BF16)
        dgq_out[...] = _rows8(_norm_bwd(q_ref, gq_ref, dq_ref, 0, dq_out, SCALE))

    vec = pl.BlockSpec((1, LANES), lambda hp, g: (0, 0))
    seq_f32 = pltpu.VMEM((SEQ, LANES), F32)
    padded = pltpu.VMEM((SEQ + 2 * B_PAD_MAX, LANES), F32)
    part = pl.BlockSpec((None, 8, LANES), lambda hp, g: (4 * g + hp, 0, 0))
    return pl.pallas_call(
        body, name="attn_b_bwd", grid=(4, 3),
        in_specs=[_seq_block(lambda hp, g: (0, QB_BLK + 4 * g + hp)), _seq_block(lambda hp, g: (0, KB_BLK + 4 * g + hp)),
                  _seq_block(lambda hp, g: (0, VB_BLK + 4 * g + hp)), vec, vec,
                  pl.BlockSpec((None, 2, Q_BLOCK, B_W), lambda hp, g: (4 * g + hp, 0, 0, 0)),
                  _seq_block(lambda hp, g: (0, hp)), _seq_block(lambda hp, g: (0, hp)), _seq_block(lambda hp, g: (0, hp))],
        out_specs=[pl.BlockSpec((None, SEQ, LANES), lambda hp, g: (4 * g + hp, 0, 0))] * 3 + [
            part, part, pl.BlockSpec((None, 2, Q_BLOCK, B_W), lambda hp, g: (4 * g + hp, 0, 0, 0))],
        out_shape=[jax.ShapeDtypeStruct((12, SEQ, LANES), BF16)] * 3
        + [jax.ShapeDtypeStruct((12, 8, LANES), F32)] * 2 + [jax.ShapeDtypeStruct((12, 2, Q_BLOCK, B_W), F32)],
        scratch_shapes=[seq_f32, padded, padded, seq_f32, seq_f32, padded, padded],
        compiler_params=_params("arbitrary", "arbitrary"),
    )(qkv, qkv, qkv, gain_q, gain_k, bias, out, lse, d_out)


def _sigmoid(t):
    return 1.0 / (1.0 + jnp.exp(-t))


def _middle(out_a, out_b, gates, x, target, w_a, w_b, w_out, b_merge):
    tm = 256
    n_steps = SEQ // tm

    def body(oa_ref, ob_ref, g_ref, x_ref, t_ref, wa_ref, wb_ref, wo_ref, bm_ref,
             dy_ref, dg_ref, doa_ref, dob_ref, dwa_ref, dwb_ref, dwo_ref, dbm_ref, sq_ref):
        @pl.when(pl.program_id(0) == 0)
        def _():
            for ref in (dwa_ref, dwb_ref, dwo_ref, dbm_ref, sq_ref):
                ref[...] = jnp.zeros_like(ref)

        gate_a, gate_b = g_ref[:, 0:512], g_ref[:, 512:1024]
        sig_a, sig_b = _sigmoid(gate_a), _sigmoid(gate_b)
        silu_a, silu_b = gate_a * sig_a, gate_b * sig_b
        oa, ob = oa_ref[...], ob_ref[...]
        ya, yb = (oa * silu_a).astype(BF16), (ob * silu_b).astype(BF16)
        br_a, br_b = _dot(ya, wa_ref[...]), _dot(yb, wb_ref[...])
        m0 = _sigmoid(g_ref[:, 1024:2048] + bm_ref[0:1, :])
        m1 = _sigmoid(g_ref[:, 2048:3072] + bm_ref[1:2, :])
        merged = (m0 * br_a + m1 * br_b).astype(BF16)
        err = (x_ref[...] + _dot(merged, wo_ref[...])) - t_ref[...]
        sq_ref[...] += jnp.sum(err * err, axis=0, keepdims=True)

        dy = err * (1.0 / D_MODEL)
        dy_ref[...] = dy
        dyb = dy.astype(BF16)
        dmerged = _dot(dyb, wo_ref[...], NT)
        dwo_ref[...] += _dot(merged, dyb, TN)
        dbr_a, dbr_b = (dmerged * m0).astype(BF16), (dmerged * m1).astype(BF16)
        dm0 = (dmerged * br_a) * (m0 * (1.0 - m0))
        dm1 = (dmerged * br_b) * (m1 * (1.0 - m1))
        dbm_ref[0:1, :] += jnp.sum(dm0, axis=0, keepdims=True)
        dbm_ref[1:2, :] += jnp.sum(dm1, axis=0, keepdims=True)
        for s in range(N_CHIPS):
            cols = slice(256 * s, 256 * (s + 1))
            dwa_ref[s] += _dot(ya, dbr_a[:, cols], TN)
            dwb_ref[s] += _dot(yb, dbr_b[:, cols], TN)
        dya, dyb_ = _dot(dbr_a, wa_ref[...], NT), _dot(dbr_b, wb_ref[...], NT)
        doa_ref[...] = dya * silu_a
        dob_ref[...] = dyb_ * silu_b
        d_gates = (((dya * oa) * (sig_a * (1.0 + gate_a * (1.0 - sig_a)))).astype(BF16),
                   ((dyb_ * ob) * (sig_b * (1.0 + gate_b * (1.0 - sig_b)))).astype(BF16),
                   dm0.astype(BF16), dm1.astype(BF16))
        blk = 0
        for part in d_gates:
            for c0 in range(0, part.shape[1], 256):
                dg_ref[blk] = part[:, c0:c0 + 256]
                blk += 1

    def rows(width):
        return pl.BlockSpec((tm, width), lambda i: (i, 0))

    def whole(*shape):
        return pl.BlockSpec(shape, lambda i: (0,) * len(shape))

    return pl.pallas_call(
        body, name="middle", grid=(n_steps,),
        in_specs=[rows(512), rows(512), rows(GATE_WIDTH), rows(D_MODEL), rows(D_MODEL),
                  whole(512, D_MODEL), whole(512, D_MODEL), whole(D_MODEL, D_MODEL), whole(2, D_MODEL)],
        out_specs=[rows(D_MODEL), pl.BlockSpec((GATE_WIDTH // 256, tm, 256), lambda i: (0, i, 0)), rows(512), rows(512),
                   whole(N_CHIPS, 512, 256), whole(N_CHIPS, 512, 256), whole(D_MODEL, D_MODEL),
                   whole(2, D_MODEL), whole(1, D_MODEL)],
        out_shape=[jax.ShapeDtypeStruct((SEQ, D_MODEL), F32), jax.ShapeDtypeStruct((GATE_WIDTH // 256, SEQ, 256), BF16),
                   jax.ShapeDtypeStruct((SEQ, 512), F32), jax.ShapeDtypeStruct((SEQ, 512), F32),
                   jax.ShapeDtypeStruct((N_CHIPS, 512, 256), F32), jax.ShapeDtypeStruct((N_CHIPS, 512, 256), F32),
                   jax.ShapeDtypeStruct((D_MODEL, D_MODEL), F32), jax.ShapeDtypeStruct((2, D_MODEL), F32),
                   jax.ShapeDtypeStruct((1, D_MODEL), F32)],
        compiler_params=_params("arbitrary"),
    )(out_a, out_b, gates, x, target, w_a, w_b, w_out, b_merge)


def _which(j, edges, fns):
    lo = 0
    for hi, fn in zip(edges, fns):
        pl.when((j >= lo) & (j < hi))(fn)
        lo = hi


def _d_w_in(d_proj, h):
    plan, step, width = [], 0, 0
    for p in d_proj:
        total = p.shape[0] * p.shape[2]
        if width + total <= W_BLOCK:
            plan.append((p.shape[0], step, 1))
            width += total
            if width == W_BLOCK:
                step, width = step + 1, 0
        else:
            assert width == 0 and total % W_BLOCK == 0
            plan.append((W_BLOCK // p.shape[2], step, total // W_BLOCK))
            step += total // W_BLOCK
    assert width == 0 and step == IN_WIDTH // W_BLOCK
    firsts = sorted({first for _, first, _ in plan})
    edges = firsts[1:] + [step]
    halves = 2

    def body(*refs):
        pieces, h_ref, o_ref, b_ref = refs[:-3], refs[-3], refs[-2], refs[-1]
        k = pl.program_id(1)

        def emit(group):
            def fn():
                cols = jnp.concatenate([ref[b] for ref in group for b in range(ref.shape[0])], axis=1)
                term = _dot(cols, h_ref[...], TN)

                @pl.when(k == 0)
                def _():
                    o_ref[...] = term

                @pl.when(k == halves - 1)
                def _():
                    total = o_ref[...] + term
                    o_ref[...] = total
                    b_ref[...] = total.astype(BF16)
            return fn

        groups = [[ref for ref, (_, first, _) in zip(pieces, plan) if first == f] for f in firsts]
        _which(pl.program_id(0), edges, [emit(group) for group in groups])

    def cols_spec(piece, n, first, steps):
        def index(j, k):
            return jnp.clip(j - first, 0, steps - 1), jnp.where((j >= first) & (j < first + steps), k, 0), 0
        return pl.BlockSpec((n, SEQ // halves, piece.shape[2]), index)

    return pl.pallas_call(
        body, name="d_w_in", grid=(step, halves),
        in_specs=[cols_spec(p, *pl_) for p, pl_ in zip(d_proj, plan)]
        + [pl.BlockSpec((SEQ // halves, D_MODEL), lambda j, k: (k, 0))],
        out_specs=[pl.BlockSpec((W_BLOCK, D_MODEL), lambda j, k: (j, 0))] * 2,
        out_shape=[jax.ShapeDtypeStruct((IN_WIDTH, D_MODEL), F32), jax.ShapeDtypeStruct((IN_WIDTH, D_MODEL), BF16)],
        compiler_params=_params("arbitrary", "arbitrary"),
    )(*d_proj, h)


def _d_x(d_proj, w_t, x, gain, dy, chip_sums):
    tm = 256
    n_steps = SEQ // tm
    n_w = IN_WIDTH // W_BLOCK
    n_p, n_s = len(d_proj), len(chip_sums)

    def body(*refs):
        pieces, w_refs = refs[:n_p], refs[n_p:n_p + n_w]
        x_ref, g_ref, dy_ref = refs[n_p + n_w:n_p + n_w + 3]
        q_refs = refs[n_p + n_w + 3:n_p + n_w + 3 + n_s]
        dx_ref, dgain_ref = refs[n_p + n_w + 3 + n_s:n_p + n_w + 5 + n_s]
        o_refs = refs[n_p + n_w + 5 + n_s:n_p + n_w + 5 + 2 * n_s]
        send_sems, recv_sems = refs[n_p + n_w + 5 + 2 * n_s:] if n_s else (None, None)

        @pl.when(pl.program_id(0) == 0)
        def _():
            dgain_ref[...] = jnp.zeros_like(dgain_ref)
            if n_s:
                for cp in _scatter_copies(q_refs, o_refs, send_sems, recv_sems):
                    cp.start()

        blocks = [(piece, k) for piece in pieces for k in range(piece.shape[0])]
        dh, group, width, blk = None, [], 0, 0
        for piece, k in blocks:
            group.append(piece[k])
            width += piece.shape[2]
            if width == W_BLOCK:
                term = _dot(jnp.concatenate(group, axis=1), w_refs[blk][...])
                dh = term if dh is None else dh + term
                group, width, blk = [], 0, blk + 1
        assert not group and blk == n_w
        xf = x_ref[...]
        r = lax.rsqrt(jnp.mean(xf * xf, axis=-1, keepdims=True) + EPS)
        xh = xf * r
        dxh = dh * g_ref[...]
        dx_ref[...] = r * (dxh - xh * jnp.mean(dxh * xh, axis=-1, keepdims=True)) + dy_ref[...]
        dgain_ref[...] += _rows8(jnp.sum(dh * xh, axis=0, keepdims=True))

        if n_s:
            @pl.when(pl.program_id(0) == n_steps - 1)
            def _():
                for cp in _scatter_copies(q_refs, o_refs, send_sems, recv_sems):
                    cp.wait()

    row = pl.BlockSpec((tm, D_MODEL), lambda i: (i, 0))
    res = pl.pallas_call(
        body, name="d_x", grid=(n_steps,),
        in_specs=[pl.BlockSpec((p.shape[0], tm, p.shape[2]), lambda i: (0, i, 0)) for p in d_proj] + _w_blocks(0, n_w)
        + [row, pl.BlockSpec((1, D_MODEL), lambda i: (0, 0)), row] + [ANY] * n_s,
        out_specs=[row, pl.BlockSpec((8, D_MODEL), lambda i: (0, 0))] + [ANY] * n_s,
        out_shape=[jax.ShapeDtypeStruct((SEQ, D_MODEL), F32), jax.ShapeDtypeStruct((8, D_MODEL), F32)]
        + [jax.ShapeDtypeStruct((3,) + q.shape[1:], BF16) for q in chip_sums],
        scratch_shapes=[pltpu.SemaphoreType.DMA((3 * n_s,)), pltpu.SemaphoreType.DMA((3 * n_s,))] if n_s else [],
        compiler_params=_params("arbitrary"),
    )(*d_proj, *([w_t] * n_w), x, gain, dy, *chip_sums)
    return res[0], res[1], res[2:]


def _my_place():
    x, y, c = lax.axis_index("x"), lax.axis_index("y"), lax.axis_index("c")
    return jnp.stack([2 * x + y, c]).astype(jnp.int32)


def _half_rows(ref, half):
    rows = ref.shape[-2] // 2
    idx = (slice(None),) * (len(ref.shape) - 2) + (pl.ds(pl.multiple_of(half * rows, 16), rows), slice(None))
    return ref.at[idx]


def _swap_halves(grads):
    n = len(grads)

    def body(*refs):
        g_refs, o_refs, (send_sems, recv_sems) = refs[:n], refs[n:2 * n], refs[2 * n:]
        x, y, c = lax.axis_index("x"), lax.axis_index("y"), lax.axis_index("c")
        copies = [pltpu.make_async_remote_copy(src_ref=_half_rows(g, 1 - c), dst_ref=o, send_sem=send_sems.at[k],
                                               recv_sem=recv_sems.at[k], device_id=(x, y, 1 - c), device_id_type=MESH)
                  for k, (g, o) in enumerate(zip(g_refs, o_refs))]
        for cp in copies:
            cp.start()
        for cp in copies:
            cp.wait()

    return pl.pallas_call(
        body, name="reduce_swap_halves", in_specs=[ANY] * n, out_specs=[ANY] * n,
        out_shape=[jax.ShapeDtypeStruct((N_CHIPS, g.shape[1] // 2, D_MODEL), g.dtype) for g in grads],
        scratch_shapes=[pltpu.SemaphoreType.DMA((n,)), pltpu.SemaphoreType.DMA((n,))],
    )(*grads)


def _row_tile(rows):
    return max(t for t in range(16, 385, 16) if rows % t == 0)


def _add_halves(place, grads, theirs, name):
    half = theirs.shape[1]
    tr = _row_tile(half)
    n = half // tr

    def body(place_ref, g_ref, t_ref, o_ref):
        o_ref[...] = (g_ref[...] + t_ref[...].astype(F32)).astype(BF16)

    return pl.pallas_call(
        body, name=name,
        grid_spec=pltpu.PrefetchScalarGridSpec(
            num_scalar_prefetch=1, grid=(N_CHIPS, n),
            in_specs=[pl.BlockSpec((None, tr, D_MODEL), lambda s, i, p: (s, p[1] * n + i, 0)),
                      pl.BlockSpec((None, tr, D_MODEL), lambda s, i, p: (s, i, 0))],
            out_specs=pl.BlockSpec((None, tr, D_MODEL), lambda s, i, p: (s, i, 0))),
        out_shape=jax.ShapeDtypeStruct((N_CHIPS, half, D_MODEL), BF16),
        compiler_params=_params("arbitrary", "arbitrary"),
    )(place, grads, theirs)


def _scatter_copies(q_refs, o_refs, send_sems, recv_sems):
    x, y, c = lax.axis_index("x"), lax.axis_index("y"), lax.axis_index("c")
    chips = [(1 - x, y), (x, 1 - y), (1 - x, 1 - y)]
    return [pltpu.make_async_remote_copy(src_ref=q.at[2 * cx + cy], dst_ref=o.at[j],
                                         send_sem=send_sems.at[3 * k + j], recv_sem=recv_sems.at[3 * k + j],
                                         device_id=(cx, cy, c), device_id_type=MESH)
            for k, (q, o) in enumerate(zip(q_refs, o_refs)) for j, (cx, cy) in enumerate(chips)]


def _add_chips(place, chip_sums, others, name):
    half = others.shape[1]
    tr = _row_tile(half)
    n = half // tr

    def body(place_ref, q_ref, o_ref, r_ref):
        acc = q_ref[...].astype(F32)
        for j in range(3):
            acc = acc + o_ref[j].astype(F32)
        r_ref[...] = acc

    return pl.pallas_call(
        body, name=name,
        grid_spec=pltpu.PrefetchScalarGridSpec(
            num_scalar_prefetch=1, grid=(n,),
            in_specs=[pl.BlockSpec((None, tr, D_MODEL), lambda i, p: (p[0], i, 0)),
                      pl.BlockSpec((3, tr, D_MODEL), lambda i, p: (0, i, 0))],
            out_specs=pl.BlockSpec((tr, D_MODEL), lambda i, p: (p[1] * n + i, 0))),
        out_shape=jax.ShapeDtypeStruct((2 * half, D_MODEL), F32),
        compiler_params=_params("arbitrary"),
    )(place, chip_sums, others)


def _join_halves(shards):
    n = len(shards)

    def body(*refs):
        o_refs, (send_sems, recv_sems) = refs[n:2 * n], refs[2 * n:]
        x, y, c = lax.axis_index("x"), lax.axis_index("y"), lax.axis_index("c")

        def copy(k, rows):
            return pltpu.make_async_remote_copy(src_ref=rows, dst_ref=rows, send_sem=send_sems.at[k],
                                                recv_sem=recv_sems.at[k], device_id=(x, y, 1 - c), device_id_type=MESH)

        sends = [copy(k, _half_rows(o, c)) for k, o in enumerate(o_refs)]
        for cp in sends:
            cp.start()
        for k, o in enumerate(o_refs):
            copy(k, _half_rows(o, 1 - c)).wait_recv()
        for cp in sends:
            cp.wait_send()

    return pl.pallas_call(
        body, name="reduce_join_halves", in_specs=[ANY] * n, out_specs=[ANY] * n,
        out_shape=[jax.ShapeDtypeStruct(s.shape, F32) for s in shards],
        input_output_aliases={k: k for k in range(n)},
        scratch_shapes=[pltpu.SemaphoreType.DMA((n,)), pltpu.SemaphoreType.DMA((n,))],
    )(*shards)


def _gather_small(block):
    rows = block.shape[0]

    def body(b_ref, o_ref, send_sems, recv_sems, local_sem):
        x, y, c = lax.axis_index("x"), lax.axis_index("y"), lax.axis_index("c")
        me, sibling = (x, y, c), (x, y, 1 - c)
        chips = [(1 - x, y), (x, 1 - y), (1 - x, 1 - y)]

        def at(px, py, pc):
            return o_ref.at[pl.ds(pl.multiple_of((4 * px + 2 * py + pc) * rows, 8), rows), :]

        def copy(k, block_of, to, src=None):
            return pltpu.make_async_remote_copy(src_ref=at(*block_of) if src is None else src, dst_ref=at(*block_of),
                                                send_sem=send_sems.at[k], recv_sem=recv_sems.at[k],
                                                device_id=to, device_id_type=MESH)

        mine = pltpu.make_async_copy(b_ref, at(*me), local_sem)
        mine.start()
        first = [copy(0, me, sibling, src=b_ref)]
        first += [copy(1 + j, me, (*chip, c), src=b_ref) for j, chip in enumerate(chips)]
        for cp in first:
            cp.start()
        passed = [copy(4 + j, (*chip, c), sibling) for j, chip in enumerate(chips)]
        for j, chip in enumerate(chips):
            copy(1 + j, (*chip, c), me).wait_recv()
            passed[j].start()
        copy(0, sibling, me).wait_recv()
        for j, chip in enumerate(chips):
            copy(4 + j, (*chip, 1 - c), me).wait_recv()
        for cp in first + passed:
            cp.wait_send()
        mine.wait()

    return pl.pallas_call(
        body, name="gather_small_grads",
        in_specs=[pl.BlockSpec(memory_space=pltpu.VMEM)], out_specs=pl.BlockSpec(memory_space=pltpu.VMEM),
        out_shape=jax.ShapeDtypeStruct((8 * rows, D_MODEL), F32),
        scratch_shapes=[pltpu.SemaphoreType.DMA((7,)), pltpu.SemaphoreType.DMA((7,)), pltpu.SemaphoreType.DMA],
    )(block)


def _sum_devices(blocks):
    def body(b_ref, o_ref):
        acc = b_ref[0:8, :]
        for dev in range(1, 8):
            acc = acc + b_ref[8 * dev:8 * dev + 8, :]
        o_ref[...] = acc

    return pl.pallas_call(body, name="sum_small_grads", out_shape=jax.ShapeDtypeStruct((8, D_MODEL), F32))(blocks)


def _adamw_math(w, g, m, v):
    m = ADAM_B1 * m + (1.0 - ADAM_B1) * g
    v = ADAM_B2 * v + (1.0 - ADAM_B2) * (g * g)
    m_hat = m / (1.0 - ADAM_B1 ** ADAM_STEP)
    v_hat = v / (1.0 - ADAM_B2 ** ADAM_STEP)
    return -ADAM_LR * (m_hat / (jnp.sqrt(v_hat) + ADAM_EPS) + ADAM_WD * w), m, v


def _adamw(w, g, m, v, name):
    r, c = w.shape
    tr = 128 if r % 128 == 0 else r

    def body(w_ref, g_ref, m_ref, v_ref, d_ref, nm_ref, nv_ref):
        d_ref[...], nm_ref[...], nv_ref[...] = _adamw_math(w_ref[...], g_ref[...], m_ref[...], v_ref[...])

    spec = pl.BlockSpec((tr, c), lambda i: (i, 0))
    return pl.pallas_call(
        body, name=name, grid=(r // tr,), in_specs=[spec] * 4, out_specs=[spec] * 3,
        out_shape=[jax.ShapeDtypeStruct((r, c), F32)] * 3, compiler_params=_params("arbitrary"),
    )(w, g, m, v)


def _adamw_small(ws, gs, ms, vs):
    n = len(ws)

    def body(*refs):
        ins, outs = refs[:4 * n], refs[4 * n:]
        for k in range(n):
            d, m, v = _adamw_math(ins[k][...], ins[n + k][...], ins[2 * n + k][...], ins[3 * n + k][...])
            outs[k][...], outs[n + k][...], outs[2 * n + k][...] = d, m, v

    shapes = [jax.ShapeDtypeStruct(w.shape, F32) for w in ws]
    res = pl.pallas_call(body, name="adamw_small", out_shape=shapes * 3)(*ws, *gs, *ms, *vs)
    return res[:n], res[n:2 * n], res[2 * n:]


def _fold_heads(partials):
    t = jnp.sum(partials[:, 0, :], axis=0)
    return (t[:HEAD_DIM] + t[HEAD_DIM:]).reshape(1, HEAD_DIM)


def _local_step(x, target, norm_gain, w_t, w_a, w_b, w_o, b_m, q_norm_a, k_norm_a, q_norm_b, k_norm_b, sink_a,
                rel_bias, start_reduce=None):
    two = lambda gain: jnp.concatenate([gain, gain], axis=1)
    bias_a = _bias_table(rel_bias[:, :8], A_HALF_WINDOW, 1)
    bias_b = jnp.concatenate([_bias_table(rel_bias[:, 8 + 8 * g:16 + 8 * g], B_HALF_WINDOW, d)
                              for g, d in enumerate(B_DILATIONS)], axis=0)

    qkv, h = _in_proj(x, norm_gain, w_t, 0, QKV_WIDTH // W_BLOCK, BF16, "in_proj_qkv", True)
    gates, = _in_proj(x, norm_gain, w_t, QKV_WIDTH // W_BLOCK, GATE_WIDTH // W_BLOCK, F32, "in_proj_gates", False)
    out_a, lse_a = _attn_a_fwd(qkv, two(q_norm_a), two(k_norm_a), bias_a, sink_a)
    out_b, lse_b = _attn_b_fwd(qkv, two(q_norm_b), two(k_norm_b), bias_b)

    dy, dgates, d_out_a, d_out_b, d_wa, d_wb, d_wo, d_bm, sq = _middle(
        out_a, out_b, gates, x, target, w_a, w_b, w_o, b_m)
    loss = (0.5 / D_MODEL) * jnp.sum(sq)

    dq_a, dkv_a, dgq_a, dgk_a, ds_a, dsink = _attn_a_bwd(
        qkv, two(q_norm_a), two(k_norm_a), bias_a, sink_a, out_a, lse_a, d_out_a)
    dq_b, dk_b, dv_b, dgq_b, dgk_b, ds_b = _attn_b_bwd(
        qkv, two(q_norm_b), two(k_norm_b), bias_b, out_b, lse_b, d_out_b)
    d_proj = (dq_a, dkv_a, dq_b, dk_b, dv_b, dgates)

    d_bm_rows = jnp.pad(d_bm.reshape(2, N_CHIPS, 256).transpose(1, 0, 2),
                        ((0, 0), (0, REST_ROWS - 514), (0, D_MODEL - 256)))
    rest = jnp.concatenate([d_wo.reshape(N_CHIPS, 256, D_MODEL), d_wa.reshape(N_CHIPS, 128, D_MODEL),
                            d_wb.reshape(N_CHIPS, 128, D_MODEL), d_bm_rows], axis=1)
    d_wt, d_wt_narrow = _d_w_in(d_proj, h)
    grads = [d_wt.reshape(N_CHIPS, W_IN_SHARD, D_MODEL), rest]
    narrow = [d_wt_narrow.reshape(N_CHIPS, W_IN_SHARD, D_MODEL), rest.astype(BF16)]
    chip_sums = start_reduce(grads, narrow) if start_reduce is not None else []
    grad_x, d_gain, others = _d_x(d_proj, w_t, x, norm_gain, dy, chip_sums)

    d_rel = jnp.concatenate(
        [_bias_grad(ds_a, A_HALF_WINDOW, 1)]
        + [_bias_grad(ds_b[4 * g:4 * g + 4], B_HALF_WINDOW, d) for g, d in enumerate(B_DILATIONS)], axis=1)
    d_sink = jnp.sum(dsink, axis=(2, 3)).reshape(1, 8)
    dgk_a_row = dgk_a[0]
    small = jnp.zeros((8, D_MODEL), F32)
    small = small.at[0].set(d_gain[0])
    small = small.at[1].set(d_rel.reshape(-1))
    misc = jnp.concatenate([_fold_heads(dgq_a), (dgk_a_row[:HEAD_DIM] + dgk_a_row[HEAD_DIM:]).reshape(1, HEAD_DIM),
                            _fold_heads(dgq_b), _fold_heads(dgk_b), d_sink], axis=1)
    small = small.at[2, :264].set(misc[0])

    return loss, grad_x, grads, small, chip_sums, others


def _unpack_weights(w_t_all, small_all):
    sm = small_all.reshape(N_CHIPS, SMALL_ROWS, D_MODEL)
    w_o = sm[:, 0:256].reshape(D_MODEL, D_MODEL)
    w_a = sm[:, 256:384].reshape(N_CHIPS, 512, 256).transpose(1, 0, 2).reshape(512, D_MODEL)
    w_b = sm[:, 384:512].reshape(N_CHIPS, 512, 256).transpose(1, 0, 2).reshape(512, D_MODEL)
    b_m = lax.bitcast_convert_type(sm[:, 512].reshape(N_CHIPS, 2, 256, 2), F32)
    return w_t_all, w_a, w_b, w_o, b_m.transpose(1, 0, 2).reshape(2, D_MODEL)


def _pack_small_weights(w_branch_a, w_branch_b, b_merge, w_out):
    b_m = jnp.pad(lax.bitcast_convert_type(b_merge, BF16).reshape(1, D_MODEL), ((0, SMALL_ROWS - 513), (0, 0)))
    return jnp.concatenate([w_out.astype(BF16), w_branch_a.astype(BF16).reshape(128, D_MODEL),
                            w_branch_b.astype(BF16).reshape(128, D_MODEL), b_m], axis=0)


def kernel(x, norm_gain, w_in, q_norm_a, k_norm_a, q_norm_b, k_norm_b, sink_a, rel_bias, w_branch_a, w_branch_b, b_merge, w_out, loss_target, m_norm_gain, m_w_in, m_q_norm_a, m_k_norm_a, m_q_norm_b, m_k_norm_b, m_sink_a, m_rel_bias, m_w_branch_a, m_w_branch_b, m_b_merge, m_w_out, v_norm_gain, v_w_in, v_q_norm_a, v_k_norm_a, v_q_norm_b, v_k_norm_b, v_sink_a, v_rel_bias, v_w_branch_a, v_w_branch_b, v_b_merge, v_w_out):
    wt_shard = _transpose_cast(w_in, BF16, "w_in_transpose")
    w_t, w_a, w_b, w_o, b_m = _unpack_weights(
        *_gather_weights(wt_shard, _pack_small_weights(w_branch_a[0], w_branch_b[0], b_merge[0], w_out[0])))

    place = _my_place()
    names = ("w_in", "rest")

    def start_reduce(grads, narrow):
        return [_add_halves(place, g, t, "reduce_add_halves_" + n) for g, t, n in zip(grads, _swap_halves(narrow), names)]

    loss_part, grad_x, _, small, chip_sums, others = _local_step(
        x[0], loss_target[0], norm_gain, w_t, w_a, w_b, w_o, b_m, q_norm_a, k_norm_a, q_norm_b, k_norm_b,
        sink_a, rel_bias, start_reduce)

    g_wt, g_rest = _join_halves([_add_chips(place, q, o, "reduce_add_chips_" + n)
                                 for q, o, n in zip(chip_sums, others, names)])
    small = _sum_devices(_gather_small(small.at[3, 0].set(loss_part)))
    loss = small[3, 0]

    g_w_in = _transpose_cast(g_wt, F32, "grad_w_in_transpose")
    g_w_out = g_rest[0:256]
    g_w_a = g_rest[256:384].reshape(512, 256)
    g_w_b = g_rest[384:512].reshape(512, 256)
    g_b_merge = g_rest[512:514, :256]
    g_norm_gain = small[0:1]
    g_rel_bias = small[1].reshape(N_BUCKETS, N_BUCKETS)
    g_q_a, g_k_a, g_q_b, g_k_b = (small[2:3, 64 * k:64 * k + 64] for k in range(4))
    g_sink = small[2:3, 256:264]

    big_names = (("w_in", w_in, g_w_in, m_w_in, v_w_in),
                 ("w_branch_a", w_branch_a, g_w_a, m_w_branch_a, v_w_branch_a),
                 ("w_branch_b", w_branch_b, g_w_b, m_w_branch_b, v_w_branch_b),
                 ("w_out", w_out, g_w_out, m_w_out, v_w_out))
    upd = {name: (g,) + tuple(_adamw(w[0], g, m[0], v[0], "adamw_" + name)) for name, w, g, m, v in big_names}
    small_names = ("norm_gain", "q_norm_a", "k_norm_a", "q_norm_b", "k_norm_b", "sink_a", "rel_bias", "b_merge")
    ws = [norm_gain, q_norm_a, k_norm_a, q_norm_b, k_norm_b, sink_a, rel_bias, b_merge[0]]
    gs = [g_norm_gain, g_q_a, g_k_a, g_q_b, g_k_b, g_sink, g_rel_bias, g_b_merge]
    ms = [m_norm_gain, m_q_norm_a, m_k_norm_a, m_q_norm_b, m_k_norm_b, m_sink_a, m_rel_bias, m_b_merge[0]]
    vs = [v_norm_gain, v_q_norm_a, v_k_norm_a, v_q_norm_b, v_k_norm_b, v_sink_a, v_rel_bias, v_b_merge[0]]
    ds, nms, nvs = _adamw_small(ws, gs, ms, vs)
    for k, name in enumerate(small_names):
        upd[name] = (gs[k], ds[k], nms[k], nvs[k])

    order = ("norm_gain", "w_in", "q_norm_a", "k_norm_a", "q_norm_b", "k_norm_b", "sink_a", "rel_bias",
             "w_branch_a", "w_branch_b", "b_merge", "w_out")
    lead = {"w_in", "w_branch_a", "w_branch_b", "b_merge", "w_out"}
    outs = [loss, grad_x[None]]
    for part in range(4):
        outs += [upd[name][part][None] if name in lead else upd[name][part] for name in order]
    return tuple(outs)
```

```python
import math

import numpy as np
import jax
import jax.numpy as jnp
from jax import lax
from jax.experimental import pallas as pl
from jax.experimental.pallas import tpu as pltpu

F32 = jnp.float32
BF16 = jnp.bfloat16

SEQ = 4096
D_MODEL = 1024
HEAD_DIM = 64
LANES = 128
EPS = 1e-6
NEG_INF = -1e30
SCALE = HEAD_DIM ** -0.5
N_BUCKETS = 32
MAX_DISTANCE = 1024
N_CHIPS = 4

A_HALF_WINDOW = 128
B_HALF_WINDOW = 64
B_DILATIONS = (1, 4, 16)
Q_BLOCK = 128

QKV_WIDTH = 5376
GATE_WIDTH = 3072
QA_BLK, KA_BLK, VA_BLK = 0, 4, 5
QB_BLK, KB_BLK, VB_BLK = 6, 18, 30
IN_WIDTH = QKV_WIDTH + GATE_WIDTH
W_IN_SHARD = IN_WIDTH // N_CHIPS

SMALL_ROWS = 544
REST_ROWS = 544

ADAM_LR = 0.001
ADAM_B1 = 0.9
ADAM_B2 = 0.999
ADAM_EPS = 1e-08
ADAM_WD = 0.01
ADAM_STEP = 10

VMEM_LIMIT = 56 * 1024 * 1024

NT = (((1,), (1,)), ((), ()))
TN = (((0,), (0,)), ((), ()))
MESH = pl.DeviceIdType.MESH
ANY = pl.BlockSpec(memory_space=pl.ANY)


def _dot(a, b, dims=None):
    if dims is None:
        return jnp.dot(a, b, preferred_element_type=F32)
    return lax.dot_general(a, b, dims, preferred_element_type=F32)


def _params(*semantics):
    return pltpu.CompilerParams(dimension_semantics=semantics or None, vmem_limit_bytes=VMEM_LIMIT)


def _line_width(half_window):
    return pl.cdiv(2 * Q_BLOCK + 2 * half_window - 1, LANES) * LANES


def _bucket_onehot(half_window, stride):
    rel = np.arange(_line_width(half_window)) - (Q_BLOCK - 1) - half_window
    band = np.abs(rel) <= half_window
    rel = rel * stride
    half, max_exact = N_BUCKETS // 2, N_BUCKETS // 4
    n = np.abs(rel)
    nf = np.maximum(n, max_exact).astype(np.float32)
    large = max_exact + (np.log(nf / np.float32(max_exact)) / np.float32(math.log(MAX_DISTANCE / max_exact))
                         * np.float32(half - max_exact)).astype(np.int32)
    large = np.minimum(large, half - 1)
    bucket = (rel > 0).astype(np.int32) * half + np.where(n < max_exact, n, large)
    onehot = (bucket[..., None] == np.arange(N_BUCKETS)) & band[..., None]
    return onehot.astype(np.float32), band


def _bias_lines(rel_bias_cols, half_window, stride):
    onehot, band = _bucket_onehot(half_window, stride)
    h = rel_bias_cols.shape[1]
    t = jnp.einsum("tb,bh->ht", jnp.asarray(onehot), rel_bias_cols, precision=lax.Precision.HIGHEST)
    t = t + jnp.asarray(np.where(band, 0.0, NEG_INF).astype(np.float32))
    return t.reshape(h // 2, 2, -1)


def _bias_grad(d_lines, half_window, stride):
    onehot, _ = _bucket_onehot(half_window, stride)
    h = d_lines.shape[0] * 2
    return jnp.einsum("tb,ht->bh", jnp.asarray(onehot), d_lines.reshape(h, -1), precision=lax.Precision.HIGHEST)


def _unroll_bias(line_ref, tile_ref, w):
    width = line_ref.shape[1]
    for j in range(2):
        rows = jnp.broadcast_to(line_ref[j:j + 1, :], (Q_BLOCK, width))
        rows = pltpu.roll(rows, width - (Q_BLOCK - 1), 1, stride=1, stride_axis=0)
        tile_ref[j * Q_BLOCK:(j + 1) * Q_BLOCK, :] = rows[:, :w]


def _fold_bias_grad(tile_ref, line_ref, w):
    width = line_ref.shape[1]
    row = lax.broadcasted_iota(jnp.int32, (Q_BLOCK, Q_BLOCK), 0)
    col = lax.broadcasted_iota(jnp.int32, (Q_BLOCK, Q_BLOCK), 1)
    flip = jnp.where(row + col == Q_BLOCK - 1, 1.0, 0.0).astype(BF16)
    for j in range(2):
        tile = tile_ref[j * Q_BLOCK:(j + 1) * Q_BLOCK, :]
        hi = tile.astype(BF16)
        lo = (tile - hi.astype(F32)).astype(BF16)
        rows = _dot(flip, hi) + _dot(flip, lo)
        rows = jnp.concatenate([rows, jnp.zeros((Q_BLOCK, width - w), F32)], axis=1)
        rows = pltpu.roll(rows, 0, 1, stride=1, stride_axis=0)
        line_ref[j:j + 1, :] = jnp.sum(rows, axis=0, keepdims=True)


def _transpose_cast(w, out_dtype, name):
    lead = (None,) * (w.ndim - 2)
    zero = (0,) * (w.ndim - 2)
    r, c = w.shape[-2:]

    def body(w_ref, o_ref):
        o_ref[...] = w_ref[...].T.astype(out_dtype)

    if r % LANES == 0:
        steps = pl.cdiv(c, LANES)
        in_spec = pl.BlockSpec(lead + (r, LANES), lambda j: zero + (0, j))
        out_spec = pl.BlockSpec((LANES, r), lambda j: (j, 0))
    else:
        steps = pl.cdiv(r, LANES)
        in_spec = pl.BlockSpec(lead + (LANES, c), lambda j: zero + (j, 0))
        out_spec = pl.BlockSpec((c, LANES), lambda j: (0, j))
    return pl.pallas_call(
        body, name=name, grid=(steps,), in_specs=[in_spec], out_specs=out_spec,
        out_shape=jax.ShapeDtypeStruct((c, r), out_dtype),
        compiler_params=_params("arbitrary"),
    )(w)


def _gather_weights(wt_shard, small_shard):
    bufs = ((W_IN_SHARD, IN_WIDTH), (SMALL_ROWS, N_CHIPS * SMALL_ROWS))

    stage_rows = 528

    def body(wt_in, sm_in, wt_out, sm_out, send_sems, recv_sems, in_sems, out_sems, stage):
        x, y, c = lax.axis_index("x"), lax.axis_index("y"), lax.axis_index("c")
        sibling = (x, y, 1 - c)
        my_chip = 2 * x + y
        refs = ((wt_in, wt_out), (sm_in, sm_out))

        def keep_own():
            pieces = [(b, r0) for b in range(2) for r0 in range(0, bufs[b][0], stage_rows)]
            outs = []
            for i, (b, r0) in enumerate(pieces):
                rows = min(stage_rows, bufs[b][0] - r0)
                slot = i % 2
                if i >= 2:
                    outs[i - 2].wait()
                buf = stage.at[slot, pl.ds(0, rows), :]
                load = pltpu.make_async_copy(refs[b][0].at[pl.ds(r0, rows), :], buf, in_sems.at[slot])
                load.start()
                load.wait()
                start = pl.multiple_of(my_chip * bufs[b][0] + r0, 16)
                outs.append(pltpu.make_async_copy(buf, refs[b][1].at[pl.ds(start, rows), :], out_sems.at[slot]))
                outs[i].start()
            for cp in outs[-2:]:
                cp.wait()

        def half_of(b, chip, half):
            rows = bufs[b][0]
            start = pl.multiple_of(chip * rows + half * (rows // 2), 16)
            return refs[b][1].at[pl.ds(start, rows // 2), :]

        def copy(k, src, dst, to):
            return pltpu.make_async_remote_copy(src_ref=src, dst_ref=dst, send_sem=send_sems.at[k],
                                                recv_sem=recv_sems.at[k], device_id=to, device_id_type=MESH)

        near = (x + (1 - c) - 2 * x * (1 - c), y + c - 2 * y * c)
        far = (x + c - 2 * x * c, y + (1 - c) - 2 * y * (1 - c))
        diag = (1 - x, 1 - y)
        chip_no = lambda chip: 2 * chip[0] + chip[1]
        sends, passed = [], []
        for b in range(2):
            rows = bufs[b][0]
            src = refs[b][0].at[pl.ds(pl.multiple_of(c * (rows // 2), 16), rows // 2), :]
            sends += [copy(3 * b, src, half_of(b, my_chip, c), (*near, c)),
                      copy(3 * b + 1, src, half_of(b, my_chip, c), (*far, c))]
        for cp in sends:
            cp.start()
        keep_own()

        def pass_on(b, j, chip):
            landed = half_of(b, chip_no(chip), c)
            fwd = copy(6 + 3 * b + j, landed, landed, sibling)
            fwd.start()
            passed.append(fwd)

        for b in range(2):
            landed = half_of(b, chip_no(near), c)
            copy(3 * b, landed, landed, sibling).wait_recv()
            relay = copy(3 * b + 2, landed, landed, (*far, c))
            relay.start()
            sends.append(relay)
            pass_on(b, 0, near)
        for b in range(2):
            for j, chip in ((1, far), (2, diag)):
                landed = half_of(b, chip_no(chip), c)
                copy(3 * b + j, landed, landed, sibling).wait_recv()
                pass_on(b, j, chip)
        for b in range(2):
            for j, chip in ((0, far), (1, near), (2, diag)):
                other = half_of(b, chip_no(chip), 1 - c)
                copy(6 + 3 * b + j, other, other, sibling).wait_recv()
        for cp in sends + passed:
            cp.wait_send()

    return pl.pallas_call(
        body, name="gather_weights",
        in_specs=[ANY, ANY], out_specs=[ANY, ANY],
        out_shape=[jax.ShapeDtypeStruct((bufs[0][1], D_MODEL), BF16),
                   jax.ShapeDtypeStruct((bufs[1][1], D_MODEL), BF16)],
        scratch_shapes=[pltpu.SemaphoreType.DMA((12,)), pltpu.SemaphoreType.DMA((12,)),
                        pltpu.SemaphoreType.DMA((2,)), pltpu.SemaphoreType.DMA((2,)),
                        pltpu.VMEM((2, stage_rows, D_MODEL), BF16)],
    )(wt_shard, small_shard)


W_BLOCK = 768


def _w_blocks(first, count):
    return [pl.BlockSpec((W_BLOCK, D_MODEL), lambda *_, k=k: (first + k, 0)) for k in range(count)]


def _in_proj(x, gain, w_t, first_block, n_blocks, out_dtype, name, keep_h):
    tm = 512

    def body(x_ref, g_ref, *refs):
        w_refs, outs = refs[:n_blocks], refs[n_blocks:]
        xf = x_ref[...]
        r = lax.rsqrt(jnp.mean(xf * xf, axis=-1, keepdims=True) + EPS)
        h = ((xf * r) * g_ref[...]).astype(BF16)
        if keep_h:
            outs[1][...] = h
        for k, w_ref in enumerate(w_refs):
            outs[0][:, k * W_BLOCK:(k + 1) * W_BLOCK] = _dot(h, w_ref[...], NT).astype(out_dtype)

    return pl.pallas_call(
        body, name=name, grid=(SEQ // tm,),
        in_specs=[pl.BlockSpec((tm, D_MODEL), lambda i: (i, 0)), pl.BlockSpec((1, D_MODEL), lambda i: (0, 0))]
        + _w_blocks(first_block, n_blocks),
        out_specs=[pl.BlockSpec((tm, W_BLOCK * n_blocks), lambda i: (i, 0)),
                   pl.BlockSpec((tm, D_MODEL), lambda i: (i, 0))][:2 if keep_h else 1],
        out_shape=[jax.ShapeDtypeStruct((SEQ, W_BLOCK * n_blocks), out_dtype),
                   jax.ShapeDtypeStruct((SEQ, D_MODEL), BF16)][:2 if keep_h else 1],
        compiler_params=_params("arbitrary"),
    )(x, gain, *([w_t] * n_blocks))


CHUNK = 256
CHUNK_UNROLL = 4
TILE_UNROLL = 8


def _low_half():
    return lax.broadcasted_iota(jnp.int32, (1, LANES), 1) < HEAD_DIM


def _half_sum(v, low):
    del low
    row = lax.broadcasted_iota(jnp.int32, (2 * LANES, LANES), 0)
    col = lax.broadcasted_iota(jnp.int32, (2 * LANES, LANES), 1)
    ones = jnp.where((row % LANES) // HEAD_DIM == col // HEAD_DIM, 1.0, 0.0).astype(BF16)
    hi = v.astype(BF16)
    lo = (v - hi.astype(F32)).astype(BF16)
    return _dot(jnp.concatenate([hi, lo], axis=1), ones)


def _chunks(fn, init=0):
    def body(i, carry):
        for u in range(CHUNK_UNROLL):
            carry = fn(pl.multiple_of((i * CHUNK_UNROLL + u) * CHUNK, CHUNK), carry)
        return carry

    return lax.fori_loop(0, SEQ // (CHUNK * CHUNK_UNROLL), body, init)


def _inv_rms(t, low):
    return lax.rsqrt(_half_sum(t * t, low) * (1.0 / HEAD_DIM) + EPS)


def _prep_q(q_ref, gain_ref, qn_ref):
    low = _low_half()

    def step(r0, carry):
        q = q_ref[pl.ds(r0, CHUNK), :].astype(F32)
        qn_ref[pl.ds(r0, CHUNK), :] = ((q * _inv_rms(q, low)) * gain_ref[...]) * SCALE
        return carry

    _chunks(step)


def _own_half(t, keep):
    return jnp.where(keep, t, pltpu.roll(t, HEAD_DIM, 1))


def _prep_kv(k_ref, v_ref, gain_ref, kp_ref, vp_ref, pad, keep=None):
    low = _low_half()
    zeros = jnp.zeros((pad, LANES), F32)
    for ref in (kp_ref, vp_ref):
        ref[pl.ds(0, pad), :] = zeros
        ref[pl.ds(pad + SEQ, pad), :] = zeros

    def step(r0, carry):
        k = k_ref[pl.ds(r0, CHUNK), :].astype(F32)
        v = v_ref[pl.ds(r0, CHUNK), :].astype(F32)
        kn = (k * _inv_rms(k, low)) * gain_ref[...]
        if keep is not None:
            kn, v = _own_half(kn, keep), _own_half(v, keep)
        kp_ref[pl.ds(pad + r0, CHUNK), :] = kn
        vp_ref[pl.ds(pad + r0, CHUNK), :] = v
        return carry

    _chunks(step)


def _tiles(d, half_window, fn):
    w = Q_BLOCK + 2 * half_window
    length = SEQ // d
    n_blocks = length // Q_BLOCK
    col = lax.broadcasted_iota(jnp.int32, (1, w), 1)

    def step(it, carry):
        c, n = it // n_blocks, it % n_blocks
        start = c + (d * Q_BLOCK) * n
        if d == 1:
            start = pl.multiple_of(start, Q_BLOCK)
            q_rows, k_rows = pl.ds(start, Q_BLOCK), pl.ds(start, w)
        else:
            q_rows, k_rows = pl.ds(start, Q_BLOCK, stride=d), pl.ds(start, w, stride=d)
        t = n * Q_BLOCK - half_window + col
        edge = jnp.where((t < 0) | (t >= length), NEG_INF, 0.0)
        fn(q_rows, k_rows, edge)
        return carry

    lax.fori_loop(0, d * n_blocks, step, 0, unroll=TILE_UNROLL)


def _stack_heads(t, low):
    return jnp.concatenate([jnp.where(low, t, 0.0), jnp.where(low, 0.0, t)], axis=0).astype(BF16)


def _unstack_heads(t, low):
    return jnp.where(low, t[:Q_BLOCK], t[Q_BLOCK:])


def _per_head(pair):
    return jnp.concatenate([jnp.full((Q_BLOCK, 1), pair[0], F32), jnp.full((Q_BLOCK, 1), pair[1], F32)], axis=0)


def _fwd_tiles(qn_ref, kp_ref, vp_ref, bias_ref, emit, *, d, half_window, sinks=None):
    low = _low_half()
    w = Q_BLOCK + 2 * half_window
    sink = None if sinks is None else _per_head(sinks)

    def tile(q_rows, k_rows, edge):
        q2 = _stack_heads(qn_ref[q_rows, :], low)
        k = kp_ref[k_rows, :].astype(BF16)
        v1 = jnp.concatenate([vp_ref[k_rows, :], jnp.ones((w, LANES), F32)], axis=1).astype(BF16)
        s = _dot(q2, k, NT) + bias_ref[...] + edge
        m = jnp.max(s, axis=-1, keepdims=True)
        if sink is not None:
            m = jnp.maximum(m, sink)
        o = _dot(jnp.exp(s - m).astype(BF16), v1)
        l = o[:, LANES:]
        if sink is not None:
            l = l + jnp.exp(sink - m)
        emit(q_rows, _unstack_heads(o[:, :LANES] * (1.0 / l), low), _unstack_heads(m + jnp.log(l), low))

    _tiles(d, half_window, tile)


def _bwd_tiles(qn_ref, kp_ref, vp_ref, bias_ref, do_ref, lse_ref, delta_ref, dq_ref, dk_ref, dv_ref, ds_ref,
               *, d, half_window, sinks=None, dsink_ref=None):
    low = _low_half()
    w = Q_BLOCK + 2 * half_window
    sink = None if sinks is None else _per_head(sinks)

    def rows_of(t):
        return jnp.concatenate([t[:, 0:1], t[:, HEAD_DIM:HEAD_DIM + 1]], axis=0)

    def tile(q_rows, k_rows, edge):
        q2 = _stack_heads(qn_ref[q_rows, :], low)
        do2 = _stack_heads(do_ref[q_rows, :], low)
        k = kp_ref[k_rows, :].astype(BF16)
        v = vp_ref[k_rows, :].astype(BF16)
        lse = rows_of(lse_ref[q_rows, :])
        delta = rows_of(delta_ref[q_rows, :])
        p = jnp.exp(_dot(q2, k, NT) + bias_ref[...] + edge - lse)
        ds = p * (_dot(do2, v, NT) - delta)
        ds_ref[...] += ds
        if sink is not None:
            dsink_ref[...] += (-jnp.exp(sink - lse) * delta).reshape(2, Q_BLOCK, 1)
        dsb, pb = ds.astype(BF16), p.astype(BF16)
        dq_ref[q_rows, :] = _unstack_heads(_dot(dsb, k), low)
        dk_ref[k_rows, :] += _dot(dsb, q2, TN)
        dv_ref[k_rows, :] += _dot(pb, do2, TN)

    _tiles(d, half_window, tile)


def _prep_delta(do_ref, o_ref, delta_ref):
    low = _low_half()

    def step(r0, carry):
        delta_ref[pl.ds(r0, CHUNK), :] = _half_sum(do_ref[pl.ds(r0, CHUNK), :] * o_ref[pl.ds(r0, CHUNK), :], low)
        return carry

    _chunks(step)


def _norm_bwd(raw_ref, gain_ref, dn_ref, dn_offset, out_ref, scale):
    low = _low_half()

    def step(r0, dgain):
        t = raw_ref[pl.ds(r0, CHUNK), :].astype(F32)
        dn = dn_ref[pl.ds(dn_offset + r0, CHUNK), :]
        dth = dn * (gain_ref[...] * scale)
        sums = _half_sum(jnp.concatenate([t * t, dth * t], axis=0), low)
        r = lax.rsqrt(sums[:CHUNK] * (1.0 / HEAD_DIM) + EPS)
        th = t * r
        out_ref[pl.ds(r0, CHUNK), :] = (r * (dth - th * (r * sums[CHUNK:] * (1.0 / HEAD_DIM)))).astype(BF16)
        return dgain + jnp.sum(dn * th, axis=0, keepdims=True) * scale

    return _chunks(step, jnp.zeros((1, LANES), F32))


def _rows8(v):
    return jnp.broadcast_to(v, (8, v.shape[-1]))


A_W = Q_BLOCK + 2 * A_HALF_WINDOW
A_PAD = A_HALF_WINDOW


def _seq_block(col_fn):
    return pl.BlockSpec((SEQ, LANES), col_fn)


def _attn_a_fwd(qkv, gain_q, gain_k, bias, sink):
    def body(sink_ref, q_ref, k_ref, v_ref, gq_ref, gk_ref, line_ref, o_ref, lse_ref, qn_ref, kp_ref, vp_ref,
             bias_ref):
        hp = pl.program_id(0)
        keep = (lax.broadcasted_iota(jnp.int32, (1, LANES), 1) // HEAD_DIM) == hp // 2
        _prep_q(q_ref, gq_ref, qn_ref)
        _prep_kv(k_ref, v_ref, gk_ref, kp_ref, vp_ref, A_PAD, keep)
        _unroll_bias(line_ref, bias_ref, A_W)

        def emit(rows, out, lse):
            o_ref[rows, :] = out
            lse_ref[rows, :] = lse

        _fwd_tiles(qn_ref, kp_ref, vp_ref, bias_ref, emit, d=1, half_window=A_HALF_WINDOW,
                   sinks=(sink_ref[2 * hp], sink_ref[2 * hp + 1]))

    vec = pl.BlockSpec((1, LANES), lambda hp, s: (0, 0))
    return pl.pallas_call(
        body, name="attn_a_fwd",
        grid_spec=pltpu.PrefetchScalarGridSpec(
            num_scalar_prefetch=1, grid=(4,),
            in_specs=[_seq_block(lambda hp, s: (0, QA_BLK + hp)), _seq_block(lambda hp, s: (0, KA_BLK)),
                      _seq_block(lambda hp, s: (0, VA_BLK)), vec, vec,
                      pl.BlockSpec((None, 2, _line_width(A_HALF_WINDOW)), lambda hp, s: (hp, 0, 0))],
            out_specs=[_seq_block(lambda hp, s: (0, hp)), _seq_block(lambda hp, s: (0, hp))],
            scratch_shapes=[pltpu.VMEM((SEQ, LANES), F32), pltpu.VMEM((SEQ + 2 * A_PAD, LANES), F32),
                            pltpu.VMEM((SEQ + 2 * A_PAD, LANES), F32), pltpu.VMEM((2 * Q_BLOCK, A_W), F32)]),
        out_shape=[jax.ShapeDtypeStruct((SEQ, 512), F32)] * 2,
        compiler_params=_params("arbitrary"),
    )(sink.reshape(8), qkv, qkv, qkv, gain_q, gain_k, bias)


def _attn_a_bwd(qkv, gain_q, gain_k, bias, sink, out, lse, d_out):
    def body(sink_ref, q_ref, k_ref, v_ref, gq_ref, gk_ref, line_ref, o_ref, lse_ref, do_ref,
             dq_out, dkv_out, dgq_out, dgk_out, dline_out, dsink_out,
             qn_ref, kp_ref, vp_ref, delta_ref, dq_ref, dk_ref, dv_ref, dk_tot, dv_tot, bias_ref, ds_out):
        hp = pl.program_id(0)
        kv_head = hp // 2
        keep = (lax.broadcasted_iota(jnp.int32, (1, LANES), 1) // HEAD_DIM) == kv_head
        _prep_q(q_ref, gq_ref, qn_ref)
        _prep_kv(k_ref, v_ref, gk_ref, kp_ref, vp_ref, A_PAD, keep)
        _prep_delta(do_ref, o_ref, delta_ref)
        _unroll_bias(line_ref, bias_ref, A_W)
        dk_ref[...] = jnp.zeros_like(dk_ref)
        dv_ref[...] = jnp.zeros_like(dv_ref)
        ds_out[...] = jnp.zeros_like(ds_out)
        dsink_out[...] = jnp.zeros_like(dsink_out)

        @pl.when(hp == 0)
        def _():
            dk_tot[...] = jnp.zeros_like(dk_tot)
            dv_tot[...] = jnp.zeros_like(dv_tot)

        _bwd_tiles(qn_ref, kp_ref, vp_ref, bias_ref, do_ref, lse_ref, delta_ref, dq_ref, dk_ref, dv_ref, ds_out,
                   d=1, half_window=A_HALF_WINDOW, sinks=(sink_ref[2 * hp], sink_ref[2 * hp + 1]),
                   dsink_ref=dsink_out)
        _fold_bias_grad(ds_out, dline_out, A_W)
        dgq_out[...] = _rows8(_norm_bwd(q_ref, gq_ref, dq_ref, 0, dq_out, SCALE))

        def fold(r0, carry):
            rows = pl.ds(A_PAD + r0, CHUNK)
            for acc, tot in ((dk_ref, dk_tot), (dv_ref, dv_tot)):
                t = acc[rows, :]
                tot[pl.ds(r0, CHUNK), :] += jnp.where(keep, t + pltpu.roll(t, HEAD_DIM, 1), 0.0)
            return carry

        _chunks(fold)

        @pl.when(hp == 3)
        def _():
            dgk_out[...] = _rows8(_norm_bwd(k_ref, gk_ref, dk_tot, 0, dkv_out.at[0], 1.0))
            dkv_out[1] = dv_tot[...].astype(BF16)

    vec = pl.BlockSpec((1, LANES), lambda hp, s: (0, 0))
    seq_f32 = pltpu.VMEM((SEQ, LANES), F32)
    padded = pltpu.VMEM((SEQ + 2 * A_PAD, LANES), F32)
    return pl.pallas_call(
        body, name="attn_a_bwd",
        grid_spec=pltpu.PrefetchScalarGridSpec(
            num_scalar_prefetch=1, grid=(4,),
            in_specs=[_seq_block(lambda hp, s: (0, QA_BLK + hp)), _seq_block(lambda hp, s: (0, KA_BLK)),
                      _seq_block(lambda hp, s: (0, VA_BLK)), vec, vec,
                      pl.BlockSpec((None, 2, _line_width(A_HALF_WINDOW)), lambda hp, s: (hp, 0, 0)),
                      _seq_block(lambda hp, s: (0, hp)), _seq_block(lambda hp, s: (0, hp)),
                      _seq_block(lambda hp, s: (0, hp))],
            out_specs=[pl.BlockSpec((None, SEQ, LANES), lambda hp, s: (hp, 0, 0)),
                       pl.BlockSpec((2, SEQ, LANES), lambda hp, s: (0, 0, 0)),
                       pl.BlockSpec((None, 8, LANES), lambda hp, s: (hp, 0, 0)),
                       pl.BlockSpec((8, LANES), lambda hp, s: (0, 0)),
                       pl.BlockSpec((None, 2, _line_width(A_HALF_WINDOW)), lambda hp, s: (hp, 0, 0)),
                       pl.BlockSpec((None, 2, Q_BLOCK, 1), lambda hp, s: (hp, 0, 0, 0))],
            scratch_shapes=[seq_f32, padded, padded, seq_f32, seq_f32, padded, padded, seq_f32, seq_f32,
                            pltpu.VMEM((2 * Q_BLOCK, A_W), F32), pltpu.VMEM((2 * Q_BLOCK, A_W), F32)]),
        out_shape=[jax.ShapeDtypeStruct((4, SEQ, LANES), BF16), jax.ShapeDtypeStruct((2, SEQ, LANES), BF16),
                   jax.ShapeDtypeStruct((4, 8, LANES), F32), jax.ShapeDtypeStruct((8, LANES), F32),
                   jax.ShapeDtypeStruct((4, 2, _line_width(A_HALF_WINDOW)), F32),
                   jax.ShapeDtypeStruct((4, 2, Q_BLOCK, 1), F32)],
        compiler_params=_params("arbitrary"),
    )(sink.reshape(8), qkv, qkv, qkv, gain_q, gain_k, bias, out, lse, d_out)


B_W = Q_BLOCK + 2 * B_HALF_WINDOW
B_PAD_MAX = B_HALF_WINDOW * B_DILATIONS[-1]


def _attn_b_fwd(qkv, gain_q, gain_k, bias):
    def body(q_ref, k_ref, v_ref, gq_ref, gk_ref, line_ref, o_ref, lse_ref, qn_ref, kp_ref, vp_ref, bias_ref):
        g = pl.program_id(1)
        _prep_q(q_ref, gq_ref, qn_ref)
        _unroll_bias(line_ref, bias_ref, B_W)

        def first(rows, out, lse):
            o_ref[rows, :] = out
            lse_ref[rows, :] = lse

        def combine(rows, out, lse):
            old = lse_ref[rows, :]
            new = jnp.maximum(old, lse) + jnp.log(1.0 + jnp.exp(-jnp.abs(old - lse)))
            o_ref[rows, :] = o_ref[rows, :] * jnp.exp(old - new) + out * jnp.exp(lse - new)
            lse_ref[rows, :] = new

        for gi, d in enumerate(B_DILATIONS):
            @pl.when(g == gi)
            def _():
                _prep_kv(k_ref, v_ref, gk_ref, kp_ref, vp_ref, B_HALF_WINDOW * d)
                _fwd_tiles(qn_ref, kp_ref, vp_ref, bias_ref, first if gi == 0 else combine,
                           d=d, half_window=B_HALF_WINDOW)

    vec = pl.BlockSpec((1, LANES), lambda hp, g: (0, 0))
    padded = pltpu.VMEM((SEQ + 2 * B_PAD_MAX, LANES), F32)
    return pl.pallas_call(
        body, name="attn_b_fwd", grid=(4, 3),
        in_specs=[_seq_block(lambda hp, g: (0, QB_BLK + 4 * g + hp)), _seq_block(lambda hp, g: (0, KB_BLK + 4 * g + hp)),
                  _seq_block(lambda hp, g: (0, VB_BLK + 4 * g + hp)), vec, vec,
                  pl.BlockSpec((None, 2, _line_width(B_HALF_WINDOW)), lambda hp, g: (4 * g + hp, 0, 0))],
        out_specs=[_seq_block(lambda hp, g: (0, hp)), _seq_block(lambda hp, g: (0, hp))],
        out_shape=[jax.ShapeDtypeStruct((SEQ, 512), F32)] * 2,
        scratch_shapes=[pltpu.VMEM((SEQ, LANES), F32), padded, padded, pltpu.VMEM((2 * Q_BLOCK, B_W), F32)],
        compiler_params=_params("arbitrary", "arbitrary"),
    )(qkv, qkv, qkv, gain_q, gain_k, bias)


def _attn_b_bwd(qkv, gain_q, gain_k, bias, out, lse, d_out):
    def body(q_ref, k_ref, v_ref, gq_ref, gk_ref, line_ref, o_ref, lse_ref, do_ref,
             dq_out, dk_out, dv_out, dgq_out, dgk_out, dline_out,
             qn_ref, kp_ref, vp_ref, delta_ref, dq_ref, dk_ref, dv_ref, bias_ref, ds_out):
        g = pl.program_id(1)
        _prep_q(q_ref, gq_ref, qn_ref)
        _prep_delta(do_ref, o_ref, delta_ref)
        _unroll_bias(line_ref, bias_ref, B_W)
        dk_ref[...] = jnp.zeros_like(dk_ref)
        dv_ref[...] = jnp.zeros_like(dv_ref)
        ds_out[...] = jnp.zeros_like(ds_out)
        for gi, d in enumerate(B_DILATIONS):
            @pl.when(g == gi)
            def _():
                pad = B_HALF_WINDOW * d
                _prep_kv(k_ref, v_ref, gk_ref, kp_ref, vp_ref, pad)
                _bwd_tiles(qn_ref, kp_ref, vp_ref, bias_ref, do_ref, lse_ref, delta_ref, dq_ref, dk_ref, dv_ref,
                           ds_out, d=d, half_window=B_HALF_WINDOW)
                dgk_out[...] = _rows8(_norm_bwd(k_ref, gk_ref, dk_ref, pad, dk_out, 1.0))
                dv_out[...] = dv_ref[pl.ds(pad, SEQ), :].astype(BF16)
        _fold_bias_grad(ds_out, dline_out, B_W)
        dgq_out[...] = _rows8(_norm_bwd(q_ref, gq_ref, dq_ref, 0, dq_out, SCALE))

    vec = pl.BlockSpec((1, LANES), lambda hp, g: (0, 0))
    seq_f32 = pltpu.VMEM((SEQ, LANES), F32)
    padded = pltpu.VMEM((SEQ + 2 * B_PAD_MAX, LANES), F32)
    part = pl.BlockSpec((None, 8, LANES), lambda hp, g: (4 * g + hp, 0, 0))
    line = pl.BlockSpec((None, 2, _line_width(B_HALF_WINDOW)), lambda hp, g: (4 * g + hp, 0, 0))
    return pl.pallas_call(
        body, name="attn_b_bwd", grid=(4, 3),
        in_specs=[_seq_block(lambda hp, g: (0, QB_BLK + 4 * g + hp)), _seq_block(lambda hp, g: (0, KB_BLK + 4 * g + hp)),
                  _seq_block(lambda hp, g: (0, VB_BLK + 4 * g + hp)), vec, vec,
                  line,
                  _seq_block(lambda hp, g: (0, hp)), _seq_block(lambda hp, g: (0, hp)), _seq_block(lambda hp, g: (0, hp))],
        out_specs=[pl.BlockSpec((None, SEQ, LANES), lambda hp, g: (4 * g + hp, 0, 0))] * 3 + [part, part, line],
        out_shape=[jax.ShapeDtypeStruct((12, SEQ, LANES), BF16)] * 3
        + [jax.ShapeDtypeStruct((12, 8, LANES), F32)] * 2
        + [jax.ShapeDtypeStruct((12, 2, _line_width(B_HALF_WINDOW)), F32)],
        scratch_shapes=[seq_f32, padded, padded, seq_f32, seq_f32, padded, padded,
                        pltpu.VMEM((2 * Q_BLOCK, B_W), F32), pltpu.VMEM((2 * Q_BLOCK, B_W), F32)],
        compiler_params=_params("arbitrary", "arbitrary"),
    )(qkv, qkv, qkv, gain_q, gain_k, bias, out, lse, d_out)


def _sigmoid(t):
    return 1.0 / (1.0 + jnp.exp(-t))


def _middle(out_a, out_b, gates, x, target, w_a, w_b, w_out, b_merge):
    tm = 256
    n_steps = SEQ // tm

    def body(oa_ref, ob_ref, g_ref, x_ref, t_ref, wa_ref, wb_ref, wo_ref, bm_ref,
             dy_ref, dg_ref, doa_ref, dob_ref, dwa_ref, dwb_ref, dwo_ref, dbm_ref, sq_ref):
        @pl.when(pl.program_id(0) == 0)
        def _():
            for ref in (dwa_ref, dwb_ref, dwo_ref, dbm_ref, sq_ref):
                ref[...] = jnp.zeros_like(ref)

        gate_a, gate_b = g_ref[:, 0:512], g_ref[:, 512:1024]
        sig_a, sig_b = _sigmoid(gate_a), _sigmoid(gate_b)
        silu_a, silu_b = gate_a * sig_a, gate_b * sig_b
        oa, ob = oa_ref[...], ob_ref[...]
        ya, yb = (oa * silu_a).astype(BF16), (ob * silu_b).astype(BF16)
        br_a, br_b = _dot(ya, wa_ref[...]), _dot(yb, wb_ref[...])
        m0 = _sigmoid(g_ref[:, 1024:2048] + bm_ref[0:1, :])
        m1 = _sigmoid(g_ref[:, 2048:3072] + bm_ref[1:2, :])
        merged = (m0 * br_a + m1 * br_b).astype(BF16)
        err = (x_ref[...] + _dot(merged, wo_ref[...])) - t_ref[...]
        sq_ref[...] += jnp.sum(err * err, axis=0, keepdims=True)

        dy = err * (1.0 / D_MODEL)
        dy_ref[...] = dy
        dyb = dy.astype(BF16)
        dmerged = _dot(dyb, wo_ref[...], NT)
        dwo_ref[...] += _dot(merged, dyb, TN)
        dbr_a, dbr_b = (dmerged * m0).astype(BF16), (dmerged * m1).astype(BF16)
        dm0 = (dmerged * br_a) * (m0 * (1.0 - m0))
        dm1 = (dmerged * br_b) * (m1 * (1.0 - m1))
        dbm_ref[0:1, :] += jnp.sum(dm0, axis=0, keepdims=True)
        dbm_ref[1:2, :] += jnp.sum(dm1, axis=0, keepdims=True)
        for s in range(N_CHIPS):
            cols = slice(256 * s, 256 * (s + 1))
            dwa_ref[s] += _dot(ya, dbr_a[:, cols], TN)
            dwb_ref[s] += _dot(yb, dbr_b[:, cols], TN)
        dya, dyb_ = _dot(dbr_a, wa_ref[...], NT), _dot(dbr_b, wb_ref[...], NT)
        doa_ref[...] = dya * silu_a
        dob_ref[...] = dyb_ * silu_b
        d_gates = (((dya * oa) * (sig_a * (1.0 + gate_a * (1.0 - sig_a)))).astype(BF16),
                   ((dyb_ * ob) * (sig_b * (1.0 + gate_b * (1.0 - sig_b)))).astype(BF16),
                   dm0.astype(BF16), dm1.astype(BF16))
        blk = 0
        for part in d_gates:
            for c0 in range(0, part.shape[1], 256):
                dg_ref[blk] = part[:, c0:c0 + 256]
                blk += 1

    def rows(width):
        return pl.BlockSpec((tm, width), lambda i: (i, 0))

    def whole(*shape):
        return pl.BlockSpec(shape, lambda i: (0,) * len(shape))

    return pl.pallas_call(
        body, name="middle", grid=(n_steps,),
        in_specs=[rows(512), rows(512), rows(GATE_WIDTH), rows(D_MODEL), rows(D_MODEL),
                  whole(512, D_MODEL), whole(512, D_MODEL), whole(D_MODEL, D_MODEL), whole(2, D_MODEL)],
        out_specs=[rows(D_MODEL), pl.BlockSpec((GATE_WIDTH // 256, tm, 256), lambda i: (0, i, 0)), rows(512), rows(512),
                   whole(N_CHIPS, 512, 256), whole(N_CHIPS, 512, 256), whole(D_MODEL, D_MODEL),
                   whole(2, D_MODEL), whole(1, D_MODEL)],
        out_shape=[jax.ShapeDtypeStruct((SEQ, D_MODEL), F32), jax.ShapeDtypeStruct((GATE_WIDTH // 256, SEQ, 256), BF16),
                   jax.ShapeDtypeStruct((SEQ, 512), F32), jax.ShapeDtypeStruct((SEQ, 512), F32),
                   jax.ShapeDtypeStruct((N_CHIPS, 512, 256), F32), jax.ShapeDtypeStruct((N_CHIPS, 512, 256), F32),
                   jax.ShapeDtypeStruct((D_MODEL, D_MODEL), F32), jax.ShapeDtypeStruct((2, D_MODEL), F32),
                   jax.ShapeDtypeStruct((1, D_MODEL), F32)],
        compiler_params=_params("arbitrary"),
    )(out_a, out_b, gates, x, target, w_a, w_b, w_out, b_merge)


def _which(j, edges, fns):
    lo = 0
    for hi, fn in zip(edges, fns):
        pl.when((j >= lo) & (j < hi))(fn)
        lo = hi


def _d_w_in(d_proj, h):
    plan, step, width = [], 0, 0
    for p in d_proj:
        total = p.shape[0] * p.shape[2]
        if width + total <= W_BLOCK:
            plan.append((p.shape[0], step, 1))
            width += total
            if width == W_BLOCK:
                step, width = step + 1, 0
        else:
            assert width == 0 and total % W_BLOCK == 0
            plan.append((W_BLOCK // p.shape[2], step, total // W_BLOCK))
            step += total // W_BLOCK
    assert width == 0 and step == IN_WIDTH // W_BLOCK
    firsts = sorted({first for _, first, _ in plan})
    edges = firsts[1:] + [step]
    halves = 2

    def body(*refs):
        pieces, h_ref, o_ref, b_ref = refs[:-3], refs[-3], refs[-2], refs[-1]
        k = pl.program_id(1)

        def emit(group):
            def fn():
                cols = jnp.concatenate([ref[b] for ref in group for b in range(ref.shape[0])], axis=1)
                term = _dot(cols, h_ref[...], TN)

                @pl.when(k == 0)
                def _():
                    o_ref[...] = term

                @pl.when(k == halves - 1)
                def _():
                    total = o_ref[...] + term
                    o_ref[...] = total
                    b_ref[...] = total.astype(BF16)
            return fn

        groups = [[ref for ref, (_, first, _) in zip(pieces, plan) if first == f] for f in firsts]
        _which(pl.program_id(0), edges, [emit(group) for group in groups])

    def cols_spec(piece, n, first, steps):
        def index(j, k):
            return jnp.clip(j - first, 0, steps - 1), jnp.where((j >= first) & (j < first + steps), k, 0), 0
        return pl.BlockSpec((n, SEQ // halves, piece.shape[2]), index)

    return pl.pallas_call(
        body, name="d_w_in", grid=(step, halves),
        in_specs=[cols_spec(p, *pl_) for p, pl_ in zip(d_proj, plan)]
        + [pl.BlockSpec((SEQ // halves, D_MODEL), lambda j, k: (k, 0))],
        out_specs=[pl.BlockSpec((W_BLOCK, D_MODEL), lambda j, k: (j, 0))] * 2,
        out_shape=[jax.ShapeDtypeStruct((IN_WIDTH, D_MODEL), F32), jax.ShapeDtypeStruct((IN_WIDTH, D_MODEL), BF16)],
        compiler_params=_params("arbitrary", "arbitrary"),
    )(*d_proj, h)


def _d_x(d_proj, w_t, x, gain, dy, chip_sums):
    tm = 256
    n_steps = SEQ // tm
    n_w = IN_WIDTH // W_BLOCK
    n_p, n_s = len(d_proj), len(chip_sums)

    def body(*refs):
        pieces, w_refs = refs[:n_p], refs[n_p:n_p + n_w]
        x_ref, g_ref, dy_ref = refs[n_p + n_w:n_p + n_w + 3]
        q_refs = refs[n_p + n_w + 3:n_p + n_w + 3 + n_s]
        dx_ref, dgain_ref = refs[n_p + n_w + 3 + n_s:n_p + n_w + 5 + n_s]
        o_refs = refs[n_p + n_w + 5 + n_s:n_p + n_w + 5 + 2 * n_s]
        send_sems, recv_sems = refs[n_p + n_w + 5 + 2 * n_s:] if n_s else (None, None)

        @pl.when(pl.program_id(0) == 0)
        def _():
            dgain_ref[...] = jnp.zeros_like(dgain_ref)
            if n_s:
                for cp in _scatter_copies(q_refs, o_refs, send_sems, recv_sems):
                    cp.start()

        blocks = [(piece, k) for piece in pieces for k in range(piece.shape[0])]
        dh, group, width, blk = None, [], 0, 0
        for piece, k in blocks:
            group.append(piece[k])
            width += piece.shape[2]
            if width == W_BLOCK:
                term = _dot(jnp.concatenate(group, axis=1), w_refs[blk][...])
                dh = term if dh is None else dh + term
                group, width, blk = [], 0, blk + 1
        assert not group and blk == n_w
        xf = x_ref[...]
        r = lax.rsqrt(jnp.mean(xf * xf, axis=-1, keepdims=True) + EPS)
        xh = xf * r
        dxh = dh * g_ref[...]
        dx_ref[...] = r * (dxh - xh * jnp.mean(dxh * xh, axis=-1, keepdims=True)) + dy_ref[...]
        dgain_ref[...] += _rows8(jnp.sum(dh * xh, axis=0, keepdims=True))

        if n_s:
            @pl.when(pl.program_id(0) == n_steps - 1)
            def _():
                for cp in _scatter_copies(q_refs, o_refs, send_sems, recv_sems):
                    cp.wait()

    row = pl.BlockSpec((tm, D_MODEL), lambda i: (i, 0))
    res = pl.pallas_call(
        body, name="d_x", grid=(n_steps,),
        in_specs=[pl.BlockSpec((p.shape[0], tm, p.shape[2]), lambda i: (0, i, 0)) for p in d_proj] + _w_blocks(0, n_w)
        + [row, pl.BlockSpec((1, D_MODEL), lambda i: (0, 0)), row] + [ANY] * n_s,
        out_specs=[row, pl.BlockSpec((8, D_MODEL), lambda i: (0, 0))] + [ANY] * n_s,
        out_shape=[jax.ShapeDtypeStruct((SEQ, D_MODEL), F32), jax.ShapeDtypeStruct((8, D_MODEL), F32)]
        + [jax.ShapeDtypeStruct((3,) + q.shape[1:], BF16) for q in chip_sums],
        scratch_shapes=[pltpu.SemaphoreType.DMA((3 * n_s,)), pltpu.SemaphoreType.DMA((3 * n_s,))] if n_s else [],
        compiler_params=_params("arbitrary"),
    )(*d_proj, *([w_t] * n_w), x, gain, dy, *chip_sums)
    return res[0], res[1], res[2:]


def _my_place():
    x, y, c = lax.axis_index("x"), lax.axis_index("y"), lax.axis_index("c")
    return jnp.stack([2 * x + y, c]).astype(jnp.int32)


def _half_rows(ref, half):
    rows = ref.shape[-2] // 2
    idx = (slice(None),) * (len(ref.shape) - 2) + (pl.ds(pl.multiple_of(half * rows, 16), rows), slice(None))
    return ref.at[idx]


def _swap_halves(grads):
    n = len(grads)

    def body(*refs):
        g_refs, o_refs, (send_sems, recv_sems) = refs[:n], refs[n:2 * n], refs[2 * n:]
        x, y, c = lax.axis_index("x"), lax.axis_index("y"), lax.axis_index("c")
        copies = [pltpu.make_async_remote_copy(src_ref=_half_rows(g, 1 - c), dst_ref=o, send_sem=send_sems.at[k],
                                               recv_sem=recv_sems.at[k], device_id=(x, y, 1 - c), device_id_type=MESH)
                  for k, (g, o) in enumerate(zip(g_refs, o_refs))]
        for cp in copies:
            cp.start()
        for cp in copies:
            cp.wait()

    return pl.pallas_call(
        body, name="reduce_swap_halves", in_specs=[ANY] * n, out_specs=[ANY] * n,
        out_shape=[jax.ShapeDtypeStruct((N_CHIPS, g.shape[1] // 2, D_MODEL), g.dtype) for g in grads],
        scratch_shapes=[pltpu.SemaphoreType.DMA((n,)), pltpu.SemaphoreType.DMA((n,))],
    )(*grads)


def _row_tile(rows):
    return max(t for t in range(16, 385, 16) if rows % t == 0)


def _add_halves(place, grads, theirs, name):
    half = theirs.shape[1]
    tr = _row_tile(half)
    n = half // tr

    def body(place_ref, g_ref, t_ref, o_ref):
        o_ref[...] = (g_ref[...] + t_ref[...].astype(F32)).astype(BF16)

    return pl.pallas_call(
        body, name=name,
        grid_spec=pltpu.PrefetchScalarGridSpec(
            num_scalar_prefetch=1, grid=(N_CHIPS, n),
            in_specs=[pl.BlockSpec((None, tr, D_MODEL), lambda s, i, p: (s, p[1] * n + i, 0)),
                      pl.BlockSpec((None, tr, D_MODEL), lambda s, i, p: (s, i, 0))],
            out_specs=pl.BlockSpec((None, tr, D_MODEL), lambda s, i, p: (s, i, 0))),
        out_shape=jax.ShapeDtypeStruct((N_CHIPS, half, D_MODEL), BF16),
        compiler_params=_params("arbitrary", "arbitrary"),
    )(place, grads, theirs)


def _scatter_copies(q_refs, o_refs, send_sems, recv_sems):
    x, y, c = lax.axis_index("x"), lax.axis_index("y"), lax.axis_index("c")
    chips = [(1 - x, y), (x, 1 - y), (1 - x, 1 - y)]
    return [pltpu.make_async_remote_copy(src_ref=q.at[2 * cx + cy], dst_ref=o.at[j],
                                         send_sem=send_sems.at[3 * k + j], recv_sem=recv_sems.at[3 * k + j],
                                         device_id=(cx, cy, c), device_id_type=MESH)
            for k, (q, o) in enumerate(zip(q_refs, o_refs)) for j, (cx, cy) in enumerate(chips)]


def _add_chips(place, chip_sums, others, name):
    half = others.shape[1]
    tr = _row_tile(half)
    n = half // tr

    def body(place_ref, q_ref, o_ref, r_ref):
        acc = q_ref[...].astype(F32)
        for j in range(3):
            acc = acc + o_ref[j].astype(F32)
        r_ref[...] = acc

    return pl.pallas_call(
        body, name=name,
        grid_spec=pltpu.PrefetchScalarGridSpec(
            num_scalar_prefetch=1, grid=(n,),
            in_specs=[pl.BlockSpec((None, tr, D_MODEL), lambda i, p: (p[0], i, 0)),
                      pl.BlockSpec((3, tr, D_MODEL), lambda i, p: (0, i, 0))],
            out_specs=pl.BlockSpec((tr, D_MODEL), lambda i, p: (p[1] * n + i, 0))),
        out_shape=jax.ShapeDtypeStruct((2 * half, D_MODEL), F32),
        compiler_params=_params("arbitrary"),
    )(place, chip_sums, others)


def _join_halves(shards):
    n = len(shards)

    def body(*refs):
        o_refs, (send_sems, recv_sems) = refs[n:2 * n], refs[2 * n:]
        x, y, c = lax.axis_index("x"), lax.axis_index("y"), lax.axis_index("c")

        def copy(k, rows):
            return pltpu.make_async_remote_copy(src_ref=rows, dst_ref=rows, send_sem=send_sems.at[k],
                                                recv_sem=recv_sems.at[k], device_id=(x, y, 1 - c), device_id_type=MESH)

        sends = [copy(k, _half_rows(o, c)) for k, o in enumerate(o_refs)]
        for cp in sends:
            cp.start()
        for k, o in enumerate(o_refs):
            copy(k, _half_rows(o, 1 - c)).wait_recv()
        for cp in sends:
            cp.wait_send()

    return pl.pallas_call(
        body, name="reduce_join_halves", in_specs=[ANY] * n, out_specs=[ANY] * n,
        out_shape=[jax.ShapeDtypeStruct(s.shape, F32) for s in shards],
        input_output_aliases={k: k for k in range(n)},
        scratch_shapes=[pltpu.SemaphoreType.DMA((n,)), pltpu.SemaphoreType.DMA((n,))],
    )(*shards)


def _gather_small(block):
    rows = block.shape[0]

    def body(b_ref, o_ref, send_sems, recv_sems, local_sem):
        x, y, c = lax.axis_index("x"), lax.axis_index("y"), lax.axis_index("c")
        me, sibling = (x, y, c), (x, y, 1 - c)
        chips = [(1 - x, y), (x, 1 - y), (1 - x, 1 - y)]

        def at(px, py, pc):
            return o_ref.at[pl.ds(pl.multiple_of((4 * px + 2 * py + pc) * rows, 8), rows), :]

        def copy(k, block_of, to, src=None):
            return pltpu.make_async_remote_copy(src_ref=at(*block_of) if src is None else src, dst_ref=at(*block_of),
                                                send_sem=send_sems.at[k], recv_sem=recv_sems.at[k],
                                                device_id=to, device_id_type=MESH)

        mine = pltpu.make_async_copy(b_ref, at(*me), local_sem)
        mine.start()
        first = [copy(0, me, sibling, src=b_ref)]
        first += [copy(1 + j, me, (*chip, c), src=b_ref) for j, chip in enumerate(chips)]
        for cp in first:
            cp.start()
        passed = [copy(4 + j, (*chip, c), sibling) for j, chip in enumerate(chips)]
        for j, chip in enumerate(chips):
            copy(1 + j, (*chip, c), me).wait_recv()
            passed[j].start()
        copy(0, sibling, me).wait_recv()
        for j, chip in enumerate(chips):
            copy(4 + j, (*chip, 1 - c), me).wait_recv()
        for cp in first + passed:
            cp.wait_send()
        mine.wait()

    return pl.pallas_call(
        body, name="gather_small_grads",
        in_specs=[pl.BlockSpec(memory_space=pltpu.VMEM)], out_specs=pl.BlockSpec(memory_space=pltpu.VMEM),
        out_shape=jax.ShapeDtypeStruct((8 * rows, D_MODEL), F32),
        scratch_shapes=[pltpu.SemaphoreType.DMA((7,)), pltpu.SemaphoreType.DMA((7,)), pltpu.SemaphoreType.DMA],
    )(block)


def _sum_devices(blocks):
    def body(b_ref, o_ref):
        acc = b_ref[0:8, :]
        for dev in range(1, 8):
            acc = acc + b_ref[8 * dev:8 * dev + 8, :]
        o_ref[...] = acc

    return pl.pallas_call(body, name="sum_small_grads", out_shape=jax.ShapeDtypeStruct((8, D_MODEL), F32))(blocks)


def _adamw_math(w, g, m, v):
    m = ADAM_B1 * m + (1.0 - ADAM_B1) * g
    v = ADAM_B2 * v + (1.0 - ADAM_B2) * (g * g)
    m_hat = m / (1.0 - ADAM_B1 ** ADAM_STEP)
    v_hat = v / (1.0 - ADAM_B2 ** ADAM_STEP)
    return -ADAM_LR * (m_hat / (jnp.sqrt(v_hat) + ADAM_EPS) + ADAM_WD * w), m, v


def _adamw(w, g, m, v, name):
    r, c = w.shape
    tr = 128 if r % 128 == 0 else r

    def body(w_ref, g_ref, m_ref, v_ref, d_ref, nm_ref, nv_ref):
        d_ref[...], nm_ref[...], nv_ref[...] = _adamw_math(w_ref[...], g_ref[...], m_ref[...], v_ref[...])

    spec = pl.BlockSpec((tr, c), lambda i: (i, 0))
    return pl.pallas_call(
        body, name=name, grid=(r // tr,), in_specs=[spec] * 4, out_specs=[spec] * 3,
        out_shape=[jax.ShapeDtypeStruct((r, c), F32)] * 3, compiler_params=_params("arbitrary"),
    )(w, g, m, v)


def _adamw_small(ws, gs, ms, vs):
    n = len(ws)

    def body(*refs):
        ins, outs = refs[:4 * n], refs[4 * n:]
        for k in range(n):
            d, m, v = _adamw_math(ins[k][...], ins[n + k][...], ins[2 * n + k][...], ins[3 * n + k][...])
            outs[k][...], outs[n + k][...], outs[2 * n + k][...] = d, m, v

    shapes = [jax.ShapeDtypeStruct(w.shape, F32) for w in ws]
    res = pl.pallas_call(body, name="adamw_small", out_shape=shapes * 3)(*ws, *gs, *ms, *vs)
    return res[:n], res[n:2 * n], res[2 * n:]


def _fold_heads(partials):
    t = jnp.sum(partials[:, 0, :], axis=0)
    return (t[:HEAD_DIM] + t[HEAD_DIM:]).reshape(1, HEAD_DIM)


def _local_step(x, target, norm_gain, w_t, w_a, w_b, w_o, b_m, q_norm_a, k_norm_a, q_norm_b, k_norm_b, sink_a,
                rel_bias, start_reduce=None):
    two = lambda gain: jnp.concatenate([gain, gain], axis=1)
    bias_a = _bias_lines(rel_bias[:, :8], A_HALF_WINDOW, 1)
    bias_b = jnp.concatenate([_bias_lines(rel_bias[:, 8 + 8 * g:16 + 8 * g], B_HALF_WINDOW, d)
                              for g, d in enumerate(B_DILATIONS)], axis=0)

    qkv, h = _in_proj(x, norm_gain, w_t, 0, QKV_WIDTH // W_BLOCK, BF16, "in_proj_qkv", True)
    gates, = _in_proj(x, norm_gain, w_t, QKV_WIDTH // W_BLOCK, GATE_WIDTH // W_BLOCK, F32, "in_proj_gates", False)
    out_a, lse_a = _attn_a_fwd(qkv, two(q_norm_a), two(k_norm_a), bias_a, sink_a)
    out_b, lse_b = _attn_b_fwd(qkv, two(q_norm_b), two(k_norm_b), bias_b)

    dy, dgates, d_out_a, d_out_b, d_wa, d_wb, d_wo, d_bm, sq = _middle(
        out_a, out_b, gates, x, target, w_a, w_b, w_o, b_m)
    loss = (0.5 / D_MODEL) * jnp.sum(sq)

    dq_a, dkv_a, dgq_a, dgk_a, ds_a, dsink = _attn_a_bwd(
        qkv, two(q_norm_a), two(k_norm_a), bias_a, sink_a, out_a, lse_a, d_out_a)
    dq_b, dk_b, dv_b, dgq_b, dgk_b, ds_b = _attn_b_bwd(
        qkv, two(q_norm_b), two(k_norm_b), bias_b, out_b, lse_b, d_out_b)
    d_proj = (dq_a, dkv_a, dq_b, dk_b, dv_b, dgates)

    d_bm_rows = jnp.pad(d_bm.reshape(2, N_CHIPS, 256).transpose(1, 0, 2),
                        ((0, 0), (0, REST_ROWS - 514), (0, D_MODEL - 256)))
    rest = jnp.concatenate([d_wo.reshape(N_CHIPS, 256, D_MODEL), d_wa.reshape(N_CHIPS, 128, D_MODEL),
                            d_wb.reshape(N_CHIPS, 128, D_MODEL), d_bm_rows], axis=1)
    d_wt, d_wt_narrow = _d_w_in(d_proj, h)
    grads = [d_wt.reshape(N_CHIPS, W_IN_SHARD, D_MODEL), rest]
    narrow = [d_wt_narrow.reshape(N_CHIPS, W_IN_SHARD, D_MODEL), rest.astype(BF16)]
    chip_sums = start_reduce(grads, narrow) if start_reduce is not None else []
    grad_x, d_gain, others = _d_x(d_proj, w_t, x, norm_gain, dy, chip_sums)

    d_rel = jnp.concatenate(
        [_bias_grad(ds_a, A_HALF_WINDOW, 1)]
        + [_bias_grad(ds_b[4 * g:4 * g + 4], B_HALF_WINDOW, d) for g, d in enumerate(B_DILATIONS)], axis=1)
    d_sink = jnp.sum(dsink, axis=(2, 3)).reshape(1, 8)
    dgk_a_row = dgk_a[0]
    small = jnp.zeros((8, D_MODEL), F32)
    small = small.at[0].set(d_gain[0])
    small = small.at[1].set(d_rel.reshape(-1))
    misc = jnp.concatenate([_fold_heads(dgq_a), (dgk_a_row[:HEAD_DIM] + dgk_a_row[HEAD_DIM:]).reshape(1, HEAD_DIM),
                            _fold_heads(dgq_b), _fold_heads(dgk_b), d_sink], axis=1)
    small = small.at[2, :264].set(misc[0])

    return loss, grad_x, grads, small, chip_sums, others


def _unpack_weights(w_t_all, small_all):
    sm = small_all.reshape(N_CHIPS, SMALL_ROWS, D_MODEL)
    w_o = sm[:, 0:256].reshape(D_MODEL, D_MODEL)
    w_a = sm[:, 256:384].reshape(N_CHIPS, 512, 256).transpose(1, 0, 2).reshape(512, D_MODEL)
    w_b = sm[:, 384:512].reshape(N_CHIPS, 512, 256).transpose(1, 0, 2).reshape(512, D_MODEL)
    b_m = lax.bitcast_convert_type(sm[:, 512].reshape(N_CHIPS, 2, 256, 2), F32)
    return w_t_all, w_a, w_b, w_o, b_m.transpose(1, 0, 2).reshape(2, D_MODEL)


def _pack_small_weights(w_branch_a, w_branch_b, b_merge, w_out):
    b_m = jnp.pad(lax.bitcast_convert_type(b_merge, BF16).reshape(1, D_MODEL), ((0, SMALL_ROWS - 513), (0, 0)))
    return jnp.concatenate([w_out.astype(BF16), w_branch_a.astype(BF16).reshape(128, D_MODEL),
                            w_branch_b.astype(BF16).reshape(128, D_MODEL), b_m], axis=0)


def kernel(x, norm_gain, w_in, q_norm_a, k_norm_a, q_norm_b, k_norm_b, sink_a, rel_bias, w_branch_a, w_branch_b, b_merge, w_out, loss_target, m_norm_gain, m_w_in, m_q_norm_a, m_k_norm_a, m_q_norm_b, m_k_norm_b, m_sink_a, m_rel_bias, m_w_branch_a, m_w_branch_b, m_b_merge, m_w_out, v_norm_gain, v_w_in, v_q_norm_a, v_k_norm_a, v_q_norm_b, v_k_norm_b, v_sink_a, v_rel_bias, v_w_branch_a, v_w_branch_b, v_b_merge, v_w_out):
    wt_shard = _transpose_cast(w_in, BF16, "w_in_transpose")
    w_t, w_a, w_b, w_o, b_m = _unpack_weights(
        *_gather_weights(wt_shard, _pack_small_weights(w_branch_a[0], w_branch_b[0], b_merge[0], w_out[0])))

    place = _my_place()
    names = ("w_in", "rest")

    def start_reduce(grads, narrow):
        return [_add_halves(place, g, t, "reduce_add_halves_" + n) for g, t, n in zip(grads, _swap_halves(narrow), names)]

    loss_part, grad_x, _, small, chip_sums, others = _local_step(
        x[0], loss_target[0], norm_gain, w_t, w_a, w_b, w_o, b_m, q_norm_a, k_norm_a, q_norm_b, k_norm_b,
        sink_a, rel_bias, start_reduce)

    g_wt, g_rest = _join_halves([_add_chips(place, q, o, "reduce_add_chips_" + n)
                                 for q, o, n in zip(chip_sums, others, names)])
    small = _sum_devices(_gather_small(small.at[3, 0].set(loss_part)))
    loss = small[3, 0]

    g_w_in = _transpose_cast(g_wt, F32, "grad_w_in_transpose")
    g_w_out = g_rest[0:256]
    g_w_a = g_rest[256:384].reshape(512, 256)
    g_w_b = g_rest[384:512].reshape(512, 256)
    g_b_merge = g_rest[512:514, :256]
    g_norm_gain = small[0:1]
    g_rel_bias = small[1].reshape(N_BUCKETS, N_BUCKETS)
    g_q_a, g_k_a, g_q_b, g_k_b = (small[2:3, 64 * k:64 * k + 64] for k in range(4))
    g_sink = small[2:3, 256:264]

    big_names = (("w_in", w_in, g_w_in, m_w_in, v_w_in),
                 ("w_branch_a", w_branch_a, g_w_a, m_w_branch_a, v_w_branch_a),
                 ("w_branch_b", w_branch_b, g_w_b, m_w_branch_b, v_w_branch_b),
                 ("w_out", w_out, g_w_out, m_w_out, v_w_out))
    upd = {name: (g,) + tuple(_adamw(w[0], g, m[0], v[0], "adamw_" + name)) for name, w, g, m, v in big_names}
    small_names = ("norm_gain", "q_norm_a", "k_norm_a", "q_norm_b", "k_norm_b", "sink_a", "rel_bias", "b_merge")
    ws = [norm_gain, q_norm_a, k_norm_a, q_norm_b, k_norm_b, sink_a, rel_bias, b_merge[0]]
    gs = [g_norm_gain, g_q_a, g_k_a, g_q_b, g_k_b, g_sink, g_rel_bias, g_b_merge]
    ms = [m_norm_gain, m_q_norm_a, m_k_norm_a, m_q_norm_b, m_k_norm_b, m_sink_a, m_rel_bias, m_b_merge[0]]
    vs = [v_norm_gain, v_q_norm_a, v_k_norm_a, v_q_norm_b, v_k_norm_b, v_sink_a, v_rel_bias, v_b_merge[0]]
    ds, nms, nvs = _adamw_small(ws, gs, ms, vs)
    for k, name in enumerate(small_names):
        upd[name] = (gs[k], ds[k], nms[k], nvs[k])

    order = ("norm_gain", "w_in", "q_norm_a", "k_norm_a", "q_norm_b", "k_norm_b", "sink_a", "rel_bias",
             "w_branch_a", "w_branch_b", "b_merge", "w_out")
    lead = {"w_in", "w_branch_a", "w_branch_b", "b_merge", "w_out"}
    outs = [loss, grad_x[None]]
    for part in range(4):
        outs += [upd[name][part][None] if name in lead else upd[name][part] for name in order]
    return tuple(outs)
```

```python
import math

import numpy as np
import jax
import jax.numpy as jnp
from jax import lax
from jax.experimental import pallas as pl
from jax.experimental.pallas import tpu as pltpu

F32 = jnp.float32
BF16 = jnp.bfloat16

SEQ = 4096
D_MODEL = 1024
HEAD_DIM = 64
LANES = 128
EPS = 1e-6
NEG_INF = -1e30
SCALE = HEAD_DIM ** -0.5
N_BUCKETS = 32
MAX_DISTANCE = 1024
N_CHIPS = 4

A_HALF_WINDOW = 128
B_HALF_WINDOW = 64
B_DILATIONS = (1, 4, 16)
Q_BLOCK = 128

QKV_WIDTH = 5376
GATE_WIDTH = 3072
QA_BLK, KA_BLK, VA_BLK = 0, 4, 5
QB_BLK, KB_BLK, VB_BLK = 6, 18, 30
IN_WIDTH = QKV_WIDTH + GATE_WIDTH
W_IN_SHARD = IN_WIDTH // N_CHIPS

SMALL_ROWS = 544
REST_ROWS = 544

ADAM_LR = 0.001
ADAM_B1 = 0.9
ADAM_B2 = 0.999
ADAM_EPS = 1e-08
ADAM_WD = 0.01
ADAM_STEP = 10

VMEM_LIMIT = 56 * 1024 * 1024

NT = (((1,), (1,)), ((), ()))
TN = (((0,), (0,)), ((), ()))
MESH = pl.DeviceIdType.MESH
ANY = pl.BlockSpec(memory_space=pl.ANY)


def _dot(a, b, dims=None):
    if dims is None:
        return jnp.dot(a, b, preferred_element_type=F32)
    return lax.dot_general(a, b, dims, preferred_element_type=F32)


def _params(*semantics):
    return pltpu.CompilerParams(dimension_semantics=semantics or None, vmem_limit_bytes=VMEM_LIMIT)


def _line_width(half_window):
    return pl.cdiv(2 * Q_BLOCK + 2 * half_window - 1, LANES) * LANES


def _bucket_onehot(half_window, stride):
    rel = np.arange(_line_width(half_window)) - (Q_BLOCK - 1) - half_window
    band = np.abs(rel) <= half_window
    rel = rel * stride
    half, max_exact = N_BUCKETS // 2, N_BUCKETS // 4
    n = np.abs(rel)
    nf = np.maximum(n, max_exact).astype(np.float32)
    large = max_exact + (np.log(nf / np.float32(max_exact)) / np.float32(math.log(MAX_DISTANCE / max_exact))
                         * np.float32(half - max_exact)).astype(np.int32)
    large = np.minimum(large, half - 1)
    bucket = (rel > 0).astype(np.int32) * half + np.where(n < max_exact, n, large)
    onehot = (bucket[..., None] == np.arange(N_BUCKETS)) & band[..., None]
    return onehot.astype(np.float32), band


def _bias_lines(rel_bias_cols, half_window, stride):
    onehot, band = _bucket_onehot(half_window, stride)
    h = rel_bias_cols.shape[1]
    t = jnp.einsum("tb,bh->ht", jnp.asarray(onehot), rel_bias_cols, precision=lax.Precision.HIGHEST)
    t = t + jnp.asarray(np.where(band, 0.0, NEG_INF).astype(np.float32))
    return t.reshape(h // 2, 2, -1)


def _bias_grad(d_lines, half_window, stride):
    onehot, _ = _bucket_onehot(half_window, stride)
    h = d_lines.shape[0] * 2
    return jnp.einsum("tb,ht->bh", jnp.asarray(onehot), d_lines.reshape(h, -1), precision=lax.Precision.HIGHEST)


def _unroll_bias(line_ref, tile_ref, w):
    width = line_ref.shape[1]
    for j in range(2):
        rows = jnp.broadcast_to(line_ref[j:j + 1, :], (Q_BLOCK, width))
        rows = pltpu.roll(rows, width - (Q_BLOCK - 1), 1, stride=1, stride_axis=0)
        tile_ref[j * Q_BLOCK:(j + 1) * Q_BLOCK, :] = rows[:, :w]


def _fold_bias_grad(tile_ref, line_ref, w):
    width = line_ref.shape[1]
    row = lax.broadcasted_iota(jnp.int32, (Q_BLOCK, Q_BLOCK), 0)
    col = lax.broadcasted_iota(jnp.int32, (Q_BLOCK, Q_BLOCK), 1)
    flip = jnp.where(row + col == Q_BLOCK - 1, 1.0, 0.0).astype(BF16)
    for j in range(2):
        tile = tile_ref[j * Q_BLOCK:(j + 1) * Q_BLOCK, :]
        hi = tile.astype(BF16)
        lo = (tile - hi.astype(F32)).astype(BF16)
        rows = _dot(flip, hi) + _dot(flip, lo)
        rows = jnp.concatenate([rows, jnp.zeros((Q_BLOCK, width - w), F32)], axis=1)
        rows = pltpu.roll(rows, 0, 1, stride=1, stride_axis=0)
        line_ref[j:j + 1, :] = jnp.sum(rows, axis=0, keepdims=True)


def _transpose_cast(w, out_dtype, name):
    lead = (None,) * (w.ndim - 2)
    zero = (0,) * (w.ndim - 2)
    r, c = w.shape[-2:]

    def body(w_ref, o_ref):
        o_ref[...] = w_ref[...].T.astype(out_dtype)

    if r % LANES == 0:
        steps = pl.cdiv(c, LANES)
        in_spec = pl.BlockSpec(lead + (r, LANES), lambda j: zero + (0, j))
        out_spec = pl.BlockSpec((LANES, r), lambda j: (j, 0))
    else:
        steps = pl.cdiv(r, LANES)
        in_spec = pl.BlockSpec(lead + (LANES, c), lambda j: zero + (j, 0))
        out_spec = pl.BlockSpec((c, LANES), lambda j: (0, j))
    return pl.pallas_call(
        body, name=name, grid=(steps,), in_specs=[in_spec], out_specs=out_spec,
        out_shape=jax.ShapeDtypeStruct((c, r), out_dtype),
        compiler_params=_params("arbitrary"),
    )(w)


def _gather_weights(wt_shard, small_shard):
    bufs = ((W_IN_SHARD, IN_WIDTH), (SMALL_ROWS, N_CHIPS * SMALL_ROWS))

    stage_rows = 528

    def body(wt_in, sm_in, wt_out, sm_out, send_sems, recv_sems, in_sems, out_sems, stage):
        x, y, c = lax.axis_index("x"), lax.axis_index("y"), lax.axis_index("c")
        sibling = (x, y, 1 - c)
        my_chip = 2 * x + y
        refs = ((wt_in, wt_out), (sm_in, sm_out))

        def keep_own():
            pieces = [(b, r0) for b in range(2) for r0 in range(0, bufs[b][0], stage_rows)]
            outs = []
            for i, (b, r0) in enumerate(pieces):
                rows = min(stage_rows, bufs[b][0] - r0)
                slot = i % 2
                if i >= 2:
                    outs[i - 2].wait()
                buf = stage.at[slot, pl.ds(0, rows), :]
                load = pltpu.make_async_copy(refs[b][0].at[pl.ds(r0, rows), :], buf, in_sems.at[slot])
                load.start()
                load.wait()
                start = pl.multiple_of(my_chip * bufs[b][0] + r0, 16)
                outs.append(pltpu.make_async_copy(buf, refs[b][1].at[pl.ds(start, rows), :], out_sems.at[slot]))
                outs[i].start()
            for cp in outs[-2:]:
                cp.wait()

        def half_of(b, chip, half):
            rows = bufs[b][0]
            start = pl.multiple_of(chip * rows + half * (rows // 2), 16)
            return refs[b][1].at[pl.ds(start, rows // 2), :]

        def copy(k, src, dst, to):
            return pltpu.make_async_remote_copy(src_ref=src, dst_ref=dst, send_sem=send_sems.at[k],
                                                recv_sem=recv_sems.at[k], device_id=to, device_id_type=MESH)

        near = (x + (1 - c) - 2 * x * (1 - c), y + c - 2 * y * c)
        far = (x + c - 2 * x * c, y + (1 - c) - 2 * y * (1 - c))
        diag = (1 - x, 1 - y)
        chip_no = lambda chip: 2 * chip[0] + chip[1]
        sends, passed = [], []
        for b in range(2):
            rows = bufs[b][0]
            src = refs[b][0].at[pl.ds(pl.multiple_of(c * (rows // 2), 16), rows // 2), :]
            sends += [copy(3 * b, src, half_of(b, my_chip, c), (*near, c)),
                      copy(3 * b + 1, src, half_of(b, my_chip, c), (*far, c))]
        for cp in sends:
            cp.start()
        keep_own()

        def pass_on(b, j, chip):
            landed = half_of(b, chip_no(chip), c)
            fwd = copy(6 + 3 * b + j, landed, landed, sibling)
            fwd.start()
            passed.append(fwd)

        for b in range(2):
            landed = half_of(b, chip_no(near), c)
            copy(3 * b, landed, landed, sibling).wait_recv()
            relay = copy(3 * b + 2, landed, landed, (*far, c))
            relay.start()
            sends.append(relay)
            pass_on(b, 0, near)
        for b in range(2):
            for j, chip in ((1, far), (2, diag)):
                landed = half_of(b, chip_no(chip), c)
                copy(3 * b + j, landed, landed, sibling).wait_recv()
                pass_on(b, j, chip)
        for b in range(2):
            for j, chip in ((0, far), (1, near), (2, diag)):
                other = half_of(b, chip_no(chip), 1 - c)
                copy(6 + 3 * b + j, other, other, sibling).wait_recv()
        for cp in sends + passed:
            cp.wait_send()

    return pl.pallas_call(
        body, name="gather_weights",
        in_specs=[ANY, ANY], out_specs=[ANY, ANY],
        out_shape=[jax.ShapeDtypeStruct((bufs[0][1], D_MODEL), BF16),
                   jax.ShapeDtypeStruct((bufs[1][1], D_MODEL), BF16)],
        scratch_shapes=[pltpu.SemaphoreType.DMA((12,)), pltpu.SemaphoreType.DMA((12,)),
                        pltpu.SemaphoreType.DMA((2,)), pltpu.SemaphoreType.DMA((2,)),
                        pltpu.VMEM((2, stage_rows, D_MODEL), BF16)],
    )(wt_shard, small_shard)


W_BLOCK = 768


def _w_blocks(first, count):
    return [pl.BlockSpec((W_BLOCK, D_MODEL), lambda *_, k=k: (first + k, 0)) for k in range(count)]


def _in_proj(x, gain, w_t, first_block, n_blocks, out_dtype, name, keep_h):
    tm = 512

    def body(x_ref, g_ref, *refs):
        w_refs, outs = refs[:n_blocks], refs[n_blocks:]
        xf = x_ref[...]
        r = lax.rsqrt(jnp.mean(xf * xf, axis=-1, keepdims=True) + EPS)
        h = ((xf * r) * g_ref[...]).astype(BF16)
        if keep_h:
            outs[1][...] = h
        for k, w_ref in enumerate(w_refs):
            outs[0][:, k * W_BLOCK:(k + 1) * W_BLOCK] = _dot(h, w_ref[...], NT).astype(out_dtype)

    return pl.pallas_call(
        body, name=name, grid=(SEQ // tm,),
        in_specs=[pl.BlockSpec((tm, D_MODEL), lambda i: (i, 0)), pl.BlockSpec((1, D_MODEL), lambda i: (0, 0))]
        + _w_blocks(first_block, n_blocks),
        out_specs=[pl.BlockSpec((tm, W_BLOCK * n_blocks), lambda i: (i, 0)),
                   pl.BlockSpec((tm, D_MODEL), lambda i: (i, 0))][:2 if keep_h else 1],
        out_shape=[jax.ShapeDtypeStruct((SEQ, W_BLOCK * n_blocks), out_dtype),
                   jax.ShapeDtypeStruct((SEQ, D_MODEL), BF16)][:2 if keep_h else 1],
        compiler_params=_params("arbitrary"),
    )(x, gain, *([w_t] * n_blocks))


CHUNK = 256
CHUNK_UNROLL = 4
TILE_UNROLL = 8


def _low_half():
    return lax.broadcasted_iota(jnp.int32, (1, LANES), 1) < HEAD_DIM


def _half_sum(v, low):
    del low
    row = lax.broadcasted_iota(jnp.int32, (2 * LANES, LANES), 0)
    col = lax.broadcasted_iota(jnp.int32, (2 * LANES, LANES), 1)
    ones = jnp.where((row % LANES) // HEAD_DIM == col // HEAD_DIM, 1.0, 0.0).astype(BF16)
    hi = v.astype(BF16)
    lo = (v - hi.astype(F32)).astype(BF16)
    return _dot(jnp.concatenate([hi, lo], axis=1), ones)


def _chunks(fn, init=0):
    def body(i, carry):
        for u in range(CHUNK_UNROLL):
            carry = fn(pl.multiple_of((i * CHUNK_UNROLL + u) * CHUNK, CHUNK), carry)
        return carry

    return lax.fori_loop(0, SEQ // (CHUNK * CHUNK_UNROLL), body, init)


def _inv_rms(t, low):
    return lax.rsqrt(_half_sum(t * t, low) * (1.0 / HEAD_DIM) + EPS)


def _prep_q(q_ref, gain_ref, qn_ref):
    low = _low_half()

    def step(r0, carry):
        q = q_ref[pl.ds(r0, CHUNK), :].astype(F32)
        qn_ref[pl.ds(r0, CHUNK), :] = ((q * _inv_rms(q, low)) * gain_ref[...]) * SCALE
        return carry

    _chunks(step)


def _own_half(t, keep):
    return jnp.where(keep, t, pltpu.roll(t, HEAD_DIM, 1))


def _prep_kv(k_ref, v_ref, gain_ref, kp_ref, vp_ref, pad, keep=None):
    low = _low_half()
    zeros = jnp.zeros((pad, LANES), F32)
    for ref in (kp_ref, vp_ref):
        ref[pl.ds(0, pad), :] = zeros
        ref[pl.ds(pad + SEQ, pad), :] = zeros

    def step(r0, carry):
        k = k_ref[pl.ds(r0, CHUNK), :].astype(F32)
        v = v_ref[pl.ds(r0, CHUNK), :].astype(F32)
        kn = (k * _inv_rms(k, low)) * gain_ref[...]
        if keep is not None:
            kn, v = _own_half(kn, keep), _own_half(v, keep)
        kp_ref[pl.ds(pad + r0, CHUNK), :] = kn
        vp_ref[pl.ds(pad + r0, CHUNK), :] = v
        return carry

    _chunks(step)


def _tiles(d, half_window, fn):
    w = Q_BLOCK + 2 * half_window
    length = SEQ // d
    n_blocks = length // Q_BLOCK
    col = lax.broadcasted_iota(jnp.int32, (1, w), 1)

    def step(it, carry):
        c, n = it // n_blocks, it % n_blocks
        start = c + (d * Q_BLOCK) * n
        if d == 1:
            start = pl.multiple_of(start, Q_BLOCK)
            q_rows, k_rows = pl.ds(start, Q_BLOCK), pl.ds(start, w)
        else:
            q_rows, k_rows = pl.ds(start, Q_BLOCK, stride=d), pl.ds(start, w, stride=d)
        t = n * Q_BLOCK - half_window + col
        edge = jnp.where((t < 0) | (t >= length), NEG_INF, 0.0)
        fn(q_rows, k_rows, edge)
        return carry

    lax.fori_loop(0, d * n_blocks, step, 0, unroll=TILE_UNROLL)


def _stack_heads(t, low):
    return jnp.concatenate([jnp.where(low, t, 0.0), jnp.where(low, 0.0, t)], axis=0).astype(BF16)


def _unstack_heads(t, low):
    return jnp.where(low, t[:Q_BLOCK], t[Q_BLOCK:])


def _per_head(pair):
    return jnp.concatenate([jnp.full((Q_BLOCK, 1), pair[0], F32), jnp.full((Q_BLOCK, 1), pair[1], F32)], axis=0)


def _fwd_tiles(qn_ref, kp_ref, vp_ref, bias_ref, emit, *, d, half_window, sinks=None):
    low = _low_half()
    w = Q_BLOCK + 2 * half_window
    sink = None if sinks is None else _per_head(sinks)

    def tile(q_rows, k_rows, edge):
        q2 = _stack_heads(qn_ref[q_rows, :], low)
        k = kp_ref[k_rows, :].astype(BF16)
        v1 = jnp.concatenate([vp_ref[k_rows, :], jnp.ones((w, LANES), F32)], axis=1).astype(BF16)
        s = _dot(q2, k, NT) + bias_ref[...] + edge
        m = jnp.max(s, axis=-1, keepdims=True)
        if sink is not None:
            m = jnp.maximum(m, sink)
        o = _dot(jnp.exp(s - m).astype(BF16), v1)
        l = o[:, LANES:]
        if sink is not None:
            l = l + jnp.exp(sink - m)
        emit(q_rows, _unstack_heads(o[:, :LANES] * (1.0 / l), low), _unstack_heads(m + jnp.log(l), low))

    _tiles(d, half_window, tile)


def _bwd_tiles(qn_ref, kp_ref, vp_ref, bias_ref, do_ref, lse_ref, delta_ref, dq_ref, dk_ref, dv_ref, ds_ref,
               *, d, half_window, sinks=None, dsink_ref=None):
    low = _low_half()
    w = Q_BLOCK + 2 * half_window
    sink = None if sinks is None else _per_head(sinks)

    def rows_of(t):
        return jnp.concatenate([t[:, 0:1], t[:, HEAD_DIM:HEAD_DIM + 1]], axis=0)

    def tile(q_rows, k_rows, edge):
        q2 = _stack_heads(qn_ref[q_rows, :], low)
        do2 = _stack_heads(do_ref[q_rows, :], low)
        k = kp_ref[k_rows, :].astype(BF16)
        v = vp_ref[k_rows, :].astype(BF16)
        lse = rows_of(lse_ref[q_rows, :])
        delta = rows_of(delta_ref[q_rows, :])
        p = jnp.exp(_dot(q2, k, NT) + bias_ref[...] + edge - lse)
        ds = p * (_dot(do2, v, NT) - delta)
        ds_ref[...] += ds
        if sink is not None:
            dsink_ref[...] += (-jnp.exp(sink - lse) * delta).reshape(2, Q_BLOCK, 1)
        dsb, pb = ds.astype(BF16), p.astype(BF16)
        dq_ref[q_rows, :] = _unstack_heads(_dot(dsb, k), low)
        dk_ref[k_rows, :] += _dot(dsb, q2, TN)
        dv_ref[k_rows, :] += _dot(pb, do2, TN)

    _tiles(d, half_window, tile)


def _prep_delta(do_ref, o_ref, delta_ref):
    low = _low_half()

    def step(r0, carry):
        delta_ref[pl.ds(r0, CHUNK), :] = _half_sum(do_ref[pl.ds(r0, CHUNK), :] * o_ref[pl.ds(r0, CHUNK), :], low)
        return carry

    _chunks(step)


def _norm_bwd(raw_ref, gain_ref, dn_ref, dn_offset, out_ref, scale):
    low = _low_half()

    def step(r0, dgain):
        t = raw_ref[pl.ds(r0, CHUNK), :].astype(F32)
        dn = dn_ref[pl.ds(dn_offset + r0, CHUNK), :]
        dth = dn * (gain_ref[...] * scale)
        sums = _half_sum(jnp.concatenate([t * t, dth * t], axis=0), low)
        r = lax.rsqrt(sums[:CHUNK] * (1.0 / HEAD_DIM) + EPS)
        th = t * r
        out_ref[pl.ds(r0, CHUNK), :] = (r * (dth - th * (r * sums[CHUNK:] * (1.0 / HEAD_DIM)))).astype(BF16)
        return dgain + jnp.sum(dn * th, axis=0, keepdims=True) * scale

    return _chunks(step, jnp.zeros((1, LANES), F32))


def _rows8(v):
    return jnp.broadcast_to(v, (8, v.shape[-1]))


A_W = Q_BLOCK + 2 * A_HALF_WINDOW
A_PAD = A_HALF_WINDOW


def _seq_block(col_fn):
    return pl.BlockSpec((SEQ, LANES), col_fn)


def _attn_a_fwd(qkv, gain_q, gain_k, bias, sink):
    def body(sink_ref, q_ref, k_ref, v_ref, gq_ref, gk_ref, line_ref, o_ref, lse_ref, qn_ref, kp_ref, vp_ref,
             bias_ref):
        hp = pl.program_id(0)
        keep = (lax.broadcasted_iota(jnp.int32, (1, LANES), 1) // HEAD_DIM) == hp // 2
        _prep_q(q_ref, gq_ref, qn_ref)
        _prep_kv(k_ref, v_ref, gk_ref, kp_ref, vp_ref, A_PAD, keep)
        _unroll_bias(line_ref, bias_ref, A_W)

        def emit(rows, out, lse):
            o_ref[rows, :] = out
            lse_ref[rows, :] = lse

        _fwd_tiles(qn_ref, kp_ref, vp_ref, bias_ref, emit, d=1, half_window=A_HALF_WINDOW,
                   sinks=(sink_ref[2 * hp], sink_ref[2 * hp + 1]))

    vec = pl.BlockSpec((1, LANES), lambda hp, s: (0, 0))
    return pl.pallas_call(
        body, name="attn_a_fwd",
        grid_spec=pltpu.PrefetchScalarGridSpec(
            num_scalar_prefetch=1, grid=(4,),
            in_specs=[_seq_block(lambda hp, s: (0, QA_BLK + hp)), _seq_block(lambda hp, s: (0, KA_BLK)),
                      _seq_block(lambda hp, s: (0, VA_BLK)), vec, vec,
                      pl.BlockSpec((None, 2, _line_width(A_HALF_WINDOW)), lambda hp, s: (hp, 0, 0))],
            out_specs=[_seq_block(lambda hp, s: (0, hp)), _seq_block(lambda hp, s: (0, hp))],
            scratch_shapes=[pltpu.VMEM((SEQ, LANES), F32), pltpu.VMEM((SEQ + 2 * A_PAD, LANES), F32),
                            pltpu.VMEM((SEQ + 2 * A_PAD, LANES), F32), pltpu.VMEM((2 * Q_BLOCK, A_W), F32)]),
        out_shape=[jax.ShapeDtypeStruct((SEQ, 512), F32)] * 2,
        compiler_params=_params("arbitrary"),
    )(sink.reshape(8), qkv, qkv, qkv, gain_q, gain_k, bias)


def _attn_a_bwd(qkv, gain_q, gain_k, bias, sink, out, lse, d_out):
    def body(sink_ref, q_ref, k_ref, v_ref, gq_ref, gk_ref, line_ref, o_ref, lse_ref, do_ref,
             dq_out, dkv_out, dgq_out, dgk_out, dline_out, dsink_out,
             qn_ref, kp_ref, vp_ref, delta_ref, dq_ref, dk_ref, dv_ref, dk_tot, dv_tot, bias_ref, ds_out):
        hp = pl.program_id(0)
        kv_head = hp // 2
        keep = (lax.broadcasted_iota(jnp.int32, (1, LANES), 1) // HEAD_DIM) == kv_head
        _prep_q(q_ref, gq_ref, qn_ref)
        _prep_kv(k_ref, v_ref, gk_ref, kp_ref, vp_ref, A_PAD, keep)
        _prep_delta(do_ref, o_ref, delta_ref)
        _unroll_bias(line_ref, bias_ref, A_W)
        dk_ref[...] = jnp.zeros_like(dk_ref)
        dv_ref[...] = jnp.zeros_like(dv_ref)
        ds_out[...] = jnp.zeros_like(ds_out)
        dsink_out[...] = jnp.zeros_like(dsink_out)

        @pl.when(hp == 0)
        def _():
            dk_tot[...] = jnp.zeros_like(dk_tot)
            dv_tot[...] = jnp.zeros_like(dv_tot)

        _bwd_tiles(qn_ref, kp_ref, vp_ref, bias_ref, do_ref, lse_ref, delta_ref, dq_ref, dk_ref, dv_ref, ds_out,
                   d=1, half_window=A_HALF_WINDOW, sinks=(sink_ref[2 * hp], sink_ref[2 * hp + 1]),
                   dsink_ref=dsink_out)
        _fold_bias_grad(ds_out, dline_out, A_W)
        dgq_out[...] = _rows8(_norm_bwd(q_ref, gq_ref, dq_ref, 0, dq_out, SCALE))

        def fold(r0, carry):
            rows = pl.ds(A_PAD + r0, CHUNK)
            for acc, tot in ((dk_ref, dk_tot), (dv_ref, dv_tot)):
                t = acc[rows, :]
                tot[pl.ds(r0, CHUNK), :] += jnp.where(keep, t + pltpu.roll(t, HEAD_DIM, 1), 0.0)
            return carry

        _chunks(fold)

        @pl.when(hp == 3)
        def _():
            dgk_out[...] = _rows8(_norm_bwd(k_ref, gk_ref, dk_tot, 0, dkv_out.at[0], 1.0))
            dkv_out[1] = dv_tot[...].astype(BF16)

    vec = pl.BlockSpec((1, LANES), lambda hp, s: (0, 0))
    seq_f32 = pltpu.VMEM((SEQ, LANES), F32)
    padded = pltpu.VMEM((SEQ + 2 * A_PAD, LANES), F32)
    return pl.pallas_call(
        body, name="attn_a_bwd",
        grid_spec=pltpu.PrefetchScalarGridSpec(
            num_scalar_prefetch=1, grid=(4,),
            in_specs=[_seq_block(lambda hp, s: (0, QA_BLK + hp)), _seq_block(lambda hp, s: (0, KA_BLK)),
                      _seq_block(lambda hp, s: (0, VA_BLK)), vec, vec,
                      pl.BlockSpec((None, 2, _line_width(A_HALF_WINDOW)), lambda hp, s: (hp, 0, 0)),
                      _seq_block(lambda hp, s: (0, hp)), _seq_block(lambda hp, s: (0, hp)),
                      _seq_block(lambda hp, s: (0, hp))],
            out_specs=[pl.BlockSpec((None, SEQ, LANES), lambda hp, s: (hp, 0, 0)),
                       pl.BlockSpec((2, SEQ, LANES), lambda hp, s: (0, 0, 0)),
                       pl.BlockSpec((None, 8, LANES), lambda hp, s: (hp, 0, 0)),
                       pl.BlockSpec((8, LANES), lambda hp, s: (0, 0)),
                       pl.BlockSpec((None, 2, _line_width(A_HALF_WINDOW)), lambda hp, s: (hp, 0, 0)),
                       pl.BlockSpec((None, 2, Q_BLOCK, 1), lambda hp, s: (hp, 0, 0, 0))],
            scratch_shapes=[seq_f32, padded, padded, seq_f32, seq_f32, padded, padded, seq_f32, seq_f32,
                            pltpu.VMEM((2 * Q_BLOCK, A_W), F32), pltpu.VMEM((2 * Q_BLOCK, A_W), F32)]),
        out_shape=[jax.ShapeDtypeStruct((4, SEQ, LANES), BF16), jax.ShapeDtypeStruct((2, SEQ, LANES), BF16),
                   jax.ShapeDtypeStruct((4, 8, LANES), F32), jax.ShapeDtypeStruct((8, LANES), F32),
                   jax.ShapeDtypeStruct((4, 2, _line_width(A_HALF_WINDOW)), F32),
                   jax.ShapeDtypeStruct((4, 2, Q_BLOCK, 1), F32)],
        compiler_params=_params("arbitrary"),
    )(sink.reshape(8), qkv, qkv, qkv, gain_q, gain_k, bias, out, lse, d_out)


B_W = Q_BLOCK + 2 * B_HALF_WINDOW
B_PAD_MAX = B_HALF_WINDOW * B_DILATIONS[-1]


def _attn_b_fwd(qkv, gain_q, gain_k, bias):
    def body(q_ref, k_ref, v_ref, gq_ref, gk_ref, line_ref, o_ref, lse_ref, qn_ref, kp_ref, vp_ref, bias_ref):
        g = pl.program_id(1)
        _prep_q(q_ref, gq_ref, qn_ref)
        _unroll_bias(line_ref, bias_ref, B_W)

        def first(rows, out, lse):
            o_ref[rows, :] = out
            lse_ref[rows, :] = lse

        def combine(rows, out, lse):
            old = lse_ref[rows, :]
            new = jnp.maximum(old, lse) + jnp.log(1.0 + jnp.exp(-jnp.abs(old - lse)))
            o_ref[rows, :] = o_ref[rows, :] * jnp.exp(old - new) + out * jnp.exp(lse - new)
            lse_ref[rows, :] = new

        for gi, d in enumerate(B_DILATIONS):
            @pl.when(g == gi)
            def _():
                _prep_kv(k_ref, v_ref, gk_ref, kp_ref, vp_ref, B_HALF_WINDOW * d)
                _fwd_tiles(qn_ref, kp_ref, vp_ref, bias_ref, first if gi == 0 else combine,
                           d=d, half_window=B_HALF_WINDOW)

    vec = pl.BlockSpec((1, LANES), lambda hp, g: (0, 0))
    padded = pltpu.VMEM((SEQ + 2 * B_PAD_MAX, LANES), F32)
    return pl.pallas_call(
        body, name="attn_b_fwd", grid=(4, 3),
        in_specs=[_seq_block(lambda hp, g: (0, QB_BLK + 4 * g + hp)), _seq_block(lambda hp, g: (0, KB_BLK + 4 * g + hp)),
                  _seq_block(lambda hp, g: (0, VB_BLK + 4 * g + hp)), vec, vec,
                  pl.BlockSpec((None, 2, _line_width(B_HALF_WINDOW)), lambda hp, g: (4 * g + hp, 0, 0))],
        out_specs=[_seq_block(lambda hp, g: (0, hp)), _seq_block(lambda hp, g: (0, hp))],
        out_shape=[jax.ShapeDtypeStruct((SEQ, 512), F32)] * 2,
        scratch_shapes=[pltpu.VMEM((SEQ, LANES), F32), padded, padded, pltpu.VMEM((2 * Q_BLOCK, B_W), F32)],
        compiler_params=_params("arbitrary", "arbitrary"),
    )(qkv, qkv, qkv, gain_q, gain_k, bias)


def _attn_b_bwd(qkv, gain_q, gain_k, bias, out, lse, d_out):
    def body(q_ref, k_ref, v_ref, gq_ref, gk_ref, line_ref, o_ref, lse_ref, do_ref,
             dq_out, dk_out, dv_out, dgq_out, dgk_out, dline_out,
             qn_ref, kp_ref, vp_ref, delta_ref, dq_ref, dk_ref, dv_ref, bias_ref, ds_out):
        g = pl.program_id(1)
        _prep_q(q_ref, gq_ref, qn_ref)
        _prep_delta(do_ref, o_ref, delta_ref)
        _unroll_bias(line_ref, bias_ref, B_W)
        dk_ref[...] = jnp.zeros_like(dk_ref)
        dv_ref[...] = jnp.zeros_like(dv_ref)
        ds_out[...] = jnp.zeros_like(ds_out)
        for gi, d in enumerate(B_DILATIONS):
            @pl.when(g == gi)
            def _():
                pad = B_HALF_WINDOW * d
                _prep_kv(k_ref, v_ref, gk_ref, kp_ref, vp_ref, pad)
                _bwd_tiles(qn_ref, kp_ref, vp_ref, bias_ref, do_ref, lse_ref, delta_ref, dq_ref, dk_ref, dv_ref,
                           ds_out, d=d, half_window=B_HALF_WINDOW)
                dgk_out[...] = _rows8(_norm_bwd(k_ref, gk_ref, dk_ref, pad, dk_out, 1.0))
                dv_out[...] = dv_ref[pl.ds(pad, SEQ), :].astype(BF16)
        _fold_bias_grad(ds_out, dline_out, B_W)
        dgq_out[...] = _rows8(_norm_bwd(q_ref, gq_ref, dq_ref, 0, dq_out, SCALE))

    vec = pl.BlockSpec((1, LANES), lambda hp, g: (0, 0))
    seq_f32 = pltpu.VMEM((SEQ, LANES), F32)
    padded = pltpu.VMEM((SEQ + 2 * B_PAD_MAX, LANES), F32)
    part = pl.BlockSpec((None, 8, LANES), lambda hp, g: (4 * g + hp, 0, 0))
    line = pl.BlockSpec((None, 2, _line_width(B_HALF_WINDOW)), lambda hp, g: (4 * g + hp, 0, 0))
    return pl.pallas_call(
        body, name="attn_b_bwd", grid=(4, 3),
        in_specs=[_seq_block(lambda hp, g: (0, QB_BLK + 4 * g + hp)), _seq_block(lambda hp, g: (0, KB_BLK + 4 * g + hp)),
                  _seq_block(lambda hp, g: (0, VB_BLK + 4 * g + hp)), vec, vec,
                  line,
                  _seq_block(lambda hp, g: (0, hp)), _seq_block(lambda hp, g: (0, hp)), _seq_block(lambda hp, g: (0, hp))],
        out_specs=[pl.BlockSpec((None, SEQ, LANES), lambda hp, g: (4 * g + hp, 0, 0))] * 3 + [part, part, line],
        out_shape=[jax.ShapeDtypeStruct((12, SEQ, LANES), BF16)] * 3
        + [jax.ShapeDtypeStruct((12, 8, LANES), F32)] * 2
        + [jax.ShapeDtypeStruct((12, 2, _line_width(B_HALF_WINDOW)), F32)],
        scratch_shapes=[seq_f32, padded, padded, seq_f32, seq_f32, padded, padded,
                        pltpu.VMEM((2 * Q_BLOCK, B_W), F32), pltpu.VMEM((2 * Q_BLOCK, B_W), F32)],
        compiler_params=_params("arbitrary", "arbitrary"),
    )(qkv, qkv, qkv, gain_q, gain_k, bias, out, lse, d_out)


def _sigmoid(t):
    return 1.0 / (1.0 + jnp.exp(-t))


def _middle(out_a, out_b, gates, x, target, w_a, w_b, w_out, b_merge):
    tm = 256
    n_steps = SEQ // tm

    def body(oa_ref, ob_ref, g_ref, x_ref, t_ref, wa_ref, wb_ref, wo_ref, bm_ref,
             dy_ref, dg_ref, doa_ref, dob_ref, dwa_ref, dwb_ref, dwo_ref, dbm_ref, sq_ref):
        @pl.when(pl.program_id(0) == 0)
        def _():
            for ref in (dwa_ref, dwb_ref, dwo_ref, dbm_ref, sq_ref):
                ref[...] = jnp.zeros_like(ref)

        gate_a, gate_b = g_ref[:, 0:512], g_ref[:, 512:1024]
        sig_a, sig_b = _sigmoid(gate_a), _sigmoid(gate_b)
        silu_a, silu_b = gate_a * sig_a, gate_b * sig_b
        oa, ob = oa_ref[...], ob_ref[...]
        ya, yb = (oa * silu_a).astype(BF16), (ob * silu_b).astype(BF16)
        br_a, br_b = _dot(ya, wa_ref[...]), _dot(yb, wb_ref[...])
        m0 = _sigmoid(g_ref[:, 1024:2048] + bm_ref[0:1, :])
        m1 = _sigmoid(g_ref[:, 2048:3072] + bm_ref[1:2, :])
        merged = (m0 * br_a + m1 * br_b).astype(BF16)
        err = (x_ref[...] + _dot(merged, wo_ref[...])) - t_ref[...]
        sq_ref[...] += jnp.sum(err * err, axis=0, keepdims=True)

        dy = err * (1.0 / D_MODEL)
        dy_ref[...] = dy
        dyb = dy.astype(BF16)
        dmerged = _dot(dyb, wo_ref[...], NT)
        dwo_ref[...] += _dot(merged, dyb, TN)
        dbr_a, dbr_b = (dmerged * m0).astype(BF16), (dmerged * m1).astype(BF16)
        dm0 = (dmerged * br_a) * (m0 * (1.0 - m0))
        dm1 = (dmerged * br_b) * (m1 * (1.0 - m1))
        dbm_ref[0:1, :] += jnp.sum(dm0, axis=0, keepdims=True)
        dbm_ref[1:2, :] += jnp.sum(dm1, axis=0, keepdims=True)
        for s in range(N_CHIPS):
            cols = slice(256 * s, 256 * (s + 1))
            dwa_ref[s] += _dot(ya, dbr_a[:, cols], TN)
            dwb_ref[s] += _dot(yb, dbr_b[:, cols], TN)
        dya, dyb_ = _dot(dbr_a, wa_ref[...], NT), _dot(dbr_b, wb_ref[...], NT)
        doa_ref[...] = dya * silu_a
        dob_ref[...] = dyb_ * silu_b
        d_gates = (((dya * oa) * (sig_a * (1.0 + gate_a * (1.0 - sig_a)))).astype(BF16),
                   ((dyb_ * ob) * (sig_b * (1.0 + gate_b * (1.0 - sig_b)))).astype(BF16),
                   dm0.astype(BF16), dm1.astype(BF16))
        blk = 0
        for part in d_gates:
            for c0 in range(0, part.shape[1], 256):
                dg_ref[blk] = part[:, c0:c0 + 256]
                blk += 1

    def rows(width):
        return pl.BlockSpec((tm, width), lambda i: (i, 0))

    def whole(*shape):
        return pl.BlockSpec(shape, lambda i: (0,) * len(shape))

    return pl.pallas_call(
        body, name="middle", grid=(n_steps,),
        in_specs=[rows(512), rows(512), rows(GATE_WIDTH), rows(D_MODEL), rows(D_MODEL),
                  whole(512, D_MODEL), whole(512, D_MODEL), whole(D_MODEL, D_MODEL), whole(2, D_MODEL)],
        out_specs=[rows(D_MODEL), pl.BlockSpec((GATE_WIDTH // 256, tm, 256), lambda i: (0, i, 0)), rows(512), rows(512),
                   whole(N_CHIPS, 512, 256), whole(N_CHIPS, 512, 256), whole(D_MODEL, D_MODEL),
                   whole(2, D_MODEL), whole(1, D_MODEL)],
        out_shape=[jax.ShapeDtypeStruct((SEQ, D_MODEL), F32), jax.ShapeDtypeStruct((GATE_WIDTH // 256, SEQ, 256), BF16),
                   jax.ShapeDtypeStruct((SEQ, 512), F32), jax.ShapeDtypeStruct((SEQ, 512), F32),
                   jax.ShapeDtypeStruct((N_CHIPS, 512, 256), F32), jax.ShapeDtypeStruct((N_CHIPS, 512, 256), F32),
                   jax.ShapeDtypeStruct((D_MODEL, D_MODEL), F32), jax.ShapeDtypeStruct((2, D_MODEL), F32),
                   jax.ShapeDtypeStruct((1, D_MODEL), F32)],
        compiler_params=_params("arbitrary"),
    )(out_a, out_b, gates, x, target, w_a, w_b, w_out, b_merge)


def _which(j, edges, fns):
    lo = 0
    for hi, fn in zip(edges, fns):
        pl.when((j >= lo) & (j < hi))(fn)
        lo = hi


def _d_w_in(d_proj, h):
    plan, step, width = [], 0, 0
    for p in d_proj:
        total = p.shape[0] * p.shape[2]
        if width + total <= W_BLOCK:
            plan.append((p.shape[0], step, 1))
            width += total
            if width == W_BLOCK:
                step, width = step + 1, 0
        else:
            assert width == 0 and total % W_BLOCK == 0
            plan.append((W_BLOCK // p.shape[2], step, total // W_BLOCK))
            step += total // W_BLOCK
    assert width == 0 and step == IN_WIDTH // W_BLOCK
    firsts = sorted({first for _, first, _ in plan})
    edges = firsts[1:] + [step]
    halves = 2

    def body(*refs):
        pieces, h_ref, o_ref, acc_ref = refs[:-3], refs[-3], refs[-2], refs[-1]
        k = pl.program_id(1)

        def emit(group):
            def fn():
                cols = jnp.concatenate([ref[b] for ref in group for b in range(ref.shape[0])], axis=1)
                term = _dot(cols, h_ref[...], TN)

                @pl.when(k == 0)
                def _():
                    acc_ref[...] = term

                @pl.when(k == halves - 1)
                def _():
                    o_ref[...] = (acc_ref[...] + term).astype(BF16)
            return fn

        groups = [[ref for ref, (_, first, _) in zip(pieces, plan) if first == f] for f in firsts]
        _which(pl.program_id(0), edges, [emit(group) for group in groups])

    def cols_spec(piece, n, first, steps):
        def index(j, k):
            return jnp.clip(j - first, 0, steps - 1), jnp.where((j >= first) & (j < first + steps), k, 0), 0
        return pl.BlockSpec((n, SEQ // halves, piece.shape[2]), index)

    return pl.pallas_call(
        body, name="d_w_in", grid=(step, halves),
        in_specs=[cols_spec(p, *pl_) for p, pl_ in zip(d_proj, plan)]
        + [pl.BlockSpec((SEQ // halves, D_MODEL), lambda j, k: (k, 0))],
        out_specs=pl.BlockSpec((W_BLOCK, D_MODEL), lambda j, k: (j, 0)),
        out_shape=jax.ShapeDtypeStruct((IN_WIDTH, D_MODEL), BF16),
        scratch_shapes=[pltpu.VMEM((W_BLOCK, D_MODEL), F32)],
        compiler_params=_params("arbitrary", "arbitrary"),
    )(*d_proj, h)


def _d_x(d_proj, w_t, x, gain, dy, chip_sums):
    tm = 256
    n_steps = SEQ // tm
    n_w = IN_WIDTH // W_BLOCK
    n_p, n_s = len(d_proj), len(chip_sums)

    def body(*refs):
        pieces, w_refs = refs[:n_p], refs[n_p:n_p + n_w]
        x_ref, g_ref, dy_ref = refs[n_p + n_w:n_p + n_w + 3]
        q_refs = refs[n_p + n_w + 3:n_p + n_w + 3 + n_s]
        dx_ref, dgain_ref = refs[n_p + n_w + 3 + n_s:n_p + n_w + 5 + n_s]
        o_refs = refs[n_p + n_w + 5 + n_s:n_p + n_w + 5 + 2 * n_s]
        send_sems, recv_sems = refs[n_p + n_w + 5 + 2 * n_s:] if n_s else (None, None)

        @pl.when(pl.program_id(0) == 0)
        def _():
            dgain_ref[...] = jnp.zeros_like(dgain_ref)
            if n_s:
                for cp in _scatter_copies(q_refs, o_refs, send_sems, recv_sems):
                    cp.start()

        blocks = [(piece, k) for piece in pieces for k in range(piece.shape[0])]
        dh, group, width, blk = None, [], 0, 0
        for piece, k in blocks:
            group.append(piece[k])
            width += piece.shape[2]
            if width == W_BLOCK:
                term = _dot(jnp.concatenate(group, axis=1), w_refs[blk][...])
                dh = term if dh is None else dh + term
                group, width, blk = [], 0, blk + 1
        assert not group and blk == n_w
        xf = x_ref[...]
        r = lax.rsqrt(jnp.mean(xf * xf, axis=-1, keepdims=True) + EPS)
        xh = xf * r
        dxh = dh * g_ref[...]
        dx_ref[...] = r * (dxh - xh * jnp.mean(dxh * xh, axis=-1, keepdims=True)) + dy_ref[...]
        dgain_ref[...] += _rows8(jnp.sum(dh * xh, axis=0, keepdims=True))

        if n_s:
            @pl.when(pl.program_id(0) == n_steps - 1)
            def _():
                for cp in _scatter_copies(q_refs, o_refs, send_sems, recv_sems):
                    cp.wait()

    row = pl.BlockSpec((tm, D_MODEL), lambda i: (i, 0))
    res = pl.pallas_call(
        body, name="d_x", grid=(n_steps,),
        in_specs=[pl.BlockSpec((p.shape[0], tm, p.shape[2]), lambda i: (0, i, 0)) for p in d_proj] + _w_blocks(0, n_w)
        + [row, pl.BlockSpec((1, D_MODEL), lambda i: (0, 0)), row] + [ANY] * n_s,
        out_specs=[row, pl.BlockSpec((8, D_MODEL), lambda i: (0, 0))] + [ANY] * n_s,
        out_shape=[jax.ShapeDtypeStruct((SEQ, D_MODEL), F32), jax.ShapeDtypeStruct((8, D_MODEL), F32)]
        + [jax.ShapeDtypeStruct((3,) + q.shape[1:], BF16) for q in chip_sums],
        scratch_shapes=[pltpu.SemaphoreType.DMA((3 * n_s,)), pltpu.SemaphoreType.DMA((3 * n_s,))] if n_s else [],
        compiler_params=_params("arbitrary"),
    )(*d_proj, *([w_t] * n_w), x, gain, dy, *chip_sums)
    return res[0], res[1], res[2:]


def _my_place():
    x, y, c = lax.axis_index("x"), lax.axis_index("y"), lax.axis_index("c")
    return jnp.stack([2 * x + y, c]).astype(jnp.int32)


def _half_rows(ref, half):
    rows = ref.shape[-2] // 2
    idx = (slice(None),) * (len(ref.shape) - 2) + (pl.ds(pl.multiple_of(half * rows, 16), rows), slice(None))
    return ref.at[idx]


def _swap_halves(grads):
    n = len(grads)

    def body(*refs):
        g_refs, o_refs, (send_sems, recv_sems) = refs[:n], refs[n:2 * n], refs[2 * n:]
        x, y, c = lax.axis_index("x"), lax.axis_index("y"), lax.axis_index("c")
        copies = [pltpu.make_async_remote_copy(src_ref=_half_rows(g, 1 - c), dst_ref=o, send_sem=send_sems.at[k],
                                               recv_sem=recv_sems.at[k], device_id=(x, y, 1 - c), device_id_type=MESH)
                  for k, (g, o) in enumerate(zip(g_refs, o_refs))]
        for cp in copies:
            cp.start()
        for cp in copies:
            cp.wait()

    return pl.pallas_call(
        body, name="reduce_swap_halves", in_specs=[ANY] * n, out_specs=[ANY] * n,
        out_shape=[jax.ShapeDtypeStruct((N_CHIPS, g.shape[1] // 2, D_MODEL), g.dtype) for g in grads],
        scratch_shapes=[pltpu.SemaphoreType.DMA((n,)), pltpu.SemaphoreType.DMA((n,))],
    )(*grads)


def _row_tile(rows):
    return max(t for t in range(16, 385, 16) if rows % t == 0)


def _add_halves(place, grads, theirs, name):
    half = theirs.shape[1]
    tr = _row_tile(half)
    n = half // tr

    def body(place_ref, g_ref, t_ref, o_ref):
        o_ref[...] = (g_ref[...].astype(F32) + t_ref[...].astype(F32)).astype(BF16)

    return pl.pallas_call(
        body, name=name,
        grid_spec=pltpu.PrefetchScalarGridSpec(
            num_scalar_prefetch=1, grid=(N_CHIPS, n),
            in_specs=[pl.BlockSpec((None, tr, D_MODEL), lambda s, i, p: (s, p[1] * n + i, 0)),
                      pl.BlockSpec((None, tr, D_MODEL), lambda s, i, p: (s, i, 0))],
            out_specs=pl.BlockSpec((None, tr, D_MODEL), lambda s, i, p: (s, i, 0))),
        out_shape=jax.ShapeDtypeStruct((N_CHIPS, half, D_MODEL), BF16),
        compiler_params=_params("arbitrary", "arbitrary"),
    )(place, grads, theirs)


def _scatter_copies(q_refs, o_refs, send_sems, recv_sems):
    x, y, c = lax.axis_index("x"), lax.axis_index("y"), lax.axis_index("c")
    chips = [(1 - x, y), (x, 1 - y), (1 - x, 1 - y)]
    return [pltpu.make_async_remote_copy(src_ref=q.at[2 * cx + cy], dst_ref=o.at[j],
                                         send_sem=send_sems.at[3 * k + j], recv_sem=recv_sems.at[3 * k + j],
                                         device_id=(cx, cy, c), device_id_type=MESH)
            for k, (q, o) in enumerate(zip(q_refs, o_refs)) for j, (cx, cy) in enumerate(chips)]


def _add_chips(place, chip_sums, others, name):
    half = others.shape[1]
    tr = _row_tile(half)
    n = half // tr

    def body(place_ref, q_ref, o_ref, r_ref):
        acc = q_ref[...].astype(F32)
        for j in range(3):
            acc = acc + o_ref[j].astype(F32)
        r_ref[...] = acc

    return pl.pallas_call(
        body, name=name,
        grid_spec=pltpu.PrefetchScalarGridSpec(
            num_scalar_prefetch=1, grid=(n,),
            in_specs=[pl.BlockSpec((None, tr, D_MODEL), lambda i, p: (p[0], i, 0)),
                      pl.BlockSpec((3, tr, D_MODEL), lambda i, p: (0, i, 0))],
            out_specs=pl.BlockSpec((tr, D_MODEL), lambda i, p: (p[1] * n + i, 0))),
        out_shape=jax.ShapeDtypeStruct((2 * half, D_MODEL), F32),
        compiler_params=_params("arbitrary"),
    )(place, chip_sums, others)


def _join_halves(shards):
    n = len(shards)

    def body(*refs):
        o_refs, (send_sems, recv_sems) = refs[n:2 * n], refs[2 * n:]
        x, y, c = lax.axis_index("x"), lax.axis_index("y"), lax.axis_index("c")

        def copy(k, rows):
            return pltpu.make_async_remote_copy(src_ref=rows, dst_ref=rows, send_sem=send_sems.at[k],
                                                recv_sem=recv_sems.at[k], device_id=(x, y, 1 - c), device_id_type=MESH)

        sends = [copy(k, _half_rows(o, c)) for k, o in enumerate(o_refs)]
        for cp in sends:
            cp.start()
        for k, o in enumerate(o_refs):
            copy(k, _half_rows(o, 1 - c)).wait_recv()
        for cp in sends:
            cp.wait_send()

    return pl.pallas_call(
        body, name="reduce_join_halves", in_specs=[ANY] * n, out_specs=[ANY] * n,
        out_shape=[jax.ShapeDtypeStruct(s.shape, F32) for s in shards],
        input_output_aliases={k: k for k in range(n)},
        scratch_shapes=[pltpu.SemaphoreType.DMA((n,)), pltpu.SemaphoreType.DMA((n,))],
    )(*shards)


def _gather_small(block):
    rows = block.shape[0]

    def body(b_ref, o_ref, send_sems, recv_sems, local_sem):
        x, y, c = lax.axis_index("x"), lax.axis_index("y"), lax.axis_index("c")
        me, sibling = (x, y, c), (x, y, 1 - c)
        chips = [(1 - x, y), (x, 1 - y), (1 - x, 1 - y)]

        def at(px, py, pc):
            return o_ref.at[pl.ds(pl.multiple_of((4 * px + 2 * py + pc) * rows, 8), rows), :]

        def copy(k, block_of, to, src=None):
            return pltpu.make_async_remote_copy(src_ref=at(*block_of) if src is None else src, dst_ref=at(*block_of),
                                                send_sem=send_sems.at[k], recv_sem=recv_sems.at[k],
                                                device_id=to, device_id_type=MESH)

        mine = pltpu.make_async_copy(b_ref, at(*me), local_sem)
        mine.start()
        first = [copy(0, me, sibling, src=b_ref)]
        first += [copy(1 + j, me, (*chip, c), src=b_ref) for j, chip in enumerate(chips)]
        for cp in first:
            cp.start()
        passed = [copy(4 + j, (*chip, c), sibling) for j, chip in enumerate(chips)]
        for j, chip in enumerate(chips):
            copy(1 + j, (*chip, c), me).wait_recv()
            passed[j].start()
        copy(0, sibling, me).wait_recv()
        for j, chip in enumerate(chips):
            copy(4 + j, (*chip, 1 - c), me).wait_recv()
        for cp in first + passed:
            cp.wait_send()
        mine.wait()

    return pl.pallas_call(
        body, name="gather_small_grads",
        in_specs=[pl.BlockSpec(memory_space=pltpu.VMEM)], out_specs=pl.BlockSpec(memory_space=pltpu.VMEM),
        out_shape=jax.ShapeDtypeStruct((8 * rows, D_MODEL), F32),
        scratch_shapes=[pltpu.SemaphoreType.DMA((7,)), pltpu.SemaphoreType.DMA((7,)), pltpu.SemaphoreType.DMA],
    )(block)


def _sum_devices(blocks):
    def body(b_ref, o_ref):
        acc = b_ref[0:8, :]
        for dev in range(1, 8):
            acc = acc + b_ref[8 * dev:8 * dev + 8, :]
        o_ref[...] = acc

    return pl.pallas_call(body, name="sum_small_grads", out_shape=jax.ShapeDtypeStruct((8, D_MODEL), F32))(blocks)


def _adamw_math(w, g, m, v):
    m = ADAM_B1 * m + (1.0 - ADAM_B1) * g
    v = ADAM_B2 * v + (1.0 - ADAM_B2) * (g * g)
    m_hat = m / (1.0 - ADAM_B1 ** ADAM_STEP)
    v_hat = v / (1.0 - ADAM_B2 ** ADAM_STEP)
    return -ADAM_LR * (m_hat / (jnp.sqrt(v_hat) + ADAM_EPS) + ADAM_WD * w), m, v


def _adamw(w, g, m, v, name):
    r, c = w.shape
    tr = 128 if r % 128 == 0 else r

    def body(w_ref, g_ref, m_ref, v_ref, d_ref, nm_ref, nv_ref):
        d_ref[...], nm_ref[...], nv_ref[...] = _adamw_math(w_ref[...], g_ref[...], m_ref[...], v_ref[...])

    spec = pl.BlockSpec((tr, c), lambda i: (i, 0))
    return pl.pallas_call(
        body, name=name, grid=(r // tr,), in_specs=[spec] * 4, out_specs=[spec] * 3,
        out_shape=[jax.ShapeDtypeStruct((r, c), F32)] * 3, compiler_params=_params("arbitrary"),
    )(w, g, m, v)


def _adamw_small(ws, gs, ms, vs):
    n = len(ws)

    def body(*refs):
        ins, outs = refs[:4 * n], refs[4 * n:]
        for k in range(n):
            d, m, v = _adamw_math(ins[k][...], ins[n + k][...], ins[2 * n + k][...], ins[3 * n + k][...])
            outs[k][...], outs[n + k][...], outs[2 * n + k][...] = d, m, v

    shapes = [jax.ShapeDtypeStruct(w.shape, F32) for w in ws]
    res = pl.pallas_call(body, name="adamw_small", out_shape=shapes * 3)(*ws, *gs, *ms, *vs)
    return res[:n], res[n:2 * n], res[2 * n:]


def _fold_heads(partials):
    t = jnp.sum(partials[:, 0, :], axis=0)
    return (t[:HEAD_DIM] + t[HEAD_DIM:]).reshape(1, HEAD_DIM)


def _local_step(x, target, norm_gain, w_t, w_a, w_b, w_o, b_m, q_norm_a, k_norm_a, q_norm_b, k_norm_b, sink_a,
                rel_bias, start_reduce=None):
    two = lambda gain: jnp.concatenate([gain, gain], axis=1)
    bias_a = _bias_lines(rel_bias[:, :8], A_HALF_WINDOW, 1)
    bias_b = jnp.concatenate([_bias_lines(rel_bias[:, 8 + 8 * g:16 + 8 * g], B_HALF_WINDOW, d)
                              for g, d in enumerate(B_DILATIONS)], axis=0)

    qkv, h = _in_proj(x, norm_gain, w_t, 0, QKV_WIDTH // W_BLOCK, BF16, "in_proj_qkv", True)
    gates, = _in_proj(x, norm_gain, w_t, QKV_WIDTH // W_BLOCK, GATE_WIDTH // W_BLOCK, F32, "in_proj_gates", False)
    out_a, lse_a = _attn_a_fwd(qkv, two(q_norm_a), two(k_norm_a), bias_a, sink_a)
    out_b, lse_b = _attn_b_fwd(qkv, two(q_norm_b), two(k_norm_b), bias_b)

    dy, dgates, d_out_a, d_out_b, d_wa, d_wb, d_wo, d_bm, sq = _middle(
        out_a, out_b, gates, x, target, w_a, w_b, w_o, b_m)
    loss = (0.5 / D_MODEL) * jnp.sum(sq)

    dq_a, dkv_a, dgq_a, dgk_a, ds_a, dsink = _attn_a_bwd(
        qkv, two(q_norm_a), two(k_norm_a), bias_a, sink_a, out_a, lse_a, d_out_a)
    dq_b, dk_b, dv_b, dgq_b, dgk_b, ds_b = _attn_b_bwd(
        qkv, two(q_norm_b), two(k_norm_b), bias_b, out_b, lse_b, d_out_b)
    d_proj = (dq_a, dkv_a, dq_b, dk_b, dv_b, dgates)

    d_bm_rows = jnp.pad(d_bm.reshape(2, N_CHIPS, 256).transpose(1, 0, 2),
                        ((0, 0), (0, REST_ROWS - 514), (0, D_MODEL - 256)))
    rest = jnp.concatenate([d_wo.reshape(N_CHIPS, 256, D_MODEL), d_wa.reshape(N_CHIPS, 128, D_MODEL),
                            d_wb.reshape(N_CHIPS, 128, D_MODEL), d_bm_rows], axis=1)
    grads = [_d_w_in(d_proj, h).reshape(N_CHIPS, W_IN_SHARD, D_MODEL), rest]
    narrow = [grads[0], rest.astype(BF16)]
    chip_sums = start_reduce(grads, narrow) if start_reduce is not None else []
    grad_x, d_gain, others = _d_x(d_proj, w_t, x, norm_gain, dy, chip_sums)

    d_rel = jnp.concatenate(
        [_bias_grad(ds_a, A_HALF_WINDOW, 1)]
        + [_bias_grad(ds_b[4 * g:4 * g + 4], B_HALF_WINDOW, d) for g, d in enumerate(B_DILATIONS)], axis=1)
    d_sink = jnp.sum(dsink, axis=(2, 3)).reshape(1, 8)
    dgk_a_row = dgk_a[0]
    small = jnp.zeros((8, D_MODEL), F32)
    small = small.at[0].set(d_gain[0])
    small = small.at[1].set(d_rel.reshape(-1))
    misc = jnp.concatenate([_fold_heads(dgq_a), (dgk_a_row[:HEAD_DIM] + dgk_a_row[HEAD_DIM:]).reshape(1, HEAD_DIM),
                            _fold_heads(dgq_b), _fold_heads(dgk_b), d_sink], axis=1)
    small = small.at[2, :264].set(misc[0])

    return loss, grad_x, grads, small, chip_sums, others


def _unpack_weights(w_t_all, small_all):
    sm = small_all.reshape(N_CHIPS, SMALL_ROWS, D_MODEL)
    w_o = sm[:, 0:256].reshape(D_MODEL, D_MODEL)
    w_a = sm[:, 256:384].reshape(N_CHIPS, 512, 256).transpose(1, 0, 2).reshape(512, D_MODEL)
    w_b = sm[:, 384:512].reshape(N_CHIPS, 512, 256).transpose(1, 0, 2).reshape(512, D_MODEL)
    b_m = lax.bitcast_convert_type(sm[:, 512].reshape(N_CHIPS, 2, 256, 2), F32)
    return w_t_all, w_a, w_b, w_o, b_m.transpose(1, 0, 2).reshape(2, D_MODEL)


def _pack_small_weights(w_branch_a, w_branch_b, b_merge, w_out):
    b_m = jnp.pad(lax.bitcast_convert_type(b_merge, BF16).reshape(1, D_MODEL), ((0, SMALL_ROWS - 513), (0, 0)))
    return jnp.concatenate([w_out.astype(BF16), w_branch_a.astype(BF16).reshape(128, D_MODEL),
                            w_branch_b.astype(BF16).reshape(128, D_MODEL), b_m], axis=0)


def kernel(x, norm_gain, w_in, q_norm_a, k_norm_a, q_norm_b, k_norm_b, sink_a, rel_bias, w_branch_a, w_branch_b, b_merge, w_out, loss_target, m_norm_gain, m_w_in, m_q_norm_a, m_k_norm_a, m_q_norm_b, m_k_norm_b, m_sink_a, m_rel_bias, m_w_branch_a, m_w_branch_b, m_b_merge, m_w_out, v_norm_gain, v_w_in, v_q_norm_a, v_k_norm_a, v_q_norm_b, v_k_norm_b, v_sink_a, v_rel_bias, v_w_branch_a, v_w_branch_b, v_b_merge, v_w_out):
    wt_shard = _transpose_cast(w_in, BF16, "w_in_transpose")
    w_t, w_a, w_b, w_o, b_m = _unpack_weights(
        *_gather_weights(wt_shard, _pack_small_weights(w_branch_a[0], w_branch_b[0], b_merge[0], w_out[0])))

    place = _my_place()
    names = ("w_in", "rest")

    def start_reduce(grads, narrow):
        return [_add_halves(place, g, t, "reduce_add_halves_" + n) for g, t, n in zip(grads, _swap_halves(narrow), names)]

    loss_part, grad_x, _, small, chip_sums, others = _local_step(
        x[0], loss_target[0], norm_gain, w_t, w_a, w_b, w_o, b_m, q_norm_a, k_norm_a, q_norm_b, k_norm_b,
        sink_a, rel_bias, start_reduce)

    g_wt, g_rest = _join_halves([_add_chips(place, q, o, "reduce_add_chips_" + n)
                                 for q, o, n in zip(chip_sums, others, names)])
    small = _sum_devices(_gather_small(small.at[3, 0].set(loss_part)))
    loss = small[3, 0]

    g_w_in = _transpose_cast(g_wt, F32, "grad_w_in_transpose")
    g_w_out = g_rest[0:256]
    g_w_a = g_rest[256:384].reshape(512, 256)
    g_w_b = g_rest[384:512].reshape(512, 256)
    g_b_merge = g_rest[512:514, :256]
    g_norm_gain = small[0:1]
    g_rel_bias = small[1].reshape(N_BUCKETS, N_BUCKETS)
    g_q_a, g_k_a, g_q_b, g_k_b = (small[2:3, 64 * k:64 * k + 64] for k in range(4))
    g_sink = small[2:3, 256:264]

    big_names = (("w_in", w_in, g_w_in, m_w_in, v_w_in),
                 ("w_branch_a", w_branch_a, g_w_a, m_w_branch_a, v_w_branch_a),
                 ("w_branch_b", w_branch_b, g_w_b, m_w_branch_b, v_w_branch_b),
                 ("w_out", w_out, g_w_out, m_w_out, v_w_out))
    upd = {name: (g,) + tuple(_adamw(w[0], g, m[0], v[0], "adamw_" + name)) for name, w, g, m, v in big_names}
    small_names = ("norm_gain", "q_norm_a", "k_norm_a", "q_norm_b", "k_norm_b", "sink_a", "rel_bias", "b_merge")
    ws = [norm_gain, q_norm_a, k_norm_a, q_norm_b, k_norm_b, sink_a, rel_bias, b_merge[0]]
    gs = [g_norm_gain, g_q_a, g_k_a, g_q_b, g_k_b, g_sink, g_rel_bias, g_b_merge]
    ms = [m_norm_gain, m_q_norm_a, m_k_norm_a, m_q_norm_b, m_k_norm_b, m_sink_a, m_rel_bias, m_b_merge[0]]
    vs = [v_norm_gain, v_q_norm_a, v_k_norm_a, v_q_norm_b, v_k_norm_b, v_sink_a, v_rel_bias, v_b_merge[0]]
    ds, nms, nvs = _adamw_small(ws, gs, ms, vs)
    for k, name in enumerate(small_names):
        upd[name] = (gs[k], ds[k], nms[k], nvs[k])

    order = ("norm_gain", "w_in", "q_norm_a", "k_norm_a", "q_norm_b", "k_norm_b", "sink_a", "rel_bias",
             "w_branch_a", "w_branch_b", "b_merge", "w_out")
    lead = {"w_in", "w_branch_a", "w_branch_b", "b_merge", "w_out"}
    outs = [loss, grad_x[None]]
    for part in range(4):
        outs += [upd[name][part][None] if name in lead else upd[name][part] for name in order]
    return tuple(outs)
```

```python
import math

import numpy as np
import jax
import jax.numpy as jnp
from jax import lax
from jax.experimental import pallas as pl
from jax.experimental.pallas import tpu as pltpu

F32 = jnp.float32
BF16 = jnp.bfloat16

SEQ = 4096
D_MODEL = 1024
HEAD_DIM = 64
LANES = 128
EPS = 1e-6
NEG_INF = -1e30
SCALE = HEAD_DIM ** -0.5
N_BUCKETS = 32
MAX_DISTANCE = 1024
N_CHIPS = 4

A_HALF_WINDOW = 128
B_HALF_WINDOW = 64
B_DILATIONS = (1, 4, 16)
Q_BLOCK = 128

QKV_WIDTH = 5376
GATE_WIDTH = 3072
QA_BLK, KA_BLK, VA_BLK = 0, 4, 5
QB_BLK, KB_BLK, VB_BLK = 6, 18, 30
IN_WIDTH = QKV_WIDTH + GATE_WIDTH
W_IN_SHARD = IN_WIDTH // N_CHIPS

SMALL_ROWS = 544
REST_ROWS = 544

ADAM_LR = 0.001
ADAM_B1 = 0.9
ADAM_B2 = 0.999
ADAM_EPS = 1e-08
ADAM_WD = 0.01
ADAM_STEP = 10

VMEM_LIMIT = 56 * 1024 * 1024

NT = (((1,), (1,)), ((), ()))
TN = (((0,), (0,)), ((), ()))
MESH = pl.DeviceIdType.MESH
ANY = pl.BlockSpec(memory_space=pl.ANY)


def _dot(a, b, dims=None):
    if dims is None:
        return jnp.dot(a, b, preferred_element_type=F32)
    return lax.dot_general(a, b, dims, preferred_element_type=F32)


def _params(*semantics):
    return pltpu.CompilerParams(dimension_semantics=semantics or None, vmem_limit_bytes=VMEM_LIMIT)


def _line_width(half_window):
    return pl.cdiv(2 * Q_BLOCK + 2 * half_window - 1, LANES) * LANES


def _bucket_onehot(half_window, stride):
    rel = np.arange(_line_width(half_window)) - (Q_BLOCK - 1) - half_window
    band = np.abs(rel) <= half_window
    rel = rel * stride
    half, max_exact = N_BUCKETS // 2, N_BUCKETS // 4
    n = np.abs(rel)
    nf = np.maximum(n, max_exact).astype(np.float32)
    large = max_exact + (np.log(nf / np.float32(max_exact)) / np.float32(math.log(MAX_DISTANCE / max_exact))
                         * np.float32(half - max_exact)).astype(np.int32)
    large = np.minimum(large, half - 1)
    bucket = (rel > 0).astype(np.int32) * half + np.where(n < max_exact, n, large)
    onehot = (bucket[..., None] == np.arange(N_BUCKETS)) & band[..., None]
    return onehot.astype(np.float32), band


def _bias_lines(rel_bias_cols, half_window, stride):
    onehot, band = _bucket_onehot(half_window, stride)
    h = rel_bias_cols.shape[1]
    t = jnp.einsum("tb,bh->ht", jnp.asarray(onehot), rel_bias_cols, precision=lax.Precision.HIGHEST)
    t = t + jnp.asarray(np.where(band, 0.0, NEG_INF).astype(np.float32))
    return t.reshape(h // 2, 2, -1)


def _bias_grad(d_lines, half_window, stride):
    onehot, _ = _bucket_onehot(half_window, stride)
    h = d_lines.shape[0] * 2
    return jnp.einsum("tb,ht->bh", jnp.asarray(onehot), d_lines.reshape(h, -1), precision=lax.Precision.HIGHEST)


def _unroll_bias(line_ref, tile_ref, w):
    width = line_ref.shape[1]
    for j in range(2):
        rows = jnp.broadcast_to(line_ref[j:j + 1, :], (Q_BLOCK, width))
        rows = pltpu.roll(rows, width - (Q_BLOCK - 1), 1, stride=1, stride_axis=0)
        tile_ref[j * Q_BLOCK:(j + 1) * Q_BLOCK, :] = rows[:, :w]


def _fold_bias_grad(tile_ref, line_ref, w):
    width = line_ref.shape[1]
    row = lax.broadcasted_iota(jnp.int32, (Q_BLOCK, Q_BLOCK), 0)
    col = lax.broadcasted_iota(jnp.int32, (Q_BLOCK, Q_BLOCK), 1)
    flip = jnp.where(row + col == Q_BLOCK - 1, 1.0, 0.0).astype(BF16)
    for j in range(2):
        tile = tile_ref[j * Q_BLOCK:(j + 1) * Q_BLOCK, :]
        hi = tile.astype(BF16)
        lo = (tile - hi.astype(F32)).astype(BF16)
        rows = _dot(flip, hi) + _dot(flip, lo)
        rows = jnp.concatenate([rows, jnp.zeros((Q_BLOCK, width - w), F32)], axis=1)
        rows = pltpu.roll(rows, 0, 1, stride=1, stride_axis=0)
        line_ref[j:j + 1, :] = jnp.sum(rows, axis=0, keepdims=True)


def _transpose_cast(w, out_dtype, name):
    lead = (None,) * (w.ndim - 2)
    zero = (0,) * (w.ndim - 2)
    r, c = w.shape[-2:]

    def body(w_ref, o_ref):
        o_ref[...] = w_ref[...].T.astype(out_dtype)

    if r % LANES == 0:
        steps = pl.cdiv(c, LANES)
        in_spec = pl.BlockSpec(lead + (r, LANES), lambda j: zero + (0, j))
        out_spec = pl.BlockSpec((LANES, r), lambda j: (j, 0))
    else:
        steps = pl.cdiv(r, LANES)
        in_spec = pl.BlockSpec(lead + (LANES, c), lambda j: zero + (j, 0))
        out_spec = pl.BlockSpec((c, LANES), lambda j: (0, j))
    return pl.pallas_call(
        body, name=name, grid=(steps,), in_specs=[in_spec], out_specs=out_spec,
        out_shape=jax.ShapeDtypeStruct((c, r), out_dtype),
        compiler_params=_params("arbitrary"),
    )(w)


def _gather_weights(wt_shard, small_shard):
    bufs = ((W_IN_SHARD, IN_WIDTH), (SMALL_ROWS, N_CHIPS * SMALL_ROWS))

    stage_rows = 528

    def body(wt_in, sm_in, wt_out, sm_out, send_sems, recv_sems, in_sems, out_sems, stage):
        x, y, c = lax.axis_index("x"), lax.axis_index("y"), lax.axis_index("c")
        sibling = (x, y, 1 - c)
        my_chip = 2 * x + y
        refs = ((wt_in, wt_out), (sm_in, sm_out))

        def keep_own():
            pieces = [(b, r0) for b in range(2) for r0 in range(0, bufs[b][0], stage_rows)]
            outs = []
            for i, (b, r0) in enumerate(pieces):
                rows = min(stage_rows, bufs[b][0] - r0)
                slot = i % 2
                if i >= 2:
                    outs[i - 2].wait()
                buf = stage.at[slot, pl.ds(0, rows), :]
                load = pltpu.make_async_copy(refs[b][0].at[pl.ds(r0, rows), :], buf, in_sems.at[slot])
                load.start()
                load.wait()
                start = pl.multiple_of(my_chip * bufs[b][0] + r0, 16)
                outs.append(pltpu.make_async_copy(buf, refs[b][1].at[pl.ds(start, rows), :], out_sems.at[slot]))
                outs[i].start()
            for cp in outs[-2:]:
                cp.wait()

        def half_of(b, chip, half):
            rows = bufs[b][0]
            start = pl.multiple_of(chip * rows + half * (rows // 2), 16)
            return refs[b][1].at[pl.ds(start, rows // 2), :]

        def copy(k, src, dst, to):
            return pltpu.make_async_remote_copy(src_ref=src, dst_ref=dst, send_sem=send_sems.at[k],
                                                recv_sem=recv_sems.at[k], device_id=to, device_id_type=MESH)

        near = (x + (1 - c) - 2 * x * (1 - c), y + c - 2 * y * c)
        far = (x + c - 2 * x * c, y + (1 - c) - 2 * y * (1 - c))
        diag = (1 - x, 1 - y)
        chip_no = lambda chip: 2 * chip[0] + chip[1]
        sends, passed = [], []
        for b in range(2):
            rows = bufs[b][0]
            src = refs[b][0].at[pl.ds(pl.multiple_of(c * (rows // 2), 16), rows // 2), :]
            sends += [copy(3 * b, src, half_of(b, my_chip, c), (*near, c)),
                      copy(3 * b + 1, src, half_of(b, my_chip, c), (*far, c))]
        for cp in sends:
            cp.start()
        keep_own()

        def pass_on(b, j, chip):
            landed = half_of(b, chip_no(chip), c)
            fwd = copy(6 + 3 * b + j, landed, landed, sibling)
            fwd.start()
            passed.append(fwd)

        for b in range(2):
            landed = half_of(b, chip_no(near), c)
            copy(3 * b, landed, landed, sibling).wait_recv()
            relay = copy(3 * b + 2, landed, landed, (*far, c))
            relay.start()
            sends.append(relay)
            pass_on(b, 0, near)
        for b in range(2):
            for j, chip in ((1, far), (2, diag)):
                landed = half_of(b, chip_no(chip), c)
                copy(3 * b + j, landed, landed, sibling).wait_recv()
                pass_on(b, j, chip)
        for b in range(2):
            for j, chip in ((0, far), (1, near), (2, diag)):
                other = half_of(b, chip_no(chip), 1 - c)
                copy(6 + 3 * b + j, other, other, sibling).wait_recv()
        for cp in sends + passed:
            cp.wait_send()

    return pl.pallas_call(
        body, name="gather_weights",
        in_specs=[ANY, ANY], out_specs=[ANY, ANY],
        out_shape=[jax.ShapeDtypeStruct((bufs[0][1], D_MODEL), BF16),
                   jax.ShapeDtypeStruct((bufs[1][1], D_MODEL), BF16)],
        scratch_shapes=[pltpu.SemaphoreType.DMA((12,)), pltpu.SemaphoreType.DMA((12,)),
                        pltpu.SemaphoreType.DMA((2,)), pltpu.SemaphoreType.DMA((2,)),
                        pltpu.VMEM((2, stage_rows, D_MODEL), BF16)],
    )(wt_shard, small_shard)


W_BLOCK = 768


def _w_blocks(first, count):
    return [pl.BlockSpec((W_BLOCK, D_MODEL), lambda *_, k=k: (first + k, 0)) for k in range(count)]


def _in_proj(x, gain, w_t, first_block, n_blocks, out_dtype, name, keep_h):
    tm = 512

    def body(x_ref, g_ref, *refs):
        w_refs, outs = refs[:n_blocks], refs[n_blocks:]
        xf = x_ref[...]
        r = lax.rsqrt(jnp.mean(xf * xf, axis=-1, keepdims=True) + EPS)
        h = ((xf * r) * g_ref[...]).astype(BF16)
        if keep_h:
            outs[1][...] = h
        for k, w_ref in enumerate(w_refs):
            outs[0][:, k * W_BLOCK:(k + 1) * W_BLOCK] = _dot(h, w_ref[...], NT).astype(out_dtype)

    return pl.pallas_call(
        body, name=name, grid=(SEQ // tm,),
        in_specs=[pl.BlockSpec((tm, D_MODEL), lambda i: (i, 0)), pl.BlockSpec((1, D_MODEL), lambda i: (0, 0))]
        + _w_blocks(first_block, n_blocks),
        out_specs=[pl.BlockSpec((tm, W_BLOCK * n_blocks), lambda i: (i, 0)),
                   pl.BlockSpec((tm, D_MODEL), lambda i: (i, 0))][:2 if keep_h else 1],
        out_shape=[jax.ShapeDtypeStruct((SEQ, W_BLOCK * n_blocks), out_dtype),
                   jax.ShapeDtypeStruct((SEQ, D_MODEL), BF16)][:2 if keep_h else 1],
        compiler_params=_params("arbitrary"),
    )(x, gain, *([w_t] * n_blocks))


CHUNK = 256
CHUNK_UNROLL = 4
TILE_UNROLL = 8


def _low_half():
    return lax.broadcasted_iota(jnp.int32, (1, LANES), 1) < HEAD_DIM


def _half_sum(v, low):
    del low
    row = lax.broadcasted_iota(jnp.int32, (2 * LANES, LANES), 0)
    col = lax.broadcasted_iota(jnp.int32, (2 * LANES, LANES), 1)
    ones = jnp.where((row % LANES) // HEAD_DIM == col // HEAD_DIM, 1.0, 0.0).astype(BF16)
    hi = v.astype(BF16)
    lo = (v - hi.astype(F32)).astype(BF16)
    return _dot(jnp.concatenate([hi, lo], axis=1), ones)


def _chunks(fn, init=0):
    def body(i, carry):
        for u in range(CHUNK_UNROLL):
            carry = fn(pl.multiple_of((i * CHUNK_UNROLL + u) * CHUNK, CHUNK), carry)
        return carry

    return lax.fori_loop(0, SEQ // (CHUNK * CHUNK_UNROLL), body, init)


def _inv_rms(t, low):
    return lax.rsqrt(_half_sum(t * t, low) * (1.0 / HEAD_DIM) + EPS)


def _prep_q(q_ref, gain_ref, qn_ref):
    low = _low_half()

    def step(r0, carry):
        q = q_ref[pl.ds(r0, CHUNK), :].astype(F32)
        qn_ref[pl.ds(r0, CHUNK), :] = ((q * _inv_rms(q, low)) * gain_ref[...]) * SCALE
        return carry

    _chunks(step)


def _own_half(t, keep):
    return jnp.where(keep, t, pltpu.roll(t, HEAD_DIM, 1))


def _prep_kv(k_ref, v_ref, gain_ref, kp_ref, vp_ref, pad, keep=None):
    low = _low_half()
    zeros = jnp.zeros((pad, LANES), F32)
    for ref in (kp_ref, vp_ref):
        ref[pl.ds(0, pad), :] = zeros
        ref[pl.ds(pad + SEQ, pad), :] = zeros

    def step(r0, carry):
        k = k_ref[pl.ds(r0, CHUNK), :].astype(F32)
        v = v_ref[pl.ds(r0, CHUNK), :].astype(F32)
        kn = (k * _inv_rms(k, low)) * gain_ref[...]
        if keep is not None:
            kn, v = _own_half(kn, keep), _own_half(v, keep)
        kp_ref[pl.ds(pad + r0, CHUNK), :] = kn
        vp_ref[pl.ds(pad + r0, CHUNK), :] = v
        return carry

    _chunks(step)


def _tiles(d, half_window, fn):
    w = Q_BLOCK + 2 * half_window
    length = SEQ // d
    n_blocks = length // Q_BLOCK
    col = lax.broadcasted_iota(jnp.int32, (1, w), 1)

    def step(it, carry):
        c, n = it // n_blocks, it % n_blocks
        start = c + (d * Q_BLOCK) * n
        if d == 1:
            start = pl.multiple_of(start, Q_BLOCK)
            q_rows, k_rows = pl.ds(start, Q_BLOCK), pl.ds(start, w)
        else:
            q_rows, k_rows = pl.ds(start, Q_BLOCK, stride=d), pl.ds(start, w, stride=d)
        t = n * Q_BLOCK - half_window + col
        edge = jnp.where((t < 0) | (t >= length), NEG_INF, 0.0)
        fn(q_rows, k_rows, edge)
        return carry

    lax.fori_loop(0, d * n_blocks, step, 0, unroll=TILE_UNROLL)


def _stack_heads(t, low):
    return jnp.concatenate([jnp.where(low, t, 0.0), jnp.where(low, 0.0, t)], axis=0).astype(BF16)


def _unstack_heads(t, low):
    return jnp.where(low, t[:Q_BLOCK], t[Q_BLOCK:])


def _per_head(pair):
    return jnp.concatenate([jnp.full((Q_BLOCK, 1), pair[0], F32), jnp.full((Q_BLOCK, 1), pair[1], F32)], axis=0)


def _fwd_tiles(qn_ref, kp_ref, vp_ref, bias_ref, emit, *, d, half_window, sinks=None):
    low = _low_half()
    w = Q_BLOCK + 2 * half_window
    sink = None if sinks is None else _per_head(sinks)

    def tile(q_rows, k_rows, edge):
        q2 = _stack_heads(qn_ref[q_rows, :], low)
        k = kp_ref[k_rows, :].astype(BF16)
        v1 = jnp.concatenate([vp_ref[k_rows, :], jnp.ones((w, LANES), F32)], axis=1).astype(BF16)
        s = _dot(q2, k, NT) + bias_ref[...] + edge
        m = jnp.max(s, axis=-1, keepdims=True)
        if sink is not None:
            m = jnp.maximum(m, sink)
        o = _dot(jnp.exp(s - m).astype(BF16), v1)
        l = o[:, LANES:]
        if sink is not None:
            l = l + jnp.exp(sink - m)
        emit(q_rows, _unstack_heads(o[:, :LANES] * (1.0 / l), low), _unstack_heads(m + jnp.log(l), low))

    _tiles(d, half_window, tile)


def _bwd_tiles(qn_ref, kp_ref, vp_ref, bias_ref, do_ref, lse_ref, delta_ref, dq_ref, dk_ref, dv_ref, ds_ref,
               *, d, half_window, sinks=None, dsink_ref=None):
    low = _low_half()
    w = Q_BLOCK + 2 * half_window
    sink = None if sinks is None else _per_head(sinks)

    def rows_of(t):
        return jnp.concatenate([t[:, 0:1], t[:, HEAD_DIM:HEAD_DIM + 1]], axis=0)

    def tile(q_rows, k_rows, edge):
        q2 = _stack_heads(qn_ref[q_rows, :], low)
        do2 = _stack_heads(do_ref[q_rows, :], low)
        k = kp_ref[k_rows, :].astype(BF16)
        v = vp_ref[k_rows, :].astype(BF16)
        lse = rows_of(lse_ref[q_rows, :])
        delta = rows_of(delta_ref[q_rows, :])
        p = jnp.exp(_dot(q2, k, NT) + bias_ref[...] + edge - lse)
        ds = p * (_dot(do2, v, NT) - delta)
        ds_ref[...] += ds
        if sink is not None:
            dsink_ref[...] += (-jnp.exp(sink - lse) * delta).reshape(2, Q_BLOCK, 1)
        dsb, pb = ds.astype(BF16), p.astype(BF16)
        dq_ref[q_rows, :] = _unstack_heads(_dot(dsb, k), low)
        dk_ref[k_rows, :] += _dot(dsb, q2, TN)
        dv_ref[k_rows, :] += _dot(pb, do2, TN)

    _tiles(d, half_window, tile)


def _norm_bwd(raw_ref, gain_ref, dn_ref, dn_offset, out_ref, scale):
    low = _low_half()

    def step(r0, dgain):
        t = raw_ref[pl.ds(r0, CHUNK), :].astype(F32)
        dn = dn_ref[pl.ds(dn_offset + r0, CHUNK), :]
        dth = dn * (gain_ref[...] * scale)
        sums = _half_sum(jnp.concatenate([t * t, dth * t], axis=0), low)
        r = lax.rsqrt(sums[:CHUNK] * (1.0 / HEAD_DIM) + EPS)
        th = t * r
        out_ref[pl.ds(r0, CHUNK), :] = (r * (dth - th * (r * sums[CHUNK:] * (1.0 / HEAD_DIM)))).astype(BF16)
        return dgain + jnp.sum(dn * th, axis=0, keepdims=True) * scale

    return _chunks(step, jnp.zeros((1, LANES), F32))


def _rows8(v):
    return jnp.broadcast_to(v, (8, v.shape[-1]))


A_W = Q_BLOCK + 2 * A_HALF_WINDOW
A_PAD = A_HALF_WINDOW


def _seq_block(col_fn):
    return pl.BlockSpec((SEQ, LANES), col_fn)


def _attn_a_fwd(qkv, gain_q, gain_k, bias, sink):
    def body(sink_ref, q_ref, k_ref, v_ref, gq_ref, gk_ref, line_ref, o_ref, lse_ref, qn_ref, kp_ref, vp_ref,
             bias_ref):
        hp = pl.program_id(0)
        keep = (lax.broadcasted_iota(jnp.int32, (1, LANES), 1) // HEAD_DIM) == hp // 2
        _prep_q(q_ref, gq_ref, qn_ref)
        _prep_kv(k_ref, v_ref, gk_ref, kp_ref, vp_ref, A_PAD, keep)
        _unroll_bias(line_ref, bias_ref, A_W)

        def emit(rows, out, lse):
            o_ref[rows, :] = out
            lse_ref[rows, :] = lse

        _fwd_tiles(qn_ref, kp_ref, vp_ref, bias_ref, emit, d=1, half_window=A_HALF_WINDOW,
                   sinks=(sink_ref[2 * hp], sink_ref[2 * hp + 1]))

    vec = pl.BlockSpec((1, LANES), lambda hp, s: (0, 0))
    return pl.pallas_call(
        body, name="attn_a_fwd",
        grid_spec=pltpu.PrefetchScalarGridSpec(
            num_scalar_prefetch=1, grid=(4,),
            in_specs=[_seq_block(lambda hp, s: (0, QA_BLK + hp)), _seq_block(lambda hp, s: (0, KA_BLK)),
                      _seq_block(lambda hp, s: (0, VA_BLK)), vec, vec,
                      pl.BlockSpec((None, 2, _line_width(A_HALF_WINDOW)), lambda hp, s: (hp, 0, 0))],
            out_specs=[_seq_block(lambda hp, s: (0, hp)), _seq_block(lambda hp, s: (0, hp))],
            scratch_shapes=[pltpu.VMEM((SEQ, LANES), F32), pltpu.VMEM((SEQ + 2 * A_PAD, LANES), F32),
                            pltpu.VMEM((SEQ + 2 * A_PAD, LANES), F32), pltpu.VMEM((2 * Q_BLOCK, A_W), F32)]),
        out_shape=[jax.ShapeDtypeStruct((SEQ, 512), F32)] * 2,
        compiler_params=_params("arbitrary"),
    )(sink.reshape(8), qkv, qkv, qkv, gain_q, gain_k, bias)


def _attn_a_bwd(qkv, gain_q, gain_k, bias, sink, delta, lse, d_out):
    def body(sink_ref, q_ref, k_ref, v_ref, gq_ref, gk_ref, line_ref, delta_ref, lse_ref, do_ref,
             dq_out, dkv_out, dgq_out, dgk_out, dline_out, dsink_out,
             qn_ref, kp_ref, vp_ref, dq_ref, dk_ref, dv_ref, dk_tot, dv_tot, bias_ref, ds_out):
        hp = pl.program_id(0)
        kv_head = hp // 2
        keep = (lax.broadcasted_iota(jnp.int32, (1, LANES), 1) // HEAD_DIM) == kv_head
        _prep_q(q_ref, gq_ref, qn_ref)
        _prep_kv(k_ref, v_ref, gk_ref, kp_ref, vp_ref, A_PAD, keep)
        _unroll_bias(line_ref, bias_ref, A_W)
        dk_ref[...] = jnp.zeros_like(dk_ref)
        dv_ref[...] = jnp.zeros_like(dv_ref)
        ds_out[...] = jnp.zeros_like(ds_out)
        dsink_out[...] = jnp.zeros_like(dsink_out)

        @pl.when(hp == 0)
        def _():
            dk_tot[...] = jnp.zeros_like(dk_tot)
            dv_tot[...] = jnp.zeros_like(dv_tot)

        _bwd_tiles(qn_ref, kp_ref, vp_ref, bias_ref, do_ref, lse_ref, delta_ref, dq_ref, dk_ref, dv_ref, ds_out,
                   d=1, half_window=A_HALF_WINDOW, sinks=(sink_ref[2 * hp], sink_ref[2 * hp + 1]),
                   dsink_ref=dsink_out)
        _fold_bias_grad(ds_out, dline_out, A_W)
        dgq_out[...] = _rows8(_norm_bwd(q_ref, gq_ref, dq_ref, 0, dq_out, SCALE))

        def fold(r0, carry):
            rows = pl.ds(A_PAD + r0, CHUNK)
            for acc, tot in ((dk_ref, dk_tot), (dv_ref, dv_tot)):
                t = acc[rows, :]
                tot[pl.ds(r0, CHUNK), :] += jnp.where(keep, t + pltpu.roll(t, HEAD_DIM, 1), 0.0)
            return carry

        _chunks(fold)

        @pl.when(hp == 3)
        def _():
            dgk_out[...] = _rows8(_norm_bwd(k_ref, gk_ref, dk_tot, 0, dkv_out.at[0], 1.0))
            dkv_out[1] = dv_tot[...].astype(BF16)

    vec = pl.BlockSpec((1, LANES), lambda hp, s: (0, 0))
    seq_f32 = pltpu.VMEM((SEQ, LANES), F32)
    padded = pltpu.VMEM((SEQ + 2 * A_PAD, LANES), F32)
    return pl.pallas_call(
        body, name="attn_a_bwd",
        grid_spec=pltpu.PrefetchScalarGridSpec(
            num_scalar_prefetch=1, grid=(4,),
            in_specs=[_seq_block(lambda hp, s: (0, QA_BLK + hp)), _seq_block(lambda hp, s: (0, KA_BLK)),
                      _seq_block(lambda hp, s: (0, VA_BLK)), vec, vec,
                      pl.BlockSpec((None, 2, _line_width(A_HALF_WINDOW)), lambda hp, s: (hp, 0, 0)),
                      _seq_block(lambda hp, s: (0, hp)), _seq_block(lambda hp, s: (0, hp)),
                      _seq_block(lambda hp, s: (0, hp))],
            out_specs=[pl.BlockSpec((None, SEQ, LANES), lambda hp, s: (hp, 0, 0)),
                       pl.BlockSpec((2, SEQ, LANES), lambda hp, s: (0, 0, 0)),
                       pl.BlockSpec((None, 8, LANES), lambda hp, s: (hp, 0, 0)),
                       pl.BlockSpec((8, LANES), lambda hp, s: (0, 0)),
                       pl.BlockSpec((None, 2, _line_width(A_HALF_WINDOW)), lambda hp, s: (hp, 0, 0)),
                       pl.BlockSpec((None, 2, Q_BLOCK, 1), lambda hp, s: (hp, 0, 0, 0))],
            scratch_shapes=[seq_f32, padded, padded, seq_f32, padded, padded, seq_f32, seq_f32,
                            pltpu.VMEM((2 * Q_BLOCK, A_W), F32), pltpu.VMEM((2 * Q_BLOCK, A_W), F32)]),
        out_shape=[jax.ShapeDtypeStruct((4, SEQ, LANES), BF16), jax.ShapeDtypeStruct((2, SEQ, LANES), BF16),
                   jax.ShapeDtypeStruct((4, 8, LANES), F32), jax.ShapeDtypeStruct((8, LANES), F32),
                   jax.ShapeDtypeStruct((4, 2, _line_width(A_HALF_WINDOW)), F32),
                   jax.ShapeDtypeStruct((4, 2, Q_BLOCK, 1), F32)],
        compiler_params=_params("arbitrary"),
    )(sink.reshape(8), qkv, qkv, qkv, gain_q, gain_k, bias, delta, lse, d_out)


B_W = Q_BLOCK + 2 * B_HALF_WINDOW
B_PAD_MAX = B_HALF_WINDOW * B_DILATIONS[-1]


def _attn_b_fwd(qkv, gain_q, gain_k, bias):
    def body(q_ref, k_ref, v_ref, gq_ref, gk_ref, line_ref, o_ref, lse_ref, qn_ref, kp_ref, vp_ref, bias_ref):
        g = pl.program_id(1)
        _prep_q(q_ref, gq_ref, qn_ref)
        _unroll_bias(line_ref, bias_ref, B_W)

        def first(rows, out, lse):
            o_ref[rows, :] = out
            lse_ref[rows, :] = lse

        def combine(rows, out, lse):
            old = lse_ref[rows, :]
            new = jnp.maximum(old, lse) + jnp.log(1.0 + jnp.exp(-jnp.abs(old - lse)))
            o_ref[rows, :] = o_ref[rows, :] * jnp.exp(old - new) + out * jnp.exp(lse - new)
            lse_ref[rows, :] = new

        for gi, d in enumerate(B_DILATIONS):
            @pl.when(g == gi)
            def _():
                _prep_kv(k_ref, v_ref, gk_ref, kp_ref, vp_ref, B_HALF_WINDOW * d)
                _fwd_tiles(qn_ref, kp_ref, vp_ref, bias_ref, first if gi == 0 else combine,
                           d=d, half_window=B_HALF_WINDOW)

    vec = pl.BlockSpec((1, LANES), lambda hp, g: (0, 0))
    padded = pltpu.VMEM((SEQ + 2 * B_PAD_MAX, LANES), F32)
    return pl.pallas_call(
        body, name="attn_b_fwd", grid=(4, 3),
        in_specs=[_seq_block(lambda hp, g: (0, QB_BLK + 4 * g + hp)), _seq_block(lambda hp, g: (0, KB_BLK + 4 * g + hp)),
                  _seq_block(lambda hp, g: (0, VB_BLK + 4 * g + hp)), vec, vec,
                  pl.BlockSpec((None, 2, _line_width(B_HALF_WINDOW)), lambda hp, g: (4 * g + hp, 0, 0))],
        out_specs=[_seq_block(lambda hp, g: (0, hp)), _seq_block(lambda hp, g: (0, hp))],
        out_shape=[jax.ShapeDtypeStruct((SEQ, 512), F32)] * 2,
        scratch_shapes=[pltpu.VMEM((SEQ, LANES), F32), padded, padded, pltpu.VMEM((2 * Q_BLOCK, B_W), F32)],
        compiler_params=_params("arbitrary", "arbitrary"),
    )(qkv, qkv, qkv, gain_q, gain_k, bias)


def _attn_b_bwd(qkv, gain_q, gain_k, bias, delta, lse, d_out):
    def body(q_ref, k_ref, v_ref, gq_ref, gk_ref, line_ref, delta_ref, lse_ref, do_ref,
             dq_out, dk_out, dv_out, dgq_out, dgk_out, dline_out,
             qn_ref, kp_ref, vp_ref, dq_ref, dk_ref, dv_ref, bias_ref, ds_out):
        g = pl.program_id(1)
        _prep_q(q_ref, gq_ref, qn_ref)
        _unroll_bias(line_ref, bias_ref, B_W)
        ds_out[...] = jnp.zeros_like(ds_out)
        for gi, d in enumerate(B_DILATIONS):
            @pl.when(g == gi)
            def _():
                pad = B_HALF_WINDOW * d
                for acc in (dk_ref, dv_ref):
                    acc[pl.ds(0, SEQ + 2 * pad), :] = jnp.zeros((SEQ + 2 * pad, LANES), F32)
                _prep_kv(k_ref, v_ref, gk_ref, kp_ref, vp_ref, pad)
                _bwd_tiles(qn_ref, kp_ref, vp_ref, bias_ref, do_ref, lse_ref, delta_ref, dq_ref, dk_ref, dv_ref,
                           ds_out, d=d, half_window=B_HALF_WINDOW)
                dgk_out[...] = _rows8(_norm_bwd(k_ref, gk_ref, dk_ref, pad, dk_out, 1.0))
                dv_out[...] = dv_ref[pl.ds(pad, SEQ), :].astype(BF16)
        _fold_bias_grad(ds_out, dline_out, B_W)
        dgq_out[...] = _rows8(_norm_bwd(q_ref, gq_ref, dq_ref, 0, dq_out, SCALE))

    vec = pl.BlockSpec((1, LANES), lambda hp, g: (0, 0))
    seq_f32 = pltpu.VMEM((SEQ, LANES), F32)
    padded = pltpu.VMEM((SEQ + 2 * B_PAD_MAX, LANES), F32)
    part = pl.BlockSpec((None, 8, LANES), lambda hp, g: (4 * g + hp, 0, 0))
    line = pl.BlockSpec((None, 2, _line_width(B_HALF_WINDOW)), lambda hp, g: (4 * g + hp, 0, 0))
    return pl.pallas_call(
        body, name="attn_b_bwd", grid=(4, 3),
        in_specs=[_seq_block(lambda hp, g: (0, QB_BLK + 4 * g + hp)), _seq_block(lambda hp, g: (0, KB_BLK + 4 * g + hp)),
                  _seq_block(lambda hp, g: (0, VB_BLK + 4 * g + hp)), vec, vec,
                  line,
                  _seq_block(lambda hp, g: (0, hp)), _seq_block(lambda hp, g: (0, hp)), _seq_block(lambda hp, g: (0, hp))],
        out_specs=[pl.BlockSpec((None, SEQ, LANES), lambda hp, g: (4 * g + hp, 0, 0))] * 3 + [part, part, line],
        out_shape=[jax.ShapeDtypeStruct((12, SEQ, LANES), BF16)] * 3
        + [jax.ShapeDtypeStruct((12, 8, LANES), F32)] * 2
        + [jax.ShapeDtypeStruct((12, 2, _line_width(B_HALF_WINDOW)), F32)],
        scratch_shapes=[seq_f32, padded, padded, seq_f32, padded, padded,
                        pltpu.VMEM((2 * Q_BLOCK, B_W), F32), pltpu.VMEM((2 * Q_BLOCK, B_W), F32)],
        compiler_params=_params("arbitrary", "arbitrary"),
    )(qkv, qkv, qkv, gain_q, gain_k, bias, delta, lse, d_out)


def _sigmoid(t):
    return 1.0 / (1.0 + jnp.exp(-t))


def _middle(out_a, out_b, gates, x, target, w_a, w_b, w_out, b_merge):
    tm = 256
    n_steps = SEQ // tm

    def body(oa_ref, ob_ref, g_ref, x_ref, t_ref, wa_ref, wb_ref, wo_ref, bm_ref,
             dy_ref, dg_ref, doa_ref, dob_ref, dla_ref, dlb_ref, dwa_ref, dwb_ref, dwo_ref, dbm_ref, sq_ref):
        @pl.when(pl.program_id(0) == 0)
        def _():
            for ref in (dwa_ref, dwb_ref, dwo_ref, dbm_ref, sq_ref):
                ref[...] = jnp.zeros_like(ref)

        gate_a, gate_b = g_ref[:, 0:512], g_ref[:, 512:1024]
        sig_a, sig_b = _sigmoid(gate_a), _sigmoid(gate_b)
        silu_a, silu_b = gate_a * sig_a, gate_b * sig_b
        oa, ob = oa_ref[...], ob_ref[...]
        ya, yb = (oa * silu_a).astype(BF16), (ob * silu_b).astype(BF16)
        br_a, br_b = _dot(ya, wa_ref[...]), _dot(yb, wb_ref[...])
        m0 = _sigmoid(g_ref[:, 1024:2048] + bm_ref[0:1, :])
        m1 = _sigmoid(g_ref[:, 2048:3072] + bm_ref[1:2, :])
        merged = (m0 * br_a + m1 * br_b).astype(BF16)
        err = (x_ref[...] + _dot(merged, wo_ref[...])) - t_ref[...]
        sq_ref[...] += jnp.sum(err * err, axis=0, keepdims=True)

        dy = err * (1.0 / D_MODEL)
        dy_ref[...] = dy
        dyb = dy.astype(BF16)
        dmerged = _dot(dyb, wo_ref[...], NT)
        dwo_ref[...] += _dot(merged, dyb, TN)
        dbr_a, dbr_b = (dmerged * m0).astype(BF16), (dmerged * m1).astype(BF16)
        dm0 = (dmerged * br_a) * (m0 * (1.0 - m0))
        dm1 = (dmerged * br_b) * (m1 * (1.0 - m1))
        dbm_ref[0:1, :] += jnp.sum(dm0, axis=0, keepdims=True)
        dbm_ref[1:2, :] += jnp.sum(dm1, axis=0, keepdims=True)
        for s in range(N_CHIPS):
            cols = slice(256 * s, 256 * (s + 1))
            dwa_ref[s] += _dot(ya, dbr_a[:, cols], TN)
            dwb_ref[s] += _dot(yb, dbr_b[:, cols], TN)
        dya, dyb_ = _dot(dbr_a, wa_ref[...], NT), _dot(dbr_b, wb_ref[...], NT)
        doa, dob = dya * silu_a, dyb_ * silu_b
        doa_ref[...] = doa
        dob_ref[...] = dob
        for blk in range(512 // LANES):
            lanes = slice(blk * LANES, (blk + 1) * LANES)
            dla_ref[:, lanes] = _half_sum(doa[:, lanes] * oa[:, lanes], None)
            dlb_ref[:, lanes] = _half_sum(dob[:, lanes] * ob[:, lanes], None)
        d_gates = (((dya * oa) * (sig_a * (1.0 + gate_a * (1.0 - sig_a)))).astype(BF16),
                   ((dyb_ * ob) * (sig_b * (1.0 + gate_b * (1.0 - sig_b)))).astype(BF16),
                   dm0.astype(BF16), dm1.astype(BF16))
        blk = 0
        for part in d_gates:
            for c0 in range(0, part.shape[1], 256):
                dg_ref[blk] = part[:, c0:c0 + 256]
                blk += 1

    def rows(width):
        return pl.BlockSpec((tm, width), lambda i: (i, 0))

    def whole(*shape):
        return pl.BlockSpec(shape, lambda i: (0,) * len(shape))

    return pl.pallas_call(
        body, name="middle", grid=(n_steps,),
        in_specs=[rows(512), rows(512), rows(GATE_WIDTH), rows(D_MODEL), rows(D_MODEL),
                  whole(512, D_MODEL), whole(512, D_MODEL), whole(D_MODEL, D_MODEL), whole(2, D_MODEL)],
        out_specs=[rows(D_MODEL), pl.BlockSpec((GATE_WIDTH // 256, tm, 256), lambda i: (0, i, 0)),
                   rows(512), rows(512), rows(512), rows(512),
                   whole(N_CHIPS, 512, 256), whole(N_CHIPS, 512, 256), whole(D_MODEL, D_MODEL),
                   whole(2, D_MODEL), whole(1, D_MODEL)],
        out_shape=[jax.ShapeDtypeStruct((SEQ, D_MODEL), F32), jax.ShapeDtypeStruct((GATE_WIDTH // 256, SEQ, 256), BF16),
                   jax.ShapeDtypeStruct((SEQ, 512), F32), jax.ShapeDtypeStruct((SEQ, 512), F32),
                   jax.ShapeDtypeStruct((SEQ, 512), F32), jax.ShapeDtypeStruct((SEQ, 512), F32),
                   jax.ShapeDtypeStruct((N_CHIPS, 512, 256), F32), jax.ShapeDtypeStruct((N_CHIPS, 512, 256), F32),
                   jax.ShapeDtypeStruct((D_MODEL, D_MODEL), F32), jax.ShapeDtypeStruct((2, D_MODEL), F32),
                   jax.ShapeDtypeStruct((1, D_MODEL), F32)],
        compiler_params=_params("arbitrary"),
    )(out_a, out_b, gates, x, target, w_a, w_b, w_out, b_merge)


def _which(j, edges, fns):
    lo = 0
    for hi, fn in zip(edges, fns):
        pl.when((j >= lo) & (j < hi))(fn)
        lo = hi


def _d_w_in(d_proj, h):
    plan, step, width = [], 0, 0
    for p in d_proj:
        total = p.shape[0] * p.shape[2]
        if width + total <= W_BLOCK:
            plan.append((p.shape[0], step, 1))
            width += total
            if width == W_BLOCK:
                step, width = step + 1, 0
        else:
            assert width == 0 and total % W_BLOCK == 0
            plan.append((W_BLOCK // p.shape[2], step, total // W_BLOCK))
            step += total // W_BLOCK
    assert width == 0 and step == IN_WIDTH // W_BLOCK
    firsts = sorted({first for _, first, _ in plan})
    edges = firsts[1:] + [step]
    halves = 2

    def body(*refs):
        pieces, h_ref, o_ref, acc_ref = refs[:-3], refs[-3], refs[-2], refs[-1]
        k = pl.program_id(1)

        def emit(group):
            def fn():
                cols = jnp.concatenate([ref[b] for ref in group for b in range(ref.shape[0])], axis=1)
                term = _dot(cols, h_ref[...], TN)

                @pl.when(k == 0)
                def _():
                    acc_ref[...] = term

                @pl.when(k == halves - 1)
                def _():
                    o_ref[...] = (acc_ref[...] + term).astype(BF16)
            return fn

        groups = [[ref for ref, (_, first, _) in zip(pieces, plan) if first == f] for f in firsts]
        _which(pl.program_id(0), edges, [emit(group) for group in groups])

    def cols_spec(piece, n, first, steps):
        def index(j, k):
            return jnp.clip(j - first, 0, steps - 1), jnp.where((j >= first) & (j < first + steps), k, 0), 0
        return pl.BlockSpec((n, SEQ // halves, piece.shape[2]), index)

    return pl.pallas_call(
        body, name="d_w_in", grid=(step, halves),
        in_specs=[cols_spec(p, *pl_) for p, pl_ in zip(d_proj, plan)]
        + [pl.BlockSpec((SEQ // halves, D_MODEL), lambda j, k: (k, 0))],
        out_specs=pl.BlockSpec((W_BLOCK, D_MODEL), lambda j, k: (j, 0)),
        out_shape=jax.ShapeDtypeStruct((IN_WIDTH, D_MODEL), BF16),
        scratch_shapes=[pltpu.VMEM((W_BLOCK, D_MODEL), F32)],
        compiler_params=_params("arbitrary", "arbitrary"),
    )(*d_proj, h)


def _d_x(d_proj, w_t, x, gain, dy, chip_sums):
    tm = 256
    n_steps = SEQ // tm
    n_w = IN_WIDTH // W_BLOCK
    n_p, n_s = len(d_proj), len(chip_sums)

    def body(*refs):
        pieces, w_refs = refs[:n_p], refs[n_p:n_p + n_w]
        x_ref, g_ref, dy_ref = refs[n_p + n_w:n_p + n_w + 3]
        q_refs = refs[n_p + n_w + 3:n_p + n_w + 3 + n_s]
        dx_ref, dgain_ref = refs[n_p + n_w + 3 + n_s:n_p + n_w + 5 + n_s]
        o_refs = refs[n_p + n_w + 5 + n_s:n_p + n_w + 5 + 2 * n_s]
        send_sems, recv_sems = refs[n_p + n_w + 5 + 2 * n_s:] if n_s else (None, None)

        @pl.when(pl.program_id(0) == 0)
        def _():
            dgain_ref[...] = jnp.zeros_like(dgain_ref)
            if n_s:
                for cp in _scatter_copies(q_refs, o_refs, send_sems, recv_sems):
                    cp.start()

        blocks = [(piece, k) for piece in pieces for k in range(piece.shape[0])]
        dh, group, width, blk = None, [], 0, 0
        for piece, k in blocks:
            group.append(piece[k])
            width += piece.shape[2]
            if width == W_BLOCK:
                term = _dot(jnp.concatenate(group, axis=1), w_refs[blk][...])
                dh = term if dh is None else dh + term
                group, width, blk = [], 0, blk + 1
        assert not group and blk == n_w
        xf = x_ref[...]
        r = lax.rsqrt(jnp.mean(xf * xf, axis=-1, keepdims=True) + EPS)
        xh = xf * r
        dxh = dh * g_ref[...]
        dx_ref[...] = r * (dxh - xh * jnp.mean(dxh * xh, axis=-1, keepdims=True)) + dy_ref[...]
        dgain_ref[...] += _rows8(jnp.sum(dh * xh, axis=0, keepdims=True))

        if n_s:
            @pl.when(pl.program_id(0) == n_steps - 1)
            def _():
                for cp in _scatter_copies(q_refs, o_refs, send_sems, recv_sems):
                    cp.wait()

    row = pl.BlockSpec((tm, D_MODEL), lambda i: (i, 0))
    res = pl.pallas_call(
        body, name="d_x", grid=(n_steps,),
        in_specs=[pl.BlockSpec((p.shape[0], tm, p.shape[2]), lambda i: (0, i, 0)) for p in d_proj] + _w_blocks(0, n_w)
        + [row, pl.BlockSpec((1, D_MODEL), lambda i: (0, 0)), row] + [ANY] * n_s,
        out_specs=[row, pl.BlockSpec((8, D_MODEL), lambda i: (0, 0))] + [ANY] * n_s,
        out_shape=[jax.ShapeDtypeStruct((SEQ, D_MODEL), F32), jax.ShapeDtypeStruct((8, D_MODEL), F32)]
        + [jax.ShapeDtypeStruct((3,) + q.shape[1:], BF16) for q in chip_sums],
        scratch_shapes=[pltpu.SemaphoreType.DMA((3 * n_s,)), pltpu.SemaphoreType.DMA((3 * n_s,))] if n_s else [],
        compiler_params=_params("arbitrary"),
    )(*d_proj, *([w_t] * n_w), x, gain, dy, *chip_sums)
    return res[0], res[1], res[2:]


def _my_place():
    x, y, c = lax.axis_index("x"), lax.axis_index("y"), lax.axis_index("c")
    return jnp.stack([2 * x + y, c]).astype(jnp.int32)


def _half_rows(ref, half):
    rows = ref.shape[-2] // 2
    idx = (slice(None),) * (len(ref.shape) - 2) + (pl.ds(pl.multiple_of(half * rows, 16), rows), slice(None))
    return ref.at[idx]


def _swap_halves(grads):
    n = len(grads)

    def body(*refs):
        g_refs, o_refs, (send_sems, recv_sems) = refs[:n], refs[n:2 * n], refs[2 * n:]
        x, y, c = lax.axis_index("x"), lax.axis_index("y"), lax.axis_index("c")
        copies = [pltpu.make_async_remote_copy(src_ref=_half_rows(g, 1 - c), dst_ref=o, send_sem=send_sems.at[k],
                                               recv_sem=recv_sems.at[k], device_id=(x, y, 1 - c), device_id_type=MESH)
                  for k, (g, o) in enumerate(zip(g_refs, o_refs))]
        for cp in copies:
            cp.start()
        for cp in copies:
            cp.wait()

    return pl.pallas_call(
        body, name="reduce_swap_halves", in_specs=[ANY] * n, out_specs=[ANY] * n,
        out_shape=[jax.ShapeDtypeStruct((N_CHIPS, g.shape[1] // 2, D_MODEL), g.dtype) for g in grads],
        scratch_shapes=[pltpu.SemaphoreType.DMA((n,)), pltpu.SemaphoreType.DMA((n,))],
    )(*grads)


def _row_tile(rows):
    return max(t for t in range(16, 385, 16) if rows % t == 0)


def _add_halves(place, grads, theirs, name):
    half = theirs.shape[1]
    tr = _row_tile(half)
    n = half // tr

    def body(place_ref, g_ref, t_ref, o_ref):
        o_ref[...] = (g_ref[...].astype(F32) + t_ref[...].astype(F32)).astype(BF16)

    return pl.pallas_call(
        body, name=name,
        grid_spec=pltpu.PrefetchScalarGridSpec(
            num_scalar_prefetch=1, grid=(N_CHIPS, n),
            in_specs=[pl.BlockSpec((None, tr, D_MODEL), lambda s, i, p: (s, p[1] * n + i, 0)),
                      pl.BlockSpec((None, tr, D_MODEL), lambda s, i, p: (s, i, 0))],
            out_specs=pl.BlockSpec((None, tr, D_MODEL), lambda s, i, p: (s, i, 0))),
        out_shape=jax.ShapeDtypeStruct((N_CHIPS, half, D_MODEL), BF16),
        compiler_params=_params("arbitrary", "arbitrary"),
    )(place, grads, theirs)


def _scatter_copies(q_refs, o_refs, send_sems, recv_sems):
    x, y, c = lax.axis_index("x"), lax.axis_index("y"), lax.axis_index("c")
    chips = [(1 - x, y), (x, 1 - y), (1 - x, 1 - y)]
    return [pltpu.make_async_remote_copy(src_ref=q.at[2 * cx + cy], dst_ref=o.at[j],
                                         send_sem=send_sems.at[3 * k + j], recv_sem=recv_sems.at[3 * k + j],
                                         device_id=(cx, cy, c), device_id_type=MESH)
            for k, (q, o) in enumerate(zip(q_refs, o_refs)) for j, (cx, cy) in enumerate(chips)]


def _add_chips(place, chip_sums, others, name):
    half = others.shape[1]
    tr = _row_tile(half)
    n = half // tr

    def body(place_ref, q_ref, o_ref, r_ref):
        acc = q_ref[...].astype(F32)
        for j in range(3):
            acc = acc + o_ref[j].astype(F32)
        r_ref[...] = acc

    return pl.pallas_call(
        body, name=name,
        grid_spec=pltpu.PrefetchScalarGridSpec(
            num_scalar_prefetch=1, grid=(n,),
            in_specs=[pl.BlockSpec((None, tr, D_MODEL), lambda i, p: (p[0], i, 0)),
                      pl.BlockSpec((3, tr, D_MODEL), lambda i, p: (0, i, 0))],
            out_specs=pl.BlockSpec((tr, D_MODEL), lambda i, p: (p[1] * n + i, 0))),
        out_shape=jax.ShapeDtypeStruct((2 * half, D_MODEL), F32),
        compiler_params=_params("arbitrary"),
    )(place, chip_sums, others)


def _join_halves(shards, block):
    n = len(shards)
    rows = block.shape[0]

    def body(*refs):
        b_ref, o_refs, sum_ref = refs[n], refs[n + 1:2 * n + 1], refs[2 * n + 1]
        send_sems, recv_sems, small_send, small_recv, local_sem, all_ref = refs[2 * n + 2:]
        x, y, c = lax.axis_index("x"), lax.axis_index("y"), lax.axis_index("c")
        me, sibling = (x, y, c), (x, y, 1 - c)
        chips = [(1 - x, y), (x, 1 - y), (1 - x, 1 - y)]

        def half(k, rows_ref):
            return pltpu.make_async_remote_copy(src_ref=rows_ref, dst_ref=rows_ref, send_sem=send_sems.at[k],
                                                recv_sem=recv_sems.at[k], device_id=sibling, device_id_type=MESH)

        def at(px, py, pc):
            return all_ref.at[pl.ds(pl.multiple_of((4 * px + 2 * py + pc) * rows, 8), rows), :]

        def small(k, block_of, to, src=None):
            return pltpu.make_async_remote_copy(src_ref=at(*block_of) if src is None else src, dst_ref=at(*block_of),
                                                send_sem=small_send.at[k], recv_sem=small_recv.at[k],
                                                device_id=to, device_id_type=MESH)

        sends = [half(k, _half_rows(o, c)) for k, o in enumerate(o_refs)]
        for cp in sends:
            cp.start()
        mine = pltpu.make_async_copy(b_ref, at(*me), local_sem)
        mine.start()
        first = [small(0, me, sibling, src=b_ref)]
        first += [small(1 + j, me, (*chip, c), src=b_ref) for j, chip in enumerate(chips)]
        for cp in first:
            cp.start()
        passed = [small(4 + j, (*chip, c), sibling) for j, chip in enumerate(chips)]
        for j, chip in enumerate(chips):
            small(1 + j, (*chip, c), me).wait_recv()
            passed[j].start()
        small(0, sibling, me).wait_recv()
        for j, chip in enumerate(chips):
            small(4 + j, (*chip, 1 - c), me).wait_recv()
        mine.wait()
        acc = all_ref[0:rows, :]
        for dev in range(1, 8):
            acc = acc + all_ref[rows * dev:rows * (dev + 1), :]
        sum_ref[...] = acc
        for k, o in enumerate(o_refs):
            half(k, _half_rows(o, 1 - c)).wait_recv()
        for cp in sends + first + passed:
            cp.wait_send()

    res = pl.pallas_call(
        body, name="reduce_join_halves", in_specs=[ANY] * n + [pl.BlockSpec(memory_space=pltpu.VMEM)],
        out_specs=[ANY] * n + [pl.BlockSpec(memory_space=pltpu.VMEM)],
        out_shape=[jax.ShapeDtypeStruct(s.shape, F32) for s in shards] + [jax.ShapeDtypeStruct(block.shape, F32)],
        input_output_aliases={k: k for k in range(n)},
        scratch_shapes=[pltpu.SemaphoreType.DMA((n,)), pltpu.SemaphoreType.DMA((n,)),
                        pltpu.SemaphoreType.DMA((7,)), pltpu.SemaphoreType.DMA((7,)), pltpu.SemaphoreType.DMA,
                        pltpu.VMEM((8 * rows, D_MODEL), F32)],
    )(*shards, block)
    return res[:n], res[n]


def _adamw_math(w, g, m, v):
    m = ADAM_B1 * m + (1.0 - ADAM_B1) * g
    v = ADAM_B2 * v + (1.0 - ADAM_B2) * (g * g)
    m_hat = m / (1.0 - ADAM_B1 ** ADAM_STEP)
    v_hat = v / (1.0 - ADAM_B2 ** ADAM_STEP)
    return -ADAM_LR * (m_hat / (jnp.sqrt(v_hat) + ADAM_EPS) + ADAM_WD * w), m, v


def _adamw(w, g, m, v, name):
    r, c = w.shape
    tr = 128 if r % 128 == 0 else r

    def body(w_ref, g_ref, m_ref, v_ref, d_ref, nm_ref, nv_ref):
        d_ref[...], nm_ref[...], nv_ref[...] = _adamw_math(w_ref[...], g_ref[...], m_ref[...], v_ref[...])

    spec = pl.BlockSpec((tr, c), lambda i: (i, 0))
    return pl.pallas_call(
        body, name=name, grid=(r // tr,), in_specs=[spec] * 4, out_specs=[spec] * 3,
        out_shape=[jax.ShapeDtypeStruct((r, c), F32)] * 3, compiler_params=_params("arbitrary"),
    )(w, g, m, v)


def _adamw_small(ws, gs, ms, vs):
    n = len(ws)

    def body(*refs):
        ins, outs = refs[:4 * n], refs[4 * n:]
        for k in range(n):
            d, m, v = _adamw_math(ins[k][...], ins[n + k][...], ins[2 * n + k][...], ins[3 * n + k][...])
            outs[k][...], outs[n + k][...], outs[2 * n + k][...] = d, m, v

    shapes = [jax.ShapeDtypeStruct(w.shape, F32) for w in ws]
    res = pl.pallas_call(body, name="adamw_small", out_shape=shapes * 3)(*ws, *gs, *ms, *vs)
    return res[:n], res[n:2 * n], res[2 * n:]


def _fold_heads(partials):
    t = jnp.sum(partials[:, 0, :], axis=0)
    return (t[:HEAD_DIM] + t[HEAD_DIM:]).reshape(1, HEAD_DIM)


def _local_step(x, target, norm_gain, w_t, w_a, w_b, w_o, b_m, q_norm_a, k_norm_a, q_norm_b, k_norm_b, sink_a,
                rel_bias, start_reduce=None):
    two = lambda gain: jnp.concatenate([gain, gain], axis=1)
    bias_a = _bias_lines(rel_bias[:, :8], A_HALF_WINDOW, 1)
    bias_b = jnp.concatenate([_bias_lines(rel_bias[:, 8 + 8 * g:16 + 8 * g], B_HALF_WINDOW, d)
                              for g, d in enumerate(B_DILATIONS)], axis=0)

    qkv, h = _in_proj(x, norm_gain, w_t, 0, QKV_WIDTH // W_BLOCK, BF16, "in_proj_qkv", True)
    gates, = _in_proj(x, norm_gain, w_t, QKV_WIDTH // W_BLOCK, GATE_WIDTH // W_BLOCK, F32, "in_proj_gates", False)
    out_a, lse_a = _attn_a_fwd(qkv, two(q_norm_a), two(k_norm_a), bias_a, sink_a)
    out_b, lse_b = _attn_b_fwd(qkv, two(q_norm_b), two(k_norm_b), bias_b)

    dy, dgates, d_out_a, d_out_b, delta_a, delta_b, d_wa, d_wb, d_wo, d_bm, sq = _middle(
        out_a, out_b, gates, x, target, w_a, w_b, w_o, b_m)
    loss = (0.5 / D_MODEL) * jnp.sum(sq)

    dq_a, dkv_a, dgq_a, dgk_a, ds_a, dsink = _attn_a_bwd(
        qkv, two(q_norm_a), two(k_norm_a), bias_a, sink_a, delta_a, lse_a, d_out_a)
    dq_b, dk_b, dv_b, dgq_b, dgk_b, ds_b = _attn_b_bwd(
        qkv, two(q_norm_b), two(k_norm_b), bias_b, delta_b, lse_b, d_out_b)
    d_proj = (dq_a, dkv_a, dq_b, dk_b, dv_b, dgates)

    d_bm_rows = jnp.pad(d_bm.reshape(2, N_CHIPS, 256).transpose(1, 0, 2),
                        ((0, 0), (0, REST_ROWS - 514), (0, D_MODEL - 256)))
    rest = jnp.concatenate([d_wo.reshape(N_CHIPS, 256, D_MODEL), d_wa.reshape(N_CHIPS, 128, D_MODEL),
                            d_wb.reshape(N_CHIPS, 128, D_MODEL), d_bm_rows], axis=1)
    grads = [_d_w_in(d_proj, h).reshape(N_CHIPS, W_IN_SHARD, D_MODEL), rest]
    narrow = [grads[0], rest.astype(BF16)]
    chip_sums = start_reduce(grads, narrow) if start_reduce is not None else []
    grad_x, d_gain, others = _d_x(d_proj, w_t, x, norm_gain, dy, chip_sums)

    d_rel = jnp.concatenate(
        [_bias_grad(ds_a, A_HALF_WINDOW, 1)]
        + [_bias_grad(ds_b[4 * g:4 * g + 4], B_HALF_WINDOW, d) for g, d in enumerate(B_DILATIONS)], axis=1)
    d_sink = jnp.sum(dsink, axis=(2, 3)).reshape(1, 8)
    dgk_a_row = dgk_a[0]
    small = jnp.zeros((8, D_MODEL), F32)
    small = small.at[0].set(d_gain[0])
    small = small.at[1].set(d_rel.reshape(-1))
    misc = jnp.concatenate([_fold_heads(dgq_a), (dgk_a_row[:HEAD_DIM] + dgk_a_row[HEAD_DIM:]).reshape(1, HEAD_DIM),
                            _fold_heads(dgq_b), _fold_heads(dgk_b), d_sink], axis=1)
    small = small.at[2, :264].set(misc[0])

    return loss, grad_x, grads, small, chip_sums, others


def _unpack_weights(w_t_all, small_all):
    sm = small_all.reshape(N_CHIPS, SMALL_ROWS, D_MODEL)
    w_o = sm[:, 0:256].reshape(D_MODEL, D_MODEL)
    w_a = sm[:, 256:384].reshape(N_CHIPS, 512, 256).transpose(1, 0, 2).reshape(512, D_MODEL)
    w_b = sm[:, 384:512].reshape(N_CHIPS, 512, 256).transpose(1, 0, 2).reshape(512, D_MODEL)
    b_m = lax.bitcast_convert_type(sm[:, 512].reshape(N_CHIPS, 2, 256, 2), F32)
    return w_t_all, w_a, w_b, w_o, b_m.transpose(1, 0, 2).reshape(2, D_MODEL)


def _pack_small_weights(w_branch_a, w_branch_b, b_merge, w_out):
    b_m = jnp.pad(lax.bitcast_convert_type(b_merge, BF16).reshape(1, D_MODEL), ((0, SMALL_ROWS - 513), (0, 0)))
    return jnp.concatenate([w_out.astype(BF16), w_branch_a.astype(BF16).reshape(128, D_MODEL),
                            w_branch_b.astype(BF16).reshape(128, D_MODEL), b_m], axis=0)


def kernel(x, norm_gain, w_in, q_norm_a, k_norm_a, q_norm_b, k_norm_b, sink_a, rel_bias, w_branch_a, w_branch_b, b_merge, w_out, loss_target, m_norm_gain, m_w_in, m_q_norm_a, m_k_norm_a, m_q_norm_b, m_k_norm_b, m_sink_a, m_rel_bias, m_w_branch_a, m_w_branch_b, m_b_merge, m_w_out, v_norm_gain, v_w_in, v_q_norm_a, v_k_norm_a, v_q_norm_b, v_k_norm_b, v_sink_a, v_rel_bias, v_w_branch_a, v_w_branch_b, v_b_merge, v_w_out):
    wt_shard = _transpose_cast(w_in, BF16, "w_in_transpose")
    w_t, w_a, w_b, w_o, b_m = _unpack_weights(
        *_gather_weights(wt_shard, _pack_small_weights(w_branch_a[0], w_branch_b[0], b_merge[0], w_out[0])))

    place = _my_place()
    names = ("w_in", "rest")

    def start_reduce(grads, narrow):
        return [_add_halves(place, g, t, "reduce_add_halves_" + n) for g, t, n in zip(grads, _swap_halves(narrow), names)]

    loss_part, grad_x, _, small, chip_sums, others = _local_step(
        x[0], loss_target[0], norm_gain, w_t, w_a, w_b, w_o, b_m, q_norm_a, k_norm_a, q_norm_b, k_norm_b,
        sink_a, rel_bias, start_reduce)

    (g_wt, g_rest), small = _join_halves(
        [_add_chips(place, q, o, "reduce_add_chips_" + n) for q, o, n in zip(chip_sums, others, names)],
        small.at[3, 0].set(loss_part))
    loss = small[3, 0]

    g_w_in = _transpose_cast(g_wt, F32, "grad_w_in_transpose")
    g_w_out = g_rest[0:256]
    g_w_a = g_rest[256:384].reshape(512, 256)
    g_w_b = g_rest[384:512].reshape(512, 256)
    g_b_merge = g_rest[512:514, :256]
    g_norm_gain = small[0:1]
    g_rel_bias = small[1].reshape(N_BUCKETS, N_BUCKETS)
    g_q_a, g_k_a, g_q_b, g_k_b = (small[2:3, 64 * k:64 * k + 64] for k in range(4))
    g_sink = small[2:3, 256:264]

    big_names = (("w_in", w_in, g_w_in, m_w_in, v_w_in),
                 ("w_branch_a", w_branch_a, g_w_a, m_w_branch_a, v_w_branch_a),
                 ("w_branch_b", w_branch_b, g_w_b, m_w_branch_b, v_w_branch_b),
                 ("w_out", w_out, g_w_out, m_w_out, v_w_out))
    upd = {name: (g,) + tuple(_adamw(w[0], g, m[0], v[0], "adamw_" + name)) for name, w, g, m, v in big_names}
    small_names = ("norm_gain", "q_norm_a", "k_norm_a", "q_norm_b", "k_norm_b", "sink_a", "rel_bias", "b_merge")
    ws = [norm_gain, q_norm_a, k_norm_a, q_norm_b, k_norm_b, sink_a, rel_bias, b_merge[0]]
    gs = [g_norm_gain, g_q_a, g_k_a, g_q_b, g_k_b, g_sink, g_rel_bias, g_b_merge]
    ms = [m_norm_gain, m_q_norm_a, m_k_norm_a, m_q_norm_b, m_k_norm_b, m_sink_a, m_rel_bias, m_b_merge[0]]
    vs = [v_norm_gain, v_q_norm_a, v_k_norm_a, v_q_norm_b, v_k_norm_b, v_sink_a, v_rel_bias, v_b_merge[0]]
    ds, nms, nvs = _adamw_small(ws, gs, ms, vs)
    for k, name in enumerate(small_names):
        upd[name] = (gs[k], ds[k], nms[k], nvs[k])

    order = ("norm_gain", "w_in", "q_norm_a", "k_norm_a", "q_norm_b", "k_norm_b", "sink_a", "rel_bias",
             "w_branch_a", "w_branch_b", "b_merge", "w_out")
    lead = {"w_in", "w_branch_a", "w_branch_b", "b_merge", "w_out"}
    outs = [loss, grad_x[None]]
    for part in range(4):
        outs += [upd[name][part][None] if name in lead else upd[name][part] for name in order]
    return tuple(outs)
```

```python
import math

import numpy as np
import jax
import jax.numpy as jnp
from jax import lax
from jax.experimental import pallas as pl
from jax.experimental.pallas import tpu as pltpu

F32 = jnp.float32
BF16 = jnp.bfloat16

SEQ = 4096
D_MODEL = 1024
HEAD_DIM = 64
LANES = 128
EPS = 1e-6
NEG_INF = -1e30
SCALE = HEAD_DIM ** -0.5
N_BUCKETS = 32
MAX_DISTANCE = 1024
N_CHIPS = 4

A_HALF_WINDOW = 128
B_HALF_WINDOW = 64
B_DILATIONS = (1, 4, 16)
Q_BLOCK = 128

QKV_WIDTH = 5376
GATE_WIDTH = 3072
QA_BLK, KA_BLK, VA_BLK = 0, 4, 5
QB_BLK, KB_BLK, VB_BLK = 6, 18, 30
IN_WIDTH = QKV_WIDTH + GATE_WIDTH
W_IN_SHARD = IN_WIDTH // N_CHIPS

SMALL_ROWS = 544
REST_ROWS = 544

ADAM_LR = 0.001
ADAM_B1 = 0.9
ADAM_B2 = 0.999
ADAM_EPS = 1e-08
ADAM_WD = 0.01
ADAM_STEP = 10

VMEM_LIMIT = 56 * 1024 * 1024

NT = (((1,), (1,)), ((), ()))
TN = (((0,), (0,)), ((), ()))
MESH = pl.DeviceIdType.MESH
ANY = pl.BlockSpec(memory_space=pl.ANY)


def _dot(a, b, dims=None):
    if dims is None:
        return jnp.dot(a, b, preferred_element_type=F32)
    return lax.dot_general(a, b, dims, preferred_element_type=F32)


def _params(*semantics):
    return pltpu.CompilerParams(dimension_semantics=semantics or None, vmem_limit_bytes=VMEM_LIMIT)


def _line_width(half_window):
    return pl.cdiv(2 * Q_BLOCK + 2 * half_window - 1, LANES) * LANES


def _bucket_onehot(half_window, stride):
    rel = np.arange(_line_width(half_window)) - (Q_BLOCK - 1) - half_window
    band = np.abs(rel) <= half_window
    rel = rel * stride
    half, max_exact = N_BUCKETS // 2, N_BUCKETS // 4
    n = np.abs(rel)
    nf = np.maximum(n, max_exact).astype(np.float32)
    large = max_exact + (np.log(nf / np.float32(max_exact)) / np.float32(math.log(MAX_DISTANCE / max_exact))
                         * np.float32(half - max_exact)).astype(np.int32)
    large = np.minimum(large, half - 1)
    bucket = (rel > 0).astype(np.int32) * half + np.where(n < max_exact, n, large)
    onehot = (bucket[..., None] == np.arange(N_BUCKETS)) & band[..., None]
    return onehot.astype(np.float32), band


def _bias_lines(rel_bias_cols, half_window, stride):
    onehot, band = _bucket_onehot(half_window, stride)
    h = rel_bias_cols.shape[1]
    t = jnp.einsum("tb,bh->ht", jnp.asarray(onehot), rel_bias_cols, precision=lax.Precision.HIGHEST)
    t = t + jnp.asarray(np.where(band, 0.0, NEG_INF).astype(np.float32))
    return t.reshape(h // 2, 2, -1)


def _bias_grad(d_lines, half_window, stride):
    onehot, _ = _bucket_onehot(half_window, stride)
    h = d_lines.shape[0] * 2
    return jnp.einsum("tb,ht->bh", jnp.asarray(onehot), d_lines.reshape(h, -1), precision=lax.Precision.HIGHEST)


def _unroll_bias(line_ref, tile_ref, w):
    width = line_ref.shape[1]
    for j in range(2):
        rows = jnp.broadcast_to(line_ref[j:j + 1, :], (Q_BLOCK, width))
        rows = pltpu.roll(rows, width - (Q_BLOCK - 1), 1, stride=1, stride_axis=0)
        tile_ref[j * Q_BLOCK:(j + 1) * Q_BLOCK, :] = rows[:, :w]


def _fold_bias_grad(tile_ref, line_ref, w):
    width = line_ref.shape[1]
    row = lax.broadcasted_iota(jnp.int32, (Q_BLOCK, Q_BLOCK), 0)
    col = lax.broadcasted_iota(jnp.int32, (Q_BLOCK, Q_BLOCK), 1)
    flip = jnp.where(row + col == Q_BLOCK - 1, 1.0, 0.0).astype(BF16)
    for j in range(2):
        tile = tile_ref[j * Q_BLOCK:(j + 1) * Q_BLOCK, :]
        hi = tile.astype(BF16)
        lo = (tile - hi.astype(F32)).astype(BF16)
        rows = _dot(flip, hi) + _dot(flip, lo)
        rows = jnp.concatenate([rows, jnp.zeros((Q_BLOCK, width - w), F32)], axis=1)
        rows = pltpu.roll(rows, 0, 1, stride=1, stride_axis=0)
        line_ref[j:j + 1, :] = jnp.sum(rows, axis=0, keepdims=True)


def _row_tile(rows):
    return max(t for t in range(16, 385, 16) if rows % t == 0)


def _cast_rows(w, out_dtype, name):
    r, c = w.shape
    tr = _row_tile(r)

    def body(w_ref, o_ref):
        o_ref[...] = w_ref[...].astype(out_dtype)

    spec = pl.BlockSpec((tr, c), lambda i: (i, 0))
    return pl.pallas_call(
        body, name=name, grid=(r // tr,), in_specs=[spec], out_specs=spec,
        out_shape=jax.ShapeDtypeStruct((r, c), out_dtype), compiler_params=_params("arbitrary"),
    )(w)


def _gather_weights(wt_shard, small_shard):
    bufs = ((W_IN_SHARD, IN_WIDTH), (SMALL_ROWS, N_CHIPS * SMALL_ROWS))

    stage_rows = 528

    def body(wt_in, sm_in, wt_out, sm_out, send_sems, recv_sems, in_sems, out_sems, stage):
        x, y, c = lax.axis_index("x"), lax.axis_index("y"), lax.axis_index("c")
        sibling = (x, y, 1 - c)
        my_chip = 2 * x + y
        refs = ((wt_in, wt_out), (sm_in, sm_out))

        def keep_own():
            pieces = [(b, r0) for b in range(2) for r0 in range(0, bufs[b][0], stage_rows)]
            outs = []
            for i, (b, r0) in enumerate(pieces):
                rows = min(stage_rows, bufs[b][0] - r0)
                slot = i % 2
                if i >= 2:
                    outs[i - 2].wait()
                buf = stage.at[slot, pl.ds(0, rows), :]
                load = pltpu.make_async_copy(refs[b][0].at[pl.ds(r0, rows), :], buf, in_sems.at[slot])
                load.start()
                load.wait()
                start = pl.multiple_of(my_chip * bufs[b][0] + r0, 16)
                outs.append(pltpu.make_async_copy(buf, refs[b][1].at[pl.ds(start, rows), :], out_sems.at[slot]))
                outs[i].start()
            for cp in outs[-2:]:
                cp.wait()

        def half_of(b, chip, half):
            rows = bufs[b][0]
            start = pl.multiple_of(chip * rows + half * (rows // 2), 16)
            return refs[b][1].at[pl.ds(start, rows // 2), :]

        def copy(k, src, dst, to):
            return pltpu.make_async_remote_copy(src_ref=src, dst_ref=dst, send_sem=send_sems.at[k],
                                                recv_sem=recv_sems.at[k], device_id=to, device_id_type=MESH)

        near = (x + (1 - c) - 2 * x * (1 - c), y + c - 2 * y * c)
        far = (x + c - 2 * x * c, y + (1 - c) - 2 * y * (1 - c))
        diag = (1 - x, 1 - y)
        chip_no = lambda chip: 2 * chip[0] + chip[1]
        sends, passed = [], []
        for b in range(2):
            rows = bufs[b][0]
            src = refs[b][0].at[pl.ds(pl.multiple_of(c * (rows // 2), 16), rows // 2), :]
            sends += [copy(3 * b, src, half_of(b, my_chip, c), (*near, c)),
                      copy(3 * b + 1, src, half_of(b, my_chip, c), (*far, c))]
        for cp in sends:
            cp.start()
        keep_own()

        def pass_on(b, j, chip):
            landed = half_of(b, chip_no(chip), c)
            fwd = copy(6 + 3 * b + j, landed, landed, sibling)
            fwd.start()
            passed.append(fwd)

        for b in range(2):
            landed = half_of(b, chip_no(near), c)
            copy(3 * b, landed, landed, sibling).wait_recv()
            relay = copy(3 * b + 2, landed, landed, (*far, c))
            relay.start()
            sends.append(relay)
            pass_on(b, 0, near)
        for b in range(2):
            for j, chip in ((1, far), (2, diag)):
                landed = half_of(b, chip_no(chip), c)
                copy(3 * b + j, landed, landed, sibling).wait_recv()
                pass_on(b, j, chip)
        for b in range(2):
            for j, chip in ((0, far), (1, near), (2, diag)):
                other = half_of(b, chip_no(chip), 1 - c)
                copy(6 + 3 * b + j, other, other, sibling).wait_recv()
        for cp in sends + passed:
            cp.wait_send()

    return pl.pallas_call(
        body, name="gather_weights",
        in_specs=[ANY, ANY], out_specs=[ANY, ANY],
        out_shape=[jax.ShapeDtypeStruct((bufs[0][1], D_MODEL), BF16),
                   jax.ShapeDtypeStruct((bufs[1][1], D_MODEL), BF16)],
        scratch_shapes=[pltpu.SemaphoreType.DMA((12,)), pltpu.SemaphoreType.DMA((12,)),
                        pltpu.SemaphoreType.DMA((2,)), pltpu.SemaphoreType.DMA((2,)),
                        pltpu.VMEM((2, stage_rows, D_MODEL), BF16)],
    )(wt_shard, small_shard)


W_BLOCK = 768


def _w_blocks(first, count):
    return [pl.BlockSpec((W_BLOCK, D_MODEL), lambda *_, k=k: (first + k, 0)) for k in range(count)]


def _in_proj(x, gain, w_t, first_block, n_blocks, out_dtype, name, keep_h):
    tm = 512

    def body(x_ref, g_ref, *refs):
        w_refs, outs = refs[:n_blocks], refs[n_blocks:]
        xf = x_ref[...]
        r = lax.rsqrt(jnp.mean(xf * xf, axis=-1, keepdims=True) + EPS)
        h = ((xf * r) * g_ref[...]).astype(BF16)
        if keep_h:
            outs[1][...] = h
        for k, w_ref in enumerate(w_refs):
            outs[0][:, k * W_BLOCK:(k + 1) * W_BLOCK] = _dot(h, w_ref[...], NT).astype(out_dtype)

    return pl.pallas_call(
        body, name=name, grid=(SEQ // tm,),
        in_specs=[pl.BlockSpec((tm, D_MODEL), lambda i: (i, 0)), pl.BlockSpec((1, D_MODEL), lambda i: (0, 0))]
        + _w_blocks(first_block, n_blocks),
        out_specs=[pl.BlockSpec((tm, W_BLOCK * n_blocks), lambda i: (i, 0)),
                   pl.BlockSpec((tm, D_MODEL), lambda i: (i, 0))][:2 if keep_h else 1],
        out_shape=[jax.ShapeDtypeStruct((SEQ, W_BLOCK * n_blocks), out_dtype),
                   jax.ShapeDtypeStruct((SEQ, D_MODEL), BF16)][:2 if keep_h else 1],
        compiler_params=_params("arbitrary"),
    )(x, gain, *([w_t] * n_blocks))


CHUNK = 256
CHUNK_UNROLL = 4
TILE_UNROLL = 8


def _low_half():
    return lax.broadcasted_iota(jnp.int32, (1, LANES), 1) < HEAD_DIM


def _half_sum(v, low):
    del low
    row = lax.broadcasted_iota(jnp.int32, (2 * LANES, LANES), 0)
    col = lax.broadcasted_iota(jnp.int32, (2 * LANES, LANES), 1)
    ones = jnp.where((row % LANES) // HEAD_DIM == col // HEAD_DIM, 1.0, 0.0).astype(BF16)
    hi = v.astype(BF16)
    lo = (v - hi.astype(F32)).astype(BF16)
    return _dot(jnp.concatenate([hi, lo], axis=1), ones)


def _chunks(fn, init=0):
    def body(i, carry):
        for u in range(CHUNK_UNROLL):
            carry = fn(pl.multiple_of((i * CHUNK_UNROLL + u) * CHUNK, CHUNK), carry)
        return carry

    return lax.fori_loop(0, SEQ // (CHUNK * CHUNK_UNROLL), body, init)


def _inv_rms(t, low):
    return lax.rsqrt(_half_sum(t * t, low) * (1.0 / HEAD_DIM) + EPS)


def _prep_q(q_ref, gain_ref, qn_ref):
    low = _low_half()

    def step(r0, carry):
        q = q_ref[pl.ds(r0, CHUNK), :].astype(F32)
        qn_ref[pl.ds(r0, CHUNK), :] = ((q * _inv_rms(q, low)) * gain_ref[...]) * SCALE
        return carry

    _chunks(step)


def _own_half(t, keep):
    return jnp.where(keep, t, pltpu.roll(t, HEAD_DIM, 1))


def _prep_kv(k_ref, v_ref, gain_ref, kp_ref, vp_ref, pad, keep=None):
    low = _low_half()
    zeros = jnp.zeros((pad, LANES), F32)
    for ref in (kp_ref, vp_ref):
        ref[pl.ds(0, pad), :] = zeros
        ref[pl.ds(pad + SEQ, pad), :] = zeros

    def step(r0, carry):
        k = k_ref[pl.ds(r0, CHUNK), :].astype(F32)
        v = v_ref[pl.ds(r0, CHUNK), :].astype(F32)
        kn = (k * _inv_rms(k, low)) * gain_ref[...]
        if keep is not None:
            kn, v = _own_half(kn, keep), _own_half(v, keep)
        kp_ref[pl.ds(pad + r0, CHUNK), :] = kn
        vp_ref[pl.ds(pad + r0, CHUNK), :] = v
        return carry

    _chunks(step)


def _tiles(d, half_window, fn):
    w = Q_BLOCK + 2 * half_window
    length = SEQ // d
    n_blocks = length // Q_BLOCK
    col = lax.broadcasted_iota(jnp.int32, (1, w), 1)

    def step(it, carry):
        c, n = it // n_blocks, it % n_blocks
        start = c + (d * Q_BLOCK) * n
        if d == 1:
            start = pl.multiple_of(start, Q_BLOCK)
            q_rows, k_rows = pl.ds(start, Q_BLOCK), pl.ds(start, w)
        else:
            q_rows, k_rows = pl.ds(start, Q_BLOCK, stride=d), pl.ds(start, w, stride=d)
        t = n * Q_BLOCK - half_window + col
        edge = jnp.where((t < 0) | (t >= length), NEG_INF, 0.0)
        fn(q_rows, k_rows, edge)
        return carry

    lax.fori_loop(0, d * n_blocks, step, 0, unroll=TILE_UNROLL)


def _stack_heads(t, low):
    return jnp.concatenate([jnp.where(low, t, 0.0), jnp.where(low, 0.0, t)], axis=0).astype(BF16)


def _unstack_heads(t, low):
    return jnp.where(low, t[:Q_BLOCK], t[Q_BLOCK:])


def _per_head(pair):
    return jnp.concatenate([jnp.full((Q_BLOCK, 1), pair[0], F32), jnp.full((Q_BLOCK, 1), pair[1], F32)], axis=0)


def _fwd_tiles(qn_ref, kp_ref, vp_ref, bias_ref, emit, *, d, half_window, sinks=None):
    low = _low_half()
    w = Q_BLOCK + 2 * half_window
    sink = None if sinks is None else _per_head(sinks)

    def tile(q_rows, k_rows, edge):
        q2 = _stack_heads(qn_ref[q_rows, :], low)
        k = kp_ref[k_rows, :].astype(BF16)
        v1 = jnp.concatenate([vp_ref[k_rows, :], jnp.ones((w, LANES), F32)], axis=1).astype(BF16)
        s = _dot(q2, k, NT) + bias_ref[...] + edge
        m = jnp.max(s, axis=-1, keepdims=True)
        if sink is not None:
            m = jnp.maximum(m, sink)
        o = _dot(jnp.exp(s - m).astype(BF16), v1)
        l = o[:, LANES:]
        if sink is not None:
            l = l + jnp.exp(sink - m)
        emit(q_rows, _unstack_heads(o[:, :LANES] * (1.0 / l), low), _unstack_heads(m + jnp.log(l), low))

    _tiles(d, half_window, tile)


def _bwd_tiles(qn_ref, kp_ref, vp_ref, bias_ref, do_ref, lse_ref, delta_ref, dq_ref, dk_ref, dv_ref, ds_ref,
               *, d, half_window, sinks=None, dsink_ref=None):
    low = _low_half()
    w = Q_BLOCK + 2 * half_window
    sink = None if sinks is None else _per_head(sinks)

    def rows_of(t):
        return jnp.concatenate([t[:, 0:1], t[:, HEAD_DIM:HEAD_DIM + 1]], axis=0)

    def tile(q_rows, k_rows, edge):
        q2 = _stack_heads(qn_ref[q_rows, :], low)
        do2 = _stack_heads(do_ref[q_rows, :], low)
        k = kp_ref[k_rows, :].astype(BF16)
        v = vp_ref[k_rows, :].astype(BF16)
        lse = rows_of(lse_ref[q_rows, :])
        delta = rows_of(delta_ref[q_rows, :])
        p = jnp.exp(_dot(q2, k, NT) + bias_ref[...] + edge - lse)
        ds = p * (_dot(do2, v, NT) - delta)
        ds_ref[...] += ds
        if sink is not None:
            dsink_ref[...] += (-jnp.exp(sink - lse) * delta).reshape(2, Q_BLOCK, 1)
        dsb, pb = ds.astype(BF16), p.astype(BF16)
        dq_ref[q_rows, :] = _unstack_heads(_dot(dsb, k), low)
        dk_ref[k_rows, :] += _dot(dsb, q2, TN)
        dv_ref[k_rows, :] += _dot(pb, do2, TN)

    _tiles(d, half_window, tile)


def _norm_bwd(raw_ref, gain_ref, dn_ref, dn_offset, out_ref, scale):
    low = _low_half()

    def step(r0, dgain):
        t = raw_ref[pl.ds(r0, CHUNK), :].astype(F32)
        dn = dn_ref[pl.ds(dn_offset + r0, CHUNK), :]
        dth = dn * (gain_ref[...] * scale)
        sums = _half_sum(jnp.concatenate([t * t, dth * t], axis=0), low)
        r = lax.rsqrt(sums[:CHUNK] * (1.0 / HEAD_DIM) + EPS)
        th = t * r
        out_ref[pl.ds(r0, CHUNK), :] = (r * (dth - th * (r * sums[CHUNK:] * (1.0 / HEAD_DIM)))).astype(BF16)
        return dgain + jnp.sum(dn * th, axis=0, keepdims=True) * scale

    return _chunks(step, jnp.zeros((1, LANES), F32))


def _rows8(v):
    return jnp.broadcast_to(v, (8, v.shape[-1]))


A_W = Q_BLOCK + 2 * A_HALF_WINDOW
A_PAD = A_HALF_WINDOW


def _seq_block(col_fn):
    return pl.BlockSpec((SEQ, LANES), col_fn)


def _attn_a_fwd(qkv, gain_q, gain_k, bias, sink):
    def body(sink_ref, q_ref, k_ref, v_ref, gq_ref, gk_ref, line_ref, o_ref, lse_ref, qn_ref, kp_ref, vp_ref,
             bias_ref):
        hp = pl.program_id(0)
        keep = (lax.broadcasted_iota(jnp.int32, (1, LANES), 1) // HEAD_DIM) == hp // 2
        _prep_q(q_ref, gq_ref, qn_ref)
        _prep_kv(k_ref, v_ref, gk_ref, kp_ref, vp_ref, A_PAD, keep)
        _unroll_bias(line_ref, bias_ref, A_W)

        def emit(rows, out, lse):
            o_ref[rows, :] = out
            lse_ref[rows, :] = lse

        _fwd_tiles(qn_ref, kp_ref, vp_ref, bias_ref, emit, d=1, half_window=A_HALF_WINDOW,
                   sinks=(sink_ref[2 * hp], sink_ref[2 * hp + 1]))

    vec = pl.BlockSpec((1, LANES), lambda hp, s: (0, 0))
    return pl.pallas_call(
        body, name="attn_a_fwd",
        grid_spec=pltpu.PrefetchScalarGridSpec(
            num_scalar_prefetch=1, grid=(4,),
            in_specs=[_seq_block(lambda hp, s: (0, QA_BLK + hp)), _seq_block(lambda hp, s: (0, KA_BLK)),
                      _seq_block(lambda hp, s: (0, VA_BLK)), vec, vec,
                      pl.BlockSpec((None, 2, _line_width(A_HALF_WINDOW)), lambda hp, s: (hp, 0, 0))],
            out_specs=[_seq_block(lambda hp, s: (0, hp)), _seq_block(lambda hp, s: (0, hp))],
            scratch_shapes=[pltpu.VMEM((SEQ, LANES), F32), pltpu.VMEM((SEQ + 2 * A_PAD, LANES), F32),
                            pltpu.VMEM((SEQ + 2 * A_PAD, LANES), F32), pltpu.VMEM((2 * Q_BLOCK, A_W), F32)]),
        out_shape=[jax.ShapeDtypeStruct((SEQ, 512), F32)] * 2,
        compiler_params=_params("arbitrary"),
    )(sink.reshape(8), qkv, qkv, qkv, gain_q, gain_k, bias)


def _attn_a_bwd(qkv, gain_q, gain_k, bias, sink, delta, lse, d_out):
    def body(sink_ref, q_ref, k_ref, v_ref, gq_ref, gk_ref, line_ref, delta_ref, lse_ref, do_ref,
             dq_out, dkv_out, dgq_out, dgk_out, dline_out, dsink_out,
             qn_ref, kp_ref, vp_ref, dq_ref, dk_ref, dv_ref, dk_tot, dv_tot, bias_ref, ds_out):
        hp = pl.program_id(0)
        kv_head = hp // 2
        keep = (lax.broadcasted_iota(jnp.int32, (1, LANES), 1) // HEAD_DIM) == kv_head
        _prep_q(q_ref, gq_ref, qn_ref)
        _prep_kv(k_ref, v_ref, gk_ref, kp_ref, vp_ref, A_PAD, keep)
        _unroll_bias(line_ref, bias_ref, A_W)
        dk_ref[...] = jnp.zeros_like(dk_ref)
        dv_ref[...] = jnp.zeros_like(dv_ref)
        ds_out[...] = jnp.zeros_like(ds_out)
        dsink_out[...] = jnp.zeros_like(dsink_out)

        @pl.when(hp == 0)
        def _():
            dk_tot[...] = jnp.zeros_like(dk_tot)
            dv_tot[...] = jnp.zeros_like(dv_tot)

        _bwd_tiles(qn_ref, kp_ref, vp_ref, bias_ref, do_ref, lse_ref, delta_ref, dq_ref, dk_ref, dv_ref, ds_out,
                   d=1, half_window=A_HALF_WINDOW, sinks=(sink_ref[2 * hp], sink_ref[2 * hp + 1]),
                   dsink_ref=dsink_out)
        _fold_bias_grad(ds_out, dline_out, A_W)
        dgq_out[...] = _rows8(_norm_bwd(q_ref, gq_ref, dq_ref, 0, dq_out, SCALE))

        def fold(r0, carry):
            rows = pl.ds(A_PAD + r0, CHUNK)
            for acc, tot in ((dk_ref, dk_tot), (dv_ref, dv_tot)):
                t = acc[rows, :]
                tot[pl.ds(r0, CHUNK), :] += jnp.where(keep, t + pltpu.roll(t, HEAD_DIM, 1), 0.0)
            return carry

        _chunks(fold)

        @pl.when(hp == 3)
        def _():
            dgk_out[...] = _rows8(_norm_bwd(k_ref, gk_ref, dk_tot, 0, dkv_out.at[0], 1.0))
            dkv_out[1] = dv_tot[...].astype(BF16)

    vec = pl.BlockSpec((1, LANES), lambda hp, s: (0, 0))
    seq_f32 = pltpu.VMEM((SEQ, LANES), F32)
    padded = pltpu.VMEM((SEQ + 2 * A_PAD, LANES), F32)
    return pl.pallas_call(
        body, name="attn_a_bwd",
        grid_spec=pltpu.PrefetchScalarGridSpec(
            num_scalar_prefetch=1, grid=(4,),
            in_specs=[_seq_block(lambda hp, s: (0, QA_BLK + hp)), _seq_block(lambda hp, s: (0, KA_BLK)),
                      _seq_block(lambda hp, s: (0, VA_BLK)), vec, vec,
                      pl.BlockSpec((None, 2, _line_width(A_HALF_WINDOW)), lambda hp, s: (hp, 0, 0)),
                      _seq_block(lambda hp, s: (0, hp)), _seq_block(lambda hp, s: (0, hp)),
                      _seq_block(lambda hp, s: (0, hp))],
            out_specs=[pl.BlockSpec((None, SEQ, LANES), lambda hp, s: (hp, 0, 0)),
                       pl.BlockSpec((2, SEQ, LANES), lambda hp, s: (0, 0, 0)),
                       pl.BlockSpec((None, 8, LANES), lambda hp, s: (hp, 0, 0)),
                       pl.BlockSpec((8, LANES), lambda hp, s: (0, 0)),
                       pl.BlockSpec((None, 2, _line_width(A_HALF_WINDOW)), lambda hp, s: (hp, 0, 0)),
                       pl.BlockSpec((None, 2, Q_BLOCK, 1), lambda hp, s: (hp, 0, 0, 0))],
            scratch_shapes=[seq_f32, padded, padded, seq_f32, padded, padded, seq_f32, seq_f32,
                            pltpu.VMEM((2 * Q_BLOCK, A_W), F32), pltpu.VMEM((2 * Q_BLOCK, A_W), F32)]),
        out_shape=[jax.ShapeDtypeStruct((4, SEQ, LANES), BF16), jax.ShapeDtypeStruct((2, SEQ, LANES), BF16),
                   jax.ShapeDtypeStruct((4, 8, LANES), F32), jax.ShapeDtypeStruct((8, LANES), F32),
                   jax.ShapeDtypeStruct((4, 2, _line_width(A_HALF_WINDOW)), F32),
                   jax.ShapeDtypeStruct((4, 2, Q_BLOCK, 1), F32)],
        compiler_params=_params("arbitrary"),
    )(sink.reshape(8), qkv, qkv, qkv, gain_q, gain_k, bias, delta, lse, d_out)


B_W = Q_BLOCK + 2 * B_HALF_WINDOW
B_PAD_MAX = B_HALF_WINDOW * B_DILATIONS[-1]


def _attn_b_fwd(qkv, gain_q, gain_k, bias):
    def body(q_ref, k_ref, v_ref, gq_ref, gk_ref, line_ref, o_ref, lse_ref, qn_ref, kp_ref, vp_ref, bias_ref):
        g = pl.program_id(1)
        _prep_q(q_ref, gq_ref, qn_ref)
        _unroll_bias(line_ref, bias_ref, B_W)

        def first(rows, out, lse):
            o_ref[rows, :] = out
            lse_ref[rows, :] = lse

        def combine(rows, out, lse):
            old = lse_ref[rows, :]
            new = jnp.maximum(old, lse) + jnp.log(1.0 + jnp.exp(-jnp.abs(old - lse)))
            o_ref[rows, :] = o_ref[rows, :] * jnp.exp(old - new) + out * jnp.exp(lse - new)
            lse_ref[rows, :] = new

        for gi, d in enumerate(B_DILATIONS):
            @pl.when(g == gi)
            def _():
                _prep_kv(k_ref, v_ref, gk_ref, kp_ref, vp_ref, B_HALF_WINDOW * d)
                _fwd_tiles(qn_ref, kp_ref, vp_ref, bias_ref, first if gi == 0 else combine,
                           d=d, half_window=B_HALF_WINDOW)

    vec = pl.BlockSpec((1, LANES), lambda hp, g: (0, 0))
    padded = pltpu.VMEM((SEQ + 2 * B_PAD_MAX, LANES), F32)
    return pl.pallas_call(
        body, name="attn_b_fwd", grid=(4, 3),
        in_specs=[_seq_block(lambda hp, g: (0, QB_BLK + 4 * g + hp)), _seq_block(lambda hp, g: (0, KB_BLK + 4 * g + hp)),
                  _seq_block(lambda hp, g: (0, VB_BLK + 4 * g + hp)), vec, vec,
                  pl.BlockSpec((None, 2, _line_width(B_HALF_WINDOW)), lambda hp, g: (4 * g + hp, 0, 0))],
        out_specs=[_seq_block(lambda hp, g: (0, hp)), _seq_block(lambda hp, g: (0, hp))],
        out_shape=[jax.ShapeDtypeStruct((SEQ, 512), F32)] * 2,
        scratch_shapes=[pltpu.VMEM((SEQ, LANES), F32), padded, padded, pltpu.VMEM((2 * Q_BLOCK, B_W), F32)],
        compiler_params=_params("arbitrary", "arbitrary"),
    )(qkv, qkv, qkv, gain_q, gain_k, bias)


def _attn_b_bwd(qkv, gain_q, gain_k, bias, delta, lse, d_out):
    def body(q_ref, k_ref, v_ref, gq_ref, gk_ref, line_ref, delta_ref, lse_ref, do_ref,
             dq_out, dk_out, dv_out, dgq_out, dgk_out, dline_out,
             qn_ref, kp_ref, vp_ref, dq_ref, dk_ref, dv_ref, bias_ref, ds_out):
        g = pl.program_id(1)
        _prep_q(q_ref, gq_ref, qn_ref)
        _unroll_bias(line_ref, bias_ref, B_W)
        ds_out[...] = jnp.zeros_like(ds_out)
        for gi, d in enumerate(B_DILATIONS):
            @pl.when(g == gi)
            def _():
                pad = B_HALF_WINDOW * d
                for acc in (dk_ref, dv_ref):
                    acc[pl.ds(0, SEQ + 2 * pad), :] = jnp.zeros((SEQ + 2 * pad, LANES), F32)
                _prep_kv(k_ref, v_ref, gk_ref, kp_ref, vp_ref, pad)
                _bwd_tiles(qn_ref, kp_ref, vp_ref, bias_ref, do_ref, lse_ref, delta_ref, dq_ref, dk_ref, dv_ref,
                           ds_out, d=d, half_window=B_HALF_WINDOW)
                dgk_out[...] = _rows8(_norm_bwd(k_ref, gk_ref, dk_ref, pad, dk_out, 1.0))
                dv_out[...] = dv_ref[pl.ds(pad, SEQ), :].astype(BF16)
        _fold_bias_grad(ds_out, dline_out, B_W)
        dgq_out[...] = _rows8(_norm_bwd(q_ref, gq_ref, dq_ref, 0, dq_out, SCALE))

    vec = pl.BlockSpec((1, LANES), lambda hp, g: (0, 0))
    seq_f32 = pltpu.VMEM((SEQ, LANES), F32)
    padded = pltpu.VMEM((SEQ + 2 * B_PAD_MAX, LANES), F32)
    part = pl.BlockSpec((None, 8, LANES), lambda hp, g: (4 * g + hp, 0, 0))
    line = pl.BlockSpec((None, 2, _line_width(B_HALF_WINDOW)), lambda hp, g: (4 * g + hp, 0, 0))
    return pl.pallas_call(
        body, name="attn_b_bwd", grid=(4, 3),
        in_specs=[_seq_block(lambda hp, g: (0, QB_BLK + 4 * g + hp)), _seq_block(lambda hp, g: (0, KB_BLK + 4 * g + hp)),
                  _seq_block(lambda hp, g: (0, VB_BLK + 4 * g + hp)), vec, vec,
                  line,
                  _seq_block(lambda hp, g: (0, hp)), _seq_block(lambda hp, g: (0, hp)), _seq_block(lambda hp, g: (0, hp))],
        out_specs=[pl.BlockSpec((None, SEQ, LANES), lambda hp, g: (4 * g + hp, 0, 0))] * 3 + [part, part, line],
        out_shape=[jax.ShapeDtypeStruct((12, SEQ, LANES), BF16)] * 3
        + [jax.ShapeDtypeStruct((12, 8, LANES), F32)] * 2
        + [jax.ShapeDtypeStruct((12, 2, _line_width(B_HALF_WINDOW)), F32)],
        scratch_shapes=[seq_f32, padded, padded, seq_f32, padded, padded,
                        pltpu.VMEM((2 * Q_BLOCK, B_W), F32), pltpu.VMEM((2 * Q_BLOCK, B_W), F32)],
        compiler_params=_params("arbitrary", "arbitrary"),
    )(qkv, qkv, qkv, gain_q, gain_k, bias, delta, lse, d_out)


def _sigmoid(t):
    return 1.0 / (1.0 + jnp.exp(-t))


def _middle(out_a, out_b, gates, x, target, w_a, w_b, w_out, b_merge):
    tm = 256
    n_steps = SEQ // tm

    def body(oa_ref, ob_ref, g_ref, x_ref, t_ref, wa_ref, wb_ref, wo_ref, bm_ref,
             dy_ref, dg_ref, doa_ref, dob_ref, dla_ref, dlb_ref, dwa_ref, dwb_ref, dwo_ref, dbm_ref, sq_ref):
        @pl.when(pl.program_id(0) == 0)
        def _():
            for ref in (dwa_ref, dwb_ref, dwo_ref, dbm_ref, sq_ref):
                ref[...] = jnp.zeros_like(ref)

        gate_a, gate_b = g_ref[:, 0:512], g_ref[:, 512:1024]
        sig_a, sig_b = _sigmoid(gate_a), _sigmoid(gate_b)
        silu_a, silu_b = gate_a * sig_a, gate_b * sig_b
        oa, ob = oa_ref[...], ob_ref[...]
        ya, yb = (oa * silu_a).astype(BF16), (ob * silu_b).astype(BF16)
        br_a, br_b = _dot(ya, wa_ref[...]), _dot(yb, wb_ref[...])
        m0 = _sigmoid(g_ref[:, 1024:2048] + bm_ref[0:1, :])
        m1 = _sigmoid(g_ref[:, 2048:3072] + bm_ref[1:2, :])
        merged = (m0 * br_a + m1 * br_b).astype(BF16)
        err = (x_ref[...] + _dot(merged, wo_ref[...])) - t_ref[...]
        sq_ref[...] += jnp.sum(err * err, axis=0, keepdims=True)

        dy = err * (1.0 / D_MODEL)
        dy_ref[...] = dy
        dyb = dy.astype(BF16)
        dmerged = _dot(dyb, wo_ref[...], NT)
        dwo_ref[...] += _dot(merged, dyb, TN)
        dbr_a, dbr_b = (dmerged * m0).astype(BF16), (dmerged * m1).astype(BF16)
        dm0 = (dmerged * br_a) * (m0 * (1.0 - m0))
        dm1 = (dmerged * br_b) * (m1 * (1.0 - m1))
        dbm_ref[0:1, :] += jnp.sum(dm0, axis=0, keepdims=True)
        dbm_ref[1:2, :] += jnp.sum(dm1, axis=0, keepdims=True)
        for s in range(N_CHIPS):
            cols = slice(256 * s, 256 * (s + 1))
            dwa_ref[s] += _dot(ya, dbr_a[:, cols], TN)
            dwb_ref[s] += _dot(yb, dbr_b[:, cols], TN)
        dya, dyb_ = _dot(dbr_a, wa_ref[...], NT), _dot(dbr_b, wb_ref[...], NT)
        doa, dob = dya * silu_a, dyb_ * silu_b
        doa_ref[...] = doa
        dob_ref[...] = dob
        for blk in range(512 // LANES):
            lanes = slice(blk * LANES, (blk + 1) * LANES)
            dla_ref[:, lanes] = _half_sum(doa[:, lanes] * oa[:, lanes], None)
            dlb_ref[:, lanes] = _half_sum(dob[:, lanes] * ob[:, lanes], None)
        d_gates = (((dya * oa) * (sig_a * (1.0 + gate_a * (1.0 - sig_a)))).astype(BF16),
                   ((dyb_ * ob) * (sig_b * (1.0 + gate_b * (1.0 - sig_b)))).astype(BF16),
                   dm0.astype(BF16), dm1.astype(BF16))
        blk = 0
        for part in d_gates:
            for c0 in range(0, part.shape[1], 256):
                dg_ref[blk] = part[:, c0:c0 + 256]
                blk += 1

    def rows(width):
        return pl.BlockSpec((tm, width), lambda i: (i, 0))

    def whole(*shape):
        return pl.BlockSpec(shape, lambda i: (0,) * len(shape))

    return pl.pallas_call(
        body, name="middle", grid=(n_steps,),
        in_specs=[rows(512), rows(512), rows(GATE_WIDTH), rows(D_MODEL), rows(D_MODEL),
                  whole(512, D_MODEL), whole(512, D_MODEL), whole(D_MODEL, D_MODEL), whole(2, D_MODEL)],
        out_specs=[rows(D_MODEL), pl.BlockSpec((GATE_WIDTH // 256, tm, 256), lambda i: (0, i, 0)),
                   rows(512), rows(512), rows(512), rows(512),
                   whole(N_CHIPS, 512, 256), whole(N_CHIPS, 512, 256), whole(D_MODEL, D_MODEL),
                   whole(2, D_MODEL), whole(1, D_MODEL)],
        out_shape=[jax.ShapeDtypeStruct((SEQ, D_MODEL), F32), jax.ShapeDtypeStruct((GATE_WIDTH // 256, SEQ, 256), BF16),
                   jax.ShapeDtypeStruct((SEQ, 512), F32), jax.ShapeDtypeStruct((SEQ, 512), F32),
                   jax.ShapeDtypeStruct((SEQ, 512), F32), jax.ShapeDtypeStruct((SEQ, 512), F32),
                   jax.ShapeDtypeStruct((N_CHIPS, 512, 256), F32), jax.ShapeDtypeStruct((N_CHIPS, 512, 256), F32),
                   jax.ShapeDtypeStruct((D_MODEL, D_MODEL), F32), jax.ShapeDtypeStruct((2, D_MODEL), F32),
                   jax.ShapeDtypeStruct((1, D_MODEL), F32)],
        compiler_params=_params("arbitrary"),
    )(out_a, out_b, gates, x, target, w_a, w_b, w_out, b_merge)


def _which(j, edges, fns):
    lo = 0
    for hi, fn in zip(edges, fns):
        pl.when((j >= lo) & (j < hi))(fn)
        lo = hi


def _d_w_in(d_proj, h):
    plan, step, width = [], 0, 0
    for p in d_proj:
        total = p.shape[0] * p.shape[2]
        if width + total <= W_BLOCK:
            plan.append((p.shape[0], step, 1))
            width += total
            if width == W_BLOCK:
                step, width = step + 1, 0
        else:
            assert width == 0 and total % W_BLOCK == 0
            plan.append((W_BLOCK // p.shape[2], step, total // W_BLOCK))
            step += total // W_BLOCK
    assert width == 0 and step == IN_WIDTH // W_BLOCK
    firsts = sorted({first for _, first, _ in plan})
    edges = firsts[1:] + [step]
    halves = 2

    def body(*refs):
        pieces, h_ref, o_ref, acc_ref = refs[:-3], refs[-3], refs[-2], refs[-1]
        k = pl.program_id(1)

        def emit(group):
            def fn():
                cols = jnp.concatenate([ref[b] for ref in group for b in range(ref.shape[0])], axis=1)
                term = _dot(cols, h_ref[...], TN)

                @pl.when(k == 0)
                def _():
                    acc_ref[...] = term

                @pl.when(k == halves - 1)
                def _():
                    o_ref[...] = (acc_ref[...] + term).astype(BF16)
            return fn

        groups = [[ref for ref, (_, first, _) in zip(pieces, plan) if first == f] for f in firsts]
        _which(pl.program_id(0), edges, [emit(group) for group in groups])

    def cols_spec(piece, n, first, steps):
        def index(j, k):
            return jnp.clip(j - first, 0, steps - 1), jnp.where((j >= first) & (j < first + steps), k, 0), 0
        return pl.BlockSpec((n, SEQ // halves, piece.shape[2]), index)

    return pl.pallas_call(
        body, name="d_w_in", grid=(step, halves),
        in_specs=[cols_spec(p, *pl_) for p, pl_ in zip(d_proj, plan)]
        + [pl.BlockSpec((SEQ // halves, D_MODEL), lambda j, k: (k, 0))],
        out_specs=pl.BlockSpec((W_BLOCK, D_MODEL), lambda j, k: (j, 0)),
        out_shape=jax.ShapeDtypeStruct((IN_WIDTH, D_MODEL), BF16),
        scratch_shapes=[pltpu.VMEM((W_BLOCK, D_MODEL), F32)],
        compiler_params=_params("arbitrary", "arbitrary"),
    )(*d_proj, h)


def _d_x(d_proj, w_t, x, gain, dy, chip_sums):
    tm = 256
    n_steps = SEQ // tm
    n_w = IN_WIDTH // W_BLOCK
    n_p, n_s = len(d_proj), len(chip_sums)

    def body(*refs):
        pieces, w_refs = refs[:n_p], refs[n_p:n_p + n_w]
        x_ref, g_ref, dy_ref = refs[n_p + n_w:n_p + n_w + 3]
        q_refs = refs[n_p + n_w + 3:n_p + n_w + 3 + n_s]
        dx_ref, dgain_ref = refs[n_p + n_w + 3 + n_s:n_p + n_w + 5 + n_s]
        o_refs = refs[n_p + n_w + 5 + n_s:n_p + n_w + 5 + 2 * n_s]
        send_sems, recv_sems = refs[n_p + n_w + 5 + 2 * n_s:] if n_s else (None, None)

        @pl.when(pl.program_id(0) == 0)
        def _():
            dgain_ref[...] = jnp.zeros_like(dgain_ref)
            if n_s:
                for cp in _scatter_copies(q_refs, o_refs, send_sems, recv_sems):
                    cp.start()

        blocks = [(piece, k) for piece in pieces for k in range(piece.shape[0])]
        dh, group, width, blk = None, [], 0, 0
        for piece, k in blocks:
            group.append(piece[k])
            width += piece.shape[2]
            if width == W_BLOCK:
                term = _dot(jnp.concatenate(group, axis=1), w_refs[blk][...])
                dh = term if dh is None else dh + term
                group, width, blk = [], 0, blk + 1
        assert not group and blk == n_w
        xf = x_ref[...]
        r = lax.rsqrt(jnp.mean(xf * xf, axis=-1, keepdims=True) + EPS)
        xh = xf * r
        dxh = dh * g_ref[...]
        dx_ref[...] = r * (dxh - xh * jnp.mean(dxh * xh, axis=-1, keepdims=True)) + dy_ref[...]
        dgain_ref[...] += _rows8(jnp.sum(dh * xh, axis=0, keepdims=True))

        if n_s:
            @pl.when(pl.program_id(0) == n_steps - 1)
            def _():
                for cp in _scatter_copies(q_refs, o_refs, send_sems, recv_sems):
                    cp.wait()

    row = pl.BlockSpec((tm, D_MODEL), lambda i: (i, 0))
    res = pl.pallas_call(
        body, name="d_x", grid=(n_steps,),
        in_specs=[pl.BlockSpec((p.shape[0], tm, p.shape[2]), lambda i: (0, i, 0)) for p in d_proj] + _w_blocks(0, n_w)
        + [row, pl.BlockSpec((1, D_MODEL), lambda i: (0, 0)), row] + [ANY] * n_s,
        out_specs=[row, pl.BlockSpec((8, D_MODEL), lambda i: (0, 0))] + [ANY] * n_s,
        out_shape=[jax.ShapeDtypeStruct((SEQ, D_MODEL), F32), jax.ShapeDtypeStruct((8, D_MODEL), F32)]
        + [jax.ShapeDtypeStruct((3,) + q.shape[1:], BF16) for q in chip_sums],
        scratch_shapes=[pltpu.SemaphoreType.DMA((3 * n_s,)), pltpu.SemaphoreType.DMA((3 * n_s,))] if n_s else [],
        compiler_params=_params("arbitrary"),
    )(*d_proj, *([w_t] * n_w), x, gain, dy, *chip_sums)
    return res[0], res[1], res[2:]


def _my_place():
    x, y, c = lax.axis_index("x"), lax.axis_index("y"), lax.axis_index("c")
    return jnp.stack([2 * x + y, c]).astype(jnp.int32)


def _half_rows(ref, half):
    rows = ref.shape[-2] // 2
    idx = (slice(None),) * (len(ref.shape) - 2) + (pl.ds(pl.multiple_of(half * rows, 16), rows), slice(None))
    return ref.at[idx]


def _swap_halves(grads):
    n = len(grads)

    def body(*refs):
        g_refs, o_refs, (send_sems, recv_sems) = refs[:n], refs[n:2 * n], refs[2 * n:]
        x, y, c = lax.axis_index("x"), lax.axis_index("y"), lax.axis_index("c")
        copies = [pltpu.make_async_remote_copy(src_ref=_half_rows(g, 1 - c), dst_ref=o, send_sem=send_sems.at[k],
                                               recv_sem=recv_sems.at[k], device_id=(x, y, 1 - c), device_id_type=MESH)
                  for k, (g, o) in enumerate(zip(g_refs, o_refs))]
        for cp in copies:
            cp.start()
        for cp in copies:
            cp.wait()

    return pl.pallas_call(
        body, name="reduce_swap_halves", in_specs=[ANY] * n, out_specs=[ANY] * n,
        out_shape=[jax.ShapeDtypeStruct((N_CHIPS, g.shape[1] // 2, D_MODEL), g.dtype) for g in grads],
        scratch_shapes=[pltpu.SemaphoreType.DMA((n,)), pltpu.SemaphoreType.DMA((n,))],
    )(*grads)


def _add_halves(place, grads, theirs, name):
    half = theirs.shape[1]
    tr = _row_tile(half)
    n = half // tr

    def body(place_ref, g_ref, t_ref, o_ref):
        o_ref[...] = (g_ref[...].astype(F32) + t_ref[...].astype(F32)).astype(BF16)

    return pl.pallas_call(
        body, name=name,
        grid_spec=pltpu.PrefetchScalarGridSpec(
            num_scalar_prefetch=1, grid=(N_CHIPS, n),
            in_specs=[pl.BlockSpec((None, tr, D_MODEL), lambda s, i, p: (s, p[1] * n + i, 0)),
                      pl.BlockSpec((None, tr, D_MODEL), lambda s, i, p: (s, i, 0))],
            out_specs=pl.BlockSpec((None, tr, D_MODEL), lambda s, i, p: (s, i, 0))),
        out_shape=jax.ShapeDtypeStruct((N_CHIPS, half, D_MODEL), BF16),
        compiler_params=_params("arbitrary", "arbitrary"),
    )(place, grads, theirs)


def _scatter_copies(q_refs, o_refs, send_sems, recv_sems):
    x, y, c = lax.axis_index("x"), lax.axis_index("y"), lax.axis_index("c")
    chips = [(1 - x, y), (x, 1 - y), (1 - x, 1 - y)]
    return [pltpu.make_async_remote_copy(src_ref=q.at[2 * cx + cy], dst_ref=o.at[j],
                                         send_sem=send_sems.at[3 * k + j], recv_sem=recv_sems.at[3 * k + j],
                                         device_id=(cx, cy, c), device_id_type=MESH)
            for k, (q, o) in enumerate(zip(q_refs, o_refs)) for j, (cx, cy) in enumerate(chips)]


def _add_chips(place, chip_sums, others, name):
    half = others.shape[1]
    tr = _row_tile(half)
    n = half // tr

    def body(place_ref, q_ref, o_ref, r_ref):
        acc = q_ref[...].astype(F32)
        for j in range(3):
            acc = acc + o_ref[j].astype(F32)
        r_ref[...] = acc

    return pl.pallas_call(
        body, name=name,
        grid_spec=pltpu.PrefetchScalarGridSpec(
            num_scalar_prefetch=1, grid=(n,),
            in_specs=[pl.BlockSpec((None, tr, D_MODEL), lambda i, p: (p[0], i, 0)),
                      pl.BlockSpec((3, tr, D_MODEL), lambda i, p: (0, i, 0))],
            out_specs=pl.BlockSpec((tr, D_MODEL), lambda i, p: (p[1] * n + i, 0))),
        out_shape=jax.ShapeDtypeStruct((2 * half, D_MODEL), F32),
        compiler_params=_params("arbitrary"),
    )(place, chip_sums, others)


def _join_halves(shards, block):
    n = len(shards)
    rows = block.shape[0]

    def body(*refs):
        b_ref, o_refs, sum_ref = refs[n], refs[n + 1:2 * n + 1], refs[2 * n + 1]
        send_sems, recv_sems, small_send, small_recv, local_sem, all_ref = refs[2 * n + 2:]
        x, y, c = lax.axis_index("x"), lax.axis_index("y"), lax.axis_index("c")
        me, sibling = (x, y, c), (x, y, 1 - c)
        chips = [(1 - x, y), (x, 1 - y), (1 - x, 1 - y)]

        def half(k, rows_ref):
            return pltpu.make_async_remote_copy(src_ref=rows_ref, dst_ref=rows_ref, send_sem=send_sems.at[k],
                                                recv_sem=recv_sems.at[k], device_id=sibling, device_id_type=MESH)

        def at(px, py, pc):
            return all_ref.at[pl.ds(pl.multiple_of((4 * px + 2 * py + pc) * rows, 8), rows), :]

        def small(k, block_of, to, src=None):
            return pltpu.make_async_remote_copy(src_ref=at(*block_of) if src is None else src, dst_ref=at(*block_of),
                                                send_sem=small_send.at[k], recv_sem=small_recv.at[k],
                                                device_id=to, device_id_type=MESH)

        sends = [half(k, _half_rows(o, c)) for k, o in enumerate(o_refs)]
        for cp in sends:
            cp.start()
        mine = pltpu.make_async_copy(b_ref, at(*me), local_sem)
        mine.start()
        first = [small(0, me, sibling, src=b_ref)]
        first += [small(1 + j, me, (*chip, c), src=b_ref) for j, chip in enumerate(chips)]
        for cp in first:
            cp.start()
        passed = [small(4 + j, (*chip, c), sibling) for j, chip in enumerate(chips)]
        for j, chip in enumerate(chips):
            small(1 + j, (*chip, c), me).wait_recv()
            passed[j].start()
        small(0, sibling, me).wait_recv()
        for j, chip in enumerate(chips):
            small(4 + j, (*chip, 1 - c), me).wait_recv()
        mine.wait()
        acc = all_ref[0:rows, :]
        for dev in range(1, 8):
            acc = acc + all_ref[rows * dev:rows * (dev + 1), :]
        sum_ref[...] = acc
        for k, o in enumerate(o_refs):
            half(k, _half_rows(o, 1 - c)).wait_recv()
        for cp in sends + first + passed:
            cp.wait_send()

    res = pl.pallas_call(
        body, name="reduce_join_halves", in_specs=[ANY] * n + [pl.BlockSpec(memory_space=pltpu.VMEM)],
        out_specs=[ANY] * n + [pl.BlockSpec(memory_space=pltpu.VMEM)],
        out_shape=[jax.ShapeDtypeStruct(s.shape, F32) for s in shards] + [jax.ShapeDtypeStruct(block.shape, F32)],
        input_output_aliases={k: k for k in range(n)},
        scratch_shapes=[pltpu.SemaphoreType.DMA((n,)), pltpu.SemaphoreType.DMA((n,)),
                        pltpu.SemaphoreType.DMA((7,)), pltpu.SemaphoreType.DMA((7,)), pltpu.SemaphoreType.DMA,
                        pltpu.VMEM((8 * rows, D_MODEL), F32)],
    )(*shards, block)
    return res[:n], res[n]


def _adamw_math(w, g, m, v):
    m = ADAM_B1 * m + (1.0 - ADAM_B1) * g
    v = ADAM_B2 * v + (1.0 - ADAM_B2) * (g * g)
    m_hat = m / (1.0 - ADAM_B1 ** ADAM_STEP)
    v_hat = v / (1.0 - ADAM_B2 ** ADAM_STEP)
    return -ADAM_LR * (m_hat / (jnp.sqrt(v_hat) + ADAM_EPS) + ADAM_WD * w), m, v


def _adamw(w, g, m, v, name):
    r, c = w.shape
    tr = _row_tile(r)

    def body(w_ref, g_ref, m_ref, v_ref, d_ref, nm_ref, nv_ref):
        d_ref[...], nm_ref[...], nv_ref[...] = _adamw_math(w_ref[...], g_ref[...], m_ref[...], v_ref[...])

    spec = pl.BlockSpec((tr, c), lambda i: (i, 0))
    return pl.pallas_call(
        body, name=name, grid=(r // tr,), in_specs=[spec] * 4, out_specs=[spec] * 3,
        out_shape=[jax.ShapeDtypeStruct((r, c), F32)] * 3, compiler_params=_params("arbitrary"),
    )(w, g, m, v)


def _adamw_small(ws, gs, ms, vs):
    n = len(ws)

    def body(*refs):
        ins, outs = refs[:4 * n], refs[4 * n:]
        for k in range(n):
            d, m, v = _adamw_math(ins[k][...], ins[n + k][...], ins[2 * n + k][...], ins[3 * n + k][...])
            outs[k][...], outs[n + k][...], outs[2 * n + k][...] = d, m, v

    shapes = [jax.ShapeDtypeStruct(w.shape, F32) for w in ws]
    res = pl.pallas_call(body, name="adamw_small", out_shape=shapes * 3)(*ws, *gs, *ms, *vs)
    return res[:n], res[n:2 * n], res[2 * n:]


def _fold_heads(partials):
    t = jnp.sum(partials[:, 0, :], axis=0)
    return (t[:HEAD_DIM] + t[HEAD_DIM:]).reshape(1, HEAD_DIM)


def _local_step(x, target, norm_gain, w_t, w_a, w_b, w_o, b_m, q_norm_a, k_norm_a, q_norm_b, k_norm_b, sink_a,
                rel_bias, start_reduce=None):
    two = lambda gain: jnp.concatenate([gain, gain], axis=1)
    bias_a = _bias_lines(rel_bias[:, :8], A_HALF_WINDOW, 1)
    bias_b = jnp.concatenate([_bias_lines(rel_bias[:, 8 + 8 * g:16 + 8 * g], B_HALF_WINDOW, d)
                              for g, d in enumerate(B_DILATIONS)], axis=0)

    qkv, h = _in_proj(x, norm_gain, w_t, 0, QKV_WIDTH // W_BLOCK, BF16, "in_proj_qkv", True)
    gates, = _in_proj(x, norm_gain, w_t, QKV_WIDTH // W_BLOCK, GATE_WIDTH // W_BLOCK, F32, "in_proj_gates", False)
    out_a, lse_a = _attn_a_fwd(qkv, two(q_norm_a), two(k_norm_a), bias_a, sink_a)
    out_b, lse_b = _attn_b_fwd(qkv, two(q_norm_b), two(k_norm_b), bias_b)

    dy, dgates, d_out_a, d_out_b, delta_a, delta_b, d_wa, d_wb, d_wo, d_bm, sq = _middle(
        out_a, out_b, gates, x, target, w_a, w_b, w_o, b_m)
    loss = (0.5 / D_MODEL) * jnp.sum(sq)

    dq_a, dkv_a, dgq_a, dgk_a, ds_a, dsink = _attn_a_bwd(
        qkv, two(q_norm_a), two(k_norm_a), bias_a, sink_a, delta_a, lse_a, d_out_a)
    dq_b, dk_b, dv_b, dgq_b, dgk_b, ds_b = _attn_b_bwd(
        qkv, two(q_norm_b), two(k_norm_b), bias_b, delta_b, lse_b, d_out_b)
    d_proj = (dq_a, dkv_a, dq_b, dk_b, dv_b, dgates)

    d_bm_rows = jnp.pad(d_bm.reshape(2, N_CHIPS, 256).transpose(1, 0, 2),
                        ((0, 0), (0, REST_ROWS - 514), (0, D_MODEL - 256)))
    rest = jnp.concatenate([d_wo.reshape(N_CHIPS, 256, D_MODEL), d_wa.reshape(N_CHIPS, 128, D_MODEL),
                            d_wb.reshape(N_CHIPS, 128, D_MODEL), d_bm_rows], axis=1)
    grads = [_d_w_in(d_proj, h).reshape(N_CHIPS, W_IN_SHARD, D_MODEL), rest]
    narrow = [grads[0], rest.astype(BF16)]
    chip_sums = start_reduce(grads, narrow) if start_reduce is not None else []
    grad_x, d_gain, others = _d_x(d_proj, w_t, x, norm_gain, dy, chip_sums)

    d_rel = jnp.concatenate(
        [_bias_grad(ds_a, A_HALF_WINDOW, 1)]
        + [_bias_grad(ds_b[4 * g:4 * g + 4], B_HALF_WINDOW, d) for g, d in enumerate(B_DILATIONS)], axis=1)
    d_sink = jnp.sum(dsink, axis=(2, 3)).reshape(1, 8)
    dgk_a_row = dgk_a[0]
    small = jnp.zeros((8, D_MODEL), F32)
    small = small.at[0].set(d_gain[0])
    small = small.at[1].set(d_rel.reshape(-1))
    misc = jnp.concatenate([_fold_heads(dgq_a), (dgk_a_row[:HEAD_DIM] + dgk_a_row[HEAD_DIM:]).reshape(1, HEAD_DIM),
                            _fold_heads(dgq_b), _fold_heads(dgk_b), d_sink], axis=1)
    small = small.at[2, :264].set(misc[0])

    return loss, grad_x, grads, small, chip_sums, others


def _unpack_weights(w_t_all, small_all):
    sm = small_all.reshape(N_CHIPS, SMALL_ROWS, D_MODEL)
    w_o = sm[:, 0:256].reshape(D_MODEL, D_MODEL)
    w_a = sm[:, 256:384].reshape(N_CHIPS, 512, 256).transpose(1, 0, 2).reshape(512, D_MODEL)
    w_b = sm[:, 384:512].reshape(N_CHIPS, 512, 256).transpose(1, 0, 2).reshape(512, D_MODEL)
    b_m = lax.bitcast_convert_type(sm[:, 512].reshape(N_CHIPS, 2, 256, 2), F32)
    return w_t_all, w_a, w_b, w_o, b_m.transpose(1, 0, 2).reshape(2, D_MODEL)


def _pack_small_weights(w_branch_a, w_branch_b, b_merge, w_out):
    b_m = jnp.pad(lax.bitcast_convert_type(b_merge, BF16).reshape(1, D_MODEL), ((0, SMALL_ROWS - 513), (0, 0)))
    return jnp.concatenate([w_out.astype(BF16), w_branch_a.astype(BF16).reshape(128, D_MODEL),
                            w_branch_b.astype(BF16).reshape(128, D_MODEL), b_m], axis=0)


def kernel(x, norm_gain, w_in, q_norm_a, k_norm_a, q_norm_b, k_norm_b, sink_a, rel_bias, w_branch_a, w_branch_b, b_merge, w_out, loss_target, m_norm_gain, m_w_in, m_q_norm_a, m_k_norm_a, m_q_norm_b, m_k_norm_b, m_sink_a, m_rel_bias, m_w_branch_a, m_w_branch_b, m_b_merge, m_w_out, v_norm_gain, v_w_in, v_q_norm_a, v_k_norm_a, v_q_norm_b, v_k_norm_b, v_sink_a, v_rel_bias, v_w_branch_a, v_w_branch_b, v_b_merge, v_w_out):
    w_in_t, m_w_in_t, v_w_in_t = (jnp.transpose(t[0]) for t in (w_in, m_w_in, v_w_in))
    wt_shard = _cast_rows(w_in_t, BF16, "w_in_cast")
    w_t, w_a, w_b, w_o, b_m = _unpack_weights(
        *_gather_weights(wt_shard, _pack_small_weights(w_branch_a[0], w_branch_b[0], b_merge[0], w_out[0])))

    place = _my_place()
    names = ("w_in", "rest")

    def start_reduce(grads, narrow):
        return [_add_halves(place, g, t, "reduce_add_halves_" + n) for g, t, n in zip(grads, _swap_halves(narrow), names)]

    loss_part, grad_x, _, small, chip_sums, others = _local_step(
        x[0], loss_target[0], norm_gain, w_t, w_a, w_b, w_o, b_m, q_norm_a, k_norm_a, q_norm_b, k_norm_b,
        sink_a, rel_bias, start_reduce)

    (g_wt, g_rest), small = _join_halves(
        [_add_chips(place, q, o, "reduce_add_chips_" + n) for q, o, n in zip(chip_sums, others, names)],
        small.at[3, 0].set(loss_part))
    loss = small[3, 0]

    g_w_out = g_rest[0:256]
    g_w_a = g_rest[256:384].reshape(512, 256)
    g_w_b = g_rest[384:512].reshape(512, 256)
    g_b_merge = g_rest[512:514, :256]
    g_norm_gain = small[0:1]
    g_rel_bias = small[1].reshape(N_BUCKETS, N_BUCKETS)
    g_q_a, g_k_a, g_q_b, g_k_b = (small[2:3, 64 * k:64 * k + 64] for k in range(4))
    g_sink = small[2:3, 256:264]

    big_names = (("w_branch_a", w_branch_a, g_w_a, m_w_branch_a, v_w_branch_a),
                 ("w_branch_b", w_branch_b, g_w_b, m_w_branch_b, v_w_branch_b),
                 ("w_out", w_out, g_w_out, m_w_out, v_w_out))
    upd = {name: (g,) + tuple(_adamw(w[0], g, m[0], v[0], "adamw_" + name)) for name, w, g, m, v in big_names}
    upd["w_in"] = tuple(jnp.transpose(t) for t in (g_wt,) + tuple(_adamw(w_in_t, g_wt, m_w_in_t, v_w_in_t, "adamw_w_in")))
    small_names = ("norm_gain", "q_norm_a", "k_norm_a", "q_norm_b", "k_norm_b", "sink_a", "rel_bias", "b_merge")
    ws = [norm_gain, q_norm_a, k_norm_a, q_norm_b, k_norm_b, sink_a, rel_bias, b_merge[0]]
    gs = [g_norm_gain, g_q_a, g_k_a, g_q_b, g_k_b, g_sink, g_rel_bias, g_b_merge]
    ms = [m_norm_gain, m_q_norm_a, m_k_norm_a, m_q_norm_b, m_k_norm_b, m_sink_a, m_rel_bias, m_b_merge[0]]
    vs = [v_norm_gain, v_q_norm_a, v_k_norm_a, v_q_norm_b, v_k_norm_b, v_sink_a, v_rel_bias, v_b_merge[0]]
    ds, nms, nvs = _adamw_small(ws, gs, ms, vs)
    for k, name in enumerate(small_names):
        upd[name] = (gs[k], ds[k], nms[k], nvs[k])

    order = ("norm_gain", "w_in", "q_norm_a", "k_norm_a", "q_norm_b", "k_norm_b", "sink_a", "rel_bias",
             "w_branch_a", "w_branch_b", "b_merge", "w_out")
    lead = {"w_in", "w_branch_a", "w_branch_b", "b_merge", "w_out"}
    outs = [loss, grad_x[None]]
    for part in range(4):
        outs += [upd[name][part][None] if name in lead else upd[name][part] for name in order]
    return tuple(outs)
```

```python
import math

import numpy as np
import jax
import jax.numpy as jnp
from jax import lax
from jax.experimental import pallas as pl
from jax.experimental.pallas import tpu as pltpu

F32 = jnp.float32
BF16 = jnp.bfloat16

SEQ = 4096
D_MODEL = 1024
HEAD_DIM = 64
LANES = 128
EPS = 1e-6
NEG_INF = -1e30
SCALE = HEAD_DIM ** -0.5
N_BUCKETS = 32
MAX_DISTANCE = 1024
N_CHIPS = 4

A_HALF_WINDOW = 128
B_HALF_WINDOW = 64
B_DILATIONS = (1, 4, 16)
Q_BLOCK = 128

QKV_WIDTH = 5376
GATE_WIDTH = 3072
QA_BLK, KA_BLK, VA_BLK = 0, 4, 5
QB_BLK, KB_BLK, VB_BLK = 6, 18, 30
IN_WIDTH = QKV_WIDTH + GATE_WIDTH
W_IN_SHARD = IN_WIDTH // N_CHIPS

SMALL_ROWS = 544
REST_ROWS = 544

ADAM_LR = 0.001
ADAM_B1 = 0.9
ADAM_B2 = 0.999
ADAM_EPS = 1e-08
ADAM_WD = 0.01
ADAM_STEP = 10

VMEM_LIMIT = 56 * 1024 * 1024

NT = (((1,), (1,)), ((), ()))
TN = (((0,), (0,)), ((), ()))
MESH = pl.DeviceIdType.MESH
ANY = pl.BlockSpec(memory_space=pl.ANY)


def _dot(a, b, dims=None):
    if dims is None:
        return jnp.dot(a, b, preferred_element_type=F32)
    return lax.dot_general(a, b, dims, preferred_element_type=F32)


def _params(*semantics):
    return pltpu.CompilerParams(dimension_semantics=semantics or None, vmem_limit_bytes=VMEM_LIMIT)


def _line_width(half_window):
    return pl.cdiv(2 * Q_BLOCK + 2 * half_window - 1, LANES) * LANES


def _bucket_onehot(half_window, stride):
    rel = np.arange(_line_width(half_window)) - (Q_BLOCK - 1) - half_window
    band = np.abs(rel) <= half_window
    rel = rel * stride
    half, max_exact = N_BUCKETS // 2, N_BUCKETS // 4
    n = np.abs(rel)
    nf = np.maximum(n, max_exact).astype(np.float32)
    large = max_exact + (np.log(nf / np.float32(max_exact)) / np.float32(math.log(MAX_DISTANCE / max_exact))
                         * np.float32(half - max_exact)).astype(np.int32)
    large = np.minimum(large, half - 1)
    bucket = (rel > 0).astype(np.int32) * half + np.where(n < max_exact, n, large)
    onehot = (bucket[..., None] == np.arange(N_BUCKETS)) & band[..., None]
    return onehot.astype(np.float32), band


def _bias_lines(rel_bias_cols, half_window, stride):
    onehot, band = _bucket_onehot(half_window, stride)
    h = rel_bias_cols.shape[1]
    t = jnp.einsum("tb,bh->ht", jnp.asarray(onehot), rel_bias_cols, precision=lax.Precision.HIGHEST)
    t = t + jnp.asarray(np.where(band, 0.0, NEG_INF).astype(np.float32))
    return t.reshape(h // 2, 2, -1)


def _bias_grad(d_lines, half_window, stride):
    onehot, _ = _bucket_onehot(half_window, stride)
    h = d_lines.shape[0] * 2
    return jnp.einsum("tb,ht->bh", jnp.asarray(onehot), d_lines.reshape(h, -1), precision=lax.Precision.HIGHEST)


def _unroll_bias(line_ref, tile_ref, w):
    width = line_ref.shape[1]
    for j in range(2):
        rows = jnp.broadcast_to(line_ref[j:j + 1, :], (Q_BLOCK, width))
        rows = pltpu.roll(rows, width - (Q_BLOCK - 1), 1, stride=1, stride_axis=0)
        tile_ref[j * Q_BLOCK:(j + 1) * Q_BLOCK, :] = rows[:, :w]


def _fold_bias_grad(tile_ref, line_ref, w):
    width = line_ref.shape[1]
    row = lax.broadcasted_iota(jnp.int32, (Q_BLOCK, Q_BLOCK), 0)
    col = lax.broadcasted_iota(jnp.int32, (Q_BLOCK, Q_BLOCK), 1)
    flip = jnp.where(row + col == Q_BLOCK - 1, 1.0, 0.0).astype(BF16)
    for j in range(2):
        tile = tile_ref[j * Q_BLOCK:(j + 1) * Q_BLOCK, :]
        hi = tile.astype(BF16)
        lo = (tile - hi.astype(F32)).astype(BF16)
        rows = _dot(flip, hi) + _dot(flip, lo)
        rows = jnp.concatenate([rows, jnp.zeros((Q_BLOCK, width - w), F32)], axis=1)
        rows = pltpu.roll(rows, 0, 1, stride=1, stride_axis=0)
        line_ref[j:j + 1, :] = jnp.sum(rows, axis=0, keepdims=True)


def _row_tile(rows):
    return max(t for t in range(16, 385, 16) if rows % t == 0)


def _cast_rows(w, out_dtype, name):
    r, c = w.shape
    tr = _row_tile(r)

    def body(w_ref, o_ref):
        o_ref[...] = w_ref[...].astype(out_dtype)

    spec = pl.BlockSpec((tr, c), lambda i: (i, 0))
    return pl.pallas_call(
        body, name=name, grid=(r // tr,), in_specs=[spec], out_specs=spec,
        out_shape=jax.ShapeDtypeStruct((r, c), out_dtype), compiler_params=_params("arbitrary"),
    )(w)


def _gather_weights(wt_shard, small_shard):
    bufs = ((W_IN_SHARD, IN_WIDTH), (SMALL_ROWS, N_CHIPS * SMALL_ROWS))

    stage_rows = 528

    def body(wt_in, sm_in, wt_out, sm_out, send_sems, recv_sems, in_sems, out_sems, stage):
        x, y, c = lax.axis_index("x"), lax.axis_index("y"), lax.axis_index("c")
        sibling = (x, y, 1 - c)
        my_chip = 2 * x + y
        refs = ((wt_in, wt_out), (sm_in, sm_out))

        def keep_own():
            pieces = [(b, r0) for b in range(2) for r0 in range(0, bufs[b][0], stage_rows)]
            outs = []
            for i, (b, r0) in enumerate(pieces):
                rows = min(stage_rows, bufs[b][0] - r0)
                slot = i % 2
                if i >= 2:
                    outs[i - 2].wait()
                buf = stage.at[slot, pl.ds(0, rows), :]
                load = pltpu.make_async_copy(refs[b][0].at[pl.ds(r0, rows), :], buf, in_sems.at[slot])
                load.start()
                load.wait()
                start = pl.multiple_of(my_chip * bufs[b][0] + r0, 16)
                outs.append(pltpu.make_async_copy(buf, refs[b][1].at[pl.ds(start, rows), :], out_sems.at[slot]))
                outs[i].start()
            for cp in outs[-2:]:
                cp.wait()

        def half_of(b, chip, half):
            rows = bufs[b][0]
            start = pl.multiple_of(chip * rows + half * (rows // 2), 16)
            return refs[b][1].at[pl.ds(start, rows // 2), :]

        def copy(k, src, dst, to):
            return pltpu.make_async_remote_copy(src_ref=src, dst_ref=dst, send_sem=send_sems.at[k],
                                                recv_sem=recv_sems.at[k], device_id=to, device_id_type=MESH)

        near = (x + (1 - c) - 2 * x * (1 - c), y + c - 2 * y * c)
        far = (x + c - 2 * x * c, y + (1 - c) - 2 * y * (1 - c))
        diag = (1 - x, 1 - y)
        chip_no = lambda chip: 2 * chip[0] + chip[1]
        sends, passed = [], []
        for b in range(2):
            rows = bufs[b][0]
            src = refs[b][0].at[pl.ds(pl.multiple_of(c * (rows // 2), 16), rows // 2), :]
            sends += [copy(3 * b, src, half_of(b, my_chip, c), (*near, c)),
                      copy(3 * b + 1, src, half_of(b, my_chip, c), (*far, c))]
        for cp in sends:
            cp.start()
        keep_own()

        def pass_on(b, j, chip):
            landed = half_of(b, chip_no(chip), c)
            fwd = copy(6 + 3 * b + j, landed, landed, sibling)
            fwd.start()
            passed.append(fwd)

        for b in range(2):
            landed = half_of(b, chip_no(near), c)
            copy(3 * b, landed, landed, sibling).wait_recv()
            relay = copy(3 * b + 2, landed, landed, (*far, c))
            relay.start()
            sends.append(relay)
            pass_on(b, 0, near)
        for b in range(2):
            for j, chip in ((1, far), (2, diag)):
                landed = half_of(b, chip_no(chip), c)
                copy(3 * b + j, landed, landed, sibling).wait_recv()
                pass_on(b, j, chip)
        for b in range(2):
            for j, chip in ((0, far), (1, near), (2, diag)):
                other = half_of(b, chip_no(chip), 1 - c)
                copy(6 + 3 * b + j, other, other, sibling).wait_recv()
        for cp in sends + passed:
            cp.wait_send()

    return pl.pallas_call(
        body, name="gather_weights",
        in_specs=[ANY, ANY], out_specs=[ANY, ANY],
        out_shape=[jax.ShapeDtypeStruct((bufs[0][1], D_MODEL), BF16),
                   jax.ShapeDtypeStruct((bufs[1][1], D_MODEL), BF16)],
        scratch_shapes=[pltpu.SemaphoreType.DMA((12,)), pltpu.SemaphoreType.DMA((12,)),
                        pltpu.SemaphoreType.DMA((2,)), pltpu.SemaphoreType.DMA((2,)),
                        pltpu.VMEM((2, stage_rows, D_MODEL), BF16)],
    )(wt_shard, small_shard)


W_BLOCK = 768


def _w_blocks(first, count):
    return [pl.BlockSpec((W_BLOCK, D_MODEL), lambda *_, k=k: (first + k, 0)) for k in range(count)]


def _in_proj(x, gain, w_t, first_block, n_blocks, out_dtype, name, keep_h):
    tm = 512

    def body(x_ref, g_ref, *refs):
        w_refs, outs = refs[:n_blocks], refs[n_blocks:]
        xf = x_ref[...]
        r = lax.rsqrt(jnp.mean(xf * xf, axis=-1, keepdims=True) + EPS)
        h = ((xf * r) * g_ref[...]).astype(BF16)
        if keep_h:
            outs[1][...] = h
        for k, w_ref in enumerate(w_refs):
            outs[0][:, k * W_BLOCK:(k + 1) * W_BLOCK] = _dot(h, w_ref[...], NT).astype(out_dtype)

    return pl.pallas_call(
        body, name=name, grid=(SEQ // tm,),
        in_specs=[pl.BlockSpec((tm, D_MODEL), lambda i: (i, 0)), pl.BlockSpec((1, D_MODEL), lambda i: (0, 0))]
        + _w_blocks(first_block, n_blocks),
        out_specs=[pl.BlockSpec((tm, W_BLOCK * n_blocks), lambda i: (i, 0)),
                   pl.BlockSpec((tm, D_MODEL), lambda i: (i, 0))][:2 if keep_h else 1],
        out_shape=[jax.ShapeDtypeStruct((SEQ, W_BLOCK * n_blocks), out_dtype),
                   jax.ShapeDtypeStruct((SEQ, D_MODEL), BF16)][:2 if keep_h else 1],
        compiler_params=_params("arbitrary"),
    )(x, gain, *([w_t] * n_blocks))


CHUNK = 256
CHUNK_UNROLL = 4
TILE_UNROLL = 8


def _low_half():
    return lax.broadcasted_iota(jnp.int32, (1, LANES), 1) < HEAD_DIM


def _half_sum(v, low):
    del low
    row = lax.broadcasted_iota(jnp.int32, (2 * LANES, LANES), 0)
    col = lax.broadcasted_iota(jnp.int32, (2 * LANES, LANES), 1)
    ones = jnp.where((row % LANES) // HEAD_DIM == col // HEAD_DIM, 1.0, 0.0).astype(BF16)
    hi = v.astype(BF16)
    lo = (v - hi.astype(F32)).astype(BF16)
    return _dot(jnp.concatenate([hi, lo], axis=1), ones)


def _chunks(fn, init=0):
    def body(i, carry):
        for u in range(CHUNK_UNROLL):
            carry = fn(pl.multiple_of((i * CHUNK_UNROLL + u) * CHUNK, CHUNK), carry)
        return carry

    return lax.fori_loop(0, SEQ // (CHUNK * CHUNK_UNROLL), body, init)


def _inv_rms(t, low):
    return lax.rsqrt(_half_sum(t * t, low) * (1.0 / HEAD_DIM) + EPS)


def _prep_q(q_ref, gain_ref, qn_ref):
    low = _low_half()

    def step(r0, carry):
        q = q_ref[pl.ds(r0, CHUNK), :].astype(F32)
        qn_ref[pl.ds(r0, CHUNK), :] = ((q * _inv_rms(q, low)) * gain_ref[...]) * SCALE
        return carry

    _chunks(step)


def _own_half(t, keep):
    return jnp.where(keep, t, pltpu.roll(t, HEAD_DIM, 1))


def _prep_kv(k_ref, v_ref, gain_ref, kp_ref, vp_ref, pad, keep=None):
    low = _low_half()
    zeros = jnp.zeros((pad, LANES), F32)
    for ref in (kp_ref, vp_ref):
        ref[pl.ds(0, pad), :] = zeros
        ref[pl.ds(pad + SEQ, pad), :] = zeros

    def step(r0, carry):
        k = k_ref[pl.ds(r0, CHUNK), :].astype(F32)
        v = v_ref[pl.ds(r0, CHUNK), :].astype(F32)
        kn = (k * _inv_rms(k, low)) * gain_ref[...]
        if keep is not None:
            kn, v = _own_half(kn, keep), _own_half(v, keep)
        kp_ref[pl.ds(pad + r0, CHUNK), :] = kn
        vp_ref[pl.ds(pad + r0, CHUNK), :] = v
        return carry

    _chunks(step)


def _tiles(d, half_window, fn):
    w = Q_BLOCK + 2 * half_window
    length = SEQ // d
    n_blocks = length // Q_BLOCK
    col = lax.broadcasted_iota(jnp.int32, (1, w), 1)

    def step(it, carry):
        c, n = it // n_blocks, it % n_blocks
        start = c + (d * Q_BLOCK) * n
        if d == 1:
            start = pl.multiple_of(start, Q_BLOCK)
            q_rows, k_rows = pl.ds(start, Q_BLOCK), pl.ds(start, w)
        else:
            q_rows, k_rows = pl.ds(start, Q_BLOCK, stride=d), pl.ds(start, w, stride=d)
        t = n * Q_BLOCK - half_window + col
        edge = jnp.where((t < 0) | (t >= length), NEG_INF, 0.0)
        fn(q_rows, k_rows, edge)
        return carry

    lax.fori_loop(0, d * n_blocks, step, 0, unroll=TILE_UNROLL)


def _stack_heads(t, low):
    return jnp.concatenate([jnp.where(low, t, 0.0), jnp.where(low, 0.0, t)], axis=0).astype(BF16)


def _unstack_heads(t, low):
    return jnp.where(low, t[:Q_BLOCK], t[Q_BLOCK:])


def _per_head(pair):
    return jnp.concatenate([jnp.full((Q_BLOCK, 1), pair[0], F32), jnp.full((Q_BLOCK, 1), pair[1], F32)], axis=0)


def _fwd_tiles(qn_ref, kp_ref, vp_ref, bias_ref, emit, *, d, half_window, sinks=None):
    low = _low_half()
    w = Q_BLOCK + 2 * half_window
    sink = None if sinks is None else _per_head(sinks)

    def tile(q_rows, k_rows, edge):
        q2 = _stack_heads(qn_ref[q_rows, :], low)
        k = kp_ref[k_rows, :].astype(BF16)
        v1 = jnp.concatenate([vp_ref[k_rows, :], jnp.ones((w, LANES), F32)], axis=1).astype(BF16)
        s = _dot(q2, k, NT) + bias_ref[...] + edge
        m = jnp.max(s, axis=-1, keepdims=True)
        if sink is not None:
            m = jnp.maximum(m, sink)
        o = _dot(jnp.exp(s - m).astype(BF16), v1)
        l = o[:, LANES:]
        if sink is not None:
            l = l + jnp.exp(sink - m)
        emit(q_rows, _unstack_heads(o[:, :LANES] * (1.0 / l), low), _unstack_heads(m + jnp.log(l), low))

    _tiles(d, half_window, tile)


def _bwd_tiles(qn_ref, kp_ref, vp_ref, bias_ref, do_ref, lse_ref, delta_ref, dq_ref, dk_ref, dv_ref, ds_ref,
               *, d, half_window, sinks=None, dsink_ref=None):
    low = _low_half()
    w = Q_BLOCK + 2 * half_window
    sink = None if sinks is None else _per_head(sinks)

    def rows_of(t):
        return jnp.concatenate([t[:, 0:1], t[:, HEAD_DIM:HEAD_DIM + 1]], axis=0)

    def tile(q_rows, k_rows, edge):
        q2 = _stack_heads(qn_ref[q_rows, :], low)
        do2 = _stack_heads(do_ref[q_rows, :], low)
        k = kp_ref[k_rows, :].astype(BF16)
        v = vp_ref[k_rows, :].astype(BF16)
        lse = rows_of(lse_ref[q_rows, :])
        delta = rows_of(delta_ref[q_rows, :])
        p = jnp.exp(_dot(q2, k, NT) + bias_ref[...] + edge - lse)
        ds = p * (_dot(do2, v, NT) - delta)
        ds_ref[...] += ds
        if sink is not None:
            dsink_ref[...] += (-jnp.exp(sink - lse) * delta).reshape(2, Q_BLOCK, 1)
        dsb, pb = ds.astype(BF16), p.astype(BF16)
        dq_ref[q_rows, :] = _unstack_heads(_dot(dsb, k), low)
        dk_ref[k_rows, :] += _dot(dsb, q2, TN)
        dv_ref[k_rows, :] += _dot(pb, do2, TN)

    _tiles(d, half_window, tile)


def _norm_bwd(raw_ref, gain_ref, dn_ref, dn_offset, out_ref, scale):
    low = _low_half()

    def step(r0, dgain):
        t = raw_ref[pl.ds(r0, CHUNK), :].astype(F32)
        dn = dn_ref[pl.ds(dn_offset + r0, CHUNK), :]
        dth = dn * (gain_ref[...] * scale)
        sums = _half_sum(jnp.concatenate([t * t, dth * t], axis=0), low)
        r = lax.rsqrt(sums[:CHUNK] * (1.0 / HEAD_DIM) + EPS)
        th = t * r
        out_ref[pl.ds(r0, CHUNK), :] = (r * (dth - th * (r * sums[CHUNK:] * (1.0 / HEAD_DIM)))).astype(BF16)
        return dgain + jnp.sum(dn * th, axis=0, keepdims=True) * scale

    return _chunks(step, jnp.zeros((1, LANES), F32))


def _rows8(v):
    return jnp.broadcast_to(v, (8, v.shape[-1]))


A_W = Q_BLOCK + 2 * A_HALF_WINDOW
A_PAD = A_HALF_WINDOW


def _seq_block(col_fn):
    return pl.BlockSpec((SEQ, LANES), col_fn)


def _attn_a_fwd(qkv, gain_q, gain_k, bias, sink):
    def body(sink_ref, q_ref, k_ref, v_ref, gq_ref, gk_ref, line_ref, o_ref, lse_ref, qn_ref, kp_ref, vp_ref,
             bias_ref):
        hp = pl.program_id(0)
        keep = (lax.broadcasted_iota(jnp.int32, (1, LANES), 1) // HEAD_DIM) == hp // 2
        _prep_q(q_ref, gq_ref, qn_ref)
        _prep_kv(k_ref, v_ref, gk_ref, kp_ref, vp_ref, A_PAD, keep)
        _unroll_bias(line_ref, bias_ref, A_W)

        def emit(rows, out, lse):
            o_ref[rows, :] = out
            lse_ref[rows, :] = lse

        _fwd_tiles(qn_ref, kp_ref, vp_ref, bias_ref, emit, d=1, half_window=A_HALF_WINDOW,
                   sinks=(sink_ref[2 * hp], sink_ref[2 * hp + 1]))

    vec = pl.BlockSpec((1, LANES), lambda hp, s: (0, 0))
    return pl.pallas_call(
        body, name="attn_a_fwd",
        grid_spec=pltpu.PrefetchScalarGridSpec(
            num_scalar_prefetch=1, grid=(4,),
            in_specs=[_seq_block(lambda hp, s: (0, QA_BLK + hp)), _seq_block(lambda hp, s: (0, KA_BLK)),
                      _seq_block(lambda hp, s: (0, VA_BLK)), vec, vec,
                      pl.BlockSpec((None, 2, _line_width(A_HALF_WINDOW)), lambda hp, s: (hp, 0, 0))],
            out_specs=[_seq_block(lambda hp, s: (0, hp)), _seq_block(lambda hp, s: (0, hp))],
            scratch_shapes=[pltpu.VMEM((SEQ, LANES), F32), pltpu.VMEM((SEQ + 2 * A_PAD, LANES), F32),
                            pltpu.VMEM((SEQ + 2 * A_PAD, LANES), F32), pltpu.VMEM((2 * Q_BLOCK, A_W), F32)]),
        out_shape=[jax.ShapeDtypeStruct((SEQ, 512), F32)] * 2,
        compiler_params=_params("arbitrary"),
    )(sink.reshape(8), qkv, qkv, qkv, gain_q, gain_k, bias)


def _attn_a_bwd(qkv, gain_q, gain_k, bias, sink, delta, lse, d_out):
    def body(sink_ref, q_ref, k_ref, v_ref, gq_ref, gk_ref, line_ref, delta_ref, lse_ref, do_ref,
             dq_out, dkv_out, dgq_out, dgk_out, dline_out, dsink_out,
             qn_ref, kp_ref, vp_ref, dq_ref, dk_ref, dv_ref, dk_tot, dv_tot, bias_ref, ds_out):
        hp = pl.program_id(0)
        kv_head = hp // 2
        keep = (lax.broadcasted_iota(jnp.int32, (1, LANES), 1) // HEAD_DIM) == kv_head
        _prep_q(q_ref, gq_ref, qn_ref)
        _prep_kv(k_ref, v_ref, gk_ref, kp_ref, vp_ref, A_PAD, keep)
        _unroll_bias(line_ref, bias_ref, A_W)
        dk_ref[...] = jnp.zeros_like(dk_ref)
        dv_ref[...] = jnp.zeros_like(dv_ref)
        ds_out[...] = jnp.zeros_like(ds_out)
        dsink_out[...] = jnp.zeros_like(dsink_out)

        @pl.when(hp == 0)
        def _():
            dk_tot[...] = jnp.zeros_like(dk_tot)
            dv_tot[...] = jnp.zeros_like(dv_tot)

        _bwd_tiles(qn_ref, kp_ref, vp_ref, bias_ref, do_ref, lse_ref, delta_ref, dq_ref, dk_ref, dv_ref, ds_out,
                   d=1, half_window=A_HALF_WINDOW, sinks=(sink_ref[2 * hp], sink_ref[2 * hp + 1]),
                   dsink_ref=dsink_out)
        _fold_bias_grad(ds_out, dline_out, A_W)
        dgq_out[...] = _rows8(_norm_bwd(q_ref, gq_ref, dq_ref, 0, dq_out, SCALE))

        def fold(r0, carry):
            rows = pl.ds(A_PAD + r0, CHUNK)
            for acc, tot in ((dk_ref, dk_tot), (dv_ref, dv_tot)):
                t = acc[rows, :]
                tot[pl.ds(r0, CHUNK), :] += jnp.where(keep, t + pltpu.roll(t, HEAD_DIM, 1), 0.0)
            return carry

        _chunks(fold)

        @pl.when(hp == 3)
        def _():
            dgk_out[...] = _rows8(_norm_bwd(k_ref, gk_ref, dk_tot, 0, dkv_out.at[0], 1.0))
            dkv_out[1] = dv_tot[...].astype(BF16)

    vec = pl.BlockSpec((1, LANES), lambda hp, s: (0, 0))
    seq_f32 = pltpu.VMEM((SEQ, LANES), F32)
    padded = pltpu.VMEM((SEQ + 2 * A_PAD, LANES), F32)
    return pl.pallas_call(
        body, name="attn_a_bwd",
        grid_spec=pltpu.PrefetchScalarGridSpec(
            num_scalar_prefetch=1, grid=(4,),
            in_specs=[_seq_block(lambda hp, s: (0, QA_BLK + hp)), _seq_block(lambda hp, s: (0, KA_BLK)),
                      _seq_block(lambda hp, s: (0, VA_BLK)), vec, vec,
                      pl.BlockSpec((None, 2, _line_width(A_HALF_WINDOW)), lambda hp, s: (hp, 0, 0)),
                      _seq_block(lambda hp, s: (0, hp)), _seq_block(lambda hp, s: (0, hp)),
                      _seq_block(lambda hp, s: (0, hp))],
            out_specs=[pl.BlockSpec((None, SEQ, LANES), lambda hp, s: (hp, 0, 0)),
                       pl.BlockSpec((2, SEQ, LANES), lambda hp, s: (0, 0, 0)),
                       pl.BlockSpec((None, 8, LANES), lambda hp, s: (hp, 0, 0)),
                       pl.BlockSpec((8, LANES), lambda hp, s: (0, 0)),
                       pl.BlockSpec((None, 2, _line_width(A_HALF_WINDOW)), lambda hp, s: (hp, 0, 0)),
                       pl.BlockSpec((None, 2, Q_BLOCK, 1), lambda hp, s: (hp, 0, 0, 0))],
            scratch_shapes=[seq_f32, padded, padded, seq_f32, padded, padded, seq_f32, seq_f32,
                            pltpu.VMEM((2 * Q_BLOCK, A_W), F32), pltpu.VMEM((2 * Q_BLOCK, A_W), F32)]),
        out_shape=[jax.ShapeDtypeStruct((4, SEQ, LANES), BF16), jax.ShapeDtypeStruct((2, SEQ, LANES), BF16),
                   jax.ShapeDtypeStruct((4, 8, LANES), F32), jax.ShapeDtypeStruct((8, LANES), F32),
                   jax.ShapeDtypeStruct((4, 2, _line_width(A_HALF_WINDOW)), F32),
                   jax.ShapeDtypeStruct((4, 2, Q_BLOCK, 1), F32)],
        compiler_params=_params("arbitrary"),
    )(sink.reshape(8), qkv, qkv, qkv, gain_q, gain_k, bias, delta, lse, d_out)


B_W = Q_BLOCK + 2 * B_HALF_WINDOW
B_PAD_MAX = B_HALF_WINDOW * B_DILATIONS[-1]


def _attn_b_fwd(qkv, gain_q, gain_k, bias):
    def body(q_ref, k_ref, v_ref, gq_ref, gk_ref, line_ref, o_ref, lse_ref, qn_ref, kp_ref, vp_ref, bias_ref):
        g = pl.program_id(1)
        _prep_q(q_ref, gq_ref, qn_ref)
        _unroll_bias(line_ref, bias_ref, B_W)

        def first(rows, out, lse):
            o_ref[rows, :] = out
            lse_ref[rows, :] = lse

        def combine(rows, out, lse):
            old = lse_ref[rows, :]
            new = jnp.maximum(old, lse) + jnp.log(1.0 + jnp.exp(-jnp.abs(old - lse)))
            o_ref[rows, :] = o_ref[rows, :] * jnp.exp(old - new) + out * jnp.exp(lse - new)
            lse_ref[rows, :] = new

        for gi, d in enumerate(B_DILATIONS):
            @pl.when(g == gi)
            def _():
                _prep_kv(k_ref, v_ref, gk_ref, kp_ref, vp_ref, B_HALF_WINDOW * d)
                _fwd_tiles(qn_ref, kp_ref, vp_ref, bias_ref, first if gi == 0 else combine,
                           d=d, half_window=B_HALF_WINDOW)

    vec = pl.BlockSpec((1, LANES), lambda hp, g: (0, 0))
    padded = pltpu.VMEM((SEQ + 2 * B_PAD_MAX, LANES), F32)
    return pl.pallas_call(
        body, name="attn_b_fwd", grid=(4, 3),
        in_specs=[_seq_block(lambda hp, g: (0, QB_BLK + 4 * g + hp)), _seq_block(lambda hp, g: (0, KB_BLK + 4 * g + hp)),
                  _seq_block(lambda hp, g: (0, VB_BLK + 4 * g + hp)), vec, vec,
                  pl.BlockSpec((None, 2, _line_width(B_HALF_WINDOW)), lambda hp, g: (4 * g + hp, 0, 0))],
        out_specs=[_seq_block(lambda hp, g: (0, hp)), _seq_block(lambda hp, g: (0, hp))],
        out_shape=[jax.ShapeDtypeStruct((SEQ, 512), F32)] * 2,
        scratch_shapes=[pltpu.VMEM((SEQ, LANES), F32), padded, padded, pltpu.VMEM((2 * Q_BLOCK, B_W), F32)],
        compiler_params=_params("arbitrary", "arbitrary"),
    )(qkv, qkv, qkv, gain_q, gain_k, bias)


def _attn_b_bwd(qkv, gain_q, gain_k, bias, delta, lse, d_out):
    def body(q_ref, k_ref, v_ref, gq_ref, gk_ref, line_ref, delta_ref, lse_ref, do_ref,
             dq_out, dk_out, dv_out, dgq_out, dgk_out, dline_out,
             qn_ref, kp_ref, vp_ref, dq_ref, dk_ref, dv_ref, bias_ref, ds_out):
        g = pl.program_id(1)
        _prep_q(q_ref, gq_ref, qn_ref)
        _unroll_bias(line_ref, bias_ref, B_W)
        ds_out[...] = jnp.zeros_like(ds_out)
        for gi, d in enumerate(B_DILATIONS):
            @pl.when(g == gi)
            def _():
                pad = B_HALF_WINDOW * d
                for acc in (dk_ref, dv_ref):
                    acc[pl.ds(0, SEQ + 2 * pad), :] = jnp.zeros((SEQ + 2 * pad, LANES), F32)
                _prep_kv(k_ref, v_ref, gk_ref, kp_ref, vp_ref, pad)
                _bwd_tiles(qn_ref, kp_ref, vp_ref, bias_ref, do_ref, lse_ref, delta_ref, dq_ref, dk_ref, dv_ref,
                           ds_out, d=d, half_window=B_HALF_WINDOW)
                dgk_out[...] = _rows8(_norm_bwd(k_ref, gk_ref, dk_ref, pad, dk_out, 1.0))
                dv_out[...] = dv_ref[pl.ds(pad, SEQ), :].astype(BF16)
        _fold_bias_grad(ds_out, dline_out, B_W)
        dgq_out[...] = _rows8(_norm_bwd(q_ref, gq_ref, dq_ref, 0, dq_out, SCALE))

    vec = pl.BlockSpec((1, LANES), lambda hp, g: (0, 0))
    seq_f32 = pltpu.VMEM((SEQ, LANES), F32)
    padded = pltpu.VMEM((SEQ + 2 * B_PAD_MAX, LANES), F32)
    part = pl.BlockSpec((None, 8, LANES), lambda hp, g: (4 * g + hp, 0, 0))
    line = pl.BlockSpec((None, 2, _line_width(B_HALF_WINDOW)), lambda hp, g: (4 * g + hp, 0, 0))
    return pl.pallas_call(
        body, name="attn_b_bwd", grid=(4, 3),
        in_specs=[_seq_block(lambda hp, g: (0, QB_BLK + 4 * g + hp)), _seq_block(lambda hp, g: (0, KB_BLK + 4 * g + hp)),
                  _seq_block(lambda hp, g: (0, VB_BLK + 4 * g + hp)), vec, vec,
                  line,
                  _seq_block(lambda hp, g: (0, hp)), _seq_block(lambda hp, g: (0, hp)), _seq_block(lambda hp, g: (0, hp))],
        out_specs=[pl.BlockSpec((None, SEQ, LANES), lambda hp, g: (4 * g + hp, 0, 0))] * 3 + [part, part, line],
        out_shape=[jax.ShapeDtypeStruct((12, SEQ, LANES), BF16)] * 3
        + [jax.ShapeDtypeStruct((12, 8, LANES), F32)] * 2
        + [jax.ShapeDtypeStruct((12, 2, _line_width(B_HALF_WINDOW)), F32)],
        scratch_shapes=[seq_f32, padded, padded, seq_f32, padded, padded,
                        pltpu.VMEM((2 * Q_BLOCK, B_W), F32), pltpu.VMEM((2 * Q_BLOCK, B_W), F32)],
        compiler_params=_params("arbitrary", "arbitrary"),
    )(qkv, qkv, qkv, gain_q, gain_k, bias, delta, lse, d_out)


def _sigmoid(t):
    return 1.0 / (1.0 + jnp.exp(-t))


def _middle(out_a, out_b, gates, x, target, w_a, w_b, w_out, b_merge):
    tm = 256
    n_steps = SEQ // tm

    def body(oa_ref, ob_ref, g_ref, x_ref, t_ref, wa_ref, wb_ref, wo_ref, bm_ref,
             dy_ref, dg_ref, doa_ref, dob_ref, dla_ref, dlb_ref, dwa_ref, dwb_ref, dwo_ref, dbm_ref, sq_ref):
        @pl.when(pl.program_id(0) == 0)
        def _():
            for ref in (dwa_ref, dwb_ref, dwo_ref, dbm_ref, sq_ref):
                ref[...] = jnp.zeros_like(ref)

        gate_a, gate_b = g_ref[:, 0:512], g_ref[:, 512:1024]
        sig_a, sig_b = _sigmoid(gate_a), _sigmoid(gate_b)
        silu_a, silu_b = gate_a * sig_a, gate_b * sig_b
        oa, ob = oa_ref[...], ob_ref[...]
        ya, yb = (oa * silu_a).astype(BF16), (ob * silu_b).astype(BF16)
        br_a, br_b = _dot(ya, wa_ref[...]), _dot(yb, wb_ref[...])
        m0 = _sigmoid(g_ref[:, 1024:2048] + bm_ref[0:1, :])
        m1 = _sigmoid(g_ref[:, 2048:3072] + bm_ref[1:2, :])
        merged = (m0 * br_a + m1 * br_b).astype(BF16)
        err = (x_ref[...] + _dot(merged, wo_ref[...])) - t_ref[...]
        sq_ref[...] += jnp.sum(err * err, axis=0, keepdims=True)

        dy = err * (1.0 / D_MODEL)
        dy_ref[...] = dy
        dyb = dy.astype(BF16)
        dmerged = _dot(dyb, wo_ref[...], NT)
        dwo_ref[...] += _dot(merged, dyb, TN)
        dbr_a, dbr_b = (dmerged * m0).astype(BF16), (dmerged * m1).astype(BF16)
        dm0 = (dmerged * br_a) * (m0 * (1.0 - m0))
        dm1 = (dmerged * br_b) * (m1 * (1.0 - m1))
        dbm_ref[0:1, :] += jnp.sum(dm0, axis=0, keepdims=True)
        dbm_ref[1:2, :] += jnp.sum(dm1, axis=0, keepdims=True)
        for s in range(N_CHIPS):
            cols = slice(256 * s, 256 * (s + 1))
            dwa_ref[s] += _dot(ya, dbr_a[:, cols], TN)
            dwb_ref[s] += _dot(yb, dbr_b[:, cols], TN)
        dya, dyb_ = _dot(dbr_a, wa_ref[...], NT), _dot(dbr_b, wb_ref[...], NT)
        doa, dob = dya * silu_a, dyb_ * silu_b
        doa_ref[...] = doa
        dob_ref[...] = dob
        for blk in range(512 // LANES):
            lanes = slice(blk * LANES, (blk + 1) * LANES)
            dla_ref[:, lanes] = _half_sum(doa[:, lanes] * oa[:, lanes], None)
            dlb_ref[:, lanes] = _half_sum(dob[:, lanes] * ob[:, lanes], None)
        d_gates = (((dya * oa) * (sig_a * (1.0 + gate_a * (1.0 - sig_a)))).astype(BF16),
                   ((dyb_ * ob) * (sig_b * (1.0 + gate_b * (1.0 - sig_b)))).astype(BF16),
                   dm0.astype(BF16), dm1.astype(BF16))
        blk = 0
        for part in d_gates:
            for c0 in range(0, part.shape[1], 256):
                dg_ref[blk] = part[:, c0:c0 + 256]
                blk += 1

    def rows(width):
        return pl.BlockSpec((tm, width), lambda i: (i, 0))

    def whole(*shape):
        return pl.BlockSpec(shape, lambda i: (0,) * len(shape))

    return pl.pallas_call(
        body, name="middle", grid=(n_steps,),
        in_specs=[rows(512), rows(512), rows(GATE_WIDTH), rows(D_MODEL), rows(D_MODEL),
                  whole(512, D_MODEL), whole(512, D_MODEL), whole(D_MODEL, D_MODEL), whole(2, D_MODEL)],
        out_specs=[rows(D_MODEL), pl.BlockSpec((GATE_WIDTH // 256, tm, 256), lambda i: (0, i, 0)),
                   rows(512), rows(512), rows(512), rows(512),
                   whole(N_CHIPS, 512, 256), whole(N_CHIPS, 512, 256), whole(D_MODEL, D_MODEL),
                   whole(2, D_MODEL), whole(1, D_MODEL)],
        out_shape=[jax.ShapeDtypeStruct((SEQ, D_MODEL), F32), jax.ShapeDtypeStruct((GATE_WIDTH // 256, SEQ, 256), BF16),
                   jax.ShapeDtypeStruct((SEQ, 512), F32), jax.ShapeDtypeStruct((SEQ, 512), F32),
                   jax.ShapeDtypeStruct((SEQ, 512), F32), jax.ShapeDtypeStruct((SEQ, 512), F32),
                   jax.ShapeDtypeStruct((N_CHIPS, 512, 256), F32), jax.ShapeDtypeStruct((N_CHIPS, 512, 256), F32),
                   jax.ShapeDtypeStruct((D_MODEL, D_MODEL), F32), jax.ShapeDtypeStruct((2, D_MODEL), F32),
                   jax.ShapeDtypeStruct((1, D_MODEL), F32)],
        compiler_params=_params("arbitrary"),
    )(out_a, out_b, gates, x, target, w_a, w_b, w_out, b_merge)


def _which(j, edges, fns):
    lo = 0
    for hi, fn in zip(edges, fns):
        pl.when((j >= lo) & (j < hi))(fn)
        lo = hi


def _d_w_in(d_proj, h):
    plan, step, width = [], 0, 0
    for p in d_proj:
        total = p.shape[0] * p.shape[2]
        if width + total <= W_BLOCK:
            plan.append((p.shape[0], step, 1))
            width += total
            if width == W_BLOCK:
                step, width = step + 1, 0
        else:
            assert width == 0 and total % W_BLOCK == 0
            plan.append((W_BLOCK // p.shape[2], step, total // W_BLOCK))
            step += total // W_BLOCK
    assert width == 0 and step == IN_WIDTH // W_BLOCK
    firsts = sorted({first for _, first, _ in plan})
    edges = firsts[1:] + [step]
    halves = 2

    def body(*refs):
        pieces, h_ref, o_ref, acc_ref = refs[:-3], refs[-3], refs[-2], refs[-1]
        k = pl.program_id(1)

        def emit(group):
            def fn():
                cols = jnp.concatenate([ref[b] for ref in group for b in range(ref.shape[0])], axis=1)
                term = _dot(cols, h_ref[...], TN)

                @pl.when(k == 0)
                def _():
                    acc_ref[...] = term

                @pl.when(k == halves - 1)
                def _():
                    o_ref[...] = (acc_ref[...] + term).astype(BF16)
            return fn

        groups = [[ref for ref, (_, first, _) in zip(pieces, plan) if first == f] for f in firsts]
        _which(pl.program_id(0), edges, [emit(group) for group in groups])

    def cols_spec(piece, n, first, steps):
        def index(j, k):
            return jnp.clip(j - first, 0, steps - 1), jnp.where((j >= first) & (j < first + steps), k, 0), 0
        return pl.BlockSpec((n, SEQ // halves, piece.shape[2]), index)

    return pl.pallas_call(
        body, name="d_w_in", grid=(step, halves),
        in_specs=[cols_spec(p, *pl_) for p, pl_ in zip(d_proj, plan)]
        + [pl.BlockSpec((SEQ // halves, D_MODEL), lambda j, k: (k, 0))],
        out_specs=pl.BlockSpec((W_BLOCK, D_MODEL), lambda j, k: (j, 0)),
        out_shape=jax.ShapeDtypeStruct((IN_WIDTH, D_MODEL), BF16),
        scratch_shapes=[pltpu.VMEM((W_BLOCK, D_MODEL), F32)],
        compiler_params=_params("arbitrary", "arbitrary"),
    )(*d_proj, h)


RELAY_STEP = 10
RELAY_ROWS = 352


def _d_x(d_proj, w_t, x, gain, dy, chip_sums):
    tm = 256
    n_steps = SEQ // tm
    n_w = IN_WIDTH // W_BLOCK
    n_p, n_s = len(d_proj), len(chip_sums)

    def body(*refs):
        pieces, w_refs = refs[:n_p], refs[n_p:n_p + n_w]
        x_ref, g_ref, dy_ref = refs[n_p + n_w:n_p + n_w + 3]
        q_refs = refs[n_p + n_w + 3:n_p + n_w + 3 + n_s]
        dx_ref, dgain_ref = refs[n_p + n_w + 3 + n_s:n_p + n_w + 5 + n_s]
        outs = refs[n_p + n_w + 5 + n_s:n_p + n_w + 5 + 4 * n_s]
        got_refs, relay_refs, sum_refs = outs[:n_s], outs[n_s:2 * n_s], outs[2 * n_s:]
        if n_s:
            send_sems, recv_sems, local_sems, a_buf, b_buf, c_buf = refs[n_p + n_w + 5 + 4 * n_s:]

        def hops():
            cx, cy, c = lax.axis_index("x"), lax.axis_index("y"), lax.axis_index("c")
            near = (cx + (1 - c) - 2 * cx * (1 - c), cy + c - 2 * cy * c)
            far = (cx + c - 2 * cx * c, cy + (1 - c) - 2 * cy * (1 - c))
            chip = lambda p: 2 * p[0] + p[1]

            def copy(k, src, dst, to):
                return pltpu.make_async_remote_copy(src_ref=src, dst_ref=dst, send_sem=send_sems.at[k],
                                                    recv_sem=recv_sems.at[k], device_id=(*to, c), device_id_type=MESH)

            first = [(copy(3 * b, q.at[chip(near)], got.at[0], near),
                      copy(3 * b + 1, q.at[3 - chip((cx, cy))], relay, near))
                     for b, (q, got, relay) in enumerate(zip(q_refs, got_refs, relay_refs))]
            second = [copy(3 * b + 2, s, got.at[1], far) for b, (s, got) in enumerate(zip(sum_refs, got_refs))]
            return first, second, chip(far)

        @pl.when(pl.program_id(0) == 0)
        def _():
            dgain_ref[...] = jnp.zeros_like(dgain_ref)
            if n_s:
                for direct, pass_on in hops()[0]:
                    direct.start()
                    pass_on.start()

        if n_s:
            @pl.when(pl.program_id(0) == RELAY_STEP)
            def _():
                first, second, far_chip = hops()
                for b, (q, relay, total) in enumerate(zip(q_refs, relay_refs, sum_refs)):
                    first[b][1].wait_recv()
                    half = relay.shape[0]
                    for r0 in range(0, half, RELAY_ROWS):
                        rows = min(RELAY_ROWS, half - r0)
                        mine = pltpu.make_async_copy(q.at[far_chip, pl.ds(r0, rows), :], a_buf.at[pl.ds(0, rows), :],
                                                     local_sems.at[0])
                        theirs = pltpu.make_async_copy(relay.at[pl.ds(r0, rows), :], b_buf.at[pl.ds(0, rows), :],
                                                       local_sems.at[1])
                        mine.start()
                        theirs.start()
                        mine.wait()
                        theirs.wait()
                        c_buf[0:rows, :] = (a_buf[0:rows, :].astype(F32) + b_buf[0:rows, :].astype(F32)).astype(BF16)
                        store = pltpu.make_async_copy(c_buf.at[pl.ds(0, rows), :], total.at[pl.ds(r0, rows), :],
                                                      local_sems.at[2])
                        store.start()
                        store.wait()
                    second[b].start()

        blocks = [(piece, k) for piece in pieces for k in range(piece.shape[0])]
        dh, group, width, blk = None, [], 0, 0
        for piece, k in blocks:
            group.append(piece[k])
            width += piece.shape[2]
            if width == W_BLOCK:
                term = _dot(jnp.concatenate(group, axis=1), w_refs[blk][...])
                dh = term if dh is None else dh + term
                group, width, blk = [], 0, blk + 1
        assert not group and blk == n_w
        xf = x_ref[...]
        r = lax.rsqrt(jnp.mean(xf * xf, axis=-1, keepdims=True) + EPS)
        xh = xf * r
        dxh = dh * g_ref[...]
        dx_ref[...] = r * (dxh - xh * jnp.mean(dxh * xh, axis=-1, keepdims=True)) + dy_ref[...]
        dgain_ref[...] += _rows8(jnp.sum(dh * xh, axis=0, keepdims=True))

        if n_s:
            @pl.when(pl.program_id(0) == n_steps - 1)
            def _():
                first, second, _ = hops()
                for direct, pass_on in first:
                    direct.wait()
                    pass_on.wait_send()
                for cp in second:
                    cp.wait()

    row = pl.BlockSpec((tm, D_MODEL), lambda i: (i, 0))
    halves = [q.shape[1] for q in chip_sums]
    res = pl.pallas_call(
        body, name="d_x", grid=(n_steps,),
        in_specs=[pl.BlockSpec((p.shape[0], tm, p.shape[2]), lambda i: (0, i, 0)) for p in d_proj] + _w_blocks(0, n_w)
        + [row, pl.BlockSpec((1, D_MODEL), lambda i: (0, 0)), row] + [ANY] * n_s,
        out_specs=[row, pl.BlockSpec((8, D_MODEL), lambda i: (0, 0))] + [ANY] * (3 * n_s),
        out_shape=[jax.ShapeDtypeStruct((SEQ, D_MODEL), F32), jax.ShapeDtypeStruct((8, D_MODEL), F32)]
        + [jax.ShapeDtypeStruct((2, half, D_MODEL), BF16) for half in halves]
        + [jax.ShapeDtypeStruct((half, D_MODEL), BF16) for half in halves] * 2,
        scratch_shapes=[pltpu.SemaphoreType.DMA((3 * n_s,)), pltpu.SemaphoreType.DMA((3 * n_s,)),
                        pltpu.SemaphoreType.DMA((3,))] + [pltpu.VMEM((RELAY_ROWS, D_MODEL), BF16)] * 3 if n_s else [],
        compiler_params=_params("arbitrary"),
    )(*d_proj, *([w_t] * n_w), x, gain, dy, *chip_sums)
    return res[0], res[1], res[2:2 + n_s]


def _my_place():
    x, y, c = lax.axis_index("x"), lax.axis_index("y"), lax.axis_index("c")
    return jnp.stack([2 * x + y, c]).astype(jnp.int32)


def _half_rows(ref, half):
    rows = ref.shape[-2] // 2
    idx = (slice(None),) * (len(ref.shape) - 2) + (pl.ds(pl.multiple_of(half * rows, 16), rows), slice(None))
    return ref.at[idx]


def _swap_halves(grads):
    n = len(grads)

    def body(*refs):
        g_refs, o_refs, (send_sems, recv_sems) = refs[:n], refs[n:2 * n], refs[2 * n:]
        x, y, c = lax.axis_index("x"), lax.axis_index("y"), lax.axis_index("c")
        copies = [pltpu.make_async_remote_copy(src_ref=_half_rows(g, 1 - c), dst_ref=o, send_sem=send_sems.at[k],
                                               recv_sem=recv_sems.at[k], device_id=(x, y, 1 - c), device_id_type=MESH)
                  for k, (g, o) in enumerate(zip(g_refs, o_refs))]
        for cp in copies:
            cp.start()
        for cp in copies:
            cp.wait()

    return pl.pallas_call(
        body, name="reduce_swap_halves", in_specs=[ANY] * n, out_specs=[ANY] * n,
        out_shape=[jax.ShapeDtypeStruct((N_CHIPS, g.shape[1] // 2, D_MODEL), g.dtype) for g in grads],
        scratch_shapes=[pltpu.SemaphoreType.DMA((n,)), pltpu.SemaphoreType.DMA((n,))],
    )(*grads)


def _add_halves(place, grads, theirs, name):
    half = theirs.shape[1]
    tr = _row_tile(half)
    n = half // tr

    def body(place_ref, g_ref, t_ref, o_ref):
        o_ref[...] = (g_ref[...].astype(F32) + t_ref[...].astype(F32)).astype(BF16)

    return pl.pallas_call(
        body, name=name,
        grid_spec=pltpu.PrefetchScalarGridSpec(
            num_scalar_prefetch=1, grid=(N_CHIPS, n),
            in_specs=[pl.BlockSpec((None, tr, D_MODEL), lambda s, i, p: (s, p[1] * n + i, 0)),
                      pl.BlockSpec((None, tr, D_MODEL), lambda s, i, p: (s, i, 0))],
            out_specs=pl.BlockSpec((None, tr, D_MODEL), lambda s, i, p: (s, i, 0))),
        out_shape=jax.ShapeDtypeStruct((N_CHIPS, half, D_MODEL), BF16),
        compiler_params=_params("arbitrary", "arbitrary"),
    )(place, grads, theirs)


def _add_chips(place, chip_sums, others, name):
    half = others.shape[1]
    tr = _row_tile(half)
    n = half // tr

    def body(place_ref, q_ref, o_ref, r_ref):
        acc = q_ref[...].astype(F32)
        for j in range(others.shape[0]):
            acc = acc + o_ref[j].astype(F32)
        r_ref[...] = acc

    return pl.pallas_call(
        body, name=name,
        grid_spec=pltpu.PrefetchScalarGridSpec(
            num_scalar_prefetch=1, grid=(n,),
            in_specs=[pl.BlockSpec((None, tr, D_MODEL), lambda i, p: (p[0], i, 0)),
                      pl.BlockSpec((others.shape[0], tr, D_MODEL), lambda i, p: (0, i, 0))],
            out_specs=pl.BlockSpec((tr, D_MODEL), lambda i, p: (p[1] * n + i, 0))),
        out_shape=jax.ShapeDtypeStruct((2 * half, D_MODEL), F32),
        compiler_params=_params("arbitrary"),
    )(place, chip_sums, others)


def _join_halves(shards, block):
    n = len(shards)
    rows = block.shape[0]

    def body(*refs):
        b_ref, o_refs, sum_ref = refs[n], refs[n + 1:2 * n + 1], refs[2 * n + 1]
        send_sems, recv_sems, small_send, small_recv, local_sem, all_ref = refs[2 * n + 2:]
        x, y, c = lax.axis_index("x"), lax.axis_index("y"), lax.axis_index("c")
        me, sibling = (x, y, c), (x, y, 1 - c)
        chips = [(1 - x, y), (x, 1 - y), (1 - x, 1 - y)]

        def half(k, rows_ref):
            return pltpu.make_async_remote_copy(src_ref=rows_ref, dst_ref=rows_ref, send_sem=send_sems.at[k],
                                                recv_sem=recv_sems.at[k], device_id=sibling, device_id_type=MESH)

        def at(px, py, pc):
            return all_ref.at[pl.ds(pl.multiple_of((4 * px + 2 * py + pc) * rows, 8), rows), :]

        def small(k, block_of, to, src=None):
            return pltpu.make_async_remote_copy(src_ref=at(*block_of) if src is None else src, dst_ref=at(*block_of),
                                                send_sem=small_send.at[k], recv_sem=small_recv.at[k],
                                                device_id=to, device_id_type=MESH)

        sends = [half(k, _half_rows(o, c)) for k, o in enumerate(o_refs)]
        for cp in sends:
            cp.start()
        mine = pltpu.make_async_copy(b_ref, at(*me), local_sem)
        mine.start()
        first = [small(0, me, sibling, src=b_ref)]
        first += [small(1 + j, me, (*chip, c), src=b_ref) for j, chip in enumerate(chips)]
        for cp in first:
            cp.start()
        passed = [small(4 + j, (*chip, c), sibling) for j, chip in enumerate(chips)]
        for j, chip in enumerate(chips):
            small(1 + j, (*chip, c), me).wait_recv()
            passed[j].start()
        small(0, sibling, me).wait_recv()
        for j, chip in enumerate(chips):
            small(4 + j, (*chip, 1 - c), me).wait_recv()
        mine.wait()
        acc = all_ref[0:rows, :]
        for dev in range(1, 8):
            acc = acc + all_ref[rows * dev:rows * (dev + 1), :]
        sum_ref[...] = acc
        for k, o in enumerate(o_refs):
            half(k, _half_rows(o, 1 - c)).wait_recv()
        for cp in sends + first + passed:
            cp.wait_send()

    res = pl.pallas_call(
        body, name="reduce_join_halves", in_specs=[ANY] * n + [pl.BlockSpec(memory_space=pltpu.VMEM)],
        out_specs=[ANY] * n + [pl.BlockSpec(memory_space=pltpu.VMEM)],
        out_shape=[jax.ShapeDtypeStruct(s.shape, F32) for s in shards] + [jax.ShapeDtypeStruct(block.shape, F32)],
        input_output_aliases={k: k for k in range(n)},
        scratch_shapes=[pltpu.SemaphoreType.DMA((n,)), pltpu.SemaphoreType.DMA((n,)),
                        pltpu.SemaphoreType.DMA((7,)), pltpu.SemaphoreType.DMA((7,)), pltpu.SemaphoreType.DMA,
                        pltpu.VMEM((8 * rows, D_MODEL), F32)],
    )(*shards, block)
    return res[:n], res[n]


def _adamw_math(w, g, m, v):
    m = ADAM_B1 * m + (1.0 - ADAM_B1) * g
    v = ADAM_B2 * v + (1.0 - ADAM_B2) * (g * g)
    m_hat = m / (1.0 - ADAM_B1 ** ADAM_STEP)
    v_hat = v / (1.0 - ADAM_B2 ** ADAM_STEP)
    return -ADAM_LR * (m_hat / (jnp.sqrt(v_hat) + ADAM_EPS) + ADAM_WD * w), m, v


def _adamw(w, g, m, v, name):
    r, c = w.shape
    tr = _row_tile(r)

    def body(w_ref, g_ref, m_ref, v_ref, d_ref, nm_ref, nv_ref):
        d_ref[...], nm_ref[...], nv_ref[...] = _adamw_math(w_ref[...], g_ref[...], m_ref[...], v_ref[...])

    spec = pl.BlockSpec((tr, c), lambda i: (i, 0))
    return pl.pallas_call(
        body, name=name, grid=(r // tr,), in_specs=[spec] * 4, out_specs=[spec] * 3,
        out_shape=[jax.ShapeDtypeStruct((r, c), F32)] * 3, compiler_params=_params("arbitrary"),
    )(w, g, m, v)


def _adamw_small(ws, gs, ms, vs):
    n = len(ws)

    def body(*refs):
        ins, outs = refs[:4 * n], refs[4 * n:]
        for k in range(n):
            d, m, v = _adamw_math(ins[k][...], ins[n + k][...], ins[2 * n + k][...], ins[3 * n + k][...])
            outs[k][...], outs[n + k][...], outs[2 * n + k][...] = d, m, v

    shapes = [jax.ShapeDtypeStruct(w.shape, F32) for w in ws]
    res = pl.pallas_call(body, name="adamw_small", out_shape=shapes * 3)(*ws, *gs, *ms, *vs)
    return res[:n], res[n:2 * n], res[2 * n:]


def _fold_heads(partials):
    t = jnp.sum(partials[:, 0, :], axis=0)
    return (t[:HEAD_DIM] + t[HEAD_DIM:]).reshape(1, HEAD_DIM)


def _local_step(x, target, norm_gain, w_t, w_a, w_b, w_o, b_m, q_norm_a, k_norm_a, q_norm_b, k_norm_b, sink_a,
                rel_bias, start_reduce=None):
    two = lambda gain: jnp.concatenate([gain, gain], axis=1)
    bias_a = _bias_lines(rel_bias[:, :8], A_HALF_WINDOW, 1)
    bias_b = jnp.concatenate([_bias_lines(rel_bias[:, 8 + 8 * g:16 + 8 * g], B_HALF_WINDOW, d)
                              for g, d in enumerate(B_DILATIONS)], axis=0)

    qkv, h = _in_proj(x, norm_gain, w_t, 0, QKV_WIDTH // W_BLOCK, BF16, "in_proj_qkv", True)
    gates, = _in_proj(x, norm_gain, w_t, QKV_WIDTH // W_BLOCK, GATE_WIDTH // W_BLOCK, F32, "in_proj_gates", False)
    out_a, lse_a = _attn_a_fwd(qkv, two(q_norm_a), two(k_norm_a), bias_a, sink_a)
    out_b, lse_b = _attn_b_fwd(qkv, two(q_norm_b), two(k_norm_b), bias_b)

    dy, dgates, d_out_a, d_out_b, delta_a, delta_b, d_wa, d_wb, d_wo, d_bm, sq = _middle(
        out_a, out_b, gates, x, target, w_a, w_b, w_o, b_m)
    loss = (0.5 / D_MODEL) * jnp.sum(sq)

    dq_a, dkv_a, dgq_a, dgk_a, ds_a, dsink = _attn_a_bwd(
        qkv, two(q_norm_a), two(k_norm_a), bias_a, sink_a, delta_a, lse_a, d_out_a)
    dq_b, dk_b, dv_b, dgq_b, dgk_b, ds_b = _attn_b_bwd(
        qkv, two(q_norm_b), two(k_norm_b), bias_b, delta_b, lse_b, d_out_b)
    d_proj = (dq_a, dkv_a, dq_b, dk_b, dv_b, dgates)

    d_bm_rows = jnp.pad(d_bm.reshape(2, N_CHIPS, 256).transpose(1, 0, 2),
                        ((0, 0), (0, REST_ROWS - 514), (0, D_MODEL - 256)))
    rest = jnp.concatenate([d_wo.reshape(N_CHIPS, 256, D_MODEL), d_wa.reshape(N_CHIPS, 128, D_MODEL),
                            d_wb.reshape(N_CHIPS, 128, D_MODEL), d_bm_rows], axis=1)
    grads = [_d_w_in(d_proj, h).reshape(N_CHIPS, W_IN_SHARD, D_MODEL), rest]
    narrow = [grads[0], rest.astype(BF16)]
    chip_sums = start_reduce(grads, narrow) if start_reduce is not None else []
    grad_x, d_gain, others = _d_x(d_proj, w_t, x, norm_gain, dy, chip_sums)

    d_rel = jnp.concatenate(
        [_bias_grad(ds_a, A_HALF_WINDOW, 1)]
        + [_bias_grad(ds_b[4 * g:4 * g + 4], B_HALF_WINDOW, d) for g, d in enumerate(B_DILATIONS)], axis=1)
    d_sink = jnp.sum(dsink, axis=(2, 3)).reshape(1, 8)
    dgk_a_row = dgk_a[0]
    small = jnp.zeros((8, D_MODEL), F32)
    small = small.at[0].set(d_gain[0])
    small = small.at[1].set(d_rel.reshape(-1))
    misc = jnp.concatenate([_fold_heads(dgq_a), (dgk_a_row[:HEAD_DIM] + dgk_a_row[HEAD_DIM:]).reshape(1, HEAD_DIM),
                            _fold_heads(dgq_b), _fold_heads(dgk_b), d_sink], axis=1)
    small = small.at[2, :264].set(misc[0])

    return loss, grad_x, grads, small, chip_sums, others


def _unpack_weights(w_t_all, small_all):
    sm = small_all.reshape(N_CHIPS, SMALL_ROWS, D_MODEL)
    w_o = sm[:, 0:256].reshape(D_MODEL, D_MODEL)
    w_a = sm[:, 256:384].reshape(N_CHIPS, 512, 256).transpose(1, 0, 2).reshape(512, D_MODEL)
    w_b = sm[:, 384:512].reshape(N_CHIPS, 512, 256).transpose(1, 0, 2).reshape(512, D_MODEL)
    b_m = lax.bitcast_convert_type(sm[:, 512].reshape(N_CHIPS, 2, 256, 2), F32)
    return w_t_all, w_a, w_b, w_o, b_m.transpose(1, 0, 2).reshape(2, D_MODEL)


def _pack_small_weights(w_branch_a, w_branch_b, b_merge, w_out):
    b_m = jnp.pad(lax.bitcast_convert_type(b_merge, BF16).reshape(1, D_MODEL), ((0, SMALL_ROWS - 513), (0, 0)))
    return jnp.concatenate([w_out.astype(BF16), w_branch_a.astype(BF16).reshape(128, D_MODEL),
                            w_branch_b.astype(BF16).reshape(128, D_MODEL), b_m], axis=0)


def kernel(x, norm_gain, w_in, q_norm_a, k_norm_a, q_norm_b, k_norm_b, sink_a, rel_bias, w_branch_a, w_branch_b, b_merge, w_out, loss_target, m_norm_gain, m_w_in, m_q_norm_a, m_k_norm_a, m_q_norm_b, m_k_norm_b, m_sink_a, m_rel_bias, m_w_branch_a, m_w_branch_b, m_b_merge, m_w_out, v_norm_gain, v_w_in, v_q_norm_a, v_k_norm_a, v_q_norm_b, v_k_norm_b, v_sink_a, v_rel_bias, v_w_branch_a, v_w_branch_b, v_b_merge, v_w_out):
    w_in_t, m_w_in_t, v_w_in_t = (jnp.transpose(t[0]) for t in (w_in, m_w_in, v_w_in))
    wt_shard = _cast_rows(w_in_t, BF16, "w_in_cast")
    w_t, w_a, w_b, w_o, b_m = _unpack_weights(
        *_gather_weights(wt_shard, _pack_small_weights(w_branch_a[0], w_branch_b[0], b_merge[0], w_out[0])))

    place = _my_place()
    names = ("w_in", "rest")

    def start_reduce(grads, narrow):
        return [_add_halves(place, g, t, "reduce_add_halves_" + n) for g, t, n in zip(grads, _swap_halves(narrow), names)]

    loss_part, grad_x, _, small, chip_sums, others = _local_step(
        x[0], loss_target[0], norm_gain, w_t, w_a, w_b, w_o, b_m, q_norm_a, k_norm_a, q_norm_b, k_norm_b,
        sink_a, rel_bias, start_reduce)

    (g_wt, g_rest), small = _join_halves(
        [_add_chips(place, q, o, "reduce_add_chips_" + n) for q, o, n in zip(chip_sums, others, names)],
        small.at[3, 0].set(loss_part))
    loss = small[3, 0]

    g_w_out = g_rest[0:256]
    g_w_a = g_rest[256:384].reshape(512, 256)
    g_w_b = g_rest[384:512].reshape(512, 256)
    g_b_merge = g_rest[512:514, :256]
    g_norm_gain = small[0:1]
    g_rel_bias = small[1].reshape(N_BUCKETS, N_BUCKETS)
    g_q_a, g_k_a, g_q_b, g_k_b = (small[2:3, 64 * k:64 * k + 64] for k in range(4))
    g_sink = small[2:3, 256:264]

    big_names = (("w_branch_a", w_branch_a, g_w_a, m_w_branch_a, v_w_branch_a),
                 ("w_branch_b", w_branch_b, g_w_b, m_w_branch_b, v_w_branch_b),
                 ("w_out", w_out, g_w_out, m_w_out, v_w_out))
    upd = {name: (g,) + tuple(_adamw(w[0], g, m[0], v[0], "adamw_" + name)) for name, w, g, m, v in big_names}
    upd["w_in"] = tuple(jnp.transpose(t) for t in (g_wt,) + tuple(_adamw(w_in_t, g_wt, m_w_in_t, v_w_in_t, "adamw_w_in")))
    small_names = ("norm_gain", "q_norm_a", "k_norm_a", "q_norm_b", "k_norm_b", "sink_a", "rel_bias", "b_merge")
    ws = [norm_gain, q_norm_a, k_norm_a, q_norm_b, k_norm_b, sink_a, rel_bias, b_merge[0]]
    gs = [g_norm_gain, g_q_a, g_k_a, g_q_b, g_k_b, g_sink, g_rel_bias, g_b_merge]
    ms = [m_norm_gain, m_q_norm_a, m_k_norm_a, m_q_norm_b, m_k_norm_b, m_sink_a, m_rel_bias, m_b_merge[0]]
    vs = [v_norm_gain, v_q_norm_a, v_k_norm_a, v_q_norm_b, v_k_norm_b, v_sink_a, v_rel_bias, v_b_merge[0]]
    ds, nms, nvs = _adamw_small(ws, gs, ms, vs)
    for k, name in enumerate(small_names):
        upd[name] = (gs[k], ds[k], nms[k], nvs[k])

    order = ("norm_gain", "w_in", "q_norm_a", "k_norm_a", "q_norm_b", "k_norm_b", "sink_a", "rel_bias",
             "w_branch_a", "w_branch_b", "b_merge", "w_out")
    lead = {"w_in", "w_branch_a", "w_branch_b", "b_merge", "w_out"}
    outs = [loss, grad_x[None]]
    for part in range(4):
        outs += [upd[name][part][None] if name in lead else upd[name][part] for name in order]
    return tuple(outs)
```

```python
import math

import numpy as np
import jax
import jax.numpy as jnp
from jax import lax
from jax.experimental import pallas as pl
from jax.experimental.pallas import tpu as pltpu

F32 = jnp.float32
BF16 = jnp.bfloat16

SEQ = 4096
D_MODEL = 1024
HEAD_DIM = 64
LANES = 128
EPS = 1e-6
NEG_INF = -1e30
SCALE = HEAD_DIM ** -0.5
N_BUCKETS = 32
MAX_DISTANCE = 1024
N_CHIPS = 4

A_HALF_WINDOW = 128
B_HALF_WINDOW = 64
B_DILATIONS = (1, 4, 16)
Q_BLOCK = 128

QKV_WIDTH = 5376
GATE_WIDTH = 3072
QA_BLK, KA_BLK, VA_BLK = 0, 4, 5
QB_BLK, KB_BLK, VB_BLK = 6, 18, 30
IN_WIDTH = QKV_WIDTH + GATE_WIDTH
W_IN_SHARD = IN_WIDTH // N_CHIPS

SMALL_ROWS = 544
REST_ROWS = 544

ADAM_LR = 0.001
ADAM_B1 = 0.9
ADAM_B2 = 0.999
ADAM_EPS = 1e-08
ADAM_WD = 0.01
ADAM_STEP = 10

VMEM_LIMIT = 56 * 1024 * 1024

NT = (((1,), (1,)), ((), ()))
TN = (((0,), (0,)), ((), ()))
MESH = pl.DeviceIdType.MESH
ANY = pl.BlockSpec(memory_space=pl.ANY)


def _dot(a, b, dims=None):
    if dims is None:
        return jnp.dot(a, b, preferred_element_type=F32)
    return lax.dot_general(a, b, dims, preferred_element_type=F32)


def _params(*semantics):
    return pltpu.CompilerParams(dimension_semantics=semantics or None, vmem_limit_bytes=VMEM_LIMIT)


def _line_width(half_window):
    return pl.cdiv(2 * Q_BLOCK + 2 * half_window - 1, LANES) * LANES


def _bucket_onehot(half_window, stride):
    rel = np.arange(_line_width(half_window)) - (Q_BLOCK - 1) - half_window
    band = np.abs(rel) <= half_window
    rel = rel * stride
    half, max_exact = N_BUCKETS // 2, N_BUCKETS // 4
    n = np.abs(rel)
    nf = np.maximum(n, max_exact).astype(np.float32)
    large = max_exact + (np.log(nf / np.float32(max_exact)) / np.float32(math.log(MAX_DISTANCE / max_exact))
                         * np.float32(half - max_exact)).astype(np.int32)
    large = np.minimum(large, half - 1)
    bucket = (rel > 0).astype(np.int32) * half + np.where(n < max_exact, n, large)
    onehot = (bucket[..., None] == np.arange(N_BUCKETS)) & band[..., None]
    return onehot.astype(np.float32), band


def _bias_lines(rel_bias_cols, half_window, stride):
    onehot, band = _bucket_onehot(half_window, stride)
    h = rel_bias_cols.shape[1]
    t = jnp.einsum("tb,bh->ht", jnp.asarray(onehot), rel_bias_cols, precision=lax.Precision.HIGHEST)
    t = t + jnp.asarray(np.where(band, 0.0, NEG_INF).astype(np.float32))
    return t.reshape(h // 2, 2, -1)


def _bias_grad(d_lines, half_window, stride):
    onehot, _ = _bucket_onehot(half_window, stride)
    h = d_lines.shape[0] * 2
    return jnp.einsum("tb,ht->bh", jnp.asarray(onehot), d_lines.reshape(h, -1), precision=lax.Precision.HIGHEST)


def _unroll_bias(line_ref, tile_ref, w):
    width = line_ref.shape[1]
    for j in range(2):
        rows = jnp.broadcast_to(line_ref[j:j + 1, :], (Q_BLOCK, width))
        rows = pltpu.roll(rows, width - (Q_BLOCK - 1), 1, stride=1, stride_axis=0)
        tile_ref[j * Q_BLOCK:(j + 1) * Q_BLOCK, :] = rows[:, :w]


def _fold_bias_grad(tile_ref, line_ref, w):
    width = line_ref.shape[1]
    row = lax.broadcasted_iota(jnp.int32, (Q_BLOCK, Q_BLOCK), 0)
    col = lax.broadcasted_iota(jnp.int32, (Q_BLOCK, Q_BLOCK), 1)
    flip = jnp.where(row + col == Q_BLOCK - 1, 1.0, 0.0).astype(BF16)
    for j in range(2):
        tile = tile_ref[j * Q_BLOCK:(j + 1) * Q_BLOCK, :]
        hi = tile.astype(BF16)
        lo = (tile - hi.astype(F32)).astype(BF16)
        rows = _dot(flip, hi) + _dot(flip, lo)
        rows = jnp.concatenate([rows, jnp.zeros((Q_BLOCK, width - w), F32)], axis=1)
        rows = pltpu.roll(rows, 0, 1, stride=1, stride_axis=0)
        line_ref[j:j + 1, :] = jnp.sum(rows, axis=0, keepdims=True)


def _row_tile(rows):
    return max(t for t in range(16, 385, 16) if rows % t == 0)


def _cast_rows(w, out_dtype, name):
    r, c = w.shape
    tr = _row_tile(r)

    def body(w_ref, o_ref):
        o_ref[...] = w_ref[...].astype(out_dtype)

    spec = pl.BlockSpec((tr, c), lambda i: (i, 0))
    return pl.pallas_call(
        body, name=name, grid=(r // tr,), in_specs=[spec], out_specs=spec,
        out_shape=jax.ShapeDtypeStruct((r, c), out_dtype), compiler_params=_params("arbitrary"),
    )(w)


def _gather_weights(wt_shard, small_shard):
    bufs = ((W_IN_SHARD, IN_WIDTH), (SMALL_ROWS, N_CHIPS * SMALL_ROWS))

    stage_rows = 528

    def body(wt_in, sm_in, wt_out, sm_out, send_sems, recv_sems, in_sems, out_sems, stage):
        x, y, c = lax.axis_index("x"), lax.axis_index("y"), lax.axis_index("c")
        sibling = (x, y, 1 - c)
        my_chip = 2 * x + y
        refs = ((wt_in, wt_out), (sm_in, sm_out))

        def keep_own():
            pieces = [(b, r0) for b in range(2) for r0 in range(0, bufs[b][0], stage_rows)]
            outs = []
            for i, (b, r0) in enumerate(pieces):
                rows = min(stage_rows, bufs[b][0] - r0)
                slot = i % 2
                if i >= 2:
                    outs[i - 2].wait()
                buf = stage.at[slot, pl.ds(0, rows), :]
                load = pltpu.make_async_copy(refs[b][0].at[pl.ds(r0, rows), :], buf, in_sems.at[slot])
                load.start()
                load.wait()
                start = pl.multiple_of(my_chip * bufs[b][0] + r0, 16)
                outs.append(pltpu.make_async_copy(buf, refs[b][1].at[pl.ds(start, rows), :], out_sems.at[slot]))
                outs[i].start()
            for cp in outs[-2:]:
                cp.wait()

        def half_of(b, chip, half):
            rows = bufs[b][0]
            start = pl.multiple_of(chip * rows + half * (rows // 2), 16)
            return refs[b][1].at[pl.ds(start, rows // 2), :]

        def copy(k, src, dst, to):
            return pltpu.make_async_remote_copy(src_ref=src, dst_ref=dst, send_sem=send_sems.at[k],
                                                recv_sem=recv_sems.at[k], device_id=to, device_id_type=MESH)

        near = (x + (1 - c) - 2 * x * (1 - c), y + c - 2 * y * c)
        far = (x + c - 2 * x * c, y + (1 - c) - 2 * y * (1 - c))
        diag = (1 - x, 1 - y)
        chip_no = lambda chip: 2 * chip[0] + chip[1]
        sends, passed = [], []
        for b in range(2):
            rows = bufs[b][0]
            src = refs[b][0].at[pl.ds(pl.multiple_of(c * (rows // 2), 16), rows // 2), :]
            sends += [copy(3 * b, src, half_of(b, my_chip, c), (*near, c)),
                      copy(3 * b + 1, src, half_of(b, my_chip, c), (*far, c))]
        for cp in sends:
            cp.start()
        keep_own()

        def pass_on(b, j, chip):
            landed = half_of(b, chip_no(chip), c)
            fwd = copy(6 + 3 * b + j, landed, landed, sibling)
            fwd.start()
            passed.append(fwd)

        for b in range(2):
            landed = half_of(b, chip_no(near), c)
            copy(3 * b, landed, landed, sibling).wait_recv()
            relay = copy(3 * b + 2, landed, landed, (*far, c))
            relay.start()
            sends.append(relay)
            pass_on(b, 0, near)
        for b in range(2):
            for j, chip in ((1, far), (2, diag)):
                landed = half_of(b, chip_no(chip), c)
                copy(3 * b + j, landed, landed, sibling).wait_recv()
                pass_on(b, j, chip)
        for b in range(2):
            for j, chip in ((0, far), (1, near), (2, diag)):
                other = half_of(b, chip_no(chip), 1 - c)
                copy(6 + 3 * b + j, other, other, sibling).wait_recv()
        for cp in sends + passed:
            cp.wait_send()

    return pl.pallas_call(
        body, name="gather_weights",
        in_specs=[ANY, ANY], out_specs=[ANY, ANY],
        out_shape=[jax.ShapeDtypeStruct((bufs[0][1], D_MODEL), BF16),
                   jax.ShapeDtypeStruct((bufs[1][1], D_MODEL), BF16)],
        scratch_shapes=[pltpu.SemaphoreType.DMA((12,)), pltpu.SemaphoreType.DMA((12,)),
                        pltpu.SemaphoreType.DMA((2,)), pltpu.SemaphoreType.DMA((2,)),
                        pltpu.VMEM((2, stage_rows, D_MODEL), BF16)],
    )(wt_shard, small_shard)


W_BLOCK = 768


def _w_blocks(first, count):
    return [pl.BlockSpec((W_BLOCK, D_MODEL), lambda *_, k=k: (first + k, 0)) for k in range(count)]


def _in_proj(x, gain, w_t, first_block, n_blocks, out_dtype, name, keep_h):
    tm = 512

    def body(x_ref, g_ref, *refs):
        w_refs, outs = refs[:n_blocks], refs[n_blocks:]
        xf = x_ref[...]
        r = lax.rsqrt(jnp.mean(xf * xf, axis=-1, keepdims=True) + EPS)
        h = ((xf * r) * g_ref[...]).astype(BF16)
        if keep_h:
            outs[1][...] = h
        for k, w_ref in enumerate(w_refs):
            outs[0][:, k * W_BLOCK:(k + 1) * W_BLOCK] = _dot(h, w_ref[...], NT).astype(out_dtype)

    return pl.pallas_call(
        body, name=name, grid=(SEQ // tm,),
        in_specs=[pl.BlockSpec((tm, D_MODEL), lambda i: (i, 0)), pl.BlockSpec((1, D_MODEL), lambda i: (0, 0))]
        + _w_blocks(first_block, n_blocks),
        out_specs=[pl.BlockSpec((tm, W_BLOCK * n_blocks), lambda i: (i, 0)),
                   pl.BlockSpec((tm, D_MODEL), lambda i: (i, 0))][:2 if keep_h else 1],
        out_shape=[jax.ShapeDtypeStruct((SEQ, W_BLOCK * n_blocks), out_dtype),
                   jax.ShapeDtypeStruct((SEQ, D_MODEL), BF16)][:2 if keep_h else 1],
        compiler_params=_params("arbitrary"),
    )(x, gain, *([w_t] * n_blocks))


CHUNK = 256
CHUNK_UNROLL = 4
TILE_UNROLL = 8


def _low_half():
    return lax.broadcasted_iota(jnp.int32, (1, LANES), 1) < HEAD_DIM


def _half_sum(v, low):
    del low
    row = lax.broadcasted_iota(jnp.int32, (2 * LANES, LANES), 0)
    col = lax.broadcasted_iota(jnp.int32, (2 * LANES, LANES), 1)
    ones = jnp.where((row % LANES) // HEAD_DIM == col // HEAD_DIM, 1.0, 0.0).astype(BF16)
    hi = v.astype(BF16)
    lo = (v - hi.astype(F32)).astype(BF16)
    return _dot(jnp.concatenate([hi, lo], axis=1), ones)


def _chunks(fn, init=0):
    def body(i, carry):
        for u in range(CHUNK_UNROLL):
            carry = fn(pl.multiple_of((i * CHUNK_UNROLL + u) * CHUNK, CHUNK), carry)
        return carry

    return lax.fori_loop(0, SEQ // (CHUNK * CHUNK_UNROLL), body, init)


def _inv_rms(t, low):
    del low
    row = lax.broadcasted_iota(jnp.int32, (LANES, LANES), 0)
    col = lax.broadcasted_iota(jnp.int32, (LANES, LANES), 1)
    ones = jnp.where(row // HEAD_DIM == col // HEAD_DIM, 1.0, 0.0).astype(BF16)
    return lax.rsqrt(_dot((t * t).astype(BF16), ones) * (1.0 / HEAD_DIM) + EPS)


def _prep_q(q_ref, gain_ref, qn_ref):
    low = _low_half()

    def step(r0, carry):
        q = q_ref[pl.ds(r0, CHUNK), :].astype(F32)
        qn_ref[pl.ds(r0, CHUNK), :] = ((q * _inv_rms(q, low)) * gain_ref[...]) * SCALE
        return carry

    _chunks(step)


def _own_half(t, keep):
    return jnp.where(keep, t, pltpu.roll(t, HEAD_DIM, 1))


def _prep_kv(k_ref, v_ref, gain_ref, kp_ref, vp_ref, pad, keep=None):
    low = _low_half()
    zeros = jnp.zeros((pad, LANES), F32)
    for ref in (kp_ref, vp_ref):
        ref[pl.ds(0, pad), :] = zeros
        ref[pl.ds(pad + SEQ, pad), :] = zeros

    def step(r0, carry):
        k = k_ref[pl.ds(r0, CHUNK), :].astype(F32)
        v = v_ref[pl.ds(r0, CHUNK), :].astype(F32)
        kn = (k * _inv_rms(k, low)) * gain_ref[...]
        if keep is not None:
            kn, v = _own_half(kn, keep), _own_half(v, keep)
        kp_ref[pl.ds(pad + r0, CHUNK), :] = kn
        vp_ref[pl.ds(pad + r0, CHUNK), :] = v
        return carry

    _chunks(step)


def _tiles(d, half_window, fn):
    w = Q_BLOCK + 2 * half_window
    length = SEQ // d
    n_blocks = length // Q_BLOCK
    col = lax.broadcasted_iota(jnp.int32, (1, w), 1)

    def step(it, carry):
        c, n = it // n_blocks, it % n_blocks
        start = c + (d * Q_BLOCK) * n
        if d == 1:
            start = pl.multiple_of(start, Q_BLOCK)
            q_rows, k_rows = pl.ds(start, Q_BLOCK), pl.ds(start, w)
        else:
            q_rows, k_rows = pl.ds(start, Q_BLOCK, stride=d), pl.ds(start, w, stride=d)
        t = n * Q_BLOCK - half_window + col
        edge = jnp.where((t < 0) | (t >= length), NEG_INF, 0.0)
        fn(q_rows, k_rows, edge)
        return carry

    lax.fori_loop(0, d * n_blocks, step, 0, unroll=TILE_UNROLL)


def _stack_heads(t, low):
    return jnp.concatenate([jnp.where(low, t, 0.0), jnp.where(low, 0.0, t)], axis=0).astype(BF16)


def _unstack_heads(t, low):
    return jnp.where(low, t[:Q_BLOCK], t[Q_BLOCK:])


def _per_head(pair):
    return jnp.concatenate([jnp.full((Q_BLOCK, 1), pair[0], F32), jnp.full((Q_BLOCK, 1), pair[1], F32)], axis=0)


def _fwd_tiles(qn_ref, kp_ref, vp_ref, bias_ref, emit, *, d, half_window, sinks=None):
    low = _low_half()
    w = Q_BLOCK + 2 * half_window
    sink = None if sinks is None else _per_head(sinks)

    def tile(q_rows, k_rows, edge):
        q2 = _stack_heads(qn_ref[q_rows, :], low)
        k = kp_ref[k_rows, :].astype(BF16)
        v1 = jnp.concatenate([vp_ref[k_rows, :], jnp.ones((w, LANES), F32)], axis=1).astype(BF16)
        s = _dot(q2, k, NT) + bias_ref[...] + edge
        m = jnp.max(s, axis=-1, keepdims=True)
        if sink is not None:
            m = jnp.maximum(m, sink)
        o = _dot(jnp.exp(s - m).astype(BF16), v1)
        l = o[:, LANES:]
        if sink is not None:
            l = l + jnp.exp(sink - m)
        emit(q_rows, _unstack_heads(o[:, :LANES] * (1.0 / l), low), _unstack_heads(m + jnp.log(l), low))

    _tiles(d, half_window, tile)


def _bwd_tiles(qn_ref, kp_ref, vp_ref, bias_ref, do_ref, lse_ref, delta_ref, dq_ref, dk_ref, dv_ref, ds_ref,
               *, d, half_window, sinks=None, dsink_ref=None):
    low = _low_half()
    w = Q_BLOCK + 2 * half_window
    sink = None if sinks is None else _per_head(sinks)

    def rows_of(t):
        return jnp.concatenate([t[:, 0:1], t[:, HEAD_DIM:HEAD_DIM + 1]], axis=0)

    def tile(q_rows, k_rows, edge):
        q2 = _stack_heads(qn_ref[q_rows, :], low)
        do2 = _stack_heads(do_ref[q_rows, :], low)
        k = kp_ref[k_rows, :].astype(BF16)
        v = vp_ref[k_rows, :].astype(BF16)
        lse = rows_of(lse_ref[q_rows, :])
        delta = rows_of(delta_ref[q_rows, :])
        p = jnp.exp(_dot(q2, k, NT) + bias_ref[...] + edge - lse)
        ds = p * (_dot(do2, v, NT) - delta)
        ds_ref[...] += ds
        if sink is not None:
            dsink_ref[...] += (-jnp.exp(sink - lse) * delta).reshape(2, Q_BLOCK, 1)
        dsb, pb = ds.astype(BF16), p.astype(BF16)
        dq_ref[q_rows, :] = _unstack_heads(_dot(dsb, k), low)
        dk_ref[k_rows, :] += _dot(dsb, q2, TN)
        dv_ref[k_rows, :] += _dot(pb, do2, TN)

    _tiles(d, half_window, tile)


def _norm_bwd(raw_ref, gain_ref, dn_ref, dn_offset, out_ref, scale):
    low = _low_half()

    def step(r0, dgain):
        t = raw_ref[pl.ds(r0, CHUNK), :].astype(F32)
        dn = dn_ref[pl.ds(dn_offset + r0, CHUNK), :]
        dth = dn * (gain_ref[...] * scale)
        sums = _half_sum(jnp.concatenate([t * t, dth * t], axis=0), low)
        r = lax.rsqrt(sums[:CHUNK] * (1.0 / HEAD_DIM) + EPS)
        th = t * r
        out_ref[pl.ds(r0, CHUNK), :] = (r * (dth - th * (r * sums[CHUNK:] * (1.0 / HEAD_DIM)))).astype(BF16)
        return dgain + jnp.sum(dn * th, axis=0, keepdims=True) * scale

    return _chunks(step, jnp.zeros((1, LANES), F32))


def _rows8(v):
    return jnp.broadcast_to(v, (8, v.shape[-1]))


A_W = Q_BLOCK + 2 * A_HALF_WINDOW
A_PAD = A_HALF_WINDOW


def _seq_block(col_fn):
    return pl.BlockSpec((SEQ, LANES), col_fn)


def _attn_a_fwd(qkv, gain_q, gain_k, bias, sink):
    def body(sink_ref, q_ref, k_ref, v_ref, gq_ref, gk_ref, line_ref, o_ref, lse_ref, qn_ref, kp_ref, vp_ref,
             bias_ref):
        hp = pl.program_id(0)
        keep = (lax.broadcasted_iota(jnp.int32, (1, LANES), 1) // HEAD_DIM) == hp // 2
        _prep_q(q_ref, gq_ref, qn_ref)
        _prep_kv(k_ref, v_ref, gk_ref, kp_ref, vp_ref, A_PAD, keep)
        _unroll_bias(line_ref, bias_ref, A_W)

        def emit(rows, out, lse):
            o_ref[rows, :] = out
            lse_ref[rows, :] = lse

        _fwd_tiles(qn_ref, kp_ref, vp_ref, bias_ref, emit, d=1, half_window=A_HALF_WINDOW,
                   sinks=(sink_ref[2 * hp], sink_ref[2 * hp + 1]))

    vec = pl.BlockSpec((1, LANES), lambda hp, s: (0, 0))
    return pl.pallas_call(
        body, name="attn_a_fwd",
        grid_spec=pltpu.PrefetchScalarGridSpec(
            num_scalar_prefetch=1, grid=(4,),
            in_specs=[_seq_block(lambda hp, s: (0, QA_BLK + hp)), _seq_block(lambda hp, s: (0, KA_BLK)),
                      _seq_block(lambda hp, s: (0, VA_BLK)), vec, vec,
                      pl.BlockSpec((None, 2, _line_width(A_HALF_WINDOW)), lambda hp, s: (hp, 0, 0))],
            out_specs=[_seq_block(lambda hp, s: (0, hp)), _seq_block(lambda hp, s: (0, hp))],
            scratch_shapes=[pltpu.VMEM((SEQ, LANES), F32), pltpu.VMEM((SEQ + 2 * A_PAD, LANES), F32),
                            pltpu.VMEM((SEQ + 2 * A_PAD, LANES), F32), pltpu.VMEM((2 * Q_BLOCK, A_W), F32)]),
        out_shape=[jax.ShapeDtypeStruct((SEQ, 512), F32)] * 2,
        compiler_params=_params("arbitrary"),
    )(sink.reshape(8), qkv, qkv, qkv, gain_q, gain_k, bias)


def _attn_a_bwd(qkv, gain_q, gain_k, bias, sink, delta, lse, d_out):
    def body(sink_ref, q_ref, k_ref, v_ref, gq_ref, gk_ref, line_ref, delta_ref, lse_ref, do_ref,
             dq_out, dkv_out, dgq_out, dgk_out, dline_out, dsink_out,
             qn_ref, kp_ref, vp_ref, dq_ref, dk_ref, dv_ref, dk_tot, dv_tot, bias_ref, ds_out):
        hp = pl.program_id(0)
        kv_head = hp // 2
        keep = (lax.broadcasted_iota(jnp.int32, (1, LANES), 1) // HEAD_DIM) == kv_head
        _prep_q(q_ref, gq_ref, qn_ref)
        _prep_kv(k_ref, v_ref, gk_ref, kp_ref, vp_ref, A_PAD, keep)
        _unroll_bias(line_ref, bias_ref, A_W)
        dk_ref[...] = jnp.zeros_like(dk_ref)
        dv_ref[...] = jnp.zeros_like(dv_ref)
        ds_out[...] = jnp.zeros_like(ds_out)
        dsink_out[...] = jnp.zeros_like(dsink_out)

        @pl.when(hp == 0)
        def _():
            dk_tot[...] = jnp.zeros_like(dk_tot)
            dv_tot[...] = jnp.zeros_like(dv_tot)

        _bwd_tiles(qn_ref, kp_ref, vp_ref, bias_ref, do_ref, lse_ref, delta_ref, dq_ref, dk_ref, dv_ref, ds_out,
                   d=1, half_window=A_HALF_WINDOW, sinks=(sink_ref[2 * hp], sink_ref[2 * hp + 1]),
                   dsink_ref=dsink_out)
        _fold_bias_grad(ds_out, dline_out, A_W)
        dgq_out[...] = _rows8(_norm_bwd(q_ref, gq_ref, dq_ref, 0, dq_out, SCALE))

        def fold(r0, carry):
            rows = pl.ds(A_PAD + r0, CHUNK)
            for acc, tot in ((dk_ref, dk_tot), (dv_ref, dv_tot)):
                t = acc[rows, :]
                tot[pl.ds(r0, CHUNK), :] += jnp.where(keep, t + pltpu.roll(t, HEAD_DIM, 1), 0.0)
            return carry

        _chunks(fold)

        @pl.when(hp == 3)
        def _():
            dgk_out[...] = _rows8(_norm_bwd(k_ref, gk_ref, dk_tot, 0, dkv_out.at[0], 1.0))
            dkv_out[1] = dv_tot[...].astype(BF16)

    vec = pl.BlockSpec((1, LANES), lambda hp, s: (0, 0))
    seq_f32 = pltpu.VMEM((SEQ, LANES), F32)
    padded = pltpu.VMEM((SEQ + 2 * A_PAD, LANES), F32)
    return pl.pallas_call(
        body, name="attn_a_bwd",
        grid_spec=pltpu.PrefetchScalarGridSpec(
            num_scalar_prefetch=1, grid=(4,),
            in_specs=[_seq_block(lambda hp, s: (0, QA_BLK + hp)), _seq_block(lambda hp, s: (0, KA_BLK)),
                      _seq_block(lambda hp, s: (0, VA_BLK)), vec, vec,
                      pl.BlockSpec((None, 2, _line_width(A_HALF_WINDOW)), lambda hp, s: (hp, 0, 0)),
                      _seq_block(lambda hp, s: (0, hp)), _seq_block(lambda hp, s: (0, hp)),
                      _seq_block(lambda hp, s: (0, hp))],
            out_specs=[pl.BlockSpec((None, SEQ, LANES), lambda hp, s: (hp, 0, 0)),
                       pl.BlockSpec((2, SEQ, LANES), lambda hp, s: (0, 0, 0)),
                       pl.BlockSpec((None, 8, LANES), lambda hp, s: (hp, 0, 0)),
                       pl.BlockSpec((8, LANES), lambda hp, s: (0, 0)),
                       pl.BlockSpec((None, 2, _line_width(A_HALF_WINDOW)), lambda hp, s: (hp, 0, 0)),
                       pl.BlockSpec((None, 2, Q_BLOCK, 1), lambda hp, s: (hp, 0, 0, 0))],
            scratch_shapes=[seq_f32, padded, padded, seq_f32, padded, padded, seq_f32, seq_f32,
                            pltpu.VMEM((2 * Q_BLOCK, A_W), F32), pltpu.VMEM((2 * Q_BLOCK, A_W), F32)]),
        out_shape=[jax.ShapeDtypeStruct((4, SEQ, LANES), BF16), jax.ShapeDtypeStruct((2, SEQ, LANES), BF16),
                   jax.ShapeDtypeStruct((4, 8, LANES), F32), jax.ShapeDtypeStruct((8, LANES), F32),
                   jax.ShapeDtypeStruct((4, 2, _line_width(A_HALF_WINDOW)), F32),
                   jax.ShapeDtypeStruct((4, 2, Q_BLOCK, 1), F32)],
        compiler_params=_params("arbitrary"),
    )(sink.reshape(8), qkv, qkv, qkv, gain_q, gain_k, bias, delta, lse, d_out)


B_W = Q_BLOCK + 2 * B_HALF_WINDOW
B_PAD_MAX = B_HALF_WINDOW * B_DILATIONS[-1]


def _attn_b_fwd(qkv, gain_q, gain_k, bias):
    def body(q_ref, k_ref, v_ref, gq_ref, gk_ref, line_ref, o_ref, lse_ref, qn_ref, kp_ref, vp_ref, bias_ref):
        g = pl.program_id(1)
        _prep_q(q_ref, gq_ref, qn_ref)
        _unroll_bias(line_ref, bias_ref, B_W)

        def first(rows, out, lse):
            o_ref[rows, :] = out
            lse_ref[rows, :] = lse

        def combine(rows, out, lse):
            old = lse_ref[rows, :]
            new = jnp.maximum(old, lse) + jnp.log(1.0 + jnp.exp(-jnp.abs(old - lse)))
            o_ref[rows, :] = o_ref[rows, :] * jnp.exp(old - new) + out * jnp.exp(lse - new)
            lse_ref[rows, :] = new

        for gi, d in enumerate(B_DILATIONS):
            @pl.when(g == gi)
            def _():
                _prep_kv(k_ref, v_ref, gk_ref, kp_ref, vp_ref, B_HALF_WINDOW * d)
                _fwd_tiles(qn_ref, kp_ref, vp_ref, bias_ref, first if gi == 0 else combine,
                           d=d, half_window=B_HALF_WINDOW)

    vec = pl.BlockSpec((1, LANES), lambda hp, g: (0, 0))
    padded = pltpu.VMEM((SEQ + 2 * B_PAD_MAX, LANES), F32)
    return pl.pallas_call(
        body, name="attn_b_fwd", grid=(4, 3),
        in_specs=[_seq_block(lambda hp, g: (0, QB_BLK + 4 * g + hp)), _seq_block(lambda hp, g: (0, KB_BLK + 4 * g + hp)),
                  _seq_block(lambda hp, g: (0, VB_BLK + 4 * g + hp)), vec, vec,
                  pl.BlockSpec((None, 2, _line_width(B_HALF_WINDOW)), lambda hp, g: (4 * g + hp, 0, 0))],
        out_specs=[_seq_block(lambda hp, g: (0, hp)), _seq_block(lambda hp, g: (0, hp))],
        out_shape=[jax.ShapeDtypeStruct((SEQ, 512), F32)] * 2,
        scratch_shapes=[pltpu.VMEM((SEQ, LANES), F32), padded, padded, pltpu.VMEM((2 * Q_BLOCK, B_W), F32)],
        compiler_params=_params("arbitrary", "arbitrary"),
    )(qkv, qkv, qkv, gain_q, gain_k, bias)


def _attn_b_bwd(qkv, gain_q, gain_k, bias, delta, lse, d_out):
    def body(q_ref, k_ref, v_ref, gq_ref, gk_ref, line_ref, delta_ref, lse_ref, do_ref,
             dq_out, dk_out, dv_out, dgq_out, dgk_out, dline_out,
             qn_ref, kp_ref, vp_ref, dq_ref, dk_ref, dv_ref, bias_ref, ds_out):
        g = pl.program_id(1)
        _prep_q(q_ref, gq_ref, qn_ref)
        _unroll_bias(line_ref, bias_ref, B_W)
        ds_out[...] = jnp.zeros_like(ds_out)
        for gi, d in enumerate(B_DILATIONS):
            @pl.when(g == gi)
            def _():
                pad = B_HALF_WINDOW * d
                for acc in (dk_ref, dv_ref):
                    acc[pl.ds(0, SEQ + 2 * pad), :] = jnp.zeros((SEQ + 2 * pad, LANES), F32)
                _prep_kv(k_ref, v_ref, gk_ref, kp_ref, vp_ref, pad)
                _bwd_tiles(qn_ref, kp_ref, vp_ref, bias_ref, do_ref, lse_ref, delta_ref, dq_ref, dk_ref, dv_ref,
                           ds_out, d=d, half_window=B_HALF_WINDOW)
                dgk_out[...] = _rows8(_norm_bwd(k_ref, gk_ref, dk_ref, pad, dk_out, 1.0))
                dv_out[...] = dv_ref[pl.ds(pad, SEQ), :].astype(BF16)
        _fold_bias_grad(ds_out, dline_out, B_W)
        dgq_out[...] = _rows8(_norm_bwd(q_ref, gq_ref, dq_ref, 0, dq_out, SCALE))

    vec = pl.BlockSpec((1, LANES), lambda hp, g: (0, 0))
    seq_f32 = pltpu.VMEM((SEQ, LANES), F32)
    padded = pltpu.VMEM((SEQ + 2 * B_PAD_MAX, LANES), F32)
    part = pl.BlockSpec((None, 8, LANES), lambda hp, g: (4 * g + hp, 0, 0))
    line = pl.BlockSpec((None, 2, _line_width(B_HALF_WINDOW)), lambda hp, g: (4 * g + hp, 0, 0))
    return pl.pallas_call(
        body, name="attn_b_bwd", grid=(4, 3),
        in_specs=[_seq_block(lambda hp, g: (0, QB_BLK + 4 * g + hp)), _seq_block(lambda hp, g: (0, KB_BLK + 4 * g + hp)),
                  _seq_block(lambda hp, g: (0, VB_BLK + 4 * g + hp)), vec, vec,
                  line,
                  _seq_block(lambda hp, g: (0, hp)), _seq_block(lambda hp, g: (0, hp)), _seq_block(lambda hp, g: (0, hp))],
        out_specs=[pl.BlockSpec((None, SEQ, LANES), lambda hp, g: (4 * g + hp, 0, 0))] * 3 + [part, part, line],
        out_shape=[jax.ShapeDtypeStruct((12, SEQ, LANES), BF16)] * 3
        + [jax.ShapeDtypeStruct((12, 8, LANES), F32)] * 2
        + [jax.ShapeDtypeStruct((12, 2, _line_width(B_HALF_WINDOW)), F32)],
        scratch_shapes=[seq_f32, padded, padded, seq_f32, padded, padded,
                        pltpu.VMEM((2 * Q_BLOCK, B_W), F32), pltpu.VMEM((2 * Q_BLOCK, B_W), F32)],
        compiler_params=_params("arbitrary", "arbitrary"),
    )(qkv, qkv, qkv, gain_q, gain_k, bias, delta, lse, d_out)


def _sigmoid(t):
    return 1.0 / (1.0 + jnp.exp(-t))


def _middle(out_a, out_b, gates, x, target, w_a, w_b, w_out, b_merge):
    tm = 256
    n_steps = SEQ // tm

    def body(oa_ref, ob_ref, g_ref, x_ref, t_ref, wa_ref, wb_ref, wo_ref, bm_ref,
             dy_ref, dg_ref, doa_ref, dob_ref, dla_ref, dlb_ref, dwa_ref, dwb_ref, dwo_ref, dbm_ref, sq_ref):
        @pl.when(pl.program_id(0) == 0)
        def _():
            for ref in (dwa_ref, dwb_ref, dwo_ref, dbm_ref, sq_ref):
                ref[...] = jnp.zeros_like(ref)

        gate_a, gate_b = g_ref[:, 0:512], g_ref[:, 512:1024]
        sig_a, sig_b = _sigmoid(gate_a), _sigmoid(gate_b)
        silu_a, silu_b = gate_a * sig_a, gate_b * sig_b
        oa, ob = oa_ref[...], ob_ref[...]
        ya, yb = (oa * silu_a).astype(BF16), (ob * silu_b).astype(BF16)
        br_a, br_b = _dot(ya, wa_ref[...]), _dot(yb, wb_ref[...])
        m0 = _sigmoid(g_ref[:, 1024:2048] + bm_ref[0:1, :])
        m1 = _sigmoid(g_ref[:, 2048:3072] + bm_ref[1:2, :])
        merged = (m0 * br_a + m1 * br_b).astype(BF16)
        err = (x_ref[...] + _dot(merged, wo_ref[...])) - t_ref[...]
        sq_ref[...] += jnp.sum(err * err, axis=0, keepdims=True)

        dy = err * (1.0 / D_MODEL)
        dy_ref[...] = dy
        dyb = dy.astype(BF16)
        dmerged = _dot(dyb, wo_ref[...], NT)
        dwo_ref[...] += _dot(merged, dyb, TN)
        dbr_a, dbr_b = (dmerged * m0).astype(BF16), (dmerged * m1).astype(BF16)
        dm0 = (dmerged * br_a) * (m0 * (1.0 - m0))
        dm1 = (dmerged * br_b) * (m1 * (1.0 - m1))
        dbm_ref[0:1, :] += jnp.sum(dm0, axis=0, keepdims=True)
        dbm_ref[1:2, :] += jnp.sum(dm1, axis=0, keepdims=True)
        for s in range(N_CHIPS):
            cols = slice(256 * s, 256 * (s + 1))
            dwa_ref[s] += _dot(ya, dbr_a[:, cols], TN)
            dwb_ref[s] += _dot(yb, dbr_b[:, cols], TN)
        dya, dyb_ = _dot(dbr_a, wa_ref[...], NT), _dot(dbr_b, wb_ref[...], NT)
        doa, dob = dya * silu_a, dyb_ * silu_b
        doa_ref[...] = doa
        dob_ref[...] = dob
        for blk in range(512 // LANES):
            lanes = slice(blk * LANES, (blk + 1) * LANES)
            dla_ref[:, lanes] = _half_sum(doa[:, lanes] * oa[:, lanes], None)
            dlb_ref[:, lanes] = _half_sum(dob[:, lanes] * ob[:, lanes], None)
        d_gates = (((dya * oa) * (sig_a * (1.0 + gate_a * (1.0 - sig_a)))).astype(BF16),
                   ((dyb_ * ob) * (sig_b * (1.0 + gate_b * (1.0 - sig_b)))).astype(BF16),
                   dm0.astype(BF16), dm1.astype(BF16))
        blk = 0
        for part in d_gates:
            for c0 in range(0, part.shape[1], 256):
                dg_ref[blk] = part[:, c0:c0 + 256]
                blk += 1

    def rows(width):
        return pl.BlockSpec((tm, width), lambda i: (i, 0))

    def whole(*shape):
        return pl.BlockSpec(shape, lambda i: (0,) * len(shape))

    return pl.pallas_call(
        body, name="middle", grid=(n_steps,),
        in_specs=[rows(512), rows(512), rows(GATE_WIDTH), rows(D_MODEL), rows(D_MODEL),
                  whole(512, D_MODEL), whole(512, D_MODEL), whole(D_MODEL, D_MODEL), whole(2, D_MODEL)],
        out_specs=[rows(D_MODEL), pl.BlockSpec((GATE_WIDTH // 256, tm, 256), lambda i: (0, i, 0)),
                   rows(512), rows(512), rows(512), rows(512),
                   whole(N_CHIPS, 512, 256), whole(N_CHIPS, 512, 256), whole(D_MODEL, D_MODEL),
                   whole(2, D_MODEL), whole(1, D_MODEL)],
        out_shape=[jax.ShapeDtypeStruct((SEQ, D_MODEL), F32), jax.ShapeDtypeStruct((GATE_WIDTH // 256, SEQ, 256), BF16),
                   jax.ShapeDtypeStruct((SEQ, 512), F32), jax.ShapeDtypeStruct((SEQ, 512), F32),
                   jax.ShapeDtypeStruct((SEQ, 512), F32), jax.ShapeDtypeStruct((SEQ, 512), F32),
                   jax.ShapeDtypeStruct((N_CHIPS, 512, 256), F32), jax.ShapeDtypeStruct((N_CHIPS, 512, 256), F32),
                   jax.ShapeDtypeStruct((D_MODEL, D_MODEL), F32), jax.ShapeDtypeStruct((2, D_MODEL), F32),
                   jax.ShapeDtypeStruct((1, D_MODEL), F32)],
        compiler_params=_params("arbitrary"),
    )(out_a, out_b, gates, x, target, w_a, w_b, w_out, b_merge)


def _which(j, edges, fns):
    lo = 0
    for hi, fn in zip(edges, fns):
        pl.when((j >= lo) & (j < hi))(fn)
        lo = hi


def _d_w_in(d_proj, h):
    plan, step, width = [], 0, 0
    for p in d_proj:
        total = p.shape[0] * p.shape[2]
        if width + total <= W_BLOCK:
            plan.append((p.shape[0], step, 1))
            width += total
            if width == W_BLOCK:
                step, width = step + 1, 0
        else:
            assert width == 0 and total % W_BLOCK == 0
            plan.append((W_BLOCK // p.shape[2], step, total // W_BLOCK))
            step += total // W_BLOCK
    assert width == 0 and step == IN_WIDTH // W_BLOCK
    firsts = sorted({first for _, first, _ in plan})
    edges = firsts[1:] + [step]
    halves = 2

    def body(*refs):
        pieces, h_ref, o_ref, acc_ref = refs[:-3], refs[-3], refs[-2], refs[-1]
        k = pl.program_id(1)

        def emit(group):
            def fn():
                cols = jnp.concatenate([ref[b] for ref in group for b in range(ref.shape[0])], axis=1)
                term = _dot(cols, h_ref[...], TN)

                @pl.when(k == 0)
                def _():
                    acc_ref[...] = term

                @pl.when(k == halves - 1)
                def _():
                    o_ref[...] = (acc_ref[...] + term).astype(BF16)
            return fn

        groups = [[ref for ref, (_, first, _) in zip(pieces, plan) if first == f] for f in firsts]
        _which(pl.program_id(0), edges, [emit(group) for group in groups])

    def cols_spec(piece, n, first, steps):
        def index(j, k):
            return jnp.clip(j - first, 0, steps - 1), jnp.where((j >= first) & (j < first + steps), k, 0), 0
        return pl.BlockSpec((n, SEQ // halves, piece.shape[2]), index)

    return pl.pallas_call(
        body, name="d_w_in", grid=(step, halves),
        in_specs=[cols_spec(p, *pl_) for p, pl_ in zip(d_proj, plan)]
        + [pl.BlockSpec((SEQ // halves, D_MODEL), lambda j, k: (k, 0))],
        out_specs=pl.BlockSpec((W_BLOCK, D_MODEL), lambda j, k: (j, 0)),
        out_shape=jax.ShapeDtypeStruct((IN_WIDTH, D_MODEL), BF16),
        scratch_shapes=[pltpu.VMEM((W_BLOCK, D_MODEL), F32)],
        compiler_params=_params("arbitrary", "arbitrary"),
    )(*d_proj, h)


RELAY_STEP = 10
RELAY_ROWS = 352


def _d_x(d_proj, w_t, x, gain, dy, chip_sums):
    tm = 256
    n_steps = SEQ // tm
    n_w = IN_WIDTH // W_BLOCK
    n_p, n_s = len(d_proj), len(chip_sums)

    def body(*refs):
        pieces, w_refs = refs[:n_p], refs[n_p:n_p + n_w]
        x_ref, g_ref, dy_ref = refs[n_p + n_w:n_p + n_w + 3]
        q_refs = refs[n_p + n_w + 3:n_p + n_w + 3 + n_s]
        dx_ref, dgain_ref = refs[n_p + n_w + 3 + n_s:n_p + n_w + 5 + n_s]
        outs = refs[n_p + n_w + 5 + n_s:n_p + n_w + 5 + 4 * n_s]
        got_refs, relay_refs, sum_refs = outs[:n_s], outs[n_s:2 * n_s], outs[2 * n_s:]
        if n_s:
            send_sems, recv_sems, local_sems, a_buf, b_buf, c_buf = refs[n_p + n_w + 5 + 4 * n_s:]

        def hops():
            cx, cy, c = lax.axis_index("x"), lax.axis_index("y"), lax.axis_index("c")
            near = (cx + (1 - c) - 2 * cx * (1 - c), cy + c - 2 * cy * c)
            far = (cx + c - 2 * cx * c, cy + (1 - c) - 2 * cy * (1 - c))
            chip = lambda p: 2 * p[0] + p[1]

            def copy(k, src, dst, to):
                return pltpu.make_async_remote_copy(src_ref=src, dst_ref=dst, send_sem=send_sems.at[k],
                                                    recv_sem=recv_sems.at[k], device_id=(*to, c), device_id_type=MESH)

            first = [(copy(3 * b, q.at[chip(near)], got.at[0], near),
                      copy(3 * b + 1, q.at[3 - chip((cx, cy))], relay, near))
                     for b, (q, got, relay) in enumerate(zip(q_refs, got_refs, relay_refs))]
            second = [copy(3 * b + 2, s, got.at[1], far) for b, (s, got) in enumerate(zip(sum_refs, got_refs))]
            return first, second, chip(far)

        @pl.when(pl.program_id(0) == 0)
        def _():
            dgain_ref[...] = jnp.zeros_like(dgain_ref)
            if n_s:
                for direct, pass_on in hops()[0]:
                    direct.start()
                    pass_on.start()

        if n_s:
            @pl.when(pl.program_id(0) == RELAY_STEP)
            def _():
                first, second, far_chip = hops()
                for b, (q, relay, total) in enumerate(zip(q_refs, relay_refs, sum_refs)):
                    first[b][1].wait_recv()
                    half = relay.shape[0]
                    for r0 in range(0, half, RELAY_ROWS):
                        rows = min(RELAY_ROWS, half - r0)
                        mine = pltpu.make_async_copy(q.at[far_chip, pl.ds(r0, rows), :], a_buf.at[pl.ds(0, rows), :],
                                                     local_sems.at[0])
                        theirs = pltpu.make_async_copy(relay.at[pl.ds(r0, rows), :], b_buf.at[pl.ds(0, rows), :],
                                                       local_sems.at[1])
                        mine.start()
                        theirs.start()
                        mine.wait()
                        theirs.wait()
                        c_buf[0:rows, :] = (a_buf[0:rows, :].astype(F32) + b_buf[0:rows, :].astype(F32)).astype(BF16)
                        store = pltpu.make_async_copy(c_buf.at[pl.ds(0, rows), :], total.at[pl.ds(r0, rows), :],
                                                      local_sems.at[2])
                        store.start()
                        store.wait()
                    second[b].start()

        blocks = [(piece, k) for piece in pieces for k in range(piece.shape[0])]
        dh, group, width, blk = None, [], 0, 0
        for piece, k in blocks:
            group.append(piece[k])
            width += piece.shape[2]
            if width == W_BLOCK:
                term = _dot(jnp.concatenate(group, axis=1), w_refs[blk][...])
                dh = term if dh is None else dh + term
                group, width, blk = [], 0, blk + 1
        assert not group and blk == n_w
        xf = x_ref[...]
        r = lax.rsqrt(jnp.mean(xf * xf, axis=-1, keepdims=True) + EPS)
        xh = xf * r
        dxh = dh * g_ref[...]
        dx_ref[...] = r * (dxh - xh * jnp.mean(dxh * xh, axis=-1, keepdims=True)) + dy_ref[...]
        dgain_ref[...] += _rows8(jnp.sum(dh * xh, axis=0, keepdims=True))

        if n_s:
            @pl.when(pl.program_id(0) == n_steps - 1)
            def _():
                first, second, _ = hops()
                for direct, pass_on in first:
                    direct.wait()
                    pass_on.wait_send()
                for cp in second:
                    cp.wait()

    row = pl.BlockSpec((tm, D_MODEL), lambda i: (i, 0))
    halves = [q.shape[1] for q in chip_sums]
    res = pl.pallas_call(
        body, name="d_x", grid=(n_steps,),
        in_specs=[pl.BlockSpec((p.shape[0], tm, p.shape[2]), lambda i: (0, i, 0)) for p in d_proj] + _w_blocks(0, n_w)
        + [row, pl.BlockSpec((1, D_MODEL), lambda i: (0, 0)), row] + [ANY] * n_s,
        out_specs=[row, pl.BlockSpec((8, D_MODEL), lambda i: (0, 0))] + [ANY] * (3 * n_s),
        out_shape=[jax.ShapeDtypeStruct((SEQ, D_MODEL), F32), jax.ShapeDtypeStruct((8, D_MODEL), F32)]
        + [jax.ShapeDtypeStruct((2, half, D_MODEL), BF16) for half in halves]
        + [jax.ShapeDtypeStruct((half, D_MODEL), BF16) for half in halves] * 2,
        scratch_shapes=[pltpu.SemaphoreType.DMA((3 * n_s,)), pltpu.SemaphoreType.DMA((3 * n_s,)),
                        pltpu.SemaphoreType.DMA((3,))] + [pltpu.VMEM((RELAY_ROWS, D_MODEL), BF16)] * 3 if n_s else [],
        compiler_params=_params("arbitrary"),
    )(*d_proj, *([w_t] * n_w), x, gain, dy, *chip_sums)
    return res[0], res[1], res[2:2 + n_s]


def _my_place():
    x, y, c = lax.axis_index("x"), lax.axis_index("y"), lax.axis_index("c")
    return jnp.stack([2 * x + y, c]).astype(jnp.int32)


def _half_rows(ref, half):
    rows = ref.shape[-2] // 2
    idx = (slice(None),) * (len(ref.shape) - 2) + (pl.ds(pl.multiple_of(half * rows, 16), rows), slice(None))
    return ref.at[idx]


def _swap_halves(grads):
    n = len(grads)

    def body(*refs):
        g_refs, o_refs, (send_sems, recv_sems) = refs[:n], refs[n:2 * n], refs[2 * n:]
        x, y, c = lax.axis_index("x"), lax.axis_index("y"), lax.axis_index("c")
        copies = [pltpu.make_async_remote_copy(src_ref=_half_rows(g, 1 - c), dst_ref=o, send_sem=send_sems.at[k],
                                               recv_sem=recv_sems.at[k], device_id=(x, y, 1 - c), device_id_type=MESH)
                  for k, (g, o) in enumerate(zip(g_refs, o_refs))]
        for cp in copies:
            cp.start()
        for cp in copies:
            cp.wait()

    return pl.pallas_call(
        body, name="reduce_swap_halves", in_specs=[ANY] * n, out_specs=[ANY] * n,
        out_shape=[jax.ShapeDtypeStruct((N_CHIPS, g.shape[1] // 2, D_MODEL), g.dtype) for g in grads],
        scratch_shapes=[pltpu.SemaphoreType.DMA((n,)), pltpu.SemaphoreType.DMA((n,))],
    )(*grads)


def _add_halves(place, grads, theirs, name):
    half = theirs.shape[1]
    tr = _row_tile(half)
    n = half // tr

    def body(place_ref, g_ref, t_ref, o_ref):
        o_ref[...] = (g_ref[...].astype(F32) + t_ref[...].astype(F32)).astype(BF16)

    return pl.pallas_call(
        body, name=name,
        grid_spec=pltpu.PrefetchScalarGridSpec(
            num_scalar_prefetch=1, grid=(N_CHIPS, n),
            in_specs=[pl.BlockSpec((None, tr, D_MODEL), lambda s, i, p: (s, p[1] * n + i, 0)),
                      pl.BlockSpec((None, tr, D_MODEL), lambda s, i, p: (s, i, 0))],
            out_specs=pl.BlockSpec((None, tr, D_MODEL), lambda s, i, p: (s, i, 0))),
        out_shape=jax.ShapeDtypeStruct((N_CHIPS, half, D_MODEL), BF16),
        compiler_params=_params("arbitrary", "arbitrary"),
    )(place, grads, theirs)


def _add_chips(place, chip_sums, others, name):
    half = others.shape[1]
    tr = _row_tile(half)
    n = half // tr

    def body(place_ref, q_ref, o_ref, r_ref):
        acc = q_ref[...].astype(F32)
        for j in range(others.shape[0]):
            acc = acc + o_ref[j].astype(F32)
        r_ref[...] = acc

    return pl.pallas_call(
        body, name=name,
        grid_spec=pltpu.PrefetchScalarGridSpec(
            num_scalar_prefetch=1, grid=(n,),
            in_specs=[pl.BlockSpec((None, tr, D_MODEL), lambda i, p: (p[0], i, 0)),
                      pl.BlockSpec((others.shape[0], tr, D_MODEL), lambda i, p: (0, i, 0))],
            out_specs=pl.BlockSpec((tr, D_MODEL), lambda i, p: (p[1] * n + i, 0))),
        out_shape=jax.ShapeDtypeStruct((2 * half, D_MODEL), F32),
        compiler_params=_params("arbitrary"),
    )(place, chip_sums, others)


def _join_halves(shards, block):
    n = len(shards)
    rows = block.shape[0]

    def body(*refs):
        b_ref, o_refs, sum_ref = refs[n], refs[n + 1:2 * n + 1], refs[2 * n + 1]
        send_sems, recv_sems, small_send, small_recv, local_sem, all_ref = refs[2 * n + 2:]
        x, y, c = lax.axis_index("x"), lax.axis_index("y"), lax.axis_index("c")
        me, sibling = (x, y, c), (x, y, 1 - c)
        chips = [(1 - x, y), (x, 1 - y), (1 - x, 1 - y)]

        def half(k, rows_ref):
            return pltpu.make_async_remote_copy(src_ref=rows_ref, dst_ref=rows_ref, send_sem=send_sems.at[k],
                                                recv_sem=recv_sems.at[k], device_id=sibling, device_id_type=MESH)

        def at(px, py, pc):
            return all_ref.at[pl.ds(pl.multiple_of((4 * px + 2 * py + pc) * rows, 8), rows), :]

        def small(k, block_of, to, src=None):
            return pltpu.make_async_remote_copy(src_ref=at(*block_of) if src is None else src, dst_ref=at(*block_of),
                                                send_sem=small_send.at[k], recv_sem=small_recv.at[k],
                                                device_id=to, device_id_type=MESH)

        sends = [half(k, _half_rows(o, c)) for k, o in enumerate(o_refs)]
        for cp in sends:
            cp.start()
        mine = pltpu.make_async_copy(b_ref, at(*me), local_sem)
        mine.start()
        first = [small(0, me, sibling, src=b_ref)]
        first += [small(1 + j, me, (*chip, c), src=b_ref) for j, chip in enumerate(chips)]
        for cp in first:
            cp.start()
        passed = [small(4 + j, (*chip, c), sibling) for j, chip in enumerate(chips)]
        for j, chip in enumerate(chips):
            small(1 + j, (*chip, c), me).wait_recv()
            passed[j].start()
        small(0, sibling, me).wait_recv()
        for j, chip in enumerate(chips):
            small(4 + j, (*chip, 1 - c), me).wait_recv()
        mine.wait()
        acc = all_ref[0:rows, :]
        for dev in range(1, 8):
            acc = acc + all_ref[rows * dev:rows * (dev + 1), :]
        sum_ref[...] = acc
        for k, o in enumerate(o_refs):
            half(k, _half_rows(o, 1 - c)).wait_recv()
        for cp in sends + first + passed:
            cp.wait_send()

    res = pl.pallas_call(
        body, name="reduce_join_halves", in_specs=[ANY] * n + [pl.BlockSpec(memory_space=pltpu.VMEM)],
        out_specs=[ANY] * n + [pl.BlockSpec(memory_space=pltpu.VMEM)],
        out_shape=[jax.ShapeDtypeStruct(s.shape, F32) for s in shards] + [jax.ShapeDtypeStruct(block.shape, F32)],
        input_output_aliases={k: k for k in range(n)},
        scratch_shapes=[pltpu.SemaphoreType.DMA((n,)), pltpu.SemaphoreType.DMA((n,)),
                        pltpu.SemaphoreType.DMA((7,)), pltpu.SemaphoreType.DMA((7,)), pltpu.SemaphoreType.DMA,
                        pltpu.VMEM((8 * rows, D_MODEL), F32)],
    )(*shards, block)
    return res[:n], res[n]


def _adamw_math(w, g, m, v):
    m = ADAM_B1 * m + (1.0 - ADAM_B1) * g
    v = ADAM_B2 * v + (1.0 - ADAM_B2) * (g * g)
    m_hat = m / (1.0 - ADAM_B1 ** ADAM_STEP)
    v_hat = v / (1.0 - ADAM_B2 ** ADAM_STEP)
    return -ADAM_LR * (m_hat / (jnp.sqrt(v_hat) + ADAM_EPS) + ADAM_WD * w), m, v


def _adamw(w, g, m, v, name):
    r, c = w.shape
    tr = _row_tile(r)

    def body(w_ref, g_ref, m_ref, v_ref, d_ref, nm_ref, nv_ref):
        d_ref[...], nm_ref[...], nv_ref[...] = _adamw_math(w_ref[...], g_ref[...], m_ref[...], v_ref[...])

    spec = pl.BlockSpec((tr, c), lambda i: (i, 0))
    return pl.pallas_call(
        body, name=name, grid=(r // tr,), in_specs=[spec] * 4, out_specs=[spec] * 3,
        out_shape=[jax.ShapeDtypeStruct((r, c), F32)] * 3, compiler_params=_params("arbitrary"),
    )(w, g, m, v)


def _adamw_small(ws, gs, ms, vs):
    n = len(ws)

    def body(*refs):
        ins, outs = refs[:4 * n], refs[4 * n:]
        for k in range(n):
            d, m, v = _adamw_math(ins[k][...], ins[n + k][...], ins[2 * n + k][...], ins[3 * n + k][...])
            outs[k][...], outs[n + k][...], outs[2 * n + k][...] = d, m, v

    shapes = [jax.ShapeDtypeStruct(w.shape, F32) for w in ws]
    res = pl.pallas_call(body, name="adamw_small", out_shape=shapes * 3)(*ws, *gs, *ms, *vs)
    return res[:n], res[n:2 * n], res[2 * n:]


def _fold_heads(partials):
    t = jnp.sum(partials[:, 0, :], axis=0)
    return (t[:HEAD_DIM] + t[HEAD_DIM:]).reshape(1, HEAD_DIM)


def _local_step(x, target, norm_gain, w_t, w_a, w_b, w_o, b_m, q_norm_a, k_norm_a, q_norm_b, k_norm_b, sink_a,
                rel_bias, start_reduce=None):
    two = lambda gain: jnp.concatenate([gain, gain], axis=1)
    bias_a = _bias_lines(rel_bias[:, :8], A_HALF_WINDOW, 1)
    bias_b = jnp.concatenate([_bias_lines(rel_bias[:, 8 + 8 * g:16 + 8 * g], B_HALF_WINDOW, d)
                              for g, d in enumerate(B_DILATIONS)], axis=0)

    qkv, h = _in_proj(x, norm_gain, w_t, 0, QKV_WIDTH // W_BLOCK, BF16, "in_proj_qkv", True)
    gates, = _in_proj(x, norm_gain, w_t, QKV_WIDTH // W_BLOCK, GATE_WIDTH // W_BLOCK, F32, "in_proj_gates", False)
    out_a, lse_a = _attn_a_fwd(qkv, two(q_norm_a), two(k_norm_a), bias_a, sink_a)
    out_b, lse_b = _attn_b_fwd(qkv, two(q_norm_b), two(k_norm_b), bias_b)

    dy, dgates, d_out_a, d_out_b, delta_a, delta_b, d_wa, d_wb, d_wo, d_bm, sq = _middle(
        out_a, out_b, gates, x, target, w_a, w_b, w_o, b_m)
    loss = (0.5 / D_MODEL) * jnp.sum(sq)

    dq_a, dkv_a, dgq_a, dgk_a, ds_a, dsink = _attn_a_bwd(
        qkv, two(q_norm_a), two(k_norm_a), bias_a, sink_a, delta_a, lse_a, d_out_a)
    dq_b, dk_b, dv_b, dgq_b, dgk_b, ds_b = _attn_b_bwd(
        qkv, two(q_norm_b), two(k_norm_b), bias_b, delta_b, lse_b, d_out_b)
    d_proj = (dq_a, dkv_a, dq_b, dk_b, dv_b, dgates)

    d_bm_rows = jnp.pad(d_bm.reshape(2, N_CHIPS, 256).transpose(1, 0, 2),
                        ((0, 0), (0, REST_ROWS - 514), (0, D_MODEL - 256)))
    rest = jnp.concatenate([d_wo.reshape(N_CHIPS, 256, D_MODEL), d_wa.reshape(N_CHIPS, 128, D_MODEL),
                            d_wb.reshape(N_CHIPS, 128, D_MODEL), d_bm_rows], axis=1)
    grads = [_d_w_in(d_proj, h).reshape(N_CHIPS, W_IN_SHARD, D_MODEL), rest]
    narrow = [grads[0], rest.astype(BF16)]
    chip_sums = start_reduce(grads, narrow) if start_reduce is not None else []
    grad_x, d_gain, others = _d_x(d_proj, w_t, x, norm_gain, dy, chip_sums)

    d_rel = jnp.concatenate(
        [_bias_grad(ds_a, A_HALF_WINDOW, 1)]
        + [_bias_grad(ds_b[4 * g:4 * g + 4], B_HALF_WINDOW, d) for g, d in enumerate(B_DILATIONS)], axis=1)
    d_sink = jnp.sum(dsink, axis=(2, 3)).reshape(1, 8)
    dgk_a_row = dgk_a[0]
    small = jnp.zeros((8, D_MODEL), F32)
    small = small.at[0].set(d_gain[0])
    small = small.at[1].set(d_rel.reshape(-1))
    misc = jnp.concatenate([_fold_heads(dgq_a), (dgk_a_row[:HEAD_DIM] + dgk_a_row[HEAD_DIM:]).reshape(1, HEAD_DIM),
                            _fold_heads(dgq_b), _fold_heads(dgk_b), d_sink], axis=1)
    small = small.at[2, :264].set(misc[0])

    return loss, grad_x, grads, small, chip_sums, others


def _unpack_weights(w_t_all, small_all):
    sm = small_all.reshape(N_CHIPS, SMALL_ROWS, D_MODEL)
    w_o = sm[:, 0:256].reshape(D_MODEL, D_MODEL)
    w_a = sm[:, 256:384].reshape(N_CHIPS, 512, 256).transpose(1, 0, 2).reshape(512, D_MODEL)
    w_b = sm[:, 384:512].reshape(N_CHIPS, 512, 256).transpose(1, 0, 2).reshape(512, D_MODEL)
    b_m = lax.bitcast_convert_type(sm[:, 512].reshape(N_CHIPS, 2, 256, 2), F32)
    return w_t_all, w_a, w_b, w_o, b_m.transpose(1, 0, 2).reshape(2, D_MODEL)


def _pack_small_weights(w_branch_a, w_branch_b, b_merge, w_out):
    b_m = jnp.pad(lax.bitcast_convert_type(b_merge, BF16).reshape(1, D_MODEL), ((0, SMALL_ROWS - 513), (0, 0)))
    return jnp.concatenate([w_out.astype(BF16), w_branch_a.astype(BF16).reshape(128, D_MODEL),
                            w_branch_b.astype(BF16).reshape(128, D_MODEL), b_m], axis=0)


def kernel(x, norm_gain, w_in, q_norm_a, k_norm_a, q_norm_b, k_norm_b, sink_a, rel_bias, w_branch_a, w_branch_b, b_merge, w_out, loss_target, m_norm_gain, m_w_in, m_q_norm_a, m_k_norm_a, m_q_norm_b, m_k_norm_b, m_sink_a, m_rel_bias, m_w_branch_a, m_w_branch_b, m_b_merge, m_w_out, v_norm_gain, v_w_in, v_q_norm_a, v_k_norm_a, v_q_norm_b, v_k_norm_b, v_sink_a, v_rel_bias, v_w_branch_a, v_w_branch_b, v_b_merge, v_w_out):
    w_in_t, m_w_in_t, v_w_in_t = (jnp.transpose(t[0]) for t in (w_in, m_w_in, v_w_in))
    wt_shard = _cast_rows(w_in_t, BF16, "w_in_cast")
    w_t, w_a, w_b, w_o, b_m = _unpack_weights(
        *_gather_weights(wt_shard, _pack_small_weights(w_branch_a[0], w_branch_b[0], b_merge[0], w_out[0])))

    place = _my_place()
    names = ("w_in", "rest")

    def start_reduce(grads, narrow):
        return [_add_halves(place, g, t, "reduce_add_halves_" + n) for g, t, n in zip(grads, _swap_halves(narrow), names)]

    loss_part, grad_x, _, small, chip_sums, others = _local_step(
        x[0], loss_target[0], norm_gain, w_t, w_a, w_b, w_o, b_m, q_norm_a, k_norm_a, q_norm_b, k_norm_b,
        sink_a, rel_bias, start_reduce)

    (g_wt, g_rest), small = _join_halves(
        [_add_chips(place, q, o, "reduce_add_chips_" + n) for q, o, n in zip(chip_sums, others, names)],
        small.at[3, 0].set(loss_part))
    loss = small[3, 0]

    g_w_out = g_rest[0:256]
    g_w_a = g_rest[256:384].reshape(512, 256)
    g_w_b = g_rest[384:512].reshape(512, 256)
    g_b_merge = g_rest[512:514, :256]
    g_norm_gain = small[0:1]
    g_rel_bias = small[1].reshape(N_BUCKETS, N_BUCKETS)
    g_q_a, g_k_a, g_q_b, g_k_b = (small[2:3, 64 * k:64 * k + 64] for k in range(4))
    g_sink = small[2:3, 256:264]

    big_names = (("w_branch_a", w_branch_a, g_w_a, m_w_branch_a, v_w_branch_a),
                 ("w_branch_b", w_branch_b, g_w_b, m_w_branch_b, v_w_branch_b),
                 ("w_out", w_out, g_w_out, m_w_out, v_w_out))
    upd = {name: (g,) + tuple(_adamw(w[0], g, m[0], v[0], "adamw_" + name)) for name, w, g, m, v in big_names}
    upd["w_in"] = tuple(jnp.transpose(t) for t in (g_wt,) + tuple(_adamw(w_in_t, g_wt, m_w_in_t, v_w_in_t, "adamw_w_in")))
    small_names = ("norm_gain", "q_norm_a", "k_norm_a", "q_norm_b", "k_norm_b", "sink_a", "rel_bias", "b_merge")
    ws = [norm_gain, q_norm_a, k_norm_a, q_norm_b, k_norm_b, sink_a, rel_bias, b_merge[0]]
    gs = [g_norm_gain, g_q_a, g_k_a, g_q_b, g_k_b, g_sink, g_rel_bias, g_b_merge]
    ms = [m_norm_gain, m_q_norm_a, m_k_norm_a, m_q_norm_b, m_k_norm_b, m_sink_a, m_rel_bias, m_b_merge[0]]
    vs = [v_norm_gain, v_q_norm_a, v_k_norm_a, v_q_norm_b, v_k_norm_b, v_sink_a, v_rel_bias, v_b_merge[0]]
    ds, nms, nvs = _adamw_small(ws, gs, ms, vs)
    for k, name in enumerate(small_names):
        upd[name] = (gs[k], ds[k], nms[k], nvs[k])

    order = ("norm_gain", "w_in", "q_norm_a", "k_norm_a", "q_norm_b", "k_norm_b", "sink_a", "rel_bias",
             "w_branch_a", "w_branch_b", "b_merge", "w_out")
    lead = {"w_in", "w_branch_a", "w_branch_b", "b_merge", "w_out"}
    outs = [loss, grad_x[None]]
    for part in range(4):
        outs += [upd[name][part][None] if name in lead else upd[name][part] for name in order]
    return tuple(outs)
```

```python
import math

import numpy as np
import jax
import jax.numpy as jnp
from jax import lax
from jax.experimental import pallas as pl
from jax.experimental.pallas import tpu as pltpu

F32 = jnp.float32
BF16 = jnp.bfloat16

SEQ = 4096
D_MODEL = 1024
HEAD_DIM = 64
LANES = 128
EPS = 1e-6
NEG_INF = -1e30
SCALE = HEAD_DIM ** -0.5
N_BUCKETS = 32
MAX_DISTANCE = 1024
N_CHIPS = 4

A_HALF_WINDOW = 128
B_HALF_WINDOW = 64
B_DILATIONS = (1, 4, 16)
Q_BLOCK = 128

QKV_WIDTH = 5376
GATE_WIDTH = 3072
QA_BLK, KA_BLK, VA_BLK = 0, 4, 5
QB_BLK, KB_BLK, VB_BLK = 6, 18, 30
IN_WIDTH = QKV_WIDTH + GATE_WIDTH
W_IN_SHARD = IN_WIDTH // N_CHIPS

SMALL_ROWS = 544
REST_ROWS = 544

ADAM_LR = 0.001
ADAM_B1 = 0.9
ADAM_B2 = 0.999
ADAM_EPS = 1e-08
ADAM_WD = 0.01
ADAM_STEP = 10

VMEM_LIMIT = 56 * 1024 * 1024

NT = (((1,), (1,)), ((), ()))
TN = (((0,), (0,)), ((), ()))
MESH = pl.DeviceIdType.MESH
ANY = pl.BlockSpec(memory_space=pl.ANY)


def _dot(a, b, dims=None):
    if dims is None:
        return jnp.dot(a, b, preferred_element_type=F32)
    return lax.dot_general(a, b, dims, preferred_element_type=F32)


def _params(*semantics):
    return pltpu.CompilerParams(dimension_semantics=semantics or None, vmem_limit_bytes=VMEM_LIMIT)


def _line_width(half_window):
    return pl.cdiv(2 * Q_BLOCK + 2 * half_window - 1, LANES) * LANES


def _bucket_onehot(half_window, stride):
    rel = np.arange(_line_width(half_window)) - (Q_BLOCK - 1) - half_window
    band = np.abs(rel) <= half_window
    rel = rel * stride
    half, max_exact = N_BUCKETS // 2, N_BUCKETS // 4
    n = np.abs(rel)
    nf = np.maximum(n, max_exact).astype(np.float32)
    large = max_exact + (np.log(nf / np.float32(max_exact)) / np.float32(math.log(MAX_DISTANCE / max_exact))
                         * np.float32(half - max_exact)).astype(np.int32)
    large = np.minimum(large, half - 1)
    bucket = (rel > 0).astype(np.int32) * half + np.where(n < max_exact, n, large)
    onehot = (bucket[..., None] == np.arange(N_BUCKETS)) & band[..., None]
    return onehot.astype(np.float32), band


def _bias_lines(rel_bias_cols, half_window, stride):
    onehot, band = _bucket_onehot(half_window, stride)
    h = rel_bias_cols.shape[1]
    t = jnp.einsum("tb,bh->ht", jnp.asarray(onehot), rel_bias_cols, precision=lax.Precision.HIGHEST)
    t = t + jnp.asarray(np.where(band, 0.0, NEG_INF).astype(np.float32))
    return t.reshape(h // 2, 2, -1)


def _bias_grad(d_lines, half_window, stride):
    onehot, _ = _bucket_onehot(half_window, stride)
    h = d_lines.shape[0] * 2
    return jnp.einsum("tb,ht->bh", jnp.asarray(onehot), d_lines.reshape(h, -1), precision=lax.Precision.HIGHEST)


def _unroll_bias(line_ref, tile_ref, w):
    width = line_ref.shape[1]
    for j in range(2):
        rows = jnp.broadcast_to(line_ref[j:j + 1, :], (Q_BLOCK, width))
        rows = pltpu.roll(rows, width - (Q_BLOCK - 1), 1, stride=1, stride_axis=0)
        tile_ref[j * Q_BLOCK:(j + 1) * Q_BLOCK, :] = rows[:, :w]


def _fold_bias_grad(tile_ref, line_ref, w):
    width = line_ref.shape[1]
    row = lax.broadcasted_iota(jnp.int32, (Q_BLOCK, Q_BLOCK), 0)
    col = lax.broadcasted_iota(jnp.int32, (Q_BLOCK, Q_BLOCK), 1)
    flip = jnp.where(row + col == Q_BLOCK - 1, 1.0, 0.0).astype(BF16)
    for j in range(2):
        tile = tile_ref[j * Q_BLOCK:(j + 1) * Q_BLOCK, :]
        hi = tile.astype(BF16)
        lo = (tile - hi.astype(F32)).astype(BF16)
        rows = _dot(flip, hi) + _dot(flip, lo)
        rows = jnp.concatenate([rows, jnp.zeros((Q_BLOCK, width - w), F32)], axis=1)
        rows = pltpu.roll(rows, 0, 1, stride=1, stride_axis=0)
        line_ref[j:j + 1, :] = jnp.sum(rows, axis=0, keepdims=True)


def _row_tile(rows):
    return max(t for t in range(16, 385, 16) if rows % t == 0)


def _cast_rows(w, out_dtype, name):
    r, c = w.shape
    tr = _row_tile(r)

    def body(w_ref, o_ref):
        o_ref[...] = w_ref[...].astype(out_dtype)

    spec = pl.BlockSpec((tr, c), lambda i: (i, 0))
    return pl.pallas_call(
        body, name=name, grid=(r // tr,), in_specs=[spec], out_specs=spec,
        out_shape=jax.ShapeDtypeStruct((r, c), out_dtype), compiler_params=_params("arbitrary"),
    )(w)


STAGE_ROWS = 528


def _gather_scratch():
    return [pltpu.SemaphoreType.DMA((6,)), pltpu.SemaphoreType.DMA((6,)), pltpu.SemaphoreType.DMA((2,)),
            pltpu.SemaphoreType.DMA((2,)), pltpu.VMEM((2, STAGE_ROWS, D_MODEL), BF16)]


def _gather_phases(src_ref, out_ref, send_sems, recv_sems, in_sems, out_sems, stage):
    rows = src_ref.shape[0]
    x, y, c = lax.axis_index("x"), lax.axis_index("y"), lax.axis_index("c")
    sibling = (x, y, 1 - c)
    near = (x + (1 - c) - 2 * x * (1 - c), y + c - 2 * y * c)
    far = (x + c - 2 * x * c, y + (1 - c) - 2 * y * (1 - c))
    diag = (1 - x, 1 - y)
    chip_no = lambda chip: 2 * chip[0] + chip[1]
    my_chip = chip_no((x, y))

    def half_of(chip, half):
        start = pl.multiple_of(chip * rows + half * (rows // 2), 16)
        return out_ref.at[pl.ds(start, rows // 2), :]

    def copy(k, src, dst, to):
        return pltpu.make_async_remote_copy(src_ref=src, dst_ref=dst, send_sem=send_sems.at[k],
                                            recv_sem=recv_sems.at[k], device_id=to, device_id_type=MESH)

    mine = src_ref.at[pl.ds(pl.multiple_of(c * (rows // 2), 16), rows // 2), :]

    def keep_own():
        outs = []
        for i, r0 in enumerate(range(0, rows, STAGE_ROWS)):
            n = min(STAGE_ROWS, rows - r0)
            slot = i % 2
            if i >= 2:
                outs[i - 2].wait()
            buf = stage.at[slot, pl.ds(0, n), :]
            load = pltpu.make_async_copy(src_ref.at[pl.ds(r0, n), :], buf, in_sems.at[slot])
            load.start()
            load.wait()
            start = pl.multiple_of(my_chip * rows + r0, 16)
            outs.append(pltpu.make_async_copy(buf, out_ref.at[pl.ds(start, n), :], out_sems.at[slot]))
            outs[i].start()
        for cp in outs[-2:]:
            cp.wait()

    def start():
        copy(0, mine, half_of(my_chip, c), (*near, c)).start()
        copy(1, mine, half_of(my_chip, c), (*far, c)).start()
        keep_own()

    def pass_on(j, chip):
        landed = half_of(chip_no(chip), c)
        copy(3 + j, landed, landed, sibling).start()

    def relay():
        landed = half_of(chip_no(near), c)
        copy(0, landed, landed, sibling).wait_recv()
        copy(2, landed, landed, (*far, c)).start()
        pass_on(0, near)

    def forward():
        for j, chip in ((1, far), (2, diag)):
            landed = half_of(chip_no(chip), c)
            copy(j, landed, landed, sibling).wait_recv()
            pass_on(j, chip)

    def finish():
        for j, chip in ((0, far), (1, near), (2, diag)):
            other = half_of(chip_no(chip), 1 - c)
            copy(3 + j, other, other, sibling).wait_recv()
        for k in range(6):
            copy(k, mine, mine, sibling).wait_send()

    return start, relay, forward, finish


def _gather_weights(shard):
    def body(src_ref, out_ref, *scratch):
        for phase in _gather_phases(src_ref, out_ref, *scratch):
            phase()

    return pl.pallas_call(
        body, name="gather_weights", in_specs=[ANY], out_specs=ANY,
        out_shape=jax.ShapeDtypeStruct((N_CHIPS * shard.shape[0], D_MODEL), BF16),
        scratch_shapes=_gather_scratch(),
    )(shard)


W_BLOCK = 768


def _w_blocks(first, count):
    return [pl.BlockSpec((W_BLOCK, D_MODEL), lambda *_, k=k: (first + k, 0)) for k in range(count)]


def _in_proj(x, gain, w_t, first_block, n_blocks, out_dtype, name, keep_h, ride=None):
    tm = 512
    n_steps = SEQ // tm
    n_out = 2 if keep_h else 1

    def body(x_ref, g_ref, *refs):
        w_refs, outs = refs[:n_blocks], refs[n_blocks + (ride is not None):n_blocks + (ride is not None) + n_out]
        if ride is not None:
            phases = _gather_phases(refs[n_blocks], *refs[n_blocks + 1 + n_out:])
            for step, phase in zip((0, 2, 4, n_steps - 1), phases):
                pl.when(pl.program_id(0) == step)(phase)
        xf = x_ref[...]
        r = lax.rsqrt(jnp.mean(xf * xf, axis=-1, keepdims=True) + EPS)
        h = ((xf * r) * g_ref[...]).astype(BF16)
        if keep_h:
            outs[1][...] = h
        for k, w_ref in enumerate(w_refs):
            outs[0][:, k * W_BLOCK:(k + 1) * W_BLOCK] = _dot(h, w_ref[...], NT).astype(out_dtype)

    riding = [] if ride is None else [ride]
    return pl.pallas_call(
        body, name=name, grid=(n_steps,),
        in_specs=[pl.BlockSpec((tm, D_MODEL), lambda i: (i, 0)), pl.BlockSpec((1, D_MODEL), lambda i: (0, 0))]
        + _w_blocks(first_block, n_blocks) + [ANY for _ in riding],
        out_specs=[pl.BlockSpec((tm, W_BLOCK * n_blocks), lambda i: (i, 0)),
                   pl.BlockSpec((tm, D_MODEL), lambda i: (i, 0))][:n_out] + [ANY for _ in riding],
        out_shape=[jax.ShapeDtypeStruct((SEQ, W_BLOCK * n_blocks), out_dtype),
                   jax.ShapeDtypeStruct((SEQ, D_MODEL), BF16)][:n_out]
        + [jax.ShapeDtypeStruct((N_CHIPS * r.shape[0], D_MODEL), BF16) for r in riding],
        scratch_shapes=_gather_scratch() if riding else [],
        compiler_params=_params("arbitrary"),
    )(x, gain, *([w_t] * n_blocks), *riding)


CHUNK = 256
CHUNK_UNROLL = 4
TILE_UNROLL = 8


def _low_half():
    return lax.broadcasted_iota(jnp.int32, (1, LANES), 1) < HEAD_DIM


def _half_sum(v, low):
    del low
    row = lax.broadcasted_iota(jnp.int32, (2 * LANES, LANES), 0)
    col = lax.broadcasted_iota(jnp.int32, (2 * LANES, LANES), 1)
    ones = jnp.where((row % LANES) // HEAD_DIM == col // HEAD_DIM, 1.0, 0.0).astype(BF16)
    hi = v.astype(BF16)
    lo = (v - hi.astype(F32)).astype(BF16)
    return _dot(jnp.concatenate([hi, lo], axis=1), ones)


def _chunks(fn, init=0):
    def body(i, carry):
        for u in range(CHUNK_UNROLL):
            carry = fn(pl.multiple_of((i * CHUNK_UNROLL + u) * CHUNK, CHUNK), carry)
        return carry

    return lax.fori_loop(0, SEQ // (CHUNK * CHUNK_UNROLL), body, init)


def _inv_rms(t, low):
    del low
    row = lax.broadcasted_iota(jnp.int32, (LANES, LANES), 0)
    col = lax.broadcasted_iota(jnp.int32, (LANES, LANES), 1)
    ones = jnp.where(row // HEAD_DIM == col // HEAD_DIM, 1.0, 0.0).astype(BF16)
    return lax.rsqrt(_dot((t * t).astype(BF16), ones) * (1.0 / HEAD_DIM) + EPS)


def _prep_q(q_ref, gain_ref, qn_ref):
    low = _low_half()

    def step(r0, carry):
        q = q_ref[pl.ds(r0, CHUNK), :].astype(F32)
        qn_ref[pl.ds(r0, CHUNK), :] = ((q * _inv_rms(q, low)) * gain_ref[...]) * SCALE
        return carry

    _chunks(step)


def _own_half(t, keep):
    return jnp.where(keep, t, pltpu.roll(t, HEAD_DIM, 1))


def _prep_kv(k_ref, v_ref, gain_ref, kp_ref, vp_ref, pad, keep=None):
    low = _low_half()
    zeros = jnp.zeros((pad, LANES), F32)
    for ref in (kp_ref, vp_ref):
        ref[pl.ds(0, pad), :] = zeros
        ref[pl.ds(pad + SEQ, pad), :] = zeros

    def step(r0, carry):
        k = k_ref[pl.ds(r0, CHUNK), :].astype(F32)
        v = v_ref[pl.ds(r0, CHUNK), :].astype(F32)
        kn = (k * _inv_rms(k, low)) * gain_ref[...]
        if keep is not None:
            kn, v = _own_half(kn, keep), _own_half(v, keep)
        kp_ref[pl.ds(pad + r0, CHUNK), :] = kn
        vp_ref[pl.ds(pad + r0, CHUNK), :] = v
        return carry

    _chunks(step)


def _tiles(d, half_window, fn):
    w = Q_BLOCK + 2 * half_window
    length = SEQ // d
    n_blocks = length // Q_BLOCK
    col = lax.broadcasted_iota(jnp.int32, (1, w), 1)

    def step(it, carry):
        c, n = it // n_blocks, it % n_blocks
        start = c + (d * Q_BLOCK) * n
        if d == 1:
            start = pl.multiple_of(start, Q_BLOCK)
            q_rows, k_rows = pl.ds(start, Q_BLOCK), pl.ds(start, w)
        else:
            q_rows, k_rows = pl.ds(start, Q_BLOCK, stride=d), pl.ds(start, w, stride=d)
        t = n * Q_BLOCK - half_window + col
        edge = jnp.where((t < 0) | (t >= length), NEG_INF, 0.0)
        fn(q_rows, k_rows, edge)
        return carry

    lax.fori_loop(0, d * n_blocks, step, 0, unroll=TILE_UNROLL)


def _stack_heads(t, low):
    return jnp.concatenate([jnp.where(low, t, 0.0), jnp.where(low, 0.0, t)], axis=0).astype(BF16)


def _unstack_heads(t, low):
    return jnp.where(low, t[:Q_BLOCK], t[Q_BLOCK:])


def _per_head(pair):
    return jnp.concatenate([jnp.full((Q_BLOCK, 1), pair[0], F32), jnp.full((Q_BLOCK, 1), pair[1], F32)], axis=0)


def _fwd_tiles(qn_ref, kp_ref, vp_ref, bias_ref, emit, *, d, half_window, sinks=None):
    low = _low_half()
    w = Q_BLOCK + 2 * half_window
    sink = None if sinks is None else _per_head(sinks)

    def tile(q_rows, k_rows, edge):
        q2 = _stack_heads(qn_ref[q_rows, :], low)
        k = kp_ref[k_rows, :].astype(BF16)
        v1 = jnp.concatenate([vp_ref[k_rows, :], jnp.ones((w, LANES), F32)], axis=1).astype(BF16)
        s = _dot(q2, k, NT) + bias_ref[...] + edge
        m = jnp.max(s, axis=-1, keepdims=True)
        if sink is not None:
            m = jnp.maximum(m, sink)
        o = _dot(jnp.exp(s - m).astype(BF16), v1)
        l = o[:, LANES:]
        if sink is not None:
            l = l + jnp.exp(sink - m)
        emit(q_rows, _unstack_heads(o[:, :LANES] * (1.0 / l), low), _unstack_heads(m + jnp.log(l), low))

    _tiles(d, half_window, tile)


def _bwd_tiles(qn_ref, kp_ref, vp_ref, bias_ref, do_ref, lse_ref, delta_ref, dq_ref, dk_ref, dv_ref, ds_ref,
               *, d, half_window, sinks=None, dsink_ref=None):
    low = _low_half()
    w = Q_BLOCK + 2 * half_window
    sink = None if sinks is None else _per_head(sinks)

    def rows_of(t):
        return jnp.concatenate([t[:, 0:1], t[:, HEAD_DIM:HEAD_DIM + 1]], axis=0)

    def tile(q_rows, k_rows, edge):
        q2 = _stack_heads(qn_ref[q_rows, :], low)
        do2 = _stack_heads(do_ref[q_rows, :], low)
        k = kp_ref[k_rows, :].astype(BF16)
        v = vp_ref[k_rows, :].astype(BF16)
        lse = rows_of(lse_ref[q_rows, :])
        delta = rows_of(delta_ref[q_rows, :])
        p = jnp.exp(_dot(q2, k, NT) + bias_ref[...] + edge - lse)
        ds = p * (_dot(do2, v, NT) - delta)
        ds_ref[...] += ds
        if sink is not None:
            dsink_ref[...] += (-jnp.exp(sink - lse) * delta).reshape(2, Q_BLOCK, 1)
        dsb, pb = ds.astype(BF16), p.astype(BF16)
        dq_ref[q_rows, :] = _unstack_heads(_dot(dsb, k), low)
        dk_ref[k_rows, :] += _dot(dsb, q2, TN)
        dv_ref[k_rows, :] += _dot(pb, do2, TN)

    _tiles(d, half_window, tile)


def _norm_bwd(raw_ref, gain_ref, dn_ref, dn_offset, out_ref, scale):
    low = _low_half()

    def step(r0, dgain):
        t = raw_ref[pl.ds(r0, CHUNK), :].astype(F32)
        dn = dn_ref[pl.ds(dn_offset + r0, CHUNK), :]
        dth = dn * (gain_ref[...] * scale)
        sums = _half_sum(jnp.concatenate([t * t, dth * t], axis=0), low)
        r = lax.rsqrt(sums[:CHUNK] * (1.0 / HEAD_DIM) + EPS)
        th = t * r
        out_ref[pl.ds(r0, CHUNK), :] = (r * (dth - th * (r * sums[CHUNK:] * (1.0 / HEAD_DIM)))).astype(BF16)
        return dgain + jnp.sum(dn * th, axis=0, keepdims=True) * scale

    return _chunks(step, jnp.zeros((1, LANES), F32))


def _rows8(v):
    return jnp.broadcast_to(v, (8, v.shape[-1]))


A_W = Q_BLOCK + 2 * A_HALF_WINDOW
A_PAD = A_HALF_WINDOW


def _seq_block(col_fn):
    return pl.BlockSpec((SEQ, LANES), col_fn)


def _attn_a_fwd(qkv, gain_q, gain_k, bias, sink):
    def body(sink_ref, q_ref, k_ref, v_ref, gq_ref, gk_ref, line_ref, o_ref, lse_ref, qn_ref, kp_ref, vp_ref,
             bias_ref):
        hp = pl.program_id(0)
        keep = (lax.broadcasted_iota(jnp.int32, (1, LANES), 1) // HEAD_DIM) == hp // 2
        _prep_q(q_ref, gq_ref, qn_ref)
        _prep_kv(k_ref, v_ref, gk_ref, kp_ref, vp_ref, A_PAD, keep)
        _unroll_bias(line_ref, bias_ref, A_W)

        def emit(rows, out, lse):
            o_ref[rows, :] = out
            lse_ref[rows, :] = lse

        _fwd_tiles(qn_ref, kp_ref, vp_ref, bias_ref, emit, d=1, half_window=A_HALF_WINDOW,
                   sinks=(sink_ref[2 * hp], sink_ref[2 * hp + 1]))

    vec = pl.BlockSpec((1, LANES), lambda hp, s: (0, 0))
    return pl.pallas_call(
        body, name="attn_a_fwd",
        grid_spec=pltpu.PrefetchScalarGridSpec(
            num_scalar_prefetch=1, grid=(4,),
            in_specs=[_seq_block(lambda hp, s: (0, QA_BLK + hp)), _seq_block(lambda hp, s: (0, KA_BLK)),
                      _seq_block(lambda hp, s: (0, VA_BLK)), vec, vec,
                      pl.BlockSpec((None, 2, _line_width(A_HALF_WINDOW)), lambda hp, s: (hp, 0, 0))],
            out_specs=[_seq_block(lambda hp, s: (0, hp)), _seq_block(lambda hp, s: (0, hp))],
            scratch_shapes=[pltpu.VMEM((SEQ, LANES), F32), pltpu.VMEM((SEQ + 2 * A_PAD, LANES), F32),
                            pltpu.VMEM((SEQ + 2 * A_PAD, LANES), F32), pltpu.VMEM((2 * Q_BLOCK, A_W), F32)]),
        out_shape=[jax.ShapeDtypeStruct((SEQ, 512), F32)] * 2,
        compiler_params=_params("arbitrary"),
    )(sink.reshape(8), qkv, qkv, qkv, gain_q, gain_k, bias)


def _attn_a_bwd(qkv, gain_q, gain_k, bias, sink, delta, lse, d_out):
    def body(sink_ref, q_ref, k_ref, v_ref, gq_ref, gk_ref, line_ref, delta_ref, lse_ref, do_ref,
             dq_out, dkv_out, dgq_out, dgk_out, dline_out, dsink_out,
             qn_ref, kp_ref, vp_ref, dq_ref, dk_ref, dv_ref, dk_tot, dv_tot, bias_ref, ds_out):
        hp = pl.program_id(0)
        kv_head = hp // 2
        keep = (lax.broadcasted_iota(jnp.int32, (1, LANES), 1) // HEAD_DIM) == kv_head
        _prep_q(q_ref, gq_ref, qn_ref)
        _prep_kv(k_ref, v_ref, gk_ref, kp_ref, vp_ref, A_PAD, keep)
        _unroll_bias(line_ref, bias_ref, A_W)
        dk_ref[...] = jnp.zeros_like(dk_ref)
        dv_ref[...] = jnp.zeros_like(dv_ref)
        ds_out[...] = jnp.zeros_like(ds_out)
        dsink_out[...] = jnp.zeros_like(dsink_out)

        @pl.when(hp == 0)
        def _():
            dk_tot[...] = jnp.zeros_like(dk_tot)
            dv_tot[...] = jnp.zeros_like(dv_tot)

        _bwd_tiles(qn_ref, kp_ref, vp_ref, bias_ref, do_ref, lse_ref, delta_ref, dq_ref, dk_ref, dv_ref, ds_out,
                   d=1, half_window=A_HALF_WINDOW, sinks=(sink_ref[2 * hp], sink_ref[2 * hp + 1]),
                   dsink_ref=dsink_out)
        _fold_bias_grad(ds_out, dline_out, A_W)
        dgq_out[...] = _rows8(_norm_bwd(q_ref, gq_ref, dq_ref, 0, dq_out, SCALE))

        def fold(r0, carry):
            rows = pl.ds(A_PAD + r0, CHUNK)
            for acc, tot in ((dk_ref, dk_tot), (dv_ref, dv_tot)):
                t = acc[rows, :]
                tot[pl.ds(r0, CHUNK), :] += jnp.where(keep, t + pltpu.roll(t, HEAD_DIM, 1), 0.0)
            return carry

        _chunks(fold)

        @pl.when(hp == 3)
        def _():
            dgk_out[...] = _rows8(_norm_bwd(k_ref, gk_ref, dk_tot, 0, dkv_out.at[0], 1.0))
            dkv_out[1] = dv_tot[...].astype(BF16)

    vec = pl.BlockSpec((1, LANES), lambda hp, s: (0, 0))
    seq_f32 = pltpu.VMEM((SEQ, LANES), F32)
    padded = pltpu.VMEM((SEQ + 2 * A_PAD, LANES), F32)
    return pl.pallas_call(
        body, name="attn_a_bwd",
        grid_spec=pltpu.PrefetchScalarGridSpec(
            num_scalar_prefetch=1, grid=(4,),
            in_specs=[_seq_block(lambda hp, s: (0, QA_BLK + hp)), _seq_block(lambda hp, s: (0, KA_BLK)),
                      _seq_block(lambda hp, s: (0, VA_BLK)), vec, vec,
                      pl.BlockSpec((None, 2, _line_width(A_HALF_WINDOW)), lambda hp, s: (hp, 0, 0)),
                      _seq_block(lambda hp, s: (0, hp)), _seq_block(lambda hp, s: (0, hp)),
                      _seq_block(lambda hp, s: (0, hp))],
            out_specs=[pl.BlockSpec((None, SEQ, LANES), lambda hp, s: (hp, 0, 0)),
                       pl.BlockSpec((2, SEQ, LANES), lambda hp, s: (0, 0, 0)),
                       pl.BlockSpec((None, 8, LANES), lambda hp, s: (hp, 0, 0)),
                       pl.BlockSpec((8, LANES), lambda hp, s: (0, 0)),
                       pl.BlockSpec((None, 2, _line_width(A_HALF_WINDOW)), lambda hp, s: (hp, 0, 0)),
                       pl.BlockSpec((None, 2, Q_BLOCK, 1), lambda hp, s: (hp, 0, 0, 0))],
            scratch_shapes=[seq_f32, padded, padded, seq_f32, padded, padded, seq_f32, seq_f32,
                            pltpu.VMEM((2 * Q_BLOCK, A_W), F32), pltpu.VMEM((2 * Q_BLOCK, A_W), F32)]),
        out_shape=[jax.ShapeDtypeStruct((4, SEQ, LANES), BF16), jax.ShapeDtypeStruct((2, SEQ, LANES), BF16),
                   jax.ShapeDtypeStruct((4, 8, LANES), F32), jax.ShapeDtypeStruct((8, LANES), F32),
                   jax.ShapeDtypeStruct((4, 2, _line_width(A_HALF_WINDOW)), F32),
                   jax.ShapeDtypeStruct((4, 2, Q_BLOCK, 1), F32)],
        compiler_params=_params("arbitrary"),
    )(sink.reshape(8), qkv, qkv, qkv, gain_q, gain_k, bias, delta, lse, d_out)


B_W = Q_BLOCK + 2 * B_HALF_WINDOW
B_PAD_MAX = B_HALF_WINDOW * B_DILATIONS[-1]


def _attn_b_fwd(qkv, gain_q, gain_k, bias):
    def body(q_ref, k_ref, v_ref, gq_ref, gk_ref, line_ref, o_ref, lse_ref, qn_ref, kp_ref, vp_ref, bias_ref):
        g = pl.program_id(1)
        _prep_q(q_ref, gq_ref, qn_ref)
        _unroll_bias(line_ref, bias_ref, B_W)

        def first(rows, out, lse):
            o_ref[rows, :] = out
            lse_ref[rows, :] = lse

        def combine(rows, out, lse):
            old = lse_ref[rows, :]
            new = jnp.maximum(old, lse) + jnp.log(1.0 + jnp.exp(-jnp.abs(old - lse)))
            o_ref[rows, :] = o_ref[rows, :] * jnp.exp(old - new) + out * jnp.exp(lse - new)
            lse_ref[rows, :] = new

        for gi, d in enumerate(B_DILATIONS):
            @pl.when(g == gi)
            def _():
                _prep_kv(k_ref, v_ref, gk_ref, kp_ref, vp_ref, B_HALF_WINDOW * d)
                _fwd_tiles(qn_ref, kp_ref, vp_ref, bias_ref, first if gi == 0 else combine,
                           d=d, half_window=B_HALF_WINDOW)

    vec = pl.BlockSpec((1, LANES), lambda hp, g: (0, 0))
    padded = pltpu.VMEM((SEQ + 2 * B_PAD_MAX, LANES), F32)
    return pl.pallas_call(
        body, name="attn_b_fwd", grid=(4, 3),
        in_specs=[_seq_block(lambda hp, g: (0, QB_BLK + 4 * g + hp)), _seq_block(lambda hp, g: (0, KB_BLK + 4 * g + hp)),
                  _seq_block(lambda hp, g: (0, VB_BLK + 4 * g + hp)), vec, vec,
                  pl.BlockSpec((None, 2, _line_width(B_HALF_WINDOW)), lambda hp, g: (4 * g + hp, 0, 0))],
        out_specs=[_seq_block(lambda hp, g: (0, hp)), _seq_block(lambda hp, g: (0, hp))],
        out_shape=[jax.ShapeDtypeStruct((SEQ, 512), F32)] * 2,
        scratch_shapes=[pltpu.VMEM((SEQ, LANES), F32), padded, padded, pltpu.VMEM((2 * Q_BLOCK, B_W), F32)],
        compiler_params=_params("arbitrary", "arbitrary"),
    )(qkv, qkv, qkv, gain_q, gain_k, bias)


def _attn_b_bwd(qkv, gain_q, gain_k, bias, delta, lse, d_out):
    def body(q_ref, k_ref, v_ref, gq_ref, gk_ref, line_ref, delta_ref, lse_ref, do_ref,
             dq_out, dk_out, dv_out, dgq_out, dgk_out, dline_out,
             qn_ref, kp_ref, vp_ref, dq_ref, dk_ref, dv_ref, bias_ref, ds_out):
        g = pl.program_id(1)
        _prep_q(q_ref, gq_ref, qn_ref)
        _unroll_bias(line_ref, bias_ref, B_W)
        ds_out[...] = jnp.zeros_like(ds_out)
        for gi, d in enumerate(B_DILATIONS):
            @pl.when(g == gi)
            def _():
                pad = B_HALF_WINDOW * d
                for acc in (dk_ref, dv_ref):
                    acc[pl.ds(0, SEQ + 2 * pad), :] = jnp.zeros((SEQ + 2 * pad, LANES), F32)
                _prep_kv(k_ref, v_ref, gk_ref, kp_ref, vp_ref, pad)
                _bwd_tiles(qn_ref, kp_ref, vp_ref, bias_ref, do_ref, lse_ref, delta_ref, dq_ref, dk_ref, dv_ref,
                           ds_out, d=d, half_window=B_HALF_WINDOW)
                dgk_out[...] = _rows8(_norm_bwd(k_ref, gk_ref, dk_ref, pad, dk_out, 1.0))
                dv_out[...] = dv_ref[pl.ds(pad, SEQ), :].astype(BF16)
        _fold_bias_grad(ds_out, dline_out, B_W)
        dgq_out[...] = _rows8(_norm_bwd(q_ref, gq_ref, dq_ref, 0, dq_out, SCALE))

    vec = pl.BlockSpec((1, LANES), lambda hp, g: (0, 0))
    seq_f32 = pltpu.VMEM((SEQ, LANES), F32)
    padded = pltpu.VMEM((SEQ + 2 * B_PAD_MAX, LANES), F32)
    part = pl.BlockSpec((None, 8, LANES), lambda hp, g: (4 * g + hp, 0, 0))
    line = pl.BlockSpec((None, 2, _line_width(B_HALF_WINDOW)), lambda hp, g: (4 * g + hp, 0, 0))
    return pl.pallas_call(
        body, name="attn_b_bwd", grid=(4, 3),
        in_specs=[_seq_block(lambda hp, g: (0, QB_BLK + 4 * g + hp)), _seq_block(lambda hp, g: (0, KB_BLK + 4 * g + hp)),
                  _seq_block(lambda hp, g: (0, VB_BLK + 4 * g + hp)), vec, vec,
                  line,
                  _seq_block(lambda hp, g: (0, hp)), _seq_block(lambda hp, g: (0, hp)), _seq_block(lambda hp, g: (0, hp))],
        out_specs=[pl.BlockSpec((None, SEQ, LANES), lambda hp, g: (4 * g + hp, 0, 0))] * 3 + [part, part, line],
        out_shape=[jax.ShapeDtypeStruct((12, SEQ, LANES), BF16)] * 3
        + [jax.ShapeDtypeStruct((12, 8, LANES), F32)] * 2
        + [jax.ShapeDtypeStruct((12, 2, _line_width(B_HALF_WINDOW)), F32)],
        scratch_shapes=[seq_f32, padded, padded, seq_f32, padded, padded,
                        pltpu.VMEM((2 * Q_BLOCK, B_W), F32), pltpu.VMEM((2 * Q_BLOCK, B_W), F32)],
        compiler_params=_params("arbitrary", "arbitrary"),
    )(qkv, qkv, qkv, gain_q, gain_k, bias, delta, lse, d_out)


def _sigmoid(t):
    return 1.0 / (1.0 + jnp.exp(-t))


def _middle(out_a, out_b, gates, x, target, w_a, w_b, w_out, b_merge):
    tm = 256
    n_steps = SEQ // tm

    def body(oa_ref, ob_ref, g_ref, x_ref, t_ref, wa_ref, wb_ref, wo_ref, bm_ref,
             dy_ref, dg_ref, doa_ref, dob_ref, dla_ref, dlb_ref, dwa_ref, dwb_ref, dwo_ref, dbm_ref, sq_ref):
        @pl.when(pl.program_id(0) == 0)
        def _():
            for ref in (dwa_ref, dwb_ref, dwo_ref, dbm_ref, sq_ref):
                ref[...] = jnp.zeros_like(ref)

        gate_a, gate_b = g_ref[:, 0:512], g_ref[:, 512:1024]
        sig_a, sig_b = _sigmoid(gate_a), _sigmoid(gate_b)
        silu_a, silu_b = gate_a * sig_a, gate_b * sig_b
        oa, ob = oa_ref[...], ob_ref[...]
        ya, yb = (oa * silu_a).astype(BF16), (ob * silu_b).astype(BF16)
        br_a, br_b = _dot(ya, wa_ref[...]), _dot(yb, wb_ref[...])
        m0 = _sigmoid(g_ref[:, 1024:2048] + bm_ref[0:1, :])
        m1 = _sigmoid(g_ref[:, 2048:3072] + bm_ref[1:2, :])
        merged = (m0 * br_a + m1 * br_b).astype(BF16)
        err = (x_ref[...] + _dot(merged, wo_ref[...])) - t_ref[...]
        sq_ref[...] += jnp.sum(err * err, axis=0, keepdims=True)

        dy = err * (1.0 / D_MODEL)
        dy_ref[...] = dy
        dyb = dy.astype(BF16)
        dmerged = _dot(dyb, wo_ref[...], NT)
        dwo_ref[...] += _dot(merged, dyb, TN)
        dbr_a, dbr_b = (dmerged * m0).astype(BF16), (dmerged * m1).astype(BF16)
        dm0 = (dmerged * br_a) * (m0 * (1.0 - m0))
        dm1 = (dmerged * br_b) * (m1 * (1.0 - m1))
        dbm_ref[0:1, :] += jnp.sum(dm0, axis=0, keepdims=True)
        dbm_ref[1:2, :] += jnp.sum(dm1, axis=0, keepdims=True)
        for s in range(N_CHIPS):
            cols = slice(256 * s, 256 * (s + 1))
            dwa_ref[s] += _dot(ya, dbr_a[:, cols], TN)
            dwb_ref[s] += _dot(yb, dbr_b[:, cols], TN)
        dya, dyb_ = _dot(dbr_a, wa_ref[...], NT), _dot(dbr_b, wb_ref[...], NT)
        doa, dob = dya * silu_a, dyb_ * silu_b
        doa_ref[...] = doa
        dob_ref[...] = dob
        for blk in range(512 // LANES):
            lanes = slice(blk * LANES, (blk + 1) * LANES)
            dla_ref[:, lanes] = _half_sum(doa[:, lanes] * oa[:, lanes], None)
            dlb_ref[:, lanes] = _half_sum(dob[:, lanes] * ob[:, lanes], None)
        d_gates = (((dya * oa) * (sig_a * (1.0 + gate_a * (1.0 - sig_a)))).astype(BF16),
                   ((dyb_ * ob) * (sig_b * (1.0 + gate_b * (1.0 - sig_b)))).astype(BF16),
                   dm0.astype(BF16), dm1.astype(BF16))
        blk = 0
        for part in d_gates:
            for c0 in range(0, part.shape[1], 256):
                dg_ref[blk] = part[:, c0:c0 + 256]
                blk += 1

    def rows(width):
        return pl.BlockSpec((tm, width), lambda i: (i, 0))

    def whole(*shape):
        return pl.BlockSpec(shape, lambda i: (0,) * len(shape))

    return pl.pallas_call(
        body, name="middle", grid=(n_steps,),
        in_specs=[rows(512), rows(512), rows(GATE_WIDTH), rows(D_MODEL), rows(D_MODEL),
                  whole(512, D_MODEL), whole(512, D_MODEL), whole(D_MODEL, D_MODEL), whole(2, D_MODEL)],
        out_specs=[rows(D_MODEL), pl.BlockSpec((GATE_WIDTH // 256, tm, 256), lambda i: (0, i, 0)),
                   rows(512), rows(512), rows(512), rows(512),
                   whole(N_CHIPS, 512, 256), whole(N_CHIPS, 512, 256), whole(D_MODEL, D_MODEL),
                   whole(2, D_MODEL), whole(1, D_MODEL)],
        out_shape=[jax.ShapeDtypeStruct((SEQ, D_MODEL), F32), jax.ShapeDtypeStruct((GATE_WIDTH // 256, SEQ, 256), BF16),
                   jax.ShapeDtypeStruct((SEQ, 512), F32), jax.ShapeDtypeStruct((SEQ, 512), F32),
                   jax.ShapeDtypeStruct((SEQ, 512), F32), jax.ShapeDtypeStruct((SEQ, 512), F32),
                   jax.ShapeDtypeStruct((N_CHIPS, 512, 256), F32), jax.ShapeDtypeStruct((N_CHIPS, 512, 256), F32),
                   jax.ShapeDtypeStruct((D_MODEL, D_MODEL), F32), jax.ShapeDtypeStruct((2, D_MODEL), F32),
                   jax.ShapeDtypeStruct((1, D_MODEL), F32)],
        compiler_params=_params("arbitrary"),
    )(out_a, out_b, gates, x, target, w_a, w_b, w_out, b_merge)


def _which(j, edges, fns):
    lo = 0
    for hi, fn in zip(edges, fns):
        pl.when((j >= lo) & (j < hi))(fn)
        lo = hi


def _d_w_in(d_proj, h):
    plan, step, width = [], 0, 0
    for p in d_proj:
        total = p.shape[0] * p.shape[2]
        if width + total <= W_BLOCK:
            plan.append((p.shape[0], step, 1))
            width += total
            if width == W_BLOCK:
                step, width = step + 1, 0
        else:
            assert width == 0 and total % W_BLOCK == 0
            plan.append((W_BLOCK // p.shape[2], step, total // W_BLOCK))
            step += total // W_BLOCK
    assert width == 0 and step == IN_WIDTH // W_BLOCK
    firsts = sorted({first for _, first, _ in plan})
    edges = firsts[1:] + [step]
    halves = 2

    def body(*refs):
        pieces, h_ref, o_ref, acc_ref = refs[:-3], refs[-3], refs[-2], refs[-1]
        k = pl.program_id(1)

        def emit(group):
            def fn():
                cols = jnp.concatenate([ref[b] for ref in group for b in range(ref.shape[0])], axis=1)
                term = _dot(cols, h_ref[...], TN)

                @pl.when(k == 0)
                def _():
                    acc_ref[...] = term

                @pl.when(k == halves - 1)
                def _():
                    o_ref[...] = (acc_ref[...] + term).astype(BF16)
            return fn

        groups = [[ref for ref, (_, first, _) in zip(pieces, plan) if first == f] for f in firsts]
        _which(pl.program_id(0), edges, [emit(group) for group in groups])

    def cols_spec(piece, n, first, steps):
        def index(j, k):
            return jnp.clip(j - first, 0, steps - 1), jnp.where((j >= first) & (j < first + steps), k, 0), 0
        return pl.BlockSpec((n, SEQ // halves, piece.shape[2]), index)

    return pl.pallas_call(
        body, name="d_w_in", grid=(step, halves),
        in_specs=[cols_spec(p, *pl_) for p, pl_ in zip(d_proj, plan)]
        + [pl.BlockSpec((SEQ // halves, D_MODEL), lambda j, k: (k, 0))],
        out_specs=pl.BlockSpec((W_BLOCK, D_MODEL), lambda j, k: (j, 0)),
        out_shape=jax.ShapeDtypeStruct((IN_WIDTH, D_MODEL), BF16),
        scratch_shapes=[pltpu.VMEM((W_BLOCK, D_MODEL), F32)],
        compiler_params=_params("arbitrary", "arbitrary"),
    )(*d_proj, h)


RELAY_STEP = 10
RELAY_ROWS = 352


def _d_x(d_proj, w_t, x, gain, dy, chip_sums):
    tm = 256
    n_steps = SEQ // tm
    n_w = IN_WIDTH // W_BLOCK
    n_p, n_s = len(d_proj), len(chip_sums)

    def body(*refs):
        pieces, w_refs = refs[:n_p], refs[n_p:n_p + n_w]
        x_ref, g_ref, dy_ref = refs[n_p + n_w:n_p + n_w + 3]
        q_refs = refs[n_p + n_w + 3:n_p + n_w + 3 + n_s]
        dx_ref, dgain_ref = refs[n_p + n_w + 3 + n_s:n_p + n_w + 5 + n_s]
        outs = refs[n_p + n_w + 5 + n_s:n_p + n_w + 5 + 4 * n_s]
        got_refs, relay_refs, sum_refs = outs[:n_s], outs[n_s:2 * n_s], outs[2 * n_s:]
        if n_s:
            send_sems, recv_sems, local_sems, a_buf, b_buf, c_buf = refs[n_p + n_w + 5 + 4 * n_s:]

        def hops():
            cx, cy, c = lax.axis_index("x"), lax.axis_index("y"), lax.axis_index("c")
            near = (cx + (1 - c) - 2 * cx * (1 - c), cy + c - 2 * cy * c)
            far = (cx + c - 2 * cx * c, cy + (1 - c) - 2 * cy * (1 - c))
            chip = lambda p: 2 * p[0] + p[1]

            def copy(k, src, dst, to):
                return pltpu.make_async_remote_copy(src_ref=src, dst_ref=dst, send_sem=send_sems.at[k],
                                                    recv_sem=recv_sems.at[k], device_id=(*to, c), device_id_type=MESH)

            first = [(copy(3 * b, q.at[chip(near)], got.at[0], near),
                      copy(3 * b + 1, q.at[3 - chip((cx, cy))], relay, near))
                     for b, (q, got, relay) in enumerate(zip(q_refs, got_refs, relay_refs))]
            second = [copy(3 * b + 2, s, got.at[1], far) for b, (s, got) in enumerate(zip(sum_refs, got_refs))]
            return first, second, chip(far)

        @pl.when(pl.program_id(0) == 0)
        def _():
            dgain_ref[...] = jnp.zeros_like(dgain_ref)
            if n_s:
                for direct, pass_on in hops()[0]:
                    direct.start()
                    pass_on.start()

        if n_s:
            @pl.when(pl.program_id(0) == RELAY_STEP)
            def _():
                first, second, far_chip = hops()
                for b, (q, relay, total) in enumerate(zip(q_refs, relay_refs, sum_refs)):
                    first[b][1].wait_recv()
                    half = relay.shape[0]
                    for r0 in range(0, half, RELAY_ROWS):
                        rows = min(RELAY_ROWS, half - r0)
                        mine = pltpu.make_async_copy(q.at[far_chip, pl.ds(r0, rows), :], a_buf.at[pl.ds(0, rows), :],
                                                     local_sems.at[0])
                        theirs = pltpu.make_async_copy(relay.at[pl.ds(r0, rows), :], b_buf.at[pl.ds(0, rows), :],
                                                       local_sems.at[1])
                        mine.start()
                        theirs.start()
                        mine.wait()
                        theirs.wait()
                        c_buf[0:rows, :] = (a_buf[0:rows, :].astype(F32) + b_buf[0:rows, :].astype(F32)).astype(BF16)
                        store = pltpu.make_async_copy(c_buf.at[pl.ds(0, rows), :], total.at[pl.ds(r0, rows), :],
                                                      local_sems.at[2])
                        store.start()
                        store.wait()
                    second[b].start()

        blocks = [(piece, k) for piece in pieces for k in range(piece.shape[0])]
        dh, group, width, blk = None, [], 0, 0
        for piece, k in blocks:
            group.append(piece[k])
            width += piece.shape[2]
            if width == W_BLOCK:
                term = _dot(jnp.concatenate(group, axis=1), w_refs[blk][...])
                dh = term if dh is None else dh + term
                group, width, blk = [], 0, blk + 1
        assert not group and blk == n_w
        xf = x_ref[...]
        r = lax.rsqrt(jnp.mean(xf * xf, axis=-1, keepdims=True) + EPS)
        xh = xf * r
        dxh = dh * g_ref[...]
        dx_ref[...] = r * (dxh - xh * jnp.mean(dxh * xh, axis=-1, keepdims=True)) + dy_ref[...]
        dgain_ref[...] += _rows8(jnp.sum(dh * xh, axis=0, keepdims=True))

        if n_s:
            @pl.when(pl.program_id(0) == n_steps - 1)
            def _():
                first, second, _ = hops()
                for direct, pass_on in first:
                    direct.wait()
                    pass_on.wait_send()
                for cp in second:
                    cp.wait()

    row = pl.BlockSpec((tm, D_MODEL), lambda i: (i, 0))
    halves = [q.shape[1] for q in chip_sums]
    res = pl.pallas_call(
        body, name="d_x", grid=(n_steps,),
        in_specs=[pl.BlockSpec((p.shape[0], tm, p.shape[2]), lambda i: (0, i, 0)) for p in d_proj] + _w_blocks(0, n_w)
        + [row, pl.BlockSpec((1, D_MODEL), lambda i: (0, 0)), row] + [ANY] * n_s,
        out_specs=[row, pl.BlockSpec((8, D_MODEL), lambda i: (0, 0))] + [ANY] * (3 * n_s),
        out_shape=[jax.ShapeDtypeStruct((SEQ, D_MODEL), F32), jax.ShapeDtypeStruct((8, D_MODEL), F32)]
        + [jax.ShapeDtypeStruct((2, half, D_MODEL), BF16) for half in halves]
        + [jax.ShapeDtypeStruct((half, D_MODEL), BF16) for half in halves] * 2,
        scratch_shapes=[pltpu.SemaphoreType.DMA((3 * n_s,)), pltpu.SemaphoreType.DMA((3 * n_s,)),
                        pltpu.SemaphoreType.DMA((3,))] + [pltpu.VMEM((RELAY_ROWS, D_MODEL), BF16)] * 3 if n_s else [],
        compiler_params=_params("arbitrary"),
    )(*d_proj, *([w_t] * n_w), x, gain, dy, *chip_sums)
    return res[0], res[1], res[2:2 + n_s]


def _my_place():
    x, y, c = lax.axis_index("x"), lax.axis_index("y"), lax.axis_index("c")
    return jnp.stack([2 * x + y, c]).astype(jnp.int32)


def _half_rows(ref, half):
    rows = ref.shape[-2] // 2
    idx = (slice(None),) * (len(ref.shape) - 2) + (pl.ds(pl.multiple_of(half * rows, 16), rows), slice(None))
    return ref.at[idx]


def _swap_halves(grads):
    n = len(grads)

    def body(*refs):
        g_refs, o_refs, (send_sems, recv_sems) = refs[:n], refs[n:2 * n], refs[2 * n:]
        x, y, c = lax.axis_index("x"), lax.axis_index("y"), lax.axis_index("c")
        copies = [pltpu.make_async_remote_copy(src_ref=_half_rows(g, 1 - c), dst_ref=o, send_sem=send_sems.at[k],
                                               recv_sem=recv_sems.at[k], device_id=(x, y, 1 - c), device_id_type=MESH)
                  for k, (g, o) in enumerate(zip(g_refs, o_refs))]
        for cp in copies:
            cp.start()
        for cp in copies:
            cp.wait()

    return pl.pallas_call(
        body, name="reduce_swap_halves", in_specs=[ANY] * n, out_specs=[ANY] * n,
        out_shape=[jax.ShapeDtypeStruct((N_CHIPS, g.shape[1] // 2, D_MODEL), g.dtype) for g in grads],
        scratch_shapes=[pltpu.SemaphoreType.DMA((n,)), pltpu.SemaphoreType.DMA((n,))],
    )(*grads)


def _add_halves(place, grads, theirs, name):
    half = theirs.shape[1]
    tr = _row_tile(half)
    n = half // tr

    def body(place_ref, g_ref, t_ref, o_ref):
        o_ref[...] = (g_ref[...].astype(F32) + t_ref[...].astype(F32)).astype(BF16)

    return pl.pallas_call(
        body, name=name,
        grid_spec=pltpu.PrefetchScalarGridSpec(
            num_scalar_prefetch=1, grid=(N_CHIPS, n),
            in_specs=[pl.BlockSpec((None, tr, D_MODEL), lambda s, i, p: (s, p[1] * n + i, 0)),
                      pl.BlockSpec((None, tr, D_MODEL), lambda s, i, p: (s, i, 0))],
            out_specs=pl.BlockSpec((None, tr, D_MODEL), lambda s, i, p: (s, i, 0))),
        out_shape=jax.ShapeDtypeStruct((N_CHIPS, half, D_MODEL), BF16),
        compiler_params=_params("arbitrary", "arbitrary"),
    )(place, grads, theirs)


def _add_chips(place, chip_sums, others, name):
    half = others.shape[1]
    tr = _row_tile(half)
    n = half // tr

    def body(place_ref, q_ref, o_ref, r_ref):
        acc = q_ref[...].astype(F32)
        for j in range(others.shape[0]):
            acc = acc + o_ref[j].astype(F32)
        r_ref[...] = acc

    return pl.pallas_call(
        body, name=name,
        grid_spec=pltpu.PrefetchScalarGridSpec(
            num_scalar_prefetch=1, grid=(n,),
            in_specs=[pl.BlockSpec((None, tr, D_MODEL), lambda i, p: (p[0], i, 0)),
                      pl.BlockSpec((others.shape[0], tr, D_MODEL), lambda i, p: (0, i, 0))],
            out_specs=pl.BlockSpec((tr, D_MODEL), lambda i, p: (p[1] * n + i, 0))),
        out_shape=jax.ShapeDtypeStruct((2 * half, D_MODEL), F32),
        compiler_params=_params("arbitrary"),
    )(place, chip_sums, others)


def _join_halves(shards, block):
    n = len(shards)
    rows = block.shape[0]

    def body(*refs):
        b_ref, o_refs, sum_ref = refs[n], refs[n + 1:2 * n + 1], refs[2 * n + 1]
        send_sems, recv_sems, small_send, small_recv, local_sem, all_ref = refs[2 * n + 2:]
        x, y, c = lax.axis_index("x"), lax.axis_index("y"), lax.axis_index("c")
        me, sibling = (x, y, c), (x, y, 1 - c)
        chips = [(1 - x, y), (x, 1 - y), (1 - x, 1 - y)]

        def half(k, rows_ref):
            return pltpu.make_async_remote_copy(src_ref=rows_ref, dst_ref=rows_ref, send_sem=send_sems.at[k],
                                                recv_sem=recv_sems.at[k], device_id=sibling, device_id_type=MESH)

        def at(px, py, pc):
            return all_ref.at[pl.ds(pl.multiple_of((4 * px + 2 * py + pc) * rows, 8), rows), :]

        def small(k, block_of, to, src=None):
            return pltpu.make_async_remote_copy(src_ref=at(*block_of) if src is None else src, dst_ref=at(*block_of),
                                                send_sem=small_send.at[k], recv_sem=small_recv.at[k],
                                                device_id=to, device_id_type=MESH)

        sends = [half(k, _half_rows(o, c)) for k, o in enumerate(o_refs)]
        for cp in sends:
            cp.start()
        mine = pltpu.make_async_copy(b_ref, at(*me), local_sem)
        mine.start()
        first = [small(0, me, sibling, src=b_ref)]
        first += [small(1 + j, me, (*chip, c), src=b_ref) for j, chip in enumerate(chips)]
        for cp in first:
            cp.start()
        passed = [small(4 + j, (*chip, c), sibling) for j, chip in enumerate(chips)]
        for j, chip in enumerate(chips):
            small(1 + j, (*chip, c), me).wait_recv()
            passed[j].start()
        small(0, sibling, me).wait_recv()
        for j, chip in enumerate(chips):
            small(4 + j, (*chip, 1 - c), me).wait_recv()
        mine.wait()
        acc = all_ref[0:rows, :]
        for dev in range(1, 8):
            acc = acc + all_ref[rows * dev:rows * (dev + 1), :]
        sum_ref[...] = acc
        for k, o in enumerate(o_refs):
            half(k, _half_rows(o, 1 - c)).wait_recv()
        for cp in sends + first + passed:
            cp.wait_send()

    res = pl.pallas_call(
        body, name="reduce_join_halves", in_specs=[ANY] * n + [pl.BlockSpec(memory_space=pltpu.VMEM)],
        out_specs=[ANY] * n + [pl.BlockSpec(memory_space=pltpu.VMEM)],
        out_shape=[jax.ShapeDtypeStruct(s.shape, F32) for s in shards] + [jax.ShapeDtypeStruct(block.shape, F32)],
        input_output_aliases={k: k for k in range(n)},
        scratch_shapes=[pltpu.SemaphoreType.DMA((n,)), pltpu.SemaphoreType.DMA((n,)),
                        pltpu.SemaphoreType.DMA((7,)), pltpu.SemaphoreType.DMA((7,)), pltpu.SemaphoreType.DMA,
                        pltpu.VMEM((8 * rows, D_MODEL), F32)],
    )(*shards, block)
    return res[:n], res[n]


def _adamw_math(w, g, m, v):
    m = ADAM_B1 * m + (1.0 - ADAM_B1) * g
    v = ADAM_B2 * v + (1.0 - ADAM_B2) * (g * g)
    m_hat = m / (1.0 - ADAM_B1 ** ADAM_STEP)
    v_hat = v / (1.0 - ADAM_B2 ** ADAM_STEP)
    return -ADAM_LR * (m_hat / (jnp.sqrt(v_hat) + ADAM_EPS) + ADAM_WD * w), m, v


def _adamw(w, g, m, v, name):
    r, c = w.shape
    tr = _row_tile(r)

    def body(w_ref, g_ref, m_ref, v_ref, d_ref, nm_ref, nv_ref):
        d_ref[...], nm_ref[...], nv_ref[...] = _adamw_math(w_ref[...], g_ref[...], m_ref[...], v_ref[...])

    spec = pl.BlockSpec((tr, c), lambda i: (i, 0))
    return pl.pallas_call(
        body, name=name, grid=(r // tr,), in_specs=[spec] * 4, out_specs=[spec] * 3,
        out_shape=[jax.ShapeDtypeStruct((r, c), F32)] * 3, compiler_params=_params("arbitrary"),
    )(w, g, m, v)


def _adamw_small(ws, gs, ms, vs):
    n = len(ws)

    def body(*refs):
        ins, outs = refs[:4 * n], refs[4 * n:]
        for k in range(n):
            d, m, v = _adamw_math(ins[k][...], ins[n + k][...], ins[2 * n + k][...], ins[3 * n + k][...])
            outs[k][...], outs[n + k][...], outs[2 * n + k][...] = d, m, v

    shapes = [jax.ShapeDtypeStruct(w.shape, F32) for w in ws]
    res = pl.pallas_call(body, name="adamw_small", out_shape=shapes * 3)(*ws, *gs, *ms, *vs)
    return res[:n], res[n:2 * n], res[2 * n:]


def _fold_heads(partials):
    t = jnp.sum(partials[:, 0, :], axis=0)
    return (t[:HEAD_DIM] + t[HEAD_DIM:]).reshape(1, HEAD_DIM)


def _local_step(x, target, norm_gain, w_t, w_a, w_b, w_o, b_m, q_norm_a, k_norm_a, q_norm_b, k_norm_b, sink_a,
                rel_bias, start_reduce=None, small_shard=None):
    two = lambda gain: jnp.concatenate([gain, gain], axis=1)
    bias_a = _bias_lines(rel_bias[:, :8], A_HALF_WINDOW, 1)
    bias_b = jnp.concatenate([_bias_lines(rel_bias[:, 8 + 8 * g:16 + 8 * g], B_HALF_WINDOW, d)
                              for g, d in enumerate(B_DILATIONS)], axis=0)

    qkv, h, *small_all = _in_proj(x, norm_gain, w_t, 0, QKV_WIDTH // W_BLOCK, BF16, "in_proj_qkv", True, small_shard)
    if small_shard is not None:
        w_a, w_b, w_o, b_m = _unpack_weights(small_all[0])
    gates, = _in_proj(x, norm_gain, w_t, QKV_WIDTH // W_BLOCK, GATE_WIDTH // W_BLOCK, F32, "in_proj_gates", False)
    out_a, lse_a = _attn_a_fwd(qkv, two(q_norm_a), two(k_norm_a), bias_a, sink_a)
    out_b, lse_b = _attn_b_fwd(qkv, two(q_norm_b), two(k_norm_b), bias_b)

    dy, dgates, d_out_a, d_out_b, delta_a, delta_b, d_wa, d_wb, d_wo, d_bm, sq = _middle(
        out_a, out_b, gates, x, target, w_a, w_b, w_o, b_m)
    loss = (0.5 / D_MODEL) * jnp.sum(sq)

    dq_a, dkv_a, dgq_a, dgk_a, ds_a, dsink = _attn_a_bwd(
        qkv, two(q_norm_a), two(k_norm_a), bias_a, sink_a, delta_a, lse_a, d_out_a)
    dq_b, dk_b, dv_b, dgq_b, dgk_b, ds_b = _attn_b_bwd(
        qkv, two(q_norm_b), two(k_norm_b), bias_b, delta_b, lse_b, d_out_b)
    d_proj = (dq_a, dkv_a, dq_b, dk_b, dv_b, dgates)

    d_bm_rows = jnp.pad(d_bm.reshape(2, N_CHIPS, 256).transpose(1, 0, 2),
                        ((0, 0), (0, REST_ROWS - 514), (0, D_MODEL - 256)))
    rest = jnp.concatenate([d_wo.reshape(N_CHIPS, 256, D_MODEL), d_wa.reshape(N_CHIPS, 128, D_MODEL),
                            d_wb.reshape(N_CHIPS, 128, D_MODEL), d_bm_rows], axis=1)
    grads = [_d_w_in(d_proj, h).reshape(N_CHIPS, W_IN_SHARD, D_MODEL), rest]
    narrow = [grads[0], rest.astype(BF16)]
    chip_sums = start_reduce(grads, narrow) if start_reduce is not None else []
    grad_x, d_gain, others = _d_x(d_proj, w_t, x, norm_gain, dy, chip_sums)

    d_rel = jnp.concatenate(
        [_bias_grad(ds_a, A_HALF_WINDOW, 1)]
        + [_bias_grad(ds_b[4 * g:4 * g + 4], B_HALF_WINDOW, d) for g, d in enumerate(B_DILATIONS)], axis=1)
    d_sink = jnp.sum(dsink, axis=(2, 3)).reshape(1, 8)
    dgk_a_row = dgk_a[0]
    small = jnp.zeros((8, D_MODEL), F32)
    small = small.at[0].set(d_gain[0])
    small = small.at[1].set(d_rel.reshape(-1))
    misc = jnp.concatenate([_fold_heads(dgq_a), (dgk_a_row[:HEAD_DIM] + dgk_a_row[HEAD_DIM:]).reshape(1, HEAD_DIM),
                            _fold_heads(dgq_b), _fold_heads(dgk_b), d_sink], axis=1)
    small = small.at[2, :264].set(misc[0])

    return loss, grad_x, grads, small, chip_sums, others


def _unpack_weights(small_all):
    sm = small_all.reshape(N_CHIPS, SMALL_ROWS, D_MODEL)
    w_o = sm[:, 0:256].reshape(D_MODEL, D_MODEL)
    w_a = sm[:, 256:384].reshape(N_CHIPS, 512, 256).transpose(1, 0, 2).reshape(512, D_MODEL)
    w_b = sm[:, 384:512].reshape(N_CHIPS, 512, 256).transpose(1, 0, 2).reshape(512, D_MODEL)
    b_m = lax.bitcast_convert_type(sm[:, 512].reshape(N_CHIPS, 2, 256, 2), F32)
    return w_a, w_b, w_o, b_m.transpose(1, 0, 2).reshape(2, D_MODEL)


def _pack_small_weights(w_branch_a, w_branch_b, b_merge, w_out):
    b_m = jnp.pad(lax.bitcast_convert_type(b_merge, BF16).reshape(1, D_MODEL), ((0, SMALL_ROWS - 513), (0, 0)))
    return jnp.concatenate([w_out.astype(BF16), w_branch_a.astype(BF16).reshape(128, D_MODEL),
                            w_branch_b.astype(BF16).reshape(128, D_MODEL), b_m], axis=0)


def kernel(x, norm_gain, w_in, q_norm_a, k_norm_a, q_norm_b, k_norm_b, sink_a, rel_bias, w_branch_a, w_branch_b, b_merge, w_out, loss_target, m_norm_gain, m_w_in, m_q_norm_a, m_k_norm_a, m_q_norm_b, m_k_norm_b, m_sink_a, m_rel_bias, m_w_branch_a, m_w_branch_b, m_b_merge, m_w_out, v_norm_gain, v_w_in, v_q_norm_a, v_k_norm_a, v_q_norm_b, v_k_norm_b, v_sink_a, v_rel_bias, v_w_branch_a, v_w_branch_b, v_b_merge, v_w_out):
    w_in_t, m_w_in_t, v_w_in_t = (jnp.transpose(t[0]) for t in (w_in, m_w_in, v_w_in))
    wt_shard = _cast_rows(w_in_t, BF16, "w_in_cast")
    w_t = _gather_weights(wt_shard)
    small_shard = _pack_small_weights(w_branch_a[0], w_branch_b[0], b_merge[0], w_out[0])

    place = _my_place()
    names = ("w_in", "rest")

    def start_reduce(grads, narrow):
        return [_add_halves(place, g, t, "reduce_add_halves_" + n) for g, t, n in zip(grads, _swap_halves(narrow), names)]

    loss_part, grad_x, _, small, chip_sums, others = _local_step(
        x[0], loss_target[0], norm_gain, w_t, None, None, None, None, q_norm_a, k_norm_a, q_norm_b, k_norm_b,
        sink_a, rel_bias, start_reduce, small_shard)

    (g_wt, g_rest), small = _join_halves(
        [_add_chips(place, q, o, "reduce_add_chips_" + n) for q, o, n in zip(chip_sums, others, names)],
        small.at[3, 0].set(loss_part))
    loss = small[3, 0]

    g_w_out = g_rest[0:256]
    g_w_a = g_rest[256:384].reshape(512, 256)
    g_w_b = g_rest[384:512].reshape(512, 256)
    g_b_merge = g_rest[512:514, :256]
    g_norm_gain = small[0:1]
    g_rel_bias = small[1].reshape(N_BUCKETS, N_BUCKETS)
    g_q_a, g_k_a, g_q_b, g_k_b = (small[2:3, 64 * k:64 * k + 64] for k in range(4))
    g_sink = small[2:3, 256:264]

    big_names = (("w_branch_a", w_branch_a, g_w_a, m_w_branch_a, v_w_branch_a),
                 ("w_branch_b", w_branch_b, g_w_b, m_w_branch_b, v_w_branch_b),
                 ("w_out", w_out, g_w_out, m_w_out, v_w_out))
    upd = {name: (g,) + tuple(_adamw(w[0], g, m[0], v[0], "adamw_" + name)) for name, w, g, m, v in big_names}
    upd["w_in"] = tuple(jnp.transpose(t) for t in (g_wt,) + tuple(_adamw(w_in_t, g_wt, m_w_in_t, v_w_in_t, "adamw_w_in")))
    small_names = ("norm_gain", "q_norm_a", "k_norm_a", "q_norm_b", "k_norm_b", "sink_a", "rel_bias", "b_merge")
    ws = [norm_gain, q_norm_a, k_norm_a, q_norm_b, k_norm_b, sink_a, rel_bias, b_merge[0]]
    gs = [g_norm_gain, g_q_a, g_k_a, g_q_b, g_k_b, g_sink, g_rel_bias, g_b_merge]
    ms = [m_norm_gain, m_q_norm_a, m_k_norm_a, m_q_norm_b, m_k_norm_b, m_sink_a, m_rel_bias, m_b_merge[0]]
    vs = [v_norm_gain, v_q_norm_a, v_k_norm_a, v_q_norm_b, v_k_norm_b, v_sink_a, v_rel_bias, v_b_merge[0]]
    ds, nms, nvs = _adamw_small(ws, gs, ms, vs)
    for k, name in enumerate(small_names):
        upd[name] = (gs[k], ds[k], nms[k], nvs[k])

    order = ("norm_gain", "w_in", "q_norm_a", "k_norm_a", "q_norm_b", "k_norm_b", "sink_a", "rel_bias",
             "w_branch_a", "w_branch_b", "b_merge", "w_out")
    lead = {"w_in", "w_branch_a", "w_branch_b", "b_merge", "w_out"}
    outs = [loss, grad_x[None]]
    for part in range(4):
        outs += [upd[name][part][None] if name in lead else upd[name][part] for name in order]
    return tuple(outs)
```

```python
import math

import numpy as np
import jax
import jax.numpy as jnp
from jax import lax
from jax.experimental import pallas as pl
from jax.experimental.pallas import tpu as pltpu

F32 = jnp.float32
BF16 = jnp.bfloat16

SEQ = 4096
D_MODEL = 1024
HEAD_DIM = 64
LANES = 128
EPS = 1e-6
NEG_INF = -1e30
SCALE = HEAD_DIM ** -0.5
N_BUCKETS = 32
MAX_DISTANCE = 1024
N_CHIPS = 4

A_HALF_WINDOW = 128
B_HALF_WINDOW = 64
B_DILATIONS = (1, 4, 16)
Q_BLOCK = 128

QKV_WIDTH = 5376
GATE_WIDTH = 3072
QA_BLK, KA_BLK, VA_BLK = 0, 4, 5
QB_BLK, KB_BLK, VB_BLK = 6, 18, 30
IN_WIDTH = QKV_WIDTH + GATE_WIDTH
W_IN_SHARD = IN_WIDTH // N_CHIPS

SMALL_ROWS = 544
REST_ROWS = 544

ADAM_LR = 0.001
ADAM_B1 = 0.9
ADAM_B2 = 0.999
ADAM_EPS = 1e-08
ADAM_WD = 0.01
ADAM_STEP = 10

VMEM_LIMIT = 56 * 1024 * 1024

NT = (((1,), (1,)), ((), ()))
TN = (((0,), (0,)), ((), ()))
MESH = pl.DeviceIdType.MESH
ANY = pl.BlockSpec(memory_space=pl.ANY)


def _dot(a, b, dims=None):
    if dims is None:
        return jnp.dot(a, b, preferred_element_type=F32)
    return lax.dot_general(a, b, dims, preferred_element_type=F32)


def _params(*semantics):
    return pltpu.CompilerParams(dimension_semantics=semantics or None, vmem_limit_bytes=VMEM_LIMIT)


def _line_width(half_window):
    return pl.cdiv(2 * Q_BLOCK + 2 * half_window - 1, LANES) * LANES


def _bucket_onehot(half_window, stride):
    rel = np.arange(_line_width(half_window)) - (Q_BLOCK - 1) - half_window
    band = np.abs(rel) <= half_window
    rel = rel * stride
    half, max_exact = N_BUCKETS // 2, N_BUCKETS // 4
    n = np.abs(rel)
    nf = np.maximum(n, max_exact).astype(np.float32)
    large = max_exact + (np.log(nf / np.float32(max_exact)) / np.float32(math.log(MAX_DISTANCE / max_exact))
                         * np.float32(half - max_exact)).astype(np.int32)
    large = np.minimum(large, half - 1)
    bucket = (rel > 0).astype(np.int32) * half + np.where(n < max_exact, n, large)
    onehot = (bucket[..., None] == np.arange(N_BUCKETS)) & band[..., None]
    return onehot.astype(np.float32), band


def _bias_lines(rel_bias_cols, half_window, stride):
    onehot, band = _bucket_onehot(half_window, stride)
    h = rel_bias_cols.shape[1]
    t = jnp.einsum("tb,bh->ht", jnp.asarray(onehot), rel_bias_cols, precision=lax.Precision.HIGHEST)
    t = t + jnp.asarray(np.where(band, 0.0, NEG_INF).astype(np.float32))
    return t.reshape(h // 2, 2, -1)


def _bias_grad(d_lines, half_window, stride):
    onehot, _ = _bucket_onehot(half_window, stride)
    h = d_lines.shape[0] * 2
    return jnp.einsum("tb,ht->bh", jnp.asarray(onehot), d_lines.reshape(h, -1), precision=lax.Precision.HIGHEST)


def _unroll_bias(line_ref, tile_ref, w):
    width = line_ref.shape[1]
    for j in range(2):
        rows = jnp.broadcast_to(line_ref[j:j + 1, :], (Q_BLOCK, width))
        rows = pltpu.roll(rows, width - (Q_BLOCK - 1), 1, stride=1, stride_axis=0)
        tile_ref[j * Q_BLOCK:(j + 1) * Q_BLOCK, :] = rows[:, :w]


def _fold_bias_grad(tile_ref, line_ref, w):
    width = line_ref.shape[1]
    row = lax.broadcasted_iota(jnp.int32, (Q_BLOCK, Q_BLOCK), 0)
    col = lax.broadcasted_iota(jnp.int32, (Q_BLOCK, Q_BLOCK), 1)
    flip = jnp.where(row + col == Q_BLOCK - 1, 1.0, 0.0).astype(BF16)
    for j in range(2):
        tile = tile_ref[j * Q_BLOCK:(j + 1) * Q_BLOCK, :]
        hi = tile.astype(BF16)
        lo = (tile - hi.astype(F32)).astype(BF16)
        rows = _dot(flip, hi) + _dot(flip, lo)
        rows = jnp.concatenate([rows, jnp.zeros((Q_BLOCK, width - w), F32)], axis=1)
        rows = pltpu.roll(rows, 0, 1, stride=1, stride_axis=0)
        line_ref[j:j + 1, :] = jnp.sum(rows, axis=0, keepdims=True)


def _row_tile(rows):
    return max(t for t in range(16, 385, 16) if rows % t == 0)


def _cast_rows(w, out_dtype, name):
    r, c = w.shape
    tr = _row_tile(r)

    def body(w_ref, o_ref):
        o_ref[...] = w_ref[...].astype(out_dtype)

    spec = pl.BlockSpec((tr, c), lambda i: (i, 0))
    return pl.pallas_call(
        body, name=name, grid=(r // tr,), in_specs=[spec], out_specs=spec,
        out_shape=jax.ShapeDtypeStruct((r, c), out_dtype), compiler_params=_params("arbitrary"),
    )(w)


STAGE_ROWS = 528


def _gather_scratch():
    return [pltpu.SemaphoreType.DMA((6,)), pltpu.SemaphoreType.DMA((6,)), pltpu.SemaphoreType.DMA((2,)),
            pltpu.SemaphoreType.DMA((2,)), pltpu.VMEM((2, STAGE_ROWS, D_MODEL), BF16)]


def _gather_phases(src_ref, out_ref, send_sems, recv_sems, in_sems, out_sems, stage):
    rows = src_ref.shape[0]
    x, y, c = lax.axis_index("x"), lax.axis_index("y"), lax.axis_index("c")
    sibling = (x, y, 1 - c)
    near = (x + (1 - c) - 2 * x * (1 - c), y + c - 2 * y * c)
    far = (x + c - 2 * x * c, y + (1 - c) - 2 * y * (1 - c))
    diag = (1 - x, 1 - y)
    chip_no = lambda chip: 2 * chip[0] + chip[1]
    my_chip = chip_no((x, y))

    def half_of(chip, half):
        start = pl.multiple_of(chip * rows + half * (rows // 2), 16)
        return out_ref.at[pl.ds(start, rows // 2), :]

    def copy(k, src, dst, to):
        return pltpu.make_async_remote_copy(src_ref=src, dst_ref=dst, send_sem=send_sems.at[k],
                                            recv_sem=recv_sems.at[k], device_id=to, device_id_type=MESH)

    mine = src_ref.at[pl.ds(pl.multiple_of(c * (rows // 2), 16), rows // 2), :]

    def keep_own():
        outs = []
        for i, r0 in enumerate(range(0, rows, STAGE_ROWS)):
            n = min(STAGE_ROWS, rows - r0)
            slot = i % 2
            if i >= 2:
                outs[i - 2].wait()
            buf = stage.at[slot, pl.ds(0, n), :]
            load = pltpu.make_async_copy(src_ref.at[pl.ds(r0, n), :], buf, in_sems.at[slot])
            load.start()
            load.wait()
            start = pl.multiple_of(my_chip * rows + r0, 16)
            outs.append(pltpu.make_async_copy(buf, out_ref.at[pl.ds(start, n), :], out_sems.at[slot]))
            outs[i].start()
        for cp in outs[-2:]:
            cp.wait()

    def start():
        copy(0, mine, half_of(my_chip, c), (*near, c)).start()
        copy(1, mine, half_of(my_chip, c), (*far, c)).start()
        keep_own()

    def pass_on(j, chip):
        landed = half_of(chip_no(chip), c)
        copy(3 + j, landed, landed, sibling).start()

    def relay():
        landed = half_of(chip_no(near), c)
        copy(0, landed, landed, sibling).wait_recv()
        copy(2, landed, landed, (*far, c)).start()
        pass_on(0, near)

    def forward():
        for j, chip in ((1, far), (2, diag)):
            landed = half_of(chip_no(chip), c)
            copy(j, landed, landed, sibling).wait_recv()
            pass_on(j, chip)

    def finish():
        for j, chip in ((0, far), (1, near), (2, diag)):
            other = half_of(chip_no(chip), 1 - c)
            copy(3 + j, other, other, sibling).wait_recv()
        for k in range(6):
            copy(k, mine, mine, sibling).wait_send()

    return start, relay, forward, finish


def _gather_weights(shard):
    def body(src_ref, out_ref, *scratch):
        for phase in _gather_phases(src_ref, out_ref, *scratch):
            phase()

    return pl.pallas_call(
        body, name="gather_weights", in_specs=[ANY], out_specs=ANY,
        out_shape=jax.ShapeDtypeStruct((N_CHIPS * shard.shape[0], D_MODEL), BF16),
        scratch_shapes=_gather_scratch(),
    )(shard)


W_BLOCK = 768


def _w_blocks(first, count):
    return [pl.BlockSpec((W_BLOCK, D_MODEL), lambda *_, k=k: (first + k, 0)) for k in range(count)]


def _in_proj(x, gain, w_t, first_block, n_blocks, out_dtype, name, keep_h, ride=None):
    tm = 512
    n_steps = SEQ // tm
    n_out = 2 if keep_h else 1

    def body(x_ref, g_ref, *refs):
        w_refs, outs = refs[:n_blocks], refs[n_blocks + (ride is not None):n_blocks + (ride is not None) + n_out]
        if ride is not None:
            phases = _gather_phases(refs[n_blocks], *refs[n_blocks + 1 + n_out:])
            for step, phase in zip((0, 2, 4, n_steps - 1), phases):
                pl.when(pl.program_id(0) == step)(phase)
        xf = x_ref[...]
        r = lax.rsqrt(jnp.mean(xf * xf, axis=-1, keepdims=True) + EPS)
        h = ((xf * r) * g_ref[...]).astype(BF16)
        if keep_h:
            outs[1][...] = h
        for k, w_ref in enumerate(w_refs):
            outs[0][:, k * W_BLOCK:(k + 1) * W_BLOCK] = _dot(h, w_ref[...], NT).astype(out_dtype)

    riding = [] if ride is None else [ride]
    return pl.pallas_call(
        body, name=name, grid=(n_steps,),
        in_specs=[pl.BlockSpec((tm, D_MODEL), lambda i: (i, 0)), pl.BlockSpec((1, D_MODEL), lambda i: (0, 0))]
        + _w_blocks(first_block, n_blocks) + [ANY for _ in riding],
        out_specs=[pl.BlockSpec((tm, W_BLOCK * n_blocks), lambda i: (i, 0)),
                   pl.BlockSpec((tm, D_MODEL), lambda i: (i, 0))][:n_out] + [ANY for _ in riding],
        out_shape=[jax.ShapeDtypeStruct((SEQ, W_BLOCK * n_blocks), out_dtype),
                   jax.ShapeDtypeStruct((SEQ, D_MODEL), BF16)][:n_out]
        + [jax.ShapeDtypeStruct((N_CHIPS * r.shape[0], D_MODEL), BF16) for r in riding],
        scratch_shapes=_gather_scratch() if riding else [],
        compiler_params=_params("arbitrary"),
    )(x, gain, *([w_t] * n_blocks), *riding)


CHUNK = 256
CHUNK_UNROLL = 4
TILE_UNROLL = 8


def _low_half():
    return lax.broadcasted_iota(jnp.int32, (1, LANES), 1) < HEAD_DIM


def _half_sum(v, low):
    del low
    row = lax.broadcasted_iota(jnp.int32, (2 * LANES, LANES), 0)
    col = lax.broadcasted_iota(jnp.int32, (2 * LANES, LANES), 1)
    ones = jnp.where((row % LANES) // HEAD_DIM == col // HEAD_DIM, 1.0, 0.0).astype(BF16)
    hi = v.astype(BF16)
    lo = (v - hi.astype(F32)).astype(BF16)
    return _dot(jnp.concatenate([hi, lo], axis=1), ones)


def _chunks(fn, init=0):
    def body(i, carry):
        for u in range(CHUNK_UNROLL):
            carry = fn(pl.multiple_of((i * CHUNK_UNROLL + u) * CHUNK, CHUNK), carry)
        return carry

    return lax.fori_loop(0, SEQ // (CHUNK * CHUNK_UNROLL), body, init)


def _inv_rms(t, low):
    del low
    row = lax.broadcasted_iota(jnp.int32, (LANES, LANES), 0)
    col = lax.broadcasted_iota(jnp.int32, (LANES, LANES), 1)
    ones = jnp.where(row // HEAD_DIM == col // HEAD_DIM, 1.0, 0.0).astype(BF16)
    return lax.rsqrt(_dot((t * t).astype(BF16), ones) * (1.0 / HEAD_DIM) + EPS)


def _prep_q(q_ref, gain_ref, qn_ref):
    low = _low_half()

    def step(r0, carry):
        q = q_ref[pl.ds(r0, CHUNK), :].astype(F32)
        qn_ref[pl.ds(r0, CHUNK), :] = ((q * _inv_rms(q, low)) * gain_ref[...]) * SCALE
        return carry

    _chunks(step)


def _own_half(t, keep):
    return jnp.where(keep, t, pltpu.roll(t, HEAD_DIM, 1))


def _prep_kv(k_ref, v_ref, gain_ref, kp_ref, vp_ref, pad, keep=None):
    low = _low_half()
    zeros = jnp.zeros((pad, LANES), F32)
    for ref in (kp_ref, vp_ref):
        ref[pl.ds(0, pad), :] = zeros
        ref[pl.ds(pad + SEQ, pad), :] = zeros

    def step(r0, carry):
        k = k_ref[pl.ds(r0, CHUNK), :].astype(F32)
        v = v_ref[pl.ds(r0, CHUNK), :].astype(F32)
        kn = (k * _inv_rms(k, low)) * gain_ref[...]
        if keep is not None:
            kn, v = _own_half(kn, keep), _own_half(v, keep)
        kp_ref[pl.ds(pad + r0, CHUNK), :] = kn
        vp_ref[pl.ds(pad + r0, CHUNK), :] = v
        return carry

    _chunks(step)


def _tiles(d, half_window, fn):
    w = Q_BLOCK + 2 * half_window
    length = SEQ // d
    n_blocks = length // Q_BLOCK
    col = lax.broadcasted_iota(jnp.int32, (1, w), 1)

    def step(it, carry):
        c, n = it // n_blocks, it % n_blocks
        start = c + (d * Q_BLOCK) * n
        if d == 1:
            start = pl.multiple_of(start, Q_BLOCK)
            q_rows, k_rows = pl.ds(start, Q_BLOCK), pl.ds(start, w)
        else:
            q_rows, k_rows = pl.ds(start, Q_BLOCK, stride=d), pl.ds(start, w, stride=d)
        t = n * Q_BLOCK - half_window + col
        edge = jnp.where((t < 0) | (t >= length), NEG_INF, 0.0)
        fn(q_rows, k_rows, edge)
        return carry

    lax.fori_loop(0, d * n_blocks, step, 0, unroll=TILE_UNROLL)


def _stack_heads(t, low):
    return jnp.concatenate([jnp.where(low, t, 0.0), jnp.where(low, 0.0, t)], axis=0).astype(BF16)


def _unstack_heads(t, low):
    return jnp.where(low, t[:Q_BLOCK], t[Q_BLOCK:])


def _per_head(pair):
    return jnp.concatenate([jnp.full((Q_BLOCK, 1), pair[0], F32), jnp.full((Q_BLOCK, 1), pair[1], F32)], axis=0)


def _fwd_tiles(qn_ref, kp_ref, vp_ref, bias_ref, emit, *, d, half_window, sinks=None):
    low = _low_half()
    w = Q_BLOCK + 2 * half_window
    sink = None if sinks is None else _per_head(sinks)

    def tile(q_rows, k_rows, edge):
        q2 = _stack_heads(qn_ref[q_rows, :], low)
        k = kp_ref[k_rows, :].astype(BF16)
        v1 = jnp.concatenate([vp_ref[k_rows, :], jnp.ones((w, LANES), F32)], axis=1).astype(BF16)
        s = _dot(q2, k, NT) + bias_ref[...] + edge
        m = jnp.max(s, axis=-1, keepdims=True)
        if sink is not None:
            m = jnp.maximum(m, sink)
        o = _dot(jnp.exp(s - m).astype(BF16), v1)
        l = o[:, LANES:]
        if sink is not None:
            l = l + jnp.exp(sink - m)
        emit(q_rows, _unstack_heads(o[:, :LANES] * (1.0 / l), low), _unstack_heads(m + jnp.log(l), low))

    _tiles(d, half_window, tile)


def _bwd_tiles(qn_ref, kp_ref, vp_ref, bias_ref, do_ref, lse_ref, delta_ref, dq_ref, dk_ref, dv_ref, ds_ref,
               *, d, half_window, sinks=None, dsink_ref=None):
    low = _low_half()
    w = Q_BLOCK + 2 * half_window
    sink = None if sinks is None else _per_head(sinks)

    def rows_of(t):
        return jnp.concatenate([t[:, 0:1], t[:, HEAD_DIM:HEAD_DIM + 1]], axis=0)

    def tile(q_rows, k_rows, edge):
        q2 = _stack_heads(qn_ref[q_rows, :], low)
        do2 = _stack_heads(do_ref[q_rows, :], low)
        k = kp_ref[k_rows, :].astype(BF16)
        v = vp_ref[k_rows, :].astype(BF16)
        lse = rows_of(lse_ref[q_rows, :])
        delta = rows_of(delta_ref[q_rows, :])
        p = jnp.exp(_dot(q2, k, NT) + bias_ref[...] + edge - lse)
        ds = p * (_dot(do2, v, NT) - delta)
        ds_ref[...] += ds
        if sink is not None:
            dsink_ref[...] += (-jnp.exp(sink - lse) * delta).reshape(2, Q_BLOCK, 1)
        dsb, pb = ds.astype(BF16), p.astype(BF16)
        dq_ref[q_rows, :] = _unstack_heads(_dot(dsb, k), low)
        dk_ref[k_rows, :] += _dot(dsb, q2, TN)
        dv_ref[k_rows, :] += _dot(pb, do2, TN)

    _tiles(d, half_window, tile)


def _norm_bwd(raw_ref, gain_ref, dn_ref, dn_offset, out_ref, scale):
    low = _low_half()

    def step(r0, dgain):
        t = raw_ref[pl.ds(r0, CHUNK), :].astype(F32)
        dn = dn_ref[pl.ds(dn_offset + r0, CHUNK), :]
        dth = dn * (gain_ref[...] * scale)
        sums = _half_sum(jnp.concatenate([t * t, dth * t], axis=0), low)
        r = lax.rsqrt(sums[:CHUNK] * (1.0 / HEAD_DIM) + EPS)
        th = t * r
        out_ref[pl.ds(r0, CHUNK), :] = (r * (dth - th * (r * sums[CHUNK:] * (1.0 / HEAD_DIM)))).astype(BF16)
        return dgain + jnp.sum(dn * th, axis=0, keepdims=True) * scale

    return _chunks(step, jnp.zeros((1, LANES), F32))


def _rows8(v):
    return jnp.broadcast_to(v, (8, v.shape[-1]))


A_W = Q_BLOCK + 2 * A_HALF_WINDOW
A_PAD = A_HALF_WINDOW


def _seq_block(col_fn):
    return pl.BlockSpec((SEQ, LANES), col_fn)


def _attn_a_fwd(qkv, gain_q, gain_k, bias, sink):
    def body(sink_ref, q_ref, k_ref, v_ref, gq_ref, gk_ref, line_ref, o_ref, lse_ref, qn_ref, kp_ref, vp_ref,
             bias_ref):
        hp = pl.program_id(0)
        keep = (lax.broadcasted_iota(jnp.int32, (1, LANES), 1) // HEAD_DIM) == hp // 2
        _prep_q(q_ref, gq_ref, qn_ref)
        _prep_kv(k_ref, v_ref, gk_ref, kp_ref, vp_ref, A_PAD, keep)
        _unroll_bias(line_ref, bias_ref, A_W)

        def emit(rows, out, lse):
            o_ref[rows, :] = out
            lse_ref[rows, :] = lse

        _fwd_tiles(qn_ref, kp_ref, vp_ref, bias_ref, emit, d=1, half_window=A_HALF_WINDOW,
                   sinks=(sink_ref[2 * hp], sink_ref[2 * hp + 1]))

    vec = pl.BlockSpec((1, LANES), lambda hp, s: (0, 0))
    return pl.pallas_call(
        body, name="attn_a_fwd",
        grid_spec=pltpu.PrefetchScalarGridSpec(
            num_scalar_prefetch=1, grid=(4,),
            in_specs=[_seq_block(lambda hp, s: (0, QA_BLK + hp)), _seq_block(lambda hp, s: (0, KA_BLK)),
                      _seq_block(lambda hp, s: (0, VA_BLK)), vec, vec,
                      pl.BlockSpec((None, 2, _line_width(A_HALF_WINDOW)), lambda hp, s: (hp, 0, 0))],
            out_specs=[_seq_block(lambda hp, s: (0, hp)), _seq_block(lambda hp, s: (0, hp))],
            scratch_shapes=[pltpu.VMEM((SEQ, LANES), F32), pltpu.VMEM((SEQ + 2 * A_PAD, LANES), F32),
                            pltpu.VMEM((SEQ + 2 * A_PAD, LANES), F32), pltpu.VMEM((2 * Q_BLOCK, A_W), F32)]),
        out_shape=[jax.ShapeDtypeStruct((SEQ, 512), F32)] * 2,
        compiler_params=_params("arbitrary"),
    )(sink.reshape(8), qkv, qkv, qkv, gain_q, gain_k, bias)


def _attn_a_bwd(qkv, gain_q, gain_k, bias, sink, delta, lse, d_out):
    def body(sink_ref, q_ref, k_ref, v_ref, gq_ref, gk_ref, line_ref, delta_ref, lse_ref, do_ref,
             dq_out, dkv_out, dgq_out, dgk_out, dline_out, dsink_out,
             qn_ref, kp_ref, vp_ref, dq_ref, dk_ref, dv_ref, dk_tot, dv_tot, bias_ref, ds_out):
        hp = pl.program_id(0)
        kv_head = hp // 2
        keep = (lax.broadcasted_iota(jnp.int32, (1, LANES), 1) // HEAD_DIM) == kv_head
        _prep_q(q_ref, gq_ref, qn_ref)
        _prep_kv(k_ref, v_ref, gk_ref, kp_ref, vp_ref, A_PAD, keep)
        _unroll_bias(line_ref, bias_ref, A_W)
        dk_ref[...] = jnp.zeros_like(dk_ref)
        dv_ref[...] = jnp.zeros_like(dv_ref)
        ds_out[...] = jnp.zeros_like(ds_out)
        dsink_out[...] = jnp.zeros_like(dsink_out)

        @pl.when(hp == 0)
        def _():
            dk_tot[...] = jnp.zeros_like(dk_tot)
            dv_tot[...] = jnp.zeros_like(dv_tot)

        _bwd_tiles(qn_ref, kp_ref, vp_ref, bias_ref, do_ref, lse_ref, delta_ref, dq_ref, dk_ref, dv_ref, ds_out,
                   d=1, half_window=A_HALF_WINDOW, sinks=(sink_ref[2 * hp], sink_ref[2 * hp + 1]),
                   dsink_ref=dsink_out)
        _fold_bias_grad(ds_out, dline_out, A_W)
        dgq_out[...] = _rows8(_norm_bwd(q_ref, gq_ref, dq_ref, 0, dq_out, SCALE))

        def fold(r0, carry):
            rows = pl.ds(A_PAD + r0, CHUNK)
            for acc, tot in ((dk_ref, dk_tot), (dv_ref, dv_tot)):
                t = acc[rows, :]
                tot[pl.ds(r0, CHUNK), :] += jnp.where(keep, t + pltpu.roll(t, HEAD_DIM, 1), 0.0)
            return carry

        _chunks(fold)

        @pl.when(hp == 3)
        def _():
            dgk_out[...] = _rows8(_norm_bwd(k_ref, gk_ref, dk_tot, 0, dkv_out.at[0], 1.0))
            dkv_out[1] = dv_tot[...].astype(BF16)

    vec = pl.BlockSpec((1, LANES), lambda hp, s: (0, 0))
    seq_f32 = pltpu.VMEM((SEQ, LANES), F32)
    padded = pltpu.VMEM((SEQ + 2 * A_PAD, LANES), F32)
    return pl.pallas_call(
        body, name="attn_a_bwd",
        grid_spec=pltpu.PrefetchScalarGridSpec(
            num_scalar_prefetch=1, grid=(4,),
            in_specs=[_seq_block(lambda hp, s: (0, QA_BLK + hp)), _seq_block(lambda hp, s: (0, KA_BLK)),
                      _seq_block(lambda hp, s: (0, VA_BLK)), vec, vec,
                      pl.BlockSpec((None, 2, _line_width(A_HALF_WINDOW)), lambda hp, s: (hp, 0, 0)),
                      _seq_block(lambda hp, s: (0, hp)), _seq_block(lambda hp, s: (0, hp)),
                      _seq_block(lambda hp, s: (0, hp))],
            out_specs=[pl.BlockSpec((None, SEQ, LANES), lambda hp, s: (hp, 0, 0)),
                       pl.BlockSpec((2, SEQ, LANES), lambda hp, s: (0, 0, 0)),
                       pl.BlockSpec((None, 8, LANES), lambda hp, s: (hp, 0, 0)),
                       pl.BlockSpec((8, LANES), lambda hp, s: (0, 0)),
                       pl.BlockSpec((None, 2, _line_width(A_HALF_WINDOW)), lambda hp, s: (hp, 0, 0)),
                       pl.BlockSpec((None, 2, Q_BLOCK, 1), lambda hp, s: (hp, 0, 0, 0))],
            scratch_shapes=[seq_f32, padded, padded, seq_f32, padded, padded, seq_f32, seq_f32,
                            pltpu.VMEM((2 * Q_BLOCK, A_W), F32), pltpu.VMEM((2 * Q_BLOCK, A_W), F32)]),
        out_shape=[jax.ShapeDtypeStruct((4, SEQ, LANES), BF16), jax.ShapeDtypeStruct((2, SEQ, LANES), BF16),
                   jax.ShapeDtypeStruct((4, 8, LANES), F32), jax.ShapeDtypeStruct((8, LANES), F32),
                   jax.ShapeDtypeStruct((4, 2, _line_width(A_HALF_WINDOW)), F32),
                   jax.ShapeDtypeStruct((4, 2, Q_BLOCK, 1), F32)],
        compiler_params=_params("arbitrary"),
    )(sink.reshape(8), qkv, qkv, qkv, gain_q, gain_k, bias, delta, lse, d_out)


B_W = Q_BLOCK + 2 * B_HALF_WINDOW
B_PAD_MAX = B_HALF_WINDOW * B_DILATIONS[-1]


def _attn_b_fwd(qkv, gain_q, gain_k, bias):
    def body(q_ref, k_ref, v_ref, gq_ref, gk_ref, line_ref, o_ref, lse_ref, qn_ref, kp_ref, vp_ref, bias_ref):
        g = pl.program_id(1)
        _prep_q(q_ref, gq_ref, qn_ref)
        _unroll_bias(line_ref, bias_ref, B_W)

        def first(rows, out, lse):
            o_ref[rows, :] = out
            lse_ref[rows, :] = lse

        def combine(rows, out, lse):
            old = lse_ref[rows, :]
            new = jnp.maximum(old, lse) + jnp.log(1.0 + jnp.exp(-jnp.abs(old - lse)))
            o_ref[rows, :] = o_ref[rows, :] * jnp.exp(old - new) + out * jnp.exp(lse - new)
            lse_ref[rows, :] = new

        for gi, d in enumerate(B_DILATIONS):
            @pl.when(g == gi)
            def _():
                _prep_kv(k_ref, v_ref, gk_ref, kp_ref, vp_ref, B_HALF_WINDOW * d)
                _fwd_tiles(qn_ref, kp_ref, vp_ref, bias_ref, first if gi == 0 else combine,
                           d=d, half_window=B_HALF_WINDOW)

    vec = pl.BlockSpec((1, LANES), lambda hp, g: (0, 0))
    padded = pltpu.VMEM((SEQ + 2 * B_PAD_MAX, LANES), F32)
    return pl.pallas_call(
        body, name="attn_b_fwd", grid=(4, 3),
        in_specs=[_seq_block(lambda hp, g: (0, QB_BLK + 4 * g + hp)), _seq_block(lambda hp, g: (0, KB_BLK + 4 * g + hp)),
                  _seq_block(lambda hp, g: (0, VB_BLK + 4 * g + hp)), vec, vec,
                  pl.BlockSpec((None, 2, _line_width(B_HALF_WINDOW)), lambda hp, g: (4 * g + hp, 0, 0))],
        out_specs=[_seq_block(lambda hp, g: (0, hp)), _seq_block(lambda hp, g: (0, hp))],
        out_shape=[jax.ShapeDtypeStruct((SEQ, 512), F32)] * 2,
        scratch_shapes=[pltpu.VMEM((SEQ, LANES), F32), padded, padded, pltpu.VMEM((2 * Q_BLOCK, B_W), F32)],
        compiler_params=_params("arbitrary", "arbitrary"),
    )(qkv, qkv, qkv, gain_q, gain_k, bias)


def _attn_b_bwd(qkv, gain_q, gain_k, bias, delta, lse, d_out):
    def body(q_ref, k_ref, v_ref, gq_ref, gk_ref, line_ref, delta_ref, lse_ref, do_ref,
             dq_out, dk_out, dv_out, dgq_out, dgk_out, dline_out,
             qn_ref, kp_ref, vp_ref, dq_ref, dk_ref, dv_ref, bias_ref, ds_out):
        g = pl.program_id(1)
        _prep_q(q_ref, gq_ref, qn_ref)
        _unroll_bias(line_ref, bias_ref, B_W)
        ds_out[...] = jnp.zeros_like(ds_out)
        for gi, d in enumerate(B_DILATIONS):
            @pl.when(g == gi)
            def _():
                pad = B_HALF_WINDOW * d
                for acc in (dk_ref, dv_ref):
                    acc[pl.ds(0, SEQ + 2 * pad), :] = jnp.zeros((SEQ + 2 * pad, LANES), F32)
                _prep_kv(k_ref, v_ref, gk_ref, kp_ref, vp_ref, pad)
                _bwd_tiles(qn_ref, kp_ref, vp_ref, bias_ref, do_ref, lse_ref, delta_ref, dq_ref, dk_ref, dv_ref,
                           ds_out, d=d, half_window=B_HALF_WINDOW)
                dgk_out[...] = _rows8(_norm_bwd(k_ref, gk_ref, dk_ref, pad, dk_out, 1.0))
                dv_out[...] = dv_ref[pl.ds(pad, SEQ), :].astype(BF16)
        _fold_bias_grad(ds_out, dline_out, B_W)
        dgq_out[...] = _rows8(_norm_bwd(q_ref, gq_ref, dq_ref, 0, dq_out, SCALE))

    vec = pl.BlockSpec((1, LANES), lambda hp, g: (0, 0))
    seq_f32 = pltpu.VMEM((SEQ, LANES), F32)
    padded = pltpu.VMEM((SEQ + 2 * B_PAD_MAX, LANES), F32)
    part = pl.BlockSpec((None, 8, LANES), lambda hp, g: (4 * g + hp, 0, 0))
    line = pl.BlockSpec((None, 2, _line_width(B_HALF_WINDOW)), lambda hp, g: (4 * g + hp, 0, 0))
    return pl.pallas_call(
        body, name="attn_b_bwd", grid=(4, 3),
        in_specs=[_seq_block(lambda hp, g: (0, QB_BLK + 4 * g + hp)), _seq_block(lambda hp, g: (0, KB_BLK + 4 * g + hp)),
                  _seq_block(lambda hp, g: (0, VB_BLK + 4 * g + hp)), vec, vec,
                  line,
                  _seq_block(lambda hp, g: (0, hp)), _seq_block(lambda hp, g: (0, hp)), _seq_block(lambda hp, g: (0, hp))],
        out_specs=[pl.BlockSpec((None, SEQ, LANES), lambda hp, g: (4 * g + hp, 0, 0))] * 3 + [part, part, line],
        out_shape=[jax.ShapeDtypeStruct((12, SEQ, LANES), BF16)] * 3
        + [jax.ShapeDtypeStruct((12, 8, LANES), F32)] * 2
        + [jax.ShapeDtypeStruct((12, 2, _line_width(B_HALF_WINDOW)), F32)],
        scratch_shapes=[seq_f32, padded, padded, seq_f32, padded, padded,
                        pltpu.VMEM((2 * Q_BLOCK, B_W), F32), pltpu.VMEM((2 * Q_BLOCK, B_W), F32)],
        compiler_params=_params("arbitrary", "arbitrary"),
    )(qkv, qkv, qkv, gain_q, gain_k, bias, delta, lse, d_out)


def _sigmoid(t):
    return 1.0 / (1.0 + jnp.exp(-t))


def _middle(out_a, out_b, gates, x, target, w_a, w_b, w_out, b_merge):
    tm = 256
    n_steps = SEQ // tm

    def body(oa_ref, ob_ref, g_ref, x_ref, t_ref, wa_ref, wb_ref, wo_ref, bm_ref,
             dy_ref, dg_ref, doa_ref, dob_ref, dla_ref, dlb_ref, dwa_ref, dwb_ref, dwo_ref, dbm_ref, sq_ref):
        @pl.when(pl.program_id(0) == 0)
        def _():
            for ref in (dwa_ref, dwb_ref, dwo_ref, dbm_ref, sq_ref):
                ref[...] = jnp.zeros_like(ref)

        gate_a, gate_b = g_ref[:, 0:512], g_ref[:, 512:1024]
        sig_a, sig_b = _sigmoid(gate_a), _sigmoid(gate_b)
        silu_a, silu_b = gate_a * sig_a, gate_b * sig_b
        oa, ob = oa_ref[...], ob_ref[...]
        ya, yb = (oa * silu_a).astype(BF16), (ob * silu_b).astype(BF16)
        br_a, br_b = _dot(ya, wa_ref[...]), _dot(yb, wb_ref[...])
        m0 = _sigmoid(g_ref[:, 1024:2048] + bm_ref[0:1, :])
        m1 = _sigmoid(g_ref[:, 2048:3072] + bm_ref[1:2, :])
        merged = (m0 * br_a + m1 * br_b).astype(BF16)
        err = (x_ref[...] + _dot(merged, wo_ref[...])) - t_ref[...]
        sq_ref[...] += jnp.sum(err * err, axis=0, keepdims=True)

        dy = err * (1.0 / D_MODEL)
        dy_ref[...] = dy
        dyb = dy.astype(BF16)
        dmerged = _dot(dyb, wo_ref[...], NT)
        dwo_ref[...] += _dot(merged, dyb, TN)
        dbr_a, dbr_b = (dmerged * m0).astype(BF16), (dmerged * m1).astype(BF16)
        dm0 = (dmerged * br_a) * (m0 * (1.0 - m0))
        dm1 = (dmerged * br_b) * (m1 * (1.0 - m1))
        dbm_ref[0:1, :] += jnp.sum(dm0, axis=0, keepdims=True)
        dbm_ref[1:2, :] += jnp.sum(dm1, axis=0, keepdims=True)
        for s in range(N_CHIPS):
            cols = slice(256 * s, 256 * (s + 1))
            dwa_ref[s] += _dot(ya, dbr_a[:, cols], TN)
            dwb_ref[s] += _dot(yb, dbr_b[:, cols], TN)
        dya, dyb_ = _dot(dbr_a, wa_ref[...], NT), _dot(dbr_b, wb_ref[...], NT)
        doa, dob = dya * silu_a, dyb_ * silu_b
        doa_ref[...] = doa
        dob_ref[...] = dob
        for blk in range(512 // LANES):
            lanes = slice(blk * LANES, (blk + 1) * LANES)
            dla_ref[:, lanes] = _half_sum(doa[:, lanes] * oa[:, lanes], None)
            dlb_ref[:, lanes] = _half_sum(dob[:, lanes] * ob[:, lanes], None)
        d_gates = (((dya * oa) * (sig_a * (1.0 + gate_a * (1.0 - sig_a)))).astype(BF16),
                   ((dyb_ * ob) * (sig_b * (1.0 + gate_b * (1.0 - sig_b)))).astype(BF16),
                   dm0.astype(BF16), dm1.astype(BF16))
        blk = 0
        for part in d_gates:
            for c0 in range(0, part.shape[1], 256):
                dg_ref[blk] = part[:, c0:c0 + 256]
                blk += 1

    def rows(width):
        return pl.BlockSpec((tm, width), lambda i: (i, 0))

    def whole(*shape):
        return pl.BlockSpec(shape, lambda i: (0,) * len(shape))

    return pl.pallas_call(
        body, name="middle", grid=(n_steps,),
        in_specs=[rows(512), rows(512), rows(GATE_WIDTH), rows(D_MODEL), rows(D_MODEL),
                  whole(512, D_MODEL), whole(512, D_MODEL), whole(D_MODEL, D_MODEL), whole(2, D_MODEL)],
        out_specs=[rows(D_MODEL), pl.BlockSpec((GATE_WIDTH // 256, tm, 256), lambda i: (0, i, 0)),
                   rows(512), rows(512), rows(512), rows(512),
                   whole(N_CHIPS, 512, 256), whole(N_CHIPS, 512, 256), whole(D_MODEL, D_MODEL),
                   whole(2, D_MODEL), whole(1, D_MODEL)],
        out_shape=[jax.ShapeDtypeStruct((SEQ, D_MODEL), F32), jax.ShapeDtypeStruct((GATE_WIDTH // 256, SEQ, 256), BF16),
                   jax.ShapeDtypeStruct((SEQ, 512), F32), jax.ShapeDtypeStruct((SEQ, 512), F32),
                   jax.ShapeDtypeStruct((SEQ, 512), F32), jax.ShapeDtypeStruct((SEQ, 512), F32),
                   jax.ShapeDtypeStruct((N_CHIPS, 512, 256), F32), jax.ShapeDtypeStruct((N_CHIPS, 512, 256), F32),
                   jax.ShapeDtypeStruct((D_MODEL, D_MODEL), F32), jax.ShapeDtypeStruct((2, D_MODEL), F32),
                   jax.ShapeDtypeStruct((1, D_MODEL), F32)],
        compiler_params=_params("arbitrary"),
    )(out_a, out_b, gates, x, target, w_a, w_b, w_out, b_merge)


def _which(j, edges, fns):
    lo = 0
    for hi, fn in zip(edges, fns):
        pl.when((j >= lo) & (j < hi))(fn)
        lo = hi


def _sibling_rows(tile, core):
    lo, hi = tile * W_BLOCK, (tile + 1) * W_BLOCK
    for chip in range(N_CHIPS):
        a = chip * W_IN_SHARD + (1 - core) * (W_IN_SHARD // 2)
        first, last = max(lo, a), min(hi, a + W_IN_SHARD // 2)
        if first < last:
            return chip, first - a, first - lo, last - first
    return None


def _d_w_in(d_proj, h, rest=None):
    plan, step, width = [], 0, 0
    for p in d_proj:
        total = p.shape[0] * p.shape[2]
        if width + total <= W_BLOCK:
            plan.append((p.shape[0], step, 1))
            width += total
            if width == W_BLOCK:
                step, width = step + 1, 0
        else:
            assert width == 0 and total % W_BLOCK == 0
            plan.append((W_BLOCK // p.shape[2], step, total // W_BLOCK))
            step += total // W_BLOCK
    assert width == 0 and step == IN_WIDTH // W_BLOCK
    firsts = sorted({first for _, first, _ in plan})
    edges = firsts[1:] + [step]
    halves = 2

    hand_over = rest is not None
    half = W_IN_SHARD // 2

    def body(*refs):
        if hand_over:
            pieces, h_ref, rest_ref = refs[:len(d_proj)], refs[len(d_proj)], refs[len(d_proj) + 1]
            o_ref, got_ref, got_rest_ref, acc_ref, send_sems, recv_sems, stage = refs[len(d_proj) + 2:]
        else:
            pieces, h_ref, o_ref, acc_ref = refs[:-3], refs[-3], refs[-2], refs[-1]
        k = pl.program_id(1)

        def emit(group):
            def fn():
                cols = jnp.concatenate([ref[b] for ref in group for b in range(ref.shape[0])], axis=1)
                term = _dot(cols, h_ref[...], TN)

                @pl.when(k == 0)
                def _():
                    acc_ref[...] = term

                @pl.when(k == halves - 1)
                def _():
                    o_ref[...] = (acc_ref[...] + term).astype(BF16)
            return fn

        groups = [[ref for ref, (_, first, _) in zip(pieces, plan) if first == f] for f in firsts]
        _which(pl.program_id(0), edges, [emit(group) for group in groups])

        if hand_over:
            cx, cy, c = lax.axis_index("x"), lax.axis_index("y"), lax.axis_index("c")
            sibling = (cx, cy, 1 - c)

            def to_sibling(sem, src, dst, recv=0):
                return pltpu.make_async_remote_copy(src_ref=src, dst_ref=dst, send_sem=send_sems.at[sem],
                                                    recv_sem=recv_sems.at[recv], device_id=sibling, device_id_type=MESH)

            def tile_copy(tile, core):
                chip, row, start, rows = _sibling_rows(tile, core)
                return to_sibling(tile % 2, stage.at[tile % 2, pl.ds(0, rows), :], got_ref.at[chip, pl.ds(row, rows), :])

            rest_copy = to_sibling(2, _half_rows(rest_ref, 1 - c), got_rest_ref, recv=1)

            @pl.when((pl.program_id(0) == 0) & (k == 0))
            def _():
                rest_copy.start()

            for tile in range(step):
                for core in range(2):
                    @pl.when((pl.program_id(0) == tile) & (k == halves - 1) & (c == core))
                    def _(tile=tile, core=core):
                        if tile >= 2 and _sibling_rows(tile - 2, core):
                            tile_copy(tile - 2, core).wait_send()
                        if _sibling_rows(tile, core):
                            _, _, start, rows = _sibling_rows(tile, core)
                            stage[tile % 2, 0:rows, :] = o_ref[start:start + rows, :]
                            tile_copy(tile, core).start()
                        if tile == step - 1:
                            for last in (step - 2, step - 1):
                                if _sibling_rows(last, core):
                                    tile_copy(last, core).wait_send()
                            rest_copy.wait()
                            to_sibling(0, got_ref, got_ref).wait_recv()

    def cols_spec(piece, n, first, steps):
        def index(j, k):
            return jnp.clip(j - first, 0, steps - 1), jnp.where((j >= first) & (j < first + steps), k, 0), 0
        return pl.BlockSpec((n, SEQ // halves, piece.shape[2]), index)

    tile_spec = pl.BlockSpec((W_BLOCK, D_MODEL), lambda j, k: (j, 0))
    in_specs = [cols_spec(p, *pl_) for p, pl_ in zip(d_proj, plan)] + [
        pl.BlockSpec((SEQ // halves, D_MODEL), lambda j, k: (k, 0))]
    acc = pltpu.VMEM((W_BLOCK, D_MODEL), F32)
    if not hand_over:
        return pl.pallas_call(
            body, name="d_w_in", grid=(step, halves), in_specs=in_specs, out_specs=tile_spec,
            out_shape=jax.ShapeDtypeStruct((IN_WIDTH, D_MODEL), BF16), scratch_shapes=[acc],
            compiler_params=_params("arbitrary", "arbitrary"),
        )(*d_proj, h)
    return pl.pallas_call(
        body, name="d_w_in", grid=(step, halves), in_specs=in_specs + [ANY], out_specs=[tile_spec, ANY, ANY],
        out_shape=[jax.ShapeDtypeStruct((IN_WIDTH, D_MODEL), BF16),
                   jax.ShapeDtypeStruct((N_CHIPS, half, D_MODEL), BF16),
                   jax.ShapeDtypeStruct((N_CHIPS, rest.shape[1] // 2, D_MODEL), BF16)],
        scratch_shapes=[acc, pltpu.SemaphoreType.DMA((3,)), pltpu.SemaphoreType.DMA((2,)),
                        pltpu.VMEM((2, W_BLOCK, D_MODEL), BF16)],
        compiler_params=_params("arbitrary", "arbitrary"),
    )(*d_proj, h, rest)


RELAY_STEP = 10
RELAY_ROWS = 352


def _d_x(d_proj, w_t, x, gain, dy, chip_sums):
    tm = 256
    n_steps = SEQ // tm
    n_w = IN_WIDTH // W_BLOCK
    n_p, n_s = len(d_proj), len(chip_sums)

    def body(*refs):
        pieces, w_refs = refs[:n_p], refs[n_p:n_p + n_w]
        x_ref, g_ref, dy_ref = refs[n_p + n_w:n_p + n_w + 3]
        q_refs = refs[n_p + n_w + 3:n_p + n_w + 3 + n_s]
        dx_ref, dgain_ref = refs[n_p + n_w + 3 + n_s:n_p + n_w + 5 + n_s]
        outs = refs[n_p + n_w + 5 + n_s:n_p + n_w + 5 + 4 * n_s]
        got_refs, relay_refs, sum_refs = outs[:n_s], outs[n_s:2 * n_s], outs[2 * n_s:]
        if n_s:
            send_sems, recv_sems, local_sems, a_buf, b_buf, c_buf = refs[n_p + n_w + 5 + 4 * n_s:]

        def hops():
            cx, cy, c = lax.axis_index("x"), lax.axis_index("y"), lax.axis_index("c")
            near = (cx + (1 - c) - 2 * cx * (1 - c), cy + c - 2 * cy * c)
            far = (cx + c - 2 * cx * c, cy + (1 - c) - 2 * cy * (1 - c))
            chip = lambda p: 2 * p[0] + p[1]

            def copy(k, src, dst, to):
                return pltpu.make_async_remote_copy(src_ref=src, dst_ref=dst, send_sem=send_sems.at[k],
                                                    recv_sem=recv_sems.at[k], device_id=(*to, c), device_id_type=MESH)

            first = [(copy(3 * b, q.at[chip(near)], got.at[0], near),
                      copy(3 * b + 1, q.at[3 - chip((cx, cy))], relay, near))
                     for b, (q, got, relay) in enumerate(zip(q_refs, got_refs, relay_refs))]
            second = [copy(3 * b + 2, s, got.at[1], far) for b, (s, got) in enumerate(zip(sum_refs, got_refs))]
            return first, second, chip(far)

        @pl.when(pl.program_id(0) == 0)
        def _():
            dgain_ref[...] = jnp.zeros_like(dgain_ref)
            if n_s:
                for direct, pass_on in hops()[0]:
                    direct.start()
                    pass_on.start()

        if n_s:
            @pl.when(pl.program_id(0) == RELAY_STEP)
            def _():
                first, second, far_chip = hops()
                for b, (q, relay, total) in enumerate(zip(q_refs, relay_refs, sum_refs)):
                    first[b][1].wait_recv()
                    half = relay.shape[0]
                    for r0 in range(0, half, RELAY_ROWS):
                        rows = min(RELAY_ROWS, half - r0)
                        mine = pltpu.make_async_copy(q.at[far_chip, pl.ds(r0, rows), :], a_buf.at[pl.ds(0, rows), :],
                                                     local_sems.at[0])
                        theirs = pltpu.make_async_copy(relay.at[pl.ds(r0, rows), :], b_buf.at[pl.ds(0, rows), :],
                                                       local_sems.at[1])
                        mine.start()
                        theirs.start()
                        mine.wait()
                        theirs.wait()
                        c_buf[0:rows, :] = (a_buf[0:rows, :].astype(F32) + b_buf[0:rows, :].astype(F32)).astype(BF16)
                        store = pltpu.make_async_copy(c_buf.at[pl.ds(0, rows), :], total.at[pl.ds(r0, rows), :],
                                                      local_sems.at[2])
                        store.start()
                        store.wait()
                    second[b].start()

        blocks = [(piece, k) for piece in pieces for k in range(piece.shape[0])]
        dh, group, width, blk = None, [], 0, 0
        for piece, k in blocks:
            group.append(piece[k])
            width += piece.shape[2]
            if width == W_BLOCK:
                term = _dot(jnp.concatenate(group, axis=1), w_refs[blk][...])
                dh = term if dh is None else dh + term
                group, width, blk = [], 0, blk + 1
        assert not group and blk == n_w
        xf = x_ref[...]
        r = lax.rsqrt(jnp.mean(xf * xf, axis=-1, keepdims=True) + EPS)
        xh = xf * r
        dxh = dh * g_ref[...]
        dx_ref[...] = r * (dxh - xh * jnp.mean(dxh * xh, axis=-1, keepdims=True)) + dy_ref[...]
        dgain_ref[...] += _rows8(jnp.sum(dh * xh, axis=0, keepdims=True))

        if n_s:
            @pl.when(pl.program_id(0) == n_steps - 1)
            def _():
                first, second, _ = hops()
                for direct, pass_on in first:
                    direct.wait()
                    pass_on.wait_send()
                for cp in second:
                    cp.wait()

    row = pl.BlockSpec((tm, D_MODEL), lambda i: (i, 0))
    halves = [q.shape[1] for q in chip_sums]
    res = pl.pallas_call(
        body, name="d_x", grid=(n_steps,),
        in_specs=[pl.BlockSpec((p.shape[0], tm, p.shape[2]), lambda i: (0, i, 0)) for p in d_proj] + _w_blocks(0, n_w)
        + [row, pl.BlockSpec((1, D_MODEL), lambda i: (0, 0)), row] + [ANY] * n_s,
        out_specs=[row, pl.BlockSpec((8, D_MODEL), lambda i: (0, 0))] + [ANY] * (3 * n_s),
        out_shape=[jax.ShapeDtypeStruct((SEQ, D_MODEL), F32), jax.ShapeDtypeStruct((8, D_MODEL), F32)]
        + [jax.ShapeDtypeStruct((2, half, D_MODEL), BF16) for half in halves]
        + [jax.ShapeDtypeStruct((half, D_MODEL), BF16) for half in halves] * 2,
        scratch_shapes=[pltpu.SemaphoreType.DMA((3 * n_s,)), pltpu.SemaphoreType.DMA((3 * n_s,)),
                        pltpu.SemaphoreType.DMA((3,))] + [pltpu.VMEM((RELAY_ROWS, D_MODEL), BF16)] * 3 if n_s else [],
        compiler_params=_params("arbitrary"),
    )(*d_proj, *([w_t] * n_w), x, gain, dy, *chip_sums)
    return res[0], res[1], res[2:2 + n_s]


def _my_place():
    x, y, c = lax.axis_index("x"), lax.axis_index("y"), lax.axis_index("c")
    return jnp.stack([2 * x + y, c]).astype(jnp.int32)


def _half_rows(ref, half):
    rows = ref.shape[-2] // 2
    idx = (slice(None),) * (len(ref.shape) - 2) + (pl.ds(pl.multiple_of(half * rows, 16), rows), slice(None))
    return ref.at[idx]


def _add_halves(place, grads, theirs, name):
    half = theirs.shape[1]
    tr = _row_tile(half)
    n = half // tr

    def body(place_ref, g_ref, t_ref, o_ref):
        o_ref[...] = (g_ref[...].astype(F32) + t_ref[...].astype(F32)).astype(BF16)

    return pl.pallas_call(
        body, name=name,
        grid_spec=pltpu.PrefetchScalarGridSpec(
            num_scalar_prefetch=1, grid=(N_CHIPS, n),
            in_specs=[pl.BlockSpec((None, tr, D_MODEL), lambda s, i, p: (s, p[1] * n + i, 0)),
                      pl.BlockSpec((None, tr, D_MODEL), lambda s, i, p: (s, i, 0))],
            out_specs=pl.BlockSpec((None, tr, D_MODEL), lambda s, i, p: (s, i, 0))),
        out_shape=jax.ShapeDtypeStruct((N_CHIPS, half, D_MODEL), BF16),
        compiler_params=_params("arbitrary", "arbitrary"),
    )(place, grads, theirs)


def _add_chips(place, chip_sums, others, name):
    half = others.shape[1]
    tr = _row_tile(half)
    n = half // tr

    def body(place_ref, q_ref, o_ref, r_ref):
        acc = q_ref[...].astype(F32)
        for j in range(others.shape[0]):
            acc = acc + o_ref[j].astype(F32)
        r_ref[...] = acc

    return pl.pallas_call(
        body, name=name,
        grid_spec=pltpu.PrefetchScalarGridSpec(
            num_scalar_prefetch=1, grid=(n,),
            in_specs=[pl.BlockSpec((None, tr, D_MODEL), lambda i, p: (p[0], i, 0)),
                      pl.BlockSpec((others.shape[0], tr, D_MODEL), lambda i, p: (0, i, 0))],
            out_specs=pl.BlockSpec((tr, D_MODEL), lambda i, p: (p[1] * n + i, 0))),
        out_shape=jax.ShapeDtypeStruct((2 * half, D_MODEL), F32),
        compiler_params=_params("arbitrary"),
    )(place, chip_sums, others)


def _join_halves(shards, block):
    n = len(shards)
    rows = block.shape[0]

    def body(*refs):
        b_ref, o_refs, sum_ref = refs[n], refs[n + 1:2 * n + 1], refs[2 * n + 1]
        send_sems, recv_sems, small_send, small_recv, local_sem, all_ref = refs[2 * n + 2:]
        x, y, c = lax.axis_index("x"), lax.axis_index("y"), lax.axis_index("c")
        me, sibling = (x, y, c), (x, y, 1 - c)
        chips = [(1 - x, y), (x, 1 - y), (1 - x, 1 - y)]

        def half(k, rows_ref):
            return pltpu.make_async_remote_copy(src_ref=rows_ref, dst_ref=rows_ref, send_sem=send_sems.at[k],
                                                recv_sem=recv_sems.at[k], device_id=sibling, device_id_type=MESH)

        def at(px, py, pc):
            return all_ref.at[pl.ds(pl.multiple_of((4 * px + 2 * py + pc) * rows, 8), rows), :]

        def small(k, block_of, to, src=None):
            return pltpu.make_async_remote_copy(src_ref=at(*block_of) if src is None else src, dst_ref=at(*block_of),
                                                send_sem=small_send.at[k], recv_sem=small_recv.at[k],
                                                device_id=to, device_id_type=MESH)

        sends = [half(k, _half_rows(o, c)) for k, o in enumerate(o_refs)]
        for cp in sends:
            cp.start()
        mine = pltpu.make_async_copy(b_ref, at(*me), local_sem)
        mine.start()
        first = [small(0, me, sibling, src=b_ref)]
        first += [small(1 + j, me, (*chip, c), src=b_ref) for j, chip in enumerate(chips)]
        for cp in first:
            cp.start()
        passed = [small(4 + j, (*chip, c), sibling) for j, chip in enumerate(chips)]
        for j, chip in enumerate(chips):
            small(1 + j, (*chip, c), me).wait_recv()
            passed[j].start()
        small(0, sibling, me).wait_recv()
        for j, chip in enumerate(chips):
            small(4 + j, (*chip, 1 - c), me).wait_recv()
        mine.wait()
        acc = all_ref[0:rows, :]
        for dev in range(1, 8):
            acc = acc + all_ref[rows * dev:rows * (dev + 1), :]
        sum_ref[...] = acc
        for k, o in enumerate(o_refs):
            half(k, _half_rows(o, 1 - c)).wait_recv()
        for cp in sends + first + passed:
            cp.wait_send()

    res = pl.pallas_call(
        body, name="reduce_join_halves", in_specs=[ANY] * n + [pl.BlockSpec(memory_space=pltpu.VMEM)],
        out_specs=[ANY] * n + [pl.BlockSpec(memory_space=pltpu.VMEM)],
        out_shape=[jax.ShapeDtypeStruct(s.shape, F32) for s in shards] + [jax.ShapeDtypeStruct(block.shape, F32)],
        input_output_aliases={k: k for k in range(n)},
        scratch_shapes=[pltpu.SemaphoreType.DMA((n,)), pltpu.SemaphoreType.DMA((n,)),
                        pltpu.SemaphoreType.DMA((7,)), pltpu.SemaphoreType.DMA((7,)), pltpu.SemaphoreType.DMA,
                        pltpu.VMEM((8 * rows, D_MODEL), F32)],
    )(*shards, block)
    return res[:n], res[n]


def _adamw_math(w, g, m, v):
    m = ADAM_B1 * m + (1.0 - ADAM_B1) * g
    v = ADAM_B2 * v + (1.0 - ADAM_B2) * (g * g)
    m_hat = m / (1.0 - ADAM_B1 ** ADAM_STEP)
    v_hat = v / (1.0 - ADAM_B2 ** ADAM_STEP)
    return -ADAM_LR * (m_hat / (jnp.sqrt(v_hat) + ADAM_EPS) + ADAM_WD * w), m, v


def _adamw(w, g, m, v, name):
    r, c = w.shape
    tr = _row_tile(r)

    def body(w_ref, g_ref, m_ref, v_ref, d_ref, nm_ref, nv_ref):
        d_ref[...], nm_ref[...], nv_ref[...] = _adamw_math(w_ref[...], g_ref[...], m_ref[...], v_ref[...])

    spec = pl.BlockSpec((tr, c), lambda i: (i, 0))
    return pl.pallas_call(
        body, name=name, grid=(r // tr,), in_specs=[spec] * 4, out_specs=[spec] * 3,
        out_shape=[jax.ShapeDtypeStruct((r, c), F32)] * 3, compiler_params=_params("arbitrary"),
    )(w, g, m, v)


def _adamw_small(ws, gs, ms, vs):
    n = len(ws)

    def body(*refs):
        ins, outs = refs[:4 * n], refs[4 * n:]
        for k in range(n):
            d, m, v = _adamw_math(ins[k][...], ins[n + k][...], ins[2 * n + k][...], ins[3 * n + k][...])
            outs[k][...], outs[n + k][...], outs[2 * n + k][...] = d, m, v

    shapes = [jax.ShapeDtypeStruct(w.shape, F32) for w in ws]
    res = pl.pallas_call(body, name="adamw_small", out_shape=shapes * 3)(*ws, *gs, *ms, *vs)
    return res[:n], res[n:2 * n], res[2 * n:]


def _fold_heads(partials):
    t = jnp.sum(partials[:, 0, :], axis=0)
    return (t[:HEAD_DIM] + t[HEAD_DIM:]).reshape(1, HEAD_DIM)


def _local_step(x, target, norm_gain, w_t, w_a, w_b, w_o, b_m, q_norm_a, k_norm_a, q_norm_b, k_norm_b, sink_a,
                rel_bias, start_reduce=None, small_shard=None):
    two = lambda gain: jnp.concatenate([gain, gain], axis=1)
    bias_a = _bias_lines(rel_bias[:, :8], A_HALF_WINDOW, 1)
    bias_b = jnp.concatenate([_bias_lines(rel_bias[:, 8 + 8 * g:16 + 8 * g], B_HALF_WINDOW, d)
                              for g, d in enumerate(B_DILATIONS)], axis=0)

    qkv, h, *small_all = _in_proj(x, norm_gain, w_t, 0, QKV_WIDTH // W_BLOCK, BF16, "in_proj_qkv", True, small_shard)
    if small_shard is not None:
        w_a, w_b, w_o, b_m = _unpack_weights(small_all[0])
    gates, = _in_proj(x, norm_gain, w_t, QKV_WIDTH // W_BLOCK, GATE_WIDTH // W_BLOCK, F32, "in_proj_gates", False)
    out_a, lse_a = _attn_a_fwd(qkv, two(q_norm_a), two(k_norm_a), bias_a, sink_a)
    out_b, lse_b = _attn_b_fwd(qkv, two(q_norm_b), two(k_norm_b), bias_b)

    dy, dgates, d_out_a, d_out_b, delta_a, delta_b, d_wa, d_wb, d_wo, d_bm, sq = _middle(
        out_a, out_b, gates, x, target, w_a, w_b, w_o, b_m)
    loss = (0.5 / D_MODEL) * jnp.sum(sq)

    dq_a, dkv_a, dgq_a, dgk_a, ds_a, dsink = _attn_a_bwd(
        qkv, two(q_norm_a), two(k_norm_a), bias_a, sink_a, delta_a, lse_a, d_out_a)
    dq_b, dk_b, dv_b, dgq_b, dgk_b, ds_b = _attn_b_bwd(
        qkv, two(q_norm_b), two(k_norm_b), bias_b, delta_b, lse_b, d_out_b)
    d_proj = (dq_a, dkv_a, dq_b, dk_b, dv_b, dgates)

    d_bm_rows = jnp.pad(d_bm.reshape(2, N_CHIPS, 256).transpose(1, 0, 2),
                        ((0, 0), (0, REST_ROWS - 514), (0, D_MODEL - 256)))
    rest = jnp.concatenate([d_wo.reshape(N_CHIPS, 256, D_MODEL), d_wa.reshape(N_CHIPS, 128, D_MODEL),
                            d_wb.reshape(N_CHIPS, 128, D_MODEL), d_bm_rows], axis=1)
    if start_reduce is None:
        grads, chip_sums = [_d_w_in(d_proj, h).reshape(N_CHIPS, W_IN_SHARD, D_MODEL), rest], []
    else:
        d_wt, *theirs = _d_w_in(d_proj, h, rest.astype(BF16))
        grads = [d_wt.reshape(N_CHIPS, W_IN_SHARD, D_MODEL), rest]
        chip_sums = start_reduce(grads, theirs)
    grad_x, d_gain, others = _d_x(d_proj, w_t, x, norm_gain, dy, chip_sums)

    d_rel = jnp.concatenate(
        [_bias_grad(ds_a, A_HALF_WINDOW, 1)]
        + [_bias_grad(ds_b[4 * g:4 * g + 4], B_HALF_WINDOW, d) for g, d in enumerate(B_DILATIONS)], axis=1)
    d_sink = jnp.sum(dsink, axis=(2, 3)).reshape(1, 8)
    dgk_a_row = dgk_a[0]
    small = jnp.zeros((8, D_MODEL), F32)
    small = small.at[0].set(d_gain[0])
    small = small.at[1].set(d_rel.reshape(-1))
    misc = jnp.concatenate([_fold_heads(dgq_a), (dgk_a_row[:HEAD_DIM] + dgk_a_row[HEAD_DIM:]).reshape(1, HEAD_DIM),
                            _fold_heads(dgq_b), _fold_heads(dgk_b), d_sink], axis=1)
    small = small.at[2, :264].set(misc[0])

    return loss, grad_x, grads, small, chip_sums, others


def _unpack_weights(small_all):
    sm = small_all.reshape(N_CHIPS, SMALL_ROWS, D_MODEL)
    w_o = sm[:, 0:256].reshape(D_MODEL, D_MODEL)
    w_a = sm[:, 256:384].reshape(N_CHIPS, 512, 256).transpose(1, 0, 2).reshape(512, D_MODEL)
    w_b = sm[:, 384:512].reshape(N_CHIPS, 512, 256).transpose(1, 0, 2).reshape(512, D_MODEL)
    b_m = lax.bitcast_convert_type(sm[:, 512].reshape(N_CHIPS, 2, 256, 2), F32)
    return w_a, w_b, w_o, b_m.transpose(1, 0, 2).reshape(2, D_MODEL)


def _pack_small_weights(w_branch_a, w_branch_b, b_merge, w_out):
    b_m = jnp.pad(lax.bitcast_convert_type(b_merge, BF16).reshape(1, D_MODEL), ((0, SMALL_ROWS - 513), (0, 0)))
    return jnp.concatenate([w_out.astype(BF16), w_branch_a.astype(BF16).reshape(128, D_MODEL),
                            w_branch_b.astype(BF16).reshape(128, D_MODEL), b_m], axis=0)


def kernel(x, norm_gain, w_in, q_norm_a, k_norm_a, q_norm_b, k_norm_b, sink_a, rel_bias, w_branch_a, w_branch_b, b_merge, w_out, loss_target, m_norm_gain, m_w_in, m_q_norm_a, m_k_norm_a, m_q_norm_b, m_k_norm_b, m_sink_a, m_rel_bias, m_w_branch_a, m_w_branch_b, m_b_merge, m_w_out, v_norm_gain, v_w_in, v_q_norm_a, v_k_norm_a, v_q_norm_b, v_k_norm_b, v_sink_a, v_rel_bias, v_w_branch_a, v_w_branch_b, v_b_merge, v_w_out):
    w_in_t, m_w_in_t, v_w_in_t = (jnp.transpose(t[0]) for t in (w_in, m_w_in, v_w_in))
    wt_shard = _cast_rows(w_in_t, BF16, "w_in_cast")
    w_t = _gather_weights(wt_shard)
    small_shard = _pack_small_weights(w_branch_a[0], w_branch_b[0], b_merge[0], w_out[0])

    place = _my_place()
    names = ("w_in", "rest")

    def start_reduce(grads, theirs):
        return [_add_halves(place, g, t, "reduce_add_halves_" + n) for g, t, n in zip(grads, theirs, names)]

    loss_part, grad_x, _, small, chip_sums, others = _local_step(
        x[0], loss_target[0], norm_gain, w_t, None, None, None, None, q_norm_a, k_norm_a, q_norm_b, k_norm_b,
        sink_a, rel_bias, start_reduce, small_shard)

    (g_wt, g_rest), small = _join_halves(
        [_add_chips(place, q, o, "reduce_add_chips_" + n) for q, o, n in zip(chip_sums, others, names)],
        small.at[3, 0].set(loss_part))
    loss = small[3, 0]

    g_w_out = g_rest[0:256]
    g_w_a = g_rest[256:384].reshape(512, 256)
    g_w_b = g_rest[384:512].reshape(512, 256)
    g_b_merge = g_rest[512:514, :256]
    g_norm_gain = small[0:1]
    g_rel_bias = small[1].reshape(N_BUCKETS, N_BUCKETS)
    g_q_a, g_k_a, g_q_b, g_k_b = (small[2:3, 64 * k:64 * k + 64] for k in range(4))
    g_sink = small[2:3, 256:264]

    big_names = (("w_branch_a", w_branch_a, g_w_a, m_w_branch_a, v_w_branch_a),
                 ("w_branch_b", w_branch_b, g_w_b, m_w_branch_b, v_w_branch_b),
                 ("w_out", w_out, g_w_out, m_w_out, v_w_out))
    upd = {name: (g,) + tuple(_adamw(w[0], g, m[0], v[0], "adamw_" + name)) for name, w, g, m, v in big_names}
    upd["w_in"] = tuple(jnp.transpose(t) for t in (g_wt,) + tuple(_adamw(w_in_t, g_wt, m_w_in_t, v_w_in_t, "adamw_w_in")))
    small_names = ("norm_gain", "q_norm_a", "k_norm_a", "q_norm_b", "k_norm_b", "sink_a", "rel_bias", "b_merge")
    ws = [norm_gain, q_norm_a, k_norm_a, q_norm_b, k_norm_b, sink_a, rel_bias, b_merge[0]]
    gs = [g_norm_gain, g_q_a, g_k_a, g_q_b, g_k_b, g_sink, g_rel_bias, g_b_merge]
    ms = [m_norm_gain, m_q_norm_a, m_k_norm_a, m_q_norm_b, m_k_norm_b, m_sink_a, m_rel_bias, m_b_merge[0]]
    vs = [v_norm_gain, v_q_norm_a, v_k_norm_a, v_q_norm_b, v_k_norm_b, v_sink_a, v_rel_bias, v_b_merge[0]]
    ds, nms, nvs = _adamw_small(ws, gs, ms, vs)
    for k, name in enumerate(small_names):
        upd[name] = (gs[k], ds[k], nms[k], nvs[k])

    order = ("norm_gain", "w_in", "q_norm_a", "k_norm_a", "q_norm_b", "k_norm_b", "sink_a", "rel_bias",
             "w_branch_a", "w_branch_b", "b_merge", "w_out")
    lead = {"w_in", "w_branch_a", "w_branch_b", "b_merge", "w_out"}
    outs = [loss, grad_x[None]]
    for part in range(4):
        outs += [upd[name][part][None] if name in lead else upd[name][part] for name in order]
    return tuple(outs)
```

```python
import math

import numpy as np
import jax
import jax.numpy as jnp
from jax import lax
from jax.experimental import pallas as pl
from jax.experimental.pallas import tpu as pltpu

F32 = jnp.float32
BF16 = jnp.bfloat16

SEQ = 4096
D_MODEL = 1024
HEAD_DIM = 64
LANES = 128
EPS = 1e-6
NEG_INF = -1e30
SCALE = HEAD_DIM ** -0.5
N_BUCKETS = 32
MAX_DISTANCE = 1024
N_CHIPS = 4

A_HALF_WINDOW = 128
B_HALF_WINDOW = 64
B_DILATIONS = (1, 4, 16)
Q_BLOCK = 128

QKV_WIDTH = 5376
GATE_WIDTH = 3072
QA_BLK, KA_BLK, VA_BLK = 0, 4, 5
QB_BLK, KB_BLK, VB_BLK = 6, 18, 30
IN_WIDTH = QKV_WIDTH + GATE_WIDTH
W_IN_SHARD = IN_WIDTH // N_CHIPS

SMALL_ROWS = 544
REST_ROWS = 544

ADAM_LR = 0.001
ADAM_B1 = 0.9
ADAM_B2 = 0.999
ADAM_EPS = 1e-08
ADAM_WD = 0.01
ADAM_STEP = 10

VMEM_LIMIT = 56 * 1024 * 1024

NT = (((1,), (1,)), ((), ()))
TN = (((0,), (0,)), ((), ()))
MESH = pl.DeviceIdType.MESH
ANY = pl.BlockSpec(memory_space=pl.ANY)


def _dot(a, b, dims=None):
    if dims is None:
        return jnp.dot(a, b, preferred_element_type=F32)
    return lax.dot_general(a, b, dims, preferred_element_type=F32)


def _params(*semantics):
    return pltpu.CompilerParams(dimension_semantics=semantics or None, vmem_limit_bytes=VMEM_LIMIT)


def _line_width(half_window):
    return pl.cdiv(2 * Q_BLOCK + 2 * half_window - 1, LANES) * LANES


def _bucket_onehot(half_window, stride):
    rel = np.arange(_line_width(half_window)) - (Q_BLOCK - 1) - half_window
    band = np.abs(rel) <= half_window
    rel = rel * stride
    half, max_exact = N_BUCKETS // 2, N_BUCKETS // 4
    n = np.abs(rel)
    nf = np.maximum(n, max_exact).astype(np.float32)
    large = max_exact + (np.log(nf / np.float32(max_exact)) / np.float32(math.log(MAX_DISTANCE / max_exact))
                         * np.float32(half - max_exact)).astype(np.int32)
    large = np.minimum(large, half - 1)
    bucket = (rel > 0).astype(np.int32) * half + np.where(n < max_exact, n, large)
    onehot = (bucket[..., None] == np.arange(N_BUCKETS)) & band[..., None]
    return onehot.astype(np.float32), band


def _bias_lines(rel_bias_cols, half_window, stride):
    onehot, band = _bucket_onehot(half_window, stride)
    h = rel_bias_cols.shape[1]
    t = jnp.einsum("tb,bh->ht", jnp.asarray(onehot), rel_bias_cols, precision=lax.Precision.HIGHEST)
    t = t + jnp.asarray(np.where(band, 0.0, NEG_INF).astype(np.float32))
    return t.reshape(h // 2, 2, -1)


def _bias_grad(d_lines, half_window, stride):
    onehot, _ = _bucket_onehot(half_window, stride)
    h = d_lines.shape[0] * 2
    return jnp.einsum("tb,ht->bh", jnp.asarray(onehot), d_lines.reshape(h, -1), precision=lax.Precision.HIGHEST)


def _unroll_bias(line_ref, tile_ref, w):
    width = line_ref.shape[1]
    for j in range(2):
        rows = jnp.broadcast_to(line_ref[j:j + 1, :], (Q_BLOCK, width))
        rows = pltpu.roll(rows, width - (Q_BLOCK - 1), 1, stride=1, stride_axis=0)
        tile_ref[j * Q_BLOCK:(j + 1) * Q_BLOCK, :] = rows[:, :w]


def _fold_bias_grad(tile_ref, line_ref, w):
    width = line_ref.shape[1]
    row = lax.broadcasted_iota(jnp.int32, (Q_BLOCK, Q_BLOCK), 0)
    col = lax.broadcasted_iota(jnp.int32, (Q_BLOCK, Q_BLOCK), 1)
    flip = jnp.where(row + col == Q_BLOCK - 1, 1.0, 0.0).astype(BF16)
    for j in range(2):
        tile = tile_ref[j * Q_BLOCK:(j + 1) * Q_BLOCK, :]
        hi = tile.astype(BF16)
        lo = (tile - hi.astype(F32)).astype(BF16)
        rows = _dot(flip, hi) + _dot(flip, lo)
        rows = jnp.concatenate([rows, jnp.zeros((Q_BLOCK, width - w), F32)], axis=1)
        rows = pltpu.roll(rows, 0, 1, stride=1, stride_axis=0)
        line_ref[j:j + 1, :] = jnp.sum(rows, axis=0, keepdims=True)


def _row_tile(rows):
    return max(t for t in range(16, 385, 16) if rows % t == 0)


def _cast_rows(w, out_dtype, name):
    r, c = w.shape
    tr = _row_tile(r)

    def body(w_ref, o_ref):
        o_ref[...] = w_ref[...].astype(out_dtype)

    spec = pl.BlockSpec((tr, c), lambda i: (i, 0))
    return pl.pallas_call(
        body, name=name, grid=(r // tr,), in_specs=[spec], out_specs=spec,
        out_shape=jax.ShapeDtypeStruct((r, c), out_dtype), compiler_params=_params("arbitrary"),
    )(w)


STAGE_ROWS = 528


def _gather_scratch():
    return [pltpu.SemaphoreType.DMA((12,)), pltpu.SemaphoreType.DMA((12,)), pltpu.SemaphoreType.DMA((2,)),
            pltpu.SemaphoreType.DMA((2,)), pltpu.VMEM((2, STAGE_ROWS, D_MODEL), BF16)]


def _gather_phases(src_ref, out_ref, send_sems, recv_sems, in_sems, out_sems, stage):
    rows = src_ref.shape[0]
    x, y, c = lax.axis_index("x"), lax.axis_index("y"), lax.axis_index("c")
    sibling = (x, y, 1 - c)
    near = (x + (1 - c) - 2 * x * (1 - c), y + c - 2 * y * c)
    far = (x + c - 2 * x * c, y + (1 - c) - 2 * y * (1 - c))
    diag = (1 - x, 1 - y)
    chip_no = lambda chip: 2 * chip[0] + chip[1]
    my_chip = chip_no((x, y))

    pieces = 2 if (rows // 2) % 32 == 0 else 1
    n = rows // 2 // pieces

    def half_of(chip, half, p):
        start = pl.multiple_of(chip * rows + half * (rows // 2) + p * n, 16)
        return out_ref.at[pl.ds(start, n), :]

    def copy(k, p, src, dst, to):
        return pltpu.make_async_remote_copy(src_ref=src, dst_ref=dst, send_sem=send_sems.at[k * pieces + p],
                                            recv_sem=recv_sems.at[k * pieces + p], device_id=to, device_id_type=MESH)

    def mine(p):
        return src_ref.at[pl.ds(pl.multiple_of(c * (rows // 2) + p * n, 16), n), :]

    def keep_own():
        outs = []
        for i, r0 in enumerate(range(0, rows, STAGE_ROWS)):
            n = min(STAGE_ROWS, rows - r0)
            slot = i % 2
            if i >= 2:
                outs[i - 2].wait()
            buf = stage.at[slot, pl.ds(0, n), :]
            load = pltpu.make_async_copy(src_ref.at[pl.ds(r0, n), :], buf, in_sems.at[slot])
            load.start()
            load.wait()
            start = pl.multiple_of(my_chip * rows + r0, 16)
            outs.append(pltpu.make_async_copy(buf, out_ref.at[pl.ds(start, n), :], out_sems.at[slot]))
            outs[i].start()
        for cp in outs[-2:]:
            cp.wait()

    def start():
        for p in range(pieces):
            copy(0, p, mine(p), half_of(my_chip, c, p), (*near, c)).start()
            copy(1, p, mine(p), half_of(my_chip, c, p), (*far, c)).start()
        keep_own()

    def pass_on(j, p, chip):
        landed = half_of(chip_no(chip), c, p)
        copy(3 + j, p, landed, landed, sibling).start()

    def relay():
        for p in range(pieces):
            landed = half_of(chip_no(near), c, p)
            copy(0, p, landed, landed, sibling).wait_recv()
            copy(2, p, landed, landed, (*far, c)).start()
            pass_on(0, p, near)

    def forward():
        for j, chip in ((1, far), (2, diag)):
            for p in range(pieces):
                landed = half_of(chip_no(chip), c, p)
                copy(j, p, landed, landed, sibling).wait_recv()
                pass_on(j, p, chip)

    def finish():
        for j, chip in ((0, far), (1, near), (2, diag)):
            for p in range(pieces):
                other = half_of(chip_no(chip), 1 - c, p)
                copy(3 + j, p, other, other, sibling).wait_recv()
        for k in range(6):
            for p in range(pieces):
                copy(k, p, mine(p), mine(p), sibling).wait_send()

    return start, relay, forward, finish


def _gather_weights(shard):
    def body(src_ref, out_ref, *scratch):
        for phase in _gather_phases(src_ref, out_ref, *scratch):
            phase()

    return pl.pallas_call(
        body, name="gather_weights", in_specs=[ANY], out_specs=ANY,
        out_shape=jax.ShapeDtypeStruct((N_CHIPS * shard.shape[0], D_MODEL), BF16),
        scratch_shapes=_gather_scratch(),
    )(shard)


W_BLOCK = 768


def _w_blocks(first, count):
    return [pl.BlockSpec((W_BLOCK, D_MODEL), lambda *_, k=k: (first + k, 0)) for k in range(count)]


def _in_proj(x, gain, w_t, first_block, n_blocks, out_dtype, name, keep_h, ride=None):
    tm = 512
    n_steps = SEQ // tm
    n_out = 2 if keep_h else 1

    def body(x_ref, g_ref, *refs):
        w_refs, outs = refs[:n_blocks], refs[n_blocks + (ride is not None):n_blocks + (ride is not None) + n_out]
        if ride is not None:
            phases = _gather_phases(refs[n_blocks], *refs[n_blocks + 1 + n_out:])
            for step, phase in zip((0, 2, 4, n_steps - 1), phases):
                pl.when(pl.program_id(0) == step)(phase)
        xf = x_ref[...]
        r = lax.rsqrt(jnp.mean(xf * xf, axis=-1, keepdims=True) + EPS)
        h = ((xf * r) * g_ref[...]).astype(BF16)
        if keep_h:
            outs[1][...] = h
        for k, w_ref in enumerate(w_refs):
            outs[0][:, k * W_BLOCK:(k + 1) * W_BLOCK] = _dot(h, w_ref[...], NT).astype(out_dtype)

    riding = [] if ride is None else [ride]
    return pl.pallas_call(
        body, name=name, grid=(n_steps,),
        in_specs=[pl.BlockSpec((tm, D_MODEL), lambda i: (i, 0)), pl.BlockSpec((1, D_MODEL), lambda i: (0, 0))]
        + _w_blocks(first_block, n_blocks) + [ANY for _ in riding],
        out_specs=[pl.BlockSpec((tm, W_BLOCK * n_blocks), lambda i: (i, 0)),
                   pl.BlockSpec((tm, D_MODEL), lambda i: (i, 0))][:n_out] + [ANY for _ in riding],
        out_shape=[jax.ShapeDtypeStruct((SEQ, W_BLOCK * n_blocks), out_dtype),
                   jax.ShapeDtypeStruct((SEQ, D_MODEL), BF16)][:n_out]
        + [jax.ShapeDtypeStruct((N_CHIPS * r.shape[0], D_MODEL), BF16) for r in riding],
        scratch_shapes=_gather_scratch() if riding else [],
        compiler_params=_params("arbitrary"),
    )(x, gain, *([w_t] * n_blocks), *riding)


CHUNK = 256
CHUNK_UNROLL = 4
TILE_UNROLL = 8


def _low_half():
    return lax.broadcasted_iota(jnp.int32, (1, LANES), 1) < HEAD_DIM


def _half_sum(v, low):
    del low
    row = lax.broadcasted_iota(jnp.int32, (2 * LANES, LANES), 0)
    col = lax.broadcasted_iota(jnp.int32, (2 * LANES, LANES), 1)
    ones = jnp.where((row % LANES) // HEAD_DIM == col // HEAD_DIM, 1.0, 0.0).astype(BF16)
    hi = v.astype(BF16)
    lo = (v - hi.astype(F32)).astype(BF16)
    return _dot(jnp.concatenate([hi, lo], axis=1), ones)


def _chunks(fn, init=0):
    def body(i, carry):
        for u in range(CHUNK_UNROLL):
            carry = fn(pl.multiple_of((i * CHUNK_UNROLL + u) * CHUNK, CHUNK), carry)
        return carry

    return lax.fori_loop(0, SEQ // (CHUNK * CHUNK_UNROLL), body, init)


def _inv_rms(t, low):
    del low
    row = lax.broadcasted_iota(jnp.int32, (LANES, LANES), 0)
    col = lax.broadcasted_iota(jnp.int32, (LANES, LANES), 1)
    ones = jnp.where(row // HEAD_DIM == col // HEAD_DIM, 1.0, 0.0).astype(BF16)
    return lax.rsqrt(_dot((t * t).astype(BF16), ones) * (1.0 / HEAD_DIM) + EPS)


def _prep_q(q_ref, gain_ref, qn_ref):
    low = _low_half()

    def step(r0, carry):
        q = q_ref[pl.ds(r0, CHUNK), :].astype(F32)
        qn_ref[pl.ds(r0, CHUNK), :] = ((q * _inv_rms(q, low)) * gain_ref[...]) * SCALE
        return carry

    _chunks(step)


def _own_half(t, keep):
    return jnp.where(keep, t, pltpu.roll(t, HEAD_DIM, 1))


def _prep_kv(k_ref, v_ref, gain_ref, kp_ref, vp_ref, pad, keep=None):
    low = _low_half()
    zeros = jnp.zeros((pad, LANES), F32)
    for ref in (kp_ref, vp_ref):
        ref[pl.ds(0, pad), :] = zeros
        ref[pl.ds(pad + SEQ, pad), :] = zeros

    def step(r0, carry):
        k = k_ref[pl.ds(r0, CHUNK), :].astype(F32)
        v = v_ref[pl.ds(r0, CHUNK), :].astype(F32)
        kn = (k * _inv_rms(k, low)) * gain_ref[...]
        if keep is not None:
            kn, v = _own_half(kn, keep), _own_half(v, keep)
        kp_ref[pl.ds(pad + r0, CHUNK), :] = kn
        vp_ref[pl.ds(pad + r0, CHUNK), :] = v
        return carry

    _chunks(step)


def _tiles(d, half_window, fn):
    w = Q_BLOCK + 2 * half_window
    length = SEQ // d
    n_blocks = length // Q_BLOCK
    col = lax.broadcasted_iota(jnp.int32, (1, w), 1)

    def step(it, carry):
        c, n = it // n_blocks, it % n_blocks
        start = c + (d * Q_BLOCK) * n
        if d == 1:
            start = pl.multiple_of(start, Q_BLOCK)
            q_rows, k_rows = pl.ds(start, Q_BLOCK), pl.ds(start, w)
        else:
            q_rows, k_rows = pl.ds(start, Q_BLOCK, stride=d), pl.ds(start, w, stride=d)
        t = n * Q_BLOCK - half_window + col
        edge = jnp.where((t < 0) | (t >= length), NEG_INF, 0.0)
        fn(q_rows, k_rows, edge)
        return carry

    lax.fori_loop(0, d * n_blocks, step, 0, unroll=TILE_UNROLL)


def _stack_heads(t, low):
    return jnp.concatenate([jnp.where(low, t, 0.0), jnp.where(low, 0.0, t)], axis=0).astype(BF16)


def _unstack_heads(t, low):
    return jnp.where(low, t[:Q_BLOCK], t[Q_BLOCK:])


def _per_head(pair):
    return jnp.concatenate([jnp.full((Q_BLOCK, 1), pair[0], F32), jnp.full((Q_BLOCK, 1), pair[1], F32)], axis=0)


def _fwd_tiles(qn_ref, kp_ref, vp_ref, bias_ref, emit, *, d, half_window, sinks=None):
    low = _low_half()
    w = Q_BLOCK + 2 * half_window
    sink = None if sinks is None else _per_head(sinks)

    def tile(q_rows, k_rows, edge):
        q2 = _stack_heads(qn_ref[q_rows, :], low)
        k = kp_ref[k_rows, :].astype(BF16)
        v1 = jnp.concatenate([vp_ref[k_rows, :], jnp.ones((w, LANES), F32)], axis=1).astype(BF16)
        s = _dot(q2, k, NT) + bias_ref[...] + edge
        m = jnp.max(s, axis=-1, keepdims=True)
        if sink is not None:
            m = jnp.maximum(m, sink)
        o = _dot(jnp.exp(s - m).astype(BF16), v1)
        l = o[:, LANES:]
        if sink is not None:
            l = l + jnp.exp(sink - m)
        emit(q_rows, _unstack_heads(o[:, :LANES] * (1.0 / l), low), _unstack_heads(m + jnp.log(l), low))

    _tiles(d, half_window, tile)


def _bwd_tiles(qn_ref, kp_ref, vp_ref, bias_ref, do_ref, lse_ref, delta_ref, dq_ref, dk_ref, dv_ref, ds_ref,
               *, d, half_window, sinks=None, dsink_ref=None):
    low = _low_half()
    w = Q_BLOCK + 2 * half_window
    sink = None if sinks is None else _per_head(sinks)

    def rows_of(t):
        return jnp.concatenate([t[:, 0:1], t[:, HEAD_DIM:HEAD_DIM + 1]], axis=0)

    def tile(q_rows, k_rows, edge):
        q2 = _stack_heads(qn_ref[q_rows, :], low)
        do2 = _stack_heads(do_ref[q_rows, :], low)
        k = kp_ref[k_rows, :].astype(BF16)
        v = vp_ref[k_rows, :].astype(BF16)
        lse = rows_of(lse_ref[q_rows, :])
        delta = rows_of(delta_ref[q_rows, :])
        p = jnp.exp(_dot(q2, k, NT) + bias_ref[...] + edge - lse)
        ds = p * (_dot(do2, v, NT) - delta)
        ds_ref[...] += ds
        if sink is not None:
            dsink_ref[...] += (-jnp.exp(sink - lse) * delta).reshape(2, Q_BLOCK, 1)
        dsb, pb = ds.astype(BF16), p.astype(BF16)
        dq_ref[q_rows, :] = _unstack_heads(_dot(dsb, k), low)
        dk_ref[k_rows, :] += _dot(dsb, q2, TN)
        dv_ref[k_rows, :] += _dot(pb, do2, TN)

    _tiles(d, half_window, tile)


def _norm_bwd(raw_ref, gain_ref, dn_ref, dn_offset, out_ref, scale):
    low = _low_half()

    def step(r0, dgain):
        t = raw_ref[pl.ds(r0, CHUNK), :].astype(F32)
        dn = dn_ref[pl.ds(dn_offset + r0, CHUNK), :]
        dth = dn * (gain_ref[...] * scale)
        sums = _half_sum(jnp.concatenate([t * t, dth * t], axis=0), low)
        r = lax.rsqrt(sums[:CHUNK] * (1.0 / HEAD_DIM) + EPS)
        th = t * r
        out_ref[pl.ds(r0, CHUNK), :] = (r * (dth - th * (r * sums[CHUNK:] * (1.0 / HEAD_DIM)))).astype(BF16)
        return dgain + jnp.sum(dn * th, axis=0, keepdims=True) * scale

    return _chunks(step, jnp.zeros((1, LANES), F32))


def _rows8(v):
    return jnp.broadcast_to(v, (8, v.shape[-1]))


A_W = Q_BLOCK + 2 * A_HALF_WINDOW
A_PAD = A_HALF_WINDOW


def _seq_block(col_fn):
    return pl.BlockSpec((SEQ, LANES), col_fn)


def _attn_a_fwd(qkv, gain_q, gain_k, bias, sink):
    def body(sink_ref, q_ref, k_ref, v_ref, gq_ref, gk_ref, line_ref, o_ref, lse_ref, qn_ref, kp_ref, vp_ref,
             bias_ref):
        hp = pl.program_id(0)
        keep = (lax.broadcasted_iota(jnp.int32, (1, LANES), 1) // HEAD_DIM) == hp // 2
        _prep_q(q_ref, gq_ref, qn_ref)
        _prep_kv(k_ref, v_ref, gk_ref, kp_ref, vp_ref, A_PAD, keep)
        _unroll_bias(line_ref, bias_ref, A_W)

        def emit(rows, out, lse):
            o_ref[rows, :] = out
            lse_ref[rows, :] = lse

        _fwd_tiles(qn_ref, kp_ref, vp_ref, bias_ref, emit, d=1, half_window=A_HALF_WINDOW,
                   sinks=(sink_ref[2 * hp], sink_ref[2 * hp + 1]))

    vec = pl.BlockSpec((1, LANES), lambda hp, s: (0, 0))
    return pl.pallas_call(
        body, name="attn_a_fwd",
        grid_spec=pltpu.PrefetchScalarGridSpec(
            num_scalar_prefetch=1, grid=(4,),
            in_specs=[_seq_block(lambda hp, s: (0, QA_BLK + hp)), _seq_block(lambda hp, s: (0, KA_BLK)),
                      _seq_block(lambda hp, s: (0, VA_BLK)), vec, vec,
                      pl.BlockSpec((None, 2, _line_width(A_HALF_WINDOW)), lambda hp, s: (hp, 0, 0))],
            out_specs=[_seq_block(lambda hp, s: (0, hp)), _seq_block(lambda hp, s: (0, hp))],
            scratch_shapes=[pltpu.VMEM((SEQ, LANES), F32), pltpu.VMEM((SEQ + 2 * A_PAD, LANES), F32),
                            pltpu.VMEM((SEQ + 2 * A_PAD, LANES), F32), pltpu.VMEM((2 * Q_BLOCK, A_W), F32)]),
        out_shape=[jax.ShapeDtypeStruct((SEQ, 512), F32)] * 2,
        compiler_params=_params("arbitrary"),
    )(sink.reshape(8), qkv, qkv, qkv, gain_q, gain_k, bias)


def _attn_a_bwd(qkv, gain_q, gain_k, bias, sink, delta, lse, d_out):
    def body(sink_ref, q_ref, k_ref, v_ref, gq_ref, gk_ref, line_ref, delta_ref, lse_ref, do_ref,
             dq_out, dkv_out, dgq_out, dgk_out, dline_out, dsink_out,
             qn_ref, kp_ref, vp_ref, dq_ref, dk_ref, dv_ref, dk_tot, dv_tot, bias_ref, ds_out):
        hp = pl.program_id(0)
        kv_head = hp // 2
        keep = (lax.broadcasted_iota(jnp.int32, (1, LANES), 1) // HEAD_DIM) == kv_head
        _prep_q(q_ref, gq_ref, qn_ref)
        _prep_kv(k_ref, v_ref, gk_ref, kp_ref, vp_ref, A_PAD, keep)
        _unroll_bias(line_ref, bias_ref, A_W)
        dk_ref[...] = jnp.zeros_like(dk_ref)
        dv_ref[...] = jnp.zeros_like(dv_ref)
        ds_out[...] = jnp.zeros_like(ds_out)
        dsink_out[...] = jnp.zeros_like(dsink_out)

        @pl.when(hp == 0)
        def _():
            dk_tot[...] = jnp.zeros_like(dk_tot)
            dv_tot[...] = jnp.zeros_like(dv_tot)

        _bwd_tiles(qn_ref, kp_ref, vp_ref, bias_ref, do_ref, lse_ref, delta_ref, dq_ref, dk_ref, dv_ref, ds_out,
                   d=1, half_window=A_HALF_WINDOW, sinks=(sink_ref[2 * hp], sink_ref[2 * hp + 1]),
                   dsink_ref=dsink_out)
        _fold_bias_grad(ds_out, dline_out, A_W)
        dgq_out[...] = _rows8(_norm_bwd(q_ref, gq_ref, dq_ref, 0, dq_out, SCALE))

        def fold(r0, carry):
            rows = pl.ds(A_PAD + r0, CHUNK)
            for acc, tot in ((dk_ref, dk_tot), (dv_ref, dv_tot)):
                t = acc[rows, :]
                tot[pl.ds(r0, CHUNK), :] += jnp.where(keep, t + pltpu.roll(t, HEAD_DIM, 1), 0.0)
            return carry

        _chunks(fold)

        @pl.when(hp == 3)
        def _():
            dgk_out[...] = _rows8(_norm_bwd(k_ref, gk_ref, dk_tot, 0, dkv_out.at[0], 1.0))
            dkv_out[1] = dv_tot[...].astype(BF16)

    vec = pl.BlockSpec((1, LANES), lambda hp, s: (0, 0))
    seq_f32 = pltpu.VMEM((SEQ, LANES), F32)
    padded = pltpu.VMEM((SEQ + 2 * A_PAD, LANES), F32)
    return pl.pallas_call(
        body, name="attn_a_bwd",
        grid_spec=pltpu.PrefetchScalarGridSpec(
            num_scalar_prefetch=1, grid=(4,),
            in_specs=[_seq_block(lambda hp, s: (0, QA_BLK + hp)), _seq_block(lambda hp, s: (0, KA_BLK)),
                      _seq_block(lambda hp, s: (0, VA_BLK)), vec, vec,
                      pl.BlockSpec((None, 2, _line_width(A_HALF_WINDOW)), lambda hp, s: (hp, 0, 0)),
                      _seq_block(lambda hp, s: (0, hp)), _seq_block(lambda hp, s: (0, hp)),
                      _seq_block(lambda hp, s: (0, hp))],
            out_specs=[pl.BlockSpec((None, SEQ, LANES), lambda hp, s: (hp, 0, 0)),
                       pl.BlockSpec((2, SEQ, LANES), lambda hp, s: (0, 0, 0)),
                       pl.BlockSpec((None, 8, LANES), lambda hp, s: (hp, 0, 0)),
                       pl.BlockSpec((8, LANES), lambda hp, s: (0, 0)),
                       pl.BlockSpec((None, 2, _line_width(A_HALF_WINDOW)), lambda hp, s: (hp, 0, 0)),
                       pl.BlockSpec((None, 2, Q_BLOCK, 1), lambda hp, s: (hp, 0, 0, 0))],
            scratch_shapes=[seq_f32, padded, padded, seq_f32, padded, padded, seq_f32, seq_f32,
                            pltpu.VMEM((2 * Q_BLOCK, A_W), F32), pltpu.VMEM((2 * Q_BLOCK, A_W), F32)]),
        out_shape=[jax.ShapeDtypeStruct((4, SEQ, LANES), BF16), jax.ShapeDtypeStruct((2, SEQ, LANES), BF16),
                   jax.ShapeDtypeStruct((4, 8, LANES), F32), jax.ShapeDtypeStruct((8, LANES), F32),
                   jax.ShapeDtypeStruct((4, 2, _line_width(A_HALF_WINDOW)), F32),
                   jax.ShapeDtypeStruct((4, 2, Q_BLOCK, 1), F32)],
        compiler_params=_params("arbitrary"),
    )(sink.reshape(8), qkv, qkv, qkv, gain_q, gain_k, bias, delta, lse, d_out)


B_W = Q_BLOCK + 2 * B_HALF_WINDOW
B_PAD_MAX = B_HALF_WINDOW * B_DILATIONS[-1]


def _attn_b_fwd(qkv, gain_q, gain_k, bias):
    def body(q_ref, k_ref, v_ref, gq_ref, gk_ref, line_ref, o_ref, lse_ref, qn_ref, kp_ref, vp_ref, bias_ref):
        g = pl.program_id(1)
        _prep_q(q_ref, gq_ref, qn_ref)
        _unroll_bias(line_ref, bias_ref, B_W)

        def first(rows, out, lse):
            o_ref[rows, :] = out
            lse_ref[rows, :] = lse

        def combine(rows, out, lse):
            old = lse_ref[rows, :]
            new = jnp.maximum(old, lse) + jnp.log(1.0 + jnp.exp(-jnp.abs(old - lse)))
            o_ref[rows, :] = o_ref[rows, :] * jnp.exp(old - new) + out * jnp.exp(lse - new)
            lse_ref[rows, :] = new

        for gi, d in enumerate(B_DILATIONS):
            @pl.when(g == gi)
            def _():
                _prep_kv(k_ref, v_ref, gk_ref, kp_ref, vp_ref, B_HALF_WINDOW * d)
                _fwd_tiles(qn_ref, kp_ref, vp_ref, bias_ref, first if gi == 0 else combine,
                           d=d, half_window=B_HALF_WINDOW)

    vec = pl.BlockSpec((1, LANES), lambda hp, g: (0, 0))
    padded = pltpu.VMEM((SEQ + 2 * B_PAD_MAX, LANES), F32)
    return pl.pallas_call(
        body, name="attn_b_fwd", grid=(4, 3),
        in_specs=[_seq_block(lambda hp, g: (0, QB_BLK + 4 * g + hp)), _seq_block(lambda hp, g: (0, KB_BLK + 4 * g + hp)),
                  _seq_block(lambda hp, g: (0, VB_BLK + 4 * g + hp)), vec, vec,
                  pl.BlockSpec((None, 2, _line_width(B_HALF_WINDOW)), lambda hp, g: (4 * g + hp, 0, 0))],
        out_specs=[_seq_block(lambda hp, g: (0, hp)), _seq_block(lambda hp, g: (0, hp))],
        out_shape=[jax.ShapeDtypeStruct((SEQ, 512), F32)] * 2,
        scratch_shapes=[pltpu.VMEM((SEQ, LANES), F32), padded, padded, pltpu.VMEM((2 * Q_BLOCK, B_W), F32)],
        compiler_params=_params("arbitrary", "arbitrary"),
    )(qkv, qkv, qkv, gain_q, gain_k, bias)


def _attn_b_bwd(qkv, gain_q, gain_k, bias, delta, lse, d_out):
    def body(q_ref, k_ref, v_ref, gq_ref, gk_ref, line_ref, delta_ref, lse_ref, do_ref,
             dq_out, dk_out, dv_out, dgq_out, dgk_out, dline_out,
             qn_ref, kp_ref, vp_ref, dq_ref, dk_ref, dv_ref, bias_ref, ds_out):
        g = pl.program_id(1)
        _prep_q(q_ref, gq_ref, qn_ref)
        _unroll_bias(line_ref, bias_ref, B_W)
        ds_out[...] = jnp.zeros_like(ds_out)
        for gi, d in enumerate(B_DILATIONS):
            @pl.when(g == gi)
            def _():
                pad = B_HALF_WINDOW * d
                for acc in (dk_ref, dv_ref):
                    acc[pl.ds(0, SEQ + 2 * pad), :] = jnp.zeros((SEQ + 2 * pad, LANES), F32)
                _prep_kv(k_ref, v_ref, gk_ref, kp_ref, vp_ref, pad)
                _bwd_tiles(qn_ref, kp_ref, vp_ref, bias_ref, do_ref, lse_ref, delta_ref, dq_ref, dk_ref, dv_ref,
                           ds_out, d=d, half_window=B_HALF_WINDOW)
                dgk_out[...] = _rows8(_norm_bwd(k_ref, gk_ref, dk_ref, pad, dk_out, 1.0))
                dv_out[...] = dv_ref[pl.ds(pad, SEQ), :].astype(BF16)
        _fold_bias_grad(ds_out, dline_out, B_W)
        dgq_out[...] = _rows8(_norm_bwd(q_ref, gq_ref, dq_ref, 0, dq_out, SCALE))

    vec = pl.BlockSpec((1, LANES), lambda hp, g: (0, 0))
    seq_f32 = pltpu.VMEM((SEQ, LANES), F32)
    padded = pltpu.VMEM((SEQ + 2 * B_PAD_MAX, LANES), F32)
    part = pl.BlockSpec((None, 8, LANES), lambda hp, g: (4 * g + hp, 0, 0))
    line = pl.BlockSpec((None, 2, _line_width(B_HALF_WINDOW)), lambda hp, g: (4 * g + hp, 0, 0))
    return pl.pallas_call(
        body, name="attn_b_bwd", grid=(4, 3),
        in_specs=[_seq_block(lambda hp, g: (0, QB_BLK + 4 * g + hp)), _seq_block(lambda hp, g: (0, KB_BLK + 4 * g + hp)),
                  _seq_block(lambda hp, g: (0, VB_BLK + 4 * g + hp)), vec, vec,
                  line,
                  _seq_block(lambda hp, g: (0, hp)), _seq_block(lambda hp, g: (0, hp)), _seq_block(lambda hp, g: (0, hp))],
        out_specs=[pl.BlockSpec((None, SEQ, LANES), lambda hp, g: (4 * g + hp, 0, 0))] * 3 + [part, part, line],
        out_shape=[jax.ShapeDtypeStruct((12, SEQ, LANES), BF16)] * 3
        + [jax.ShapeDtypeStruct((12, 8, LANES), F32)] * 2
        + [jax.ShapeDtypeStruct((12, 2, _line_width(B_HALF_WINDOW)), F32)],
        scratch_shapes=[seq_f32, padded, padded, seq_f32, padded, padded,
                        pltpu.VMEM((2 * Q_BLOCK, B_W), F32), pltpu.VMEM((2 * Q_BLOCK, B_W), F32)],
        compiler_params=_params("arbitrary", "arbitrary"),
    )(qkv, qkv, qkv, gain_q, gain_k, bias, delta, lse, d_out)


def _sigmoid(t):
    return 1.0 / (1.0 + jnp.exp(-t))


def _middle(out_a, out_b, gates, x, target, w_a, w_b, w_out, b_merge):
    tm = 256
    n_steps = SEQ // tm

    def body(oa_ref, ob_ref, g_ref, x_ref, t_ref, wa_ref, wb_ref, wo_ref, bm_ref,
             dy_ref, dg_ref, doa_ref, dob_ref, dla_ref, dlb_ref, dwa_ref, dwb_ref, dwo_ref, dbm_ref, sq_ref):
        @pl.when(pl.program_id(0) == 0)
        def _():
            for ref in (dwa_ref, dwb_ref, dwo_ref, dbm_ref, sq_ref):
                ref[...] = jnp.zeros_like(ref)

        gate_a, gate_b = g_ref[:, 0:512], g_ref[:, 512:1024]
        sig_a, sig_b = _sigmoid(gate_a), _sigmoid(gate_b)
        silu_a, silu_b = gate_a * sig_a, gate_b * sig_b
        oa, ob = oa_ref[...], ob_ref[...]
        ya, yb = (oa * silu_a).astype(BF16), (ob * silu_b).astype(BF16)
        br_a, br_b = _dot(ya, wa_ref[...]), _dot(yb, wb_ref[...])
        m0 = _sigmoid(g_ref[:, 1024:2048] + bm_ref[0:1, :])
        m1 = _sigmoid(g_ref[:, 2048:3072] + bm_ref[1:2, :])
        merged = (m0 * br_a + m1 * br_b).astype(BF16)
        err = (x_ref[...] + _dot(merged, wo_ref[...])) - t_ref[...]
        sq_ref[...] += jnp.sum(err * err, axis=0, keepdims=True)

        dy = err * (1.0 / D_MODEL)
        dy_ref[...] = dy
        dyb = dy.astype(BF16)
        dmerged = _dot(dyb, wo_ref[...], NT)
        dwo_ref[...] += _dot(merged, dyb, TN)
        dbr_a, dbr_b = (dmerged * m0).astype(BF16), (dmerged * m1).astype(BF16)
        dm0 = (dmerged * br_a) * (m0 * (1.0 - m0))
        dm1 = (dmerged * br_b) * (m1 * (1.0 - m1))
        dbm_ref[0:1, :] += jnp.sum(dm0, axis=0, keepdims=True)
        dbm_ref[1:2, :] += jnp.sum(dm1, axis=0, keepdims=True)
        for s in range(N_CHIPS):
            cols = slice(256 * s, 256 * (s + 1))
            dwa_ref[s] += _dot(ya, dbr_a[:, cols], TN)
            dwb_ref[s] += _dot(yb, dbr_b[:, cols], TN)
        dya, dyb_ = _dot(dbr_a, wa_ref[...], NT), _dot(dbr_b, wb_ref[...], NT)
        doa, dob = dya * silu_a, dyb_ * silu_b
        doa_ref[...] = doa
        dob_ref[...] = dob
        for blk in range(512 // LANES):
            lanes = slice(blk * LANES, (blk + 1) * LANES)
            dla_ref[:, lanes] = _half_sum(doa[:, lanes] * oa[:, lanes], None)
            dlb_ref[:, lanes] = _half_sum(dob[:, lanes] * ob[:, lanes], None)
        d_gates = (((dya * oa) * (sig_a * (1.0 + gate_a * (1.0 - sig_a)))).astype(BF16),
                   ((dyb_ * ob) * (sig_b * (1.0 + gate_b * (1.0 - sig_b)))).astype(BF16),
                   dm0.astype(BF16), dm1.astype(BF16))
        blk = 0
        for part in d_gates:
            for c0 in range(0, part.shape[1], 256):
                dg_ref[blk] = part[:, c0:c0 + 256]
                blk += 1

    def rows(width):
        return pl.BlockSpec((tm, width), lambda i: (i, 0))

    def whole(*shape):
        return pl.BlockSpec(shape, lambda i: (0,) * len(shape))

    return pl.pallas_call(
        body, name="middle", grid=(n_steps,),
        in_specs=[rows(512), rows(512), rows(GATE_WIDTH), rows(D_MODEL), rows(D_MODEL),
                  whole(512, D_MODEL), whole(512, D_MODEL), whole(D_MODEL, D_MODEL), whole(2, D_MODEL)],
        out_specs=[rows(D_MODEL), pl.BlockSpec((GATE_WIDTH // 256, tm, 256), lambda i: (0, i, 0)),
                   rows(512), rows(512), rows(512), rows(512),
                   whole(N_CHIPS, 512, 256), whole(N_CHIPS, 512, 256), whole(D_MODEL, D_MODEL),
                   whole(2, D_MODEL), whole(1, D_MODEL)],
        out_shape=[jax.ShapeDtypeStruct((SEQ, D_MODEL), F32), jax.ShapeDtypeStruct((GATE_WIDTH // 256, SEQ, 256), BF16),
                   jax.ShapeDtypeStruct((SEQ, 512), F32), jax.ShapeDtypeStruct((SEQ, 512), F32),
                   jax.ShapeDtypeStruct((SEQ, 512), F32), jax.ShapeDtypeStruct((SEQ, 512), F32),
                   jax.ShapeDtypeStruct((N_CHIPS, 512, 256), F32), jax.ShapeDtypeStruct((N_CHIPS, 512, 256), F32),
                   jax.ShapeDtypeStruct((D_MODEL, D_MODEL), F32), jax.ShapeDtypeStruct((2, D_MODEL), F32),
                   jax.ShapeDtypeStruct((1, D_MODEL), F32)],
        compiler_params=_params("arbitrary"),
    )(out_a, out_b, gates, x, target, w_a, w_b, w_out, b_merge)


def _which(j, edges, fns):
    lo = 0
    for hi, fn in zip(edges, fns):
        pl.when((j >= lo) & (j < hi))(fn)
        lo = hi


def _sibling_rows(tile, core):
    lo, hi = tile * W_BLOCK, (tile + 1) * W_BLOCK
    for chip in range(N_CHIPS):
        a = chip * W_IN_SHARD + (1 - core) * (W_IN_SHARD // 2)
        first, last = max(lo, a), min(hi, a + W_IN_SHARD // 2)
        if first < last:
            return chip, first - a, first - lo, last - first
    return None


def _d_w_in(d_proj, h, rest=None):
    plan, step, width = [], 0, 0
    for p in d_proj:
        total = p.shape[0] * p.shape[2]
        if width + total <= W_BLOCK:
            plan.append((p.shape[0], step, 1))
            width += total
            if width == W_BLOCK:
                step, width = step + 1, 0
        else:
            assert width == 0 and total % W_BLOCK == 0
            plan.append((W_BLOCK // p.shape[2], step, total // W_BLOCK))
            step += total // W_BLOCK
    assert width == 0 and step == IN_WIDTH // W_BLOCK
    firsts = sorted({first for _, first, _ in plan})
    edges = firsts[1:] + [step]
    halves = 2

    hand_over = rest is not None
    half = W_IN_SHARD // 2

    def body(*refs):
        if hand_over:
            pieces, h_ref, rest_ref = refs[:len(d_proj)], refs[len(d_proj)], refs[len(d_proj) + 1]
            o_ref, got_ref, got_rest_ref, acc_ref, send_sems, recv_sems, stage = refs[len(d_proj) + 2:]
        else:
            pieces, h_ref, o_ref, acc_ref = refs[:-3], refs[-3], refs[-2], refs[-1]
        k = pl.program_id(1)

        def emit(group):
            def fn():
                cols = jnp.concatenate([ref[b] for ref in group for b in range(ref.shape[0])], axis=1)
                term = _dot(cols, h_ref[...], TN)

                @pl.when(k == 0)
                def _():
                    acc_ref[...] = term

                @pl.when(k == halves - 1)
                def _():
                    o_ref[...] = (acc_ref[...] + term).astype(BF16)
            return fn

        groups = [[ref for ref, (_, first, _) in zip(pieces, plan) if first == f] for f in firsts]
        _which(pl.program_id(0), edges, [emit(group) for group in groups])

        if hand_over:
            cx, cy, c = lax.axis_index("x"), lax.axis_index("y"), lax.axis_index("c")
            sibling = (cx, cy, 1 - c)

            def to_sibling(sem, src, dst, recv=0):
                return pltpu.make_async_remote_copy(src_ref=src, dst_ref=dst, send_sem=send_sems.at[sem],
                                                    recv_sem=recv_sems.at[recv], device_id=sibling, device_id_type=MESH)

            def tile_copy(tile, core):
                chip, row, start, rows = _sibling_rows(tile, core)
                return to_sibling(tile % 2, stage.at[tile % 2, pl.ds(0, rows), :], got_ref.at[chip, pl.ds(row, rows), :])

            rest_copy = to_sibling(2, _half_rows(rest_ref, 1 - c), got_rest_ref, recv=1)

            @pl.when((pl.program_id(0) == 0) & (k == 0))
            def _():
                rest_copy.start()

            for tile in range(step):
                for core in range(2):
                    @pl.when((pl.program_id(0) == tile) & (k == halves - 1) & (c == core))
                    def _(tile=tile, core=core):
                        if tile >= 2 and _sibling_rows(tile - 2, core):
                            tile_copy(tile - 2, core).wait_send()
                        if _sibling_rows(tile, core):
                            _, _, start, rows = _sibling_rows(tile, core)
                            stage[tile % 2, 0:rows, :] = o_ref[start:start + rows, :]
                            tile_copy(tile, core).start()
                        if tile == step - 1:
                            for last in (step - 2, step - 1):
                                if _sibling_rows(last, core):
                                    tile_copy(last, core).wait_send()
                            rest_copy.wait()
                            to_sibling(0, got_ref, got_ref).wait_recv()

    def cols_spec(piece, n, first, steps):
        def index(j, k):
            return jnp.clip(j - first, 0, steps - 1), jnp.where((j >= first) & (j < first + steps), k, 0), 0
        return pl.BlockSpec((n, SEQ // halves, piece.shape[2]), index)

    tile_spec = pl.BlockSpec((W_BLOCK, D_MODEL), lambda j, k: (j, 0))
    in_specs = [cols_spec(p, *pl_) for p, pl_ in zip(d_proj, plan)] + [
        pl.BlockSpec((SEQ // halves, D_MODEL), lambda j, k: (k, 0))]
    acc = pltpu.VMEM((W_BLOCK, D_MODEL), F32)
    if not hand_over:
        return pl.pallas_call(
            body, name="d_w_in", grid=(step, halves), in_specs=in_specs, out_specs=tile_spec,
            out_shape=jax.ShapeDtypeStruct((IN_WIDTH, D_MODEL), BF16), scratch_shapes=[acc],
            compiler_params=_params("arbitrary", "arbitrary"),
        )(*d_proj, h)
    return pl.pallas_call(
        body, name="d_w_in", grid=(step, halves), in_specs=in_specs + [ANY], out_specs=[tile_spec, ANY, ANY],
        out_shape=[jax.ShapeDtypeStruct((IN_WIDTH, D_MODEL), BF16),
                   jax.ShapeDtypeStruct((N_CHIPS, half, D_MODEL), BF16),
                   jax.ShapeDtypeStruct((N_CHIPS, rest.shape[1] // 2, D_MODEL), BF16)],
        scratch_shapes=[acc, pltpu.SemaphoreType.DMA((3,)), pltpu.SemaphoreType.DMA((2,)),
                        pltpu.VMEM((2, W_BLOCK, D_MODEL), BF16)],
        compiler_params=_params("arbitrary", "arbitrary"),
    )(*d_proj, h, rest)


RELAY_STEP = 9
RELAY_ROWS = 352


def _d_x(d_proj, w_t, x, gain, dy, chip_sums):
    tm = 256
    n_steps = SEQ // tm
    n_w = IN_WIDTH // W_BLOCK
    n_p, n_s = len(d_proj), len(chip_sums)

    def body(*refs):
        pieces, w_refs = refs[:n_p], refs[n_p:n_p + n_w]
        x_ref, g_ref, dy_ref = refs[n_p + n_w:n_p + n_w + 3]
        q_refs = refs[n_p + n_w + 3:n_p + n_w + 3 + n_s]
        dx_ref, dgain_ref = refs[n_p + n_w + 3 + n_s:n_p + n_w + 5 + n_s]
        outs = refs[n_p + n_w + 5 + n_s:n_p + n_w + 5 + 4 * n_s]
        got_refs, relay_refs, sum_refs = outs[:n_s], outs[n_s:2 * n_s], outs[2 * n_s:]
        if n_s:
            send_sems, recv_sems, local_sems, a_buf, b_buf, c_buf = refs[n_p + n_w + 5 + 4 * n_s:]

        def hops():
            cx, cy, c = lax.axis_index("x"), lax.axis_index("y"), lax.axis_index("c")
            near = (cx + (1 - c) - 2 * cx * (1 - c), cy + c - 2 * cy * c)
            far = (cx + c - 2 * cx * c, cy + (1 - c) - 2 * cy * (1 - c))
            chip = lambda p: 2 * p[0] + p[1]

            def copy(k, src, dst, to):
                return pltpu.make_async_remote_copy(src_ref=src, dst_ref=dst, send_sem=send_sems.at[k],
                                                    recv_sem=recv_sems.at[k], device_id=(*to, c), device_id_type=MESH)

            first = [(copy(3 * b, q.at[chip(near)], got.at[0], near),
                      copy(3 * b + 1, q.at[3 - chip((cx, cy))], relay, near))
                     for b, (q, got, relay) in enumerate(zip(q_refs, got_refs, relay_refs))]
            second = [copy(3 * b + 2, s, got.at[1], far) for b, (s, got) in enumerate(zip(sum_refs, got_refs))]
            return first, second, chip(far)

        @pl.when(pl.program_id(0) == 0)
        def _():
            dgain_ref[...] = jnp.zeros_like(dgain_ref)
            if n_s:
                for direct, pass_on in hops()[0]:
                    direct.start()
                    pass_on.start()

        if n_s:
            @pl.when(pl.program_id(0) == RELAY_STEP)
            def _():
                first, second, far_chip = hops()
                for b, (q, relay, total) in enumerate(zip(q_refs, relay_refs, sum_refs)):
                    first[b][1].wait_recv()
                    half = relay.shape[0]
                    for r0 in range(0, half, RELAY_ROWS):
                        rows = min(RELAY_ROWS, half - r0)
                        mine = pltpu.make_async_copy(q.at[far_chip, pl.ds(r0, rows), :], a_buf.at[pl.ds(0, rows), :],
                                                     local_sems.at[0])
                        theirs = pltpu.make_async_copy(relay.at[pl.ds(r0, rows), :], b_buf.at[pl.ds(0, rows), :],
                                                       local_sems.at[1])
                        mine.start()
                        theirs.start()
                        mine.wait()
                        theirs.wait()
                        c_buf[0:rows, :] = (a_buf[0:rows, :].astype(F32) + b_buf[0:rows, :].astype(F32)).astype(BF16)
                        store = pltpu.make_async_copy(c_buf.at[pl.ds(0, rows), :], total.at[pl.ds(r0, rows), :],
                                                      local_sems.at[2])
                        store.start()
                        store.wait()
                    second[b].start()

        blocks = [(piece, k) for piece in pieces for k in range(piece.shape[0])]
        dh, group, width, blk = None, [], 0, 0
        for piece, k in blocks:
            group.append(piece[k])
            width += piece.shape[2]
            if width == W_BLOCK:
                term = _dot(jnp.concatenate(group, axis=1), w_refs[blk][...])
                dh = term if dh is None else dh + term
                group, width, blk = [], 0, blk + 1
        assert not group and blk == n_w
        xf = x_ref[...]
        r = lax.rsqrt(jnp.mean(xf * xf, axis=-1, keepdims=True) + EPS)
        xh = xf * r
        dxh = dh * g_ref[...]
        dx_ref[...] = r * (dxh - xh * jnp.mean(dxh * xh, axis=-1, keepdims=True)) + dy_ref[...]
        dgain_ref[...] += _rows8(jnp.sum(dh * xh, axis=0, keepdims=True))

        if n_s:
            @pl.when(pl.program_id(0) == n_steps - 1)
            def _():
                first, second, _ = hops()
                for direct, pass_on in first:
                    direct.wait()
                    pass_on.wait_send()
                for cp in second:
                    cp.wait()

    row = pl.BlockSpec((tm, D_MODEL), lambda i: (i, 0))
    halves = [q.shape[1] for q in chip_sums]
    res = pl.pallas_call(
        body, name="d_x", grid=(n_steps,),
        in_specs=[pl.BlockSpec((p.shape[0], tm, p.shape[2]), lambda i: (0, i, 0)) for p in d_proj] + _w_blocks(0, n_w)
        + [row, pl.BlockSpec((1, D_MODEL), lambda i: (0, 0)), row] + [ANY] * n_s,
        out_specs=[row, pl.BlockSpec((8, D_MODEL), lambda i: (0, 0))] + [ANY] * (3 * n_s),
        out_shape=[jax.ShapeDtypeStruct((SEQ, D_MODEL), F32), jax.ShapeDtypeStruct((8, D_MODEL), F32)]
        + [jax.ShapeDtypeStruct((2, half, D_MODEL), BF16) for half in halves]
        + [jax.ShapeDtypeStruct((half, D_MODEL), BF16) for half in halves] * 2,
        scratch_shapes=[pltpu.SemaphoreType.DMA((3 * n_s,)), pltpu.SemaphoreType.DMA((3 * n_s,)),
                        pltpu.SemaphoreType.DMA((3,))] + [pltpu.VMEM((RELAY_ROWS, D_MODEL), BF16)] * 3 if n_s else [],
        compiler_params=_params("arbitrary"),
    )(*d_proj, *([w_t] * n_w), x, gain, dy, *chip_sums)
    return res[0], res[1], res[2:2 + n_s]


def _my_place():
    x, y, c = lax.axis_index("x"), lax.axis_index("y"), lax.axis_index("c")
    return jnp.stack([2 * x + y, c]).astype(jnp.int32)


def _half_rows(ref, half):
    rows = ref.shape[-2] // 2
    idx = (slice(None),) * (len(ref.shape) - 2) + (pl.ds(pl.multiple_of(half * rows, 16), rows), slice(None))
    return ref.at[idx]


def _add_halves(place, grads, theirs, name):
    half = theirs.shape[1]
    tr = _row_tile(half)
    n = half // tr

    def body(place_ref, g_ref, t_ref, o_ref):
        o_ref[...] = (g_ref[...].astype(F32) + t_ref[...].astype(F32)).astype(BF16)

    return pl.pallas_call(
        body, name=name,
        grid_spec=pltpu.PrefetchScalarGridSpec(
            num_scalar_prefetch=1, grid=(N_CHIPS, n),
            in_specs=[pl.BlockSpec((None, tr, D_MODEL), lambda s, i, p: (s, p[1] * n + i, 0)),
                      pl.BlockSpec((None, tr, D_MODEL), lambda s, i, p: (s, i, 0))],
            out_specs=pl.BlockSpec((None, tr, D_MODEL), lambda s, i, p: (s, i, 0))),
        out_shape=jax.ShapeDtypeStruct((N_CHIPS, half, D_MODEL), BF16),
        compiler_params=_params("arbitrary", "arbitrary"),
    )(place, grads, theirs)


def _add_chips(place, chip_sums, others, name):
    half = others.shape[1]
    tr = _row_tile(half)
    n = half // tr

    def body(place_ref, q_ref, o_ref, r_ref):
        acc = q_ref[...].astype(F32)
        for j in range(others.shape[0]):
            acc = acc + o_ref[j].astype(F32)
        r_ref[...] = acc

    return pl.pallas_call(
        body, name=name,
        grid_spec=pltpu.PrefetchScalarGridSpec(
            num_scalar_prefetch=1, grid=(n,),
            in_specs=[pl.BlockSpec((None, tr, D_MODEL), lambda i, p: (p[0], i, 0)),
                      pl.BlockSpec((others.shape[0], tr, D_MODEL), lambda i, p: (0, i, 0))],
            out_specs=pl.BlockSpec((tr, D_MODEL), lambda i, p: (p[1] * n + i, 0))),
        out_shape=jax.ShapeDtypeStruct((2 * half, D_MODEL), F32),
        compiler_params=_params("arbitrary"),
    )(place, chip_sums, others)


def _join_halves(shards, block):
    n = len(shards)
    rows = block.shape[0]

    def body(*refs):
        b_ref, o_refs, sum_ref = refs[n], refs[n + 1:2 * n + 1], refs[2 * n + 1]
        send_sems, recv_sems, small_send, small_recv, local_sem, all_ref = refs[2 * n + 2:]
        x, y, c = lax.axis_index("x"), lax.axis_index("y"), lax.axis_index("c")
        me, sibling = (x, y, c), (x, y, 1 - c)
        chips = [(1 - x, y), (x, 1 - y), (1 - x, 1 - y)]

        def half(k, rows_ref):
            return pltpu.make_async_remote_copy(src_ref=rows_ref, dst_ref=rows_ref, send_sem=send_sems.at[k],
                                                recv_sem=recv_sems.at[k], device_id=sibling, device_id_type=MESH)

        def at(px, py, pc):
            return all_ref.at[pl.ds(pl.multiple_of((4 * px + 2 * py + pc) * rows, 8), rows), :]

        def small(k, block_of, to, src=None):
            return pltpu.make_async_remote_copy(src_ref=at(*block_of) if src is None else src, dst_ref=at(*block_of),
                                                send_sem=small_send.at[k], recv_sem=small_recv.at[k],
                                                device_id=to, device_id_type=MESH)

        sends = [half(k, _half_rows(o, c)) for k, o in enumerate(o_refs)]
        for cp in sends:
            cp.start()
        mine = pltpu.make_async_copy(b_ref, at(*me), local_sem)
        mine.start()
        first = [small(0, me, sibling, src=b_ref)]
        first += [small(1 + j, me, (*chip, c), src=b_ref) for j, chip in enumerate(chips)]
        for cp in first:
            cp.start()
        passed = [small(4 + j, (*chip, c), sibling) for j, chip in enumerate(chips)]
        for j, chip in enumerate(chips):
            small(1 + j, (*chip, c), me).wait_recv()
            passed[j].start()
        small(0, sibling, me).wait_recv()
        for j, chip in enumerate(chips):
            small(4 + j, (*chip, 1 - c), me).wait_recv()
        mine.wait()
        acc = all_ref[0:rows, :]
        for dev in range(1, 8):
            acc = acc + all_ref[rows * dev:rows * (dev + 1), :]
        sum_ref[...] = acc
        for k, o in enumerate(o_refs):
            half(k, _half_rows(o, 1 - c)).wait_recv()
        for cp in sends + first + passed:
            cp.wait_send()

    res = pl.pallas_call(
        body, name="reduce_join_halves", in_specs=[ANY] * n + [pl.BlockSpec(memory_space=pltpu.VMEM)],
        out_specs=[ANY] * n + [pl.BlockSpec(memory_space=pltpu.VMEM)],
        out_shape=[jax.ShapeDtypeStruct(s.shape, F32) for s in shards] + [jax.ShapeDtypeStruct(block.shape, F32)],
        input_output_aliases={k: k for k in range(n)},
        scratch_shapes=[pltpu.SemaphoreType.DMA((n,)), pltpu.SemaphoreType.DMA((n,)),
                        pltpu.SemaphoreType.DMA((7,)), pltpu.SemaphoreType.DMA((7,)), pltpu.SemaphoreType.DMA,
                        pltpu.VMEM((8 * rows, D_MODEL), F32)],
    )(*shards, block)
    return res[:n], res[n]


def _adamw_math(w, g, m, v):
    m = ADAM_B1 * m + (1.0 - ADAM_B1) * g
    v = ADAM_B2 * v + (1.0 - ADAM_B2) * (g * g)
    m_hat = m / (1.0 - ADAM_B1 ** ADAM_STEP)
    v_hat = v / (1.0 - ADAM_B2 ** ADAM_STEP)
    return -ADAM_LR * (m_hat / (jnp.sqrt(v_hat) + ADAM_EPS) + ADAM_WD * w), m, v


def _adamw(w, g, m, v, name):
    r, c = w.shape
    tr = _row_tile(r)

    def body(w_ref, g_ref, m_ref, v_ref, d_ref, nm_ref, nv_ref):
        d_ref[...], nm_ref[...], nv_ref[...] = _adamw_math(w_ref[...], g_ref[...], m_ref[...], v_ref[...])

    spec = pl.BlockSpec((tr, c), lambda i: (i, 0))
    return pl.pallas_call(
        body, name=name, grid=(r // tr,), in_specs=[spec] * 4, out_specs=[spec] * 3,
        out_shape=[jax.ShapeDtypeStruct((r, c), F32)] * 3, compiler_params=_params("arbitrary"),
    )(w, g, m, v)


def _adamw_small(ws, gs, ms, vs):
    n = len(ws)

    def body(*refs):
        ins, outs = refs[:4 * n], refs[4 * n:]
        for k in range(n):
            d, m, v = _adamw_math(ins[k][...], ins[n + k][...], ins[2 * n + k][...], ins[3 * n + k][...])
            outs[k][...], outs[n + k][...], outs[2 * n + k][...] = d, m, v

    shapes = [jax.ShapeDtypeStruct(w.shape, F32) for w in ws]
    res = pl.pallas_call(body, name="adamw_small", out_shape=shapes * 3)(*ws, *gs, *ms, *vs)
    return res[:n], res[n:2 * n], res[2 * n:]


def _fold_heads(partials):
    t = jnp.sum(partials[:, 0, :], axis=0)
    return (t[:HEAD_DIM] + t[HEAD_DIM:]).reshape(1, HEAD_DIM)


def _local_step(x, target, norm_gain, w_t, w_a, w_b, w_o, b_m, q_norm_a, k_norm_a, q_norm_b, k_norm_b, sink_a,
                rel_bias, start_reduce=None, small_shard=None):
    two = lambda gain: jnp.concatenate([gain, gain], axis=1)
    bias_a = _bias_lines(rel_bias[:, :8], A_HALF_WINDOW, 1)
    bias_b = jnp.concatenate([_bias_lines(rel_bias[:, 8 + 8 * g:16 + 8 * g], B_HALF_WINDOW, d)
                              for g, d in enumerate(B_DILATIONS)], axis=0)

    qkv, h, *small_all = _in_proj(x, norm_gain, w_t, 0, QKV_WIDTH // W_BLOCK, BF16, "in_proj_qkv", True, small_shard)
    if small_shard is not None:
        w_a, w_b, w_o, b_m = _unpack_weights(small_all[0])
    gates, = _in_proj(x, norm_gain, w_t, QKV_WIDTH // W_BLOCK, GATE_WIDTH // W_BLOCK, F32, "in_proj_gates", False)
    out_a, lse_a = _attn_a_fwd(qkv, two(q_norm_a), two(k_norm_a), bias_a, sink_a)
    out_b, lse_b = _attn_b_fwd(qkv, two(q_norm_b), two(k_norm_b), bias_b)

    dy, dgates, d_out_a, d_out_b, delta_a, delta_b, d_wa, d_wb, d_wo, d_bm, sq = _middle(
        out_a, out_b, gates, x, target, w_a, w_b, w_o, b_m)
    loss = (0.5 / D_MODEL) * jnp.sum(sq)

    dq_a, dkv_a, dgq_a, dgk_a, ds_a, dsink = _attn_a_bwd(
        qkv, two(q_norm_a), two(k_norm_a), bias_a, sink_a, delta_a, lse_a, d_out_a)
    dq_b, dk_b, dv_b, dgq_b, dgk_b, ds_b = _attn_b_bwd(
        qkv, two(q_norm_b), two(k_norm_b), bias_b, delta_b, lse_b, d_out_b)
    d_proj = (dq_a, dkv_a, dq_b, dk_b, dv_b, dgates)

    d_bm_rows = jnp.pad(d_bm.reshape(2, N_CHIPS, 256).transpose(1, 0, 2),
                        ((0, 0), (0, REST_ROWS - 514), (0, D_MODEL - 256)))
    rest = jnp.concatenate([d_wo.reshape(N_CHIPS, 256, D_MODEL), d_wa.reshape(N_CHIPS, 128, D_MODEL),
                            d_wb.reshape(N_CHIPS, 128, D_MODEL), d_bm_rows], axis=1)
    if start_reduce is None:
        grads, chip_sums = [_d_w_in(d_proj, h).reshape(N_CHIPS, W_IN_SHARD, D_MODEL), rest], []
    else:
        d_wt, *theirs = _d_w_in(d_proj, h, rest.astype(BF16))
        grads = [d_wt.reshape(N_CHIPS, W_IN_SHARD, D_MODEL), rest]
        chip_sums = start_reduce(grads, theirs)
    grad_x, d_gain, others = _d_x(d_proj, w_t, x, norm_gain, dy, chip_sums)

    d_rel = jnp.concatenate(
        [_bias_grad(ds_a, A_HALF_WINDOW, 1)]
        + [_bias_grad(ds_b[4 * g:4 * g + 4], B_HALF_WINDOW, d) for g, d in enumerate(B_DILATIONS)], axis=1)
    d_sink = jnp.sum(dsink, axis=(2, 3)).reshape(1, 8)
    dgk_a_row = dgk_a[0]
    small = jnp.zeros((8, D_MODEL), F32)
    small = small.at[0].set(d_gain[0])
    small = small.at[1].set(d_rel.reshape(-1))
    misc = jnp.concatenate([_fold_heads(dgq_a), (dgk_a_row[:HEAD_DIM] + dgk_a_row[HEAD_DIM:]).reshape(1, HEAD_DIM),
                            _fold_heads(dgq_b), _fold_heads(dgk_b), d_sink], axis=1)
    small = small.at[2, :264].set(misc[0])

    return loss, grad_x, grads, small, chip_sums, others


def _unpack_weights(small_all):
    sm = small_all.reshape(N_CHIPS, SMALL_ROWS, D_MODEL)
    w_o = sm[:, 0:256].reshape(D_MODEL, D_MODEL)
    w_a = sm[:, 256:384].reshape(N_CHIPS, 512, 256).transpose(1, 0, 2).reshape(512, D_MODEL)
    w_b = sm[:, 384:512].reshape(N_CHIPS, 512, 256).transpose(1, 0, 2).reshape(512, D_MODEL)
    b_m = lax.bitcast_convert_type(sm[:, 512].reshape(N_CHIPS, 2, 256, 2), F32)
    return w_a, w_b, w_o, b_m.transpose(1, 0, 2).reshape(2, D_MODEL)


def _pack_small_weights(w_branch_a, w_branch_b, b_merge, w_out):
    b_m = jnp.pad(lax.bitcast_convert_type(b_merge, BF16).reshape(1, D_MODEL), ((0, SMALL_ROWS - 513), (0, 0)))
    return jnp.concatenate([w_out.astype(BF16), w_branch_a.astype(BF16).reshape(128, D_MODEL),
                            w_branch_b.astype(BF16).reshape(128, D_MODEL), b_m], axis=0)


def kernel(x, norm_gain, w_in, q_norm_a, k_norm_a, q_norm_b, k_norm_b, sink_a, rel_bias, w_branch_a, w_branch_b, b_merge, w_out, loss_target, m_norm_gain, m_w_in, m_q_norm_a, m_k_norm_a, m_q_norm_b, m_k_norm_b, m_sink_a, m_rel_bias, m_w_branch_a, m_w_branch_b, m_b_merge, m_w_out, v_norm_gain, v_w_in, v_q_norm_a, v_k_norm_a, v_q_norm_b, v_k_norm_b, v_sink_a, v_rel_bias, v_w_branch_a, v_w_branch_b, v_b_merge, v_w_out):
    w_in_t, m_w_in_t, v_w_in_t = (jnp.transpose(t[0]) for t in (w_in, m_w_in, v_w_in))
    wt_shard = _cast_rows(w_in_t, BF16, "w_in_cast")
    w_t = _gather_weights(wt_shard)
    small_shard = _pack_small_weights(w_branch_a[0], w_branch_b[0], b_merge[0], w_out[0])

    place = _my_place()
    names = ("w_in", "rest")

    def start_reduce(grads, theirs):
        return [_add_halves(place, g, t, "reduce_add_halves_" + n) for g, t, n in zip(grads, theirs, names)]

    loss_part, grad_x, _, small, chip_sums, others = _local_step(
        x[0], loss_target[0], norm_gain, w_t, None, None, None, None, q_norm_a, k_norm_a, q_norm_b, k_norm_b,
        sink_a, rel_bias, start_reduce, small_shard)

    (g_wt, g_rest), small = _join_halves(
        [_add_chips(place, q, o, "reduce_add_chips_" + n) for q, o, n in zip(chip_sums, others, names)],
        small.at[3, 0].set(loss_part))
    loss = small[3, 0]

    g_w_out = g_rest[0:256]
    g_w_a = g_rest[256:384].reshape(512, 256)
    g_w_b = g_rest[384:512].reshape(512, 256)
    g_b_merge = g_rest[512:514, :256]
    g_norm_gain = small[0:1]
    g_rel_bias = small[1].reshape(N_BUCKETS, N_BUCKETS)
    g_q_a, g_k_a, g_q_b, g_k_b = (small[2:3, 64 * k:64 * k + 64] for k in range(4))
    g_sink = small[2:3, 256:264]

    big_names = (("w_branch_a", w_branch_a, g_w_a, m_w_branch_a, v_w_branch_a),
                 ("w_branch_b", w_branch_b, g_w_b, m_w_branch_b, v_w_branch_b),
                 ("w_out", w_out, g_w_out, m_w_out, v_w_out))
    upd = {name: (g,) + tuple(_adamw(w[0], g, m[0], v[0], "adamw_" + name)) for name, w, g, m, v in big_names}
    upd["w_in"] = tuple(jnp.transpose(t) for t in (g_wt,) + tuple(_adamw(w_in_t, g_wt, m_w_in_t, v_w_in_t, "adamw_w_in")))
    small_names = ("norm_gain", "q_norm_a", "k_norm_a", "q_norm_b", "k_norm_b", "sink_a", "rel_bias", "b_merge")
    ws = [norm_gain, q_norm_a, k_norm_a, q_norm_b, k_norm_b, sink_a, rel_bias, b_merge[0]]
    gs = [g_norm_gain, g_q_a, g_k_a, g_q_b, g_k_b, g_sink, g_rel_bias, g_b_merge]
    ms = [m_norm_gain, m_q_norm_a, m_k_norm_a, m_q_norm_b, m_k_norm_b, m_sink_a, m_rel_bias, m_b_merge[0]]
    vs = [v_norm_gain, v_q_norm_a, v_k_norm_a, v_q_norm_b, v_k_norm_b, v_sink_a, v_rel_bias, v_b_merge[0]]
    ds, nms, nvs = _adamw_small(ws, gs, ms, vs)
    for k, name in enumerate(small_names):
        upd[name] = (gs[k], ds[k], nms[k], nvs[k])

    order = ("norm_gain", "w_in", "q_norm_a", "k_norm_a", "q_norm_b", "k_norm_b", "sink_a", "rel_bias",
             "w_branch_a", "w_branch_b", "b_merge", "w_out")
    lead = {"w_in", "w_branch_a", "w_branch_b", "b_merge", "w_out"}
    outs = [loss, grad_x[None]]
    for part in range(4):
        outs += [upd[name][part][None] if name in lead else upd[name][part] for name in order]
    return tuple(outs)
```

```python
import math

import numpy as np
import jax
import jax.numpy as jnp
from jax import lax
from jax.experimental import pallas as pl
from jax.experimental.pallas import tpu as pltpu

F32 = jnp.float32
BF16 = jnp.bfloat16

SEQ = 4096
D_MODEL = 1024
HEAD_DIM = 64
LANES = 128
EPS = 1e-6
NEG_INF = -1e30
SCALE = HEAD_DIM ** -0.5
N_BUCKETS = 32
MAX_DISTANCE = 1024
N_CHIPS = 4

A_HALF_WINDOW = 128
B_HALF_WINDOW = 64
B_DILATIONS = (1, 4, 16)
Q_BLOCK = 128

QKV_WIDTH = 5376
GATE_WIDTH = 3072
QA_BLK, KA_BLK, VA_BLK = 0, 4, 5
QB_BLK, KB_BLK, VB_BLK = 6, 18, 30
IN_WIDTH = QKV_WIDTH + GATE_WIDTH
W_IN_SHARD = IN_WIDTH // N_CHIPS

SMALL_ROWS = 544
REST_ROWS = 544

ADAM_LR = 0.001
ADAM_B1 = 0.9
ADAM_B2 = 0.999
ADAM_EPS = 1e-08
ADAM_WD = 0.01
ADAM_STEP = 10

VMEM_LIMIT = 56 * 1024 * 1024

NT = (((1,), (1,)), ((), ()))
TN = (((0,), (0,)), ((), ()))
MESH = pl.DeviceIdType.MESH
ANY = pl.BlockSpec(memory_space=pl.ANY)


def _dot(a, b, dims=None):
    if dims is None:
        return jnp.dot(a, b, preferred_element_type=F32)
    return lax.dot_general(a, b, dims, preferred_element_type=F32)


def _params(*semantics):
    return pltpu.CompilerParams(dimension_semantics=semantics or None, vmem_limit_bytes=VMEM_LIMIT)


def _line_width(half_window):
    return pl.cdiv(2 * Q_BLOCK + 2 * half_window - 1, LANES) * LANES


def _bucket_onehot(half_window, stride):
    rel = np.arange(_line_width(half_window)) - (Q_BLOCK - 1) - half_window
    band = np.abs(rel) <= half_window
    rel = rel * stride
    half, max_exact = N_BUCKETS // 2, N_BUCKETS // 4
    n = np.abs(rel)
    nf = np.maximum(n, max_exact).astype(np.float32)
    large = max_exact + (np.log(nf / np.float32(max_exact)) / np.float32(math.log(MAX_DISTANCE / max_exact))
                         * np.float32(half - max_exact)).astype(np.int32)
    large = np.minimum(large, half - 1)
    bucket = (rel > 0).astype(np.int32) * half + np.where(n < max_exact, n, large)
    onehot = (bucket[..., None] == np.arange(N_BUCKETS)) & band[..., None]
    return onehot.astype(np.float32), band


def _bias_lines(rel_bias_cols, half_window, stride):
    onehot, band = _bucket_onehot(half_window, stride)
    h = rel_bias_cols.shape[1]
    t = jnp.einsum("tb,bh->ht", jnp.asarray(onehot), rel_bias_cols, precision=lax.Precision.HIGHEST)
    t = t + jnp.asarray(np.where(band, 0.0, NEG_INF).astype(np.float32))
    return t.reshape(h // 2, 2, -1)


def _bias_grad(d_lines, half_window, stride):
    onehot, _ = _bucket_onehot(half_window, stride)
    h = d_lines.shape[0] * 2
    return jnp.einsum("tb,ht->bh", jnp.asarray(onehot), d_lines.reshape(h, -1), precision=lax.Precision.HIGHEST)


def _unroll_bias(line_ref, tile_ref, w):
    width = line_ref.shape[1]
    for j in range(2):
        rows = jnp.broadcast_to(line_ref[j:j + 1, :], (Q_BLOCK, width))
        rows = pltpu.roll(rows, width - (Q_BLOCK - 1), 1, stride=1, stride_axis=0)
        tile_ref[j * Q_BLOCK:(j + 1) * Q_BLOCK, :] = rows[:, :w]


def _fold_bias_grad(tile_ref, line_ref, w):
    width = line_ref.shape[1]
    row = lax.broadcasted_iota(jnp.int32, (Q_BLOCK, Q_BLOCK), 0)
    col = lax.broadcasted_iota(jnp.int32, (Q_BLOCK, Q_BLOCK), 1)
    flip = jnp.where(row + col == Q_BLOCK - 1, 1.0, 0.0).astype(BF16)
    for j in range(2):
        tile = tile_ref[j * Q_BLOCK:(j + 1) * Q_BLOCK, :]
        hi = tile.astype(BF16)
        lo = (tile - hi.astype(F32)).astype(BF16)
        rows = _dot(flip, hi) + _dot(flip, lo)
        rows = jnp.concatenate([rows, jnp.zeros((Q_BLOCK, width - w), F32)], axis=1)
        rows = pltpu.roll(rows, 0, 1, stride=1, stride_axis=0)
        line_ref[j:j + 1, :] = jnp.sum(rows, axis=0, keepdims=True)


def _row_tile(rows):
    return max(t for t in range(16, 385, 16) if rows % t == 0)


def _cast_rows(w, out_dtype, name):
    r, c = w.shape
    tr = _row_tile(r)

    def body(w_ref, o_ref):
        o_ref[...] = w_ref[...].astype(out_dtype)

    spec = pl.BlockSpec((tr, c), lambda i: (i, 0))
    return pl.pallas_call(
        body, name=name, grid=(r // tr,), in_specs=[spec], out_specs=spec,
        out_shape=jax.ShapeDtypeStruct((r, c), out_dtype), compiler_params=_params("arbitrary"),
    )(w)


STAGE_ROWS = 528


def _gather_scratch():
    return [pltpu.SemaphoreType.DMA((12,)), pltpu.SemaphoreType.DMA((12,)), pltpu.SemaphoreType.DMA((2,)),
            pltpu.SemaphoreType.DMA((2,)), pltpu.VMEM((2, STAGE_ROWS, D_MODEL), BF16)]


def _gather_phases(src_ref, out_ref, send_sems, recv_sems, in_sems, out_sems, stage):
    rows = src_ref.shape[0]
    x, y, c = lax.axis_index("x"), lax.axis_index("y"), lax.axis_index("c")
    sibling = (x, y, 1 - c)
    near = (x + (1 - c) - 2 * x * (1 - c), y + c - 2 * y * c)
    far = (x + c - 2 * x * c, y + (1 - c) - 2 * y * (1 - c))
    diag = (1 - x, 1 - y)
    chip_no = lambda chip: 2 * chip[0] + chip[1]
    my_chip = chip_no((x, y))

    pieces = 2 if (rows // 2) % 32 == 0 else 1
    n = rows // 2 // pieces

    def half_of(chip, half, p):
        start = pl.multiple_of(chip * rows + half * (rows // 2) + p * n, 16)
        return out_ref.at[pl.ds(start, n), :]

    def copy(k, p, src, dst, to):
        return pltpu.make_async_remote_copy(src_ref=src, dst_ref=dst, send_sem=send_sems.at[k * pieces + p],
                                            recv_sem=recv_sems.at[k * pieces + p], device_id=to, device_id_type=MESH)

    def mine(p):
        return src_ref.at[pl.ds(pl.multiple_of(c * (rows // 2) + p * n, 16), n), :]

    def keep_own():
        outs = []
        for i, r0 in enumerate(range(0, rows, STAGE_ROWS)):
            n = min(STAGE_ROWS, rows - r0)
            slot = i % 2
            if i >= 2:
                outs[i - 2].wait()
            buf = stage.at[slot, pl.ds(0, n), :]
            load = pltpu.make_async_copy(src_ref.at[pl.ds(r0, n), :], buf, in_sems.at[slot])
            load.start()
            load.wait()
            start = pl.multiple_of(my_chip * rows + r0, 16)
            outs.append(pltpu.make_async_copy(buf, out_ref.at[pl.ds(start, n), :], out_sems.at[slot]))
            outs[i].start()
        for cp in outs[-2:]:
            cp.wait()

    def start():
        for p in range(pieces):
            copy(0, p, mine(p), half_of(my_chip, c, p), (*near, c)).start()
            copy(1, p, mine(p), half_of(my_chip, c, p), (*far, c)).start()
        keep_own()

    def pass_on(j, p, chip):
        landed = half_of(chip_no(chip), c, p)
        copy(3 + j, p, landed, landed, sibling).start()

    def relay():
        for p in range(pieces):
            landed = half_of(chip_no(near), c, p)
            copy(0, p, landed, landed, sibling).wait_recv()
            copy(2, p, landed, landed, (*far, c)).start()
            pass_on(0, p, near)

    def forward():
        for j, chip in ((1, far), (2, diag)):
            for p in range(pieces):
                landed = half_of(chip_no(chip), c, p)
                copy(j, p, landed, landed, sibling).wait_recv()
                pass_on(j, p, chip)

    def finish():
        for j, chip in ((0, far), (1, near), (2, diag)):
            for p in range(pieces):
                other = half_of(chip_no(chip), 1 - c, p)
                copy(3 + j, p, other, other, sibling).wait_recv()
        for k in range(6):
            for p in range(pieces):
                copy(k, p, mine(p), mine(p), sibling).wait_send()

    return start, relay, forward, finish


def _gather_weights(shard):
    def body(src_ref, out_ref, *scratch):
        for phase in _gather_phases(src_ref, out_ref, *scratch):
            phase()

    return pl.pallas_call(
        body, name="gather_weights", in_specs=[ANY], out_specs=ANY,
        out_shape=jax.ShapeDtypeStruct((N_CHIPS * shard.shape[0], D_MODEL), BF16),
        scratch_shapes=_gather_scratch(),
    )(shard)


W_BLOCK = 768


def _w_blocks(first, count):
    return [pl.BlockSpec((W_BLOCK, D_MODEL), lambda *_, k=k: (first + k, 0)) for k in range(count)]


def _in_proj(x, gain, w_t, first_block, n_blocks, out_dtype, name, keep_h, ride=None):
    tm = 512
    n_steps = SEQ // tm
    n_out = 2 if keep_h else 1

    def body(x_ref, g_ref, *refs):
        w_refs, outs = refs[:n_blocks], refs[n_blocks + (ride is not None):n_blocks + (ride is not None) + n_out]
        if ride is not None:
            phases = _gather_phases(refs[n_blocks], *refs[n_blocks + 1 + n_out:])
            for step, phase in zip((0, 2, 4, n_steps - 1), phases):
                pl.when(pl.program_id(0) == step)(phase)
        xf = x_ref[...]
        r = lax.rsqrt(jnp.mean(xf * xf, axis=-1, keepdims=True) + EPS)
        h = ((xf * r) * g_ref[...]).astype(BF16)
        if keep_h:
            outs[1][...] = h
        for k, w_ref in enumerate(w_refs):
            outs[0][:, k * W_BLOCK:(k + 1) * W_BLOCK] = _dot(h, w_ref[...], NT).astype(out_dtype)

    riding = [] if ride is None else [ride]
    return pl.pallas_call(
        body, name=name, grid=(n_steps,),
        in_specs=[pl.BlockSpec((tm, D_MODEL), lambda i: (i, 0)), pl.BlockSpec((1, D_MODEL), lambda i: (0, 0))]
        + _w_blocks(first_block, n_blocks) + [ANY for _ in riding],
        out_specs=[pl.BlockSpec((tm, W_BLOCK * n_blocks), lambda i: (i, 0)),
                   pl.BlockSpec((tm, D_MODEL), lambda i: (i, 0))][:n_out] + [ANY for _ in riding],
        out_shape=[jax.ShapeDtypeStruct((SEQ, W_BLOCK * n_blocks), out_dtype),
                   jax.ShapeDtypeStruct((SEQ, D_MODEL), BF16)][:n_out]
        + [jax.ShapeDtypeStruct((N_CHIPS * r.shape[0], D_MODEL), BF16) for r in riding],
        scratch_shapes=_gather_scratch() if riding else [],
        compiler_params=_params("arbitrary"),
    )(x, gain, *([w_t] * n_blocks), *riding)


CHUNK = 256
CHUNK_UNROLL = 4
TILE_UNROLL = 8


def _low_half():
    return lax.broadcasted_iota(jnp.int32, (1, LANES), 1) < HEAD_DIM


def _half_sum(v, low):
    del low
    row = lax.broadcasted_iota(jnp.int32, (2 * LANES, LANES), 0)
    col = lax.broadcasted_iota(jnp.int32, (2 * LANES, LANES), 1)
    ones = jnp.where((row % LANES) // HEAD_DIM == col // HEAD_DIM, 1.0, 0.0).astype(BF16)
    hi = v.astype(BF16)
    lo = (v - hi.astype(F32)).astype(BF16)
    return _dot(jnp.concatenate([hi, lo], axis=1), ones)


def _chunks(fn, init=0):
    def body(i, carry):
        for u in range(CHUNK_UNROLL):
            carry = fn(pl.multiple_of((i * CHUNK_UNROLL + u) * CHUNK, CHUNK), carry)
        return carry

    return lax.fori_loop(0, SEQ // (CHUNK * CHUNK_UNROLL), body, init)


def _inv_rms(t, low):
    del low
    row = lax.broadcasted_iota(jnp.int32, (LANES, LANES), 0)
    col = lax.broadcasted_iota(jnp.int32, (LANES, LANES), 1)
    ones = jnp.where(row // HEAD_DIM == col // HEAD_DIM, 1.0, 0.0).astype(BF16)
    return lax.rsqrt(_dot((t * t).astype(BF16), ones) * (1.0 / HEAD_DIM) + EPS)


def _prep_q(q_ref, gain_ref, qn_ref):
    low = _low_half()

    def step(r0, carry):
        q = q_ref[pl.ds(r0, CHUNK), :].astype(F32)
        qn_ref[pl.ds(r0, CHUNK), :] = ((q * _inv_rms(q, low)) * gain_ref[...]) * SCALE
        return carry

    _chunks(step)


def _own_half(t, keep):
    return jnp.where(keep, t, pltpu.roll(t, HEAD_DIM, 1))


def _prep_kv(k_ref, v_ref, gain_ref, kp_ref, vp_ref, pad, keep=None):
    low = _low_half()
    zeros = jnp.zeros((pad, LANES), F32)
    for ref in (kp_ref, vp_ref):
        ref[pl.ds(0, pad), :] = zeros
        ref[pl.ds(pad + SEQ, pad), :] = zeros

    def step(r0, carry):
        k = k_ref[pl.ds(r0, CHUNK), :].astype(F32)
        v = v_ref[pl.ds(r0, CHUNK), :].astype(F32)
        kn = (k * _inv_rms(k, low)) * gain_ref[...]
        if keep is not None:
            kn, v = _own_half(kn, keep), _own_half(v, keep)
        kp_ref[pl.ds(pad + r0, CHUNK), :] = kn
        vp_ref[pl.ds(pad + r0, CHUNK), :] = v
        return carry

    _chunks(step)


def _tiles(d, half_window, fn):
    w = Q_BLOCK + 2 * half_window
    length = SEQ // d
    n_blocks = length // Q_BLOCK
    col = lax.broadcasted_iota(jnp.int32, (1, w), 1)

    def step(it, carry):
        c, n = it // n_blocks, it % n_blocks
        start = c + (d * Q_BLOCK) * n
        if d == 1:
            start = pl.multiple_of(start, Q_BLOCK)
            q_rows, k_rows = pl.ds(start, Q_BLOCK), pl.ds(start, w)
        else:
            q_rows, k_rows = pl.ds(start, Q_BLOCK, stride=d), pl.ds(start, w, stride=d)
        t = n * Q_BLOCK - half_window + col
        edge = jnp.where((t < 0) | (t >= length), NEG_INF, 0.0)
        fn(q_rows, k_rows, edge)
        return carry

    lax.fori_loop(0, d * n_blocks, step, 0, unroll=TILE_UNROLL)


def _stack_heads(t, low):
    return jnp.concatenate([jnp.where(low, t, 0.0), jnp.where(low, 0.0, t)], axis=0).astype(BF16)


def _unstack_heads(t, low):
    return jnp.where(low, t[:Q_BLOCK], t[Q_BLOCK:])


def _per_head(pair):
    return jnp.concatenate([jnp.full((Q_BLOCK, 1), pair[0], F32), jnp.full((Q_BLOCK, 1), pair[1], F32)], axis=0)


def _fwd_tiles(qn_ref, kp_ref, vp_ref, bias_ref, emit, *, d, half_window, sinks=None):
    low = _low_half()
    w = Q_BLOCK + 2 * half_window
    sink = None if sinks is None else _per_head(sinks)

    def tile(q_rows, k_rows, edge):
        q2 = _stack_heads(qn_ref[q_rows, :], low)
        k = kp_ref[k_rows, :].astype(BF16)
        v1 = jnp.concatenate([vp_ref[k_rows, :], jnp.ones((w, LANES), F32)], axis=1).astype(BF16)
        s = _dot(q2, k, NT) + bias_ref[...] + edge
        m = jnp.max(s, axis=-1, keepdims=True)
        if sink is not None:
            m = jnp.maximum(m, sink)
        o = _dot(jnp.exp(s - m).astype(BF16), v1)
        l = o[:, LANES:]
        if sink is not None:
            l = l + jnp.exp(sink - m)
        emit(q_rows, _unstack_heads(o[:, :LANES] * (1.0 / l), low), _unstack_heads(m + jnp.log(l), low))

    _tiles(d, half_window, tile)


def _bwd_tiles(qn_ref, kp_ref, vp_ref, bias_ref, do_ref, lse_ref, delta_ref, dq_ref, dk_ref, dv_ref, ds_ref,
               *, d, half_window, sinks=None, dsink_ref=None):
    low = _low_half()
    w = Q_BLOCK + 2 * half_window
    sink = None if sinks is None else _per_head(sinks)

    def rows_of(t):
        return jnp.concatenate([t[:, 0:1], t[:, HEAD_DIM:HEAD_DIM + 1]], axis=0)

    def tile(q_rows, k_rows, edge):
        q2 = _stack_heads(qn_ref[q_rows, :], low)
        do2 = _stack_heads(do_ref[q_rows, :], low)
        k = kp_ref[k_rows, :].astype(BF16)
        v = vp_ref[k_rows, :].astype(BF16)
        lse = rows_of(lse_ref[q_rows, :])
        delta = rows_of(delta_ref[q_rows, :])
        p = jnp.exp(_dot(q2, k, NT) + bias_ref[...] + edge - lse)
        ds = p * (_dot(do2, v, NT) - delta)
        ds_ref[...] += ds
        if sink is not None:
            dsink_ref[...] += (-jnp.exp(sink - lse) * delta).reshape(2, Q_BLOCK, 1)
        dsb, pb = ds.astype(BF16), p.astype(BF16)
        dq_ref[q_rows, :] = _unstack_heads(_dot(dsb, k), low)
        dk_ref[k_rows, :] += _dot(dsb, q2, TN)
        dv_ref[k_rows, :] += _dot(pb, do2, TN)

    _tiles(d, half_window, tile)


def _norm_bwd(raw_ref, gain_ref, dn_ref, dn_offset, out_ref, scale):
    low = _low_half()

    def step(r0, dgain):
        t = raw_ref[pl.ds(r0, CHUNK), :].astype(F32)
        dn = dn_ref[pl.ds(dn_offset + r0, CHUNK), :]
        dth = dn * (gain_ref[...] * scale)
        sums = _half_sum(jnp.concatenate([t * t, dth * t], axis=0), low)
        r = lax.rsqrt(sums[:CHUNK] * (1.0 / HEAD_DIM) + EPS)
        th = t * r
        out_ref[pl.ds(r0, CHUNK), :] = (r * (dth - th * (r * sums[CHUNK:] * (1.0 / HEAD_DIM)))).astype(BF16)
        return dgain + jnp.sum(dn * th, axis=0, keepdims=True) * scale

    return _chunks(step, jnp.zeros((1, LANES), F32))


def _rows8(v):
    return jnp.broadcast_to(v, (8, v.shape[-1]))


A_W = Q_BLOCK + 2 * A_HALF_WINDOW
A_PAD = A_HALF_WINDOW


def _seq_block(col_fn):
    return pl.BlockSpec((SEQ, LANES), col_fn)


def _attn_a_fwd(qkv, gain_q, gain_k, bias, sink):
    def body(sink_ref, q_ref, k_ref, v_ref, gq_ref, gk_ref, line_ref, o_ref, lse_ref, qn_ref, kp_ref, vp_ref,
             bias_ref):
        hp = pl.program_id(0)
        keep = (lax.broadcasted_iota(jnp.int32, (1, LANES), 1) // HEAD_DIM) == hp // 2
        _prep_q(q_ref, gq_ref, qn_ref)
        _prep_kv(k_ref, v_ref, gk_ref, kp_ref, vp_ref, A_PAD, keep)
        _unroll_bias(line_ref, bias_ref, A_W)

        def emit(rows, out, lse):
            o_ref[rows, :] = out
            lse_ref[rows, :] = lse

        _fwd_tiles(qn_ref, kp_ref, vp_ref, bias_ref, emit, d=1, half_window=A_HALF_WINDOW,
                   sinks=(sink_ref[2 * hp], sink_ref[2 * hp + 1]))

    vec = pl.BlockSpec((1, LANES), lambda hp, s: (0, 0))
    return pl.pallas_call(
        body, name="attn_a_fwd",
        grid_spec=pltpu.PrefetchScalarGridSpec(
            num_scalar_prefetch=1, grid=(4,),
            in_specs=[_seq_block(lambda hp, s: (0, QA_BLK + hp)), _seq_block(lambda hp, s: (0, KA_BLK)),
                      _seq_block(lambda hp, s: (0, VA_BLK)), vec, vec,
                      pl.BlockSpec((None, 2, _line_width(A_HALF_WINDOW)), lambda hp, s: (hp, 0, 0))],
            out_specs=[_seq_block(lambda hp, s: (0, hp)), _seq_block(lambda hp, s: (0, hp))],
            scratch_shapes=[pltpu.VMEM((SEQ, LANES), F32), pltpu.VMEM((SEQ + 2 * A_PAD, LANES), F32),
                            pltpu.VMEM((SEQ + 2 * A_PAD, LANES), F32), pltpu.VMEM((2 * Q_BLOCK, A_W), F32)]),
        out_shape=[jax.ShapeDtypeStruct((SEQ, 512), F32)] * 2,
        compiler_params=_params("arbitrary"),
    )(sink.reshape(8), qkv, qkv, qkv, gain_q, gain_k, bias)


def _attn_a_bwd(qkv, gain_q, gain_k, bias, sink, delta, lse, d_out):
    def body(sink_ref, q_ref, k_ref, v_ref, gq_ref, gk_ref, line_ref, delta_ref, lse_ref, do_ref,
             dq_out, dkv_out, dgq_out, dgk_out, dline_out, dsink_out,
             qn_ref, kp_ref, vp_ref, dq_ref, dk_ref, dv_ref, dk_tot, dv_tot, bias_ref, ds_out):
        hp = pl.program_id(0)
        kv_head = hp // 2
        keep = (lax.broadcasted_iota(jnp.int32, (1, LANES), 1) // HEAD_DIM) == kv_head
        _prep_q(q_ref, gq_ref, qn_ref)
        _prep_kv(k_ref, v_ref, gk_ref, kp_ref, vp_ref, A_PAD, keep)
        _unroll_bias(line_ref, bias_ref, A_W)
        dk_ref[...] = jnp.zeros_like(dk_ref)
        dv_ref[...] = jnp.zeros_like(dv_ref)
        ds_out[...] = jnp.zeros_like(ds_out)
        dsink_out[...] = jnp.zeros_like(dsink_out)

        @pl.when(hp == 0)
        def _():
            dk_tot[...] = jnp.zeros_like(dk_tot)
            dv_tot[...] = jnp.zeros_like(dv_tot)

        _bwd_tiles(qn_ref, kp_ref, vp_ref, bias_ref, do_ref, lse_ref, delta_ref, dq_ref, dk_ref, dv_ref, ds_out,
                   d=1, half_window=A_HALF_WINDOW, sinks=(sink_ref[2 * hp], sink_ref[2 * hp + 1]),
                   dsink_ref=dsink_out)
        _fold_bias_grad(ds_out, dline_out, A_W)
        dgq_out[...] = _rows8(_norm_bwd(q_ref, gq_ref, dq_ref, 0, dq_out, SCALE))

        def fold(r0, carry):
            rows = pl.ds(A_PAD + r0, CHUNK)
            for acc, tot in ((dk_ref, dk_tot), (dv_ref, dv_tot)):
                t = acc[rows, :]
                tot[pl.ds(r0, CHUNK), :] += jnp.where(keep, t + pltpu.roll(t, HEAD_DIM, 1), 0.0)
            return carry

        _chunks(fold)

        @pl.when(hp == 3)
        def _():
            dgk_out[...] = _rows8(_norm_bwd(k_ref, gk_ref, dk_tot, 0, dkv_out.at[0], 1.0))
            dkv_out[1] = dv_tot[...].astype(BF16)

    vec = pl.BlockSpec((1, LANES), lambda hp, s: (0, 0))
    seq_f32 = pltpu.VMEM((SEQ, LANES), F32)
    padded = pltpu.VMEM((SEQ + 2 * A_PAD, LANES), F32)
    return pl.pallas_call(
        body, name="attn_a_bwd",
        grid_spec=pltpu.PrefetchScalarGridSpec(
            num_scalar_prefetch=1, grid=(4,),
            in_specs=[_seq_block(lambda hp, s: (0, QA_BLK + hp)), _seq_block(lambda hp, s: (0, KA_BLK)),
                      _seq_block(lambda hp, s: (0, VA_BLK)), vec, vec,
                      pl.BlockSpec((None, 2, _line_width(A_HALF_WINDOW)), lambda hp, s: (hp, 0, 0)),
                      _seq_block(lambda hp, s: (0, hp)), _seq_block(lambda hp, s: (0, hp)),
                      _seq_block(lambda hp, s: (0, hp))],
            out_specs=[pl.BlockSpec((None, SEQ, LANES), lambda hp, s: (hp, 0, 0)),
                       pl.BlockSpec((2, SEQ, LANES), lambda hp, s: (0, 0, 0)),
                       pl.BlockSpec((None, 8, LANES), lambda hp, s: (hp, 0, 0)),
                       pl.BlockSpec((8, LANES), lambda hp, s: (0, 0)),
                       pl.BlockSpec((None, 2, _line_width(A_HALF_WINDOW)), lambda hp, s: (hp, 0, 0)),
                       pl.BlockSpec((None, 2, Q_BLOCK, 1), lambda hp, s: (hp, 0, 0, 0))],
            scratch_shapes=[seq_f32, padded, padded, seq_f32, padded, padded, seq_f32, seq_f32,
                            pltpu.VMEM((2 * Q_BLOCK, A_W), F32), pltpu.VMEM((2 * Q_BLOCK, A_W), F32)]),
        out_shape=[jax.ShapeDtypeStruct((4, SEQ, LANES), BF16), jax.ShapeDtypeStruct((2, SEQ, LANES), BF16),
                   jax.ShapeDtypeStruct((4, 8, LANES), F32), jax.ShapeDtypeStruct((8, LANES), F32),
                   jax.ShapeDtypeStruct((4, 2, _line_width(A_HALF_WINDOW)), F32),
                   jax.ShapeDtypeStruct((4, 2, Q_BLOCK, 1), F32)],
        compiler_params=_params("arbitrary"),
    )(sink.reshape(8), qkv, qkv, qkv, gain_q, gain_k, bias, delta, lse, d_out)


B_W = Q_BLOCK + 2 * B_HALF_WINDOW
B_PAD_MAX = B_HALF_WINDOW * B_DILATIONS[-1]


def _attn_b_fwd(qkv, gain_q, gain_k, bias):
    def body(q_ref, k_ref, v_ref, gq_ref, gk_ref, line_ref, o_ref, lse_ref, qn_ref, kp_ref, vp_ref, bias_ref):
        g = pl.program_id(1)
        _prep_q(q_ref, gq_ref, qn_ref)
        _unroll_bias(line_ref, bias_ref, B_W)

        def first(rows, out, lse):
            o_ref[rows, :] = out
            lse_ref[rows, :] = lse

        def combine(rows, out, lse):
            old = lse_ref[rows, :]
            new = jnp.maximum(old, lse) + jnp.log(1.0 + jnp.exp(-jnp.abs(old - lse)))
            o_ref[rows, :] = o_ref[rows, :] * jnp.exp(old - new) + out * jnp.exp(lse - new)
            lse_ref[rows, :] = new

        for gi, d in enumerate(B_DILATIONS):
            @pl.when(g == gi)
            def _():
                _prep_kv(k_ref, v_ref, gk_ref, kp_ref, vp_ref, B_HALF_WINDOW * d)
                _fwd_tiles(qn_ref, kp_ref, vp_ref, bias_ref, first if gi == 0 else combine,
                           d=d, half_window=B_HALF_WINDOW)

    vec = pl.BlockSpec((1, LANES), lambda hp, g: (0, 0))
    padded = pltpu.VMEM((SEQ + 2 * B_PAD_MAX, LANES), F32)
    return pl.pallas_call(
        body, name="attn_b_fwd", grid=(4, 3),
        in_specs=[_seq_block(lambda hp, g: (0, QB_BLK + 4 * g + hp)), _seq_block(lambda hp, g: (0, KB_BLK + 4 * g + hp)),
                  _seq_block(lambda hp, g: (0, VB_BLK + 4 * g + hp)), vec, vec,
                  pl.BlockSpec((None, 2, _line_width(B_HALF_WINDOW)), lambda hp, g: (4 * g + hp, 0, 0))],
        out_specs=[_seq_block(lambda hp, g: (0, hp)), _seq_block(lambda hp, g: (0, hp))],
        out_shape=[jax.ShapeDtypeStruct((SEQ, 512), F32)] * 2,
        scratch_shapes=[pltpu.VMEM((SEQ, LANES), F32), padded, padded, pltpu.VMEM((2 * Q_BLOCK, B_W), F32)],
        compiler_params=_params("arbitrary", "arbitrary"),
    )(qkv, qkv, qkv, gain_q, gain_k, bias)


def _attn_b_bwd(qkv, gain_q, gain_k, bias, delta, lse, d_out):
    def body(q_ref, k_ref, v_ref, gq_ref, gk_ref, line_ref, delta_ref, lse_ref, do_ref,
             dq_out, dk_out, dv_out, dgq_out, dgk_out, dline_out,
             qn_ref, kp_ref, vp_ref, dq_ref, dk_ref, dv_ref, bias_ref, ds_out):
        g = pl.program_id(1)
        _prep_q(q_ref, gq_ref, qn_ref)
        _unroll_bias(line_ref, bias_ref, B_W)
        ds_out[...] = jnp.zeros_like(ds_out)
        for gi, d in enumerate(B_DILATIONS):
            @pl.when(g == gi)
            def _():
                pad = B_HALF_WINDOW * d
                for acc in (dk_ref, dv_ref):
                    acc[pl.ds(0, SEQ + 2 * pad), :] = jnp.zeros((SEQ + 2 * pad, LANES), F32)
                _prep_kv(k_ref, v_ref, gk_ref, kp_ref, vp_ref, pad)
                _bwd_tiles(qn_ref, kp_ref, vp_ref, bias_ref, do_ref, lse_ref, delta_ref, dq_ref, dk_ref, dv_ref,
                           ds_out, d=d, half_window=B_HALF_WINDOW)
                dgk_out[...] = _rows8(_norm_bwd(k_ref, gk_ref, dk_ref, pad, dk_out, 1.0))
                dv_out[...] = dv_ref[pl.ds(pad, SEQ), :].astype(BF16)
        _fold_bias_grad(ds_out, dline_out, B_W)
        dgq_out[...] = _rows8(_norm_bwd(q_ref, gq_ref, dq_ref, 0, dq_out, SCALE))

    vec = pl.BlockSpec((1, LANES), lambda hp, g: (0, 0))
    seq_f32 = pltpu.VMEM((SEQ, LANES), F32)
    padded = pltpu.VMEM((SEQ + 2 * B_PAD_MAX, LANES), F32)
    part = pl.BlockSpec((None, 8, LANES), lambda hp, g: (4 * g + hp, 0, 0))
    line = pl.BlockSpec((None, 2, _line_width(B_HALF_WINDOW)), lambda hp, g: (4 * g + hp, 0, 0))
    return pl.pallas_call(
        body, name="attn_b_bwd", grid=(4, 3),
        in_specs=[_seq_block(lambda hp, g: (0, QB_BLK + 4 * g + hp)), _seq_block(lambda hp, g: (0, KB_BLK + 4 * g + hp)),
                  _seq_block(lambda hp, g: (0, VB_BLK + 4 * g + hp)), vec, vec,
                  line,
                  _seq_block(lambda hp, g: (0, hp)), _seq_block(lambda hp, g: (0, hp)), _seq_block(lambda hp, g: (0, hp))],
        out_specs=[pl.BlockSpec((None, SEQ, LANES), lambda hp, g: (4 * g + hp, 0, 0))] * 3 + [part, part, line],
        out_shape=[jax.ShapeDtypeStruct((12, SEQ, LANES), BF16)] * 3
        + [jax.ShapeDtypeStruct((12, 8, LANES), F32)] * 2
        + [jax.ShapeDtypeStruct((12, 2, _line_width(B_HALF_WINDOW)), F32)],
        scratch_shapes=[seq_f32, padded, padded, seq_f32, padded, padded,
                        pltpu.VMEM((2 * Q_BLOCK, B_W), F32), pltpu.VMEM((2 * Q_BLOCK, B_W), F32)],
        compiler_params=_params("arbitrary", "arbitrary"),
    )(qkv, qkv, qkv, gain_q, gain_k, bias, delta, lse, d_out)


def _sigmoid(t):
    return 1.0 / (1.0 + jnp.exp(-t))


def _middle(out_a, out_b, gates, x, target, w_a, w_b, w_out, b_merge):
    tm = 256
    n_steps = SEQ // tm

    def body(oa_ref, ob_ref, g_ref, x_ref, t_ref, wa_ref, wb_ref, wo_ref, bm_ref,
             dy_ref, dg_ref, doa_ref, dob_ref, dla_ref, dlb_ref, dwa_ref, dwb_ref, dwo_ref, dbm_ref, sq_ref):
        @pl.when(pl.program_id(0) == 0)
        def _():
            for ref in (dwa_ref, dwb_ref, dwo_ref, dbm_ref, sq_ref):
                ref[...] = jnp.zeros_like(ref)

        gate_a, gate_b = g_ref[:, 0:512], g_ref[:, 512:1024]
        sig_a, sig_b = _sigmoid(gate_a), _sigmoid(gate_b)
        silu_a, silu_b = gate_a * sig_a, gate_b * sig_b
        oa, ob = oa_ref[...], ob_ref[...]
        ya, yb = (oa * silu_a).astype(BF16), (ob * silu_b).astype(BF16)
        br_a, br_b = _dot(ya, wa_ref[...]), _dot(yb, wb_ref[...])
        m0 = _sigmoid(g_ref[:, 1024:2048] + bm_ref[0:1, :])
        m1 = _sigmoid(g_ref[:, 2048:3072] + bm_ref[1:2, :])
        merged = (m0 * br_a + m1 * br_b).astype(BF16)
        err = (x_ref[...] + _dot(merged, wo_ref[...])) - t_ref[...]
        sq_ref[...] += jnp.sum(err * err, axis=0, keepdims=True)

        dy = err * (1.0 / D_MODEL)
        dy_ref[...] = dy
        dyb = dy.astype(BF16)
        dmerged = _dot(dyb, wo_ref[...], NT)
        dwo_ref[...] += _dot(merged, dyb, TN)
        dbr_a, dbr_b = (dmerged * m0).astype(BF16), (dmerged * m1).astype(BF16)
        dm0 = (dmerged * br_a) * (m0 * (1.0 - m0))
        dm1 = (dmerged * br_b) * (m1 * (1.0 - m1))
        dbm_ref[0:1, :] += jnp.sum(dm0, axis=0, keepdims=True)
        dbm_ref[1:2, :] += jnp.sum(dm1, axis=0, keepdims=True)
        for s in range(N_CHIPS):
            cols = slice(256 * s, 256 * (s + 1))
            dwa_ref[s] += _dot(ya, dbr_a[:, cols], TN)
            dwb_ref[s] += _dot(yb, dbr_b[:, cols], TN)
        dya, dyb_ = _dot(dbr_a, wa_ref[...], NT), _dot(dbr_b, wb_ref[...], NT)
        doa, dob = dya * silu_a, dyb_ * silu_b
        doa_ref[...] = doa
        dob_ref[...] = dob
        for blk in range(512 // LANES):
            lanes = slice(blk * LANES, (blk + 1) * LANES)
            dla_ref[:, lanes] = _half_sum(doa[:, lanes] * oa[:, lanes], None)
            dlb_ref[:, lanes] = _half_sum(dob[:, lanes] * ob[:, lanes], None)
        d_gates = (((dya * oa) * (sig_a * (1.0 + gate_a * (1.0 - sig_a)))).astype(BF16),
                   ((dyb_ * ob) * (sig_b * (1.0 + gate_b * (1.0 - sig_b)))).astype(BF16),
                   dm0.astype(BF16), dm1.astype(BF16))
        blk = 0
        for part in d_gates:
            for c0 in range(0, part.shape[1], 256):
                dg_ref[blk] = part[:, c0:c0 + 256]
                blk += 1

    def rows(width):
        return pl.BlockSpec((tm, width), lambda i: (i, 0))

    def whole(*shape):
        return pl.BlockSpec(shape, lambda i: (0,) * len(shape))

    return pl.pallas_call(
        body, name="middle", grid=(n_steps,),
        in_specs=[rows(512), rows(512), rows(GATE_WIDTH), rows(D_MODEL), rows(D_MODEL),
                  whole(512, D_MODEL), whole(512, D_MODEL), whole(D_MODEL, D_MODEL), whole(2, D_MODEL)],
        out_specs=[rows(D_MODEL), pl.BlockSpec((GATE_WIDTH // 256, tm, 256), lambda i: (0, i, 0)),
                   rows(512), rows(512), rows(512), rows(512),
                   whole(N_CHIPS, 512, 256), whole(N_CHIPS, 512, 256), whole(D_MODEL, D_MODEL),
                   whole(2, D_MODEL), whole(1, D_MODEL)],
        out_shape=[jax.ShapeDtypeStruct((SEQ, D_MODEL), F32), jax.ShapeDtypeStruct((GATE_WIDTH // 256, SEQ, 256), BF16),
                   jax.ShapeDtypeStruct((SEQ, 512), F32), jax.ShapeDtypeStruct((SEQ, 512), F32),
                   jax.ShapeDtypeStruct((SEQ, 512), F32), jax.ShapeDtypeStruct((SEQ, 512), F32),
                   jax.ShapeDtypeStruct((N_CHIPS, 512, 256), F32), jax.ShapeDtypeStruct((N_CHIPS, 512, 256), F32),
                   jax.ShapeDtypeStruct((D_MODEL, D_MODEL), F32), jax.ShapeDtypeStruct((2, D_MODEL), F32),
                   jax.ShapeDtypeStruct((1, D_MODEL), F32)],
        compiler_params=_params("arbitrary"),
    )(out_a, out_b, gates, x, target, w_a, w_b, w_out, b_merge)


def _which(j, edges, fns):
    lo = 0
    for hi, fn in zip(edges, fns):
        pl.when((j >= lo) & (j < hi))(fn)
        lo = hi


def _sibling_rows(tile, core):
    lo, hi = tile * W_BLOCK, (tile + 1) * W_BLOCK
    for chip in range(N_CHIPS):
        a = chip * W_IN_SHARD + (1 - core) * (W_IN_SHARD // 2)
        first, last = max(lo, a), min(hi, a + W_IN_SHARD // 2)
        if first < last:
            return chip, first - a, first - lo, last - first
    return None


def _d_w_in(d_proj, h, rest=None):
    plan, step, width = [], 0, 0
    for p in d_proj:
        total = p.shape[0] * p.shape[2]
        if width + total <= W_BLOCK:
            plan.append((p.shape[0], step, 1))
            width += total
            if width == W_BLOCK:
                step, width = step + 1, 0
        else:
            assert width == 0 and total % W_BLOCK == 0
            plan.append((W_BLOCK // p.shape[2], step, total // W_BLOCK))
            step += total // W_BLOCK
    assert width == 0 and step == IN_WIDTH // W_BLOCK
    firsts = sorted({first for _, first, _ in plan})
    edges = firsts[1:] + [step]
    halves = 2

    hand_over = rest is not None
    half = W_IN_SHARD // 2

    def body(*refs):
        if hand_over:
            pieces, h_ref, rest_ref = refs[:len(d_proj)], refs[len(d_proj)], refs[len(d_proj) + 1]
            o_ref, got_ref, got_rest_ref, acc_ref, send_sems, recv_sems, stage = refs[len(d_proj) + 2:]
        else:
            pieces, h_ref, o_ref, acc_ref = refs[:-3], refs[-3], refs[-2], refs[-1]
        k = pl.program_id(1)

        def emit(group):
            def fn():
                cols = jnp.concatenate([ref[b] for ref in group for b in range(ref.shape[0])], axis=1)
                term = _dot(cols, h_ref[...], TN)

                @pl.when(k == 0)
                def _():
                    acc_ref[...] = term

                @pl.when(k == halves - 1)
                def _():
                    o_ref[...] = (acc_ref[...] + term).astype(BF16)
            return fn

        groups = [[ref for ref, (_, first, _) in zip(pieces, plan) if first == f] for f in firsts]
        _which(pl.program_id(0), edges, [emit(group) for group in groups])

        if hand_over:
            cx, cy, c = lax.axis_index("x"), lax.axis_index("y"), lax.axis_index("c")
            sibling = (cx, cy, 1 - c)

            def to_sibling(sem, src, dst, recv=0):
                return pltpu.make_async_remote_copy(src_ref=src, dst_ref=dst, send_sem=send_sems.at[sem],
                                                    recv_sem=recv_sems.at[recv], device_id=sibling, device_id_type=MESH)

            def tile_copy(tile, core):
                chip, row, start, rows = _sibling_rows(tile, core)
                return to_sibling(tile % 2, stage.at[tile % 2, pl.ds(0, rows), :], got_ref.at[chip, pl.ds(row, rows), :])

            rest_copy = to_sibling(2, _half_rows(rest_ref, 1 - c), got_rest_ref, recv=1)

            @pl.when((pl.program_id(0) == 0) & (k == 0))
            def _():
                rest_copy.start()

            for tile in range(step):
                for core in range(2):
                    @pl.when((pl.program_id(0) == tile) & (k == halves - 1) & (c == core))
                    def _(tile=tile, core=core):
                        if tile >= 2 and _sibling_rows(tile - 2, core):
                            tile_copy(tile - 2, core).wait_send()
                        if _sibling_rows(tile, core):
                            _, _, start, rows = _sibling_rows(tile, core)
                            stage[tile % 2, 0:rows, :] = o_ref[start:start + rows, :]
                            tile_copy(tile, core).start()
                        if tile == step - 1:
                            for last in (step - 2, step - 1):
                                if _sibling_rows(last, core):
                                    tile_copy(last, core).wait_send()
                            rest_copy.wait()
                            to_sibling(0, got_ref, got_ref).wait_recv()

    def cols_spec(piece, n, first, steps):
        def index(j, k):
            return jnp.clip(j - first, 0, steps - 1), jnp.where((j >= first) & (j < first + steps), k, 0), 0
        return pl.BlockSpec((n, SEQ // halves, piece.shape[2]), index)

    tile_spec = pl.BlockSpec((W_BLOCK, D_MODEL), lambda j, k: (j, 0))
    in_specs = [cols_spec(p, *pl_) for p, pl_ in zip(d_proj, plan)] + [
        pl.BlockSpec((SEQ // halves, D_MODEL), lambda j, k: (k, 0))]
    acc = pltpu.VMEM((W_BLOCK, D_MODEL), F32)
    if not hand_over:
        return pl.pallas_call(
            body, name="d_w_in", grid=(step, halves), in_specs=in_specs, out_specs=tile_spec,
            out_shape=jax.ShapeDtypeStruct((IN_WIDTH, D_MODEL), BF16), scratch_shapes=[acc],
            compiler_params=_params("arbitrary", "arbitrary"),
        )(*d_proj, h)
    return pl.pallas_call(
        body, name="d_w_in", grid=(step, halves), in_specs=in_specs + [ANY], out_specs=[tile_spec, ANY, ANY],
        out_shape=[jax.ShapeDtypeStruct((IN_WIDTH, D_MODEL), BF16),
                   jax.ShapeDtypeStruct((N_CHIPS, half, D_MODEL), BF16),
                   jax.ShapeDtypeStruct((N_CHIPS, rest.shape[1] // 2, D_MODEL), BF16)],
        scratch_shapes=[acc, pltpu.SemaphoreType.DMA((3,)), pltpu.SemaphoreType.DMA((2,)),
                        pltpu.VMEM((2, W_BLOCK, D_MODEL), BF16)],
        compiler_params=_params("arbitrary", "arbitrary"),
    )(*d_proj, h, rest)


RELAY_STEP = 10
RELAY_ROWS = 352


def _d_x(d_proj, w_t, x, gain, dy, chip_sums):
    tm = 256
    n_steps = SEQ // tm
    n_w = IN_WIDTH // W_BLOCK
    n_p, n_s = len(d_proj), len(chip_sums)

    def body(*refs):
        pieces, w_refs = refs[:n_p], refs[n_p:n_p + n_w]
        x_ref, g_ref, dy_ref = refs[n_p + n_w:n_p + n_w + 3]
        q_refs = refs[n_p + n_w + 3:n_p + n_w + 3 + n_s]
        dx_ref, dgain_ref = refs[n_p + n_w + 3 + n_s:n_p + n_w + 5 + n_s]
        outs = refs[n_p + n_w + 5 + n_s:n_p + n_w + 5 + 4 * n_s]
        got_refs, relay_refs, sum_refs = outs[:n_s], outs[n_s:2 * n_s], outs[2 * n_s:]
        if n_s:
            send_sems, recv_sems, local_sems, a_buf, b_buf, c_buf = refs[n_p + n_w + 5 + 4 * n_s:]

        def hops():
            cx, cy, c = lax.axis_index("x"), lax.axis_index("y"), lax.axis_index("c")
            near = (cx + (1 - c) - 2 * cx * (1 - c), cy + c - 2 * cy * c)
            far = (cx + c - 2 * cx * c, cy + (1 - c) - 2 * cy * (1 - c))
            chip = lambda p: 2 * p[0] + p[1]

            def copy(k, src, dst, to):
                return pltpu.make_async_remote_copy(src_ref=src, dst_ref=dst, send_sem=send_sems.at[k],
                                                    recv_sem=recv_sems.at[k], device_id=(*to, c), device_id_type=MESH)

            first = [(copy(3 * b, q.at[chip(near)], got.at[0], near),
                      copy(3 * b + 1, q.at[3 - chip((cx, cy))], relay, near))
                     for b, (q, got, relay) in enumerate(zip(q_refs, got_refs, relay_refs))]
            second = [copy(3 * b + 2, s, got.at[1], far) for b, (s, got) in enumerate(zip(sum_refs, got_refs))]
            return first, second, chip(far)

        @pl.when(pl.program_id(0) == 0)
        def _():
            dgain_ref[...] = jnp.zeros_like(dgain_ref)
            if n_s:
                for direct, pass_on in hops()[0]:
                    direct.start()
                    pass_on.start()

        if n_s:
            @pl.when(pl.program_id(0) == RELAY_STEP)
            def _():
                first, second, far_chip = hops()
                for b, (q, relay, total) in enumerate(zip(q_refs, relay_refs, sum_refs)):
                    first[b][1].wait_recv()
                    half = relay.shape[0]
                    for r0 in range(0, half, RELAY_ROWS):
                        rows = min(RELAY_ROWS, half - r0)
                        mine = pltpu.make_async_copy(q.at[far_chip, pl.ds(r0, rows), :], a_buf.at[pl.ds(0, rows), :],
                                                     local_sems.at[0])
                        theirs = pltpu.make_async_copy(relay.at[pl.ds(r0, rows), :], b_buf.at[pl.ds(0, rows), :],
                                                       local_sems.at[1])
                        mine.start()
                        theirs.start()
                        mine.wait()
                        theirs.wait()
                        c_buf[0:rows, :] = (a_buf[0:rows, :].astype(F32) + b_buf[0:rows, :].astype(F32)).astype(BF16)
                        store = pltpu.make_async_copy(c_buf.at[pl.ds(0, rows), :], total.at[pl.ds(r0, rows), :],
                                                      local_sems.at[2])
                        store.start()
                        store.wait()
                    second[b].start()

        blocks = [(piece, k) for piece in pieces for k in range(piece.shape[0])]
        dh, group, width, blk = None, [], 0, 0
        for piece, k in blocks:
            group.append(piece[k])
            width += piece.shape[2]
            if width == W_BLOCK:
                term = _dot(jnp.concatenate(group, axis=1), w_refs[blk][...])
                dh = term if dh is None else dh + term
                group, width, blk = [], 0, blk + 1
        assert not group and blk == n_w
        xf = x_ref[...]
        r = lax.rsqrt(jnp.mean(xf * xf, axis=-1, keepdims=True) + EPS)
        xh = xf * r
        dxh = dh * g_ref[...]
        dx_ref[...] = r * (dxh - xh * jnp.mean(dxh * xh, axis=-1, keepdims=True)) + dy_ref[...]
        dgain_ref[...] += _rows8(jnp.sum(dh * xh, axis=0, keepdims=True))

        if n_s:
            @pl.when(pl.program_id(0) == n_steps - 1)
            def _():
                first, second, _ = hops()
                for direct, pass_on in first:
                    direct.wait()
                    pass_on.wait_send()
                for cp in second:
                    cp.wait()

    row = pl.BlockSpec((tm, D_MODEL), lambda i: (i, 0))
    halves = [q.shape[1] for q in chip_sums]
    res = pl.pallas_call(
        body, name="d_x", grid=(n_steps,),
        in_specs=[pl.BlockSpec((p.shape[0], tm, p.shape[2]), lambda i: (0, i, 0)) for p in d_proj] + _w_blocks(0, n_w)
        + [row, pl.BlockSpec((1, D_MODEL), lambda i: (0, 0)), row] + [ANY] * n_s,
        out_specs=[row, pl.BlockSpec((8, D_MODEL), lambda i: (0, 0))] + [ANY] * (3 * n_s),
        out_shape=[jax.ShapeDtypeStruct((SEQ, D_MODEL), F32), jax.ShapeDtypeStruct((8, D_MODEL), F32)]
        + [jax.ShapeDtypeStruct((2, half, D_MODEL), BF16) for half in halves]
        + [jax.ShapeDtypeStruct((half, D_MODEL), BF16) for half in halves] * 2,
        scratch_shapes=[pltpu.SemaphoreType.DMA((3 * n_s,)), pltpu.SemaphoreType.DMA((3 * n_s,)),
                        pltpu.SemaphoreType.DMA((3,))] + [pltpu.VMEM((RELAY_ROWS, D_MODEL), BF16)] * 3 if n_s else [],
        compiler_params=_params("arbitrary"),
    )(*d_proj, *([w_t] * n_w), x, gain, dy, *chip_sums)
    return res[0], res[1], res[2:2 + n_s]


def _my_place():
    x, y, c = lax.axis_index("x"), lax.axis_index("y"), lax.axis_index("c")
    return jnp.stack([2 * x + y, c]).astype(jnp.int32)


def _half_rows(ref, half):
    rows = ref.shape[-2] // 2
    idx = (slice(None),) * (len(ref.shape) - 2) + (pl.ds(pl.multiple_of(half * rows, 16), rows), slice(None))
    return ref.at[idx]


def _add_halves(place, grads, theirs, name):
    half = theirs.shape[1]
    tr = _row_tile(half)
    n = half // tr

    def body(place_ref, g_ref, t_ref, o_ref):
        o_ref[...] = (g_ref[...].astype(F32) + t_ref[...].astype(F32)).astype(BF16)

    return pl.pallas_call(
        body, name=name,
        grid_spec=pltpu.PrefetchScalarGridSpec(
            num_scalar_prefetch=1, grid=(N_CHIPS, n),
            in_specs=[pl.BlockSpec((None, tr, D_MODEL), lambda s, i, p: (s, p[1] * n + i, 0)),
                      pl.BlockSpec((None, tr, D_MODEL), lambda s, i, p: (s, i, 0))],
            out_specs=pl.BlockSpec((None, tr, D_MODEL), lambda s, i, p: (s, i, 0))),
        out_shape=jax.ShapeDtypeStruct((N_CHIPS, half, D_MODEL), BF16),
        compiler_params=_params("arbitrary", "arbitrary"),
    )(place, grads, theirs)


def _add_chips(place, chip_sums, others, name):
    half = others.shape[1]
    tr = _row_tile(half)
    n = half // tr

    def body(place_ref, q_ref, o_ref, r_ref):
        acc = q_ref[...].astype(F32)
        for j in range(others.shape[0]):
            acc = acc + o_ref[j].astype(F32)
        r_ref[...] = acc

    return pl.pallas_call(
        body, name=name,
        grid_spec=pltpu.PrefetchScalarGridSpec(
            num_scalar_prefetch=1, grid=(n,),
            in_specs=[pl.BlockSpec((None, tr, D_MODEL), lambda i, p: (p[0], i, 0)),
                      pl.BlockSpec((others.shape[0], tr, D_MODEL), lambda i, p: (0, i, 0))],
            out_specs=pl.BlockSpec((tr, D_MODEL), lambda i, p: (p[1] * n + i, 0))),
        out_shape=jax.ShapeDtypeStruct((2 * half, D_MODEL), F32),
        compiler_params=_params("arbitrary"),
    )(place, chip_sums, others)


def _join_halves(shards, block):
    n = len(shards)
    rows = block.shape[0]

    def body(*refs):
        b_ref, o_refs, sum_ref = refs[n], refs[n + 1:2 * n + 1], refs[2 * n + 1]
        send_sems, recv_sems, small_send, small_recv, local_sem, all_ref = refs[2 * n + 2:]
        x, y, c = lax.axis_index("x"), lax.axis_index("y"), lax.axis_index("c")
        me, sibling = (x, y, c), (x, y, 1 - c)
        chips = [(1 - x, y), (x, 1 - y), (1 - x, 1 - y)]

        def half(k, rows_ref):
            return pltpu.make_async_remote_copy(src_ref=rows_ref, dst_ref=rows_ref, send_sem=send_sems.at[k],
                                                recv_sem=recv_sems.at[k], device_id=sibling, device_id_type=MESH)

        def at(px, py, pc):
            return all_ref.at[pl.ds(pl.multiple_of((4 * px + 2 * py + pc) * rows, 8), rows), :]

        def small(k, block_of, to, src=None):
            return pltpu.make_async_remote_copy(src_ref=at(*block_of) if src is None else src, dst_ref=at(*block_of),
                                                send_sem=small_send.at[k], recv_sem=small_recv.at[k],
                                                device_id=to, device_id_type=MESH)

        sends = [half(k, _half_rows(o, c)) for k, o in enumerate(o_refs)]
        for cp in sends:
            cp.start()
        mine = pltpu.make_async_copy(b_ref, at(*me), local_sem)
        mine.start()
        first = [small(0, me, sibling, src=b_ref)]
        first += [small(1 + j, me, (*chip, c), src=b_ref) for j, chip in enumerate(chips)]
        for cp in first:
            cp.start()
        passed = [small(4 + j, (*chip, c), sibling) for j, chip in enumerate(chips)]
        for j, chip in enumerate(chips):
            small(1 + j, (*chip, c), me).wait_recv()
            passed[j].start()
        small(0, sibling, me).wait_recv()
        for j, chip in enumerate(chips):
            small(4 + j, (*chip, 1 - c), me).wait_recv()
        mine.wait()
        acc = all_ref[0:rows, :]
        for dev in range(1, 8):
            acc = acc + all_ref[rows * dev:rows * (dev + 1), :]
        sum_ref[...] = acc
        for k, o in enumerate(o_refs):
            half(k, _half_rows(o, 1 - c)).wait_recv()
        for cp in sends + first + passed:
            cp.wait_send()

    res = pl.pallas_call(
        body, name="reduce_join_halves", in_specs=[ANY] * n + [pl.BlockSpec(memory_space=pltpu.VMEM)],
        out_specs=[ANY] * n + [pl.BlockSpec(memory_space=pltpu.VMEM)],
        out_shape=[jax.ShapeDtypeStruct(s.shape, F32) for s in shards] + [jax.ShapeDtypeStruct(block.shape, F32)],
        input_output_aliases={k: k for k in range(n)},
        scratch_shapes=[pltpu.SemaphoreType.DMA((n,)), pltpu.SemaphoreType.DMA((n,)),
                        pltpu.SemaphoreType.DMA((7,)), pltpu.SemaphoreType.DMA((7,)), pltpu.SemaphoreType.DMA,
                        pltpu.VMEM((8 * rows, D_MODEL), F32)],
    )(*shards, block)
    return res[:n], res[n]


def _adamw_math(w, g, m, v):
    m = ADAM_B1 * m + (1.0 - ADAM_B1) * g
    v = ADAM_B2 * v + (1.0 - ADAM_B2) * (g * g)
    m_hat = m / (1.0 - ADAM_B1 ** ADAM_STEP)
    v_hat = v / (1.0 - ADAM_B2 ** ADAM_STEP)
    return -ADAM_LR * (m_hat / (jnp.sqrt(v_hat) + ADAM_EPS) + ADAM_WD * w), m, v


def _adamw(w, g, m, v, name):
    r, c = w.shape
    tr = _row_tile(r)

    def body(w_ref, g_ref, m_ref, v_ref, d_ref, nm_ref, nv_ref):
        d_ref[...], nm_ref[...], nv_ref[...] = _adamw_math(w_ref[...], g_ref[...], m_ref[...], v_ref[...])

    spec = pl.BlockSpec((tr, c), lambda i: (i, 0))
    return pl.pallas_call(
        body, name=name, grid=(r // tr,), in_specs=[spec] * 4, out_specs=[spec] * 3,
        out_shape=[jax.ShapeDtypeStruct((r, c), F32)] * 3, compiler_params=_params("arbitrary"),
    )(w, g, m, v)


def _adamw_small(ws, gs, ms, vs):
    n = len(ws)

    def body(*refs):
        ins, outs = refs[:4 * n], refs[4 * n:]
        for k in range(n):
            d, m, v = _adamw_math(ins[k][...], ins[n + k][...], ins[2 * n + k][...], ins[3 * n + k][...])
            outs[k][...], outs[n + k][...], outs[2 * n + k][...] = d, m, v

    shapes = [jax.ShapeDtypeStruct(w.shape, F32) for w in ws]
    res = pl.pallas_call(body, name="adamw_small", out_shape=shapes * 3)(*ws, *gs, *ms, *vs)
    return res[:n], res[n:2 * n], res[2 * n:]


def _fold_heads(partials):
    t = jnp.sum(partials[:, 0, :], axis=0)
    return (t[:HEAD_DIM] + t[HEAD_DIM:]).reshape(1, HEAD_DIM)


def _local_step(x, target, norm_gain, w_t, w_a, w_b, w_o, b_m, q_norm_a, k_norm_a, q_norm_b, k_norm_b, sink_a,
                rel_bias, start_reduce=None, small_shard=None):
    two = lambda gain: jnp.concatenate([gain, gain], axis=1)
    bias_a = _bias_lines(rel_bias[:, :8], A_HALF_WINDOW, 1)
    bias_b = jnp.concatenate([_bias_lines(rel_bias[:, 8 + 8 * g:16 + 8 * g], B_HALF_WINDOW, d)
                              for g, d in enumerate(B_DILATIONS)], axis=0)

    qkv, h, *small_all = _in_proj(x, norm_gain, w_t, 0, QKV_WIDTH // W_BLOCK, BF16, "in_proj_qkv", True, small_shard)
    if small_shard is not None:
        w_a, w_b, w_o, b_m = _unpack_weights(small_all[0])
    gates, = _in_proj(x, norm_gain, w_t, QKV_WIDTH // W_BLOCK, GATE_WIDTH // W_BLOCK, F32, "in_proj_gates", False)
    out_a, lse_a = _attn_a_fwd(qkv, two(q_norm_a), two(k_norm_a), bias_a, sink_a)
    out_b, lse_b = _attn_b_fwd(qkv, two(q_norm_b), two(k_norm_b), bias_b)

    dy, dgates, d_out_a, d_out_b, delta_a, delta_b, d_wa, d_wb, d_wo, d_bm, sq = _middle(
        out_a, out_b, gates, x, target, w_a, w_b, w_o, b_m)
    loss = (0.5 / D_MODEL) * jnp.sum(sq)

    dq_a, dkv_a, dgq_a, dgk_a, ds_a, dsink = _attn_a_bwd(
        qkv, two(q_norm_a), two(k_norm_a), bias_a, sink_a, delta_a, lse_a, d_out_a)
    dq_b, dk_b, dv_b, dgq_b, dgk_b, ds_b = _attn_b_bwd(
        qkv, two(q_norm_b), two(k_norm_b), bias_b, delta_b, lse_b, d_out_b)
    d_proj = (dq_a, dkv_a, dq_b, dk_b, dv_b, dgates)

    d_bm_rows = jnp.pad(d_bm.reshape(2, N_CHIPS, 256).transpose(1, 0, 2),
                        ((0, 0), (0, REST_ROWS - 514), (0, D_MODEL - 256)))
    rest = jnp.concatenate([d_wo.reshape(N_CHIPS, 256, D_MODEL), d_wa.reshape(N_CHIPS, 128, D_MODEL),
                            d_wb.reshape(N_CHIPS, 128, D_MODEL), d_bm_rows], axis=1)
    if start_reduce is None:
        grads, chip_sums = [_d_w_in(d_proj, h).reshape(N_CHIPS, W_IN_SHARD, D_MODEL), rest], []
    else:
        d_wt, *theirs = _d_w_in(d_proj, h, rest.astype(BF16))
        grads = [d_wt.reshape(N_CHIPS, W_IN_SHARD, D_MODEL), rest]
        chip_sums = start_reduce(grads, theirs)
    grad_x, d_gain, others = _d_x(d_proj, w_t, x, norm_gain, dy, chip_sums)

    d_rel = jnp.concatenate(
        [_bias_grad(ds_a, A_HALF_WINDOW, 1)]
        + [_bias_grad(ds_b[4 * g:4 * g + 4], B_HALF_WINDOW, d) for g, d in enumerate(B_DILATIONS)], axis=1)
    d_sink = jnp.sum(dsink, axis=(2, 3)).reshape(1, 8)
    dgk_a_row = dgk_a[0]
    small = jnp.zeros((8, D_MODEL), F32)
    small = small.at[0].set(d_gain[0])
    small = small.at[1].set(d_rel.reshape(-1))
    misc = jnp.concatenate([_fold_heads(dgq_a), (dgk_a_row[:HEAD_DIM] + dgk_a_row[HEAD_DIM:]).reshape(1, HEAD_DIM),
                            _fold_heads(dgq_b), _fold_heads(dgk_b), d_sink], axis=1)
    small = small.at[2, :264].set(misc[0])

    return loss, grad_x, grads, small, chip_sums, others


def _unpack_weights(small_all):
    sm = small_all.reshape(N_CHIPS, SMALL_ROWS, D_MODEL)
    w_o = sm[:, 0:256].reshape(D_MODEL, D_MODEL)
    w_a = sm[:, 256:384].reshape(N_CHIPS, 512, 256).transpose(1, 0, 2).reshape(512, D_MODEL)
    w_b = sm[:, 384:512].reshape(N_CHIPS, 512, 256).transpose(1, 0, 2).reshape(512, D_MODEL)
    b_m = lax.bitcast_convert_type(sm[:, 512].reshape(N_CHIPS, 2, 256, 2), F32)
    return w_a, w_b, w_o, b_m.transpose(1, 0, 2).reshape(2, D_MODEL)


def _pack_small_weights(w_branch_a, w_branch_b, b_merge, w_out):
    b_m = jnp.pad(lax.bitcast_convert_type(b_merge, BF16).reshape(1, D_MODEL), ((0, SMALL_ROWS - 513), (0, 0)))
    return jnp.concatenate([w_out.astype(BF16), w_branch_a.astype(BF16).reshape(128, D_MODEL),
                            w_branch_b.astype(BF16).reshape(128, D_MODEL), b_m], axis=0)


def kernel(x, norm_gain, w_in, q_norm_a, k_norm_a, q_norm_b, k_norm_b, sink_a, rel_bias, w_branch_a, w_branch_b, b_merge, w_out, loss_target, m_norm_gain, m_w_in, m_q_norm_a, m_k_norm_a, m_q_norm_b, m_k_norm_b, m_sink_a, m_rel_bias, m_w_branch_a, m_w_branch_b, m_b_merge, m_w_out, v_norm_gain, v_w_in, v_q_norm_a, v_k_norm_a, v_q_norm_b, v_k_norm_b, v_sink_a, v_rel_bias, v_w_branch_a, v_w_branch_b, v_b_merge, v_w_out):
    w_in_t, m_w_in_t, v_w_in_t = (jnp.transpose(t[0]) for t in (w_in, m_w_in, v_w_in))
    wt_shard = _cast_rows(w_in_t, BF16, "w_in_cast")
    w_t = _gather_weights(wt_shard)
    small_shard = _pack_small_weights(w_branch_a[0], w_branch_b[0], b_merge[0], w_out[0])

    place = _my_place()
    names = ("w_in", "rest")

    def start_reduce(grads, theirs):
        return [_add_halves(place, g, t, "reduce_add_halves_" + n) for g, t, n in zip(grads, theirs, names)]

    loss_part, grad_x, _, small, chip_sums, others = _local_step(
        x[0], loss_target[0], norm_gain, w_t, None, None, None, None, q_norm_a, k_norm_a, q_norm_b, k_norm_b,
        sink_a, rel_bias, start_reduce, small_shard)

    (g_wt, g_rest), small = _join_halves(
        [_add_chips(place, q, o, "reduce_add_chips_" + n) for q, o, n in zip(chip_sums, others, names)],
        small.at[3, 0].set(loss_part))
    loss = small[3, 0]

    g_w_out = g_rest[0:256]
    g_w_a = g_rest[256:384].reshape(512, 256)
    g_w_b = g_rest[384:512].reshape(512, 256)
    g_b_merge = g_rest[512:514, :256]
    g_norm_gain = small[0:1]
    g_rel_bias = small[1].reshape(N_BUCKETS, N_BUCKETS)
    g_q_a, g_k_a, g_q_b, g_k_b = (small[2:3, 64 * k:64 * k + 64] for k in range(4))
    g_sink = small[2:3, 256:264]

    big_names = (("w_branch_a", w_branch_a, g_w_a, m_w_branch_a, v_w_branch_a),
                 ("w_branch_b", w_branch_b, g_w_b, m_w_branch_b, v_w_branch_b),
                 ("w_out", w_out, g_w_out, m_w_out, v_w_out))
    upd = {name: (g,) + tuple(_adamw(w[0], g, m[0], v[0], "adamw_" + name)) for name, w, g, m, v in big_names}
    upd["w_in"] = tuple(jnp.transpose(t) for t in (g_wt,) + tuple(_adamw(w_in_t, g_wt, m_w_in_t, v_w_in_t, "adamw_w_in")))
    small_names = ("norm_gain", "q_norm_a", "k_norm_a", "q_norm_b", "k_norm_b", "sink_a", "rel_bias", "b_merge")
    ws = [norm_gain, q_norm_a, k_norm_a, q_norm_b, k_norm_b, sink_a, rel_bias, b_merge[0]]
    gs = [g_norm_gain, g_q_a, g_k_a, g_q_b, g_k_b, g_sink, g_rel_bias, g_b_merge]
    ms = [m_norm_gain, m_q_norm_a, m_k_norm_a, m_q_norm_b, m_k_norm_b, m_sink_a, m_rel_bias, m_b_merge[0]]
    vs = [v_norm_gain, v_q_norm_a, v_k_norm_a, v_q_norm_b, v_k_norm_b, v_sink_a, v_rel_bias, v_b_merge[0]]
    ds, nms, nvs = _adamw_small(ws, gs, ms, vs)
    for k, name in enumerate(small_names):
        upd[name] = (gs[k], ds[k], nms[k], nvs[k])

    order = ("norm_gain", "w_in", "q_norm_a", "k_norm_a", "q_norm_b", "k_norm_b", "sink_a", "rel_bias",
             "w_branch_a", "w_branch_b", "b_merge", "w_out")
    lead = {"w_in", "w_branch_a", "w_branch_b", "b_merge", "w_out"}
    outs = [loss, grad_x[None]]
    for part in range(4):
        outs += [upd[name][part][None] if name in lead else upd[name][part] for name in order]
    return tuple(outs)
```

```python
import math

import numpy as np
import jax
import jax.numpy as jnp
from jax import lax
from jax.experimental import pallas as pl
from jax.experimental.pallas import tpu as pltpu

F32 = jnp.float32
BF16 = jnp.bfloat16

SEQ = 4096
D_MODEL = 1024
HEAD_DIM = 64
LANES = 128
EPS = 1e-6
NEG_INF = -1e30
SCALE = HEAD_DIM ** -0.5
N_BUCKETS = 32
MAX_DISTANCE = 1024
N_CHIPS = 4

A_HALF_WINDOW = 128
B_HALF_WINDOW = 64
B_DILATIONS = (1, 4, 16)
Q_BLOCK = 128

QKV_WIDTH = 5376
GATE_WIDTH = 3072
QA_BLK, KA_BLK, VA_BLK = 0, 4, 5
QB_BLK, KB_BLK, VB_BLK = 6, 18, 30
IN_WIDTH = QKV_WIDTH + GATE_WIDTH
W_IN_SHARD = IN_WIDTH // N_CHIPS

SMALL_ROWS = 544
REST_ROWS = 544

ADAM_LR = 0.001
ADAM_B1 = 0.9
ADAM_B2 = 0.999
ADAM_EPS = 1e-08
ADAM_WD = 0.01
ADAM_STEP = 10

VMEM_LIMIT = 56 * 1024 * 1024

NT = (((1,), (1,)), ((), ()))
TN = (((0,), (0,)), ((), ()))
MESH = pl.DeviceIdType.MESH
ANY = pl.BlockSpec(memory_space=pl.ANY)


def _dot(a, b, dims=None):
    if dims is None:
        return jnp.dot(a, b, preferred_element_type=F32)
    return lax.dot_general(a, b, dims, preferred_element_type=F32)


def _params(*semantics):
    return pltpu.CompilerParams(dimension_semantics=semantics or None, vmem_limit_bytes=VMEM_LIMIT)


def _line_width(half_window):
    return pl.cdiv(2 * Q_BLOCK + 2 * half_window - 1, LANES) * LANES


def _bucket_onehot(half_window, stride):
    rel = np.arange(_line_width(half_window)) - (Q_BLOCK - 1) - half_window
    band = np.abs(rel) <= half_window
    rel = rel * stride
    half, max_exact = N_BUCKETS // 2, N_BUCKETS // 4
    n = np.abs(rel)
    nf = np.maximum(n, max_exact).astype(np.float32)
    large = max_exact + (np.log(nf / np.float32(max_exact)) / np.float32(math.log(MAX_DISTANCE / max_exact))
                         * np.float32(half - max_exact)).astype(np.int32)
    large = np.minimum(large, half - 1)
    bucket = (rel > 0).astype(np.int32) * half + np.where(n < max_exact, n, large)
    onehot = (bucket[..., None] == np.arange(N_BUCKETS)) & band[..., None]
    return onehot.astype(np.float32), band


def _bias_lines(rel_bias_cols, half_window, stride):
    onehot, band = _bucket_onehot(half_window, stride)
    h = rel_bias_cols.shape[1]
    t = jnp.einsum("tb,bh->ht", jnp.asarray(onehot), rel_bias_cols, precision=lax.Precision.HIGHEST)
    t = t + jnp.asarray(np.where(band, 0.0, NEG_INF).astype(np.float32))
    return t.reshape(h // 2, 2, -1)


def _bias_grad(d_lines, half_window, stride):
    onehot, _ = _bucket_onehot(half_window, stride)
    h = d_lines.shape[0] * 2
    return jnp.einsum("tb,ht->bh", jnp.asarray(onehot), d_lines.reshape(h, -1), precision=lax.Precision.HIGHEST)


def _unroll_bias(line_ref, tile_ref, w):
    width = line_ref.shape[1]
    for j in range(2):
        rows = jnp.broadcast_to(line_ref[j:j + 1, :], (Q_BLOCK, width))
        rows = pltpu.roll(rows, width - (Q_BLOCK - 1), 1, stride=1, stride_axis=0)
        tile_ref[j * Q_BLOCK:(j + 1) * Q_BLOCK, :] = rows[:, :w]


def _fold_bias_grad(tile_ref, line_ref, w):
    width = line_ref.shape[1]
    row = lax.broadcasted_iota(jnp.int32, (Q_BLOCK, Q_BLOCK), 0)
    col = lax.broadcasted_iota(jnp.int32, (Q_BLOCK, Q_BLOCK), 1)
    flip = jnp.where(row + col == Q_BLOCK - 1, 1.0, 0.0).astype(BF16)
    for j in range(2):
        tile = tile_ref[j * Q_BLOCK:(j + 1) * Q_BLOCK, :]
        hi = tile.astype(BF16)
        lo = (tile - hi.astype(F32)).astype(BF16)
        rows = _dot(flip, hi) + _dot(flip, lo)
        rows = jnp.concatenate([rows, jnp.zeros((Q_BLOCK, width - w), F32)], axis=1)
        rows = pltpu.roll(rows, 0, 1, stride=1, stride_axis=0)
        line_ref[j:j + 1, :] = jnp.sum(rows, axis=0, keepdims=True)


def _row_tile(rows):
    return max(t for t in range(16, 385, 16) if rows % t == 0)


def _cast_rows(w, out_dtype, name):
    r, c = w.shape
    tr = _row_tile(r)

    def body(w_ref, o_ref):
        o_ref[...] = w_ref[...].astype(out_dtype)

    spec = pl.BlockSpec((tr, c), lambda i: (i, 0))
    return pl.pallas_call(
        body, name=name, grid=(r // tr,), in_specs=[spec], out_specs=spec,
        out_shape=jax.ShapeDtypeStruct((r, c), out_dtype), compiler_params=_params("arbitrary"),
    )(w)


STAGE_ROWS = 528


def _gather_scratch():
    return [pltpu.SemaphoreType.DMA((12,)), pltpu.SemaphoreType.DMA((12,)), pltpu.SemaphoreType.DMA((2,)),
            pltpu.SemaphoreType.DMA((2,)), pltpu.VMEM((2, STAGE_ROWS, D_MODEL), BF16)]


def _gather_phases(src_ref, out_ref, send_sems, recv_sems, in_sems, out_sems, stage):
    rows = src_ref.shape[0]
    x, y, c = lax.axis_index("x"), lax.axis_index("y"), lax.axis_index("c")
    sibling = (x, y, 1 - c)
    near = (x + (1 - c) - 2 * x * (1 - c), y + c - 2 * y * c)
    far = (x + c - 2 * x * c, y + (1 - c) - 2 * y * (1 - c))
    diag = (1 - x, 1 - y)
    chip_no = lambda chip: 2 * chip[0] + chip[1]
    my_chip = chip_no((x, y))

    pieces = 2 if (rows // 2) % 32 == 0 else 1
    n = rows // 2 // pieces

    def half_of(chip, half, p):
        start = pl.multiple_of(chip * rows + half * (rows // 2) + p * n, 16)
        return out_ref.at[pl.ds(start, n), :]

    def copy(k, p, src, dst, to):
        return pltpu.make_async_remote_copy(src_ref=src, dst_ref=dst, send_sem=send_sems.at[k * pieces + p],
                                            recv_sem=recv_sems.at[k * pieces + p], device_id=to, device_id_type=MESH)

    def mine(p):
        return src_ref.at[pl.ds(pl.multiple_of(c * (rows // 2) + p * n, 16), n), :]

    def keep_own():
        outs = []
        for i, r0 in enumerate(range(0, rows, STAGE_ROWS)):
            n = min(STAGE_ROWS, rows - r0)
            slot = i % 2
            if i >= 2:
                outs[i - 2].wait()
            buf = stage.at[slot, pl.ds(0, n), :]
            load = pltpu.make_async_copy(src_ref.at[pl.ds(r0, n), :], buf, in_sems.at[slot])
            load.start()
            load.wait()
            start = pl.multiple_of(my_chip * rows + r0, 16)
            outs.append(pltpu.make_async_copy(buf, out_ref.at[pl.ds(start, n), :], out_sems.at[slot]))
            outs[i].start()
        for cp in outs[-2:]:
            cp.wait()

    def start():
        for p in range(pieces):
            copy(0, p, mine(p), half_of(my_chip, c, p), (*near, c)).start()
            copy(1, p, mine(p), half_of(my_chip, c, p), (*far, c)).start()
        keep_own()

    def pass_on(j, p, chip):
        landed = half_of(chip_no(chip), c, p)
        copy(3 + j, p, landed, landed, sibling).start()

    def relay():
        for p in range(pieces):
            landed = half_of(chip_no(near), c, p)
            copy(0, p, landed, landed, sibling).wait_recv()
            copy(2, p, landed, landed, (*far, c)).start()
            pass_on(0, p, near)

    def forward():
        for j, chip in ((1, far), (2, diag)):
            for p in range(pieces):
                landed = half_of(chip_no(chip), c, p)
                copy(j, p, landed, landed, sibling).wait_recv()
                pass_on(j, p, chip)

    def finish():
        for j, chip in ((0, far), (1, near), (2, diag)):
            for p in range(pieces):
                other = half_of(chip_no(chip), 1 - c, p)
                copy(3 + j, p, other, other, sibling).wait_recv()
        for k in range(6):
            for p in range(pieces):
                copy(k, p, mine(p), mine(p), sibling).wait_send()

    return start, relay, forward, finish


def _gather_weights(shard):
    def body(src_ref, out_ref, *scratch):
        for phase in _gather_phases(src_ref, out_ref, *scratch):
            phase()

    return pl.pallas_call(
        body, name="gather_weights", in_specs=[ANY], out_specs=ANY,
        out_shape=jax.ShapeDtypeStruct((N_CHIPS * shard.shape[0], D_MODEL), BF16),
        scratch_shapes=_gather_scratch(),
    )(shard)


W_BLOCK = 768


def _w_blocks(first, count):
    return [pl.BlockSpec((W_BLOCK, D_MODEL), lambda *_, k=k: (first + k, 0)) for k in range(count)]


def _in_proj(x, gain, w_t, first_block, n_blocks, out_dtype, name, keep_h, ride=None):
    tm = 512
    n_steps = SEQ // tm
    n_out = 2 if keep_h else 1

    def body(x_ref, g_ref, *refs):
        w_refs, outs = refs[:n_blocks], refs[n_blocks + (ride is not None):n_blocks + (ride is not None) + n_out]
        if ride is not None:
            phases = _gather_phases(refs[n_blocks], *refs[n_blocks + 1 + n_out:])
            for step, phase in zip((0, 2, 4, n_steps - 1), phases):
                pl.when(pl.program_id(0) == step)(phase)
        xf = x_ref[...]
        r = lax.rsqrt(jnp.mean(xf * xf, axis=-1, keepdims=True) + EPS)
        h = ((xf * r) * g_ref[...]).astype(BF16)
        if keep_h:
            outs[1][...] = h
        for k, w_ref in enumerate(w_refs):
            outs[0][:, k * W_BLOCK:(k + 1) * W_BLOCK] = _dot(h, w_ref[...], NT).astype(out_dtype)

    riding = [] if ride is None else [ride]
    return pl.pallas_call(
        body, name=name, grid=(n_steps,),
        in_specs=[pl.BlockSpec((tm, D_MODEL), lambda i: (i, 0)), pl.BlockSpec((1, D_MODEL), lambda i: (0, 0))]
        + _w_blocks(first_block, n_blocks) + [ANY for _ in riding],
        out_specs=[pl.BlockSpec((tm, W_BLOCK * n_blocks), lambda i: (i, 0)),
                   pl.BlockSpec((tm, D_MODEL), lambda i: (i, 0))][:n_out] + [ANY for _ in riding],
        out_shape=[jax.ShapeDtypeStruct((SEQ, W_BLOCK * n_blocks), out_dtype),
                   jax.ShapeDtypeStruct((SEQ, D_MODEL), BF16)][:n_out]
        + [jax.ShapeDtypeStruct((N_CHIPS * r.shape[0], D_MODEL), BF16) for r in riding],
        scratch_shapes=_gather_scratch() if riding else [],
        compiler_params=_params("arbitrary"),
    )(x, gain, *([w_t] * n_blocks), *riding)


CHUNK = 256
CHUNK_UNROLL = 4
TILE_UNROLL = 8


def _low_half():
    return lax.broadcasted_iota(jnp.int32, (1, LANES), 1) < HEAD_DIM


def _half_sum(v, low):
    del low
    row = lax.broadcasted_iota(jnp.int32, (2 * LANES, LANES), 0)
    col = lax.broadcasted_iota(jnp.int32, (2 * LANES, LANES), 1)
    ones = jnp.where((row % LANES) // HEAD_DIM == col // HEAD_DIM, 1.0, 0.0).astype(BF16)
    hi = v.astype(BF16)
    lo = (v - hi.astype(F32)).astype(BF16)
    return _dot(jnp.concatenate([hi, lo], axis=1), ones)


def _chunks(fn, init=0):
    def body(i, carry):
        for u in range(CHUNK_UNROLL):
            carry = fn(pl.multiple_of((i * CHUNK_UNROLL + u) * CHUNK, CHUNK), carry)
        return carry

    return lax.fori_loop(0, SEQ // (CHUNK * CHUNK_UNROLL), body, init)


def _inv_rms(t, low):
    del low
    row = lax.broadcasted_iota(jnp.int32, (LANES, LANES), 0)
    col = lax.broadcasted_iota(jnp.int32, (LANES, LANES), 1)
    ones = jnp.where(row // HEAD_DIM == col // HEAD_DIM, 1.0, 0.0).astype(BF16)
    return lax.rsqrt(_dot((t * t).astype(BF16), ones) * (1.0 / HEAD_DIM) + EPS)


def _prep_q(q_ref, gain_ref, qn_ref):
    low = _low_half()

    def step(r0, carry):
        q = q_ref[pl.ds(r0, CHUNK), :].astype(F32)
        qn_ref[pl.ds(r0, CHUNK), :] = ((q * _inv_rms(q, low)) * gain_ref[...]) * SCALE
        return carry

    _chunks(step)


def _own_half(t, keep):
    return jnp.where(keep, t, pltpu.roll(t, HEAD_DIM, 1))


def _prep_kv(k_ref, v_ref, gain_ref, kp_ref, vp_ref, pad, keep=None):
    low = _low_half()
    zeros = jnp.zeros((pad, LANES), F32)
    for ref in (kp_ref, vp_ref):
        ref[pl.ds(0, pad), :] = zeros
        ref[pl.ds(pad + SEQ, pad), :] = zeros

    def step(r0, carry):
        k = k_ref[pl.ds(r0, CHUNK), :].astype(F32)
        v = v_ref[pl.ds(r0, CHUNK), :].astype(F32)
        kn = (k * _inv_rms(k, low)) * gain_ref[...]
        if keep is not None:
            kn, v = _own_half(kn, keep), _own_half(v, keep)
        kp_ref[pl.ds(pad + r0, CHUNK), :] = kn
        vp_ref[pl.ds(pad + r0, CHUNK), :] = v
        return carry

    _chunks(step)


def _tiles(d, half_window, fn, unroll=TILE_UNROLL):
    w = Q_BLOCK + 2 * half_window
    length = SEQ // d
    n_blocks = length // Q_BLOCK
    col = lax.broadcasted_iota(jnp.int32, (1, w), 1)

    def step(it, carry):
        c, n = it // n_blocks, it % n_blocks
        start = c + (d * Q_BLOCK) * n
        if d == 1:
            start = pl.multiple_of(start, Q_BLOCK)
            q_rows, k_rows = pl.ds(start, Q_BLOCK), pl.ds(start, w)
        else:
            q_rows, k_rows = pl.ds(start, Q_BLOCK, stride=d), pl.ds(start, w, stride=d)
        t = n * Q_BLOCK - half_window + col
        edge = jnp.where((t < 0) | (t >= length), NEG_INF, 0.0)
        fn(q_rows, k_rows, edge)
        return carry

    lax.fori_loop(0, d * n_blocks, step, 0, unroll=unroll)


def _stack_heads(t, low):
    return jnp.concatenate([jnp.where(low, t, 0.0), jnp.where(low, 0.0, t)], axis=0).astype(BF16)


def _unstack_heads(t, low):
    return jnp.where(low, t[:Q_BLOCK], t[Q_BLOCK:])


def _per_head(pair):
    return jnp.concatenate([jnp.full((Q_BLOCK, 1), pair[0], F32), jnp.full((Q_BLOCK, 1), pair[1], F32)], axis=0)


def _fwd_tiles(qn_ref, kp_ref, vp_ref, bias_ref, emit, *, d, half_window, sinks=None):
    low = _low_half()
    w = Q_BLOCK + 2 * half_window
    sink = None if sinks is None else _per_head(sinks)

    def tile(q_rows, k_rows, edge):
        q2 = _stack_heads(qn_ref[q_rows, :], low)
        k = kp_ref[k_rows, :].astype(BF16)
        v1 = jnp.concatenate([vp_ref[k_rows, :], jnp.ones((w, LANES), F32)], axis=1).astype(BF16)
        s = _dot(q2, k, NT) + bias_ref[...] + edge
        m = jnp.max(s, axis=-1, keepdims=True)
        if sink is not None:
            m = jnp.maximum(m, sink)
        o = _dot(jnp.exp(s - m).astype(BF16), v1)
        l = o[:, LANES:]
        if sink is not None:
            l = l + jnp.exp(sink - m)
        emit(q_rows, _unstack_heads(o[:, :LANES] * (1.0 / l), low), _unstack_heads(m + jnp.log(l), low))

    _tiles(d, half_window, tile, unroll=2 * TILE_UNROLL)


def _bwd_tiles(qn_ref, kp_ref, vp_ref, bias_ref, do_ref, lse_ref, delta_ref, dq_ref, dk_ref, dv_ref, ds_ref,
               *, d, half_window, sinks=None, dsink_ref=None):
    low = _low_half()
    w = Q_BLOCK + 2 * half_window
    sink = None if sinks is None else _per_head(sinks)

    def rows_of(t):
        return jnp.concatenate([t[:, 0:1], t[:, HEAD_DIM:HEAD_DIM + 1]], axis=0)

    def tile(q_rows, k_rows, edge):
        q2 = _stack_heads(qn_ref[q_rows, :], low)
        do2 = _stack_heads(do_ref[q_rows, :], low)
        k = kp_ref[k_rows, :].astype(BF16)
        v = vp_ref[k_rows, :].astype(BF16)
        lse = rows_of(lse_ref[q_rows, :])
        delta = rows_of(delta_ref[q_rows, :])
        p = jnp.exp(_dot(q2, k, NT) + bias_ref[...] + edge - lse)
        ds = p * (_dot(do2, v, NT) - delta)
        ds_ref[...] += ds
        if sink is not None:
            dsink_ref[...] += (-jnp.exp(sink - lse) * delta).reshape(2, Q_BLOCK, 1)
        dsb, pb = ds.astype(BF16), p.astype(BF16)
        dq_ref[q_rows, :] = _unstack_heads(_dot(dsb, k), low)
        dk_ref[k_rows, :] += _dot(dsb, q2, TN)
        dv_ref[k_rows, :] += _dot(pb, do2, TN)

    _tiles(d, half_window, tile)


def _norm_bwd(raw_ref, gain_ref, dn_ref, dn_offset, out_ref, scale):
    low = _low_half()

    def step(r0, dgain):
        t = raw_ref[pl.ds(r0, CHUNK), :].astype(F32)
        dn = dn_ref[pl.ds(dn_offset + r0, CHUNK), :]
        dth = dn * (gain_ref[...] * scale)
        sums = _half_sum(jnp.concatenate([t * t, dth * t], axis=0), low)
        r = lax.rsqrt(sums[:CHUNK] * (1.0 / HEAD_DIM) + EPS)
        th = t * r
        out_ref[pl.ds(r0, CHUNK), :] = (r * (dth - th * (r * sums[CHUNK:] * (1.0 / HEAD_DIM)))).astype(BF16)
        return dgain + jnp.sum(dn * th, axis=0, keepdims=True) * scale

    return _chunks(step, jnp.zeros((1, LANES), F32))


def _rows8(v):
    return jnp.broadcast_to(v, (8, v.shape[-1]))


A_W = Q_BLOCK + 2 * A_HALF_WINDOW
A_PAD = A_HALF_WINDOW


def _seq_block(col_fn):
    return pl.BlockSpec((SEQ, LANES), col_fn)


def _attn_a_fwd(qkv, gain_q, gain_k, bias, sink):
    def body(sink_ref, q_ref, k_ref, v_ref, gq_ref, gk_ref, line_ref, o_ref, lse_ref, qn_ref, kp_ref, vp_ref,
             bias_ref):
        hp = pl.program_id(0)
        keep = (lax.broadcasted_iota(jnp.int32, (1, LANES), 1) // HEAD_DIM) == hp // 2
        _prep_q(q_ref, gq_ref, qn_ref)
        _prep_kv(k_ref, v_ref, gk_ref, kp_ref, vp_ref, A_PAD, keep)
        _unroll_bias(line_ref, bias_ref, A_W)

        def emit(rows, out, lse):
            o_ref[rows, :] = out
            lse_ref[rows, :] = lse

        _fwd_tiles(qn_ref, kp_ref, vp_ref, bias_ref, emit, d=1, half_window=A_HALF_WINDOW,
                   sinks=(sink_ref[2 * hp], sink_ref[2 * hp + 1]))

    vec = pl.BlockSpec((1, LANES), lambda hp, s: (0, 0))
    return pl.pallas_call(
        body, name="attn_a_fwd",
        grid_spec=pltpu.PrefetchScalarGridSpec(
            num_scalar_prefetch=1, grid=(4,),
            in_specs=[_seq_block(lambda hp, s: (0, QA_BLK + hp)), _seq_block(lambda hp, s: (0, KA_BLK)),
                      _seq_block(lambda hp, s: (0, VA_BLK)), vec, vec,
                      pl.BlockSpec((None, 2, _line_width(A_HALF_WINDOW)), lambda hp, s: (hp, 0, 0))],
            out_specs=[_seq_block(lambda hp, s: (0, hp)), _seq_block(lambda hp, s: (0, hp))],
            scratch_shapes=[pltpu.VMEM((SEQ, LANES), F32), pltpu.VMEM((SEQ + 2 * A_PAD, LANES), F32),
                            pltpu.VMEM((SEQ + 2 * A_PAD, LANES), F32), pltpu.VMEM((2 * Q_BLOCK, A_W), F32)]),
        out_shape=[jax.ShapeDtypeStruct((SEQ, 512), F32)] * 2,
        compiler_params=_params("arbitrary"),
    )(sink.reshape(8), qkv, qkv, qkv, gain_q, gain_k, bias)


def _attn_a_bwd(qkv, gain_q, gain_k, bias, sink, delta, lse, d_out):
    def body(sink_ref, q_ref, k_ref, v_ref, gq_ref, gk_ref, line_ref, delta_ref, lse_ref, do_ref,
             dq_out, dkv_out, dgq_out, dgk_out, dline_out, dsink_out,
             qn_ref, kp_ref, vp_ref, dq_ref, dk_ref, dv_ref, dk_tot, dv_tot, bias_ref, ds_out):
        hp = pl.program_id(0)
        kv_head = hp // 2
        keep = (lax.broadcasted_iota(jnp.int32, (1, LANES), 1) // HEAD_DIM) == kv_head
        _prep_q(q_ref, gq_ref, qn_ref)
        _prep_kv(k_ref, v_ref, gk_ref, kp_ref, vp_ref, A_PAD, keep)
        _unroll_bias(line_ref, bias_ref, A_W)
        dk_ref[...] = jnp.zeros_like(dk_ref)
        dv_ref[...] = jnp.zeros_like(dv_ref)
        ds_out[...] = jnp.zeros_like(ds_out)
        dsink_out[...] = jnp.zeros_like(dsink_out)

        @pl.when(hp == 0)
        def _():
            dk_tot[...] = jnp.zeros_like(dk_tot)
            dv_tot[...] = jnp.zeros_like(dv_tot)

        _bwd_tiles(qn_ref, kp_ref, vp_ref, bias_ref, do_ref, lse_ref, delta_ref, dq_ref, dk_ref, dv_ref, ds_out,
                   d=1, half_window=A_HALF_WINDOW, sinks=(sink_ref[2 * hp], sink_ref[2 * hp + 1]),
                   dsink_ref=dsink_out)
        _fold_bias_grad(ds_out, dline_out, A_W)
        dgq_out[...] = _rows8(_norm_bwd(q_ref, gq_ref, dq_ref, 0, dq_out, SCALE))

        def fold(r0, carry):
            rows = pl.ds(A_PAD + r0, CHUNK)
            for acc, tot in ((dk_ref, dk_tot), (dv_ref, dv_tot)):
                t = acc[rows, :]
                tot[pl.ds(r0, CHUNK), :] += jnp.where(keep, t + pltpu.roll(t, HEAD_DIM, 1), 0.0)
            return carry

        _chunks(fold)

        @pl.when(hp == 3)
        def _():
            dgk_out[...] = _rows8(_norm_bwd(k_ref, gk_ref, dk_tot, 0, dkv_out.at[0], 1.0))
            dkv_out[1] = dv_tot[...].astype(BF16)

    vec = pl.BlockSpec((1, LANES), lambda hp, s: (0, 0))
    seq_f32 = pltpu.VMEM((SEQ, LANES), F32)
    padded = pltpu.VMEM((SEQ + 2 * A_PAD, LANES), F32)
    return pl.pallas_call(
        body, name="attn_a_bwd",
        grid_spec=pltpu.PrefetchScalarGridSpec(
            num_scalar_prefetch=1, grid=(4,),
            in_specs=[_seq_block(lambda hp, s: (0, QA_BLK + hp)), _seq_block(lambda hp, s: (0, KA_BLK)),
                      _seq_block(lambda hp, s: (0, VA_BLK)), vec, vec,
                      pl.BlockSpec((None, 2, _line_width(A_HALF_WINDOW)), lambda hp, s: (hp, 0, 0)),
                      _seq_block(lambda hp, s: (0, hp)), _seq_block(lambda hp, s: (0, hp)),
                      _seq_block(lambda hp, s: (0, hp))],
            out_specs=[pl.BlockSpec((None, SEQ, LANES), lambda hp, s: (hp, 0, 0)),
                       pl.BlockSpec((2, SEQ, LANES), lambda hp, s: (0, 0, 0)),
                       pl.BlockSpec((None, 8, LANES), lambda hp, s: (hp, 0, 0)),
                       pl.BlockSpec((8, LANES), lambda hp, s: (0, 0)),
                       pl.BlockSpec((None, 2, _line_width(A_HALF_WINDOW)), lambda hp, s: (hp, 0, 0)),
                       pl.BlockSpec((None, 2, Q_BLOCK, 1), lambda hp, s: (hp, 0, 0, 0))],
            scratch_shapes=[seq_f32, padded, padded, seq_f32, padded, padded, seq_f32, seq_f32,
                            pltpu.VMEM((2 * Q_BLOCK, A_W), F32), pltpu.VMEM((2 * Q_BLOCK, A_W), F32)]),
        out_shape=[jax.ShapeDtypeStruct((4, SEQ, LANES), BF16), jax.ShapeDtypeStruct((2, SEQ, LANES), BF16),
                   jax.ShapeDtypeStruct((4, 8, LANES), F32), jax.ShapeDtypeStruct((8, LANES), F32),
                   jax.ShapeDtypeStruct((4, 2, _line_width(A_HALF_WINDOW)), F32),
                   jax.ShapeDtypeStruct((4, 2, Q_BLOCK, 1), F32)],
        compiler_params=_params("arbitrary"),
    )(sink.reshape(8), qkv, qkv, qkv, gain_q, gain_k, bias, delta, lse, d_out)


B_W = Q_BLOCK + 2 * B_HALF_WINDOW
B_PAD_MAX = B_HALF_WINDOW * B_DILATIONS[-1]


def _attn_b_fwd(qkv, gain_q, gain_k, bias):
    def body(q_ref, k_ref, v_ref, gq_ref, gk_ref, line_ref, o_ref, lse_ref, qn_ref, kp_ref, vp_ref, bias_ref):
        g = pl.program_id(1)
        _prep_q(q_ref, gq_ref, qn_ref)
        _unroll_bias(line_ref, bias_ref, B_W)

        def first(rows, out, lse):
            o_ref[rows, :] = out
            lse_ref[rows, :] = lse

        def combine(rows, out, lse):
            old = lse_ref[rows, :]
            new = jnp.maximum(old, lse) + jnp.log(1.0 + jnp.exp(-jnp.abs(old - lse)))
            o_ref[rows, :] = o_ref[rows, :] * jnp.exp(old - new) + out * jnp.exp(lse - new)
            lse_ref[rows, :] = new

        for gi, d in enumerate(B_DILATIONS):
            @pl.when(g == gi)
            def _():
                _prep_kv(k_ref, v_ref, gk_ref, kp_ref, vp_ref, B_HALF_WINDOW * d)
                _fwd_tiles(qn_ref, kp_ref, vp_ref, bias_ref, first if gi == 0 else combine,
                           d=d, half_window=B_HALF_WINDOW)

    vec = pl.BlockSpec((1, LANES), lambda hp, g: (0, 0))
    padded = pltpu.VMEM((SEQ + 2 * B_PAD_MAX, LANES), F32)
    return pl.pallas_call(
        body, name="attn_b_fwd", grid=(4, 3),
        in_specs=[_seq_block(lambda hp, g: (0, QB_BLK + 4 * g + hp)), _seq_block(lambda hp, g: (0, KB_BLK + 4 * g + hp)),
                  _seq_block(lambda hp, g: (0, VB_BLK + 4 * g + hp)), vec, vec,
                  pl.BlockSpec((None, 2, _line_width(B_HALF_WINDOW)), lambda hp, g: (4 * g + hp, 0, 0))],
        out_specs=[_seq_block(lambda hp, g: (0, hp)), _seq_block(lambda hp, g: (0, hp))],
        out_shape=[jax.ShapeDtypeStruct((SEQ, 512), F32)] * 2,
        scratch_shapes=[pltpu.VMEM((SEQ, LANES), F32), padded, padded, pltpu.VMEM((2 * Q_BLOCK, B_W), F32)],
        compiler_params=_params("arbitrary", "arbitrary"),
    )(qkv, qkv, qkv, gain_q, gain_k, bias)


def _attn_b_bwd(qkv, gain_q, gain_k, bias, delta, lse, d_out):
    def body(q_ref, k_ref, v_ref, gq_ref, gk_ref, line_ref, delta_ref, lse_ref, do_ref,
             dq_out, dk_out, dv_out, dgq_out, dgk_out, dline_out,
             qn_ref, kp_ref, vp_ref, dq_ref, dk_ref, dv_ref, bias_ref, ds_out):
        g = pl.program_id(1)
        _prep_q(q_ref, gq_ref, qn_ref)
        _unroll_bias(line_ref, bias_ref, B_W)
        ds_out[...] = jnp.zeros_like(ds_out)
        for gi, d in enumerate(B_DILATIONS):
            @pl.when(g == gi)
            def _():
                pad = B_HALF_WINDOW * d
                for acc in (dk_ref, dv_ref):
                    acc[pl.ds(0, SEQ + 2 * pad), :] = jnp.zeros((SEQ + 2 * pad, LANES), F32)
                _prep_kv(k_ref, v_ref, gk_ref, kp_ref, vp_ref, pad)
                _bwd_tiles(qn_ref, kp_ref, vp_ref, bias_ref, do_ref, lse_ref, delta_ref, dq_ref, dk_ref, dv_ref,
                           ds_out, d=d, half_window=B_HALF_WINDOW)
                dgk_out[...] = _rows8(_norm_bwd(k_ref, gk_ref, dk_ref, pad, dk_out, 1.0))
                dv_out[...] = dv_ref[pl.ds(pad, SEQ), :].astype(BF16)
        _fold_bias_grad(ds_out, dline_out, B_W)
        dgq_out[...] = _rows8(_norm_bwd(q_ref, gq_ref, dq_ref, 0, dq_out, SCALE))

    vec = pl.BlockSpec((1, LANES), lambda hp, g: (0, 0))
    seq_f32 = pltpu.VMEM((SEQ, LANES), F32)
    padded = pltpu.VMEM((SEQ + 2 * B_PAD_MAX, LANES), F32)
    part = pl.BlockSpec((None, 8, LANES), lambda hp, g: (4 * g + hp, 0, 0))
    line = pl.BlockSpec((None, 2, _line_width(B_HALF_WINDOW)), lambda hp, g: (4 * g + hp, 0, 0))
    return pl.pallas_call(
        body, name="attn_b_bwd", grid=(4, 3),
        in_specs=[_seq_block(lambda hp, g: (0, QB_BLK + 4 * g + hp)), _seq_block(lambda hp, g: (0, KB_BLK + 4 * g + hp)),
                  _seq_block(lambda hp, g: (0, VB_BLK + 4 * g + hp)), vec, vec,
                  line,
                  _seq_block(lambda hp, g: (0, hp)), _seq_block(lambda hp, g: (0, hp)), _seq_block(lambda hp, g: (0, hp))],
        out_specs=[pl.BlockSpec((None, SEQ, LANES), lambda hp, g: (4 * g + hp, 0, 0))] * 3 + [part, part, line],
        out_shape=[jax.ShapeDtypeStruct((12, SEQ, LANES), BF16)] * 3
        + [jax.ShapeDtypeStruct((12, 8, LANES), F32)] * 2
        + [jax.ShapeDtypeStruct((12, 2, _line_width(B_HALF_WINDOW)), F32)],
        scratch_shapes=[seq_f32, padded, padded, seq_f32, padded, padded,
                        pltpu.VMEM((2 * Q_BLOCK, B_W), F32), pltpu.VMEM((2 * Q_BLOCK, B_W), F32)],
        compiler_params=_params("arbitrary", "arbitrary"),
    )(qkv, qkv, qkv, gain_q, gain_k, bias, delta, lse, d_out)


def _sigmoid(t):
    return 1.0 / (1.0 + jnp.exp(-t))


def _middle(out_a, out_b, gates, x, target, w_a, w_b, w_out, b_merge):
    tm = 256
    n_steps = SEQ // tm

    def body(oa_ref, ob_ref, g_ref, x_ref, t_ref, wa_ref, wb_ref, wo_ref, bm_ref,
             dy_ref, dg_ref, doa_ref, dob_ref, dla_ref, dlb_ref, dwa_ref, dwb_ref, dwo_ref, dbm_ref, sq_ref):
        @pl.when(pl.program_id(0) == 0)
        def _():
            for ref in (dwa_ref, dwb_ref, dwo_ref, dbm_ref, sq_ref):
                ref[...] = jnp.zeros_like(ref)

        gate_a, gate_b = g_ref[:, 0:512], g_ref[:, 512:1024]
        sig_a, sig_b = _sigmoid(gate_a), _sigmoid(gate_b)
        silu_a, silu_b = gate_a * sig_a, gate_b * sig_b
        oa, ob = oa_ref[...], ob_ref[...]
        ya, yb = (oa * silu_a).astype(BF16), (ob * silu_b).astype(BF16)
        br_a, br_b = _dot(ya, wa_ref[...]), _dot(yb, wb_ref[...])
        m0 = _sigmoid(g_ref[:, 1024:2048] + bm_ref[0:1, :])
        m1 = _sigmoid(g_ref[:, 2048:3072] + bm_ref[1:2, :])
        merged = (m0 * br_a + m1 * br_b).astype(BF16)
        err = (x_ref[...] + _dot(merged, wo_ref[...])) - t_ref[...]
        sq_ref[...] += jnp.sum(err * err, axis=0, keepdims=True)

        dy = err * (1.0 / D_MODEL)
        dy_ref[...] = dy
        dyb = dy.astype(BF16)
        dmerged = _dot(dyb, wo_ref[...], NT)
        dwo_ref[...] += _dot(merged, dyb, TN)
        dbr_a, dbr_b = (dmerged * m0).astype(BF16), (dmerged * m1).astype(BF16)
        dm0 = (dmerged * br_a) * (m0 * (1.0 - m0))
        dm1 = (dmerged * br_b) * (m1 * (1.0 - m1))
        dbm_ref[0:1, :] += jnp.sum(dm0, axis=0, keepdims=True)
        dbm_ref[1:2, :] += jnp.sum(dm1, axis=0, keepdims=True)
        for s in range(N_CHIPS):
            cols = slice(256 * s, 256 * (s + 1))
            dwa_ref[s] += _dot(ya, dbr_a[:, cols], TN)
            dwb_ref[s] += _dot(yb, dbr_b[:, cols], TN)
        dya, dyb_ = _dot(dbr_a, wa_ref[...], NT), _dot(dbr_b, wb_ref[...], NT)
        doa, dob = dya * silu_a, dyb_ * silu_b
        doa_ref[...] = doa
        dob_ref[...] = dob
        for blk in range(512 // LANES):
            lanes = slice(blk * LANES, (blk + 1) * LANES)
            dla_ref[:, lanes] = _half_sum(doa[:, lanes] * oa[:, lanes], None)
            dlb_ref[:, lanes] = _half_sum(dob[:, lanes] * ob[:, lanes], None)
        d_gates = (((dya * oa) * (sig_a * (1.0 + gate_a * (1.0 - sig_a)))).astype(BF16),
                   ((dyb_ * ob) * (sig_b * (1.0 + gate_b * (1.0 - sig_b)))).astype(BF16),
                   dm0.astype(BF16), dm1.astype(BF16))
        blk = 0
        for part in d_gates:
            for c0 in range(0, part.shape[1], 256):
                dg_ref[blk] = part[:, c0:c0 + 256]
                blk += 1

    def rows(width):
        return pl.BlockSpec((tm, width), lambda i: (i, 0))

    def whole(*shape):
        return pl.BlockSpec(shape, lambda i: (0,) * len(shape))

    return pl.pallas_call(
        body, name="middle", grid=(n_steps,),
        in_specs=[rows(512), rows(512), rows(GATE_WIDTH), rows(D_MODEL), rows(D_MODEL),
                  whole(512, D_MODEL), whole(512, D_MODEL), whole(D_MODEL, D_MODEL), whole(2, D_MODEL)],
        out_specs=[rows(D_MODEL), pl.BlockSpec((GATE_WIDTH // 256, tm, 256), lambda i: (0, i, 0)),
                   rows(512), rows(512), rows(512), rows(512),
                   whole(N_CHIPS, 512, 256), whole(N_CHIPS, 512, 256), whole(D_MODEL, D_MODEL),
                   whole(2, D_MODEL), whole(1, D_MODEL)],
        out_shape=[jax.ShapeDtypeStruct((SEQ, D_MODEL), F32), jax.ShapeDtypeStruct((GATE_WIDTH // 256, SEQ, 256), BF16),
                   jax.ShapeDtypeStruct((SEQ, 512), F32), jax.ShapeDtypeStruct((SEQ, 512), F32),
                   jax.ShapeDtypeStruct((SEQ, 512), F32), jax.ShapeDtypeStruct((SEQ, 512), F32),
                   jax.ShapeDtypeStruct((N_CHIPS, 512, 256), F32), jax.ShapeDtypeStruct((N_CHIPS, 512, 256), F32),
                   jax.ShapeDtypeStruct((D_MODEL, D_MODEL), F32), jax.ShapeDtypeStruct((2, D_MODEL), F32),
                   jax.ShapeDtypeStruct((1, D_MODEL), F32)],
        compiler_params=_params("arbitrary"),
    )(out_a, out_b, gates, x, target, w_a, w_b, w_out, b_merge)


def _which(j, edges, fns):
    lo = 0
    for hi, fn in zip(edges, fns):
        pl.when((j >= lo) & (j < hi))(fn)
        lo = hi


def _sibling_rows(tile, core):
    lo, hi = tile * W_BLOCK, (tile + 1) * W_BLOCK
    for chip in range(N_CHIPS):
        a = chip * W_IN_SHARD + (1 - core) * (W_IN_SHARD // 2)
        first, last = max(lo, a), min(hi, a + W_IN_SHARD // 2)
        if first < last:
            return chip, first - a, first - lo, last - first
    return None


def _d_w_in(d_proj, h, rest=None):
    plan, step, width = [], 0, 0
    for p in d_proj:
        total = p.shape[0] * p.shape[2]
        if width + total <= W_BLOCK:
            plan.append((p.shape[0], step, 1))
            width += total
            if width == W_BLOCK:
                step, width = step + 1, 0
        else:
            assert width == 0 and total % W_BLOCK == 0
            plan.append((W_BLOCK // p.shape[2], step, total // W_BLOCK))
            step += total // W_BLOCK
    assert width == 0 and step == IN_WIDTH // W_BLOCK
    firsts = sorted({first for _, first, _ in plan})
    edges = firsts[1:] + [step]
    halves = 2

    hand_over = rest is not None
    half = W_IN_SHARD // 2

    def body(*refs):
        if hand_over:
            pieces, h_ref, rest_ref = refs[:len(d_proj)], refs[len(d_proj)], refs[len(d_proj) + 1]
            o_ref, got_ref, got_rest_ref, acc_ref, send_sems, recv_sems, stage = refs[len(d_proj) + 2:]
        else:
            pieces, h_ref, o_ref, acc_ref = refs[:-3], refs[-3], refs[-2], refs[-1]
        k = pl.program_id(1)

        def emit(group):
            def fn():
                cols = jnp.concatenate([ref[b] for ref in group for b in range(ref.shape[0])], axis=1)
                term = _dot(cols, h_ref[...], TN)

                @pl.when(k == 0)
                def _():
                    acc_ref[...] = term

                @pl.when(k == halves - 1)
                def _():
                    o_ref[...] = (acc_ref[...] + term).astype(BF16)
            return fn

        groups = [[ref for ref, (_, first, _) in zip(pieces, plan) if first == f] for f in firsts]
        _which(pl.program_id(0), edges, [emit(group) for group in groups])

        if hand_over:
            cx, cy, c = lax.axis_index("x"), lax.axis_index("y"), lax.axis_index("c")
            sibling = (cx, cy, 1 - c)

            def to_sibling(sem, src, dst, recv=0):
                return pltpu.make_async_remote_copy(src_ref=src, dst_ref=dst, send_sem=send_sems.at[sem],
                                                    recv_sem=recv_sems.at[recv], device_id=sibling, device_id_type=MESH)

            def tile_copy(tile, core):
                chip, row, start, rows = _sibling_rows(tile, core)
                return to_sibling(tile % 2, stage.at[tile % 2, pl.ds(0, rows), :], got_ref.at[chip, pl.ds(row, rows), :])

            rest_copy = to_sibling(2, _half_rows(rest_ref, 1 - c), got_rest_ref, recv=1)

            @pl.when((pl.program_id(0) == 0) & (k == 0))
            def _():
                rest_copy.start()

            for tile in range(step):
                for core in range(2):
                    @pl.when((pl.program_id(0) == tile) & (k == halves - 1) & (c == core))
                    def _(tile=tile, core=core):
                        if tile >= 2 and _sibling_rows(tile - 2, core):
                            tile_copy(tile - 2, core).wait_send()
                        if _sibling_rows(tile, core):
                            _, _, start, rows = _sibling_rows(tile, core)
                            stage[tile % 2, 0:rows, :] = o_ref[start:start + rows, :]
                            tile_copy(tile, core).start()
                        if tile == step - 1:
                            for last in (step - 2, step - 1):
                                if _sibling_rows(last, core):
                                    tile_copy(last, core).wait_send()
                            rest_copy.wait()
                            to_sibling(0, got_ref, got_ref).wait_recv()

    def cols_spec(piece, n, first, steps):
        def index(j, k):
            return jnp.clip(j - first, 0, steps - 1), jnp.where((j >= first) & (j < first + steps), k, 0), 0
        return pl.BlockSpec((n, SEQ // halves, piece.shape[2]), index)

    tile_spec = pl.BlockSpec((W_BLOCK, D_MODEL), lambda j, k: (j, 0))
    in_specs = [cols_spec(p, *pl_) for p, pl_ in zip(d_proj, plan)] + [
        pl.BlockSpec((SEQ // halves, D_MODEL), lambda j, k: (k, 0))]
    acc = pltpu.VMEM((W_BLOCK, D_MODEL), F32)
    if not hand_over:
        return pl.pallas_call(
            body, name="d_w_in", grid=(step, halves), in_specs=in_specs, out_specs=tile_spec,
            out_shape=jax.ShapeDtypeStruct((IN_WIDTH, D_MODEL), BF16), scratch_shapes=[acc],
            compiler_params=_params("arbitrary", "arbitrary"),
        )(*d_proj, h)
    return pl.pallas_call(
        body, name="d_w_in", grid=(step, halves), in_specs=in_specs + [ANY], out_specs=[tile_spec, ANY, ANY],
        out_shape=[jax.ShapeDtypeStruct((IN_WIDTH, D_MODEL), BF16),
                   jax.ShapeDtypeStruct((N_CHIPS, half, D_MODEL), BF16),
                   jax.ShapeDtypeStruct((N_CHIPS, rest.shape[1] // 2, D_MODEL), BF16)],
        scratch_shapes=[acc, pltpu.SemaphoreType.DMA((3,)), pltpu.SemaphoreType.DMA((2,)),
                        pltpu.VMEM((2, W_BLOCK, D_MODEL), BF16)],
        compiler_params=_params("arbitrary", "arbitrary"),
    )(*d_proj, h, rest)


RELAY_STEP = 10
RELAY_ROWS = 352


def _d_x(d_proj, w_t, x, gain, dy, chip_sums):
    tm = 256
    n_steps = SEQ // tm
    n_w = IN_WIDTH // W_BLOCK
    n_p, n_s = len(d_proj), len(chip_sums)

    def body(*refs):
        pieces, w_refs = refs[:n_p], refs[n_p:n_p + n_w]
        x_ref, g_ref, dy_ref = refs[n_p + n_w:n_p + n_w + 3]
        q_refs = refs[n_p + n_w + 3:n_p + n_w + 3 + n_s]
        dx_ref, dgain_ref = refs[n_p + n_w + 3 + n_s:n_p + n_w + 5 + n_s]
        outs = refs[n_p + n_w + 5 + n_s:n_p + n_w + 5 + 4 * n_s]
        got_refs, relay_refs, sum_refs = outs[:n_s], outs[n_s:2 * n_s], outs[2 * n_s:]
        if n_s:
            send_sems, recv_sems, local_sems, a_buf, b_buf, c_buf = refs[n_p + n_w + 5 + 4 * n_s:]

        def hops():
            cx, cy, c = lax.axis_index("x"), lax.axis_index("y"), lax.axis_index("c")
            near = (cx + (1 - c) - 2 * cx * (1 - c), cy + c - 2 * cy * c)
            far = (cx + c - 2 * cx * c, cy + (1 - c) - 2 * cy * (1 - c))
            chip = lambda p: 2 * p[0] + p[1]

            def copy(k, src, dst, to):
                return pltpu.make_async_remote_copy(src_ref=src, dst_ref=dst, send_sem=send_sems.at[k],
                                                    recv_sem=recv_sems.at[k], device_id=(*to, c), device_id_type=MESH)

            first = [(copy(3 * b, q.at[chip(near)], got.at[0], near),
                      copy(3 * b + 1, q.at[3 - chip((cx, cy))], relay, near))
                     for b, (q, got, relay) in enumerate(zip(q_refs, got_refs, relay_refs))]
            second = [copy(3 * b + 2, s, got.at[1], far) for b, (s, got) in enumerate(zip(sum_refs, got_refs))]
            return first, second, chip(far)

        @pl.when(pl.program_id(0) == 0)
        def _():
            dgain_ref[...] = jnp.zeros_like(dgain_ref)
            if n_s:
                for direct, pass_on in hops()[0]:
                    direct.start()
                    pass_on.start()

        if n_s:
            @pl.when(pl.program_id(0) == RELAY_STEP)
            def _():
                first, second, far_chip = hops()
                for b, (q, relay, total) in enumerate(zip(q_refs, relay_refs, sum_refs)):
                    first[b][1].wait_recv()
                    half = relay.shape[0]
                    for r0 in range(0, half, RELAY_ROWS):
                        rows = min(RELAY_ROWS, half - r0)
                        mine = pltpu.make_async_copy(q.at[far_chip, pl.ds(r0, rows), :], a_buf.at[pl.ds(0, rows), :],
                                                     local_sems.at[0])
                        theirs = pltpu.make_async_copy(relay.at[pl.ds(r0, rows), :], b_buf.at[pl.ds(0, rows), :],
                                                       local_sems.at[1])
                        mine.start()
                        theirs.start()
                        mine.wait()
                        theirs.wait()
                        c_buf[0:rows, :] = (a_buf[0:rows, :].astype(F32) + b_buf[0:rows, :].astype(F32)).astype(BF16)
                        store = pltpu.make_async_copy(c_buf.at[pl.ds(0, rows), :], total.at[pl.ds(r0, rows), :],
                                                      local_sems.at[2])
                        store.start()
                        store.wait()
                    second[b].start()

        blocks = [(piece, k) for piece in pieces for k in range(piece.shape[0])]
        dh, group, width, blk = None, [], 0, 0
        for piece, k in blocks:
            group.append(piece[k])
            width += piece.shape[2]
            if width == W_BLOCK:
                term = _dot(jnp.concatenate(group, axis=1), w_refs[blk][...])
                dh = term if dh is None else dh + term
                group, width, blk = [], 0, blk + 1
        assert not group and blk == n_w
        xf = x_ref[...]
        r = lax.rsqrt(jnp.mean(xf * xf, axis=-1, keepdims=True) + EPS)
        xh = xf * r
        dxh = dh * g_ref[...]
        dx_ref[...] = r * (dxh - xh * jnp.mean(dxh * xh, axis=-1, keepdims=True)) + dy_ref[...]
        dgain_ref[...] += _rows8(jnp.sum(dh * xh, axis=0, keepdims=True))

        if n_s:
            @pl.when(pl.program_id(0) == n_steps - 1)
            def _():
                first, second, _ = hops()
                for direct, pass_on in first:
                    direct.wait()
                    pass_on.wait_send()
                for cp in second:
                    cp.wait()

    row = pl.BlockSpec((tm, D_MODEL), lambda i: (i, 0))
    halves = [q.shape[1] for q in chip_sums]
    res = pl.pallas_call(
        body, name="d_x", grid=(n_steps,),
        in_specs=[pl.BlockSpec((p.shape[0], tm, p.shape[2]), lambda i: (0, i, 0)) for p in d_proj] + _w_blocks(0, n_w)
        + [row, pl.BlockSpec((1, D_MODEL), lambda i: (0, 0)), row] + [ANY] * n_s,
        out_specs=[row, pl.BlockSpec((8, D_MODEL), lambda i: (0, 0))] + [ANY] * (3 * n_s),
        out_shape=[jax.ShapeDtypeStruct((SEQ, D_MODEL), F32), jax.ShapeDtypeStruct((8, D_MODEL), F32)]
        + [jax.ShapeDtypeStruct((2, half, D_MODEL), BF16) for half in halves]
        + [jax.ShapeDtypeStruct((half, D_MODEL), BF16) for half in halves] * 2,
        scratch_shapes=[pltpu.SemaphoreType.DMA((3 * n_s,)), pltpu.SemaphoreType.DMA((3 * n_s,)),
                        pltpu.SemaphoreType.DMA((3,))] + [pltpu.VMEM((RELAY_ROWS, D_MODEL), BF16)] * 3 if n_s else [],
        compiler_params=_params("arbitrary"),
    )(*d_proj, *([w_t] * n_w), x, gain, dy, *chip_sums)
    return res[0], res[1], res[2:2 + n_s]


def _my_place():
    x, y, c = lax.axis_index("x"), lax.axis_index("y"), lax.axis_index("c")
    return jnp.stack([2 * x + y, c]).astype(jnp.int32)


def _half_rows(ref, half):
    rows = ref.shape[-2] // 2
    idx = (slice(None),) * (len(ref.shape) - 2) + (pl.ds(pl.multiple_of(half * rows, 16), rows), slice(None))
    return ref.at[idx]


def _add_halves(place, grads, theirs, name):
    half = theirs.shape[1]
    tr = _row_tile(half)
    n = half // tr

    def body(place_ref, g_ref, t_ref, o_ref):
        o_ref[...] = (g_ref[...].astype(F32) + t_ref[...].astype(F32)).astype(BF16)

    return pl.pallas_call(
        body, name=name,
        grid_spec=pltpu.PrefetchScalarGridSpec(
            num_scalar_prefetch=1, grid=(N_CHIPS, n),
            in_specs=[pl.BlockSpec((None, tr, D_MODEL), lambda s, i, p: (s, p[1] * n + i, 0)),
                      pl.BlockSpec((None, tr, D_MODEL), lambda s, i, p: (s, i, 0))],
            out_specs=pl.BlockSpec((None, tr, D_MODEL), lambda s, i, p: (s, i, 0))),
        out_shape=jax.ShapeDtypeStruct((N_CHIPS, half, D_MODEL), BF16),
        compiler_params=_params("arbitrary", "arbitrary"),
    )(place, grads, theirs)


def _add_chips(place, chip_sums, others, name):
    half = others.shape[1]
    tr = _row_tile(half)
    n = half // tr

    def body(place_ref, q_ref, o_ref, r_ref):
        acc = q_ref[...].astype(F32)
        for j in range(others.shape[0]):
            acc = acc + o_ref[j].astype(F32)
        r_ref[...] = acc

    return pl.pallas_call(
        body, name=name,
        grid_spec=pltpu.PrefetchScalarGridSpec(
            num_scalar_prefetch=1, grid=(n,),
            in_specs=[pl.BlockSpec((None, tr, D_MODEL), lambda i, p: (p[0], i, 0)),
                      pl.BlockSpec((others.shape[0], tr, D_MODEL), lambda i, p: (0, i, 0))],
            out_specs=pl.BlockSpec((tr, D_MODEL), lambda i, p: (p[1] * n + i, 0))),
        out_shape=jax.ShapeDtypeStruct((2 * half, D_MODEL), F32),
        compiler_params=_params("arbitrary"),
    )(place, chip_sums, others)


def _join_halves(shards, block):
    n = len(shards)
    rows = block.shape[0]

    def body(*refs):
        b_ref, o_refs, sum_ref = refs[n], refs[n + 1:2 * n + 1], refs[2 * n + 1]
        send_sems, recv_sems, small_send, small_recv, local_sem, all_ref = refs[2 * n + 2:]
        x, y, c = lax.axis_index("x"), lax.axis_index("y"), lax.axis_index("c")
        me, sibling = (x, y, c), (x, y, 1 - c)
        chips = [(1 - x, y), (x, 1 - y), (1 - x, 1 - y)]

        def half(k, rows_ref):
            return pltpu.make_async_remote_copy(src_ref=rows_ref, dst_ref=rows_ref, send_sem=send_sems.at[k],
                                                recv_sem=recv_sems.at[k], device_id=sibling, device_id_type=MESH)

        def at(px, py, pc):
            return all_ref.at[pl.ds(pl.multiple_of((4 * px + 2 * py + pc) * rows, 8), rows), :]

        def small(k, block_of, to, src=None):
            return pltpu.make_async_remote_copy(src_ref=at(*block_of) if src is None else src, dst_ref=at(*block_of),
                                                send_sem=small_send.at[k], recv_sem=small_recv.at[k],
                                                device_id=to, device_id_type=MESH)

        sends = [half(k, _half_rows(o, c)) for k, o in enumerate(o_refs)]
        for cp in sends:
            cp.start()
        mine = pltpu.make_async_copy(b_ref, at(*me), local_sem)
        mine.start()
        first = [small(0, me, sibling, src=b_ref)]
        first += [small(1 + j, me, (*chip, c), src=b_ref) for j, chip in enumerate(chips)]
        for cp in first:
            cp.start()
        passed = [small(4 + j, (*chip, c), sibling) for j, chip in enumerate(chips)]
        for j, chip in enumerate(chips):
            small(1 + j, (*chip, c), me).wait_recv()
            passed[j].start()
        small(0, sibling, me).wait_recv()
        for j, chip in enumerate(chips):
            small(4 + j, (*chip, 1 - c), me).wait_recv()
        mine.wait()
        acc = all_ref[0:rows, :]
        for dev in range(1, 8):
            acc = acc + all_ref[rows * dev:rows * (dev + 1), :]
        sum_ref[...] = acc
        for k, o in enumerate(o_refs):
            half(k, _half_rows(o, 1 - c)).wait_recv()
        for cp in sends + first + passed:
            cp.wait_send()

    res = pl.pallas_call(
        body, name="reduce_join_halves", in_specs=[ANY] * n + [pl.BlockSpec(memory_space=pltpu.VMEM)],
        out_specs=[ANY] * n + [pl.BlockSpec(memory_space=pltpu.VMEM)],
        out_shape=[jax.ShapeDtypeStruct(s.shape, F32) for s in shards] + [jax.ShapeDtypeStruct(block.shape, F32)],
        input_output_aliases={k: k for k in range(n)},
        scratch_shapes=[pltpu.SemaphoreType.DMA((n,)), pltpu.SemaphoreType.DMA((n,)),
                        pltpu.SemaphoreType.DMA((7,)), pltpu.SemaphoreType.DMA((7,)), pltpu.SemaphoreType.DMA,
                        pltpu.VMEM((8 * rows, D_MODEL), F32)],
    )(*shards, block)
    return res[:n], res[n]


def _adamw_math(w, g, m, v):
    m = ADAM_B1 * m + (1.0 - ADAM_B1) * g
    v = ADAM_B2 * v + (1.0 - ADAM_B2) * (g * g)
    m_hat = m / (1.0 - ADAM_B1 ** ADAM_STEP)
    v_hat = v / (1.0 - ADAM_B2 ** ADAM_STEP)
    return -ADAM_LR * (m_hat / (jnp.sqrt(v_hat) + ADAM_EPS) + ADAM_WD * w), m, v


def _adamw(w, g, m, v, name):
    r, c = w.shape
    tr = _row_tile(r)

    def body(w_ref, g_ref, m_ref, v_ref, d_ref, nm_ref, nv_ref):
        d_ref[...], nm_ref[...], nv_ref[...] = _adamw_math(w_ref[...], g_ref[...], m_ref[...], v_ref[...])

    spec = pl.BlockSpec((tr, c), lambda i: (i, 0))
    return pl.pallas_call(
        body, name=name, grid=(r // tr,), in_specs=[spec] * 4, out_specs=[spec] * 3,
        out_shape=[jax.ShapeDtypeStruct((r, c), F32)] * 3, compiler_params=_params("arbitrary"),
    )(w, g, m, v)


def _adamw_small(ws, gs, ms, vs):
    n = len(ws)

    def body(*refs):
        ins, outs = refs[:4 * n], refs[4 * n:]
        for k in range(n):
            d, m, v = _adamw_math(ins[k][...], ins[n + k][...], ins[2 * n + k][...], ins[3 * n + k][...])
            outs[k][...], outs[n + k][...], outs[2 * n + k][...] = d, m, v

    shapes = [jax.ShapeDtypeStruct(w.shape, F32) for w in ws]
    res = pl.pallas_call(body, name="adamw_small", out_shape=shapes * 3)(*ws, *gs, *ms, *vs)
    return res[:n], res[n:2 * n], res[2 * n:]


def _fold_heads(partials):
    t = jnp.sum(partials[:, 0, :], axis=0)
    return (t[:HEAD_DIM] + t[HEAD_DIM:]).reshape(1, HEAD_DIM)


def _local_step(x, target, norm_gain, w_t, w_a, w_b, w_o, b_m, q_norm_a, k_norm_a, q_norm_b, k_norm_b, sink_a,
                rel_bias, start_reduce=None, small_shard=None):
    two = lambda gain: jnp.concatenate([gain, gain], axis=1)
    bias_a = _bias_lines(rel_bias[:, :8], A_HALF_WINDOW, 1)
    bias_b = jnp.concatenate([_bias_lines(rel_bias[:, 8 + 8 * g:16 + 8 * g], B_HALF_WINDOW, d)
                              for g, d in enumerate(B_DILATIONS)], axis=0)

    qkv, h, *small_all = _in_proj(x, norm_gain, w_t, 0, QKV_WIDTH // W_BLOCK, BF16, "in_proj_qkv", True, small_shard)
    if small_shard is not None:
        w_a, w_b, w_o, b_m = _unpack_weights(small_all[0])
    gates, = _in_proj(x, norm_gain, w_t, QKV_WIDTH // W_BLOCK, GATE_WIDTH // W_BLOCK, F32, "in_proj_gates", False)
    out_a, lse_a = _attn_a_fwd(qkv, two(q_norm_a), two(k_norm_a), bias_a, sink_a)
    out_b, lse_b = _attn_b_fwd(qkv, two(q_norm_b), two(k_norm_b), bias_b)

    dy, dgates, d_out_a, d_out_b, delta_a, delta_b, d_wa, d_wb, d_wo, d_bm, sq = _middle(
        out_a, out_b, gates, x, target, w_a, w_b, w_o, b_m)
    loss = (0.5 / D_MODEL) * jnp.sum(sq)

    dq_a, dkv_a, dgq_a, dgk_a, ds_a, dsink = _attn_a_bwd(
        qkv, two(q_norm_a), two(k_norm_a), bias_a, sink_a, delta_a, lse_a, d_out_a)
    dq_b, dk_b, dv_b, dgq_b, dgk_b, ds_b = _attn_b_bwd(
        qkv, two(q_norm_b), two(k_norm_b), bias_b, delta_b, lse_b, d_out_b)
    d_proj = (dq_a, dkv_a, dq_b, dk_b, dv_b, dgates)

    d_bm_rows = jnp.pad(d_bm.reshape(2, N_CHIPS, 256).transpose(1, 0, 2),
                        ((0, 0), (0, REST_ROWS - 514), (0, D_MODEL - 256)))
    rest = jnp.concatenate([d_wo.reshape(N_CHIPS, 256, D_MODEL), d_wa.reshape(N_CHIPS, 128, D_MODEL),
                            d_wb.reshape(N_CHIPS, 128, D_MODEL), d_bm_rows], axis=1)
    if start_reduce is None:
        grads, chip_sums = [_d_w_in(d_proj, h).reshape(N_CHIPS, W_IN_SHARD, D_MODEL), rest], []
    else:
        d_wt, *theirs = _d_w_in(d_proj, h, rest.astype(BF16))
        grads = [d_wt.reshape(N_CHIPS, W_IN_SHARD, D_MODEL), rest]
        chip_sums = start_reduce(grads, theirs)
    grad_x, d_gain, others = _d_x(d_proj, w_t, x, norm_gain, dy, chip_sums)

    d_rel = jnp.concatenate(
        [_bias_grad(ds_a, A_HALF_WINDOW, 1)]
        + [_bias_grad(ds_b[4 * g:4 * g + 4], B_HALF_WINDOW, d) for g, d in enumerate(B_DILATIONS)], axis=1)
    d_sink = jnp.sum(dsink, axis=(2, 3)).reshape(1, 8)
    dgk_a_row = dgk_a[0]
    small = jnp.zeros((8, D_MODEL), F32)
    small = small.at[0].set(d_gain[0])
    small = small.at[1].set(d_rel.reshape(-1))
    misc = jnp.concatenate([_fold_heads(dgq_a), (dgk_a_row[:HEAD_DIM] + dgk_a_row[HEAD_DIM:]).reshape(1, HEAD_DIM),
                            _fold_heads(dgq_b), _fold_heads(dgk_b), d_sink], axis=1)
    small = small.at[2, :264].set(misc[0])

    return loss, grad_x, grads, small, chip_sums, others


def _unpack_weights(small_all):
    sm = small_all.reshape(N_CHIPS, SMALL_ROWS, D_MODEL)
    w_o = sm[:, 0:256].reshape(D_MODEL, D_MODEL)
    w_a = sm[:, 256:384].reshape(N_CHIPS, 512, 256).transpose(1, 0, 2).reshape(512, D_MODEL)
    w_b = sm[:, 384:512].reshape(N_CHIPS, 512, 256).transpose(1, 0, 2).reshape(512, D_MODEL)
    b_m = lax.bitcast_convert_type(sm[:, 512].reshape(N_CHIPS, 2, 256, 2), F32)
    return w_a, w_b, w_o, b_m.transpose(1, 0, 2).reshape(2, D_MODEL)


def _pack_small_weights(w_branch_a, w_branch_b, b_merge, w_out):
    b_m = jnp.pad(lax.bitcast_convert_type(b_merge, BF16).reshape(1, D_MODEL), ((0, SMALL_ROWS - 513), (0, 0)))
    return jnp.concatenate([w_out.astype(BF16), w_branch_a.astype(BF16).reshape(128, D_MODEL),
                            w_branch_b.astype(BF16).reshape(128, D_MODEL), b_m], axis=0)


def kernel(x, norm_gain, w_in, q_norm_a, k_norm_a, q_norm_b, k_norm_b, sink_a, rel_bias, w_branch_a, w_branch_b, b_merge, w_out, loss_target, m_norm_gain, m_w_in, m_q_norm_a, m_k_norm_a, m_q_norm_b, m_k_norm_b, m_sink_a, m_rel_bias, m_w_branch_a, m_w_branch_b, m_b_merge, m_w_out, v_norm_gain, v_w_in, v_q_norm_a, v_k_norm_a, v_q_norm_b, v_k_norm_b, v_sink_a, v_rel_bias, v_w_branch_a, v_w_branch_b, v_b_merge, v_w_out):
    w_in_t, m_w_in_t, v_w_in_t = (jnp.transpose(t[0]) for t in (w_in, m_w_in, v_w_in))
    wt_shard = _cast_rows(w_in_t, BF16, "w_in_cast")
    w_t = _gather_weights(wt_shard)
    small_shard = _pack_small_weights(w_branch_a[0], w_branch_b[0], b_merge[0], w_out[0])

    place = _my_place()
    names = ("w_in", "rest")

    def start_reduce(grads, theirs):
        return [_add_halves(place, g, t, "reduce_add_halves_" + n) for g, t, n in zip(grads, theirs, names)]

    loss_part, grad_x, _, small, chip_sums, others = _local_step(
        x[0], loss_target[0], norm_gain, w_t, None, None, None, None, q_norm_a, k_norm_a, q_norm_b, k_norm_b,
        sink_a, rel_bias, start_reduce, small_shard)

    (g_wt, g_rest), small = _join_halves(
        [_add_chips(place, q, o, "reduce_add_chips_" + n) for q, o, n in zip(chip_sums, others, names)],
        small.at[3, 0].set(loss_part))
    loss = small[3, 0]

    g_w_out = g_rest[0:256]
    g_w_a = g_rest[256:384].reshape(512, 256)
    g_w_b = g_rest[384:512].reshape(512, 256)
    g_b_merge = g_rest[512:514, :256]
    g_norm_gain = small[0:1]
    g_rel_bias = small[1].reshape(N_BUCKETS, N_BUCKETS)
    g_q_a, g_k_a, g_q_b, g_k_b = (small[2:3, 64 * k:64 * k + 64] for k in range(4))
    g_sink = small[2:3, 256:264]

    big_names = (("w_branch_a", w_branch_a, g_w_a, m_w_branch_a, v_w_branch_a),
                 ("w_branch_b", w_branch_b, g_w_b, m_w_branch_b, v_w_branch_b),
                 ("w_out", w_out, g_w_out, m_w_out, v_w_out))
    upd = {name: (g,) + tuple(_adamw(w[0], g, m[0], v[0], "adamw_" + name)) for name, w, g, m, v in big_names}
    upd["w_in"] = tuple(jnp.transpose(t) for t in (g_wt,) + tuple(_adamw(w_in_t, g_wt, m_w_in_t, v_w_in_t, "adamw_w_in")))
    small_names = ("norm_gain", "q_norm_a", "k_norm_a", "q_norm_b", "k_norm_b", "sink_a", "rel_bias", "b_merge")
    ws = [norm_gain, q_norm_a, k_norm_a, q_norm_b, k_norm_b, sink_a, rel_bias, b_merge[0]]
    gs = [g_norm_gain, g_q_a, g_k_a, g_q_b, g_k_b, g_sink, g_rel_bias, g_b_merge]
    ms = [m_norm_gain, m_q_norm_a, m_k_norm_a, m_q_norm_b, m_k_norm_b, m_sink_a, m_rel_bias, m_b_merge[0]]
    vs = [v_norm_gain, v_q_norm_a, v_k_norm_a, v_q_norm_b, v_k_norm_b, v_sink_a, v_rel_bias, v_b_merge[0]]
    ds, nms, nvs = _adamw_small(ws, gs, ms, vs)
    for k, name in enumerate(small_names):
        upd[name] = (gs[k], ds[k], nms[k], nvs[k])

    order = ("norm_gain", "w_in", "q_norm_a", "k_norm_a", "q_norm_b", "k_norm_b", "sink_a", "rel_bias",
             "w_branch_a", "w_branch_b", "b_merge", "w_out")
    lead = {"w_in", "w_branch_a", "w_branch_b", "b_merge", "w_out"}
    outs = [loss, grad_x[None]]
    for part in range(4):
        outs += [upd[name][part][None] if name in lead else upd[name][part] for name in order]
    return tuple(outs)
```

```python
import math

import numpy as np
import jax
import jax.numpy as jnp
from jax import lax
from jax.experimental import pallas as pl
from jax.experimental.pallas import tpu as pltpu

F32 = jnp.float32
BF16 = jnp.bfloat16

SEQ = 4096
D_MODEL = 1024
HEAD_DIM = 64
LANES = 128
EPS = 1e-6
NEG_INF = -1e30
SCALE = HEAD_DIM ** -0.5
N_BUCKETS = 32
MAX_DISTANCE = 1024
N_CHIPS = 4

A_HALF_WINDOW = 128
B_HALF_WINDOW = 64
B_DILATIONS = (1, 4, 16)
Q_BLOCK = 128

QKV_WIDTH = 5376
GATE_WIDTH = 3072
QA_BLK, KA_BLK, VA_BLK = 0, 4, 5
QB_BLK, KB_BLK, VB_BLK = 6, 18, 30
IN_WIDTH = QKV_WIDTH + GATE_WIDTH
W_IN_SHARD = IN_WIDTH // N_CHIPS

SMALL_ROWS = 544
REST_ROWS = 544

ADAM_LR = 0.001
ADAM_B1 = 0.9
ADAM_B2 = 0.999
ADAM_EPS = 1e-08
ADAM_WD = 0.01
ADAM_STEP = 10

VMEM_LIMIT = 56 * 1024 * 1024

NT = (((1,), (1,)), ((), ()))
TN = (((0,), (0,)), ((), ()))
MESH = pl.DeviceIdType.MESH
ANY = pl.BlockSpec(memory_space=pl.ANY)


def _dot(a, b, dims=None):
    if dims is None:
        return jnp.dot(a, b, preferred_element_type=F32)
    return lax.dot_general(a, b, dims, preferred_element_type=F32)


def _params(*semantics):
    return pltpu.CompilerParams(dimension_semantics=semantics or None, vmem_limit_bytes=VMEM_LIMIT)


def _line_width(half_window):
    return pl.cdiv(2 * Q_BLOCK + 2 * half_window - 1, LANES) * LANES


def _bucket_onehot(half_window, stride):
    rel = np.arange(_line_width(half_window)) - (Q_BLOCK - 1) - half_window
    band = np.abs(rel) <= half_window
    rel = rel * stride
    half, max_exact = N_BUCKETS // 2, N_BUCKETS // 4
    n = np.abs(rel)
    nf = np.maximum(n, max_exact).astype(np.float32)
    large = max_exact + (np.log(nf / np.float32(max_exact)) / np.float32(math.log(MAX_DISTANCE / max_exact))
                         * np.float32(half - max_exact)).astype(np.int32)
    large = np.minimum(large, half - 1)
    bucket = (rel > 0).astype(np.int32) * half + np.where(n < max_exact, n, large)
    onehot = (bucket[..., None] == np.arange(N_BUCKETS)) & band[..., None]
    return onehot.astype(np.float32), band


def _bias_lines(rel_bias_cols, half_window, stride):
    onehot, band = _bucket_onehot(half_window, stride)
    h = rel_bias_cols.shape[1]
    t = jnp.einsum("tb,bh->ht", jnp.asarray(onehot), rel_bias_cols, precision=lax.Precision.HIGHEST)
    t = t + jnp.asarray(np.where(band, 0.0, NEG_INF).astype(np.float32))
    return t.reshape(h // 2, 2, -1)


def _bias_grad(d_lines, half_window, stride):
    onehot, _ = _bucket_onehot(half_window, stride)
    h = d_lines.shape[0] * 2
    return jnp.einsum("tb,ht->bh", jnp.asarray(onehot), d_lines.reshape(h, -1), precision=lax.Precision.HIGHEST)


def _unroll_bias(line_ref, tile_ref, w):
    width = line_ref.shape[1]
    for j in range(2):
        rows = jnp.broadcast_to(line_ref[j:j + 1, :], (Q_BLOCK, width))
        rows = pltpu.roll(rows, width - (Q_BLOCK - 1), 1, stride=1, stride_axis=0)
        tile_ref[j * Q_BLOCK:(j + 1) * Q_BLOCK, :] = rows[:, :w]


def _fold_bias_grad(tile_ref, line_ref, w):
    width = line_ref.shape[1]
    row = lax.broadcasted_iota(jnp.int32, (Q_BLOCK, Q_BLOCK), 0)
    col = lax.broadcasted_iota(jnp.int32, (Q_BLOCK, Q_BLOCK), 1)
    flip = jnp.where(row + col == Q_BLOCK - 1, 1.0, 0.0).astype(BF16)
    for j in range(2):
        tile = tile_ref[j * Q_BLOCK:(j + 1) * Q_BLOCK, :]
        hi = tile.astype(BF16)
        lo = (tile - hi.astype(F32)).astype(BF16)
        rows = _dot(flip, hi) + _dot(flip, lo)
        rows = jnp.concatenate([rows, jnp.zeros((Q_BLOCK, width - w), F32)], axis=1)
        rows = pltpu.roll(rows, 0, 1, stride=1, stride_axis=0)
        line_ref[j:j + 1, :] = jnp.sum(rows, axis=0, keepdims=True)


def _row_tile(rows):
    return max(t for t in range(16, 385, 16) if rows % t == 0)


def _cast_rows(w, out_dtype, name):
    r, c = w.shape
    tr = _row_tile(r)

    def body(w_ref, o_ref):
        o_ref[...] = w_ref[...].astype(out_dtype)

    spec = pl.BlockSpec((tr, c), lambda i: (i, 0))
    return pl.pallas_call(
        body, name=name, grid=(r // tr,), in_specs=[spec], out_specs=spec,
        out_shape=jax.ShapeDtypeStruct((r, c), out_dtype), compiler_params=_params("arbitrary"),
    )(w)


STAGE_ROWS = 528


def _gather_scratch():
    return [pltpu.SemaphoreType.DMA((12,)), pltpu.SemaphoreType.DMA((12,)), pltpu.SemaphoreType.DMA((2,)),
            pltpu.SemaphoreType.DMA((2,)), pltpu.VMEM((2, STAGE_ROWS, D_MODEL), BF16)]


def _gather_phases(src_ref, out_ref, send_sems, recv_sems, in_sems, out_sems, stage):
    rows = src_ref.shape[0]
    x, y, c = lax.axis_index("x"), lax.axis_index("y"), lax.axis_index("c")
    sibling = (x, y, 1 - c)
    near = (x + (1 - c) - 2 * x * (1 - c), y + c - 2 * y * c)
    far = (x + c - 2 * x * c, y + (1 - c) - 2 * y * (1 - c))
    diag = (1 - x, 1 - y)
    chip_no = lambda chip: 2 * chip[0] + chip[1]
    my_chip = chip_no((x, y))

    pieces = 2 if (rows // 2) % 32 == 0 else 1
    n = rows // 2 // pieces

    def half_of(chip, half, p):
        start = pl.multiple_of(chip * rows + half * (rows // 2) + p * n, 16)
        return out_ref.at[pl.ds(start, n), :]

    def copy(k, p, src, dst, to):
        return pltpu.make_async_remote_copy(src_ref=src, dst_ref=dst, send_sem=send_sems.at[k * pieces + p],
                                            recv_sem=recv_sems.at[k * pieces + p], device_id=to, device_id_type=MESH)

    def mine(p):
        return src_ref.at[pl.ds(pl.multiple_of(c * (rows // 2) + p * n, 16), n), :]

    def keep_own():
        outs = []
        for i, r0 in enumerate(range(0, rows, STAGE_ROWS)):
            n = min(STAGE_ROWS, rows - r0)
            slot = i % 2
            if i >= 2:
                outs[i - 2].wait()
            buf = stage.at[slot, pl.ds(0, n), :]
            load = pltpu.make_async_copy(src_ref.at[pl.ds(r0, n), :], buf, in_sems.at[slot])
            load.start()
            load.wait()
            start = pl.multiple_of(my_chip * rows + r0, 16)
            outs.append(pltpu.make_async_copy(buf, out_ref.at[pl.ds(start, n), :], out_sems.at[slot]))
            outs[i].start()
        for cp in outs[-2:]:
            cp.wait()

    def start():
        for p in range(pieces):
            copy(0, p, mine(p), half_of(my_chip, c, p), (*near, c)).start()
            copy(1, p, mine(p), half_of(my_chip, c, p), (*far, c)).start()
        keep_own()

    def pass_on(j, p, chip):
        landed = half_of(chip_no(chip), c, p)
        copy(3 + j, p, landed, landed, sibling).start()

    def relay():
        for p in range(pieces):
            landed = half_of(chip_no(near), c, p)
            copy(0, p, landed, landed, sibling).wait_recv()
            copy(2, p, landed, landed, (*far, c)).start()
            pass_on(0, p, near)

    def forward():
        for j, chip in ((1, far), (2, diag)):
            for p in range(pieces):
                landed = half_of(chip_no(chip), c, p)
                copy(j, p, landed, landed, sibling).wait_recv()
                pass_on(j, p, chip)

    def finish():
        for j, chip in ((0, far), (1, near), (2, diag)):
            for p in range(pieces):
                other = half_of(chip_no(chip), 1 - c, p)
                copy(3 + j, p, other, other, sibling).wait_recv()
        for k in range(6):
            for p in range(pieces):
                copy(k, p, mine(p), mine(p), sibling).wait_send()

    return start, relay, forward, finish


def _gather_weights(shard):
    def body(src_ref, out_ref, *scratch):
        for phase in _gather_phases(src_ref, out_ref, *scratch):
            phase()

    return pl.pallas_call(
        body, name="gather_weights", in_specs=[ANY], out_specs=ANY,
        out_shape=jax.ShapeDtypeStruct((N_CHIPS * shard.shape[0], D_MODEL), BF16),
        scratch_shapes=_gather_scratch(),
    )(shard)


W_BLOCK = 768


def _w_blocks(first, count):
    return [pl.BlockSpec((W_BLOCK, D_MODEL), lambda *_, k=k: (first + k, 0)) for k in range(count)]


def _in_proj(x, gain, w_t, first_block, n_blocks, out_dtype, name, keep_h, ride=None):
    tm = 512
    n_steps = SEQ // tm
    n_out = 2 if keep_h else 1

    def body(x_ref, g_ref, *refs):
        w_refs, outs = refs[:n_blocks], refs[n_blocks + (ride is not None):n_blocks + (ride is not None) + n_out]
        if ride is not None:
            phases = _gather_phases(refs[n_blocks], *refs[n_blocks + 1 + n_out:])
            for step, phase in zip((0, 2, 4, n_steps - 1), phases):
                pl.when(pl.program_id(0) == step)(phase)
        xf = x_ref[...]
        r = lax.rsqrt(jnp.mean(xf * xf, axis=-1, keepdims=True) + EPS)
        h = ((xf * r) * g_ref[...]).astype(BF16)
        if keep_h:
            outs[1][...] = h
        for k, w_ref in enumerate(w_refs):
            outs[0][:, k * W_BLOCK:(k + 1) * W_BLOCK] = _dot(h, w_ref[...], NT).astype(out_dtype)

    riding = [] if ride is None else [ride]
    return pl.pallas_call(
        body, name=name, grid=(n_steps,),
        in_specs=[pl.BlockSpec((tm, D_MODEL), lambda i: (i, 0)), pl.BlockSpec((1, D_MODEL), lambda i: (0, 0))]
        + _w_blocks(first_block, n_blocks) + [ANY for _ in riding],
        out_specs=[pl.BlockSpec((tm, W_BLOCK * n_blocks), lambda i: (i, 0)),
                   pl.BlockSpec((tm, D_MODEL), lambda i: (i, 0))][:n_out] + [ANY for _ in riding],
        out_shape=[jax.ShapeDtypeStruct((SEQ, W_BLOCK * n_blocks), out_dtype),
                   jax.ShapeDtypeStruct((SEQ, D_MODEL), BF16)][:n_out]
        + [jax.ShapeDtypeStruct((N_CHIPS * r.shape[0], D_MODEL), BF16) for r in riding],
        scratch_shapes=_gather_scratch() if riding else [],
        compiler_params=_params("arbitrary"),
    )(x, gain, *([w_t] * n_blocks), *riding)


CHUNK = 256
CHUNK_UNROLL = 4
TILE_UNROLL = 8


def _low_half():
    return lax.broadcasted_iota(jnp.int32, (1, LANES), 1) < HEAD_DIM


def _half_sum(v, low):
    del low
    row = lax.broadcasted_iota(jnp.int32, (2 * LANES, LANES), 0)
    col = lax.broadcasted_iota(jnp.int32, (2 * LANES, LANES), 1)
    ones = jnp.where((row % LANES) // HEAD_DIM == col // HEAD_DIM, 1.0, 0.0).astype(BF16)
    hi = v.astype(BF16)
    lo = (v - hi.astype(F32)).astype(BF16)
    return _dot(jnp.concatenate([hi, lo], axis=1), ones)


def _chunks(fn, init=0):
    def body(i, carry):
        for u in range(CHUNK_UNROLL):
            carry = fn(pl.multiple_of((i * CHUNK_UNROLL + u) * CHUNK, CHUNK), carry)
        return carry

    return lax.fori_loop(0, SEQ // (CHUNK * CHUNK_UNROLL), body, init)


def _inv_rms(t, low):
    del low
    row = lax.broadcasted_iota(jnp.int32, (LANES, LANES), 0)
    col = lax.broadcasted_iota(jnp.int32, (LANES, LANES), 1)
    ones = jnp.where(row // HEAD_DIM == col // HEAD_DIM, 1.0, 0.0).astype(BF16)
    return lax.rsqrt(_dot((t * t).astype(BF16), ones) * (1.0 / HEAD_DIM) + EPS)


def _prep_q(q_ref, gain_ref, qn_ref):
    low = _low_half()

    def step(r0, carry):
        q = q_ref[pl.ds(r0, CHUNK), :].astype(F32)
        qn_ref[pl.ds(r0, CHUNK), :] = ((q * _inv_rms(q, low)) * gain_ref[...]) * SCALE
        return carry

    _chunks(step)


def _own_half(t, keep):
    return jnp.where(keep, t, pltpu.roll(t, HEAD_DIM, 1))


def _prep_kv(k_ref, v_ref, gain_ref, kp_ref, vp_ref, pad, keep=None):
    low = _low_half()
    zeros = jnp.zeros((pad, LANES), F32)
    for ref in (kp_ref, vp_ref):
        ref[pl.ds(0, pad), :] = zeros
        ref[pl.ds(pad + SEQ, pad), :] = zeros

    def step(r0, carry):
        k = k_ref[pl.ds(r0, CHUNK), :].astype(F32)
        v = v_ref[pl.ds(r0, CHUNK), :].astype(F32)
        kn = (k * _inv_rms(k, low)) * gain_ref[...]
        if keep is not None:
            kn, v = _own_half(kn, keep), _own_half(v, keep)
        kp_ref[pl.ds(pad + r0, CHUNK), :] = kn
        vp_ref[pl.ds(pad + r0, CHUNK), :] = v
        return carry

    _chunks(step)


def _tiles(d, half_window, fn):
    w = Q_BLOCK + 2 * half_window
    length = SEQ // d
    n_blocks = length // Q_BLOCK
    col = lax.broadcasted_iota(jnp.int32, (1, w), 1)

    def step(it, carry):
        c, n = it // n_blocks, it % n_blocks
        start = c + (d * Q_BLOCK) * n
        if d == 1:
            start = pl.multiple_of(start, Q_BLOCK)
            q_rows, k_rows = pl.ds(start, Q_BLOCK), pl.ds(start, w)
        else:
            q_rows, k_rows = pl.ds(start, Q_BLOCK, stride=d), pl.ds(start, w, stride=d)
        t = n * Q_BLOCK - half_window + col
        edge = jnp.where((t < 0) | (t >= length), NEG_INF, 0.0)
        fn(q_rows, k_rows, edge)
        return carry

    lax.fori_loop(0, d * n_blocks, step, 0, unroll=TILE_UNROLL)


def _stack_heads(t, low):
    return jnp.concatenate([jnp.where(low, t, 0.0), jnp.where(low, 0.0, t)], axis=0).astype(BF16)


def _unstack_heads(t, low):
    return jnp.where(low, t[:Q_BLOCK], t[Q_BLOCK:])


def _per_head(pair):
    return jnp.concatenate([jnp.full((Q_BLOCK, 1), pair[0], F32), jnp.full((Q_BLOCK, 1), pair[1], F32)], axis=0)


def _fwd_tiles(qn_ref, kp_ref, vp_ref, bias_ref, emit, *, d, half_window, sinks=None):
    low = _low_half()
    w = Q_BLOCK + 2 * half_window
    sink = None if sinks is None else _per_head(sinks)

    def tile(q_rows, k_rows, edge):
        q2 = _stack_heads(qn_ref[q_rows, :], low)
        k = kp_ref[k_rows, :].astype(BF16)
        v1 = jnp.concatenate([vp_ref[k_rows, :], jnp.ones((w, LANES), F32)], axis=1).astype(BF16)
        s = _dot(q2, k, NT) + bias_ref[...] + edge
        m = jnp.max(s, axis=-1, keepdims=True)
        if sink is not None:
            m = jnp.maximum(m, sink)
        o = _dot(jnp.exp(s - m).astype(BF16), v1)
        l = o[:, LANES:]
        if sink is not None:
            l = l + jnp.exp(sink - m)
        emit(q_rows, _unstack_heads(o[:, :LANES] * (1.0 / l), low), _unstack_heads(m + jnp.log(l), low))

    _tiles(d, half_window, tile)


def _bwd_tiles(qn_ref, kp_ref, vp_ref, bias_ref, do_ref, lse_ref, delta_ref, dq_ref, dk_ref, dv_ref, ds_ref,
               *, d, half_window, sinks=None, dsink_ref=None):
    low = _low_half()
    w = Q_BLOCK + 2 * half_window
    sink = None if sinks is None else _per_head(sinks)

    def rows_of(t):
        return jnp.concatenate([t[:, 0:1], t[:, HEAD_DIM:HEAD_DIM + 1]], axis=0)

    def tile(q_rows, k_rows, edge):
        q2 = _stack_heads(qn_ref[q_rows, :], low)
        do2 = _stack_heads(do_ref[q_rows, :], low)
        k = kp_ref[k_rows, :].astype(BF16)
        v = vp_ref[k_rows, :].astype(BF16)
        lse = rows_of(lse_ref[q_rows, :])
        delta = rows_of(delta_ref[q_rows, :])
        p = jnp.exp(_dot(q2, k, NT) + bias_ref[...] + edge - lse)
        ds = p * (_dot(do2, v, NT) - delta)
        ds_ref[...] += ds
        if sink is not None:
            dsink_ref[...] += (-jnp.exp(sink - lse) * delta).reshape(2, Q_BLOCK, 1)
        dsb, pb = ds.astype(BF16), p.astype(BF16)
        dq_ref[q_rows, :] = _unstack_heads(_dot(dsb, k), low)
        dk_ref[k_rows, :] += _dot(dsb, q2, TN)
        dv_ref[k_rows, :] += _dot(pb, do2, TN)

    _tiles(d, half_window, tile)


def _norm_bwd(raw_ref, gain_ref, dn_ref, dn_offset, out_ref, scale):
    low = _low_half()

    def step(r0, dgain):
        t = raw_ref[pl.ds(r0, CHUNK), :].astype(F32)
        dn = dn_ref[pl.ds(dn_offset + r0, CHUNK), :]
        dth = dn * (gain_ref[...] * scale)
        r = _inv_rms(t, low)
        th = t * r
        out_ref[pl.ds(r0, CHUNK), :] = (r * (dth - th * (r * _half_sum(dth * t, low) * (1.0 / HEAD_DIM)))).astype(BF16)
        return dgain + jnp.sum(dn * th, axis=0, keepdims=True) * scale

    return _chunks(step, jnp.zeros((1, LANES), F32))


def _rows8(v):
    return jnp.broadcast_to(v, (8, v.shape[-1]))


A_W = Q_BLOCK + 2 * A_HALF_WINDOW
A_PAD = A_HALF_WINDOW


def _seq_block(col_fn):
    return pl.BlockSpec((SEQ, LANES), col_fn)


def _attn_a_fwd(qkv, gain_q, gain_k, bias, sink):
    def body(sink_ref, q_ref, k_ref, v_ref, gq_ref, gk_ref, line_ref, o_ref, lse_ref, qn_ref, kp_ref, vp_ref,
             bias_ref):
        hp = pl.program_id(0)
        keep = (lax.broadcasted_iota(jnp.int32, (1, LANES), 1) // HEAD_DIM) == hp // 2
        _prep_q(q_ref, gq_ref, qn_ref)
        _prep_kv(k_ref, v_ref, gk_ref, kp_ref, vp_ref, A_PAD, keep)
        _unroll_bias(line_ref, bias_ref, A_W)

        def emit(rows, out, lse):
            o_ref[rows, :] = out
            lse_ref[rows, :] = lse

        _fwd_tiles(qn_ref, kp_ref, vp_ref, bias_ref, emit, d=1, half_window=A_HALF_WINDOW,
                   sinks=(sink_ref[2 * hp], sink_ref[2 * hp + 1]))

    vec = pl.BlockSpec((1, LANES), lambda hp, s: (0, 0))
    return pl.pallas_call(
        body, name="attn_a_fwd",
        grid_spec=pltpu.PrefetchScalarGridSpec(
            num_scalar_prefetch=1, grid=(4,),
            in_specs=[_seq_block(lambda hp, s: (0, QA_BLK + hp)), _seq_block(lambda hp, s: (0, KA_BLK)),
                      _seq_block(lambda hp, s: (0, VA_BLK)), vec, vec,
                      pl.BlockSpec((None, 2, _line_width(A_HALF_WINDOW)), lambda hp, s: (hp, 0, 0))],
            out_specs=[_seq_block(lambda hp, s: (0, hp)), _seq_block(lambda hp, s: (0, hp))],
            scratch_shapes=[pltpu.VMEM((SEQ, LANES), F32), pltpu.VMEM((SEQ + 2 * A_PAD, LANES), F32),
                            pltpu.VMEM((SEQ + 2 * A_PAD, LANES), F32), pltpu.VMEM((2 * Q_BLOCK, A_W), F32)]),
        out_shape=[jax.ShapeDtypeStruct((SEQ, 512), F32)] * 2,
        compiler_params=_params("arbitrary"),
    )(sink.reshape(8), qkv, qkv, qkv, gain_q, gain_k, bias)


def _attn_a_bwd(qkv, gain_q, gain_k, bias, sink, delta, lse, d_out):
    def body(sink_ref, q_ref, k_ref, v_ref, gq_ref, gk_ref, line_ref, delta_ref, lse_ref, do_ref,
             dq_out, dkv_out, dgq_out, dgk_out, dline_out, dsink_out,
             qn_ref, kp_ref, vp_ref, dq_ref, dk_ref, dv_ref, dk_tot, dv_tot, bias_ref, ds_out):
        hp = pl.program_id(0)
        kv_head = hp // 2
        keep = (lax.broadcasted_iota(jnp.int32, (1, LANES), 1) // HEAD_DIM) == kv_head
        _prep_q(q_ref, gq_ref, qn_ref)
        _prep_kv(k_ref, v_ref, gk_ref, kp_ref, vp_ref, A_PAD, keep)
        _unroll_bias(line_ref, bias_ref, A_W)
        ds_out[...] = jnp.zeros_like(ds_out)
        dsink_out[...] = jnp.zeros_like(dsink_out)

        @pl.when(hp % 2 == 0)
        def _():
            dk_ref[...] = jnp.zeros_like(dk_ref)
            dv_ref[...] = jnp.zeros_like(dv_ref)

        @pl.when(hp == 0)
        def _():
            dk_tot[...] = jnp.zeros_like(dk_tot)
            dv_tot[...] = jnp.zeros_like(dv_tot)

        _bwd_tiles(qn_ref, kp_ref, vp_ref, bias_ref, do_ref, lse_ref, delta_ref, dq_ref, dk_ref, dv_ref, ds_out,
                   d=1, half_window=A_HALF_WINDOW, sinks=(sink_ref[2 * hp], sink_ref[2 * hp + 1]),
                   dsink_ref=dsink_out)
        _fold_bias_grad(ds_out, dline_out, A_W)
        dgq_out[...] = _rows8(_norm_bwd(q_ref, gq_ref, dq_ref, 0, dq_out, SCALE))

        def fold(r0, carry):
            rows = pl.ds(A_PAD + r0, CHUNK)
            for acc, tot in ((dk_ref, dk_tot), (dv_ref, dv_tot)):
                t = acc[rows, :]
                tot[pl.ds(r0, CHUNK), :] += jnp.where(keep, t + pltpu.roll(t, HEAD_DIM, 1), 0.0)
            return carry

        @pl.when(hp % 2 == 1)
        def _():
            _chunks(fold)

        @pl.when(hp == 3)
        def _():
            dgk_out[...] = _rows8(_norm_bwd(k_ref, gk_ref, dk_tot, 0, dkv_out.at[0], 1.0))
            dkv_out[1] = dv_tot[...].astype(BF16)

    vec = pl.BlockSpec((1, LANES), lambda hp, s: (0, 0))
    seq_f32 = pltpu.VMEM((SEQ, LANES), F32)
    padded = pltpu.VMEM((SEQ + 2 * A_PAD, LANES), F32)
    return pl.pallas_call(
        body, name="attn_a_bwd",
        grid_spec=pltpu.PrefetchScalarGridSpec(
            num_scalar_prefetch=1, grid=(4,),
            in_specs=[_seq_block(lambda hp, s: (0, QA_BLK + hp)), _seq_block(lambda hp, s: (0, KA_BLK)),
                      _seq_block(lambda hp, s: (0, VA_BLK)), vec, vec,
                      pl.BlockSpec((None, 2, _line_width(A_HALF_WINDOW)), lambda hp, s: (hp, 0, 0)),
                      _seq_block(lambda hp, s: (0, hp)), _seq_block(lambda hp, s: (0, hp)),
                      _seq_block(lambda hp, s: (0, hp))],
            out_specs=[pl.BlockSpec((None, SEQ, LANES), lambda hp, s: (hp, 0, 0)),
                       pl.BlockSpec((2, SEQ, LANES), lambda hp, s: (0, 0, 0)),
                       pl.BlockSpec((None, 8, LANES), lambda hp, s: (hp, 0, 0)),
                       pl.BlockSpec((8, LANES), lambda hp, s: (0, 0)),
                       pl.BlockSpec((None, 2, _line_width(A_HALF_WINDOW)), lambda hp, s: (hp, 0, 0)),
                       pl.BlockSpec((None, 2, Q_BLOCK, 1), lambda hp, s: (hp, 0, 0, 0))],
            scratch_shapes=[seq_f32, padded, padded, seq_f32, padded, padded, seq_f32, seq_f32,
                            pltpu.VMEM((2 * Q_BLOCK, A_W), F32), pltpu.VMEM((2 * Q_BLOCK, A_W), F32)]),
        out_shape=[jax.ShapeDtypeStruct((4, SEQ, LANES), BF16), jax.ShapeDtypeStruct((2, SEQ, LANES), BF16),
                   jax.ShapeDtypeStruct((4, 8, LANES), F32), jax.ShapeDtypeStruct((8, LANES), F32),
                   jax.ShapeDtypeStruct((4, 2, _line_width(A_HALF_WINDOW)), F32),
                   jax.ShapeDtypeStruct((4, 2, Q_BLOCK, 1), F32)],
        compiler_params=_params("arbitrary"),
    )(sink.reshape(8), qkv, qkv, qkv, gain_q, gain_k, bias, delta, lse, d_out)


B_W = Q_BLOCK + 2 * B_HALF_WINDOW
B_PAD_MAX = B_HALF_WINDOW * B_DILATIONS[-1]


def _attn_b_fwd(qkv, gain_q, gain_k, bias):
    def body(q_ref, k_ref, v_ref, gq_ref, gk_ref, line_ref, o_ref, lse_ref, qn_ref, kp_ref, vp_ref, bias_ref):
        g = pl.program_id(1)
        _prep_q(q_ref, gq_ref, qn_ref)
        _unroll_bias(line_ref, bias_ref, B_W)

        def first(rows, out, lse):
            o_ref[rows, :] = out
            lse_ref[rows, :] = lse

        def combine(rows, out, lse):
            old = lse_ref[rows, :]
            new = jnp.maximum(old, lse) + jnp.log(1.0 + jnp.exp(-jnp.abs(old - lse)))
            o_ref[rows, :] = o_ref[rows, :] * jnp.exp(old - new) + out * jnp.exp(lse - new)
            lse_ref[rows, :] = new

        for gi, d in enumerate(B_DILATIONS):
            @pl.when(g == gi)
            def _():
                _prep_kv(k_ref, v_ref, gk_ref, kp_ref, vp_ref, B_HALF_WINDOW * d)
                _fwd_tiles(qn_ref, kp_ref, vp_ref, bias_ref, first if gi == 0 else combine,
                           d=d, half_window=B_HALF_WINDOW)

    vec = pl.BlockSpec((1, LANES), lambda hp, g: (0, 0))
    padded = pltpu.VMEM((SEQ + 2 * B_PAD_MAX, LANES), F32)
    return pl.pallas_call(
        body, name="attn_b_fwd", grid=(4, 3),
        in_specs=[_seq_block(lambda hp, g: (0, QB_BLK + 4 * g + hp)), _seq_block(lambda hp, g: (0, KB_BLK + 4 * g + hp)),
                  _seq_block(lambda hp, g: (0, VB_BLK + 4 * g + hp)), vec, vec,
                  pl.BlockSpec((None, 2, _line_width(B_HALF_WINDOW)), lambda hp, g: (4 * g + hp, 0, 0))],
        out_specs=[_seq_block(lambda hp, g: (0, hp)), _seq_block(lambda hp, g: (0, hp))],
        out_shape=[jax.ShapeDtypeStruct((SEQ, 512), F32)] * 2,
        scratch_shapes=[pltpu.VMEM((SEQ, LANES), F32), padded, padded, pltpu.VMEM((2 * Q_BLOCK, B_W), F32)],
        compiler_params=_params("arbitrary", "arbitrary"),
    )(qkv, qkv, qkv, gain_q, gain_k, bias)


def _attn_b_bwd(qkv, gain_q, gain_k, bias, delta, lse, d_out):
    def body(q_ref, k_ref, v_ref, gq_ref, gk_ref, line_ref, delta_ref, lse_ref, do_ref,
             dq_out, dk_out, dv_out, dgq_out, dgk_out, dline_out,
             qn_ref, kp_ref, vp_ref, dq_ref, dk_ref, dv_ref, bias_ref, ds_out):
        g = pl.program_id(1)
        _prep_q(q_ref, gq_ref, qn_ref)
        _unroll_bias(line_ref, bias_ref, B_W)
        ds_out[...] = jnp.zeros_like(ds_out)
        for gi, d in enumerate(B_DILATIONS):
            @pl.when(g == gi)
            def _():
                pad = B_HALF_WINDOW * d
                for acc in (dk_ref, dv_ref):
                    acc[pl.ds(0, SEQ + 2 * pad), :] = jnp.zeros((SEQ + 2 * pad, LANES), F32)
                _prep_kv(k_ref, v_ref, gk_ref, kp_ref, vp_ref, pad)
                _bwd_tiles(qn_ref, kp_ref, vp_ref, bias_ref, do_ref, lse_ref, delta_ref, dq_ref, dk_ref, dv_ref,
                           ds_out, d=d, half_window=B_HALF_WINDOW)
                dgk_out[...] = _rows8(_norm_bwd(k_ref, gk_ref, dk_ref, pad, dk_out, 1.0))
                dv_out[...] = dv_ref[pl.ds(pad, SEQ), :].astype(BF16)
        _fold_bias_grad(ds_out, dline_out, B_W)
        dgq_out[...] = _rows8(_norm_bwd(q_ref, gq_ref, dq_ref, 0, dq_out, SCALE))

    vec = pl.BlockSpec((1, LANES), lambda hp, g: (0, 0))
    seq_f32 = pltpu.VMEM((SEQ, LANES), F32)
    padded = pltpu.VMEM((SEQ + 2 * B_PAD_MAX, LANES), F32)
    part = pl.BlockSpec((None, 8, LANES), lambda hp, g: (4 * g + hp, 0, 0))
    line = pl.BlockSpec((None, 2, _line_width(B_HALF_WINDOW)), lambda hp, g: (4 * g + hp, 0, 0))
    return pl.pallas_call(
        body, name="attn_b_bwd", grid=(4, 3),
        in_specs=[_seq_block(lambda hp, g: (0, QB_BLK + 4 * g + hp)), _seq_block(lambda hp, g: (0, KB_BLK + 4 * g + hp)),
                  _seq_block(lambda hp, g: (0, VB_BLK + 4 * g + hp)), vec, vec,
                  line,
                  _seq_block(lambda hp, g: (0, hp)), _seq_block(lambda hp, g: (0, hp)), _seq_block(lambda hp, g: (0, hp))],
        out_specs=[pl.BlockSpec((None, SEQ, LANES), lambda hp, g: (4 * g + hp, 0, 0))] * 3 + [part, part, line],
        out_shape=[jax.ShapeDtypeStruct((12, SEQ, LANES), BF16)] * 3
        + [jax.ShapeDtypeStruct((12, 8, LANES), F32)] * 2
        + [jax.ShapeDtypeStruct((12, 2, _line_width(B_HALF_WINDOW)), F32)],
        scratch_shapes=[seq_f32, padded, padded, seq_f32, padded, padded,
                        pltpu.VMEM((2 * Q_BLOCK, B_W), F32), pltpu.VMEM((2 * Q_BLOCK, B_W), F32)],
        compiler_params=_params("arbitrary", "arbitrary"),
    )(qkv, qkv, qkv, gain_q, gain_k, bias, delta, lse, d_out)


def _sigmoid(t):
    return 1.0 / (1.0 + jnp.exp(-t))


def _middle(out_a, out_b, gates, x, target, w_a, w_b, w_out, b_merge):
    tm = 256
    n_steps = SEQ // tm

    def body(oa_ref, ob_ref, g_ref, x_ref, t_ref, wa_ref, wb_ref, wo_ref, bm_ref,
             dy_ref, dg_ref, doa_ref, dob_ref, dla_ref, dlb_ref, dwa_ref, dwb_ref, dwo_ref, dbm_ref, sq_ref):
        @pl.when(pl.program_id(0) == 0)
        def _():
            for ref in (dwa_ref, dwb_ref, dwo_ref, dbm_ref, sq_ref):
                ref[...] = jnp.zeros_like(ref)

        gate_a, gate_b = g_ref[:, 0:512], g_ref[:, 512:1024]
        sig_a, sig_b = _sigmoid(gate_a), _sigmoid(gate_b)
        silu_a, silu_b = gate_a * sig_a, gate_b * sig_b
        oa, ob = oa_ref[...], ob_ref[...]
        ya, yb = (oa * silu_a).astype(BF16), (ob * silu_b).astype(BF16)
        br_a, br_b = _dot(ya, wa_ref[...]), _dot(yb, wb_ref[...])
        m0 = _sigmoid(g_ref[:, 1024:2048] + bm_ref[0:1, :])
        m1 = _sigmoid(g_ref[:, 2048:3072] + bm_ref[1:2, :])
        merged = (m0 * br_a + m1 * br_b).astype(BF16)
        err = (x_ref[...] + _dot(merged, wo_ref[...])) - t_ref[...]
        sq_ref[...] += jnp.sum(err * err, axis=0, keepdims=True)

        dy = err * (1.0 / D_MODEL)
        dy_ref[...] = dy
        dyb = dy.astype(BF16)
        dmerged = _dot(dyb, wo_ref[...], NT)
        dwo_ref[...] += _dot(merged, dyb, TN)
        dbr_a, dbr_b = (dmerged * m0).astype(BF16), (dmerged * m1).astype(BF16)
        dm0 = (dmerged * br_a) * (m0 * (1.0 - m0))
        dm1 = (dmerged * br_b) * (m1 * (1.0 - m1))
        dbm_ref[0:1, :] += jnp.sum(dm0, axis=0, keepdims=True)
        dbm_ref[1:2, :] += jnp.sum(dm1, axis=0, keepdims=True)
        for s in range(N_CHIPS):
            cols = slice(256 * s, 256 * (s + 1))
            dwa_ref[s] += _dot(ya, dbr_a[:, cols], TN)
            dwb_ref[s] += _dot(yb, dbr_b[:, cols], TN)
        dya, dyb_ = _dot(dbr_a, wa_ref[...], NT), _dot(dbr_b, wb_ref[...], NT)
        doa, dob = dya * silu_a, dyb_ * silu_b
        doa_ref[...] = doa
        dob_ref[...] = dob
        for blk in range(512 // LANES):
            lanes = slice(blk * LANES, (blk + 1) * LANES)
            dla_ref[:, lanes] = _half_sum(doa[:, lanes] * oa[:, lanes], None)
            dlb_ref[:, lanes] = _half_sum(dob[:, lanes] * ob[:, lanes], None)
        d_gates = (((dya * oa) * (sig_a * (1.0 + gate_a * (1.0 - sig_a)))).astype(BF16),
                   ((dyb_ * ob) * (sig_b * (1.0 + gate_b * (1.0 - sig_b)))).astype(BF16),
                   dm0.astype(BF16), dm1.astype(BF16))
        blk = 0
        for part in d_gates:
            for c0 in range(0, part.shape[1], 256):
                dg_ref[blk] = part[:, c0:c0 + 256]
                blk += 1

    def rows(width):
        return pl.BlockSpec((tm, width), lambda i: (i, 0))

    def whole(*shape):
        return pl.BlockSpec(shape, lambda i: (0,) * len(shape))

    return pl.pallas_call(
        body, name="middle", grid=(n_steps,),
        in_specs=[rows(512), rows(512), rows(GATE_WIDTH), rows(D_MODEL), rows(D_MODEL),
                  whole(512, D_MODEL), whole(512, D_MODEL), whole(D_MODEL, D_MODEL), whole(2, D_MODEL)],
        out_specs=[rows(D_MODEL), pl.BlockSpec((GATE_WIDTH // 256, tm, 256), lambda i: (0, i, 0)),
                   rows(512), rows(512), rows(512), rows(512),
                   whole(N_CHIPS, 512, 256), whole(N_CHIPS, 512, 256), whole(D_MODEL, D_MODEL),
                   whole(2, D_MODEL), whole(1, D_MODEL)],
        out_shape=[jax.ShapeDtypeStruct((SEQ, D_MODEL), F32), jax.ShapeDtypeStruct((GATE_WIDTH // 256, SEQ, 256), BF16),
                   jax.ShapeDtypeStruct((SEQ, 512), F32), jax.ShapeDtypeStruct((SEQ, 512), F32),
                   jax.ShapeDtypeStruct((SEQ, 512), F32), jax.ShapeDtypeStruct((SEQ, 512), F32),
                   jax.ShapeDtypeStruct((N_CHIPS, 512, 256), F32), jax.ShapeDtypeStruct((N_CHIPS, 512, 256), F32),
                   jax.ShapeDtypeStruct((D_MODEL, D_MODEL), F32), jax.ShapeDtypeStruct((2, D_MODEL), F32),
                   jax.ShapeDtypeStruct((1, D_MODEL), F32)],
        compiler_params=_params("arbitrary"),
    )(out_a, out_b, gates, x, target, w_a, w_b, w_out, b_merge)


def _which(j, edges, fns):
    lo = 0
    for hi, fn in zip(edges, fns):
        pl.when((j >= lo) & (j < hi))(fn)
        lo = hi


def _sibling_rows(tile, core):
    lo, hi = tile * W_BLOCK, (tile + 1) * W_BLOCK
    for chip in range(N_CHIPS):
        a = chip * W_IN_SHARD + (1 - core) * (W_IN_SHARD // 2)
        first, last = max(lo, a), min(hi, a + W_IN_SHARD // 2)
        if first < last:
            return chip, first - a, first - lo, last - first
    return None


def _d_w_in(d_proj, h, rest=None):
    plan, step, width = [], 0, 0
    for p in d_proj:
        total = p.shape[0] * p.shape[2]
        if width + total <= W_BLOCK:
            plan.append((p.shape[0], step, 1))
            width += total
            if width == W_BLOCK:
                step, width = step + 1, 0
        else:
            assert width == 0 and total % W_BLOCK == 0
            plan.append((W_BLOCK // p.shape[2], step, total // W_BLOCK))
            step += total // W_BLOCK
    assert width == 0 and step == IN_WIDTH // W_BLOCK
    firsts = sorted({first for _, first, _ in plan})
    edges = firsts[1:] + [step]
    halves = 2

    hand_over = rest is not None
    half = W_IN_SHARD // 2

    def body(*refs):
        if hand_over:
            pieces, h_ref, rest_ref = refs[:len(d_proj)], refs[len(d_proj)], refs[len(d_proj) + 1]
            o_ref, got_ref, got_rest_ref, acc_ref, send_sems, recv_sems, stage = refs[len(d_proj) + 2:]
        else:
            pieces, h_ref, o_ref, acc_ref = refs[:-3], refs[-3], refs[-2], refs[-1]
        k = pl.program_id(1)

        def emit(group):
            def fn():
                cols = jnp.concatenate([ref[b] for ref in group for b in range(ref.shape[0])], axis=1)
                term = _dot(cols, h_ref[...], TN)

                @pl.when(k == 0)
                def _():
                    acc_ref[...] = term

                @pl.when(k == halves - 1)
                def _():
                    o_ref[...] = (acc_ref[...] + term).astype(BF16)
            return fn

        groups = [[ref for ref, (_, first, _) in zip(pieces, plan) if first == f] for f in firsts]
        _which(pl.program_id(0), edges, [emit(group) for group in groups])

        if hand_over:
            cx, cy, c = lax.axis_index("x"), lax.axis_index("y"), lax.axis_index("c")
            sibling = (cx, cy, 1 - c)

            def to_sibling(sem, src, dst, recv=0):
                return pltpu.make_async_remote_copy(src_ref=src, dst_ref=dst, send_sem=send_sems.at[sem],
                                                    recv_sem=recv_sems.at[recv], device_id=sibling, device_id_type=MESH)

            def tile_copy(tile, core):
                chip, row, start, rows = _sibling_rows(tile, core)
                return to_sibling(tile % 2, stage.at[tile % 2, pl.ds(0, rows), :], got_ref.at[chip, pl.ds(row, rows), :])

            rest_copy = to_sibling(2, _half_rows(rest_ref, 1 - c), got_rest_ref, recv=1)

            @pl.when((pl.program_id(0) == 0) & (k == 0))
            def _():
                rest_copy.start()

            for tile in range(step):
                for core in range(2):
                    @pl.when((pl.program_id(0) == tile) & (k == halves - 1) & (c == core))
                    def _(tile=tile, core=core):
                        if tile >= 2 and _sibling_rows(tile - 2, core):
                            tile_copy(tile - 2, core).wait_send()
                        if _sibling_rows(tile, core):
                            _, _, start, rows = _sibling_rows(tile, core)
                            stage[tile % 2, 0:rows, :] = o_ref[start:start + rows, :]
                            tile_copy(tile, core).start()
                        if tile == step - 1:
                            for last in (step - 2, step - 1):
                                if _sibling_rows(last, core):
                                    tile_copy(last, core).wait_send()
                            rest_copy.wait()
                            to_sibling(0, got_ref, got_ref).wait_recv()

    def cols_spec(piece, n, first, steps):
        def index(j, k):
            return jnp.clip(j - first, 0, steps - 1), jnp.where((j >= first) & (j < first + steps), k, 0), 0
        return pl.BlockSpec((n, SEQ // halves, piece.shape[2]), index)

    tile_spec = pl.BlockSpec((W_BLOCK, D_MODEL), lambda j, k: (j, 0))
    in_specs = [cols_spec(p, *pl_) for p, pl_ in zip(d_proj, plan)] + [
        pl.BlockSpec((SEQ // halves, D_MODEL), lambda j, k: (k, 0))]
    acc = pltpu.VMEM((W_BLOCK, D_MODEL), F32)
    if not hand_over:
        return pl.pallas_call(
            body, name="d_w_in", grid=(step, halves), in_specs=in_specs, out_specs=tile_spec,
            out_shape=jax.ShapeDtypeStruct((IN_WIDTH, D_MODEL), BF16), scratch_shapes=[acc],
            compiler_params=_params("arbitrary", "arbitrary"),
        )(*d_proj, h)
    return pl.pallas_call(
        body, name="d_w_in", grid=(step, halves), in_specs=in_specs + [ANY], out_specs=[tile_spec, ANY, ANY],
        out_shape=[jax.ShapeDtypeStruct((IN_WIDTH, D_MODEL), BF16),
                   jax.ShapeDtypeStruct((N_CHIPS, half, D_MODEL), BF16),
                   jax.ShapeDtypeStruct((N_CHIPS, rest.shape[1] // 2, D_MODEL), BF16)],
        scratch_shapes=[acc, pltpu.SemaphoreType.DMA((3,)), pltpu.SemaphoreType.DMA((2,)),
                        pltpu.VMEM((2, W_BLOCK, D_MODEL), BF16)],
        compiler_params=_params("arbitrary", "arbitrary"),
    )(*d_proj, h, rest)


RELAY_STEP = 10
RELAY_ROWS = 352


def _d_x(d_proj, w_t, x, gain, dy, chip_sums):
    tm = 256
    n_steps = SEQ // tm
    n_w = IN_WIDTH // W_BLOCK
    n_p, n_s = len(d_proj), len(chip_sums)

    def body(*refs):
        pieces, w_refs = refs[:n_p], refs[n_p:n_p + n_w]
        x_ref, g_ref, dy_ref = refs[n_p + n_w:n_p + n_w + 3]
        q_refs = refs[n_p + n_w + 3:n_p + n_w + 3 + n_s]
        dx_ref, dgain_ref = refs[n_p + n_w + 3 + n_s:n_p + n_w + 5 + n_s]
        outs = refs[n_p + n_w + 5 + n_s:n_p + n_w + 5 + 4 * n_s]
        got_refs, relay_refs, sum_refs = outs[:n_s], outs[n_s:2 * n_s], outs[2 * n_s:]
        if n_s:
            send_sems, recv_sems, local_sems, a_buf, b_buf, c_buf = refs[n_p + n_w + 5 + 4 * n_s:]

        def hops():
            cx, cy, c = lax.axis_index("x"), lax.axis_index("y"), lax.axis_index("c")
            near = (cx + (1 - c) - 2 * cx * (1 - c), cy + c - 2 * cy * c)
            far = (cx + c - 2 * cx * c, cy + (1 - c) - 2 * cy * (1 - c))
            chip = lambda p: 2 * p[0] + p[1]

            def copy(k, src, dst, to):
                return pltpu.make_async_remote_copy(src_ref=src, dst_ref=dst, send_sem=send_sems.at[k],
                                                    recv_sem=recv_sems.at[k], device_id=(*to, c), device_id_type=MESH)

            first = [(copy(3 * b, q.at[chip(near)], got.at[0], near),
                      copy(3 * b + 1, q.at[3 - chip((cx, cy))], relay, near))
                     for b, (q, got, relay) in enumerate(zip(q_refs, got_refs, relay_refs))]
            second = [copy(3 * b + 2, s, got.at[1], far) for b, (s, got) in enumerate(zip(sum_refs, got_refs))]
            return first, second, chip(far)

        @pl.when(pl.program_id(0) == 0)
        def _():
            dgain_ref[...] = jnp.zeros_like(dgain_ref)
            if n_s:
                for direct, pass_on in hops()[0]:
                    direct.start()
                    pass_on.start()

        if n_s:
            @pl.when(pl.program_id(0) == RELAY_STEP)
            def _():
                first, second, far_chip = hops()
                for b, (q, relay, total) in enumerate(zip(q_refs, relay_refs, sum_refs)):
                    first[b][1].wait_recv()
                    half = relay.shape[0]
                    for r0 in range(0, half, RELAY_ROWS):
                        rows = min(RELAY_ROWS, half - r0)
                        mine = pltpu.make_async_copy(q.at[far_chip, pl.ds(r0, rows), :], a_buf.at[pl.ds(0, rows), :],
                                                     local_sems.at[0])
                        theirs = pltpu.make_async_copy(relay.at[pl.ds(r0, rows), :], b_buf.at[pl.ds(0, rows), :],
                                                       local_sems.at[1])
                        mine.start()
                        theirs.start()
                        mine.wait()
                        theirs.wait()
                        c_buf[0:rows, :] = (a_buf[0:rows, :].astype(F32) + b_buf[0:rows, :].astype(F32)).astype(BF16)
                        store = pltpu.make_async_copy(c_buf.at[pl.ds(0, rows), :], total.at[pl.ds(r0, rows), :],
                                                      local_sems.at[2])
                        store.start()
                        store.wait()
                    second[b].start()

        blocks = [(piece, k) for piece in pieces for k in range(piece.shape[0])]
        dh, group, width, blk = None, [], 0, 0
        for piece, k in blocks:
            group.append(piece[k])
            width += piece.shape[2]
            if width == W_BLOCK:
                term = _dot(jnp.concatenate(group, axis=1), w_refs[blk][...])
                dh = term if dh is None else dh + term
                group, width, blk = [], 0, blk + 1
        assert not group and blk == n_w
        xf = x_ref[...]
        r = lax.rsqrt(jnp.mean(xf * xf, axis=-1, keepdims=True) + EPS)
        xh = xf * r
        dxh = dh * g_ref[...]
        dx_ref[...] = r * (dxh - xh * jnp.mean(dxh * xh, axis=-1, keepdims=True)) + dy_ref[...]
        dgain_ref[...] += _rows8(jnp.sum(dh * xh, axis=0, keepdims=True))

        if n_s:
            @pl.when(pl.program_id(0) == n_steps - 1)
            def _():
                first, second, _ = hops()
                for direct, pass_on in first:
                    direct.wait()
                    pass_on.wait_send()
                for cp in second:
                    cp.wait()

    row = pl.BlockSpec((tm, D_MODEL), lambda i: (i, 0))
    halves = [q.shape[1] for q in chip_sums]
    res = pl.pallas_call(
        body, name="d_x", grid=(n_steps,),
        in_specs=[pl.BlockSpec((p.shape[0], tm, p.shape[2]), lambda i: (0, i, 0)) for p in d_proj] + _w_blocks(0, n_w)
        + [row, pl.BlockSpec((1, D_MODEL), lambda i: (0, 0)), row] + [ANY] * n_s,
        out_specs=[row, pl.BlockSpec((8, D_MODEL), lambda i: (0, 0))] + [ANY] * (3 * n_s),
        out_shape=[jax.ShapeDtypeStruct((SEQ, D_MODEL), F32), jax.ShapeDtypeStruct((8, D_MODEL), F32)]
        + [jax.ShapeDtypeStruct((2, half, D_MODEL), BF16) for half in halves]
        + [jax.ShapeDtypeStruct((half, D_MODEL), BF16) for half in halves] * 2,
        scratch_shapes=[pltpu.SemaphoreType.DMA((3 * n_s,)), pltpu.SemaphoreType.DMA((3 * n_s,)),
                        pltpu.SemaphoreType.DMA((3,))] + [pltpu.VMEM((RELAY_ROWS, D_MODEL), BF16)] * 3 if n_s else [],
        compiler_params=_params("arbitrary"),
    )(*d_proj, *([w_t] * n_w), x, gain, dy, *chip_sums)
    return res[0], res[1], res[2:2 + n_s]


def _my_place():
    x, y, c = lax.axis_index("x"), lax.axis_index("y"), lax.axis_index("c")
    return jnp.stack([2 * x + y, c]).astype(jnp.int32)


def _half_rows(ref, half):
    rows = ref.shape[-2] // 2
    idx = (slice(None),) * (len(ref.shape) - 2) + (pl.ds(pl.multiple_of(half * rows, 16), rows), slice(None))
    return ref.at[idx]


def _add_halves(place, grads, theirs, name):
    half = theirs.shape[1]
    tr = _row_tile(half)
    n = half // tr

    def body(place_ref, g_ref, t_ref, o_ref):
        o_ref[...] = (g_ref[...].astype(F32) + t_ref[...].astype(F32)).astype(BF16)

    return pl.pallas_call(
        body, name=name,
        grid_spec=pltpu.PrefetchScalarGridSpec(
            num_scalar_prefetch=1, grid=(N_CHIPS, n),
            in_specs=[pl.BlockSpec((None, tr, D_MODEL), lambda s, i, p: (s, p[1] * n + i, 0)),
                      pl.BlockSpec((None, tr, D_MODEL), lambda s, i, p: (s, i, 0))],
            out_specs=pl.BlockSpec((None, tr, D_MODEL), lambda s, i, p: (s, i, 0))),
        out_shape=jax.ShapeDtypeStruct((N_CHIPS, half, D_MODEL), BF16),
        compiler_params=_params("arbitrary", "arbitrary"),
    )(place, grads, theirs)


def _add_chips(place, chip_sums, others, name):
    half = others.shape[1]
    tr = _row_tile(half)
    n = half // tr

    def body(place_ref, q_ref, o_ref, r_ref):
        acc = q_ref[...].astype(F32)
        for j in range(others.shape[0]):
            acc = acc + o_ref[j].astype(F32)
        r_ref[...] = acc

    return pl.pallas_call(
        body, name=name,
        grid_spec=pltpu.PrefetchScalarGridSpec(
            num_scalar_prefetch=1, grid=(n,),
            in_specs=[pl.BlockSpec((None, tr, D_MODEL), lambda i, p: (p[0], i, 0)),
                      pl.BlockSpec((others.shape[0], tr, D_MODEL), lambda i, p: (0, i, 0))],
            out_specs=pl.BlockSpec((tr, D_MODEL), lambda i, p: (p[1] * n + i, 0))),
        out_shape=jax.ShapeDtypeStruct((2 * half, D_MODEL), F32),
        compiler_params=_params("arbitrary"),
    )(place, chip_sums, others)


def _join_halves(shards, block):
    n = len(shards)
    rows = block.shape[0]

    def body(*refs):
        b_ref, o_refs, sum_ref = refs[n], refs[n + 1:2 * n + 1], refs[2 * n + 1]
        send_sems, recv_sems, small_send, small_recv, local_sem, all_ref = refs[2 * n + 2:]
        x, y, c = lax.axis_index("x"), lax.axis_index("y"), lax.axis_index("c")
        me, sibling = (x, y, c), (x, y, 1 - c)
        chips = [(1 - x, y), (x, 1 - y), (1 - x, 1 - y)]

        def half(k, rows_ref):
            return pltpu.make_async_remote_copy(src_ref=rows_ref, dst_ref=rows_ref, send_sem=send_sems.at[k],
                                                recv_sem=recv_sems.at[k], device_id=sibling, device_id_type=MESH)

        def at(px, py, pc):
            return all_ref.at[pl.ds(pl.multiple_of((4 * px + 2 * py + pc) * rows, 8), rows), :]

        def small(k, block_of, to, src=None):
            return pltpu.make_async_remote_copy(src_ref=at(*block_of) if src is None else src, dst_ref=at(*block_of),
                                                send_sem=small_send.at[k], recv_sem=small_recv.at[k],
                                                device_id=to, device_id_type=MESH)

        sends = [half(k, _half_rows(o, c)) for k, o in enumerate(o_refs)]
        for cp in sends:
            cp.start()
        mine = pltpu.make_async_copy(b_ref, at(*me), local_sem)
        mine.start()
        first = [small(0, me, sibling, src=b_ref)]
        first += [small(1 + j, me, (*chip, c), src=b_ref) for j, chip in enumerate(chips)]
        for cp in first:
            cp.start()
        passed = [small(4 + j, (*chip, c), sibling) for j, chip in enumerate(chips)]
        for j, chip in enumerate(chips):
            small(1 + j, (*chip, c), me).wait_recv()
            passed[j].start()
        small(0, sibling, me).wait_recv()
        for j, chip in enumerate(chips):
            small(4 + j, (*chip, 1 - c), me).wait_recv()
        mine.wait()
        acc = all_ref[0:rows, :]
        for dev in range(1, 8):
            acc = acc + all_ref[rows * dev:rows * (dev + 1), :]
        sum_ref[...] = acc
        for k, o in enumerate(o_refs):
            half(k, _half_rows(o, 1 - c)).wait_recv()
        for cp in sends + first + passed:
            cp.wait_send()

    res = pl.pallas_call(
        body, name="reduce_join_halves", in_specs=[ANY] * n + [pl.BlockSpec(memory_space=pltpu.VMEM)],
        out_specs=[ANY] * n + [pl.BlockSpec(memory_space=pltpu.VMEM)],
        out_shape=[jax.ShapeDtypeStruct(s.shape, F32) for s in shards] + [jax.ShapeDtypeStruct(block.shape, F32)],
        input_output_aliases={k: k for k in range(n)},
        scratch_shapes=[pltpu.SemaphoreType.DMA((n,)), pltpu.SemaphoreType.DMA((n,)),
                        pltpu.SemaphoreType.DMA((7,)), pltpu.SemaphoreType.DMA((7,)), pltpu.SemaphoreType.DMA,
                        pltpu.VMEM((8 * rows, D_MODEL), F32)],
    )(*shards, block)
    return res[:n], res[n]


def _adamw_math(w, g, m, v):
    m = ADAM_B1 * m + (1.0 - ADAM_B1) * g
    v = ADAM_B2 * v + (1.0 - ADAM_B2) * (g * g)
    m_hat = m / (1.0 - ADAM_B1 ** ADAM_STEP)
    v_hat = v / (1.0 - ADAM_B2 ** ADAM_STEP)
    return -ADAM_LR * (m_hat / (jnp.sqrt(v_hat) + ADAM_EPS) + ADAM_WD * w), m, v


def _adamw(w, g, m, v, name):
    r, c = w.shape
    tr = _row_tile(r)

    def body(w_ref, g_ref, m_ref, v_ref, d_ref, nm_ref, nv_ref):
        d_ref[...], nm_ref[...], nv_ref[...] = _adamw_math(w_ref[...], g_ref[...], m_ref[...], v_ref[...])

    spec = pl.BlockSpec((tr, c), lambda i: (i, 0))
    return pl.pallas_call(
        body, name=name, grid=(r // tr,), in_specs=[spec] * 4, out_specs=[spec] * 3,
        out_shape=[jax.ShapeDtypeStruct((r, c), F32)] * 3, compiler_params=_params("arbitrary"),
    )(w, g, m, v)


def _adamw_small(ws, gs, ms, vs):
    n = len(ws)

    def body(*refs):
        ins, outs = refs[:4 * n], refs[4 * n:]
        for k in range(n):
            d, m, v = _adamw_math(ins[k][...], ins[n + k][...], ins[2 * n + k][...], ins[3 * n + k][...])
            outs[k][...], outs[n + k][...], outs[2 * n + k][...] = d, m, v

    shapes = [jax.ShapeDtypeStruct(w.shape, F32) for w in ws]
    res = pl.pallas_call(body, name="adamw_small", out_shape=shapes * 3)(*ws, *gs, *ms, *vs)
    return res[:n], res[n:2 * n], res[2 * n:]


def _fold_heads(partials):
    t = jnp.sum(partials[:, 0, :], axis=0)
    return (t[:HEAD_DIM] + t[HEAD_DIM:]).reshape(1, HEAD_DIM)


def _local_step(x, target, norm_gain, w_t, w_a, w_b, w_o, b_m, q_norm_a, k_norm_a, q_norm_b, k_norm_b, sink_a,
                rel_bias, start_reduce=None, small_shard=None):
    two = lambda gain: jnp.concatenate([gain, gain], axis=1)
    bias_a = _bias_lines(rel_bias[:, :8], A_HALF_WINDOW, 1)
    bias_b = jnp.concatenate([_bias_lines(rel_bias[:, 8 + 8 * g:16 + 8 * g], B_HALF_WINDOW, d)
                              for g, d in enumerate(B_DILATIONS)], axis=0)

    qkv, h, *small_all = _in_proj(x, norm_gain, w_t, 0, QKV_WIDTH // W_BLOCK, BF16, "in_proj_qkv", True, small_shard)
    if small_shard is not None:
        w_a, w_b, w_o, b_m = _unpack_weights(small_all[0])
    gates, = _in_proj(x, norm_gain, w_t, QKV_WIDTH // W_BLOCK, GATE_WIDTH // W_BLOCK, F32, "in_proj_gates", False)
    out_a, lse_a = _attn_a_fwd(qkv, two(q_norm_a), two(k_norm_a), bias_a, sink_a)
    out_b, lse_b = _attn_b_fwd(qkv, two(q_norm_b), two(k_norm_b), bias_b)

    dy, dgates, d_out_a, d_out_b, delta_a, delta_b, d_wa, d_wb, d_wo, d_bm, sq = _middle(
        out_a, out_b, gates, x, target, w_a, w_b, w_o, b_m)
    loss = (0.5 / D_MODEL) * jnp.sum(sq)

    dq_a, dkv_a, dgq_a, dgk_a, ds_a, dsink = _attn_a_bwd(
        qkv, two(q_norm_a), two(k_norm_a), bias_a, sink_a, delta_a, lse_a, d_out_a)
    dq_b, dk_b, dv_b, dgq_b, dgk_b, ds_b = _attn_b_bwd(
        qkv, two(q_norm_b), two(k_norm_b), bias_b, delta_b, lse_b, d_out_b)
    d_proj = (dq_a, dkv_a, dq_b, dk_b, dv_b, dgates)

    d_bm_rows = jnp.pad(d_bm.reshape(2, N_CHIPS, 256).transpose(1, 0, 2),
                        ((0, 0), (0, REST_ROWS - 514), (0, D_MODEL - 256)))
    rest = jnp.concatenate([d_wo.reshape(N_CHIPS, 256, D_MODEL), d_wa.reshape(N_CHIPS, 128, D_MODEL),
                            d_wb.reshape(N_CHIPS, 128, D_MODEL), d_bm_rows], axis=1)
    if start_reduce is None:
        grads, chip_sums = [_d_w_in(d_proj, h).reshape(N_CHIPS, W_IN_SHARD, D_MODEL), rest], []
    else:
        d_wt, *theirs = _d_w_in(d_proj, h, rest.astype(BF16))
        grads = [d_wt.reshape(N_CHIPS, W_IN_SHARD, D_MODEL), rest]
        chip_sums = start_reduce(grads, theirs)
    grad_x, d_gain, others = _d_x(d_proj, w_t, x, norm_gain, dy, chip_sums)

    d_rel = jnp.concatenate(
        [_bias_grad(ds_a, A_HALF_WINDOW, 1)]
        + [_bias_grad(ds_b[4 * g:4 * g + 4], B_HALF_WINDOW, d) for g, d in enumerate(B_DILATIONS)], axis=1)
    d_sink = jnp.sum(dsink, axis=(2, 3)).reshape(1, 8)
    dgk_a_row = dgk_a[0]
    small = jnp.zeros((8, D_MODEL), F32)
    small = small.at[0].set(d_gain[0])
    small = small.at[1].set(d_rel.reshape(-1))
    misc = jnp.concatenate([_fold_heads(dgq_a), (dgk_a_row[:HEAD_DIM] + dgk_a_row[HEAD_DIM:]).reshape(1, HEAD_DIM),
                            _fold_heads(dgq_b), _fold_heads(dgk_b), d_sink], axis=1)
    small = small.at[2, :264].set(misc[0])

    return loss, grad_x, grads, small, chip_sums, others


def _unpack_weights(small_all):
    sm = small_all.reshape(N_CHIPS, SMALL_ROWS, D_MODEL)
    w_o = sm[:, 0:256].reshape(D_MODEL, D_MODEL)
    w_a = sm[:, 256:384].reshape(N_CHIPS, 512, 256).transpose(1, 0, 2).reshape(512, D_MODEL)
    w_b = sm[:, 384:512].reshape(N_CHIPS, 512, 256).transpose(1, 0, 2).reshape(512, D_MODEL)
    b_m = lax.bitcast_convert_type(sm[:, 512].reshape(N_CHIPS, 2, 256, 2), F32)
    return w_a, w_b, w_o, b_m.transpose(1, 0, 2).reshape(2, D_MODEL)


def _pack_small_weights(w_branch_a, w_branch_b, b_merge, w_out):
    b_m = jnp.pad(lax.bitcast_convert_type(b_merge, BF16).reshape(1, D_MODEL), ((0, SMALL_ROWS - 513), (0, 0)))
    return jnp.concatenate([w_out.astype(BF16), w_branch_a.astype(BF16).reshape(128, D_MODEL),
                            w_branch_b.astype(BF16).reshape(128, D_MODEL), b_m], axis=0)


def kernel(x, norm_gain, w_in, q_norm_a, k_norm_a, q_norm_b, k_norm_b, sink_a, rel_bias, w_branch_a, w_branch_b, b_merge, w_out, loss_target, m_norm_gain, m_w_in, m_q_norm_a, m_k_norm_a, m_q_norm_b, m_k_norm_b, m_sink_a, m_rel_bias, m_w_branch_a, m_w_branch_b, m_b_merge, m_w_out, v_norm_gain, v_w_in, v_q_norm_a, v_k_norm_a, v_q_norm_b, v_k_norm_b, v_sink_a, v_rel_bias, v_w_branch_a, v_w_branch_b, v_b_merge, v_w_out):
    w_in_t, m_w_in_t, v_w_in_t = (jnp.transpose(t[0]) for t in (w_in, m_w_in, v_w_in))
    wt_shard = _cast_rows(w_in_t, BF16, "w_in_cast")
    w_t = _gather_weights(wt_shard)
    small_shard = _pack_small_weights(w_branch_a[0], w_branch_b[0], b_merge[0], w_out[0])

    place = _my_place()
    names = ("w_in", "rest")

    def start_reduce(grads, theirs):
        return [_add_halves(place, g, t, "reduce_add_halves_" + n) for g, t, n in zip(grads, theirs, names)]

    loss_part, grad_x, _, small, chip_sums, others = _local_step(
        x[0], loss_target[0], norm_gain, w_t, None, None, None, None, q_norm_a, k_norm_a, q_norm_b, k_norm_b,
        sink_a, rel_bias, start_reduce, small_shard)

    (g_wt, g_rest), small = _join_halves(
        [_add_chips(place, q, o, "reduce_add_chips_" + n) for q, o, n in zip(chip_sums, others, names)],
        small.at[3, 0].set(loss_part))
    loss = small[3, 0]

    g_w_out = g_rest[0:256]
    g_w_a = g_rest[256:384].reshape(512, 256)
    g_w_b = g_rest[384:512].reshape(512, 256)
    g_b_merge = g_rest[512:514, :256]
    g_norm_gain = small[0:1]
    g_rel_bias = small[1].reshape(N_BUCKETS, N_BUCKETS)
    g_q_a, g_k_a, g_q_b, g_k_b = (small[2:3, 64 * k:64 * k + 64] for k in range(4))
    g_sink = small[2:3, 256:264]

    big_names = (("w_branch_a", w_branch_a, g_w_a, m_w_branch_a, v_w_branch_a),
                 ("w_branch_b", w_branch_b, g_w_b, m_w_branch_b, v_w_branch_b),
                 ("w_out", w_out, g_w_out, m_w_out, v_w_out))
    upd = {name: (g,) + tuple(_adamw(w[0], g, m[0], v[0], "adamw_" + name)) for name, w, g, m, v in big_names}
    upd["w_in"] = tuple(jnp.transpose(t) for t in (g_wt,) + tuple(_adamw(w_in_t, g_wt, m_w_in_t, v_w_in_t, "adamw_w_in")))
    small_names = ("norm_gain", "q_norm_a", "k_norm_a", "q_norm_b", "k_norm_b", "sink_a", "rel_bias", "b_merge")
    ws = [norm_gain, q_norm_a, k_norm_a, q_norm_b, k_norm_b, sink_a, rel_bias, b_merge[0]]
    gs = [g_norm_gain, g_q_a, g_k_a, g_q_b, g_k_b, g_sink, g_rel_bias, g_b_merge]
    ms = [m_norm_gain, m_q_norm_a, m_k_norm_a, m_q_norm_b, m_k_norm_b, m_sink_a, m_rel_bias, m_b_merge[0]]
    vs = [v_norm_gain, v_q_norm_a, v_k_norm_a, v_q_norm_b, v_k_norm_b, v_sink_a, v_rel_bias, v_b_merge[0]]
    ds, nms, nvs = _adamw_small(ws, gs, ms, vs)
    for k, name in enumerate(small_names):
        upd[name] = (gs[k], ds[k], nms[k], nvs[k])

    order = ("norm_gain", "w_in", "q_norm_a", "k_norm_a", "q_norm_b", "k_norm_b", "sink_a", "rel_bias",
             "w_branch_a", "w_branch_b", "b_merge", "w_out")
    lead = {"w_in", "w_branch_a", "w_branch_b", "b_merge", "w_out"}
    outs = [loss, grad_x[None]]
    for part in range(4):
        outs += [upd[name][part][None] if name in lead else upd[name][part] for name in order]
    return tuple(outs)
```

```python
import math

import numpy as np
import jax
import jax.numpy as jnp
from jax import lax
from jax.experimental import pallas as pl
from jax.experimental.pallas import tpu as pltpu

F32 = jnp.float32
BF16 = jnp.bfloat16

SEQ = 4096
D_MODEL = 1024
HEAD_DIM = 64
LANES = 128
EPS = 1e-6
NEG_INF = -1e30
SCALE = HEAD_DIM ** -0.5
N_BUCKETS = 32
MAX_DISTANCE = 1024
N_CHIPS = 4

A_HALF_WINDOW = 128
B_HALF_WINDOW = 64
B_DILATIONS = (1, 4, 16)
Q_BLOCK = 128

QKV_WIDTH = 5376
GATE_WIDTH = 3072
QA_BLK, KA_BLK, VA_BLK = 0, 4, 5
QB_BLK, KB_BLK, VB_BLK = 6, 18, 30
IN_WIDTH = QKV_WIDTH + GATE_WIDTH
W_IN_SHARD = IN_WIDTH // N_CHIPS

SMALL_ROWS = 544
REST_ROWS = 544

ADAM_LR = 0.001
ADAM_B1 = 0.9
ADAM_B2 = 0.999
ADAM_EPS = 1e-08
ADAM_WD = 0.01
ADAM_STEP = 10

VMEM_LIMIT = 56 * 1024 * 1024

NT = (((1,), (1,)), ((), ()))
TN = (((0,), (0,)), ((), ()))
MESH = pl.DeviceIdType.MESH
ANY = pl.BlockSpec(memory_space=pl.ANY)


def _dot(a, b, dims=None):
    if dims is None:
        return jnp.dot(a, b, preferred_element_type=F32)
    return lax.dot_general(a, b, dims, preferred_element_type=F32)


def _params(*semantics):
    return pltpu.CompilerParams(dimension_semantics=semantics or None, vmem_limit_bytes=VMEM_LIMIT)


def _line_width(half_window):
    return pl.cdiv(2 * Q_BLOCK + 2 * half_window - 1, LANES) * LANES


def _bucket_onehot(half_window, stride):
    rel = np.arange(_line_width(half_window)) - (Q_BLOCK - 1) - half_window
    band = np.abs(rel) <= half_window
    rel = rel * stride
    half, max_exact = N_BUCKETS // 2, N_BUCKETS // 4
    n = np.abs(rel)
    nf = np.maximum(n, max_exact).astype(np.float32)
    large = max_exact + (np.log(nf / np.float32(max_exact)) / np.float32(math.log(MAX_DISTANCE / max_exact))
                         * np.float32(half - max_exact)).astype(np.int32)
    large = np.minimum(large, half - 1)
    bucket = (rel > 0).astype(np.int32) * half + np.where(n < max_exact, n, large)
    onehot = (bucket[..., None] == np.arange(N_BUCKETS)) & band[..., None]
    return onehot.astype(np.float32), band


def _bias_lines(rel_bias_cols, half_window, stride):
    onehot, band = _bucket_onehot(half_window, stride)
    h = rel_bias_cols.shape[1]
    t = jnp.einsum("tb,bh->ht", jnp.asarray(onehot), rel_bias_cols, precision=lax.Precision.HIGHEST)
    t = t + jnp.asarray(np.where(band, 0.0, NEG_INF).astype(np.float32))
    return t.reshape(h // 2, 2, -1)


def _bias_grad(d_lines, half_window, stride):
    onehot, _ = _bucket_onehot(half_window, stride)
    h = d_lines.shape[0] * 2
    return jnp.einsum("tb,ht->bh", jnp.asarray(onehot), d_lines.reshape(h, -1), precision=lax.Precision.HIGHEST)


def _unroll_bias(line_ref, tile_ref, w):
    width = line_ref.shape[1]
    for j in range(2):
        rows = jnp.broadcast_to(line_ref[j:j + 1, :], (Q_BLOCK, width))
        rows = pltpu.roll(rows, width - (Q_BLOCK - 1), 1, stride=1, stride_axis=0)
        tile_ref[j * Q_BLOCK:(j + 1) * Q_BLOCK, :] = rows[:, :w]


def _fold_bias_grad(tile_ref, line_ref, w):
    width = line_ref.shape[1]
    row = lax.broadcasted_iota(jnp.int32, (Q_BLOCK, Q_BLOCK), 0)
    col = lax.broadcasted_iota(jnp.int32, (Q_BLOCK, Q_BLOCK), 1)
    flip = jnp.where(row + col == Q_BLOCK - 1, 1.0, 0.0).astype(BF16)
    for j in range(2):
        tile = tile_ref[j * Q_BLOCK:(j + 1) * Q_BLOCK, :]
        hi = tile.astype(BF16)
        lo = (tile - hi.astype(F32)).astype(BF16)
        rows = _dot(flip, hi) + _dot(flip, lo)
        rows = jnp.concatenate([rows, jnp.zeros((Q_BLOCK, width - w), F32)], axis=1)
        rows = pltpu.roll(rows, 0, 1, stride=1, stride_axis=0)
        line_ref[j:j + 1, :] = jnp.sum(rows, axis=0, keepdims=True)


def _row_tile(rows):
    return max(t for t in range(16, 385, 16) if rows % t == 0)


def _cast_rows(w, out_dtype, name):
    r, c = w.shape
    tr = _row_tile(r)

    def body(w_ref, o_ref):
        o_ref[...] = w_ref[...].astype(out_dtype)

    spec = pl.BlockSpec((tr, c), lambda i: (i, 0))
    return pl.pallas_call(
        body, name=name, grid=(r // tr,), in_specs=[spec], out_specs=spec,
        out_shape=jax.ShapeDtypeStruct((r, c), out_dtype), compiler_params=_params("arbitrary"),
    )(w)


STAGE_ROWS = 528


def _gather_scratch():
    return [pltpu.SemaphoreType.DMA((12,)), pltpu.SemaphoreType.DMA((12,)), pltpu.SemaphoreType.DMA((2,)),
            pltpu.SemaphoreType.DMA((2,)), pltpu.VMEM((2, STAGE_ROWS, D_MODEL), BF16)]


def _gather_phases(src_ref, out_ref, send_sems, recv_sems, in_sems, out_sems, stage):
    rows = src_ref.shape[0]
    x, y, c = lax.axis_index("x"), lax.axis_index("y"), lax.axis_index("c")
    sibling = (x, y, 1 - c)
    near = (x + (1 - c) - 2 * x * (1 - c), y + c - 2 * y * c)
    far = (x + c - 2 * x * c, y + (1 - c) - 2 * y * (1 - c))
    diag = (1 - x, 1 - y)
    chip_no = lambda chip: 2 * chip[0] + chip[1]
    my_chip = chip_no((x, y))

    pieces = 2 if (rows // 2) % 32 == 0 else 1
    n = rows // 2 // pieces

    def half_of(chip, half, p):
        start = pl.multiple_of(chip * rows + half * (rows // 2) + p * n, 16)
        return out_ref.at[pl.ds(start, n), :]

    def copy(k, p, src, dst, to):
        return pltpu.make_async_remote_copy(src_ref=src, dst_ref=dst, send_sem=send_sems.at[k * pieces + p],
                                            recv_sem=recv_sems.at[k * pieces + p], device_id=to, device_id_type=MESH)

    def mine(p):
        return src_ref.at[pl.ds(pl.multiple_of(c * (rows // 2) + p * n, 16), n), :]

    def keep_own():
        outs = []
        for i, r0 in enumerate(range(0, rows, STAGE_ROWS)):
            n = min(STAGE_ROWS, rows - r0)
            slot = i % 2
            if i >= 2:
                outs[i - 2].wait()
            buf = stage.at[slot, pl.ds(0, n), :]
            load = pltpu.make_async_copy(src_ref.at[pl.ds(r0, n), :], buf, in_sems.at[slot])
            load.start()
            load.wait()
            start = pl.multiple_of(my_chip * rows + r0, 16)
            outs.append(pltpu.make_async_copy(buf, out_ref.at[pl.ds(start, n), :], out_sems.at[slot]))
            outs[i].start()
        for cp in outs[-2:]:
            cp.wait()

    def start():
        for p in range(pieces):
            copy(0, p, mine(p), half_of(my_chip, c, p), (*near, c)).start()
            copy(1, p, mine(p), half_of(my_chip, c, p), (*far, c)).start()
        keep_own()

    def pass_on(j, p, chip):
        landed = half_of(chip_no(chip), c, p)
        copy(3 + j, p, landed, landed, sibling).start()

    def relay():
        for p in range(pieces):
            landed = half_of(chip_no(near), c, p)
            copy(0, p, landed, landed, sibling).wait_recv()
            copy(2, p, landed, landed, (*far, c)).start()
            pass_on(0, p, near)

    def forward():
        for j, chip in ((1, far), (2, diag)):
            for p in range(pieces):
                landed = half_of(chip_no(chip), c, p)
                copy(j, p, landed, landed, sibling).wait_recv()
                pass_on(j, p, chip)

    def finish():
        for j, chip in ((0, far), (1, near), (2, diag)):
            for p in range(pieces):
                other = half_of(chip_no(chip), 1 - c, p)
                copy(3 + j, p, other, other, sibling).wait_recv()
        for k in range(6):
            for p in range(pieces):
                copy(k, p, mine(p), mine(p), sibling).wait_send()

    return start, relay, forward, finish


def _gather_weights(shard):
    def body(src_ref, out_ref, *scratch):
        for phase in _gather_phases(src_ref, out_ref, *scratch):
            phase()

    return pl.pallas_call(
        body, name="gather_weights", in_specs=[ANY], out_specs=ANY,
        out_shape=jax.ShapeDtypeStruct((N_CHIPS * shard.shape[0], D_MODEL), BF16),
        scratch_shapes=_gather_scratch(),
    )(shard)


W_BLOCK = 768


def _w_blocks(first, count):
    return [pl.BlockSpec((W_BLOCK, D_MODEL), lambda *_, k=k: (first + k, 0)) for k in range(count)]


def _in_proj(x, gain, w_t, first_block, n_blocks, out_dtype, name, keep_h, ride=None):
    tm = 512
    n_steps = SEQ // tm
    n_out = 2 if keep_h else 1

    def body(x_ref, g_ref, *refs):
        w_refs, outs = refs[:n_blocks], refs[n_blocks + (ride is not None):n_blocks + (ride is not None) + n_out]
        if ride is not None:
            phases = _gather_phases(refs[n_blocks], *refs[n_blocks + 1 + n_out:])
            for step, phase in zip((0, 2, 4, n_steps - 1), phases):
                pl.when(pl.program_id(0) == step)(phase)
        xf = x_ref[...]
        r = lax.rsqrt(jnp.mean(xf * xf, axis=-1, keepdims=True) + EPS)
        h = ((xf * r) * g_ref[...]).astype(BF16)
        if keep_h:
            outs[1][...] = h
        for k, w_ref in enumerate(w_refs):
            outs[0][:, k * W_BLOCK:(k + 1) * W_BLOCK] = _dot(h, w_ref[...], NT).astype(out_dtype)

    riding = [] if ride is None else [ride]
    return pl.pallas_call(
        body, name=name, grid=(n_steps,),
        in_specs=[pl.BlockSpec((tm, D_MODEL), lambda i: (i, 0)), pl.BlockSpec((1, D_MODEL), lambda i: (0, 0))]
        + _w_blocks(first_block, n_blocks) + [ANY for _ in riding],
        out_specs=[pl.BlockSpec((tm, W_BLOCK * n_blocks), lambda i: (i, 0)),
                   pl.BlockSpec((tm, D_MODEL), lambda i: (i, 0))][:n_out] + [ANY for _ in riding],
        out_shape=[jax.ShapeDtypeStruct((SEQ, W_BLOCK * n_blocks), out_dtype),
                   jax.ShapeDtypeStruct((SEQ, D_MODEL), BF16)][:n_out]
        + [jax.ShapeDtypeStruct((N_CHIPS * r.shape[0], D_MODEL), BF16) for r in riding],
        scratch_shapes=_gather_scratch() if riding else [],
        compiler_params=_params("arbitrary"),
    )(x, gain, *([w_t] * n_blocks), *riding)


CHUNK = 256
CHUNK_UNROLL = 8
TILE_UNROLL = 8


def _low_half():
    return lax.broadcasted_iota(jnp.int32, (1, LANES), 1) < HEAD_DIM


def _half_sum(v, low):
    del low
    row = lax.broadcasted_iota(jnp.int32, (2 * LANES, LANES), 0)
    col = lax.broadcasted_iota(jnp.int32, (2 * LANES, LANES), 1)
    ones = jnp.where((row % LANES) // HEAD_DIM == col // HEAD_DIM, 1.0, 0.0).astype(BF16)
    hi = v.astype(BF16)
    lo = (v - hi.astype(F32)).astype(BF16)
    return _dot(jnp.concatenate([hi, lo], axis=1), ones)


def _chunks(fn, init=0):
    def body(i, carry):
        for u in range(CHUNK_UNROLL):
            carry = fn(pl.multiple_of((i * CHUNK_UNROLL + u) * CHUNK, CHUNK), carry)
        return carry

    return lax.fori_loop(0, SEQ // (CHUNK * CHUNK_UNROLL), body, init)


def _inv_rms(t, low):
    del low
    row = lax.broadcasted_iota(jnp.int32, (LANES, LANES), 0)
    col = lax.broadcasted_iota(jnp.int32, (LANES, LANES), 1)
    ones = jnp.where(row // HEAD_DIM == col // HEAD_DIM, 1.0, 0.0).astype(BF16)
    return lax.rsqrt(_dot((t * t).astype(BF16), ones) * (1.0 / HEAD_DIM) + EPS)


def _prep_q(q_ref, gain_ref, qn_ref):
    low = _low_half()

    def step(r0, carry):
        q = q_ref[pl.ds(r0, CHUNK), :].astype(F32)
        qn_ref[pl.ds(r0, CHUNK), :] = ((q * _inv_rms(q, low)) * gain_ref[...]) * SCALE
        return carry

    _chunks(step)


def _own_half(t, keep):
    return jnp.where(keep, t, pltpu.roll(t, HEAD_DIM, 1))


def _prep_kv(k_ref, v_ref, gain_ref, kp_ref, vp_ref, pad, keep=None):
    low = _low_half()
    zeros = jnp.zeros((pad, LANES), F32)
    for ref in (kp_ref, vp_ref):
        ref[pl.ds(0, pad), :] = zeros
        ref[pl.ds(pad + SEQ, pad), :] = zeros

    def step(r0, carry):
        k = k_ref[pl.ds(r0, CHUNK), :].astype(F32)
        v = v_ref[pl.ds(r0, CHUNK), :].astype(F32)
        kn = (k * _inv_rms(k, low)) * gain_ref[...]
        if keep is not None:
            kn, v = _own_half(kn, keep), _own_half(v, keep)
        kp_ref[pl.ds(pad + r0, CHUNK), :] = kn
        vp_ref[pl.ds(pad + r0, CHUNK), :] = v
        return carry

    _chunks(step)


def _tiles(d, half_window, fn):
    w = Q_BLOCK + 2 * half_window
    length = SEQ // d
    n_blocks = length // Q_BLOCK
    col = lax.broadcasted_iota(jnp.int32, (1, w), 1)

    def step(it, carry):
        c, n = it // n_blocks, it % n_blocks
        start = c + (d * Q_BLOCK) * n
        if d == 1:
            start = pl.multiple_of(start, Q_BLOCK)
            q_rows, k_rows = pl.ds(start, Q_BLOCK), pl.ds(start, w)
        else:
            q_rows, k_rows = pl.ds(start, Q_BLOCK, stride=d), pl.ds(start, w, stride=d)
        t = n * Q_BLOCK - half_window + col
        edge = jnp.where((t < 0) | (t >= length), NEG_INF, 0.0)
        fn(q_rows, k_rows, edge)
        return carry

    lax.fori_loop(0, d * n_blocks, step, 0, unroll=TILE_UNROLL)


def _stack_heads(t, low):
    return jnp.concatenate([jnp.where(low, t, 0.0), jnp.where(low, 0.0, t)], axis=0).astype(BF16)


def _unstack_heads(t, low):
    return jnp.where(low, t[:Q_BLOCK], t[Q_BLOCK:])


def _per_head(pair):
    return jnp.concatenate([jnp.full((Q_BLOCK, 1), pair[0], F32), jnp.full((Q_BLOCK, 1), pair[1], F32)], axis=0)


def _fwd_tiles(qn_ref, kp_ref, vp_ref, bias_ref, emit, *, d, half_window, sinks=None):
    low = _low_half()
    w = Q_BLOCK + 2 * half_window
    sink = None if sinks is None else _per_head(sinks)

    def tile(q_rows, k_rows, edge):
        q2 = _stack_heads(qn_ref[q_rows, :], low)
        k = kp_ref[k_rows, :].astype(BF16)
        v1 = jnp.concatenate([vp_ref[k_rows, :], jnp.ones((w, LANES), F32)], axis=1).astype(BF16)
        s = _dot(q2, k, NT) + bias_ref[...] + edge
        m = jnp.max(s, axis=-1, keepdims=True)
        if sink is not None:
            m = jnp.maximum(m, sink)
        o = _dot(jnp.exp(s - m).astype(BF16), v1)
        l = o[:, LANES:]
        if sink is not None:
            l = l + jnp.exp(sink - m)
        emit(q_rows, _unstack_heads(o[:, :LANES] * (1.0 / l), low), _unstack_heads(m + jnp.log(l), low))

    _tiles(d, half_window, tile)


def _bwd_tiles(qn_ref, kp_ref, vp_ref, bias_ref, do_ref, lse_ref, delta_ref, dq_ref, dk_ref, dv_ref, ds_ref,
               *, d, half_window, sinks=None, dsink_ref=None):
    low = _low_half()
    w = Q_BLOCK + 2 * half_window
    sink = None if sinks is None else _per_head(sinks)

    def rows_of(t):
        return jnp.concatenate([t[:, 0:1], t[:, HEAD_DIM:HEAD_DIM + 1]], axis=0)

    def tile(q_rows, k_rows, edge):
        q2 = _stack_heads(qn_ref[q_rows, :], low)
        do2 = _stack_heads(do_ref[q_rows, :], low)
        k = kp_ref[k_rows, :].astype(BF16)
        v = vp_ref[k_rows, :].astype(BF16)
        lse = rows_of(lse_ref[q_rows, :])
        delta = rows_of(delta_ref[q_rows, :])
        p = jnp.exp(_dot(q2, k, NT) + bias_ref[...] + edge - lse)
        ds = p * (_dot(do2, v, NT) - delta)
        ds_ref[...] += ds
        if sink is not None:
            dsink_ref[...] += (-jnp.exp(sink - lse) * delta).reshape(2, Q_BLOCK, 1)
        dsb, pb = ds.astype(BF16), p.astype(BF16)
        dq_ref[q_rows, :] = _unstack_heads(_dot(dsb, k), low)
        dk_ref[k_rows, :] += _dot(dsb, q2, TN)
        dv_ref[k_rows, :] += _dot(pb, do2, TN)

    _tiles(d, half_window, tile)


def _norm_bwd(raw_ref, gain_ref, dn_ref, dn_offset, out_ref, scale):
    low = _low_half()

    def step(r0, dgain):
        t = raw_ref[pl.ds(r0, CHUNK), :].astype(F32)
        dn = dn_ref[pl.ds(dn_offset + r0, CHUNK), :]
        dth = dn * (gain_ref[...] * scale)
        r = _inv_rms(t, low)
        th = t * r
        out_ref[pl.ds(r0, CHUNK), :] = (r * (dth - th * (r * _half_sum(dth * t, low) * (1.0 / HEAD_DIM)))).astype(BF16)
        return dgain + jnp.sum(dn * th, axis=0, keepdims=True) * scale

    return _chunks(step, jnp.zeros((1, LANES), F32))


def _rows8(v):
    return jnp.broadcast_to(v, (8, v.shape[-1]))


A_W = Q_BLOCK + 2 * A_HALF_WINDOW
A_PAD = A_HALF_WINDOW


def _seq_block(col_fn):
    return pl.BlockSpec((SEQ, LANES), col_fn)


def _attn_a_fwd(qkv, gain_q, gain_k, bias, sink):
    def body(sink_ref, q_ref, k_ref, v_ref, gq_ref, gk_ref, line_ref, o_ref, lse_ref, qn_ref, kp_ref, vp_ref,
             bias_ref):
        hp = pl.program_id(0)
        keep = (lax.broadcasted_iota(jnp.int32, (1, LANES), 1) // HEAD_DIM) == hp // 2
        _prep_q(q_ref, gq_ref, qn_ref)
        _prep_kv(k_ref, v_ref, gk_ref, kp_ref, vp_ref, A_PAD, keep)
        _unroll_bias(line_ref, bias_ref, A_W)

        def emit(rows, out, lse):
            o_ref[rows, :] = out
            lse_ref[rows, :] = lse

        _fwd_tiles(qn_ref, kp_ref, vp_ref, bias_ref, emit, d=1, half_window=A_HALF_WINDOW,
                   sinks=(sink_ref[2 * hp], sink_ref[2 * hp + 1]))

    vec = pl.BlockSpec((1, LANES), lambda hp, s: (0, 0))
    return pl.pallas_call(
        body, name="attn_a_fwd",
        grid_spec=pltpu.PrefetchScalarGridSpec(
            num_scalar_prefetch=1, grid=(4,),
            in_specs=[_seq_block(lambda hp, s: (0, QA_BLK + hp)), _seq_block(lambda hp, s: (0, KA_BLK)),
                      _seq_block(lambda hp, s: (0, VA_BLK)), vec, vec,
                      pl.BlockSpec((None, 2, _line_width(A_HALF_WINDOW)), lambda hp, s: (hp, 0, 0))],
            out_specs=[_seq_block(lambda hp, s: (0, hp)), _seq_block(lambda hp, s: (0, hp))],
            scratch_shapes=[pltpu.VMEM((SEQ, LANES), F32), pltpu.VMEM((SEQ + 2 * A_PAD, LANES), F32),
                            pltpu.VMEM((SEQ + 2 * A_PAD, LANES), F32), pltpu.VMEM((2 * Q_BLOCK, A_W), F32)]),
        out_shape=[jax.ShapeDtypeStruct((SEQ, 512), F32)] * 2,
        compiler_params=_params("arbitrary"),
    )(sink.reshape(8), qkv, qkv, qkv, gain_q, gain_k, bias)


def _attn_a_bwd(qkv, gain_q, gain_k, bias, sink, delta, lse, d_out):
    def body(sink_ref, q_ref, k_ref, v_ref, gq_ref, gk_ref, line_ref, delta_ref, lse_ref, do_ref,
             dq_out, dkv_out, dgq_out, dgk_out, dline_out, dsink_out,
             qn_ref, kp_ref, vp_ref, dq_ref, dk_ref, dv_ref, dk_tot, dv_tot, bias_ref, ds_out):
        hp = pl.program_id(0)
        kv_head = hp // 2
        keep = (lax.broadcasted_iota(jnp.int32, (1, LANES), 1) // HEAD_DIM) == kv_head
        _prep_q(q_ref, gq_ref, qn_ref)
        _prep_kv(k_ref, v_ref, gk_ref, kp_ref, vp_ref, A_PAD, keep)
        _unroll_bias(line_ref, bias_ref, A_W)
        ds_out[...] = jnp.zeros_like(ds_out)
        dsink_out[...] = jnp.zeros_like(dsink_out)

        @pl.when(hp % 2 == 0)
        def _():
            dk_ref[...] = jnp.zeros_like(dk_ref)
            dv_ref[...] = jnp.zeros_like(dv_ref)

        @pl.when(hp == 0)
        def _():
            dk_tot[...] = jnp.zeros_like(dk_tot)
            dv_tot[...] = jnp.zeros_like(dv_tot)

        _bwd_tiles(qn_ref, kp_ref, vp_ref, bias_ref, do_ref, lse_ref, delta_ref, dq_ref, dk_ref, dv_ref, ds_out,
                   d=1, half_window=A_HALF_WINDOW, sinks=(sink_ref[2 * hp], sink_ref[2 * hp + 1]),
                   dsink_ref=dsink_out)
        _fold_bias_grad(ds_out, dline_out, A_W)
        dgq_out[...] = _rows8(_norm_bwd(q_ref, gq_ref, dq_ref, 0, dq_out, SCALE))

        def fold(r0, carry):
            rows = pl.ds(A_PAD + r0, CHUNK)
            for acc, tot in ((dk_ref, dk_tot), (dv_ref, dv_tot)):
                t = acc[rows, :]
                tot[pl.ds(r0, CHUNK), :] += jnp.where(keep, t + pltpu.roll(t, HEAD_DIM, 1), 0.0)
            return carry

        @pl.when(hp % 2 == 1)
        def _():
            _chunks(fold)

        @pl.when(hp == 3)
        def _():
            dgk_out[...] = _rows8(_norm_bwd(k_ref, gk_ref, dk_tot, 0, dkv_out.at[0], 1.0))
            dkv_out[1] = dv_tot[...].astype(BF16)

    vec = pl.BlockSpec((1, LANES), lambda hp, s: (0, 0))
    seq_f32 = pltpu.VMEM((SEQ, LANES), F32)
    padded = pltpu.VMEM((SEQ + 2 * A_PAD, LANES), F32)
    return pl.pallas_call(
        body, name="attn_a_bwd",
        grid_spec=pltpu.PrefetchScalarGridSpec(
            num_scalar_prefetch=1, grid=(4,),
            in_specs=[_seq_block(lambda hp, s: (0, QA_BLK + hp)), _seq_block(lambda hp, s: (0, KA_BLK)),
                      _seq_block(lambda hp, s: (0, VA_BLK)), vec, vec,
                      pl.BlockSpec((None, 2, _line_width(A_HALF_WINDOW)), lambda hp, s: (hp, 0, 0)),
                      _seq_block(lambda hp, s: (0, hp)), _seq_block(lambda hp, s: (0, hp)),
                      _seq_block(lambda hp, s: (0, hp))],
            out_specs=[pl.BlockSpec((None, SEQ, LANES), lambda hp, s: (hp, 0, 0)),
                       pl.BlockSpec((2, SEQ, LANES), lambda hp, s: (0, 0, 0)),
                       pl.BlockSpec((None, 8, LANES), lambda hp, s: (hp, 0, 0)),
                       pl.BlockSpec((8, LANES), lambda hp, s: (0, 0)),
                       pl.BlockSpec((None, 2, _line_width(A_HALF_WINDOW)), lambda hp, s: (hp, 0, 0)),
                       pl.BlockSpec((None, 2, Q_BLOCK, 1), lambda hp, s: (hp, 0, 0, 0))],
            scratch_shapes=[seq_f32, padded, padded, seq_f32, padded, padded, seq_f32, seq_f32,
                            pltpu.VMEM((2 * Q_BLOCK, A_W), F32), pltpu.VMEM((2 * Q_BLOCK, A_W), F32)]),
        out_shape=[jax.ShapeDtypeStruct((4, SEQ, LANES), BF16), jax.ShapeDtypeStruct((2, SEQ, LANES), BF16),
                   jax.ShapeDtypeStruct((4, 8, LANES), F32), jax.ShapeDtypeStruct((8, LANES), F32),
                   jax.ShapeDtypeStruct((4, 2, _line_width(A_HALF_WINDOW)), F32),
                   jax.ShapeDtypeStruct((4, 2, Q_BLOCK, 1), F32)],
        compiler_params=_params("arbitrary"),
    )(sink.reshape(8), qkv, qkv, qkv, gain_q, gain_k, bias, delta, lse, d_out)


B_W = Q_BLOCK + 2 * B_HALF_WINDOW
B_PAD_MAX = B_HALF_WINDOW * B_DILATIONS[-1]


def _attn_b_fwd(qkv, gain_q, gain_k, bias):
    def body(q_ref, k_ref, v_ref, gq_ref, gk_ref, line_ref, o_ref, lse_ref, qn_ref, kp_ref, vp_ref, bias_ref):
        g = pl.program_id(1)
        _prep_q(q_ref, gq_ref, qn_ref)
        _unroll_bias(line_ref, bias_ref, B_W)

        def first(rows, out, lse):
            o_ref[rows, :] = out
            lse_ref[rows, :] = lse

        def combine(rows, out, lse):
            old = lse_ref[rows, :]
            new = jnp.maximum(old, lse) + jnp.log(1.0 + jnp.exp(-jnp.abs(old - lse)))
            o_ref[rows, :] = o_ref[rows, :] * jnp.exp(old - new) + out * jnp.exp(lse - new)
            lse_ref[rows, :] = new

        for gi, d in enumerate(B_DILATIONS):
            @pl.when(g == gi)
            def _():
                _prep_kv(k_ref, v_ref, gk_ref, kp_ref, vp_ref, B_HALF_WINDOW * d)
                _fwd_tiles(qn_ref, kp_ref, vp_ref, bias_ref, first if gi == 0 else combine,
                           d=d, half_window=B_HALF_WINDOW)

    vec = pl.BlockSpec((1, LANES), lambda hp, g: (0, 0))
    padded = pltpu.VMEM((SEQ + 2 * B_PAD_MAX, LANES), F32)
    return pl.pallas_call(
        body, name="attn_b_fwd", grid=(4, 3),
        in_specs=[_seq_block(lambda hp, g: (0, QB_BLK + 4 * g + hp)), _seq_block(lambda hp, g: (0, KB_BLK + 4 * g + hp)),
                  _seq_block(lambda hp, g: (0, VB_BLK + 4 * g + hp)), vec, vec,
                  pl.BlockSpec((None, 2, _line_width(B_HALF_WINDOW)), lambda hp, g: (4 * g + hp, 0, 0))],
        out_specs=[_seq_block(lambda hp, g: (0, hp)), _seq_block(lambda hp, g: (0, hp))],
        out_shape=[jax.ShapeDtypeStruct((SEQ, 512), F32)] * 2,
        scratch_shapes=[pltpu.VMEM((SEQ, LANES), F32), padded, padded, pltpu.VMEM((2 * Q_BLOCK, B_W), F32)],
        compiler_params=_params("arbitrary", "arbitrary"),
    )(qkv, qkv, qkv, gain_q, gain_k, bias)


def _attn_b_bwd(qkv, gain_q, gain_k, bias, delta, lse, d_out):
    def body(q_ref, k_ref, v_ref, gq_ref, gk_ref, line_ref, delta_ref, lse_ref, do_ref,
             dq_out, dk_out, dv_out, dgq_out, dgk_out, dline_out,
             qn_ref, kp_ref, vp_ref, dq_ref, dk_ref, dv_ref, bias_ref, ds_out):
        g = pl.program_id(1)
        _prep_q(q_ref, gq_ref, qn_ref)
        _unroll_bias(line_ref, bias_ref, B_W)
        ds_out[...] = jnp.zeros_like(ds_out)
        for gi, d in enumerate(B_DILATIONS):
            @pl.when(g == gi)
            def _():
                pad = B_HALF_WINDOW * d
                for acc in (dk_ref, dv_ref):
                    acc[pl.ds(0, SEQ + 2 * pad), :] = jnp.zeros((SEQ + 2 * pad, LANES), F32)
                _prep_kv(k_ref, v_ref, gk_ref, kp_ref, vp_ref, pad)
                _bwd_tiles(qn_ref, kp_ref, vp_ref, bias_ref, do_ref, lse_ref, delta_ref, dq_ref, dk_ref, dv_ref,
                           ds_out, d=d, half_window=B_HALF_WINDOW)
                dgk_out[...] = _rows8(_norm_bwd(k_ref, gk_ref, dk_ref, pad, dk_out, 1.0))
                dv_out[...] = dv_ref[pl.ds(pad, SEQ), :].astype(BF16)
        _fold_bias_grad(ds_out, dline_out, B_W)
        dgq_out[...] = _rows8(_norm_bwd(q_ref, gq_ref, dq_ref, 0, dq_out, SCALE))

    vec = pl.BlockSpec((1, LANES), lambda hp, g: (0, 0))
    seq_f32 = pltpu.VMEM((SEQ, LANES), F32)
    padded = pltpu.VMEM((SEQ + 2 * B_PAD_MAX, LANES), F32)
    part = pl.BlockSpec((None, 8, LANES), lambda hp, g: (4 * g + hp, 0, 0))
    line = pl.BlockSpec((None, 2, _line_width(B_HALF_WINDOW)), lambda hp, g: (4 * g + hp, 0, 0))
    return pl.pallas_call(
        body, name="attn_b_bwd", grid=(4, 3),
        in_specs=[_seq_block(lambda hp, g: (0, QB_BLK + 4 * g + hp)), _seq_block(lambda hp, g: (0, KB_BLK + 4 * g + hp)),
                  _seq_block(lambda hp, g: (0, VB_BLK + 4 * g + hp)), vec, vec,
                  line,
                  _seq_block(lambda hp, g: (0, hp)), _seq_block(lambda hp, g: (0, hp)), _seq_block(lambda hp, g: (0, hp))],
        out_specs=[pl.BlockSpec((None, SEQ, LANES), lambda hp, g: (4 * g + hp, 0, 0))] * 3 + [part, part, line],
        out_shape=[jax.ShapeDtypeStruct((12, SEQ, LANES), BF16)] * 3
        + [jax.ShapeDtypeStruct((12, 8, LANES), F32)] * 2
        + [jax.ShapeDtypeStruct((12, 2, _line_width(B_HALF_WINDOW)), F32)],
        scratch_shapes=[seq_f32, padded, padded, seq_f32, padded, padded,
                        pltpu.VMEM((2 * Q_BLOCK, B_W), F32), pltpu.VMEM((2 * Q_BLOCK, B_W), F32)],
        compiler_params=_params("arbitrary", "arbitrary"),
    )(qkv, qkv, qkv, gain_q, gain_k, bias, delta, lse, d_out)


def _sigmoid(t):
    return 1.0 / (1.0 + jnp.exp(-t))


def _middle(out_a, out_b, gates, x, target, w_a, w_b, w_out, b_merge):
    tm = 256
    n_steps = SEQ // tm

    def body(oa_ref, ob_ref, g_ref, x_ref, t_ref, wa_ref, wb_ref, wo_ref, bm_ref,
             dy_ref, dg_ref, doa_ref, dob_ref, dla_ref, dlb_ref, dwa_ref, dwb_ref, dwo_ref, dbm_ref, sq_ref):
        @pl.when(pl.program_id(0) == 0)
        def _():
            for ref in (dwa_ref, dwb_ref, dwo_ref, dbm_ref, sq_ref):
                ref[...] = jnp.zeros_like(ref)

        gate_a, gate_b = g_ref[:, 0:512], g_ref[:, 512:1024]
        sig_a, sig_b = _sigmoid(gate_a), _sigmoid(gate_b)
        silu_a, silu_b = gate_a * sig_a, gate_b * sig_b
        oa, ob = oa_ref[...], ob_ref[...]
        ya, yb = (oa * silu_a).astype(BF16), (ob * silu_b).astype(BF16)
        br_a, br_b = _dot(ya, wa_ref[...]), _dot(yb, wb_ref[...])
        m0 = _sigmoid(g_ref[:, 1024:2048] + bm_ref[0:1, :])
        m1 = _sigmoid(g_ref[:, 2048:3072] + bm_ref[1:2, :])
        merged = (m0 * br_a + m1 * br_b).astype(BF16)
        err = (x_ref[...] + _dot(merged, wo_ref[...])) - t_ref[...]
        sq_ref[...] += jnp.sum(err * err, axis=0, keepdims=True)

        dy = err * (1.0 / D_MODEL)
        dy_ref[...] = dy
        dyb = dy.astype(BF16)
        dmerged = _dot(dyb, wo_ref[...], NT)
        dwo_ref[...] += _dot(merged, dyb, TN)
        dbr_a, dbr_b = (dmerged * m0).astype(BF16), (dmerged * m1).astype(BF16)
        dm0 = (dmerged * br_a) * (m0 * (1.0 - m0))
        dm1 = (dmerged * br_b) * (m1 * (1.0 - m1))
        dbm_ref[0:1, :] += jnp.sum(dm0, axis=0, keepdims=True)
        dbm_ref[1:2, :] += jnp.sum(dm1, axis=0, keepdims=True)
        for s in range(N_CHIPS):
            cols = slice(256 * s, 256 * (s + 1))
            dwa_ref[s] += _dot(ya, dbr_a[:, cols], TN)
            dwb_ref[s] += _dot(yb, dbr_b[:, cols], TN)
        dya, dyb_ = _dot(dbr_a, wa_ref[...], NT), _dot(dbr_b, wb_ref[...], NT)
        doa, dob = dya * silu_a, dyb_ * silu_b
        doa_ref[...] = doa
        dob_ref[...] = dob
        for blk in range(512 // LANES):
            lanes = slice(blk * LANES, (blk + 1) * LANES)
            dla_ref[:, lanes] = _half_sum(doa[:, lanes] * oa[:, lanes], None)
            dlb_ref[:, lanes] = _half_sum(dob[:, lanes] * ob[:, lanes], None)
        d_gates = (((dya * oa) * (sig_a * (1.0 + gate_a * (1.0 - sig_a)))).astype(BF16),
                   ((dyb_ * ob) * (sig_b * (1.0 + gate_b * (1.0 - sig_b)))).astype(BF16),
                   dm0.astype(BF16), dm1.astype(BF16))
        blk = 0
        for part in d_gates:
            for c0 in range(0, part.shape[1], 256):
                dg_ref[blk] = part[:, c0:c0 + 256]
                blk += 1

    def rows(width):
        return pl.BlockSpec((tm, width), lambda i: (i, 0))

    def whole(*shape):
        return pl.BlockSpec(shape, lambda i: (0,) * len(shape))

    return pl.pallas_call(
        body, name="middle", grid=(n_steps,),
        in_specs=[rows(512), rows(512), rows(GATE_WIDTH), rows(D_MODEL), rows(D_MODEL),
                  whole(512, D_MODEL), whole(512, D_MODEL), whole(D_MODEL, D_MODEL), whole(2, D_MODEL)],
        out_specs=[rows(D_MODEL), pl.BlockSpec((GATE_WIDTH // 256, tm, 256), lambda i: (0, i, 0)),
                   rows(512), rows(512), rows(512), rows(512),
                   whole(N_CHIPS, 512, 256), whole(N_CHIPS, 512, 256), whole(D_MODEL, D_MODEL),
                   whole(2, D_MODEL), whole(1, D_MODEL)],
        out_shape=[jax.ShapeDtypeStruct((SEQ, D_MODEL), F32), jax.ShapeDtypeStruct((GATE_WIDTH // 256, SEQ, 256), BF16),
                   jax.ShapeDtypeStruct((SEQ, 512), F32), jax.ShapeDtypeStruct((SEQ, 512), F32),
                   jax.ShapeDtypeStruct((SEQ, 512), F32), jax.ShapeDtypeStruct((SEQ, 512), F32),
                   jax.ShapeDtypeStruct((N_CHIPS, 512, 256), F32), jax.ShapeDtypeStruct((N_CHIPS, 512, 256), F32),
                   jax.ShapeDtypeStruct((D_MODEL, D_MODEL), F32), jax.ShapeDtypeStruct((2, D_MODEL), F32),
                   jax.ShapeDtypeStruct((1, D_MODEL), F32)],
        compiler_params=_params("arbitrary"),
    )(out_a, out_b, gates, x, target, w_a, w_b, w_out, b_merge)


def _which(j, edges, fns):
    lo = 0
    for hi, fn in zip(edges, fns):
        pl.when((j >= lo) & (j < hi))(fn)
        lo = hi


def _sibling_rows(tile, core):
    lo, hi = tile * W_BLOCK, (tile + 1) * W_BLOCK
    for chip in range(N_CHIPS):
        a = chip * W_IN_SHARD + (1 - core) * (W_IN_SHARD // 2)
        first, last = max(lo, a), min(hi, a + W_IN_SHARD // 2)
        if first < last:
            return chip, first - a, first - lo, last - first
    return None


def _d_w_in(d_proj, h, rest=None):
    plan, step, width = [], 0, 0
    for p in d_proj:
        total = p.shape[0] * p.shape[2]
        if width + total <= W_BLOCK:
            plan.append((p.shape[0], step, 1))
            width += total
            if width == W_BLOCK:
                step, width = step + 1, 0
        else:
            assert width == 0 and total % W_BLOCK == 0
            plan.append((W_BLOCK // p.shape[2], step, total // W_BLOCK))
            step += total // W_BLOCK
    assert width == 0 and step == IN_WIDTH // W_BLOCK
    firsts = sorted({first for _, first, _ in plan})
    edges = firsts[1:] + [step]
    halves = 2

    hand_over = rest is not None
    half = W_IN_SHARD // 2

    def body(*refs):
        if hand_over:
            pieces, h_ref, rest_ref = refs[:len(d_proj)], refs[len(d_proj)], refs[len(d_proj) + 1]
            o_ref, got_ref, got_rest_ref, acc_ref, send_sems, recv_sems, stage = refs[len(d_proj) + 2:]
        else:
            pieces, h_ref, o_ref, acc_ref = refs[:-3], refs[-3], refs[-2], refs[-1]
        k = pl.program_id(1)

        def emit(group):
            def fn():
                cols = jnp.concatenate([ref[b] for ref in group for b in range(ref.shape[0])], axis=1)
                term = _dot(cols, h_ref[...], TN)

                @pl.when(k == 0)
                def _():
                    acc_ref[...] = term

                @pl.when(k == halves - 1)
                def _():
                    o_ref[...] = (acc_ref[...] + term).astype(BF16)
            return fn

        groups = [[ref for ref, (_, first, _) in zip(pieces, plan) if first == f] for f in firsts]
        _which(pl.program_id(0), edges, [emit(group) for group in groups])

        if hand_over:
            cx, cy, c = lax.axis_index("x"), lax.axis_index("y"), lax.axis_index("c")
            sibling = (cx, cy, 1 - c)

            def to_sibling(sem, src, dst, recv=0):
                return pltpu.make_async_remote_copy(src_ref=src, dst_ref=dst, send_sem=send_sems.at[sem],
                                                    recv_sem=recv_sems.at[recv], device_id=sibling, device_id_type=MESH)

            def tile_copy(tile, core):
                chip, row, start, rows = _sibling_rows(tile, core)
                return to_sibling(tile % 2, stage.at[tile % 2, pl.ds(0, rows), :], got_ref.at[chip, pl.ds(row, rows), :])

            rest_copy = to_sibling(2, _half_rows(rest_ref, 1 - c), got_rest_ref, recv=1)

            @pl.when((pl.program_id(0) == 0) & (k == 0))
            def _():
                rest_copy.start()

            for tile in range(step):
                for core in range(2):
                    @pl.when((pl.program_id(0) == tile) & (k == halves - 1) & (c == core))
                    def _(tile=tile, core=core):
                        if tile >= 2 and _sibling_rows(tile - 2, core):
                            tile_copy(tile - 2, core).wait_send()
                        if _sibling_rows(tile, core):
                            _, _, start, rows = _sibling_rows(tile, core)
                            stage[tile % 2, 0:rows, :] = o_ref[start:start + rows, :]
                            tile_copy(tile, core).start()
                        if tile == step - 1:
                            for last in (step - 2, step - 1):
                                if _sibling_rows(last, core):
                                    tile_copy(last, core).wait_send()
                            rest_copy.wait()
                            to_sibling(0, got_ref, got_ref).wait_recv()

    def cols_spec(piece, n, first, steps):
        def index(j, k):
            return jnp.clip(j - first, 0, steps - 1), jnp.where((j >= first) & (j < first + steps), k, 0), 0
        return pl.BlockSpec((n, SEQ // halves, piece.shape[2]), index)

    tile_spec = pl.BlockSpec((W_BLOCK, D_MODEL), lambda j, k: (j, 0))
    in_specs = [cols_spec(p, *pl_) for p, pl_ in zip(d_proj, plan)] + [
        pl.BlockSpec((SEQ // halves, D_MODEL), lambda j, k: (k, 0))]
    acc = pltpu.VMEM((W_BLOCK, D_MODEL), F32)
    if not hand_over:
        return pl.pallas_call(
            body, name="d_w_in", grid=(step, halves), in_specs=in_specs, out_specs=tile_spec,
            out_shape=jax.ShapeDtypeStruct((IN_WIDTH, D_MODEL), BF16), scratch_shapes=[acc],
            compiler_params=_params("arbitrary", "arbitrary"),
        )(*d_proj, h)
    return pl.pallas_call(
        body, name="d_w_in", grid=(step, halves), in_specs=in_specs + [ANY], out_specs=[tile_spec, ANY, ANY],
        out_shape=[jax.ShapeDtypeStruct((IN_WIDTH, D_MODEL), BF16),
                   jax.ShapeDtypeStruct((N_CHIPS, half, D_MODEL), BF16),
                   jax.ShapeDtypeStruct((N_CHIPS, rest.shape[1] // 2, D_MODEL), BF16)],
        scratch_shapes=[acc, pltpu.SemaphoreType.DMA((3,)), pltpu.SemaphoreType.DMA((2,)),
                        pltpu.VMEM((2, W_BLOCK, D_MODEL), BF16)],
        compiler_params=_params("arbitrary", "arbitrary"),
    )(*d_proj, h, rest)


RELAY_STEP = 10
RELAY_ROWS = 352


def _d_x(d_proj, w_t, x, gain, dy, chip_sums):
    tm = 256
    n_steps = SEQ // tm
    n_w = IN_WIDTH // W_BLOCK
    n_p, n_s = len(d_proj), len(chip_sums)

    def body(*refs):
        pieces, w_refs = refs[:n_p], refs[n_p:n_p + n_w]
        x_ref, g_ref, dy_ref = refs[n_p + n_w:n_p + n_w + 3]
        q_refs = refs[n_p + n_w + 3:n_p + n_w + 3 + n_s]
        dx_ref, dgain_ref = refs[n_p + n_w + 3 + n_s:n_p + n_w + 5 + n_s]
        outs = refs[n_p + n_w + 5 + n_s:n_p + n_w + 5 + 4 * n_s]
        got_refs, relay_refs, sum_refs = outs[:n_s], outs[n_s:2 * n_s], outs[2 * n_s:]
        if n_s:
            send_sems, recv_sems, local_sems, a_buf, b_buf, c_buf = refs[n_p + n_w + 5 + 4 * n_s:]

        def hops():
            cx, cy, c = lax.axis_index("x"), lax.axis_index("y"), lax.axis_index("c")
            near = (cx + (1 - c) - 2 * cx * (1 - c), cy + c - 2 * cy * c)
            far = (cx + c - 2 * cx * c, cy + (1 - c) - 2 * cy * (1 - c))
            chip = lambda p: 2 * p[0] + p[1]

            def copy(k, src, dst, to):
                return pltpu.make_async_remote_copy(src_ref=src, dst_ref=dst, send_sem=send_sems.at[k],
                                                    recv_sem=recv_sems.at[k], device_id=(*to, c), device_id_type=MESH)

            first = [(copy(3 * b, q.at[chip(near)], got.at[0], near),
                      copy(3 * b + 1, q.at[3 - chip((cx, cy))], relay, near))
                     for b, (q, got, relay) in enumerate(zip(q_refs, got_refs, relay_refs))]
            second = [copy(3 * b + 2, s, got.at[1], far) for b, (s, got) in enumerate(zip(sum_refs, got_refs))]
            return first, second, chip(far)

        @pl.when(pl.program_id(0) == 0)
        def _():
            dgain_ref[...] = jnp.zeros_like(dgain_ref)
            if n_s:
                for direct, pass_on in hops()[0]:
                    direct.start()
                    pass_on.start()

        if n_s:
            @pl.when(pl.program_id(0) == RELAY_STEP)
            def _():
                first, second, far_chip = hops()
                for b, (q, relay, total) in enumerate(zip(q_refs, relay_refs, sum_refs)):
                    first[b][1].wait_recv()
                    half = relay.shape[0]
                    for r0 in range(0, half, RELAY_ROWS):
                        rows = min(RELAY_ROWS, half - r0)
                        mine = pltpu.make_async_copy(q.at[far_chip, pl.ds(r0, rows), :], a_buf.at[pl.ds(0, rows), :],
                                                     local_sems.at[0])
                        theirs = pltpu.make_async_copy(relay.at[pl.ds(r0, rows), :], b_buf.at[pl.ds(0, rows), :],
                                                       local_sems.at[1])
                        mine.start()
                        theirs.start()
                        mine.wait()
                        theirs.wait()
                        c_buf[0:rows, :] = (a_buf[0:rows, :].astype(F32) + b_buf[0:rows, :].astype(F32)).astype(BF16)
                        store = pltpu.make_async_copy(c_buf.at[pl.ds(0, rows), :], total.at[pl.ds(r0, rows), :],
                                                      local_sems.at[2])
                        store.start()
                        store.wait()
                    second[b].start()

        blocks = [(piece, k) for piece in pieces for k in range(piece.shape[0])]
        dh, group, width, blk = None, [], 0, 0
        for piece, k in blocks:
            group.append(piece[k])
            width += piece.shape[2]
            if width == W_BLOCK:
                term = _dot(jnp.concatenate(group, axis=1), w_refs[blk][...])
                dh = term if dh is None else dh + term
                group, width, blk = [], 0, blk + 1
        assert not group and blk == n_w
        xf = x_ref[...]
        r = lax.rsqrt(jnp.mean(xf * xf, axis=-1, keepdims=True) + EPS)
        xh = xf * r
        dxh = dh * g_ref[...]
        dx_ref[...] = r * (dxh - xh * jnp.mean(dxh * xh, axis=-1, keepdims=True)) + dy_ref[...]
        dgain_ref[...] += _rows8(jnp.sum(dh * xh, axis=0, keepdims=True))

        if n_s:
            @pl.when(pl.program_id(0) == n_steps - 1)
            def _():
                first, second, _ = hops()
                for direct, pass_on in first:
                    direct.wait()
                    pass_on.wait_send()
                for cp in second:
                    cp.wait()

    row = pl.BlockSpec((tm, D_MODEL), lambda i: (i, 0))
    halves = [q.shape[1] for q in chip_sums]
    res = pl.pallas_call(
        body, name="d_x", grid=(n_steps,),
        in_specs=[pl.BlockSpec((p.shape[0], tm, p.shape[2]), lambda i: (0, i, 0)) for p in d_proj] + _w_blocks(0, n_w)
        + [row, pl.BlockSpec((1, D_MODEL), lambda i: (0, 0)), row] + [ANY] * n_s,
        out_specs=[row, pl.BlockSpec((8, D_MODEL), lambda i: (0, 0))] + [ANY] * (3 * n_s),
        out_shape=[jax.ShapeDtypeStruct((SEQ, D_MODEL), F32), jax.ShapeDtypeStruct((8, D_MODEL), F32)]
        + [jax.ShapeDtypeStruct((2, half, D_MODEL), BF16) for half in halves]
        + [jax.ShapeDtypeStruct((half, D_MODEL), BF16) for half in halves] * 2,
        scratch_shapes=[pltpu.SemaphoreType.DMA((3 * n_s,)), pltpu.SemaphoreType.DMA((3 * n_s,)),
                        pltpu.SemaphoreType.DMA((3,))] + [pltpu.VMEM((RELAY_ROWS, D_MODEL), BF16)] * 3 if n_s else [],
        compiler_params=_params("arbitrary"),
    )(*d_proj, *([w_t] * n_w), x, gain, dy, *chip_sums)
    return res[0], res[1], res[2:2 + n_s]


def _my_place():
    x, y, c = lax.axis_index("x"), lax.axis_index("y"), lax.axis_index("c")
    return jnp.stack([2 * x + y, c]).astype(jnp.int32)


def _half_rows(ref, half):
    rows = ref.shape[-2] // 2
    idx = (slice(None),) * (len(ref.shape) - 2) + (pl.ds(pl.multiple_of(half * rows, 16), rows), slice(None))
    return ref.at[idx]


def _add_halves(place, grads, theirs, name):
    half = theirs.shape[1]
    tr = _row_tile(half)
    n = half // tr

    def body(place_ref, g_ref, t_ref, o_ref):
        o_ref[...] = (g_ref[...].astype(F32) + t_ref[...].astype(F32)).astype(BF16)

    return pl.pallas_call(
        body, name=name,
        grid_spec=pltpu.PrefetchScalarGridSpec(
            num_scalar_prefetch=1, grid=(N_CHIPS, n),
            in_specs=[pl.BlockSpec((None, tr, D_MODEL), lambda s, i, p: (s, p[1] * n + i, 0)),
                      pl.BlockSpec((None, tr, D_MODEL), lambda s, i, p: (s, i, 0))],
            out_specs=pl.BlockSpec((None, tr, D_MODEL), lambda s, i, p: (s, i, 0))),
        out_shape=jax.ShapeDtypeStruct((N_CHIPS, half, D_MODEL), BF16),
        compiler_params=_params("arbitrary", "arbitrary"),
    )(place, grads, theirs)


def _add_chips(place, chip_sums, others, name):
    half = others.shape[1]
    tr = _row_tile(half)
    n = half // tr

    def body(place_ref, q_ref, o_ref, r_ref):
        acc = q_ref[...].astype(F32)
        for j in range(others.shape[0]):
            acc = acc + o_ref[j].astype(F32)
        r_ref[...] = acc

    return pl.pallas_call(
        body, name=name,
        grid_spec=pltpu.PrefetchScalarGridSpec(
            num_scalar_prefetch=1, grid=(n,),
            in_specs=[pl.BlockSpec((None, tr, D_MODEL), lambda i, p: (p[0], i, 0)),
                      pl.BlockSpec((others.shape[0], tr, D_MODEL), lambda i, p: (0, i, 0))],
            out_specs=pl.BlockSpec((tr, D_MODEL), lambda i, p: (p[1] * n + i, 0))),
        out_shape=jax.ShapeDtypeStruct((2 * half, D_MODEL), F32),
        compiler_params=_params("arbitrary"),
    )(place, chip_sums, others)


def _join_halves(shards, block):
    n = len(shards)
    rows = block.shape[0]

    def body(*refs):
        b_ref, o_refs, sum_ref = refs[n], refs[n + 1:2 * n + 1], refs[2 * n + 1]
        send_sems, recv_sems, small_send, small_recv, local_sem, all_ref = refs[2 * n + 2:]
        x, y, c = lax.axis_index("x"), lax.axis_index("y"), lax.axis_index("c")
        me, sibling = (x, y, c), (x, y, 1 - c)
        chips = [(1 - x, y), (x, 1 - y), (1 - x, 1 - y)]

        def half(k, rows_ref):
            return pltpu.make_async_remote_copy(src_ref=rows_ref, dst_ref=rows_ref, send_sem=send_sems.at[k],
                                                recv_sem=recv_sems.at[k], device_id=sibling, device_id_type=MESH)

        def at(px, py, pc):
            return all_ref.at[pl.ds(pl.multiple_of((4 * px + 2 * py + pc) * rows, 8), rows), :]

        def small(k, block_of, to, src=None):
            return pltpu.make_async_remote_copy(src_ref=at(*block_of) if src is None else src, dst_ref=at(*block_of),
                                                send_sem=small_send.at[k], recv_sem=small_recv.at[k],
                                                device_id=to, device_id_type=MESH)

        sends = [half(k, _half_rows(o, c)) for k, o in enumerate(o_refs)]
        for cp in sends:
            cp.start()
        mine = pltpu.make_async_copy(b_ref, at(*me), local_sem)
        mine.start()
        first = [small(0, me, sibling, src=b_ref)]
        first += [small(1 + j, me, (*chip, c), src=b_ref) for j, chip in enumerate(chips)]
        for cp in first:
            cp.start()
        passed = [small(4 + j, (*chip, c), sibling) for j, chip in enumerate(chips)]
        for j, chip in enumerate(chips):
            small(1 + j, (*chip, c), me).wait_recv()
            passed[j].start()
        small(0, sibling, me).wait_recv()
        for j, chip in enumerate(chips):
            small(4 + j, (*chip, 1 - c), me).wait_recv()
        mine.wait()
        acc = all_ref[0:rows, :]
        for dev in range(1, 8):
            acc = acc + all_ref[rows * dev:rows * (dev + 1), :]
        sum_ref[...] = acc
        for k, o in enumerate(o_refs):
            half(k, _half_rows(o, 1 - c)).wait_recv()
        for cp in sends + first + passed:
            cp.wait_send()

    res = pl.pallas_call(
        body, name="reduce_join_halves", in_specs=[ANY] * n + [pl.BlockSpec(memory_space=pltpu.VMEM)],
        out_specs=[ANY] * n + [pl.BlockSpec(memory_space=pltpu.VMEM)],
        out_shape=[jax.ShapeDtypeStruct(s.shape, F32) for s in shards] + [jax.ShapeDtypeStruct(block.shape, F32)],
        input_output_aliases={k: k for k in range(n)},
        scratch_shapes=[pltpu.SemaphoreType.DMA((n,)), pltpu.SemaphoreType.DMA((n,)),
                        pltpu.SemaphoreType.DMA((7,)), pltpu.SemaphoreType.DMA((7,)), pltpu.SemaphoreType.DMA,
                        pltpu.VMEM((8 * rows, D_MODEL), F32)],
    )(*shards, block)
    return res[:n], res[n]


def _adamw_math(w, g, m, v):
    m = ADAM_B1 * m + (1.0 - ADAM_B1) * g
    v = ADAM_B2 * v + (1.0 - ADAM_B2) * (g * g)
    m_hat = m / (1.0 - ADAM_B1 ** ADAM_STEP)
    v_hat = v / (1.0 - ADAM_B2 ** ADAM_STEP)
    return -ADAM_LR * (m_hat / (jnp.sqrt(v_hat) + ADAM_EPS) + ADAM_WD * w), m, v


def _adamw(w, g, m, v, name):
    r, c = w.shape
    tr = _row_tile(r)

    def body(w_ref, g_ref, m_ref, v_ref, d_ref, nm_ref, nv_ref):
        d_ref[...], nm_ref[...], nv_ref[...] = _adamw_math(w_ref[...], g_ref[...], m_ref[...], v_ref[...])

    spec = pl.BlockSpec((tr, c), lambda i: (i, 0))
    return pl.pallas_call(
        body, name=name, grid=(r // tr,), in_specs=[spec] * 4, out_specs=[spec] * 3,
        out_shape=[jax.ShapeDtypeStruct((r, c), F32)] * 3, compiler_params=_params("arbitrary"),
    )(w, g, m, v)


def _adamw_small(ws, gs, ms, vs):
    n = len(ws)

    def body(*refs):
        ins, outs = refs[:4 * n], refs[4 * n:]
        for k in range(n):
            d, m, v = _adamw_math(ins[k][...], ins[n + k][...], ins[2 * n + k][...], ins[3 * n + k][...])
            outs[k][...], outs[n + k][...], outs[2 * n + k][...] = d, m, v

    shapes = [jax.ShapeDtypeStruct(w.shape, F32) for w in ws]
    res = pl.pallas_call(body, name="adamw_small", out_shape=shapes * 3)(*ws, *gs, *ms, *vs)
    return res[:n], res[n:2 * n], res[2 * n:]


def _fold_heads(partials):
    t = jnp.sum(partials[:, 0, :], axis=0)
    return (t[:HEAD_DIM] + t[HEAD_DIM:]).reshape(1, HEAD_DIM)


def _local_step(x, target, norm_gain, w_t, w_a, w_b, w_o, b_m, q_norm_a, k_norm_a, q_norm_b, k_norm_b, sink_a,
                rel_bias, start_reduce=None, small_shard=None):
    two = lambda gain: jnp.concatenate([gain, gain], axis=1)
    bias_a = _bias_lines(rel_bias[:, :8], A_HALF_WINDOW, 1)
    bias_b = jnp.concatenate([_bias_lines(rel_bias[:, 8 + 8 * g:16 + 8 * g], B_HALF_WINDOW, d)
                              for g, d in enumerate(B_DILATIONS)], axis=0)

    qkv, h, *small_all = _in_proj(x, norm_gain, w_t, 0, QKV_WIDTH // W_BLOCK, BF16, "in_proj_qkv", True, small_shard)
    if small_shard is not None:
        w_a, w_b, w_o, b_m = _unpack_weights(small_all[0])
    gates, = _in_proj(x, norm_gain, w_t, QKV_WIDTH // W_BLOCK, GATE_WIDTH // W_BLOCK, F32, "in_proj_gates", False)
    out_a, lse_a = _attn_a_fwd(qkv, two(q_norm_a), two(k_norm_a), bias_a, sink_a)
    out_b, lse_b = _attn_b_fwd(qkv, two(q_norm_b), two(k_norm_b), bias_b)

    dy, dgates, d_out_a, d_out_b, delta_a, delta_b, d_wa, d_wb, d_wo, d_bm, sq = _middle(
        out_a, out_b, gates, x, target, w_a, w_b, w_o, b_m)
    loss = (0.5 / D_MODEL) * jnp.sum(sq)

    dq_a, dkv_a, dgq_a, dgk_a, ds_a, dsink = _attn_a_bwd(
        qkv, two(q_norm_a), two(k_norm_a), bias_a, sink_a, delta_a, lse_a, d_out_a)
    dq_b, dk_b, dv_b, dgq_b, dgk_b, ds_b = _attn_b_bwd(
        qkv, two(q_norm_b), two(k_norm_b), bias_b, delta_b, lse_b, d_out_b)
    d_proj = (dq_a, dkv_a, dq_b, dk_b, dv_b, dgates)

    d_bm_rows = jnp.pad(d_bm.reshape(2, N_CHIPS, 256).transpose(1, 0, 2),
                        ((0, 0), (0, REST_ROWS - 514), (0, D_MODEL - 256)))
    rest = jnp.concatenate([d_wo.reshape(N_CHIPS, 256, D_MODEL), d_wa.reshape(N_CHIPS, 128, D_MODEL),
                            d_wb.reshape(N_CHIPS, 128, D_MODEL), d_bm_rows], axis=1)
    if start_reduce is None:
        grads, chip_sums = [_d_w_in(d_proj, h).reshape(N_CHIPS, W_IN_SHARD, D_MODEL), rest], []
    else:
        d_wt, *theirs = _d_w_in(d_proj, h, rest.astype(BF16))
        grads = [d_wt.reshape(N_CHIPS, W_IN_SHARD, D_MODEL), rest]
        chip_sums = start_reduce(grads, theirs)
    grad_x, d_gain, others = _d_x(d_proj, w_t, x, norm_gain, dy, chip_sums)

    d_rel = jnp.concatenate(
        [_bias_grad(ds_a, A_HALF_WINDOW, 1)]
        + [_bias_grad(ds_b[4 * g:4 * g + 4], B_HALF_WINDOW, d) for g, d in enumerate(B_DILATIONS)], axis=1)
    d_sink = jnp.sum(dsink, axis=(2, 3)).reshape(1, 8)
    dgk_a_row = dgk_a[0]
    small = jnp.zeros((8, D_MODEL), F32)
    small = small.at[0].set(d_gain[0])
    small = small.at[1].set(d_rel.reshape(-1))
    misc = jnp.concatenate([_fold_heads(dgq_a), (dgk_a_row[:HEAD_DIM] + dgk_a_row[HEAD_DIM:]).reshape(1, HEAD_DIM),
                            _fold_heads(dgq_b), _fold_heads(dgk_b), d_sink], axis=1)
    small = small.at[2, :264].set(misc[0])

    return loss, grad_x, grads, small, chip_sums, others


def _unpack_weights(small_all):
    sm = small_all.reshape(N_CHIPS, SMALL_ROWS, D_MODEL)
    w_o = sm[:, 0:256].reshape(D_MODEL, D_MODEL)
    w_a = sm[:, 256:384].reshape(N_CHIPS, 512, 256).transpose(1, 0, 2).reshape(512, D_MODEL)
    w_b = sm[:, 384:512].reshape(N_CHIPS, 512, 256).transpose(1, 0, 2).reshape(512, D_MODEL)
    b_m = lax.bitcast_convert_type(sm[:, 512].reshape(N_CHIPS, 2, 256, 2), F32)
    return w_a, w_b, w_o, b_m.transpose(1, 0, 2).reshape(2, D_MODEL)


def _pack_small_weights(w_branch_a, w_branch_b, b_merge, w_out):
    b_m = jnp.pad(lax.bitcast_convert_type(b_merge, BF16).reshape(1, D_MODEL), ((0, SMALL_ROWS - 513), (0, 0)))
    return jnp.concatenate([w_out.astype(BF16), w_branch_a.astype(BF16).reshape(128, D_MODEL),
                            w_branch_b.astype(BF16).reshape(128, D_MODEL), b_m], axis=0)


def kernel(x, norm_gain, w_in, q_norm_a, k_norm_a, q_norm_b, k_norm_b, sink_a, rel_bias, w_branch_a, w_branch_b, b_merge, w_out, loss_target, m_norm_gain, m_w_in, m_q_norm_a, m_k_norm_a, m_q_norm_b, m_k_norm_b, m_sink_a, m_rel_bias, m_w_branch_a, m_w_branch_b, m_b_merge, m_w_out, v_norm_gain, v_w_in, v_q_norm_a, v_k_norm_a, v_q_norm_b, v_k_norm_b, v_sink_a, v_rel_bias, v_w_branch_a, v_w_branch_b, v_b_merge, v_w_out):
    w_in_t, m_w_in_t, v_w_in_t = (jnp.transpose(t[0]) for t in (w_in, m_w_in, v_w_in))
    wt_shard = _cast_rows(w_in_t, BF16, "w_in_cast")
    w_t = _gather_weights(wt_shard)
    small_shard = _pack_small_weights(w_branch_a[0], w_branch_b[0], b_merge[0], w_out[0])

    place = _my_place()
    names = ("w_in", "rest")

    def start_reduce(grads, theirs):
        return [_add_halves(place, g, t, "reduce_add_halves_" + n) for g, t, n in zip(grads, theirs, names)]

    loss_part, grad_x, _, small, chip_sums, others = _local_step(
        x[0], loss_target[0], norm_gain, w_t, None, None, None, None, q_norm_a, k_norm_a, q_norm_b, k_norm_b,
        sink_a, rel_bias, start_reduce, small_shard)

    (g_wt, g_rest), small = _join_halves(
        [_add_chips(place, q, o, "reduce_add_chips_" + n) for q, o, n in zip(chip_sums, others, names)],
        small.at[3, 0].set(loss_part))
    loss = small[3, 0]

    g_w_out = g_rest[0:256]
    g_w_a = g_rest[256:384].reshape(512, 256)
    g_w_b = g_rest[384:512].reshape(512, 256)
    g_b_merge = g_rest[512:514, :256]
    g_norm_gain = small[0:1]
    g_rel_bias = small[1].reshape(N_BUCKETS, N_BUCKETS)
    g_q_a, g_k_a, g_q_b, g_k_b = (small[2:3, 64 * k:64 * k + 64] for k in range(4))
    g_sink = small[2:3, 256:264]

    big_names = (("w_branch_a", w_branch_a, g_w_a, m_w_branch_a, v_w_branch_a),
                 ("w_branch_b", w_branch_b, g_w_b, m_w_branch_b, v_w_branch_b),
                 ("w_out", w_out, g_w_out, m_w_out, v_w_out))
    upd = {name: (g,) + tuple(_adamw(w[0], g, m[0], v[0], "adamw_" + name)) for name, w, g, m, v in big_names}
    upd["w_in"] = tuple(jnp.transpose(t) for t in (g_wt,) + tuple(_adamw(w_in_t, g_wt, m_w_in_t, v_w_in_t, "adamw_w_in")))
    small_names = ("norm_gain", "q_norm_a", "k_norm_a", "q_norm_b", "k_norm_b", "sink_a", "rel_bias", "b_merge")
    ws = [norm_gain, q_norm_a, k_norm_a, q_norm_b, k_norm_b, sink_a, rel_bias, b_merge[0]]
    gs = [g_norm_gain, g_q_a, g_k_a, g_q_b, g_k_b, g_sink, g_rel_bias, g_b_merge]
    ms = [m_norm_gain, m_q_norm_a, m_k_norm_a, m_q_norm_b, m_k_norm_b, m_sink_a, m_rel_bias, m_b_merge[0]]
    vs = [v_norm_gain, v_q_norm_a, v_k_norm_a, v_q_norm_b, v_k_norm_b, v_sink_a, v_rel_bias, v_b_merge[0]]
    ds, nms, nvs = _adamw_small(ws, gs, ms, vs)
    for k, name in enumerate(small_names):
        upd[name] = (gs[k], ds[k], nms[k], nvs[k])

    order = ("norm_gain", "w_in", "q_norm_a", "k_norm_a", "q_norm_b", "k_norm_b", "sink_a", "rel_bias",
             "w_branch_a", "w_branch_b", "b_merge", "w_out")
    lead = {"w_in", "w_branch_a", "w_branch_b", "b_merge", "w_out"}
    outs = [loss, grad_x[None]]
    for part in range(4):
        outs += [upd[name][part][None] if name in lead else upd[name][part] for name in order]
    return tuple(outs)
```

```python
import math

import numpy as np
import jax
import jax.numpy as jnp
from jax import lax
from jax.experimental import pallas as pl
from jax.experimental.pallas import tpu as pltpu

F32 = jnp.float32
BF16 = jnp.bfloat16

SEQ = 4096
D_MODEL = 1024
HEAD_DIM = 64
LANES = 128
EPS = 1e-6
NEG_INF = -1e30
SCALE = HEAD_DIM ** -0.5
N_BUCKETS = 32
MAX_DISTANCE = 1024
N_CHIPS = 4

A_HALF_WINDOW = 128
B_HALF_WINDOW = 64
B_DILATIONS = (1, 4, 16)
Q_BLOCK = 128

QKV_WIDTH = 5376
GATE_WIDTH = 3072
QA_BLK, KA_BLK, VA_BLK = 0, 4, 5
QB_BLK, KB_BLK, VB_BLK = 6, 18, 30
IN_WIDTH = QKV_WIDTH + GATE_WIDTH
W_IN_SHARD = IN_WIDTH // N_CHIPS

SMALL_ROWS = 544
REST_ROWS = 544

ADAM_LR = 0.001
ADAM_B1 = 0.9
ADAM_B2 = 0.999
ADAM_EPS = 1e-08
ADAM_WD = 0.01
ADAM_STEP = 10

VMEM_LIMIT = 56 * 1024 * 1024

NT = (((1,), (1,)), ((), ()))
TN = (((0,), (0,)), ((), ()))
MESH = pl.DeviceIdType.MESH
ANY = pl.BlockSpec(memory_space=pl.ANY)


def _dot(a, b, dims=None):
    if dims is None:
        return jnp.dot(a, b, preferred_element_type=F32)
    return lax.dot_general(a, b, dims, preferred_element_type=F32)


def _params(*semantics):
    return pltpu.CompilerParams(dimension_semantics=semantics or None, vmem_limit_bytes=VMEM_LIMIT)


def _line_width(half_window):
    return pl.cdiv(2 * Q_BLOCK + 2 * half_window - 1, LANES) * LANES


def _bucket_onehot(half_window, stride):
    rel = np.arange(_line_width(half_window)) - (Q_BLOCK - 1) - half_window
    band = np.abs(rel) <= half_window
    rel = rel * stride
    half, max_exact = N_BUCKETS // 2, N_BUCKETS // 4
    n = np.abs(rel)
    nf = np.maximum(n, max_exact).astype(np.float32)
    large = max_exact + (np.log(nf / np.float32(max_exact)) / np.float32(math.log(MAX_DISTANCE / max_exact))
                         * np.float32(half - max_exact)).astype(np.int32)
    large = np.minimum(large, half - 1)
    bucket = (rel > 0).astype(np.int32) * half + np.where(n < max_exact, n, large)
    onehot = (bucket[..., None] == np.arange(N_BUCKETS)) & band[..., None]
    return onehot.astype(np.float32), band


def _bias_lines(rel_bias_cols, half_window, stride):
    onehot, band = _bucket_onehot(half_window, stride)
    h = rel_bias_cols.shape[1]
    t = jnp.einsum("tb,bh->ht", jnp.asarray(onehot), rel_bias_cols, precision=lax.Precision.HIGHEST)
    t = t + jnp.asarray(np.where(band, 0.0, NEG_INF).astype(np.float32))
    return t.reshape(h // 2, 2, -1)


def _bias_grad(d_lines, half_window, stride):
    onehot, _ = _bucket_onehot(half_window, stride)
    h = d_lines.shape[0] * 2
    return jnp.einsum("tb,ht->bh", jnp.asarray(onehot), d_lines.reshape(h, -1), precision=lax.Precision.HIGHEST)


def _unroll_bias(line_ref, tile_ref, w):
    width = line_ref.shape[1]
    for j in range(2):
        rows = jnp.broadcast_to(line_ref[j:j + 1, :], (Q_BLOCK, width))
        rows = pltpu.roll(rows, width - (Q_BLOCK - 1), 1, stride=1, stride_axis=0)
        tile_ref[j * Q_BLOCK:(j + 1) * Q_BLOCK, :] = rows[:, :w]


def _fold_bias_grad(tile_ref, line_ref, w):
    width = line_ref.shape[1]
    row = lax.broadcasted_iota(jnp.int32, (Q_BLOCK, Q_BLOCK), 0)
    col = lax.broadcasted_iota(jnp.int32, (Q_BLOCK, Q_BLOCK), 1)
    flip = jnp.where(row + col == Q_BLOCK - 1, 1.0, 0.0).astype(BF16)
    for j in range(2):
        tile = tile_ref[j * Q_BLOCK:(j + 1) * Q_BLOCK, :]
        hi = tile.astype(BF16)
        lo = (tile - hi.astype(F32)).astype(BF16)
        rows = _dot(flip, hi) + _dot(flip, lo)
        rows = jnp.concatenate([rows, jnp.zeros((Q_BLOCK, width - w), F32)], axis=1)
        rows = pltpu.roll(rows, 0, 1, stride=1, stride_axis=0)
        line_ref[j:j + 1, :] = jnp.sum(rows, axis=0, keepdims=True)


def _row_tile(rows):
    return max(t for t in range(16, 385, 16) if rows % t == 0)


def _cast_rows(w, out_dtype, name):
    r, c = w.shape
    tr = _row_tile(r)

    def body(w_ref, o_ref):
        o_ref[...] = w_ref[...].astype(out_dtype)

    spec = pl.BlockSpec((tr, c), lambda i: (i, 0))
    return pl.pallas_call(
        body, name=name, grid=(r // tr,), in_specs=[spec], out_specs=spec,
        out_shape=jax.ShapeDtypeStruct((r, c), out_dtype), compiler_params=_params("arbitrary"),
    )(w)


STAGE_ROWS = 528


def _gather_scratch():
    return [pltpu.SemaphoreType.DMA((12,)), pltpu.SemaphoreType.DMA((12,)), pltpu.SemaphoreType.DMA((2,)),
            pltpu.SemaphoreType.DMA((2,)), pltpu.VMEM((2, STAGE_ROWS, D_MODEL), BF16)]


def _gather_phases(src_ref, out_ref, send_sems, recv_sems, in_sems, out_sems, stage):
    rows = src_ref.shape[0]
    x, y, c = lax.axis_index("x"), lax.axis_index("y"), lax.axis_index("c")
    sibling = (x, y, 1 - c)
    near = (x + (1 - c) - 2 * x * (1 - c), y + c - 2 * y * c)
    far = (x + c - 2 * x * c, y + (1 - c) - 2 * y * (1 - c))
    diag = (1 - x, 1 - y)
    chip_no = lambda chip: 2 * chip[0] + chip[1]
    my_chip = chip_no((x, y))

    pieces = 2 if (rows // 2) % 32 == 0 else 1
    n = rows // 2 // pieces

    def half_of(chip, half, p):
        start = pl.multiple_of(chip * rows + half * (rows // 2) + p * n, 16)
        return out_ref.at[pl.ds(start, n), :]

    def copy(k, p, src, dst, to):
        return pltpu.make_async_remote_copy(src_ref=src, dst_ref=dst, send_sem=send_sems.at[k * pieces + p],
                                            recv_sem=recv_sems.at[k * pieces + p], device_id=to, device_id_type=MESH)

    def mine(p):
        return src_ref.at[pl.ds(pl.multiple_of(c * (rows // 2) + p * n, 16), n), :]

    def keep_own():
        outs = []
        for i, r0 in enumerate(range(0, rows, STAGE_ROWS)):
            n = min(STAGE_ROWS, rows - r0)
            slot = i % 2
            if i >= 2:
                outs[i - 2].wait()
            buf = stage.at[slot, pl.ds(0, n), :]
            load = pltpu.make_async_copy(src_ref.at[pl.ds(r0, n), :], buf, in_sems.at[slot])
            load.start()
            load.wait()
            start = pl.multiple_of(my_chip * rows + r0, 16)
            outs.append(pltpu.make_async_copy(buf, out_ref.at[pl.ds(start, n), :], out_sems.at[slot]))
            outs[i].start()
        for cp in outs[-2:]:
            cp.wait()

    def start():
        for p in range(pieces):
            copy(0, p, mine(p), half_of(my_chip, c, p), (*near, c)).start()
            copy(1, p, mine(p), half_of(my_chip, c, p), (*far, c)).start()
        keep_own()

    def pass_on(j, p, chip):
        landed = half_of(chip_no(chip), c, p)
        copy(3 + j, p, landed, landed, sibling).start()

    def relay():
        for p in range(pieces):
            landed = half_of(chip_no(near), c, p)
            copy(0, p, landed, landed, sibling).wait_recv()
            copy(2, p, landed, landed, (*far, c)).start()
            pass_on(0, p, near)

    def forward():
        for j, chip in ((1, far), (2, diag)):
            for p in range(pieces):
                landed = half_of(chip_no(chip), c, p)
                copy(j, p, landed, landed, sibling).wait_recv()
                pass_on(j, p, chip)

    def finish():
        for j, chip in ((0, far), (1, near), (2, diag)):
            for p in range(pieces):
                other = half_of(chip_no(chip), 1 - c, p)
                copy(3 + j, p, other, other, sibling).wait_recv()
        for k in range(6):
            for p in range(pieces):
                copy(k, p, mine(p), mine(p), sibling).wait_send()

    return start, relay, forward, finish


def _gather_weights(shard):
    def body(src_ref, out_ref, *scratch):
        for phase in _gather_phases(src_ref, out_ref, *scratch):
            phase()

    return pl.pallas_call(
        body, name="gather_weights", in_specs=[ANY], out_specs=ANY,
        out_shape=jax.ShapeDtypeStruct((N_CHIPS * shard.shape[0], D_MODEL), BF16),
        scratch_shapes=_gather_scratch(),
    )(shard)


W_BLOCK = 768


def _w_blocks(first, count):
    return [pl.BlockSpec((W_BLOCK, D_MODEL), lambda *_, k=k: (first + k, 0)) for k in range(count)]


def _in_proj(x, gain, w_t, first_block, n_blocks, out_dtype, name, keep_h, ride=None):
    tm = 512
    n_steps = SEQ // tm
    n_out = 2 if keep_h else 1

    def body(x_ref, g_ref, *refs):
        w_refs, outs = refs[:n_blocks], refs[n_blocks + (ride is not None):n_blocks + (ride is not None) + n_out]
        if ride is not None:
            phases = _gather_phases(refs[n_blocks], *refs[n_blocks + 1 + n_out:])
            for step, phase in zip((0, 2, 4, n_steps - 1), phases):
                pl.when(pl.program_id(0) == step)(phase)
        xf = x_ref[...]
        r = lax.rsqrt(jnp.mean(xf * xf, axis=-1, keepdims=True) + EPS)
        h = ((xf * r) * g_ref[...]).astype(BF16)
        if keep_h:
            outs[1][...] = h
        for k, w_ref in enumerate(w_refs):
            outs[0][:, k * W_BLOCK:(k + 1) * W_BLOCK] = _dot(h, w_ref[...], NT).astype(out_dtype)

    riding = [] if ride is None else [ride]
    return pl.pallas_call(
        body, name=name, grid=(n_steps,),
        in_specs=[pl.BlockSpec((tm, D_MODEL), lambda i: (i, 0)), pl.BlockSpec((1, D_MODEL), lambda i: (0, 0))]
        + _w_blocks(first_block, n_blocks) + [ANY for _ in riding],
        out_specs=[pl.BlockSpec((tm, W_BLOCK * n_blocks), lambda i: (i, 0)),
                   pl.BlockSpec((tm, D_MODEL), lambda i: (i, 0))][:n_out] + [ANY for _ in riding],
        out_shape=[jax.ShapeDtypeStruct((SEQ, W_BLOCK * n_blocks), out_dtype),
                   jax.ShapeDtypeStruct((SEQ, D_MODEL), BF16)][:n_out]
        + [jax.ShapeDtypeStruct((N_CHIPS * r.shape[0], D_MODEL), BF16) for r in riding],
        scratch_shapes=_gather_scratch() if riding else [],
        compiler_params=_params("arbitrary"),
    )(x, gain, *([w_t] * n_blocks), *riding)


CHUNK = 256
CHUNK_UNROLL = 8
TILE_UNROLL = 8


def _low_half():
    return lax.broadcasted_iota(jnp.int32, (1, LANES), 1) < HEAD_DIM


def _half_sum(v, low):
    del low
    row = lax.broadcasted_iota(jnp.int32, (2 * LANES, LANES), 0)
    col = lax.broadcasted_iota(jnp.int32, (2 * LANES, LANES), 1)
    ones = jnp.where((row % LANES) // HEAD_DIM == col // HEAD_DIM, 1.0, 0.0).astype(BF16)
    hi = v.astype(BF16)
    lo = (v - hi.astype(F32)).astype(BF16)
    return _dot(jnp.concatenate([hi, lo], axis=1), ones)


def _chunks(fn, init=0):
    def body(i, carry):
        for u in range(CHUNK_UNROLL):
            carry = fn(pl.multiple_of((i * CHUNK_UNROLL + u) * CHUNK, CHUNK), carry)
        return carry

    return lax.fori_loop(0, SEQ // (CHUNK * CHUNK_UNROLL), body, init)


def _inv_rms(t, low):
    del low
    row = lax.broadcasted_iota(jnp.int32, (LANES, LANES), 0)
    col = lax.broadcasted_iota(jnp.int32, (LANES, LANES), 1)
    ones = jnp.where(row // HEAD_DIM == col // HEAD_DIM, 1.0, 0.0).astype(BF16)
    return lax.rsqrt(_dot((t * t).astype(BF16), ones) * (1.0 / HEAD_DIM) + EPS)


def _prep_q(q_ref, gain_ref, qn_ref):
    low = _low_half()

    def step(r0, carry):
        q = q_ref[pl.ds(r0, CHUNK), :].astype(F32)
        qn_ref[pl.ds(r0, CHUNK), :] = ((q * _inv_rms(q, low)) * gain_ref[...]) * SCALE
        return carry

    _chunks(step)


def _own_half(t, keep):
    return jnp.where(keep, t, pltpu.roll(t, HEAD_DIM, 1))


def _prep_kv(k_ref, v_ref, gain_ref, kp_ref, vp_ref, pad, keep=None):
    low = _low_half()
    zeros = jnp.zeros((pad, LANES), F32)
    for ref in (kp_ref, vp_ref):
        ref[pl.ds(0, pad), :] = zeros
        ref[pl.ds(pad + SEQ, pad), :] = zeros

    def step(r0, carry):
        k = k_ref[pl.ds(r0, CHUNK), :].astype(F32)
        v = v_ref[pl.ds(r0, CHUNK), :].astype(F32)
        kn = (k * _inv_rms(k, low)) * gain_ref[...]
        if keep is not None:
            kn, v = _own_half(kn, keep), _own_half(v, keep)
        kp_ref[pl.ds(pad + r0, CHUNK), :] = kn
        vp_ref[pl.ds(pad + r0, CHUNK), :] = v
        return carry

    _chunks(step)


def _tiles(d, half_window, fn):
    w = Q_BLOCK + 2 * half_window
    length = SEQ // d
    n_blocks = length // Q_BLOCK
    col = lax.broadcasted_iota(jnp.int32, (1, w), 1)

    def step(it, carry):
        c, n = it // n_blocks, it % n_blocks
        start = c + (d * Q_BLOCK) * n
        if d == 1:
            start = pl.multiple_of(start, Q_BLOCK)
            q_rows, k_rows = pl.ds(start, Q_BLOCK), pl.ds(start, w)
        else:
            q_rows, k_rows = pl.ds(start, Q_BLOCK, stride=d), pl.ds(start, w, stride=d)
        t = n * Q_BLOCK - half_window + col
        edge = jnp.where((t < 0) | (t >= length), NEG_INF, 0.0)
        fn(q_rows, k_rows, edge)
        return carry

    lax.fori_loop(0, d * n_blocks, step, 0, unroll=TILE_UNROLL)


def _stack_heads(t, low):
    return jnp.concatenate([jnp.where(low, t, 0.0), jnp.where(low, 0.0, t)], axis=0).astype(BF16)


def _unstack_heads(t, low):
    return jnp.where(low, t[:Q_BLOCK], t[Q_BLOCK:])


def _per_head(pair):
    return jnp.concatenate([jnp.full((Q_BLOCK, 1), pair[0], F32), jnp.full((Q_BLOCK, 1), pair[1], F32)], axis=0)


def _fwd_tiles(qn_ref, kp_ref, vp_ref, bias_ref, emit, *, d, half_window, sinks=None):
    low = _low_half()
    w = Q_BLOCK + 2 * half_window
    sink = None if sinks is None else _per_head(sinks)

    def tile(q_rows, k_rows, edge):
        q2 = _stack_heads(qn_ref[q_rows, :], low)
        k = kp_ref[k_rows, :].astype(BF16)
        v1 = jnp.concatenate([vp_ref[k_rows, :], jnp.ones((w, LANES), F32)], axis=1).astype(BF16)
        s = _dot(q2, k, NT) + bias_ref[...] + edge
        m = jnp.max(s, axis=-1, keepdims=True)
        if sink is not None:
            m = jnp.maximum(m, sink)
        o = _dot(jnp.exp(s - m).astype(BF16), v1)
        l = o[:, LANES:]
        if sink is not None:
            l = l + jnp.exp(sink - m)
        emit(q_rows, _unstack_heads(o[:, :LANES] * (1.0 / l), low), _unstack_heads(m + jnp.log(l), low))

    _tiles(d, half_window, tile)


def _bwd_tiles(qn_ref, kp_ref, vp_ref, bias_ref, do_ref, lse_ref, delta_ref, dq_ref, dk_ref, dv_ref, ds_ref,
               *, d, half_window, sinks=None, dsink_ref=None):
    low = _low_half()
    w = Q_BLOCK + 2 * half_window
    sink = None if sinks is None else _per_head(sinks)

    def rows_of(t):
        return jnp.concatenate([t[:, 0:1], t[:, HEAD_DIM:HEAD_DIM + 1]], axis=0)

    def tile(q_rows, k_rows, edge):
        q2 = _stack_heads(qn_ref[q_rows, :], low)
        do2 = _stack_heads(do_ref[q_rows, :], low)
        k = kp_ref[k_rows, :].astype(BF16)
        v = vp_ref[k_rows, :].astype(BF16)
        lse = rows_of(lse_ref[q_rows, :])
        delta = rows_of(delta_ref[q_rows, :])
        p = jnp.exp(_dot(q2, k, NT) + bias_ref[...] + edge - lse)
        ds = p * (_dot(do2, v, NT) - delta)
        ds_ref[...] += ds
        if sink is not None:
            dsink_ref[...] += (-jnp.exp(sink - lse) * delta).reshape(2, Q_BLOCK, 1)
        dsb, pb = ds.astype(BF16), p.astype(BF16)
        dq_ref[q_rows, :] = _unstack_heads(_dot(dsb, k), low)
        dk_ref[k_rows, :] += _dot(dsb, q2, TN)
        dv_ref[k_rows, :] += _dot(pb, do2, TN)

    _tiles(d, half_window, tile)


def _norm_bwd(raw_ref, gain_ref, dn_ref, dn_offset, out_ref, scale):
    low = _low_half()

    def step(r0, dgain):
        t = raw_ref[pl.ds(r0, CHUNK), :].astype(F32)
        dn = dn_ref[pl.ds(dn_offset + r0, CHUNK), :]
        dth = dn * (gain_ref[...] * scale)
        r = _inv_rms(t, low)
        th = t * r
        out_ref[pl.ds(r0, CHUNK), :] = (r * (dth - th * (r * _half_sum(dth * t, low) * (1.0 / HEAD_DIM)))).astype(BF16)
        return dgain + jnp.sum(dn * th, axis=0, keepdims=True) * scale

    return _chunks(step, jnp.zeros((1, LANES), F32))


def _rows8(v):
    return jnp.broadcast_to(v, (8, v.shape[-1]))


A_W = Q_BLOCK + 2 * A_HALF_WINDOW
A_PAD = A_HALF_WINDOW


def _seq_block(col_fn):
    return pl.BlockSpec((SEQ, LANES), col_fn)


def _attn_a_fwd(qkv, gain_q, gain_k, bias, sink):
    def body(sink_ref, q_ref, k_ref, v_ref, gq_ref, gk_ref, line_ref, o_ref, lse_ref, qn_ref, kp_ref, vp_ref,
             bias_ref):
        hp = pl.program_id(0)
        keep = (lax.broadcasted_iota(jnp.int32, (1, LANES), 1) // HEAD_DIM) == hp // 2
        _prep_q(q_ref, gq_ref, qn_ref)
        _prep_kv(k_ref, v_ref, gk_ref, kp_ref, vp_ref, A_PAD, keep)
        _unroll_bias(line_ref, bias_ref, A_W)

        def emit(rows, out, lse):
            o_ref[rows, :] = out
            lse_ref[rows, :] = lse

        _fwd_tiles(qn_ref, kp_ref, vp_ref, bias_ref, emit, d=1, half_window=A_HALF_WINDOW,
                   sinks=(sink_ref[2 * hp], sink_ref[2 * hp + 1]))

    vec = pl.BlockSpec((1, LANES), lambda hp, s: (0, 0))
    return pl.pallas_call(
        body, name="attn_a_fwd",
        grid_spec=pltpu.PrefetchScalarGridSpec(
            num_scalar_prefetch=1, grid=(4,),
            in_specs=[_seq_block(lambda hp, s: (0, QA_BLK + hp)), _seq_block(lambda hp, s: (0, KA_BLK)),
                      _seq_block(lambda hp, s: (0, VA_BLK)), vec, vec,
                      pl.BlockSpec((None, 2, _line_width(A_HALF_WINDOW)), lambda hp, s: (hp, 0, 0))],
            out_specs=[_seq_block(lambda hp, s: (0, hp)), _seq_block(lambda hp, s: (0, hp))],
            scratch_shapes=[pltpu.VMEM((SEQ, LANES), F32), pltpu.VMEM((SEQ + 2 * A_PAD, LANES), F32),
                            pltpu.VMEM((SEQ + 2 * A_PAD, LANES), F32), pltpu.VMEM((2 * Q_BLOCK, A_W), F32)]),
        out_shape=[jax.ShapeDtypeStruct((SEQ, 512), F32)] * 2,
        compiler_params=_params("arbitrary"),
    )(sink.reshape(8), qkv, qkv, qkv, gain_q, gain_k, bias)


def _attn_a_bwd(qkv, gain_q, gain_k, bias, sink, delta, lse, d_out):
    def body(sink_ref, q_ref, k_ref, v_ref, gq_ref, gk_ref, line_ref, delta_ref, lse_ref, do_ref,
             dq_out, dkv_out, dgq_out, dgk_out, dline_out, dsink_out,
             qn_ref, kp_ref, vp_ref, dq_ref, dk_ref, dv_ref, dk_tot, dv_tot, bias_ref, ds_out):
        hp = pl.program_id(0)
        kv_head = hp // 2
        keep = (lax.broadcasted_iota(jnp.int32, (1, LANES), 1) // HEAD_DIM) == kv_head
        _prep_q(q_ref, gq_ref, qn_ref)
        _prep_kv(k_ref, v_ref, gk_ref, kp_ref, vp_ref, A_PAD, keep)
        _unroll_bias(line_ref, bias_ref, A_W)
        ds_out[...] = jnp.zeros_like(ds_out)
        dsink_out[...] = jnp.zeros_like(dsink_out)

        @pl.when(hp % 2 == 0)
        def _():
            dk_ref[...] = jnp.zeros_like(dk_ref)
            dv_ref[...] = jnp.zeros_like(dv_ref)

        @pl.when(hp == 0)
        def _():
            dk_tot[...] = jnp.zeros_like(dk_tot)
            dv_tot[...] = jnp.zeros_like(dv_tot)

        _bwd_tiles(qn_ref, kp_ref, vp_ref, bias_ref, do_ref, lse_ref, delta_ref, dq_ref, dk_ref, dv_ref, ds_out,
                   d=1, half_window=A_HALF_WINDOW, sinks=(sink_ref[2 * hp], sink_ref[2 * hp + 1]),
                   dsink_ref=dsink_out)
        _fold_bias_grad(ds_out, dline_out, A_W)
        dgq_out[...] = _rows8(_norm_bwd(q_ref, gq_ref, dq_ref, 0, dq_out, SCALE))

        def fold(r0, carry):
            rows = pl.ds(A_PAD + r0, CHUNK)
            for acc, tot in ((dk_ref, dk_tot), (dv_ref, dv_tot)):
                t = acc[rows, :]
                tot[pl.ds(r0, CHUNK), :] += jnp.where(keep, t + pltpu.roll(t, HEAD_DIM, 1), 0.0)
            return carry

        @pl.when(hp % 2 == 1)
        def _():
            _chunks(fold)

        @pl.when(hp == 3)
        def _():
            dgk_out[...] = _rows8(_norm_bwd(k_ref, gk_ref, dk_tot, 0, dkv_out.at[0], 1.0))
            dkv_out[1] = dv_tot[...].astype(BF16)

    vec = pl.BlockSpec((1, LANES), lambda hp, s: (0, 0))
    seq_f32 = pltpu.VMEM((SEQ, LANES), F32)
    padded = pltpu.VMEM((SEQ + 2 * A_PAD, LANES), F32)
    return pl.pallas_call(
        body, name="attn_a_bwd",
        grid_spec=pltpu.PrefetchScalarGridSpec(
            num_scalar_prefetch=1, grid=(4,),
            in_specs=[_seq_block(lambda hp, s: (0, QA_BLK + hp)), _seq_block(lambda hp, s: (0, KA_BLK)),
                      _seq_block(lambda hp, s: (0, VA_BLK)), vec, vec,
                      pl.BlockSpec((None, 2, _line_width(A_HALF_WINDOW)), lambda hp, s: (hp, 0, 0)),
                      _seq_block(lambda hp, s: (0, hp)), _seq_block(lambda hp, s: (0, hp)),
                      _seq_block(lambda hp, s: (0, hp))],
            out_specs=[pl.BlockSpec((None, SEQ, LANES), lambda hp, s: (hp, 0, 0)),
                       pl.BlockSpec((2, SEQ, LANES), lambda hp, s: (0, 0, 0)),
                       pl.BlockSpec((None, 8, LANES), lambda hp, s: (hp, 0, 0)),
                       pl.BlockSpec((8, LANES), lambda hp, s: (0, 0)),
                       pl.BlockSpec((None, 2, _line_width(A_HALF_WINDOW)), lambda hp, s: (hp, 0, 0)),
                       pl.BlockSpec((None, 2, Q_BLOCK, 1), lambda hp, s: (hp, 0, 0, 0))],
            scratch_shapes=[seq_f32, padded, padded, seq_f32, padded, padded, seq_f32, seq_f32,
                            pltpu.VMEM((2 * Q_BLOCK, A_W), F32), pltpu.VMEM((2 * Q_BLOCK, A_W), F32)]),
        out_shape=[jax.ShapeDtypeStruct((4, SEQ, LANES), BF16), jax.ShapeDtypeStruct((2, SEQ, LANES), BF16),
                   jax.ShapeDtypeStruct((4, 8, LANES), F32), jax.ShapeDtypeStruct((8, LANES), F32),
                   jax.ShapeDtypeStruct((4, 2, _line_width(A_HALF_WINDOW)), F32),
                   jax.ShapeDtypeStruct((4, 2, Q_BLOCK, 1), F32)],
        compiler_params=_params("arbitrary"),
    )(sink.reshape(8), qkv, qkv, qkv, gain_q, gain_k, bias, delta, lse, d_out)


B_W = Q_BLOCK + 2 * B_HALF_WINDOW
B_PAD_MAX = B_HALF_WINDOW * B_DILATIONS[-1]


def _attn_b_fwd(qkv, gain_q, gain_k, bias):
    def body(q_ref, k_ref, v_ref, gq_ref, gk_ref, line_ref, o_ref, lse_ref, qn_ref, kp_ref, vp_ref, bias_ref):
        g = pl.program_id(1)
        _prep_q(q_ref, gq_ref, qn_ref)
        _unroll_bias(line_ref, bias_ref, B_W)

        def first(rows, out, lse):
            o_ref[rows, :] = out
            lse_ref[rows, :] = lse

        def combine(rows, out, lse):
            old = lse_ref[rows, :]
            new = jnp.maximum(old, lse) + jnp.log(1.0 + jnp.exp(-jnp.abs(old - lse)))
            o_ref[rows, :] = o_ref[rows, :] * jnp.exp(old - new) + out * jnp.exp(lse - new)
            lse_ref[rows, :] = new

        for gi, d in enumerate(B_DILATIONS):
            @pl.when(g == gi)
            def _():
                _prep_kv(k_ref, v_ref, gk_ref, kp_ref, vp_ref, B_HALF_WINDOW * d)
                _fwd_tiles(qn_ref, kp_ref, vp_ref, bias_ref, first if gi == 0 else combine,
                           d=d, half_window=B_HALF_WINDOW)

    vec = pl.BlockSpec((1, LANES), lambda hp, g: (0, 0))
    padded = pltpu.VMEM((SEQ + 2 * B_PAD_MAX, LANES), F32)
    return pl.pallas_call(
        body, name="attn_b_fwd", grid=(4, 3),
        in_specs=[_seq_block(lambda hp, g: (0, QB_BLK + 4 * g + hp)), _seq_block(lambda hp, g: (0, KB_BLK + 4 * g + hp)),
                  _seq_block(lambda hp, g: (0, VB_BLK + 4 * g + hp)), vec, vec,
                  pl.BlockSpec((None, 2, _line_width(B_HALF_WINDOW)), lambda hp, g: (4 * g + hp, 0, 0))],
        out_specs=[_seq_block(lambda hp, g: (0, hp)), _seq_block(lambda hp, g: (0, hp))],
        out_shape=[jax.ShapeDtypeStruct((SEQ, 512), F32)] * 2,
        scratch_shapes=[pltpu.VMEM((SEQ, LANES), F32), padded, padded, pltpu.VMEM((2 * Q_BLOCK, B_W), F32)],
        compiler_params=_params("arbitrary", "arbitrary"),
    )(qkv, qkv, qkv, gain_q, gain_k, bias)


def _attn_b_bwd(qkv, gain_q, gain_k, bias, delta, lse, d_out):
    def body(q_ref, k_ref, v_ref, gq_ref, gk_ref, line_ref, delta_ref, lse_ref, do_ref,
             dq_out, dk_out, dv_out, dgq_out, dgk_out, dline_out,
             qn_ref, kp_ref, vp_ref, dq_ref, dk_ref, dv_ref, bias_ref, ds_out):
        g = pl.program_id(1)
        _prep_q(q_ref, gq_ref, qn_ref)
        _unroll_bias(line_ref, bias_ref, B_W)
        ds_out[...] = jnp.zeros_like(ds_out)
        for gi, d in enumerate(B_DILATIONS):
            @pl.when(g == gi)
            def _():
                pad = B_HALF_WINDOW * d
                for acc in (dk_ref, dv_ref):
                    acc[pl.ds(0, SEQ + 2 * pad), :] = jnp.zeros((SEQ + 2 * pad, LANES), F32)
                _prep_kv(k_ref, v_ref, gk_ref, kp_ref, vp_ref, pad)
                _bwd_tiles(qn_ref, kp_ref, vp_ref, bias_ref, do_ref, lse_ref, delta_ref, dq_ref, dk_ref, dv_ref,
                           ds_out, d=d, half_window=B_HALF_WINDOW)
                dgk_out[...] = _rows8(_norm_bwd(k_ref, gk_ref, dk_ref, pad, dk_out, 1.0))
                dv_out[...] = dv_ref[pl.ds(pad, SEQ), :].astype(BF16)
        _fold_bias_grad(ds_out, dline_out, B_W)
        dgq_out[...] = _rows8(_norm_bwd(q_ref, gq_ref, dq_ref, 0, dq_out, SCALE))

    vec = pl.BlockSpec((1, LANES), lambda hp, g: (0, 0))
    seq_f32 = pltpu.VMEM((SEQ, LANES), F32)
    padded = pltpu.VMEM((SEQ + 2 * B_PAD_MAX, LANES), F32)
    part = pl.BlockSpec((None, 8, LANES), lambda hp, g: (4 * g + hp, 0, 0))
    line = pl.BlockSpec((None, 2, _line_width(B_HALF_WINDOW)), lambda hp, g: (4 * g + hp, 0, 0))
    return pl.pallas_call(
        body, name="attn_b_bwd", grid=(4, 3),
        in_specs=[_seq_block(lambda hp, g: (0, QB_BLK + 4 * g + hp)), _seq_block(lambda hp, g: (0, KB_BLK + 4 * g + hp)),
                  _seq_block(lambda hp, g: (0, VB_BLK + 4 * g + hp)), vec, vec,
                  line,
                  _seq_block(lambda hp, g: (0, hp)), _seq_block(lambda hp, g: (0, hp)), _seq_block(lambda hp, g: (0, hp))],
        out_specs=[pl.BlockSpec((None, SEQ, LANES), lambda hp, g: (4 * g + hp, 0, 0))] * 3 + [part, part, line],
        out_shape=[jax.ShapeDtypeStruct((12, SEQ, LANES), BF16)] * 3
        + [jax.ShapeDtypeStruct((12, 8, LANES), F32)] * 2
        + [jax.ShapeDtypeStruct((12, 2, _line_width(B_HALF_WINDOW)), F32)],
        scratch_shapes=[seq_f32, padded, padded, seq_f32, padded, padded,
                        pltpu.VMEM((2 * Q_BLOCK, B_W), F32), pltpu.VMEM((2 * Q_BLOCK, B_W), F32)],
        compiler_params=_params("arbitrary", "arbitrary"),
    )(qkv, qkv, qkv, gain_q, gain_k, bias, delta, lse, d_out)


def _sigmoid(t):
    return 1.0 / (1.0 + jnp.exp(-t))


def _middle(out_a, out_b, gates, x, target, w_a, w_b, w_out, b_merge):
    tm = 256
    n_steps = SEQ // tm

    def body(oa_ref, ob_ref, g_ref, x_ref, t_ref, wa_ref, wb_ref, wo_ref, bm_ref,
             dy_ref, dg_ref, doa_ref, dob_ref, dla_ref, dlb_ref, dwa_ref, dwb_ref, dwo_ref, dbm_ref, sq_ref):
        @pl.when(pl.program_id(0) == 0)
        def _():
            for ref in (dwa_ref, dwb_ref, dwo_ref, dbm_ref, sq_ref):
                ref[...] = jnp.zeros_like(ref)

        gate_a, gate_b = g_ref[:, 0:512], g_ref[:, 512:1024]
        sig_a, sig_b = _sigmoid(gate_a), _sigmoid(gate_b)
        silu_a, silu_b = gate_a * sig_a, gate_b * sig_b
        oa, ob = oa_ref[...], ob_ref[...]
        ya, yb = (oa * silu_a).astype(BF16), (ob * silu_b).astype(BF16)
        br_a, br_b = _dot(ya, wa_ref[...]), _dot(yb, wb_ref[...])
        m0 = _sigmoid(g_ref[:, 1024:2048] + bm_ref[0:1, :])
        m1 = _sigmoid(g_ref[:, 2048:3072] + bm_ref[1:2, :])
        merged = (m0 * br_a + m1 * br_b).astype(BF16)
        err = (x_ref[...] + _dot(merged, wo_ref[...])) - t_ref[...]
        sq_ref[...] += jnp.sum(err * err, axis=0, keepdims=True)

        dy = err * (1.0 / D_MODEL)
        dy_ref[...] = dy
        dyb = dy.astype(BF16)
        dmerged = _dot(dyb, wo_ref[...], NT)
        dwo_ref[...] += _dot(merged, dyb, TN)
        dbr_a, dbr_b = (dmerged * m0).astype(BF16), (dmerged * m1).astype(BF16)
        dm0 = (dmerged * br_a) * (m0 * (1.0 - m0))
        dm1 = (dmerged * br_b) * (m1 * (1.0 - m1))
        dbm_ref[0:1, :] += jnp.sum(dm0, axis=0, keepdims=True)
        dbm_ref[1:2, :] += jnp.sum(dm1, axis=0, keepdims=True)
        for s in range(N_CHIPS):
            cols = slice(256 * s, 256 * (s + 1))
            dwa_ref[s] += _dot(ya, dbr_a[:, cols], TN)
            dwb_ref[s] += _dot(yb, dbr_b[:, cols], TN)
        dya, dyb_ = _dot(dbr_a, wa_ref[...], NT), _dot(dbr_b, wb_ref[...], NT)
        doa, dob = dya * silu_a, dyb_ * silu_b
        doa_ref[...] = doa
        dob_ref[...] = dob
        for blk in range(512 // LANES):
            lanes = slice(blk * LANES, (blk + 1) * LANES)
            dla_ref[:, lanes] = _half_sum(doa[:, lanes] * oa[:, lanes], None)
            dlb_ref[:, lanes] = _half_sum(dob[:, lanes] * ob[:, lanes], None)
        d_gates = (((dya * oa) * (sig_a * (1.0 + gate_a * (1.0 - sig_a)))).astype(BF16),
                   ((dyb_ * ob) * (sig_b * (1.0 + gate_b * (1.0 - sig_b)))).astype(BF16),
                   dm0.astype(BF16), dm1.astype(BF16))
        blk = 0
        for part in d_gates:
            for c0 in range(0, part.shape[1], 256):
                dg_ref[blk] = part[:, c0:c0 + 256]
                blk += 1

    def rows(width):
        return pl.BlockSpec((tm, width), lambda i: (i, 0))

    def whole(*shape):
        return pl.BlockSpec(shape, lambda i: (0,) * len(shape))

    return pl.pallas_call(
        body, name="middle", grid=(n_steps,),
        in_specs=[rows(512), rows(512), rows(GATE_WIDTH), rows(D_MODEL), rows(D_MODEL),
                  whole(512, D_MODEL), whole(512, D_MODEL), whole(D_MODEL, D_MODEL), whole(2, D_MODEL)],
        out_specs=[rows(D_MODEL), pl.BlockSpec((GATE_WIDTH // 256, tm, 256), lambda i: (0, i, 0)),
                   rows(512), rows(512), rows(512), rows(512),
                   whole(N_CHIPS, 512, 256), whole(N_CHIPS, 512, 256), whole(D_MODEL, D_MODEL),
                   whole(2, D_MODEL), whole(1, D_MODEL)],
        out_shape=[jax.ShapeDtypeStruct((SEQ, D_MODEL), F32), jax.ShapeDtypeStruct((GATE_WIDTH // 256, SEQ, 256), BF16),
                   jax.ShapeDtypeStruct((SEQ, 512), F32), jax.ShapeDtypeStruct((SEQ, 512), F32),
                   jax.ShapeDtypeStruct((SEQ, 512), F32), jax.ShapeDtypeStruct((SEQ, 512), F32),
                   jax.ShapeDtypeStruct((N_CHIPS, 512, 256), F32), jax.ShapeDtypeStruct((N_CHIPS, 512, 256), F32),
                   jax.ShapeDtypeStruct((D_MODEL, D_MODEL), F32), jax.ShapeDtypeStruct((2, D_MODEL), F32),
                   jax.ShapeDtypeStruct((1, D_MODEL), F32)],
        compiler_params=_params("arbitrary"),
    )(out_a, out_b, gates, x, target, w_a, w_b, w_out, b_merge)


def _which(j, edges, fns):
    lo = 0
    for hi, fn in zip(edges, fns):
        pl.when((j >= lo) & (j < hi))(fn)
        lo = hi


def _sibling_rows(tile, core):
    lo, hi = tile * W_BLOCK, (tile + 1) * W_BLOCK
    for chip in range(N_CHIPS):
        a = chip * W_IN_SHARD + (1 - core) * (W_IN_SHARD // 2)
        first, last = max(lo, a), min(hi, a + W_IN_SHARD // 2)
        if first < last:
            return chip, first - a, first - lo, last - first
    return None


def _d_w_in(d_proj, h, rest=None):
    plan, step, width = [], 0, 0
    for p in d_proj:
        total = p.shape[0] * p.shape[2]
        if width + total <= W_BLOCK:
            plan.append((p.shape[0], step, 1))
            width += total
            if width == W_BLOCK:
                step, width = step + 1, 0
        else:
            assert width == 0 and total % W_BLOCK == 0
            plan.append((W_BLOCK // p.shape[2], step, total // W_BLOCK))
            step += total // W_BLOCK
    assert width == 0 and step == IN_WIDTH // W_BLOCK
    firsts = sorted({first for _, first, _ in plan})
    edges = firsts[1:] + [step]
    halves = 2

    hand_over = rest is not None
    half = W_IN_SHARD // 2

    def body(*refs):
        if hand_over:
            pieces, h_ref, rest_ref = refs[:len(d_proj)], refs[len(d_proj)], refs[len(d_proj) + 1]
            o_ref, got_ref, got_rest_ref, acc_ref, send_sems, recv_sems, stage = refs[len(d_proj) + 2:]
        else:
            pieces, h_ref, o_ref, acc_ref = refs[:-3], refs[-3], refs[-2], refs[-1]
        k = pl.program_id(1)

        def emit(group):
            def fn():
                cols = jnp.concatenate([ref[b] for ref in group for b in range(ref.shape[0])], axis=1)
                term = _dot(cols, h_ref[...], TN)

                @pl.when(k == 0)
                def _():
                    acc_ref[...] = term

                @pl.when(k == halves - 1)
                def _():
                    o_ref[...] = (acc_ref[...] + term).astype(BF16)
            return fn

        groups = [[ref for ref, (_, first, _) in zip(pieces, plan) if first == f] for f in firsts]
        _which(pl.program_id(0), edges, [emit(group) for group in groups])

        if hand_over:
            cx, cy, c = lax.axis_index("x"), lax.axis_index("y"), lax.axis_index("c")
            sibling = (cx, cy, 1 - c)

            def to_sibling(sem, src, dst, recv=0):
                return pltpu.make_async_remote_copy(src_ref=src, dst_ref=dst, send_sem=send_sems.at[sem],
                                                    recv_sem=recv_sems.at[recv], device_id=sibling, device_id_type=MESH)

            def tile_copy(tile, core):
                chip, row, start, rows = _sibling_rows(tile, core)
                return to_sibling(tile % 2, stage.at[tile % 2, pl.ds(0, rows), :], got_ref.at[chip, pl.ds(row, rows), :])

            rest_copy = to_sibling(2, _half_rows(rest_ref, 1 - c), got_rest_ref, recv=1)

            @pl.when((pl.program_id(0) == 0) & (k == 0))
            def _():
                rest_copy.start()

            for tile in range(step):
                for core in range(2):
                    @pl.when((pl.program_id(0) == tile) & (k == halves - 1) & (c == core))
                    def _(tile=tile, core=core):
                        if tile >= 2 and _sibling_rows(tile - 2, core):
                            tile_copy(tile - 2, core).wait_send()
                        if _sibling_rows(tile, core):
                            _, _, start, rows = _sibling_rows(tile, core)
                            stage[tile % 2, 0:rows, :] = o_ref[start:start + rows, :]
                            tile_copy(tile, core).start()
                        if tile == step - 1:
                            for last in (step - 2, step - 1):
                                if _sibling_rows(last, core):
                                    tile_copy(last, core).wait_send()
                            rest_copy.wait()
                            to_sibling(0, got_ref, got_ref).wait_recv()

    def cols_spec(piece, n, first, steps):
        def index(j, k):
            return jnp.clip(j - first, 0, steps - 1), jnp.where((j >= first) & (j < first + steps), k, 0), 0
        return pl.BlockSpec((n, SEQ // halves, piece.shape[2]), index)

    tile_spec = pl.BlockSpec((W_BLOCK, D_MODEL), lambda j, k: (j, 0))
    in_specs = [cols_spec(p, *pl_) for p, pl_ in zip(d_proj, plan)] + [
        pl.BlockSpec((SEQ // halves, D_MODEL), lambda j, k: (k, 0))]
    acc = pltpu.VMEM((W_BLOCK, D_MODEL), F32)
    if not hand_over:
        return pl.pallas_call(
            body, name="d_w_in", grid=(step, halves), in_specs=in_specs, out_specs=tile_spec,
            out_shape=jax.ShapeDtypeStruct((IN_WIDTH, D_MODEL), BF16), scratch_shapes=[acc],
            compiler_params=_params("arbitrary", "arbitrary"),
        )(*d_proj, h)
    return pl.pallas_call(
        body, name="d_w_in", grid=(step, halves), in_specs=in_specs + [ANY], out_specs=[tile_spec, ANY, ANY],
        out_shape=[jax.ShapeDtypeStruct((IN_WIDTH, D_MODEL), BF16),
                   jax.ShapeDtypeStruct((N_CHIPS, half, D_MODEL), BF16),
                   jax.ShapeDtypeStruct((N_CHIPS, rest.shape[1] // 2, D_MODEL), BF16)],
        scratch_shapes=[acc, pltpu.SemaphoreType.DMA((3,)), pltpu.SemaphoreType.DMA((2,)),
                        pltpu.VMEM((2, W_BLOCK, D_MODEL), BF16)],
        compiler_params=_params("arbitrary", "arbitrary"),
    )(*d_proj, h, rest)


RELAY_STEP = 10
RELAY_ROWS = 352


def _d_x(d_proj, w_t, x, gain, dy, chip_sums):
    tm = 256
    n_steps = SEQ // tm
    n_w = IN_WIDTH // W_BLOCK
    n_p, n_s = len(d_proj), len(chip_sums)

    def body(*refs):
        pieces, w_refs = refs[:n_p], refs[n_p:n_p + n_w]
        x_ref, g_ref, dy_ref = refs[n_p + n_w:n_p + n_w + 3]
        q_refs = refs[n_p + n_w + 3:n_p + n_w + 3 + n_s]
        dx_ref, dgain_ref = refs[n_p + n_w + 3 + n_s:n_p + n_w + 5 + n_s]
        outs = refs[n_p + n_w + 5 + n_s:n_p + n_w + 5 + 4 * n_s]
        got_refs, relay_refs, sum_refs = outs[:n_s], outs[n_s:2 * n_s], outs[2 * n_s:]
        if n_s:
            send_sems, recv_sems, local_sems, a_buf, b_buf, c_buf = refs[n_p + n_w + 5 + 4 * n_s:]

        def hops():
            cx, cy, c = lax.axis_index("x"), lax.axis_index("y"), lax.axis_index("c")
            near = (cx + (1 - c) - 2 * cx * (1 - c), cy + c - 2 * cy * c)
            far = (cx + c - 2 * cx * c, cy + (1 - c) - 2 * cy * (1 - c))
            chip = lambda p: 2 * p[0] + p[1]

            def copy(k, src, dst, to):
                return pltpu.make_async_remote_copy(src_ref=src, dst_ref=dst, send_sem=send_sems.at[k],
                                                    recv_sem=recv_sems.at[k], device_id=(*to, c), device_id_type=MESH)

            first = [(copy(3 * b, q.at[chip(near)], got.at[0], near),
                      copy(3 * b + 1, q.at[3 - chip((cx, cy))], relay, near))
                     for b, (q, got, relay) in enumerate(zip(q_refs, got_refs, relay_refs))]
            second = [copy(3 * b + 2, s, got.at[1], far) for b, (s, got) in enumerate(zip(sum_refs, got_refs))]
            return first, second, chip(far)

        @pl.when(pl.program_id(0) == 0)
        def _():
            dgain_ref[...] = jnp.zeros_like(dgain_ref)
            if n_s:
                for direct, pass_on in hops()[0]:
                    direct.start()
                    pass_on.start()

        if n_s:
            @pl.when(pl.program_id(0) == RELAY_STEP)
            def _():
                first, second, far_chip = hops()
                for b, (q, relay, total) in enumerate(zip(q_refs, relay_refs, sum_refs)):
                    first[b][1].wait_recv()
                    half = relay.shape[0]
                    for r0 in range(0, half, RELAY_ROWS):
                        rows = min(RELAY_ROWS, half - r0)
                        mine = pltpu.make_async_copy(q.at[far_chip, pl.ds(r0, rows), :], a_buf.at[pl.ds(0, rows), :],
                                                     local_sems.at[0])
                        theirs = pltpu.make_async_copy(relay.at[pl.ds(r0, rows), :], b_buf.at[pl.ds(0, rows), :],
                                                       local_sems.at[1])
                        mine.start()
                        theirs.start()
                        mine.wait()
                        theirs.wait()
                        c_buf[0:rows, :] = (a_buf[0:rows, :].astype(F32) + b_buf[0:rows, :].astype(F32)).astype(BF16)
                        store = pltpu.make_async_copy(c_buf.at[pl.ds(0, rows), :], total.at[pl.ds(r0, rows), :],
                                                      local_sems.at[2])
                        store.start()
                        store.wait()
                    second[b].start()

        blocks = [(piece, k) for piece in pieces for k in range(piece.shape[0])]
        dh, group, width, blk = None, [], 0, 0
        for piece, k in blocks:
            group.append(piece[k])
            width += piece.shape[2]
            if width == W_BLOCK:
                term = _dot(jnp.concatenate(group, axis=1), w_refs[blk][...])
                dh = term if dh is None else dh + term
                group, width, blk = [], 0, blk + 1
        assert not group and blk == n_w
        xf = x_ref[...]
        r = lax.rsqrt(jnp.mean(xf * xf, axis=-1, keepdims=True) + EPS)
        xh = xf * r
        dxh = dh * g_ref[...]
        dx_ref[...] = r * (dxh - xh * jnp.mean(dxh * xh, axis=-1, keepdims=True)) + dy_ref[...]
        dgain_ref[...] += _rows8(jnp.sum(dh * xh, axis=0, keepdims=True))

        if n_s:
            @pl.when(pl.program_id(0) == n_steps - 1)
            def _():
                first, second, _ = hops()
                for direct, pass_on in first:
                    direct.wait()
                    pass_on.wait_send()
                for cp in second:
                    cp.wait()

    row = pl.BlockSpec((tm, D_MODEL), lambda i: (i, 0))
    halves = [q.shape[1] for q in chip_sums]
    res = pl.pallas_call(
        body, name="d_x", grid=(n_steps,),
        in_specs=[pl.BlockSpec((p.shape[0], tm, p.shape[2]), lambda i: (0, i, 0)) for p in d_proj] + _w_blocks(0, n_w)
        + [row, pl.BlockSpec((1, D_MODEL), lambda i: (0, 0)), row] + [ANY] * n_s,
        out_specs=[row, pl.BlockSpec((8, D_MODEL), lambda i: (0, 0))] + [ANY] * (3 * n_s),
        out_shape=[jax.ShapeDtypeStruct((SEQ, D_MODEL), F32), jax.ShapeDtypeStruct((8, D_MODEL), F32)]
        + [jax.ShapeDtypeStruct((2, half, D_MODEL), BF16) for half in halves]
        + [jax.ShapeDtypeStruct((half, D_MODEL), BF16) for half in halves] * 2,
        scratch_shapes=[pltpu.SemaphoreType.DMA((3 * n_s,)), pltpu.SemaphoreType.DMA((3 * n_s,)),
                        pltpu.SemaphoreType.DMA((3,))] + [pltpu.VMEM((RELAY_ROWS, D_MODEL), BF16)] * 3 if n_s else [],
        compiler_params=_params("arbitrary"),
    )(*d_proj, *([w_t] * n_w), x, gain, dy, *chip_sums)
    return res[0], res[1], res[2:2 + n_s]


def _my_place():
    x, y, c = lax.axis_index("x"), lax.axis_index("y"), lax.axis_index("c")
    return jnp.stack([2 * x + y, c]).astype(jnp.int32)


def _half_rows(ref, half):
    rows = ref.shape[-2] // 2
    idx = (slice(None),) * (len(ref.shape) - 2) + (pl.ds(pl.multiple_of(half * rows, 16), rows), slice(None))
    return ref.at[idx]


def _add_halves(place, grads, theirs, name):
    half = theirs.shape[1]
    tr = _row_tile(half)
    n = half // tr

    def body(place_ref, g_ref, t_ref, o_ref):
        o_ref[...] = (g_ref[...].astype(F32) + t_ref[...].astype(F32)).astype(BF16)

    return pl.pallas_call(
        body, name=name,
        grid_spec=pltpu.PrefetchScalarGridSpec(
            num_scalar_prefetch=1, grid=(N_CHIPS, n),
            in_specs=[pl.BlockSpec((None, tr, D_MODEL), lambda s, i, p: (s, p[1] * n + i, 0)),
                      pl.BlockSpec((None, tr, D_MODEL), lambda s, i, p: (s, i, 0))],
            out_specs=pl.BlockSpec((None, tr, D_MODEL), lambda s, i, p: (s, i, 0))),
        out_shape=jax.ShapeDtypeStruct((N_CHIPS, half, D_MODEL), BF16),
        compiler_params=_params("arbitrary", "arbitrary"),
    )(place, grads, theirs)


def _add_chips(place, chip_sums, others, name):
    half = others.shape[1]
    tr = _row_tile(half)
    n = half // tr

    def body(place_ref, q_ref, o_ref, r_ref):
        acc = q_ref[...].astype(F32)
        for j in range(others.shape[0]):
            acc = acc + o_ref[j].astype(F32)
        r_ref[...] = acc

    return pl.pallas_call(
        body, name=name,
        grid_spec=pltpu.PrefetchScalarGridSpec(
            num_scalar_prefetch=1, grid=(n,),
            in_specs=[pl.BlockSpec((None, tr, D_MODEL), lambda i, p: (p[0], i, 0)),
                      pl.BlockSpec((others.shape[0], tr, D_MODEL), lambda i, p: (0, i, 0))],
            out_specs=pl.BlockSpec((tr, D_MODEL), lambda i, p: (p[1] * n + i, 0))),
        out_shape=jax.ShapeDtypeStruct((2 * half, D_MODEL), F32),
        compiler_params=_params("arbitrary"),
    )(place, chip_sums, others)


def _join_halves(shards, block):
    n = len(shards)
    rows = block.shape[0]

    def body(*refs):
        b_ref, o_refs, sum_ref = refs[n], refs[n + 1:2 * n + 1], refs[2 * n + 1]
        send_sems, recv_sems, small_send, small_recv, local_sem, all_ref = refs[2 * n + 2:]
        x, y, c = lax.axis_index("x"), lax.axis_index("y"), lax.axis_index("c")
        me, sibling = (x, y, c), (x, y, 1 - c)
        chips = [(1 - x, y), (x, 1 - y), (1 - x, 1 - y)]

        def half(k, rows_ref):
            return pltpu.make_async_remote_copy(src_ref=rows_ref, dst_ref=rows_ref, send_sem=send_sems.at[k],
                                                recv_sem=recv_sems.at[k], device_id=sibling, device_id_type=MESH)

        def at(px, py, pc):
            return all_ref.at[pl.ds(pl.multiple_of((4 * px + 2 * py + pc) * rows, 8), rows), :]

        def small(k, block_of, to, src=None):
            return pltpu.make_async_remote_copy(src_ref=at(*block_of) if src is None else src, dst_ref=at(*block_of),
                                                send_sem=small_send.at[k], recv_sem=small_recv.at[k],
                                                device_id=to, device_id_type=MESH)

        sends = [half(k, _half_rows(o, c)) for k, o in enumerate(o_refs)]
        for cp in sends:
            cp.start()
        mine = pltpu.make_async_copy(b_ref, at(*me), local_sem)
        mine.start()
        first = [small(0, me, sibling, src=b_ref)]
        first += [small(1 + j, me, (*chip, c), src=b_ref) for j, chip in enumerate(chips)]
        for cp in first:
            cp.start()
        passed = [small(4 + j, (*chip, c), sibling) for j, chip in enumerate(chips)]
        for j, chip in enumerate(chips):
            small(1 + j, (*chip, c), me).wait_recv()
            passed[j].start()
        small(0, sibling, me).wait_recv()
        for j, chip in enumerate(chips):
            small(4 + j, (*chip, 1 - c), me).wait_recv()
        mine.wait()
        acc = all_ref[0:rows, :]
        for dev in range(1, 8):
            acc = acc + all_ref[rows * dev:rows * (dev + 1), :]
        sum_ref[...] = acc
        for k, o in enumerate(o_refs):
            half(k, _half_rows(o, 1 - c)).wait_recv()
        for cp in sends + first + passed:
            cp.wait_send()

    res = pl.pallas_call(
        body, name="reduce_join_halves", in_specs=[ANY] * n + [pl.BlockSpec(memory_space=pltpu.VMEM)],
        out_specs=[ANY] * n + [pl.BlockSpec(memory_space=pltpu.VMEM)],
        out_shape=[jax.ShapeDtypeStruct(s.shape, F32) for s in shards] + [jax.ShapeDtypeStruct(block.shape, F32)],
        input_output_aliases={k: k for k in range(n)},
        scratch_shapes=[pltpu.SemaphoreType.DMA((n,)), pltpu.SemaphoreType.DMA((n,)),
                        pltpu.SemaphoreType.DMA((7,)), pltpu.SemaphoreType.DMA((7,)), pltpu.SemaphoreType.DMA,
                        pltpu.VMEM((8 * rows, D_MODEL), F32)],
    )(*shards, block)
    return res[:n], res[n]


def _adamw_math(w, g, m, v):
    m = ADAM_B1 * m + (1.0 - ADAM_B1) * g
    v = ADAM_B2 * v + (1.0 - ADAM_B2) * (g * g)
    m_hat = m / (1.0 - ADAM_B1 ** ADAM_STEP)
    v_hat = v / (1.0 - ADAM_B2 ** ADAM_STEP)
    return -ADAM_LR * (m_hat / (jnp.sqrt(v_hat) + ADAM_EPS) + ADAM_WD * w), m, v


def _adamw(w, g, m, v, name):
    r, c = w.shape
    tr = _row_tile(r)

    def body(w_ref, g_ref, m_ref, v_ref, d_ref, nm_ref, nv_ref):
        d_ref[...], nm_ref[...], nv_ref[...] = _adamw_math(w_ref[...], g_ref[...], m_ref[...], v_ref[...])

    spec = pl.BlockSpec((tr, c), lambda i: (i, 0))
    return pl.pallas_call(
        body, name=name, grid=(r // tr,), in_specs=[spec] * 4, out_specs=[spec] * 3,
        out_shape=[jax.ShapeDtypeStruct((r, c), F32)] * 3, compiler_params=_params("arbitrary"),
    )(w, g, m, v)


def _adamw_small(ws, gs, ms, vs):
    n = len(ws)

    def body(*refs):
        ins, outs = refs[:4 * n], refs[4 * n:]
        for k in range(n):
            d, m, v = _adamw_math(ins[k][...], ins[n + k][...], ins[2 * n + k][...], ins[3 * n + k][...])
            outs[k][...], outs[n + k][...], outs[2 * n + k][...] = d, m, v

    shapes = [jax.ShapeDtypeStruct(w.shape, F32) for w in ws]
    res = pl.pallas_call(body, name="adamw_small", out_shape=shapes * 3, compiler_params=_params())(*ws, *gs, *ms, *vs)
    return res[:n], res[n:2 * n], res[2 * n:]


def _fold_heads(partials):
    t = jnp.sum(partials[:, 0, :], axis=0)
    return (t[:HEAD_DIM] + t[HEAD_DIM:]).reshape(1, HEAD_DIM)


def _local_step(x, target, norm_gain, w_t, w_a, w_b, w_o, b_m, q_norm_a, k_norm_a, q_norm_b, k_norm_b, sink_a,
                rel_bias, start_reduce=None, small_shard=None):
    two = lambda gain: jnp.concatenate([gain, gain], axis=1)
    bias_a = _bias_lines(rel_bias[:, :8], A_HALF_WINDOW, 1)
    bias_b = jnp.concatenate([_bias_lines(rel_bias[:, 8 + 8 * g:16 + 8 * g], B_HALF_WINDOW, d)
                              for g, d in enumerate(B_DILATIONS)], axis=0)

    qkv, h, *small_all = _in_proj(x, norm_gain, w_t, 0, QKV_WIDTH // W_BLOCK, BF16, "in_proj_qkv", True, small_shard)
    if small_shard is not None:
        w_a, w_b, w_o, b_m = _unpack_weights(small_all[0])
    gates, = _in_proj(x, norm_gain, w_t, QKV_WIDTH // W_BLOCK, GATE_WIDTH // W_BLOCK, F32, "in_proj_gates", False)
    out_a, lse_a = _attn_a_fwd(qkv, two(q_norm_a), two(k_norm_a), bias_a, sink_a)
    out_b, lse_b = _attn_b_fwd(qkv, two(q_norm_b), two(k_norm_b), bias_b)

    dy, dgates, d_out_a, d_out_b, delta_a, delta_b, d_wa, d_wb, d_wo, d_bm, sq = _middle(
        out_a, out_b, gates, x, target, w_a, w_b, w_o, b_m)
    loss = (0.5 / D_MODEL) * jnp.sum(sq)

    dq_a, dkv_a, dgq_a, dgk_a, ds_a, dsink = _attn_a_bwd(
        qkv, two(q_norm_a), two(k_norm_a), bias_a, sink_a, delta_a, lse_a, d_out_a)
    dq_b, dk_b, dv_b, dgq_b, dgk_b, ds_b = _attn_b_bwd(
        qkv, two(q_norm_b), two(k_norm_b), bias_b, delta_b, lse_b, d_out_b)
    d_proj = (dq_a, dkv_a, dq_b, dk_b, dv_b, dgates)

    d_bm_rows = jnp.pad(d_bm.reshape(2, N_CHIPS, 256).transpose(1, 0, 2),
                        ((0, 0), (0, REST_ROWS - 514), (0, D_MODEL - 256)))
    rest = jnp.concatenate([d_wo.reshape(N_CHIPS, 256, D_MODEL), d_wa.reshape(N_CHIPS, 128, D_MODEL),
                            d_wb.reshape(N_CHIPS, 128, D_MODEL), d_bm_rows], axis=1)
    if start_reduce is None:
        grads, chip_sums = [_d_w_in(d_proj, h).reshape(N_CHIPS, W_IN_SHARD, D_MODEL), rest], []
    else:
        d_wt, *theirs = _d_w_in(d_proj, h, rest.astype(BF16))
        grads = [d_wt.reshape(N_CHIPS, W_IN_SHARD, D_MODEL), rest]
        chip_sums = start_reduce(grads, theirs)
    grad_x, d_gain, others = _d_x(d_proj, w_t, x, norm_gain, dy, chip_sums)

    d_rel = jnp.concatenate(
        [_bias_grad(ds_a, A_HALF_WINDOW, 1)]
        + [_bias_grad(ds_b[4 * g:4 * g + 4], B_HALF_WINDOW, d) for g, d in enumerate(B_DILATIONS)], axis=1)
    d_sink = jnp.sum(dsink, axis=(2, 3)).reshape(1, 8)
    dgk_a_row = dgk_a[0]
    small = jnp.zeros((8, D_MODEL), F32)
    small = small.at[0].set(d_gain[0])
    small = small.at[1].set(d_rel.reshape(-1))
    misc = jnp.concatenate([_fold_heads(dgq_a), (dgk_a_row[:HEAD_DIM] + dgk_a_row[HEAD_DIM:]).reshape(1, HEAD_DIM),
                            _fold_heads(dgq_b), _fold_heads(dgk_b), d_sink], axis=1)
    small = small.at[2, :264].set(misc[0])

    return loss, grad_x, grads, small, chip_sums, others


def _unpack_weights(small_all):
    sm = small_all.reshape(N_CHIPS, SMALL_ROWS, D_MODEL)
    w_o = sm[:, 0:256].reshape(D_MODEL, D_MODEL)
    w_a = sm[:, 256:384].reshape(N_CHIPS, 512, 256).transpose(1, 0, 2).reshape(512, D_MODEL)
    w_b = sm[:, 384:512].reshape(N_CHIPS, 512, 256).transpose(1, 0, 2).reshape(512, D_MODEL)
    b_m = lax.bitcast_convert_type(sm[:, 512].reshape(N_CHIPS, 2, 256, 2), F32)
    return w_a, w_b, w_o, b_m.transpose(1, 0, 2).reshape(2, D_MODEL)


def _pack_small_weights(w_branch_a, w_branch_b, b_merge, w_out):
    b_m = jnp.pad(lax.bitcast_convert_type(b_merge, BF16).reshape(1, D_MODEL), ((0, SMALL_ROWS - 513), (0, 0)))
    return jnp.concatenate([w_out.astype(BF16), w_branch_a.astype(BF16).reshape(128, D_MODEL),
                            w_branch_b.astype(BF16).reshape(128, D_MODEL), b_m], axis=0)


def kernel(x, norm_gain, w_in, q_norm_a, k_norm_a, q_norm_b, k_norm_b, sink_a, rel_bias, w_branch_a, w_branch_b, b_merge, w_out, loss_target, m_norm_gain, m_w_in, m_q_norm_a, m_k_norm_a, m_q_norm_b, m_k_norm_b, m_sink_a, m_rel_bias, m_w_branch_a, m_w_branch_b, m_b_merge, m_w_out, v_norm_gain, v_w_in, v_q_norm_a, v_k_norm_a, v_q_norm_b, v_k_norm_b, v_sink_a, v_rel_bias, v_w_branch_a, v_w_branch_b, v_b_merge, v_w_out):
    w_in_t, m_w_in_t, v_w_in_t = (jnp.transpose(t[0]) for t in (w_in, m_w_in, v_w_in))
    wt_shard = _cast_rows(w_in_t, BF16, "w_in_cast")
    w_t = _gather_weights(wt_shard)
    small_shard = _pack_small_weights(w_branch_a[0], w_branch_b[0], b_merge[0], w_out[0])

    place = _my_place()
    names = ("w_in", "rest")

    def start_reduce(grads, theirs):
        return [_add_halves(place, g, t, "reduce_add_halves_" + n) for g, t, n in zip(grads, theirs, names)]

    loss_part, grad_x, _, small, chip_sums, others = _local_step(
        x[0], loss_target[0], norm_gain, w_t, None, None, None, None, q_norm_a, k_norm_a, q_norm_b, k_norm_b,
        sink_a, rel_bias, start_reduce, small_shard)

    (g_wt, g_rest), small = _join_halves(
        [_add_chips(place, q, o, "reduce_add_chips_" + n) for q, o, n in zip(chip_sums, others, names)],
        small.at[3, 0].set(loss_part))
    loss = small[3, 0]

    g_w_out = g_rest[0:256]
    g_w_a = g_rest[256:384].reshape(512, 256)
    g_w_b = g_rest[384:512].reshape(512, 256)
    g_b_merge = g_rest[512:514, :256]
    g_norm_gain = small[0:1]
    g_rel_bias = small[1].reshape(N_BUCKETS, N_BUCKETS)
    g_q_a, g_k_a, g_q_b, g_k_b = (small[2:3, 64 * k:64 * k + 64] for k in range(4))
    g_sink = small[2:3, 256:264]

    upd = {"w_in": tuple(jnp.transpose(t) for t in
                         (g_wt,) + tuple(_adamw(w_in_t, g_wt, m_w_in_t, v_w_in_t, "adamw_w_in")))}
    small_names = ("norm_gain", "q_norm_a", "k_norm_a", "q_norm_b", "k_norm_b", "sink_a", "rel_bias", "b_merge",
                   "w_branch_a", "w_branch_b", "w_out")
    ws = [norm_gain, q_norm_a, k_norm_a, q_norm_b, k_norm_b, sink_a, rel_bias, b_merge[0],
          w_branch_a[0], w_branch_b[0], w_out[0]]
    gs = [g_norm_gain, g_q_a, g_k_a, g_q_b, g_k_b, g_sink, g_rel_bias, g_b_merge, g_w_a, g_w_b, g_w_out]
    ms = [m_norm_gain, m_q_norm_a, m_k_norm_a, m_q_norm_b, m_k_norm_b, m_sink_a, m_rel_bias, m_b_merge[0],
          m_w_branch_a[0], m_w_branch_b[0], m_w_out[0]]
    vs = [v_norm_gain, v_q_norm_a, v_k_norm_a, v_q_norm_b, v_k_norm_b, v_sink_a, v_rel_bias, v_b_merge[0],
          v_w_branch_a[0], v_w_branch_b[0], v_w_out[0]]
    ds, nms, nvs = _adamw_small(ws, gs, ms, vs)
    for k, name in enumerate(small_names):
        upd[name] = (gs[k], ds[k], nms[k], nvs[k])

    order = ("norm_gain", "w_in", "q_norm_a", "k_norm_a", "q_norm_b", "k_norm_b", "sink_a", "rel_bias",
             "w_branch_a", "w_branch_b", "b_merge", "w_out")
    lead = {"w_in", "w_branch_a", "w_branch_b", "b_merge", "w_out"}
    outs = [loss, grad_x[None]]
    for part in range(4):
        outs += [upd[name][part][None] if name in lead else upd[name][part] for name in order]
    return tuple(outs)
```

```python
import math

import numpy as np
import jax
import jax.numpy as jnp
from jax import lax
from jax.experimental import pallas as pl
from jax.experimental.pallas import tpu as pltpu

F32 = jnp.float32
BF16 = jnp.bfloat16

SEQ = 4096
D_MODEL = 1024
HEAD_DIM = 64
LANES = 128
EPS = 1e-6
NEG_INF = -1e30
SCALE = HEAD_DIM ** -0.5
N_BUCKETS = 32
MAX_DISTANCE = 1024
N_CHIPS = 4

A_HALF_WINDOW = 128
B_HALF_WINDOW = 64
B_DILATIONS = (1, 4, 16)
Q_BLOCK = 128

QKV_WIDTH = 5376
GATE_WIDTH = 3072
QA_BLK, KA_BLK, VA_BLK = 0, 4, 5
QB_BLK, KB_BLK, VB_BLK = 6, 18, 30
IN_WIDTH = QKV_WIDTH + GATE_WIDTH
W_IN_SHARD = IN_WIDTH // N_CHIPS

SMALL_ROWS = 544
REST_ROWS = 544

ADAM_LR = 0.001
ADAM_B1 = 0.9
ADAM_B2 = 0.999
ADAM_EPS = 1e-08
ADAM_WD = 0.01
ADAM_STEP = 10

VMEM_LIMIT = 56 * 1024 * 1024

NT = (((1,), (1,)), ((), ()))
TN = (((0,), (0,)), ((), ()))
MESH = pl.DeviceIdType.MESH
ANY = pl.BlockSpec(memory_space=pl.ANY)


def _dot(a, b, dims=None):
    if dims is None:
        return jnp.dot(a, b, preferred_element_type=F32)
    return lax.dot_general(a, b, dims, preferred_element_type=F32)


def _params(*semantics):
    return pltpu.CompilerParams(dimension_semantics=semantics or None, vmem_limit_bytes=VMEM_LIMIT)


def _line_width(half_window):
    return pl.cdiv(2 * Q_BLOCK + 2 * half_window - 1, LANES) * LANES


def _bucket_onehot(half_window, stride):
    rel = np.arange(_line_width(half_window)) - (Q_BLOCK - 1) - half_window
    band = np.abs(rel) <= half_window
    rel = rel * stride
    half, max_exact = N_BUCKETS // 2, N_BUCKETS // 4
    n = np.abs(rel)
    nf = np.maximum(n, max_exact).astype(np.float32)
    large = max_exact + (np.log(nf / np.float32(max_exact)) / np.float32(math.log(MAX_DISTANCE / max_exact))
                         * np.float32(half - max_exact)).astype(np.int32)
    large = np.minimum(large, half - 1)
    bucket = (rel > 0).astype(np.int32) * half + np.where(n < max_exact, n, large)
    onehot = (bucket[..., None] == np.arange(N_BUCKETS)) & band[..., None]
    return onehot.astype(np.float32), band


def _bias_lines(rel_bias_cols, half_window, stride):
    onehot, band = _bucket_onehot(half_window, stride)
    h = rel_bias_cols.shape[1]
    t = jnp.einsum("tb,bh->ht", jnp.asarray(onehot), rel_bias_cols, precision=lax.Precision.HIGHEST)
    t = t + jnp.asarray(np.where(band, 0.0, NEG_INF).astype(np.float32))
    return t.reshape(h // 2, 2, -1)


def _bias_grad(d_lines, half_window, stride):
    onehot, _ = _bucket_onehot(half_window, stride)
    h = d_lines.shape[0] * 2
    return jnp.einsum("tb,ht->bh", jnp.asarray(onehot), d_lines.reshape(h, -1), precision=lax.Precision.HIGHEST)


def _unroll_bias(line_ref, tile_ref, w):
    width = line_ref.shape[1]
    for j in range(2):
        rows = jnp.broadcast_to(line_ref[j:j + 1, :], (Q_BLOCK, width))
        rows = pltpu.roll(rows, width - (Q_BLOCK - 1), 1, stride=1, stride_axis=0)
        tile_ref[j * Q_BLOCK:(j + 1) * Q_BLOCK, :] = rows[:, :w]


def _fold_bias_grad(tile_ref, line_ref, w):
    width = line_ref.shape[1]
    row = lax.broadcasted_iota(jnp.int32, (Q_BLOCK, Q_BLOCK), 0)
    col = lax.broadcasted_iota(jnp.int32, (Q_BLOCK, Q_BLOCK), 1)
    flip = jnp.where(row + col == Q_BLOCK - 1, 1.0, 0.0).astype(BF16)
    for j in range(2):
        tile = tile_ref[j * Q_BLOCK:(j + 1) * Q_BLOCK, :]
        hi = tile.astype(BF16)
        lo = (tile - hi.astype(F32)).astype(BF16)
        rows = _dot(flip, hi) + _dot(flip, lo)
        rows = jnp.concatenate([rows, jnp.zeros((Q_BLOCK, width - w), F32)], axis=1)
        rows = pltpu.roll(rows, 0, 1, stride=1, stride_axis=0)
        line_ref[j:j + 1, :] = jnp.sum(rows, axis=0, keepdims=True)


def _row_tile(rows):
    return max(t for t in range(16, 385, 16) if rows % t == 0)


def _cast_rows(w, out_dtype, name):
    r, c = w.shape
    tr = _row_tile(r)

    def body(w_ref, o_ref):
        o_ref[...] = w_ref[...].astype(out_dtype)

    spec = pl.BlockSpec((tr, c), lambda i: (i, 0))
    return pl.pallas_call(
        body, name=name, grid=(r // tr,), in_specs=[spec], out_specs=spec,
        out_shape=jax.ShapeDtypeStruct((r, c), out_dtype), compiler_params=_params("arbitrary"),
    )(w)


STAGE_ROWS = 528


def _gather_scratch():
    return [pltpu.SemaphoreType.DMA((12,)), pltpu.SemaphoreType.DMA((12,)), pltpu.SemaphoreType.DMA((2,)),
            pltpu.SemaphoreType.DMA((2,)), pltpu.VMEM((2, STAGE_ROWS, D_MODEL), BF16)]


def _gather_phases(src_ref, out_ref, send_sems, recv_sems, in_sems, out_sems, stage):
    rows = src_ref.shape[0]
    x, y, c = lax.axis_index("x"), lax.axis_index("y"), lax.axis_index("c")
    sibling = (x, y, 1 - c)
    near = (x + (1 - c) - 2 * x * (1 - c), y + c - 2 * y * c)
    far = (x + c - 2 * x * c, y + (1 - c) - 2 * y * (1 - c))
    diag = (1 - x, 1 - y)
    chip_no = lambda chip: 2 * chip[0] + chip[1]
    my_chip = chip_no((x, y))

    pieces = 2 if (rows // 2) % 32 == 0 else 1
    n = rows // 2 // pieces

    def half_of(chip, half, p):
        start = pl.multiple_of(chip * rows + half * (rows // 2) + p * n, 16)
        return out_ref.at[pl.ds(start, n), :]

    def copy(k, p, src, dst, to):
        return pltpu.make_async_remote_copy(src_ref=src, dst_ref=dst, send_sem=send_sems.at[k * pieces + p],
                                            recv_sem=recv_sems.at[k * pieces + p], device_id=to, device_id_type=MESH)

    def mine(p):
        return src_ref.at[pl.ds(pl.multiple_of(c * (rows // 2) + p * n, 16), n), :]

    def keep_own():
        outs = []
        for i, r0 in enumerate(range(0, rows, STAGE_ROWS)):
            n = min(STAGE_ROWS, rows - r0)
            slot = i % 2
            if i >= 2:
                outs[i - 2].wait()
            buf = stage.at[slot, pl.ds(0, n), :]
            load = pltpu.make_async_copy(src_ref.at[pl.ds(r0, n), :], buf, in_sems.at[slot])
            load.start()
            load.wait()
            start = pl.multiple_of(my_chip * rows + r0, 16)
            outs.append(pltpu.make_async_copy(buf, out_ref.at[pl.ds(start, n), :], out_sems.at[slot]))
            outs[i].start()
        for cp in outs[-2:]:
            cp.wait()

    def start():
        for p in range(pieces):
            copy(0, p, mine(p), half_of(my_chip, c, p), (*near, c)).start()
            copy(1, p, mine(p), half_of(my_chip, c, p), (*far, c)).start()
        keep_own()

    def pass_on(j, p, chip):
        landed = half_of(chip_no(chip), c, p)
        copy(3 + j, p, landed, landed, sibling).start()

    def relay():
        for p in range(pieces):
            landed = half_of(chip_no(near), c, p)
            copy(0, p, landed, landed, sibling).wait_recv()
            copy(2, p, landed, landed, (*far, c)).start()
            pass_on(0, p, near)

    def forward():
        for j, chip in ((1, far), (2, diag)):
            for p in range(pieces):
                landed = half_of(chip_no(chip), c, p)
                copy(j, p, landed, landed, sibling).wait_recv()
                pass_on(j, p, chip)

    def finish():
        for j, chip in ((0, far), (1, near), (2, diag)):
            for p in range(pieces):
                other = half_of(chip_no(chip), 1 - c, p)
                copy(3 + j, p, other, other, sibling).wait_recv()
        for k in range(6):
            for p in range(pieces):
                copy(k, p, mine(p), mine(p), sibling).wait_send()

    return start, relay, forward, finish


def _gather_weights(shard):
    def body(src_ref, out_ref, *scratch):
        for phase in _gather_phases(src_ref, out_ref, *scratch):
            phase()

    return pl.pallas_call(
        body, name="gather_weights", in_specs=[ANY], out_specs=ANY,
        out_shape=jax.ShapeDtypeStruct((N_CHIPS * shard.shape[0], D_MODEL), BF16),
        scratch_shapes=_gather_scratch(),
    )(shard)


W_BLOCK = 768


def _w_blocks(first, count):
    return [pl.BlockSpec((W_BLOCK, D_MODEL), lambda *_, k=k: (first + k, 0)) for k in range(count)]


def _in_proj(x, gain, w_t, first_block, n_blocks, out_dtype, name, keep_h, ride=None):
    tm = 512
    n_steps = SEQ // tm
    n_out = 2 if keep_h else 1

    def body(x_ref, g_ref, *refs):
        w_refs, outs = refs[:n_blocks], refs[n_blocks + (ride is not None):n_blocks + (ride is not None) + n_out]
        if ride is not None:
            phases = _gather_phases(refs[n_blocks], *refs[n_blocks + 1 + n_out:])
            for step, phase in zip((0, 2, 4, n_steps - 1), phases):
                pl.when(pl.program_id(0) == step)(phase)
        xf = x_ref[...]
        r = lax.rsqrt(jnp.mean(xf * xf, axis=-1, keepdims=True) + EPS)
        h = ((xf * r) * g_ref[...]).astype(BF16)
        if keep_h:
            outs[1][...] = h
        for k, w_ref in enumerate(w_refs):
            outs[0][:, k * W_BLOCK:(k + 1) * W_BLOCK] = _dot(h, w_ref[...], NT).astype(out_dtype)

    riding = [] if ride is None else [ride]
    return pl.pallas_call(
        body, name=name, grid=(n_steps,),
        in_specs=[pl.BlockSpec((tm, D_MODEL), lambda i: (i, 0)), pl.BlockSpec((1, D_MODEL), lambda i: (0, 0))]
        + _w_blocks(first_block, n_blocks) + [ANY for _ in riding],
        out_specs=[pl.BlockSpec((tm, W_BLOCK * n_blocks), lambda i: (i, 0)),
                   pl.BlockSpec((tm, D_MODEL), lambda i: (i, 0))][:n_out] + [ANY for _ in riding],
        out_shape=[jax.ShapeDtypeStruct((SEQ, W_BLOCK * n_blocks), out_dtype),
                   jax.ShapeDtypeStruct((SEQ, D_MODEL), BF16)][:n_out]
        + [jax.ShapeDtypeStruct((N_CHIPS * r.shape[0], D_MODEL), BF16) for r in riding],
        scratch_shapes=_gather_scratch() if riding else [],
        compiler_params=_params("arbitrary"),
    )(x, gain, *([w_t] * n_blocks), *riding)


CHUNK = 256
CHUNK_UNROLL = 8
TILE_UNROLL = 8


def _low_half():
    return lax.broadcasted_iota(jnp.int32, (1, LANES), 1) < HEAD_DIM


def _half_sum(v, low):
    del low
    row = lax.broadcasted_iota(jnp.int32, (2 * LANES, LANES), 0)
    col = lax.broadcasted_iota(jnp.int32, (2 * LANES, LANES), 1)
    ones = jnp.where((row % LANES) // HEAD_DIM == col // HEAD_DIM, 1.0, 0.0).astype(BF16)
    hi = v.astype(BF16)
    lo = (v - hi.astype(F32)).astype(BF16)
    return _dot(jnp.concatenate([hi, lo], axis=1), ones)


def _chunks(fn, init=0):
    def body(i, carry):
        for u in range(CHUNK_UNROLL):
            carry = fn(pl.multiple_of((i * CHUNK_UNROLL + u) * CHUNK, CHUNK), carry)
        return carry

    return lax.fori_loop(0, SEQ // (CHUNK * CHUNK_UNROLL), body, init)


def _inv_rms(t, low):
    del low
    row = lax.broadcasted_iota(jnp.int32, (LANES, LANES), 0)
    col = lax.broadcasted_iota(jnp.int32, (LANES, LANES), 1)
    ones = jnp.where(row // HEAD_DIM == col // HEAD_DIM, 1.0, 0.0).astype(BF16)
    return lax.rsqrt(_dot((t * t).astype(BF16), ones) * (1.0 / HEAD_DIM) + EPS)


def _prep_q(q_ref, gain_ref, qn_ref):
    low = _low_half()

    def step(r0, carry):
        q = q_ref[pl.ds(r0, CHUNK), :].astype(F32)
        qn_ref[pl.ds(r0, CHUNK), :] = ((q * _inv_rms(q, low)) * gain_ref[...]) * SCALE
        return carry

    _chunks(step)


def _own_half(t, keep):
    return jnp.where(keep, t, pltpu.roll(t, HEAD_DIM, 1))


def _prep_kv(k_ref, v_ref, gain_ref, kp_ref, vp_ref, pad, keep=None):
    low = _low_half()
    zeros = jnp.zeros((pad, LANES), F32)
    for ref in (kp_ref, vp_ref):
        ref[pl.ds(0, pad), :] = zeros
        ref[pl.ds(pad + SEQ, pad), :] = zeros

    def step(r0, carry):
        k = k_ref[pl.ds(r0, CHUNK), :].astype(F32)
        v = v_ref[pl.ds(r0, CHUNK), :].astype(F32)
        kn = (k * _inv_rms(k, low)) * gain_ref[...]
        if keep is not None:
            kn, v = _own_half(kn, keep), _own_half(v, keep)
        kp_ref[pl.ds(pad + r0, CHUNK), :] = kn
        vp_ref[pl.ds(pad + r0, CHUNK), :] = v
        return carry

    _chunks(step)


def _tiles(d, half_window, fn):
    w = Q_BLOCK + 2 * half_window
    length = SEQ // d
    n_blocks = length // Q_BLOCK
    col = lax.broadcasted_iota(jnp.int32, (1, w), 1)

    def step(it, carry):
        c, n = it // n_blocks, it % n_blocks
        start = c + (d * Q_BLOCK) * n
        if d == 1:
            start = pl.multiple_of(start, Q_BLOCK)
            q_rows, k_rows = pl.ds(start, Q_BLOCK), pl.ds(start, w)
        else:
            q_rows, k_rows = pl.ds(start, Q_BLOCK, stride=d), pl.ds(start, w, stride=d)
        t = n * Q_BLOCK - half_window + col
        edge = jnp.where((t < 0) | (t >= length), NEG_INF, 0.0)
        fn(q_rows, k_rows, edge)
        return carry

    lax.fori_loop(0, d * n_blocks, step, 0, unroll=TILE_UNROLL)


def _stack_heads(t, low):
    return jnp.concatenate([jnp.where(low, t, 0.0), jnp.where(low, 0.0, t)], axis=0).astype(BF16)


def _unstack_heads(t, low):
    return jnp.where(low, t[:Q_BLOCK], t[Q_BLOCK:])


def _per_head(pair):
    return jnp.concatenate([jnp.full((Q_BLOCK, 1), pair[0], F32), jnp.full((Q_BLOCK, 1), pair[1], F32)], axis=0)


def _fwd_tiles(qn_ref, kp_ref, vp_ref, bias_ref, emit, *, d, half_window, sinks=None):
    low = _low_half()
    w = Q_BLOCK + 2 * half_window
    sink = None if sinks is None else _per_head(sinks)

    def tile(q_rows, k_rows, edge):
        q2 = _stack_heads(qn_ref[q_rows, :], low)
        k = kp_ref[k_rows, :].astype(BF16)
        v1 = jnp.concatenate([vp_ref[k_rows, :], jnp.ones((w, LANES), F32)], axis=1).astype(BF16)
        s = _dot(q2, k, NT) + bias_ref[...] + edge
        m = jnp.max(s, axis=-1, keepdims=True)
        if sink is not None:
            m = jnp.maximum(m, sink)
        o = _dot(jnp.exp(s - m).astype(BF16), v1)
        l = o[:, LANES:]
        if sink is not None:
            l = l + jnp.exp(sink - m)
        emit(q_rows, _unstack_heads(o[:, :LANES] * (1.0 / l), low), _unstack_heads(m + jnp.log(l), low))

    _tiles(d, half_window, tile)


def _bwd_tiles(qn_ref, kp_ref, vp_ref, bias_ref, do_ref, lse_ref, delta_ref, dq_ref, dk_ref, dv_ref, ds_ref,
               *, d, half_window, sinks=None, dsink_ref=None):
    low = _low_half()
    w = Q_BLOCK + 2 * half_window
    sink = None if sinks is None else _per_head(sinks)

    def rows_of(t):
        return jnp.concatenate([t[:, 0:1], t[:, HEAD_DIM:HEAD_DIM + 1]], axis=0)

    def tile(q_rows, k_rows, edge):
        q2 = _stack_heads(qn_ref[q_rows, :], low)
        do2 = _stack_heads(do_ref[q_rows, :], low)
        k = kp_ref[k_rows, :].astype(BF16)
        v = vp_ref[k_rows, :].astype(BF16)
        lse = rows_of(lse_ref[q_rows, :])
        delta = rows_of(delta_ref[q_rows, :])
        p = jnp.exp(_dot(q2, k, NT) + bias_ref[...] + edge - lse)
        ds = p * (_dot(do2, v, NT) - delta)
        ds_ref[...] += ds
        if sink is not None:
            dsink_ref[...] += (-jnp.exp(sink - lse) * delta).reshape(2, Q_BLOCK, 1)
        dsb, pb = ds.astype(BF16), p.astype(BF16)
        dq_ref[q_rows, :] = _unstack_heads(_dot(dsb, k), low)
        dk_ref[k_rows, :] += _dot(dsb, q2, TN)
        dv_ref[k_rows, :] += _dot(pb, do2, TN)

    _tiles(d, half_window, tile)


def _norm_bwd(raw_ref, gain_ref, dn_ref, dn_offset, out_ref, scale):
    low = _low_half()

    def step(r0, dgain):
        t = raw_ref[pl.ds(r0, CHUNK), :].astype(F32)
        dn = dn_ref[pl.ds(dn_offset + r0, CHUNK), :]
        dth = dn * (gain_ref[...] * scale)
        r = _inv_rms(t, low)
        th = t * r
        out_ref[pl.ds(r0, CHUNK), :] = (r * (dth - th * (r * _half_sum(dth * t, low) * (1.0 / HEAD_DIM)))).astype(BF16)
        return dgain + jnp.sum(dn * th, axis=0, keepdims=True) * scale

    return _chunks(step, jnp.zeros((1, LANES), F32))


def _rows8(v):
    return jnp.broadcast_to(v, (8, v.shape[-1]))


A_W = Q_BLOCK + 2 * A_HALF_WINDOW
A_PAD = A_HALF_WINDOW


def _seq_block(col_fn):
    return pl.BlockSpec((SEQ, LANES), col_fn)


def _attn_a_fwd(qkv, gain_q, gain_k, bias, sink):
    def body(sink_ref, q_ref, k_ref, v_ref, gq_ref, gk_ref, line_ref, o_ref, lse_ref, qn_ref, kp_ref, vp_ref,
             bias_ref):
        hp = pl.program_id(0)
        keep = (lax.broadcasted_iota(jnp.int32, (1, LANES), 1) // HEAD_DIM) == hp // 2
        _prep_q(q_ref, gq_ref, qn_ref)
        _prep_kv(k_ref, v_ref, gk_ref, kp_ref, vp_ref, A_PAD, keep)
        _unroll_bias(line_ref, bias_ref, A_W)

        def emit(rows, out, lse):
            o_ref[rows, :] = out
            lse_ref[rows, :] = lse

        _fwd_tiles(qn_ref, kp_ref, vp_ref, bias_ref, emit, d=1, half_window=A_HALF_WINDOW,
                   sinks=(sink_ref[2 * hp], sink_ref[2 * hp + 1]))

    vec = pl.BlockSpec((1, LANES), lambda hp, s: (0, 0))
    return pl.pallas_call(
        body, name="attn_a_fwd",
        grid_spec=pltpu.PrefetchScalarGridSpec(
            num_scalar_prefetch=1, grid=(4,),
            in_specs=[_seq_block(lambda hp, s: (0, QA_BLK + hp)), _seq_block(lambda hp, s: (0, KA_BLK)),
                      _seq_block(lambda hp, s: (0, VA_BLK)), vec, vec,
                      pl.BlockSpec((None, 2, _line_width(A_HALF_WINDOW)), lambda hp, s: (hp, 0, 0))],
            out_specs=[_seq_block(lambda hp, s: (0, hp)), _seq_block(lambda hp, s: (0, hp))],
            scratch_shapes=[pltpu.VMEM((SEQ, LANES), F32), pltpu.VMEM((SEQ + 2 * A_PAD, LANES), F32),
                            pltpu.VMEM((SEQ + 2 * A_PAD, LANES), F32), pltpu.VMEM((2 * Q_BLOCK, A_W), F32)]),
        out_shape=[jax.ShapeDtypeStruct((SEQ, 512), F32)] * 2,
        compiler_params=_params("arbitrary"),
    )(sink.reshape(8), qkv, qkv, qkv, gain_q, gain_k, bias)


def _attn_a_bwd(qkv, gain_q, gain_k, bias, sink, delta, lse, d_out):
    def body(sink_ref, q_ref, k_ref, v_ref, gq_ref, gk_ref, line_ref, delta_ref, lse_ref, do_ref,
             dq_out, dkv_out, dgq_out, dgk_out, dline_out, dsink_out,
             qn_ref, kp_ref, vp_ref, dq_ref, dk_ref, dv_ref, dk_tot, dv_tot, bias_ref, ds_out):
        hp = pl.program_id(0)
        kv_head = hp // 2
        keep = (lax.broadcasted_iota(jnp.int32, (1, LANES), 1) // HEAD_DIM) == kv_head
        _prep_q(q_ref, gq_ref, qn_ref)
        _prep_kv(k_ref, v_ref, gk_ref, kp_ref, vp_ref, A_PAD, keep)
        _unroll_bias(line_ref, bias_ref, A_W)
        ds_out[...] = jnp.zeros_like(ds_out)
        dsink_out[...] = jnp.zeros_like(dsink_out)

        @pl.when(hp % 2 == 0)
        def _():
            dk_ref[...] = jnp.zeros_like(dk_ref)
            dv_ref[...] = jnp.zeros_like(dv_ref)

        @pl.when(hp == 0)
        def _():
            dk_tot[...] = jnp.zeros_like(dk_tot)
            dv_tot[...] = jnp.zeros_like(dv_tot)

        _bwd_tiles(qn_ref, kp_ref, vp_ref, bias_ref, do_ref, lse_ref, delta_ref, dq_ref, dk_ref, dv_ref, ds_out,
                   d=1, half_window=A_HALF_WINDOW, sinks=(sink_ref[2 * hp], sink_ref[2 * hp + 1]),
                   dsink_ref=dsink_out)
        _fold_bias_grad(ds_out, dline_out, A_W)
        dgq_out[...] = _rows8(_norm_bwd(q_ref, gq_ref, dq_ref, 0, dq_out, SCALE))

        def fold(r0, carry):
            rows = pl.ds(A_PAD + r0, CHUNK)
            for acc, tot in ((dk_ref, dk_tot), (dv_ref, dv_tot)):
                t = acc[rows, :]
                tot[pl.ds(r0, CHUNK), :] += jnp.where(keep, t + pltpu.roll(t, HEAD_DIM, 1), 0.0)
            return carry

        @pl.when(hp % 2 == 1)
        def _():
            _chunks(fold)

        @pl.when(hp == 3)
        def _():
            dgk_out[...] = _rows8(_norm_bwd(k_ref, gk_ref, dk_tot, 0, dkv_out.at[0], 1.0))
            dkv_out[1] = dv_tot[...].astype(BF16)

    vec = pl.BlockSpec((1, LANES), lambda hp, s: (0, 0))
    seq_f32 = pltpu.VMEM((SEQ, LANES), F32)
    padded = pltpu.VMEM((SEQ + 2 * A_PAD, LANES), F32)
    return pl.pallas_call(
        body, name="attn_a_bwd",
        grid_spec=pltpu.PrefetchScalarGridSpec(
            num_scalar_prefetch=1, grid=(4,),
            in_specs=[_seq_block(lambda hp, s: (0, QA_BLK + hp)), _seq_block(lambda hp, s: (0, KA_BLK)),
                      _seq_block(lambda hp, s: (0, VA_BLK)), vec, vec,
                      pl.BlockSpec((None, 2, _line_width(A_HALF_WINDOW)), lambda hp, s: (hp, 0, 0)),
                      _seq_block(lambda hp, s: (0, hp)), _seq_block(lambda hp, s: (0, hp)),
                      _seq_block(lambda hp, s: (0, hp))],
            out_specs=[pl.BlockSpec((None, SEQ, LANES), lambda hp, s: (hp, 0, 0)),
                       pl.BlockSpec((2, SEQ, LANES), lambda hp, s: (0, 0, 0)),
                       pl.BlockSpec((None, 8, LANES), lambda hp, s: (hp, 0, 0)),
                       pl.BlockSpec((8, LANES), lambda hp, s: (0, 0)),
                       pl.BlockSpec((None, 2, _line_width(A_HALF_WINDOW)), lambda hp, s: (hp, 0, 0)),
                       pl.BlockSpec((None, 2, Q_BLOCK, 1), lambda hp, s: (hp, 0, 0, 0))],
            scratch_shapes=[seq_f32, padded, padded, seq_f32, padded, padded, seq_f32, seq_f32,
                            pltpu.VMEM((2 * Q_BLOCK, A_W), F32), pltpu.VMEM((2 * Q_BLOCK, A_W), F32)]),
        out_shape=[jax.ShapeDtypeStruct((4, SEQ, LANES), BF16), jax.ShapeDtypeStruct((2, SEQ, LANES), BF16),
                   jax.ShapeDtypeStruct((4, 8, LANES), F32), jax.ShapeDtypeStruct((8, LANES), F32),
                   jax.ShapeDtypeStruct((4, 2, _line_width(A_HALF_WINDOW)), F32),
                   jax.ShapeDtypeStruct((4, 2, Q_BLOCK, 1), F32)],
        compiler_params=_params("arbitrary"),
    )(sink.reshape(8), qkv, qkv, qkv, gain_q, gain_k, bias, delta, lse, d_out)


B_W = Q_BLOCK + 2 * B_HALF_WINDOW
B_PAD_MAX = B_HALF_WINDOW * B_DILATIONS[-1]


def _attn_b_fwd(qkv, gain_q, gain_k, bias):
    def body(q_ref, k_ref, v_ref, gq_ref, gk_ref, line_ref, o_ref, lse_ref, qn_ref, kp_ref, vp_ref, bias_ref,
             og_ref, lg_ref):
        g = pl.program_id(1)
        _prep_q(q_ref, gq_ref, qn_ref)
        _unroll_bias(line_ref, bias_ref, B_W)

        for gi, d in enumerate(B_DILATIONS):
            @pl.when(g == gi)
            def _(gi=gi, d=d):
                def keep(rows, out, lse):
                    og_ref.at[gi][rows, :] = out
                    lg_ref.at[gi][rows, :] = lse

                _prep_kv(k_ref, v_ref, gk_ref, kp_ref, vp_ref, B_HALF_WINDOW * d)
                _fwd_tiles(qn_ref, kp_ref, vp_ref, bias_ref, keep, d=d, half_window=B_HALF_WINDOW)

        @pl.when(g == len(B_DILATIONS) - 1)
        def _():
            def combine(r0, carry):
                rows = pl.ds(r0, CHUNK)
                lses = [lg_ref[gi, rows, :] for gi in range(len(B_DILATIONS))]
                top = jnp.maximum(jnp.maximum(lses[0], lses[1]), lses[2])
                weights = [jnp.exp(l - top) for l in lses]
                total = weights[0] + weights[1] + weights[2]
                out = sum(w * og_ref[gi, rows, :] for gi, w in enumerate(weights))
                o_ref[rows, :] = out * (1.0 / total)
                lse_ref[rows, :] = top + jnp.log(total)
                return carry

            _chunks(combine)

    vec = pl.BlockSpec((1, LANES), lambda hp, g: (0, 0))
    padded = pltpu.VMEM((SEQ + 2 * B_PAD_MAX, LANES), F32)
    return pl.pallas_call(
        body, name="attn_b_fwd", grid=(4, 3),
        in_specs=[_seq_block(lambda hp, g: (0, QB_BLK + 4 * g + hp)), _seq_block(lambda hp, g: (0, KB_BLK + 4 * g + hp)),
                  _seq_block(lambda hp, g: (0, VB_BLK + 4 * g + hp)), vec, vec,
                  pl.BlockSpec((None, 2, _line_width(B_HALF_WINDOW)), lambda hp, g: (4 * g + hp, 0, 0))],
        out_specs=[_seq_block(lambda hp, g: (0, hp)), _seq_block(lambda hp, g: (0, hp))],
        out_shape=[jax.ShapeDtypeStruct((SEQ, 512), F32)] * 2,
        scratch_shapes=[pltpu.VMEM((SEQ, LANES), F32), padded, padded, pltpu.VMEM((2 * Q_BLOCK, B_W), F32),
                        pltpu.VMEM((len(B_DILATIONS), SEQ, LANES), F32), pltpu.VMEM((len(B_DILATIONS), SEQ, LANES), F32)],
        compiler_params=_params("arbitrary", "arbitrary"),
    )(qkv, qkv, qkv, gain_q, gain_k, bias)


def _attn_b_bwd(qkv, gain_q, gain_k, bias, delta, lse, d_out):
    def body(q_ref, k_ref, v_ref, gq_ref, gk_ref, line_ref, delta_ref, lse_ref, do_ref,
             dq_out, dk_out, dv_out, dgq_out, dgk_out, dline_out,
             qn_ref, kp_ref, vp_ref, dq_ref, dk_ref, dv_ref, bias_ref, ds_out):
        g = pl.program_id(1)
        _prep_q(q_ref, gq_ref, qn_ref)
        _unroll_bias(line_ref, bias_ref, B_W)
        ds_out[...] = jnp.zeros_like(ds_out)
        for gi, d in enumerate(B_DILATIONS):
            @pl.when(g == gi)
            def _():
                pad = B_HALF_WINDOW * d
                for acc in (dk_ref, dv_ref):
                    acc[pl.ds(0, SEQ + 2 * pad), :] = jnp.zeros((SEQ + 2 * pad, LANES), F32)
                _prep_kv(k_ref, v_ref, gk_ref, kp_ref, vp_ref, pad)
                _bwd_tiles(qn_ref, kp_ref, vp_ref, bias_ref, do_ref, lse_ref, delta_ref, dq_ref, dk_ref, dv_ref,
                           ds_out, d=d, half_window=B_HALF_WINDOW)
                dgk_out[...] = _rows8(_norm_bwd(k_ref, gk_ref, dk_ref, pad, dk_out, 1.0))
                dv_out[...] = dv_ref[pl.ds(pad, SEQ), :].astype(BF16)
        _fold_bias_grad(ds_out, dline_out, B_W)
        dgq_out[...] = _rows8(_norm_bwd(q_ref, gq_ref, dq_ref, 0, dq_out, SCALE))

    vec = pl.BlockSpec((1, LANES), lambda hp, g: (0, 0))
    seq_f32 = pltpu.VMEM((SEQ, LANES), F32)
    padded = pltpu.VMEM((SEQ + 2 * B_PAD_MAX, LANES), F32)
    part = pl.BlockSpec((None, 8, LANES), lambda hp, g: (4 * g + hp, 0, 0))
    line = pl.BlockSpec((None, 2, _line_width(B_HALF_WINDOW)), lambda hp, g: (4 * g + hp, 0, 0))
    return pl.pallas_call(
        body, name="attn_b_bwd", grid=(4, 3),
        in_specs=[_seq_block(lambda hp, g: (0, QB_BLK + 4 * g + hp)), _seq_block(lambda hp, g: (0, KB_BLK + 4 * g + hp)),
                  _seq_block(lambda hp, g: (0, VB_BLK + 4 * g + hp)), vec, vec,
                  line,
                  _seq_block(lambda hp, g: (0, hp)), _seq_block(lambda hp, g: (0, hp)), _seq_block(lambda hp, g: (0, hp))],
        out_specs=[pl.BlockSpec((None, SEQ, LANES), lambda hp, g: (4 * g + hp, 0, 0))] * 3 + [part, part, line],
        out_shape=[jax.ShapeDtypeStruct((12, SEQ, LANES), BF16)] * 3
        + [jax.ShapeDtypeStruct((12, 8, LANES), F32)] * 2
        + [jax.ShapeDtypeStruct((12, 2, _line_width(B_HALF_WINDOW)), F32)],
        scratch_shapes=[seq_f32, padded, padded, seq_f32, padded, padded,
                        pltpu.VMEM((2 * Q_BLOCK, B_W), F32), pltpu.VMEM((2 * Q_BLOCK, B_W), F32)],
        compiler_params=_params("arbitrary", "arbitrary"),
    )(qkv, qkv, qkv, gain_q, gain_k, bias, delta, lse, d_out)


def _sigmoid(t):
    return 1.0 / (1.0 + jnp.exp(-t))


def _middle(out_a, out_b, gates, x, target, w_a, w_b, w_out, b_merge):
    tm = 256
    n_steps = SEQ // tm

    def body(oa_ref, ob_ref, g_ref, x_ref, t_ref, wa_ref, wb_ref, wo_ref, bm_ref,
             dy_ref, dg_ref, doa_ref, dob_ref, dla_ref, dlb_ref, dwa_ref, dwb_ref, dwo_ref, dbm_ref, sq_ref):
        @pl.when(pl.program_id(0) == 0)
        def _():
            for ref in (dwa_ref, dwb_ref, dwo_ref, dbm_ref, sq_ref):
                ref[...] = jnp.zeros_like(ref)

        gate_a, gate_b = g_ref[:, 0:512], g_ref[:, 512:1024]
        sig_a, sig_b = _sigmoid(gate_a), _sigmoid(gate_b)
        silu_a, silu_b = gate_a * sig_a, gate_b * sig_b
        oa, ob = oa_ref[...], ob_ref[...]
        ya, yb = (oa * silu_a).astype(BF16), (ob * silu_b).astype(BF16)
        br_a, br_b = _dot(ya, wa_ref[...]), _dot(yb, wb_ref[...])
        m0 = _sigmoid(g_ref[:, 1024:2048] + bm_ref[0:1, :])
        m1 = _sigmoid(g_ref[:, 2048:3072] + bm_ref[1:2, :])
        merged = (m0 * br_a + m1 * br_b).astype(BF16)
        err = (x_ref[...] + _dot(merged, wo_ref[...])) - t_ref[...]
        sq_ref[...] += jnp.sum(err * err, axis=0, keepdims=True)

        dy = err * (1.0 / D_MODEL)
        dy_ref[...] = dy
        dyb = dy.astype(BF16)
        dmerged = _dot(dyb, wo_ref[...], NT)
        dwo_ref[...] += _dot(merged, dyb, TN)
        dbr_a, dbr_b = (dmerged * m0).astype(BF16), (dmerged * m1).astype(BF16)
        dm0 = (dmerged * br_a) * (m0 * (1.0 - m0))
        dm1 = (dmerged * br_b) * (m1 * (1.0 - m1))
        dbm_ref[0:1, :] += jnp.sum(dm0, axis=0, keepdims=True)
        dbm_ref[1:2, :] += jnp.sum(dm1, axis=0, keepdims=True)
        for s in range(N_CHIPS):
            cols = slice(256 * s, 256 * (s + 1))
            dwa_ref[s] += _dot(ya, dbr_a[:, cols], TN)
            dwb_ref[s] += _dot(yb, dbr_b[:, cols], TN)
        dya, dyb_ = _dot(dbr_a, wa_ref[...], NT), _dot(dbr_b, wb_ref[...], NT)
        doa, dob = dya * silu_a, dyb_ * silu_b
        doa_ref[...] = doa
        dob_ref[...] = dob
        for blk in range(512 // LANES):
            lanes = slice(blk * LANES, (blk + 1) * LANES)
            dla_ref[:, lanes] = _half_sum(doa[:, lanes] * oa[:, lanes], None)
            dlb_ref[:, lanes] = _half_sum(dob[:, lanes] * ob[:, lanes], None)
        d_gates = (((dya * oa) * (sig_a * (1.0 + gate_a * (1.0 - sig_a)))).astype(BF16),
                   ((dyb_ * ob) * (sig_b * (1.0 + gate_b * (1.0 - sig_b)))).astype(BF16),
                   dm0.astype(BF16), dm1.astype(BF16))
        blk = 0
        for part in d_gates:
            for c0 in range(0, part.shape[1], 256):
                dg_ref[blk] = part[:, c0:c0 + 256]
                blk += 1

    def rows(width):
        return pl.BlockSpec((tm, width), lambda i: (i, 0))

    def whole(*shape):
        return pl.BlockSpec(shape, lambda i: (0,) * len(shape))

    return pl.pallas_call(
        body, name="middle", grid=(n_steps,),
        in_specs=[rows(512), rows(512), rows(GATE_WIDTH), rows(D_MODEL), rows(D_MODEL),
                  whole(512, D_MODEL), whole(512, D_MODEL), whole(D_MODEL, D_MODEL), whole(2, D_MODEL)],
        out_specs=[rows(D_MODEL), pl.BlockSpec((GATE_WIDTH // 256, tm, 256), lambda i: (0, i, 0)),
                   rows(512), rows(512), rows(512), rows(512),
                   whole(N_CHIPS, 512, 256), whole(N_CHIPS, 512, 256), whole(D_MODEL, D_MODEL),
                   whole(2, D_MODEL), whole(1, D_MODEL)],
        out_shape=[jax.ShapeDtypeStruct((SEQ, D_MODEL), F32), jax.ShapeDtypeStruct((GATE_WIDTH // 256, SEQ, 256), BF16),
                   jax.ShapeDtypeStruct((SEQ, 512), F32), jax.ShapeDtypeStruct((SEQ, 512), F32),
                   jax.ShapeDtypeStruct((SEQ, 512), F32), jax.ShapeDtypeStruct((SEQ, 512), F32),
                   jax.ShapeDtypeStruct((N_CHIPS, 512, 256), F32), jax.ShapeDtypeStruct((N_CHIPS, 512, 256), F32),
                   jax.ShapeDtypeStruct((D_MODEL, D_MODEL), F32), jax.ShapeDtypeStruct((2, D_MODEL), F32),
                   jax.ShapeDtypeStruct((1, D_MODEL), F32)],
        compiler_params=_params("arbitrary"),
    )(out_a, out_b, gates, x, target, w_a, w_b, w_out, b_merge)


def _which(j, edges, fns):
    lo = 0
    for hi, fn in zip(edges, fns):
        pl.when((j >= lo) & (j < hi))(fn)
        lo = hi


def _sibling_rows(tile, core):
    lo, hi = tile * W_BLOCK, (tile + 1) * W_BLOCK
    for chip in range(N_CHIPS):
        a = chip * W_IN_SHARD + (1 - core) * (W_IN_SHARD // 2)
        first, last = max(lo, a), min(hi, a + W_IN_SHARD // 2)
        if first < last:
            return chip, first - a, first - lo, last - first
    return None


def _d_w_in(d_proj, h, rest=None):
    plan, step, width = [], 0, 0
    for p in d_proj:
        total = p.shape[0] * p.shape[2]
        if width + total <= W_BLOCK:
            plan.append((p.shape[0], step, 1))
            width += total
            if width == W_BLOCK:
                step, width = step + 1, 0
        else:
            assert width == 0 and total % W_BLOCK == 0
            plan.append((W_BLOCK // p.shape[2], step, total // W_BLOCK))
            step += total // W_BLOCK
    assert width == 0 and step == IN_WIDTH // W_BLOCK
    firsts = sorted({first for _, first, _ in plan})
    edges = firsts[1:] + [step]
    halves = 2

    hand_over = rest is not None
    half = W_IN_SHARD // 2

    def body(*refs):
        if hand_over:
            pieces, h_ref, rest_ref = refs[:len(d_proj)], refs[len(d_proj)], refs[len(d_proj) + 1]
            o_ref, got_ref, got_rest_ref, acc_ref, send_sems, recv_sems, stage = refs[len(d_proj) + 2:]
        else:
            pieces, h_ref, o_ref, acc_ref = refs[:-3], refs[-3], refs[-2], refs[-1]
        k = pl.program_id(1)

        def emit(group):
            def fn():
                cols = jnp.concatenate([ref[b] for ref in group for b in range(ref.shape[0])], axis=1)
                term = _dot(cols, h_ref[...], TN)

                @pl.when(k == 0)
                def _():
                    acc_ref[...] = term

                @pl.when(k == halves - 1)
                def _():
                    o_ref[...] = (acc_ref[...] + term).astype(BF16)
            return fn

        groups = [[ref for ref, (_, first, _) in zip(pieces, plan) if first == f] for f in firsts]
        _which(pl.program_id(0), edges, [emit(group) for group in groups])

        if hand_over:
            cx, cy, c = lax.axis_index("x"), lax.axis_index("y"), lax.axis_index("c")
            sibling = (cx, cy, 1 - c)

            def to_sibling(sem, src, dst, recv=0):
                return pltpu.make_async_remote_copy(src_ref=src, dst_ref=dst, send_sem=send_sems.at[sem],
                                                    recv_sem=recv_sems.at[recv], device_id=sibling, device_id_type=MESH)

            def tile_copy(tile, core):
                chip, row, start, rows = _sibling_rows(tile, core)
                return to_sibling(tile % 2, stage.at[tile % 2, pl.ds(0, rows), :], got_ref.at[chip, pl.ds(row, rows), :])

            rest_copy = to_sibling(2, _half_rows(rest_ref, 1 - c), got_rest_ref, recv=1)

            @pl.when((pl.program_id(0) == 0) & (k == 0))
            def _():
                rest_copy.start()

            for tile in range(step):
                for core in range(2):
                    @pl.when((pl.program_id(0) == tile) & (k == halves - 1) & (c == core))
                    def _(tile=tile, core=core):
                        if tile >= 2 and _sibling_rows(tile - 2, core):
                            tile_copy(tile - 2, core).wait_send()
                        if _sibling_rows(tile, core):
                            _, _, start, rows = _sibling_rows(tile, core)
                            stage[tile % 2, 0:rows, :] = o_ref[start:start + rows, :]
                            tile_copy(tile, core).start()
                        if tile == step - 1:
                            for last in (step - 2, step - 1):
                                if _sibling_rows(last, core):
                                    tile_copy(last, core).wait_send()
                            rest_copy.wait()
                            to_sibling(0, got_ref, got_ref).wait_recv()

    def cols_spec(piece, n, first, steps):
        def index(j, k):
            return jnp.clip(j - first, 0, steps - 1), jnp.where((j >= first) & (j < first + steps), k, 0), 0
        return pl.BlockSpec((n, SEQ // halves, piece.shape[2]), index)

    tile_spec = pl.BlockSpec((W_BLOCK, D_MODEL), lambda j, k: (j, 0))
    in_specs = [cols_spec(p, *pl_) for p, pl_ in zip(d_proj, plan)] + [
        pl.BlockSpec((SEQ // halves, D_MODEL), lambda j, k: (k, 0))]
    acc = pltpu.VMEM((W_BLOCK, D_MODEL), F32)
    if not hand_over:
        return pl.pallas_call(
            body, name="d_w_in", grid=(step, halves), in_specs=in_specs, out_specs=tile_spec,
            out_shape=jax.ShapeDtypeStruct((IN_WIDTH, D_MODEL), BF16), scratch_shapes=[acc],
            compiler_params=_params("arbitrary", "arbitrary"),
        )(*d_proj, h)
    return pl.pallas_call(
        body, name="d_w_in", grid=(step, halves), in_specs=in_specs + [ANY], out_specs=[tile_spec, ANY, ANY],
        out_shape=[jax.ShapeDtypeStruct((IN_WIDTH, D_MODEL), BF16),
                   jax.ShapeDtypeStruct((N_CHIPS, half, D_MODEL), BF16),
                   jax.ShapeDtypeStruct((N_CHIPS, rest.shape[1] // 2, D_MODEL), BF16)],
        scratch_shapes=[acc, pltpu.SemaphoreType.DMA((3,)), pltpu.SemaphoreType.DMA((2,)),
                        pltpu.VMEM((2, W_BLOCK, D_MODEL), BF16)],
        compiler_params=_params("arbitrary", "arbitrary"),
    )(*d_proj, h, rest)


RELAY_STEP = 10
RELAY_ROWS = 352


def _d_x(d_proj, w_t, x, gain, dy, chip_sums):
    tm = 256
    n_steps = SEQ // tm
    n_w = IN_WIDTH // W_BLOCK
    n_p, n_s = len(d_proj), len(chip_sums)

    def body(*refs):
        pieces, w_refs = refs[:n_p], refs[n_p:n_p + n_w]
        x_ref, g_ref, dy_ref = refs[n_p + n_w:n_p + n_w + 3]
        q_refs = refs[n_p + n_w + 3:n_p + n_w + 3 + n_s]
        dx_ref, dgain_ref = refs[n_p + n_w + 3 + n_s:n_p + n_w + 5 + n_s]
        outs = refs[n_p + n_w + 5 + n_s:n_p + n_w + 5 + 4 * n_s]
        got_refs, relay_refs, sum_refs = outs[:n_s], outs[n_s:2 * n_s], outs[2 * n_s:]
        if n_s:
            send_sems, recv_sems, local_sems, a_buf, b_buf, c_buf = refs[n_p + n_w + 5 + 4 * n_s:]

        def hops():
            cx, cy, c = lax.axis_index("x"), lax.axis_index("y"), lax.axis_index("c")
            near = (cx + (1 - c) - 2 * cx * (1 - c), cy + c - 2 * cy * c)
            far = (cx + c - 2 * cx * c, cy + (1 - c) - 2 * cy * (1 - c))
            chip = lambda p: 2 * p[0] + p[1]

            def copy(k, src, dst, to):
                return pltpu.make_async_remote_copy(src_ref=src, dst_ref=dst, send_sem=send_sems.at[k],
                                                    recv_sem=recv_sems.at[k], device_id=(*to, c), device_id_type=MESH)

            first = [(copy(3 * b, q.at[chip(near)], got.at[0], near),
                      copy(3 * b + 1, q.at[3 - chip((cx, cy))], relay, near))
                     for b, (q, got, relay) in enumerate(zip(q_refs, got_refs, relay_refs))]
            second = [copy(3 * b + 2, s, got.at[1], far) for b, (s, got) in enumerate(zip(sum_refs, got_refs))]
            return first, second, chip(far)

        @pl.when(pl.program_id(0) == 0)
        def _():
            dgain_ref[...] = jnp.zeros_like(dgain_ref)
            if n_s:
                for direct, pass_on in hops()[0]:
                    direct.start()
                    pass_on.start()

        if n_s:
            @pl.when(pl.program_id(0) == RELAY_STEP)
            def _():
                first, second, far_chip = hops()
                for b, (q, relay, total) in enumerate(zip(q_refs, relay_refs, sum_refs)):
                    first[b][1].wait_recv()
                    half = relay.shape[0]
                    for r0 in range(0, half, RELAY_ROWS):
                        rows = min(RELAY_ROWS, half - r0)
                        mine = pltpu.make_async_copy(q.at[far_chip, pl.ds(r0, rows), :], a_buf.at[pl.ds(0, rows), :],
                                                     local_sems.at[0])
                        theirs = pltpu.make_async_copy(relay.at[pl.ds(r0, rows), :], b_buf.at[pl.ds(0, rows), :],
                                                       local_sems.at[1])
                        mine.start()
                        theirs.start()
                        mine.wait()
                        theirs.wait()
                        c_buf[0:rows, :] = (a_buf[0:rows, :].astype(F32) + b_buf[0:rows, :].astype(F32)).astype(BF16)
                        store = pltpu.make_async_copy(c_buf.at[pl.ds(0, rows), :], total.at[pl.ds(r0, rows), :],
                                                      local_sems.at[2])
                        store.start()
                        store.wait()
                    second[b].start()

        blocks = [(piece, k) for piece in pieces for k in range(piece.shape[0])]
        dh, group, width, blk = None, [], 0, 0
        for piece, k in blocks:
            group.append(piece[k])
            width += piece.shape[2]
            if width == W_BLOCK:
                term = _dot(jnp.concatenate(group, axis=1), w_refs[blk][...])
                dh = term if dh is None else dh + term
                group, width, blk = [], 0, blk + 1
        assert not group and blk == n_w
        xf = x_ref[...]
        r = lax.rsqrt(jnp.mean(xf * xf, axis=-1, keepdims=True) + EPS)
        xh = xf * r
        dxh = dh * g_ref[...]
        dx_ref[...] = r * (dxh - xh * jnp.mean(dxh * xh, axis=-1, keepdims=True)) + dy_ref[...]
        dgain_ref[...] += _rows8(jnp.sum(dh * xh, axis=0, keepdims=True))

        if n_s:
            @pl.when(pl.program_id(0) == n_steps - 1)
            def _():
                first, second, _ = hops()
                for direct, pass_on in first:
                    direct.wait()
                    pass_on.wait_send()
                for cp in second:
                    cp.wait()

    row = pl.BlockSpec((tm, D_MODEL), lambda i: (i, 0))
    halves = [q.shape[1] for q in chip_sums]
    res = pl.pallas_call(
        body, name="d_x", grid=(n_steps,),
        in_specs=[pl.BlockSpec((p.shape[0], tm, p.shape[2]), lambda i: (0, i, 0)) for p in d_proj] + _w_blocks(0, n_w)
        + [row, pl.BlockSpec((1, D_MODEL), lambda i: (0, 0)), row] + [ANY] * n_s,
        out_specs=[row, pl.BlockSpec((8, D_MODEL), lambda i: (0, 0))] + [ANY] * (3 * n_s),
        out_shape=[jax.ShapeDtypeStruct((SEQ, D_MODEL), F32), jax.ShapeDtypeStruct((8, D_MODEL), F32)]
        + [jax.ShapeDtypeStruct((2, half, D_MODEL), BF16) for half in halves]
        + [jax.ShapeDtypeStruct((half, D_MODEL), BF16) for half in halves] * 2,
        scratch_shapes=[pltpu.SemaphoreType.DMA((3 * n_s,)), pltpu.SemaphoreType.DMA((3 * n_s,)),
                        pltpu.SemaphoreType.DMA((3,))] + [pltpu.VMEM((RELAY_ROWS, D_MODEL), BF16)] * 3 if n_s else [],
        compiler_params=_params("arbitrary"),
    )(*d_proj, *([w_t] * n_w), x, gain, dy, *chip_sums)
    return res[0], res[1], res[2:2 + n_s]


def _my_place():
    x, y, c = lax.axis_index("x"), lax.axis_index("y"), lax.axis_index("c")
    return jnp.stack([2 * x + y, c]).astype(jnp.int32)


def _half_rows(ref, half):
    rows = ref.shape[-2] // 2
    idx = (slice(None),) * (len(ref.shape) - 2) + (pl.ds(pl.multiple_of(half * rows, 16), rows), slice(None))
    return ref.at[idx]


def _add_halves(place, grads, theirs, name):
    half = theirs.shape[1]
    tr = _row_tile(half)
    n = half // tr

    def body(place_ref, g_ref, t_ref, o_ref):
        o_ref[...] = (g_ref[...].astype(F32) + t_ref[...].astype(F32)).astype(BF16)

    return pl.pallas_call(
        body, name=name,
        grid_spec=pltpu.PrefetchScalarGridSpec(
            num_scalar_prefetch=1, grid=(N_CHIPS, n),
            in_specs=[pl.BlockSpec((None, tr, D_MODEL), lambda s, i, p: (s, p[1] * n + i, 0)),
                      pl.BlockSpec((None, tr, D_MODEL), lambda s, i, p: (s, i, 0))],
            out_specs=pl.BlockSpec((None, tr, D_MODEL), lambda s, i, p: (s, i, 0))),
        out_shape=jax.ShapeDtypeStruct((N_CHIPS, half, D_MODEL), BF16),
        compiler_params=_params("arbitrary", "arbitrary"),
    )(place, grads, theirs)


def _add_chips(place, chip_sums, others, name):
    half = others.shape[1]
    tr = _row_tile(half)
    n = half // tr

    def body(place_ref, q_ref, o_ref, r_ref):
        acc = q_ref[...].astype(F32)
        for j in range(others.shape[0]):
            acc = acc + o_ref[j].astype(F32)
        r_ref[...] = acc

    return pl.pallas_call(
        body, name=name,
        grid_spec=pltpu.PrefetchScalarGridSpec(
            num_scalar_prefetch=1, grid=(n,),
            in_specs=[pl.BlockSpec((None, tr, D_MODEL), lambda i, p: (p[0], i, 0)),
                      pl.BlockSpec((others.shape[0], tr, D_MODEL), lambda i, p: (0, i, 0))],
            out_specs=pl.BlockSpec((tr, D_MODEL), lambda i, p: (p[1] * n + i, 0))),
        out_shape=jax.ShapeDtypeStruct((2 * half, D_MODEL), F32),
        compiler_params=_params("arbitrary"),
    )(place, chip_sums, others)


def _join_halves(shards, block):
    n = len(shards)
    rows = block.shape[0]

    def body(*refs):
        b_ref, o_refs, sum_ref = refs[n], refs[n + 1:2 * n + 1], refs[2 * n + 1]
        send_sems, recv_sems, small_send, small_recv, local_sem, all_ref = refs[2 * n + 2:]
        x, y, c = lax.axis_index("x"), lax.axis_index("y"), lax.axis_index("c")
        me, sibling = (x, y, c), (x, y, 1 - c)
        chips = [(1 - x, y), (x, 1 - y), (1 - x, 1 - y)]

        def half(k, rows_ref):
            return pltpu.make_async_remote_copy(src_ref=rows_ref, dst_ref=rows_ref, send_sem=send_sems.at[k],
                                                recv_sem=recv_sems.at[k], device_id=sibling, device_id_type=MESH)

        def at(px, py, pc):
            return all_ref.at[pl.ds(pl.multiple_of((4 * px + 2 * py + pc) * rows, 8), rows), :]

        def small(k, block_of, to, src=None):
            return pltpu.make_async_remote_copy(src_ref=at(*block_of) if src is None else src, dst_ref=at(*block_of),
                                                send_sem=small_send.at[k], recv_sem=small_recv.at[k],
                                                device_id=to, device_id_type=MESH)

        sends = [half(k, _half_rows(o, c)) for k, o in enumerate(o_refs)]
        for cp in sends:
            cp.start()
        mine = pltpu.make_async_copy(b_ref, at(*me), local_sem)
        mine.start()
        first = [small(0, me, sibling, src=b_ref)]
        first += [small(1 + j, me, (*chip, c), src=b_ref) for j, chip in enumerate(chips)]
        for cp in first:
            cp.start()
        passed = [small(4 + j, (*chip, c), sibling) for j, chip in enumerate(chips)]
        for j, chip in enumerate(chips):
            small(1 + j, (*chip, c), me).wait_recv()
            passed[j].start()
        small(0, sibling, me).wait_recv()
        for j, chip in enumerate(chips):
            small(4 + j, (*chip, 1 - c), me).wait_recv()
        mine.wait()
        acc = all_ref[0:rows, :]
        for dev in range(1, 8):
            acc = acc + all_ref[rows * dev:rows * (dev + 1), :]
        sum_ref[...] = acc
        for k, o in enumerate(o_refs):
            half(k, _half_rows(o, 1 - c)).wait_recv()
        for cp in sends + first + passed:
            cp.wait_send()

    res = pl.pallas_call(
        body, name="reduce_join_halves", in_specs=[ANY] * n + [pl.BlockSpec(memory_space=pltpu.VMEM)],
        out_specs=[ANY] * n + [pl.BlockSpec(memory_space=pltpu.VMEM)],
        out_shape=[jax.ShapeDtypeStruct(s.shape, F32) for s in shards] + [jax.ShapeDtypeStruct(block.shape, F32)],
        input_output_aliases={k: k for k in range(n)},
        scratch_shapes=[pltpu.SemaphoreType.DMA((n,)), pltpu.SemaphoreType.DMA((n,)),
                        pltpu.SemaphoreType.DMA((7,)), pltpu.SemaphoreType.DMA((7,)), pltpu.SemaphoreType.DMA,
                        pltpu.VMEM((8 * rows, D_MODEL), F32)],
    )(*shards, block)
    return res[:n], res[n]


def _adamw_math(w, g, m, v):
    m = ADAM_B1 * m + (1.0 - ADAM_B1) * g
    v = ADAM_B2 * v + (1.0 - ADAM_B2) * (g * g)
    m_hat = m / (1.0 - ADAM_B1 ** ADAM_STEP)
    v_hat = v / (1.0 - ADAM_B2 ** ADAM_STEP)
    return -ADAM_LR * (m_hat / (jnp.sqrt(v_hat) + ADAM_EPS) + ADAM_WD * w), m, v


def _adamw(w, g, m, v, name):
    r, c = w.shape
    tr = _row_tile(r)

    def body(w_ref, g_ref, m_ref, v_ref, d_ref, nm_ref, nv_ref):
        d_ref[...], nm_ref[...], nv_ref[...] = _adamw_math(w_ref[...], g_ref[...], m_ref[...], v_ref[...])

    spec = pl.BlockSpec((tr, c), lambda i: (i, 0))
    return pl.pallas_call(
        body, name=name, grid=(r // tr,), in_specs=[spec] * 4, out_specs=[spec] * 3,
        out_shape=[jax.ShapeDtypeStruct((r, c), F32)] * 3, compiler_params=_params("arbitrary"),
    )(w, g, m, v)


def _adamw_small(ws, gs, ms, vs):
    n = len(ws)

    def body(*refs):
        ins, outs = refs[:4 * n], refs[4 * n:]
        for k in range(n):
            d, m, v = _adamw_math(ins[k][...], ins[n + k][...], ins[2 * n + k][...], ins[3 * n + k][...])
            outs[k][...], outs[n + k][...], outs[2 * n + k][...] = d, m, v

    shapes = [jax.ShapeDtypeStruct(w.shape, F32) for w in ws]
    res = pl.pallas_call(body, name="adamw_small", out_shape=shapes * 3)(*ws, *gs, *ms, *vs)
    return res[:n], res[n:2 * n], res[2 * n:]


def _fold_heads(partials):
    t = jnp.sum(partials[:, 0, :], axis=0)
    return (t[:HEAD_DIM] + t[HEAD_DIM:]).reshape(1, HEAD_DIM)


def _local_step(x, target, norm_gain, w_t, w_a, w_b, w_o, b_m, q_norm_a, k_norm_a, q_norm_b, k_norm_b, sink_a,
                rel_bias, start_reduce=None, small_shard=None):
    two = lambda gain: jnp.concatenate([gain, gain], axis=1)
    bias_a = _bias_lines(rel_bias[:, :8], A_HALF_WINDOW, 1)
    bias_b = jnp.concatenate([_bias_lines(rel_bias[:, 8 + 8 * g:16 + 8 * g], B_HALF_WINDOW, d)
                              for g, d in enumerate(B_DILATIONS)], axis=0)

    qkv, h, *small_all = _in_proj(x, norm_gain, w_t, 0, QKV_WIDTH // W_BLOCK, BF16, "in_proj_qkv", True, small_shard)
    if small_shard is not None:
        w_a, w_b, w_o, b_m = _unpack_weights(small_all[0])
    gates, = _in_proj(x, norm_gain, w_t, QKV_WIDTH // W_BLOCK, GATE_WIDTH // W_BLOCK, F32, "in_proj_gates", False)
    out_a, lse_a = _attn_a_fwd(qkv, two(q_norm_a), two(k_norm_a), bias_a, sink_a)
    out_b, lse_b = _attn_b_fwd(qkv, two(q_norm_b), two(k_norm_b), bias_b)

    dy, dgates, d_out_a, d_out_b, delta_a, delta_b, d_wa, d_wb, d_wo, d_bm, sq = _middle(
        out_a, out_b, gates, x, target, w_a, w_b, w_o, b_m)
    loss = (0.5 / D_MODEL) * jnp.sum(sq)

    dq_a, dkv_a, dgq_a, dgk_a, ds_a, dsink = _attn_a_bwd(
        qkv, two(q_norm_a), two(k_norm_a), bias_a, sink_a, delta_a, lse_a, d_out_a)
    dq_b, dk_b, dv_b, dgq_b, dgk_b, ds_b = _attn_b_bwd(
        qkv, two(q_norm_b), two(k_norm_b), bias_b, delta_b, lse_b, d_out_b)
    d_proj = (dq_a, dkv_a, dq_b, dk_b, dv_b, dgates)

    d_bm_rows = jnp.pad(d_bm.reshape(2, N_CHIPS, 256).transpose(1, 0, 2),
                        ((0, 0), (0, REST_ROWS - 514), (0, D_MODEL - 256)))
    rest = jnp.concatenate([d_wo.reshape(N_CHIPS, 256, D_MODEL), d_wa.reshape(N_CHIPS, 128, D_MODEL),
                            d_wb.reshape(N_CHIPS, 128, D_MODEL), d_bm_rows], axis=1)
    if start_reduce is None:
        grads, chip_sums = [_d_w_in(d_proj, h).reshape(N_CHIPS, W_IN_SHARD, D_MODEL), rest], []
    else:
        d_wt, *theirs = _d_w_in(d_proj, h, rest.astype(BF16))
        grads = [d_wt.reshape(N_CHIPS, W_IN_SHARD, D_MODEL), rest]
        chip_sums = start_reduce(grads, theirs)
    grad_x, d_gain, others = _d_x(d_proj, w_t, x, norm_gain, dy, chip_sums)

    d_rel = jnp.concatenate(
        [_bias_grad(ds_a, A_HALF_WINDOW, 1)]
        + [_bias_grad(ds_b[4 * g:4 * g + 4], B_HALF_WINDOW, d) for g, d in enumerate(B_DILATIONS)], axis=1)
    d_sink = jnp.sum(dsink, axis=(2, 3)).reshape(1, 8)
    dgk_a_row = dgk_a[0]
    small = jnp.zeros((8, D_MODEL), F32)
    small = small.at[0].set(d_gain[0])
    small = small.at[1].set(d_rel.reshape(-1))
    misc = jnp.concatenate([_fold_heads(dgq_a), (dgk_a_row[:HEAD_DIM] + dgk_a_row[HEAD_DIM:]).reshape(1, HEAD_DIM),
                            _fold_heads(dgq_b), _fold_heads(dgk_b), d_sink], axis=1)
    small = small.at[2, :264].set(misc[0])

    return loss, grad_x, grads, small, chip_sums, others


def _unpack_weights(small_all):
    sm = small_all.reshape(N_CHIPS, SMALL_ROWS, D_MODEL)
    w_o = sm[:, 0:256].reshape(D_MODEL, D_MODEL)
    w_a = sm[:, 256:384].reshape(N_CHIPS, 512, 256).transpose(1, 0, 2).reshape(512, D_MODEL)
    w_b = sm[:, 384:512].reshape(N_CHIPS, 512, 256).transpose(1, 0, 2).reshape(512, D_MODEL)
    b_m = lax.bitcast_convert_type(sm[:, 512].reshape(N_CHIPS, 2, 256, 2), F32)
    return w_a, w_b, w_o, b_m.transpose(1, 0, 2).reshape(2, D_MODEL)


def _pack_small_weights(w_branch_a, w_branch_b, b_merge, w_out):
    b_m = jnp.pad(lax.bitcast_convert_type(b_merge, BF16).reshape(1, D_MODEL), ((0, SMALL_ROWS - 513), (0, 0)))
    return jnp.concatenate([w_out.astype(BF16), w_branch_a.astype(BF16).reshape(128, D_MODEL),
                            w_branch_b.astype(BF16).reshape(128, D_MODEL), b_m], axis=0)


def kernel(x, norm_gain, w_in, q_norm_a, k_norm_a, q_norm_b, k_norm_b, sink_a, rel_bias, w_branch_a, w_branch_b, b_merge, w_out, loss_target, m_norm_gain, m_w_in, m_q_norm_a, m_k_norm_a, m_q_norm_b, m_k_norm_b, m_sink_a, m_rel_bias, m_w_branch_a, m_w_branch_b, m_b_merge, m_w_out, v_norm_gain, v_w_in, v_q_norm_a, v_k_norm_a, v_q_norm_b, v_k_norm_b, v_sink_a, v_rel_bias, v_w_branch_a, v_w_branch_b, v_b_merge, v_w_out):
    w_in_t, m_w_in_t, v_w_in_t = (jnp.transpose(t[0]) for t in (w_in, m_w_in, v_w_in))
    wt_shard = _cast_rows(w_in_t, BF16, "w_in_cast")
    w_t = _gather_weights(wt_shard)
    small_shard = _pack_small_weights(w_branch_a[0], w_branch_b[0], b_merge[0], w_out[0])

    place = _my_place()
    names = ("w_in", "rest")

    def start_reduce(grads, theirs):
        return [_add_halves(place, g, t, "reduce_add_halves_" + n) for g, t, n in zip(grads, theirs, names)]

    loss_part, grad_x, _, small, chip_sums, others = _local_step(
        x[0], loss_target[0], norm_gain, w_t, None, None, None, None, q_norm_a, k_norm_a, q_norm_b, k_norm_b,
        sink_a, rel_bias, start_reduce, small_shard)

    (g_wt, g_rest), small = _join_halves(
        [_add_chips(place, q, o, "reduce_add_chips_" + n) for q, o, n in zip(chip_sums, others, names)],
        small.at[3, 0].set(loss_part))
    loss = small[3, 0]

    g_w_out = g_rest[0:256]
    g_w_a = g_rest[256:384].reshape(512, 256)
    g_w_b = g_rest[384:512].reshape(512, 256)
    g_b_merge = g_rest[512:514, :256]
    g_norm_gain = small[0:1]
    g_rel_bias = small[1].reshape(N_BUCKETS, N_BUCKETS)
    g_q_a, g_k_a, g_q_b, g_k_b = (small[2:3, 64 * k:64 * k + 64] for k in range(4))
    g_sink = small[2:3, 256:264]

    big_names = (("w_branch_a", w_branch_a, g_w_a, m_w_branch_a, v_w_branch_a),
                 ("w_branch_b", w_branch_b, g_w_b, m_w_branch_b, v_w_branch_b),
                 ("w_out", w_out, g_w_out, m_w_out, v_w_out))
    upd = {name: (g,) + tuple(_adamw(w[0], g, m[0], v[0], "adamw_" + name)) for name, w, g, m, v in big_names}
    upd["w_in"] = tuple(jnp.transpose(t) for t in (g_wt,) + tuple(_adamw(w_in_t, g_wt, m_w_in_t, v_w_in_t, "adamw_w_in")))
    small_names = ("norm_gain", "q_norm_a", "k_norm_a", "q_norm_b", "k_norm_b", "sink_a", "rel_bias", "b_merge")
    ws = [norm_gain, q_norm_a, k_norm_a, q_norm_b, k_norm_b, sink_a, rel_bias, b_merge[0]]
    gs = [g_norm_gain, g_q_a, g_k_a, g_q_b, g_k_b, g_sink, g_rel_bias, g_b_merge]
    ms = [m_norm_gain, m_q_norm_a, m_k_norm_a, m_q_norm_b, m_k_norm_b, m_sink_a, m_rel_bias, m_b_merge[0]]
    vs = [v_norm_gain, v_q_norm_a, v_k_norm_a, v_q_norm_b, v_k_norm_b, v_sink_a, v_rel_bias, v_b_merge[0]]
    ds, nms, nvs = _adamw_small(ws, gs, ms, vs)
    for k, name in enumerate(small_names):
        upd[name] = (gs[k], ds[k], nms[k], nvs[k])

    order = ("norm_gain", "w_in", "q_norm_a", "k_norm_a", "q_norm_b", "k_norm_b", "sink_a", "rel_bias",
             "w_branch_a", "w_branch_b", "b_merge", "w_out")
    lead = {"w_in", "w_branch_a", "w_branch_b", "b_merge", "w_out"}
    outs = [loss, grad_x[None]]
    for part in range(4):
        outs += [upd[name][part][None] if name in lead else upd[name][part] for name in order]
    return tuple(outs)
```

```python
import math

import numpy as np
import jax
import jax.numpy as jnp
from jax import lax
from jax.experimental import pallas as pl
from jax.experimental.pallas import tpu as pltpu

F32 = jnp.float32
BF16 = jnp.bfloat16

SEQ = 4096
D_MODEL = 1024
HEAD_DIM = 64
LANES = 128
EPS = 1e-6
NEG_INF = -1e30
SCALE = HEAD_DIM ** -0.5
N_BUCKETS = 32
MAX_DISTANCE = 1024
N_CHIPS = 4

A_HALF_WINDOW = 128
B_HALF_WINDOW = 64
B_DILATIONS = (1, 4, 16)
Q_BLOCK = 128

QKV_WIDTH = 5376
GATE_WIDTH = 3072
QA_BLK, KA_BLK, VA_BLK = 0, 4, 5
QB_BLK, KB_BLK, VB_BLK = 6, 18, 30
IN_WIDTH = QKV_WIDTH + GATE_WIDTH
W_IN_SHARD = IN_WIDTH // N_CHIPS

SMALL_ROWS = 544
REST_ROWS = 544

ADAM_LR = 0.001
ADAM_B1 = 0.9
ADAM_B2 = 0.999
ADAM_EPS = 1e-08
ADAM_WD = 0.01
ADAM_STEP = 10

VMEM_LIMIT = 56 * 1024 * 1024

NT = (((1,), (1,)), ((), ()))
TN = (((0,), (0,)), ((), ()))
MESH = pl.DeviceIdType.MESH
ANY = pl.BlockSpec(memory_space=pl.ANY)


def _dot(a, b, dims=None):
    if dims is None:
        return jnp.dot(a, b, preferred_element_type=F32)
    return lax.dot_general(a, b, dims, preferred_element_type=F32)


def _params(*semantics):
    return pltpu.CompilerParams(dimension_semantics=semantics or None, vmem_limit_bytes=VMEM_LIMIT)


def _line_width(half_window):
    return pl.cdiv(2 * Q_BLOCK + 2 * half_window - 1, LANES) * LANES


def _bucket_onehot(half_window, stride):
    rel = np.arange(_line_width(half_window)) - (Q_BLOCK - 1) - half_window
    band = np.abs(rel) <= half_window
    rel = rel * stride
    half, max_exact = N_BUCKETS // 2, N_BUCKETS // 4
    n = np.abs(rel)
    nf = np.maximum(n, max_exact).astype(np.float32)
    large = max_exact + (np.log(nf / np.float32(max_exact)) / np.float32(math.log(MAX_DISTANCE / max_exact))
                         * np.float32(half - max_exact)).astype(np.int32)
    large = np.minimum(large, half - 1)
    bucket = (rel > 0).astype(np.int32) * half + np.where(n < max_exact, n, large)
    onehot = (bucket[..., None] == np.arange(N_BUCKETS)) & band[..., None]
    return onehot.astype(np.float32), band


def _bias_lines(rel_bias_cols, half_window, stride):
    onehot, band = _bucket_onehot(half_window, stride)
    h = rel_bias_cols.shape[1]
    t = jnp.einsum("tb,bh->ht", jnp.asarray(onehot), rel_bias_cols, precision=lax.Precision.HIGHEST)
    t = t + jnp.asarray(np.where(band, 0.0, NEG_INF).astype(np.float32))
    return t.reshape(h // 2, 2, -1)


def _bias_grad(d_lines, half_window, stride):
    onehot, _ = _bucket_onehot(half_window, stride)
    h = d_lines.shape[0] * 2
    return jnp.einsum("tb,ht->bh", jnp.asarray(onehot), d_lines.reshape(h, -1), precision=lax.Precision.HIGHEST)


def _unroll_bias(line_ref, tile_ref, w):
    width = line_ref.shape[1]
    for j in range(2):
        rows = jnp.broadcast_to(line_ref[j:j + 1, :], (Q_BLOCK, width))
        rows = pltpu.roll(rows, width - (Q_BLOCK - 1), 1, stride=1, stride_axis=0)
        tile_ref[j * Q_BLOCK:(j + 1) * Q_BLOCK, :] = rows[:, :w]


def _fold_bias_grad(tile_ref, line_ref, w):
    width = line_ref.shape[1]
    row = lax.broadcasted_iota(jnp.int32, (Q_BLOCK, Q_BLOCK), 0)
    col = lax.broadcasted_iota(jnp.int32, (Q_BLOCK, Q_BLOCK), 1)
    flip = jnp.where(row + col == Q_BLOCK - 1, 1.0, 0.0).astype(BF16)
    for j in range(2):
        tile = tile_ref[j * Q_BLOCK:(j + 1) * Q_BLOCK, :]
        hi = tile.astype(BF16)
        lo = (tile - hi.astype(F32)).astype(BF16)
        rows = _dot(flip, hi) + _dot(flip, lo)
        rows = jnp.concatenate([rows, jnp.zeros((Q_BLOCK, width - w), F32)], axis=1)
        rows = pltpu.roll(rows, 0, 1, stride=1, stride_axis=0)
        line_ref[j:j + 1, :] = jnp.sum(rows, axis=0, keepdims=True)


def _row_tile(rows):
    return max(t for t in range(16, 385, 16) if rows % t == 0)


def _cast_rows(w, out_dtype, name):
    r, c = w.shape
    tr = _row_tile(r)

    def body(w_ref, o_ref):
        o_ref[...] = w_ref[...].astype(out_dtype)

    spec = pl.BlockSpec((tr, c), lambda i: (i, 0))
    return pl.pallas_call(
        body, name=name, grid=(r // tr,), in_specs=[spec], out_specs=spec,
        out_shape=jax.ShapeDtypeStruct((r, c), out_dtype), compiler_params=_params("arbitrary"),
    )(w)


STAGE_ROWS = 528


def _gather_scratch():
    return [pltpu.SemaphoreType.DMA((12,)), pltpu.SemaphoreType.DMA((12,)), pltpu.SemaphoreType.DMA((2,)),
            pltpu.SemaphoreType.DMA((2,)), pltpu.VMEM((2, STAGE_ROWS, D_MODEL), BF16)]


def _gather_phases(src_ref, out_ref, send_sems, recv_sems, in_sems, out_sems, stage):
    rows = src_ref.shape[0]
    x, y, c = lax.axis_index("x"), lax.axis_index("y"), lax.axis_index("c")
    sibling = (x, y, 1 - c)
    near = (x + (1 - c) - 2 * x * (1 - c), y + c - 2 * y * c)
    far = (x + c - 2 * x * c, y + (1 - c) - 2 * y * (1 - c))
    diag = (1 - x, 1 - y)
    chip_no = lambda chip: 2 * chip[0] + chip[1]
    my_chip = chip_no((x, y))

    pieces = 2 if (rows // 2) % 32 == 0 else 1
    n = rows // 2 // pieces

    def half_of(chip, half, p):
        start = pl.multiple_of(chip * rows + half * (rows // 2) + p * n, 16)
        return out_ref.at[pl.ds(start, n), :]

    def copy(k, p, src, dst, to):
        return pltpu.make_async_remote_copy(src_ref=src, dst_ref=dst, send_sem=send_sems.at[k * pieces + p],
                                            recv_sem=recv_sems.at[k * pieces + p], device_id=to, device_id_type=MESH)

    def mine(p):
        return src_ref.at[pl.ds(pl.multiple_of(c * (rows // 2) + p * n, 16), n), :]

    def keep_own():
        outs = []
        for i, r0 in enumerate(range(0, rows, STAGE_ROWS)):
            n = min(STAGE_ROWS, rows - r0)
            slot = i % 2
            if i >= 2:
                outs[i - 2].wait()
            buf = stage.at[slot, pl.ds(0, n), :]
            load = pltpu.make_async_copy(src_ref.at[pl.ds(r0, n), :], buf, in_sems.at[slot])
            load.start()
            load.wait()
            start = pl.multiple_of(my_chip * rows + r0, 16)
            outs.append(pltpu.make_async_copy(buf, out_ref.at[pl.ds(start, n), :], out_sems.at[slot]))
            outs[i].start()
        for cp in outs[-2:]:
            cp.wait()

    def start():
        for p in range(pieces):
            copy(0, p, mine(p), half_of(my_chip, c, p), (*near, c)).start()
            copy(1, p, mine(p), half_of(my_chip, c, p), (*far, c)).start()
        keep_own()

    def pass_on(j, p, chip):
        landed = half_of(chip_no(chip), c, p)
        copy(3 + j, p, landed, landed, sibling).start()

    def relay():
        for p in range(pieces):
            landed = half_of(chip_no(near), c, p)
            copy(0, p, landed, landed, sibling).wait_recv()
            copy(2, p, landed, landed, (*far, c)).start()
            pass_on(0, p, near)

    def forward():
        for j, chip in ((1, far), (2, diag)):
            for p in range(pieces):
                landed = half_of(chip_no(chip), c, p)
                copy(j, p, landed, landed, sibling).wait_recv()
                pass_on(j, p, chip)

    def finish():
        for j, chip in ((0, far), (1, near), (2, diag)):
            for p in range(pieces):
                other = half_of(chip_no(chip), 1 - c, p)
                copy(3 + j, p, other, other, sibling).wait_recv()
        for k in range(6):
            for p in range(pieces):
                copy(k, p, mine(p), mine(p), sibling).wait_send()

    return start, relay, forward, finish


def _gather_weights(shard):
    def body(src_ref, out_ref, *scratch):
        for phase in _gather_phases(src_ref, out_ref, *scratch):
            phase()

    return pl.pallas_call(
        body, name="gather_weights", in_specs=[ANY], out_specs=ANY,
        out_shape=jax.ShapeDtypeStruct((N_CHIPS * shard.shape[0], D_MODEL), BF16),
        scratch_shapes=_gather_scratch(),
    )(shard)


W_BLOCK = 768


def _w_blocks(first, count):
    return [pl.BlockSpec((W_BLOCK, D_MODEL), lambda *_, k=k: (first + k, 0)) for k in range(count)]


def _in_proj(x, gain, w_t, first_block, n_blocks, out_dtype, name, keep_h, ride=None):
    tm = 512
    n_steps = SEQ // tm
    n_out = 2 if keep_h else 1

    def body(x_ref, g_ref, *refs):
        w_refs, outs = refs[:n_blocks], refs[n_blocks + (ride is not None):n_blocks + (ride is not None) + n_out]
        if ride is not None:
            phases = _gather_phases(refs[n_blocks], *refs[n_blocks + 1 + n_out:])
            for step, phase in zip((0, 2, 4, n_steps - 1), phases):
                pl.when(pl.program_id(0) == step)(phase)
        xf = x_ref[...]
        r = lax.rsqrt(jnp.mean(xf * xf, axis=-1, keepdims=True) + EPS)
        h = ((xf * r) * g_ref[...]).astype(BF16)
        if keep_h:
            outs[1][...] = h
        for k, w_ref in enumerate(w_refs):
            outs[0][:, k * W_BLOCK:(k + 1) * W_BLOCK] = _dot(h, w_ref[...], NT).astype(out_dtype)

    riding = [] if ride is None else [ride]
    return pl.pallas_call(
        body, name=name, grid=(n_steps,),
        in_specs=[pl.BlockSpec((tm, D_MODEL), lambda i: (i, 0)), pl.BlockSpec((1, D_MODEL), lambda i: (0, 0))]
        + _w_blocks(first_block, n_blocks) + [ANY for _ in riding],
        out_specs=[pl.BlockSpec((tm, W_BLOCK * n_blocks), lambda i: (i, 0)),
                   pl.BlockSpec((tm, D_MODEL), lambda i: (i, 0))][:n_out] + [ANY for _ in riding],
        out_shape=[jax.ShapeDtypeStruct((SEQ, W_BLOCK * n_blocks), out_dtype),
                   jax.ShapeDtypeStruct((SEQ, D_MODEL), BF16)][:n_out]
        + [jax.ShapeDtypeStruct((N_CHIPS * r.shape[0], D_MODEL), BF16) for r in riding],
        scratch_shapes=_gather_scratch() if riding else [],
        compiler_params=_params("arbitrary"),
    )(x, gain, *([w_t] * n_blocks), *riding)


CHUNK = 256
CHUNK_UNROLL = 8
TILE_UNROLL = 8


def _low_half():
    return lax.broadcasted_iota(jnp.int32, (1, LANES), 1) < HEAD_DIM


def _half_sum(v, low):
    del low
    row = lax.broadcasted_iota(jnp.int32, (2 * LANES, LANES), 0)
    col = lax.broadcasted_iota(jnp.int32, (2 * LANES, LANES), 1)
    ones = jnp.where((row % LANES) // HEAD_DIM == col // HEAD_DIM, 1.0, 0.0).astype(BF16)
    hi = v.astype(BF16)
    lo = (v - hi.astype(F32)).astype(BF16)
    return _dot(jnp.concatenate([hi, lo], axis=1), ones)


def _chunks(fn, init=0):
    def body(i, carry):
        for u in range(CHUNK_UNROLL):
            carry = fn(pl.multiple_of((i * CHUNK_UNROLL + u) * CHUNK, CHUNK), carry)
        return carry

    return lax.fori_loop(0, SEQ // (CHUNK * CHUNK_UNROLL), body, init)


def _inv_rms(t, low):
    del low
    row = lax.broadcasted_iota(jnp.int32, (LANES, LANES), 0)
    col = lax.broadcasted_iota(jnp.int32, (LANES, LANES), 1)
    ones = jnp.where(row // HEAD_DIM == col // HEAD_DIM, 1.0, 0.0).astype(BF16)
    return lax.rsqrt(_dot((t * t).astype(BF16), ones) * (1.0 / HEAD_DIM) + EPS)


def _prep_q(q_ref, gain_ref, qn_ref):
    low = _low_half()

    def step(r0, carry):
        q = q_ref[pl.ds(r0, CHUNK), :].astype(F32)
        qn_ref[pl.ds(r0, CHUNK), :] = ((q * _inv_rms(q, low)) * gain_ref[...]) * SCALE
        return carry

    _chunks(step)


def _own_half(t, keep):
    return jnp.where(keep, t, pltpu.roll(t, HEAD_DIM, 1))


def _prep_kv(k_ref, v_ref, gain_ref, kp_ref, vp_ref, pad, keep=None):
    low = _low_half()
    zeros = jnp.zeros((pad, LANES), F32)
    for ref in (kp_ref, vp_ref):
        ref[pl.ds(0, pad), :] = zeros
        ref[pl.ds(pad + SEQ, pad), :] = zeros

    def step(r0, carry):
        k = k_ref[pl.ds(r0, CHUNK), :].astype(F32)
        v = v_ref[pl.ds(r0, CHUNK), :].astype(F32)
        kn = (k * _inv_rms(k, low)) * gain_ref[...]
        if keep is not None:
            kn, v = _own_half(kn, keep), _own_half(v, keep)
        kp_ref[pl.ds(pad + r0, CHUNK), :] = kn
        vp_ref[pl.ds(pad + r0, CHUNK), :] = v
        return carry

    _chunks(step)


def _tiles(d, half_window, fn, split_rows=0):
    w = Q_BLOCK + 2 * half_window
    length = SEQ // d
    n_blocks = length // Q_BLOCK
    col = lax.broadcasted_iota(jnp.int32, (1, w), 1)

    def step(it, carry):
        c, n = it // n_blocks, it % n_blocks
        start = c + (d * Q_BLOCK) * n
        if d == 1:
            start = pl.multiple_of(start, Q_BLOCK)
            q_rows, k_rows = pl.ds(start, Q_BLOCK), pl.ds(start, w)
        else:
            q_rows, k_rows = pl.ds(start, Q_BLOCK, stride=d), pl.ds(start, w, stride=d)
        t = n * Q_BLOCK - half_window + col
        edge = jnp.where((t < 0) | (t >= length), NEG_INF, 0.0)
        own = start + (n % 2) * split_rows
        own_rows = pl.ds(pl.multiple_of(own, Q_BLOCK), w) if d == 1 else pl.ds(own, w, stride=d)
        fn(q_rows, k_rows, edge, own_rows)
        return carry

    lax.fori_loop(0, d * n_blocks, step, 0, unroll=TILE_UNROLL)


def _stack_heads(t, low):
    return jnp.concatenate([jnp.where(low, t, 0.0), jnp.where(low, 0.0, t)], axis=0).astype(BF16)


def _unstack_heads(t, low):
    return jnp.where(low, t[:Q_BLOCK], t[Q_BLOCK:])


def _per_head(pair):
    return jnp.concatenate([jnp.full((Q_BLOCK, 1), pair[0], F32), jnp.full((Q_BLOCK, 1), pair[1], F32)], axis=0)


def _fwd_tiles(qn_ref, kp_ref, vp_ref, bias_ref, emit, *, d, half_window, sinks=None):
    low = _low_half()
    w = Q_BLOCK + 2 * half_window
    sink = None if sinks is None else _per_head(sinks)

    def tile(q_rows, k_rows, edge, _):
        q2 = _stack_heads(qn_ref[q_rows, :], low)
        k = kp_ref[k_rows, :].astype(BF16)
        v1 = jnp.concatenate([vp_ref[k_rows, :], jnp.ones((w, LANES), F32)], axis=1).astype(BF16)
        s = _dot(q2, k, NT) + bias_ref[...] + edge
        m = jnp.max(s, axis=-1, keepdims=True)
        if sink is not None:
            m = jnp.maximum(m, sink)
        o = _dot(jnp.exp(s - m).astype(BF16), v1)
        l = o[:, LANES:]
        if sink is not None:
            l = l + jnp.exp(sink - m)
        emit(q_rows, _unstack_heads(o[:, :LANES] * (1.0 / l), low), _unstack_heads(m + jnp.log(l), low))

    _tiles(d, half_window, tile)


def _bwd_tiles(qn_ref, kp_ref, vp_ref, bias_ref, do_ref, lse_ref, delta_ref, dq_ref, dk_ref, dv_ref, ds_ref,
               *, d, half_window, sinks=None, dsink_ref=None, split_rows=0):
    low = _low_half()
    w = Q_BLOCK + 2 * half_window
    sink = None if sinks is None else _per_head(sinks)

    def rows_of(t):
        return jnp.concatenate([t[:, 0:1], t[:, HEAD_DIM:HEAD_DIM + 1]], axis=0)

    def tile(q_rows, k_rows, edge, own_rows):
        q2 = _stack_heads(qn_ref[q_rows, :], low)
        do2 = _stack_heads(do_ref[q_rows, :], low)
        k = kp_ref[k_rows, :].astype(BF16)
        v = vp_ref[k_rows, :].astype(BF16)
        lse = rows_of(lse_ref[q_rows, :])
        delta = rows_of(delta_ref[q_rows, :])
        p = jnp.exp(_dot(q2, k, NT) + bias_ref[...] + edge - lse)
        ds = p * (_dot(do2, v, NT) - delta)
        ds_ref[...] += ds
        if sink is not None:
            dsink_ref[...] += (-jnp.exp(sink - lse) * delta).reshape(2, Q_BLOCK, 1)
        dsb, pb = ds.astype(BF16), p.astype(BF16)
        dq_ref[q_rows, :] = _unstack_heads(_dot(dsb, k), low)
        if split_rows:
            dk_ref[own_rows, :] = _dot(dsb, q2, TN)
            dv_ref[own_rows, :] = _dot(pb, do2, TN)
        else:
            dk_ref[k_rows, :] += _dot(dsb, q2, TN)
            dv_ref[k_rows, :] += _dot(pb, do2, TN)

    _tiles(d, half_window, tile, split_rows)


def _norm_bwd(raw_ref, gain_ref, dn_ref, dn_offset, out_ref, scale, second_offset=None):
    low = _low_half()

    def step(r0, dgain):
        t = raw_ref[pl.ds(r0, CHUNK), :].astype(F32)
        dn = dn_ref[pl.ds(dn_offset + r0, CHUNK), :]
        if second_offset is not None:
            dn = dn + dn_ref[pl.ds(second_offset + r0, CHUNK), :]
        dth = dn * (gain_ref[...] * scale)
        r = _inv_rms(t, low)
        th = t * r
        out_ref[pl.ds(r0, CHUNK), :] = (r * (dth - th * (r * _half_sum(dth * t, low) * (1.0 / HEAD_DIM)))).astype(BF16)
        return dgain + jnp.sum(dn * th, axis=0, keepdims=True) * scale

    return _chunks(step, jnp.zeros((1, LANES), F32))


def _rows8(v):
    return jnp.broadcast_to(v, (8, v.shape[-1]))


A_W = Q_BLOCK + 2 * A_HALF_WINDOW
A_PAD = A_HALF_WINDOW


def _seq_block(col_fn):
    return pl.BlockSpec((SEQ, LANES), col_fn)


def _attn_a_fwd(qkv, gain_q, gain_k, bias, sink):
    def body(sink_ref, q_ref, k_ref, v_ref, gq_ref, gk_ref, line_ref, o_ref, lse_ref, qn_ref, kp_ref, vp_ref,
             bias_ref):
        hp = pl.program_id(0)
        keep = (lax.broadcasted_iota(jnp.int32, (1, LANES), 1) // HEAD_DIM) == hp // 2
        _prep_q(q_ref, gq_ref, qn_ref)
        _prep_kv(k_ref, v_ref, gk_ref, kp_ref, vp_ref, A_PAD, keep)
        _unroll_bias(line_ref, bias_ref, A_W)

        def emit(rows, out, lse):
            o_ref[rows, :] = out
            lse_ref[rows, :] = lse

        _fwd_tiles(qn_ref, kp_ref, vp_ref, bias_ref, emit, d=1, half_window=A_HALF_WINDOW,
                   sinks=(sink_ref[2 * hp], sink_ref[2 * hp + 1]))

    vec = pl.BlockSpec((1, LANES), lambda hp, s: (0, 0))
    return pl.pallas_call(
        body, name="attn_a_fwd",
        grid_spec=pltpu.PrefetchScalarGridSpec(
            num_scalar_prefetch=1, grid=(4,),
            in_specs=[_seq_block(lambda hp, s: (0, QA_BLK + hp)), _seq_block(lambda hp, s: (0, KA_BLK)),
                      _seq_block(lambda hp, s: (0, VA_BLK)), vec, vec,
                      pl.BlockSpec((None, 2, _line_width(A_HALF_WINDOW)), lambda hp, s: (hp, 0, 0))],
            out_specs=[_seq_block(lambda hp, s: (0, hp)), _seq_block(lambda hp, s: (0, hp))],
            scratch_shapes=[pltpu.VMEM((SEQ, LANES), F32), pltpu.VMEM((SEQ + 2 * A_PAD, LANES), F32),
                            pltpu.VMEM((SEQ + 2 * A_PAD, LANES), F32), pltpu.VMEM((2 * Q_BLOCK, A_W), F32)]),
        out_shape=[jax.ShapeDtypeStruct((SEQ, 512), F32)] * 2,
        compiler_params=_params("arbitrary"),
    )(sink.reshape(8), qkv, qkv, qkv, gain_q, gain_k, bias)


def _attn_a_bwd(qkv, gain_q, gain_k, bias, sink, delta, lse, d_out):
    def body(sink_ref, q_ref, k_ref, v_ref, gq_ref, gk_ref, line_ref, delta_ref, lse_ref, do_ref,
             dq_out, dkv_out, dgq_out, dgk_out, dline_out, dsink_out,
             qn_ref, kp_ref, vp_ref, dq_ref, dk_ref, dv_ref, dk_tot, dv_tot, bias_ref, ds_out):
        hp = pl.program_id(0)
        kv_head = hp // 2
        keep = (lax.broadcasted_iota(jnp.int32, (1, LANES), 1) // HEAD_DIM) == kv_head
        _prep_q(q_ref, gq_ref, qn_ref)
        _prep_kv(k_ref, v_ref, gk_ref, kp_ref, vp_ref, A_PAD, keep)
        _unroll_bias(line_ref, bias_ref, A_W)
        ds_out[...] = jnp.zeros_like(ds_out)
        dsink_out[...] = jnp.zeros_like(dsink_out)

        @pl.when(hp % 2 == 0)
        def _():
            dk_ref[...] = jnp.zeros_like(dk_ref)
            dv_ref[...] = jnp.zeros_like(dv_ref)

        @pl.when(hp == 0)
        def _():
            dk_tot[...] = jnp.zeros_like(dk_tot)
            dv_tot[...] = jnp.zeros_like(dv_tot)

        _bwd_tiles(qn_ref, kp_ref, vp_ref, bias_ref, do_ref, lse_ref, delta_ref, dq_ref, dk_ref, dv_ref, ds_out,
                   d=1, half_window=A_HALF_WINDOW, sinks=(sink_ref[2 * hp], sink_ref[2 * hp + 1]),
                   dsink_ref=dsink_out)
        _fold_bias_grad(ds_out, dline_out, A_W)
        dgq_out[...] = _rows8(_norm_bwd(q_ref, gq_ref, dq_ref, 0, dq_out, SCALE))

        def fold(r0, carry):
            rows = pl.ds(A_PAD + r0, CHUNK)
            for acc, tot in ((dk_ref, dk_tot), (dv_ref, dv_tot)):
                t = acc[rows, :]
                tot[pl.ds(r0, CHUNK), :] += jnp.where(keep, t + pltpu.roll(t, HEAD_DIM, 1), 0.0)
            return carry

        @pl.when(hp % 2 == 1)
        def _():
            _chunks(fold)

        @pl.when(hp == 3)
        def _():
            dgk_out[...] = _rows8(_norm_bwd(k_ref, gk_ref, dk_tot, 0, dkv_out.at[0], 1.0))
            dkv_out[1] = dv_tot[...].astype(BF16)

    vec = pl.BlockSpec((1, LANES), lambda hp, s: (0, 0))
    seq_f32 = pltpu.VMEM((SEQ, LANES), F32)
    padded = pltpu.VMEM((SEQ + 2 * A_PAD, LANES), F32)
    return pl.pallas_call(
        body, name="attn_a_bwd",
        grid_spec=pltpu.PrefetchScalarGridSpec(
            num_scalar_prefetch=1, grid=(4,),
            in_specs=[_seq_block(lambda hp, s: (0, QA_BLK + hp)), _seq_block(lambda hp, s: (0, KA_BLK)),
                      _seq_block(lambda hp, s: (0, VA_BLK)), vec, vec,
                      pl.BlockSpec((None, 2, _line_width(A_HALF_WINDOW)), lambda hp, s: (hp, 0, 0)),
                      _seq_block(lambda hp, s: (0, hp)), _seq_block(lambda hp, s: (0, hp)),
                      _seq_block(lambda hp, s: (0, hp))],
            out_specs=[pl.BlockSpec((None, SEQ, LANES), lambda hp, s: (hp, 0, 0)),
                       pl.BlockSpec((2, SEQ, LANES), lambda hp, s: (0, 0, 0)),
                       pl.BlockSpec((None, 8, LANES), lambda hp, s: (hp, 0, 0)),
                       pl.BlockSpec((8, LANES), lambda hp, s: (0, 0)),
                       pl.BlockSpec((None, 2, _line_width(A_HALF_WINDOW)), lambda hp, s: (hp, 0, 0)),
                       pl.BlockSpec((None, 2, Q_BLOCK, 1), lambda hp, s: (hp, 0, 0, 0))],
            scratch_shapes=[seq_f32, padded, padded, seq_f32, padded, padded, seq_f32, seq_f32,
                            pltpu.VMEM((2 * Q_BLOCK, A_W), F32), pltpu.VMEM((2 * Q_BLOCK, A_W), F32)]),
        out_shape=[jax.ShapeDtypeStruct((4, SEQ, LANES), BF16), jax.ShapeDtypeStruct((2, SEQ, LANES), BF16),
                   jax.ShapeDtypeStruct((4, 8, LANES), F32), jax.ShapeDtypeStruct((8, LANES), F32),
                   jax.ShapeDtypeStruct((4, 2, _line_width(A_HALF_WINDOW)), F32),
                   jax.ShapeDtypeStruct((4, 2, Q_BLOCK, 1), F32)],
        compiler_params=_params("arbitrary"),
    )(sink.reshape(8), qkv, qkv, qkv, gain_q, gain_k, bias, delta, lse, d_out)


B_W = Q_BLOCK + 2 * B_HALF_WINDOW
B_PAD_MAX = B_HALF_WINDOW * B_DILATIONS[-1]
B_ACC_ROWS = SEQ + 2 * B_PAD_MAX


def _attn_b_fwd(qkv, gain_q, gain_k, bias):
    def body(q_ref, k_ref, v_ref, gq_ref, gk_ref, line_ref, o_ref, lse_ref, qn_ref, kp_ref, vp_ref, bias_ref,
             og_ref, lg_ref):
        g = pl.program_id(1)
        _prep_q(q_ref, gq_ref, qn_ref)
        _unroll_bias(line_ref, bias_ref, B_W)

        for gi, d in enumerate(B_DILATIONS):
            @pl.when(g == gi)
            def _(gi=gi, d=d):
                def keep(rows, out, lse):
                    og_ref.at[gi][rows, :] = out
                    lg_ref.at[gi][rows, :] = lse

                _prep_kv(k_ref, v_ref, gk_ref, kp_ref, vp_ref, B_HALF_WINDOW * d)
                _fwd_tiles(qn_ref, kp_ref, vp_ref, bias_ref, keep, d=d, half_window=B_HALF_WINDOW)

        @pl.when(g == len(B_DILATIONS) - 1)
        def _():
            def combine(r0, carry):
                rows = pl.ds(r0, CHUNK)
                lses = [lg_ref[gi, rows, :] for gi in range(len(B_DILATIONS))]
                top = jnp.maximum(jnp.maximum(lses[0], lses[1]), lses[2])
                weights = [jnp.exp(l - top) for l in lses]
                total = weights[0] + weights[1] + weights[2]
                out = sum(w * og_ref[gi, rows, :] for gi, w in enumerate(weights))
                o_ref[rows, :] = out * (1.0 / total)
                lse_ref[rows, :] = top + jnp.log(total)
                return carry

            _chunks(combine)

    vec = pl.BlockSpec((1, LANES), lambda hp, g: (0, 0))
    padded = pltpu.VMEM((SEQ + 2 * B_PAD_MAX, LANES), F32)
    return pl.pallas_call(
        body, name="attn_b_fwd", grid=(4, 3),
        in_specs=[_seq_block(lambda hp, g: (0, QB_BLK + 4 * g + hp)), _seq_block(lambda hp, g: (0, KB_BLK + 4 * g + hp)),
                  _seq_block(lambda hp, g: (0, VB_BLK + 4 * g + hp)), vec, vec,
                  pl.BlockSpec((None, 2, _line_width(B_HALF_WINDOW)), lambda hp, g: (4 * g + hp, 0, 0))],
        out_specs=[_seq_block(lambda hp, g: (0, hp)), _seq_block(lambda hp, g: (0, hp))],
        out_shape=[jax.ShapeDtypeStruct((SEQ, 512), F32)] * 2,
        scratch_shapes=[pltpu.VMEM((SEQ, LANES), F32), padded, padded, pltpu.VMEM((2 * Q_BLOCK, B_W), F32),
                        pltpu.VMEM((len(B_DILATIONS), SEQ, LANES), F32), pltpu.VMEM((len(B_DILATIONS), SEQ, LANES), F32)],
        compiler_params=_params("arbitrary", "arbitrary"),
    )(qkv, qkv, qkv, gain_q, gain_k, bias)


def _attn_b_bwd(qkv, gain_q, gain_k, bias, delta, lse, d_out):
    def body(q_ref, k_ref, v_ref, gq_ref, gk_ref, line_ref, delta_ref, lse_ref, do_ref,
             dq_out, dk_out, dv_out, dgq_out, dgk_out, dline_out,
             qn_ref, kp_ref, vp_ref, dq_ref, dk_ref, dv_ref, bias_ref, ds_out):
        g = pl.program_id(1)
        _prep_q(q_ref, gq_ref, qn_ref)
        _unroll_bias(line_ref, bias_ref, B_W)
        ds_out[...] = jnp.zeros_like(ds_out)
        for gi, d in enumerate(B_DILATIONS):
            @pl.when(g == gi)
            def _():
                pad = B_HALF_WINDOW * d
                for acc in (dk_ref, dv_ref):
                    for base in (0, B_ACC_ROWS):
                        acc[pl.ds(base, SEQ + 2 * pad), :] = jnp.zeros((SEQ + 2 * pad, LANES), F32)
                _prep_kv(k_ref, v_ref, gk_ref, kp_ref, vp_ref, pad)
                _bwd_tiles(qn_ref, kp_ref, vp_ref, bias_ref, do_ref, lse_ref, delta_ref, dq_ref, dk_ref, dv_ref,
                           ds_out, d=d, half_window=B_HALF_WINDOW, split_rows=B_ACC_ROWS)
                dgk_out[...] = _rows8(_norm_bwd(k_ref, gk_ref, dk_ref, pad, dk_out, 1.0, B_ACC_ROWS + pad))
                dv_out[...] = (dv_ref[pl.ds(pad, SEQ), :] + dv_ref[pl.ds(B_ACC_ROWS + pad, SEQ), :]).astype(BF16)
        _fold_bias_grad(ds_out, dline_out, B_W)
        dgq_out[...] = _rows8(_norm_bwd(q_ref, gq_ref, dq_ref, 0, dq_out, SCALE))

    vec = pl.BlockSpec((1, LANES), lambda hp, g: (0, 0))
    seq_f32 = pltpu.VMEM((SEQ, LANES), F32)
    padded = pltpu.VMEM((SEQ + 2 * B_PAD_MAX, LANES), F32)
    part = pl.BlockSpec((None, 8, LANES), lambda hp, g: (4 * g + hp, 0, 0))
    line = pl.BlockSpec((None, 2, _line_width(B_HALF_WINDOW)), lambda hp, g: (4 * g + hp, 0, 0))
    return pl.pallas_call(
        body, name="attn_b_bwd", grid=(4, 3),
        in_specs=[_seq_block(lambda hp, g: (0, QB_BLK + 4 * g + hp)), _seq_block(lambda hp, g: (0, KB_BLK + 4 * g + hp)),
                  _seq_block(lambda hp, g: (0, VB_BLK + 4 * g + hp)), vec, vec,
                  line,
                  _seq_block(lambda hp, g: (0, hp)), _seq_block(lambda hp, g: (0, hp)), _seq_block(lambda hp, g: (0, hp))],
        out_specs=[pl.BlockSpec((None, SEQ, LANES), lambda hp, g: (4 * g + hp, 0, 0))] * 3 + [part, part, line],
        out_shape=[jax.ShapeDtypeStruct((12, SEQ, LANES), BF16)] * 3
        + [jax.ShapeDtypeStruct((12, 8, LANES), F32)] * 2
        + [jax.ShapeDtypeStruct((12, 2, _line_width(B_HALF_WINDOW)), F32)],
        scratch_shapes=[seq_f32, padded, padded, seq_f32] + [pltpu.VMEM((2 * B_ACC_ROWS, LANES), F32)] * 2 + [
                        pltpu.VMEM((2 * Q_BLOCK, B_W), F32), pltpu.VMEM((2 * Q_BLOCK, B_W), F32)],
        compiler_params=_params("arbitrary", "arbitrary"),
    )(qkv, qkv, qkv, gain_q, gain_k, bias, delta, lse, d_out)


def _sigmoid(t):
    return 1.0 / (1.0 + jnp.exp(-t))


def _middle(out_a, out_b, gates, x, target, w_a, w_b, w_out, b_merge):
    tm = 256
    n_steps = SEQ // tm

    def body(oa_ref, ob_ref, g_ref, x_ref, t_ref, wa_ref, wb_ref, wo_ref, bm_ref,
             dy_ref, dg_ref, doa_ref, dob_ref, dla_ref, dlb_ref, dwa_ref, dwb_ref, dwo_ref, dbm_ref, sq_ref):
        @pl.when(pl.program_id(0) == 0)
        def _():
            for ref in (dwa_ref, dwb_ref, dwo_ref, dbm_ref, sq_ref):
                ref[...] = jnp.zeros_like(ref)

        gate_a, gate_b = g_ref[:, 0:512], g_ref[:, 512:1024]
        sig_a, sig_b = _sigmoid(gate_a), _sigmoid(gate_b)
        silu_a, silu_b = gate_a * sig_a, gate_b * sig_b
        oa, ob = oa_ref[...], ob_ref[...]
        ya, yb = (oa * silu_a).astype(BF16), (ob * silu_b).astype(BF16)
        br_a, br_b = _dot(ya, wa_ref[...]), _dot(yb, wb_ref[...])
        m0 = _sigmoid(g_ref[:, 1024:2048] + bm_ref[0:1, :])
        m1 = _sigmoid(g_ref[:, 2048:3072] + bm_ref[1:2, :])
        merged = (m0 * br_a + m1 * br_b).astype(BF16)
        err = (x_ref[...] + _dot(merged, wo_ref[...])) - t_ref[...]
        sq_ref[...] += jnp.sum(err * err, axis=0, keepdims=True)

        dy = err * (1.0 / D_MODEL)
        dy_ref[...] = dy
        dyb = dy.astype(BF16)
        dmerged = _dot(dyb, wo_ref[...], NT)
        dwo_ref[...] += _dot(merged, dyb, TN)
        dbr_a, dbr_b = (dmerged * m0).astype(BF16), (dmerged * m1).astype(BF16)
        dm0 = (dmerged * br_a) * (m0 * (1.0 - m0))
        dm1 = (dmerged * br_b) * (m1 * (1.0 - m1))
        dbm_ref[0:1, :] += jnp.sum(dm0, axis=0, keepdims=True)
        dbm_ref[1:2, :] += jnp.sum(dm1, axis=0, keepdims=True)
        for s in range(N_CHIPS):
            cols = slice(256 * s, 256 * (s + 1))
            dwa_ref[s] += _dot(ya, dbr_a[:, cols], TN)
            dwb_ref[s] += _dot(yb, dbr_b[:, cols], TN)
        dya, dyb_ = _dot(dbr_a, wa_ref[...], NT), _dot(dbr_b, wb_ref[...], NT)
        doa, dob = dya * silu_a, dyb_ * silu_b
        doa_ref[...] = doa
        dob_ref[...] = dob
        for blk in range(512 // LANES):
            lanes = slice(blk * LANES, (blk + 1) * LANES)
            dla_ref[:, lanes] = _half_sum(doa[:, lanes] * oa[:, lanes], None)
            dlb_ref[:, lanes] = _half_sum(dob[:, lanes] * ob[:, lanes], None)
        d_gates = (((dya * oa) * (sig_a * (1.0 + gate_a * (1.0 - sig_a)))).astype(BF16),
                   ((dyb_ * ob) * (sig_b * (1.0 + gate_b * (1.0 - sig_b)))).astype(BF16),
                   dm0.astype(BF16), dm1.astype(BF16))
        blk = 0
        for part in d_gates:
            for c0 in range(0, part.shape[1], 256):
                dg_ref[blk] = part[:, c0:c0 + 256]
                blk += 1

    def rows(width):
        return pl.BlockSpec((tm, width), lambda i: (i, 0))

    def whole(*shape):
        return pl.BlockSpec(shape, lambda i: (0,) * len(shape))

    return pl.pallas_call(
        body, name="middle", grid=(n_steps,),
        in_specs=[rows(512), rows(512), rows(GATE_WIDTH), rows(D_MODEL), rows(D_MODEL),
                  whole(512, D_MODEL), whole(512, D_MODEL), whole(D_MODEL, D_MODEL), whole(2, D_MODEL)],
        out_specs=[rows(D_MODEL), pl.BlockSpec((GATE_WIDTH // 256, tm, 256), lambda i: (0, i, 0)),
                   rows(512), rows(512), rows(512), rows(512),
                   whole(N_CHIPS, 512, 256), whole(N_CHIPS, 512, 256), whole(D_MODEL, D_MODEL),
                   whole(2, D_MODEL), whole(1, D_MODEL)],
        out_shape=[jax.ShapeDtypeStruct((SEQ, D_MODEL), F32), jax.ShapeDtypeStruct((GATE_WIDTH // 256, SEQ, 256), BF16),
                   jax.ShapeDtypeStruct((SEQ, 512), F32), jax.ShapeDtypeStruct((SEQ, 512), F32),
                   jax.ShapeDtypeStruct((SEQ, 512), F32), jax.ShapeDtypeStruct((SEQ, 512), F32),
                   jax.ShapeDtypeStruct((N_CHIPS, 512, 256), F32), jax.ShapeDtypeStruct((N_CHIPS, 512, 256), F32),
                   jax.ShapeDtypeStruct((D_MODEL, D_MODEL), F32), jax.ShapeDtypeStruct((2, D_MODEL), F32),
                   jax.ShapeDtypeStruct((1, D_MODEL), F32)],
        compiler_params=_params("arbitrary"),
    )(out_a, out_b, gates, x, target, w_a, w_b, w_out, b_merge)


def _which(j, edges, fns):
    lo = 0
    for hi, fn in zip(edges, fns):
        pl.when((j >= lo) & (j < hi))(fn)
        lo = hi


def _sibling_rows(tile, core):
    lo, hi = tile * W_BLOCK, (tile + 1) * W_BLOCK
    for chip in range(N_CHIPS):
        a = chip * W_IN_SHARD + (1 - core) * (W_IN_SHARD // 2)
        first, last = max(lo, a), min(hi, a + W_IN_SHARD // 2)
        if first < last:
            return chip, first - a, first - lo, last - first
    return None


def _d_w_in(d_proj, h, rest=None):
    plan, step, width = [], 0, 0
    for p in d_proj:
        total = p.shape[0] * p.shape[2]
        if width + total <= W_BLOCK:
            plan.append((p.shape[0], step, 1))
            width += total
            if width == W_BLOCK:
                step, width = step + 1, 0
        else:
            assert width == 0 and total % W_BLOCK == 0
            plan.append((W_BLOCK // p.shape[2], step, total // W_BLOCK))
            step += total // W_BLOCK
    assert width == 0 and step == IN_WIDTH // W_BLOCK
    firsts = sorted({first for _, first, _ in plan})
    edges = firsts[1:] + [step]
    halves = 2

    hand_over = rest is not None
    half = W_IN_SHARD // 2

    def body(*refs):
        if hand_over:
            pieces, h_ref, rest_ref = refs[:len(d_proj)], refs[len(d_proj)], refs[len(d_proj) + 1]
            o_ref, got_ref, got_rest_ref, acc_ref, send_sems, recv_sems, stage = refs[len(d_proj) + 2:]
        else:
            pieces, h_ref, o_ref, acc_ref = refs[:-3], refs[-3], refs[-2], refs[-1]
        k = pl.program_id(1)

        def emit(group):
            def fn():
                cols = jnp.concatenate([ref[b] for ref in group for b in range(ref.shape[0])], axis=1)
                term = _dot(cols, h_ref[...], TN)

                @pl.when(k == 0)
                def _():
                    acc_ref[...] = term

                @pl.when(k == halves - 1)
                def _():
                    o_ref[...] = (acc_ref[...] + term).astype(BF16)
            return fn

        groups = [[ref for ref, (_, first, _) in zip(pieces, plan) if first == f] for f in firsts]
        _which(pl.program_id(0), edges, [emit(group) for group in groups])

        if hand_over:
            cx, cy, c = lax.axis_index("x"), lax.axis_index("y"), lax.axis_index("c")
            sibling = (cx, cy, 1 - c)

            def to_sibling(sem, src, dst, recv=0):
                return pltpu.make_async_remote_copy(src_ref=src, dst_ref=dst, send_sem=send_sems.at[sem],
                                                    recv_sem=recv_sems.at[recv], device_id=sibling, device_id_type=MESH)

            def tile_copy(tile, core):
                chip, row, start, rows = _sibling_rows(tile, core)
                return to_sibling(tile % 2, stage.at[tile % 2, pl.ds(0, rows), :], got_ref.at[chip, pl.ds(row, rows), :])

            rest_copy = to_sibling(2, _half_rows(rest_ref, 1 - c), got_rest_ref, recv=1)

            @pl.when((pl.program_id(0) == 0) & (k == 0))
            def _():
                rest_copy.start()

            for tile in range(step):
                for core in range(2):
                    @pl.when((pl.program_id(0) == tile) & (k == halves - 1) & (c == core))
                    def _(tile=tile, core=core):
                        if tile >= 2 and _sibling_rows(tile - 2, core):
                            tile_copy(tile - 2, core).wait_send()
                        if _sibling_rows(tile, core):
                            _, _, start, rows = _sibling_rows(tile, core)
                            stage[tile % 2, 0:rows, :] = o_ref[start:start + rows, :]
                            tile_copy(tile, core).start()
                        if tile == step - 1:
                            for last in (step - 2, step - 1):
                                if _sibling_rows(last, core):
                                    tile_copy(last, core).wait_send()
                            rest_copy.wait()
                            to_sibling(0, got_ref, got_ref).wait_recv()

    def cols_spec(piece, n, first, steps):
        def index(j, k):
            return jnp.clip(j - first, 0, steps - 1), jnp.where((j >= first) & (j < first + steps), k, 0), 0
        return pl.BlockSpec((n, SEQ // halves, piece.shape[2]), index)

    tile_spec = pl.BlockSpec((W_BLOCK, D_MODEL), lambda j, k: (j, 0))
    in_specs = [cols_spec(p, *pl_) for p, pl_ in zip(d_proj, plan)] + [
        pl.BlockSpec((SEQ // halves, D_MODEL), lambda j, k: (k, 0))]
    acc = pltpu.VMEM((W_BLOCK, D_MODEL), F32)
    if not hand_over:
        return pl.pallas_call(
            body, name="d_w_in", grid=(step, halves), in_specs=in_specs, out_specs=tile_spec,
            out_shape=jax.ShapeDtypeStruct((IN_WIDTH, D_MODEL), BF16), scratch_shapes=[acc],
            compiler_params=_params("arbitrary", "arbitrary"),
        )(*d_proj, h)
    return pl.pallas_call(
        body, name="d_w_in", grid=(step, halves), in_specs=in_specs + [ANY], out_specs=[tile_spec, ANY, ANY],
        out_shape=[jax.ShapeDtypeStruct((IN_WIDTH, D_MODEL), BF16),
                   jax.ShapeDtypeStruct((N_CHIPS, half, D_MODEL), BF16),
                   jax.ShapeDtypeStruct((N_CHIPS, rest.shape[1] // 2, D_MODEL), BF16)],
        scratch_shapes=[acc, pltpu.SemaphoreType.DMA((3,)), pltpu.SemaphoreType.DMA((2,)),
                        pltpu.VMEM((2, W_BLOCK, D_MODEL), BF16)],
        compiler_params=_params("arbitrary", "arbitrary"),
    )(*d_proj, h, rest)


RELAY_STEP = 10
RELAY_ROWS = 352


def _d_x(d_proj, w_t, x, gain, dy, chip_sums):
    tm = 256
    n_steps = SEQ // tm
    n_w = IN_WIDTH // W_BLOCK
    n_p, n_s = len(d_proj), len(chip_sums)

    def body(*refs):
        pieces, w_refs = refs[:n_p], refs[n_p:n_p + n_w]
        x_ref, g_ref, dy_ref = refs[n_p + n_w:n_p + n_w + 3]
        q_refs = refs[n_p + n_w + 3:n_p + n_w + 3 + n_s]
        dx_ref, dgain_ref = refs[n_p + n_w + 3 + n_s:n_p + n_w + 5 + n_s]
        outs = refs[n_p + n_w + 5 + n_s:n_p + n_w + 5 + 4 * n_s]
        got_refs, relay_refs, sum_refs = outs[:n_s], outs[n_s:2 * n_s], outs[2 * n_s:]
        if n_s:
            send_sems, recv_sems, local_sems, a_buf, b_buf, c_buf = refs[n_p + n_w + 5 + 4 * n_s:]

        def hops():
            cx, cy, c = lax.axis_index("x"), lax.axis_index("y"), lax.axis_index("c")
            near = (cx + (1 - c) - 2 * cx * (1 - c), cy + c - 2 * cy * c)
            far = (cx + c - 2 * cx * c, cy + (1 - c) - 2 * cy * (1 - c))
            chip = lambda p: 2 * p[0] + p[1]

            def copy(k, src, dst, to):
                return pltpu.make_async_remote_copy(src_ref=src, dst_ref=dst, send_sem=send_sems.at[k],
                                                    recv_sem=recv_sems.at[k], device_id=(*to, c), device_id_type=MESH)

            first = [(copy(3 * b, q.at[chip(near)], got.at[0], near),
                      copy(3 * b + 1, q.at[3 - chip((cx, cy))], relay, near))
                     for b, (q, got, relay) in enumerate(zip(q_refs, got_refs, relay_refs))]
            second = [copy(3 * b + 2, s, got.at[1], far) for b, (s, got) in enumerate(zip(sum_refs, got_refs))]
            return first, second, chip(far)

        @pl.when(pl.program_id(0) == 0)
        def _():
            dgain_ref[...] = jnp.zeros_like(dgain_ref)
            if n_s:
                for direct, pass_on in hops()[0]:
                    direct.start()
                    pass_on.start()

        if n_s:
            @pl.when(pl.program_id(0) == RELAY_STEP)
            def _():
                first, second, far_chip = hops()
                for b, (q, relay, total) in enumerate(zip(q_refs, relay_refs, sum_refs)):
                    first[b][1].wait_recv()
                    half = relay.shape[0]
                    for r0 in range(0, half, RELAY_ROWS):
                        rows = min(RELAY_ROWS, half - r0)
                        mine = pltpu.make_async_copy(q.at[far_chip, pl.ds(r0, rows), :], a_buf.at[pl.ds(0, rows), :],
                                                     local_sems.at[0])
                        theirs = pltpu.make_async_copy(relay.at[pl.ds(r0, rows), :], b_buf.at[pl.ds(0, rows), :],
                                                       local_sems.at[1])
                        mine.start()
                        theirs.start()
                        mine.wait()
                        theirs.wait()
                        c_buf[0:rows, :] = (a_buf[0:rows, :].astype(F32) + b_buf[0:rows, :].astype(F32)).astype(BF16)
                        store = pltpu.make_async_copy(c_buf.at[pl.ds(0, rows), :], total.at[pl.ds(r0, rows), :],
                                                      local_sems.at[2])
                        store.start()
                        store.wait()
                    second[b].start()

        blocks = [(piece, k) for piece in pieces for k in range(piece.shape[0])]
        dh, group, width, blk = None, [], 0, 0
        for piece, k in blocks:
            group.append(piece[k])
            width += piece.shape[2]
            if width == W_BLOCK:
                term = _dot(jnp.concatenate(group, axis=1), w_refs[blk][...])
                dh = term if dh is None else dh + term
                group, width, blk = [], 0, blk + 1
        assert not group and blk == n_w
        xf = x_ref[...]
        r = lax.rsqrt(jnp.mean(xf * xf, axis=-1, keepdims=True) + EPS)
        xh = xf * r
        dxh = dh * g_ref[...]
        dx_ref[...] = r * (dxh - xh * jnp.mean(dxh * xh, axis=-1, keepdims=True)) + dy_ref[...]
        dgain_ref[...] += _rows8(jnp.sum(dh * xh, axis=0, keepdims=True))

        if n_s:
            @pl.when(pl.program_id(0) == n_steps - 1)
            def _():
                first, second, _ = hops()
                for direct, pass_on in first:
                    direct.wait()
                    pass_on.wait_send()
                for cp in second:
                    cp.wait()

    row = pl.BlockSpec((tm, D_MODEL), lambda i: (i, 0))
    halves = [q.shape[1] for q in chip_sums]
    res = pl.pallas_call(
        body, name="d_x", grid=(n_steps,),
        in_specs=[pl.BlockSpec((p.shape[0], tm, p.shape[2]), lambda i: (0, i, 0)) for p in d_proj] + _w_blocks(0, n_w)
        + [row, pl.BlockSpec((1, D_MODEL), lambda i: (0, 0)), row] + [ANY] * n_s,
        out_specs=[row, pl.BlockSpec((8, D_MODEL), lambda i: (0, 0))] + [ANY] * (3 * n_s),
        out_shape=[jax.ShapeDtypeStruct((SEQ, D_MODEL), F32), jax.ShapeDtypeStruct((8, D_MODEL), F32)]
        + [jax.ShapeDtypeStruct((2, half, D_MODEL), BF16) for half in halves]
        + [jax.ShapeDtypeStruct((half, D_MODEL), BF16) for half in halves] * 2,
        scratch_shapes=[pltpu.SemaphoreType.DMA((3 * n_s,)), pltpu.SemaphoreType.DMA((3 * n_s,)),
                        pltpu.SemaphoreType.DMA((3,))] + [pltpu.VMEM((RELAY_ROWS, D_MODEL), BF16)] * 3 if n_s else [],
        compiler_params=_params("arbitrary"),
    )(*d_proj, *([w_t] * n_w), x, gain, dy, *chip_sums)
    return res[0], res[1], res[2:2 + n_s]


def _my_place():
    x, y, c = lax.axis_index("x"), lax.axis_index("y"), lax.axis_index("c")
    return jnp.stack([2 * x + y, c]).astype(jnp.int32)


def _half_rows(ref, half):
    rows = ref.shape[-2] // 2
    idx = (slice(None),) * (len(ref.shape) - 2) + (pl.ds(pl.multiple_of(half * rows, 16), rows), slice(None))
    return ref.at[idx]


def _add_halves(place, grads, theirs, name):
    half = theirs.shape[1]
    tr = _row_tile(half)
    n = half // tr

    def body(place_ref, g_ref, t_ref, o_ref):
        o_ref[...] = (g_ref[...].astype(F32) + t_ref[...].astype(F32)).astype(BF16)

    return pl.pallas_call(
        body, name=name,
        grid_spec=pltpu.PrefetchScalarGridSpec(
            num_scalar_prefetch=1, grid=(N_CHIPS, n),
            in_specs=[pl.BlockSpec((None, tr, D_MODEL), lambda s, i, p: (s, p[1] * n + i, 0)),
                      pl.BlockSpec((None, tr, D_MODEL), lambda s, i, p: (s, i, 0))],
            out_specs=pl.BlockSpec((None, tr, D_MODEL), lambda s, i, p: (s, i, 0))),
        out_shape=jax.ShapeDtypeStruct((N_CHIPS, half, D_MODEL), BF16),
        compiler_params=_params("arbitrary", "arbitrary"),
    )(place, grads, theirs)


def _add_chips(place, chip_sums, others, name):
    half = others.shape[1]
    tr = _row_tile(half)
    n = half // tr

    def body(place_ref, q_ref, o_ref, r_ref):
        acc = q_ref[...].astype(F32)
        for j in range(others.shape[0]):
            acc = acc + o_ref[j].astype(F32)
        r_ref[...] = acc

    return pl.pallas_call(
        body, name=name,
        grid_spec=pltpu.PrefetchScalarGridSpec(
            num_scalar_prefetch=1, grid=(n,),
            in_specs=[pl.BlockSpec((None, tr, D_MODEL), lambda i, p: (p[0], i, 0)),
                      pl.BlockSpec((others.shape[0], tr, D_MODEL), lambda i, p: (0, i, 0))],
            out_specs=pl.BlockSpec((tr, D_MODEL), lambda i, p: (p[1] * n + i, 0))),
        out_shape=jax.ShapeDtypeStruct((2 * half, D_MODEL), F32),
        compiler_params=_params("arbitrary"),
    )(place, chip_sums, others)


def _join_halves(shards, block):
    n = len(shards)
    rows = block.shape[0]

    def body(*refs):
        b_ref, o_refs, sum_ref = refs[n], refs[n + 1:2 * n + 1], refs[2 * n + 1]
        send_sems, recv_sems, small_send, small_recv, local_sem, all_ref = refs[2 * n + 2:]
        x, y, c = lax.axis_index("x"), lax.axis_index("y"), lax.axis_index("c")
        me, sibling = (x, y, c), (x, y, 1 - c)
        chips = [(1 - x, y), (x, 1 - y), (1 - x, 1 - y)]

        def half(k, rows_ref):
            return pltpu.make_async_remote_copy(src_ref=rows_ref, dst_ref=rows_ref, send_sem=send_sems.at[k],
                                                recv_sem=recv_sems.at[k], device_id=sibling, device_id_type=MESH)

        def at(px, py, pc):
            return all_ref.at[pl.ds(pl.multiple_of((4 * px + 2 * py + pc) * rows, 8), rows), :]

        def small(k, block_of, to, src=None):
            return pltpu.make_async_remote_copy(src_ref=at(*block_of) if src is None else src, dst_ref=at(*block_of),
                                                send_sem=small_send.at[k], recv_sem=small_recv.at[k],
                                                device_id=to, device_id_type=MESH)

        sends = [half(k, _half_rows(o, c)) for k, o in enumerate(o_refs)]
        for cp in sends:
            cp.start()
        mine = pltpu.make_async_copy(b_ref, at(*me), local_sem)
        mine.start()
        first = [small(0, me, sibling, src=b_ref)]
        first += [small(1 + j, me, (*chip, c), src=b_ref) for j, chip in enumerate(chips)]
        for cp in first:
            cp.start()
        passed = [small(4 + j, (*chip, c), sibling) for j, chip in enumerate(chips)]
        for j, chip in enumerate(chips):
            small(1 + j, (*chip, c), me).wait_recv()
            passed[j].start()
        small(0, sibling, me).wait_recv()
        for j, chip in enumerate(chips):
            small(4 + j, (*chip, 1 - c), me).wait_recv()
        mine.wait()
        acc = all_ref[0:rows, :]
        for dev in range(1, 8):
            acc = acc + all_ref[rows * dev:rows * (dev + 1), :]
        sum_ref[...] = acc
        for k, o in enumerate(o_refs):
            half(k, _half_rows(o, 1 - c)).wait_recv()
        for cp in sends + first + passed:
            cp.wait_send()

    res = pl.pallas_call(
        body, name="reduce_join_halves", in_specs=[ANY] * n + [pl.BlockSpec(memory_space=pltpu.VMEM)],
        out_specs=[ANY] * n + [pl.BlockSpec(memory_space=pltpu.VMEM)],
        out_shape=[jax.ShapeDtypeStruct(s.shape, F32) for s in shards] + [jax.ShapeDtypeStruct(block.shape, F32)],
        input_output_aliases={k: k for k in range(n)},
        scratch_shapes=[pltpu.SemaphoreType.DMA((n,)), pltpu.SemaphoreType.DMA((n,)),
                        pltpu.SemaphoreType.DMA((7,)), pltpu.SemaphoreType.DMA((7,)), pltpu.SemaphoreType.DMA,
                        pltpu.VMEM((8 * rows, D_MODEL), F32)],
    )(*shards, block)
    return res[:n], res[n]


def _adamw_math(w, g, m, v):
    m = ADAM_B1 * m + (1.0 - ADAM_B1) * g
    v = ADAM_B2 * v + (1.0 - ADAM_B2) * (g * g)
    m_hat = m / (1.0 - ADAM_B1 ** ADAM_STEP)
    v_hat = v / (1.0 - ADAM_B2 ** ADAM_STEP)
    return -ADAM_LR * (m_hat / (jnp.sqrt(v_hat) + ADAM_EPS) + ADAM_WD * w), m, v


def _adamw(w, g, m, v, name):
    r, c = w.shape
    tr = _row_tile(r)

    def body(w_ref, g_ref, m_ref, v_ref, d_ref, nm_ref, nv_ref):
        d_ref[...], nm_ref[...], nv_ref[...] = _adamw_math(w_ref[...], g_ref[...], m_ref[...], v_ref[...])

    spec = pl.BlockSpec((tr, c), lambda i: (i, 0))
    return pl.pallas_call(
        body, name=name, grid=(r // tr,), in_specs=[spec] * 4, out_specs=[spec] * 3,
        out_shape=[jax.ShapeDtypeStruct((r, c), F32)] * 3, compiler_params=_params("arbitrary"),
    )(w, g, m, v)


def _adamw_small(ws, gs, ms, vs):
    n = len(ws)

    def body(*refs):
        ins, outs = refs[:4 * n], refs[4 * n:]
        for k in range(n):
            d, m, v = _adamw_math(ins[k][...], ins[n + k][...], ins[2 * n + k][...], ins[3 * n + k][...])
            outs[k][...], outs[n + k][...], outs[2 * n + k][...] = d, m, v

    shapes = [jax.ShapeDtypeStruct(w.shape, F32) for w in ws]
    res = pl.pallas_call(body, name="adamw_small", out_shape=shapes * 3)(*ws, *gs, *ms, *vs)
    return res[:n], res[n:2 * n], res[2 * n:]


def _fold_heads(partials):
    t = jnp.sum(partials[:, 0, :], axis=0)
    return (t[:HEAD_DIM] + t[HEAD_DIM:]).reshape(1, HEAD_DIM)


def _local_step(x, target, norm_gain, w_t, w_a, w_b, w_o, b_m, q_norm_a, k_norm_a, q_norm_b, k_norm_b, sink_a,
                rel_bias, start_reduce=None, small_shard=None):
    two = lambda gain: jnp.concatenate([gain, gain], axis=1)
    bias_a = _bias_lines(rel_bias[:, :8], A_HALF_WINDOW, 1)
    bias_b = jnp.concatenate([_bias_lines(rel_bias[:, 8 + 8 * g:16 + 8 * g], B_HALF_WINDOW, d)
                              for g, d in enumerate(B_DILATIONS)], axis=0)

    qkv, h, *small_all = _in_proj(x, norm_gain, w_t, 0, QKV_WIDTH // W_BLOCK, BF16, "in_proj_qkv", True, small_shard)
    if small_shard is not None:
        w_a, w_b, w_o, b_m = _unpack_weights(small_all[0])
    gates, = _in_proj(x, norm_gain, w_t, QKV_WIDTH // W_BLOCK, GATE_WIDTH // W_BLOCK, F32, "in_proj_gates", False)
    out_a, lse_a = _attn_a_fwd(qkv, two(q_norm_a), two(k_norm_a), bias_a, sink_a)
    out_b, lse_b = _attn_b_fwd(qkv, two(q_norm_b), two(k_norm_b), bias_b)

    dy, dgates, d_out_a, d_out_b, delta_a, delta_b, d_wa, d_wb, d_wo, d_bm, sq = _middle(
        out_a, out_b, gates, x, target, w_a, w_b, w_o, b_m)
    loss = (0.5 / D_MODEL) * jnp.sum(sq)

    dq_a, dkv_a, dgq_a, dgk_a, ds_a, dsink = _attn_a_bwd(
        qkv, two(q_norm_a), two(k_norm_a), bias_a, sink_a, delta_a, lse_a, d_out_a)
    dq_b, dk_b, dv_b, dgq_b, dgk_b, ds_b = _attn_b_bwd(
        qkv, two(q_norm_b), two(k_norm_b), bias_b, delta_b, lse_b, d_out_b)
    d_proj = (dq_a, dkv_a, dq_b, dk_b, dv_b, dgates)

    d_bm_rows = jnp.pad(d_bm.reshape(2, N_CHIPS, 256).transpose(1, 0, 2),
                        ((0, 0), (0, REST_ROWS - 514), (0, D_MODEL - 256)))
    rest = jnp.concatenate([d_wo.reshape(N_CHIPS, 256, D_MODEL), d_wa.reshape(N_CHIPS, 128, D_MODEL),
                            d_wb.reshape(N_CHIPS, 128, D_MODEL), d_bm_rows], axis=1)
    if start_reduce is None:
        grads, chip_sums = [_d_w_in(d_proj, h).reshape(N_CHIPS, W_IN_SHARD, D_MODEL), rest], []
    else:
        d_wt, *theirs = _d_w_in(d_proj, h, rest.astype(BF16))
        grads = [d_wt.reshape(N_CHIPS, W_IN_SHARD, D_MODEL), rest]
        chip_sums = start_reduce(grads, theirs)
    grad_x, d_gain, others = _d_x(d_proj, w_t, x, norm_gain, dy, chip_sums)

    d_rel = jnp.concatenate(
        [_bias_grad(ds_a, A_HALF_WINDOW, 1)]
        + [_bias_grad(ds_b[4 * g:4 * g + 4], B_HALF_WINDOW, d) for g, d in enumerate(B_DILATIONS)], axis=1)
    d_sink = jnp.sum(dsink, axis=(2, 3)).reshape(1, 8)
    dgk_a_row = dgk_a[0]
    small = jnp.zeros((8, D_MODEL), F32)
    small = small.at[0].set(d_gain[0])
    small = small.at[1].set(d_rel.reshape(-1))
    misc = jnp.concatenate([_fold_heads(dgq_a), (dgk_a_row[:HEAD_DIM] + dgk_a_row[HEAD_DIM:]).reshape(1, HEAD_DIM),
                            _fold_heads(dgq_b), _fold_heads(dgk_b), d_sink], axis=1)
    small = small.at[2, :264].set(misc[0])

    return loss, grad_x, grads, small, chip_sums, others


def _unpack_weights(small_all):
    sm = small_all.reshape(N_CHIPS, SMALL_ROWS, D_MODEL)
    w_o = sm[:, 0:256].reshape(D_MODEL, D_MODEL)
    w_a = sm[:, 256:384].reshape(N_CHIPS, 512, 256).transpose(1, 0, 2).reshape(512, D_MODEL)
    w_b = sm[:, 384:512].reshape(N_CHIPS, 512, 256).transpose(1, 0, 2).reshape(512, D_MODEL)
    b_m = lax.bitcast_convert_type(sm[:, 512].reshape(N_CHIPS, 2, 256, 2), F32)
    return w_a, w_b, w_o, b_m.transpose(1, 0, 2).reshape(2, D_MODEL)


def _pack_small_weights(w_branch_a, w_branch_b, b_merge, w_out):
    b_m = jnp.pad(lax.bitcast_convert_type(b_merge, BF16).reshape(1, D_MODEL), ((0, SMALL_ROWS - 513), (0, 0)))
    return jnp.concatenate([w_out.astype(BF16), w_branch_a.astype(BF16).reshape(128, D_MODEL),
                            w_branch_b.astype(BF16).reshape(128, D_MODEL), b_m], axis=0)


def kernel(x, norm_gain, w_in, q_norm_a, k_norm_a, q_norm_b, k_norm_b, sink_a, rel_bias, w_branch_a, w_branch_b, b_merge, w_out, loss_target, m_norm_gain, m_w_in, m_q_norm_a, m_k_norm_a, m_q_norm_b, m_k_norm_b, m_sink_a, m_rel_bias, m_w_branch_a, m_w_branch_b, m_b_merge, m_w_out, v_norm_gain, v_w_in, v_q_norm_a, v_k_norm_a, v_q_norm_b, v_k_norm_b, v_sink_a, v_rel_bias, v_w_branch_a, v_w_branch_b, v_b_merge, v_w_out):
    w_in_t, m_w_in_t, v_w_in_t = (jnp.transpose(t[0]) for t in (w_in, m_w_in, v_w_in))
    wt_shard = _cast_rows(w_in_t, BF16, "w_in_cast")
    w_t = _gather_weights(wt_shard)
    small_shard = _pack_small_weights(w_branch_a[0], w_branch_b[0], b_merge[0], w_out[0])

    place = _my_place()
    names = ("w_in", "rest")

    def start_reduce(grads, theirs):
        return [_add_halves(place, g, t, "reduce_add_halves_" + n) for g, t, n in zip(grads, theirs, names)]

    loss_part, grad_x, _, small, chip_sums, others = _local_step(
        x[0], loss_target[0], norm_gain, w_t, None, None, None, None, q_norm_a, k_norm_a, q_norm_b, k_norm_b,
        sink_a, rel_bias, start_reduce, small_shard)

    (g_wt, g_rest), small = _join_halves(
        [_add_chips(place, q, o, "reduce_add_chips_" + n) for q, o, n in zip(chip_sums, others, names)],
        small.at[3, 0].set(loss_part))
    loss = small[3, 0]

    g_w_out = g_rest[0:256]
    g_w_a = g_rest[256:384].reshape(512, 256)
    g_w_b = g_rest[384:512].reshape(512, 256)
    g_b_merge = g_rest[512:514, :256]
    g_norm_gain = small[0:1]
    g_rel_bias = small[1].reshape(N_BUCKETS, N_BUCKETS)
    g_q_a, g_k_a, g_q_b, g_k_b = (small[2:3, 64 * k:64 * k + 64] for k in range(4))
    g_sink = small[2:3, 256:264]

    big_names = (("w_branch_a", w_branch_a, g_w_a, m_w_branch_a, v_w_branch_a),
                 ("w_branch_b", w_branch_b, g_w_b, m_w_branch_b, v_w_branch_b),
                 ("w_out", w_out, g_w_out, m_w_out, v_w_out))
    upd = {name: (g,) + tuple(_adamw(w[0], g, m[0], v[0], "adamw_" + name)) for name, w, g, m, v in big_names}
    upd["w_in"] = tuple(jnp.transpose(t) for t in (g_wt,) + tuple(_adamw(w_in_t, g_wt, m_w_in_t, v_w_in_t, "adamw_w_in")))
    small_names = ("norm_gain", "q_norm_a", "k_norm_a", "q_norm_b", "k_norm_b", "sink_a", "rel_bias", "b_merge")
    ws = [norm_gain, q_norm_a, k_norm_a, q_norm_b, k_norm_b, sink_a, rel_bias, b_merge[0]]
    gs = [g_norm_gain, g_q_a, g_k_a, g_q_b, g_k_b, g_sink, g_rel_bias, g_b_merge]
    ms = [m_norm_gain, m_q_norm_a, m_k_norm_a, m_q_norm_b, m_k_norm_b, m_sink_a, m_rel_bias, m_b_merge[0]]
    vs = [v_norm_gain, v_q_norm_a, v_k_norm_a, v_q_norm_b, v_k_norm_b, v_sink_a, v_rel_bias, v_b_merge[0]]
    ds, nms, nvs = _adamw_small(ws, gs, ms, vs)
    for k, name in enumerate(small_names):
        upd[name] = (gs[k], ds[k], nms[k], nvs[k])

    order = ("norm_gain", "w_in", "q_norm_a", "k_norm_a", "q_norm_b", "k_norm_b", "sink_a", "rel_bias",
             "w_branch_a", "w_branch_b", "b_merge", "w_out")
    lead = {"w_in", "w_branch_a", "w_branch_b", "b_merge", "w_out"}
    outs = [loss, grad_x[None]]
    for part in range(4):
        outs += [upd[name][part][None] if name in lead else upd[name][part] for name in order]
    return tuple(outs)
```
